```python
import jax, jax.numpy as jnp
from jax import lax
import numpy as np

D_MODEL = 1024
BATCH = 8
SEQ = 2048
DEPTH = 1

RET_HEADS = 4
RET_HEAD_DIM = 128
RET_WIDTH = RET_HEADS * RET_HEAD_DIM
RET_CHUNK = 128
ROPE_BASE = 10000.0
SGU_GROUPS = 4
SGU_GROUP_DIM = 128
SGU_WIDTH = SGU_GROUPS * SGU_GROUP_DIM
SGU_CHUNK = 128
MIX_WIDTH = RET_WIDTH + SGU_WIDTH
PROJ_WIDTH = 4 * RET_WIDTH + 2 * SGU_WIDTH
D_FF = 2816
CONV_WIDTH = 3
EPS = 1e-6

kernel_name = "hymba_style_retention_sgu_convffn"


def rmsnorm(x, g):
    xf = x.astype(jnp.float32)
    y = xf * lax.rsqrt(jnp.mean(xf * xf, axis=-1, keepdims=True) + EPS)
    return (y * g.astype(jnp.float32)).astype(x.dtype)


def layernorm(x, g, b):
    xf = x.astype(jnp.float32)
    mu = jnp.mean(xf, axis=-1, keepdims=True)
    xc = xf - mu
    y = xc * lax.rsqrt(jnp.mean(xc * xc, axis=-1, keepdims=True) + EPS)
    return (y * g.astype(jnp.float32) + b.astype(jnp.float32)).astype(x.dtype)


def rotary(x, cos, sin):
    half = x.shape[-1] // 2
    x1, x2 = x[..., :half], x[..., half:]
    c = cos[None, :, None, :]
    s = sin[None, :, None, :]
    return jnp.concatenate([x1 * c - x2 * s, x2 * c + x1 * s], axis=-1)


def retention_chunkwise(q, k, v):
    B, S, H, D = q.shape
    C = RET_CHUNK
    N = S // C
    dt = q.dtype
    log_gamma = jnp.log(1.0 - jnp.power(2.0, -5.0 - jnp.arange(H, dtype=jnp.float32)))
    pos = jnp.arange(C, dtype=jnp.float32)
    diff = pos[:, None] - pos[None, :]
    decay_mask = jnp.where(diff >= 0.0,
                           jnp.exp(log_gamma[:, None, None] * jnp.maximum(diff, 0.0)[None]),
                           0.0).astype(dt)
    k_decay = jnp.exp(log_gamma[:, None] * (C - 1.0 - pos)[None]).astype(dt)
    q_decay = jnp.exp(log_gamma[:, None] * (pos + 1.0)[None]).astype(dt)
    chunk_decay = jnp.exp(log_gamma * C).astype(dt)

    def to_chunks(t):
        return t.reshape(B, N, C, H, D).transpose(0, 3, 1, 2, 4)

    qc, kc, vc = to_chunks(q), to_chunks(k), to_chunks(v)
    scores = jnp.einsum('bhnqd,bhnkd->bhnqk', qc, kc) * decay_mask[None, :, None]
    intra = jnp.einsum('bhnqk,bhnkd->bhnqd', scores, vc)
    kv = jnp.einsum('bhnkd,bhnke->bhnde', kc * k_decay[None, :, None, :, None], vc)

    def step(state, kv_n):
        return state * chunk_decay[None, :, None, None] + kv_n, state

    init = jnp.zeros((B, H, D, D), dtype=kv.dtype)
    _, s_prev = lax.scan(step, init, jnp.moveaxis(kv, 2, 0))
    s_prev = jnp.moveaxis(s_prev, 0, 2)
    cross = jnp.einsum('bhnqd,bhnde->bhnqe', qc * q_decay[None, :, None, :, None], s_prev)
    out = intra + cross
    return out.transpose(0, 2, 3, 1, 4).reshape(B, S, H, D)


def spatial_gating_chunked(u, v, ln_g, ln_b, w_s, b_s):
    B, S, _ = u.shape
    C = SGU_CHUNK
    N = S // C
    G, dg = SGU_GROUPS, SGU_GROUP_DIM
    vn = layernorm(v.reshape(B, N, C, G, dg), ln_g, ln_b)
    causal = jnp.tril(jnp.ones((C, C), dtype=w_s.dtype))
    w = w_s * causal[None]
    mixed = jnp.einsum('gts,bnsgd->bntgd', w, vn) + b_s.T[None, None, :, :, None]
    return u * mixed.reshape(B, S, G * dg)


def causal_depthwise_conv(h, w, b):
    S = h.shape[1]
    hp = jnp.pad(h, ((0, 0), (CONV_WIDTH - 1, 0), (0, 0)))
    y = hp[:, 0:S] * w[0]
    for j in range(1, CONV_WIDTH):
        y = y + hp[:, j:j + S] * w[j]
    return y + b


def _fwd_setup_inputs(seed: int = 0) -> dict:
    key = jax.random.key(seed)
    ks = jax.random.split(key, 16)
    f32 = jnp.float32
    nrm = lambda k, shape, scale: jax.random.normal(k, shape, f32) * scale
    return {
        "x": nrm(ks[0], (BATCH, SEQ, D_MODEL), 1.0),
        "mix_norm_g": 1.0 + nrm(ks[1], (DEPTH, D_MODEL), 0.01),
        "w_in": nrm(ks[2], (DEPTH, D_MODEL, PROJ_WIDTH), D_MODEL ** -0.5),
        "ret_norm_g": 1.0 + nrm(ks[3], (DEPTH, RET_WIDTH), 0.01),
        "sgu_ln_g": 1.0 + nrm(ks[4], (DEPTH, SGU_GROUPS, SGU_GROUP_DIM), 0.01),
        "sgu_ln_b": nrm(ks[5], (DEPTH, SGU_GROUPS, SGU_GROUP_DIM), 0.01),
        "sgu_w_s": nrm(ks[6], (DEPTH, SGU_GROUPS, SGU_CHUNK, SGU_CHUNK), SGU_CHUNK ** -0.5),
        "sgu_b_s": 1.0 + nrm(ks[7], (DEPTH, SGU_GROUPS, SGU_CHUNK), 0.01),
        "w_out": nrm(ks[8], (DEPTH, MIX_WIDTH, D_MODEL), MIX_WIDTH ** -0.5),
        "ffn_norm_g": 1.0 + nrm(ks[9], (DEPTH, D_MODEL), 0.01),
        "w_up": nrm(ks[10], (DEPTH, D_MODEL, 2 * D_FF), D_MODEL ** -0.5),
        "conv_w": nrm(ks[11], (DEPTH, CONV_WIDTH, 2 * D_FF), CONV_WIDTH ** -0.5),
        "conv_b": nrm(ks[12], (DEPTH, 2 * D_FF), 0.01),
        "w_down": nrm(ks[13], (DEPTH, D_FF, D_MODEL), D_FF ** -0.5),
        "final_norm_g": 1.0 + nrm(ks[14], (D_MODEL,), 0.01),
    }


def _fwd_reference(x, mix_norm_g, w_in, ret_norm_g, sgu_ln_g, sgu_ln_b, sgu_w_s, sgu_b_s,
              w_out, ffn_norm_g, w_up, conv_w, conv_b, w_down, final_norm_g):
    B, S, _ = x.shape
    half = RET_HEAD_DIM // 2
    inv_freq = jnp.power(ROPE_BASE, -jnp.arange(half, dtype=jnp.float32) / half)
    ang = jnp.arange(S, dtype=jnp.float32)[:, None] * inv_freq[None, :]
    cos, sin = jnp.cos(ang).astype(x.dtype), jnp.sin(ang).astype(x.dtype)
    splits = [RET_WIDTH, 2 * RET_WIDTH, 3 * RET_WIDTH, 4 * RET_WIDTH, 4 * RET_WIDTH + SGU_WIDTH]

    for l in range(DEPTH):
        h = rmsnorm(x, mix_norm_g[l])
        proj = h @ w_in[l]
        q, k, v, g, u, sv = jnp.split(proj, splits, axis=-1)
        q = rotary(q.reshape(B, S, RET_HEADS, RET_HEAD_DIM), cos, sin)
        k = rotary(k.reshape(B, S, RET_HEADS, RET_HEAD_DIM), cos, sin) * (RET_HEAD_DIM ** -0.5)
        v = v.reshape(B, S, RET_HEADS, RET_HEAD_DIM)
        ret = retention_chunkwise(q, k, v)
        ret = rmsnorm(ret, ret_norm_g[l].reshape(RET_HEADS, RET_HEAD_DIM)).reshape(B, S, RET_WIDTH)
        ret = jax.nn.silu(g) * ret
        sgu = spatial_gating_chunked(jax.nn.gelu(u, approximate=False),
                                     jax.nn.gelu(sv, approximate=False),
                                     sgu_ln_g[l], sgu_ln_b[l], sgu_w_s[l], sgu_b_s[l])
        mixed = jnp.concatenate([ret, sgu], axis=-1) @ w_out[l]
        x = x + mixed
        h = rmsnorm(x, ffn_norm_g[l])
        up = causal_depthwise_conv(h @ w_up[l], conv_w[l], conv_b[l])
        a, bgate = jnp.split(up, [D_FF], axis=-1)
        x = x + (jax.nn.silu(a) * bgate) @ w_down[l]

    return rmsnorm(x, final_norm_g)


import jax as _jax
import jax.numpy as _jnp

TWIN_FORMAT = 'train_step'
FWD_PARAMS = ['x', 'mix_norm_g', 'w_in', 'ret_norm_g', 'sgu_ln_g', 'sgu_ln_b', 'sgu_w_s', 'sgu_b_s', 'w_out', 'ffn_norm_g', 'w_up', 'conv_w', 'conv_b', 'w_down', 'final_norm_g']
TWIN_WEIGHTS = ['mix_norm_g', 'w_in', 'ret_norm_g', 'sgu_ln_g', 'sgu_ln_b', 'sgu_w_s', 'sgu_b_s', 'w_out', 'ffn_norm_g', 'w_up', 'conv_w', 'conv_b', 'w_down', 'final_norm_g']
TWIN_DIFF_INPUT = 'x'
TWIN_INPUTS = ['x', 'mix_norm_g', 'w_in', 'ret_norm_g', 'sgu_ln_g', 'sgu_ln_b', 'sgu_w_s', 'sgu_b_s', 'w_out', 'ffn_norm_g', 'w_up', 'conv_w', 'conv_b', 'w_down', 'final_norm_g', 'loss_target', 'm_mix_norm_g', 'm_w_in', 'm_ret_norm_g', 'm_sgu_ln_g', 'm_sgu_ln_b', 'm_sgu_w_s', 'm_sgu_b_s', 'm_w_out', 'm_ffn_norm_g', 'm_w_up', 'm_conv_w', 'm_conv_b', 'm_w_down', 'm_final_norm_g', 'v_mix_norm_g', 'v_w_in', 'v_ret_norm_g', 'v_sgu_ln_g', 'v_sgu_ln_b', 'v_sgu_w_s', 'v_sgu_b_s', 'v_w_out', 'v_ffn_norm_g', 'v_w_up', 'v_conv_w', 'v_conv_b', 'v_w_down', 'v_final_norm_g']
TWIN_OUTPUTS = ['loss', 'grad_x', 'grad_mix_norm_g', 'grad_w_in', 'grad_ret_norm_g', 'grad_sgu_ln_g', 'grad_sgu_ln_b', 'grad_sgu_w_s', 'grad_sgu_b_s', 'grad_w_out', 'grad_ffn_norm_g', 'grad_w_up', 'grad_conv_w', 'grad_conv_b', 'grad_w_down', 'grad_final_norm_g', 'delta_mix_norm_g', 'delta_w_in', 'delta_ret_norm_g', 'delta_sgu_ln_g', 'delta_sgu_ln_b', 'delta_sgu_w_s', 'delta_sgu_b_s', 'delta_w_out', 'delta_ffn_norm_g', 'delta_w_up', 'delta_conv_w', 'delta_conv_b', 'delta_w_down', 'delta_final_norm_g', 'new_m_mix_norm_g', 'new_m_w_in', 'new_m_ret_norm_g', 'new_m_sgu_ln_g', 'new_m_sgu_ln_b', 'new_m_sgu_w_s', 'new_m_sgu_b_s', 'new_m_w_out', 'new_m_ffn_norm_g', 'new_m_w_up', 'new_m_conv_w', 'new_m_conv_b', 'new_m_w_down', 'new_m_final_norm_g', 'new_v_mix_norm_g', 'new_v_w_in', 'new_v_ret_norm_g', 'new_v_sgu_ln_g', 'new_v_sgu_ln_b', 'new_v_sgu_w_s', 'new_v_sgu_b_s', 'new_v_w_out', 'new_v_ffn_norm_g', 'new_v_w_up', 'new_v_conv_w', 'new_v_conv_b', 'new_v_w_down', 'new_v_final_norm_g']
TWIN_LEAF_KINDS = {'loss': 'loss', 'grad_x': 'grad_x', 'grad_mix_norm_g': 'grad_w', 'grad_w_in': 'grad_w', 'grad_ret_norm_g': 'grad_w', 'grad_sgu_ln_g': 'grad_w', 'grad_sgu_ln_b': 'grad_w', 'grad_sgu_w_s': 'grad_w', 'grad_sgu_b_s': 'grad_w', 'grad_w_out': 'grad_w', 'grad_ffn_norm_g': 'grad_w', 'grad_w_up': 'grad_w', 'grad_conv_w': 'grad_w', 'grad_conv_b': 'grad_w', 'grad_w_down': 'grad_w', 'grad_final_norm_g': 'grad_w', 'delta_mix_norm_g': 'delta_w', 'delta_w_in': 'delta_w', 'delta_ret_norm_g': 'delta_w', 'delta_sgu_ln_g': 'delta_w', 'delta_sgu_ln_b': 'delta_w', 'delta_sgu_w_s': 'delta_w', 'delta_sgu_b_s': 'delta_w', 'delta_w_out': 'delta_w', 'delta_ffn_norm_g': 'delta_w', 'delta_w_up': 'delta_w', 'delta_conv_w': 'delta_w', 'delta_conv_b': 'delta_w', 'delta_w_down': 'delta_w', 'delta_final_norm_g': 'delta_w', 'new_m_mix_norm_g': 'new_m', 'new_m_w_in': 'new_m', 'new_m_ret_norm_g': 'new_m', 'new_m_sgu_ln_g': 'new_m', 'new_m_sgu_ln_b': 'new_m', 'new_m_sgu_w_s': 'new_m', 'new_m_sgu_b_s': 'new_m', 'new_m_w_out': 'new_m', 'new_m_ffn_norm_g': 'new_m', 'new_m_w_up': 'new_m', 'new_m_conv_w': 'new_m', 'new_m_conv_b': 'new_m', 'new_m_w_down': 'new_m', 'new_m_final_norm_g': 'new_m', 'new_v_mix_norm_g': 'new_v', 'new_v_w_in': 'new_v', 'new_v_ret_norm_g': 'new_v', 'new_v_sgu_ln_g': 'new_v', 'new_v_sgu_ln_b': 'new_v', 'new_v_sgu_w_s': 'new_v', 'new_v_sgu_b_s': 'new_v', 'new_v_w_out': 'new_v', 'new_v_ffn_norm_g': 'new_v', 'new_v_w_up': 'new_v', 'new_v_conv_w': 'new_v', 'new_v_conv_b': 'new_v', 'new_v_w_down': 'new_v', 'new_v_final_norm_g': 'new_v'}


def _forward(args):
    return _fwd_reference(*[args[k] for k in FWD_PARAMS])


def _output_shape():
    out = _jax.eval_shape(lambda: _forward(_fwd_setup_inputs(0)))
    return out.shape, out.dtype

N_MICROBATCH = 1
ADAM_LR = 0.001
ADAM_B1 = 0.9
ADAM_B2 = 0.999
ADAM_EPS = 1e-08
ADAM_WD = 0.01
ADAM_STEP = 10
PER_EXAMPLE_BATCH_AXIS = {'x': 0, 'loss_target': 0}
SHARED_INPUTS = []
_WEIGHT_DTYPES = {'mix_norm_g': _jnp.float32, 'w_in': _jnp.float32, 'ret_norm_g': _jnp.float32, 'sgu_ln_g': _jnp.float32, 'sgu_ln_b': _jnp.float32, 'sgu_w_s': _jnp.float32, 'sgu_b_s': _jnp.float32, 'w_out': _jnp.float32, 'ffn_norm_g': _jnp.float32, 'w_up': _jnp.float32, 'conv_w': _jnp.float32, 'conv_b': _jnp.float32, 'w_down': _jnp.float32, 'final_norm_g': _jnp.float32}
MOMENT_SCALE = {'mix_norm_g': 1.331419e-01, 'w_in': 7.192684e-02, 'ret_norm_g': 7.128002e-02, 'sgu_ln_g': 5.185776e-02, 'sgu_ln_b': 5.370906e-02, 'sgu_w_s': 5.180298e-02, 'sgu_b_s': 7.527820e-02, 'w_out': 7.981218e-02, 'ffn_norm_g': 8.418449e-02, 'w_up': 3.509757e-02, 'conv_w': 3.493303e-02, 'conv_b': 3.429519e-02, 'w_down': 5.746124e-02, 'final_norm_g': 1.599236e+01}


def _to_microbatches(a, axis):
    t = _jnp.moveaxis(a, axis, 0)
    t = t.reshape((N_MICROBATCH, t.shape[0] // N_MICROBATCH) + t.shape[1:])
    return _jnp.moveaxis(t, 1, axis + 1)


def setup_inputs(seed: int = 0) -> dict:
    inp = _fwd_setup_inputs(seed)
    key = _jax.random.fold_in(_jax.random.key(seed), 7919)
    shape, _ = _output_shape()
    out = dict(inp)
    out["loss_target"] = _jax.random.normal(_jax.random.fold_in(key, 0), shape, _jnp.float32)
    for i, name in enumerate(TWIN_WEIGHTS):
        w = inp[name].astype(_jnp.float32)
        if MOMENT_SCALE is None:
            s = _jnp.sqrt(_jnp.mean(_jnp.square(w)) + 1e-30)
        else:
            s = MOMENT_SCALE[name]
        km, kv = _jax.random.split(_jax.random.fold_in(key, i + 1))
        out[name] = w
        out["m_" + name] = s * _jax.random.normal(km, w.shape, _jnp.float32)
        out["v_" + name] = (s * s) * _jax.random.uniform(kv, w.shape, _jnp.float32, 0.5, 1.5)
    if N_MICROBATCH > 1:
        for name, axis in PER_EXAMPLE_BATCH_AXIS.items():
            out[name] = _to_microbatches(out[name], axis)
    return {'x': out['x'], 'mix_norm_g': out['mix_norm_g'], 'w_in': out['w_in'], 'ret_norm_g': out['ret_norm_g'], 'sgu_ln_g': out['sgu_ln_g'], 'sgu_ln_b': out['sgu_ln_b'], 'sgu_w_s': out['sgu_w_s'], 'sgu_b_s': out['sgu_b_s'], 'w_out': out['w_out'], 'ffn_norm_g': out['ffn_norm_g'], 'w_up': out['w_up'], 'conv_w': out['conv_w'], 'conv_b': out['conv_b'], 'w_down': out['w_down'], 'final_norm_g': out['final_norm_g'], 'loss_target': out['loss_target'], 'm_mix_norm_g': out['m_mix_norm_g'], 'm_w_in': out['m_w_in'], 'm_ret_norm_g': out['m_ret_norm_g'], 'm_sgu_ln_g': out['m_sgu_ln_g'], 'm_sgu_ln_b': out['m_sgu_ln_b'], 'm_sgu_w_s': out['m_sgu_w_s'], 'm_sgu_b_s': out['m_sgu_b_s'], 'm_w_out': out['m_w_out'], 'm_ffn_norm_g': out['m_ffn_norm_g'], 'm_w_up': out['m_w_up'], 'm_conv_w': out['m_conv_w'], 'm_conv_b': out['m_conv_b'], 'm_w_down': out['m_w_down'], 'm_final_norm_g': out['m_final_norm_g'], 'v_mix_norm_g': out['v_mix_norm_g'], 'v_w_in': out['v_w_in'], 'v_ret_norm_g': out['v_ret_norm_g'], 'v_sgu_ln_g': out['v_sgu_ln_g'], 'v_sgu_ln_b': out['v_sgu_ln_b'], 'v_sgu_w_s': out['v_sgu_w_s'], 'v_sgu_b_s': out['v_sgu_b_s'], 'v_w_out': out['v_w_out'], 'v_ffn_norm_g': out['v_ffn_norm_g'], 'v_w_up': out['v_w_up'], 'v_conv_w': out['v_conv_w'], 'v_conv_b': out['v_conv_b'], 'v_w_down': out['v_w_down'], 'v_final_norm_g': out['v_final_norm_g']}


def _loss(weights, diff, rest, loss_target):
    with _jax.named_scope("forward"):
        args = {**rest, TWIN_DIFF_INPUT: diff, **{k: w.astype(_WEIGHT_DTYPES[k]) for k, w in weights.items()}}
        y = _forward(args)
    with _jax.named_scope("loss_head"):
        err = _jnp.square(y.astype(_jnp.float32) - loss_target)
        return 0.5 * _jnp.sum(_jnp.mean(err, axis=-1)) if err.ndim else 0.5 * err


def _adamw(w, g, m, v):
    m = ADAM_B1 * m + (1.0 - ADAM_B1) * g
    v = ADAM_B2 * v + (1.0 - ADAM_B2) * _jnp.square(g)
    m_hat = m / (1.0 - ADAM_B1 ** ADAM_STEP)
    v_hat = v / (1.0 - ADAM_B2 ** ADAM_STEP)
    delta = -ADAM_LR * (m_hat / (_jnp.sqrt(v_hat) + ADAM_EPS) + ADAM_WD * w)
    return delta, m, v


def reference(x, mix_norm_g, w_in, ret_norm_g, sgu_ln_g, sgu_ln_b, sgu_w_s, sgu_b_s, w_out, ffn_norm_g, w_up, conv_w, conv_b, w_down, final_norm_g, loss_target, m_mix_norm_g, m_w_in, m_ret_norm_g, m_sgu_ln_g, m_sgu_ln_b, m_sgu_w_s, m_sgu_b_s, m_w_out, m_ffn_norm_g, m_w_up, m_conv_w, m_conv_b, m_w_down, m_final_norm_g, v_mix_norm_g, v_w_in, v_ret_norm_g, v_sgu_ln_g, v_sgu_ln_b, v_sgu_w_s, v_sgu_b_s, v_w_out, v_ffn_norm_g, v_w_up, v_conv_w, v_conv_b, v_w_down, v_final_norm_g):
    given = dict(x=x, mix_norm_g=mix_norm_g, w_in=w_in, ret_norm_g=ret_norm_g, sgu_ln_g=sgu_ln_g, sgu_ln_b=sgu_ln_b, sgu_w_s=sgu_w_s, sgu_b_s=sgu_b_s, w_out=w_out, ffn_norm_g=ffn_norm_g, w_up=w_up, conv_w=conv_w, conv_b=conv_b, w_down=w_down, final_norm_g=final_norm_g, loss_target=loss_target, m_mix_norm_g=m_mix_norm_g, m_w_in=m_w_in, m_ret_norm_g=m_ret_norm_g, m_sgu_ln_g=m_sgu_ln_g, m_sgu_ln_b=m_sgu_ln_b, m_sgu_w_s=m_sgu_w_s, m_sgu_b_s=m_sgu_b_s, m_w_out=m_w_out, m_ffn_norm_g=m_ffn_norm_g, m_w_up=m_w_up, m_conv_w=m_conv_w, m_conv_b=m_conv_b, m_w_down=m_w_down, m_final_norm_g=m_final_norm_g, v_mix_norm_g=v_mix_norm_g, v_w_in=v_w_in, v_ret_norm_g=v_ret_norm_g, v_sgu_ln_g=v_sgu_ln_g, v_sgu_ln_b=v_sgu_ln_b, v_sgu_w_s=v_sgu_w_s, v_sgu_b_s=v_sgu_b_s, v_w_out=v_w_out, v_ffn_norm_g=v_ffn_norm_g, v_w_up=v_w_up, v_conv_w=v_conv_w, v_conv_b=v_conv_b, v_w_down=v_w_down, v_final_norm_g=v_final_norm_g)
    weights = {n: given[n] for n in TWIN_WEIGHTS}
    shared = {n: given[n] for n in SHARED_INPUTS}
    per_example = {n: given[n] for n in ['x']}
    grad_fn = _jax.value_and_grad(_loss, argnums=(0, 1))

    def one_microbatch(ex, loss_target):
        ex = dict(ex)
        diff = ex.pop(TWIN_DIFF_INPUT)
        return grad_fn(weights, diff, {**shared, **ex}, loss_target)

    if N_MICROBATCH == 1:
        loss, (grad_w, grad_x) = one_microbatch(per_example, given["loss_target"])
    else:
        def body(carry, xs):
            loss_sum, grad_sum = carry
            l_k, (gw_k, gx_k) = one_microbatch(xs[0], xs[1])
            with _jax.named_scope("update"):
                return (loss_sum + l_k, _jax.tree.map(_jnp.add, grad_sum, gw_k)), gx_k

        init = (_jnp.zeros((), _jnp.float32), _jax.tree.map(_jnp.zeros_like, weights))
        (loss, grad_w), grad_x = _jax.lax.scan(body, init, (per_example, given["loss_target"]))
    with _jax.named_scope("update"):
        delta_w, new_m, new_v = {}, {}, {}
        for n in TWIN_WEIGHTS:
            delta_w[n], new_m[n], new_v[n] = _adamw(weights[n], grad_w[n], given["m_" + n], given["v_" + n])
    return (loss, grad_x, *[grad_w[n] for n in TWIN_WEIGHTS], *[delta_w[n] for n in TWIN_WEIGHTS],
            *[new_m[n] for n in TWIN_WEIGHTS], *[new_v[n] for n in TWIN_WEIGHTS])
```

```python
import functools
import math

import jax
import jax.numpy as jnp
import numpy as np
from jax import lax
from jax.experimental import pallas as pl
from jax.experimental.pallas import tpu as pltpu

F32 = jnp.float32
BF16 = jnp.bfloat16
MESH = pl.DeviceIdType.MESH

N_DEV = 8
SEQ = 2048
D_MODEL = 1024
CHUNK = 128
N_CHUNK = SEQ // CHUNK
HEADS = 4
HEAD_DIM = 128
RET_W = 512
SGU_W = 512
PROJ_W = 3072
D_FF = 2816
FF_SHARD = 704
N_FF_PAIR = 4
IN_SHARD = PROJ_W // N_DEV
OUT_SHARD = D_MODEL // N_DEV
DOWN_SHARD = D_FF // N_DEV
TM = 256
N_TB = SEQ // TM
EPS = 1e-6
ROPE_BASE = 10000.0
K_SCALE = HEAD_DIM ** -0.5
INV_SQRT2 = 0.7071067811865476
INV_SQRT_2PI = 0.3989422804014327

ADAM_LR = 0.001
ADAM_B1 = 0.9
ADAM_B2 = 0.999
ADAM_EPS = 1e-08
ADAM_WD = 0.01
ADAM_STEP = 10

VMEM_LIMIT = 56 * 1024 * 1024


def _cparams(sem=None, vmem=VMEM_LIMIT):
    return pltpu.CompilerParams(dimension_semantics=sem, vmem_limit_bytes=vmem)


def _resident(shape):
    nd = len(shape)
    return pl.BlockSpec(shape, lambda *_: (0,) * nd, pipeline_mode=pl.Buffered(1))


def _dot(a, b):
    return jnp.dot(a, b, preferred_element_type=F32)


def _dot_nt(a, b):
    return lax.dot_general(a, b, (((1,), (1,)), ((), ())), preferred_element_type=F32)


def _dot_tn(a, b):
    return lax.dot_general(a, b, (((0,), (0,)), ((), ())), preferred_element_type=F32)


def _sigmoid(x):
    return 1.0 / (1.0 + jnp.exp(-x))


def _gelu(x):
    return 0.5 * x * (1.0 + lax.erf(x * INV_SQRT2))


def _gelu_grad(x):
    return 0.5 * (1.0 + lax.erf(x * INV_SQRT2)) + x * (jnp.exp(-0.5 * x * x) * INV_SQRT_2PI)


def _rot(xh, cos2, sin2):
    return xh * cos2 + pltpu.roll(xh, HEAD_DIM // 2, 1) * sin2


def _rot_t(dh, cos2, sin2):
    return dh * cos2 + pltpu.roll(dh * sin2, HEAD_DIM // 2, 1)


def _rope_tables():
    half = HEAD_DIM // 2
    inv_freq = jnp.power(ROPE_BASE, -jnp.arange(half, dtype=F32) / half)
    ang = jnp.arange(SEQ, dtype=F32)[:, None] * inv_freq[None, :]
    cos, sin = jnp.cos(ang), jnp.sin(ang)
    cos2 = jnp.tile(jnp.concatenate([cos, cos], axis=-1), (1, HEADS))
    sin2 = jnp.tile(jnp.concatenate([-sin, sin], axis=-1), (1, HEADS))
    return cos2, sin2


def _decay_tables():
    log_gamma = jnp.log(1.0 - jnp.power(2.0, -5.0 - jnp.arange(HEADS, dtype=F32)))
    pos = jnp.arange(CHUNK, dtype=F32)
    diff = pos[:, None] - pos[None, :]
    mask = jnp.where(diff >= 0.0, jnp.exp(log_gamma[:, None, None] * jnp.maximum(diff, 0.0)[None]), 0.0)
    k_decay = jnp.exp(log_gamma[:, None] * (CHUNK - 1.0 - pos)[None])
    q_decay = jnp.exp(log_gamma[:, None] * (pos + 1.0)[None])
    kd = jnp.broadcast_to(k_decay[:, :, None], (HEADS, CHUNK, HEAD_DIM))
    qd = jnp.broadcast_to(q_decay[:, :, None], (HEADS, CHUNK, HEAD_DIM))
    return mask.astype(F32), qd.astype(F32), kd.astype(F32)


def _chunk_decay():
    lg = np.log(np.float32(1.0) - np.power(np.float32(2.0), -5.0 - np.arange(HEADS, dtype=np.float32))).astype(np.float32)
    return [float(np.exp(lg[h] * np.float32(CHUNK))) for h in range(HEADS)]


def _all_gather_weights(w_in, w_out, w_up, w_down):
    n_w = 4

    def body(in_ref, out_ref, up_ref, dn_ref, gin, gout, gup, gdn, s_in, s_out, s_up, s_dn, send_sems, recv_sems, local_sems):
        x, y, c = lax.axis_index("x"), lax.axis_index("y"), lax.axis_index("c")
        sibling = (x, y, 1 - c)
        chips = [(1 - x, y), (x, 1 - y), (1 - x, 1 - y)]
        stages = [s_in, s_out, s_up, s_dn]
        s_in[...] = in_ref[...].astype(BF16)
        s_out[...] = out_ref[...].astype(BF16)
        s_up[...] = up_ref[...].astype(BF16)
        s_dn[...] = dn_ref[...].astype(BF16)

        def slot(w, px, py, pc):
            dev = 4 * px + 2 * py + pc
            if w == 0:
                return gin.at[:, pl.ds(pl.multiple_of(dev * IN_SHARD, 128), IN_SHARD)]
            if w == 1:
                return gout.at[pl.ds(pl.multiple_of(dev * OUT_SHARD, 128), OUT_SHARD), :]
            if w == 2:
                return gup.at[dev]
            return gdn.at[pl.ds(pl.multiple_of(dev * DOWN_SHARD, 32), DOWN_SHARD), :]

        def copy(w, k, block, to, src=None):
            return pltpu.make_async_remote_copy(
                src_ref=slot(w, *block) if src is None else src, dst_ref=slot(w, *block),
                send_sem=send_sems.at[w, k], recv_sem=recv_sems.at[w, k], device_id=to, device_id_type=MESH)

        me = (x, y, c)
        mine = [pltpu.make_async_copy(stages[w], slot(w, *me), local_sems.at[w]) for w in range(n_w)]
        for cp in mine:
            cp.start()
        first = []
        for w in range(n_w):
            first.append(copy(w, 0, me, sibling, src=stages[w]))
            first += [copy(w, 1 + j, me, (*chip, c), src=stages[w]) for j, chip in enumerate(chips)]
        for cp in first:
            cp.start()
        passed = []
        for w in range(n_w):
            for j, chip in enumerate(chips):
                copy(w, 1 + j, (*chip, c), me).wait_recv()
                fwd = copy(w, 4 + j, (*chip, c), sibling)
                fwd.start()
                passed.append(fwd)
        for w in range(n_w):
            copy(w, 0, sibling, me).wait_recv()
            for j, chip in enumerate(chips):
                copy(w, 4 + j, (*chip, 1 - c), me).wait_recv()
        for cp in first + passed:
            cp.wait_send()
        for cp in mine:
            cp.wait()

    vm = pl.BlockSpec(memory_space=pltpu.VMEM)
    hbm = pl.BlockSpec(memory_space=pl.ANY)
    return pl.pallas_call(
        body, name="ag_weights",
        out_shape=(jax.ShapeDtypeStruct((D_MODEL, PROJ_W), BF16), jax.ShapeDtypeStruct((D_MODEL, D_MODEL), BF16),
                   jax.ShapeDtypeStruct((N_DEV, D_MODEL, FF_SHARD), BF16), jax.ShapeDtypeStruct((D_FF, D_MODEL), BF16)),
        in_specs=[vm, vm, vm, vm], out_specs=(hbm, hbm, hbm, hbm),
        scratch_shapes=[pltpu.VMEM((D_MODEL, IN_SHARD), BF16), pltpu.VMEM((OUT_SHARD, D_MODEL), BF16),
                        pltpu.VMEM((D_MODEL, FF_SHARD), BF16), pltpu.VMEM((DOWN_SHARD, D_MODEL), BF16),
                        pltpu.SemaphoreType.DMA((n_w, 7)), pltpu.SemaphoreType.DMA((n_w, 7)), pltpu.SemaphoreType.DMA((n_w,))],
        compiler_params=_cparams(),
    )(w_in, w_out, w_up, w_down)


def _fwd_proj(x, g1, win_g, cos2, sin2):
    def body(x_ref, g_ref, w_ref, cos_ref, sin_ref, proj_ref, h1_ref):
        xb = x_ref[...]
        r = lax.rsqrt(jnp.mean(xb * xb, axis=-1, keepdims=True) + EPS)
        h = ((xb * r) * g_ref[...]).astype(BF16)
        h1_ref[...] = h
        p = _dot(h, w_ref[...])
        for hd in range(HEADS):
            sl = slice(hd * HEAD_DIM, (hd + 1) * HEAD_DIM)
            c2, s2 = cos_ref[:, sl], sin_ref[:, sl]
            proj_ref[:, sl] = _rot(p[:, sl], c2, s2)
            ks = slice(RET_W + hd * HEAD_DIM, RET_W + (hd + 1) * HEAD_DIM)
            proj_ref[:, ks] = _rot(p[:, ks], c2, s2) * K_SCALE
        proj_ref[:, 2 * RET_W:] = p[:, 2 * RET_W:]

    tok = lambda w: pl.BlockSpec((TM, w), lambda i: (i, 0))
    return pl.pallas_call(
        body, name="fwd_proj", grid=(N_TB,),
        out_shape=(jax.ShapeDtypeStruct((SEQ, PROJ_W), F32), jax.ShapeDtypeStruct((SEQ, D_MODEL), BF16)),
        in_specs=[tok(D_MODEL), _resident((1, D_MODEL)), _resident((D_MODEL, PROJ_W)), tok(RET_W), tok(RET_W)],
        out_specs=(tok(PROJ_W), tok(D_MODEL)),
        compiler_params=_cparams(("parallel",)),
    )(x, g1, win_g, cos2, sin2)


def _causal(w):
    r = lax.broadcasted_iota(jnp.int32, (CHUNK, CHUNK), 0)
    c = lax.broadcasted_iota(jnp.int32, (CHUNK, CHUNK), 1)
    return jnp.where(r >= c, w, 0.0)


def _fwd_mix(x, proj, wout_g, grn, lng, lnb, ws, bsb, mask, qdec, kdec):
    cdec = _chunk_decay()

    def body(x_ref, p_ref, w_ref, grn_ref, lng_ref, lnb_ref, ws_ref, bsb_ref, m_ref, qd_ref, kd_ref,
             x2_ref, cat_ref, o_ref, sp_ref, state):
        @pl.when(pl.program_id(0) == 0)
        def _():
            state[...] = jnp.zeros_like(state)

        for h in range(HEADS):
            sl = slice(h * HEAD_DIM, (h + 1) * HEAD_DIM)
            q = p_ref[:, sl]
            k = p_ref[:, RET_W + h * HEAD_DIM:RET_W + (h + 1) * HEAD_DIM]
            v = p_ref[:, 2 * RET_W + h * HEAD_DIM:2 * RET_W + (h + 1) * HEAD_DIM]
            g = p_ref[:, 3 * RET_W + h * HEAD_DIM:3 * RET_W + (h + 1) * HEAD_DIM]
            qb, kb, vb = q.astype(BF16), k.astype(BF16), v.astype(BF16)
            a = _dot_nt(qb, kb) * m_ref[h]
            spb = state[h].astype(BF16)
            sp_ref[0, h] = spb
            o = _dot(a.astype(BF16), vb) + _dot((q * qd_ref[h]).astype(BF16), spb)
            state[h] = state[h] * cdec[h] + _dot_tn((k * kd_ref[h]).astype(BF16), vb)
            o_ref[:, sl] = o
            rinv = lax.rsqrt(jnp.mean(o * o, axis=-1, keepdims=True) + EPS)
            rn = (o * rinv) * grn_ref[:, sl]
            cat_ref[:, sl] = ((g * _sigmoid(g)) * rn).astype(BF16)
        for gi in range(HEADS):
            sl = slice(gi * HEAD_DIM, (gi + 1) * HEAD_DIM)
            u = p_ref[:, 4 * RET_W + gi * HEAD_DIM:4 * RET_W + (gi + 1) * HEAD_DIM]
            sv = p_ref[:, 4 * RET_W + SGU_W + gi * HEAD_DIM:4 * RET_W + SGU_W + (gi + 1) * HEAD_DIM]
            gv = _gelu(sv)
            xc = gv - jnp.mean(gv, axis=-1, keepdims=True)
            vn = (xc * lax.rsqrt(jnp.mean(xc * xc, axis=-1, keepdims=True) + EPS)) * lng_ref[:, sl] + lnb_ref[:, sl]
            mixed = _dot(_causal(ws_ref[gi]).astype(BF16), vn.astype(BF16)) + bsb_ref[gi]
            cat_ref[:, RET_W + gi * HEAD_DIM:RET_W + (gi + 1) * HEAD_DIM] = (_gelu(u) * mixed).astype(BF16)
        x2_ref[...] = x_ref[...] + _dot(cat_ref[...], w_ref[...])

    ch = lambda w: pl.BlockSpec((CHUNK, w), lambda i: (i, 0))
    hcc = (HEADS, CHUNK, CHUNK)
    return pl.pallas_call(
        body, name="fwd_mix", grid=(N_CHUNK,),
        out_shape=(jax.ShapeDtypeStruct((SEQ, D_MODEL), F32), jax.ShapeDtypeStruct((SEQ, D_MODEL), BF16),
                   jax.ShapeDtypeStruct((SEQ, RET_W), F32), jax.ShapeDtypeStruct((N_CHUNK, HEADS, HEAD_DIM, HEAD_DIM), BF16)),
        in_specs=[ch(D_MODEL), ch(PROJ_W), _resident((D_MODEL, D_MODEL)), _resident((1, RET_W)), _resident((1, SGU_W)),
                  _resident((1, SGU_W)), _resident(hcc), _resident(hcc), _resident(hcc), _resident(hcc), _resident(hcc)],
        out_specs=(ch(D_MODEL), ch(D_MODEL), ch(RET_W), pl.BlockSpec((1, HEADS, HEAD_DIM, HEAD_DIM), lambda i: (i, 0, 0, 0))),
        scratch_shapes=[pltpu.VMEM((HEADS, HEAD_DIM, HEAD_DIM), F32)],
        compiler_params=_cparams(("arbitrary",)),
    )(x, proj, wout_g, grn, lng, lnb, ws, bsb, mask, qdec, kdec)


def _conv_taps(p, prev8):
    row = lax.broadcasted_iota(jnp.int32, p.shape, 0)
    p1 = jnp.where(row == 0, prev8[7:8, :], pltpu.roll(p, 1, 0))
    p2 = jnp.where(row == 0, prev8[6:7, :], jnp.where(row == 1, prev8[7:8, :], pltpu.roll(p, 2, 0)))
    return p1, p2


def _fwd_ffn(x2, g2, wup_g, cw_g, cb_g, wdn_g, gf, tgt):
    def body(x_ref, g_ref, wu_ref, cw_ref, cb_ref, wd_ref, gf_ref, t_ref, h2_ref, up_ref, act_ref, x3_ref, loss_ref, carry):
        @pl.when(pl.program_id(0) == 0)
        def _():
            carry[...] = jnp.zeros_like(carry)

        xb = x_ref[...]
        r = lax.rsqrt(jnp.mean(xb * xb, axis=-1, keepdims=True) + EPS)
        h = ((xb * r) * g_ref[...]).astype(BF16)
        h2_ref[...] = h
        acc = xb
        for j in range(N_FF_PAIR):
            u = []
            for s in (j, j + N_FF_PAIR):
                p = _dot(h, wu_ref[s])
                up_ref[s] = p.astype(BF16)
                p1, p2 = _conv_taps(p, carry[s])
                carry[s] = p[TM - 8:, :]
                cw = cw_ref[s]
                u.append(p2 * cw[0:1, :] + p1 * cw[1:2, :] + p * cw[2:3, :] + cb_ref[s])
            a = ((u[0] * _sigmoid(u[0])) * u[1]).astype(BF16)
            act_ref[j] = a
            acc = acc + _dot(a, wd_ref[pl.ds(j * FF_SHARD, FF_SHARD), :])
        x3_ref[...] = acc
        r3 = lax.rsqrt(jnp.mean(acc * acc, axis=-1, keepdims=True) + EPS)
        diff = (acc * r3) * gf_ref[...] - t_ref[...]
        loss_ref[...] = jnp.full(loss_ref.shape, 0.5 * jnp.sum(jnp.mean(diff * diff, axis=-1)), F32)

    tok = lambda w: pl.BlockSpec((TM, w), lambda i: (i, 0))
    return pl.pallas_call(
        body, name="fwd_ffn", grid=(N_TB,),
        out_shape=(jax.ShapeDtypeStruct((SEQ, D_MODEL), BF16), jax.ShapeDtypeStruct((N_DEV, SEQ, FF_SHARD), BF16),
                   jax.ShapeDtypeStruct((N_FF_PAIR, SEQ, FF_SHARD), BF16), jax.ShapeDtypeStruct((SEQ, D_MODEL), F32),
                   jax.ShapeDtypeStruct((N_TB, 8, 128), F32)),
        in_specs=[tok(D_MODEL), _resident((1, D_MODEL)), _resident((N_DEV, D_MODEL, FF_SHARD)), _resident((N_DEV, 3, FF_SHARD)),
                  _resident((N_DEV, 1, FF_SHARD)), _resident((D_FF, D_MODEL)), _resident((1, D_MODEL)), tok(D_MODEL)],
        out_specs=(tok(D_MODEL), pl.BlockSpec((N_DEV, TM, FF_SHARD), lambda i: (0, i, 0)),
                   pl.BlockSpec((N_FF_PAIR, TM, FF_SHARD), lambda i: (0, i, 0)), tok(D_MODEL),
                   pl.BlockSpec((1, 8, 128), lambda i: (i, 0, 0))),
        scratch_shapes=[pltpu.VMEM((N_DEV, 8, FF_SHARD), F32)],
        compiler_params=_cparams(("arbitrary",)),
    )(x2, g2, wup_g, cw_g, cb_g, wdn_g, gf, tgt)


def _bwd_ffn(x3, tgt, gf, x2, g2, up_pre, wup_g, cw_g, cb_g, wdn_g):
    rows16 = TM // 16

    def body(x3_ref, t_ref, gf_ref, x2_ref, g2_ref, up_ref, halo_ref, wu_ref, cw_ref, cb_ref, wd_ref,
             dx3_ref, dpre_ref, dx2_ref, dgf_ref, dg2_ref, dcv_ref, nxt):
        i = pl.program_id(0)

        @pl.when(i == 0)
        def _():
            nxt[...] = jnp.zeros_like(nxt)
            dgf_ref[...] = jnp.zeros_like(dgf_ref)
            dg2_ref[...] = jnp.zeros_like(dg2_ref)
            dcv_ref[...] = jnp.zeros_like(dcv_ref)

        first_block = i == N_TB - 1
        x3 = x3_ref[...]
        r3 = lax.rsqrt(jnp.mean(x3 * x3, axis=-1, keepdims=True) + EPS)
        xh3 = x3 * r3
        dy = (xh3 * gf_ref[...] - t_ref[...]) * (1.0 / D_MODEL)
        dgf_ref[0:1, :] += jnp.sum(dy * xh3, axis=0, keepdims=True)
        t3 = dy * gf_ref[...]
        dx3 = r3 * (t3 - xh3 * jnp.mean(t3 * xh3, axis=-1, keepdims=True))
        dx3b = dx3.astype(BF16)
        dx3_ref[...] = dx3b
        dh2 = jnp.zeros((TM, D_MODEL), F32)
        row = lax.broadcasted_iota(jnp.int32, (TM, FF_SHARD), 0)
        for j in range(N_FF_PAIR):
            dact = _dot_nt(dx3b, wd_ref[pl.ds(j * FF_SHARD, FF_SHARD), :])
            u, taps = [], []
            for s in (j, j + N_FF_PAIR):
                p = up_ref[s].astype(F32)
                prev8 = jnp.where(first_block, 0.0, halo_ref[s, 8:16, :].astype(F32))
                p1, p2 = _conv_taps(p, prev8)
                cw = cw_ref[s]
                u.append(p2 * cw[0:1, :] + p1 * cw[1:2, :] + p * cw[2:3, :] + cb_ref[s])
                taps.append((p2, p1, p))
            sg = _sigmoid(u[0])
            du = [dact * u[1] * (sg * (1.0 + u[0] * (1.0 - sg))), dact * (u[0] * sg)]
            for n, s in enumerate((j, j + N_FF_PAIR)):
                d = du[n]
                cw = cw_ref[s]
                nx = nxt[s]
                n1 = jnp.where(row == TM - 1, nx[0:1, :], pltpu.roll(d, TM - 1, 0))
                n2 = jnp.where(row == TM - 2, nx[0:1, :], jnp.where(row == TM - 1, nx[1:2, :], pltpu.roll(d, TM - 2, 0)))
                nxt[s] = d[0:8, :]
                dp = (d * cw[2:3, :] + n1 * cw[1:2, :] + n2 * cw[0:1, :]).astype(BF16)
                dpre_ref[s] = dp
                for tap in range(3):
                    dcv_ref[s, tap:tap + 1, :] += jnp.sum(d * taps[n][tap], axis=0, keepdims=True)
                dcv_ref[s, 3:4, :] += jnp.sum(d, axis=0, keepdims=True)
                dh2 = dh2 + _dot_nt(dp, wu_ref[s])
        x2 = x2_ref[...]
        r2 = lax.rsqrt(jnp.mean(x2 * x2, axis=-1, keepdims=True) + EPS)
        xh2 = x2 * r2
        dg2_ref[0:1, :] += jnp.sum(dh2 * xh2, axis=0, keepdims=True)
        t2 = dh2 * g2_ref[...]
        dx2_ref[...] = dx3 + r2 * (t2 - xh2 * jnp.mean(t2 * xh2, axis=-1, keepdims=True))

    rev = lambda w: pl.BlockSpec((TM, w), lambda i: (N_TB - 1 - i, 0))
    rev3 = lambda: pl.BlockSpec((N_DEV, TM, FF_SHARD), lambda i: (0, N_TB - 1 - i, 0))
    halo = pl.BlockSpec((N_DEV, 16, FF_SHARD), lambda i: (0, jnp.maximum((N_TB - 1 - i) * rows16 - 1, 0), 0))
    acc = lambda s: pl.BlockSpec(s, lambda i: (0,) * len(s))
    return pl.pallas_call(
        body, name="bwd_ffn", grid=(N_TB,),
        out_shape=(jax.ShapeDtypeStruct((SEQ, D_MODEL), BF16), jax.ShapeDtypeStruct((N_DEV, SEQ, FF_SHARD), BF16),
                   jax.ShapeDtypeStruct((SEQ, D_MODEL), F32), jax.ShapeDtypeStruct((8, D_MODEL), F32),
                   jax.ShapeDtypeStruct((8, D_MODEL), F32), jax.ShapeDtypeStruct((N_DEV, 8, FF_SHARD), F32)),
        in_specs=[rev(D_MODEL), rev(D_MODEL), _resident((1, D_MODEL)), rev(D_MODEL), _resident((1, D_MODEL)), rev3(), halo,
                  _resident((N_DEV, D_MODEL, FF_SHARD)), _resident((N_DEV, 3, FF_SHARD)), _resident((N_DEV, 1, FF_SHARD)),
                  _resident((D_FF, D_MODEL))],
        out_specs=(rev(D_MODEL), rev3(), rev(D_MODEL), acc((8, D_MODEL)), acc((8, D_MODEL)), acc((N_DEV, 8, FF_SHARD))),
        scratch_shapes=[pltpu.VMEM((N_DEV, 8, FF_SHARD), F32)],
        compiler_params=_cparams(("arbitrary",)),
    )(x3, tgt, gf, x2, g2, up_pre, up_pre, wup_g, cw_g, cb_g, wdn_g)


def _bwd_mix(dx2, proj, o, sprev, wout_g, grn, lng, lnb, ws, bsb, mask, qdec, kdec, cos2, sin2):
    cdec = _chunk_decay()

    def body(dx2_ref, p_ref, o_ref, sp_ref, w_ref, grn_ref, lng_ref, lnb_ref, ws_ref, bsb_ref, m_ref, qd_ref, kd_ref,
             cos_ref, sin_ref, dp_ref, dgrn_ref, dlng_ref, dlnb_ref, dws_ref, dbs_ref, dstate):
        i = pl.program_id(0)

        @pl.when(i == 0)
        def _():
            dstate[...] = jnp.zeros_like(dstate)
            dgrn_ref[...] = jnp.zeros_like(dgrn_ref)
            dlng_ref[...] = jnp.zeros_like(dlng_ref)
            dlnb_ref[...] = jnp.zeros_like(dlnb_ref)
            dws_ref[...] = jnp.zeros_like(dws_ref)
            dbs_ref[...] = jnp.zeros_like(dbs_ref)

        dmix = _dot_nt(dx2_ref[...].astype(BF16), w_ref[...])
        for h in range(HEADS):
            sl = slice(h * HEAD_DIM, (h + 1) * HEAD_DIM)
            q = p_ref[:, sl]
            k = p_ref[:, RET_W + h * HEAD_DIM:RET_W + (h + 1) * HEAD_DIM]
            v = p_ref[:, 2 * RET_W + h * HEAD_DIM:2 * RET_W + (h + 1) * HEAD_DIM]
            g = p_ref[:, 3 * RET_W + h * HEAD_DIM:3 * RET_W + (h + 1) * HEAD_DIM]
            o = o_ref[:, sl]
            rinv = lax.rsqrt(jnp.mean(o * o, axis=-1, keepdims=True) + EPS)
            oh = o * rinv
            gr = grn_ref[:, sl]
            sg = _sigmoid(g)
            dret = dmix[:, sl]
            dp_ref[:, 3 * RET_W + h * HEAD_DIM:3 * RET_W + (h + 1) * HEAD_DIM] = (
                dret * (oh * gr) * (sg * (1.0 + g * (1.0 - sg)))).astype(BF16)
            drn = dret * (g * sg)
            dgrn_ref[0:1, sl] += jnp.sum(drn * oh, axis=0, keepdims=True)
            t = drn * gr
            do = rinv * (t - oh * jnp.mean(t * oh, axis=-1, keepdims=True))
            qb, kb, vb, dob = q.astype(BF16), k.astype(BF16), v.astype(BF16), do.astype(BF16)
            m = m_ref[h]
            ab = (_dot_nt(qb, kb) * m).astype(BF16)
            dab = (_dot_nt(dob, vb) * m).astype(BF16)
            spb = sp_ref[0, h]
            dsn = dstate[h]
            dsnb = dsn.astype(BF16)
            qdb = (q * qd_ref[h]).astype(BF16)
            kdb = (k * kd_ref[h]).astype(BF16)
            dq = _dot(dab, kb) + _dot_nt(dob, spb) * qd_ref[h]
            dk = _dot_tn(dab, qb) + _dot_nt(vb, dsnb) * kd_ref[h]
            dv = _dot_tn(ab, dob) + _dot(kdb, dsnb)
            dstate[h] = dsn * cdec[h] + _dot_tn(qdb, dob)
            c2, s2 = cos_ref[:, sl], sin_ref[:, sl]
            dp_ref[:, sl] = _rot_t(dq, c2, s2).astype(BF16)
            dp_ref[:, RET_W + h * HEAD_DIM:RET_W + (h + 1) * HEAD_DIM] = _rot_t(dk * K_SCALE, c2, s2).astype(BF16)
            dp_ref[:, 2 * RET_W + h * HEAD_DIM:2 * RET_W + (h + 1) * HEAD_DIM] = dv.astype(BF16)
        for gi in range(HEADS):
            sl = slice(gi * HEAD_DIM, (gi + 1) * HEAD_DIM)
            u = p_ref[:, 4 * RET_W + gi * HEAD_DIM:4 * RET_W + (gi + 1) * HEAD_DIM]
            sv = p_ref[:, 4 * RET_W + SGU_W + gi * HEAD_DIM:4 * RET_W + SGU_W + (gi + 1) * HEAD_DIM]
            gv = _gelu(sv)
            xc = gv - jnp.mean(gv, axis=-1, keepdims=True)
            rstd = lax.rsqrt(jnp.mean(xc * xc, axis=-1, keepdims=True) + EPS)
            xh = xc * rstd
            lg = lng_ref[:, sl]
            vnb = (xh * lg + lnb_ref[:, sl]).astype(BF16)
            wcb = _causal(ws_ref[gi]).astype(BF16)
            mixed = _dot(wcb, vnb) + bsb_ref[gi]
            dsgu = dmix[:, RET_W + gi * HEAD_DIM:RET_W + (gi + 1) * HEAD_DIM]
            dmixed = dsgu * _gelu(u)
            dmb = dmixed.astype(BF16)
            dws_ref[gi] += _causal(_dot_nt(dmb, vnb))
            dbs_ref[gi] += dmixed
            dvn = _dot_tn(wcb, dmb)
            dlng_ref[0:1, sl] += jnp.sum(dvn * xh, axis=0, keepdims=True)
            dlnb_ref[0:1, sl] += jnp.sum(dvn, axis=0, keepdims=True)
            dxh = dvn * lg
            dgv = rstd * (dxh - jnp.mean(dxh, axis=-1, keepdims=True) - xh * jnp.mean(dxh * xh, axis=-1, keepdims=True))
            dp_ref[:, 4 * RET_W + gi * HEAD_DIM:4 * RET_W + (gi + 1) * HEAD_DIM] = (dsgu * mixed * _gelu_grad(u)).astype(BF16)
            dp_ref[:, 4 * RET_W + SGU_W + gi * HEAD_DIM:4 * RET_W + SGU_W + (gi + 1) * HEAD_DIM] = (
                dgv * _gelu_grad(sv)).astype(BF16)

        @pl.when(i == N_CHUNK - 1)
        def _():
            for gi in range(HEADS):
                dbs_ref[gi] = jnp.broadcast_to(jnp.sum(dbs_ref[gi], axis=-1, keepdims=True), (CHUNK, CHUNK))

    rev = lambda w: pl.BlockSpec((CHUNK, w), lambda i: (N_CHUNK - 1 - i, 0))
    hcc = (HEADS, CHUNK, CHUNK)
    acc = lambda s: pl.BlockSpec(s, lambda i: (0,) * len(s))
    return pl.pallas_call(
        body, name="bwd_mix", grid=(N_CHUNK,),
        out_shape=(jax.ShapeDtypeStruct((SEQ, PROJ_W), BF16), jax.ShapeDtypeStruct((8, RET_W), F32),
                   jax.ShapeDtypeStruct((8, SGU_W), F32), jax.ShapeDtypeStruct((8, SGU_W), F32),
                   jax.ShapeDtypeStruct(hcc, F32), jax.ShapeDtypeStruct(hcc, F32)),
        in_specs=[rev(D_MODEL), rev(PROJ_W), rev(RET_W),
                  pl.BlockSpec((1, HEADS, HEAD_DIM, HEAD_DIM), lambda i: (N_CHUNK - 1 - i, 0, 0, 0)),
                  _resident((D_MODEL, D_MODEL)), _resident((1, RET_W)), _resident((1, SGU_W)), _resident((1, SGU_W)),
                  _resident(hcc), _resident(hcc), _resident(hcc), _resident(hcc), _resident(hcc), rev(RET_W), rev(RET_W)],
        out_specs=(rev(PROJ_W), acc((8, RET_W)), acc((8, SGU_W)), acc((8, SGU_W)), acc(hcc), acc(hcc)),
        scratch_shapes=[pltpu.VMEM((HEADS, HEAD_DIM, HEAD_DIM), F32)],
        compiler_params=_cparams(("arbitrary",)),
    )(dx2, proj, o, sprev, wout_g, grn, lng, lnb, ws, bsb, mask, qdec, kdec, cos2, sin2)


def _bwd_proj(dproj, win_g, x, g1, dx2):
    def body(dp_ref, w_ref, x_ref, g_ref, dx2_ref, dx_ref, dg_ref):
        @pl.when(pl.program_id(0) == 0)
        def _():
            dg_ref[...] = jnp.zeros_like(dg_ref)

        dh = _dot_nt(dp_ref[...], w_ref[...])
        xb = x_ref[...]
        r = lax.rsqrt(jnp.mean(xb * xb, axis=-1, keepdims=True) + EPS)
        xh = xb * r
        dg_ref[0:1, :] += jnp.sum(dh * xh, axis=0, keepdims=True)
        t = dh * g_ref[...]
        dx_ref[...] = dx2_ref[...] + r * (t - xh * jnp.mean(t * xh, axis=-1, keepdims=True))

    tok = lambda w: pl.BlockSpec((TM, w), lambda i: (i, 0))
    return pl.pallas_call(
        body, name="bwd_proj", grid=(N_TB,),
        out_shape=(jax.ShapeDtypeStruct((SEQ, D_MODEL), F32), jax.ShapeDtypeStruct((8, D_MODEL), F32)),
        in_specs=[tok(PROJ_W), _resident((D_MODEL, PROJ_W)), tok(D_MODEL), _resident((1, D_MODEL)), tok(D_MODEL)],
        out_specs=(tok(D_MODEL), pl.BlockSpec((8, D_MODEL), lambda i: (0, 0))),
        compiler_params=_cparams(("arbitrary",)),
    )(dproj, win_g, x, g1, dx2)


def _wgrad(name, a, b, n_a, n_b, tn=None, tk=512):
    n = max(n_a, n_b, 1)
    m_w, n_w = a.shape[-1], b.shape[-1]
    tn = n_w if tn is None else tn
    n_k = SEQ // tk

    def body(a_ref, b_ref, o_ref, acc):
        k = pl.program_id(2)

        @pl.when(k == 0)
        def _():
            acc[...] = jnp.zeros_like(acc)

        av = a_ref[0] if n_a else a_ref[...]
        bv = b_ref[0] if n_b else b_ref[...]
        acc[...] += _dot_tn(av.astype(BF16), bv.astype(BF16))

        @pl.when(k == n_k - 1)
        def _():
            o_ref[0] = acc[...].astype(BF16)

    a_spec = pl.BlockSpec((1, tk, m_w), lambda j, t, k: (j, k, 0)) if n_a else pl.BlockSpec((tk, m_w), lambda j, t, k: (k, 0))
    b_spec = pl.BlockSpec((1, tk, tn), lambda j, t, k: (j, k, t)) if n_b else pl.BlockSpec((tk, tn), lambda j, t, k: (k, t))
    return pl.pallas_call(
        body, name=name, grid=(n, n_w // tn, n_k),
        out_shape=jax.ShapeDtypeStruct((n, m_w, n_w), BF16),
        in_specs=[a_spec, b_spec], out_specs=pl.BlockSpec((1, m_w, tn), lambda j, t, k: (j, 0, t)),
        scratch_shapes=[pltpu.VMEM((m_w, tn), F32)],
        compiler_params=_cparams(("parallel", "parallel", "arbitrary")),
    )(a, b)


def _reduce_scatter_grads(gin, gout, gup, gdn):
    n_w = 4
    shapes = [(D_MODEL, IN_SHARD), (OUT_SHARD, D_MODEL), (D_MODEL, FF_SHARD), (DOWN_SHARD, D_MODEL)]
    row_chunk = [128, 128, 128, 176]

    def body(gin_ref, gout_ref, gup_ref, gdn_ref, o_in, o_out, o_up, o_dn, l_in, l_out, l_up, l_dn,
             a_in, a_out, a_up, a_dn, b_in, b_out, b_up, b_dn, t_in, t_out, t_up, t_dn, r_in, r_out, r_up, r_dn,
             s1_send, s1_recv, s2_send, s2_recv, ld_sems):
        x, y, c = lax.axis_index("x"), lax.axis_index("y"), lax.axis_index("c")
        sibling = (x, y, 1 - c)
        chips = [(x, y), (1 - x, y), (x, 1 - y), (1 - x, 1 - y)]
        srcs = [gin_ref, gout_ref, gup_ref, gdn_ref]
        outs = [o_in, o_out, o_up, o_dn]
        land1 = [l_in, l_out, l_up, l_dn]
        bufa = [a_in, a_out, a_up, a_dn]
        bufb = [b_in, b_out, b_up, b_dn]
        stage2 = [t_in, t_out, t_up, t_dn]
        land2 = [r_in, r_out, r_up, r_dn]

        def block(w, px, py, pc):
            dev = 4 * px + 2 * py + pc
            if w == 0:
                return srcs[0].at[:, pl.ds(pl.multiple_of(dev * IN_SHARD, 128), IN_SHARD)]
            if w == 1:
                return srcs[1].at[pl.ds(pl.multiple_of(dev * OUT_SHARD, 128), OUT_SHARD), :]
            if w == 2:
                return srcs[2].at[dev]
            return srcs[3].at[pl.ds(pl.multiple_of(dev * DOWN_SHARD, 32), DOWN_SHARD), :]

        def copy1(w, k):
            return pltpu.make_async_remote_copy(
                src_ref=block(w, *chips[k], 1 - c), dst_ref=land1[w].at[k],
                send_sem=s1_send.at[w, k], recv_sem=s1_recv.at[w, k], device_id=sibling, device_id_type=MESH)

        def copy2(w, k):
            return pltpu.make_async_remote_copy(
                src_ref=stage2[w].at[k - 1], dst_ref=land2[w].at[k - 1],
                send_sem=s2_send.at[w, k - 1], recv_sem=s2_recv.at[w, k - 1], device_id=(*chips[k], c), device_id_type=MESH)

        first = [copy1(w, k) for w in range(n_w) for k in range(4)]
        for cp in first:
            cp.start()
        second = []
        for w in range(n_w):
            for k in (1, 2, 3, 0):
                la = pltpu.make_async_copy(block(w, *chips[k], c), bufa[w], ld_sems.at[0])
                la.start()
                copy1(w, k).wait_recv()
                lb = pltpu.make_async_copy(land1[w].at[k], bufb[w], ld_sems.at[1])
                lb.start()
                la.wait()
                lb.wait()
                for r0 in range(0, shapes[w][0], row_chunk[w]):
                    rs = pl.ds(r0, row_chunk[w])
                    s = bufa[w][rs, :].astype(F32) + bufb[w][rs, :].astype(F32)
                    if k == 0:
                        outs[w][rs, :] = s
                    else:
                        stage2[w][k - 1, rs, :] = s.astype(BF16)
                if k:
                    cp = copy2(w, k)
                    cp.start()
                    second.append(cp)
        for w in range(n_w):
            for k in (1, 2, 3):
                copy2(w, k).wait_recv()
            for r0 in range(0, shapes[w][0], row_chunk[w]):
                rs = pl.ds(r0, row_chunk[w])
                outs[w][rs, :] = ((outs[w][rs, :] + land2[w][0, rs, :].astype(F32)) + land2[w][1, rs, :].astype(F32)) + land2[w][2, rs, :].astype(F32)
        for cp in first + second:
            cp.wait_send()

    hbm = pl.BlockSpec(memory_space=pl.ANY)
    vm = pl.BlockSpec(memory_space=pltpu.VMEM)
    res = pl.pallas_call(
        body, name="rs_grads",
        out_shape=tuple(jax.ShapeDtypeStruct(s, F32) for s in shapes) + tuple(jax.ShapeDtypeStruct((4,) + s, BF16) for s in shapes),
        in_specs=[hbm] * 4, out_specs=(vm,) * 4 + (hbm,) * 4,
        scratch_shapes=[pltpu.VMEM(s, BF16) for s in shapes] + [pltpu.VMEM(s, BF16) for s in shapes]
        + [pltpu.VMEM((3,) + s, BF16) for s in shapes] + [pltpu.VMEM((3,) + s, BF16) for s in shapes]
        + [pltpu.SemaphoreType.DMA((n_w, 4)), pltpu.SemaphoreType.DMA((n_w, 4)), pltpu.SemaphoreType.DMA((n_w, 3)),
           pltpu.SemaphoreType.DMA((n_w, 3)), pltpu.SemaphoreType.DMA((2,))],
        compiler_params=_cparams(),
    )(gin, gout, gup, gdn)
    return res[:4]


def _all_reduce_small(p1, p2):
    def body(p1_ref, p2_ref, r1_ref, r2_ref, g1, g2, send_sems, recv_sems):
        x, y, c = lax.axis_index("x"), lax.axis_index("y"), lax.axis_index("c")
        me = 4 * x + 2 * y + c
        g1[me] = p1_ref[...]
        g2[me] = p2_ref[...]
        sends = []
        def flipped(k):
            return (1 - x if k & 4 else x, 1 - y if k & 2 else y, 1 - c if k & 1 else c)

        for k in range(1, N_DEV):
            to = flipped(k)
            for n, (src, dst) in enumerate(((p1_ref, g1), (p2_ref, g2))):
                cp = pltpu.make_async_remote_copy(src_ref=src, dst_ref=dst.at[me], send_sem=send_sems.at[n, k - 1],
                                                  recv_sem=recv_sems.at[n, k - 1], device_id=to, device_id_type=MESH)
                cp.start()
                sends.append(cp)
        for k in range(1, N_DEV):
            fx, fy, fc = flipped(k)
            frm = 4 * fx + 2 * fy + fc
            for n, (src, dst) in enumerate(((p1_ref, g1), (p2_ref, g2))):
                pltpu.make_async_remote_copy(src_ref=src, dst_ref=dst.at[frm], send_sem=send_sems.at[n, k - 1],
                                             recv_sem=recv_sems.at[n, k - 1], device_id=(x, y, c), device_id_type=MESH).wait_recv()
        for cp in sends:
            cp.wait_send()
        a1, a2 = g1[0], g2[0]
        for d in range(1, N_DEV):
            a1 = a1 + g1[d]
            a2 = a2 + g2[d]
        r1_ref[...] = a1
        r2_ref[...] = a2

    vm = pl.BlockSpec(memory_space=pltpu.VMEM)
    return pl.pallas_call(
        body, name="ar_small",
        out_shape=(jax.ShapeDtypeStruct(p1.shape, F32), jax.ShapeDtypeStruct(p2.shape, F32)),
        in_specs=[vm, vm], out_specs=(vm, vm),
        scratch_shapes=[pltpu.VMEM((N_DEV,) + p1.shape, F32), pltpu.VMEM((N_DEV,) + p2.shape, F32),
                        pltpu.SemaphoreType.DMA((2, N_DEV - 1)), pltpu.SemaphoreType.DMA((2, N_DEV - 1))],
        compiler_params=_cparams(),
    )(p1, p2)


def _adamw(name, w, g, m, v, rows):
    r, cdim = w.shape
    c1 = 1.0 - ADAM_B1 ** ADAM_STEP
    c2 = 1.0 - ADAM_B2 ** ADAM_STEP

    def body(w_ref, g_ref, m_ref, v_ref, d_ref, nm_ref, nv_ref):
        gg = g_ref[...]
        nm = ADAM_B1 * m_ref[...] + (1.0 - ADAM_B1) * gg
        nv = ADAM_B2 * v_ref[...] + (1.0 - ADAM_B2) * (gg * gg)
        nm_ref[...] = nm
        nv_ref[...] = nv
        d_ref[...] = -ADAM_LR * ((nm / c1) / (jnp.sqrt(nv / c2) + ADAM_EPS) + ADAM_WD * w_ref[...])

    spec = pl.BlockSpec((rows, cdim), lambda i: (i, 0))
    sh = jax.ShapeDtypeStruct((r, cdim), F32)
    return pl.pallas_call(
        body, name=name, grid=(r // rows,), out_shape=(sh, sh, sh),
        in_specs=[spec] * 4, out_specs=(spec,) * 3, compiler_params=_cparams(("parallel",)),
    )(w, g, m, v)


def _pack_rows(parts):
    flat = []
    for p in parts:
        f = p.reshape(-1)
        pad = (-f.shape[0]) % 1024
        flat.append(jnp.pad(f, (0, pad)) if pad else f)
    return jnp.concatenate(flat).reshape(-1, 128)


def _unpack_rows(packed, shapes):
    out, off = [], 0
    flat = packed.reshape(-1)
    for s in shapes:
        n = math.prod(s)
        out.append(flat[off:off + n].reshape(s))
        off += n + ((-n) % 1024)
    return out


def kernel(x, mix_norm_g, w_in, ret_norm_g, sgu_ln_g, sgu_ln_b, sgu_w_s, sgu_b_s, w_out, ffn_norm_g, w_up, conv_w, conv_b, w_down, final_norm_g, loss_target, m_mix_norm_g, m_w_in, m_ret_norm_g, m_sgu_ln_g, m_sgu_ln_b, m_sgu_w_s, m_sgu_b_s, m_w_out, m_ffn_norm_g, m_w_up, m_conv_w, m_conv_b, m_w_down, m_final_norm_g, v_mix_norm_g, v_w_in, v_ret_norm_g, v_sgu_ln_g, v_sgu_ln_b, v_sgu_w_s, v_sgu_b_s, v_w_out, v_ffn_norm_g, v_w_up, v_conv_w, v_conv_b, v_w_down, v_final_norm_g):
    me = 4 * lax.axis_index("x") + 2 * lax.axis_index("y") + lax.axis_index("c")
    xs = x[0]
    tgt = loss_target[0]
    cos2, sin2 = _rope_tables()
    mask, qdec, kdec = _decay_tables()
    grn = ret_norm_g.reshape(1, RET_W)
    lng = sgu_ln_g.reshape(1, SGU_W)
    lnb = sgu_ln_b.reshape(1, SGU_W)
    ws = sgu_w_s[0]
    bsb = jnp.broadcast_to(sgu_b_s[0][:, :, None], (HEADS, CHUNK, HEAD_DIM))
    gf = final_norm_g.reshape(1, D_MODEL)
    cb_g = conv_b.reshape(N_DEV, 1, FF_SHARD)

    win_g, wout_g, wup_g, wdn_g = _all_gather_weights(w_in[0], w_out[0], w_up[0], w_down[0])
    cw_slab = jnp.where(lax.broadcasted_iota(jnp.int32, (N_DEV, 8, FF_SHARD), 0) == me,
                        jnp.pad(conv_w[0], ((0, 5), (0, 0)))[None], 0.0)
    cw_full, _ = _all_reduce_small(cw_slab, jnp.zeros((8, 128), F32))
    cw_g = cw_full[:, 0:3, :]

    proj, h1 = _fwd_proj(xs, mix_norm_g, win_g, cos2, sin2)
    x2, mixcat, o, sprev = _fwd_mix(xs, proj, wout_g, grn, lng, lnb, ws, bsb, mask, qdec, kdec)
    h2, up_pre, act, x3, loss_parts = _fwd_ffn(x2, ffn_norm_g, wup_g, cw_g, cb_g, wdn_g, gf, tgt)
    loss = lax.psum(jnp.sum(loss_parts[:, 0, 0]), ("x", "y", "c"))

    dx3, dpre, dx2, dgf, dg2, dcv = _bwd_ffn(x3, tgt, gf, x2, ffn_norm_g, up_pre, wup_g, cw_g, cb_g, wdn_g)
    gdn_p = _wgrad("wgrad_down", act, dx3, N_FF_PAIR, 0).reshape(D_FF, D_MODEL)
    gup_p = _wgrad("wgrad_up", h2, dpre, 0, N_DEV)
    dproj, dgrn, dlng, dlnb, dws, dbs = _bwd_mix(dx2, proj, o, sprev, wout_g, grn, lng, lnb, ws, bsb, mask, qdec, kdec, cos2, sin2)
    gout_p = _wgrad("wgrad_out", mixcat, dx2, 0, 0)[0]
    grad_x, dg1 = _bwd_proj(dproj, win_g, xs, mix_norm_g, dx2)
    gin_p = _wgrad("wgrad_in", h1, dproj, 0, 0, tn=768)[0]

    g_in, g_out, g_up, g_dn = _reduce_scatter_grads(gin_p, gout_p, gup_p, gdn_p)
    small_shapes = [(1, D_MODEL), (1, RET_W), (1, HEADS, HEAD_DIM), (1, HEADS, HEAD_DIM), (1, HEADS, CHUNK, CHUNK),
                    (1, HEADS, CHUNK), (1, D_MODEL), (D_MODEL,)]
    small_parts = [dg1[0], dgrn[0], dlng[0], dlnb[0], dws, dbs[:, :, 0], dg2[0], dgf[0]]
    red_cv, red_small = _all_reduce_small(dcv, _pack_rows(small_parts))
    g_small = _unpack_rows(red_small, small_shapes)
    g_conv_w = lax.dynamic_index_in_dim(red_cv, me, 0, keepdims=True)[:, 0:3, :]
    g_conv_b = red_cv[:, 3, :].reshape(1, 2 * D_FF)

    big = {}
    for name, w, g, m, v, rows in (("w_in", w_in, g_in, m_w_in, v_w_in, 256), ("w_out", w_out, g_out, m_w_out, v_w_out, 128),
                                   ("w_up", w_up, g_up, m_w_up, v_w_up, 256), ("w_down", w_down, g_dn, m_w_down, v_w_down, 88)):
        d, nm, nv = _adamw("adamw_" + name, w[0], g, m[0], v[0], rows)
        big[name] = (g[None], d[None], nm[None], nv[None])
    names_small = ["mix_norm_g", "ret_norm_g", "sgu_ln_g", "sgu_ln_b", "sgu_w_s", "sgu_b_s", "ffn_norm_g", "final_norm_g",
                   "conv_w", "conv_b"]
    w_small = [mix_norm_g, ret_norm_g, sgu_ln_g, sgu_ln_b, sgu_w_s, sgu_b_s, ffn_norm_g, final_norm_g, conv_w, conv_b]
    m_small = [m_mix_norm_g, m_ret_norm_g, m_sgu_ln_g, m_sgu_ln_b, m_sgu_w_s, m_sgu_b_s, m_ffn_norm_g, m_final_norm_g, m_conv_w, m_conv_b]
    v_small = [v_mix_norm_g, v_ret_norm_g, v_sgu_ln_g, v_sgu_ln_b, v_sgu_w_s, v_sgu_b_s, v_ffn_norm_g, v_final_norm_g, v_conv_w, v_conv_b]
    g_all_small = g_small + [g_conv_w, g_conv_b]
    shapes_all = [w.shape for w in w_small]
    pk = [_pack_rows(t) for t in (w_small, g_all_small, m_small, v_small)]
    d_pk, nm_pk, nv_pk = _adamw("adamw_small", pk[0], pk[1], pk[2], pk[3], pk[0].shape[0])
    d_s, nm_s, nv_s = (_unpack_rows(t, shapes_all) for t in (d_pk, nm_pk, nv_pk))
    small = {n: (g_all_small[i].reshape(shapes_all[i]), d_s[i], nm_s[i], nv_s[i]) for i, n in enumerate(names_small)}

    order = ["mix_norm_g", "w_in", "ret_norm_g", "sgu_ln_g", "sgu_ln_b", "sgu_w_s", "sgu_b_s", "w_out", "ffn_norm_g", "w_up",
             "conv_w", "conv_b", "w_down", "final_norm_g"]
    table = {**big, **small}
    outs = [loss, grad_x[None]]
    for col in range(4):
        outs += [table[n][col] for n in order]
    return tuple(outs)
```

```python
import functools
import math

import jax
import jax.numpy as jnp
import numpy as np
from jax import lax
from jax.experimental import pallas as pl
from jax.experimental.pallas import tpu as pltpu

F32 = jnp.float32
BF16 = jnp.bfloat16
MESH = pl.DeviceIdType.MESH

N_DEV = 8
SEQ = 2048
D_MODEL = 1024
CHUNK = 128
N_CHUNK = SEQ // CHUNK
HEADS = 4
HEAD_DIM = 128
RET_W = 512
SGU_W = 512
PROJ_W = 3072
D_FF = 2816
FF_SHARD = 704
N_FF_PAIR = 4
IN_SHARD = PROJ_W // N_DEV
OUT_SHARD = D_MODEL // N_DEV
DOWN_SHARD = D_FF // N_DEV
TM = 256
N_TB = SEQ // TM
EPS = 1e-6
ROPE_BASE = 10000.0
K_SCALE = HEAD_DIM ** -0.5
INV_SQRT2 = 0.7071067811865476
INV_SQRT_2PI = 0.3989422804014327

ADAM_LR = 0.001
ADAM_B1 = 0.9
ADAM_B2 = 0.999
ADAM_EPS = 1e-08
ADAM_WD = 0.01
ADAM_STEP = 10

VMEM_LIMIT = 56 * 1024 * 1024


def _cparams(sem=None, vmem=VMEM_LIMIT):
    return pltpu.CompilerParams(dimension_semantics=sem, vmem_limit_bytes=vmem)


def _resident(shape):
    nd = len(shape)
    return pl.BlockSpec(shape, lambda *_: (0,) * nd, pipeline_mode=pl.Buffered(1))


def _dot(a, b):
    return jnp.dot(a, b, preferred_element_type=F32)


def _dot_nt(a, b):
    return lax.dot_general(a, b, (((1,), (1,)), ((), ())), preferred_element_type=F32)


def _dot_tn(a, b):
    return lax.dot_general(a, b, (((0,), (0,)), ((), ())), preferred_element_type=F32)


def _sigmoid(x):
    return 1.0 / (1.0 + jnp.exp(-x))


def _gelu(x):
    return 0.5 * x * (1.0 + lax.erf(x * INV_SQRT2))


def _gelu_grad(x):
    return 0.5 * (1.0 + lax.erf(x * INV_SQRT2)) + x * (jnp.exp(-0.5 * x * x) * INV_SQRT_2PI)


def _rot(xh, cos2, sin2):
    return xh * cos2 + pltpu.roll(xh, HEAD_DIM // 2, 1) * sin2


def _rot_t(dh, cos2, sin2):
    return dh * cos2 + pltpu.roll(dh * sin2, HEAD_DIM // 2, 1)


def _rope_tables():
    half = HEAD_DIM // 2
    inv_freq = jnp.power(ROPE_BASE, -jnp.arange(half, dtype=F32) / half)
    ang = jnp.arange(SEQ, dtype=F32)[:, None] * inv_freq[None, :]
    cos, sin = jnp.cos(ang), jnp.sin(ang)
    cos2 = jnp.tile(jnp.concatenate([cos, cos], axis=-1), (1, HEADS))
    sin2 = jnp.tile(jnp.concatenate([-sin, sin], axis=-1), (1, HEADS))
    return cos2, sin2


def _decay_tables():
    log_gamma = jnp.log(1.0 - jnp.power(2.0, -5.0 - jnp.arange(HEADS, dtype=F32)))
    pos = jnp.arange(CHUNK, dtype=F32)
    diff = pos[:, None] - pos[None, :]
    mask = jnp.where(diff >= 0.0, jnp.exp(log_gamma[:, None, None] * jnp.maximum(diff, 0.0)[None]), 0.0)
    k_decay = jnp.exp(log_gamma[:, None] * (CHUNK - 1.0 - pos)[None])
    q_decay = jnp.exp(log_gamma[:, None] * (pos + 1.0)[None])
    kd = jnp.broadcast_to(k_decay[:, :, None], (HEADS, CHUNK, HEAD_DIM))
    qd = jnp.broadcast_to(q_decay[:, :, None], (HEADS, CHUNK, HEAD_DIM))
    return mask.astype(F32), qd.astype(F32), kd.astype(F32)


def _chunk_decay():
    lg = np.log(np.float32(1.0) - np.power(np.float32(2.0), -5.0 - np.arange(HEADS, dtype=np.float32))).astype(np.float32)
    return [float(np.exp(lg[h] * np.float32(CHUNK))) for h in range(HEADS)]


def _all_gather_weights(w_in, w_out, w_up, w_down, conv_w):
    n_w = 5

    def body(in_ref, out_ref, up_ref, dn_ref, cw_ref, gin, gout, gup, gdn, gcw, s_in, s_out, s_up, s_dn, s_cw,
             send_sems, recv_sems, local_sems):
        x, y, c = lax.axis_index("x"), lax.axis_index("y"), lax.axis_index("c")
        sibling = (x, y, 1 - c)
        chips = [(1 - x, y), (x, 1 - y), (1 - x, 1 - y)]
        stages = [s_in, s_out, s_up, s_dn, s_cw]
        s_in[...] = in_ref[...].astype(BF16)
        s_out[...] = out_ref[...].astype(BF16)
        s_up[...] = up_ref[...].astype(BF16)
        s_dn[...] = dn_ref[...].astype(BF16)
        s_cw[...] = jnp.zeros_like(s_cw)
        s_cw[0:3, :] = cw_ref[...]

        def slot(w, px, py, pc):
            dev = 4 * px + 2 * py + pc
            if w == 0:
                return gin.at[:, pl.ds(pl.multiple_of(dev * IN_SHARD, 128), IN_SHARD)]
            if w == 1:
                return gout.at[pl.ds(pl.multiple_of(dev * OUT_SHARD, 128), OUT_SHARD), :]
            if w == 2:
                return gup.at[dev]
            if w == 4:
                return gcw.at[dev]
            return gdn.at[pl.ds(pl.multiple_of(dev * DOWN_SHARD, 32), DOWN_SHARD), :]

        def copy(w, k, block, to, src=None):
            return pltpu.make_async_remote_copy(
                src_ref=slot(w, *block) if src is None else src, dst_ref=slot(w, *block),
                send_sem=send_sems.at[w, k], recv_sem=recv_sems.at[w, k], device_id=to, device_id_type=MESH)

        me = (x, y, c)
        mine = [pltpu.make_async_copy(stages[w], slot(w, *me), local_sems.at[w]) for w in range(n_w)]
        for cp in mine:
            cp.start()
        first = []
        for w in range(n_w):
            first.append(copy(w, 0, me, sibling, src=stages[w]))
            first += [copy(w, 1 + j, me, (*chip, c), src=stages[w]) for j, chip in enumerate(chips)]
        for cp in first:
            cp.start()
        passed = []
        for w in range(n_w):
            for j, chip in enumerate(chips):
                copy(w, 1 + j, (*chip, c), me).wait_recv()
                fwd = copy(w, 4 + j, (*chip, c), sibling)
                fwd.start()
                passed.append(fwd)
        for w in range(n_w):
            copy(w, 0, sibling, me).wait_recv()
            for j, chip in enumerate(chips):
                copy(w, 4 + j, (*chip, 1 - c), me).wait_recv()
        for cp in first + passed:
            cp.wait_send()
        for cp in mine:
            cp.wait()

    vm = pl.BlockSpec(memory_space=pltpu.VMEM)
    hbm = pl.BlockSpec(memory_space=pl.ANY)
    return pl.pallas_call(
        body, name="ag_weights",
        out_shape=(jax.ShapeDtypeStruct((D_MODEL, PROJ_W), BF16), jax.ShapeDtypeStruct((D_MODEL, D_MODEL), BF16),
                   jax.ShapeDtypeStruct((N_DEV, D_MODEL, FF_SHARD), BF16), jax.ShapeDtypeStruct((D_FF, D_MODEL), BF16),
                   jax.ShapeDtypeStruct((N_DEV, 8, FF_SHARD), F32)),
        in_specs=[vm, vm, vm, vm, vm], out_specs=(hbm, hbm, hbm, hbm, hbm),
        scratch_shapes=[pltpu.VMEM((D_MODEL, IN_SHARD), BF16), pltpu.VMEM((OUT_SHARD, D_MODEL), BF16),
                        pltpu.VMEM((D_MODEL, FF_SHARD), BF16), pltpu.VMEM((DOWN_SHARD, D_MODEL), BF16),
                        pltpu.VMEM((8, FF_SHARD), F32),
                        pltpu.SemaphoreType.DMA((n_w, 7)), pltpu.SemaphoreType.DMA((n_w, 7)), pltpu.SemaphoreType.DMA((n_w,))],
        compiler_params=_cparams(),
    )(w_in, w_out, w_up, w_down, conv_w)


def _fwd_proj(x, g1, win_g, cos2, sin2):
    def body(x_ref, g_ref, w_ref, cos_ref, sin_ref, proj_ref, h1_ref):
        xb = x_ref[...]
        r = lax.rsqrt(jnp.mean(xb * xb, axis=-1, keepdims=True) + EPS)
        h = ((xb * r) * g_ref[...]).astype(BF16)
        h1_ref[...] = h
        p = _dot(h, w_ref[...])
        for hd in range(HEADS):
            sl = slice(hd * HEAD_DIM, (hd + 1) * HEAD_DIM)
            c2, s2 = cos_ref[:, sl], sin_ref[:, sl]
            proj_ref[:, sl] = _rot(p[:, sl], c2, s2)
            ks = slice(RET_W + hd * HEAD_DIM, RET_W + (hd + 1) * HEAD_DIM)
            proj_ref[:, ks] = _rot(p[:, ks], c2, s2) * K_SCALE
        proj_ref[:, 2 * RET_W:] = p[:, 2 * RET_W:]

    tok = lambda w: pl.BlockSpec((TM, w), lambda i: (i, 0))
    return pl.pallas_call(
        body, name="fwd_proj", grid=(N_TB,),
        out_shape=(jax.ShapeDtypeStruct((SEQ, PROJ_W), F32), jax.ShapeDtypeStruct((SEQ, D_MODEL), BF16)),
        in_specs=[tok(D_MODEL), _resident((1, D_MODEL)), _resident((D_MODEL, PROJ_W)), tok(RET_W), tok(RET_W)],
        out_specs=(tok(PROJ_W), tok(D_MODEL)),
        compiler_params=_cparams(("parallel",)),
    )(x, g1, win_g, cos2, sin2)


def _causal(w):
    r = lax.broadcasted_iota(jnp.int32, (CHUNK, CHUNK), 0)
    c = lax.broadcasted_iota(jnp.int32, (CHUNK, CHUNK), 1)
    return jnp.where(r >= c, w, 0.0)


def _fwd_mix(x, proj, wout_g, grn, lng, lnb, ws, bsb, mask, qdec, kdec):
    cdec = _chunk_decay()

    def body(x_ref, p_ref, w_ref, grn_ref, lng_ref, lnb_ref, ws_ref, bsb_ref, m_ref, qd_ref, kd_ref,
             x2_ref, cat_ref, o_ref, sp_ref, state):
        @pl.when(pl.program_id(0) == 0)
        def _():
            state[...] = jnp.zeros_like(state)

        for h in range(HEADS):
            sl = slice(h * HEAD_DIM, (h + 1) * HEAD_DIM)
            q = p_ref[:, sl]
            k = p_ref[:, RET_W + h * HEAD_DIM:RET_W + (h + 1) * HEAD_DIM]
            v = p_ref[:, 2 * RET_W + h * HEAD_DIM:2 * RET_W + (h + 1) * HEAD_DIM]
            g = p_ref[:, 3 * RET_W + h * HEAD_DIM:3 * RET_W + (h + 1) * HEAD_DIM]
            qb, kb, vb = q.astype(BF16), k.astype(BF16), v.astype(BF16)
            a = _dot_nt(qb, kb) * m_ref[h]
            spb = state[h].astype(BF16)
            sp_ref[0, h] = spb
            o = _dot(a.astype(BF16), vb) + _dot((q * qd_ref[h]).astype(BF16), spb)
            state[h] = state[h] * cdec[h] + _dot_tn((k * kd_ref[h]).astype(BF16), vb)
            o_ref[:, sl] = o
            rinv = lax.rsqrt(jnp.mean(o * o, axis=-1, keepdims=True) + EPS)
            rn = (o * rinv) * grn_ref[:, sl]
            cat_ref[:, sl] = ((g * _sigmoid(g)) * rn).astype(BF16)
        for gi in range(HEADS):
            sl = slice(gi * HEAD_DIM, (gi + 1) * HEAD_DIM)
            u = p_ref[:, 4 * RET_W + gi * HEAD_DIM:4 * RET_W + (gi + 1) * HEAD_DIM]
            sv = p_ref[:, 4 * RET_W + SGU_W + gi * HEAD_DIM:4 * RET_W + SGU_W + (gi + 1) * HEAD_DIM]
            gv = _gelu(sv)
            xc = gv - jnp.mean(gv, axis=-1, keepdims=True)
            vn = (xc * lax.rsqrt(jnp.mean(xc * xc, axis=-1, keepdims=True) + EPS)) * lng_ref[:, sl] + lnb_ref[:, sl]
            mixed = _dot(_causal(ws_ref[gi]).astype(BF16), vn.astype(BF16)) + bsb_ref[gi]
            cat_ref[:, RET_W + gi * HEAD_DIM:RET_W + (gi + 1) * HEAD_DIM] = (_gelu(u) * mixed).astype(BF16)
        x2_ref[...] = x_ref[...] + _dot(cat_ref[...], w_ref[...])

    ch = lambda w: pl.BlockSpec((CHUNK, w), lambda i: (i, 0))
    hcc = (HEADS, CHUNK, CHUNK)
    return pl.pallas_call(
        body, name="fwd_mix", grid=(N_CHUNK,),
        out_shape=(jax.ShapeDtypeStruct((SEQ, D_MODEL), F32), jax.ShapeDtypeStruct((SEQ, D_MODEL), BF16),
                   jax.ShapeDtypeStruct((SEQ, RET_W), F32), jax.ShapeDtypeStruct((N_CHUNK, HEADS, HEAD_DIM, HEAD_DIM), BF16)),
        in_specs=[ch(D_MODEL), ch(PROJ_W), _resident((D_MODEL, D_MODEL)), _resident((1, RET_W)), _resident((1, SGU_W)),
                  _resident((1, SGU_W)), _resident(hcc), _resident(hcc), _resident(hcc), _resident(hcc), _resident(hcc)],
        out_specs=(ch(D_MODEL), ch(D_MODEL), ch(RET_W), pl.BlockSpec((1, HEADS, HEAD_DIM, HEAD_DIM), lambda i: (i, 0, 0, 0))),
        scratch_shapes=[pltpu.VMEM((HEADS, HEAD_DIM, HEAD_DIM), F32)],
        compiler_params=_cparams(("arbitrary",)),
    )(x, proj, wout_g, grn, lng, lnb, ws, bsb, mask, qdec, kdec)


def _conv_taps(p, prev8):
    row = lax.broadcasted_iota(jnp.int32, p.shape, 0)
    p1 = jnp.where(row == 0, prev8[7:8, :], pltpu.roll(p, 1, 0))
    p2 = jnp.where(row == 0, prev8[6:7, :], jnp.where(row == 1, prev8[7:8, :], pltpu.roll(p, 2, 0)))
    return p1, p2


def _fwd_ffn(x2, g2, wup_g, cw_g, cb_g, wdn_g, gf, tgt):
    def body(x_ref, g_ref, wu_ref, cw_ref, cb_ref, wd_ref, gf_ref, t_ref, h2_ref, up_ref, u_ref, act_ref, x3_ref, loss_ref, carry):
        @pl.when(pl.program_id(0) == 0)
        def _():
            carry[...] = jnp.zeros_like(carry)

        xb = x_ref[...]
        r = lax.rsqrt(jnp.mean(xb * xb, axis=-1, keepdims=True) + EPS)
        h = ((xb * r) * g_ref[...]).astype(BF16)
        h2_ref[...] = h
        acc = xb
        for j in range(N_FF_PAIR):
            u = []
            for s in (j, j + N_FF_PAIR):
                p = _dot(h, wu_ref[s])
                up_ref[s] = p.astype(BF16)
                p1, p2 = _conv_taps(p, carry[s])
                carry[s] = p[TM - 8:, :]
                cw = cw_ref[s]
                us = p2 * cw[0:1, :] + p1 * cw[1:2, :] + p * cw[2:3, :] + cb_ref[s]
                u_ref[s] = us.astype(BF16)
                u.append(us)
            a = ((u[0] * _sigmoid(u[0])) * u[1]).astype(BF16)
            act_ref[j] = a
            acc = acc + _dot(a, wd_ref[pl.ds(j * FF_SHARD, FF_SHARD), :])
        x3_ref[...] = acc
        r3 = lax.rsqrt(jnp.mean(acc * acc, axis=-1, keepdims=True) + EPS)
        diff = (acc * r3) * gf_ref[...] - t_ref[...]
        loss_ref[...] = jnp.full(loss_ref.shape, 0.5 * jnp.sum(jnp.mean(diff * diff, axis=-1)), F32)

    tok = lambda w: pl.BlockSpec((TM, w), lambda i: (i, 0))
    return pl.pallas_call(
        body, name="fwd_ffn", grid=(N_TB,),
        out_shape=(jax.ShapeDtypeStruct((SEQ, D_MODEL), BF16), jax.ShapeDtypeStruct((N_DEV, SEQ, FF_SHARD), BF16),
                   jax.ShapeDtypeStruct((N_DEV, SEQ, FF_SHARD), BF16),
                   jax.ShapeDtypeStruct((N_FF_PAIR, SEQ, FF_SHARD), BF16), jax.ShapeDtypeStruct((SEQ, D_MODEL), F32),
                   jax.ShapeDtypeStruct((N_TB, 8, 128), F32)),
        in_specs=[tok(D_MODEL), _resident((1, D_MODEL)), _resident((N_DEV, D_MODEL, FF_SHARD)), _resident((N_DEV, 8, FF_SHARD)),
                  _resident((N_DEV, 1, FF_SHARD)), _resident((D_FF, D_MODEL)), _resident((1, D_MODEL)), tok(D_MODEL)],
        out_specs=(tok(D_MODEL), pl.BlockSpec((N_DEV, TM, FF_SHARD), lambda i: (0, i, 0)),
                   pl.BlockSpec((N_DEV, TM, FF_SHARD), lambda i: (0, i, 0)),
                   pl.BlockSpec((N_FF_PAIR, TM, FF_SHARD), lambda i: (0, i, 0)), tok(D_MODEL),
                   pl.BlockSpec((1, 8, 128), lambda i: (i, 0, 0))),
        scratch_shapes=[pltpu.VMEM((N_DEV, 8, FF_SHARD), F32)],
        compiler_params=_cparams(("arbitrary",)),
    )(x2, g2, wup_g, cw_g, cb_g, wdn_g, gf, tgt)


def _bwd_ffn(x3, tgt, gf, x2, g2, up_pre, u_conv, wup_g, cw_g, wdn_g):
    def body(x3_ref, t_ref, gf_ref, x2_ref, g2_ref, up_ref, u_ref, wu_ref, cw_ref, wd_ref,
             dx3_ref, dpre_ref, dx2_ref, dgf_ref, dg2_ref, dcv_ref, nxt):
        i = pl.program_id(0)

        @pl.when(i == 0)
        def _():
            nxt[...] = jnp.zeros_like(nxt)
            dgf_ref[...] = jnp.zeros_like(dgf_ref)
            dg2_ref[...] = jnp.zeros_like(dg2_ref)
            dcv_ref[...] = jnp.zeros_like(dcv_ref)

        x3 = x3_ref[...]
        r3 = lax.rsqrt(jnp.mean(x3 * x3, axis=-1, keepdims=True) + EPS)
        xh3 = x3 * r3
        dy = (xh3 * gf_ref[...] - t_ref[...]) * (1.0 / D_MODEL)
        dgf_ref[0:1, :] += jnp.sum(dy * xh3, axis=0, keepdims=True)
        t3 = dy * gf_ref[...]
        dx3 = r3 * (t3 - xh3 * jnp.mean(t3 * xh3, axis=-1, keepdims=True))
        dx3b = dx3.astype(BF16)
        dx3_ref[...] = dx3b
        dh2 = jnp.zeros((TM, D_MODEL), F32)
        row = lax.broadcasted_iota(jnp.int32, (TM, FF_SHARD), 0)
        for j in range(N_FF_PAIR):
            dact = _dot_nt(dx3b, wd_ref[pl.ds(j * FF_SHARD, FF_SHARD), :])
            ua = u_ref[j].astype(F32)
            ub = u_ref[j + N_FF_PAIR].astype(F32)
            sg = _sigmoid(ua)
            du = [dact * ub * (sg * (1.0 + ua * (1.0 - sg))), dact * (ua * sg)]
            for n, s in enumerate((j, j + N_FF_PAIR)):
                d = du[n]
                cw = cw_ref[s]
                nx = nxt[s]
                n1 = jnp.where(row == TM - 1, nx[0:1, :], pltpu.roll(d, TM - 1, 0))
                n2 = jnp.where(row == TM - 2, nx[0:1, :], jnp.where(row == TM - 1, nx[1:2, :], pltpu.roll(d, TM - 2, 0)))
                nxt[s] = d[0:8, :]
                dp = (d * cw[2:3, :] + n1 * cw[1:2, :] + n2 * cw[0:1, :]).astype(BF16)
                dpre_ref[s] = dp
                p = up_ref[s].astype(F32)
                dcv_ref[s, 0:1, :] += jnp.sum(n2 * p, axis=0, keepdims=True)
                dcv_ref[s, 1:2, :] += jnp.sum(n1 * p, axis=0, keepdims=True)
                dcv_ref[s, 2:3, :] += jnp.sum(d * p, axis=0, keepdims=True)
                dcv_ref[s, 3:4, :] += jnp.sum(d, axis=0, keepdims=True)
                dh2 = dh2 + _dot_nt(dp, wu_ref[s])
        x2 = x2_ref[...]
        r2 = lax.rsqrt(jnp.mean(x2 * x2, axis=-1, keepdims=True) + EPS)
        xh2 = x2 * r2
        dg2_ref[0:1, :] += jnp.sum(dh2 * xh2, axis=0, keepdims=True)
        t2 = dh2 * g2_ref[...]
        dx2_ref[...] = dx3 + r2 * (t2 - xh2 * jnp.mean(t2 * xh2, axis=-1, keepdims=True))

    rev = lambda w: pl.BlockSpec((TM, w), lambda i: (N_TB - 1 - i, 0))
    rev3 = lambda: pl.BlockSpec((N_DEV, TM, FF_SHARD), lambda i: (0, N_TB - 1 - i, 0))
    acc = lambda s: pl.BlockSpec(s, lambda i: (0,) * len(s))
    return pl.pallas_call(
        body, name="bwd_ffn", grid=(N_TB,),
        out_shape=(jax.ShapeDtypeStruct((SEQ, D_MODEL), BF16), jax.ShapeDtypeStruct((N_DEV, SEQ, FF_SHARD), BF16),
                   jax.ShapeDtypeStruct((SEQ, D_MODEL), F32), jax.ShapeDtypeStruct((8, D_MODEL), F32),
                   jax.ShapeDtypeStruct((8, D_MODEL), F32), jax.ShapeDtypeStruct((N_DEV, 8, FF_SHARD), F32)),
        in_specs=[rev(D_MODEL), rev(D_MODEL), _resident((1, D_MODEL)), rev(D_MODEL), _resident((1, D_MODEL)), rev3(), rev3(),
                  _resident((N_DEV, D_MODEL, FF_SHARD)), _resident((N_DEV, 8, FF_SHARD)), _resident((D_FF, D_MODEL))],
        out_specs=(rev(D_MODEL), rev3(), rev(D_MODEL), acc((8, D_MODEL)), acc((8, D_MODEL)), acc((N_DEV, 8, FF_SHARD))),
        scratch_shapes=[pltpu.VMEM((N_DEV, 8, FF_SHARD), F32)],
        compiler_params=_cparams(("arbitrary",)),
    )(x3, tgt, gf, x2, g2, up_pre, u_conv, wup_g, cw_g, wdn_g)


def _bwd_mix(dx2, proj, o, sprev, wout_g, grn, lng, lnb, ws, bsb, mask, qdec, kdec, cos2, sin2):
    cdec = _chunk_decay()

    def body(dx2_ref, p_ref, o_ref, sp_ref, w_ref, grn_ref, lng_ref, lnb_ref, ws_ref, bsb_ref, m_ref, qd_ref, kd_ref,
             cos_ref, sin_ref, dp_ref, dgrn_ref, dlng_ref, dlnb_ref, dws_ref, dbs_ref, dstate, dbs_acc):
        i = pl.program_id(0)

        @pl.when(i == 0)
        def _():
            dstate[...] = jnp.zeros_like(dstate)
            dgrn_ref[...] = jnp.zeros_like(dgrn_ref)
            dlng_ref[...] = jnp.zeros_like(dlng_ref)
            dlnb_ref[...] = jnp.zeros_like(dlnb_ref)
            dws_ref[...] = jnp.zeros_like(dws_ref)
            dbs_ref[...] = jnp.zeros_like(dbs_ref)
            dbs_acc[...] = jnp.zeros_like(dbs_acc)

        dmix = _dot_nt(dx2_ref[...].astype(BF16), w_ref[...])
        for h in range(HEADS):
            sl = slice(h * HEAD_DIM, (h + 1) * HEAD_DIM)
            q = p_ref[:, sl]
            k = p_ref[:, RET_W + h * HEAD_DIM:RET_W + (h + 1) * HEAD_DIM]
            v = p_ref[:, 2 * RET_W + h * HEAD_DIM:2 * RET_W + (h + 1) * HEAD_DIM]
            g = p_ref[:, 3 * RET_W + h * HEAD_DIM:3 * RET_W + (h + 1) * HEAD_DIM]
            o = o_ref[:, sl]
            rinv = lax.rsqrt(jnp.mean(o * o, axis=-1, keepdims=True) + EPS)
            oh = o * rinv
            gr = grn_ref[:, sl]
            sg = _sigmoid(g)
            dret = dmix[:, sl]
            dp_ref[:, 3 * RET_W + h * HEAD_DIM:3 * RET_W + (h + 1) * HEAD_DIM] = (
                dret * (oh * gr) * (sg * (1.0 + g * (1.0 - sg)))).astype(BF16)
            drn = dret * (g * sg)
            dgrn_ref[0:1, sl] += jnp.sum(drn * oh, axis=0, keepdims=True)
            t = drn * gr
            do = rinv * (t - oh * jnp.mean(t * oh, axis=-1, keepdims=True))
            qb, kb, vb, dob = q.astype(BF16), k.astype(BF16), v.astype(BF16), do.astype(BF16)
            m = m_ref[h]
            ab = (_dot_nt(qb, kb) * m).astype(BF16)
            dab = (_dot_nt(dob, vb) * m).astype(BF16)
            spb = sp_ref[0, h]
            dsn = dstate[h]
            dsnb = dsn.astype(BF16)
            qdb = (q * qd_ref[h]).astype(BF16)
            kdb = (k * kd_ref[h]).astype(BF16)
            dq = _dot(dab, kb) + _dot_nt(dob, spb) * qd_ref[h]
            dk = _dot_tn(dab, qb) + _dot_nt(vb, dsnb) * kd_ref[h]
            dv = _dot_tn(ab, dob) + _dot(kdb, dsnb)
            dstate[h] = dsn * cdec[h] + _dot_tn(qdb, dob)
            c2, s2 = cos_ref[:, sl], sin_ref[:, sl]
            dp_ref[:, sl] = _rot_t(dq, c2, s2).astype(BF16)
            dp_ref[:, RET_W + h * HEAD_DIM:RET_W + (h + 1) * HEAD_DIM] = _rot_t(dk * K_SCALE, c2, s2).astype(BF16)
            dp_ref[:, 2 * RET_W + h * HEAD_DIM:2 * RET_W + (h + 1) * HEAD_DIM] = dv.astype(BF16)
        for gi in range(HEADS):
            sl = slice(gi * HEAD_DIM, (gi + 1) * HEAD_DIM)
            u = p_ref[:, 4 * RET_W + gi * HEAD_DIM:4 * RET_W + (gi + 1) * HEAD_DIM]
            sv = p_ref[:, 4 * RET_W + SGU_W + gi * HEAD_DIM:4 * RET_W + SGU_W + (gi + 1) * HEAD_DIM]
            gv = _gelu(sv)
            xc = gv - jnp.mean(gv, axis=-1, keepdims=True)
            rstd = lax.rsqrt(jnp.mean(xc * xc, axis=-1, keepdims=True) + EPS)
            xh = xc * rstd
            lg = lng_ref[:, sl]
            vnb = (xh * lg + lnb_ref[:, sl]).astype(BF16)
            wcb = _causal(ws_ref[gi]).astype(BF16)
            mixed = _dot(wcb, vnb) + bsb_ref[gi]
            dsgu = dmix[:, RET_W + gi * HEAD_DIM:RET_W + (gi + 1) * HEAD_DIM]
            dmixed = dsgu * _gelu(u)
            dmb = dmixed.astype(BF16)
            dws_ref[gi] += _causal(_dot_nt(dmb, vnb))
            dbs_acc[gi] += dmixed
            dvn = _dot_tn(wcb, dmb)
            dlng_ref[gi:gi + 1, :] += jnp.sum(dvn * xh, axis=0, keepdims=True)
            dlnb_ref[gi:gi + 1, :] += jnp.sum(dvn, axis=0, keepdims=True)
            dxh = dvn * lg
            dgv = rstd * (dxh - jnp.mean(dxh, axis=-1, keepdims=True) - xh * jnp.mean(dxh * xh, axis=-1, keepdims=True))
            dp_ref[:, 4 * RET_W + gi * HEAD_DIM:4 * RET_W + (gi + 1) * HEAD_DIM] = (dsgu * mixed * _gelu_grad(u)).astype(BF16)
            dp_ref[:, 4 * RET_W + SGU_W + gi * HEAD_DIM:4 * RET_W + SGU_W + (gi + 1) * HEAD_DIM] = (
                dgv * _gelu_grad(sv)).astype(BF16)

        @pl.when(i == N_CHUNK - 1)
        def _():
            for gi in range(HEADS):
                col = jnp.broadcast_to(jnp.sum(dbs_acc[gi], axis=-1, keepdims=True), (CHUNK, CHUNK))
                dbs_ref[gi:gi + 1, :] = jnp.transpose(col)[0:1, :]

    rev = lambda w: pl.BlockSpec((CHUNK, w), lambda i: (N_CHUNK - 1 - i, 0))
    hcc = (HEADS, CHUNK, CHUNK)
    acc = lambda s: pl.BlockSpec(s, lambda i: (0,) * len(s))
    return pl.pallas_call(
        body, name="bwd_mix", grid=(N_CHUNK,),
        out_shape=(jax.ShapeDtypeStruct((SEQ, PROJ_W), BF16), jax.ShapeDtypeStruct((8, RET_W), F32),
                   jax.ShapeDtypeStruct((8, HEAD_DIM), F32), jax.ShapeDtypeStruct((8, HEAD_DIM), F32),
                   jax.ShapeDtypeStruct(hcc, F32), jax.ShapeDtypeStruct((8, CHUNK), F32)),
        in_specs=[rev(D_MODEL), rev(PROJ_W), rev(RET_W),
                  pl.BlockSpec((1, HEADS, HEAD_DIM, HEAD_DIM), lambda i: (N_CHUNK - 1 - i, 0, 0, 0)),
                  _resident((D_MODEL, D_MODEL)), _resident((1, RET_W)), _resident((1, SGU_W)), _resident((1, SGU_W)),
                  _resident(hcc), _resident(hcc), _resident(hcc), _resident(hcc), _resident(hcc), rev(RET_W), rev(RET_W)],
        out_specs=(rev(PROJ_W), acc((8, RET_W)), acc((8, HEAD_DIM)), acc((8, HEAD_DIM)), acc(hcc), acc((8, CHUNK))),
        scratch_shapes=[pltpu.VMEM((HEADS, HEAD_DIM, HEAD_DIM), F32), pltpu.VMEM((HEADS, CHUNK, CHUNK), F32)],
        compiler_params=_cparams(("arbitrary",)),
    )(dx2, proj, o, sprev, wout_g, grn, lng, lnb, ws, bsb, mask, qdec, kdec, cos2, sin2)


def _bwd_proj(dproj, win_g, x, g1, dx2):
    def body(dp_ref, w_ref, x_ref, g_ref, dx2_ref, dx_ref, dg_ref):
        @pl.when(pl.program_id(0) == 0)
        def _():
            dg_ref[...] = jnp.zeros_like(dg_ref)

        dh = _dot_nt(dp_ref[...], w_ref[...])
        xb = x_ref[...]
        r = lax.rsqrt(jnp.mean(xb * xb, axis=-1, keepdims=True) + EPS)
        xh = xb * r
        dg_ref[0:1, :] += jnp.sum(dh * xh, axis=0, keepdims=True)
        t = dh * g_ref[...]
        dx_ref[...] = dx2_ref[...] + r * (t - xh * jnp.mean(t * xh, axis=-1, keepdims=True))

    tok = lambda w: pl.BlockSpec((TM, w), lambda i: (i, 0))
    return pl.pallas_call(
        body, name="bwd_proj", grid=(N_TB,),
        out_shape=(jax.ShapeDtypeStruct((SEQ, D_MODEL), F32), jax.ShapeDtypeStruct((8, D_MODEL), F32)),
        in_specs=[tok(PROJ_W), _resident((D_MODEL, PROJ_W)), tok(D_MODEL), _resident((1, D_MODEL)), tok(D_MODEL)],
        out_specs=(tok(D_MODEL), pl.BlockSpec((8, D_MODEL), lambda i: (0, 0))),
        compiler_params=_cparams(("arbitrary",)),
    )(dproj, win_g, x, g1, dx2)


def _wgrad(name, a, b, n_a, n_b, tn=None):
    n = max(n_a, n_b, 1)
    m_w, n_w = a.shape[-1], b.shape[-1]
    tn = n_w if tn is None else tn

    def body(a_ref, b_ref, o_ref):
        av = a_ref[0] if n_a else a_ref[...]
        bv = b_ref[0] if n_b else b_ref[...]
        o_ref[0] = _dot_tn(av.astype(BF16), bv.astype(BF16)).astype(BF16)

    a_spec = pl.BlockSpec((1, SEQ, m_w), lambda j, t: (j, 0, 0)) if n_a else pl.BlockSpec((SEQ, m_w), lambda j, t: (0, 0))
    b_spec = pl.BlockSpec((1, SEQ, tn), lambda j, t: (j, 0, t)) if n_b else pl.BlockSpec((SEQ, tn), lambda j, t: (0, t))
    return pl.pallas_call(
        body, name=name, grid=(n, n_w // tn),
        out_shape=jax.ShapeDtypeStruct((n, m_w, n_w), BF16),
        in_specs=[a_spec, b_spec], out_specs=pl.BlockSpec((1, m_w, tn), lambda j, t: (j, 0, t)),
        compiler_params=_cparams(("parallel", "parallel")),
    )(a, b)


def _reduce_scatter_grads(gin, gout, gup, gdn):
    n_w = 4
    shapes = [(D_MODEL, IN_SHARD), (OUT_SHARD, D_MODEL), (D_MODEL, FF_SHARD), (DOWN_SHARD, D_MODEL)]
    row_chunk = [128, 128, 128, 176]

    def body(gin_ref, gout_ref, gup_ref, gdn_ref, o_in, o_out, o_up, o_dn, l_in, l_out, l_up, l_dn,
             a_in, a_out, a_up, a_dn, b_in, b_out, b_up, b_dn, t_in, t_out, t_up, t_dn, r_in, r_out, r_up, r_dn,
             s1_send, s1_recv, s2_send, s2_recv, ld_sems):
        x, y, c = lax.axis_index("x"), lax.axis_index("y"), lax.axis_index("c")
        sibling = (x, y, 1 - c)
        chips = [(x, y), (1 - x, y), (x, 1 - y), (1 - x, 1 - y)]
        srcs = [gin_ref, gout_ref, gup_ref, gdn_ref]
        outs = [o_in, o_out, o_up, o_dn]
        land1 = [l_in, l_out, l_up, l_dn]
        bufa = [a_in, a_out, a_up, a_dn]
        bufb = [b_in, b_out, b_up, b_dn]
        stage2 = [t_in, t_out, t_up, t_dn]
        land2 = [r_in, r_out, r_up, r_dn]

        def block(w, px, py, pc):
            dev = 4 * px + 2 * py + pc
            if w == 0:
                return srcs[0].at[:, pl.ds(pl.multiple_of(dev * IN_SHARD, 128), IN_SHARD)]
            if w == 1:
                return srcs[1].at[pl.ds(pl.multiple_of(dev * OUT_SHARD, 128), OUT_SHARD), :]
            if w == 2:
                return srcs[2].at[dev]
            return srcs[3].at[pl.ds(pl.multiple_of(dev * DOWN_SHARD, 32), DOWN_SHARD), :]

        def copy1(w, k):
            return pltpu.make_async_remote_copy(
                src_ref=block(w, *chips[k], 1 - c), dst_ref=land1[w].at[k],
                send_sem=s1_send.at[w, k], recv_sem=s1_recv.at[w, k], device_id=sibling, device_id_type=MESH)

        def copy2(w, k):
            return pltpu.make_async_remote_copy(
                src_ref=stage2[w].at[k - 1], dst_ref=land2[w].at[k - 1],
                send_sem=s2_send.at[w, k - 1], recv_sem=s2_recv.at[w, k - 1], device_id=(*chips[k], c), device_id_type=MESH)

        first = [copy1(w, k) for w in range(n_w) for k in range(4)]
        for cp in first:
            cp.start()
        second = []
        for w in range(n_w):
            for k in (1, 2, 3, 0):
                la = pltpu.make_async_copy(block(w, *chips[k], c), bufa[w], ld_sems.at[0])
                la.start()
                copy1(w, k).wait_recv()
                lb = pltpu.make_async_copy(land1[w].at[k], bufb[w], ld_sems.at[1])
                lb.start()
                la.wait()
                lb.wait()
                for r0 in range(0, shapes[w][0], row_chunk[w]):
                    rs = pl.ds(r0, row_chunk[w])
                    s = bufa[w][rs, :].astype(F32) + bufb[w][rs, :].astype(F32)
                    if k == 0:
                        outs[w][rs, :] = s
                    else:
                        stage2[w][k - 1, rs, :] = s.astype(BF16)
                if k:
                    cp = copy2(w, k)
                    cp.start()
                    second.append(cp)
        for w in range(n_w):
            for k in (1, 2, 3):
                copy2(w, k).wait_recv()
            for r0 in range(0, shapes[w][0], row_chunk[w]):
                rs = pl.ds(r0, row_chunk[w])
                outs[w][rs, :] = ((outs[w][rs, :] + land2[w][0, rs, :].astype(F32)) + land2[w][1, rs, :].astype(F32)) + land2[w][2, rs, :].astype(F32)
        for cp in first + second:
            cp.wait_send()

    hbm = pl.BlockSpec(memory_space=pl.ANY)
    vm = pl.BlockSpec(memory_space=pltpu.VMEM)
    res = pl.pallas_call(
        body, name="rs_grads",
        out_shape=tuple(jax.ShapeDtypeStruct(s, F32) for s in shapes) + tuple(jax.ShapeDtypeStruct((4,) + s, BF16) for s in shapes),
        in_specs=[hbm] * 4, out_specs=(vm,) * 4 + (hbm,) * 4,
        scratch_shapes=[pltpu.VMEM(s, BF16) for s in shapes] + [pltpu.VMEM(s, BF16) for s in shapes]
        + [pltpu.VMEM((3,) + s, BF16) for s in shapes] + [pltpu.VMEM((3,) + s, BF16) for s in shapes]
        + [pltpu.SemaphoreType.DMA((n_w, 4)), pltpu.SemaphoreType.DMA((n_w, 4)), pltpu.SemaphoreType.DMA((n_w, 3)),
           pltpu.SemaphoreType.DMA((n_w, 3)), pltpu.SemaphoreType.DMA((2,))],
        compiler_params=_cparams(),
    )(gin, gout, gup, gdn)
    return res[:4]


PACK_W = 1024


def _all_reduce_small(dg1, dg2, dgf, dgrn, dlng, dlnb, dbs, loss_parts, dws, dcv):
    n_a = 3

    def body(dg1_ref, dg2_ref, dgf_ref, dgrn_ref, dlng_ref, dlnb_ref, dbs_ref, loss_ref, dws_ref, dcv_ref,
             rp_ref, rws_ref, rcv_ref, pack, rx_p, rx_ws, rx_cv, cs_p, cs_ws, cs_cv, g_p, g_ws, g_cv,
             s1_send, s1_recv, s2_send, s2_recv, s3_send, s3_recv):
        x, y, c = lax.axis_index("x"), lax.axis_index("y"), lax.axis_index("c")
        sibling = (x, y, 1 - c)
        chips = [(1 - x, y), (x, 1 - y), (1 - x, 1 - y)]
        pack[...] = jnp.zeros_like(pack)
        pack[0, 0:1, :] = dg1_ref[0:1, :]
        pack[0, 1:2, :] = dg2_ref[0:1, :]
        pack[0, 2:3, :] = dgf_ref[0:1, :]
        pack[0, 3:4, 0:RET_W] = dgrn_ref[0:1, :]
        lsum = loss_ref[0, 0:1, :]
        for i in range(1, N_TB):
            lsum = lsum + loss_ref[i, 0:1, :]
        pack[0, 3:4, RET_W:RET_W + 128] = lsum
        pack[1, 0:HEADS, 0:128] = dlng_ref[0:HEADS, :]
        pack[1, 0:HEADS, 128:256] = dlnb_ref[0:HEADS, :]
        pack[1, 0:HEADS, 256:384] = dbs_ref[0:HEADS, :]

        srcs = [pack, dws_ref, dcv_ref]
        outs = [rp_ref, rws_ref, rcv_ref]
        rxs = [rx_p, rx_ws, rx_cv]
        css = [cs_p, cs_ws, cs_cv]
        gs = [g_p, g_ws, g_cv]
        hl = [1, HEADS // 2, N_DEV // 2]

        def half(ref, a, h):
            return ref.at[pl.ds(h * hl[a], hl[a])]

        ex1 = [pltpu.make_async_remote_copy(src_ref=half(srcs[a], a, 1 - c), dst_ref=rxs[a], send_sem=s1_send.at[a],
                                            recv_sem=s1_recv.at[a], device_id=sibling, device_id_type=MESH) for a in range(n_a)]
        for cp in ex1:
            cp.start()
        ex2 = []
        for a in range(n_a):
            ex1[a].wait_recv()
            css[a][...] = half(srcs[a], a, c)[...] + rxs[a][...]
            for j, chip in enumerate(chips):
                cp = pltpu.make_async_remote_copy(src_ref=css[a], dst_ref=gs[a].at[j], send_sem=s2_send.at[a, j],
                                                  recv_sem=s2_recv.at[a, j], device_id=(*chip, c), device_id_type=MESH)
                cp.start()
                ex2.append(cp)
        ex3 = []
        for a in range(n_a):
            for j in range(3):
                ex2[3 * a + j].wait_recv()
            tot = None
            for q in range(4):
                k = jnp.where(x != (q >> 1), 1, 0) + jnp.where(y != (q & 1), 2, 0)
                term = jnp.where(k == 0, css[a][...], jnp.where(k == 1, gs[a][0], jnp.where(k == 2, gs[a][1], gs[a][2])))
                tot = term if tot is None else tot + term
            half(outs[a], a, c)[...] = tot
            cp = pltpu.make_async_remote_copy(src_ref=half(outs[a], a, c), dst_ref=half(outs[a], a, c), send_sem=s3_send.at[a],
                                              recv_sem=s3_recv.at[a], device_id=sibling, device_id_type=MESH)
            cp.start()
            ex3.append(cp)
        for a in range(n_a):
            pltpu.make_async_remote_copy(src_ref=half(outs[a], a, 1 - c), dst_ref=half(outs[a], a, 1 - c), send_sem=s3_send.at[a],
                                         recv_sem=s3_recv.at[a], device_id=sibling, device_id_type=MESH).wait_recv()
        for cp in ex1 + ex2 + ex3:
            cp.wait_send()

    vm = pl.BlockSpec(memory_space=pltpu.VMEM)
    full = [(2, 8, PACK_W), (HEADS, CHUNK, CHUNK), (N_DEV, 8, FF_SHARD)]
    halves = [(s[0] // 2,) + s[1:] for s in full]
    return pl.pallas_call(
        body, name="ar_small",
        out_shape=tuple(jax.ShapeDtypeStruct(s, F32) for s in full),
        in_specs=[vm] * 10, out_specs=(vm,) * 3,
        scratch_shapes=[pltpu.VMEM(full[0], F32)] + [pltpu.VMEM(s, F32) for s in halves] + [pltpu.VMEM(s, F32) for s in halves]
        + [pltpu.VMEM((3,) + s, F32) for s in halves]
        + [pltpu.SemaphoreType.DMA((n_a,)), pltpu.SemaphoreType.DMA((n_a,)), pltpu.SemaphoreType.DMA((n_a, 3)),
           pltpu.SemaphoreType.DMA((n_a, 3)), pltpu.SemaphoreType.DMA((n_a,)), pltpu.SemaphoreType.DMA((n_a,))],
        compiler_params=_cparams(),
    )(dg1, dg2, dgf, dgrn, dlng, dlnb, dbs, loss_parts, dws, dcv)


def _adam_math(w, g, m, v):
    nm = ADAM_B1 * m + (1.0 - ADAM_B1) * g
    nv = ADAM_B2 * v + (1.0 - ADAM_B2) * (g * g)
    d = -ADAM_LR * ((nm / (1.0 - ADAM_B1 ** ADAM_STEP)) / (jnp.sqrt(nv / (1.0 - ADAM_B2 ** ADAM_STEP)) + ADAM_EPS) + ADAM_WD * w)
    return d, nm, nv


def _adamw(name, w, g, m, v, rows):
    _, r, cdim = w.shape

    def body(w_ref, g_ref, m_ref, v_ref, go_ref, d_ref, nm_ref, nv_ref):
        gg = g_ref[...]
        go_ref[0] = gg
        d, nm, nv = _adam_math(w_ref[0], gg, m_ref[0], v_ref[0])
        d_ref[0], nm_ref[0], nv_ref[0] = d, nm, nv

    spec3 = pl.BlockSpec((1, rows, cdim), lambda i: (0, i, 0))
    sh = jax.ShapeDtypeStruct((1, r, cdim), F32)
    return pl.pallas_call(
        body, name=name, grid=(r // rows,), out_shape=(sh, sh, sh, sh),
        in_specs=[spec3, pl.BlockSpec((rows, cdim), lambda i: (i, 0)), spec3, spec3], out_specs=(spec3,) * 4,
        compiler_params=_cparams(("parallel",)),
    )(w, g, m, v)


def _adamw_small(rp, rws, rcv, params):
    n_p = len(params)

    def body(*refs):
        rp_ref, rws_ref, rcv_ref = refs[:3]
        ins = refs[3:3 + 3 * n_p]
        outs = refs[3 + 3 * n_p:]
        me = 4 * lax.axis_index("x") + 2 * lax.axis_index("y") + lax.axis_index("c")
        grads = [rp_ref[0, 0:1, :], rp_ref[0, 1:2, :], rp_ref[0, 2:3, :], rp_ref[0, 3:4, 0:RET_W],
                 rp_ref[1, 0:HEADS, 0:128], rp_ref[1, 0:HEADS, 128:256], rp_ref[1, 0:HEADS, 256:384],
                 rws_ref[...], rcv_ref[me][0:3, :], None]
        for p in range(n_p):
            w_ref, m_ref, v_ref = ins[3 * p:3 * p + 3]
            o = outs[4 * p:4 * p + 4]
            if p == n_p - 1:
                for j in range(N_DEV):
                    g = rcv_ref[j, 3:4, :]
                    res = (g,) + _adam_math(w_ref[j:j + 1, :], g, m_ref[j:j + 1, :], v_ref[j:j + 1, :])
                    for t in range(4):
                        o[t][j:j + 1, :] = res[t]
                continue
            lead = w_ref.ndim > grads[p].ndim
            rd = (lambda r: r[0]) if lead else (lambda r: r[...])
            res = (grads[p],) + _adam_math(rd(w_ref), grads[p], rd(m_ref), rd(v_ref))
            for t in range(4):
                if lead:
                    o[t][0] = res[t]
                else:
                    o[t][...] = res[t]

    vm = pl.BlockSpec(memory_space=pltpu.VMEM)
    flat = [a for tr in params for a in tr]
    out_shape = tuple(jax.ShapeDtypeStruct(tr[0].shape, F32) for tr in params for _ in range(4))
    res = pl.pallas_call(
        body, name="adamw_small", out_shape=out_shape, in_specs=[vm] * (3 + len(flat)), out_specs=(vm,) * len(out_shape),
        compiler_params=_cparams(),
    )(rp, rws, rcv, *flat)
    return [res[4 * p:4 * p + 4] for p in range(n_p)]


def kernel(x, mix_norm_g, w_in, ret_norm_g, sgu_ln_g, sgu_ln_b, sgu_w_s, sgu_b_s, w_out, ffn_norm_g, w_up, conv_w, conv_b, w_down, final_norm_g, loss_target, m_mix_norm_g, m_w_in, m_ret_norm_g, m_sgu_ln_g, m_sgu_ln_b, m_sgu_w_s, m_sgu_b_s, m_w_out, m_ffn_norm_g, m_w_up, m_conv_w, m_conv_b, m_w_down, m_final_norm_g, v_mix_norm_g, v_w_in, v_ret_norm_g, v_sgu_ln_g, v_sgu_ln_b, v_sgu_w_s, v_sgu_b_s, v_w_out, v_ffn_norm_g, v_w_up, v_conv_w, v_conv_b, v_w_down, v_final_norm_g):
    xs = x[0]
    tgt = loss_target[0]
    cos2, sin2 = _rope_tables()
    mask, qdec, kdec = _decay_tables()
    grn = ret_norm_g.reshape(1, RET_W)
    lng = sgu_ln_g.reshape(1, SGU_W)
    lnb = sgu_ln_b.reshape(1, SGU_W)
    ws = sgu_w_s[0]
    bsb = jnp.broadcast_to(sgu_b_s[0][:, :, None], (HEADS, CHUNK, HEAD_DIM))
    gf = final_norm_g.reshape(1, D_MODEL)
    cb_g = conv_b.reshape(N_DEV, 1, FF_SHARD)

    win_g, wout_g, wup_g, wdn_g, cw_g = _all_gather_weights(w_in[0], w_out[0], w_up[0], w_down[0], conv_w[0])

    proj, h1 = _fwd_proj(xs, mix_norm_g, win_g, cos2, sin2)
    x2, mixcat, o, sprev = _fwd_mix(xs, proj, wout_g, grn, lng, lnb, ws, bsb, mask, qdec, kdec)
    h2, up_pre, u_conv, act, x3, loss_parts = _fwd_ffn(x2, ffn_norm_g, wup_g, cw_g, cb_g, wdn_g, gf, tgt)

    dx3, dpre, dx2, dgf, dg2, dcv = _bwd_ffn(x3, tgt, gf, x2, ffn_norm_g, up_pre, u_conv, wup_g, cw_g, wdn_g)
    gdn_p = _wgrad("wgrad_down", act, dx3, N_FF_PAIR, 0).reshape(D_FF, D_MODEL)
    gup_p = _wgrad("wgrad_up", h2, dpre, 0, N_DEV)
    dproj, dgrn, dlng, dlnb, dws, dbs = _bwd_mix(dx2, proj, o, sprev, wout_g, grn, lng, lnb, ws, bsb, mask, qdec, kdec, cos2, sin2)
    gout_p = _wgrad("wgrad_out", mixcat, dx2, 0, 0, tn=512)[0]
    grad_x, dg1 = _bwd_proj(dproj, win_g, xs, mix_norm_g, dx2)
    gin_p = _wgrad("wgrad_in", h1, dproj, 0, 0, tn=768)[0]

    g_in, g_out, g_up, g_dn = _reduce_scatter_grads(gin_p, gout_p, gup_p, gdn_p)
    rp, rws, rcv = _all_reduce_small(dg1, dg2, dgf, dgrn, dlng, dlnb, dbs, loss_parts, dws, dcv)
    loss = rp[0, 3, RET_W]

    table = {}
    for name, w, g, m, v, rows in (("w_in", w_in, g_in, m_w_in, v_w_in, 256), ("w_out", w_out, g_out, m_w_out, v_w_out, 128),
                                   ("w_up", w_up, g_up, m_w_up, v_w_up, 256), ("w_down", w_down, g_dn, m_w_down, v_w_down, 88)):
        table[name] = _adamw("adamw_" + name, w, g, m, v, rows)
    row = lambda a: a.reshape(1, D_MODEL)
    slab = lambda a: a.reshape(N_DEV, FF_SHARD)
    names_small = ["mix_norm_g", "ffn_norm_g", "final_norm_g", "ret_norm_g", "sgu_ln_g", "sgu_ln_b", "sgu_b_s", "sgu_w_s",
                   "conv_w", "conv_b"]
    params = [(mix_norm_g, m_mix_norm_g, v_mix_norm_g), (ffn_norm_g, m_ffn_norm_g, v_ffn_norm_g),
              (row(final_norm_g), row(m_final_norm_g), row(v_final_norm_g)), (ret_norm_g, m_ret_norm_g, v_ret_norm_g),
              (sgu_ln_g, m_sgu_ln_g, v_sgu_ln_g), (sgu_ln_b, m_sgu_ln_b, v_sgu_ln_b), (sgu_b_s, m_sgu_b_s, v_sgu_b_s),
              (sgu_w_s, m_sgu_w_s, v_sgu_w_s), (conv_w, m_conv_w, v_conv_w), (slab(conv_b), slab(m_conv_b), slab(v_conv_b))]
    for n, res in zip(names_small, _adamw_small(rp, rws, rcv, params)):
        table[n] = res
    table["final_norm_g"] = tuple(a.reshape(D_MODEL) for a in table["final_norm_g"])
    table["conv_b"] = tuple(a.reshape(1, 2 * D_FF) for a in table["conv_b"])

    order = ["mix_norm_g", "w_in", "ret_norm_g", "sgu_ln_g", "sgu_ln_b", "sgu_w_s", "sgu_b_s", "w_out", "ffn_norm_g", "w_up",
             "conv_w", "conv_b", "w_down", "final_norm_g"]
    outs = [loss, grad_x[None]]
    for col in range(4):
        outs += [table[n][col] for n in order]
    return tuple(outs)
```

```python
import functools
import math

import jax
import jax.numpy as jnp
import numpy as np
from jax import lax
from jax.experimental import pallas as pl
from jax.experimental.pallas import tpu as pltpu

F32 = jnp.float32
BF16 = jnp.bfloat16
MESH = pl.DeviceIdType.MESH

N_DEV = 8
SEQ = 2048
D_MODEL = 1024
CHUNK = 128
N_CHUNK = SEQ // CHUNK
HEADS = 4
HEAD_DIM = 128
RET_W = 512
SGU_W = 512
PROJ_W = 3072
D_FF = 2816
FF_SHARD = 704
N_FF_PAIR = 4
IN_SHARD = PROJ_W // N_DEV
OUT_SHARD = D_MODEL // N_DEV
DOWN_SHARD = D_FF // N_DEV
TM = 256
N_TB = SEQ // TM
EPS = 1e-6
ROPE_BASE = 10000.0
K_SCALE = HEAD_DIM ** -0.5
INV_SQRT2 = 0.7071067811865476
INV_SQRT_2PI = 0.3989422804014327

ADAM_LR = 0.001
ADAM_B1 = 0.9
ADAM_B2 = 0.999
ADAM_EPS = 1e-08
ADAM_WD = 0.01
ADAM_STEP = 10

VMEM_LIMIT = 56 * 1024 * 1024


def _cparams(sem=None, vmem=VMEM_LIMIT):
    return pltpu.CompilerParams(dimension_semantics=sem, vmem_limit_bytes=vmem)


def _resident(shape):
    nd = len(shape)
    return pl.BlockSpec(shape, lambda *_: (0,) * nd, pipeline_mode=pl.Buffered(1))


def _dot(a, b):
    return jnp.dot(a, b, preferred_element_type=F32)


def _dot_nt(a, b):
    return lax.dot_general(a, b, (((1,), (1,)), ((), ())), preferred_element_type=F32)


def _dot_tn(a, b):
    return lax.dot_general(a, b, (((0,), (0,)), ((), ())), preferred_element_type=F32)


def _sigmoid(x):
    return 1.0 / (1.0 + jnp.exp(-x))


def _gelu(x):
    return 0.5 * x * (1.0 + lax.erf(x * INV_SQRT2))


def _gelu_grad(x):
    return 0.5 * (1.0 + lax.erf(x * INV_SQRT2)) + x * (jnp.exp(-0.5 * x * x) * INV_SQRT_2PI)


def _rot(xh, cos2, sin2):
    return xh * cos2 + pltpu.roll(xh, HEAD_DIM // 2, 1) * sin2


def _rot_t(dh, cos2, sin2):
    return dh * cos2 + pltpu.roll(dh * sin2, HEAD_DIM // 2, 1)


def _rope_tables():
    half = HEAD_DIM // 2
    inv_freq = jnp.power(ROPE_BASE, -jnp.arange(half, dtype=F32) / half)
    ang = jnp.arange(SEQ, dtype=F32)[:, None] * inv_freq[None, :]
    cos, sin = jnp.cos(ang), jnp.sin(ang)
    cos2 = jnp.tile(jnp.concatenate([cos, cos], axis=-1), (1, HEADS))
    sin2 = jnp.tile(jnp.concatenate([-sin, sin], axis=-1), (1, HEADS))
    return cos2, sin2


def _decay_tables():
    log_gamma = jnp.log(1.0 - jnp.power(2.0, -5.0 - jnp.arange(HEADS, dtype=F32)))
    pos = jnp.arange(CHUNK, dtype=F32)
    diff = pos[:, None] - pos[None, :]
    mask = jnp.where(diff >= 0.0, jnp.exp(log_gamma[:, None, None] * jnp.maximum(diff, 0.0)[None]), 0.0)
    k_decay = jnp.exp(log_gamma[:, None] * (CHUNK - 1.0 - pos)[None])
    q_decay = jnp.exp(log_gamma[:, None] * (pos + 1.0)[None])
    kd = jnp.broadcast_to(k_decay[:, :, None], (HEADS, CHUNK, HEAD_DIM))
    qd = jnp.broadcast_to(q_decay[:, :, None], (HEADS, CHUNK, HEAD_DIM))
    return mask.astype(F32), qd.astype(F32), kd.astype(F32)


def _chunk_decay():
    lg = np.log(np.float32(1.0) - np.power(np.float32(2.0), -5.0 - np.arange(HEADS, dtype=np.float32))).astype(np.float32)
    return [float(np.exp(lg[h] * np.float32(CHUNK))) for h in range(HEADS)]


W_IN, W_OUT, W_UP, W_DOWN, W_CONV = range(5)
GATHERED = {W_IN: ((D_MODEL, PROJ_W), BF16), W_OUT: ((D_MODEL, D_MODEL), BF16), W_UP: ((N_DEV, D_MODEL, FF_SHARD), BF16),
            W_DOWN: ((D_FF, D_MODEL), BF16), W_CONV: ((N_DEV, 8, FF_SHARD), F32)}
SHARD = {W_IN: (D_MODEL, IN_SHARD), W_OUT: (OUT_SHARD, D_MODEL), W_UP: (D_MODEL, FF_SHARD), W_DOWN: (DOWN_SHARD, D_MODEL),
         W_CONV: (8, FF_SHARD)}


class _Gather:
    def __init__(self, ids, stages, gathered, send_sems, recv_sems, local_sems):
        self.ids, self.stages, self.gathered = ids, stages, gathered
        self.send_sems, self.recv_sems, self.local_sems = send_sems, recv_sems, local_sems
        self.x, self.y, self.c = lax.axis_index("x"), lax.axis_index("y"), lax.axis_index("c")
        self.me = (self.x, self.y, self.c)
        self.sibling = (self.x, self.y, 1 - self.c)
        self.chips = [(1 - self.x, self.y), (self.x, 1 - self.y), (1 - self.x, 1 - self.y)]

    def slot(self, n, px, py, pc):
        dev = 4 * px + 2 * py + pc
        w, g = self.ids[n], self.gathered[n]
        if w == W_IN:
            return g.at[:, pl.ds(pl.multiple_of(dev * IN_SHARD, 128), IN_SHARD)]
        if w == W_OUT:
            return g.at[pl.ds(pl.multiple_of(dev * OUT_SHARD, 128), OUT_SHARD), :]
        if w == W_DOWN:
            return g.at[pl.ds(pl.multiple_of(dev * DOWN_SHARD, 32), DOWN_SHARD), :]
        return g.at[dev]

    def copy(self, n, k, block, to, src=None):
        return pltpu.make_async_remote_copy(
            src_ref=self.slot(n, *block) if src is None else src, dst_ref=self.slot(n, *block),
            send_sem=self.send_sems.at[n, k], recv_sem=self.recv_sems.at[n, k], device_id=to, device_id_type=MESH)

    def _mine(self):
        return [pltpu.make_async_copy(self.stages[n], self.slot(n, *self.me), self.local_sems.at[n]) for n in range(len(self.ids))]

    def _first(self):
        out = []
        for n in range(len(self.ids)):
            out.append(self.copy(n, 0, self.me, self.sibling, src=self.stages[n]))
            out += [self.copy(n, 1 + j, self.me, (*chip, self.c), src=self.stages[n]) for j, chip in enumerate(self.chips)]
        return out

    def start(self):
        for cp in self._mine() + self._first():
            cp.start()

    def finish(self):
        passed = []
        for n in range(len(self.ids)):
            for j, chip in enumerate(self.chips):
                self.copy(n, 1 + j, (*chip, self.c), self.me).wait_recv()
                fwd = self.copy(n, 4 + j, (*chip, self.c), self.sibling)
                fwd.start()
                passed.append(fwd)
        for n in range(len(self.ids)):
            self.copy(n, 0, self.sibling, self.me).wait_recv()
            for j, chip in enumerate(self.chips):
                self.copy(n, 4 + j, (*chip, 1 - self.c), self.me).wait_recv()
        for cp in self._first() + passed:
            cp.wait_send()
        for cp in self._mine():
            cp.wait()


def _gather_scratch(n):
    return [pltpu.SemaphoreType.DMA((n, 7)), pltpu.SemaphoreType.DMA((n, 7)), pltpu.SemaphoreType.DMA((n,))]


def _gathered_shapes(ids):
    return tuple(jax.ShapeDtypeStruct(*GATHERED[w]) for w in ids)


def _ag_first(w_in, w_out, w_up, w_down, conv_w):
    ids = [W_IN, W_CONV]

    def body(in_ref, out_ref, up_ref, dn_ref, cw_ref, gin, gcw, so_ref, su_ref, sd_ref, s_in, s_cw, send_sems, recv_sems, local_sems):
        s_in[...] = in_ref[...].astype(BF16)
        s_cw[...] = jnp.zeros_like(s_cw)
        s_cw[0:3, :] = cw_ref[...]
        ag = _Gather(ids, [s_in, s_cw], [gin, gcw], send_sems, recv_sems, local_sems)
        ag.start()
        so_ref[...] = out_ref[...].astype(BF16)
        su_ref[...] = up_ref[...].astype(BF16)
        sd_ref[...] = dn_ref[...].astype(BF16)
        ag.finish()

    vm = pl.BlockSpec(memory_space=pltpu.VMEM)
    hbm = pl.BlockSpec(memory_space=pl.ANY)
    return pl.pallas_call(
        body, name="ag_first",
        out_shape=_gathered_shapes(ids) + tuple(jax.ShapeDtypeStruct(SHARD[w], BF16) for w in (W_OUT, W_UP, W_DOWN)),
        in_specs=[vm] * 5, out_specs=(hbm, hbm, vm, vm, vm),
        scratch_shapes=[pltpu.VMEM(SHARD[W_IN], BF16), pltpu.VMEM(SHARD[W_CONV], F32)] + _gather_scratch(len(ids)),
        compiler_params=_cparams(),
    )(w_in, w_out, w_up, w_down, conv_w)


def _fwd_proj(x, g1, win_g, cos2, sin2, so, sd):
    ids = [W_OUT, W_DOWN]

    def body(x_ref, g_ref, w_ref, cos_ref, sin_ref, so_ref, sd_ref, proj_ref, h1_ref, gout, gdn, send_sems, recv_sems, local_sems):
        ag = _Gather(ids, [so_ref, sd_ref], [gout, gdn], send_sems, recv_sems, local_sems)

        @pl.when(pl.program_id(0) == 0)
        def _():
            ag.start()

        xb = x_ref[...]
        r = lax.rsqrt(jnp.mean(xb * xb, axis=-1, keepdims=True) + EPS)
        h = ((xb * r) * g_ref[...]).astype(BF16)
        h1_ref[...] = h
        p = _dot(h, w_ref[...])
        for hd in range(HEADS):
            sl = slice(hd * HEAD_DIM, (hd + 1) * HEAD_DIM)
            c2, s2 = cos_ref[:, sl], sin_ref[:, sl]
            proj_ref[:, sl] = _rot(p[:, sl], c2, s2)
            ks = slice(RET_W + hd * HEAD_DIM, RET_W + (hd + 1) * HEAD_DIM)
            proj_ref[:, ks] = _rot(p[:, ks], c2, s2) * K_SCALE
        proj_ref[:, 2 * RET_W:] = p[:, 2 * RET_W:]

        @pl.when(pl.program_id(0) == N_TB - 1)
        def _():
            ag.finish()

    tok = lambda w: pl.BlockSpec((TM, w), lambda i: (i, 0))
    hbm = pl.BlockSpec(memory_space=pl.ANY)
    return pl.pallas_call(
        body, name="fwd_proj", grid=(N_TB,),
        out_shape=(jax.ShapeDtypeStruct((SEQ, PROJ_W), F32), jax.ShapeDtypeStruct((SEQ, D_MODEL), BF16)) + _gathered_shapes(ids),
        in_specs=[tok(D_MODEL), _resident((1, D_MODEL)), _resident((D_MODEL, PROJ_W)), tok(RET_W), tok(RET_W), hbm, hbm],
        out_specs=(tok(PROJ_W), tok(D_MODEL), hbm, hbm),
        scratch_shapes=_gather_scratch(len(ids)),
        compiler_params=_cparams(("arbitrary",)),
    )(x, g1, win_g, cos2, sin2, so, sd)


def _causal(w):
    r = lax.broadcasted_iota(jnp.int32, (CHUNK, CHUNK), 0)
    c = lax.broadcasted_iota(jnp.int32, (CHUNK, CHUNK), 1)
    return jnp.where(r >= c, w, 0.0)


def _fwd_mix(x, proj, wout_g, grn, lng, lnb, ws, bsb, mask, qdec, kdec, su):
    cdec = _chunk_decay()
    ids = [W_UP]

    def body(x_ref, p_ref, w_ref, grn_ref, lng_ref, lnb_ref, ws_ref, bsb_ref, m_ref, qd_ref, kd_ref, su_ref,
             x2_ref, cat_ref, o_ref, sp_ref, gup, state, send_sems, recv_sems, local_sems):
        ag = _Gather(ids, [su_ref], [gup], send_sems, recv_sems, local_sems)

        @pl.when(pl.program_id(0) == 0)
        def _():
            state[...] = jnp.zeros_like(state)
            ag.start()

        for h in range(HEADS):
            sl = slice(h * HEAD_DIM, (h + 1) * HEAD_DIM)
            q = p_ref[:, sl]
            k = p_ref[:, RET_W + h * HEAD_DIM:RET_W + (h + 1) * HEAD_DIM]
            v = p_ref[:, 2 * RET_W + h * HEAD_DIM:2 * RET_W + (h + 1) * HEAD_DIM]
            g = p_ref[:, 3 * RET_W + h * HEAD_DIM:3 * RET_W + (h + 1) * HEAD_DIM]
            qb, kb, vb = q.astype(BF16), k.astype(BF16), v.astype(BF16)
            a = _dot_nt(qb, kb) * m_ref[h]
            spb = state[h].astype(BF16)
            sp_ref[0, h] = spb
            o = _dot(a.astype(BF16), vb) + _dot((q * qd_ref[h]).astype(BF16), spb)
            state[h] = state[h] * cdec[h] + _dot_tn((k * kd_ref[h]).astype(BF16), vb)
            o_ref[:, sl] = o
            rinv = lax.rsqrt(jnp.mean(o * o, axis=-1, keepdims=True) + EPS)
            rn = (o * rinv) * grn_ref[:, sl]
            cat_ref[:, sl] = ((g * _sigmoid(g)) * rn).astype(BF16)
        for gi in range(HEADS):
            sl = slice(gi * HEAD_DIM, (gi + 1) * HEAD_DIM)
            u = p_ref[:, 4 * RET_W + gi * HEAD_DIM:4 * RET_W + (gi + 1) * HEAD_DIM]
            sv = p_ref[:, 4 * RET_W + SGU_W + gi * HEAD_DIM:4 * RET_W + SGU_W + (gi + 1) * HEAD_DIM]
            gv = _gelu(sv)
            xc = gv - jnp.mean(gv, axis=-1, keepdims=True)
            vn = (xc * lax.rsqrt(jnp.mean(xc * xc, axis=-1, keepdims=True) + EPS)) * lng_ref[:, sl] + lnb_ref[:, sl]
            mixed = _dot(_causal(ws_ref[gi]).astype(BF16), vn.astype(BF16)) + bsb_ref[gi]
            cat_ref[:, RET_W + gi * HEAD_DIM:RET_W + (gi + 1) * HEAD_DIM] = (_gelu(u) * mixed).astype(BF16)
        x2_ref[...] = x_ref[...] + _dot(cat_ref[...], w_ref[...])

        @pl.when(pl.program_id(0) == N_CHUNK - 1)
        def _():
            ag.finish()

    ch = lambda w: pl.BlockSpec((CHUNK, w), lambda i: (i, 0))
    hcc = (HEADS, CHUNK, CHUNK)
    hbm = pl.BlockSpec(memory_space=pl.ANY)
    return pl.pallas_call(
        body, name="fwd_mix", grid=(N_CHUNK,),
        out_shape=(jax.ShapeDtypeStruct((SEQ, D_MODEL), F32), jax.ShapeDtypeStruct((SEQ, D_MODEL), BF16),
                   jax.ShapeDtypeStruct((SEQ, RET_W), F32), jax.ShapeDtypeStruct((N_CHUNK, HEADS, HEAD_DIM, HEAD_DIM), BF16))
        + _gathered_shapes(ids),
        in_specs=[ch(D_MODEL), ch(PROJ_W), _resident((D_MODEL, D_MODEL)), _resident((1, RET_W)), _resident((1, SGU_W)),
                  _resident((1, SGU_W)), _resident(hcc), _resident(hcc), _resident(hcc), _resident(hcc), _resident(hcc), hbm],
        out_specs=(ch(D_MODEL), ch(D_MODEL), ch(RET_W), pl.BlockSpec((1, HEADS, HEAD_DIM, HEAD_DIM), lambda i: (i, 0, 0, 0)), hbm),
        scratch_shapes=[pltpu.VMEM((HEADS, HEAD_DIM, HEAD_DIM), F32)] + _gather_scratch(len(ids)),
        compiler_params=_cparams(("arbitrary",)),
    )(x, proj, wout_g, grn, lng, lnb, ws, bsb, mask, qdec, kdec, su)


def _conv_taps(p, prev8):
    row = lax.broadcasted_iota(jnp.int32, p.shape, 0)
    p1 = jnp.where(row == 0, prev8[7:8, :], pltpu.roll(p, 1, 0))
    p2 = jnp.where(row == 0, prev8[6:7, :], jnp.where(row == 1, prev8[7:8, :], pltpu.roll(p, 2, 0)))
    return p1, p2


def _fwd_ffn(x2, g2, wup_g, cw_g, cb_g, wdn_g, gf, tgt):
    def body(x_ref, g_ref, wu_ref, cw_ref, cb_ref, wd_ref, gf_ref, t_ref, h2_ref, up_ref, u_ref, act_ref, x3_ref, loss_ref, carry):
        @pl.when(pl.program_id(0) == 0)
        def _():
            carry[...] = jnp.zeros_like(carry)

        xb = x_ref[...]
        r = lax.rsqrt(jnp.mean(xb * xb, axis=-1, keepdims=True) + EPS)
        h = ((xb * r) * g_ref[...]).astype(BF16)
        h2_ref[...] = h
        acc = xb
        for j in range(N_FF_PAIR):
            u = []
            for s in (j, j + N_FF_PAIR):
                p = _dot(h, wu_ref[s])
                up_ref[s] = p.astype(BF16)
                p1, p2 = _conv_taps(p, carry[s])
                carry[s] = p[TM - 8:, :]
                cw = cw_ref[s]
                us = p2 * cw[0:1, :] + p1 * cw[1:2, :] + p * cw[2:3, :] + cb_ref[s]
                u_ref[s] = us.astype(BF16)
                u.append(us)
            a = ((u[0] * _sigmoid(u[0])) * u[1]).astype(BF16)
            act_ref[j] = a
            acc = acc + _dot(a, wd_ref[pl.ds(j * FF_SHARD, FF_SHARD), :])
        x3_ref[...] = acc
        r3 = lax.rsqrt(jnp.mean(acc * acc, axis=-1, keepdims=True) + EPS)
        diff = (acc * r3) * gf_ref[...] - t_ref[...]
        loss_ref[...] = jnp.full(loss_ref.shape, 0.5 * jnp.sum(jnp.mean(diff * diff, axis=-1)), F32)

    tok = lambda w: pl.BlockSpec((TM, w), lambda i: (i, 0))
    return pl.pallas_call(
        body, name="fwd_ffn", grid=(N_TB,),
        out_shape=(jax.ShapeDtypeStruct((SEQ, D_MODEL), BF16), jax.ShapeDtypeStruct((N_DEV, SEQ, FF_SHARD), BF16),
                   jax.ShapeDtypeStruct((N_DEV, SEQ, FF_SHARD), BF16),
                   jax.ShapeDtypeStruct((N_FF_PAIR, SEQ, FF_SHARD), BF16), jax.ShapeDtypeStruct((SEQ, D_MODEL), F32),
                   jax.ShapeDtypeStruct((N_TB, 8, 128), F32)),
        in_specs=[tok(D_MODEL), _resident((1, D_MODEL)), _resident((N_DEV, D_MODEL, FF_SHARD)), _resident((N_DEV, 8, FF_SHARD)),
                  _resident((N_DEV, 1, FF_SHARD)), _resident((D_FF, D_MODEL)), _resident((1, D_MODEL)), tok(D_MODEL)],
        out_specs=(tok(D_MODEL), pl.BlockSpec((N_DEV, TM, FF_SHARD), lambda i: (0, i, 0)),
                   pl.BlockSpec((N_DEV, TM, FF_SHARD), lambda i: (0, i, 0)),
                   pl.BlockSpec((N_FF_PAIR, TM, FF_SHARD), lambda i: (0, i, 0)), tok(D_MODEL),
                   pl.BlockSpec((1, 8, 128), lambda i: (i, 0, 0))),
        scratch_shapes=[pltpu.VMEM((N_DEV, 8, FF_SHARD), F32)],
        compiler_params=_cparams(("arbitrary",)),
    )(x2, g2, wup_g, cw_g, cb_g, wdn_g, gf, tgt)


def _bwd_ffn(x3, tgt, gf, x2, g2, up_pre, u_conv, wup_g, cw_g, wdn_g):
    def body(x3_ref, t_ref, gf_ref, x2_ref, g2_ref, up_ref, u_ref, wu_ref, cw_ref, wd_ref,
             dx3_ref, dpre_ref, dx2_ref, dgf_ref, dg2_ref, dcv_ref, nxt):
        i = pl.program_id(0)

        @pl.when(i == 0)
        def _():
            nxt[...] = jnp.zeros_like(nxt)
            dgf_ref[...] = jnp.zeros_like(dgf_ref)
            dg2_ref[...] = jnp.zeros_like(dg2_ref)
            dcv_ref[...] = jnp.zeros_like(dcv_ref)

        x3 = x3_ref[...]
        r3 = lax.rsqrt(jnp.mean(x3 * x3, axis=-1, keepdims=True) + EPS)
        xh3 = x3 * r3
        dy = (xh3 * gf_ref[...] - t_ref[...]) * (1.0 / D_MODEL)
        dgf_ref[0:1, :] += jnp.sum(dy * xh3, axis=0, keepdims=True)
        t3 = dy * gf_ref[...]
        dx3 = r3 * (t3 - xh3 * jnp.mean(t3 * xh3, axis=-1, keepdims=True))
        dx3b = dx3.astype(BF16)
        dx3_ref[...] = dx3b
        dh2 = jnp.zeros((TM, D_MODEL), F32)
        row = lax.broadcasted_iota(jnp.int32, (TM, FF_SHARD), 0)
        for j in range(N_FF_PAIR):
            dact = _dot_nt(dx3b, wd_ref[pl.ds(j * FF_SHARD, FF_SHARD), :])
            ua = u_ref[j].astype(F32)
            ub = u_ref[j + N_FF_PAIR].astype(F32)
            sg = _sigmoid(ua)
            du = [dact * ub * (sg * (1.0 + ua * (1.0 - sg))), dact * (ua * sg)]
            for n, s in enumerate((j, j + N_FF_PAIR)):
                d = du[n]
                cw = cw_ref[s]
                nx = nxt[s]
                n1 = jnp.where(row == TM - 1, nx[0:1, :], pltpu.roll(d, TM - 1, 0))
                n2 = jnp.where(row == TM - 2, nx[0:1, :], jnp.where(row == TM - 1, nx[1:2, :], pltpu.roll(d, TM - 2, 0)))
                nxt[s] = d[0:8, :]
                dp = (d * cw[2:3, :] + n1 * cw[1:2, :] + n2 * cw[0:1, :]).astype(BF16)
                dpre_ref[s] = dp
                p = up_ref[s].astype(F32)
                dcv_ref[s, 0:1, :] += jnp.sum(n2 * p, axis=0, keepdims=True)
                dcv_ref[s, 1:2, :] += jnp.sum(n1 * p, axis=0, keepdims=True)
                dcv_ref[s, 2:3, :] += jnp.sum(d * p, axis=0, keepdims=True)
                dcv_ref[s, 3:4, :] += jnp.sum(d, axis=0, keepdims=True)
                dh2 = dh2 + _dot_nt(dp, wu_ref[s])
        x2 = x2_ref[...]
        r2 = lax.rsqrt(jnp.mean(x2 * x2, axis=-1, keepdims=True) + EPS)
        xh2 = x2 * r2
        dg2_ref[0:1, :] += jnp.sum(dh2 * xh2, axis=0, keepdims=True)
        t2 = dh2 * g2_ref[...]
        dx2_ref[...] = dx3 + r2 * (t2 - xh2 * jnp.mean(t2 * xh2, axis=-1, keepdims=True))

    rev = lambda w: pl.BlockSpec((TM, w), lambda i: (N_TB - 1 - i, 0))
    rev3 = lambda: pl.BlockSpec((N_DEV, TM, FF_SHARD), lambda i: (0, N_TB - 1 - i, 0))
    acc = lambda s: pl.BlockSpec(s, lambda i: (0,) * len(s))
    return pl.pallas_call(
        body, name="bwd_ffn", grid=(N_TB,),
        out_shape=(jax.ShapeDtypeStruct((SEQ, D_MODEL), BF16), jax.ShapeDtypeStruct((N_DEV, SEQ, FF_SHARD), BF16),
                   jax.ShapeDtypeStruct((SEQ, D_MODEL), F32), jax.ShapeDtypeStruct((8, D_MODEL), F32),
                   jax.ShapeDtypeStruct((8, D_MODEL), F32), jax.ShapeDtypeStruct((N_DEV, 8, FF_SHARD), F32)),
        in_specs=[rev(D_MODEL), rev(D_MODEL), _resident((1, D_MODEL)), rev(D_MODEL), _resident((1, D_MODEL)), rev3(), rev3(),
                  _resident((N_DEV, D_MODEL, FF_SHARD)), _resident((N_DEV, 8, FF_SHARD)), _resident((D_FF, D_MODEL))],
        out_specs=(rev(D_MODEL), rev3(), rev(D_MODEL), acc((8, D_MODEL)), acc((8, D_MODEL)), acc((N_DEV, 8, FF_SHARD))),
        scratch_shapes=[pltpu.VMEM((N_DEV, 8, FF_SHARD), F32)],
        compiler_params=_cparams(("arbitrary",)),
    )(x3, tgt, gf, x2, g2, up_pre, u_conv, wup_g, cw_g, wdn_g)


def _bwd_mix(dx2, proj, o, sprev, wout_g, grn, lng, lnb, ws, bsb, mask, qdec, kdec, cos2, sin2):
    cdec = _chunk_decay()

    def body(dx2_ref, p_ref, o_ref, sp_ref, w_ref, grn_ref, lng_ref, lnb_ref, ws_ref, bsb_ref, m_ref, qd_ref, kd_ref,
             cos_ref, sin_ref, dp_ref, dgrn_ref, dlng_ref, dlnb_ref, dws_ref, dbs_ref, dstate, dbs_acc):
        i = pl.program_id(0)

        @pl.when(i == 0)
        def _():
            dstate[...] = jnp.zeros_like(dstate)
            dgrn_ref[...] = jnp.zeros_like(dgrn_ref)
            dlng_ref[...] = jnp.zeros_like(dlng_ref)
            dlnb_ref[...] = jnp.zeros_like(dlnb_ref)
            dws_ref[...] = jnp.zeros_like(dws_ref)
            dbs_ref[...] = jnp.zeros_like(dbs_ref)
            dbs_acc[...] = jnp.zeros_like(dbs_acc)

        dmix = _dot_nt(dx2_ref[...].astype(BF16), w_ref[...])
        for h in range(HEADS):
            sl = slice(h * HEAD_DIM, (h + 1) * HEAD_DIM)
            q = p_ref[:, sl]
            k = p_ref[:, RET_W + h * HEAD_DIM:RET_W + (h + 1) * HEAD_DIM]
            v = p_ref[:, 2 * RET_W + h * HEAD_DIM:2 * RET_W + (h + 1) * HEAD_DIM]
            g = p_ref[:, 3 * RET_W + h * HEAD_DIM:3 * RET_W + (h + 1) * HEAD_DIM]
            o = o_ref[:, sl]
            rinv = lax.rsqrt(jnp.mean(o * o, axis=-1, keepdims=True) + EPS)
            oh = o * rinv
            gr = grn_ref[:, sl]
            sg = _sigmoid(g)
            dret = dmix[:, sl]
            dp_ref[:, 3 * RET_W + h * HEAD_DIM:3 * RET_W + (h + 1) * HEAD_DIM] = (
                dret * (oh * gr) * (sg * (1.0 + g * (1.0 - sg)))).astype(BF16)
            drn = dret * (g * sg)
            dgrn_ref[0:1, sl] += jnp.sum(drn * oh, axis=0, keepdims=True)
            t = drn * gr
            do = rinv * (t - oh * jnp.mean(t * oh, axis=-1, keepdims=True))
            qb, kb, vb, dob = q.astype(BF16), k.astype(BF16), v.astype(BF16), do.astype(BF16)
            m = m_ref[h]
            ab = (_dot_nt(qb, kb) * m).astype(BF16)
            dab = (_dot_nt(dob, vb) * m).astype(BF16)
            spb = sp_ref[0, h]
            dsn = dstate[h]
            dsnb = dsn.astype(BF16)
            qdb = (q * qd_ref[h]).astype(BF16)
            kdb = (k * kd_ref[h]).astype(BF16)
            dq = _dot(dab, kb) + _dot_nt(dob, spb) * qd_ref[h]
            dk = _dot_tn(dab, qb) + _dot_nt(vb, dsnb) * kd_ref[h]
            dv = _dot_tn(ab, dob) + _dot(kdb, dsnb)
            dstate[h] = dsn * cdec[h] + _dot_tn(qdb, dob)
            c2, s2 = cos_ref[:, sl], sin_ref[:, sl]
            dp_ref[:, sl] = _rot_t(dq, c2, s2).astype(BF16)
            dp_ref[:, RET_W + h * HEAD_DIM:RET_W + (h + 1) * HEAD_DIM] = _rot_t(dk * K_SCALE, c2, s2).astype(BF16)
            dp_ref[:, 2 * RET_W + h * HEAD_DIM:2 * RET_W + (h + 1) * HEAD_DIM] = dv.astype(BF16)
        for gi in range(HEADS):
            sl = slice(gi * HEAD_DIM, (gi + 1) * HEAD_DIM)
            u = p_ref[:, 4 * RET_W + gi * HEAD_DIM:4 * RET_W + (gi + 1) * HEAD_DIM]
            sv = p_ref[:, 4 * RET_W + SGU_W + gi * HEAD_DIM:4 * RET_W + SGU_W + (gi + 1) * HEAD_DIM]
            gv = _gelu(sv)
            xc = gv - jnp.mean(gv, axis=-1, keepdims=True)
            rstd = lax.rsqrt(jnp.mean(xc * xc, axis=-1, keepdims=True) + EPS)
            xh = xc * rstd
            lg = lng_ref[:, sl]
            vnb = (xh * lg + lnb_ref[:, sl]).astype(BF16)
            wcb = _causal(ws_ref[gi]).astype(BF16)
            mixed = _dot(wcb, vnb) + bsb_ref[gi]
            dsgu = dmix[:, RET_W + gi * HEAD_DIM:RET_W + (gi + 1) * HEAD_DIM]
            dmixed = dsgu * _gelu(u)
            dmb = dmixed.astype(BF16)
            dws_ref[gi] += _causal(_dot_nt(dmb, vnb))
            dbs_acc[gi] += dmixed
            dvn = _dot_tn(wcb, dmb)
            dlng_ref[gi:gi + 1, :] += jnp.sum(dvn * xh, axis=0, keepdims=True)
            dlnb_ref[gi:gi + 1, :] += jnp.sum(dvn, axis=0, keepdims=True)
            dxh = dvn * lg
            dgv = rstd * (dxh - jnp.mean(dxh, axis=-1, keepdims=True) - xh * jnp.mean(dxh * xh, axis=-1, keepdims=True))
            dp_ref[:, 4 * RET_W + gi * HEAD_DIM:4 * RET_W + (gi + 1) * HEAD_DIM] = (dsgu * mixed * _gelu_grad(u)).astype(BF16)
            dp_ref[:, 4 * RET_W + SGU_W + gi * HEAD_DIM:4 * RET_W + SGU_W + (gi + 1) * HEAD_DIM] = (
                dgv * _gelu_grad(sv)).astype(BF16)

        @pl.when(i == N_CHUNK - 1)
        def _():
            for gi in range(HEADS):
                col = jnp.broadcast_to(jnp.sum(dbs_acc[gi], axis=-1, keepdims=True), (CHUNK, CHUNK))
                dbs_ref[gi:gi + 1, :] = jnp.transpose(col)[0:1, :]

    rev = lambda w: pl.BlockSpec((CHUNK, w), lambda i: (N_CHUNK - 1 - i, 0))
    hcc = (HEADS, CHUNK, CHUNK)
    acc = lambda s: pl.BlockSpec(s, lambda i: (0,) * len(s))
    return pl.pallas_call(
        body, name="bwd_mix", grid=(N_CHUNK,),
        out_shape=(jax.ShapeDtypeStruct((SEQ, PROJ_W), BF16), jax.ShapeDtypeStruct((8, RET_W), F32),
                   jax.ShapeDtypeStruct((8, HEAD_DIM), F32), jax.ShapeDtypeStruct((8, HEAD_DIM), F32),
                   jax.ShapeDtypeStruct(hcc, F32), jax.ShapeDtypeStruct((8, CHUNK), F32)),
        in_specs=[rev(D_MODEL), rev(PROJ_W), rev(RET_W),
                  pl.BlockSpec((1, HEADS, HEAD_DIM, HEAD_DIM), lambda i: (N_CHUNK - 1 - i, 0, 0, 0)),
                  _resident((D_MODEL, D_MODEL)), _resident((1, RET_W)), _resident((1, SGU_W)), _resident((1, SGU_W)),
                  _resident(hcc), _resident(hcc), _resident(hcc), _resident(hcc), _resident(hcc), rev(RET_W), rev(RET_W)],
        out_specs=(rev(PROJ_W), acc((8, RET_W)), acc((8, HEAD_DIM)), acc((8, HEAD_DIM)), acc(hcc), acc((8, CHUNK))),
        scratch_shapes=[pltpu.VMEM((HEADS, HEAD_DIM, HEAD_DIM), F32), pltpu.VMEM((HEADS, CHUNK, CHUNK), F32)],
        compiler_params=_cparams(("arbitrary",)),
    )(dx2, proj, o, sprev, wout_g, grn, lng, lnb, ws, bsb, mask, qdec, kdec, cos2, sin2)


def _bwd_proj(dproj, win_g, x, g1, dx2):
    def body(dp_ref, w_ref, x_ref, g_ref, dx2_ref, dx_ref, dg_ref):
        @pl.when(pl.program_id(0) == 0)
        def _():
            dg_ref[...] = jnp.zeros_like(dg_ref)

        dh = _dot_nt(dp_ref[...], w_ref[...])
        xb = x_ref[...]
        r = lax.rsqrt(jnp.mean(xb * xb, axis=-1, keepdims=True) + EPS)
        xh = xb * r
        dg_ref[0:1, :] += jnp.sum(dh * xh, axis=0, keepdims=True)
        t = dh * g_ref[...]
        dx_ref[...] = dx2_ref[...] + r * (t - xh * jnp.mean(t * xh, axis=-1, keepdims=True))

    tok = lambda w: pl.BlockSpec((TM, w), lambda i: (i, 0))
    return pl.pallas_call(
        body, name="bwd_proj", grid=(N_TB,),
        out_shape=(jax.ShapeDtypeStruct((SEQ, D_MODEL), F32), jax.ShapeDtypeStruct((8, D_MODEL), F32)),
        in_specs=[tok(PROJ_W), _resident((D_MODEL, PROJ_W)), tok(D_MODEL), _resident((1, D_MODEL)), tok(D_MODEL)],
        out_specs=(tok(D_MODEL), pl.BlockSpec((8, D_MODEL), lambda i: (0, 0))),
        compiler_params=_cparams(("arbitrary",)),
    )(dproj, win_g, x, g1, dx2)


def _wgrad(name, a, b, n_a, n_b, tn=None):
    n = max(n_a, n_b, 1)
    m_w, n_w = a.shape[-1], b.shape[-1]
    tn = n_w if tn is None else tn

    def body(a_ref, b_ref, o_ref):
        av = a_ref[0] if n_a else a_ref[...]
        bv = b_ref[0] if n_b else b_ref[...]
        o_ref[0] = _dot_tn(av.astype(BF16), bv.astype(BF16)).astype(BF16)

    a_spec = pl.BlockSpec((1, SEQ, m_w), lambda j, t: (j, 0, 0)) if n_a else pl.BlockSpec((SEQ, m_w), lambda j, t: (0, 0))
    b_spec = pl.BlockSpec((1, SEQ, tn), lambda j, t: (j, 0, t)) if n_b else pl.BlockSpec((SEQ, tn), lambda j, t: (0, t))
    return pl.pallas_call(
        body, name=name, grid=(n, n_w // tn),
        out_shape=jax.ShapeDtypeStruct((n, m_w, n_w), BF16),
        in_specs=[a_spec, b_spec], out_specs=pl.BlockSpec((1, m_w, tn), lambda j, t: (j, 0, t)),
        compiler_params=_cparams(("parallel", "parallel")),
    )(a, b)


def _reduce_scatter_grads(gin, gout, gup, gdn):
    n_w = 4
    shapes = [(D_MODEL, IN_SHARD), (OUT_SHARD, D_MODEL), (D_MODEL, FF_SHARD), (DOWN_SHARD, D_MODEL)]
    row_chunk = [128, 128, 128, 176]

    def body(gin_ref, gout_ref, gup_ref, gdn_ref, o_in, o_out, o_up, o_dn, l_in, l_out, l_up, l_dn,
             a_in, a_out, a_up, a_dn, b_in, b_out, b_up, b_dn, t_in, t_out, t_up, t_dn, r_in, r_out, r_up, r_dn,
             s1_send, s1_recv, s2_send, s2_recv, ld_sems):
        x, y, c = lax.axis_index("x"), lax.axis_index("y"), lax.axis_index("c")
        sibling = (x, y, 1 - c)
        chips = [(x, y), (1 - x, y), (x, 1 - y), (1 - x, 1 - y)]
        srcs = [gin_ref, gout_ref, gup_ref, gdn_ref]
        outs = [o_in, o_out, o_up, o_dn]
        land1 = [l_in, l_out, l_up, l_dn]
        bufa = [a_in, a_out, a_up, a_dn]
        bufb = [b_in, b_out, b_up, b_dn]
        stage2 = [t_in, t_out, t_up, t_dn]
        land2 = [r_in, r_out, r_up, r_dn]

        def block(w, px, py, pc):
            dev = 4 * px + 2 * py + pc
            if w == 0:
                return srcs[0].at[:, pl.ds(pl.multiple_of(dev * IN_SHARD, 128), IN_SHARD)]
            if w == 1:
                return srcs[1].at[pl.ds(pl.multiple_of(dev * OUT_SHARD, 128), OUT_SHARD), :]
            if w == 2:
                return srcs[2].at[dev]
            return srcs[3].at[pl.ds(pl.multiple_of(dev * DOWN_SHARD, 32), DOWN_SHARD), :]

        def copy1(w, k):
            return pltpu.make_async_remote_copy(
                src_ref=block(w, *chips[k], 1 - c), dst_ref=land1[w].at[k],
                send_sem=s1_send.at[w, k], recv_sem=s1_recv.at[w, k], device_id=sibling, device_id_type=MESH)

        def copy2(w, k):
            return pltpu.make_async_remote_copy(
                src_ref=stage2[w].at[k - 1], dst_ref=land2[w].at[k - 1],
                send_sem=s2_send.at[w, k - 1], recv_sem=s2_recv.at[w, k - 1], device_id=(*chips[k], c), device_id_type=MESH)

        first = [copy1(w, k) for w in range(n_w) for k in range(4)]
        for cp in first:
            cp.start()
        second = []
        for w in range(n_w):
            for k in (1, 2, 3, 0):
                la = pltpu.make_async_copy(block(w, *chips[k], c), bufa[w], ld_sems.at[0])
                la.start()
                copy1(w, k).wait_recv()
                lb = pltpu.make_async_copy(land1[w].at[k], bufb[w], ld_sems.at[1])
                lb.start()
                la.wait()
                lb.wait()
                for r0 in range(0, shapes[w][0], row_chunk[w]):
                    rs = pl.ds(r0, row_chunk[w])
                    s = bufa[w][rs, :].astype(F32) + bufb[w][rs, :].astype(F32)
                    if k == 0:
                        outs[w][rs, :] = s
                    else:
                        stage2[w][k - 1, rs, :] = s.astype(BF16)
                if k:
                    cp = copy2(w, k)
                    cp.start()
                    second.append(cp)
        for w in range(n_w):
            for k in (1, 2, 3):
                copy2(w, k).wait_recv()
            for r0 in range(0, shapes[w][0], row_chunk[w]):
                rs = pl.ds(r0, row_chunk[w])
                outs[w][rs, :] = ((outs[w][rs, :] + land2[w][0, rs, :].astype(F32)) + land2[w][1, rs, :].astype(F32)) + land2[w][2, rs, :].astype(F32)
        for cp in first + second:
            cp.wait_send()

    hbm = pl.BlockSpec(memory_space=pl.ANY)
    vm = pl.BlockSpec(memory_space=pltpu.VMEM)
    res = pl.pallas_call(
        body, name="rs_grads",
        out_shape=tuple(jax.ShapeDtypeStruct(s, F32) for s in shapes) + tuple(jax.ShapeDtypeStruct((4,) + s, BF16) for s in shapes),
        in_specs=[hbm] * 4, out_specs=(vm,) * 4 + (hbm,) * 4,
        scratch_shapes=[pltpu.VMEM(s, BF16) for s in shapes] + [pltpu.VMEM(s, BF16) for s in shapes]
        + [pltpu.VMEM((3,) + s, BF16) for s in shapes] + [pltpu.VMEM((3,) + s, BF16) for s in shapes]
        + [pltpu.SemaphoreType.DMA((n_w, 4)), pltpu.SemaphoreType.DMA((n_w, 4)), pltpu.SemaphoreType.DMA((n_w, 3)),
           pltpu.SemaphoreType.DMA((n_w, 3)), pltpu.SemaphoreType.DMA((2,))],
        compiler_params=_cparams(),
    )(gin, gout, gup, gdn)
    return res[:4]


PACK_W = 1024


def _all_reduce_small(dg1, dg2, dgf, dgrn, dlng, dlnb, dbs, loss_parts, dws, dcv):
    n_a = 3

    def body(dg1_ref, dg2_ref, dgf_ref, dgrn_ref, dlng_ref, dlnb_ref, dbs_ref, loss_ref, dws_ref, dcv_ref,
             rp_ref, rws_ref, rcv_ref, pack, rx_p, rx_ws, rx_cv, cs_p, cs_ws, cs_cv, g_p, g_ws, g_cv,
             s1_send, s1_recv, s2_send, s2_recv, s3_send, s3_recv):
        x, y, c = lax.axis_index("x"), lax.axis_index("y"), lax.axis_index("c")
        sibling = (x, y, 1 - c)
        chips = [(1 - x, y), (x, 1 - y), (1 - x, 1 - y)]
        pack[...] = jnp.zeros_like(pack)
        pack[0, 0:1, :] = dg1_ref[0:1, :]
        pack[0, 1:2, :] = dg2_ref[0:1, :]
        pack[0, 2:3, :] = dgf_ref[0:1, :]
        pack[0, 3:4, 0:RET_W] = dgrn_ref[0:1, :]
        lsum = loss_ref[0, 0:1, :]
        for i in range(1, N_TB):
            lsum = lsum + loss_ref[i, 0:1, :]
        pack[0, 3:4, RET_W:RET_W + 128] = lsum
        pack[1, 0:HEADS, 0:128] = dlng_ref[0:HEADS, :]
        pack[1, 0:HEADS, 128:256] = dlnb_ref[0:HEADS, :]
        pack[1, 0:HEADS, 256:384] = dbs_ref[0:HEADS, :]

        srcs = [pack, dws_ref, dcv_ref]
        outs = [rp_ref, rws_ref, rcv_ref]
        rxs = [rx_p, rx_ws, rx_cv]
        css = [cs_p, cs_ws, cs_cv]
        gs = [g_p, g_ws, g_cv]
        hl = [1, HEADS // 2, N_DEV // 2]

        def half(ref, a, h):
            return ref.at[pl.ds(h * hl[a], hl[a])]

        ex1 = [pltpu.make_async_remote_copy(src_ref=half(srcs[a], a, 1 - c), dst_ref=rxs[a], send_sem=s1_send.at[a],
                                            recv_sem=s1_recv.at[a], device_id=sibling, device_id_type=MESH) for a in range(n_a)]
        for cp in ex1:
            cp.start()
        ex2 = []
        for a in range(n_a):
            ex1[a].wait_recv()
            css[a][...] = half(srcs[a], a, c)[...] + rxs[a][...]
            for j, chip in enumerate(chips):
                cp = pltpu.make_async_remote_copy(src_ref=css[a], dst_ref=gs[a].at[j], send_sem=s2_send.at[a, j],
                                                  recv_sem=s2_recv.at[a, j], device_id=(*chip, c), device_id_type=MESH)
                cp.start()
                ex2.append(cp)
        ex3 = []
        for a in range(n_a):
            for j in range(3):
                ex2[3 * a + j].wait_recv()
            tot = None
            for q in range(4):
                k = jnp.where(x != (q >> 1), 1, 0) + jnp.where(y != (q & 1), 2, 0)
                term = jnp.where(k == 0, css[a][...], jnp.where(k == 1, gs[a][0], jnp.where(k == 2, gs[a][1], gs[a][2])))
                tot = term if tot is None else tot + term
            half(outs[a], a, c)[...] = tot
            cp = pltpu.make_async_remote_copy(src_ref=half(outs[a], a, c), dst_ref=half(outs[a], a, c), send_sem=s3_send.at[a],
                                              recv_sem=s3_recv.at[a], device_id=sibling, device_id_type=MESH)
            cp.start()
            ex3.append(cp)
        for a in range(n_a):
            pltpu.make_async_remote_copy(src_ref=half(outs[a], a, 1 - c), dst_ref=half(outs[a], a, 1 - c), send_sem=s3_send.at[a],
                                         recv_sem=s3_recv.at[a], device_id=sibling, device_id_type=MESH).wait_recv()
        for cp in ex1 + ex2 + ex3:
            cp.wait_send()

    vm = pl.BlockSpec(memory_space=pltpu.VMEM)
    full = [(2, 8, PACK_W), (HEADS, CHUNK, CHUNK), (N_DEV, 8, FF_SHARD)]
    halves = [(s[0] // 2,) + s[1:] for s in full]
    return pl.pallas_call(
        body, name="ar_small",
        out_shape=tuple(jax.ShapeDtypeStruct(s, F32) for s in full),
        in_specs=[vm] * 10, out_specs=(vm,) * 3,
        scratch_shapes=[pltpu.VMEM(full[0], F32)] + [pltpu.VMEM(s, F32) for s in halves] + [pltpu.VMEM(s, F32) for s in halves]
        + [pltpu.VMEM((3,) + s, F32) for s in halves]
        + [pltpu.SemaphoreType.DMA((n_a,)), pltpu.SemaphoreType.DMA((n_a,)), pltpu.SemaphoreType.DMA((n_a, 3)),
           pltpu.SemaphoreType.DMA((n_a, 3)), pltpu.SemaphoreType.DMA((n_a,)), pltpu.SemaphoreType.DMA((n_a,))],
        compiler_params=_cparams(),
    )(dg1, dg2, dgf, dgrn, dlng, dlnb, dbs, loss_parts, dws, dcv)


def _adam_math(w, g, m, v):
    nm = ADAM_B1 * m + (1.0 - ADAM_B1) * g
    nv = ADAM_B2 * v + (1.0 - ADAM_B2) * (g * g)
    d = -ADAM_LR * ((nm / (1.0 - ADAM_B1 ** ADAM_STEP)) / (jnp.sqrt(nv / (1.0 - ADAM_B2 ** ADAM_STEP)) + ADAM_EPS) + ADAM_WD * w)
    return d, nm, nv


def _adamw(name, w, g, m, v, rows):
    _, r, cdim = w.shape

    def body(w_ref, g_ref, m_ref, v_ref, go_ref, d_ref, nm_ref, nv_ref):
        gg = g_ref[...]
        go_ref[0] = gg
        d, nm, nv = _adam_math(w_ref[0], gg, m_ref[0], v_ref[0])
        d_ref[0], nm_ref[0], nv_ref[0] = d, nm, nv

    spec3 = pl.BlockSpec((1, rows, cdim), lambda i: (0, i, 0))
    sh = jax.ShapeDtypeStruct((1, r, cdim), F32)
    return pl.pallas_call(
        body, name=name, grid=(r // rows,), out_shape=(sh, sh, sh, sh),
        in_specs=[spec3, pl.BlockSpec((rows, cdim), lambda i: (i, 0)), spec3, spec3], out_specs=(spec3,) * 4,
        compiler_params=_cparams(("parallel",)),
    )(w, g, m, v)


def _adamw_small(rp, rws, rcv, params):
    n_p = len(params)

    def body(*refs):
        rp_ref, rws_ref, rcv_ref = refs[:3]
        ins = refs[3:3 + 3 * n_p]
        outs = refs[3 + 3 * n_p:]
        me = 4 * lax.axis_index("x") + 2 * lax.axis_index("y") + lax.axis_index("c")
        grads = [rp_ref[0, 0:1, :], rp_ref[0, 1:2, :], rp_ref[0, 2:3, :], rp_ref[0, 3:4, 0:RET_W],
                 rp_ref[1, 0:HEADS, 0:128], rp_ref[1, 0:HEADS, 128:256], rp_ref[1, 0:HEADS, 256:384],
                 rws_ref[...], rcv_ref[me][0:3, :], None]
        for p in range(n_p):
            w_ref, m_ref, v_ref = ins[3 * p:3 * p + 3]
            o = outs[4 * p:4 * p + 4]
            if p == n_p - 1:
                for j in range(N_DEV):
                    g = rcv_ref[j, 3:4, :]
                    res = (g,) + _adam_math(w_ref[j:j + 1, :], g, m_ref[j:j + 1, :], v_ref[j:j + 1, :])
                    for t in range(4):
                        o[t][j:j + 1, :] = res[t]
                continue
            lead = w_ref.ndim > grads[p].ndim
            rd = (lambda r: r[0]) if lead else (lambda r: r[...])
            res = (grads[p],) + _adam_math(rd(w_ref), grads[p], rd(m_ref), rd(v_ref))
            for t in range(4):
                if lead:
                    o[t][0] = res[t]
                else:
                    o[t][...] = res[t]

    vm = pl.BlockSpec(memory_space=pltpu.VMEM)
    flat = [a for tr in params for a in tr]
    out_shape = tuple(jax.ShapeDtypeStruct(tr[0].shape, F32) for tr in params for _ in range(4))
    res = pl.pallas_call(
        body, name="adamw_small", out_shape=out_shape, in_specs=[vm] * (3 + len(flat)), out_specs=(vm,) * len(out_shape),
        compiler_params=_cparams(),
    )(rp, rws, rcv, *flat)
    return [res[4 * p:4 * p + 4] for p in range(n_p)]


def kernel(x, mix_norm_g, w_in, ret_norm_g, sgu_ln_g, sgu_ln_b, sgu_w_s, sgu_b_s, w_out, ffn_norm_g, w_up, conv_w, conv_b, w_down, final_norm_g, loss_target, m_mix_norm_g, m_w_in, m_ret_norm_g, m_sgu_ln_g, m_sgu_ln_b, m_sgu_w_s, m_sgu_b_s, m_w_out, m_ffn_norm_g, m_w_up, m_conv_w, m_conv_b, m_w_down, m_final_norm_g, v_mix_norm_g, v_w_in, v_ret_norm_g, v_sgu_ln_g, v_sgu_ln_b, v_sgu_w_s, v_sgu_b_s, v_w_out, v_ffn_norm_g, v_w_up, v_conv_w, v_conv_b, v_w_down, v_final_norm_g):
    xs = x[0]
    tgt = loss_target[0]
    cos2, sin2 = _rope_tables()
    mask, qdec, kdec = _decay_tables()
    grn = ret_norm_g.reshape(1, RET_W)
    lng = sgu_ln_g.reshape(1, SGU_W)
    lnb = sgu_ln_b.reshape(1, SGU_W)
    ws = sgu_w_s[0]
    bsb = jnp.broadcast_to(sgu_b_s[0][:, :, None], (HEADS, CHUNK, HEAD_DIM))
    gf = final_norm_g.reshape(1, D_MODEL)
    cb_g = conv_b.reshape(N_DEV, 1, FF_SHARD)

    win_g, cw_g, so, su, sd = _ag_first(w_in[0], w_out[0], w_up[0], w_down[0], conv_w[0])

    proj, h1, wout_g, wdn_g = _fwd_proj(xs, mix_norm_g, win_g, cos2, sin2, so, sd)
    x2, mixcat, o, sprev, wup_g = _fwd_mix(xs, proj, wout_g, grn, lng, lnb, ws, bsb, mask, qdec, kdec, su)
    h2, up_pre, u_conv, act, x3, loss_parts = _fwd_ffn(x2, ffn_norm_g, wup_g, cw_g, cb_g, wdn_g, gf, tgt)

    dx3, dpre, dx2, dgf, dg2, dcv = _bwd_ffn(x3, tgt, gf, x2, ffn_norm_g, up_pre, u_conv, wup_g, cw_g, wdn_g)
    gdn_p = _wgrad("wgrad_down", act, dx3, N_FF_PAIR, 0).reshape(D_FF, D_MODEL)
    gup_p = _wgrad("wgrad_up", h2, dpre, 0, N_DEV)
    dproj, dgrn, dlng, dlnb, dws, dbs = _bwd_mix(dx2, proj, o, sprev, wout_g, grn, lng, lnb, ws, bsb, mask, qdec, kdec, cos2, sin2)
    gout_p = _wgrad("wgrad_out", mixcat, dx2, 0, 0, tn=512)[0]
    grad_x, dg1 = _bwd_proj(dproj, win_g, xs, mix_norm_g, dx2)
    gin_p = _wgrad("wgrad_in", h1, dproj, 0, 0, tn=768)[0]

    g_in, g_out, g_up, g_dn = _reduce_scatter_grads(gin_p, gout_p, gup_p, gdn_p)
    rp, rws, rcv = _all_reduce_small(dg1, dg2, dgf, dgrn, dlng, dlnb, dbs, loss_parts, dws, dcv)
    loss = rp[0, 3, RET_W]

    table = {}
    for name, w, g, m, v, rows in (("w_in", w_in, g_in, m_w_in, v_w_in, 256), ("w_out", w_out, g_out, m_w_out, v_w_out, 128),
                                   ("w_up", w_up, g_up, m_w_up, v_w_up, 256), ("w_down", w_down, g_dn, m_w_down, v_w_down, 88)):
        table[name] = _adamw("adamw_" + name, w, g, m, v, rows)
    row = lambda a: a.reshape(1, D_MODEL)
    slab = lambda a: a.reshape(N_DEV, FF_SHARD)
    names_small = ["mix_norm_g", "ffn_norm_g", "final_norm_g", "ret_norm_g", "sgu_ln_g", "sgu_ln_b", "sgu_b_s", "sgu_w_s",
                   "conv_w", "conv_b"]
    params = [(mix_norm_g, m_mix_norm_g, v_mix_norm_g), (ffn_norm_g, m_ffn_norm_g, v_ffn_norm_g),
              (row(final_norm_g), row(m_final_norm_g), row(v_final_norm_g)), (ret_norm_g, m_ret_norm_g, v_ret_norm_g),
              (sgu_ln_g, m_sgu_ln_g, v_sgu_ln_g), (sgu_ln_b, m_sgu_ln_b, v_sgu_ln_b), (sgu_b_s, m_sgu_b_s, v_sgu_b_s),
              (sgu_w_s, m_sgu_w_s, v_sgu_w_s), (conv_w, m_conv_w, v_conv_w), (slab(conv_b), slab(m_conv_b), slab(v_conv_b))]
    for n, res in zip(names_small, _adamw_small(rp, rws, rcv, params)):
        table[n] = res
    table["final_norm_g"] = tuple(a.reshape(D_MODEL) for a in table["final_norm_g"])
    table["conv_b"] = tuple(a.reshape(1, 2 * D_FF) for a in table["conv_b"])

    order = ["mix_norm_g", "w_in", "ret_norm_g", "sgu_ln_g", "sgu_ln_b", "sgu_w_s", "sgu_b_s", "w_out", "ffn_norm_g", "w_up",
             "conv_w", "conv_b", "w_down", "final_norm_g"]
    outs = [loss, grad_x[None]]
    for col in range(4):
        outs += [table[n][col] for n in order]
    return tuple(outs)
```

```python
import functools
import math

import jax
import jax.numpy as jnp
import numpy as np
from jax import lax
from jax.experimental import pallas as pl
from jax.experimental.pallas import tpu as pltpu

F32 = jnp.float32
BF16 = jnp.bfloat16
MESH = pl.DeviceIdType.MESH

N_DEV = 8
SEQ = 2048
D_MODEL = 1024
CHUNK = 128
N_CHUNK = SEQ // CHUNK
HEADS = 4
HEAD_DIM = 128
RET_W = 512
SGU_W = 512
PROJ_W = 3072
D_FF = 2816
FF_SHARD = 704
N_FF_PAIR = 4
IN_SHARD = PROJ_W // N_DEV
OUT_SHARD = D_MODEL // N_DEV
DOWN_SHARD = D_FF // N_DEV
TM = 256
N_TB = SEQ // TM
EPS = 1e-6
ROPE_BASE = 10000.0
K_SCALE = HEAD_DIM ** -0.5
INV_SQRT2 = 0.7071067811865476
INV_SQRT_2PI = 0.3989422804014327

ADAM_LR = 0.001
ADAM_B1 = 0.9
ADAM_B2 = 0.999
ADAM_EPS = 1e-08
ADAM_WD = 0.01
ADAM_STEP = 10

VMEM_LIMIT = 56 * 1024 * 1024


def _cparams(sem=None, vmem=VMEM_LIMIT):
    return pltpu.CompilerParams(dimension_semantics=sem, vmem_limit_bytes=vmem)


def _resident(shape):
    nd = len(shape)
    return pl.BlockSpec(shape, lambda *_: (0,) * nd, pipeline_mode=pl.Buffered(1))


def _dot(a, b):
    return jnp.dot(a, b, preferred_element_type=F32)


def _dot_nt(a, b):
    return lax.dot_general(a, b, (((1,), (1,)), ((), ())), preferred_element_type=F32)


def _dot_tn(a, b):
    return lax.dot_general(a, b, (((0,), (0,)), ((), ())), preferred_element_type=F32)


def _sigmoid(x):
    return 1.0 / (1.0 + jnp.exp(-x))


def _gelu(x):
    return 0.5 * x * (1.0 + lax.erf(x * INV_SQRT2))


def _gelu_grad(x):
    return 0.5 * (1.0 + lax.erf(x * INV_SQRT2)) + x * (jnp.exp(-0.5 * x * x) * INV_SQRT_2PI)


def _rot(xh, cos2, sin2):
    return xh * cos2 + pltpu.roll(xh, HEAD_DIM // 2, 1) * sin2


def _rot_t(dh, cos2, sin2):
    return dh * cos2 + pltpu.roll(dh * sin2, HEAD_DIM // 2, 1)


def _rope_tables():
    half = HEAD_DIM // 2
    inv_freq = jnp.power(ROPE_BASE, -jnp.arange(half, dtype=F32) / half)
    ang = jnp.arange(SEQ, dtype=F32)[:, None] * inv_freq[None, :]
    cos, sin = jnp.cos(ang), jnp.sin(ang)
    cos2 = jnp.tile(jnp.concatenate([cos, cos], axis=-1), (1, HEADS))
    sin2 = jnp.tile(jnp.concatenate([-sin, sin], axis=-1), (1, HEADS))
    return cos2, sin2


def _decay_tables():
    log_gamma = jnp.log(1.0 - jnp.power(2.0, -5.0 - jnp.arange(HEADS, dtype=F32)))
    pos = jnp.arange(CHUNK, dtype=F32)
    diff = pos[:, None] - pos[None, :]
    mask = jnp.where(diff >= 0.0, jnp.exp(log_gamma[:, None, None] * jnp.maximum(diff, 0.0)[None]), 0.0)
    k_decay = jnp.exp(log_gamma[:, None] * (CHUNK - 1.0 - pos)[None])
    q_decay = jnp.exp(log_gamma[:, None] * (pos + 1.0)[None])
    kd = jnp.broadcast_to(k_decay[:, :, None], (HEADS, CHUNK, HEAD_DIM))
    qd = jnp.broadcast_to(q_decay[:, :, None], (HEADS, CHUNK, HEAD_DIM))
    return mask.astype(F32), qd.astype(F32), kd.astype(F32)


def _chunk_decay():
    lg = np.log(np.float32(1.0) - np.power(np.float32(2.0), -5.0 - np.arange(HEADS, dtype=np.float32))).astype(np.float32)
    return [float(np.exp(lg[h] * np.float32(CHUNK))) for h in range(HEADS)]


W_IN, W_OUT, W_UP, W_DOWN, W_CONV = range(5)
GATHERED = {W_IN: ((D_MODEL, PROJ_W), BF16), W_OUT: ((D_MODEL, D_MODEL), BF16), W_UP: ((N_DEV, D_MODEL, FF_SHARD), BF16),
            W_DOWN: ((D_FF, D_MODEL), BF16), W_CONV: ((N_DEV, 8, FF_SHARD), F32)}
SHARD = {W_IN: (D_MODEL, IN_SHARD), W_OUT: (OUT_SHARD, D_MODEL), W_UP: (D_MODEL, FF_SHARD), W_DOWN: (DOWN_SHARD, D_MODEL),
         W_CONV: (8, FF_SHARD)}


class _Gather:
    def __init__(self, ids, stages, gathered, send_sems, recv_sems, local_sems):
        self.ids, self.stages, self.gathered = ids, stages, gathered
        self.send_sems, self.recv_sems, self.local_sems = send_sems, recv_sems, local_sems
        self.x, self.y, self.c = lax.axis_index("x"), lax.axis_index("y"), lax.axis_index("c")
        self.me = (self.x, self.y, self.c)
        self.sibling = (self.x, self.y, 1 - self.c)
        self.chips = [(1 - self.x, self.y), (self.x, 1 - self.y), (1 - self.x, 1 - self.y)]

    def slot(self, n, px, py, pc):
        dev = 4 * px + 2 * py + pc
        w, g = self.ids[n], self.gathered[n]
        if w == W_IN:
            return g.at[:, pl.ds(pl.multiple_of(dev * IN_SHARD, 128), IN_SHARD)]
        if w == W_OUT:
            return g.at[pl.ds(pl.multiple_of(dev * OUT_SHARD, 128), OUT_SHARD), :]
        if w == W_DOWN:
            return g.at[pl.ds(pl.multiple_of(dev * DOWN_SHARD, 32), DOWN_SHARD), :]
        return g.at[dev]

    def copy(self, n, k, block, to, src=None):
        return pltpu.make_async_remote_copy(
            src_ref=self.slot(n, *block) if src is None else src, dst_ref=self.slot(n, *block),
            send_sem=self.send_sems.at[n, k], recv_sem=self.recv_sems.at[n, k], device_id=to, device_id_type=MESH)

    def _mine(self):
        return [pltpu.make_async_copy(self.stages[n], self.slot(n, *self.me), self.local_sems.at[n]) for n in range(len(self.ids))]

    def _first(self):
        out = []
        for n in range(len(self.ids)):
            out.append(self.copy(n, 0, self.me, self.sibling, src=self.stages[n]))
            out += [self.copy(n, 1 + j, self.me, (*chip, self.c), src=self.stages[n]) for j, chip in enumerate(self.chips)]
        return out

    def start(self):
        for cp in self._mine() + self._first():
            cp.start()

    def finish(self):
        passed = []
        for n in range(len(self.ids)):
            for j, chip in enumerate(self.chips):
                self.copy(n, 1 + j, (*chip, self.c), self.me).wait_recv()
                fwd = self.copy(n, 4 + j, (*chip, self.c), self.sibling)
                fwd.start()
                passed.append(fwd)
        for n in range(len(self.ids)):
            self.copy(n, 0, self.sibling, self.me).wait_recv()
            for j, chip in enumerate(self.chips):
                self.copy(n, 4 + j, (*chip, 1 - self.c), self.me).wait_recv()
        for cp in self._first() + passed:
            cp.wait_send()
        for cp in self._mine():
            cp.wait()


def _gather_scratch(n):
    return [pltpu.SemaphoreType.DMA((n, 7)), pltpu.SemaphoreType.DMA((n, 7)), pltpu.SemaphoreType.DMA((n,))]


def _gathered_shapes(ids):
    return tuple(jax.ShapeDtypeStruct(*GATHERED[w]) for w in ids)


def _ag_first(w_in, w_out, w_up, w_down, conv_w):
    ids = [W_IN, W_CONV]

    def body(in_ref, out_ref, up_ref, dn_ref, cw_ref, gin, gcw, so_ref, su_ref, sd_ref, s_in, s_cw, send_sems, recv_sems, local_sems):
        s_in[...] = in_ref[...].astype(BF16)
        s_cw[...] = jnp.zeros_like(s_cw)
        s_cw[0:3, :] = cw_ref[...]
        ag = _Gather(ids, [s_in, s_cw], [gin, gcw], send_sems, recv_sems, local_sems)
        ag.start()
        so_ref[...] = out_ref[...].astype(BF16)
        su_ref[...] = up_ref[...].astype(BF16)
        sd_ref[...] = dn_ref[...].astype(BF16)
        ag.finish()

    vm = pl.BlockSpec(memory_space=pltpu.VMEM)
    hbm = pl.BlockSpec(memory_space=pl.ANY)
    return pl.pallas_call(
        body, name="ag_first",
        out_shape=_gathered_shapes(ids) + tuple(jax.ShapeDtypeStruct(SHARD[w], BF16) for w in (W_OUT, W_UP, W_DOWN)),
        in_specs=[vm] * 5, out_specs=(hbm, hbm, vm, vm, vm),
        scratch_shapes=[pltpu.VMEM(SHARD[W_IN], BF16), pltpu.VMEM(SHARD[W_CONV], F32)] + _gather_scratch(len(ids)),
        compiler_params=_cparams(),
    )(w_in, w_out, w_up, w_down, conv_w)


def _fwd_proj(x, g1, win_g, cos2, sin2, so, sd):
    ids = [W_OUT, W_DOWN]

    def body(x_ref, g_ref, w_ref, cos_ref, sin_ref, so_ref, sd_ref, proj_ref, h1_ref, gout, gdn, send_sems, recv_sems, local_sems):
        ag = _Gather(ids, [so_ref, sd_ref], [gout, gdn], send_sems, recv_sems, local_sems)

        @pl.when(pl.program_id(0) == 0)
        def _():
            ag.start()

        xb = x_ref[...]
        r = lax.rsqrt(jnp.mean(xb * xb, axis=-1, keepdims=True) + EPS)
        h = ((xb * r) * g_ref[...]).astype(BF16)
        h1_ref[...] = h
        p = _dot(h, w_ref[...])
        for hd in range(HEADS):
            sl = slice(hd * HEAD_DIM, (hd + 1) * HEAD_DIM)
            c2, s2 = cos_ref[:, sl], sin_ref[:, sl]
            proj_ref[:, sl] = _rot(p[:, sl], c2, s2)
            ks = slice(RET_W + hd * HEAD_DIM, RET_W + (hd + 1) * HEAD_DIM)
            proj_ref[:, ks] = _rot(p[:, ks], c2, s2) * K_SCALE
        proj_ref[:, 2 * RET_W:] = p[:, 2 * RET_W:]

        @pl.when(pl.program_id(0) == N_TB - 1)
        def _():
            ag.finish()

    tok = lambda w: pl.BlockSpec((TM, w), lambda i: (i, 0))
    hbm = pl.BlockSpec(memory_space=pl.ANY)
    return pl.pallas_call(
        body, name="fwd_proj", grid=(N_TB,),
        out_shape=(jax.ShapeDtypeStruct((SEQ, PROJ_W), F32), jax.ShapeDtypeStruct((SEQ, D_MODEL), BF16)) + _gathered_shapes(ids),
        in_specs=[tok(D_MODEL), _resident((1, D_MODEL)), _resident((D_MODEL, PROJ_W)), tok(RET_W), tok(RET_W), hbm, hbm],
        out_specs=(tok(PROJ_W), tok(D_MODEL), hbm, hbm),
        scratch_shapes=_gather_scratch(len(ids)),
        compiler_params=_cparams(("arbitrary",)),
    )(x, g1, win_g, cos2, sin2, so, sd)


def _causal(w):
    r = lax.broadcasted_iota(jnp.int32, (CHUNK, CHUNK), 0)
    c = lax.broadcasted_iota(jnp.int32, (CHUNK, CHUNK), 1)
    return jnp.where(r >= c, w, 0.0)


def _fwd_mix(x, proj, wout_g, grn, lng, lnb, ws, bsb, mask, qdec, kdec, su):
    cdec = _chunk_decay()
    ids = [W_UP]

    def body(x_ref, p_ref, w_ref, grn_ref, lng_ref, lnb_ref, ws_ref, bsb_ref, m_ref, qd_ref, kd_ref, su_ref,
             x2_ref, cat_ref, o_ref, sp_ref, gup, state, send_sems, recv_sems, local_sems):
        ag = _Gather(ids, [su_ref], [gup], send_sems, recv_sems, local_sems)

        @pl.when(pl.program_id(0) == 0)
        def _():
            state[...] = jnp.zeros_like(state)
            ag.start()

        for h in range(HEADS):
            sl = slice(h * HEAD_DIM, (h + 1) * HEAD_DIM)
            q = p_ref[:, sl]
            k = p_ref[:, RET_W + h * HEAD_DIM:RET_W + (h + 1) * HEAD_DIM]
            v = p_ref[:, 2 * RET_W + h * HEAD_DIM:2 * RET_W + (h + 1) * HEAD_DIM]
            g = p_ref[:, 3 * RET_W + h * HEAD_DIM:3 * RET_W + (h + 1) * HEAD_DIM]
            qb, kb, vb = q.astype(BF16), k.astype(BF16), v.astype(BF16)
            a = _dot_nt(qb, kb) * m_ref[h]
            spb = state[h].astype(BF16)
            sp_ref[0, h] = spb
            o = _dot(a.astype(BF16), vb) + _dot((q * qd_ref[h]).astype(BF16), spb)
            state[h] = state[h] * cdec[h] + _dot_tn((k * kd_ref[h]).astype(BF16), vb)
            o_ref[:, sl] = o
            rinv = lax.rsqrt(jnp.mean(o * o, axis=-1, keepdims=True) + EPS)
            rn = (o * rinv) * grn_ref[:, sl]
            cat_ref[:, sl] = ((g * _sigmoid(g)) * rn).astype(BF16)
        for gi in range(HEADS):
            sl = slice(gi * HEAD_DIM, (gi + 1) * HEAD_DIM)
            u = p_ref[:, 4 * RET_W + gi * HEAD_DIM:4 * RET_W + (gi + 1) * HEAD_DIM]
            sv = p_ref[:, 4 * RET_W + SGU_W + gi * HEAD_DIM:4 * RET_W + SGU_W + (gi + 1) * HEAD_DIM]
            gv = _gelu(sv)
            xc = gv - jnp.mean(gv, axis=-1, keepdims=True)
            vn = (xc * lax.rsqrt(jnp.mean(xc * xc, axis=-1, keepdims=True) + EPS)) * lng_ref[:, sl] + lnb_ref[:, sl]
            mixed = _dot(_causal(ws_ref[gi]).astype(BF16), vn.astype(BF16)) + bsb_ref[gi]
            cat_ref[:, RET_W + gi * HEAD_DIM:RET_W + (gi + 1) * HEAD_DIM] = (_gelu(u) * mixed).astype(BF16)
        x2_ref[...] = x_ref[...] + _dot(cat_ref[...], w_ref[...])

        @pl.when(pl.program_id(0) == N_CHUNK - 1)
        def _():
            ag.finish()

    ch = lambda w: pl.BlockSpec((CHUNK, w), lambda i: (i, 0))
    hcc = (HEADS, CHUNK, CHUNK)
    hbm = pl.BlockSpec(memory_space=pl.ANY)
    return pl.pallas_call(
        body, name="fwd_mix", grid=(N_CHUNK,),
        out_shape=(jax.ShapeDtypeStruct((SEQ, D_MODEL), F32), jax.ShapeDtypeStruct((SEQ, D_MODEL), BF16),
                   jax.ShapeDtypeStruct((SEQ, RET_W), F32), jax.ShapeDtypeStruct((N_CHUNK, HEADS, HEAD_DIM, HEAD_DIM), BF16))
        + _gathered_shapes(ids),
        in_specs=[ch(D_MODEL), ch(PROJ_W), _resident((D_MODEL, D_MODEL)), _resident((1, RET_W)), _resident((1, SGU_W)),
                  _resident((1, SGU_W)), _resident(hcc), _resident(hcc), _resident(hcc), _resident(hcc), _resident(hcc), hbm],
        out_specs=(ch(D_MODEL), ch(D_MODEL), ch(RET_W), pl.BlockSpec((1, HEADS, HEAD_DIM, HEAD_DIM), lambda i: (i, 0, 0, 0)), hbm),
        scratch_shapes=[pltpu.VMEM((HEADS, HEAD_DIM, HEAD_DIM), F32)] + _gather_scratch(len(ids)),
        compiler_params=_cparams(("arbitrary",)),
    )(x, proj, wout_g, grn, lng, lnb, ws, bsb, mask, qdec, kdec, su)


def _conv_taps(p, prev8):
    row = lax.broadcasted_iota(jnp.int32, p.shape, 0)
    p1 = jnp.where(row == 0, prev8[7:8, :], pltpu.roll(p, 1, 0))
    p2 = jnp.where(row == 0, prev8[6:7, :], jnp.where(row == 1, prev8[7:8, :], pltpu.roll(p, 2, 0)))
    return p1, p2


def _fwd_ffn(x2, g2, wup_g, cw_g, cb_g, wdn_g, gf, tgt):
    def body(x_ref, g_ref, wu_ref, cw_ref, cb_ref, wd_ref, gf_ref, t_ref, h2_ref, up_ref, u_ref, act_ref, x3_ref, loss_ref, carry):
        @pl.when(pl.program_id(0) == 0)
        def _():
            carry[...] = jnp.zeros_like(carry)

        xb = x_ref[...]
        r = lax.rsqrt(jnp.mean(xb * xb, axis=-1, keepdims=True) + EPS)
        h = ((xb * r) * g_ref[...]).astype(BF16)
        h2_ref[...] = h
        acc = xb
        for j in range(N_FF_PAIR):
            u = []
            for s in (j, j + N_FF_PAIR):
                p = _dot(h, wu_ref[s])
                up_ref[s] = p.astype(BF16)
                p1, p2 = _conv_taps(p, carry[s])
                carry[s] = p[TM - 8:, :]
                cw = cw_ref[s]
                us = p2 * cw[0:1, :] + p1 * cw[1:2, :] + p * cw[2:3, :] + cb_ref[s]
                u_ref[s] = us.astype(BF16)
                u.append(us)
            a = ((u[0] * _sigmoid(u[0])) * u[1]).astype(BF16)
            act_ref[j] = a
            acc = acc + _dot(a, wd_ref[pl.ds(j * FF_SHARD, FF_SHARD), :])
        x3_ref[...] = acc
        r3 = lax.rsqrt(jnp.mean(acc * acc, axis=-1, keepdims=True) + EPS)
        diff = (acc * r3) * gf_ref[...] - t_ref[...]
        loss_ref[...] = jnp.full(loss_ref.shape, 0.5 * jnp.sum(jnp.mean(diff * diff, axis=-1)), F32)

    tok = lambda w: pl.BlockSpec((TM, w), lambda i: (i, 0))
    return pl.pallas_call(
        body, name="fwd_ffn", grid=(N_TB,),
        out_shape=(jax.ShapeDtypeStruct((SEQ, D_MODEL), BF16), jax.ShapeDtypeStruct((N_DEV, SEQ, FF_SHARD), BF16),
                   jax.ShapeDtypeStruct((N_DEV, SEQ, FF_SHARD), BF16),
                   jax.ShapeDtypeStruct((N_FF_PAIR, SEQ, FF_SHARD), BF16), jax.ShapeDtypeStruct((SEQ, D_MODEL), F32),
                   jax.ShapeDtypeStruct((N_TB, 8, 128), F32)),
        in_specs=[tok(D_MODEL), _resident((1, D_MODEL)), _resident((N_DEV, D_MODEL, FF_SHARD)), _resident((N_DEV, 8, FF_SHARD)),
                  _resident((N_DEV, 1, FF_SHARD)), _resident((D_FF, D_MODEL)), _resident((1, D_MODEL)), tok(D_MODEL)],
        out_specs=(tok(D_MODEL), pl.BlockSpec((N_DEV, TM, FF_SHARD), lambda i: (0, i, 0)),
                   pl.BlockSpec((N_DEV, TM, FF_SHARD), lambda i: (0, i, 0)),
                   pl.BlockSpec((N_FF_PAIR, TM, FF_SHARD), lambda i: (0, i, 0)), tok(D_MODEL),
                   pl.BlockSpec((1, 8, 128), lambda i: (i, 0, 0))),
        scratch_shapes=[pltpu.VMEM((N_DEV, 8, FF_SHARD), F32)],
        compiler_params=_cparams(("arbitrary",)),
    )(x2, g2, wup_g, cw_g, cb_g, wdn_g, gf, tgt)


def _bwd_ffn(x3, tgt, gf, x2, g2, up_pre, u_conv, wup_g, cw_g, wdn_g):
    def body(x3_ref, t_ref, gf_ref, x2_ref, g2_ref, up_ref, u_ref, wu_ref, cw_ref, wd_ref,
             dx3_ref, dpre_ref, dx2_ref, dgf_ref, dg2_ref, dcv_ref, nxt):
        i = pl.program_id(0)

        @pl.when(i == 0)
        def _():
            nxt[...] = jnp.zeros_like(nxt)
            dgf_ref[...] = jnp.zeros_like(dgf_ref)
            dg2_ref[...] = jnp.zeros_like(dg2_ref)
            dcv_ref[...] = jnp.zeros_like(dcv_ref)

        x3 = x3_ref[...]
        r3 = lax.rsqrt(jnp.mean(x3 * x3, axis=-1, keepdims=True) + EPS)
        xh3 = x3 * r3
        dy = (xh3 * gf_ref[...] - t_ref[...]) * (1.0 / D_MODEL)
        dgf_ref[0:1, :] += jnp.sum(dy * xh3, axis=0, keepdims=True)
        t3 = dy * gf_ref[...]
        dx3 = r3 * (t3 - xh3 * jnp.mean(t3 * xh3, axis=-1, keepdims=True))
        dx3b = dx3.astype(BF16)
        dx3_ref[...] = dx3b
        dh2 = jnp.zeros((TM, D_MODEL), F32)
        row = lax.broadcasted_iota(jnp.int32, (TM, FF_SHARD), 0)
        for j in range(N_FF_PAIR):
            dact = _dot_nt(dx3b, wd_ref[pl.ds(j * FF_SHARD, FF_SHARD), :])
            ua = u_ref[j].astype(F32)
            ub = u_ref[j + N_FF_PAIR].astype(F32)
            sg = _sigmoid(ua)
            du = [dact * ub * (sg * (1.0 + ua * (1.0 - sg))), dact * (ua * sg)]
            for n, s in enumerate((j, j + N_FF_PAIR)):
                d = du[n]
                cw = cw_ref[s]
                nx = nxt[s]
                n1 = jnp.where(row == TM - 1, nx[0:1, :], pltpu.roll(d, TM - 1, 0))
                n2 = jnp.where(row == TM - 2, nx[0:1, :], jnp.where(row == TM - 1, nx[1:2, :], pltpu.roll(d, TM - 2, 0)))
                nxt[s] = d[0:8, :]
                dp = (d * cw[2:3, :] + n1 * cw[1:2, :] + n2 * cw[0:1, :]).astype(BF16)
                dpre_ref[s] = dp
                p = up_ref[s].astype(F32)
                dcv_ref[s, 0:1, :] += jnp.sum(n2 * p, axis=0, keepdims=True)
                dcv_ref[s, 1:2, :] += jnp.sum(n1 * p, axis=0, keepdims=True)
                dcv_ref[s, 2:3, :] += jnp.sum(d * p, axis=0, keepdims=True)
                dcv_ref[s, 3:4, :] += jnp.sum(d, axis=0, keepdims=True)
                dh2 = dh2 + _dot_nt(dp, wu_ref[s])
        x2 = x2_ref[...]
        r2 = lax.rsqrt(jnp.mean(x2 * x2, axis=-1, keepdims=True) + EPS)
        xh2 = x2 * r2
        dg2_ref[0:1, :] += jnp.sum(dh2 * xh2, axis=0, keepdims=True)
        t2 = dh2 * g2_ref[...]
        dx2_ref[...] = dx3 + r2 * (t2 - xh2 * jnp.mean(t2 * xh2, axis=-1, keepdims=True))

    rev = lambda w: pl.BlockSpec((TM, w), lambda i: (N_TB - 1 - i, 0))
    rev3 = lambda: pl.BlockSpec((N_DEV, TM, FF_SHARD), lambda i: (0, N_TB - 1 - i, 0))
    acc = lambda s: pl.BlockSpec(s, lambda i: (0,) * len(s))
    return pl.pallas_call(
        body, name="bwd_ffn", grid=(N_TB,),
        out_shape=(jax.ShapeDtypeStruct((SEQ, D_MODEL), BF16), jax.ShapeDtypeStruct((N_DEV, SEQ, FF_SHARD), BF16),
                   jax.ShapeDtypeStruct((SEQ, D_MODEL), F32), jax.ShapeDtypeStruct((8, D_MODEL), F32),
                   jax.ShapeDtypeStruct((8, D_MODEL), F32), jax.ShapeDtypeStruct((N_DEV, 8, FF_SHARD), F32)),
        in_specs=[rev(D_MODEL), rev(D_MODEL), _resident((1, D_MODEL)), rev(D_MODEL), _resident((1, D_MODEL)), rev3(), rev3(),
                  _resident((N_DEV, D_MODEL, FF_SHARD)), _resident((N_DEV, 8, FF_SHARD)), _resident((D_FF, D_MODEL))],
        out_specs=(rev(D_MODEL), rev3(), rev(D_MODEL), acc((8, D_MODEL)), acc((8, D_MODEL)), acc((N_DEV, 8, FF_SHARD))),
        scratch_shapes=[pltpu.VMEM((N_DEV, 8, FF_SHARD), F32)],
        compiler_params=_cparams(("arbitrary",)),
    )(x3, tgt, gf, x2, g2, up_pre, u_conv, wup_g, cw_g, wdn_g)


def _bwd_mix(dx2, proj, o, sprev, wout_g, grn, lng, lnb, ws, bsb, mask, qdec, kdec, cos2, sin2, gup_p):
    cdec = _chunk_decay()

    def body(dx2_ref, p_ref, o_ref, sp_ref, w_ref, grn_ref, lng_ref, lnb_ref, ws_ref, bsb_ref, m_ref, qd_ref, kd_ref,
             cos_ref, sin_ref, gup_ref, dp_ref, dgrn_ref, dlng_ref, dlnb_ref, dws_ref, dbs_ref, rs_out, land1, dstate, dbs_acc,
             *rs_scratch):
        i = pl.program_id(0)
        rs = _Scatter(W_UP, gup_ref, rs_out, land1, *rs_scratch)
        pl.when(i == 0)(rs.phase1)
        pl.when(i == 4)(rs.phase2)

        @pl.when(i == 0)
        def _():
            dstate[...] = jnp.zeros_like(dstate)
            dgrn_ref[...] = jnp.zeros_like(dgrn_ref)
            dlng_ref[...] = jnp.zeros_like(dlng_ref)
            dlnb_ref[...] = jnp.zeros_like(dlnb_ref)
            dws_ref[...] = jnp.zeros_like(dws_ref)
            dbs_ref[...] = jnp.zeros_like(dbs_ref)
            dbs_acc[...] = jnp.zeros_like(dbs_acc)

        dmix = _dot_nt(dx2_ref[...].astype(BF16), w_ref[...])
        for h in range(HEADS):
            sl = slice(h * HEAD_DIM, (h + 1) * HEAD_DIM)
            q = p_ref[:, sl]
            k = p_ref[:, RET_W + h * HEAD_DIM:RET_W + (h + 1) * HEAD_DIM]
            v = p_ref[:, 2 * RET_W + h * HEAD_DIM:2 * RET_W + (h + 1) * HEAD_DIM]
            g = p_ref[:, 3 * RET_W + h * HEAD_DIM:3 * RET_W + (h + 1) * HEAD_DIM]
            o = o_ref[:, sl]
            rinv = lax.rsqrt(jnp.mean(o * o, axis=-1, keepdims=True) + EPS)
            oh = o * rinv
            gr = grn_ref[:, sl]
            sg = _sigmoid(g)
            dret = dmix[:, sl]
            dp_ref[:, 3 * RET_W + h * HEAD_DIM:3 * RET_W + (h + 1) * HEAD_DIM] = (
                dret * (oh * gr) * (sg * (1.0 + g * (1.0 - sg)))).astype(BF16)
            drn = dret * (g * sg)
            dgrn_ref[0:1, sl] += jnp.sum(drn * oh, axis=0, keepdims=True)
            t = drn * gr
            do = rinv * (t - oh * jnp.mean(t * oh, axis=-1, keepdims=True))
            qb, kb, vb, dob = q.astype(BF16), k.astype(BF16), v.astype(BF16), do.astype(BF16)
            m = m_ref[h]
            ab = (_dot_nt(qb, kb) * m).astype(BF16)
            dab = (_dot_nt(dob, vb) * m).astype(BF16)
            spb = sp_ref[0, h]
            dsn = dstate[h]
            dsnb = dsn.astype(BF16)
            qdb = (q * qd_ref[h]).astype(BF16)
            kdb = (k * kd_ref[h]).astype(BF16)
            dq = _dot(dab, kb) + _dot_nt(dob, spb) * qd_ref[h]
            dk = _dot_tn(dab, qb) + _dot_nt(vb, dsnb) * kd_ref[h]
            dv = _dot_tn(ab, dob) + _dot(kdb, dsnb)
            dstate[h] = dsn * cdec[h] + _dot_tn(qdb, dob)
            c2, s2 = cos_ref[:, sl], sin_ref[:, sl]
            dp_ref[:, sl] = _rot_t(dq, c2, s2).astype(BF16)
            dp_ref[:, RET_W + h * HEAD_DIM:RET_W + (h + 1) * HEAD_DIM] = _rot_t(dk * K_SCALE, c2, s2).astype(BF16)
            dp_ref[:, 2 * RET_W + h * HEAD_DIM:2 * RET_W + (h + 1) * HEAD_DIM] = dv.astype(BF16)
        for gi in range(HEADS):
            sl = slice(gi * HEAD_DIM, (gi + 1) * HEAD_DIM)
            u = p_ref[:, 4 * RET_W + gi * HEAD_DIM:4 * RET_W + (gi + 1) * HEAD_DIM]
            sv = p_ref[:, 4 * RET_W + SGU_W + gi * HEAD_DIM:4 * RET_W + SGU_W + (gi + 1) * HEAD_DIM]
            gv = _gelu(sv)
            xc = gv - jnp.mean(gv, axis=-1, keepdims=True)
            rstd = lax.rsqrt(jnp.mean(xc * xc, axis=-1, keepdims=True) + EPS)
            xh = xc * rstd
            lg = lng_ref[:, sl]
            vnb = (xh * lg + lnb_ref[:, sl]).astype(BF16)
            wcb = _causal(ws_ref[gi]).astype(BF16)
            mixed = _dot(wcb, vnb) + bsb_ref[gi]
            dsgu = dmix[:, RET_W + gi * HEAD_DIM:RET_W + (gi + 1) * HEAD_DIM]
            dmixed = dsgu * _gelu(u)
            dmb = dmixed.astype(BF16)
            dws_ref[gi] += _causal(_dot_nt(dmb, vnb))
            dbs_acc[gi] += dmixed
            dvn = _dot_tn(wcb, dmb)
            dlng_ref[gi:gi + 1, :] += jnp.sum(dvn * xh, axis=0, keepdims=True)
            dlnb_ref[gi:gi + 1, :] += jnp.sum(dvn, axis=0, keepdims=True)
            dxh = dvn * lg
            dgv = rstd * (dxh - jnp.mean(dxh, axis=-1, keepdims=True) - xh * jnp.mean(dxh * xh, axis=-1, keepdims=True))
            dp_ref[:, 4 * RET_W + gi * HEAD_DIM:4 * RET_W + (gi + 1) * HEAD_DIM] = (dsgu * mixed * _gelu_grad(u)).astype(BF16)
            dp_ref[:, 4 * RET_W + SGU_W + gi * HEAD_DIM:4 * RET_W + SGU_W + (gi + 1) * HEAD_DIM] = (
                dgv * _gelu_grad(sv)).astype(BF16)

        @pl.when(i == N_CHUNK - 1)
        def _():
            for gi in range(HEADS):
                col = jnp.broadcast_to(jnp.sum(dbs_acc[gi], axis=-1, keepdims=True), (CHUNK, CHUNK))
                dbs_ref[gi:gi + 1, :] = jnp.transpose(col)[0:1, :]
            rs.phase3()

    rev = lambda w: pl.BlockSpec((CHUNK, w), lambda i: (N_CHUNK - 1 - i, 0))
    hcc = (HEADS, CHUNK, CHUNK)
    acc = lambda s: pl.BlockSpec(s, lambda i: (0,) * len(s))
    res = pl.pallas_call(
        body, name="bwd_mix", grid=(N_CHUNK,),
        out_shape=(jax.ShapeDtypeStruct((SEQ, PROJ_W), BF16), jax.ShapeDtypeStruct((8, RET_W), F32),
                   jax.ShapeDtypeStruct((8, HEAD_DIM), F32), jax.ShapeDtypeStruct((8, HEAD_DIM), F32),
                   jax.ShapeDtypeStruct(hcc, F32), jax.ShapeDtypeStruct((8, CHUNK), F32)) + _scatter_out_shapes(W_UP),
        in_specs=[rev(D_MODEL), rev(PROJ_W), rev(RET_W),
                  pl.BlockSpec((1, HEADS, HEAD_DIM, HEAD_DIM), lambda i: (N_CHUNK - 1 - i, 0, 0, 0)),
                  _resident((D_MODEL, D_MODEL)), _resident((1, RET_W)), _resident((1, SGU_W)), _resident((1, SGU_W)),
                  _resident(hcc), _resident(hcc), _resident(hcc), _resident(hcc), _resident(hcc), rev(RET_W), rev(RET_W),
                  pl.BlockSpec(memory_space=pl.ANY)],
        out_specs=(rev(PROJ_W), acc((8, RET_W)), acc((8, HEAD_DIM)), acc((8, HEAD_DIM)), acc(hcc), acc((8, CHUNK)))
        + _scatter_out_specs(),
        scratch_shapes=[pltpu.VMEM((HEADS, HEAD_DIM, HEAD_DIM), F32), pltpu.VMEM((HEADS, CHUNK, CHUNK), F32)] + _scatter_scratch(W_UP),
        compiler_params=_cparams(("arbitrary",)),
    )(dx2, proj, o, sprev, wout_g, grn, lng, lnb, ws, bsb, mask, qdec, kdec, cos2, sin2, gup_p)
    return res[:7]


def _bwd_proj(dproj, win_g, x, g1, dx2):
    def body(dp_ref, w_ref, x_ref, g_ref, dx2_ref, dx_ref, dg_ref):
        @pl.when(pl.program_id(0) == 0)
        def _():
            dg_ref[...] = jnp.zeros_like(dg_ref)

        dh = _dot_nt(dp_ref[...], w_ref[...])
        xb = x_ref[...]
        r = lax.rsqrt(jnp.mean(xb * xb, axis=-1, keepdims=True) + EPS)
        xh = xb * r
        dg_ref[0:1, :] += jnp.sum(dh * xh, axis=0, keepdims=True)
        t = dh * g_ref[...]
        dx_ref[...] = dx2_ref[...] + r * (t - xh * jnp.mean(t * xh, axis=-1, keepdims=True))

    tok = lambda w: pl.BlockSpec((TM, w), lambda i: (i, 0))
    return pl.pallas_call(
        body, name="bwd_proj", grid=(N_TB,),
        out_shape=(jax.ShapeDtypeStruct((SEQ, D_MODEL), F32), jax.ShapeDtypeStruct((8, D_MODEL), F32)),
        in_specs=[tok(PROJ_W), _resident((D_MODEL, PROJ_W)), tok(D_MODEL), _resident((1, D_MODEL)), tok(D_MODEL)],
        out_specs=(tok(D_MODEL), pl.BlockSpec((8, D_MODEL), lambda i: (0, 0))),
        compiler_params=_cparams(("arbitrary",)),
    )(dproj, win_g, x, g1, dx2)


def _wgrad(name, a, b, n_a, n_b, tn=None, host=None):
    n = max(n_a, n_b, 1)
    m_w, n_w = a.shape[-1], b.shape[-1]
    tn = n_w if tn is None else tn
    n_steps = n * (n_w // tn)

    def body(a_ref, b_ref, *rest):
        if host is None:
            (o_ref,) = rest
        else:
            p_ref, o_ref, rs_out, land1 = rest[:4]
            rs = _Scatter(host[0], p_ref, rs_out, land1, *rest[4:])
            step = pl.program_id(0) * (n_w // tn) + pl.program_id(1)
            pl.when(step == 0)(rs.phase1)
            pl.when(step == 1)(rs.phase2)
        av = a_ref[0] if n_a else a_ref[...]
        bv = b_ref[0] if n_b else b_ref[...]
        o_ref[0] = _dot_tn(av.astype(BF16), bv.astype(BF16)).astype(BF16)
        if host is not None:
            pl.when(step == n_steps - 1)(rs.phase3)

    a_spec = pl.BlockSpec((1, SEQ, m_w), lambda j, t: (j, 0, 0)) if n_a else pl.BlockSpec((SEQ, m_w), lambda j, t: (0, 0))
    b_spec = pl.BlockSpec((1, SEQ, tn), lambda j, t: (j, 0, t)) if n_b else pl.BlockSpec((SEQ, tn), lambda j, t: (0, t))
    o_shape = jax.ShapeDtypeStruct((n, m_w, n_w), BF16)
    o_spec = pl.BlockSpec((1, m_w, tn), lambda j, t: (j, 0, t))
    if host is None:
        return pl.pallas_call(
            body, name=name, grid=(n, n_w // tn), out_shape=o_shape, in_specs=[a_spec, b_spec], out_specs=o_spec,
            compiler_params=_cparams(("parallel", "parallel")),
        )(a, b)
    assert n_steps >= 3
    res = pl.pallas_call(
        body, name=name, grid=(n, n_w // tn), out_shape=(o_shape,) + _scatter_out_shapes(host[0]),
        in_specs=[a_spec, b_spec, pl.BlockSpec(memory_space=pl.ANY)], out_specs=(o_spec,) + _scatter_out_specs(),
        scratch_shapes=_scatter_scratch(host[0]), compiler_params=_cparams(("arbitrary", "arbitrary")),
    )(a, b, host[1])
    return res[0], res[1]


RS_ROWS = {W_IN: 128, W_OUT: 128, W_UP: 128, W_DOWN: 176}


class _Scatter:
    def __init__(self, w, partial, out, land1, bufa, bufb, stage2, land2, s1_send, s1_recv, s2_send, s2_recv, ld_sems):
        self.w, self.partial, self.out, self.land1 = w, partial, out, land1
        self.bufa, self.bufb, self.stage2, self.land2 = bufa, bufb, stage2, land2
        self.s1_send, self.s1_recv, self.s2_send, self.s2_recv, self.ld_sems = s1_send, s1_recv, s2_send, s2_recv, ld_sems
        self.x, self.y, self.c = lax.axis_index("x"), lax.axis_index("y"), lax.axis_index("c")
        self.sibling = (self.x, self.y, 1 - self.c)
        self.chips = [(self.x, self.y), (1 - self.x, self.y), (self.x, 1 - self.y), (1 - self.x, 1 - self.y)]

    def block(self, px, py, pc):
        dev = 4 * px + 2 * py + pc
        if self.w == W_IN:
            return self.partial.at[:, pl.ds(pl.multiple_of(dev * IN_SHARD, 128), IN_SHARD)]
        if self.w == W_OUT:
            return self.partial.at[pl.ds(pl.multiple_of(dev * OUT_SHARD, 128), OUT_SHARD), :]
        if self.w == W_DOWN:
            return self.partial.at[pl.ds(pl.multiple_of(dev * DOWN_SHARD, 32), DOWN_SHARD), :]
        return self.partial.at[dev]

    def copy1(self, k):
        return pltpu.make_async_remote_copy(
            src_ref=self.block(*self.chips[k], 1 - self.c), dst_ref=self.land1.at[k],
            send_sem=self.s1_send.at[k], recv_sem=self.s1_recv.at[k], device_id=self.sibling, device_id_type=MESH)

    def copy2(self, k):
        return pltpu.make_async_remote_copy(
            src_ref=self.stage2.at[k - 1], dst_ref=self.land2.at[k - 1],
            send_sem=self.s2_send.at[k - 1], recv_sem=self.s2_recv.at[k - 1], device_id=(*self.chips[k], self.c), device_id_type=MESH)

    def _rows(self):
        n, step = SHARD[self.w][0], RS_ROWS[self.w]
        return [pl.ds(r0, step) for r0 in range(0, n, step)]

    def phase1(self):
        for k in range(4):
            self.copy1(k).start()

    def phase2(self):
        for k in (1, 2, 3, 0):
            la = pltpu.make_async_copy(self.block(*self.chips[k], self.c), self.bufa, self.ld_sems.at[0])
            la.start()
            self.copy1(k).wait_recv()
            lb = pltpu.make_async_copy(self.land1.at[k], self.bufb, self.ld_sems.at[1])
            lb.start()
            la.wait()
            lb.wait()
            for rs in self._rows():
                s = self.bufa[rs, :].astype(F32) + self.bufb[rs, :].astype(F32)
                if k == 0:
                    self.out[rs, :] = s
                else:
                    self.stage2[k - 1, rs, :] = s.astype(BF16)
            if k:
                self.copy2(k).start()

    def phase3(self):
        for k in (1, 2, 3):
            self.copy2(k).wait_recv()
        for rs in self._rows():
            self.out[rs, :] = ((self.out[rs, :] + self.land2[0, rs, :].astype(F32)) + self.land2[1, rs, :].astype(F32)) \
                + self.land2[2, rs, :].astype(F32)
        for k in range(4):
            self.copy1(k).wait_send()
        for k in (1, 2, 3):
            self.copy2(k).wait_send()


def _scatter_out_shapes(w):
    return (jax.ShapeDtypeStruct(SHARD[w], F32), jax.ShapeDtypeStruct((4,) + SHARD[w], BF16))


def _scatter_out_specs():
    return (pl.BlockSpec(memory_space=pltpu.VMEM), pl.BlockSpec(memory_space=pl.ANY))


def _scatter_scratch(w):
    s = SHARD[w]
    return [pltpu.VMEM(s, BF16), pltpu.VMEM(s, BF16), pltpu.VMEM((3,) + s, BF16), pltpu.VMEM((3,) + s, BF16),
            pltpu.SemaphoreType.DMA((4,)), pltpu.SemaphoreType.DMA((4,)), pltpu.SemaphoreType.DMA((3,)),
            pltpu.SemaphoreType.DMA((3,)), pltpu.SemaphoreType.DMA((2,))]


N_SCATTER_SCRATCH = 9


def _rs_last(w, partial):
    def body(p_ref, out, land1, *scratch):
        rs = _Scatter(w, p_ref, out, land1, *scratch)
        rs.phase1()
        rs.phase2()
        rs.phase3()

    return pl.pallas_call(
        body, name="rs_last", out_shape=_scatter_out_shapes(w), in_specs=[pl.BlockSpec(memory_space=pl.ANY)],
        out_specs=_scatter_out_specs(), scratch_shapes=_scatter_scratch(w), compiler_params=_cparams(),
    )(partial)[0]


PACK_W = 1024


def _all_reduce_small(dg1, dg2, dgf, dgrn, dlng, dlnb, dbs, loss_parts, dws, dcv):
    n_a = 3

    def body(dg1_ref, dg2_ref, dgf_ref, dgrn_ref, dlng_ref, dlnb_ref, dbs_ref, loss_ref, dws_ref, dcv_ref,
             rp_ref, rws_ref, rcv_ref, pack, rx_p, rx_ws, rx_cv, cs_p, cs_ws, cs_cv, g_p, g_ws, g_cv,
             s1_send, s1_recv, s2_send, s2_recv, s3_send, s3_recv):
        x, y, c = lax.axis_index("x"), lax.axis_index("y"), lax.axis_index("c")
        sibling = (x, y, 1 - c)
        chips = [(1 - x, y), (x, 1 - y), (1 - x, 1 - y)]
        pack[...] = jnp.zeros_like(pack)
        pack[0, 0:1, :] = dg1_ref[0:1, :]
        pack[0, 1:2, :] = dg2_ref[0:1, :]
        pack[0, 2:3, :] = dgf_ref[0:1, :]
        pack[0, 3:4, 0:RET_W] = dgrn_ref[0:1, :]
        lsum = loss_ref[0, 0:1, :]
        for i in range(1, N_TB):
            lsum = lsum + loss_ref[i, 0:1, :]
        pack[0, 3:4, RET_W:RET_W + 128] = lsum
        pack[1, 0:HEADS, 0:128] = dlng_ref[0:HEADS, :]
        pack[1, 0:HEADS, 128:256] = dlnb_ref[0:HEADS, :]
        pack[1, 0:HEADS, 256:384] = dbs_ref[0:HEADS, :]

        srcs = [pack, dws_ref, dcv_ref]
        outs = [rp_ref, rws_ref, rcv_ref]
        rxs = [rx_p, rx_ws, rx_cv]
        css = [cs_p, cs_ws, cs_cv]
        gs = [g_p, g_ws, g_cv]
        hl = [1, HEADS // 2, N_DEV // 2]

        def half(ref, a, h):
            return ref.at[pl.ds(h * hl[a], hl[a])]

        ex1 = [pltpu.make_async_remote_copy(src_ref=half(srcs[a], a, 1 - c), dst_ref=rxs[a], send_sem=s1_send.at[a],
                                            recv_sem=s1_recv.at[a], device_id=sibling, device_id_type=MESH) for a in range(n_a)]
        for cp in ex1:
            cp.start()
        ex2 = []
        for a in range(n_a):
            ex1[a].wait_recv()
            css[a][...] = half(srcs[a], a, c)[...] + rxs[a][...]
            for j, chip in enumerate(chips):
                cp = pltpu.make_async_remote_copy(src_ref=css[a], dst_ref=gs[a].at[j], send_sem=s2_send.at[a, j],
                                                  recv_sem=s2_recv.at[a, j], device_id=(*chip, c), device_id_type=MESH)
                cp.start()
                ex2.append(cp)
        ex3 = []
        for a in range(n_a):
            for j in range(3):
                ex2[3 * a + j].wait_recv()
            tot = None
            for q in range(4):
                k = jnp.where(x != (q >> 1), 1, 0) + jnp.where(y != (q & 1), 2, 0)
                term = jnp.where(k == 0, css[a][...], jnp.where(k == 1, gs[a][0], jnp.where(k == 2, gs[a][1], gs[a][2])))
                tot = term if tot is None else tot + term
            half(outs[a], a, c)[...] = tot
            cp = pltpu.make_async_remote_copy(src_ref=half(outs[a], a, c), dst_ref=half(outs[a], a, c), send_sem=s3_send.at[a],
                                              recv_sem=s3_recv.at[a], device_id=sibling, device_id_type=MESH)
            cp.start()
            ex3.append(cp)
        for a in range(n_a):
            pltpu.make_async_remote_copy(src_ref=half(outs[a], a, 1 - c), dst_ref=half(outs[a], a, 1 - c), send_sem=s3_send.at[a],
                                         recv_sem=s3_recv.at[a], device_id=sibling, device_id_type=MESH).wait_recv()
        for cp in ex1 + ex2 + ex3:
            cp.wait_send()

    vm = pl.BlockSpec(memory_space=pltpu.VMEM)
    full = [(2, 8, PACK_W), (HEADS, CHUNK, CHUNK), (N_DEV, 8, FF_SHARD)]
    halves = [(s[0] // 2,) + s[1:] for s in full]
    return pl.pallas_call(
        body, name="ar_small",
        out_shape=tuple(jax.ShapeDtypeStruct(s, F32) for s in full),
        in_specs=[vm] * 10, out_specs=(vm,) * 3,
        scratch_shapes=[pltpu.VMEM(full[0], F32)] + [pltpu.VMEM(s, F32) for s in halves] + [pltpu.VMEM(s, F32) for s in halves]
        + [pltpu.VMEM((3,) + s, F32) for s in halves]
        + [pltpu.SemaphoreType.DMA((n_a,)), pltpu.SemaphoreType.DMA((n_a,)), pltpu.SemaphoreType.DMA((n_a, 3)),
           pltpu.SemaphoreType.DMA((n_a, 3)), pltpu.SemaphoreType.DMA((n_a,)), pltpu.SemaphoreType.DMA((n_a,))],
        compiler_params=_cparams(),
    )(dg1, dg2, dgf, dgrn, dlng, dlnb, dbs, loss_parts, dws, dcv)


def _adam_math(w, g, m, v):
    nm = ADAM_B1 * m + (1.0 - ADAM_B1) * g
    nv = ADAM_B2 * v + (1.0 - ADAM_B2) * (g * g)
    d = -ADAM_LR * ((nm / (1.0 - ADAM_B1 ** ADAM_STEP)) / (jnp.sqrt(nv / (1.0 - ADAM_B2 ** ADAM_STEP)) + ADAM_EPS) + ADAM_WD * w)
    return d, nm, nv


def _adamw(name, w, g, m, v, rows):
    _, r, cdim = w.shape

    def body(w_ref, g_ref, m_ref, v_ref, go_ref, d_ref, nm_ref, nv_ref):
        gg = g_ref[...]
        go_ref[0] = gg
        d, nm, nv = _adam_math(w_ref[0], gg, m_ref[0], v_ref[0])
        d_ref[0], nm_ref[0], nv_ref[0] = d, nm, nv

    spec3 = pl.BlockSpec((1, rows, cdim), lambda i: (0, i, 0))
    sh = jax.ShapeDtypeStruct((1, r, cdim), F32)
    return pl.pallas_call(
        body, name=name, grid=(r // rows,), out_shape=(sh, sh, sh, sh),
        in_specs=[spec3, pl.BlockSpec((rows, cdim), lambda i: (i, 0)), spec3, spec3], out_specs=(spec3,) * 4,
        compiler_params=_cparams(("parallel",)),
    )(w, g, m, v)


def _adamw_small(rp, rws, rcv, params):
    n_p = len(params)

    def body(*refs):
        rp_ref, rws_ref, rcv_ref = refs[:3]
        ins = refs[3:3 + 3 * n_p]
        outs = refs[3 + 3 * n_p:]
        me = 4 * lax.axis_index("x") + 2 * lax.axis_index("y") + lax.axis_index("c")
        grads = [rp_ref[0, 0:1, :], rp_ref[0, 1:2, :], rp_ref[0, 2:3, :], rp_ref[0, 3:4, 0:RET_W],
                 rp_ref[1, 0:HEADS, 0:128], rp_ref[1, 0:HEADS, 128:256], rp_ref[1, 0:HEADS, 256:384],
                 rws_ref[...], rcv_ref[me][0:3, :], None]
        for p in range(n_p):
            w_ref, m_ref, v_ref = ins[3 * p:3 * p + 3]
            o = outs[4 * p:4 * p + 4]
            if p == n_p - 1:
                for j in range(N_DEV):
                    g = rcv_ref[j, 3:4, :]
                    res = (g,) + _adam_math(w_ref[j:j + 1, :], g, m_ref[j:j + 1, :], v_ref[j:j + 1, :])
                    for t in range(4):
                        o[t][j:j + 1, :] = res[t]
                continue
            lead = w_ref.ndim > grads[p].ndim
            rd = (lambda r: r[0]) if lead else (lambda r: r[...])
            res = (grads[p],) + _adam_math(rd(w_ref), grads[p], rd(m_ref), rd(v_ref))
            for t in range(4):
                if lead:
                    o[t][0] = res[t]
                else:
                    o[t][...] = res[t]

    vm = pl.BlockSpec(memory_space=pltpu.VMEM)
    flat = [a for tr in params for a in tr]
    out_shape = tuple(jax.ShapeDtypeStruct(tr[0].shape, F32) for tr in params for _ in range(4))
    res = pl.pallas_call(
        body, name="adamw_small", out_shape=out_shape, in_specs=[vm] * (3 + len(flat)), out_specs=(vm,) * len(out_shape),
        compiler_params=_cparams(),
    )(rp, rws, rcv, *flat)
    return [res[4 * p:4 * p + 4] for p in range(n_p)]


def kernel(x, mix_norm_g, w_in, ret_norm_g, sgu_ln_g, sgu_ln_b, sgu_w_s, sgu_b_s, w_out, ffn_norm_g, w_up, conv_w, conv_b, w_down, final_norm_g, loss_target, m_mix_norm_g, m_w_in, m_ret_norm_g, m_sgu_ln_g, m_sgu_ln_b, m_sgu_w_s, m_sgu_b_s, m_w_out, m_ffn_norm_g, m_w_up, m_conv_w, m_conv_b, m_w_down, m_final_norm_g, v_mix_norm_g, v_w_in, v_ret_norm_g, v_sgu_ln_g, v_sgu_ln_b, v_sgu_w_s, v_sgu_b_s, v_w_out, v_ffn_norm_g, v_w_up, v_conv_w, v_conv_b, v_w_down, v_final_norm_g):
    xs = x[0]
    tgt = loss_target[0]
    cos2, sin2 = _rope_tables()
    mask, qdec, kdec = _decay_tables()
    grn = ret_norm_g.reshape(1, RET_W)
    lng = sgu_ln_g.reshape(1, SGU_W)
    lnb = sgu_ln_b.reshape(1, SGU_W)
    ws = sgu_w_s[0]
    bsb = jnp.broadcast_to(sgu_b_s[0][:, :, None], (HEADS, CHUNK, HEAD_DIM))
    gf = final_norm_g.reshape(1, D_MODEL)
    cb_g = conv_b.reshape(N_DEV, 1, FF_SHARD)

    win_g, cw_g, so, su, sd = _ag_first(w_in[0], w_out[0], w_up[0], w_down[0], conv_w[0])

    proj, h1, wout_g, wdn_g = _fwd_proj(xs, mix_norm_g, win_g, cos2, sin2, so, sd)
    x2, mixcat, o, sprev, wup_g = _fwd_mix(xs, proj, wout_g, grn, lng, lnb, ws, bsb, mask, qdec, kdec, su)
    h2, up_pre, u_conv, act, x3, loss_parts = _fwd_ffn(x2, ffn_norm_g, wup_g, cw_g, cb_g, wdn_g, gf, tgt)

    dx3, dpre, dx2, dgf, dg2, dcv = _bwd_ffn(x3, tgt, gf, x2, ffn_norm_g, up_pre, u_conv, wup_g, cw_g, wdn_g)
    gdn_p = _wgrad("wgrad_down", act, dx3, N_FF_PAIR, 0).reshape(D_FF, D_MODEL)
    gout_p = _wgrad("wgrad_out", mixcat, dx2, 0, 0, tn=512)[0]
    gup_p, g_dn = _wgrad("wgrad_up", h2, dpre, 0, N_DEV, host=(W_DOWN, gdn_p))
    dproj, dgrn, dlng, dlnb, dws, dbs, g_up = _bwd_mix(dx2, proj, o, sprev, wout_g, grn, lng, lnb, ws, bsb, mask, qdec, kdec,
                                                      cos2, sin2, gup_p)
    grad_x, dg1 = _bwd_proj(dproj, win_g, xs, mix_norm_g, dx2)
    gin_p, g_out = _wgrad("wgrad_in", h1, dproj, 0, 0, tn=768, host=(W_OUT, gout_p))
    g_in = _rs_last(W_IN, gin_p[0])
    rp, rws, rcv = _all_reduce_small(dg1, dg2, dgf, dgrn, dlng, dlnb, dbs, loss_parts, dws, dcv)
    loss = rp[0, 3, RET_W]

    table = {}
    for name, w, g, m, v, rows in (("w_in", w_in, g_in, m_w_in, v_w_in, 256), ("w_out", w_out, g_out, m_w_out, v_w_out, 128),
                                   ("w_up", w_up, g_up, m_w_up, v_w_up, 256), ("w_down", w_down, g_dn, m_w_down, v_w_down, 88)):
        table[name] = _adamw("adamw_" + name, w, g, m, v, rows)
    row = lambda a: a.reshape(1, D_MODEL)
    slab = lambda a: a.reshape(N_DEV, FF_SHARD)
    names_small = ["mix_norm_g", "ffn_norm_g", "final_norm_g", "ret_norm_g", "sgu_ln_g", "sgu_ln_b", "sgu_b_s", "sgu_w_s",
                   "conv_w", "conv_b"]
    params = [(mix_norm_g, m_mix_norm_g, v_mix_norm_g), (ffn_norm_g, m_ffn_norm_g, v_ffn_norm_g),
              (row(final_norm_g), row(m_final_norm_g), row(v_final_norm_g)), (ret_norm_g, m_ret_norm_g, v_ret_norm_g),
              (sgu_ln_g, m_sgu_ln_g, v_sgu_ln_g), (sgu_ln_b, m_sgu_ln_b, v_sgu_ln_b), (sgu_b_s, m_sgu_b_s, v_sgu_b_s),
              (sgu_w_s, m_sgu_w_s, v_sgu_w_s), (conv_w, m_conv_w, v_conv_w), (slab(conv_b), slab(m_conv_b), slab(v_conv_b))]
    for n, res in zip(names_small, _adamw_small(rp, rws, rcv, params)):
        table[n] = res
    table["final_norm_g"] = tuple(a.reshape(D_MODEL) for a in table["final_norm_g"])
    table["conv_b"] = tuple(a.reshape(1, 2 * D_FF) for a in table["conv_b"])

    order = ["mix_norm_g", "w_in", "ret_norm_g", "sgu_ln_g", "sgu_ln_b", "sgu_w_s", "sgu_b_s", "w_out", "ffn_norm_g", "w_up",
             "conv_w", "conv_b", "w_down", "final_norm_g"]
    outs = [loss, grad_x[None]]
    for col in range(4):
        outs += [table[n][col] for n in order]
    return tuple(outs)
```

```python
import functools
import math

import jax
import jax.numpy as jnp
import numpy as np
from jax import lax
from jax.experimental import pallas as pl
from jax.experimental.pallas import tpu as pltpu

F32 = jnp.float32
BF16 = jnp.bfloat16
MESH = pl.DeviceIdType.MESH

N_DEV = 8
SEQ = 2048
D_MODEL = 1024
CHUNK = 128
N_CHUNK = SEQ // CHUNK
HEADS = 4
HEAD_DIM = 128
RET_W = 512
SGU_W = 512
PROJ_W = 3072
D_FF = 2816
FF_SHARD = 704
N_FF_PAIR = 4
IN_SHARD = PROJ_W // N_DEV
OUT_SHARD = D_MODEL // N_DEV
DOWN_SHARD = D_FF // N_DEV
TM = 256
N_TB = SEQ // TM
EPS = 1e-6
ROPE_BASE = 10000.0
K_SCALE = HEAD_DIM ** -0.5
INV_SQRT2 = 0.7071067811865476
INV_SQRT_2PI = 0.3989422804014327

ADAM_LR = 0.001
ADAM_B1 = 0.9
ADAM_B2 = 0.999
ADAM_EPS = 1e-08
ADAM_WD = 0.01
ADAM_STEP = 10

VMEM_LIMIT = 56 * 1024 * 1024


def _cparams(sem=None, vmem=VMEM_LIMIT):
    return pltpu.CompilerParams(dimension_semantics=sem, vmem_limit_bytes=vmem)


def _resident(shape):
    nd = len(shape)
    return pl.BlockSpec(shape, lambda *_: (0,) * nd, pipeline_mode=pl.Buffered(1))


def _dot(a, b):
    return jnp.dot(a, b, preferred_element_type=F32)


def _dot_nt(a, b):
    return lax.dot_general(a, b, (((1,), (1,)), ((), ())), preferred_element_type=F32)


def _dot_tn(a, b):
    return lax.dot_general(a, b, (((0,), (0,)), ((), ())), preferred_element_type=F32)


def _sigmoid(x):
    return 1.0 / (1.0 + jnp.exp(-x))


def _gelu(x):
    return 0.5 * x * (1.0 + lax.erf(x * INV_SQRT2))


def _gelu_grad(x):
    return 0.5 * (1.0 + lax.erf(x * INV_SQRT2)) + x * (jnp.exp(-0.5 * x * x) * INV_SQRT_2PI)


def _rot(xh, cos2, sin2):
    return xh * cos2 + pltpu.roll(xh, HEAD_DIM // 2, 1) * sin2


def _rot_t(dh, cos2, sin2):
    return dh * cos2 + pltpu.roll(dh * sin2, HEAD_DIM // 2, 1)


def _rope_tables():
    half = HEAD_DIM // 2
    inv_freq = jnp.power(ROPE_BASE, -jnp.arange(half, dtype=F32) / half)
    ang = jnp.arange(SEQ, dtype=F32)[:, None] * inv_freq[None, :]
    cos, sin = jnp.cos(ang), jnp.sin(ang)
    cos2 = jnp.concatenate([cos, cos], axis=-1)
    sin2 = jnp.concatenate([-sin, sin], axis=-1)
    return cos2, sin2


def _decay_tables():
    log_gamma = jnp.log(1.0 - jnp.power(2.0, -5.0 - jnp.arange(HEADS, dtype=F32)))
    pos = jnp.arange(CHUNK, dtype=F32)
    diff = pos[:, None] - pos[None, :]
    mask = jnp.where(diff >= 0.0, jnp.exp(log_gamma[:, None, None] * jnp.maximum(diff, 0.0)[None]), 0.0)
    k_decay = jnp.exp(log_gamma[:, None] * (CHUNK - 1.0 - pos)[None])
    q_decay = jnp.exp(log_gamma[:, None] * (pos + 1.0)[None])
    kd = jnp.broadcast_to(k_decay[:, :, None], (HEADS, CHUNK, HEAD_DIM))
    qd = jnp.broadcast_to(q_decay[:, :, None], (HEADS, CHUNK, HEAD_DIM))
    return mask.astype(F32), qd.astype(F32), kd.astype(F32)


def _chunk_decay():
    lg = np.log(np.float32(1.0) - np.power(np.float32(2.0), -5.0 - np.arange(HEADS, dtype=np.float32))).astype(np.float32)
    return [float(np.exp(lg[h] * np.float32(CHUNK))) for h in range(HEADS)]


W_IN, W_OUT, W_UP, W_DOWN, W_CONV = range(5)
GATHERED = {W_IN: ((D_MODEL, PROJ_W), BF16), W_OUT: ((D_MODEL, D_MODEL), BF16), W_UP: ((N_DEV, D_MODEL, FF_SHARD), BF16),
            W_DOWN: ((D_FF, D_MODEL), BF16), W_CONV: ((N_DEV, 8, FF_SHARD), F32)}
SHARD = {W_IN: (D_MODEL, IN_SHARD), W_OUT: (OUT_SHARD, D_MODEL), W_UP: (D_MODEL, FF_SHARD), W_DOWN: (DOWN_SHARD, D_MODEL),
         W_CONV: (8, FF_SHARD)}


class _Gather:
    def __init__(self, ids, stages, gathered, send_sems, recv_sems, local_sems):
        self.ids, self.stages, self.gathered = ids, stages, gathered
        self.send_sems, self.recv_sems, self.local_sems = send_sems, recv_sems, local_sems
        self.x, self.y, self.c = lax.axis_index("x"), lax.axis_index("y"), lax.axis_index("c")
        self.me = (self.x, self.y, self.c)
        self.sibling = (self.x, self.y, 1 - self.c)
        self.chips = [(1 - self.x, self.y), (self.x, 1 - self.y), (1 - self.x, 1 - self.y)]

    def slot(self, n, px, py, pc):
        dev = 4 * px + 2 * py + pc
        w, g = self.ids[n], self.gathered[n]
        if w == W_IN:
            return g.at[:, pl.ds(pl.multiple_of(dev * IN_SHARD, 128), IN_SHARD)]
        if w == W_OUT:
            return g.at[pl.ds(pl.multiple_of(dev * OUT_SHARD, 128), OUT_SHARD), :]
        if w == W_DOWN:
            return g.at[pl.ds(pl.multiple_of(dev * DOWN_SHARD, 32), DOWN_SHARD), :]
        return g.at[dev]

    def copy(self, n, k, block, to, src=None):
        return pltpu.make_async_remote_copy(
            src_ref=self.slot(n, *block) if src is None else src, dst_ref=self.slot(n, *block),
            send_sem=self.send_sems.at[n, k], recv_sem=self.recv_sems.at[n, k], device_id=to, device_id_type=MESH)

    def _mine(self):
        return [pltpu.make_async_copy(self.stages[n], self.slot(n, *self.me), self.local_sems.at[n]) for n in range(len(self.ids))]

    def _first(self):
        out = []
        for n in range(len(self.ids)):
            out.append(self.copy(n, 0, self.me, self.sibling, src=self.stages[n]))
            out += [self.copy(n, 1 + j, self.me, (*chip, self.c), src=self.stages[n]) for j, chip in enumerate(self.chips)]
        return out

    def start(self):
        for cp in self._mine() + self._first():
            cp.start()

    def finish(self):
        passed = []
        for n in range(len(self.ids)):
            for j, chip in enumerate(self.chips):
                self.copy(n, 1 + j, (*chip, self.c), self.me).wait_recv()
                fwd = self.copy(n, 4 + j, (*chip, self.c), self.sibling)
                fwd.start()
                passed.append(fwd)
        for n in range(len(self.ids)):
            self.copy(n, 0, self.sibling, self.me).wait_recv()
            for j, chip in enumerate(self.chips):
                self.copy(n, 4 + j, (*chip, 1 - self.c), self.me).wait_recv()
        for cp in self._first() + passed:
            cp.wait_send()
        for cp in self._mine():
            cp.wait()


def _gather_scratch(n):
    return [pltpu.SemaphoreType.DMA((n, 7)), pltpu.SemaphoreType.DMA((n, 7)), pltpu.SemaphoreType.DMA((n,))]


def _gathered_shapes(ids):
    return tuple(jax.ShapeDtypeStruct(*GATHERED[w]) for w in ids)


def _ag_first(w_in, w_out, w_up, w_down, conv_w):
    ids = [W_IN, W_CONV]

    def body(in_ref, out_ref, up_ref, dn_ref, cw_ref, gin, gcw, so_ref, su_ref, sd_ref, s_in, s_cw, send_sems, recv_sems, local_sems):
        s_in[...] = in_ref[...].astype(BF16)
        s_cw[...] = jnp.zeros_like(s_cw)
        s_cw[0:3, :] = cw_ref[...]
        ag = _Gather(ids, [s_in, s_cw], [gin, gcw], send_sems, recv_sems, local_sems)
        ag.start()
        so_ref[...] = out_ref[...].astype(BF16)
        su_ref[...] = up_ref[...].astype(BF16)
        sd_ref[...] = dn_ref[...].astype(BF16)
        ag.finish()

    vm = pl.BlockSpec(memory_space=pltpu.VMEM)
    hbm = pl.BlockSpec(memory_space=pl.ANY)
    return pl.pallas_call(
        body, name="ag_first",
        out_shape=_gathered_shapes(ids) + tuple(jax.ShapeDtypeStruct(SHARD[w], BF16) for w in (W_OUT, W_UP, W_DOWN)),
        in_specs=[vm] * 5, out_specs=(hbm, hbm, vm, vm, vm),
        scratch_shapes=[pltpu.VMEM(SHARD[W_IN], BF16), pltpu.VMEM(SHARD[W_CONV], F32)] + _gather_scratch(len(ids)),
        compiler_params=_cparams(),
    )(w_in, w_out, w_up, w_down, conv_w)


def _fwd_proj(x, g1, win_g, cos2, sin2, so, sd):
    ids = [W_OUT, W_DOWN]

    def body(x_ref, g_ref, w_ref, cos_ref, sin_ref, so_ref, sd_ref, proj_ref, h1_ref, gout, gdn, send_sems, recv_sems, local_sems):
        ag = _Gather(ids, [so_ref, sd_ref], [gout, gdn], send_sems, recv_sems, local_sems)

        @pl.when(pl.program_id(0) == 0)
        def _():
            ag.start()

        xb = x_ref[...]
        r = lax.rsqrt(jnp.mean(xb * xb, axis=-1, keepdims=True) + EPS)
        h = ((xb * r) * g_ref[...]).astype(BF16)
        h1_ref[...] = h
        p = _dot(h, w_ref[...])
        for hd in range(HEADS):
            sl = slice(hd * HEAD_DIM, (hd + 1) * HEAD_DIM)
            c2, s2 = cos_ref[...], sin_ref[...]
            proj_ref[:, sl] = _rot(p[:, sl], c2, s2)
            ks = slice(RET_W + hd * HEAD_DIM, RET_W + (hd + 1) * HEAD_DIM)
            proj_ref[:, ks] = _rot(p[:, ks], c2, s2) * K_SCALE
        proj_ref[:, 2 * RET_W:] = p[:, 2 * RET_W:]

        @pl.when(pl.program_id(0) == N_TB - 1)
        def _():
            ag.finish()

    tok = lambda w: pl.BlockSpec((TM, w), lambda i: (i, 0))
    hbm = pl.BlockSpec(memory_space=pl.ANY)
    return pl.pallas_call(
        body, name="fwd_proj", grid=(N_TB,),
        out_shape=(jax.ShapeDtypeStruct((SEQ, PROJ_W), F32), jax.ShapeDtypeStruct((SEQ, D_MODEL), BF16)) + _gathered_shapes(ids),
        in_specs=[tok(D_MODEL), _resident((1, D_MODEL)), _resident((D_MODEL, PROJ_W)), tok(HEAD_DIM), tok(HEAD_DIM), hbm, hbm],
        out_specs=(tok(PROJ_W), tok(D_MODEL), hbm, hbm),
        scratch_shapes=_gather_scratch(len(ids)),
        compiler_params=_cparams(("arbitrary",)),
    )(x, g1, win_g, cos2, sin2, so, sd)


def _causal(w):
    r = lax.broadcasted_iota(jnp.int32, (CHUNK, CHUNK), 0)
    c = lax.broadcasted_iota(jnp.int32, (CHUNK, CHUNK), 1)
    return jnp.where(r >= c, w, 0.0)


def _fwd_mix(x, proj, wout_g, grn, lng, lnb, ws, bsb, mask, qdec, kdec, su):
    cdec = _chunk_decay()
    ids = [W_UP]

    def body(x_ref, p_ref, w_ref, grn_ref, lng_ref, lnb_ref, ws_ref, bsb_ref, m_ref, qd_ref, kd_ref, su_ref,
             x2_ref, cat_ref, o_ref, sp_ref, gup, state, send_sems, recv_sems, local_sems):
        ag = _Gather(ids, [su_ref], [gup], send_sems, recv_sems, local_sems)

        @pl.when(pl.program_id(0) == 0)
        def _():
            state[...] = jnp.zeros_like(state)
            ag.start()

        for h in range(HEADS):
            sl = slice(h * HEAD_DIM, (h + 1) * HEAD_DIM)
            q = p_ref[:, sl]
            k = p_ref[:, RET_W + h * HEAD_DIM:RET_W + (h + 1) * HEAD_DIM]
            v = p_ref[:, 2 * RET_W + h * HEAD_DIM:2 * RET_W + (h + 1) * HEAD_DIM]
            g = p_ref[:, 3 * RET_W + h * HEAD_DIM:3 * RET_W + (h + 1) * HEAD_DIM]
            qb, kb, vb = q.astype(BF16), k.astype(BF16), v.astype(BF16)
            a = _dot_nt(qb, kb) * m_ref[h]
            spb = state[h].astype(BF16)
            sp_ref[0, h] = spb
            o = _dot(a.astype(BF16), vb) + _dot((q * qd_ref[h]).astype(BF16), spb)
            state[h] = state[h] * cdec[h] + _dot_tn((k * kd_ref[h]).astype(BF16), vb)
            o_ref[:, sl] = o
            rinv = lax.rsqrt(jnp.mean(o * o, axis=-1, keepdims=True) + EPS)
            rn = (o * rinv) * grn_ref[:, sl]
            cat_ref[:, sl] = ((g * _sigmoid(g)) * rn).astype(BF16)
        for gi in range(HEADS):
            sl = slice(gi * HEAD_DIM, (gi + 1) * HEAD_DIM)
            u = p_ref[:, 4 * RET_W + gi * HEAD_DIM:4 * RET_W + (gi + 1) * HEAD_DIM]
            sv = p_ref[:, 4 * RET_W + SGU_W + gi * HEAD_DIM:4 * RET_W + SGU_W + (gi + 1) * HEAD_DIM]
            gv = _gelu(sv)
            xc = gv - jnp.mean(gv, axis=-1, keepdims=True)
            vn = (xc * lax.rsqrt(jnp.mean(xc * xc, axis=-1, keepdims=True) + EPS)) * lng_ref[:, sl] + lnb_ref[:, sl]
            mixed = _dot(_causal(ws_ref[gi]).astype(BF16), vn.astype(BF16)) + bsb_ref[gi]
            cat_ref[:, RET_W + gi * HEAD_DIM:RET_W + (gi + 1) * HEAD_DIM] = (_gelu(u) * mixed).astype(BF16)
        x2_ref[...] = x_ref[...] + _dot(cat_ref[...], w_ref[...])

        @pl.when(pl.program_id(0) == N_CHUNK - 1)
        def _():
            ag.finish()

    ch = lambda w: pl.BlockSpec((CHUNK, w), lambda i: (i, 0))
    hcc = (HEADS, CHUNK, CHUNK)
    hbm = pl.BlockSpec(memory_space=pl.ANY)
    return pl.pallas_call(
        body, name="fwd_mix", grid=(N_CHUNK,),
        out_shape=(jax.ShapeDtypeStruct((SEQ, D_MODEL), F32), jax.ShapeDtypeStruct((SEQ, D_MODEL), BF16),
                   jax.ShapeDtypeStruct((SEQ, RET_W), F32), jax.ShapeDtypeStruct((N_CHUNK, HEADS, HEAD_DIM, HEAD_DIM), BF16))
        + _gathered_shapes(ids),
        in_specs=[ch(D_MODEL), ch(PROJ_W), _resident((D_MODEL, D_MODEL)), _resident((1, RET_W)), _resident((1, SGU_W)),
                  _resident((1, SGU_W)), _resident(hcc), _resident(hcc), _resident(hcc), _resident(hcc), _resident(hcc), hbm],
        out_specs=(ch(D_MODEL), ch(D_MODEL), ch(RET_W), pl.BlockSpec((1, HEADS, HEAD_DIM, HEAD_DIM), lambda i: (i, 0, 0, 0)), hbm),
        scratch_shapes=[pltpu.VMEM((HEADS, HEAD_DIM, HEAD_DIM), F32)] + _gather_scratch(len(ids)),
        compiler_params=_cparams(("arbitrary",)),
    )(x, proj, wout_g, grn, lng, lnb, ws, bsb, mask, qdec, kdec, su)


def _conv_taps(p, prev8):
    row = lax.broadcasted_iota(jnp.int32, p.shape, 0)
    p1 = jnp.where(row == 0, prev8[7:8, :], pltpu.roll(p, 1, 0))
    p2 = jnp.where(row == 0, prev8[6:7, :], jnp.where(row == 1, prev8[7:8, :], pltpu.roll(p, 2, 0)))
    return p1, p2


def _fwd_ffn(x2, g2, wup_g, cw_g, cb_g, wdn_g, gf, tgt):
    def body(x_ref, g_ref, wu_ref, cw_ref, cb_ref, wd_ref, gf_ref, t_ref, h2_ref, up_ref, u_ref, act_ref, x3_ref, loss_ref, carry):
        @pl.when(pl.program_id(0) == 0)
        def _():
            carry[...] = jnp.zeros_like(carry)

        xb = x_ref[...]
        r = lax.rsqrt(jnp.mean(xb * xb, axis=-1, keepdims=True) + EPS)
        h = ((xb * r) * g_ref[...]).astype(BF16)
        h2_ref[...] = h
        acc = xb
        for j in range(N_FF_PAIR):
            u = []
            for s in (j, j + N_FF_PAIR):
                p = _dot(h, wu_ref[s])
                up_ref[s] = p.astype(BF16)
                p1, p2 = _conv_taps(p, carry[s])
                carry[s] = p[TM - 8:, :]
                cw = cw_ref[s]
                us = p2 * cw[0:1, :] + p1 * cw[1:2, :] + p * cw[2:3, :] + cb_ref[s]
                u_ref[s] = us.astype(BF16)
                u.append(us)
            a = ((u[0] * _sigmoid(u[0])) * u[1]).astype(BF16)
            act_ref[j] = a
            acc = acc + _dot(a, wd_ref[pl.ds(j * FF_SHARD, FF_SHARD), :])
        x3_ref[...] = acc
        r3 = lax.rsqrt(jnp.mean(acc * acc, axis=-1, keepdims=True) + EPS)
        diff = (acc * r3) * gf_ref[...] - t_ref[...]
        loss_ref[...] = jnp.full(loss_ref.shape, 0.5 * jnp.sum(jnp.mean(diff * diff, axis=-1)), F32)

    tok = lambda w: pl.BlockSpec((TM, w), lambda i: (i, 0))
    return pl.pallas_call(
        body, name="fwd_ffn", grid=(N_TB,),
        out_shape=(jax.ShapeDtypeStruct((SEQ, D_MODEL), BF16), jax.ShapeDtypeStruct((N_DEV, SEQ, FF_SHARD), BF16),
                   jax.ShapeDtypeStruct((N_DEV, SEQ, FF_SHARD), BF16),
                   jax.ShapeDtypeStruct((N_FF_PAIR, SEQ, FF_SHARD), BF16), jax.ShapeDtypeStruct((SEQ, D_MODEL), F32),
                   jax.ShapeDtypeStruct((N_TB, 8, 128), F32)),
        in_specs=[tok(D_MODEL), _resident((1, D_MODEL)), _resident((N_DEV, D_MODEL, FF_SHARD)), _resident((N_DEV, 8, FF_SHARD)),
                  _resident((N_DEV, 1, FF_SHARD)), _resident((D_FF, D_MODEL)), _resident((1, D_MODEL)), tok(D_MODEL)],
        out_specs=(tok(D_MODEL), pl.BlockSpec((N_DEV, TM, FF_SHARD), lambda i: (0, i, 0)),
                   pl.BlockSpec((N_DEV, TM, FF_SHARD), lambda i: (0, i, 0)),
                   pl.BlockSpec((N_FF_PAIR, TM, FF_SHARD), lambda i: (0, i, 0)), tok(D_MODEL),
                   pl.BlockSpec((1, 8, 128), lambda i: (i, 0, 0))),
        scratch_shapes=[pltpu.VMEM((N_DEV, 8, FF_SHARD), F32)],
        compiler_params=_cparams(("arbitrary",)),
    )(x2, g2, wup_g, cw_g, cb_g, wdn_g, gf, tgt)


def _bwd_ffn(x3, tgt, gf, x2, g2, up_pre, u_conv, wup_g, cw_g, wdn_g):
    def body(x3_ref, t_ref, gf_ref, x2_ref, g2_ref, up_ref, u_ref, wu_ref, cw_ref, wd_ref,
             dx3_ref, dpre_ref, dx2_ref, dgf_ref, dg2_ref, dcv_ref, nxt):
        i = pl.program_id(0)

        @pl.when(i == 0)
        def _():
            nxt[...] = jnp.zeros_like(nxt)
            dgf_ref[...] = jnp.zeros_like(dgf_ref)
            dg2_ref[...] = jnp.zeros_like(dg2_ref)
            dcv_ref[...] = jnp.zeros_like(dcv_ref)

        x3 = x3_ref[...]
        r3 = lax.rsqrt(jnp.mean(x3 * x3, axis=-1, keepdims=True) + EPS)
        xh3 = x3 * r3
        dy = (xh3 * gf_ref[...] - t_ref[...]) * (1.0 / D_MODEL)
        dgf_ref[0:1, :] += jnp.sum(dy * xh3, axis=0, keepdims=True)
        t3 = dy * gf_ref[...]
        dx3 = r3 * (t3 - xh3 * jnp.mean(t3 * xh3, axis=-1, keepdims=True))
        dx3b = dx3.astype(BF16)
        dx3_ref[...] = dx3b
        dh2 = jnp.zeros((TM, D_MODEL), F32)
        row = lax.broadcasted_iota(jnp.int32, (TM, FF_SHARD), 0)
        for j in range(N_FF_PAIR):
            dact = _dot_nt(dx3b, wd_ref[pl.ds(j * FF_SHARD, FF_SHARD), :])
            ua = u_ref[j].astype(F32)
            ub = u_ref[j + N_FF_PAIR].astype(F32)
            sg = _sigmoid(ua)
            du = [dact * ub * (sg * (1.0 + ua * (1.0 - sg))), dact * (ua * sg)]
            for n, s in enumerate((j, j + N_FF_PAIR)):
                d = du[n]
                cw = cw_ref[s]
                nx = nxt[s]
                n1 = jnp.where(row == TM - 1, nx[0:1, :], pltpu.roll(d, TM - 1, 0))
                n2 = jnp.where(row == TM - 2, nx[0:1, :], jnp.where(row == TM - 1, nx[1:2, :], pltpu.roll(d, TM - 2, 0)))
                nxt[s] = d[0:8, :]
                dp = (d * cw[2:3, :] + n1 * cw[1:2, :] + n2 * cw[0:1, :]).astype(BF16)
                dpre_ref[s] = dp
                p = up_ref[s].astype(F32)
                dcv_ref[s, 0:1, :] += jnp.sum(n2 * p, axis=0, keepdims=True)
                dcv_ref[s, 1:2, :] += jnp.sum(n1 * p, axis=0, keepdims=True)
                dcv_ref[s, 2:3, :] += jnp.sum(d * p, axis=0, keepdims=True)
                dcv_ref[s, 3:4, :] += jnp.sum(d, axis=0, keepdims=True)
                dh2 = dh2 + _dot_nt(dp, wu_ref[s])
        x2 = x2_ref[...]
        r2 = lax.rsqrt(jnp.mean(x2 * x2, axis=-1, keepdims=True) + EPS)
        xh2 = x2 * r2
        dg2_ref[0:1, :] += jnp.sum(dh2 * xh2, axis=0, keepdims=True)
        t2 = dh2 * g2_ref[...]
        dx2_ref[...] = dx3 + r2 * (t2 - xh2 * jnp.mean(t2 * xh2, axis=-1, keepdims=True))

    rev = lambda w: pl.BlockSpec((TM, w), lambda i: (N_TB - 1 - i, 0))
    rev3 = lambda: pl.BlockSpec((N_DEV, TM, FF_SHARD), lambda i: (0, N_TB - 1 - i, 0))
    acc = lambda s: pl.BlockSpec(s, lambda i: (0,) * len(s))
    return pl.pallas_call(
        body, name="bwd_ffn", grid=(N_TB,),
        out_shape=(jax.ShapeDtypeStruct((SEQ, D_MODEL), BF16), jax.ShapeDtypeStruct((N_DEV, SEQ, FF_SHARD), BF16),
                   jax.ShapeDtypeStruct((SEQ, D_MODEL), F32), jax.ShapeDtypeStruct((8, D_MODEL), F32),
                   jax.ShapeDtypeStruct((8, D_MODEL), F32), jax.ShapeDtypeStruct((N_DEV, 8, FF_SHARD), F32)),
        in_specs=[rev(D_MODEL), rev(D_MODEL), _resident((1, D_MODEL)), rev(D_MODEL), _resident((1, D_MODEL)), rev3(), rev3(),
                  _resident((N_DEV, D_MODEL, FF_SHARD)), _resident((N_DEV, 8, FF_SHARD)), _resident((D_FF, D_MODEL))],
        out_specs=(rev(D_MODEL), rev3(), rev(D_MODEL), acc((8, D_MODEL)), acc((8, D_MODEL)), acc((N_DEV, 8, FF_SHARD))),
        scratch_shapes=[pltpu.VMEM((N_DEV, 8, FF_SHARD), F32)],
        compiler_params=_cparams(("arbitrary",)),
    )(x3, tgt, gf, x2, g2, up_pre, u_conv, wup_g, cw_g, wdn_g)


def _bwd_mix(dx2, proj, o, sprev, wout_g, grn, lng, lnb, ws, bsb, mask, qdec, kdec, cos2, sin2, hosted):
    cdec = _chunk_decay()
    geoms = [g for g, _ in hosted]
    n_h = len(hosted)

    def body(dx2_ref, p_ref, o_ref, sp_ref, w_ref, grn_ref, lng_ref, lnb_ref, ws_ref, bsb_ref, m_ref, qd_ref, kd_ref,
             cos_ref, sin_ref, *rest):
        dp_ref, dgrn_ref, dlng_ref, dlnb_ref, dws_ref, dbs_ref = rest[n_h:n_h + 6]
        dstate, dbs_acc = rest[3 * n_h + 6:3 * n_h + 8]
        i = pl.program_id(0)
        rs = _Scatters(geoms, rest[:n_h], rest[n_h + 6:3 * n_h + 6], rest[3 * n_h + 8:])
        pl.when(i == 0)(rs.phase1)
        pl.when(i == 3)(rs.phase2)

        @pl.when(i == 0)
        def _():
            dstate[...] = jnp.zeros_like(dstate)
            dgrn_ref[...] = jnp.zeros_like(dgrn_ref)
            dlng_ref[...] = jnp.zeros_like(dlng_ref)
            dlnb_ref[...] = jnp.zeros_like(dlnb_ref)
            dws_ref[...] = jnp.zeros_like(dws_ref)
            dbs_ref[...] = jnp.zeros_like(dbs_ref)
            dbs_acc[...] = jnp.zeros_like(dbs_acc)

        dmix = _dot_nt(dx2_ref[...].astype(BF16), w_ref[...])
        for h in range(HEADS):
            sl = slice(h * HEAD_DIM, (h + 1) * HEAD_DIM)
            q = p_ref[:, sl]
            k = p_ref[:, RET_W + h * HEAD_DIM:RET_W + (h + 1) * HEAD_DIM]
            v = p_ref[:, 2 * RET_W + h * HEAD_DIM:2 * RET_W + (h + 1) * HEAD_DIM]
            g = p_ref[:, 3 * RET_W + h * HEAD_DIM:3 * RET_W + (h + 1) * HEAD_DIM]
            o = o_ref[:, sl]
            rinv = lax.rsqrt(jnp.mean(o * o, axis=-1, keepdims=True) + EPS)
            oh = o * rinv
            gr = grn_ref[:, sl]
            sg = _sigmoid(g)
            dret = dmix[:, sl]
            dp_ref[:, 3 * RET_W + h * HEAD_DIM:3 * RET_W + (h + 1) * HEAD_DIM] = (
                dret * (oh * gr) * (sg * (1.0 + g * (1.0 - sg)))).astype(BF16)
            drn = dret * (g * sg)
            dgrn_ref[0:1, sl] += jnp.sum(drn * oh, axis=0, keepdims=True)
            t = drn * gr
            do = rinv * (t - oh * jnp.mean(t * oh, axis=-1, keepdims=True))
            qb, kb, vb, dob = q.astype(BF16), k.astype(BF16), v.astype(BF16), do.astype(BF16)
            m = m_ref[h]
            ab = (_dot_nt(qb, kb) * m).astype(BF16)
            dab = (_dot_nt(dob, vb) * m).astype(BF16)
            spb = sp_ref[0, h]
            dsn = dstate[h]
            dsnb = dsn.astype(BF16)
            qdb = (q * qd_ref[h]).astype(BF16)
            kdb = (k * kd_ref[h]).astype(BF16)
            dq = _dot(dab, kb) + _dot_nt(dob, spb) * qd_ref[h]
            dk = _dot_tn(dab, qb) + _dot_nt(vb, dsnb) * kd_ref[h]
            dv = _dot_tn(ab, dob) + _dot(kdb, dsnb)
            dstate[h] = dsn * cdec[h] + _dot_tn(qdb, dob)
            c2, s2 = cos_ref[...], sin_ref[...]
            dp_ref[:, sl] = _rot_t(dq, c2, s2).astype(BF16)
            dp_ref[:, RET_W + h * HEAD_DIM:RET_W + (h + 1) * HEAD_DIM] = _rot_t(dk * K_SCALE, c2, s2).astype(BF16)
            dp_ref[:, 2 * RET_W + h * HEAD_DIM:2 * RET_W + (h + 1) * HEAD_DIM] = dv.astype(BF16)
        for gi in range(HEADS):
            sl = slice(gi * HEAD_DIM, (gi + 1) * HEAD_DIM)
            u = p_ref[:, 4 * RET_W + gi * HEAD_DIM:4 * RET_W + (gi + 1) * HEAD_DIM]
            sv = p_ref[:, 4 * RET_W + SGU_W + gi * HEAD_DIM:4 * RET_W + SGU_W + (gi + 1) * HEAD_DIM]
            gv = _gelu(sv)
            xc = gv - jnp.mean(gv, axis=-1, keepdims=True)
            rstd = lax.rsqrt(jnp.mean(xc * xc, axis=-1, keepdims=True) + EPS)
            xh = xc * rstd
            lg = lng_ref[:, sl]
            vnb = (xh * lg + lnb_ref[:, sl]).astype(BF16)
            wcb = _causal(ws_ref[gi]).astype(BF16)
            mixed = _dot(wcb, vnb) + bsb_ref[gi]
            dsgu = dmix[:, RET_W + gi * HEAD_DIM:RET_W + (gi + 1) * HEAD_DIM]
            dmixed = dsgu * _gelu(u)
            dmb = dmixed.astype(BF16)
            dws_ref[gi] += _causal(_dot_nt(dmb, vnb))
            dbs_acc[gi] += dmixed
            dvn = _dot_tn(wcb, dmb)
            dlng_ref[gi:gi + 1, :] += jnp.sum(dvn * xh, axis=0, keepdims=True)
            dlnb_ref[gi:gi + 1, :] += jnp.sum(dvn, axis=0, keepdims=True)
            dxh = dvn * lg
            dgv = rstd * (dxh - jnp.mean(dxh, axis=-1, keepdims=True) - xh * jnp.mean(dxh * xh, axis=-1, keepdims=True))
            dp_ref[:, 4 * RET_W + gi * HEAD_DIM:4 * RET_W + (gi + 1) * HEAD_DIM] = (dsgu * mixed * _gelu_grad(u)).astype(BF16)
            dp_ref[:, 4 * RET_W + SGU_W + gi * HEAD_DIM:4 * RET_W + SGU_W + (gi + 1) * HEAD_DIM] = (
                dgv * _gelu_grad(sv)).astype(BF16)

        @pl.when(i == N_CHUNK - 1)
        def _():
            for gi in range(HEADS):
                col = jnp.broadcast_to(jnp.sum(dbs_acc[gi], axis=-1, keepdims=True), (CHUNK, CHUNK))
                dbs_ref[gi:gi + 1, :] = jnp.transpose(col)[0:1, :]
            rs.phase3()

    rev = lambda w: pl.BlockSpec((CHUNK, w), lambda i: (N_CHUNK - 1 - i, 0))
    hcc = (HEADS, CHUNK, CHUNK)
    acc = lambda s: pl.BlockSpec(s, lambda i: (0,) * len(s))
    res = pl.pallas_call(
        body, name="bwd_mix", grid=(N_CHUNK,),
        out_shape=(jax.ShapeDtypeStruct((SEQ, PROJ_W), BF16), jax.ShapeDtypeStruct((8, RET_W), F32),
                   jax.ShapeDtypeStruct((8, HEAD_DIM), F32), jax.ShapeDtypeStruct((8, HEAD_DIM), F32),
                   jax.ShapeDtypeStruct(hcc, F32), jax.ShapeDtypeStruct((8, CHUNK), F32)) + _scatter_out_shapes(geoms),
        in_specs=[rev(D_MODEL), rev(PROJ_W), rev(RET_W),
                  pl.BlockSpec((1, HEADS, HEAD_DIM, HEAD_DIM), lambda i: (N_CHUNK - 1 - i, 0, 0, 0)),
                  _resident((D_MODEL, D_MODEL)), _resident((1, RET_W)), _resident((1, SGU_W)), _resident((1, SGU_W)),
                  _resident(hcc), _resident(hcc), _resident(hcc), _resident(hcc), _resident(hcc), rev(HEAD_DIM), rev(HEAD_DIM)]
        + [pl.BlockSpec(memory_space=pl.ANY)] * n_h,
        out_specs=(rev(PROJ_W), acc((8, RET_W)), acc((8, HEAD_DIM)), acc((8, HEAD_DIM)), acc(hcc), acc((8, CHUNK)))
        + _scatter_out_specs(geoms),
        scratch_shapes=[pltpu.VMEM((HEADS, HEAD_DIM, HEAD_DIM), F32), pltpu.VMEM((HEADS, CHUNK, CHUNK), F32)] + _scatter_scratch(geoms),
        compiler_params=_cparams(("arbitrary",)),
    )(dx2, proj, o, sprev, wout_g, grn, lng, lnb, ws, bsb, mask, qdec, kdec, cos2, sin2, *[p for _, p in hosted])
    return res[:6] + tuple(res[6 + 2 * i] for i in range(n_h))


def _bwd_proj(dproj, win_g, x, g1, dx2, gin_p):
    geoms = [W_IN]

    def body(dp_ref, w_ref, x_ref, g_ref, dx2_ref, gin_ref, dx_ref, dg_ref, rs_out, land1, *rs_scratch):
        rs = _Scatters(geoms, [gin_ref], [rs_out, land1], rs_scratch)
        pl.when(pl.program_id(0) == 0)(rs.phase1)
        pl.when(pl.program_id(0) == 2)(rs.phase2)

        @pl.when(pl.program_id(0) == 0)
        def _():
            dg_ref[...] = jnp.zeros_like(dg_ref)

        dh = _dot_nt(dp_ref[...], w_ref[...])
        xb = x_ref[...]
        r = lax.rsqrt(jnp.mean(xb * xb, axis=-1, keepdims=True) + EPS)
        xh = xb * r
        dg_ref[0:1, :] += jnp.sum(dh * xh, axis=0, keepdims=True)
        t = dh * g_ref[...]
        dx_ref[...] = dx2_ref[...] + r * (t - xh * jnp.mean(t * xh, axis=-1, keepdims=True))
        pl.when(pl.program_id(0) == N_TB - 1)(rs.phase3)

    tok = lambda w: pl.BlockSpec((TM, w), lambda i: (i, 0))
    res = pl.pallas_call(
        body, name="bwd_proj", grid=(N_TB,),
        out_shape=(jax.ShapeDtypeStruct((SEQ, D_MODEL), F32), jax.ShapeDtypeStruct((8, D_MODEL), F32)) + _scatter_out_shapes(geoms),
        in_specs=[tok(PROJ_W), _resident((D_MODEL, PROJ_W)), tok(D_MODEL), _resident((1, D_MODEL)), tok(D_MODEL),
                  pl.BlockSpec(memory_space=pl.ANY)],
        out_specs=(tok(D_MODEL), pl.BlockSpec((8, D_MODEL), lambda i: (0, 0))) + _scatter_out_specs(geoms),
        scratch_shapes=_scatter_scratch(geoms),
        compiler_params=_cparams(("arbitrary",)),
    )(dproj, win_g, x, g1, dx2, gin_p)
    return res[:3]


def _wgrad(name, a, b, n_a, n_b, tn=None, hosted=()):
    n = max(n_a, n_b, 1)
    m_w, n_w = a.shape[-1], b.shape[-1]
    tn = n_w if tn is None else tn
    n_steps = n * (n_w // tn)
    geoms = [g for g, _ in hosted]
    n_h = len(hosted)

    def body(a_ref, b_ref, *rest):
        o_ref = rest[n_h]
        if n_h:
            rs = _Scatters(geoms, rest[:n_h], rest[n_h + 1:3 * n_h + 1], rest[3 * n_h + 1:])
            step = pl.program_id(0) * (n_w // tn) + pl.program_id(1)
            pl.when(step == 0)(rs.phase1)
            pl.when(step == 1)(rs.phase2)
        av = a_ref[0] if n_a else a_ref[...]
        bv = b_ref[0] if n_b else b_ref[...]
        o_ref[0] = _dot_tn(av.astype(BF16), bv.astype(BF16)).astype(BF16)
        if n_h:
            pl.when(step == n_steps - 1)(rs.phase3)

    a_spec = pl.BlockSpec((1, SEQ, m_w), lambda j, t: (j, 0, 0)) if n_a else pl.BlockSpec((SEQ, m_w), lambda j, t: (0, 0))
    b_spec = pl.BlockSpec((1, SEQ, tn), lambda j, t: (j, 0, t)) if n_b else pl.BlockSpec((SEQ, tn), lambda j, t: (0, t))
    o_shape = jax.ShapeDtypeStruct((n, m_w, n_w), BF16)
    o_spec = pl.BlockSpec((1, m_w, tn), lambda j, t: (j, 0, t))
    assert not n_h or n_steps >= 3
    res = pl.pallas_call(
        body, name=name, grid=(n, n_w // tn), out_shape=(o_shape,) + _scatter_out_shapes(geoms),
        in_specs=[a_spec, b_spec] + [pl.BlockSpec(memory_space=pl.ANY)] * n_h, out_specs=(o_spec,) + _scatter_out_specs(geoms),
        scratch_shapes=_scatter_scratch(geoms),
        compiler_params=_cparams(("arbitrary", "arbitrary") if n_h else ("parallel", "parallel")),
    )(a, b, *[p for _, p in hosted])
    return (res[0],) + tuple(res[1 + 2 * i] for i in range(n_h))


RS_ROWS = {W_IN: 128, W_OUT: 128, W_UP: 128, W_DOWN: 176}


class _Scatter:
    def __init__(self, geom, partial, out, land1, bufa, bufb, stage2, land2, s1_send, s1_recv, s2_send, s2_recv, ld_sems):
        self.w, self.row0, self.shape = _geom(geom)
        self.partial, self.out, self.land1 = partial, out, land1
        self.bufa, self.bufb, self.stage2, self.land2 = bufa, bufb, stage2, land2
        self.s1_send, self.s1_recv, self.s2_send, self.s2_recv, self.ld_sems = s1_send, s1_recv, s2_send, s2_recv, ld_sems
        self.x, self.y, self.c = lax.axis_index("x"), lax.axis_index("y"), lax.axis_index("c")
        self.sibling = (self.x, self.y, 1 - self.c)
        self.chips = [(self.x, self.y), (1 - self.x, self.y), (self.x, 1 - self.y), (1 - self.x, 1 - self.y)]

    def block(self, px, py, pc):
        dev = 4 * px + 2 * py + pc
        if self.w == W_IN:
            return self.partial.at[:, pl.ds(pl.multiple_of(dev * IN_SHARD, 128), IN_SHARD)]
        if self.w == W_OUT:
            return self.partial.at[pl.ds(pl.multiple_of(dev * OUT_SHARD, 128), OUT_SHARD), :]
        if self.w == W_DOWN:
            return self.partial.at[pl.ds(pl.multiple_of(dev * DOWN_SHARD, 32), DOWN_SHARD), :]
        return self.partial.at[dev, pl.ds(self.row0, self.shape[0]), :]

    def copy1(self, k):
        return pltpu.make_async_remote_copy(
            src_ref=self.block(*self.chips[k], 1 - self.c), dst_ref=self.land1.at[k],
            send_sem=self.s1_send.at[k], recv_sem=self.s1_recv.at[k], device_id=self.sibling, device_id_type=MESH)

    def copy2(self, k):
        return pltpu.make_async_remote_copy(
            src_ref=self.stage2.at[k - 1], dst_ref=self.land2.at[k - 1],
            send_sem=self.s2_send.at[k - 1], recv_sem=self.s2_recv.at[k - 1], device_id=(*self.chips[k], self.c), device_id_type=MESH)

    def _rows(self):
        step = RS_ROWS[self.w]
        return [pl.ds(r0, step) for r0 in range(0, self.shape[0], step)]

    def phase1(self):
        for k in range(4):
            self.copy1(k).start()

    def phase2(self):
        for k in (1, 2, 3, 0):
            la = pltpu.make_async_copy(self.block(*self.chips[k], self.c), self.bufa, self.ld_sems.at[0])
            la.start()
            self.copy1(k).wait_recv()
            lb = pltpu.make_async_copy(self.land1.at[k], self.bufb, self.ld_sems.at[1])
            lb.start()
            la.wait()
            lb.wait()
            for rs in self._rows():
                s = self.bufa[rs, :].astype(F32) + self.bufb[rs, :].astype(F32)
                if k == 0:
                    self.out[rs, :] = s
                else:
                    self.stage2[k - 1, rs, :] = s.astype(BF16)
            if k:
                self.copy2(k).start()

    def phase3(self):
        for k in (1, 2, 3):
            self.copy2(k).wait_recv()
        for rs in self._rows():
            self.out[rs, :] = ((self.out[rs, :] + self.land2[0, rs, :].astype(F32)) + self.land2[1, rs, :].astype(F32)) \
                + self.land2[2, rs, :].astype(F32)
        for k in range(4):
            self.copy1(k).wait_send()
        for k in (1, 2, 3):
            self.copy2(k).wait_send()


def _geom(geom):
    if isinstance(geom, tuple):
        w, row0, rows = geom
        assert w == W_UP
        return w, row0, (rows, SHARD[w][1])
    return geom, 0, SHARD[geom]


N_SCATTER_SCRATCH = 9


def _scatter_out_shapes(geoms):
    out = ()
    for g in geoms:
        s = _geom(g)[2]
        out += (jax.ShapeDtypeStruct(s, F32), jax.ShapeDtypeStruct((4,) + s, BF16))
    return out


def _scatter_out_specs(geoms):
    return (pl.BlockSpec(memory_space=pltpu.VMEM), pl.BlockSpec(memory_space=pl.ANY)) * len(geoms)


def _scatter_scratch(geoms):
    out = []
    for g in geoms:
        s = _geom(g)[2]
        out += [pltpu.VMEM(s, BF16), pltpu.VMEM(s, BF16), pltpu.VMEM((3,) + s, BF16), pltpu.VMEM((3,) + s, BF16),
                pltpu.SemaphoreType.DMA((4,)), pltpu.SemaphoreType.DMA((4,)), pltpu.SemaphoreType.DMA((3,)),
                pltpu.SemaphoreType.DMA((3,)), pltpu.SemaphoreType.DMA((2,))]
    return out


class _Scatters:
    def __init__(self, geoms, p_refs, out_refs, scratch):
        k = N_SCATTER_SCRATCH
        self.items = [_Scatter(g, p_refs[i], out_refs[2 * i], out_refs[2 * i + 1], *scratch[k * i:k * i + k])
                      for i, g in enumerate(geoms)]

    def phase1(self):
        for s in self.items:
            s.phase1()

    def phase2(self):
        for s in self.items:
            s.phase2()

    def phase3(self):
        for s in self.items:
            s.phase3()


PACK_W = 1024


def _all_reduce_small(dg1, dg2, dgf, dgrn, dlng, dlnb, dbs, loss_parts, dws, dcv):
    n_a = 3

    def body(dg1_ref, dg2_ref, dgf_ref, dgrn_ref, dlng_ref, dlnb_ref, dbs_ref, loss_ref, dws_ref, dcv_ref,
             rp_ref, rws_ref, rcv_ref, pack, rx_p, rx_ws, rx_cv, cs_p, cs_ws, cs_cv, g_p, g_ws, g_cv,
             s1_send, s1_recv, s2_send, s2_recv, s3_send, s3_recv):
        x, y, c = lax.axis_index("x"), lax.axis_index("y"), lax.axis_index("c")
        sibling = (x, y, 1 - c)
        chips = [(1 - x, y), (x, 1 - y), (1 - x, 1 - y)]
        pack[...] = jnp.zeros_like(pack)
        pack[0, 0:1, :] = dg1_ref[0:1, :]
        pack[0, 1:2, :] = dg2_ref[0:1, :]
        pack[0, 2:3, :] = dgf_ref[0:1, :]
        pack[0, 3:4, 0:RET_W] = dgrn_ref[0:1, :]
        lsum = loss_ref[0, 0:1, :]
        for i in range(1, N_TB):
            lsum = lsum + loss_ref[i, 0:1, :]
        pack[0, 3:4, RET_W:RET_W + 128] = lsum
        pack[1, 0:HEADS, 0:128] = dlng_ref[0:HEADS, :]
        pack[1, 0:HEADS, 128:256] = dlnb_ref[0:HEADS, :]
        pack[1, 0:HEADS, 256:384] = dbs_ref[0:HEADS, :]

        srcs = [pack, dws_ref, dcv_ref]
        outs = [rp_ref, rws_ref, rcv_ref]
        rxs = [rx_p, rx_ws, rx_cv]
        css = [cs_p, cs_ws, cs_cv]
        gs = [g_p, g_ws, g_cv]
        hl = [1, HEADS // 2, N_DEV // 2]

        def half(ref, a, h):
            return ref.at[pl.ds(h * hl[a], hl[a])]

        ex1 = [pltpu.make_async_remote_copy(src_ref=half(srcs[a], a, 1 - c), dst_ref=rxs[a], send_sem=s1_send.at[a],
                                            recv_sem=s1_recv.at[a], device_id=sibling, device_id_type=MESH) for a in range(n_a)]
        for cp in ex1:
            cp.start()
        ex2 = []
        for a in range(n_a):
            ex1[a].wait_recv()
            css[a][...] = half(srcs[a], a, c)[...] + rxs[a][...]
            for j, chip in enumerate(chips):
                cp = pltpu.make_async_remote_copy(src_ref=css[a], dst_ref=gs[a].at[j], send_sem=s2_send.at[a, j],
                                                  recv_sem=s2_recv.at[a, j], device_id=(*chip, c), device_id_type=MESH)
                cp.start()
                ex2.append(cp)
        ex3 = []
        for a in range(n_a):
            for j in range(3):
                ex2[3 * a + j].wait_recv()
            tot = None
            for q in range(4):
                k = jnp.where(x != (q >> 1), 1, 0) + jnp.where(y != (q & 1), 2, 0)
                term = jnp.where(k == 0, css[a][...], jnp.where(k == 1, gs[a][0], jnp.where(k == 2, gs[a][1], gs[a][2])))
                tot = term if tot is None else tot + term
            half(outs[a], a, c)[...] = tot
            cp = pltpu.make_async_remote_copy(src_ref=half(outs[a], a, c), dst_ref=half(outs[a], a, c), send_sem=s3_send.at[a],
                                              recv_sem=s3_recv.at[a], device_id=sibling, device_id_type=MESH)
            cp.start()
            ex3.append(cp)
        for a in range(n_a):
            pltpu.make_async_remote_copy(src_ref=half(outs[a], a, 1 - c), dst_ref=half(outs[a], a, 1 - c), send_sem=s3_send.at[a],
                                         recv_sem=s3_recv.at[a], device_id=sibling, device_id_type=MESH).wait_recv()
        for cp in ex1 + ex2 + ex3:
            cp.wait_send()

    vm = pl.BlockSpec(memory_space=pltpu.VMEM)
    full = [(2, 8, PACK_W), (HEADS, CHUNK, CHUNK), (N_DEV, 8, FF_SHARD)]
    halves = [(s[0] // 2,) + s[1:] for s in full]
    return pl.pallas_call(
        body, name="ar_small",
        out_shape=tuple(jax.ShapeDtypeStruct(s, F32) for s in full),
        in_specs=[vm] * 10, out_specs=(vm,) * 3,
        scratch_shapes=[pltpu.VMEM(full[0], F32)] + [pltpu.VMEM(s, F32) for s in halves] + [pltpu.VMEM(s, F32) for s in halves]
        + [pltpu.VMEM((3,) + s, F32) for s in halves]
        + [pltpu.SemaphoreType.DMA((n_a,)), pltpu.SemaphoreType.DMA((n_a,)), pltpu.SemaphoreType.DMA((n_a, 3)),
           pltpu.SemaphoreType.DMA((n_a, 3)), pltpu.SemaphoreType.DMA((n_a,)), pltpu.SemaphoreType.DMA((n_a,))],
        compiler_params=_cparams(),
    )(dg1, dg2, dgf, dgrn, dlng, dlnb, dbs, loss_parts, dws, dcv)


def _adam_math(w, g, m, v):
    nm = ADAM_B1 * m + (1.0 - ADAM_B1) * g
    nv = ADAM_B2 * v + (1.0 - ADAM_B2) * (g * g)
    d = -ADAM_LR * ((nm / (1.0 - ADAM_B1 ** ADAM_STEP)) / (jnp.sqrt(nv / (1.0 - ADAM_B2 ** ADAM_STEP)) + ADAM_EPS) + ADAM_WD * w)
    return d, nm, nv


def _adamw(name, w, g, m, v, rows):
    _, r, cdim = w.shape

    def body(w_ref, g_ref, m_ref, v_ref, go_ref, d_ref, nm_ref, nv_ref):
        gg = g_ref[...]
        go_ref[0] = gg
        d, nm, nv = _adam_math(w_ref[0], gg, m_ref[0], v_ref[0])
        d_ref[0], nm_ref[0], nv_ref[0] = d, nm, nv

    spec3 = pl.BlockSpec((1, rows, cdim), lambda i: (0, i, 0))
    sh = jax.ShapeDtypeStruct((1, r, cdim), F32)
    return pl.pallas_call(
        body, name=name, grid=(r // rows,), out_shape=(sh, sh, sh, sh),
        in_specs=[spec3, pl.BlockSpec((rows, cdim), lambda i: (i, 0)), spec3, spec3], out_specs=(spec3,) * 4,
        compiler_params=_cparams(("parallel",)),
    )(w, g, m, v)


def _adamw_small(rp, rws, rcv, params):
    n_p = len(params)

    def body(*refs):
        rp_ref, rws_ref, rcv_ref = refs[:3]
        ins = refs[3:3 + 3 * n_p]
        outs = refs[3 + 3 * n_p:]
        me = 4 * lax.axis_index("x") + 2 * lax.axis_index("y") + lax.axis_index("c")
        grads = [rp_ref[0, 0:1, :], rp_ref[0, 1:2, :], rp_ref[0, 2:3, :], rp_ref[0, 3:4, 0:RET_W],
                 rp_ref[1, 0:HEADS, 0:128], rp_ref[1, 0:HEADS, 128:256], rp_ref[1, 0:HEADS, 256:384],
                 rws_ref[...], rcv_ref[me][0:3, :], None]
        for p in range(n_p):
            w_ref, m_ref, v_ref = ins[3 * p:3 * p + 3]
            o = outs[4 * p:4 * p + 4]
            if p == n_p - 1:
                for j in range(N_DEV):
                    g = rcv_ref[j, 3:4, :]
                    res = (g,) + _adam_math(w_ref[j:j + 1, :], g, m_ref[j:j + 1, :], v_ref[j:j + 1, :])
                    for t in range(4):
                        o[t][j:j + 1, :] = res[t]
                continue
            lead = w_ref.ndim > grads[p].ndim
            rd = (lambda r: r[0]) if lead else (lambda r: r[...])
            res = (grads[p],) + _adam_math(rd(w_ref), grads[p], rd(m_ref), rd(v_ref))
            for t in range(4):
                if lead:
                    o[t][0] = res[t]
                else:
                    o[t][...] = res[t]

    vm = pl.BlockSpec(memory_space=pltpu.VMEM)
    flat = [a for tr in params for a in tr]
    out_shape = tuple(jax.ShapeDtypeStruct(tr[0].shape, F32) for tr in params for _ in range(4))
    res = pl.pallas_call(
        body, name="adamw_small", out_shape=out_shape, in_specs=[vm] * (3 + len(flat)), out_specs=(vm,) * len(out_shape),
        compiler_params=_cparams(),
    )(rp, rws, rcv, *flat)
    return [res[4 * p:4 * p + 4] for p in range(n_p)]


def kernel(x, mix_norm_g, w_in, ret_norm_g, sgu_ln_g, sgu_ln_b, sgu_w_s, sgu_b_s, w_out, ffn_norm_g, w_up, conv_w, conv_b, w_down, final_norm_g, loss_target, m_mix_norm_g, m_w_in, m_ret_norm_g, m_sgu_ln_g, m_sgu_ln_b, m_sgu_w_s, m_sgu_b_s, m_w_out, m_ffn_norm_g, m_w_up, m_conv_w, m_conv_b, m_w_down, m_final_norm_g, v_mix_norm_g, v_w_in, v_ret_norm_g, v_sgu_ln_g, v_sgu_ln_b, v_sgu_w_s, v_sgu_b_s, v_w_out, v_ffn_norm_g, v_w_up, v_conv_w, v_conv_b, v_w_down, v_final_norm_g):
    xs = x[0]
    tgt = loss_target[0]
    cos2, sin2 = _rope_tables()
    mask, qdec, kdec = _decay_tables()
    grn = ret_norm_g.reshape(1, RET_W)
    lng = sgu_ln_g.reshape(1, SGU_W)
    lnb = sgu_ln_b.reshape(1, SGU_W)
    ws = sgu_w_s[0]
    bsb = jnp.broadcast_to(sgu_b_s[0][:, :, None], (HEADS, CHUNK, HEAD_DIM))
    gf = final_norm_g.reshape(1, D_MODEL)
    cb_g = conv_b.reshape(N_DEV, 1, FF_SHARD)

    win_g, cw_g, so, su, sd = _ag_first(w_in[0], w_out[0], w_up[0], w_down[0], conv_w[0])

    proj, h1, wout_g, wdn_g = _fwd_proj(xs, mix_norm_g, win_g, cos2, sin2, so, sd)
    x2, mixcat, o, sprev, wup_g = _fwd_mix(xs, proj, wout_g, grn, lng, lnb, ws, bsb, mask, qdec, kdec, su)
    h2, up_pre, u_conv, act, x3, loss_parts = _fwd_ffn(x2, ffn_norm_g, wup_g, cw_g, cb_g, wdn_g, gf, tgt)

    dx3, dpre, dx2, dgf, dg2, dcv = _bwd_ffn(x3, tgt, gf, x2, ffn_norm_g, up_pre, u_conv, wup_g, cw_g, wdn_g)
    half = D_MODEL // 2
    gdn_p = _wgrad("wgrad_down", act, dx3, N_FF_PAIR, 0)[0].reshape(D_FF, D_MODEL)
    gout_p = _wgrad("wgrad_out", mixcat, dx2, 0, 0, tn=512)[0][0]
    gup_p, g_dn = _wgrad("wgrad_up", h2, dpre, 0, N_DEV, hosted=[(W_DOWN, gdn_p)])
    dproj, dgrn, dlng, dlnb, dws, dbs, g_up_a, g_out = _bwd_mix(
        dx2, proj, o, sprev, wout_g, grn, lng, lnb, ws, bsb, mask, qdec, kdec, cos2, sin2,
        [((W_UP, 0, half), gup_p), (W_OUT, gout_p)])
    gin_p, g_up_b = _wgrad("wgrad_in", h1, dproj, 0, 0, tn=768, hosted=[((W_UP, half, half), gup_p)])
    grad_x, dg1, g_in = _bwd_proj(dproj, win_g, xs, mix_norm_g, dx2, gin_p[0])
    g_up = jnp.concatenate([g_up_a, g_up_b], axis=0)
    rp, rws, rcv = _all_reduce_small(dg1, dg2, dgf, dgrn, dlng, dlnb, dbs, loss_parts, dws, dcv)
    loss = rp[0, 3, RET_W]

    table = {}
    for name, w, g, m, v, rows in (("w_in", w_in, g_in, m_w_in, v_w_in, 256), ("w_out", w_out, g_out, m_w_out, v_w_out, 128),
                                   ("w_up", w_up, g_up, m_w_up, v_w_up, 256), ("w_down", w_down, g_dn, m_w_down, v_w_down, 88)):
        table[name] = _adamw("adamw_" + name, w, g, m, v, rows)
    row = lambda a: a.reshape(1, D_MODEL)
    slab = lambda a: a.reshape(N_DEV, FF_SHARD)
    names_small = ["mix_norm_g", "ffn_norm_g", "final_norm_g", "ret_norm_g", "sgu_ln_g", "sgu_ln_b", "sgu_b_s", "sgu_w_s",
                   "conv_w", "conv_b"]
    params = [(mix_norm_g, m_mix_norm_g, v_mix_norm_g), (ffn_norm_g, m_ffn_norm_g, v_ffn_norm_g),
              (row(final_norm_g), row(m_final_norm_g), row(v_final_norm_g)), (ret_norm_g, m_ret_norm_g, v_ret_norm_g),
              (sgu_ln_g, m_sgu_ln_g, v_sgu_ln_g), (sgu_ln_b, m_sgu_ln_b, v_sgu_ln_b), (sgu_b_s, m_sgu_b_s, v_sgu_b_s),
              (sgu_w_s, m_sgu_w_s, v_sgu_w_s), (conv_w, m_conv_w, v_conv_w), (slab(conv_b), slab(m_conv_b), slab(v_conv_b))]
    for n, res in zip(names_small, _adamw_small(rp, rws, rcv, params)):
        table[n] = res
    table["final_norm_g"] = tuple(a.reshape(D_MODEL) for a in table["final_norm_g"])
    table["conv_b"] = tuple(a.reshape(1, 2 * D_FF) for a in table["conv_b"])

    order = ["mix_norm_g", "w_in", "ret_norm_g", "sgu_ln_g", "sgu_ln_b", "sgu_w_s", "sgu_b_s", "w_out", "ffn_norm_g", "w_up",
             "conv_w", "conv_b", "w_down", "final_norm_g"]
    outs = [loss, grad_x[None]]
    for col in range(4):
        outs += [table[n][col] for n in order]
    return tuple(outs)
```

```python
import functools
import math

import jax
import jax.numpy as jnp
import numpy as np
from jax import lax
from jax.experimental import pallas as pl
from jax.experimental.pallas import tpu as pltpu

F32 = jnp.float32
BF16 = jnp.bfloat16
MESH = pl.DeviceIdType.MESH

N_DEV = 8
SEQ = 2048
D_MODEL = 1024
CHUNK = 128
N_CHUNK = SEQ // CHUNK
HEADS = 4
HEAD_DIM = 128
RET_W = 512
SGU_W = 512
PROJ_W = 3072
D_FF = 2816
FF_SHARD = 704
N_FF_PAIR = 4
IN_SHARD = PROJ_W // N_DEV
OUT_SHARD = D_MODEL // N_DEV
DOWN_SHARD = D_FF // N_DEV
TM = 256
N_TB = SEQ // TM
EPS = 1e-6
ROPE_BASE = 10000.0
K_SCALE = HEAD_DIM ** -0.5
INV_SQRT2 = 0.7071067811865476
INV_SQRT_2PI = 0.3989422804014327

ADAM_LR = 0.001
ADAM_B1 = 0.9
ADAM_B2 = 0.999
ADAM_EPS = 1e-08
ADAM_WD = 0.01
ADAM_STEP = 10

VMEM_LIMIT = 56 * 1024 * 1024


def _cparams(sem=None, vmem=VMEM_LIMIT):
    return pltpu.CompilerParams(dimension_semantics=sem, vmem_limit_bytes=vmem)


def _resident(shape):
    nd = len(shape)
    return pl.BlockSpec(shape, lambda *_: (0,) * nd, pipeline_mode=pl.Buffered(1))


def _dot(a, b):
    return jnp.dot(a, b, preferred_element_type=F32)


def _dot_nt(a, b):
    return lax.dot_general(a, b, (((1,), (1,)), ((), ())), preferred_element_type=F32)


def _dot_tn(a, b):
    return lax.dot_general(a, b, (((0,), (0,)), ((), ())), preferred_element_type=F32)


def _sigmoid(x):
    return 1.0 / (1.0 + jnp.exp(-x))


def _gelu(x):
    return 0.5 * x * (1.0 + lax.erf(x * INV_SQRT2))


def _gelu_grad(x):
    return 0.5 * (1.0 + lax.erf(x * INV_SQRT2)) + x * (jnp.exp(-0.5 * x * x) * INV_SQRT_2PI)


def _rot(xh, cos2, sin2):
    return xh * cos2 + pltpu.roll(xh, HEAD_DIM // 2, 1) * sin2


def _rot_t(dh, cos2, sin2):
    return dh * cos2 + pltpu.roll(dh * sin2, HEAD_DIM // 2, 1)


def _rope_tables():
    half = HEAD_DIM // 2
    inv_freq = jnp.power(ROPE_BASE, -jnp.arange(half, dtype=F32) / half)
    ang = jnp.arange(SEQ, dtype=F32)[:, None] * inv_freq[None, :]
    cos, sin = jnp.cos(ang), jnp.sin(ang)
    cos2 = jnp.concatenate([cos, cos], axis=-1)
    sin2 = jnp.concatenate([-sin, sin], axis=-1)
    return cos2, sin2


def _decay_tables():
    log_gamma = jnp.log(1.0 - jnp.power(2.0, -5.0 - jnp.arange(HEADS, dtype=F32)))
    pos = jnp.arange(CHUNK, dtype=F32)
    diff = pos[:, None] - pos[None, :]
    mask = jnp.where(diff >= 0.0, jnp.exp(log_gamma[:, None, None] * jnp.maximum(diff, 0.0)[None]), 0.0)
    k_decay = jnp.exp(log_gamma[:, None] * (CHUNK - 1.0 - pos)[None])
    q_decay = jnp.exp(log_gamma[:, None] * (pos + 1.0)[None])
    kd = jnp.broadcast_to(k_decay[:, :, None], (HEADS, CHUNK, HEAD_DIM))
    qd = jnp.broadcast_to(q_decay[:, :, None], (HEADS, CHUNK, HEAD_DIM))
    return mask.astype(F32), qd.astype(F32), kd.astype(F32)


def _chunk_decay():
    lg = np.log(np.float32(1.0) - np.power(np.float32(2.0), -5.0 - np.arange(HEADS, dtype=np.float32))).astype(np.float32)
    return [float(np.exp(lg[h] * np.float32(CHUNK))) for h in range(HEADS)]


W_IN, W_OUT, W_UP, W_DOWN, W_CONV = range(5)
GATHERED = {W_IN: ((D_MODEL, PROJ_W), BF16), W_OUT: ((D_MODEL, D_MODEL), BF16), W_UP: ((N_DEV, D_MODEL, FF_SHARD), BF16),
            W_DOWN: ((D_FF, D_MODEL), BF16), W_CONV: ((N_DEV, 8, FF_SHARD), F32)}
SHARD = {W_IN: (D_MODEL, IN_SHARD), W_OUT: (OUT_SHARD, D_MODEL), W_UP: (D_MODEL, FF_SHARD), W_DOWN: (DOWN_SHARD, D_MODEL),
         W_CONV: (8, FF_SHARD)}


class _Gather:
    def __init__(self, ids, stages, gathered, send_sems, recv_sems, local_sems):
        self.ids, self.stages, self.gathered = ids, stages, gathered
        self.send_sems, self.recv_sems, self.local_sems = send_sems, recv_sems, local_sems
        self.x, self.y, self.c = lax.axis_index("x"), lax.axis_index("y"), lax.axis_index("c")
        self.me = (self.x, self.y, self.c)
        self.sibling = (self.x, self.y, 1 - self.c)
        self.chips = [(1 - self.x, self.y), (self.x, 1 - self.y), (1 - self.x, 1 - self.y)]

    def slot(self, n, px, py, pc):
        dev = 4 * px + 2 * py + pc
        w, g = self.ids[n], self.gathered[n]
        if w == W_IN:
            return g.at[:, pl.ds(pl.multiple_of(dev * IN_SHARD, 128), IN_SHARD)]
        if w == W_OUT:
            return g.at[pl.ds(pl.multiple_of(dev * OUT_SHARD, 128), OUT_SHARD), :]
        if w == W_DOWN:
            return g.at[pl.ds(pl.multiple_of(dev * DOWN_SHARD, 32), DOWN_SHARD), :]
        return g.at[dev]

    def copy(self, n, k, block, to, src=None):
        return pltpu.make_async_remote_copy(
            src_ref=self.slot(n, *block) if src is None else src, dst_ref=self.slot(n, *block),
            send_sem=self.send_sems.at[n, k], recv_sem=self.recv_sems.at[n, k], device_id=to, device_id_type=MESH)

    def _mine(self):
        return [pltpu.make_async_copy(self.stages[n], self.slot(n, *self.me), self.local_sems.at[n]) for n in range(len(self.ids))]

    def _first(self):
        out = []
        for n in range(len(self.ids)):
            out.append(self.copy(n, 0, self.me, self.sibling, src=self.stages[n]))
            out += [self.copy(n, 1 + j, self.me, (*chip, self.c), src=self.stages[n]) for j, chip in enumerate(self.chips)]
        return out

    def start(self):
        for cp in self._mine() + self._first():
            cp.start()

    def finish(self):
        passed = []
        for n in range(len(self.ids)):
            for j, chip in enumerate(self.chips):
                self.copy(n, 1 + j, (*chip, self.c), self.me).wait_recv()
                fwd = self.copy(n, 4 + j, (*chip, self.c), self.sibling)
                fwd.start()
                passed.append(fwd)
        for n in range(len(self.ids)):
            self.copy(n, 0, self.sibling, self.me).wait_recv()
            for j, chip in enumerate(self.chips):
                self.copy(n, 4 + j, (*chip, 1 - self.c), self.me).wait_recv()
        for cp in self._first() + passed:
            cp.wait_send()
        for cp in self._mine():
            cp.wait()


def _gather_scratch(n):
    return [pltpu.SemaphoreType.DMA((n, 7)), pltpu.SemaphoreType.DMA((n, 7)), pltpu.SemaphoreType.DMA((n,))]


def _gathered_shapes(ids):
    return tuple(jax.ShapeDtypeStruct(*GATHERED[w]) for w in ids)


def _ag_first(w_in, w_out, w_up, w_down, conv_w):
    ids = [W_IN, W_CONV]

    def body(in_ref, out_ref, up_ref, dn_ref, cw_ref, gin, gcw, so_ref, su_ref, sd_ref, s_in, s_cw, send_sems, recv_sems, local_sems):
        s_in[...] = in_ref[...].astype(BF16)
        s_cw[...] = jnp.zeros_like(s_cw)
        s_cw[0:3, :] = cw_ref[...]
        ag = _Gather(ids, [s_in, s_cw], [gin, gcw], send_sems, recv_sems, local_sems)
        ag.start()
        so_ref[...] = out_ref[...].astype(BF16)
        su_ref[...] = up_ref[...].astype(BF16)
        sd_ref[...] = dn_ref[...].astype(BF16)
        ag.finish()

    vm = pl.BlockSpec(memory_space=pltpu.VMEM)
    hbm = pl.BlockSpec(memory_space=pl.ANY)
    return pl.pallas_call(
        body, name="ag_first",
        out_shape=_gathered_shapes(ids) + tuple(jax.ShapeDtypeStruct(SHARD[w], BF16) for w in (W_OUT, W_UP, W_DOWN)),
        in_specs=[vm] * 5, out_specs=(hbm, hbm, vm, vm, vm),
        scratch_shapes=[pltpu.VMEM(SHARD[W_IN], BF16), pltpu.VMEM(SHARD[W_CONV], F32)] + _gather_scratch(len(ids)),
        compiler_params=_cparams(),
    )(w_in, w_out, w_up, w_down, conv_w)


def _fwd_proj(x, g1, win_g, cos2, sin2, so, sd):
    ids = [W_OUT, W_DOWN]

    def body(x_ref, g_ref, w_ref, cos_ref, sin_ref, so_ref, sd_ref, proj_ref, h1_ref, gout, gdn, send_sems, recv_sems, local_sems):
        ag = _Gather(ids, [so_ref, sd_ref], [gout, gdn], send_sems, recv_sems, local_sems)

        @pl.when(pl.program_id(0) == 0)
        def _():
            ag.start()

        xb = x_ref[...]
        r = lax.rsqrt(jnp.mean(xb * xb, axis=-1, keepdims=True) + EPS)
        h = ((xb * r) * g_ref[...]).astype(BF16)
        h1_ref[...] = h
        p = _dot(h, w_ref[...])
        for hd in range(HEADS):
            sl = slice(hd * HEAD_DIM, (hd + 1) * HEAD_DIM)
            c2, s2 = cos_ref[...], sin_ref[...]
            proj_ref[:, sl] = _rot(p[:, sl], c2, s2)
            ks = slice(RET_W + hd * HEAD_DIM, RET_W + (hd + 1) * HEAD_DIM)
            proj_ref[:, ks] = _rot(p[:, ks], c2, s2) * K_SCALE
        proj_ref[:, 2 * RET_W:] = p[:, 2 * RET_W:]

        @pl.when(pl.program_id(0) == N_TB - 1)
        def _():
            ag.finish()

    tok = lambda w: pl.BlockSpec((TM, w), lambda i: (i, 0))
    hbm = pl.BlockSpec(memory_space=pl.ANY)
    return pl.pallas_call(
        body, name="fwd_proj", grid=(N_TB,),
        out_shape=(jax.ShapeDtypeStruct((SEQ, PROJ_W), F32), jax.ShapeDtypeStruct((SEQ, D_MODEL), BF16)) + _gathered_shapes(ids),
        in_specs=[tok(D_MODEL), _resident((1, D_MODEL)), _resident((D_MODEL, PROJ_W)), tok(HEAD_DIM), tok(HEAD_DIM), hbm, hbm],
        out_specs=(tok(PROJ_W), tok(D_MODEL), hbm, hbm),
        scratch_shapes=_gather_scratch(len(ids)),
        compiler_params=_cparams(("arbitrary",)),
    )(x, g1, win_g, cos2, sin2, so, sd)


def _causal(w):
    r = lax.broadcasted_iota(jnp.int32, (CHUNK, CHUNK), 0)
    c = lax.broadcasted_iota(jnp.int32, (CHUNK, CHUNK), 1)
    return jnp.where(r >= c, w, 0.0)


def _fwd_mix(x, proj, wout_g, grn, lng, lnb, ws, bsb, mask, qdec, kdec, su):
    cdec = _chunk_decay()
    ids = [W_UP]

    def body(x_ref, p_ref, w_ref, grn_ref, lng_ref, lnb_ref, ws_ref, bsb_ref, m_ref, qd_ref, kd_ref, su_ref,
             x2_ref, cat_ref, o_ref, sp_ref, gup, state, send_sems, recv_sems, local_sems):
        ag = _Gather(ids, [su_ref], [gup], send_sems, recv_sems, local_sems)

        @pl.when(pl.program_id(0) == 0)
        def _():
            state[...] = jnp.zeros_like(state)
            ag.start()

        for h in range(HEADS):
            sl = slice(h * HEAD_DIM, (h + 1) * HEAD_DIM)
            q = p_ref[:, sl]
            k = p_ref[:, RET_W + h * HEAD_DIM:RET_W + (h + 1) * HEAD_DIM]
            v = p_ref[:, 2 * RET_W + h * HEAD_DIM:2 * RET_W + (h + 1) * HEAD_DIM]
            g = p_ref[:, 3 * RET_W + h * HEAD_DIM:3 * RET_W + (h + 1) * HEAD_DIM]
            qb, kb, vb = q.astype(BF16), k.astype(BF16), v.astype(BF16)
            a = _dot_nt(qb, kb) * m_ref[h]
            spb = state[h].astype(BF16)
            sp_ref[0, h] = spb
            o = _dot(a.astype(BF16), vb) + _dot((q * qd_ref[h]).astype(BF16), spb)
            state[h] = state[h] * cdec[h] + _dot_tn((k * kd_ref[h]).astype(BF16), vb)
            o_ref[:, sl] = o
            rinv = lax.rsqrt(jnp.mean(o * o, axis=-1, keepdims=True) + EPS)
            rn = (o * rinv) * grn_ref[:, sl]
            cat_ref[:, sl] = ((g * _sigmoid(g)) * rn).astype(BF16)
        for gi in range(HEADS):
            sl = slice(gi * HEAD_DIM, (gi + 1) * HEAD_DIM)
            u = p_ref[:, 4 * RET_W + gi * HEAD_DIM:4 * RET_W + (gi + 1) * HEAD_DIM]
            sv = p_ref[:, 4 * RET_W + SGU_W + gi * HEAD_DIM:4 * RET_W + SGU_W + (gi + 1) * HEAD_DIM]
            gv = _gelu(sv)
            xc = gv - jnp.mean(gv, axis=-1, keepdims=True)
            vn = (xc * lax.rsqrt(jnp.mean(xc * xc, axis=-1, keepdims=True) + EPS)) * lng_ref[:, sl] + lnb_ref[:, sl]
            mixed = _dot(_causal(ws_ref[gi]).astype(BF16), vn.astype(BF16)) + bsb_ref[gi]
            cat_ref[:, RET_W + gi * HEAD_DIM:RET_W + (gi + 1) * HEAD_DIM] = (_gelu(u) * mixed).astype(BF16)
        x2_ref[...] = x_ref[...] + _dot(cat_ref[...], w_ref[...])

        @pl.when(pl.program_id(0) == N_CHUNK - 1)
        def _():
            ag.finish()

    ch = lambda w: pl.BlockSpec((CHUNK, w), lambda i: (i, 0))
    hcc = (HEADS, CHUNK, CHUNK)
    hbm = pl.BlockSpec(memory_space=pl.ANY)
    return pl.pallas_call(
        body, name="fwd_mix", grid=(N_CHUNK,),
        out_shape=(jax.ShapeDtypeStruct((SEQ, D_MODEL), F32), jax.ShapeDtypeStruct((SEQ, D_MODEL), BF16),
                   jax.ShapeDtypeStruct((SEQ, RET_W), F32), jax.ShapeDtypeStruct((N_CHUNK, HEADS, HEAD_DIM, HEAD_DIM), BF16))
        + _gathered_shapes(ids),
        in_specs=[ch(D_MODEL), ch(PROJ_W), _resident((D_MODEL, D_MODEL)), _resident((1, RET_W)), _resident((1, SGU_W)),
                  _resident((1, SGU_W)), _resident(hcc), _resident(hcc), _resident(hcc), _resident(hcc), _resident(hcc), hbm],
        out_specs=(ch(D_MODEL), ch(D_MODEL), ch(RET_W), pl.BlockSpec((1, HEADS, HEAD_DIM, HEAD_DIM), lambda i: (i, 0, 0, 0)), hbm),
        scratch_shapes=[pltpu.VMEM((HEADS, HEAD_DIM, HEAD_DIM), F32)] + _gather_scratch(len(ids)),
        compiler_params=_cparams(("arbitrary",)),
    )(x, proj, wout_g, grn, lng, lnb, ws, bsb, mask, qdec, kdec, su)


def _conv_taps(p, prev8):
    row = lax.broadcasted_iota(jnp.int32, p.shape, 0)
    p1 = jnp.where(row == 0, prev8[7:8, :], pltpu.roll(p, 1, 0))
    p2 = jnp.where(row == 0, prev8[6:7, :], jnp.where(row == 1, prev8[7:8, :], pltpu.roll(p, 2, 0)))
    return p1, p2


def _fwd_ffn(x2, g2, wup_g, cw_g, cb_g, wdn_g, gf, tgt):
    def body(x_ref, g_ref, wu_ref, cw_ref, cb_ref, wd_ref, gf_ref, t_ref, h2_ref, up_ref, u_ref, act_ref, x3_ref, loss_ref, carry):
        @pl.when(pl.program_id(0) == 0)
        def _():
            carry[...] = jnp.zeros_like(carry)

        xb = x_ref[...]
        r = lax.rsqrt(jnp.mean(xb * xb, axis=-1, keepdims=True) + EPS)
        h = ((xb * r) * g_ref[...]).astype(BF16)
        h2_ref[...] = h
        acc = xb
        for j in range(N_FF_PAIR):
            u = []
            for s in (j, j + N_FF_PAIR):
                p = _dot(h, wu_ref[s])
                up_ref[s] = p.astype(BF16)
                p1, p2 = _conv_taps(p, carry[s])
                carry[s] = p[TM - 8:, :]
                cw = cw_ref[s]
                us = p2 * cw[0:1, :] + p1 * cw[1:2, :] + p * cw[2:3, :] + cb_ref[s]
                u_ref[s] = us.astype(BF16)
                u.append(us)
            a = ((u[0] * _sigmoid(u[0])) * u[1]).astype(BF16)
            act_ref[j] = a
            acc = acc + _dot(a, wd_ref[pl.ds(j * FF_SHARD, FF_SHARD), :])
        x3_ref[...] = acc
        r3 = lax.rsqrt(jnp.mean(acc * acc, axis=-1, keepdims=True) + EPS)
        diff = (acc * r3) * gf_ref[...] - t_ref[...]
        loss_ref[...] = jnp.full(loss_ref.shape, 0.5 * jnp.sum(jnp.mean(diff * diff, axis=-1)), F32)

    tok = lambda w: pl.BlockSpec((TM, w), lambda i: (i, 0))
    return pl.pallas_call(
        body, name="fwd_ffn", grid=(N_TB,),
        out_shape=(jax.ShapeDtypeStruct((SEQ, D_MODEL), BF16), jax.ShapeDtypeStruct((N_DEV, SEQ, FF_SHARD), BF16),
                   jax.ShapeDtypeStruct((N_DEV, SEQ, FF_SHARD), BF16),
                   jax.ShapeDtypeStruct((N_FF_PAIR, SEQ, FF_SHARD), BF16), jax.ShapeDtypeStruct((SEQ, D_MODEL), F32),
                   jax.ShapeDtypeStruct((N_TB, 8, 128), F32)),
        in_specs=[tok(D_MODEL), _resident((1, D_MODEL)), _resident((N_DEV, D_MODEL, FF_SHARD)), _resident((N_DEV, 8, FF_SHARD)),
                  _resident((N_DEV, 1, FF_SHARD)), _resident((D_FF, D_MODEL)), _resident((1, D_MODEL)), tok(D_MODEL)],
        out_specs=(tok(D_MODEL), pl.BlockSpec((N_DEV, TM, FF_SHARD), lambda i: (0, i, 0)),
                   pl.BlockSpec((N_DEV, TM, FF_SHARD), lambda i: (0, i, 0)),
                   pl.BlockSpec((N_FF_PAIR, TM, FF_SHARD), lambda i: (0, i, 0)), tok(D_MODEL),
                   pl.BlockSpec((1, 8, 128), lambda i: (i, 0, 0))),
        scratch_shapes=[pltpu.VMEM((N_DEV, 8, FF_SHARD), F32)],
        compiler_params=_cparams(("arbitrary",)),
    )(x2, g2, wup_g, cw_g, cb_g, wdn_g, gf, tgt)


def _bwd_ffn(x3, tgt, gf, x2, g2, up_pre, u_conv, wup_g, cw_g, wdn_g):
    def body(x3_ref, t_ref, gf_ref, x2_ref, g2_ref, up_ref, u_ref, wu_ref, cw_ref, wd_ref,
             dx3_ref, dpre_ref, dx2_ref, dgf_ref, dg2_ref, dcv_ref, nxt):
        i = pl.program_id(0)

        @pl.when(i == 0)
        def _():
            nxt[...] = jnp.zeros_like(nxt)
            dgf_ref[...] = jnp.zeros_like(dgf_ref)
            dg2_ref[...] = jnp.zeros_like(dg2_ref)
            dcv_ref[...] = jnp.zeros_like(dcv_ref)

        x3 = x3_ref[...]
        r3 = lax.rsqrt(jnp.mean(x3 * x3, axis=-1, keepdims=True) + EPS)
        xh3 = x3 * r3
        dy = (xh3 * gf_ref[...] - t_ref[...]) * (1.0 / D_MODEL)
        dgf_ref[0:1, :] += jnp.sum(dy * xh3, axis=0, keepdims=True)
        t3 = dy * gf_ref[...]
        dx3 = r3 * (t3 - xh3 * jnp.mean(t3 * xh3, axis=-1, keepdims=True))
        dx3b = dx3.astype(BF16)
        dx3_ref[...] = dx3b
        dh2 = jnp.zeros((TM, D_MODEL), F32)
        row = lax.broadcasted_iota(jnp.int32, (TM, FF_SHARD), 0)
        for j in range(N_FF_PAIR):
            dact = _dot_nt(dx3b, wd_ref[pl.ds(j * FF_SHARD, FF_SHARD), :])
            ua = u_ref[j].astype(F32)
            ub = u_ref[j + N_FF_PAIR].astype(F32)
            sg = _sigmoid(ua)
            du = [dact * ub * (sg * (1.0 + ua * (1.0 - sg))), dact * (ua * sg)]
            for n, s in enumerate((j, j + N_FF_PAIR)):
                d = du[n]
                cw = cw_ref[s]
                nx = nxt[s]
                n1 = jnp.where(row == TM - 1, nx[0:1, :], pltpu.roll(d, TM - 1, 0))
                n2 = jnp.where(row == TM - 2, nx[0:1, :], jnp.where(row == TM - 1, nx[1:2, :], pltpu.roll(d, TM - 2, 0)))
                nxt[s] = d[0:8, :]
                dp = (d * cw[2:3, :] + n1 * cw[1:2, :] + n2 * cw[0:1, :]).astype(BF16)
                dpre_ref[s] = dp
                p = up_ref[s].astype(F32)
                dcv_ref[s, 0:1, :] += jnp.sum(n2 * p, axis=0, keepdims=True)
                dcv_ref[s, 1:2, :] += jnp.sum(n1 * p, axis=0, keepdims=True)
                dcv_ref[s, 2:3, :] += jnp.sum(d * p, axis=0, keepdims=True)
                dcv_ref[s, 3:4, :] += jnp.sum(d, axis=0, keepdims=True)
                dh2 = dh2 + _dot_nt(dp, wu_ref[s])
        x2 = x2_ref[...]
        r2 = lax.rsqrt(jnp.mean(x2 * x2, axis=-1, keepdims=True) + EPS)
        xh2 = x2 * r2
        dg2_ref[0:1, :] += jnp.sum(dh2 * xh2, axis=0, keepdims=True)
        t2 = dh2 * g2_ref[...]
        dx2_ref[...] = dx3 + r2 * (t2 - xh2 * jnp.mean(t2 * xh2, axis=-1, keepdims=True))

    rev = lambda w: pl.BlockSpec((TM, w), lambda i: (N_TB - 1 - i, 0))
    rev3 = lambda: pl.BlockSpec((N_DEV, TM, FF_SHARD), lambda i: (0, N_TB - 1 - i, 0))
    acc = lambda s: pl.BlockSpec(s, lambda i: (0,) * len(s))
    return pl.pallas_call(
        body, name="bwd_ffn", grid=(N_TB,),
        out_shape=(jax.ShapeDtypeStruct((SEQ, D_MODEL), BF16), jax.ShapeDtypeStruct((N_DEV, SEQ, FF_SHARD), BF16),
                   jax.ShapeDtypeStruct((SEQ, D_MODEL), F32), jax.ShapeDtypeStruct((8, D_MODEL), F32),
                   jax.ShapeDtypeStruct((8, D_MODEL), F32), jax.ShapeDtypeStruct((N_DEV, 8, FF_SHARD), F32)),
        in_specs=[rev(D_MODEL), rev(D_MODEL), _resident((1, D_MODEL)), rev(D_MODEL), _resident((1, D_MODEL)), rev3(), rev3(),
                  _resident((N_DEV, D_MODEL, FF_SHARD)), _resident((N_DEV, 8, FF_SHARD)), _resident((D_FF, D_MODEL))],
        out_specs=(rev(D_MODEL), rev3(), rev(D_MODEL), acc((8, D_MODEL)), acc((8, D_MODEL)), acc((N_DEV, 8, FF_SHARD))),
        scratch_shapes=[pltpu.VMEM((N_DEV, 8, FF_SHARD), F32)],
        compiler_params=_cparams(("arbitrary",)),
    )(x3, tgt, gf, x2, g2, up_pre, u_conv, wup_g, cw_g, wdn_g)


def _bwd_mix(dx2, proj, o, sprev, wout_g, grn, lng, lnb, ws, bsb, mask, qdec, kdec, cos2, sin2, hosted):
    cdec = _chunk_decay()
    geoms = [g for g, _ in hosted]
    n_h = len(hosted)

    def body(dx2_ref, p_ref, o_ref, sp_ref, w_ref, grn_ref, lng_ref, lnb_ref, ws_ref, bsb_ref, m_ref, qd_ref, kd_ref,
             cos_ref, sin_ref, *rest):
        dp_ref, dgrn_ref, dlng_ref, dlnb_ref, dws_ref, dbs_ref = rest[n_h:n_h + 6]
        dstate, dbs_acc = rest[2 * n_h + 6:2 * n_h + 8]
        i = pl.program_id(0)
        rs = _Scatters(geoms, rest[:n_h], rest[n_h + 6:2 * n_h + 6], rest[2 * n_h + 8:])
        pl.when(i == 0)(rs.phase1)
        pl.when(i == 3)(rs.phase2)

        @pl.when(i == 0)
        def _():
            dstate[...] = jnp.zeros_like(dstate)
            dgrn_ref[...] = jnp.zeros_like(dgrn_ref)
            dlng_ref[...] = jnp.zeros_like(dlng_ref)
            dlnb_ref[...] = jnp.zeros_like(dlnb_ref)
            dws_ref[...] = jnp.zeros_like(dws_ref)
            dbs_ref[...] = jnp.zeros_like(dbs_ref)
            dbs_acc[...] = jnp.zeros_like(dbs_acc)

        dmix = _dot_nt(dx2_ref[...].astype(BF16), w_ref[...])
        for h in range(HEADS):
            sl = slice(h * HEAD_DIM, (h + 1) * HEAD_DIM)
            q = p_ref[:, sl]
            k = p_ref[:, RET_W + h * HEAD_DIM:RET_W + (h + 1) * HEAD_DIM]
            v = p_ref[:, 2 * RET_W + h * HEAD_DIM:2 * RET_W + (h + 1) * HEAD_DIM]
            g = p_ref[:, 3 * RET_W + h * HEAD_DIM:3 * RET_W + (h + 1) * HEAD_DIM]
            o = o_ref[:, sl]
            rinv = lax.rsqrt(jnp.mean(o * o, axis=-1, keepdims=True) + EPS)
            oh = o * rinv
            gr = grn_ref[:, sl]
            sg = _sigmoid(g)
            dret = dmix[:, sl]
            dp_ref[:, 3 * RET_W + h * HEAD_DIM:3 * RET_W + (h + 1) * HEAD_DIM] = (
                dret * (oh * gr) * (sg * (1.0 + g * (1.0 - sg)))).astype(BF16)
            drn = dret * (g * sg)
            dgrn_ref[0:1, sl] += jnp.sum(drn * oh, axis=0, keepdims=True)
            t = drn * gr
            do = rinv * (t - oh * jnp.mean(t * oh, axis=-1, keepdims=True))
            qb, kb, vb, dob = q.astype(BF16), k.astype(BF16), v.astype(BF16), do.astype(BF16)
            m = m_ref[h]
            ab = (_dot_nt(qb, kb) * m).astype(BF16)
            dab = (_dot_nt(dob, vb) * m).astype(BF16)
            spb = sp_ref[0, h]
            dsn = dstate[h]
            dsnb = dsn.astype(BF16)
            qdb = (q * qd_ref[h]).astype(BF16)
            kdb = (k * kd_ref[h]).astype(BF16)
            dq = _dot(dab, kb) + _dot_nt(dob, spb) * qd_ref[h]
            dk = _dot_tn(dab, qb) + _dot_nt(vb, dsnb) * kd_ref[h]
            dv = _dot_tn(ab, dob) + _dot(kdb, dsnb)
            dstate[h] = dsn * cdec[h] + _dot_tn(qdb, dob)
            c2, s2 = cos_ref[...], sin_ref[...]
            dp_ref[:, sl] = _rot_t(dq, c2, s2).astype(BF16)
            dp_ref[:, RET_W + h * HEAD_DIM:RET_W + (h + 1) * HEAD_DIM] = _rot_t(dk * K_SCALE, c2, s2).astype(BF16)
            dp_ref[:, 2 * RET_W + h * HEAD_DIM:2 * RET_W + (h + 1) * HEAD_DIM] = dv.astype(BF16)
        for gi in range(HEADS):
            sl = slice(gi * HEAD_DIM, (gi + 1) * HEAD_DIM)
            u = p_ref[:, 4 * RET_W + gi * HEAD_DIM:4 * RET_W + (gi + 1) * HEAD_DIM]
            sv = p_ref[:, 4 * RET_W + SGU_W + gi * HEAD_DIM:4 * RET_W + SGU_W + (gi + 1) * HEAD_DIM]
            gv = _gelu(sv)
            xc = gv - jnp.mean(gv, axis=-1, keepdims=True)
            rstd = lax.rsqrt(jnp.mean(xc * xc, axis=-1, keepdims=True) + EPS)
            xh = xc * rstd
            lg = lng_ref[:, sl]
            vnb = (xh * lg + lnb_ref[:, sl]).astype(BF16)
            wcb = _causal(ws_ref[gi]).astype(BF16)
            mixed = _dot(wcb, vnb) + bsb_ref[gi]
            dsgu = dmix[:, RET_W + gi * HEAD_DIM:RET_W + (gi + 1) * HEAD_DIM]
            dmixed = dsgu * _gelu(u)
            dmb = dmixed.astype(BF16)
            dws_ref[gi] += _causal(_dot_nt(dmb, vnb))
            dbs_acc[gi] += dmixed
            dvn = _dot_tn(wcb, dmb)
            dlng_ref[gi:gi + 1, :] += jnp.sum(dvn * xh, axis=0, keepdims=True)
            dlnb_ref[gi:gi + 1, :] += jnp.sum(dvn, axis=0, keepdims=True)
            dxh = dvn * lg
            dgv = rstd * (dxh - jnp.mean(dxh, axis=-1, keepdims=True) - xh * jnp.mean(dxh * xh, axis=-1, keepdims=True))
            dp_ref[:, 4 * RET_W + gi * HEAD_DIM:4 * RET_W + (gi + 1) * HEAD_DIM] = (dsgu * mixed * _gelu_grad(u)).astype(BF16)
            dp_ref[:, 4 * RET_W + SGU_W + gi * HEAD_DIM:4 * RET_W + SGU_W + (gi + 1) * HEAD_DIM] = (
                dgv * _gelu_grad(sv)).astype(BF16)

        @pl.when(i == N_CHUNK - 1)
        def _():
            for gi in range(HEADS):
                col = jnp.broadcast_to(jnp.sum(dbs_acc[gi], axis=-1, keepdims=True), (CHUNK, CHUNK))
                dbs_ref[gi:gi + 1, :] = jnp.transpose(col)[0:1, :]
            rs.phase3()

    rev = lambda w: pl.BlockSpec((CHUNK, w), lambda i: (N_CHUNK - 1 - i, 0))
    hcc = (HEADS, CHUNK, CHUNK)
    acc = lambda s: pl.BlockSpec(s, lambda i: (0,) * len(s))
    res = pl.pallas_call(
        body, name="bwd_mix", grid=(N_CHUNK,),
        out_shape=(jax.ShapeDtypeStruct((SEQ, PROJ_W), BF16), jax.ShapeDtypeStruct((8, RET_W), F32),
                   jax.ShapeDtypeStruct((8, HEAD_DIM), F32), jax.ShapeDtypeStruct((8, HEAD_DIM), F32),
                   jax.ShapeDtypeStruct(hcc, F32), jax.ShapeDtypeStruct((8, CHUNK), F32)) + _scatter_out_shapes(geoms),
        in_specs=[rev(D_MODEL), rev(PROJ_W), rev(RET_W),
                  pl.BlockSpec((1, HEADS, HEAD_DIM, HEAD_DIM), lambda i: (N_CHUNK - 1 - i, 0, 0, 0)),
                  _resident((D_MODEL, D_MODEL)), _resident((1, RET_W)), _resident((1, SGU_W)), _resident((1, SGU_W)),
                  _resident(hcc), _resident(hcc), _resident(hcc), _resident(hcc), _resident(hcc), rev(HEAD_DIM), rev(HEAD_DIM)]
        + [pl.BlockSpec(memory_space=pl.ANY)] * n_h,
        out_specs=(rev(PROJ_W), acc((8, RET_W)), acc((8, HEAD_DIM)), acc((8, HEAD_DIM)), acc(hcc), acc((8, CHUNK)))
        + _scatter_out_specs(geoms),
        scratch_shapes=[pltpu.VMEM((HEADS, HEAD_DIM, HEAD_DIM), F32), pltpu.VMEM((HEADS, CHUNK, CHUNK), F32)] + _scatter_scratch(geoms),
        compiler_params=_cparams(("arbitrary",)),
    )(dx2, proj, o, sprev, wout_g, grn, lng, lnb, ws, bsb, mask, qdec, kdec, cos2, sin2, *[p for _, p in hosted])
    return tuple(res[:6 + n_h])


def _bwd_proj(dproj, win_g, x, g1, dx2, gin_p):
    geoms = [W_IN]

    def body(dp_ref, w_ref, x_ref, g_ref, dx2_ref, gin_ref, dx_ref, dg_ref, rs_out, *rs_scratch):
        rs = _Scatters(geoms, [gin_ref], [rs_out], rs_scratch)
        pl.when(pl.program_id(0) == 0)(rs.phase1)
        pl.when(pl.program_id(0) == 2)(rs.phase2)

        @pl.when(pl.program_id(0) == 0)
        def _():
            dg_ref[...] = jnp.zeros_like(dg_ref)

        dh = _dot_nt(dp_ref[...], w_ref[...])
        xb = x_ref[...]
        r = lax.rsqrt(jnp.mean(xb * xb, axis=-1, keepdims=True) + EPS)
        xh = xb * r
        dg_ref[0:1, :] += jnp.sum(dh * xh, axis=0, keepdims=True)
        t = dh * g_ref[...]
        dx_ref[...] = dx2_ref[...] + r * (t - xh * jnp.mean(t * xh, axis=-1, keepdims=True))
        pl.when(pl.program_id(0) == N_TB - 1)(rs.phase3)

    tok = lambda w: pl.BlockSpec((TM, w), lambda i: (i, 0))
    res = pl.pallas_call(
        body, name="bwd_proj", grid=(N_TB,),
        out_shape=(jax.ShapeDtypeStruct((SEQ, D_MODEL), F32), jax.ShapeDtypeStruct((8, D_MODEL), F32)) + _scatter_out_shapes(geoms),
        in_specs=[tok(PROJ_W), _resident((D_MODEL, PROJ_W)), tok(D_MODEL), _resident((1, D_MODEL)), tok(D_MODEL),
                  pl.BlockSpec(memory_space=pl.ANY)],
        out_specs=(tok(D_MODEL), pl.BlockSpec((8, D_MODEL), lambda i: (0, 0))) + _scatter_out_specs(geoms),
        scratch_shapes=_scatter_scratch(geoms),
        compiler_params=_cparams(("arbitrary",)),
    )(dproj, win_g, x, g1, dx2, gin_p)
    return res[:3]


def _wgrad(name, a, b, n_a, n_b, tn=None, hosted=()):
    n = max(n_a, n_b, 1)
    m_w, n_w = a.shape[-1], b.shape[-1]
    tn = n_w if tn is None else tn
    n_steps = n * (n_w // tn)
    geoms = [g for g, _ in hosted]
    n_h = len(hosted)

    def body(a_ref, b_ref, *rest):
        o_ref = rest[n_h]
        if n_h:
            rs = _Scatters(geoms, rest[:n_h], rest[n_h + 1:2 * n_h + 1], rest[2 * n_h + 1:])
            step = pl.program_id(0) * (n_w // tn) + pl.program_id(1)
            pl.when(step == 0)(rs.phase1)
            pl.when(step == 1)(rs.phase2)
        av = a_ref[0] if n_a else a_ref[...]
        bv = b_ref[0] if n_b else b_ref[...]
        o_ref[0] = _dot_tn(av.astype(BF16), bv.astype(BF16)).astype(BF16)
        if n_h:
            pl.when(step == n_steps - 1)(rs.phase3)

    a_spec = pl.BlockSpec((1, SEQ, m_w), lambda j, t: (j, 0, 0)) if n_a else pl.BlockSpec((SEQ, m_w), lambda j, t: (0, 0))
    b_spec = pl.BlockSpec((1, SEQ, tn), lambda j, t: (j, 0, t)) if n_b else pl.BlockSpec((SEQ, tn), lambda j, t: (0, t))
    o_shape = jax.ShapeDtypeStruct((n, m_w, n_w), BF16)
    o_spec = pl.BlockSpec((1, m_w, tn), lambda j, t: (j, 0, t))
    assert not n_h or n_steps >= 3
    res = pl.pallas_call(
        body, name=name, grid=(n, n_w // tn), out_shape=(o_shape,) + _scatter_out_shapes(geoms),
        in_specs=[a_spec, b_spec] + [pl.BlockSpec(memory_space=pl.ANY)] * n_h, out_specs=(o_spec,) + _scatter_out_specs(geoms),
        scratch_shapes=_scatter_scratch(geoms),
        compiler_params=_cparams(("arbitrary", "arbitrary") if n_h else ("parallel", "parallel")),
    )(a, b, *[p for _, p in hosted])
    return tuple(res[:1 + n_h])


RS_ROWS = {W_IN: 128, W_OUT: 128, W_UP: 128, W_DOWN: 176}


class _Scatter:
    def __init__(self, geom, partial, out, land1, mine, stage2, land2, s1_send, s1_recv, s2_send, s2_recv, ld_sems):
        self.w, self.row0, self.shape = _geom(geom)
        self.partial, self.out, self.land1 = partial, out, land1
        self.mine, self.stage2, self.land2 = mine, stage2, land2
        self.s1_send, self.s1_recv, self.s2_send, self.s2_recv, self.ld_sems = s1_send, s1_recv, s2_send, s2_recv, ld_sems
        self.x, self.y, self.c = lax.axis_index("x"), lax.axis_index("y"), lax.axis_index("c")
        self.sibling = (self.x, self.y, 1 - self.c)
        self.chips = [(self.x, self.y), (1 - self.x, self.y), (self.x, 1 - self.y), (1 - self.x, 1 - self.y)]

    def block(self, px, py, pc):
        dev = 4 * px + 2 * py + pc
        if self.w == W_IN:
            return self.partial.at[:, pl.ds(pl.multiple_of(dev * IN_SHARD, 128), IN_SHARD)]
        if self.w == W_OUT:
            return self.partial.at[pl.ds(pl.multiple_of(dev * OUT_SHARD, 128), OUT_SHARD), :]
        if self.w == W_DOWN:
            return self.partial.at[pl.ds(pl.multiple_of(dev * DOWN_SHARD, 32), DOWN_SHARD), :]
        return self.partial.at[dev, pl.ds(self.row0, self.shape[0]), :]

    def copy1(self, k):
        return pltpu.make_async_remote_copy(
            src_ref=self.block(*self.chips[k], 1 - self.c), dst_ref=self.land1.at[k],
            send_sem=self.s1_send.at[k], recv_sem=self.s1_recv.at[k], device_id=self.sibling, device_id_type=MESH)

    def copy2(self, k):
        return pltpu.make_async_remote_copy(
            src_ref=self.stage2.at[k - 1], dst_ref=self.land2.at[k - 1],
            send_sem=self.s2_send.at[k - 1], recv_sem=self.s2_recv.at[k - 1], device_id=(*self.chips[k], self.c), device_id_type=MESH)

    def _rows(self):
        step = RS_ROWS[self.w]
        return [pl.ds(r0, step) for r0 in range(0, self.shape[0], step)]

    def load(self, k):
        return pltpu.make_async_copy(self.block(*self.chips[k], self.c), self.mine.at[k], self.ld_sems.at[k])

    def phase1(self):
        for k in range(4):
            self.copy1(k).start()
        for k in range(4):
            self.load(k).start()

    def phase2(self):
        for k in (3, 1, 2, 0):
            self.copy1(k).wait_recv()
            self.load(k).wait()
            for rs in self._rows():
                s = self.mine[k, rs, :].astype(F32) + self.land1[k, rs, :].astype(F32)
                if k == 0:
                    self.out[rs, :] = s
                else:
                    self.stage2[k - 1, rs, :] = s.astype(BF16)
            if k:
                self.copy2(k).start()

    def phase3(self):
        for k in (1, 2, 3):
            self.copy2(k).wait_recv()
        for rs in self._rows():
            self.out[rs, :] = ((self.out[rs, :] + self.land2[0, rs, :].astype(F32)) + self.land2[1, rs, :].astype(F32)) \
                + self.land2[2, rs, :].astype(F32)
        for k in range(4):
            self.copy1(k).wait_send()
        for k in (1, 2, 3):
            self.copy2(k).wait_send()


def _geom(geom):
    if isinstance(geom, tuple):
        w, row0, rows = geom
        assert w == W_UP
        return w, row0, (rows, SHARD[w][1])
    return geom, 0, SHARD[geom]


N_SCATTER_SCRATCH = 9


def _scatter_out_shapes(geoms):
    return tuple(jax.ShapeDtypeStruct(_geom(g)[2], F32) for g in geoms)


def _scatter_out_specs(geoms):
    return (pl.BlockSpec(memory_space=pltpu.VMEM),) * len(geoms)


def _scatter_scratch(geoms):
    out = []
    for g in geoms:
        s = _geom(g)[2]
        out += [pltpu.VMEM((4,) + s, BF16), pltpu.VMEM((4,) + s, BF16), pltpu.VMEM((3,) + s, BF16), pltpu.VMEM((3,) + s, BF16),
                pltpu.SemaphoreType.DMA((4,)), pltpu.SemaphoreType.DMA((4,)), pltpu.SemaphoreType.DMA((3,)),
                pltpu.SemaphoreType.DMA((3,)), pltpu.SemaphoreType.DMA((4,))]
    return out


class _Scatters:
    def __init__(self, geoms, p_refs, out_refs, scratch):
        k = N_SCATTER_SCRATCH
        self.items = [_Scatter(g, p_refs[i], out_refs[i], *scratch[k * i:k * i + k]) for i, g in enumerate(geoms)]

    def phase1(self):
        for s in self.items:
            s.phase1()

    def phase2(self):
        for s in self.items:
            s.phase2()

    def phase3(self):
        for s in self.items:
            s.phase3()


PACK_W = 1024


def _all_reduce_small(dg1, dg2, dgf, dgrn, dlng, dlnb, dbs, loss_parts, dws, dcv):
    n_a = 3

    def body(dg1_ref, dg2_ref, dgf_ref, dgrn_ref, dlng_ref, dlnb_ref, dbs_ref, loss_ref, dws_ref, dcv_ref,
             rp_ref, rws_ref, rcv_ref, pack, rx_p, rx_ws, rx_cv, cs_p, cs_ws, cs_cv, g_p, g_ws, g_cv,
             s1_send, s1_recv, s2_send, s2_recv, s3_send, s3_recv):
        x, y, c = lax.axis_index("x"), lax.axis_index("y"), lax.axis_index("c")
        sibling = (x, y, 1 - c)
        chips = [(1 - x, y), (x, 1 - y), (1 - x, 1 - y)]
        pack[...] = jnp.zeros_like(pack)
        pack[0, 0:1, :] = dg1_ref[0:1, :]
        pack[0, 1:2, :] = dg2_ref[0:1, :]
        pack[0, 2:3, :] = dgf_ref[0:1, :]
        pack[0, 3:4, 0:RET_W] = dgrn_ref[0:1, :]
        lsum = loss_ref[0, 0:1, :]
        for i in range(1, N_TB):
            lsum = lsum + loss_ref[i, 0:1, :]
        pack[0, 3:4, RET_W:RET_W + 128] = lsum
        pack[1, 0:HEADS, 0:128] = dlng_ref[0:HEADS, :]
        pack[1, 0:HEADS, 128:256] = dlnb_ref[0:HEADS, :]
        pack[1, 0:HEADS, 256:384] = dbs_ref[0:HEADS, :]

        srcs = [pack, dws_ref, dcv_ref]
        outs = [rp_ref, rws_ref, rcv_ref]
        rxs = [rx_p, rx_ws, rx_cv]
        css = [cs_p, cs_ws, cs_cv]
        gs = [g_p, g_ws, g_cv]
        hl = [1, HEADS // 2, N_DEV // 2]

        def half(ref, a, h):
            return ref.at[pl.ds(h * hl[a], hl[a])]

        ex1 = [pltpu.make_async_remote_copy(src_ref=half(srcs[a], a, 1 - c), dst_ref=rxs[a], send_sem=s1_send.at[a],
                                            recv_sem=s1_recv.at[a], device_id=sibling, device_id_type=MESH) for a in range(n_a)]
        for cp in ex1:
            cp.start()
        ex2 = []
        for a in range(n_a):
            ex1[a].wait_recv()
            css[a][...] = half(srcs[a], a, c)[...] + rxs[a][...]
            for j, chip in enumerate(chips):
                cp = pltpu.make_async_remote_copy(src_ref=css[a], dst_ref=gs[a].at[j], send_sem=s2_send.at[a, j],
                                                  recv_sem=s2_recv.at[a, j], device_id=(*chip, c), device_id_type=MESH)
                cp.start()
                ex2.append(cp)
        ex3 = []
        for a in range(n_a):
            for j in range(3):
                ex2[3 * a + j].wait_recv()
            tot = None
            for q in range(4):
                k = jnp.where(x != (q >> 1), 1, 0) + jnp.where(y != (q & 1), 2, 0)
                term = jnp.where(k == 0, css[a][...], jnp.where(k == 1, gs[a][0], jnp.where(k == 2, gs[a][1], gs[a][2])))
                tot = term if tot is None else tot + term
            half(outs[a], a, c)[...] = tot
            cp = pltpu.make_async_remote_copy(src_ref=half(outs[a], a, c), dst_ref=half(outs[a], a, c), send_sem=s3_send.at[a],
                                              recv_sem=s3_recv.at[a], device_id=sibling, device_id_type=MESH)
            cp.start()
            ex3.append(cp)
        for a in range(n_a):
            pltpu.make_async_remote_copy(src_ref=half(outs[a], a, 1 - c), dst_ref=half(outs[a], a, 1 - c), send_sem=s3_send.at[a],
                                         recv_sem=s3_recv.at[a], device_id=sibling, device_id_type=MESH).wait_recv()
        for cp in ex1 + ex2 + ex3:
            cp.wait_send()

    vm = pl.BlockSpec(memory_space=pltpu.VMEM)
    full = [(2, 8, PACK_W), (HEADS, CHUNK, CHUNK), (N_DEV, 8, FF_SHARD)]
    halves = [(s[0] // 2,) + s[1:] for s in full]
    return pl.pallas_call(
        body, name="ar_small",
        out_shape=tuple(jax.ShapeDtypeStruct(s, F32) for s in full),
        in_specs=[vm] * 10, out_specs=(vm,) * 3,
        scratch_shapes=[pltpu.VMEM(full[0], F32)] + [pltpu.VMEM(s, F32) for s in halves] + [pltpu.VMEM(s, F32) for s in halves]
        + [pltpu.VMEM((3,) + s, F32) for s in halves]
        + [pltpu.SemaphoreType.DMA((n_a,)), pltpu.SemaphoreType.DMA((n_a,)), pltpu.SemaphoreType.DMA((n_a, 3)),
           pltpu.SemaphoreType.DMA((n_a, 3)), pltpu.SemaphoreType.DMA((n_a,)), pltpu.SemaphoreType.DMA((n_a,))],
        compiler_params=_cparams(),
    )(dg1, dg2, dgf, dgrn, dlng, dlnb, dbs, loss_parts, dws, dcv)


def _adam_math(w, g, m, v):
    nm = ADAM_B1 * m + (1.0 - ADAM_B1) * g
    nv = ADAM_B2 * v + (1.0 - ADAM_B2) * (g * g)
    d = -ADAM_LR * ((nm / (1.0 - ADAM_B1 ** ADAM_STEP)) / (jnp.sqrt(nv / (1.0 - ADAM_B2 ** ADAM_STEP)) + ADAM_EPS) + ADAM_WD * w)
    return d, nm, nv


def _adamw(name, w, gs, m, v, rows):
    _, r, cdim = w.shape
    n_steps = r // rows
    half = n_steps // len(gs)

    def body(w_ref, *rest):
        g_refs, (m_ref, v_ref, go_ref, d_ref, nm_ref, nv_ref) = rest[:len(gs)], rest[len(gs):]
        gg = g_refs[0][...]
        if len(gs) == 2:
            gg = jnp.where(pl.program_id(0) < half, gg, g_refs[1][...])
        go_ref[0] = gg
        d, nm, nv = _adam_math(w_ref[0], gg, m_ref[0], v_ref[0])
        d_ref[0], nm_ref[0], nv_ref[0] = d, nm, nv

    spec3 = pl.BlockSpec((1, rows, cdim), lambda i: (0, i, 0))
    if len(gs) == 1:
        g_specs = [pl.BlockSpec((rows, cdim), lambda i: (i, 0))]
    else:
        g_specs = [pl.BlockSpec((rows, cdim), lambda i: (jnp.minimum(i, half - 1), 0)),
                   pl.BlockSpec((rows, cdim), lambda i: (jnp.maximum(i - half, 0), 0))]
    sh = jax.ShapeDtypeStruct((1, r, cdim), F32)
    return pl.pallas_call(
        body, name=name, grid=(n_steps,), out_shape=(sh, sh, sh, sh),
        in_specs=[spec3] + g_specs + [spec3, spec3], out_specs=(spec3,) * 4,
        compiler_params=_cparams(("parallel",)),
    )(w, *gs, m, v)


def _adamw_small(rp, rws, rcv, params):
    n_p = len(params)

    def body(*refs):
        rp_ref, rws_ref, rcv_ref = refs[:3]
        ins = refs[3:3 + 3 * n_p]
        outs = refs[3 + 3 * n_p:]
        me = 4 * lax.axis_index("x") + 2 * lax.axis_index("y") + lax.axis_index("c")
        grads = [rp_ref[0, 0:1, :], rp_ref[0, 1:2, :], rp_ref[0, 2:3, :], rp_ref[0, 3:4, 0:RET_W],
                 rp_ref[1, 0:HEADS, 0:128], rp_ref[1, 0:HEADS, 128:256], rp_ref[1, 0:HEADS, 256:384],
                 rws_ref[...], rcv_ref[me][0:3, :], None]
        for p in range(n_p):
            w_ref, m_ref, v_ref = ins[3 * p:3 * p + 3]
            o = outs[4 * p:4 * p + 4]
            if p == n_p - 1:
                for j in range(N_DEV):
                    g = rcv_ref[j, 3:4, :]
                    res = (g,) + _adam_math(w_ref[j:j + 1, :], g, m_ref[j:j + 1, :], v_ref[j:j + 1, :])
                    for t in range(4):
                        o[t][j:j + 1, :] = res[t]
                continue
            lead = w_ref.ndim > grads[p].ndim
            rd = (lambda r: r[0]) if lead else (lambda r: r[...])
            res = (grads[p],) + _adam_math(rd(w_ref), grads[p], rd(m_ref), rd(v_ref))
            for t in range(4):
                if lead:
                    o[t][0] = res[t]
                else:
                    o[t][...] = res[t]

    vm = pl.BlockSpec(memory_space=pltpu.VMEM)
    flat = [a for tr in params for a in tr]
    out_shape = tuple(jax.ShapeDtypeStruct(tr[0].shape, F32) for tr in params for _ in range(4))
    res = pl.pallas_call(
        body, name="adamw_small", out_shape=out_shape, in_specs=[vm] * (3 + len(flat)), out_specs=(vm,) * len(out_shape),
        compiler_params=_cparams(),
    )(rp, rws, rcv, *flat)
    return [res[4 * p:4 * p + 4] for p in range(n_p)]


def kernel(x, mix_norm_g, w_in, ret_norm_g, sgu_ln_g, sgu_ln_b, sgu_w_s, sgu_b_s, w_out, ffn_norm_g, w_up, conv_w, conv_b, w_down, final_norm_g, loss_target, m_mix_norm_g, m_w_in, m_ret_norm_g, m_sgu_ln_g, m_sgu_ln_b, m_sgu_w_s, m_sgu_b_s, m_w_out, m_ffn_norm_g, m_w_up, m_conv_w, m_conv_b, m_w_down, m_final_norm_g, v_mix_norm_g, v_w_in, v_ret_norm_g, v_sgu_ln_g, v_sgu_ln_b, v_sgu_w_s, v_sgu_b_s, v_w_out, v_ffn_norm_g, v_w_up, v_conv_w, v_conv_b, v_w_down, v_final_norm_g):
    xs = x[0]
    tgt = loss_target[0]
    cos2, sin2 = _rope_tables()
    mask, qdec, kdec = _decay_tables()
    grn = ret_norm_g.reshape(1, RET_W)
    lng = sgu_ln_g.reshape(1, SGU_W)
    lnb = sgu_ln_b.reshape(1, SGU_W)
    ws = sgu_w_s[0]
    bsb = jnp.broadcast_to(sgu_b_s[0][:, :, None], (HEADS, CHUNK, HEAD_DIM))
    gf = final_norm_g.reshape(1, D_MODEL)
    cb_g = conv_b.reshape(N_DEV, 1, FF_SHARD)

    win_g, cw_g, so, su, sd = _ag_first(w_in[0], w_out[0], w_up[0], w_down[0], conv_w[0])

    proj, h1, wout_g, wdn_g = _fwd_proj(xs, mix_norm_g, win_g, cos2, sin2, so, sd)
    x2, mixcat, o, sprev, wup_g = _fwd_mix(xs, proj, wout_g, grn, lng, lnb, ws, bsb, mask, qdec, kdec, su)
    h2, up_pre, u_conv, act, x3, loss_parts = _fwd_ffn(x2, ffn_norm_g, wup_g, cw_g, cb_g, wdn_g, gf, tgt)

    dx3, dpre, dx2, dgf, dg2, dcv = _bwd_ffn(x3, tgt, gf, x2, ffn_norm_g, up_pre, u_conv, wup_g, cw_g, wdn_g)
    half = D_MODEL // 2
    gdn_p = _wgrad("wgrad_down", act, dx3, N_FF_PAIR, 0)[0].reshape(D_FF, D_MODEL)
    gout_p = _wgrad("wgrad_out", mixcat, dx2, 0, 0, tn=512)[0][0]
    gup_p, g_dn = _wgrad("wgrad_up", h2, dpre, 0, N_DEV, hosted=[(W_DOWN, gdn_p)])
    dproj, dgrn, dlng, dlnb, dws, dbs, g_up_a, g_out = _bwd_mix(
        dx2, proj, o, sprev, wout_g, grn, lng, lnb, ws, bsb, mask, qdec, kdec, cos2, sin2,
        [((W_UP, 0, half), gup_p), (W_OUT, gout_p)])
    gin_p, g_up_b = _wgrad("wgrad_in", h1, dproj, 0, 0, tn=768, hosted=[((W_UP, half, half), gup_p)])
    grad_x, dg1, g_in = _bwd_proj(dproj, win_g, xs, mix_norm_g, dx2, gin_p[0])
    rp, rws, rcv = _all_reduce_small(dg1, dg2, dgf, dgrn, dlng, dlnb, dbs, loss_parts, dws, dcv)
    loss = rp[0, 3, RET_W]

    table = {}
    for name, w, gs, m, v, rows in (("w_in", w_in, [g_in], m_w_in, v_w_in, 256), ("w_out", w_out, [g_out], m_w_out, v_w_out, 128),
                                    ("w_up", w_up, [g_up_a, g_up_b], m_w_up, v_w_up, 256),
                                    ("w_down", w_down, [g_dn], m_w_down, v_w_down, 88)):
        table[name] = _adamw("adamw_" + name, w, gs, m, v, rows)
    row = lambda a: a.reshape(1, D_MODEL)
    slab = lambda a: a.reshape(N_DEV, FF_SHARD)
    names_small = ["mix_norm_g", "ffn_norm_g", "final_norm_g", "ret_norm_g", "sgu_ln_g", "sgu_ln_b", "sgu_b_s", "sgu_w_s",
                   "conv_w", "conv_b"]
    params = [(mix_norm_g, m_mix_norm_g, v_mix_norm_g), (ffn_norm_g, m_ffn_norm_g, v_ffn_norm_g),
              (row(final_norm_g), row(m_final_norm_g), row(v_final_norm_g)), (ret_norm_g, m_ret_norm_g, v_ret_norm_g),
              (sgu_ln_g, m_sgu_ln_g, v_sgu_ln_g), (sgu_ln_b, m_sgu_ln_b, v_sgu_ln_b), (sgu_b_s, m_sgu_b_s, v_sgu_b_s),
              (sgu_w_s, m_sgu_w_s, v_sgu_w_s), (conv_w, m_conv_w, v_conv_w), (slab(conv_b), slab(m_conv_b), slab(v_conv_b))]
    for n, res in zip(names_small, _adamw_small(rp, rws, rcv, params)):
        table[n] = res
    table["final_norm_g"] = tuple(a.reshape(D_MODEL) for a in table["final_norm_g"])
    table["conv_b"] = tuple(a.reshape(1, 2 * D_FF) for a in table["conv_b"])

    order = ["mix_norm_g", "w_in", "ret_norm_g", "sgu_ln_g", "sgu_ln_b", "sgu_w_s", "sgu_b_s", "w_out", "ffn_norm_g", "w_up",
             "conv_w", "conv_b", "w_down", "final_norm_g"]
    outs = [loss, grad_x[None]]
    for col in range(4):
        outs += [table[n][col] for n in order]
    return tuple(outs)
```

```python
import functools
import math

import jax
import jax.numpy as jnp
import numpy as np
from jax import lax
from jax.experimental import pallas as pl
from jax.experimental.pallas import tpu as pltpu

F32 = jnp.float32
BF16 = jnp.bfloat16
MESH = pl.DeviceIdType.MESH

N_DEV = 8
SEQ = 2048
D_MODEL = 1024
CHUNK = 128
N_CHUNK = SEQ // CHUNK
HEADS = 4
HEAD_DIM = 128
RET_W = 512
SGU_W = 512
PROJ_W = 3072
D_FF = 2816
FF_SHARD = 704
FF_TILE = 1408
N_FF_TILE = D_FF // FF_TILE
IN_SHARD = PROJ_W // N_DEV
OUT_SHARD = D_MODEL // N_DEV
DOWN_SHARD = D_FF // N_DEV
TM = 256
N_TB = SEQ // TM
EPS = 1e-6
ROPE_BASE = 10000.0
K_SCALE = HEAD_DIM ** -0.5
INV_SQRT2 = 0.7071067811865476
INV_SQRT_2PI = 0.3989422804014327

ADAM_LR = 0.001
ADAM_B1 = 0.9
ADAM_B2 = 0.999
ADAM_EPS = 1e-08
ADAM_WD = 0.01
ADAM_STEP = 10

VMEM_LIMIT = 56 * 1024 * 1024


def _cparams(sem=None, vmem=VMEM_LIMIT):
    return pltpu.CompilerParams(dimension_semantics=sem, vmem_limit_bytes=vmem)


def _resident(shape):
    nd = len(shape)
    return pl.BlockSpec(shape, lambda *_: (0,) * nd, pipeline_mode=pl.Buffered(1))


def _dot(a, b):
    return jnp.dot(a, b, preferred_element_type=F32)


def _dot_nt(a, b):
    return lax.dot_general(a, b, (((1,), (1,)), ((), ())), preferred_element_type=F32)


def _dot_tn(a, b):
    return lax.dot_general(a, b, (((0,), (0,)), ((), ())), preferred_element_type=F32)


def _sigmoid(x):
    return 1.0 / (1.0 + jnp.exp(-x))


def _gelu(x):
    return 0.5 * x * (1.0 + lax.erf(x * INV_SQRT2))


def _gelu_grad(x):
    return 0.5 * (1.0 + lax.erf(x * INV_SQRT2)) + x * (jnp.exp(-0.5 * x * x) * INV_SQRT_2PI)


def _rot(xh, cos2, sin2):
    return xh * cos2 + pltpu.roll(xh, HEAD_DIM // 2, 1) * sin2


def _rot_t(dh, cos2, sin2):
    return dh * cos2 + pltpu.roll(dh * sin2, HEAD_DIM // 2, 1)


def _rope_tables():
    half = HEAD_DIM // 2
    inv_freq = jnp.power(ROPE_BASE, -jnp.arange(half, dtype=F32) / half)
    ang = jnp.arange(SEQ, dtype=F32)[:, None] * inv_freq[None, :]
    cos, sin = jnp.cos(ang), jnp.sin(ang)
    cos2 = jnp.concatenate([cos, cos], axis=-1)
    sin2 = jnp.concatenate([-sin, sin], axis=-1)
    return cos2, sin2


def _decay_tables():
    log_gamma = jnp.log(1.0 - jnp.power(2.0, -5.0 - jnp.arange(HEADS, dtype=F32)))
    pos = jnp.arange(CHUNK, dtype=F32)
    diff = pos[:, None] - pos[None, :]
    mask = jnp.where(diff >= 0.0, jnp.exp(log_gamma[:, None, None] * jnp.maximum(diff, 0.0)[None]), 0.0)
    k_decay = jnp.exp(log_gamma[:, None] * (CHUNK - 1.0 - pos)[None])
    q_decay = jnp.exp(log_gamma[:, None] * (pos + 1.0)[None])
    kd = jnp.broadcast_to(k_decay[:, :, None], (HEADS, CHUNK, HEAD_DIM))
    qd = jnp.broadcast_to(q_decay[:, :, None], (HEADS, CHUNK, HEAD_DIM))
    return mask.astype(F32), qd.astype(F32), kd.astype(F32)


def _chunk_decay():
    lg = np.log(np.float32(1.0) - np.power(np.float32(2.0), -5.0 - np.arange(HEADS, dtype=np.float32))).astype(np.float32)
    return [float(np.exp(lg[h] * np.float32(CHUNK))) for h in range(HEADS)]


W_IN, W_OUT, W_UP, W_DOWN, W_CONV = range(5)
GATHERED = {W_IN: ((D_MODEL, PROJ_W), BF16), W_OUT: ((D_MODEL, D_MODEL), BF16), W_UP: ((2 * D_FF, D_MODEL), BF16),
            W_DOWN: ((D_FF, D_MODEL), BF16), W_CONV: ((N_DEV, 8, FF_SHARD), F32)}
SHARD = {W_IN: (D_MODEL, IN_SHARD), W_OUT: (OUT_SHARD, D_MODEL), W_UP: (FF_SHARD, D_MODEL), W_DOWN: (DOWN_SHARD, D_MODEL),
         W_CONV: (8, FF_SHARD)}


class _Gather:
    def __init__(self, ids, stages, gathered, send_sems, recv_sems, local_sems):
        self.ids, self.stages, self.gathered = ids, stages, gathered
        self.send_sems, self.recv_sems, self.local_sems = send_sems, recv_sems, local_sems
        self.x, self.y, self.c = lax.axis_index("x"), lax.axis_index("y"), lax.axis_index("c")
        self.me = (self.x, self.y, self.c)
        self.sibling = (self.x, self.y, 1 - self.c)
        self.chips = [(1 - self.x, self.y), (self.x, 1 - self.y), (1 - self.x, 1 - self.y)]

    def slot(self, n, px, py, pc):
        dev = 4 * px + 2 * py + pc
        w, g = self.ids[n], self.gathered[n]
        if w == W_IN:
            return g.at[:, pl.ds(pl.multiple_of(dev * IN_SHARD, 128), IN_SHARD)]
        if w == W_OUT:
            return g.at[pl.ds(pl.multiple_of(dev * OUT_SHARD, 128), OUT_SHARD), :]
        if w == W_DOWN:
            return g.at[pl.ds(pl.multiple_of(dev * DOWN_SHARD, 32), DOWN_SHARD), :]
        if w == W_UP:
            return g.at[pl.ds(pl.multiple_of(dev * FF_SHARD, 32), FF_SHARD), :]
        return g.at[dev]

    def copy(self, n, k, block, to, src=None):
        return pltpu.make_async_remote_copy(
            src_ref=self.slot(n, *block) if src is None else src, dst_ref=self.slot(n, *block),
            send_sem=self.send_sems.at[n, k], recv_sem=self.recv_sems.at[n, k], device_id=to, device_id_type=MESH)

    def _mine(self):
        return [pltpu.make_async_copy(self.stages[n], self.slot(n, *self.me), self.local_sems.at[n]) for n in range(len(self.ids))]

    def _first(self):
        out = []
        for n in range(len(self.ids)):
            out.append(self.copy(n, 0, self.me, self.sibling, src=self.stages[n]))
            out += [self.copy(n, 1 + j, self.me, (*chip, self.c), src=self.stages[n]) for j, chip in enumerate(self.chips)]
        return out

    def start(self):
        for cp in self._mine() + self._first():
            cp.start()

    def finish(self):
        passed = []
        for n in range(len(self.ids)):
            for j, chip in enumerate(self.chips):
                self.copy(n, 1 + j, (*chip, self.c), self.me).wait_recv()
                fwd = self.copy(n, 4 + j, (*chip, self.c), self.sibling)
                fwd.start()
                passed.append(fwd)
        for n in range(len(self.ids)):
            self.copy(n, 0, self.sibling, self.me).wait_recv()
            for j, chip in enumerate(self.chips):
                self.copy(n, 4 + j, (*chip, 1 - self.c), self.me).wait_recv()
        for cp in self._first() + passed:
            cp.wait_send()
        for cp in self._mine():
            cp.wait()


def _gather_scratch(n):
    return [pltpu.SemaphoreType.DMA((n, 7)), pltpu.SemaphoreType.DMA((n, 7)), pltpu.SemaphoreType.DMA((n,))]


def _gathered_shapes(ids):
    return tuple(jax.ShapeDtypeStruct(*GATHERED[w]) for w in ids)


def _ag_first(w_in, w_out, w_up, w_down, conv_w):
    ids = [W_IN, W_CONV]

    def body(in_ref, out_ref, up_ref, dn_ref, cw_ref, gin, gcw, so_ref, su_ref, sd_ref, s_in, s_cw, send_sems, recv_sems, local_sems):
        s_in[...] = in_ref[...].astype(BF16)
        s_cw[...] = jnp.zeros_like(s_cw)
        s_cw[0:3, :] = cw_ref[...]
        ag = _Gather(ids, [s_in, s_cw], [gin, gcw], send_sems, recv_sems, local_sems)
        ag.start()
        so_ref[...] = out_ref[...].astype(BF16)
        su_ref[...] = up_ref[...].astype(BF16)
        sd_ref[...] = dn_ref[...].astype(BF16)
        ag.finish()

    vm = pl.BlockSpec(memory_space=pltpu.VMEM)
    hbm = pl.BlockSpec(memory_space=pl.ANY)
    return pl.pallas_call(
        body, name="ag_first",
        out_shape=_gathered_shapes(ids) + tuple(jax.ShapeDtypeStruct(SHARD[w], BF16) for w in (W_OUT, W_UP, W_DOWN)),
        in_specs=[vm] * 5, out_specs=(hbm, hbm, vm, vm, vm),
        scratch_shapes=[pltpu.VMEM(SHARD[W_IN], BF16), pltpu.VMEM(SHARD[W_CONV], F32)] + _gather_scratch(len(ids)),
        compiler_params=_cparams(),
    )(w_in, w_out, w_up, w_down, conv_w)


def _fwd_proj(x, g1, win_g, cos2, sin2, so, sd):
    ids = [W_OUT, W_DOWN]

    def body(x_ref, g_ref, w_ref, cos_ref, sin_ref, so_ref, sd_ref, proj_ref, h1_ref, gout, gdn, send_sems, recv_sems, local_sems):
        ag = _Gather(ids, [so_ref, sd_ref], [gout, gdn], send_sems, recv_sems, local_sems)

        @pl.when(pl.program_id(0) == 0)
        def _():
            ag.start()

        xb = x_ref[...]
        r = lax.rsqrt(jnp.mean(xb * xb, axis=-1, keepdims=True) + EPS)
        h = ((xb * r) * g_ref[...]).astype(BF16)
        h1_ref[...] = h
        p = _dot(h, w_ref[...])
        for hd in range(HEADS):
            sl = slice(hd * HEAD_DIM, (hd + 1) * HEAD_DIM)
            c2, s2 = cos_ref[...], sin_ref[...]
            proj_ref[:, sl] = _rot(p[:, sl], c2, s2)
            ks = slice(RET_W + hd * HEAD_DIM, RET_W + (hd + 1) * HEAD_DIM)
            proj_ref[:, ks] = _rot(p[:, ks], c2, s2) * K_SCALE
        proj_ref[:, 2 * RET_W:] = p[:, 2 * RET_W:]

        @pl.when(pl.program_id(0) == N_TB - 1)
        def _():
            ag.finish()

    tok = lambda w: pl.BlockSpec((TM, w), lambda i: (i, 0))
    hbm = pl.BlockSpec(memory_space=pl.ANY)
    return pl.pallas_call(
        body, name="fwd_proj", grid=(N_TB,),
        out_shape=(jax.ShapeDtypeStruct((SEQ, PROJ_W), F32), jax.ShapeDtypeStruct((SEQ, D_MODEL), BF16)) + _gathered_shapes(ids),
        in_specs=[tok(D_MODEL), _resident((1, D_MODEL)), _resident((D_MODEL, PROJ_W)), tok(HEAD_DIM), tok(HEAD_DIM), hbm, hbm],
        out_specs=(tok(PROJ_W), tok(D_MODEL), hbm, hbm),
        scratch_shapes=_gather_scratch(len(ids)),
        compiler_params=_cparams(("arbitrary",)),
    )(x, g1, win_g, cos2, sin2, so, sd)


def _causal(w):
    r = lax.broadcasted_iota(jnp.int32, (CHUNK, CHUNK), 0)
    c = lax.broadcasted_iota(jnp.int32, (CHUNK, CHUNK), 1)
    return jnp.where(r >= c, w, 0.0)


def _fwd_mix(x, proj, wout_g, grn, lng, lnb, ws, bsb, mask, qdec, kdec, su):
    cdec = _chunk_decay()
    ids = [W_UP]

    def body(x_ref, p_ref, w_ref, grn_ref, lng_ref, lnb_ref, ws_ref, bsb_ref, m_ref, qd_ref, kd_ref, su_ref,
             x2_ref, cat_ref, o_ref, sp_ref, gup, state, send_sems, recv_sems, local_sems):
        ag = _Gather(ids, [su_ref], [gup], send_sems, recv_sems, local_sems)

        @pl.when(pl.program_id(0) == 0)
        def _():
            state[...] = jnp.zeros_like(state)
            ag.start()

        for h in range(HEADS):
            sl = slice(h * HEAD_DIM, (h + 1) * HEAD_DIM)
            q = p_ref[:, sl]
            k = p_ref[:, RET_W + h * HEAD_DIM:RET_W + (h + 1) * HEAD_DIM]
            v = p_ref[:, 2 * RET_W + h * HEAD_DIM:2 * RET_W + (h + 1) * HEAD_DIM]
            g = p_ref[:, 3 * RET_W + h * HEAD_DIM:3 * RET_W + (h + 1) * HEAD_DIM]
            qb, kb, vb = q.astype(BF16), k.astype(BF16), v.astype(BF16)
            a = _dot_nt(qb, kb) * m_ref[h]
            spb = state[h].astype(BF16)
            sp_ref[0, h] = spb
            o = _dot(a.astype(BF16), vb) + _dot((q * qd_ref[h]).astype(BF16), spb)
            state[h] = state[h] * cdec[h] + _dot_tn((k * kd_ref[h]).astype(BF16), vb)
            o_ref[:, sl] = o
            rinv = lax.rsqrt(jnp.mean(o * o, axis=-1, keepdims=True) + EPS)
            rn = (o * rinv) * grn_ref[:, sl]
            cat_ref[:, sl] = ((g * _sigmoid(g)) * rn).astype(BF16)
        for gi in range(HEADS):
            sl = slice(gi * HEAD_DIM, (gi + 1) * HEAD_DIM)
            u = p_ref[:, 4 * RET_W + gi * HEAD_DIM:4 * RET_W + (gi + 1) * HEAD_DIM]
            sv = p_ref[:, 4 * RET_W + SGU_W + gi * HEAD_DIM:4 * RET_W + SGU_W + (gi + 1) * HEAD_DIM]
            gv = _gelu(sv)
            xc = gv - jnp.mean(gv, axis=-1, keepdims=True)
            vn = (xc * lax.rsqrt(jnp.mean(xc * xc, axis=-1, keepdims=True) + EPS)) * lng_ref[:, sl] + lnb_ref[:, sl]
            mixed = _dot(_causal(ws_ref[gi]).astype(BF16), vn.astype(BF16)) + bsb_ref[gi]
            cat_ref[:, RET_W + gi * HEAD_DIM:RET_W + (gi + 1) * HEAD_DIM] = (_gelu(u) * mixed).astype(BF16)
        x2_ref[...] = x_ref[...] + _dot(cat_ref[...], w_ref[...])

        @pl.when(pl.program_id(0) == N_CHUNK - 1)
        def _():
            ag.finish()

    ch = lambda w: pl.BlockSpec((CHUNK, w), lambda i: (i, 0))
    hcc = (HEADS, CHUNK, CHUNK)
    hbm = pl.BlockSpec(memory_space=pl.ANY)
    return pl.pallas_call(
        body, name="fwd_mix", grid=(N_CHUNK,),
        out_shape=(jax.ShapeDtypeStruct((SEQ, D_MODEL), F32), jax.ShapeDtypeStruct((SEQ, D_MODEL), BF16),
                   jax.ShapeDtypeStruct((SEQ, RET_W), F32), jax.ShapeDtypeStruct((N_CHUNK, HEADS, HEAD_DIM, HEAD_DIM), BF16))
        + _gathered_shapes(ids),
        in_specs=[ch(D_MODEL), ch(PROJ_W), _resident((D_MODEL, D_MODEL)), _resident((1, RET_W)), _resident((1, SGU_W)),
                  _resident((1, SGU_W)), _resident(hcc), _resident(hcc), _resident(hcc), _resident(hcc), _resident(hcc), hbm],
        out_specs=(ch(D_MODEL), ch(D_MODEL), ch(RET_W), pl.BlockSpec((1, HEADS, HEAD_DIM, HEAD_DIM), lambda i: (i, 0, 0, 0)), hbm),
        scratch_shapes=[pltpu.VMEM((HEADS, HEAD_DIM, HEAD_DIM), F32)] + _gather_scratch(len(ids)),
        compiler_params=_cparams(("arbitrary",)),
    )(x, proj, wout_g, grn, lng, lnb, ws, bsb, mask, qdec, kdec, su)


def _conv_taps(p, prev8):
    row = lax.broadcasted_iota(jnp.int32, p.shape, 0)
    p1 = jnp.where(row == 0, prev8[7:8, :], pltpu.roll(p, 1, 0))
    p2 = jnp.where(row == 0, prev8[6:7, :], jnp.where(row == 1, prev8[7:8, :], pltpu.roll(p, 2, 0)))
    return p1, p2


def _fwd_ffn(x2, g2, wup_g, cw_g, cb_g, wdn_g, gf, tgt):
    def body(x_ref, g_ref, wu_ref, cw_ref, cb_ref, wd_ref, gf_ref, t_ref, h2_ref, up_ref, u_ref, act_ref, x3_ref, loss_ref, carry):
        @pl.when(pl.program_id(0) == 0)
        def _():
            carry[...] = jnp.zeros_like(carry)

        xb = x_ref[...]
        r = lax.rsqrt(jnp.mean(xb * xb, axis=-1, keepdims=True) + EPS)
        h = ((xb * r) * g_ref[...]).astype(BF16)
        h2_ref[...] = h
        acc = xb
        for t in range(N_FF_TILE):
            u = []
            for c0 in (t * FF_TILE, D_FF + t * FF_TILE):
                cs = slice(c0, c0 + FF_TILE)
                p = _dot_nt(h, wu_ref[pl.ds(c0, FF_TILE), :])
                up_ref[:, cs] = p.astype(BF16)
                p1, p2 = _conv_taps(p, carry[:, cs])
                carry[:, cs] = p[TM - 8:, :]
                us = p2 * cw_ref[0:1, cs] + p1 * cw_ref[1:2, cs] + p * cw_ref[2:3, cs] + cb_ref[:, cs]
                u_ref[:, cs] = us.astype(BF16)
                u.append(us)
            a = ((u[0] * _sigmoid(u[0])) * u[1]).astype(BF16)
            act_ref[:, t * FF_TILE:(t + 1) * FF_TILE] = a
            acc = acc + _dot(a, wd_ref[pl.ds(t * FF_TILE, FF_TILE), :])
        x3_ref[...] = acc
        r3 = lax.rsqrt(jnp.mean(acc * acc, axis=-1, keepdims=True) + EPS)
        diff = (acc * r3) * gf_ref[...] - t_ref[...]
        loss_ref[...] = jnp.full(loss_ref.shape, 0.5 * jnp.sum(jnp.mean(diff * diff, axis=-1)), F32)

    tok = lambda w: pl.BlockSpec((TM, w), lambda i: (i, 0))
    return pl.pallas_call(
        body, name="fwd_ffn", grid=(N_TB,),
        out_shape=(jax.ShapeDtypeStruct((SEQ, D_MODEL), BF16), jax.ShapeDtypeStruct((SEQ, 2 * D_FF), BF16),
                   jax.ShapeDtypeStruct((SEQ, 2 * D_FF), BF16),
                   jax.ShapeDtypeStruct((SEQ, D_FF), BF16), jax.ShapeDtypeStruct((SEQ, D_MODEL), F32),
                   jax.ShapeDtypeStruct((N_TB, 8, 128), F32)),
        in_specs=[tok(D_MODEL), _resident((1, D_MODEL)), _resident((2 * D_FF, D_MODEL)), _resident((8, 2 * D_FF)),
                  _resident((1, 2 * D_FF)), _resident((D_FF, D_MODEL)), _resident((1, D_MODEL)), tok(D_MODEL)],
        out_specs=(tok(D_MODEL), tok(2 * D_FF), tok(2 * D_FF), tok(D_FF), tok(D_MODEL),
                   pl.BlockSpec((1, 8, 128), lambda i: (i, 0, 0))),
        scratch_shapes=[pltpu.VMEM((8, 2 * D_FF), F32)],
        compiler_params=_cparams(("arbitrary",)),
    )(x2, g2, wup_g, cw_g, cb_g, wdn_g, gf, tgt)


def _bwd_ffn(x3, tgt, gf, x2, g2, up_pre, u_conv, wup_g, cw_g, wdn_g):
    def body(x3_ref, t_ref, gf_ref, x2_ref, g2_ref, up_ref, u_ref, wu_ref, cw_ref, wd_ref,
             dx3_ref, dpre_ref, dx2_ref, dgf_ref, dg2_ref, dcv_ref, nxt):
        i = pl.program_id(0)

        @pl.when(i == 0)
        def _():
            nxt[...] = jnp.zeros_like(nxt)
            dgf_ref[...] = jnp.zeros_like(dgf_ref)
            dg2_ref[...] = jnp.zeros_like(dg2_ref)
            dcv_ref[...] = jnp.zeros_like(dcv_ref)

        x3 = x3_ref[...]
        r3 = lax.rsqrt(jnp.mean(x3 * x3, axis=-1, keepdims=True) + EPS)
        xh3 = x3 * r3
        dy = (xh3 * gf_ref[...] - t_ref[...]) * (1.0 / D_MODEL)
        dgf_ref[0:1, :] += jnp.sum(dy * xh3, axis=0, keepdims=True)
        t3 = dy * gf_ref[...]
        dx3 = r3 * (t3 - xh3 * jnp.mean(t3 * xh3, axis=-1, keepdims=True))
        dx3b = dx3.astype(BF16)
        dx3_ref[...] = dx3b
        dh2 = jnp.zeros((TM, D_MODEL), F32)
        row = lax.broadcasted_iota(jnp.int32, (TM, FF_TILE), 0)
        for t in range(N_FF_TILE):
            ts = slice(t * FF_TILE, (t + 1) * FF_TILE)
            dact = _dot_nt(dx3b, wd_ref[pl.ds(t * FF_TILE, FF_TILE), :])
            ua = u_ref[:, ts].astype(F32)
            ub = u_ref[:, D_FF + t * FF_TILE:D_FF + (t + 1) * FF_TILE].astype(F32)
            sg = _sigmoid(ua)
            du = [dact * ub * (sg * (1.0 + ua * (1.0 - sg))), dact * (ua * sg)]
            for n in range(2):
                d = du[n]
                c0 = n * D_FF + t * FF_TILE
                cs = slice(c0, c0 + FF_TILE)
                nx = nxt[:, cs]
                n1 = jnp.where(row == TM - 1, nx[0:1, :], pltpu.roll(d, TM - 1, 0))
                n2 = jnp.where(row == TM - 2, nx[0:1, :], jnp.where(row == TM - 1, nx[1:2, :], pltpu.roll(d, TM - 2, 0)))
                nxt[:, cs] = d[0:8, :]
                dp = (d * cw_ref[2:3, cs] + n1 * cw_ref[1:2, cs] + n2 * cw_ref[0:1, cs]).astype(BF16)
                dpre_ref[:, cs] = dp
                p = up_ref[:, cs].astype(F32)
                dcv_ref[n, 0:1, ts] += jnp.sum(n2 * p, axis=0, keepdims=True)
                dcv_ref[n, 1:2, ts] += jnp.sum(n1 * p, axis=0, keepdims=True)
                dcv_ref[n, 2:3, ts] += jnp.sum(d * p, axis=0, keepdims=True)
                dcv_ref[n, 3:4, ts] += jnp.sum(d, axis=0, keepdims=True)
                dh2 = dh2 + _dot(dp, wu_ref[pl.ds(c0, FF_TILE), :])
        x2 = x2_ref[...]
        r2 = lax.rsqrt(jnp.mean(x2 * x2, axis=-1, keepdims=True) + EPS)
        xh2 = x2 * r2
        dg2_ref[0:1, :] += jnp.sum(dh2 * xh2, axis=0, keepdims=True)
        t2 = dh2 * g2_ref[...]
        dx2_ref[...] = dx3 + r2 * (t2 - xh2 * jnp.mean(t2 * xh2, axis=-1, keepdims=True))

    rev = lambda w: pl.BlockSpec((TM, w), lambda i: (N_TB - 1 - i, 0))
    acc = lambda s: pl.BlockSpec(s, lambda i: (0,) * len(s))
    return pl.pallas_call(
        body, name="bwd_ffn", grid=(N_TB,),
        out_shape=(jax.ShapeDtypeStruct((SEQ, D_MODEL), BF16), jax.ShapeDtypeStruct((SEQ, 2 * D_FF), BF16),
                   jax.ShapeDtypeStruct((SEQ, D_MODEL), F32), jax.ShapeDtypeStruct((8, D_MODEL), F32),
                   jax.ShapeDtypeStruct((8, D_MODEL), F32), jax.ShapeDtypeStruct((2, 8, D_FF), F32)),
        in_specs=[rev(D_MODEL), rev(D_MODEL), _resident((1, D_MODEL)), rev(D_MODEL), _resident((1, D_MODEL)), rev(2 * D_FF),
                  rev(2 * D_FF), _resident((2 * D_FF, D_MODEL)), _resident((8, 2 * D_FF)), _resident((D_FF, D_MODEL))],
        out_specs=(rev(D_MODEL), rev(2 * D_FF), rev(D_MODEL), acc((8, D_MODEL)), acc((8, D_MODEL)), acc((2, 8, D_FF))),
        scratch_shapes=[pltpu.VMEM((8, 2 * D_FF), F32)],
        compiler_params=_cparams(("arbitrary",)),
    )(x3, tgt, gf, x2, g2, up_pre, u_conv, wup_g, cw_g, wdn_g)


def _bwd_mix(dx2, proj, o, sprev, wout_g, grn, lng, lnb, ws, bsb, mask, qdec, kdec, cos2, sin2, hosted):
    cdec = _chunk_decay()
    geoms = [g for g, _ in hosted]
    n_h = len(hosted)

    def body(dx2_ref, p_ref, o_ref, sp_ref, w_ref, grn_ref, lng_ref, lnb_ref, ws_ref, bsb_ref, m_ref, qd_ref, kd_ref,
             cos_ref, sin_ref, *rest):
        dp_ref, dgrn_ref, dlng_ref, dlnb_ref, dws_ref, dbs_ref = rest[n_h:n_h + 6]
        dstate, dbs_acc = rest[2 * n_h + 6:2 * n_h + 8]
        i = pl.program_id(0)
        rs = _Scatters(geoms, rest[:n_h], rest[n_h + 6:2 * n_h + 6], rest[2 * n_h + 8:])
        pl.when(i == 0)(rs.phase1)
        pl.when(i == 3)(rs.phase2)

        @pl.when(i == 0)
        def _():
            dstate[...] = jnp.zeros_like(dstate)
            dgrn_ref[...] = jnp.zeros_like(dgrn_ref)
            dlng_ref[...] = jnp.zeros_like(dlng_ref)
            dlnb_ref[...] = jnp.zeros_like(dlnb_ref)
            dws_ref[...] = jnp.zeros_like(dws_ref)
            dbs_ref[...] = jnp.zeros_like(dbs_ref)
            dbs_acc[...] = jnp.zeros_like(dbs_acc)

        dmix = _dot_nt(dx2_ref[...].astype(BF16), w_ref[...])
        for h in range(HEADS):
            sl = slice(h * HEAD_DIM, (h + 1) * HEAD_DIM)
            q = p_ref[:, sl]
            k = p_ref[:, RET_W + h * HEAD_DIM:RET_W + (h + 1) * HEAD_DIM]
            v = p_ref[:, 2 * RET_W + h * HEAD_DIM:2 * RET_W + (h + 1) * HEAD_DIM]
            g = p_ref[:, 3 * RET_W + h * HEAD_DIM:3 * RET_W + (h + 1) * HEAD_DIM]
            o = o_ref[:, sl]
            rinv = lax.rsqrt(jnp.mean(o * o, axis=-1, keepdims=True) + EPS)
            oh = o * rinv
            gr = grn_ref[:, sl]
            sg = _sigmoid(g)
            dret = dmix[:, sl]
            dp_ref[:, 3 * RET_W + h * HEAD_DIM:3 * RET_W + (h + 1) * HEAD_DIM] = (
                dret * (oh * gr) * (sg * (1.0 + g * (1.0 - sg)))).astype(BF16)
            drn = dret * (g * sg)
            dgrn_ref[0:1, sl] += jnp.sum(drn * oh, axis=0, keepdims=True)
            t = drn * gr
            do = rinv * (t - oh * jnp.mean(t * oh, axis=-1, keepdims=True))
            qb, kb, vb, dob = q.astype(BF16), k.astype(BF16), v.astype(BF16), do.astype(BF16)
            m = m_ref[h]
            ab = (_dot_nt(qb, kb) * m).astype(BF16)
            dab = (_dot_nt(dob, vb) * m).astype(BF16)
            spb = sp_ref[0, h]
            dsn = dstate[h]
            dsnb = dsn.astype(BF16)
            qdb = (q * qd_ref[h]).astype(BF16)
            kdb = (k * kd_ref[h]).astype(BF16)
            dq = _dot(dab, kb) + _dot_nt(dob, spb) * qd_ref[h]
            dk = _dot_tn(dab, qb) + _dot_nt(vb, dsnb) * kd_ref[h]
            dv = _dot_tn(ab, dob) + _dot(kdb, dsnb)
            dstate[h] = dsn * cdec[h] + _dot_tn(qdb, dob)
            c2, s2 = cos_ref[...], sin_ref[...]
            dp_ref[:, sl] = _rot_t(dq, c2, s2).astype(BF16)
            dp_ref[:, RET_W + h * HEAD_DIM:RET_W + (h + 1) * HEAD_DIM] = _rot_t(dk * K_SCALE, c2, s2).astype(BF16)
            dp_ref[:, 2 * RET_W + h * HEAD_DIM:2 * RET_W + (h + 1) * HEAD_DIM] = dv.astype(BF16)
        for gi in range(HEADS):
            sl = slice(gi * HEAD_DIM, (gi + 1) * HEAD_DIM)
            u = p_ref[:, 4 * RET_W + gi * HEAD_DIM:4 * RET_W + (gi + 1) * HEAD_DIM]
            sv = p_ref[:, 4 * RET_W + SGU_W + gi * HEAD_DIM:4 * RET_W + SGU_W + (gi + 1) * HEAD_DIM]
            gv = _gelu(sv)
            xc = gv - jnp.mean(gv, axis=-1, keepdims=True)
            rstd = lax.rsqrt(jnp.mean(xc * xc, axis=-1, keepdims=True) + EPS)
            xh = xc * rstd
            lg = lng_ref[:, sl]
            vnb = (xh * lg + lnb_ref[:, sl]).astype(BF16)
            wcb = _causal(ws_ref[gi]).astype(BF16)
            mixed = _dot(wcb, vnb) + bsb_ref[gi]
            dsgu = dmix[:, RET_W + gi * HEAD_DIM:RET_W + (gi + 1) * HEAD_DIM]
            dmixed = dsgu * _gelu(u)
            dmb = dmixed.astype(BF16)
            dws_ref[gi] += _causal(_dot_nt(dmb, vnb))
            dbs_acc[gi] += dmixed
            dvn = _dot_tn(wcb, dmb)
            dlng_ref[gi:gi + 1, :] += jnp.sum(dvn * xh, axis=0, keepdims=True)
            dlnb_ref[gi:gi + 1, :] += jnp.sum(dvn, axis=0, keepdims=True)
            dxh = dvn * lg
            dgv = rstd * (dxh - jnp.mean(dxh, axis=-1, keepdims=True) - xh * jnp.mean(dxh * xh, axis=-1, keepdims=True))
            dp_ref[:, 4 * RET_W + gi * HEAD_DIM:4 * RET_W + (gi + 1) * HEAD_DIM] = (dsgu * mixed * _gelu_grad(u)).astype(BF16)
            dp_ref[:, 4 * RET_W + SGU_W + gi * HEAD_DIM:4 * RET_W + SGU_W + (gi + 1) * HEAD_DIM] = (
                dgv * _gelu_grad(sv)).astype(BF16)

        @pl.when(i == N_CHUNK - 1)
        def _():
            for gi in range(HEADS):
                col = jnp.broadcast_to(jnp.sum(dbs_acc[gi], axis=-1, keepdims=True), (CHUNK, CHUNK))
                dbs_ref[gi:gi + 1, :] = jnp.transpose(col)[0:1, :]
            rs.phase3()

    rev = lambda w: pl.BlockSpec((CHUNK, w), lambda i: (N_CHUNK - 1 - i, 0))
    hcc = (HEADS, CHUNK, CHUNK)
    acc = lambda s: pl.BlockSpec(s, lambda i: (0,) * len(s))
    res = pl.pallas_call(
        body, name="bwd_mix", grid=(N_CHUNK,),
        out_shape=(jax.ShapeDtypeStruct((SEQ, PROJ_W), BF16), jax.ShapeDtypeStruct((8, RET_W), F32),
                   jax.ShapeDtypeStruct((8, HEAD_DIM), F32), jax.ShapeDtypeStruct((8, HEAD_DIM), F32),
                   jax.ShapeDtypeStruct(hcc, F32), jax.ShapeDtypeStruct((8, CHUNK), F32)) + _scatter_out_shapes(geoms),
        in_specs=[rev(D_MODEL), rev(PROJ_W), rev(RET_W),
                  pl.BlockSpec((1, HEADS, HEAD_DIM, HEAD_DIM), lambda i: (N_CHUNK - 1 - i, 0, 0, 0)),
                  _resident((D_MODEL, D_MODEL)), _resident((1, RET_W)), _resident((1, SGU_W)), _resident((1, SGU_W)),
                  _resident(hcc), _resident(hcc), _resident(hcc), _resident(hcc), _resident(hcc), rev(HEAD_DIM), rev(HEAD_DIM)]
        + [pl.BlockSpec(memory_space=pl.ANY)] * n_h,
        out_specs=(rev(PROJ_W), acc((8, RET_W)), acc((8, HEAD_DIM)), acc((8, HEAD_DIM)), acc(hcc), acc((8, CHUNK)))
        + _scatter_out_specs(geoms),
        scratch_shapes=[pltpu.VMEM((HEADS, HEAD_DIM, HEAD_DIM), F32), pltpu.VMEM((HEADS, CHUNK, CHUNK), F32)] + _scatter_scratch(geoms),
        compiler_params=_cparams(("arbitrary",)),
    )(dx2, proj, o, sprev, wout_g, grn, lng, lnb, ws, bsb, mask, qdec, kdec, cos2, sin2, *[p for _, p in hosted])
    return tuple(res[:6 + n_h])


def _bwd_proj(dproj, win_g, x, g1, dx2, gin_p):
    geoms = [W_IN]

    def body(dp_ref, w_ref, x_ref, g_ref, dx2_ref, gin_ref, dx_ref, dg_ref, rs_out, *rs_scratch):
        rs = _Scatters(geoms, [gin_ref], [rs_out], rs_scratch)
        pl.when(pl.program_id(0) == 0)(rs.phase1)
        pl.when(pl.program_id(0) == 2)(rs.phase2)

        @pl.when(pl.program_id(0) == 0)
        def _():
            dg_ref[...] = jnp.zeros_like(dg_ref)

        dh = _dot_nt(dp_ref[...], w_ref[...])
        xb = x_ref[...]
        r = lax.rsqrt(jnp.mean(xb * xb, axis=-1, keepdims=True) + EPS)
        xh = xb * r
        dg_ref[0:1, :] += jnp.sum(dh * xh, axis=0, keepdims=True)
        t = dh * g_ref[...]
        dx_ref[...] = dx2_ref[...] + r * (t - xh * jnp.mean(t * xh, axis=-1, keepdims=True))
        pl.when(pl.program_id(0) == N_TB - 1)(rs.phase3)

    tok = lambda w: pl.BlockSpec((TM, w), lambda i: (i, 0))
    res = pl.pallas_call(
        body, name="bwd_proj", grid=(N_TB,),
        out_shape=(jax.ShapeDtypeStruct((SEQ, D_MODEL), F32), jax.ShapeDtypeStruct((8, D_MODEL), F32)) + _scatter_out_shapes(geoms),
        in_specs=[tok(PROJ_W), _resident((D_MODEL, PROJ_W)), tok(D_MODEL), _resident((1, D_MODEL)), tok(D_MODEL),
                  pl.BlockSpec(memory_space=pl.ANY)],
        out_specs=(tok(D_MODEL), pl.BlockSpec((8, D_MODEL), lambda i: (0, 0))) + _scatter_out_specs(geoms),
        scratch_shapes=_scatter_scratch(geoms),
        compiler_params=_cparams(("arbitrary",)),
    )(dproj, win_g, x, g1, dx2, gin_p)
    return res[:3]


def _wgrad(name, a, b, tm=None, tn=None, hosted=()):
    m_w, n_w = a.shape[-1], b.shape[-1]
    tm = m_w if tm is None else tm
    tn = n_w if tn is None else tn
    n_steps = (m_w // tm) * (n_w // tn)
    geoms = [g for g, _ in hosted]
    n_h = len(hosted)

    def body(a_ref, b_ref, *rest):
        o_ref = rest[n_h]
        if n_h:
            rs = _Scatters(geoms, rest[:n_h], rest[n_h + 1:2 * n_h + 1], rest[2 * n_h + 1:])
            step = pl.program_id(0) * (n_w // tn) + pl.program_id(1)
            pl.when(step == 0)(rs.phase1)
            pl.when(step == 1)(rs.phase2)
        o_ref[...] = _dot_tn(a_ref[...].astype(BF16), b_ref[...].astype(BF16)).astype(BF16)
        if n_h:
            pl.when(step == n_steps - 1)(rs.phase3)

    assert not n_h or n_steps >= 3
    res = pl.pallas_call(
        body, name=name, grid=(m_w // tm, n_w // tn),
        out_shape=(jax.ShapeDtypeStruct((m_w, n_w), BF16),) + _scatter_out_shapes(geoms),
        in_specs=[pl.BlockSpec((SEQ, tm), lambda i, j: (0, i)), pl.BlockSpec((SEQ, tn), lambda i, j: (0, j))]
        + [pl.BlockSpec(memory_space=pl.ANY)] * n_h,
        out_specs=(pl.BlockSpec((tm, tn), lambda i, j: (i, j)),) + _scatter_out_specs(geoms),
        scratch_shapes=_scatter_scratch(geoms),
        compiler_params=_cparams(("arbitrary", "arbitrary") if n_h else ("parallel", "parallel")),
    )(a, b, *[p for _, p in hosted])
    return tuple(res[:1 + n_h])


RS_ROWS = {W_IN: 128, W_OUT: 128, W_UP: 176, W_DOWN: 176}


class _Scatter:
    def __init__(self, geom, partial, out, land1, mine, stage2, land2, s1_send, s1_recv, s2_send, s2_recv, ld_sems):
        self.w, self.row0, self.shape = _geom(geom)
        self.partial, self.out, self.land1 = partial, out, land1
        self.mine, self.stage2, self.land2 = mine, stage2, land2
        self.s1_send, self.s1_recv, self.s2_send, self.s2_recv, self.ld_sems = s1_send, s1_recv, s2_send, s2_recv, ld_sems
        self.x, self.y, self.c = lax.axis_index("x"), lax.axis_index("y"), lax.axis_index("c")
        self.sibling = (self.x, self.y, 1 - self.c)
        self.chips = [(self.x, self.y), (1 - self.x, self.y), (self.x, 1 - self.y), (1 - self.x, 1 - self.y)]

    def block(self, px, py, pc):
        dev = 4 * px + 2 * py + pc
        if self.w == W_IN:
            return self.partial.at[:, pl.ds(pl.multiple_of(dev * IN_SHARD, 128), IN_SHARD)]
        if self.w == W_OUT:
            return self.partial.at[pl.ds(pl.multiple_of(dev * OUT_SHARD, 128), OUT_SHARD), :]
        if self.w == W_DOWN:
            return self.partial.at[pl.ds(pl.multiple_of(dev * DOWN_SHARD, 32), DOWN_SHARD), :]
        return self.partial.at[pl.ds(pl.multiple_of(dev * FF_SHARD + self.row0, 32), self.shape[0]), :]

    def copy1(self, k):
        return pltpu.make_async_remote_copy(
            src_ref=self.block(*self.chips[k], 1 - self.c), dst_ref=self.land1.at[k],
            send_sem=self.s1_send.at[k], recv_sem=self.s1_recv.at[k], device_id=self.sibling, device_id_type=MESH)

    def copy2(self, k):
        return pltpu.make_async_remote_copy(
            src_ref=self.stage2.at[k - 1], dst_ref=self.land2.at[k - 1],
            send_sem=self.s2_send.at[k - 1], recv_sem=self.s2_recv.at[k - 1], device_id=(*self.chips[k], self.c), device_id_type=MESH)

    def _rows(self):
        step = RS_ROWS[self.w]
        return [pl.ds(r0, step) for r0 in range(0, self.shape[0], step)]

    def load(self, k):
        return pltpu.make_async_copy(self.block(*self.chips[k], self.c), self.mine.at[k], self.ld_sems.at[k])

    def phase1(self):
        for k in range(4):
            self.copy1(k).start()
        for k in range(4):
            self.load(k).start()

    def phase2(self):
        for k in (3, 1, 2, 0):
            self.copy1(k).wait_recv()
            self.load(k).wait()
            for rs in self._rows():
                s = self.mine[k, rs, :].astype(F32) + self.land1[k, rs, :].astype(F32)
                if k == 0:
                    self.out[rs, :] = s
                else:
                    self.stage2[k - 1, rs, :] = s.astype(BF16)
            if k:
                self.copy2(k).start()

    def phase3(self):
        for k in (1, 2, 3):
            self.copy2(k).wait_recv()
        for rs in self._rows():
            self.out[rs, :] = ((self.out[rs, :] + self.land2[0, rs, :].astype(F32)) + self.land2[1, rs, :].astype(F32)) \
                + self.land2[2, rs, :].astype(F32)
        for k in range(4):
            self.copy1(k).wait_send()
        for k in (1, 2, 3):
            self.copy2(k).wait_send()


def _geom(geom):
    if isinstance(geom, tuple):
        w, row0, rows = geom
        assert w == W_UP
        return w, row0, (rows, SHARD[w][1])
    return geom, 0, SHARD[geom]


N_SCATTER_SCRATCH = 9


def _scatter_out_shapes(geoms):
    return tuple(jax.ShapeDtypeStruct(_geom(g)[2], F32) for g in geoms)


def _scatter_out_specs(geoms):
    return (pl.BlockSpec(memory_space=pltpu.VMEM),) * len(geoms)


def _scatter_scratch(geoms):
    out = []
    for g in geoms:
        s = _geom(g)[2]
        out += [pltpu.VMEM((4,) + s, BF16), pltpu.VMEM((4,) + s, BF16), pltpu.VMEM((3,) + s, BF16), pltpu.VMEM((3,) + s, BF16),
                pltpu.SemaphoreType.DMA((4,)), pltpu.SemaphoreType.DMA((4,)), pltpu.SemaphoreType.DMA((3,)),
                pltpu.SemaphoreType.DMA((3,)), pltpu.SemaphoreType.DMA((4,))]
    return out


class _Scatters:
    def __init__(self, geoms, p_refs, out_refs, scratch):
        k = N_SCATTER_SCRATCH
        self.items = [_Scatter(g, p_refs[i], out_refs[i], *scratch[k * i:k * i + k]) for i, g in enumerate(geoms)]

    def phase1(self):
        for s in self.items:
            s.phase1()

    def phase2(self):
        for s in self.items:
            s.phase2()

    def phase3(self):
        for s in self.items:
            s.phase3()


PACK_W = 1024


def _all_reduce_small(dg1, dg2, dgf, dgrn, dlng, dlnb, dbs, loss_parts, dws, dcv):
    n_a = 3

    def body(dg1_ref, dg2_ref, dgf_ref, dgrn_ref, dlng_ref, dlnb_ref, dbs_ref, loss_ref, dws_ref, dcv_ref,
             rp_ref, rws_ref, rcv_ref, pack, rx_p, rx_ws, rx_cv, cs_p, cs_ws, cs_cv, g_p, g_ws, g_cv,
             s1_send, s1_recv, s2_send, s2_recv, s3_send, s3_recv):
        x, y, c = lax.axis_index("x"), lax.axis_index("y"), lax.axis_index("c")
        sibling = (x, y, 1 - c)
        chips = [(1 - x, y), (x, 1 - y), (1 - x, 1 - y)]
        pack[...] = jnp.zeros_like(pack)
        pack[0, 0:1, :] = dg1_ref[0:1, :]
        pack[0, 1:2, :] = dg2_ref[0:1, :]
        pack[0, 2:3, :] = dgf_ref[0:1, :]
        pack[0, 3:4, 0:RET_W] = dgrn_ref[0:1, :]
        lsum = loss_ref[0, 0:1, :]
        for i in range(1, N_TB):
            lsum = lsum + loss_ref[i, 0:1, :]
        pack[0, 3:4, RET_W:RET_W + 128] = lsum
        pack[1, 0:HEADS, 0:128] = dlng_ref[0:HEADS, :]
        pack[1, 0:HEADS, 128:256] = dlnb_ref[0:HEADS, :]
        pack[1, 0:HEADS, 256:384] = dbs_ref[0:HEADS, :]

        srcs = [pack, dws_ref, dcv_ref]
        outs = [rp_ref, rws_ref, rcv_ref]
        rxs = [rx_p, rx_ws, rx_cv]
        css = [cs_p, cs_ws, cs_cv]
        gs = [g_p, g_ws, g_cv]
        hl = [1, HEADS // 2, 1]

        def half(ref, a, h):
            return ref.at[pl.ds(h * hl[a], hl[a])]

        ex1 = [pltpu.make_async_remote_copy(src_ref=half(srcs[a], a, 1 - c), dst_ref=rxs[a], send_sem=s1_send.at[a],
                                            recv_sem=s1_recv.at[a], device_id=sibling, device_id_type=MESH) for a in range(n_a)]
        for cp in ex1:
            cp.start()
        ex2 = []
        for a in range(n_a):
            ex1[a].wait_recv()
            css[a][...] = half(srcs[a], a, c)[...] + rxs[a][...]
            for j, chip in enumerate(chips):
                cp = pltpu.make_async_remote_copy(src_ref=css[a], dst_ref=gs[a].at[j], send_sem=s2_send.at[a, j],
                                                  recv_sem=s2_recv.at[a, j], device_id=(*chip, c), device_id_type=MESH)
                cp.start()
                ex2.append(cp)
        ex3 = []
        for a in range(n_a):
            for j in range(3):
                ex2[3 * a + j].wait_recv()
            tot = None
            for q in range(4):
                k = jnp.where(x != (q >> 1), 1, 0) + jnp.where(y != (q & 1), 2, 0)
                term = jnp.where(k == 0, css[a][...], jnp.where(k == 1, gs[a][0], jnp.where(k == 2, gs[a][1], gs[a][2])))
                tot = term if tot is None else tot + term
            half(outs[a], a, c)[...] = tot
            cp = pltpu.make_async_remote_copy(src_ref=half(outs[a], a, c), dst_ref=half(outs[a], a, c), send_sem=s3_send.at[a],
                                              recv_sem=s3_recv.at[a], device_id=sibling, device_id_type=MESH)
            cp.start()
            ex3.append(cp)
        for a in range(n_a):
            pltpu.make_async_remote_copy(src_ref=half(outs[a], a, 1 - c), dst_ref=half(outs[a], a, 1 - c), send_sem=s3_send.at[a],
                                         recv_sem=s3_recv.at[a], device_id=sibling, device_id_type=MESH).wait_recv()
        for cp in ex1 + ex2 + ex3:
            cp.wait_send()

    vm = pl.BlockSpec(memory_space=pltpu.VMEM)
    full = [(2, 8, PACK_W), (HEADS, CHUNK, CHUNK), (2, 8, D_FF)]
    halves = [(s[0] // 2,) + s[1:] for s in full]
    return pl.pallas_call(
        body, name="ar_small",
        out_shape=tuple(jax.ShapeDtypeStruct(s, F32) for s in full),
        in_specs=[vm] * 10, out_specs=(vm,) * 3,
        scratch_shapes=[pltpu.VMEM(full[0], F32)] + [pltpu.VMEM(s, F32) for s in halves] + [pltpu.VMEM(s, F32) for s in halves]
        + [pltpu.VMEM((3,) + s, F32) for s in halves]
        + [pltpu.SemaphoreType.DMA((n_a,)), pltpu.SemaphoreType.DMA((n_a,)), pltpu.SemaphoreType.DMA((n_a, 3)),
           pltpu.SemaphoreType.DMA((n_a, 3)), pltpu.SemaphoreType.DMA((n_a,)), pltpu.SemaphoreType.DMA((n_a,))],
        compiler_params=_cparams(),
    )(dg1, dg2, dgf, dgrn, dlng, dlnb, dbs, loss_parts, dws, dcv)


def _adam_math(w, g, m, v):
    nm = ADAM_B1 * m + (1.0 - ADAM_B1) * g
    nv = ADAM_B2 * v + (1.0 - ADAM_B2) * (g * g)
    d = -ADAM_LR * ((nm / (1.0 - ADAM_B1 ** ADAM_STEP)) / (jnp.sqrt(nv / (1.0 - ADAM_B2 ** ADAM_STEP)) + ADAM_EPS) + ADAM_WD * w)
    return d, nm, nv


def _adamw(name, w, gs, m, v, rows):
    _, r, cdim = w.shape
    n_steps = r // rows
    half = n_steps // len(gs)

    def body(w_ref, *rest):
        g_refs, (m_ref, v_ref, go_ref, d_ref, nm_ref, nv_ref) = rest[:len(gs)], rest[len(gs):]
        gg = g_refs[0][...]
        if len(gs) == 2:
            gg = jnp.where(pl.program_id(0) < half, gg, g_refs[1][...])
        go_ref[0] = gg
        d, nm, nv = _adam_math(w_ref[0], gg, m_ref[0], v_ref[0])
        d_ref[0], nm_ref[0], nv_ref[0] = d, nm, nv

    spec3 = pl.BlockSpec((1, rows, cdim), lambda i: (0, i, 0))
    if len(gs) == 1:
        g_specs = [pl.BlockSpec((rows, cdim), lambda i: (i, 0))]
    else:
        g_specs = [pl.BlockSpec((rows, cdim), lambda i: (jnp.minimum(i, half - 1), 0)),
                   pl.BlockSpec((rows, cdim), lambda i: (jnp.maximum(i - half, 0), 0))]
    sh = jax.ShapeDtypeStruct((1, r, cdim), F32)
    return pl.pallas_call(
        body, name=name, grid=(n_steps,), out_shape=(sh, sh, sh, sh),
        in_specs=[spec3] + g_specs + [spec3, spec3], out_specs=(spec3,) * 4,
        compiler_params=_cparams(("parallel",)),
    )(w, *gs, m, v)


def _adamw_small(rp, rws, rcv, gcw, params):
    n_p = len(params)

    def body(*refs):
        rp_ref, rws_ref, rcv_ref, gcw_ref = refs[:4]
        ins = refs[4:4 + 3 * n_p]
        outs = refs[4 + 3 * n_p:]
        grads = [rp_ref[0, 0:1, :], rp_ref[0, 1:2, :], rp_ref[0, 2:3, :], rp_ref[0, 3:4, 0:RET_W],
                 rp_ref[1, 0:HEADS, 0:128], rp_ref[1, 0:HEADS, 128:256], rp_ref[1, 0:HEADS, 256:384],
                 rws_ref[...], gcw_ref[0], None]
        for p in range(n_p):
            w_ref, m_ref, v_ref = ins[3 * p:3 * p + 3]
            o = outs[4 * p:4 * p + 4]
            if p == n_p - 1:
                for hf in range(2):
                    cs = slice(hf * D_FF, (hf + 1) * D_FF)
                    g = rcv_ref[hf, 3:4, :]
                    res = (g,) + _adam_math(w_ref[:, cs], g, m_ref[:, cs], v_ref[:, cs])
                    for t in range(4):
                        o[t][:, cs] = res[t]
                continue
            lead = w_ref.ndim > grads[p].ndim
            rd = (lambda r: r[0]) if lead else (lambda r: r[...])
            res = (grads[p],) + _adam_math(rd(w_ref), grads[p], rd(m_ref), rd(v_ref))
            for t in range(4):
                if lead:
                    o[t][0] = res[t]
                else:
                    o[t][...] = res[t]

    vm = pl.BlockSpec(memory_space=pltpu.VMEM)
    flat = [a for tr in params for a in tr]
    out_shape = tuple(jax.ShapeDtypeStruct(tr[0].shape, F32) for tr in params for _ in range(4))
    res = pl.pallas_call(
        body, name="adamw_small", out_shape=out_shape, in_specs=[vm] * (4 + len(flat)), out_specs=(vm,) * len(out_shape),
        compiler_params=_cparams(),
    )(rp, rws, rcv, gcw, *flat)
    return [res[4 * p:4 * p + 4] for p in range(n_p)]


def kernel(x, mix_norm_g, w_in, ret_norm_g, sgu_ln_g, sgu_ln_b, sgu_w_s, sgu_b_s, w_out, ffn_norm_g, w_up, conv_w, conv_b, w_down, final_norm_g, loss_target, m_mix_norm_g, m_w_in, m_ret_norm_g, m_sgu_ln_g, m_sgu_ln_b, m_sgu_w_s, m_sgu_b_s, m_w_out, m_ffn_norm_g, m_w_up, m_conv_w, m_conv_b, m_w_down, m_final_norm_g, v_mix_norm_g, v_w_in, v_ret_norm_g, v_sgu_ln_g, v_sgu_ln_b, v_sgu_w_s, v_sgu_b_s, v_w_out, v_ffn_norm_g, v_w_up, v_conv_w, v_conv_b, v_w_down, v_final_norm_g):
    xs = x[0]
    tgt = loss_target[0]
    cos2, sin2 = _rope_tables()
    mask, qdec, kdec = _decay_tables()
    grn = ret_norm_g.reshape(1, RET_W)
    lng = sgu_ln_g.reshape(1, SGU_W)
    lnb = sgu_ln_b.reshape(1, SGU_W)
    ws = sgu_w_s[0]
    bsb = jnp.broadcast_to(sgu_b_s[0][:, :, None], (HEADS, CHUNK, HEAD_DIM))
    gf = final_norm_g.reshape(1, D_MODEL)
    me = 4 * lax.axis_index("x") + 2 * lax.axis_index("y") + lax.axis_index("c")
    tr = lambda a: jnp.transpose(a[0])[None]

    win_g, cw_sh, so, su, sd = _ag_first(w_in[0], w_out[0], tr(w_up)[0], w_down[0], conv_w[0])
    cw_g = jnp.transpose(cw_sh, (1, 0, 2)).reshape(8, 2 * D_FF)

    proj, h1, wout_g, wdn_g = _fwd_proj(xs, mix_norm_g, win_g, cos2, sin2, so, sd)
    x2, mixcat, o, sprev, wup_g = _fwd_mix(xs, proj, wout_g, grn, lng, lnb, ws, bsb, mask, qdec, kdec, su)
    h2, up_pre, u_conv, act, x3, loss_parts = _fwd_ffn(x2, ffn_norm_g, wup_g, cw_g, conv_b, wdn_g, gf, tgt)

    dx3, dpre, dx2, dgf, dg2, dcv = _bwd_ffn(x3, tgt, gf, x2, ffn_norm_g, up_pre, u_conv, wup_g, cw_g, wdn_g)
    band = FF_SHARD // 2
    (gdn_p,) = _wgrad("wgrad_down", act, dx3, tm=FF_TILE)
    (gout_p,) = _wgrad("wgrad_out", mixcat, dx2, tn=512)
    gup_p, g_dn = _wgrad("wgrad_up", dpre, h2, tm=FF_TILE, hosted=[(W_DOWN, gdn_p)])
    dproj, dgrn, dlng, dlnb, dws, dbs, g_up_a, g_out = _bwd_mix(
        dx2, proj, o, sprev, wout_g, grn, lng, lnb, ws, bsb, mask, qdec, kdec, cos2, sin2,
        [((W_UP, 0, band), gup_p), (W_OUT, gout_p)])
    gin_p, g_up_b = _wgrad("wgrad_in", h1, dproj, tn=768, hosted=[((W_UP, band, band), gup_p)])
    grad_x, dg1, g_in = _bwd_proj(dproj, win_g, xs, mix_norm_g, dx2, gin_p)
    rp, rws, rcv = _all_reduce_small(dg1, dg2, dgf, dgrn, dlng, dlnb, dbs, loss_parts, dws, dcv)
    loss = rp[0, 3, RET_W]
    gcw = lax.dynamic_slice(rcv, (me // (N_DEV // 2), 0, (me % (N_DEV // 2)) * FF_SHARD), (1, 3, FF_SHARD))

    table = {}
    for name, w, gs, m, v, rows in (("w_in", w_in, [g_in], m_w_in, v_w_in, 256), ("w_out", w_out, [g_out], m_w_out, v_w_out, 128),
                                    ("w_up", tr(w_up), [g_up_a, g_up_b], tr(m_w_up), tr(v_w_up), 176),
                                    ("w_down", w_down, [g_dn], m_w_down, v_w_down, 88)):
        table[name] = _adamw("adamw_" + name, w, gs, m, v, rows)
    table["w_up"] = tuple(tr(a) for a in table["w_up"])
    row = lambda a: a.reshape(1, D_MODEL)
    names_small = ["mix_norm_g", "ffn_norm_g", "final_norm_g", "ret_norm_g", "sgu_ln_g", "sgu_ln_b", "sgu_b_s", "sgu_w_s",
                   "conv_w", "conv_b"]
    params = [(mix_norm_g, m_mix_norm_g, v_mix_norm_g), (ffn_norm_g, m_ffn_norm_g, v_ffn_norm_g),
              (row(final_norm_g), row(m_final_norm_g), row(v_final_norm_g)), (ret_norm_g, m_ret_norm_g, v_ret_norm_g),
              (sgu_ln_g, m_sgu_ln_g, v_sgu_ln_g), (sgu_ln_b, m_sgu_ln_b, v_sgu_ln_b), (sgu_b_s, m_sgu_b_s, v_sgu_b_s),
              (sgu_w_s, m_sgu_w_s, v_sgu_w_s), (conv_w, m_conv_w, v_conv_w), (conv_b, m_conv_b, v_conv_b)]
    for n, res in zip(names_small, _adamw_small(rp, rws, rcv, gcw, params)):
        table[n] = res
    table["final_norm_g"] = tuple(a.reshape(D_MODEL) for a in table["final_norm_g"])

    order = ["mix_norm_g", "w_in", "ret_norm_g", "sgu_ln_g", "sgu_ln_b", "sgu_w_s", "sgu_b_s", "w_out", "ffn_norm_g", "w_up",
             "conv_w", "conv_b", "w_down", "final_norm_g"]
    outs = [loss, grad_x[None]]
    for col in range(4):
        outs += [table[n][col] for n in order]
    return tuple(outs)
```

```python
import functools
import math

import jax
import jax.numpy as jnp
import numpy as np
from jax import lax
from jax.experimental import pallas as pl
from jax.experimental.pallas import tpu as pltpu

F32 = jnp.float32
BF16 = jnp.bfloat16
MESH = pl.DeviceIdType.MESH

N_DEV = 8
SEQ = 2048
D_MODEL = 1024
CHUNK = 128
N_CHUNK = SEQ // CHUNK
HEADS = 4
HEAD_DIM = 128
RET_W = 512
SGU_W = 512
PROJ_W = 3072
D_FF = 2816
FF_SHARD = 704
FF_TILE = 1408
N_FF_TILE = D_FF // FF_TILE
IN_SHARD = PROJ_W // N_DEV
OUT_SHARD = D_MODEL // N_DEV
DOWN_SHARD = D_FF // N_DEV
TM = 256
N_TB = SEQ // TM
EPS = 1e-6
ROPE_BASE = 10000.0
K_SCALE = HEAD_DIM ** -0.5
INV_SQRT2 = 0.7071067811865476
INV_SQRT_2PI = 0.3989422804014327

ADAM_LR = 0.001
ADAM_B1 = 0.9
ADAM_B2 = 0.999
ADAM_EPS = 1e-08
ADAM_WD = 0.01
ADAM_STEP = 10

VMEM_LIMIT = 56 * 1024 * 1024


def _cparams(sem=None, vmem=VMEM_LIMIT):
    return pltpu.CompilerParams(dimension_semantics=sem, vmem_limit_bytes=vmem)


def _resident(shape):
    nd = len(shape)
    return pl.BlockSpec(shape, lambda *_: (0,) * nd, pipeline_mode=pl.Buffered(1))


def _dot(a, b):
    return jnp.dot(a, b, preferred_element_type=F32)


def _dot_nt(a, b):
    return lax.dot_general(a, b, (((1,), (1,)), ((), ())), preferred_element_type=F32)


def _dot_tn(a, b):
    return lax.dot_general(a, b, (((0,), (0,)), ((), ())), preferred_element_type=F32)


def _sigmoid(x):
    return 1.0 / (1.0 + jnp.exp(-x))


def _gelu(x):
    return 0.5 * x * (1.0 + lax.erf(x * INV_SQRT2))


def _gelu_grad(x):
    return 0.5 * (1.0 + lax.erf(x * INV_SQRT2)) + x * (jnp.exp(-0.5 * x * x) * INV_SQRT_2PI)


def _rot(xh, cos2, sin2):
    return xh * cos2 + pltpu.roll(xh, HEAD_DIM // 2, 1) * sin2


def _rot_t(dh, cos2, sin2):
    return dh * cos2 + pltpu.roll(dh * sin2, HEAD_DIM // 2, 1)


def _rope_tables():
    half = HEAD_DIM // 2
    inv_freq = jnp.power(ROPE_BASE, -jnp.arange(half, dtype=F32) / half)
    ang = jnp.arange(SEQ, dtype=F32)[:, None] * inv_freq[None, :]
    cos, sin = jnp.cos(ang), jnp.sin(ang)
    cos2 = jnp.concatenate([cos, cos], axis=-1)
    sin2 = jnp.concatenate([-sin, sin], axis=-1)
    return cos2, sin2


def _decay_tables():
    log_gamma = jnp.log(1.0 - jnp.power(2.0, -5.0 - jnp.arange(HEADS, dtype=F32)))
    pos = jnp.arange(CHUNK, dtype=F32)
    diff = pos[:, None] - pos[None, :]
    mask = jnp.where(diff >= 0.0, jnp.exp(log_gamma[:, None, None] * jnp.maximum(diff, 0.0)[None]), 0.0)
    k_decay = jnp.exp(log_gamma[:, None] * (CHUNK - 1.0 - pos)[None])
    q_decay = jnp.exp(log_gamma[:, None] * (pos + 1.0)[None])
    kd = jnp.broadcast_to(k_decay[:, :, None], (HEADS, CHUNK, HEAD_DIM))
    qd = jnp.broadcast_to(q_decay[:, :, None], (HEADS, CHUNK, HEAD_DIM))
    return mask.astype(F32), qd.astype(F32), kd.astype(F32)


def _chunk_decay():
    lg = np.log(np.float32(1.0) - np.power(np.float32(2.0), -5.0 - np.arange(HEADS, dtype=np.float32))).astype(np.float32)
    return [float(np.exp(lg[h] * np.float32(CHUNK))) for h in range(HEADS)]


W_IN, W_OUT, W_UP, W_DOWN, W_CONV = range(5)
GATHERED = {W_IN: ((D_MODEL, PROJ_W), BF16), W_OUT: ((D_MODEL, D_MODEL), BF16), W_UP: ((2 * D_FF, D_MODEL), BF16),
            W_DOWN: ((D_FF, D_MODEL), BF16), W_CONV: ((N_DEV, 8, FF_SHARD), F32)}
SHARD = {W_IN: (D_MODEL, IN_SHARD), W_OUT: (OUT_SHARD, D_MODEL), W_UP: (FF_SHARD, D_MODEL), W_DOWN: (DOWN_SHARD, D_MODEL),
         W_CONV: (8, FF_SHARD)}


class _Gather:
    N_SEMS = 9

    def __init__(self, ids, stages, gathered, send_sems, recv_sems, local_sems):
        self.ids, self.stages, self.gathered = ids, stages, gathered
        self.send_sems, self.recv_sems, self.local_sems = send_sems, recv_sems, local_sems
        self.x, self.y, self.c = lax.axis_index("x"), lax.axis_index("y"), lax.axis_index("c")
        self.me = (self.x, self.y, self.c)
        self.sibling = (self.x, self.y, 1 - self.c)
        self.chips = [(1 - self.x, self.y), (self.x, 1 - self.y), (1 - self.x, 1 - self.y)]

    def slot(self, n, px, py, pc):
        dev = 4 * px + 2 * py + pc
        w, g = self.ids[n], self.gathered[n]
        if w == W_IN:
            return g.at[:, pl.ds(pl.multiple_of(dev * IN_SHARD, 128), IN_SHARD)]
        if w == W_OUT:
            return g.at[pl.ds(pl.multiple_of(dev * OUT_SHARD, 128), OUT_SHARD), :]
        if w == W_DOWN:
            return g.at[pl.ds(pl.multiple_of(dev * DOWN_SHARD, 32), DOWN_SHARD), :]
        if w == W_UP:
            return g.at[pl.ds(pl.multiple_of(dev * FF_SHARD, 32), FF_SHARD), :]
        return g.at[dev]

    def half(self, n, px, py, pc, h):
        dev = 4 * px + 2 * py + pc
        w, g = self.ids[n], self.gathered[n]
        if w == W_IN:
            return g.at[pl.ds(h * (D_MODEL // 2), D_MODEL // 2), pl.ds(pl.multiple_of(dev * IN_SHARD, 128), IN_SHARD)]
        rows = SHARD[w][0] // 2
        return g.at[pl.ds(pl.multiple_of(dev * SHARD[w][0] + h * rows, 16), rows), :]

    def tree(self, n):
        return self.ids[n] != W_CONV

    def copy(self, n, k, block, to, src=None, h=None):
        ref = self.slot(n, *block) if h is None else self.half(n, *block, h)
        return pltpu.make_async_remote_copy(
            src_ref=ref if src is None else src, dst_ref=ref,
            send_sem=self.send_sems.at[n, k], recv_sem=self.recv_sems.at[n, k], device_id=to, device_id_type=MESH)

    def _mine(self):
        return [pltpu.make_async_copy(self.stages[n], self.slot(n, *self.me), self.local_sems.at[n]) for n in range(len(self.ids))]

    def _first(self):
        out = []
        for n in range(len(self.ids)):
            out.append(self.copy(n, 0, self.me, self.sibling, src=self.stages[n]))
            out += [self.copy(n, 1 + j, self.me, (*chip, self.c), src=self.stages[n])
                    for j, chip in enumerate(self.chips[:2] if self.tree(n) else self.chips)]
        return out

    def start(self):
        for cp in self._mine() + self._first():
            cp.start()

    def finish(self):
        cx, cy, cd = [(*chip, self.c) for chip in self.chips]
        ox, oy, od = [(*chip, 1 - self.c) for chip in self.chips]
        passed = []

        def go(cp):
            cp.start()
            passed.append(cp)

        for n in range(len(self.ids)):
            if self.tree(n):
                self.copy(n, 1, cx, self.me).wait_recv()
                go(self.copy(n, 3, cx, cy, h=0))
                go(self.copy(n, 5, cx, self.sibling))
                self.copy(n, 2, cy, self.me).wait_recv()
                go(self.copy(n, 4, cy, cx, h=1))
                go(self.copy(n, 6, cy, self.sibling))
            else:
                for j, dev in enumerate((cx, cy, cd)):
                    self.copy(n, 1 + j, dev, self.me).wait_recv()
                    go(self.copy(n, 4 + j, dev, self.sibling))
        for n in range(len(self.ids)):
            if self.tree(n):
                self.copy(n, 3, cd, self.me, h=0).wait_recv()
                go(self.copy(n, 7, cd, self.sibling, h=0))
                self.copy(n, 4, cd, self.me, h=1).wait_recv()
                go(self.copy(n, 8, cd, self.sibling, h=1))
        for n in range(len(self.ids)):
            self.copy(n, 0, self.sibling, self.me).wait_recv()
            if self.tree(n):
                self.copy(n, 5, ox, self.me).wait_recv()
                self.copy(n, 6, oy, self.me).wait_recv()
                self.copy(n, 7, od, self.me, h=0).wait_recv()
                self.copy(n, 8, od, self.me, h=1).wait_recv()
            else:
                for j, dev in enumerate((ox, oy, od)):
                    self.copy(n, 4 + j, dev, self.me).wait_recv()
        for cp in self._first() + passed:
            cp.wait_send()
        for cp in self._mine():
            cp.wait()


def _gather_scratch(n):
    return [pltpu.SemaphoreType.DMA((n, _Gather.N_SEMS)), pltpu.SemaphoreType.DMA((n, _Gather.N_SEMS)), pltpu.SemaphoreType.DMA((n,))]


def _gathered_shapes(ids):
    return tuple(jax.ShapeDtypeStruct(*GATHERED[w]) for w in ids)


def _ag_first(w_in, w_out, w_up, w_down, conv_w):
    ids = [W_IN, W_CONV]

    def body(in_ref, out_ref, up_ref, dn_ref, cw_ref, gin, gcw, so_ref, su_ref, sd_ref, s_in, s_cw, send_sems, recv_sems, local_sems):
        s_in[...] = in_ref[...].astype(BF16)
        s_cw[...] = jnp.zeros_like(s_cw)
        s_cw[0:3, :] = cw_ref[...]
        ag = _Gather(ids, [s_in, s_cw], [gin, gcw], send_sems, recv_sems, local_sems)
        ag.start()
        so_ref[...] = out_ref[...].astype(BF16)
        su_ref[...] = up_ref[...].astype(BF16)
        sd_ref[...] = dn_ref[...].astype(BF16)
        ag.finish()

    vm = pl.BlockSpec(memory_space=pltpu.VMEM)
    hbm = pl.BlockSpec(memory_space=pl.ANY)
    return pl.pallas_call(
        body, name="ag_first",
        out_shape=_gathered_shapes(ids) + tuple(jax.ShapeDtypeStruct(SHARD[w], BF16) for w in (W_OUT, W_UP, W_DOWN)),
        in_specs=[vm] * 5, out_specs=(hbm, hbm, vm, vm, vm),
        scratch_shapes=[pltpu.VMEM(SHARD[W_IN], BF16), pltpu.VMEM(SHARD[W_CONV], F32)] + _gather_scratch(len(ids)),
        compiler_params=_cparams(),
    )(w_in, w_out, w_up, w_down, conv_w)


def _fwd_proj(x, g1, win_g, cos2, sin2, so, sd):
    ids = [W_OUT, W_DOWN]

    def body(x_ref, g_ref, w_ref, cos_ref, sin_ref, so_ref, sd_ref, proj_ref, h1_ref, gout, gdn, send_sems, recv_sems, local_sems):
        ag = _Gather(ids, [so_ref, sd_ref], [gout, gdn], send_sems, recv_sems, local_sems)

        @pl.when(pl.program_id(0) == 0)
        def _():
            ag.start()

        xb = x_ref[...]
        r = lax.rsqrt(jnp.mean(xb * xb, axis=-1, keepdims=True) + EPS)
        h = ((xb * r) * g_ref[...]).astype(BF16)
        h1_ref[...] = h
        p = _dot(h, w_ref[...])
        for hd in range(HEADS):
            sl = slice(hd * HEAD_DIM, (hd + 1) * HEAD_DIM)
            c2, s2 = cos_ref[...], sin_ref[...]
            proj_ref[:, sl] = _rot(p[:, sl], c2, s2)
            ks = slice(RET_W + hd * HEAD_DIM, RET_W + (hd + 1) * HEAD_DIM)
            proj_ref[:, ks] = _rot(p[:, ks], c2, s2) * K_SCALE
        proj_ref[:, 2 * RET_W:] = p[:, 2 * RET_W:]

        @pl.when(pl.program_id(0) == N_TB - 1)
        def _():
            ag.finish()

    tok = lambda w: pl.BlockSpec((TM, w), lambda i: (i, 0))
    hbm = pl.BlockSpec(memory_space=pl.ANY)
    return pl.pallas_call(
        body, name="fwd_proj", grid=(N_TB,),
        out_shape=(jax.ShapeDtypeStruct((SEQ, PROJ_W), F32), jax.ShapeDtypeStruct((SEQ, D_MODEL), BF16)) + _gathered_shapes(ids),
        in_specs=[tok(D_MODEL), _resident((1, D_MODEL)), _resident((D_MODEL, PROJ_W)), tok(HEAD_DIM), tok(HEAD_DIM), hbm, hbm],
        out_specs=(tok(PROJ_W), tok(D_MODEL), hbm, hbm),
        scratch_shapes=_gather_scratch(len(ids)),
        compiler_params=_cparams(("arbitrary",)),
    )(x, g1, win_g, cos2, sin2, so, sd)


def _causal(w):
    r = lax.broadcasted_iota(jnp.int32, (CHUNK, CHUNK), 0)
    c = lax.broadcasted_iota(jnp.int32, (CHUNK, CHUNK), 1)
    return jnp.where(r >= c, w, 0.0)


def _fwd_mix(x, proj, wout_g, grn, lng, lnb, ws, bsb, mask, qdec, kdec, su):
    cdec = _chunk_decay()
    ids = [W_UP]

    def body(x_ref, p_ref, w_ref, grn_ref, lng_ref, lnb_ref, ws_ref, bsb_ref, m_ref, qd_ref, kd_ref, su_ref,
             x2_ref, cat_ref, o_ref, sp_ref, gup, state, send_sems, recv_sems, local_sems):
        ag = _Gather(ids, [su_ref], [gup], send_sems, recv_sems, local_sems)

        @pl.when(pl.program_id(0) == 0)
        def _():
            state[...] = jnp.zeros_like(state)
            ag.start()

        for h in range(HEADS):
            sl = slice(h * HEAD_DIM, (h + 1) * HEAD_DIM)
            q = p_ref[:, sl]
            k = p_ref[:, RET_W + h * HEAD_DIM:RET_W + (h + 1) * HEAD_DIM]
            v = p_ref[:, 2 * RET_W + h * HEAD_DIM:2 * RET_W + (h + 1) * HEAD_DIM]
            g = p_ref[:, 3 * RET_W + h * HEAD_DIM:3 * RET_W + (h + 1) * HEAD_DIM]
            qb, kb, vb = q.astype(BF16), k.astype(BF16), v.astype(BF16)
            a = _dot_nt(qb, kb) * m_ref[h]
            spb = state[h].astype(BF16)
            sp_ref[0, h] = spb
            o = _dot(a.astype(BF16), vb) + _dot((q * qd_ref[h]).astype(BF16), spb)
            state[h] = state[h] * cdec[h] + _dot_tn((k * kd_ref[h]).astype(BF16), vb)
            o_ref[:, sl] = o
            rinv = lax.rsqrt(jnp.mean(o * o, axis=-1, keepdims=True) + EPS)
            rn = (o * rinv) * grn_ref[:, sl]
            cat_ref[:, sl] = ((g * _sigmoid(g)) * rn).astype(BF16)
        for gi in range(HEADS):
            sl = slice(gi * HEAD_DIM, (gi + 1) * HEAD_DIM)
            u = p_ref[:, 4 * RET_W + gi * HEAD_DIM:4 * RET_W + (gi + 1) * HEAD_DIM]
            sv = p_ref[:, 4 * RET_W + SGU_W + gi * HEAD_DIM:4 * RET_W + SGU_W + (gi + 1) * HEAD_DIM]
            gv = _gelu(sv)
            xc = gv - jnp.mean(gv, axis=-1, keepdims=True)
            vn = (xc * lax.rsqrt(jnp.mean(xc * xc, axis=-1, keepdims=True) + EPS)) * lng_ref[:, sl] + lnb_ref[:, sl]
            mixed = _dot(_causal(ws_ref[gi]).astype(BF16), vn.astype(BF16)) + bsb_ref[gi]
            cat_ref[:, RET_W + gi * HEAD_DIM:RET_W + (gi + 1) * HEAD_DIM] = (_gelu(u) * mixed).astype(BF16)
        x2_ref[...] = x_ref[...] + _dot(cat_ref[...], w_ref[...])

        @pl.when(pl.program_id(0) == N_CHUNK - 1)
        def _():
            ag.finish()

    ch = lambda w: pl.BlockSpec((CHUNK, w), lambda i: (i, 0))
    hcc = (HEADS, CHUNK, CHUNK)
    hbm = pl.BlockSpec(memory_space=pl.ANY)
    return pl.pallas_call(
        body, name="fwd_mix", grid=(N_CHUNK,),
        out_shape=(jax.ShapeDtypeStruct((SEQ, D_MODEL), F32), jax.ShapeDtypeStruct((SEQ, D_MODEL), BF16),
                   jax.ShapeDtypeStruct((SEQ, RET_W), F32), jax.ShapeDtypeStruct((N_CHUNK, HEADS, HEAD_DIM, HEAD_DIM), BF16))
        + _gathered_shapes(ids),
        in_specs=[ch(D_MODEL), ch(PROJ_W), _resident((D_MODEL, D_MODEL)), _resident((1, RET_W)), _resident((1, SGU_W)),
                  _resident((1, SGU_W)), _resident(hcc), _resident(hcc), _resident(hcc), _resident(hcc), _resident(hcc), hbm],
        out_specs=(ch(D_MODEL), ch(D_MODEL), ch(RET_W), pl.BlockSpec((1, HEADS, HEAD_DIM, HEAD_DIM), lambda i: (i, 0, 0, 0)), hbm),
        scratch_shapes=[pltpu.VMEM((HEADS, HEAD_DIM, HEAD_DIM), F32)] + _gather_scratch(len(ids)),
        compiler_params=_cparams(("arbitrary",)),
    )(x, proj, wout_g, grn, lng, lnb, ws, bsb, mask, qdec, kdec, su)


def _conv_taps(p, prev8):
    row = lax.broadcasted_iota(jnp.int32, p.shape, 0)
    p1 = jnp.where(row == 0, prev8[7:8, :], pltpu.roll(p, 1, 0))
    p2 = jnp.where(row == 0, prev8[6:7, :], jnp.where(row == 1, prev8[7:8, :], pltpu.roll(p, 2, 0)))
    return p1, p2


def _fwd_ffn(x2, g2, wup_g, cw_g, cb_g, wdn_g, gf, tgt):
    def body(x_ref, g_ref, wu_ref, cw_ref, cb_ref, wd_ref, gf_ref, t_ref, h2_ref, up_ref, u_ref, act_ref, x3_ref, loss_ref, carry):
        @pl.when(pl.program_id(0) == 0)
        def _():
            carry[...] = jnp.zeros_like(carry)

        xb = x_ref[...]
        r = lax.rsqrt(jnp.mean(xb * xb, axis=-1, keepdims=True) + EPS)
        h = ((xb * r) * g_ref[...]).astype(BF16)
        h2_ref[...] = h
        acc = xb
        for t in range(N_FF_TILE):
            u = []
            for c0 in (t * FF_TILE, D_FF + t * FF_TILE):
                cs = slice(c0, c0 + FF_TILE)
                p = _dot_nt(h, wu_ref[pl.ds(c0, FF_TILE), :])
                up_ref[:, cs] = p.astype(BF16)
                p1, p2 = _conv_taps(p, carry[:, cs])
                carry[:, cs] = p[TM - 8:, :]
                us = p2 * cw_ref[0:1, cs] + p1 * cw_ref[1:2, cs] + p * cw_ref[2:3, cs] + cb_ref[:, cs]
                u_ref[:, cs] = us.astype(BF16)
                u.append(us)
            a = ((u[0] * _sigmoid(u[0])) * u[1]).astype(BF16)
            act_ref[:, t * FF_TILE:(t + 1) * FF_TILE] = a
            acc = acc + _dot(a, wd_ref[pl.ds(t * FF_TILE, FF_TILE), :])
        x3_ref[...] = acc
        r3 = lax.rsqrt(jnp.mean(acc * acc, axis=-1, keepdims=True) + EPS)
        diff = (acc * r3) * gf_ref[...] - t_ref[...]
        loss_ref[...] = jnp.full(loss_ref.shape, 0.5 * jnp.sum(jnp.mean(diff * diff, axis=-1)), F32)

    tok = lambda w: pl.BlockSpec((TM, w), lambda i: (i, 0))
    return pl.pallas_call(
        body, name="fwd_ffn", grid=(N_TB,),
        out_shape=(jax.ShapeDtypeStruct((SEQ, D_MODEL), BF16), jax.ShapeDtypeStruct((SEQ, 2 * D_FF), BF16),
                   jax.ShapeDtypeStruct((SEQ, 2 * D_FF), BF16),
                   jax.ShapeDtypeStruct((SEQ, D_FF), BF16), jax.ShapeDtypeStruct((SEQ, D_MODEL), F32),
                   jax.ShapeDtypeStruct((N_TB, 8, 128), F32)),
        in_specs=[tok(D_MODEL), _resident((1, D_MODEL)), _resident((2 * D_FF, D_MODEL)), _resident((8, 2 * D_FF)),
                  _resident((1, 2 * D_FF)), _resident((D_FF, D_MODEL)), _resident((1, D_MODEL)), tok(D_MODEL)],
        out_specs=(tok(D_MODEL), tok(2 * D_FF), tok(2 * D_FF), tok(D_FF), tok(D_MODEL),
                   pl.BlockSpec((1, 8, 128), lambda i: (i, 0, 0))),
        scratch_shapes=[pltpu.VMEM((8, 2 * D_FF), F32)],
        compiler_params=_cparams(("arbitrary",)),
    )(x2, g2, wup_g, cw_g, cb_g, wdn_g, gf, tgt)


def _bwd_ffn(x3, tgt, gf, x2, g2, up_pre, u_conv, wup_g, cw_g, wdn_g):
    def body(x3_ref, t_ref, gf_ref, x2_ref, g2_ref, up_ref, u_ref, wu_ref, cw_ref, wd_ref,
             dx3_ref, dpre_ref, dx2_ref, dgf_ref, dg2_ref, dcv_ref, nxt):
        i = pl.program_id(0)

        @pl.when(i == 0)
        def _():
            nxt[...] = jnp.zeros_like(nxt)
            dgf_ref[...] = jnp.zeros_like(dgf_ref)
            dg2_ref[...] = jnp.zeros_like(dg2_ref)
            dcv_ref[...] = jnp.zeros_like(dcv_ref)

        x3 = x3_ref[...]
        r3 = lax.rsqrt(jnp.mean(x3 * x3, axis=-1, keepdims=True) + EPS)
        xh3 = x3 * r3
        dy = (xh3 * gf_ref[...] - t_ref[...]) * (1.0 / D_MODEL)
        dgf_ref[0:1, :] += jnp.sum(dy * xh3, axis=0, keepdims=True)
        t3 = dy * gf_ref[...]
        dx3 = r3 * (t3 - xh3 * jnp.mean(t3 * xh3, axis=-1, keepdims=True))
        dx3b = dx3.astype(BF16)
        dx3_ref[...] = dx3b
        dh2 = jnp.zeros((TM, D_MODEL), F32)
        row = lax.broadcasted_iota(jnp.int32, (TM, FF_TILE), 0)
        for t in range(N_FF_TILE):
            ts = slice(t * FF_TILE, (t + 1) * FF_TILE)
            dact = _dot_nt(dx3b, wd_ref[pl.ds(t * FF_TILE, FF_TILE), :])
            ua = u_ref[:, ts].astype(F32)
            ub = u_ref[:, D_FF + t * FF_TILE:D_FF + (t + 1) * FF_TILE].astype(F32)
            sg = _sigmoid(ua)
            du = [dact * ub * (sg * (1.0 + ua * (1.0 - sg))), dact * (ua * sg)]
            for n in range(2):
                d = du[n]
                c0 = n * D_FF + t * FF_TILE
                cs = slice(c0, c0 + FF_TILE)
                nx = nxt[:, cs]
                n1 = jnp.where(row == TM - 1, nx[0:1, :], pltpu.roll(d, TM - 1, 0))
                n2 = jnp.where(row == TM - 2, nx[0:1, :], jnp.where(row == TM - 1, nx[1:2, :], pltpu.roll(d, TM - 2, 0)))
                nxt[:, cs] = d[0:8, :]
                dp = (d * cw_ref[2:3, cs] + n1 * cw_ref[1:2, cs] + n2 * cw_ref[0:1, cs]).astype(BF16)
                dpre_ref[:, cs] = dp
                p = up_ref[:, cs].astype(F32)
                dcv_ref[n, 0:1, ts] += jnp.sum(n2 * p, axis=0, keepdims=True)
                dcv_ref[n, 1:2, ts] += jnp.sum(n1 * p, axis=0, keepdims=True)
                dcv_ref[n, 2:3, ts] += jnp.sum(d * p, axis=0, keepdims=True)
                dcv_ref[n, 3:4, ts] += jnp.sum(d, axis=0, keepdims=True)
                dh2 = dh2 + _dot(dp, wu_ref[pl.ds(c0, FF_TILE), :])
        x2 = x2_ref[...]
        r2 = lax.rsqrt(jnp.mean(x2 * x2, axis=-1, keepdims=True) + EPS)
        xh2 = x2 * r2
        dg2_ref[0:1, :] += jnp.sum(dh2 * xh2, axis=0, keepdims=True)
        t2 = dh2 * g2_ref[...]
        dx2_ref[...] = dx3 + r2 * (t2 - xh2 * jnp.mean(t2 * xh2, axis=-1, keepdims=True))

    rev = lambda w: pl.BlockSpec((TM, w), lambda i: (N_TB - 1 - i, 0))
    acc = lambda s: pl.BlockSpec(s, lambda i: (0,) * len(s))
    return pl.pallas_call(
        body, name="bwd_ffn", grid=(N_TB,),
        out_shape=(jax.ShapeDtypeStruct((SEQ, D_MODEL), BF16), jax.ShapeDtypeStruct((SEQ, 2 * D_FF), BF16),
                   jax.ShapeDtypeStruct((SEQ, D_MODEL), F32), jax.ShapeDtypeStruct((8, D_MODEL), F32),
                   jax.ShapeDtypeStruct((8, D_MODEL), F32), jax.ShapeDtypeStruct((2, 8, D_FF), F32)),
        in_specs=[rev(D_MODEL), rev(D_MODEL), _resident((1, D_MODEL)), rev(D_MODEL), _resident((1, D_MODEL)), rev(2 * D_FF),
                  rev(2 * D_FF), _resident((2 * D_FF, D_MODEL)), _resident((8, 2 * D_FF)), _resident((D_FF, D_MODEL))],
        out_specs=(rev(D_MODEL), rev(2 * D_FF), rev(D_MODEL), acc((8, D_MODEL)), acc((8, D_MODEL)), acc((2, 8, D_FF))),
        scratch_shapes=[pltpu.VMEM((8, 2 * D_FF), F32)],
        compiler_params=_cparams(("arbitrary",)),
    )(x3, tgt, gf, x2, g2, up_pre, u_conv, wup_g, cw_g, wdn_g)


def _bwd_mix(dx2, proj, o, sprev, wout_g, grn, lng, lnb, ws, bsb, mask, qdec, kdec, cos2, sin2, hosted):
    cdec = _chunk_decay()
    geoms = [g for g, _ in hosted]
    n_h = len(hosted)

    def body(dx2_ref, p_ref, o_ref, sp_ref, w_ref, grn_ref, lng_ref, lnb_ref, ws_ref, bsb_ref, m_ref, qd_ref, kd_ref,
             cos_ref, sin_ref, *rest):
        dp_ref, dgrn_ref, dlng_ref, dlnb_ref, dws_ref, dbs_ref = rest[n_h:n_h + 6]
        dstate, dbs_acc = rest[2 * n_h + 6:2 * n_h + 8]
        i = pl.program_id(0)
        rs = _Scatters(geoms, rest[:n_h], rest[n_h + 6:2 * n_h + 6], rest[2 * n_h + 8:])
        pl.when(i == 0)(rs.phase1)
        pl.when(i == 3)(rs.phase2)

        @pl.when(i == 0)
        def _():
            dstate[...] = jnp.zeros_like(dstate)
            dgrn_ref[...] = jnp.zeros_like(dgrn_ref)
            dlng_ref[...] = jnp.zeros_like(dlng_ref)
            dlnb_ref[...] = jnp.zeros_like(dlnb_ref)
            dws_ref[...] = jnp.zeros_like(dws_ref)
            dbs_ref[...] = jnp.zeros_like(dbs_ref)
            dbs_acc[...] = jnp.zeros_like(dbs_acc)

        dmix = _dot_nt(dx2_ref[...].astype(BF16), w_ref[...])
        for h in range(HEADS):
            sl = slice(h * HEAD_DIM, (h + 1) * HEAD_DIM)
            q = p_ref[:, sl]
            k = p_ref[:, RET_W + h * HEAD_DIM:RET_W + (h + 1) * HEAD_DIM]
            v = p_ref[:, 2 * RET_W + h * HEAD_DIM:2 * RET_W + (h + 1) * HEAD_DIM]
            g = p_ref[:, 3 * RET_W + h * HEAD_DIM:3 * RET_W + (h + 1) * HEAD_DIM]
            o = o_ref[:, sl]
            rinv = lax.rsqrt(jnp.mean(o * o, axis=-1, keepdims=True) + EPS)
            oh = o * rinv
            gr = grn_ref[:, sl]
            sg = _sigmoid(g)
            dret = dmix[:, sl]
            dp_ref[:, 3 * RET_W + h * HEAD_DIM:3 * RET_W + (h + 1) * HEAD_DIM] = (
                dret * (oh * gr) * (sg * (1.0 + g * (1.0 - sg)))).astype(BF16)
            drn = dret * (g * sg)
            dgrn_ref[0:1, sl] += jnp.sum(drn * oh, axis=0, keepdims=True)
            t = drn * gr
            do = rinv * (t - oh * jnp.mean(t * oh, axis=-1, keepdims=True))
            qb, kb, vb, dob = q.astype(BF16), k.astype(BF16), v.astype(BF16), do.astype(BF16)
            m = m_ref[h]
            ab = (_dot_nt(qb, kb) * m).astype(BF16)
            dab = (_dot_nt(dob, vb) * m).astype(BF16)
            spb = sp_ref[0, h]
            dsn = dstate[h]
            dsnb = dsn.astype(BF16)
            qdb = (q * qd_ref[h]).astype(BF16)
            kdb = (k * kd_ref[h]).astype(BF16)
            dq = _dot(dab, kb) + _dot_nt(dob, spb) * qd_ref[h]
            dk = _dot_tn(dab, qb) + _dot_nt(vb, dsnb) * kd_ref[h]
            dv = _dot_tn(ab, dob) + _dot(kdb, dsnb)
            dstate[h] = dsn * cdec[h] + _dot_tn(qdb, dob)
            c2, s2 = cos_ref[...], sin_ref[...]
            dp_ref[:, sl] = _rot_t(dq, c2, s2).astype(BF16)
            dp_ref[:, RET_W + h * HEAD_DIM:RET_W + (h + 1) * HEAD_DIM] = _rot_t(dk * K_SCALE, c2, s2).astype(BF16)
            dp_ref[:, 2 * RET_W + h * HEAD_DIM:2 * RET_W + (h + 1) * HEAD_DIM] = dv.astype(BF16)
        for gi in range(HEADS):
            sl = slice(gi * HEAD_DIM, (gi + 1) * HEAD_DIM)
            u = p_ref[:, 4 * RET_W + gi * HEAD_DIM:4 * RET_W + (gi + 1) * HEAD_DIM]
            sv = p_ref[:, 4 * RET_W + SGU_W + gi * HEAD_DIM:4 * RET_W + SGU_W + (gi + 1) * HEAD_DIM]
            gv = _gelu(sv)
            xc = gv - jnp.mean(gv, axis=-1, keepdims=True)
            rstd = lax.rsqrt(jnp.mean(xc * xc, axis=-1, keepdims=True) + EPS)
            xh = xc * rstd
            lg = lng_ref[:, sl]
            vnb = (xh * lg + lnb_ref[:, sl]).astype(BF16)
            wcb = _causal(ws_ref[gi]).astype(BF16)
            mixed = _dot(wcb, vnb) + bsb_ref[gi]
            dsgu = dmix[:, RET_W + gi * HEAD_DIM:RET_W + (gi + 1) * HEAD_DIM]
            dmixed = dsgu * _gelu(u)
            dmb = dmixed.astype(BF16)
            dws_ref[gi] += _causal(_dot_nt(dmb, vnb))
            dbs_acc[gi] += dmixed
            dvn = _dot_tn(wcb, dmb)
            dlng_ref[gi:gi + 1, :] += jnp.sum(dvn * xh, axis=0, keepdims=True)
            dlnb_ref[gi:gi + 1, :] += jnp.sum(dvn, axis=0, keepdims=True)
            dxh = dvn * lg
            dgv = rstd * (dxh - jnp.mean(dxh, axis=-1, keepdims=True) - xh * jnp.mean(dxh * xh, axis=-1, keepdims=True))
            dp_ref[:, 4 * RET_W + gi * HEAD_DIM:4 * RET_W + (gi + 1) * HEAD_DIM] = (dsgu * mixed * _gelu_grad(u)).astype(BF16)
            dp_ref[:, 4 * RET_W + SGU_W + gi * HEAD_DIM:4 * RET_W + SGU_W + (gi + 1) * HEAD_DIM] = (
                dgv * _gelu_grad(sv)).astype(BF16)

        @pl.when(i == N_CHUNK - 1)
        def _():
            for gi in range(HEADS):
                col = jnp.broadcast_to(jnp.sum(dbs_acc[gi], axis=-1, keepdims=True), (CHUNK, CHUNK))
                dbs_ref[gi:gi + 1, :] = jnp.transpose(col)[0:1, :]
            rs.phase3()

    rev = lambda w: pl.BlockSpec((CHUNK, w), lambda i: (N_CHUNK - 1 - i, 0))
    hcc = (HEADS, CHUNK, CHUNK)
    acc = lambda s: pl.BlockSpec(s, lambda i: (0,) * len(s))
    res = pl.pallas_call(
        body, name="bwd_mix", grid=(N_CHUNK,),
        out_shape=(jax.ShapeDtypeStruct((SEQ, PROJ_W), BF16), jax.ShapeDtypeStruct((8, RET_W), F32),
                   jax.ShapeDtypeStruct((8, HEAD_DIM), F32), jax.ShapeDtypeStruct((8, HEAD_DIM), F32),
                   jax.ShapeDtypeStruct(hcc, F32), jax.ShapeDtypeStruct((8, CHUNK), F32)) + _scatter_out_shapes(geoms),
        in_specs=[rev(D_MODEL), rev(PROJ_W), rev(RET_W),
                  pl.BlockSpec((1, HEADS, HEAD_DIM, HEAD_DIM), lambda i: (N_CHUNK - 1 - i, 0, 0, 0)),
                  _resident((D_MODEL, D_MODEL)), _resident((1, RET_W)), _resident((1, SGU_W)), _resident((1, SGU_W)),
                  _resident(hcc), _resident(hcc), _resident(hcc), _resident(hcc), _resident(hcc), rev(HEAD_DIM), rev(HEAD_DIM)]
        + [pl.BlockSpec(memory_space=pl.ANY)] * n_h,
        out_specs=(rev(PROJ_W), acc((8, RET_W)), acc((8, HEAD_DIM)), acc((8, HEAD_DIM)), acc(hcc), acc((8, CHUNK)))
        + _scatter_out_specs(geoms),
        scratch_shapes=[pltpu.VMEM((HEADS, HEAD_DIM, HEAD_DIM), F32), pltpu.VMEM((HEADS, CHUNK, CHUNK), F32)] + _scatter_scratch(geoms),
        compiler_params=_cparams(("arbitrary",)),
    )(dx2, proj, o, sprev, wout_g, grn, lng, lnb, ws, bsb, mask, qdec, kdec, cos2, sin2, *[p for _, p in hosted])
    return tuple(res[:6 + n_h])


def _bwd_proj(dproj, win_g, x, g1, dx2, gin_p):
    geoms = [W_IN]

    def body(dp_ref, w_ref, x_ref, g_ref, dx2_ref, gin_ref, dx_ref, dg_ref, rs_out, *rs_scratch):
        rs = _Scatters(geoms, [gin_ref], [rs_out], rs_scratch)
        pl.when(pl.program_id(0) == 0)(rs.phase1)
        pl.when(pl.program_id(0) == 2)(rs.phase2)

        @pl.when(pl.program_id(0) == 0)
        def _():
            dg_ref[...] = jnp.zeros_like(dg_ref)

        dh = _dot_nt(dp_ref[...], w_ref[...])
        xb = x_ref[...]
        r = lax.rsqrt(jnp.mean(xb * xb, axis=-1, keepdims=True) + EPS)
        xh = xb * r
        dg_ref[0:1, :] += jnp.sum(dh * xh, axis=0, keepdims=True)
        t = dh * g_ref[...]
        dx_ref[...] = dx2_ref[...] + r * (t - xh * jnp.mean(t * xh, axis=-1, keepdims=True))
        pl.when(pl.program_id(0) == N_TB - 1)(rs.phase3)

    tok = lambda w: pl.BlockSpec((TM, w), lambda i: (i, 0))
    res = pl.pallas_call(
        body, name="bwd_proj", grid=(N_TB,),
        out_shape=(jax.ShapeDtypeStruct((SEQ, D_MODEL), F32), jax.ShapeDtypeStruct((8, D_MODEL), F32)) + _scatter_out_shapes(geoms),
        in_specs=[tok(PROJ_W), _resident((D_MODEL, PROJ_W)), tok(D_MODEL), _resident((1, D_MODEL)), tok(D_MODEL),
                  pl.BlockSpec(memory_space=pl.ANY)],
        out_specs=(tok(D_MODEL), pl.BlockSpec((8, D_MODEL), lambda i: (0, 0))) + _scatter_out_specs(geoms),
        scratch_shapes=_scatter_scratch(geoms),
        compiler_params=_cparams(("arbitrary",)),
    )(dproj, win_g, x, g1, dx2, gin_p)
    return res[:3]


def _wgrad(name, a, b, tm=None, tn=None, hosted=()):
    m_w, n_w = a.shape[-1], b.shape[-1]
    tm = m_w if tm is None else tm
    tn = n_w if tn is None else tn
    n_steps = (m_w // tm) * (n_w // tn)
    geoms = [g for g, _ in hosted]
    n_h = len(hosted)

    def body(a_ref, b_ref, *rest):
        o_ref = rest[n_h]
        if n_h:
            rs = _Scatters(geoms, rest[:n_h], rest[n_h + 1:2 * n_h + 1], rest[2 * n_h + 1:])
            step = pl.program_id(0) * (n_w // tn) + pl.program_id(1)
            pl.when(step == 0)(rs.phase1)
            pl.when(step == 1)(rs.phase2)
        o_ref[...] = _dot_tn(a_ref[...].astype(BF16), b_ref[...].astype(BF16)).astype(BF16)
        if n_h:
            pl.when(step == n_steps - 1)(rs.phase3)

    assert not n_h or n_steps >= 3
    res = pl.pallas_call(
        body, name=name, grid=(m_w // tm, n_w // tn),
        out_shape=(jax.ShapeDtypeStruct((m_w, n_w), BF16),) + _scatter_out_shapes(geoms),
        in_specs=[pl.BlockSpec((SEQ, tm), lambda i, j: (0, i)), pl.BlockSpec((SEQ, tn), lambda i, j: (0, j))]
        + [pl.BlockSpec(memory_space=pl.ANY)] * n_h,
        out_specs=(pl.BlockSpec((tm, tn), lambda i, j: (i, j)),) + _scatter_out_specs(geoms),
        scratch_shapes=_scatter_scratch(geoms),
        compiler_params=_cparams(("arbitrary", "arbitrary") if n_h else ("parallel", "parallel")),
    )(a, b, *[p for _, p in hosted])
    return tuple(res[:1 + n_h])


RS_ROWS = {W_IN: 128, W_OUT: 128, W_UP: 176, W_DOWN: 176}


class _Scatter:
    def __init__(self, geom, partial, out, land1, mine, stage2, land2, s1_send, s1_recv, s2_send, s2_recv, ld_sems):
        self.w, self.row0, self.shape = _geom(geom)
        self.partial, self.out, self.land1 = partial, out, land1
        self.mine, self.stage2, self.land2 = mine, stage2, land2
        self.s1_send, self.s1_recv, self.s2_send, self.s2_recv, self.ld_sems = s1_send, s1_recv, s2_send, s2_recv, ld_sems
        self.x, self.y, self.c = lax.axis_index("x"), lax.axis_index("y"), lax.axis_index("c")
        self.sibling = (self.x, self.y, 1 - self.c)
        self.chips = [(self.x, self.y), (1 - self.x, self.y), (self.x, 1 - self.y), (1 - self.x, 1 - self.y)]

    def block(self, px, py, pc):
        dev = 4 * px + 2 * py + pc
        if self.w == W_IN:
            return self.partial.at[:, pl.ds(pl.multiple_of(dev * IN_SHARD, 128), IN_SHARD)]
        if self.w == W_OUT:
            return self.partial.at[pl.ds(pl.multiple_of(dev * OUT_SHARD, 128), OUT_SHARD), :]
        if self.w == W_DOWN:
            return self.partial.at[pl.ds(pl.multiple_of(dev * DOWN_SHARD, 32), DOWN_SHARD), :]
        return self.partial.at[pl.ds(pl.multiple_of(dev * FF_SHARD + self.row0, 32), self.shape[0]), :]

    def copy1(self, k):
        return pltpu.make_async_remote_copy(
            src_ref=self.block(*self.chips[k], 1 - self.c), dst_ref=self.land1.at[k],
            send_sem=self.s1_send.at[k], recv_sem=self.s1_recv.at[k], device_id=self.sibling, device_id_type=MESH)

    def copy2(self, k):
        return pltpu.make_async_remote_copy(
            src_ref=self.stage2.at[k - 1], dst_ref=self.land2.at[k - 1],
            send_sem=self.s2_send.at[k - 1], recv_sem=self.s2_recv.at[k - 1], device_id=(*self.chips[k], self.c), device_id_type=MESH)

    def _rows(self):
        step = RS_ROWS[self.w]
        return [pl.ds(r0, step) for r0 in range(0, self.shape[0], step)]

    def load(self, k):
        return pltpu.make_async_copy(self.block(*self.chips[k], self.c), self.mine.at[k], self.ld_sems.at[k])

    def phase1(self):
        for k in range(4):
            self.copy1(k).start()
        for k in range(4):
            self.load(k).start()

    def phase2(self):
        for k in (3, 1, 2, 0):
            self.copy1(k).wait_recv()
            self.load(k).wait()
            for rs in self._rows():
                s = self.mine[k, rs, :].astype(F32) + self.land1[k, rs, :].astype(F32)
                if k == 0:
                    self.out[rs, :] = s
                else:
                    self.stage2[k - 1, rs, :] = s.astype(BF16)
            if k:
                self.copy2(k).start()

    def phase3(self):
        for k in (1, 2, 3):
            self.copy2(k).wait_recv()
        for rs in self._rows():
            self.out[rs, :] = ((self.out[rs, :] + self.land2[0, rs, :].astype(F32)) + self.land2[1, rs, :].astype(F32)) \
                + self.land2[2, rs, :].astype(F32)
        for k in range(4):
            self.copy1(k).wait_send()
        for k in (1, 2, 3):
            self.copy2(k).wait_send()


def _geom(geom):
    if isinstance(geom, tuple):
        w, row0, rows = geom
        assert w == W_UP
        return w, row0, (rows, SHARD[w][1])
    return geom, 0, SHARD[geom]


N_SCATTER_SCRATCH = 9


def _scatter_out_shapes(geoms):
    return tuple(jax.ShapeDtypeStruct(_geom(g)[2], F32) for g in geoms)


def _scatter_out_specs(geoms):
    return (pl.BlockSpec(memory_space=pltpu.VMEM),) * len(geoms)


def _scatter_scratch(geoms):
    out = []
    for g in geoms:
        s = _geom(g)[2]
        out += [pltpu.VMEM((4,) + s, BF16), pltpu.VMEM((4,) + s, BF16), pltpu.VMEM((3,) + s, BF16), pltpu.VMEM((3,) + s, BF16),
                pltpu.SemaphoreType.DMA((4,)), pltpu.SemaphoreType.DMA((4,)), pltpu.SemaphoreType.DMA((3,)),
                pltpu.SemaphoreType.DMA((3,)), pltpu.SemaphoreType.DMA((4,))]
    return out


class _Scatters:
    def __init__(self, geoms, p_refs, out_refs, scratch):
        k = N_SCATTER_SCRATCH
        self.items = [_Scatter(g, p_refs[i], out_refs[i], *scratch[k * i:k * i + k]) for i, g in enumerate(geoms)]

    def phase1(self):
        for s in self.items:
            s.phase1()

    def phase2(self):
        for s in self.items:
            s.phase2()

    def phase3(self):
        for s in self.items:
            s.phase3()


PACK_W = 1024


def _all_reduce_small(dg1, dg2, dgf, dgrn, dlng, dlnb, dbs, loss_parts, dws, dcv):
    n_a = 3

    def body(dg1_ref, dg2_ref, dgf_ref, dgrn_ref, dlng_ref, dlnb_ref, dbs_ref, loss_ref, dws_ref, dcv_ref,
             rp_ref, rws_ref, rcv_ref, pack, rx_p, rx_ws, rx_cv, cs_p, cs_ws, cs_cv, g_p, g_ws, g_cv,
             s1_send, s1_recv, s2_send, s2_recv, s3_send, s3_recv):
        x, y, c = lax.axis_index("x"), lax.axis_index("y"), lax.axis_index("c")
        sibling = (x, y, 1 - c)
        chips = [(1 - x, y), (x, 1 - y), (1 - x, 1 - y)]
        pack[...] = jnp.zeros_like(pack)
        pack[0, 0:1, :] = dg1_ref[0:1, :]
        pack[0, 1:2, :] = dg2_ref[0:1, :]
        pack[0, 2:3, :] = dgf_ref[0:1, :]
        pack[0, 3:4, 0:RET_W] = dgrn_ref[0:1, :]
        lsum = loss_ref[0, 0:1, :]
        for i in range(1, N_TB):
            lsum = lsum + loss_ref[i, 0:1, :]
        pack[0, 3:4, RET_W:RET_W + 128] = lsum
        pack[1, 0:HEADS, 0:128] = dlng_ref[0:HEADS, :]
        pack[1, 0:HEADS, 128:256] = dlnb_ref[0:HEADS, :]
        pack[1, 0:HEADS, 256:384] = dbs_ref[0:HEADS, :]

        srcs = [pack, dws_ref, dcv_ref]
        outs = [rp_ref, rws_ref, rcv_ref]
        rxs = [rx_p, rx_ws, rx_cv]
        css = [cs_p, cs_ws, cs_cv]
        gs = [g_p, g_ws, g_cv]
        hl = [1, HEADS // 2, 1]

        def half(ref, a, h):
            return ref.at[pl.ds(h * hl[a], hl[a])]

        ex1 = [pltpu.make_async_remote_copy(src_ref=half(srcs[a], a, 1 - c), dst_ref=rxs[a], send_sem=s1_send.at[a],
                                            recv_sem=s1_recv.at[a], device_id=sibling, device_id_type=MESH) for a in range(n_a)]
        for cp in ex1:
            cp.start()
        ex2 = []
        for a in range(n_a):
            ex1[a].wait_recv()
            css[a][...] = half(srcs[a], a, c)[...] + rxs[a][...]
            for j, chip in enumerate(chips):
                cp = pltpu.make_async_remote_copy(src_ref=css[a], dst_ref=gs[a].at[j], send_sem=s2_send.at[a, j],
                                                  recv_sem=s2_recv.at[a, j], device_id=(*chip, c), device_id_type=MESH)
                cp.start()
                ex2.append(cp)
        ex3 = []
        for a in range(n_a):
            for j in range(3):
                ex2[3 * a + j].wait_recv()
            tot = None
            for q in range(4):
                k = jnp.where(x != (q >> 1), 1, 0) + jnp.where(y != (q & 1), 2, 0)
                term = jnp.where(k == 0, css[a][...], jnp.where(k == 1, gs[a][0], jnp.where(k == 2, gs[a][1], gs[a][2])))
                tot = term if tot is None else tot + term
            half(outs[a], a, c)[...] = tot
            cp = pltpu.make_async_remote_copy(src_ref=half(outs[a], a, c), dst_ref=half(outs[a], a, c), send_sem=s3_send.at[a],
                                              recv_sem=s3_recv.at[a], device_id=sibling, device_id_type=MESH)
            cp.start()
            ex3.append(cp)
        for a in range(n_a):
            pltpu.make_async_remote_copy(src_ref=half(outs[a], a, 1 - c), dst_ref=half(outs[a], a, 1 - c), send_sem=s3_send.at[a],
                                         recv_sem=s3_recv.at[a], device_id=sibling, device_id_type=MESH).wait_recv()
        for cp in ex1 + ex2 + ex3:
            cp.wait_send()

    vm = pl.BlockSpec(memory_space=pltpu.VMEM)
    full = [(2, 8, PACK_W), (HEADS, CHUNK, CHUNK), (2, 8, D_FF)]
    halves = [(s[0] // 2,) + s[1:] for s in full]
    return pl.pallas_call(
        body, name="ar_small",
        out_shape=tuple(jax.ShapeDtypeStruct(s, F32) for s in full),
        in_specs=[vm] * 10, out_specs=(vm,) * 3,
        scratch_shapes=[pltpu.VMEM(full[0], F32)] + [pltpu.VMEM(s, F32) for s in halves] + [pltpu.VMEM(s, F32) for s in halves]
        + [pltpu.VMEM((3,) + s, F32) for s in halves]
        + [pltpu.SemaphoreType.DMA((n_a,)), pltpu.SemaphoreType.DMA((n_a,)), pltpu.SemaphoreType.DMA((n_a, 3)),
           pltpu.SemaphoreType.DMA((n_a, 3)), pltpu.SemaphoreType.DMA((n_a,)), pltpu.SemaphoreType.DMA((n_a,))],
        compiler_params=_cparams(),
    )(dg1, dg2, dgf, dgrn, dlng, dlnb, dbs, loss_parts, dws, dcv)


def _adam_math(w, g, m, v):
    nm = ADAM_B1 * m + (1.0 - ADAM_B1) * g
    nv = ADAM_B2 * v + (1.0 - ADAM_B2) * (g * g)
    d = -ADAM_LR * ((nm / (1.0 - ADAM_B1 ** ADAM_STEP)) / (jnp.sqrt(nv / (1.0 - ADAM_B2 ** ADAM_STEP)) + ADAM_EPS) + ADAM_WD * w)
    return d, nm, nv


def _adamw(name, w, gs, m, v, rows):
    _, r, cdim = w.shape
    n_steps = r // rows
    half = n_steps // len(gs)

    def body(w_ref, *rest):
        g_refs, (m_ref, v_ref, go_ref, d_ref, nm_ref, nv_ref) = rest[:len(gs)], rest[len(gs):]
        gg = g_refs[0][...]
        if len(gs) == 2:
            gg = jnp.where(pl.program_id(0) < half, gg, g_refs[1][...])
        go_ref[0] = gg
        d, nm, nv = _adam_math(w_ref[0], gg, m_ref[0], v_ref[0])
        d_ref[0], nm_ref[0], nv_ref[0] = d, nm, nv

    spec3 = pl.BlockSpec((1, rows, cdim), lambda i: (0, i, 0))
    if len(gs) == 1:
        g_specs = [pl.BlockSpec((rows, cdim), lambda i: (i, 0))]
    else:
        g_specs = [pl.BlockSpec((rows, cdim), lambda i: (jnp.minimum(i, half - 1), 0)),
                   pl.BlockSpec((rows, cdim), lambda i: (jnp.maximum(i - half, 0), 0))]
    sh = jax.ShapeDtypeStruct((1, r, cdim), F32)
    return pl.pallas_call(
        body, name=name, grid=(n_steps,), out_shape=(sh, sh, sh, sh),
        in_specs=[spec3] + g_specs + [spec3, spec3], out_specs=(spec3,) * 4,
        compiler_params=_cparams(("parallel",)),
    )(w, *gs, m, v)


def _adamw_small(rp, rws, rcv, gcw, params):
    n_p = len(params)

    def body(*refs):
        rp_ref, rws_ref, rcv_ref, gcw_ref = refs[:4]
        ins = refs[4:4 + 3 * n_p]
        outs = refs[4 + 3 * n_p:]
        grads = [rp_ref[0, 0:1, :], rp_ref[0, 1:2, :], rp_ref[0, 2:3, :], rp_ref[0, 3:4, 0:RET_W],
                 rp_ref[1, 0:HEADS, 0:128], rp_ref[1, 0:HEADS, 128:256], rp_ref[1, 0:HEADS, 256:384],
                 rws_ref[...], gcw_ref[0], None]
        for p in range(n_p):
            w_ref, m_ref, v_ref = ins[3 * p:3 * p + 3]
            o = outs[4 * p:4 * p + 4]
            if p == n_p - 1:
                for hf in range(2):
                    cs = slice(hf * D_FF, (hf + 1) * D_FF)
                    g = rcv_ref[hf, 3:4, :]
                    res = (g,) + _adam_math(w_ref[:, cs], g, m_ref[:, cs], v_ref[:, cs])
                    for t in range(4):
                        o[t][:, cs] = res[t]
                continue
            lead = w_ref.ndim > grads[p].ndim
            rd = (lambda r: r[0]) if lead else (lambda r: r[...])
            res = (grads[p],) + _adam_math(rd(w_ref), grads[p], rd(m_ref), rd(v_ref))
            for t in range(4):
                if lead:
                    o[t][0] = res[t]
                else:
                    o[t][...] = res[t]

    vm = pl.BlockSpec(memory_space=pltpu.VMEM)
    flat = [a for tr in params for a in tr]
    out_shape = tuple(jax.ShapeDtypeStruct(tr[0].shape, F32) for tr in params for _ in range(4))
    res = pl.pallas_call(
        body, name="adamw_small", out_shape=out_shape, in_specs=[vm] * (4 + len(flat)), out_specs=(vm,) * len(out_shape),
        compiler_params=_cparams(),
    )(rp, rws, rcv, gcw, *flat)
    return [res[4 * p:4 * p + 4] for p in range(n_p)]


def kernel(x, mix_norm_g, w_in, ret_norm_g, sgu_ln_g, sgu_ln_b, sgu_w_s, sgu_b_s, w_out, ffn_norm_g, w_up, conv_w, conv_b, w_down, final_norm_g, loss_target, m_mix_norm_g, m_w_in, m_ret_norm_g, m_sgu_ln_g, m_sgu_ln_b, m_sgu_w_s, m_sgu_b_s, m_w_out, m_ffn_norm_g, m_w_up, m_conv_w, m_conv_b, m_w_down, m_final_norm_g, v_mix_norm_g, v_w_in, v_ret_norm_g, v_sgu_ln_g, v_sgu_ln_b, v_sgu_w_s, v_sgu_b_s, v_w_out, v_ffn_norm_g, v_w_up, v_conv_w, v_conv_b, v_w_down, v_final_norm_g):
    xs = x[0]
    tgt = loss_target[0]
    cos2, sin2 = _rope_tables()
    mask, qdec, kdec = _decay_tables()
    grn = ret_norm_g.reshape(1, RET_W)
    lng = sgu_ln_g.reshape(1, SGU_W)
    lnb = sgu_ln_b.reshape(1, SGU_W)
    ws = sgu_w_s[0]
    bsb = jnp.broadcast_to(sgu_b_s[0][:, :, None], (HEADS, CHUNK, HEAD_DIM))
    gf = final_norm_g.reshape(1, D_MODEL)
    me = 4 * lax.axis_index("x") + 2 * lax.axis_index("y") + lax.axis_index("c")
    tr = lambda a: jnp.transpose(a[0])[None]

    win_g, cw_sh, so, su, sd = _ag_first(w_in[0], w_out[0], tr(w_up)[0], w_down[0], conv_w[0])
    cw_g = jnp.transpose(cw_sh, (1, 0, 2)).reshape(8, 2 * D_FF)

    proj, h1, wout_g, wdn_g = _fwd_proj(xs, mix_norm_g, win_g, cos2, sin2, so, sd)
    x2, mixcat, o, sprev, wup_g = _fwd_mix(xs, proj, wout_g, grn, lng, lnb, ws, bsb, mask, qdec, kdec, su)
    h2, up_pre, u_conv, act, x3, loss_parts = _fwd_ffn(x2, ffn_norm_g, wup_g, cw_g, conv_b, wdn_g, gf, tgt)

    dx3, dpre, dx2, dgf, dg2, dcv = _bwd_ffn(x3, tgt, gf, x2, ffn_norm_g, up_pre, u_conv, wup_g, cw_g, wdn_g)
    band = FF_SHARD // 2
    (gdn_p,) = _wgrad("wgrad_down", act, dx3, tm=FF_TILE)
    (gout_p,) = _wgrad("wgrad_out", mixcat, dx2, tn=512)
    gup_p, g_dn = _wgrad("wgrad_up", dpre, h2, tm=FF_TILE, hosted=[(W_DOWN, gdn_p)])
    dproj, dgrn, dlng, dlnb, dws, dbs, g_up_a, g_out = _bwd_mix(
        dx2, proj, o, sprev, wout_g, grn, lng, lnb, ws, bsb, mask, qdec, kdec, cos2, sin2,
        [((W_UP, 0, band), gup_p), (W_OUT, gout_p)])
    gin_p, g_up_b = _wgrad("wgrad_in", h1, dproj, tn=768, hosted=[((W_UP, band, band), gup_p)])
    grad_x, dg1, g_in = _bwd_proj(dproj, win_g, xs, mix_norm_g, dx2, gin_p)
    rp, rws, rcv = _all_reduce_small(dg1, dg2, dgf, dgrn, dlng, dlnb, dbs, loss_parts, dws, dcv)
    loss = rp[0, 3, RET_W]
    gcw = lax.dynamic_slice(rcv, (me // (N_DEV // 2), 0, (me % (N_DEV // 2)) * FF_SHARD), (1, 3, FF_SHARD))

    table = {}
    for name, w, gs, m, v, rows in (("w_in", w_in, [g_in], m_w_in, v_w_in, 256), ("w_out", w_out, [g_out], m_w_out, v_w_out, 128),
                                    ("w_up", tr(w_up), [g_up_a, g_up_b], tr(m_w_up), tr(v_w_up), 176),
                                    ("w_down", w_down, [g_dn], m_w_down, v_w_down, 88)):
        table[name] = _adamw("adamw_" + name, w, gs, m, v, rows)
    table["w_up"] = tuple(tr(a) for a in table["w_up"])
    row = lambda a: a.reshape(1, D_MODEL)
    names_small = ["mix_norm_g", "ffn_norm_g", "final_norm_g", "ret_norm_g", "sgu_ln_g", "sgu_ln_b", "sgu_b_s", "sgu_w_s",
                   "conv_w", "conv_b"]
    params = [(mix_norm_g, m_mix_norm_g, v_mix_norm_g), (ffn_norm_g, m_ffn_norm_g, v_ffn_norm_g),
              (row(final_norm_g), row(m_final_norm_g), row(v_final_norm_g)), (ret_norm_g, m_ret_norm_g, v_ret_norm_g),
              (sgu_ln_g, m_sgu_ln_g, v_sgu_ln_g), (sgu_ln_b, m_sgu_ln_b, v_sgu_ln_b), (sgu_b_s, m_sgu_b_s, v_sgu_b_s),
              (sgu_w_s, m_sgu_w_s, v_sgu_w_s), (conv_w, m_conv_w, v_conv_w), (conv_b, m_conv_b, v_conv_b)]
    for n, res in zip(names_small, _adamw_small(rp, rws, rcv, gcw, params)):
        table[n] = res
    table["final_norm_g"] = tuple(a.reshape(D_MODEL) for a in table["final_norm_g"])

    order = ["mix_norm_g", "w_in", "ret_norm_g", "sgu_ln_g", "sgu_ln_b", "sgu_w_s", "sgu_b_s", "w_out", "ffn_norm_g", "w_up",
             "conv_w", "conv_b", "w_down", "final_norm_g"]
    outs = [loss, grad_x[None]]
    for col in range(4):
        outs += [table[n][col] for n in order]
    return tuple(outs)
```

```python
import functools
import math

import jax
import jax.numpy as jnp
import numpy as np
from jax import lax
from jax.experimental import pallas as pl
from jax.experimental.pallas import tpu as pltpu

F32 = jnp.float32
BF16 = jnp.bfloat16
MESH = pl.DeviceIdType.MESH

N_DEV = 8
SEQ = 2048
D_MODEL = 1024
CHUNK = 128
N_CHUNK = SEQ // CHUNK
HEADS = 4
HEAD_DIM = 128
RET_W = 512
SGU_W = 512
PROJ_W = 3072
D_FF = 2816
FF_SHARD = 704
FF_TILE = 1408
N_FF_TILE = D_FF // FF_TILE
IN_SHARD = PROJ_W // N_DEV
OUT_SHARD = D_MODEL // N_DEV
DOWN_SHARD = D_FF // N_DEV
TM = 256
N_TB = SEQ // TM
EPS = 1e-6
ROPE_BASE = 10000.0
K_SCALE = HEAD_DIM ** -0.5
INV_SQRT2 = 0.7071067811865476
INV_SQRT_2PI = 0.3989422804014327

ADAM_LR = 0.001
ADAM_B1 = 0.9
ADAM_B2 = 0.999
ADAM_EPS = 1e-08
ADAM_WD = 0.01
ADAM_STEP = 10

VMEM_LIMIT = 56 * 1024 * 1024


def _cparams(sem=None, vmem=VMEM_LIMIT):
    return pltpu.CompilerParams(dimension_semantics=sem, vmem_limit_bytes=vmem)


def _resident(shape):
    nd = len(shape)
    return pl.BlockSpec(shape, lambda *_: (0,) * nd, pipeline_mode=pl.Buffered(1))


def _dot(a, b):
    return jnp.dot(a, b, preferred_element_type=F32)


def _dot_nt(a, b):
    return lax.dot_general(a, b, (((1,), (1,)), ((), ())), preferred_element_type=F32)


def _dot_tn(a, b):
    return lax.dot_general(a, b, (((0,), (0,)), ((), ())), preferred_element_type=F32)


def _sigmoid(x):
    return 1.0 / (1.0 + jnp.exp(-x))


def _gelu(x):
    return 0.5 * x * (1.0 + lax.erf(x * INV_SQRT2))


def _gelu_grad(x):
    return 0.5 * (1.0 + lax.erf(x * INV_SQRT2)) + x * (jnp.exp(-0.5 * x * x) * INV_SQRT_2PI)


def _rot(xh, cos2, sin2):
    return xh * cos2 + pltpu.roll(xh, HEAD_DIM // 2, 1) * sin2


def _rot_t(dh, cos2, sin2):
    return dh * cos2 + pltpu.roll(dh * sin2, HEAD_DIM // 2, 1)


def _rope_tables():
    half = HEAD_DIM // 2
    inv_freq = jnp.power(ROPE_BASE, -jnp.arange(half, dtype=F32) / half)
    ang = jnp.arange(SEQ, dtype=F32)[:, None] * inv_freq[None, :]
    cos, sin = jnp.cos(ang), jnp.sin(ang)
    cos2 = jnp.concatenate([cos, cos], axis=-1)
    sin2 = jnp.concatenate([-sin, sin], axis=-1)
    return cos2, sin2


def _decay_tables():
    log_gamma = jnp.log(1.0 - jnp.power(2.0, -5.0 - jnp.arange(HEADS, dtype=F32)))
    pos = jnp.arange(CHUNK, dtype=F32)
    diff = pos[:, None] - pos[None, :]
    mask = jnp.where(diff >= 0.0, jnp.exp(log_gamma[:, None, None] * jnp.maximum(diff, 0.0)[None]), 0.0)
    k_decay = jnp.exp(log_gamma[:, None] * (CHUNK - 1.0 - pos)[None])
    q_decay = jnp.exp(log_gamma[:, None] * (pos + 1.0)[None])
    kd = jnp.broadcast_to(k_decay[:, :, None], (HEADS, CHUNK, HEAD_DIM))
    qd = jnp.broadcast_to(q_decay[:, :, None], (HEADS, CHUNK, HEAD_DIM))
    return mask.astype(F32), qd.astype(F32), kd.astype(F32)


def _chunk_decay():
    lg = np.log(np.float32(1.0) - np.power(np.float32(2.0), -5.0 - np.arange(HEADS, dtype=np.float32))).astype(np.float32)
    return [float(np.exp(lg[h] * np.float32(CHUNK))) for h in range(HEADS)]


W_IN, W_OUT, W_UP, W_DOWN, W_CONV = range(5)
GATHERED = {W_IN: ((D_MODEL, PROJ_W), BF16), W_OUT: ((D_MODEL, D_MODEL), BF16), W_UP: ((2 * D_FF, D_MODEL), BF16),
            W_DOWN: ((D_FF, D_MODEL), BF16), W_CONV: ((N_DEV, 8, FF_SHARD), F32)}
SHARD = {W_IN: (D_MODEL, IN_SHARD), W_OUT: (OUT_SHARD, D_MODEL), W_UP: (FF_SHARD, D_MODEL), W_DOWN: (DOWN_SHARD, D_MODEL),
         W_CONV: (8, FF_SHARD)}


class _Gather:
    N_SEMS = 9

    def __init__(self, ids, stages, gathered, send_sems, recv_sems, local_sems):
        self.ids, self.stages, self.gathered = ids, stages, gathered
        self.send_sems, self.recv_sems, self.local_sems = send_sems, recv_sems, local_sems
        self.x, self.y, self.c = lax.axis_index("x"), lax.axis_index("y"), lax.axis_index("c")
        self.me = (self.x, self.y, self.c)
        self.sibling = (self.x, self.y, 1 - self.c)
        self.chips = [(1 - self.x, self.y), (self.x, 1 - self.y), (1 - self.x, 1 - self.y)]

    def slot(self, n, px, py, pc):
        dev = 4 * px + 2 * py + pc
        w, g = self.ids[n], self.gathered[n]
        if w == W_IN:
            return g.at[:, pl.ds(pl.multiple_of(dev * IN_SHARD, 128), IN_SHARD)]
        if w == W_OUT:
            return g.at[pl.ds(pl.multiple_of(dev * OUT_SHARD, 128), OUT_SHARD), :]
        if w == W_DOWN:
            return g.at[pl.ds(pl.multiple_of(dev * DOWN_SHARD, 32), DOWN_SHARD), :]
        if w == W_UP:
            return g.at[pl.ds(pl.multiple_of(dev * FF_SHARD, 32), FF_SHARD), :]
        return g.at[dev]

    def half(self, n, px, py, pc, h):
        dev = 4 * px + 2 * py + pc
        w, g = self.ids[n], self.gathered[n]
        if w == W_IN:
            return g.at[pl.ds(h * (D_MODEL // 2), D_MODEL // 2), pl.ds(pl.multiple_of(dev * IN_SHARD, 128), IN_SHARD)]
        rows = SHARD[w][0] // 2
        return g.at[pl.ds(pl.multiple_of(dev * SHARD[w][0] + h * rows, 16), rows), :]

    def tree(self, n):
        return self.ids[n] != W_CONV

    def copy(self, n, k, block, to, src=None, h=None):
        ref = self.slot(n, *block) if h is None else self.half(n, *block, h)
        return pltpu.make_async_remote_copy(
            src_ref=ref if src is None else src, dst_ref=ref,
            send_sem=self.send_sems.at[n, k], recv_sem=self.recv_sems.at[n, k], device_id=to, device_id_type=MESH)

    def _mine(self):
        return [pltpu.make_async_copy(self.stages[n], self.slot(n, *self.me), self.local_sems.at[n]) for n in range(len(self.ids))]

    def _first(self):
        out = []
        for n in range(len(self.ids)):
            out.append(self.copy(n, 0, self.me, self.sibling, src=self.stages[n]))
            out += [self.copy(n, 1 + j, self.me, (*chip, self.c), src=self.stages[n])
                    for j, chip in enumerate(self.chips[:2] if self.tree(n) else self.chips)]
        return out

    def start(self):
        for cp in self._mine() + self._first():
            cp.start()

    def finish(self):
        cx, cy, cd = [(*chip, self.c) for chip in self.chips]
        ox, oy, od = [(*chip, 1 - self.c) for chip in self.chips]
        passed = []

        def go(cp):
            cp.start()
            passed.append(cp)

        for n in range(len(self.ids)):
            if self.tree(n):
                self.copy(n, 1, cx, self.me).wait_recv()
                go(self.copy(n, 3, cx, cy, h=0))
                go(self.copy(n, 5, cx, self.sibling))
                self.copy(n, 2, cy, self.me).wait_recv()
                go(self.copy(n, 4, cy, cx, h=1))
                go(self.copy(n, 6, cy, self.sibling))
            else:
                for j, dev in enumerate((cx, cy, cd)):
                    self.copy(n, 1 + j, dev, self.me).wait_recv()
                    go(self.copy(n, 4 + j, dev, self.sibling))
        for n in range(len(self.ids)):
            if self.tree(n):
                self.copy(n, 3, cd, self.me, h=0).wait_recv()
                go(self.copy(n, 7, cd, self.sibling, h=0))
                self.copy(n, 4, cd, self.me, h=1).wait_recv()
                go(self.copy(n, 8, cd, self.sibling, h=1))
        for n in range(len(self.ids)):
            self.copy(n, 0, self.sibling, self.me).wait_recv()
            if self.tree(n):
                self.copy(n, 5, ox, self.me).wait_recv()
                self.copy(n, 6, oy, self.me).wait_recv()
                self.copy(n, 7, od, self.me, h=0).wait_recv()
                self.copy(n, 8, od, self.me, h=1).wait_recv()
            else:
                for j, dev in enumerate((ox, oy, od)):
                    self.copy(n, 4 + j, dev, self.me).wait_recv()
        for cp in self._first() + passed:
            cp.wait_send()
        for cp in self._mine():
            cp.wait()


def _gather_scratch(n):
    return [pltpu.SemaphoreType.DMA((n, _Gather.N_SEMS)), pltpu.SemaphoreType.DMA((n, _Gather.N_SEMS)), pltpu.SemaphoreType.DMA((n,))]


def _gathered_shapes(ids):
    return tuple(jax.ShapeDtypeStruct(*GATHERED[w]) for w in ids)


def _ag_first(w_in, w_out, w_up, w_down, conv_w):
    ids = [W_IN, W_CONV]

    def body(in_ref, out_ref, up_ref, dn_ref, cw_ref, gin, gcw, so_ref, su_ref, sd_ref, s_in, s_cw, send_sems, recv_sems, local_sems):
        s_in[...] = in_ref[...].astype(BF16)
        s_cw[...] = jnp.zeros_like(s_cw)
        s_cw[0:3, :] = cw_ref[...]
        ag = _Gather(ids, [s_in, s_cw], [gin, gcw], send_sems, recv_sems, local_sems)
        ag.start()
        so_ref[...] = out_ref[...].astype(BF16)
        su_ref[...] = up_ref[...].astype(BF16)
        sd_ref[...] = dn_ref[...].astype(BF16)
        ag.finish()

    vm = pl.BlockSpec(memory_space=pltpu.VMEM)
    hbm = pl.BlockSpec(memory_space=pl.ANY)
    return pl.pallas_call(
        body, name="ag_first",
        out_shape=_gathered_shapes(ids) + tuple(jax.ShapeDtypeStruct(SHARD[w], BF16) for w in (W_OUT, W_UP, W_DOWN)),
        in_specs=[vm] * 5, out_specs=(hbm, hbm, vm, vm, vm),
        scratch_shapes=[pltpu.VMEM(SHARD[W_IN], BF16), pltpu.VMEM(SHARD[W_CONV], F32)] + _gather_scratch(len(ids)),
        compiler_params=_cparams(),
    )(w_in, w_out, w_up, w_down, conv_w)


def _fwd_proj(x, g1, win_g, cos2, sin2, so, sd):
    ids = [W_OUT, W_DOWN]

    def body(x_ref, g_ref, w_ref, cos_ref, sin_ref, so_ref, sd_ref, proj_ref, h1_ref, gout, gdn, send_sems, recv_sems, local_sems):
        ag = _Gather(ids, [so_ref, sd_ref], [gout, gdn], send_sems, recv_sems, local_sems)

        @pl.when(pl.program_id(0) == 0)
        def _():
            ag.start()

        xb = x_ref[...]
        r = lax.rsqrt(jnp.mean(xb * xb, axis=-1, keepdims=True) + EPS)
        h = ((xb * r) * g_ref[...]).astype(BF16)
        h1_ref[...] = h
        p = _dot(h, w_ref[...])
        for hd in range(HEADS):
            sl = slice(hd * HEAD_DIM, (hd + 1) * HEAD_DIM)
            c2, s2 = cos_ref[...], sin_ref[...]
            proj_ref[:, sl] = _rot(p[:, sl], c2, s2)
            ks = slice(RET_W + hd * HEAD_DIM, RET_W + (hd + 1) * HEAD_DIM)
            proj_ref[:, ks] = _rot(p[:, ks], c2, s2) * K_SCALE
        proj_ref[:, 2 * RET_W:] = p[:, 2 * RET_W:]

        @pl.when(pl.program_id(0) == N_TB - 1)
        def _():
            ag.finish()

    tok = lambda w: pl.BlockSpec((TM, w), lambda i: (i, 0))
    hbm = pl.BlockSpec(memory_space=pl.ANY)
    return pl.pallas_call(
        body, name="fwd_proj", grid=(N_TB,),
        out_shape=(jax.ShapeDtypeStruct((SEQ, PROJ_W), F32), jax.ShapeDtypeStruct((SEQ, D_MODEL), BF16)) + _gathered_shapes(ids),
        in_specs=[tok(D_MODEL), _resident((1, D_MODEL)), _resident((D_MODEL, PROJ_W)), tok(HEAD_DIM), tok(HEAD_DIM), hbm, hbm],
        out_specs=(tok(PROJ_W), tok(D_MODEL), hbm, hbm),
        scratch_shapes=_gather_scratch(len(ids)),
        compiler_params=_cparams(("arbitrary",)),
    )(x, g1, win_g, cos2, sin2, so, sd)


def _causal(w):
    r = lax.broadcasted_iota(jnp.int32, (CHUNK, CHUNK), 0)
    c = lax.broadcasted_iota(jnp.int32, (CHUNK, CHUNK), 1)
    return jnp.where(r >= c, w, 0.0)


def _fwd_mix(x, proj, wout_g, grn, lng, lnb, ws, bsb, mask, qdec, kdec, su):
    cdec = _chunk_decay()
    ids = [W_UP]

    def body(x_ref, p_ref, w_ref, grn_ref, lng_ref, lnb_ref, ws_ref, bsb_ref, m_ref, qd_ref, kd_ref, su_ref,
             x2_ref, cat_ref, o_ref, sp_ref, gup, state, send_sems, recv_sems, local_sems):
        ag = _Gather(ids, [su_ref], [gup], send_sems, recv_sems, local_sems)

        @pl.when(pl.program_id(0) == 0)
        def _():
            state[...] = jnp.zeros_like(state)
            ag.start()

        for h in range(HEADS):
            sl = slice(h * HEAD_DIM, (h + 1) * HEAD_DIM)
            q = p_ref[:, sl]
            k = p_ref[:, RET_W + h * HEAD_DIM:RET_W + (h + 1) * HEAD_DIM]
            v = p_ref[:, 2 * RET_W + h * HEAD_DIM:2 * RET_W + (h + 1) * HEAD_DIM]
            g = p_ref[:, 3 * RET_W + h * HEAD_DIM:3 * RET_W + (h + 1) * HEAD_DIM]
            qb, kb, vb = q.astype(BF16), k.astype(BF16), v.astype(BF16)
            a = _dot_nt(qb, kb) * m_ref[h]
            spb = state[h].astype(BF16)
            sp_ref[0, h] = spb
            o = _dot(a.astype(BF16), vb) + _dot((q * qd_ref[h]).astype(BF16), spb)
            state[h] = state[h] * cdec[h] + _dot_tn((k * kd_ref[h]).astype(BF16), vb)
            o_ref[:, sl] = o
            rinv = lax.rsqrt(jnp.mean(o * o, axis=-1, keepdims=True) + EPS)
            rn = (o * rinv) * grn_ref[:, sl]
            cat_ref[:, sl] = ((g * _sigmoid(g)) * rn).astype(BF16)
        for gi in range(HEADS):
            sl = slice(gi * HEAD_DIM, (gi + 1) * HEAD_DIM)
            u = p_ref[:, 4 * RET_W + gi * HEAD_DIM:4 * RET_W + (gi + 1) * HEAD_DIM]
            sv = p_ref[:, 4 * RET_W + SGU_W + gi * HEAD_DIM:4 * RET_W + SGU_W + (gi + 1) * HEAD_DIM]
            gv = _gelu(sv)
            xc = gv - jnp.mean(gv, axis=-1, keepdims=True)
            vn = (xc * lax.rsqrt(jnp.mean(xc * xc, axis=-1, keepdims=True) + EPS)) * lng_ref[:, sl] + lnb_ref[:, sl]
            mixed = _dot(_causal(ws_ref[gi]).astype(BF16), vn.astype(BF16)) + bsb_ref[gi]
            cat_ref[:, RET_W + gi * HEAD_DIM:RET_W + (gi + 1) * HEAD_DIM] = (_gelu(u) * mixed).astype(BF16)
        x2_ref[...] = x_ref[...] + _dot(cat_ref[...], w_ref[...])

        @pl.when(pl.program_id(0) == N_CHUNK - 1)
        def _():
            ag.finish()

    ch = lambda w: pl.BlockSpec((CHUNK, w), lambda i: (i, 0))
    hcc = (HEADS, CHUNK, CHUNK)
    hbm = pl.BlockSpec(memory_space=pl.ANY)
    return pl.pallas_call(
        body, name="fwd_mix", grid=(N_CHUNK,),
        out_shape=(jax.ShapeDtypeStruct((SEQ, D_MODEL), F32), jax.ShapeDtypeStruct((SEQ, D_MODEL), BF16),
                   jax.ShapeDtypeStruct((SEQ, RET_W), F32), jax.ShapeDtypeStruct((N_CHUNK, HEADS, HEAD_DIM, HEAD_DIM), BF16))
        + _gathered_shapes(ids),
        in_specs=[ch(D_MODEL), ch(PROJ_W), _resident((D_MODEL, D_MODEL)), _resident((1, RET_W)), _resident((1, SGU_W)),
                  _resident((1, SGU_W)), _resident(hcc), _resident(hcc), _resident(hcc), _resident(hcc), _resident(hcc), hbm],
        out_specs=(ch(D_MODEL), ch(D_MODEL), ch(RET_W), pl.BlockSpec((1, HEADS, HEAD_DIM, HEAD_DIM), lambda i: (i, 0, 0, 0)), hbm),
        scratch_shapes=[pltpu.VMEM((HEADS, HEAD_DIM, HEAD_DIM), F32)] + _gather_scratch(len(ids)),
        compiler_params=_cparams(("arbitrary",)),
    )(x, proj, wout_g, grn, lng, lnb, ws, bsb, mask, qdec, kdec, su)


def _conv_taps(p, prev8):
    row = lax.broadcasted_iota(jnp.int32, p.shape, 0)
    p1 = jnp.where(row == 0, prev8[7:8, :], pltpu.roll(p, 1, 0))
    p2 = jnp.where(row == 0, prev8[6:7, :], jnp.where(row == 1, prev8[7:8, :], pltpu.roll(p, 2, 0)))
    return p1, p2


def _fwd_ffn(x2, g2, wup_g, cw_g, cb_g, wdn_g, gf, tgt):
    def body(x_ref, g_ref, wu_ref, cw_ref, cb_ref, wd_ref, gf_ref, t_ref, h2_ref, up_ref, u_ref, act_ref, x3_ref, loss_ref, carry):
        @pl.when(pl.program_id(0) == 0)
        def _():
            carry[...] = jnp.zeros_like(carry)

        xb = x_ref[...]
        r = lax.rsqrt(jnp.mean(xb * xb, axis=-1, keepdims=True) + EPS)
        h = ((xb * r) * g_ref[...]).astype(BF16)
        h2_ref[...] = h
        acc = xb
        for t in range(N_FF_TILE):
            u = []
            for c0 in (t * FF_TILE, D_FF + t * FF_TILE):
                cs = slice(c0, c0 + FF_TILE)
                p = _dot_nt(h, wu_ref[pl.ds(c0, FF_TILE), :])
                up_ref[:, cs] = p.astype(BF16)
                p1, p2 = _conv_taps(p, carry[:, cs])
                carry[:, cs] = p[TM - 8:, :]
                us = p2 * cw_ref[0:1, cs] + p1 * cw_ref[1:2, cs] + p * cw_ref[2:3, cs] + cb_ref[:, cs]
                u_ref[:, cs] = us.astype(BF16)
                u.append(us)
            a = ((u[0] * _sigmoid(u[0])) * u[1]).astype(BF16)
            act_ref[:, t * FF_TILE:(t + 1) * FF_TILE] = a
            acc = acc + _dot(a, wd_ref[pl.ds(t * FF_TILE, FF_TILE), :])
        x3_ref[...] = acc
        r3 = lax.rsqrt(jnp.mean(acc * acc, axis=-1, keepdims=True) + EPS)
        diff = (acc * r3) * gf_ref[...] - t_ref[...]
        loss_ref[...] = jnp.full(loss_ref.shape, 0.5 * jnp.sum(jnp.mean(diff * diff, axis=-1)), F32)

    tok = lambda w: pl.BlockSpec((TM, w), lambda i: (i, 0))
    return pl.pallas_call(
        body, name="fwd_ffn", grid=(N_TB,),
        out_shape=(jax.ShapeDtypeStruct((SEQ, D_MODEL), BF16), jax.ShapeDtypeStruct((SEQ, 2 * D_FF), BF16),
                   jax.ShapeDtypeStruct((SEQ, 2 * D_FF), BF16),
                   jax.ShapeDtypeStruct((SEQ, D_FF), BF16), jax.ShapeDtypeStruct((SEQ, D_MODEL), F32),
                   jax.ShapeDtypeStruct((N_TB, 8, 128), F32)),
        in_specs=[tok(D_MODEL), _resident((1, D_MODEL)), _resident((2 * D_FF, D_MODEL)), _resident((8, 2 * D_FF)),
                  _resident((1, 2 * D_FF)), _resident((D_FF, D_MODEL)), _resident((1, D_MODEL)), tok(D_MODEL)],
        out_specs=(tok(D_MODEL), tok(2 * D_FF), tok(2 * D_FF), tok(D_FF), tok(D_MODEL),
                   pl.BlockSpec((1, 8, 128), lambda i: (i, 0, 0))),
        scratch_shapes=[pltpu.VMEM((8, 2 * D_FF), F32)],
        compiler_params=_cparams(("arbitrary",)),
    )(x2, g2, wup_g, cw_g, cb_g, wdn_g, gf, tgt)


def _bwd_ffn(x3, tgt, gf, x2, g2, up_pre, u_conv, wup_g, cw_g, wdn_g):
    def body(x3_ref, t_ref, gf_ref, x2_ref, g2_ref, up_ref, u_ref, wu_ref, cw_ref, wd_ref,
             dx3_ref, dpre_ref, dx2_ref, dgf_ref, dg2_ref, dcv_ref, nxt):
        i = pl.program_id(0)

        @pl.when(i == 0)
        def _():
            nxt[...] = jnp.zeros_like(nxt)
            dgf_ref[...] = jnp.zeros_like(dgf_ref)
            dg2_ref[...] = jnp.zeros_like(dg2_ref)
            dcv_ref[...] = jnp.zeros_like(dcv_ref)

        x3 = x3_ref[...]
        r3 = lax.rsqrt(jnp.mean(x3 * x3, axis=-1, keepdims=True) + EPS)
        xh3 = x3 * r3
        dy = (xh3 * gf_ref[...] - t_ref[...]) * (1.0 / D_MODEL)
        dgf_ref[0:1, :] += jnp.sum(dy * xh3, axis=0, keepdims=True)
        t3 = dy * gf_ref[...]
        dx3 = r3 * (t3 - xh3 * jnp.mean(t3 * xh3, axis=-1, keepdims=True))
        dx3b = dx3.astype(BF16)
        dx3_ref[...] = dx3b
        dh2 = jnp.zeros((TM, D_MODEL), F32)
        row = lax.broadcasted_iota(jnp.int32, (TM, FF_TILE), 0)
        for t in range(N_FF_TILE):
            ts = slice(t * FF_TILE, (t + 1) * FF_TILE)
            dact = _dot_nt(dx3b, wd_ref[pl.ds(t * FF_TILE, FF_TILE), :])
            ua = u_ref[:, ts].astype(F32)
            ub = u_ref[:, D_FF + t * FF_TILE:D_FF + (t + 1) * FF_TILE].astype(F32)
            sg = _sigmoid(ua)
            du = [dact * ub * (sg * (1.0 + ua * (1.0 - sg))), dact * (ua * sg)]
            for n in range(2):
                d = du[n]
                c0 = n * D_FF + t * FF_TILE
                cs = slice(c0, c0 + FF_TILE)
                nx = nxt[:, cs]
                n1 = jnp.where(row == TM - 1, nx[0:1, :], pltpu.roll(d, TM - 1, 0))
                n2 = jnp.where(row == TM - 2, nx[0:1, :], jnp.where(row == TM - 1, nx[1:2, :], pltpu.roll(d, TM - 2, 0)))
                nxt[:, cs] = d[0:8, :]
                dp = (d * cw_ref[2:3, cs] + n1 * cw_ref[1:2, cs] + n2 * cw_ref[0:1, cs]).astype(BF16)
                dpre_ref[:, cs] = dp
                p = up_ref[:, cs].astype(F32)
                dcv_ref[n, 0:1, ts] += jnp.sum(n2 * p, axis=0, keepdims=True)
                dcv_ref[n, 1:2, ts] += jnp.sum(n1 * p, axis=0, keepdims=True)
                dcv_ref[n, 2:3, ts] += jnp.sum(d * p, axis=0, keepdims=True)
                dcv_ref[n, 3:4, ts] += jnp.sum(d, axis=0, keepdims=True)
                dh2 = dh2 + _dot(dp, wu_ref[pl.ds(c0, FF_TILE), :])
        x2 = x2_ref[...]
        r2 = lax.rsqrt(jnp.mean(x2 * x2, axis=-1, keepdims=True) + EPS)
        xh2 = x2 * r2
        dg2_ref[0:1, :] += jnp.sum(dh2 * xh2, axis=0, keepdims=True)
        t2 = dh2 * g2_ref[...]
        dx2_ref[...] = dx3 + r2 * (t2 - xh2 * jnp.mean(t2 * xh2, axis=-1, keepdims=True))

    rev = lambda w: pl.BlockSpec((TM, w), lambda i: (N_TB - 1 - i, 0))
    acc = lambda s: pl.BlockSpec(s, lambda i: (0,) * len(s))
    return pl.pallas_call(
        body, name="bwd_ffn", grid=(N_TB,),
        out_shape=(jax.ShapeDtypeStruct((SEQ, D_MODEL), BF16), jax.ShapeDtypeStruct((SEQ, 2 * D_FF), BF16),
                   jax.ShapeDtypeStruct((SEQ, D_MODEL), F32), jax.ShapeDtypeStruct((8, D_MODEL), F32),
                   jax.ShapeDtypeStruct((8, D_MODEL), F32), jax.ShapeDtypeStruct((2, 8, D_FF), F32)),
        in_specs=[rev(D_MODEL), rev(D_MODEL), _resident((1, D_MODEL)), rev(D_MODEL), _resident((1, D_MODEL)), rev(2 * D_FF),
                  rev(2 * D_FF), _resident((2 * D_FF, D_MODEL)), _resident((8, 2 * D_FF)), _resident((D_FF, D_MODEL))],
        out_specs=(rev(D_MODEL), rev(2 * D_FF), rev(D_MODEL), acc((8, D_MODEL)), acc((8, D_MODEL)), acc((2, 8, D_FF))),
        scratch_shapes=[pltpu.VMEM((8, 2 * D_FF), F32)],
        compiler_params=_cparams(("arbitrary",)),
    )(x3, tgt, gf, x2, g2, up_pre, u_conv, wup_g, cw_g, wdn_g)


def _bwd_mix(dx2, proj, o, sprev, wout_g, grn, lng, lnb, ws, bsb, mask, qdec, kdec, cos2, sin2, hosted):
    cdec = _chunk_decay()
    geoms = [g for g, _ in hosted]
    n_h = len(hosted)

    def body(dx2_ref, p_ref, o_ref, sp_ref, w_ref, grn_ref, lng_ref, lnb_ref, ws_ref, bsb_ref, m_ref, qd_ref, kd_ref,
             cos_ref, sin_ref, *rest):
        dp_ref, dgrn_ref, dlng_ref, dlnb_ref, dws_ref, dbs_ref = rest[n_h:n_h + 6]
        dstate, dbs_acc = rest[2 * n_h + 6:2 * n_h + 8]
        i = pl.program_id(0)
        rs = _Scatters(geoms, rest[:n_h], rest[n_h + 6:2 * n_h + 6], rest[2 * n_h + 8:])
        pl.when(i == 0)(rs.phase1)
        pl.when(i == 3)(rs.phase2)
        pl.when(i == 6)(rs.phase2b)

        @pl.when(i == 0)
        def _():
            dstate[...] = jnp.zeros_like(dstate)
            dgrn_ref[...] = jnp.zeros_like(dgrn_ref)
            dlng_ref[...] = jnp.zeros_like(dlng_ref)
            dlnb_ref[...] = jnp.zeros_like(dlnb_ref)
            dws_ref[...] = jnp.zeros_like(dws_ref)
            dbs_ref[...] = jnp.zeros_like(dbs_ref)
            dbs_acc[...] = jnp.zeros_like(dbs_acc)

        dmix = _dot_nt(dx2_ref[...].astype(BF16), w_ref[...])
        for h in range(HEADS):
            sl = slice(h * HEAD_DIM, (h + 1) * HEAD_DIM)
            q = p_ref[:, sl]
            k = p_ref[:, RET_W + h * HEAD_DIM:RET_W + (h + 1) * HEAD_DIM]
            v = p_ref[:, 2 * RET_W + h * HEAD_DIM:2 * RET_W + (h + 1) * HEAD_DIM]
            g = p_ref[:, 3 * RET_W + h * HEAD_DIM:3 * RET_W + (h + 1) * HEAD_DIM]
            o = o_ref[:, sl]
            rinv = lax.rsqrt(jnp.mean(o * o, axis=-1, keepdims=True) + EPS)
            oh = o * rinv
            gr = grn_ref[:, sl]
            sg = _sigmoid(g)
            dret = dmix[:, sl]
            dp_ref[:, 3 * RET_W + h * HEAD_DIM:3 * RET_W + (h + 1) * HEAD_DIM] = (
                dret * (oh * gr) * (sg * (1.0 + g * (1.0 - sg)))).astype(BF16)
            drn = dret * (g * sg)
            dgrn_ref[0:1, sl] += jnp.sum(drn * oh, axis=0, keepdims=True)
            t = drn * gr
            do = rinv * (t - oh * jnp.mean(t * oh, axis=-1, keepdims=True))
            qb, kb, vb, dob = q.astype(BF16), k.astype(BF16), v.astype(BF16), do.astype(BF16)
            m = m_ref[h]
            ab = (_dot_nt(qb, kb) * m).astype(BF16)
            dab = (_dot_nt(dob, vb) * m).astype(BF16)
            spb = sp_ref[0, h]
            dsn = dstate[h]
            dsnb = dsn.astype(BF16)
            qdb = (q * qd_ref[h]).astype(BF16)
            kdb = (k * kd_ref[h]).astype(BF16)
            dq = _dot(dab, kb) + _dot_nt(dob, spb) * qd_ref[h]
            dk = _dot_tn(dab, qb) + _dot_nt(vb, dsnb) * kd_ref[h]
            dv = _dot_tn(ab, dob) + _dot(kdb, dsnb)
            dstate[h] = dsn * cdec[h] + _dot_tn(qdb, dob)
            c2, s2 = cos_ref[...], sin_ref[...]
            dp_ref[:, sl] = _rot_t(dq, c2, s2).astype(BF16)
            dp_ref[:, RET_W + h * HEAD_DIM:RET_W + (h + 1) * HEAD_DIM] = _rot_t(dk * K_SCALE, c2, s2).astype(BF16)
            dp_ref[:, 2 * RET_W + h * HEAD_DIM:2 * RET_W + (h + 1) * HEAD_DIM] = dv.astype(BF16)
        for gi in range(HEADS):
            sl = slice(gi * HEAD_DIM, (gi + 1) * HEAD_DIM)
            u = p_ref[:, 4 * RET_W + gi * HEAD_DIM:4 * RET_W + (gi + 1) * HEAD_DIM]
            sv = p_ref[:, 4 * RET_W + SGU_W + gi * HEAD_DIM:4 * RET_W + SGU_W + (gi + 1) * HEAD_DIM]
            gv = _gelu(sv)
            xc = gv - jnp.mean(gv, axis=-1, keepdims=True)
            rstd = lax.rsqrt(jnp.mean(xc * xc, axis=-1, keepdims=True) + EPS)
            xh = xc * rstd
            lg = lng_ref[:, sl]
            vnb = (xh * lg + lnb_ref[:, sl]).astype(BF16)
            wcb = _causal(ws_ref[gi]).astype(BF16)
            mixed = _dot(wcb, vnb) + bsb_ref[gi]
            dsgu = dmix[:, RET_W + gi * HEAD_DIM:RET_W + (gi + 1) * HEAD_DIM]
            dmixed = dsgu * _gelu(u)
            dmb = dmixed.astype(BF16)
            dws_ref[gi] += _causal(_dot_nt(dmb, vnb))
            dbs_acc[gi] += dmixed
            dvn = _dot_tn(wcb, dmb)
            dlng_ref[gi:gi + 1, :] += jnp.sum(dvn * xh, axis=0, keepdims=True)
            dlnb_ref[gi:gi + 1, :] += jnp.sum(dvn, axis=0, keepdims=True)
            dxh = dvn * lg
            dgv = rstd * (dxh - jnp.mean(dxh, axis=-1, keepdims=True) - xh * jnp.mean(dxh * xh, axis=-1, keepdims=True))
            dp_ref[:, 4 * RET_W + gi * HEAD_DIM:4 * RET_W + (gi + 1) * HEAD_DIM] = (dsgu * mixed * _gelu_grad(u)).astype(BF16)
            dp_ref[:, 4 * RET_W + SGU_W + gi * HEAD_DIM:4 * RET_W + SGU_W + (gi + 1) * HEAD_DIM] = (
                dgv * _gelu_grad(sv)).astype(BF16)

        @pl.when(i == N_CHUNK - 1)
        def _():
            for gi in range(HEADS):
                col = jnp.broadcast_to(jnp.sum(dbs_acc[gi], axis=-1, keepdims=True), (CHUNK, CHUNK))
                dbs_ref[gi:gi + 1, :] = jnp.transpose(col)[0:1, :]
            rs.phase3()

    rev = lambda w: pl.BlockSpec((CHUNK, w), lambda i: (N_CHUNK - 1 - i, 0))
    hcc = (HEADS, CHUNK, CHUNK)
    acc = lambda s: pl.BlockSpec(s, lambda i: (0,) * len(s))
    res = pl.pallas_call(
        body, name="bwd_mix", grid=(N_CHUNK,),
        out_shape=(jax.ShapeDtypeStruct((SEQ, PROJ_W), BF16), jax.ShapeDtypeStruct((8, RET_W), F32),
                   jax.ShapeDtypeStruct((8, HEAD_DIM), F32), jax.ShapeDtypeStruct((8, HEAD_DIM), F32),
                   jax.ShapeDtypeStruct(hcc, F32), jax.ShapeDtypeStruct((8, CHUNK), F32)) + _scatter_out_shapes(geoms),
        in_specs=[rev(D_MODEL), rev(PROJ_W), rev(RET_W),
                  pl.BlockSpec((1, HEADS, HEAD_DIM, HEAD_DIM), lambda i: (N_CHUNK - 1 - i, 0, 0, 0)),
                  _resident((D_MODEL, D_MODEL)), _resident((1, RET_W)), _resident((1, SGU_W)), _resident((1, SGU_W)),
                  _resident(hcc), _resident(hcc), _resident(hcc), _resident(hcc), _resident(hcc), rev(HEAD_DIM), rev(HEAD_DIM)]
        + [pl.BlockSpec(memory_space=pl.ANY)] * n_h,
        out_specs=(rev(PROJ_W), acc((8, RET_W)), acc((8, HEAD_DIM)), acc((8, HEAD_DIM)), acc(hcc), acc((8, CHUNK)))
        + _scatter_out_specs(geoms),
        scratch_shapes=[pltpu.VMEM((HEADS, HEAD_DIM, HEAD_DIM), F32), pltpu.VMEM((HEADS, CHUNK, CHUNK), F32)] + _scatter_scratch(geoms),
        compiler_params=_cparams(("arbitrary",)),
    )(dx2, proj, o, sprev, wout_g, grn, lng, lnb, ws, bsb, mask, qdec, kdec, cos2, sin2, *[p for _, p in hosted])
    return tuple(res[:6 + n_h])


def _bwd_proj(dproj, win_g, x, g1, dx2, gin_p):
    geoms = [W_IN]

    def body(dp_ref, w_ref, x_ref, g_ref, dx2_ref, gin_ref, dx_ref, dg_ref, rs_out, *rs_scratch):
        rs = _Scatters(geoms, [gin_ref], [rs_out], rs_scratch)
        pl.when(pl.program_id(0) == 0)(rs.phase1)
        pl.when(pl.program_id(0) == 2)(rs.phase2)
        pl.when(pl.program_id(0) == 4)(rs.phase2b)

        @pl.when(pl.program_id(0) == 0)
        def _():
            dg_ref[...] = jnp.zeros_like(dg_ref)

        dh = _dot_nt(dp_ref[...], w_ref[...])
        xb = x_ref[...]
        r = lax.rsqrt(jnp.mean(xb * xb, axis=-1, keepdims=True) + EPS)
        xh = xb * r
        dg_ref[0:1, :] += jnp.sum(dh * xh, axis=0, keepdims=True)
        t = dh * g_ref[...]
        dx_ref[...] = dx2_ref[...] + r * (t - xh * jnp.mean(t * xh, axis=-1, keepdims=True))
        pl.when(pl.program_id(0) == N_TB - 1)(rs.phase3)

    tok = lambda w: pl.BlockSpec((TM, w), lambda i: (i, 0))
    res = pl.pallas_call(
        body, name="bwd_proj", grid=(N_TB,),
        out_shape=(jax.ShapeDtypeStruct((SEQ, D_MODEL), F32), jax.ShapeDtypeStruct((8, D_MODEL), F32)) + _scatter_out_shapes(geoms),
        in_specs=[tok(PROJ_W), _resident((D_MODEL, PROJ_W)), tok(D_MODEL), _resident((1, D_MODEL)), tok(D_MODEL),
                  pl.BlockSpec(memory_space=pl.ANY)],
        out_specs=(tok(D_MODEL), pl.BlockSpec((8, D_MODEL), lambda i: (0, 0))) + _scatter_out_specs(geoms),
        scratch_shapes=_scatter_scratch(geoms),
        compiler_params=_cparams(("arbitrary",)),
    )(dproj, win_g, x, g1, dx2, gin_p)
    return res[:3]


def _wgrad(name, a, b, tm=None, tn=None, hosted=()):
    m_w, n_w = a.shape[-1], b.shape[-1]
    tm = m_w if tm is None else tm
    tn = n_w if tn is None else tn
    n_steps = (m_w // tm) * (n_w // tn)
    geoms = [g for g, _ in hosted]
    n_h = len(hosted)

    def body(a_ref, b_ref, *rest):
        o_ref = rest[n_h]
        if n_h:
            rs = _Scatters(geoms, rest[:n_h], rest[n_h + 1:2 * n_h + 1], rest[2 * n_h + 1:])
            step = pl.program_id(0) * (n_w // tn) + pl.program_id(1)
            pl.when(step == 0)(rs.phase1)
            pl.when(step == 1)(rs.phase2)
            pl.when(step == 2)(rs.phase2b)
        o_ref[...] = _dot_tn(a_ref[...].astype(BF16), b_ref[...].astype(BF16)).astype(BF16)
        if n_h:
            pl.when(step == n_steps - 1)(rs.phase3)

    assert not n_h or n_steps >= 4
    res = pl.pallas_call(
        body, name=name, grid=(m_w // tm, n_w // tn),
        out_shape=(jax.ShapeDtypeStruct((m_w, n_w), BF16),) + _scatter_out_shapes(geoms),
        in_specs=[pl.BlockSpec((SEQ, tm), lambda i, j: (0, i)), pl.BlockSpec((SEQ, tn), lambda i, j: (0, j))]
        + [pl.BlockSpec(memory_space=pl.ANY)] * n_h,
        out_specs=(pl.BlockSpec((tm, tn), lambda i, j: (i, j)),) + _scatter_out_specs(geoms),
        scratch_shapes=_scatter_scratch(geoms),
        compiler_params=_cparams(("arbitrary", "arbitrary") if n_h else ("parallel", "parallel")),
    )(a, b, *[p for _, p in hosted])
    return tuple(res[:1 + n_h])


RS_ROWS = {W_IN: 128, W_OUT: 64, W_UP: 176, W_DOWN: 176}


class _Scatter:
    def __init__(self, geom, partial, out, land1, mine, stage2, land2, comb, s1_send, s1_recv, s2_send, s2_recv, ld_sems):
        self.w, self.row0, self.shape = _geom(geom)
        self.partial, self.out, self.land1 = partial, out, land1
        self.mine, self.stage2, self.land2, self.comb = mine, stage2, land2, comb
        self.hr = self.shape[0] // 2
        self.s1_send, self.s1_recv, self.s2_send, self.s2_recv, self.ld_sems = s1_send, s1_recv, s2_send, s2_recv, ld_sems
        self.x, self.y, self.c = lax.axis_index("x"), lax.axis_index("y"), lax.axis_index("c")
        self.sibling = (self.x, self.y, 1 - self.c)
        self.chips = [(self.x, self.y), (1 - self.x, self.y), (self.x, 1 - self.y), (1 - self.x, 1 - self.y)]

    def block(self, px, py, pc):
        dev = 4 * px + 2 * py + pc
        if self.w == W_IN:
            return self.partial.at[:, pl.ds(pl.multiple_of(dev * IN_SHARD, 128), IN_SHARD)]
        if self.w == W_OUT:
            return self.partial.at[pl.ds(pl.multiple_of(dev * OUT_SHARD, 128), OUT_SHARD), :]
        if self.w == W_DOWN:
            return self.partial.at[pl.ds(pl.multiple_of(dev * DOWN_SHARD, 32), DOWN_SHARD), :]
        return self.partial.at[pl.ds(pl.multiple_of(dev * FF_SHARD + self.row0, 32), self.shape[0]), :]

    def copy1(self, k):
        return pltpu.make_async_remote_copy(
            src_ref=self.block(*self.chips[k], 1 - self.c), dst_ref=self.land1.at[k],
            send_sem=self.s1_send.at[k], recv_sem=self.s1_recv.at[k], device_id=self.sibling, device_id_type=MESH)

    STAGE2 = [(1, 0, 1), (3, 0, 1), (2, 1, 2), (3, 1, 2), (1, 1, 1), (2, 0, 2)]

    def copy2(self, j):
        blk, h, to = self.STAGE2[j]
        src = self.comb.at[j - 4] if j >= 4 else self.stage2.at[blk - 1, pl.ds(h * self.hr, self.hr), :]
        return pltpu.make_async_remote_copy(
            src_ref=src, dst_ref=self.land2.at[j], send_sem=self.s2_send.at[j], recv_sem=self.s2_recv.at[j],
            device_id=(*self.chips[to], self.c), device_id_type=MESH)

    def _rows(self, h=None):
        step = RS_ROWS[self.w]
        lo, n = (0, self.shape[0]) if h is None else (h * self.hr, self.hr)
        return [pl.ds(r0, step) for r0 in range(lo, lo + n, step)]

    def load(self, k):
        return pltpu.make_async_copy(self.block(*self.chips[k], self.c), self.mine.at[k], self.ld_sems.at[k])

    def phase1(self):
        for k in range(4):
            self.copy1(k).start()
        for k in range(4):
            self.load(k).start()

    def phase2(self):
        for k in (3, 1, 2, 0):
            self.copy1(k).wait_recv()
            self.load(k).wait()
            for rs in self._rows():
                s = self.mine[k, rs, :].astype(F32) + self.land1[k, rs, :].astype(F32)
                if k == 0:
                    self.out[rs, :] = s
                else:
                    self.stage2[k - 1, rs, :] = s.astype(BF16)
            for j in {3: (1, 3), 1: (0,), 2: (2,), 0: ()}[k]:
                self.copy2(j).start()

    def phase2b(self):
        for j, got in ((4, 3), (5, 1)):
            blk, h, _ = self.STAGE2[j]
            self.copy2(got).wait_recv()
            for i, rs in enumerate(self._rows(h)):
                lr = pl.ds(i * RS_ROWS[self.w], RS_ROWS[self.w])
                self.comb[j - 4, lr, :] = (self.stage2[blk - 1, rs, :].astype(F32) + self.land2[got, lr, :].astype(F32)).astype(BF16)
            self.copy2(j).start()

    def phase3(self):
        for j in (0, 5, 4, 2):
            self.copy2(j).wait_recv()
        for h, (first, second) in enumerate(((0, 5), (4, 2))):
            for i, rs in enumerate(self._rows(h)):
                lr = pl.ds(i * RS_ROWS[self.w], RS_ROWS[self.w])
                self.out[rs, :] = (self.out[rs, :] + self.land2[first, lr, :].astype(F32)) + self.land2[second, lr, :].astype(F32)
        for k in range(4):
            self.copy1(k).wait_send()
        for j in range(6):
            self.copy2(j).wait_send()


def _geom(geom):
    if isinstance(geom, tuple):
        w, row0, rows = geom
        assert w == W_UP
        return w, row0, (rows, SHARD[w][1])
    return geom, 0, SHARD[geom]


N_SCATTER_SCRATCH = 10


def _scatter_out_shapes(geoms):
    return tuple(jax.ShapeDtypeStruct(_geom(g)[2], F32) for g in geoms)


def _scatter_out_specs(geoms):
    return (pl.BlockSpec(memory_space=pltpu.VMEM),) * len(geoms)


def _scatter_scratch(geoms):
    out = []
    for g in geoms:
        s = _geom(g)[2]
        hs = (s[0] // 2, s[1])
        out += [pltpu.VMEM((4,) + s, BF16), pltpu.VMEM((4,) + s, BF16), pltpu.VMEM((3,) + s, BF16), pltpu.VMEM((6,) + hs, BF16),
                pltpu.VMEM((2,) + hs, BF16),
                pltpu.SemaphoreType.DMA((4,)), pltpu.SemaphoreType.DMA((4,)), pltpu.SemaphoreType.DMA((6,)),
                pltpu.SemaphoreType.DMA((6,)), pltpu.SemaphoreType.DMA((4,))]
    return out


class _Scatters:
    def __init__(self, geoms, p_refs, out_refs, scratch):
        k = N_SCATTER_SCRATCH
        self.items = [_Scatter(g, p_refs[i], out_refs[i], *scratch[k * i:k * i + k]) for i, g in enumerate(geoms)]

    def phase1(self):
        for s in self.items:
            s.phase1()

    def phase2(self):
        for s in self.items:
            s.phase2()

    def phase2b(self):
        for s in self.items:
            s.phase2b()

    def phase3(self):
        for s in self.items:
            s.phase3()


PACK_W = 1024


def _all_reduce_small(dg1, dg2, dgf, dgrn, dlng, dlnb, dbs, loss_parts, dws, dcv):
    n_a = 3

    def body(dg1_ref, dg2_ref, dgf_ref, dgrn_ref, dlng_ref, dlnb_ref, dbs_ref, loss_ref, dws_ref, dcv_ref,
             rp_ref, rws_ref, rcv_ref, pack, rx_p, rx_ws, rx_cv, cs_p, cs_ws, cs_cv, g_p, g_ws, g_cv,
             s1_send, s1_recv, s2_send, s2_recv, s3_send, s3_recv):
        x, y, c = lax.axis_index("x"), lax.axis_index("y"), lax.axis_index("c")
        sibling = (x, y, 1 - c)
        chips = [(1 - x, y), (x, 1 - y), (1 - x, 1 - y)]
        pack[...] = jnp.zeros_like(pack)
        pack[0, 0:1, :] = dg1_ref[0:1, :]
        pack[0, 1:2, :] = dg2_ref[0:1, :]
        pack[0, 2:3, :] = dgf_ref[0:1, :]
        pack[0, 3:4, 0:RET_W] = dgrn_ref[0:1, :]
        lsum = loss_ref[0, 0:1, :]
        for i in range(1, N_TB):
            lsum = lsum + loss_ref[i, 0:1, :]
        pack[0, 3:4, RET_W:RET_W + 128] = lsum
        pack[1, 0:HEADS, 0:128] = dlng_ref[0:HEADS, :]
        pack[1, 0:HEADS, 128:256] = dlnb_ref[0:HEADS, :]
        pack[1, 0:HEADS, 256:384] = dbs_ref[0:HEADS, :]

        srcs = [pack, dws_ref, dcv_ref]
        outs = [rp_ref, rws_ref, rcv_ref]
        rxs = [rx_p, rx_ws, rx_cv]
        css = [cs_p, cs_ws, cs_cv]
        gs = [g_p, g_ws, g_cv]
        hl = [1, HEADS // 2, 1]

        def half(ref, a, h):
            return ref.at[pl.ds(h * hl[a], hl[a])]

        ex1 = [pltpu.make_async_remote_copy(src_ref=half(srcs[a], a, 1 - c), dst_ref=rxs[a], send_sem=s1_send.at[a],
                                            recv_sem=s1_recv.at[a], device_id=sibling, device_id_type=MESH) for a in range(n_a)]
        for cp in ex1:
            cp.start()
        ex2 = []
        for a in range(n_a):
            ex1[a].wait_recv()
            css[a][...] = half(srcs[a], a, c)[...] + rxs[a][...]
            for j, chip in enumerate(chips):
                cp = pltpu.make_async_remote_copy(src_ref=css[a], dst_ref=gs[a].at[j], send_sem=s2_send.at[a, j],
                                                  recv_sem=s2_recv.at[a, j], device_id=(*chip, c), device_id_type=MESH)
                cp.start()
                ex2.append(cp)
        ex3 = []
        for a in range(n_a):
            for j in range(3):
                ex2[3 * a + j].wait_recv()
            tot = None
            for q in range(4):
                k = jnp.where(x != (q >> 1), 1, 0) + jnp.where(y != (q & 1), 2, 0)
                term = jnp.where(k == 0, css[a][...], jnp.where(k == 1, gs[a][0], jnp.where(k == 2, gs[a][1], gs[a][2])))
                tot = term if tot is None else tot + term
            half(outs[a], a, c)[...] = tot
            cp = pltpu.make_async_remote_copy(src_ref=half(outs[a], a, c), dst_ref=half(outs[a], a, c), send_sem=s3_send.at[a],
                                              recv_sem=s3_recv.at[a], device_id=sibling, device_id_type=MESH)
            cp.start()
            ex3.append(cp)
        for a in range(n_a):
            pltpu.make_async_remote_copy(src_ref=half(outs[a], a, 1 - c), dst_ref=half(outs[a], a, 1 - c), send_sem=s3_send.at[a],
                                         recv_sem=s3_recv.at[a], device_id=sibling, device_id_type=MESH).wait_recv()
        for cp in ex1 + ex2 + ex3:
            cp.wait_send()

    vm = pl.BlockSpec(memory_space=pltpu.VMEM)
    full = [(2, 8, PACK_W), (HEADS, CHUNK, CHUNK), (2, 8, D_FF)]
    halves = [(s[0] // 2,) + s[1:] for s in full]
    return pl.pallas_call(
        body, name="ar_small",
        out_shape=tuple(jax.ShapeDtypeStruct(s, F32) for s in full),
        in_specs=[vm] * 10, out_specs=(vm,) * 3,
        scratch_shapes=[pltpu.VMEM(full[0], F32)] + [pltpu.VMEM(s, F32) for s in halves] + [pltpu.VMEM(s, F32) for s in halves]
        + [pltpu.VMEM((3,) + s, F32) for s in halves]
        + [pltpu.SemaphoreType.DMA((n_a,)), pltpu.SemaphoreType.DMA((n_a,)), pltpu.SemaphoreType.DMA((n_a, 3)),
           pltpu.SemaphoreType.DMA((n_a, 3)), pltpu.SemaphoreType.DMA((n_a,)), pltpu.SemaphoreType.DMA((n_a,))],
        compiler_params=_cparams(),
    )(dg1, dg2, dgf, dgrn, dlng, dlnb, dbs, loss_parts, dws, dcv)


def _adam_math(w, g, m, v):
    nm = ADAM_B1 * m + (1.0 - ADAM_B1) * g
    nv = ADAM_B2 * v + (1.0 - ADAM_B2) * (g * g)
    d = -ADAM_LR * ((nm / (1.0 - ADAM_B1 ** ADAM_STEP)) / (jnp.sqrt(nv / (1.0 - ADAM_B2 ** ADAM_STEP)) + ADAM_EPS) + ADAM_WD * w)
    return d, nm, nv


def _adamw(name, w, gs, m, v, rows):
    _, r, cdim = w.shape
    n_steps = r // rows
    half = n_steps // len(gs)

    def body(w_ref, *rest):
        g_refs, (m_ref, v_ref, go_ref, d_ref, nm_ref, nv_ref) = rest[:len(gs)], rest[len(gs):]
        gg = g_refs[0][...]
        if len(gs) == 2:
            gg = jnp.where(pl.program_id(0) < half, gg, g_refs[1][...])
        go_ref[0] = gg
        d, nm, nv = _adam_math(w_ref[0], gg, m_ref[0], v_ref[0])
        d_ref[0], nm_ref[0], nv_ref[0] = d, nm, nv

    spec3 = pl.BlockSpec((1, rows, cdim), lambda i: (0, i, 0))
    if len(gs) == 1:
        g_specs = [pl.BlockSpec((rows, cdim), lambda i: (i, 0))]
    else:
        g_specs = [pl.BlockSpec((rows, cdim), lambda i: (jnp.minimum(i, half - 1), 0)),
                   pl.BlockSpec((rows, cdim), lambda i: (jnp.maximum(i - half, 0), 0))]
    sh = jax.ShapeDtypeStruct((1, r, cdim), F32)
    return pl.pallas_call(
        body, name=name, grid=(n_steps,), out_shape=(sh, sh, sh, sh),
        in_specs=[spec3] + g_specs + [spec3, spec3], out_specs=(spec3,) * 4,
        compiler_params=_cparams(("parallel",)),
    )(w, *gs, m, v)


def _adamw_small(rp, rws, rcv, gcw, params):
    n_p = len(params)

    def body(*refs):
        rp_ref, rws_ref, rcv_ref, gcw_ref = refs[:4]
        ins = refs[4:4 + 3 * n_p]
        outs = refs[4 + 3 * n_p:]
        grads = [rp_ref[0, 0:1, :], rp_ref[0, 1:2, :], rp_ref[0, 2:3, :], rp_ref[0, 3:4, 0:RET_W],
                 rp_ref[1, 0:HEADS, 0:128], rp_ref[1, 0:HEADS, 128:256], rp_ref[1, 0:HEADS, 256:384],
                 rws_ref[...], gcw_ref[0], None]
        for p in range(n_p):
            w_ref, m_ref, v_ref = ins[3 * p:3 * p + 3]
            o = outs[4 * p:4 * p + 4]
            if p == n_p - 1:
                for hf in range(2):
                    cs = slice(hf * D_FF, (hf + 1) * D_FF)
                    g = rcv_ref[hf, 3:4, :]
                    res = (g,) + _adam_math(w_ref[:, cs], g, m_ref[:, cs], v_ref[:, cs])
                    for t in range(4):
                        o[t][:, cs] = res[t]
                continue
            lead = w_ref.ndim > grads[p].ndim
            rd = (lambda r: r[0]) if lead else (lambda r: r[...])
            res = (grads[p],) + _adam_math(rd(w_ref), grads[p], rd(m_ref), rd(v_ref))
            for t in range(4):
                if lead:
                    o[t][0] = res[t]
                else:
                    o[t][...] = res[t]

    vm = pl.BlockSpec(memory_space=pltpu.VMEM)
    flat = [a for tr in params for a in tr]
    out_shape = tuple(jax.ShapeDtypeStruct(tr[0].shape, F32) for tr in params for _ in range(4))
    res = pl.pallas_call(
        body, name="adamw_small", out_shape=out_shape, in_specs=[vm] * (4 + len(flat)), out_specs=(vm,) * len(out_shape),
        compiler_params=_cparams(),
    )(rp, rws, rcv, gcw, *flat)
    return [res[4 * p:4 * p + 4] for p in range(n_p)]


def kernel(x, mix_norm_g, w_in, ret_norm_g, sgu_ln_g, sgu_ln_b, sgu_w_s, sgu_b_s, w_out, ffn_norm_g, w_up, conv_w, conv_b, w_down, final_norm_g, loss_target, m_mix_norm_g, m_w_in, m_ret_norm_g, m_sgu_ln_g, m_sgu_ln_b, m_sgu_w_s, m_sgu_b_s, m_w_out, m_ffn_norm_g, m_w_up, m_conv_w, m_conv_b, m_w_down, m_final_norm_g, v_mix_norm_g, v_w_in, v_ret_norm_g, v_sgu_ln_g, v_sgu_ln_b, v_sgu_w_s, v_sgu_b_s, v_w_out, v_ffn_norm_g, v_w_up, v_conv_w, v_conv_b, v_w_down, v_final_norm_g):
    xs = x[0]
    tgt = loss_target[0]
    cos2, sin2 = _rope_tables()
    mask, qdec, kdec = _decay_tables()
    grn = ret_norm_g.reshape(1, RET_W)
    lng = sgu_ln_g.reshape(1, SGU_W)
    lnb = sgu_ln_b.reshape(1, SGU_W)
    ws = sgu_w_s[0]
    bsb = jnp.broadcast_to(sgu_b_s[0][:, :, None], (HEADS, CHUNK, HEAD_DIM))
    gf = final_norm_g.reshape(1, D_MODEL)
    me = 4 * lax.axis_index("x") + 2 * lax.axis_index("y") + lax.axis_index("c")
    tr = lambda a: jnp.transpose(a[0])[None]

    win_g, cw_sh, so, su, sd = _ag_first(w_in[0], w_out[0], tr(w_up)[0], w_down[0], conv_w[0])
    cw_g = jnp.transpose(cw_sh, (1, 0, 2)).reshape(8, 2 * D_FF)

    proj, h1, wout_g, wdn_g = _fwd_proj(xs, mix_norm_g, win_g, cos2, sin2, so, sd)
    x2, mixcat, o, sprev, wup_g = _fwd_mix(xs, proj, wout_g, grn, lng, lnb, ws, bsb, mask, qdec, kdec, su)
    h2, up_pre, u_conv, act, x3, loss_parts = _fwd_ffn(x2, ffn_norm_g, wup_g, cw_g, conv_b, wdn_g, gf, tgt)

    dx3, dpre, dx2, dgf, dg2, dcv = _bwd_ffn(x3, tgt, gf, x2, ffn_norm_g, up_pre, u_conv, wup_g, cw_g, wdn_g)
    band = FF_SHARD // 2
    (gdn_p,) = _wgrad("wgrad_down", act, dx3, tm=FF_TILE)
    (gout_p,) = _wgrad("wgrad_out", mixcat, dx2, tn=512)
    gup_p, g_dn = _wgrad("wgrad_up", dpre, h2, tm=FF_TILE, hosted=[(W_DOWN, gdn_p)])
    dproj, dgrn, dlng, dlnb, dws, dbs, g_up_a, g_out = _bwd_mix(
        dx2, proj, o, sprev, wout_g, grn, lng, lnb, ws, bsb, mask, qdec, kdec, cos2, sin2,
        [((W_UP, 0, band), gup_p), (W_OUT, gout_p)])
    gin_p, g_up_b = _wgrad("wgrad_in", h1, dproj, tn=768, hosted=[((W_UP, band, band), gup_p)])
    grad_x, dg1, g_in = _bwd_proj(dproj, win_g, xs, mix_norm_g, dx2, gin_p)
    rp, rws, rcv = _all_reduce_small(dg1, dg2, dgf, dgrn, dlng, dlnb, dbs, loss_parts, dws, dcv)
    loss = rp[0, 3, RET_W]
    gcw = lax.dynamic_slice(rcv, (me // (N_DEV // 2), 0, (me % (N_DEV // 2)) * FF_SHARD), (1, 3, FF_SHARD))

    table = {}
    for name, w, gs, m, v, rows in (("w_in", w_in, [g_in], m_w_in, v_w_in, 256), ("w_out", w_out, [g_out], m_w_out, v_w_out, 128),
                                    ("w_up", tr(w_up), [g_up_a, g_up_b], tr(m_w_up), tr(v_w_up), 176),
                                    ("w_down", w_down, [g_dn], m_w_down, v_w_down, 88)):
        table[name] = _adamw("adamw_" + name, w, gs, m, v, rows)
    table["w_up"] = tuple(tr(a) for a in table["w_up"])
    row = lambda a: a.reshape(1, D_MODEL)
    names_small = ["mix_norm_g", "ffn_norm_g", "final_norm_g", "ret_norm_g", "sgu_ln_g", "sgu_ln_b", "sgu_b_s", "sgu_w_s",
                   "conv_w", "conv_b"]
    params = [(mix_norm_g, m_mix_norm_g, v_mix_norm_g), (ffn_norm_g, m_ffn_norm_g, v_ffn_norm_g),
              (row(final_norm_g), row(m_final_norm_g), row(v_final_norm_g)), (ret_norm_g, m_ret_norm_g, v_ret_norm_g),
              (sgu_ln_g, m_sgu_ln_g, v_sgu_ln_g), (sgu_ln_b, m_sgu_ln_b, v_sgu_ln_b), (sgu_b_s, m_sgu_b_s, v_sgu_b_s),
              (sgu_w_s, m_sgu_w_s, v_sgu_w_s), (conv_w, m_conv_w, v_conv_w), (conv_b, m_conv_b, v_conv_b)]
    for n, res in zip(names_small, _adamw_small(rp, rws, rcv, gcw, params)):
        table[n] = res
    table["final_norm_g"] = tuple(a.reshape(D_MODEL) for a in table["final_norm_g"])

    order = ["mix_norm_g", "w_in", "ret_norm_g", "sgu_ln_g", "sgu_ln_b", "sgu_w_s", "sgu_b_s", "w_out", "ffn_norm_g", "w_up",
             "conv_w", "conv_b", "w_down", "final_norm_g"]
    outs = [loss, grad_x[None]]
    for col in range(4):
        outs += [table[n][col] for n in order]
    return tuple(outs)
```

```python
import functools
import math

import jax
import jax.numpy as jnp
import numpy as np
from jax import lax
from jax.experimental import pallas as pl
from jax.experimental.pallas import tpu as pltpu

F32 = jnp.float32
BF16 = jnp.bfloat16
MESH = pl.DeviceIdType.MESH

N_DEV = 8
SEQ = 2048
D_MODEL = 1024
CHUNK = 128
N_CHUNK = SEQ // CHUNK
HEADS = 4
HEAD_DIM = 128
RET_W = 512
SGU_W = 512
PROJ_W = 3072
D_FF = 2816
FF_SHARD = 704
FF_TILE = 1408
FF_TILES = ((0, 1536), (1536, 1280))
IN_SHARD = PROJ_W // N_DEV
OUT_SHARD = D_MODEL // N_DEV
DOWN_SHARD = D_FF // N_DEV
TM = 256
N_TB = SEQ // TM
EPS = 1e-6
ROPE_BASE = 10000.0
K_SCALE = HEAD_DIM ** -0.5
INV_SQRT2 = 0.7071067811865476
INV_SQRT_2PI = 0.3989422804014327

ADAM_LR = 0.001
ADAM_B1 = 0.9
ADAM_B2 = 0.999
ADAM_EPS = 1e-08
ADAM_WD = 0.01
ADAM_STEP = 10

VMEM_LIMIT = 56 * 1024 * 1024


def _cparams(sem=None, vmem=VMEM_LIMIT):
    return pltpu.CompilerParams(dimension_semantics=sem, vmem_limit_bytes=vmem)


def _resident(shape):
    nd = len(shape)
    return pl.BlockSpec(shape, lambda *_: (0,) * nd, pipeline_mode=pl.Buffered(1))


def _dot(a, b):
    return jnp.dot(a, b, preferred_element_type=F32)


def _dot_nt(a, b):
    return lax.dot_general(a, b, (((1,), (1,)), ((), ())), preferred_element_type=F32)


def _dot_tn(a, b):
    return lax.dot_general(a, b, (((0,), (0,)), ((), ())), preferred_element_type=F32)


def _sigmoid(x):
    return 1.0 / (1.0 + jnp.exp(-x))


def _gelu(x):
    return 0.5 * x * (1.0 + lax.erf(x * INV_SQRT2))


def _gelu_grad(x):
    return 0.5 * (1.0 + lax.erf(x * INV_SQRT2)) + x * (jnp.exp(-0.5 * x * x) * INV_SQRT_2PI)


def _rot(xh, cos2, sin2):
    return xh * cos2 + pltpu.roll(xh, HEAD_DIM // 2, 1) * sin2


def _rot_t(dh, cos2, sin2):
    return dh * cos2 + pltpu.roll(dh * sin2, HEAD_DIM // 2, 1)


def _rope_tables():
    half = HEAD_DIM // 2
    inv_freq = jnp.power(ROPE_BASE, -jnp.arange(half, dtype=F32) / half)
    ang = jnp.arange(SEQ, dtype=F32)[:, None] * inv_freq[None, :]
    cos, sin = jnp.cos(ang), jnp.sin(ang)
    cos2 = jnp.concatenate([cos, cos], axis=-1)
    sin2 = jnp.concatenate([-sin, sin], axis=-1)
    return cos2, sin2


def _decay_tables():
    log_gamma = jnp.log(1.0 - jnp.power(2.0, -5.0 - jnp.arange(HEADS, dtype=F32)))
    pos = jnp.arange(CHUNK, dtype=F32)
    diff = pos[:, None] - pos[None, :]
    mask = jnp.where(diff >= 0.0, jnp.exp(log_gamma[:, None, None] * jnp.maximum(diff, 0.0)[None]), 0.0)
    k_decay = jnp.exp(log_gamma[:, None] * (CHUNK - 1.0 - pos)[None])
    q_decay = jnp.exp(log_gamma[:, None] * (pos + 1.0)[None])
    kd = jnp.broadcast_to(k_decay[:, :, None], (HEADS, CHUNK, HEAD_DIM))
    qd = jnp.broadcast_to(q_decay[:, :, None], (HEADS, CHUNK, HEAD_DIM))
    return mask.astype(F32), qd.astype(F32), kd.astype(F32)


def _chunk_decay():
    lg = np.log(np.float32(1.0) - np.power(np.float32(2.0), -5.0 - np.arange(HEADS, dtype=np.float32))).astype(np.float32)
    return [float(np.exp(lg[h] * np.float32(CHUNK))) for h in range(HEADS)]


W_IN, W_OUT, W_UP, W_DOWN, W_CONV = range(5)
GATHERED = {W_IN: ((D_MODEL, PROJ_W), BF16), W_OUT: ((D_MODEL, D_MODEL), BF16), W_UP: ((2 * D_FF, D_MODEL), BF16),
            W_DOWN: ((D_FF, D_MODEL), BF16), W_CONV: ((N_DEV, 8, FF_SHARD), F32)}
SHARD = {W_IN: (D_MODEL, IN_SHARD), W_OUT: (OUT_SHARD, D_MODEL), W_UP: (FF_SHARD, D_MODEL), W_DOWN: (DOWN_SHARD, D_MODEL),
         W_CONV: (8, FF_SHARD)}


class _Gather:
    N_SEMS = 9

    def __init__(self, ids, stages, gathered, send_sems, recv_sems, local_sems):
        self.ids, self.stages, self.gathered = ids, stages, gathered
        self.send_sems, self.recv_sems, self.local_sems = send_sems, recv_sems, local_sems
        self.x, self.y, self.c = lax.axis_index("x"), lax.axis_index("y"), lax.axis_index("c")
        self.me = (self.x, self.y, self.c)
        self.sibling = (self.x, self.y, 1 - self.c)
        self.chips = [(1 - self.x, self.y), (self.x, 1 - self.y), (1 - self.x, 1 - self.y)]

    def slot(self, n, px, py, pc):
        dev = 4 * px + 2 * py + pc
        w, g = self.ids[n], self.gathered[n]
        if w == W_IN:
            return g.at[:, pl.ds(pl.multiple_of(dev * IN_SHARD, 128), IN_SHARD)]
        if w == W_OUT:
            return g.at[pl.ds(pl.multiple_of(dev * OUT_SHARD, 128), OUT_SHARD), :]
        if w == W_DOWN:
            return g.at[pl.ds(pl.multiple_of(dev * DOWN_SHARD, 32), DOWN_SHARD), :]
        if w == W_UP:
            return g.at[pl.ds(pl.multiple_of(dev * FF_SHARD, 32), FF_SHARD), :]
        return g.at[dev]

    def half(self, n, px, py, pc, h):
        dev = 4 * px + 2 * py + pc
        w, g = self.ids[n], self.gathered[n]
        if w == W_IN:
            return g.at[pl.ds(h * (D_MODEL // 2), D_MODEL // 2), pl.ds(pl.multiple_of(dev * IN_SHARD, 128), IN_SHARD)]
        rows = SHARD[w][0] // 2
        return g.at[pl.ds(pl.multiple_of(dev * SHARD[w][0] + h * rows, 16), rows), :]

    def tree(self, n):
        return self.ids[n] != W_CONV

    def copy(self, n, k, block, to, src=None, h=None):
        ref = self.slot(n, *block) if h is None else self.half(n, *block, h)
        return pltpu.make_async_remote_copy(
            src_ref=ref if src is None else src, dst_ref=ref,
            send_sem=self.send_sems.at[n, k], recv_sem=self.recv_sems.at[n, k], device_id=to, device_id_type=MESH)

    def _mine(self):
        return [pltpu.make_async_copy(self.stages[n], self.slot(n, *self.me), self.local_sems.at[n]) for n in range(len(self.ids))]

    def _first(self):
        out = []
        for n in range(len(self.ids)):
            out.append(self.copy(n, 0, self.me, self.sibling, src=self.stages[n]))
            out += [self.copy(n, 1 + j, self.me, (*chip, self.c), src=self.stages[n])
                    for j, chip in enumerate(self.chips[:2] if self.tree(n) else self.chips)]
        return out

    def start(self):
        for cp in self._mine() + self._first():
            cp.start()

    def finish(self):
        self.forward_near()
        self.forward_far()
        self.wait_all()

    def _go(self, cp):
        cp.start()
        self.passed.append(cp)

    def forward_near(self):
        cx, cy, cd = [(*chip, self.c) for chip in self.chips]
        self.passed = []
        go = self._go
        for n in range(len(self.ids)):
            if self.tree(n):
                self.copy(n, 1, cx, self.me).wait_recv()
                go(self.copy(n, 3, cx, cy, h=0))
                go(self.copy(n, 5, cx, self.sibling))
                self.copy(n, 2, cy, self.me).wait_recv()
                go(self.copy(n, 4, cy, cx, h=1))
                go(self.copy(n, 6, cy, self.sibling))
            else:
                for j, dev in enumerate((cx, cy, cd)):
                    self.copy(n, 1 + j, dev, self.me).wait_recv()
                    go(self.copy(n, 4 + j, dev, self.sibling))

    def forward_far(self):
        cd = (*self.chips[2], self.c)
        go = self._go
        for n in range(len(self.ids)):
            if self.tree(n):
                self.copy(n, 3, cd, self.me, h=0).wait_recv()
                go(self.copy(n, 7, cd, self.sibling, h=0))
                self.copy(n, 4, cd, self.me, h=1).wait_recv()
                go(self.copy(n, 8, cd, self.sibling, h=1))

    def wait_all(self):
        ox, oy, od = [(*chip, 1 - self.c) for chip in self.chips]
        for n in range(len(self.ids)):
            self.copy(n, 0, self.sibling, self.me).wait_recv()
            if self.tree(n):
                self.copy(n, 5, ox, self.me).wait_recv()
                self.copy(n, 6, oy, self.me).wait_recv()
                self.copy(n, 7, od, self.me, h=0).wait_recv()
                self.copy(n, 8, od, self.me, h=1).wait_recv()
            else:
                for j, dev in enumerate((ox, oy, od)):
                    self.copy(n, 4 + j, dev, self.me).wait_recv()
        for cp in self._first() + self.passed:
            cp.wait_send()
        for cp in self._mine():
            cp.wait()


def _gather_scratch(n):
    return [pltpu.SemaphoreType.DMA((n, _Gather.N_SEMS)), pltpu.SemaphoreType.DMA((n, _Gather.N_SEMS)), pltpu.SemaphoreType.DMA((n,))]


def _gathered_shapes(ids):
    return tuple(jax.ShapeDtypeStruct(*GATHERED[w]) for w in ids)


def _fwd_proj(x, g1, cos2, sin2, w_in, w_out, w_up, w_down, conv_w):
    ids_a, ids_b = [W_IN, W_CONV], [W_OUT, W_DOWN]

    def body(x_ref, g_ref, cos_ref, sin_ref, in_hbm, out_hbm, up_hbm, dn_hbm, cw_ref,
             proj_ref, h1_ref, gin, gcw, gout, gdn, su_ref,
             w_vm, s_in, s_cw, s_out, s_dn, f_in, f_out, f_up, f_dn, ld_sems,
             a_send, a_recv, a_local, b_send, b_recv, b_local):
        ag_a = _Gather(ids_a, [s_in, s_cw], [gin, gcw], a_send, a_recv, a_local)
        ag_b = _Gather(ids_b, [s_out, s_dn], [gout, gdn], b_send, b_recv, b_local)

        @pl.when(pl.program_id(0) == 0)
        def _():
            loads = [pltpu.make_async_copy(src, dst, ld_sems.at[i])
                     for i, (src, dst) in enumerate(((in_hbm, f_in), (out_hbm, f_out), (dn_hbm, f_dn), (up_hbm, f_up)))]
            for cp in loads:
                cp.start()
            s_cw[...] = jnp.zeros_like(s_cw)
            s_cw[0:3, :] = cw_ref[...]
            loads[0].wait()
            s_in[...] = f_in[...].astype(BF16)
            ag_a.start()
            loads[1].wait()
            s_out[...] = f_out[...].astype(BF16)
            loads[2].wait()
            s_dn[...] = f_dn[...].astype(BF16)
            ag_a.forward_near()
            ag_b.start()
            loads[3].wait()
            su_ref[...] = f_up[...].astype(BF16)
            ag_a.forward_far()
            ag_a.wait_all()
            fill = pltpu.make_async_copy(gin, w_vm, ld_sems.at[4])
            fill.start()
            fill.wait()

        xb = x_ref[...]
        r = lax.rsqrt(jnp.mean(xb * xb, axis=-1, keepdims=True) + EPS)
        h = ((xb * r) * g_ref[...]).astype(BF16)
        h1_ref[...] = h
        p = _dot(h, w_vm[...])
        for hd in range(HEADS):
            sl = slice(hd * HEAD_DIM, (hd + 1) * HEAD_DIM)
            c2, s2 = cos_ref[...], sin_ref[...]
            proj_ref[:, sl] = _rot(p[:, sl], c2, s2)
            ks = slice(RET_W + hd * HEAD_DIM, RET_W + (hd + 1) * HEAD_DIM)
            proj_ref[:, ks] = _rot(p[:, ks], c2, s2) * K_SCALE
        proj_ref[:, 2 * RET_W:] = p[:, 2 * RET_W:]

        @pl.when(pl.program_id(0) == N_TB - 1)
        def _():
            ag_b.finish()

    tok = lambda w: pl.BlockSpec((TM, w), lambda i: (i, 0))
    hbm = pl.BlockSpec(memory_space=pl.ANY)
    vm = pl.BlockSpec(memory_space=pltpu.VMEM)
    return pl.pallas_call(
        body, name="fwd_proj", grid=(N_TB,),
        out_shape=(jax.ShapeDtypeStruct((SEQ, PROJ_W), F32), jax.ShapeDtypeStruct((SEQ, D_MODEL), BF16))
        + _gathered_shapes(ids_a + ids_b) + (jax.ShapeDtypeStruct(SHARD[W_UP], BF16),),
        in_specs=[tok(D_MODEL), _resident((1, D_MODEL)), tok(HEAD_DIM), tok(HEAD_DIM), hbm, hbm, hbm, hbm, vm],
        out_specs=(tok(PROJ_W), tok(D_MODEL), hbm, hbm, hbm, hbm, vm),
        scratch_shapes=[pltpu.VMEM((D_MODEL, PROJ_W), BF16), pltpu.VMEM(SHARD[W_IN], BF16), pltpu.VMEM(SHARD[W_CONV], F32),
                        pltpu.VMEM(SHARD[W_OUT], BF16), pltpu.VMEM(SHARD[W_DOWN], BF16),
                        pltpu.VMEM(SHARD[W_IN], F32), pltpu.VMEM(SHARD[W_OUT], F32), pltpu.VMEM(SHARD[W_UP], F32),
                        pltpu.VMEM(SHARD[W_DOWN], F32), pltpu.SemaphoreType.DMA((5,))]
        + _gather_scratch(len(ids_a)) + _gather_scratch(len(ids_b)),
        compiler_params=_cparams(("arbitrary",)),
    )(x, g1, cos2, sin2, w_in, w_out, w_up, w_down, conv_w)


def _causal(w):
    r = lax.broadcasted_iota(jnp.int32, (CHUNK, CHUNK), 0)
    c = lax.broadcasted_iota(jnp.int32, (CHUNK, CHUNK), 1)
    return jnp.where(r >= c, w, 0.0)


def _fwd_mix(x, proj, wout_g, grn, lng, lnb, ws, bsb, mask, qdec, kdec, su):
    cdec = _chunk_decay()
    ids = [W_UP]

    def body(x_ref, p_ref, w_ref, grn_ref, lng_ref, lnb_ref, ws_ref, bsb_ref, m_ref, qd_ref, kd_ref, su_ref,
             x2_ref, cat_ref, o_ref, sp_ref, gup, state, send_sems, recv_sems, local_sems):
        ag = _Gather(ids, [su_ref], [gup], send_sems, recv_sems, local_sems)

        @pl.when(pl.program_id(0) == 0)
        def _():
            state[...] = jnp.zeros_like(state)
            ag.start()

        for h in range(HEADS):
            sl = slice(h * HEAD_DIM, (h + 1) * HEAD_DIM)
            q = p_ref[:, sl]
            k = p_ref[:, RET_W + h * HEAD_DIM:RET_W + (h + 1) * HEAD_DIM]
            v = p_ref[:, 2 * RET_W + h * HEAD_DIM:2 * RET_W + (h + 1) * HEAD_DIM]
            g = p_ref[:, 3 * RET_W + h * HEAD_DIM:3 * RET_W + (h + 1) * HEAD_DIM]
            qb, kb, vb = q.astype(BF16), k.astype(BF16), v.astype(BF16)
            a = _dot_nt(qb, kb) * m_ref[h]
            spb = state[h].astype(BF16)
            sp_ref[0, h] = spb
            o = _dot(a.astype(BF16), vb) + _dot((q * qd_ref[h]).astype(BF16), spb)
            state[h] = state[h] * cdec[h] + _dot_tn((k * kd_ref[h]).astype(BF16), vb)
            o_ref[:, sl] = o
            rinv = lax.rsqrt(jnp.mean(o * o, axis=-1, keepdims=True) + EPS)
            rn = (o * rinv) * grn_ref[:, sl]
            cat_ref[:, sl] = ((g * _sigmoid(g)) * rn).astype(BF16)
        for gi in range(HEADS):
            sl = slice(gi * HEAD_DIM, (gi + 1) * HEAD_DIM)
            u = p_ref[:, 4 * RET_W + gi * HEAD_DIM:4 * RET_W + (gi + 1) * HEAD_DIM]
            sv = p_ref[:, 4 * RET_W + SGU_W + gi * HEAD_DIM:4 * RET_W + SGU_W + (gi + 1) * HEAD_DIM]
            gv = _gelu(sv)
            xc = gv - jnp.mean(gv, axis=-1, keepdims=True)
            vn = (xc * lax.rsqrt(jnp.mean(xc * xc, axis=-1, keepdims=True) + EPS)) * lng_ref[:, sl] + lnb_ref[:, sl]
            mixed = _dot(_causal(ws_ref[gi]).astype(BF16), vn.astype(BF16)) + bsb_ref[gi]
            cat_ref[:, RET_W + gi * HEAD_DIM:RET_W + (gi + 1) * HEAD_DIM] = (_gelu(u) * mixed).astype(BF16)
        x2_ref[...] = x_ref[...] + _dot(cat_ref[...], w_ref[...])

        @pl.when(pl.program_id(0) == N_CHUNK - 1)
        def _():
            ag.finish()

    ch = lambda w: pl.BlockSpec((CHUNK, w), lambda i: (i, 0))
    hcc = (HEADS, CHUNK, CHUNK)
    hbm = pl.BlockSpec(memory_space=pl.ANY)
    return pl.pallas_call(
        body, name="fwd_mix", grid=(N_CHUNK,),
        out_shape=(jax.ShapeDtypeStruct((SEQ, D_MODEL), F32), jax.ShapeDtypeStruct((SEQ, D_MODEL), BF16),
                   jax.ShapeDtypeStruct((SEQ, RET_W), F32), jax.ShapeDtypeStruct((N_CHUNK, HEADS, HEAD_DIM, HEAD_DIM), BF16))
        + _gathered_shapes(ids),
        in_specs=[ch(D_MODEL), ch(PROJ_W), _resident((D_MODEL, D_MODEL)), _resident((1, RET_W)), _resident((1, SGU_W)),
                  _resident((1, SGU_W)), _resident(hcc), _resident(hcc), _resident(hcc), _resident(hcc), _resident(hcc), hbm],
        out_specs=(ch(D_MODEL), ch(D_MODEL), ch(RET_W), pl.BlockSpec((1, HEADS, HEAD_DIM, HEAD_DIM), lambda i: (i, 0, 0, 0)), hbm),
        scratch_shapes=[pltpu.VMEM((HEADS, HEAD_DIM, HEAD_DIM), F32)] + _gather_scratch(len(ids)),
        compiler_params=_cparams(("arbitrary",)),
    )(x, proj, wout_g, grn, lng, lnb, ws, bsb, mask, qdec, kdec, su)


def _conv_taps(p, prev8):
    row = lax.broadcasted_iota(jnp.int32, p.shape, 0)
    p1 = jnp.where(row == 0, prev8[7:8, :], pltpu.roll(p, 1, 0))
    p2 = jnp.where(row == 0, prev8[6:7, :], jnp.where(row == 1, prev8[7:8, :], pltpu.roll(p, 2, 0)))
    return p1, p2


def _fwd_ffn(x2, g2, wup_g, cw_g, cb_g, wdn_g, gf, tgt):
    def body(x_ref, g_ref, wu_ref, cw_ref, cb_ref, wd_ref, gf_ref, t_ref, h2_ref, up_ref, u_ref, act_ref, x3_ref, loss_ref, carry):
        @pl.when(pl.program_id(0) == 0)
        def _():
            carry[...] = jnp.zeros_like(carry)

        xb = x_ref[...]
        r = lax.rsqrt(jnp.mean(xb * xb, axis=-1, keepdims=True) + EPS)
        h = ((xb * r) * g_ref[...]).astype(BF16)
        h2_ref[...] = h
        acc = xb
        for t0, tw in FF_TILES:
            u = []
            for c0 in (t0, D_FF + t0):
                cs = slice(c0, c0 + tw)
                p = _dot_nt(h, wu_ref[pl.ds(c0, tw), :])
                up_ref[:, cs] = p.astype(BF16)
                p1, p2 = _conv_taps(p, carry[:, cs])
                carry[:, cs] = p[TM - 8:, :]
                us = p2 * cw_ref[0:1, cs] + p1 * cw_ref[1:2, cs] + p * cw_ref[2:3, cs] + cb_ref[:, cs]
                u_ref[:, cs] = us.astype(BF16)
                u.append(us)
            a = ((u[0] * _sigmoid(u[0])) * u[1]).astype(BF16)
            act_ref[:, t0:t0 + tw] = a
            acc = acc + _dot(a, wd_ref[pl.ds(t0, tw), :])
        x3_ref[...] = acc
        r3 = lax.rsqrt(jnp.mean(acc * acc, axis=-1, keepdims=True) + EPS)
        diff = (acc * r3) * gf_ref[...] - t_ref[...]
        loss_ref[...] = jnp.full(loss_ref.shape, 0.5 * jnp.sum(jnp.mean(diff * diff, axis=-1)), F32)

    tok = lambda w: pl.BlockSpec((TM, w), lambda i: (i, 0))
    return pl.pallas_call(
        body, name="fwd_ffn", grid=(N_TB,),
        out_shape=(jax.ShapeDtypeStruct((SEQ, D_MODEL), BF16), jax.ShapeDtypeStruct((SEQ, 2 * D_FF), BF16),
                   jax.ShapeDtypeStruct((SEQ, 2 * D_FF), BF16),
                   jax.ShapeDtypeStruct((SEQ, D_FF), BF16), jax.ShapeDtypeStruct((SEQ, D_MODEL), F32),
                   jax.ShapeDtypeStruct((N_TB, 8, 128), F32)),
        in_specs=[tok(D_MODEL), _resident((1, D_MODEL)), _resident((2 * D_FF, D_MODEL)), _resident((8, 2 * D_FF)),
                  _resident((1, 2 * D_FF)), _resident((D_FF, D_MODEL)), _resident((1, D_MODEL)), tok(D_MODEL)],
        out_specs=(tok(D_MODEL), tok(2 * D_FF), tok(2 * D_FF), tok(D_FF), tok(D_MODEL),
                   pl.BlockSpec((1, 8, 128), lambda i: (i, 0, 0))),
        scratch_shapes=[pltpu.VMEM((8, 2 * D_FF), F32)],
        compiler_params=_cparams(("arbitrary",)),
    )(x2, g2, wup_g, cw_g, cb_g, wdn_g, gf, tgt)


def _bwd_ffn(x3, tgt, gf, x2, g2, up_pre, u_conv, wup_g, cw_g, wdn_g):
    def body(x3_ref, t_ref, gf_ref, x2_ref, g2_ref, up_ref, u_ref, wu_ref, cw_ref, wd_ref,
             dx3_ref, dpre_ref, dx2_ref, dgf_ref, dg2_ref, dcv_ref, nxt):
        i = pl.program_id(0)

        @pl.when(i == 0)
        def _():
            nxt[...] = jnp.zeros_like(nxt)
            dgf_ref[...] = jnp.zeros_like(dgf_ref)
            dg2_ref[...] = jnp.zeros_like(dg2_ref)
            dcv_ref[...] = jnp.zeros_like(dcv_ref)

        x3 = x3_ref[...]
        r3 = lax.rsqrt(jnp.mean(x3 * x3, axis=-1, keepdims=True) + EPS)
        xh3 = x3 * r3
        dy = (xh3 * gf_ref[...] - t_ref[...]) * (1.0 / D_MODEL)
        dgf_ref[0:1, :] += jnp.sum(dy * xh3, axis=0, keepdims=True)
        t3 = dy * gf_ref[...]
        dx3 = r3 * (t3 - xh3 * jnp.mean(t3 * xh3, axis=-1, keepdims=True))
        dx3b = dx3.astype(BF16)
        dx3_ref[...] = dx3b
        dh2 = jnp.zeros((TM, D_MODEL), F32)
        for t0, tw in FF_TILES:
            row = lax.broadcasted_iota(jnp.int32, (TM, tw), 0)
            ts = slice(t0, t0 + tw)
            dact = _dot_nt(dx3b, wd_ref[pl.ds(t0, tw), :])
            ua = u_ref[:, ts].astype(F32)
            ub = u_ref[:, D_FF + t0:D_FF + t0 + tw].astype(F32)
            sg = _sigmoid(ua)
            du = [dact * ub * (sg * (1.0 + ua * (1.0 - sg))), dact * (ua * sg)]
            for n in range(2):
                d = du[n]
                c0 = n * D_FF + t0
                cs = slice(c0, c0 + tw)
                nx = nxt[:, cs]
                n1 = jnp.where(row == TM - 1, nx[0:1, :], pltpu.roll(d, TM - 1, 0))
                n2 = jnp.where(row == TM - 2, nx[0:1, :], jnp.where(row == TM - 1, nx[1:2, :], pltpu.roll(d, TM - 2, 0)))
                nxt[:, cs] = d[0:8, :]
                dp = (d * cw_ref[2:3, cs] + n1 * cw_ref[1:2, cs] + n2 * cw_ref[0:1, cs]).astype(BF16)
                dpre_ref[:, cs] = dp
                p = up_ref[:, cs].astype(F32)
                dcv_ref[n, 0:1, ts] += jnp.sum(n2 * p, axis=0, keepdims=True)
                dcv_ref[n, 1:2, ts] += jnp.sum(n1 * p, axis=0, keepdims=True)
                dcv_ref[n, 2:3, ts] += jnp.sum(d * p, axis=0, keepdims=True)
                dcv_ref[n, 3:4, ts] += jnp.sum(d, axis=0, keepdims=True)
                dh2 = dh2 + _dot(dp, wu_ref[pl.ds(c0, tw), :])
        x2 = x2_ref[...]
        r2 = lax.rsqrt(jnp.mean(x2 * x2, axis=-1, keepdims=True) + EPS)
        xh2 = x2 * r2
        dg2_ref[0:1, :] += jnp.sum(dh2 * xh2, axis=0, keepdims=True)
        t2 = dh2 * g2_ref[...]
        dx2_ref[...] = dx3 + r2 * (t2 - xh2 * jnp.mean(t2 * xh2, axis=-1, keepdims=True))

    rev = lambda w: pl.BlockSpec((TM, w), lambda i: (N_TB - 1 - i, 0))
    acc = lambda s: pl.BlockSpec(s, lambda i: (0,) * len(s))
    return pl.pallas_call(
        body, name="bwd_ffn", grid=(N_TB,),
        out_shape=(jax.ShapeDtypeStruct((SEQ, D_MODEL), BF16), jax.ShapeDtypeStruct((SEQ, 2 * D_FF), BF16),
                   jax.ShapeDtypeStruct((SEQ, D_MODEL), F32), jax.ShapeDtypeStruct((8, D_MODEL), F32),
                   jax.ShapeDtypeStruct((8, D_MODEL), F32), jax.ShapeDtypeStruct((2, 8, D_FF), F32)),
        in_specs=[rev(D_MODEL), rev(D_MODEL), _resident((1, D_MODEL)), rev(D_MODEL), _resident((1, D_MODEL)), rev(2 * D_FF),
                  rev(2 * D_FF), _resident((2 * D_FF, D_MODEL)), _resident((8, 2 * D_FF)), _resident((D_FF, D_MODEL))],
        out_specs=(rev(D_MODEL), rev(2 * D_FF), rev(D_MODEL), acc((8, D_MODEL)), acc((8, D_MODEL)), acc((2, 8, D_FF))),
        scratch_shapes=[pltpu.VMEM((8, 2 * D_FF), F32)],
        compiler_params=_cparams(("arbitrary",)),
    )(x3, tgt, gf, x2, g2, up_pre, u_conv, wup_g, cw_g, wdn_g)


def _bwd_mix(dx2, proj, o, sprev, wout_g, grn, lng, lnb, ws, bsb, mask, qdec, kdec, cos2, sin2, hosted):
    cdec = _chunk_decay()
    geoms = [g for g, _ in hosted]
    n_h = len(hosted)

    def body(dx2_ref, p_ref, o_ref, sp_ref, w_ref, grn_ref, lng_ref, lnb_ref, ws_ref, bsb_ref, m_ref, qd_ref, kd_ref,
             cos_ref, sin_ref, *rest):
        dp_ref, dgrn_ref, dlng_ref, dlnb_ref, dws_ref, dbs_ref = rest[n_h:n_h + 6]
        dstate, dbs_acc = rest[2 * n_h + 6:2 * n_h + 8]
        i = pl.program_id(0)
        rs = _Scatters(geoms, rest[:n_h], rest[n_h + 6:2 * n_h + 6], rest[2 * n_h + 8:])
        pl.when(i == 0)(rs.phase1)
        pl.when(i == 3)(rs.phase2)
        pl.when(i == 6)(rs.phase2b)

        @pl.when(i == 0)
        def _():
            dstate[...] = jnp.zeros_like(dstate)
            dgrn_ref[...] = jnp.zeros_like(dgrn_ref)
            dlng_ref[...] = jnp.zeros_like(dlng_ref)
            dlnb_ref[...] = jnp.zeros_like(dlnb_ref)
            dws_ref[...] = jnp.zeros_like(dws_ref)
            dbs_ref[...] = jnp.zeros_like(dbs_ref)
            dbs_acc[...] = jnp.zeros_like(dbs_acc)

        dmix = _dot_nt(dx2_ref[...].astype(BF16), w_ref[...])
        for h in range(HEADS):
            sl = slice(h * HEAD_DIM, (h + 1) * HEAD_DIM)
            q = p_ref[:, sl]
            k = p_ref[:, RET_W + h * HEAD_DIM:RET_W + (h + 1) * HEAD_DIM]
            v = p_ref[:, 2 * RET_W + h * HEAD_DIM:2 * RET_W + (h + 1) * HEAD_DIM]
            g = p_ref[:, 3 * RET_W + h * HEAD_DIM:3 * RET_W + (h + 1) * HEAD_DIM]
            o = o_ref[:, sl]
            rinv = lax.rsqrt(jnp.mean(o * o, axis=-1, keepdims=True) + EPS)
            oh = o * rinv
            gr = grn_ref[:, sl]
            sg = _sigmoid(g)
            dret = dmix[:, sl]
            dp_ref[:, 3 * RET_W + h * HEAD_DIM:3 * RET_W + (h + 1) * HEAD_DIM] = (
                dret * (oh * gr) * (sg * (1.0 + g * (1.0 - sg)))).astype(BF16)
            drn = dret * (g * sg)
            dgrn_ref[0:1, sl] += jnp.sum(drn * oh, axis=0, keepdims=True)
            t = drn * gr
            do = rinv * (t - oh * jnp.mean(t * oh, axis=-1, keepdims=True))
            qb, kb, vb, dob = q.astype(BF16), k.astype(BF16), v.astype(BF16), do.astype(BF16)
            m = m_ref[h]
            ab = (_dot_nt(qb, kb) * m).astype(BF16)
            dab = (_dot_nt(dob, vb) * m).astype(BF16)
            spb = sp_ref[0, h]
            dsn = dstate[h]
            dsnb = dsn.astype(BF16)
            qdb = (q * qd_ref[h]).astype(BF16)
            kdb = (k * kd_ref[h]).astype(BF16)
            dq = _dot(dab, kb) + _dot_nt(dob, spb) * qd_ref[h]
            dk = _dot_tn(dab, qb) + _dot_nt(vb, dsnb) * kd_ref[h]
            dv = _dot_tn(ab, dob) + _dot(kdb, dsnb)
            dstate[h] = dsn * cdec[h] + _dot_tn(qdb, dob)
            c2, s2 = cos_ref[...], sin_ref[...]
            dp_ref[:, sl] = _rot_t(dq, c2, s2).astype(BF16)
            dp_ref[:, RET_W + h * HEAD_DIM:RET_W + (h + 1) * HEAD_DIM] = _rot_t(dk * K_SCALE, c2, s2).astype(BF16)
            dp_ref[:, 2 * RET_W + h * HEAD_DIM:2 * RET_W + (h + 1) * HEAD_DIM] = dv.astype(BF16)
        for gi in range(HEADS):
            sl = slice(gi * HEAD_DIM, (gi + 1) * HEAD_DIM)
            u = p_ref[:, 4 * RET_W + gi * HEAD_DIM:4 * RET_W + (gi + 1) * HEAD_DIM]
            sv = p_ref[:, 4 * RET_W + SGU_W + gi * HEAD_DIM:4 * RET_W + SGU_W + (gi + 1) * HEAD_DIM]
            gv = _gelu(sv)
            xc = gv - jnp.mean(gv, axis=-1, keepdims=True)
            rstd = lax.rsqrt(jnp.mean(xc * xc, axis=-1, keepdims=True) + EPS)
            xh = xc * rstd
            lg = lng_ref[:, sl]
            vnb = (xh * lg + lnb_ref[:, sl]).astype(BF16)
            wcb = _causal(ws_ref[gi]).astype(BF16)
            mixed = _dot(wcb, vnb) + bsb_ref[gi]
            dsgu = dmix[:, RET_W + gi * HEAD_DIM:RET_W + (gi + 1) * HEAD_DIM]
            dmixed = dsgu * _gelu(u)
            dmb = dmixed.astype(BF16)
            dws_ref[gi] += _causal(_dot_nt(dmb, vnb))
            dbs_acc[gi] += dmixed
            dvn = _dot_tn(wcb, dmb)
            dlng_ref[gi:gi + 1, :] += jnp.sum(dvn * xh, axis=0, keepdims=True)
            dlnb_ref[gi:gi + 1, :] += jnp.sum(dvn, axis=0, keepdims=True)
            dxh = dvn * lg
            dgv = rstd * (dxh - jnp.mean(dxh, axis=-1, keepdims=True) - xh * jnp.mean(dxh * xh, axis=-1, keepdims=True))
            dp_ref[:, 4 * RET_W + gi * HEAD_DIM:4 * RET_W + (gi + 1) * HEAD_DIM] = (dsgu * mixed * _gelu_grad(u)).astype(BF16)
            dp_ref[:, 4 * RET_W + SGU_W + gi * HEAD_DIM:4 * RET_W + SGU_W + (gi + 1) * HEAD_DIM] = (
                dgv * _gelu_grad(sv)).astype(BF16)

        @pl.when(i == N_CHUNK - 1)
        def _():
            for gi in range(HEADS):
                col = jnp.broadcast_to(jnp.sum(dbs_acc[gi], axis=-1, keepdims=True), (CHUNK, CHUNK))
                dbs_ref[gi:gi + 1, :] = jnp.transpose(col)[0:1, :]
            rs.phase3()

    rev = lambda w: pl.BlockSpec((CHUNK, w), lambda i: (N_CHUNK - 1 - i, 0))
    hcc = (HEADS, CHUNK, CHUNK)
    acc = lambda s: pl.BlockSpec(s, lambda i: (0,) * len(s))
    res = pl.pallas_call(
        body, name="bwd_mix", grid=(N_CHUNK,),
        out_shape=(jax.ShapeDtypeStruct((SEQ, PROJ_W), BF16), jax.ShapeDtypeStruct((8, RET_W), F32),
                   jax.ShapeDtypeStruct((8, HEAD_DIM), F32), jax.ShapeDtypeStruct((8, HEAD_DIM), F32),
                   jax.ShapeDtypeStruct(hcc, F32), jax.ShapeDtypeStruct((8, CHUNK), F32)) + _scatter_out_shapes(geoms),
        in_specs=[rev(D_MODEL), rev(PROJ_W), rev(RET_W),
                  pl.BlockSpec((1, HEADS, HEAD_DIM, HEAD_DIM), lambda i: (N_CHUNK - 1 - i, 0, 0, 0)),
                  _resident((D_MODEL, D_MODEL)), _resident((1, RET_W)), _resident((1, SGU_W)), _resident((1, SGU_W)),
                  _resident(hcc), _resident(hcc), _resident(hcc), _resident(hcc), _resident(hcc), rev(HEAD_DIM), rev(HEAD_DIM)]
        + [pl.BlockSpec(memory_space=pl.ANY)] * n_h,
        out_specs=(rev(PROJ_W), acc((8, RET_W)), acc((8, HEAD_DIM)), acc((8, HEAD_DIM)), acc(hcc), acc((8, CHUNK)))
        + _scatter_out_specs(geoms),
        scratch_shapes=[pltpu.VMEM((HEADS, HEAD_DIM, HEAD_DIM), F32), pltpu.VMEM((HEADS, CHUNK, CHUNK), F32)] + _scatter_scratch(geoms),
        compiler_params=_cparams(("arbitrary",)),
    )(dx2, proj, o, sprev, wout_g, grn, lng, lnb, ws, bsb, mask, qdec, kdec, cos2, sin2, *[p for _, p in hosted])
    return tuple(res[:6 + n_h])


def _bwd_proj(dproj, win_g, x, g1, dx2, gin_p):
    geoms = [W_IN]

    def body(dp_ref, w_ref, x_ref, g_ref, dx2_ref, gin_ref, dx_ref, dg_ref, rs_out, *rs_scratch):
        rs = _Scatters(geoms, [gin_ref], [rs_out], rs_scratch)
        pl.when(pl.program_id(0) == 0)(rs.phase1)
        pl.when(pl.program_id(0) == 2)(rs.phase2)
        pl.when(pl.program_id(0) == 4)(rs.phase2b)

        @pl.when(pl.program_id(0) == 0)
        def _():
            dg_ref[...] = jnp.zeros_like(dg_ref)

        dh = _dot_nt(dp_ref[...], w_ref[...])
        xb = x_ref[...]
        r = lax.rsqrt(jnp.mean(xb * xb, axis=-1, keepdims=True) + EPS)
        xh = xb * r
        dg_ref[0:1, :] += jnp.sum(dh * xh, axis=0, keepdims=True)
        t = dh * g_ref[...]
        dx_ref[...] = dx2_ref[...] + r * (t - xh * jnp.mean(t * xh, axis=-1, keepdims=True))
        pl.when(pl.program_id(0) == N_TB - 1)(rs.phase3)

    tok = lambda w: pl.BlockSpec((TM, w), lambda i: (i, 0))
    res = pl.pallas_call(
        body, name="bwd_proj", grid=(N_TB,),
        out_shape=(jax.ShapeDtypeStruct((SEQ, D_MODEL), F32), jax.ShapeDtypeStruct((8, D_MODEL), F32)) + _scatter_out_shapes(geoms),
        in_specs=[tok(PROJ_W), _resident((D_MODEL, PROJ_W)), tok(D_MODEL), _resident((1, D_MODEL)), tok(D_MODEL),
                  pl.BlockSpec(memory_space=pl.ANY)],
        out_specs=(tok(D_MODEL), pl.BlockSpec((8, D_MODEL), lambda i: (0, 0))) + _scatter_out_specs(geoms),
        scratch_shapes=_scatter_scratch(geoms),
        compiler_params=_cparams(("arbitrary",)),
    )(dproj, win_g, x, g1, dx2, gin_p)
    return res[:3]


def _wgrad(name, a, b, tm=None, tn=None, hosted=()):
    m_w, n_w = a.shape[-1], b.shape[-1]
    tm = m_w if tm is None else tm
    tn = n_w if tn is None else tn
    n_steps = (m_w // tm) * (n_w // tn)
    geoms = [g for g, _ in hosted]
    n_h = len(hosted)

    def body(a_ref, b_ref, *rest):
        o_ref = rest[n_h]
        if n_h:
            rs = _Scatters(geoms, rest[:n_h], rest[n_h + 1:2 * n_h + 1], rest[2 * n_h + 1:])
            step = pl.program_id(0) * (n_w // tn) + pl.program_id(1)
            pl.when(step == 0)(rs.phase1)
            pl.when(step == 1)(rs.phase2)
            pl.when(step == 2)(rs.phase2b)
        o_ref[...] = _dot_tn(a_ref[...].astype(BF16), b_ref[...].astype(BF16)).astype(BF16)
        if n_h:
            pl.when(step == n_steps - 1)(rs.phase3)

    assert not n_h or n_steps >= 4
    res = pl.pallas_call(
        body, name=name, grid=(m_w // tm, n_w // tn),
        out_shape=(jax.ShapeDtypeStruct((m_w, n_w), BF16),) + _scatter_out_shapes(geoms),
        in_specs=[pl.BlockSpec((SEQ, tm), lambda i, j: (0, i)), pl.BlockSpec((SEQ, tn), lambda i, j: (0, j))]
        + [pl.BlockSpec(memory_space=pl.ANY)] * n_h,
        out_specs=(pl.BlockSpec((tm, tn), lambda i, j: (i, j)),) + _scatter_out_specs(geoms),
        scratch_shapes=_scatter_scratch(geoms),
        compiler_params=_cparams(("arbitrary", "arbitrary") if n_h else ("parallel", "parallel")),
    )(a, b, *[p for _, p in hosted])
    return tuple(res[:1 + n_h])


RS_ROWS = {W_IN: 128, W_OUT: 64, W_UP: 176, W_DOWN: 176}


class _Scatter:
    def __init__(self, geom, partial, out, land1, mine, stage2, land2, comb, s1_send, s1_recv, s2_send, s2_recv, ld_sems):
        self.w, self.row0, self.shape = _geom(geom)
        self.partial, self.out, self.land1 = partial, out, land1
        self.mine, self.stage2, self.land2, self.comb = mine, stage2, land2, comb
        self.hr = self.shape[0] // 2
        self.s1_send, self.s1_recv, self.s2_send, self.s2_recv, self.ld_sems = s1_send, s1_recv, s2_send, s2_recv, ld_sems
        self.x, self.y, self.c = lax.axis_index("x"), lax.axis_index("y"), lax.axis_index("c")
        self.sibling = (self.x, self.y, 1 - self.c)
        self.chips = [(self.x, self.y), (1 - self.x, self.y), (self.x, 1 - self.y), (1 - self.x, 1 - self.y)]

    def block(self, px, py, pc):
        dev = 4 * px + 2 * py + pc
        if self.w == W_IN:
            return self.partial.at[:, pl.ds(pl.multiple_of(dev * IN_SHARD, 128), IN_SHARD)]
        if self.w == W_OUT:
            return self.partial.at[pl.ds(pl.multiple_of(dev * OUT_SHARD, 128), OUT_SHARD), :]
        if self.w == W_DOWN:
            return self.partial.at[pl.ds(pl.multiple_of(dev * DOWN_SHARD, 32), DOWN_SHARD), :]
        return self.partial.at[pl.ds(pl.multiple_of(dev * FF_SHARD + self.row0, 32), self.shape[0]), :]

    def copy1(self, k):
        return pltpu.make_async_remote_copy(
            src_ref=self.block(*self.chips[k], 1 - self.c), dst_ref=self.land1.at[k],
            send_sem=self.s1_send.at[k], recv_sem=self.s1_recv.at[k], device_id=self.sibling, device_id_type=MESH)

    STAGE2 = [(1, 0, 1), (3, 0, 1), (2, 1, 2), (3, 1, 2), (1, 1, 1), (2, 0, 2)]

    def copy2(self, j):
        blk, h, to = self.STAGE2[j]
        src = self.comb.at[j - 4] if j >= 4 else self.stage2.at[blk - 1, pl.ds(h * self.hr, self.hr), :]
        return pltpu.make_async_remote_copy(
            src_ref=src, dst_ref=self.land2.at[j], send_sem=self.s2_send.at[j], recv_sem=self.s2_recv.at[j],
            device_id=(*self.chips[to], self.c), device_id_type=MESH)

    def _rows(self, h=None):
        step = RS_ROWS[self.w]
        lo, n = (0, self.shape[0]) if h is None else (h * self.hr, self.hr)
        return [pl.ds(r0, step) for r0 in range(lo, lo + n, step)]

    def load(self, k):
        return pltpu.make_async_copy(self.block(*self.chips[k], self.c), self.mine.at[k], self.ld_sems.at[k])

    def phase1(self):
        for k in range(4):
            self.copy1(k).start()
        for k in range(4):
            self.load(k).start()

    def phase2(self):
        for k in (3, 1, 2, 0):
            self.copy1(k).wait_recv()
            self.load(k).wait()
            for rs in self._rows():
                s = self.mine[k, rs, :].astype(F32) + self.land1[k, rs, :].astype(F32)
                if k == 0:
                    self.out[rs, :] = s
                else:
                    self.stage2[k - 1, rs, :] = s.astype(BF16)
            for j in {3: (1, 3), 1: (0,), 2: (2,), 0: ()}[k]:
                self.copy2(j).start()

    def phase2b(self):
        for j, got in ((4, 3), (5, 1)):
            blk, h, _ = self.STAGE2[j]
            self.copy2(got).wait_recv()
            for i, rs in enumerate(self._rows(h)):
                lr = pl.ds(i * RS_ROWS[self.w], RS_ROWS[self.w])
                self.comb[j - 4, lr, :] = (self.stage2[blk - 1, rs, :].astype(F32) + self.land2[got, lr, :].astype(F32)).astype(BF16)
            self.copy2(j).start()

    def phase3(self):
        for j in (0, 5, 4, 2):
            self.copy2(j).wait_recv()
        for h, (first, second) in enumerate(((0, 5), (4, 2))):
            for i, rs in enumerate(self._rows(h)):
                lr = pl.ds(i * RS_ROWS[self.w], RS_ROWS[self.w])
                self.out[rs, :] = (self.out[rs, :] + self.land2[first, lr, :].astype(F32)) + self.land2[second, lr, :].astype(F32)
        for k in range(4):
            self.copy1(k).wait_send()
        for j in range(6):
            self.copy2(j).wait_send()


def _geom(geom):
    if isinstance(geom, tuple):
        w, row0, rows = geom
        assert w == W_UP
        return w, row0, (rows, SHARD[w][1])
    return geom, 0, SHARD[geom]


N_SCATTER_SCRATCH = 10


def _scatter_out_shapes(geoms):
    return tuple(jax.ShapeDtypeStruct(_geom(g)[2], F32) for g in geoms)


def _scatter_out_specs(geoms):
    return (pl.BlockSpec(memory_space=pltpu.VMEM),) * len(geoms)


def _scatter_scratch(geoms):
    out = []
    for g in geoms:
        s = _geom(g)[2]
        hs = (s[0] // 2, s[1])
        out += [pltpu.VMEM((4,) + s, BF16), pltpu.VMEM((4,) + s, BF16), pltpu.VMEM((3,) + s, BF16), pltpu.VMEM((6,) + hs, BF16),
                pltpu.VMEM((2,) + hs, BF16),
                pltpu.SemaphoreType.DMA((4,)), pltpu.SemaphoreType.DMA((4,)), pltpu.SemaphoreType.DMA((6,)),
                pltpu.SemaphoreType.DMA((6,)), pltpu.SemaphoreType.DMA((4,))]
    return out


class _Scatters:
    def __init__(self, geoms, p_refs, out_refs, scratch):
        k = N_SCATTER_SCRATCH
        self.items = [_Scatter(g, p_refs[i], out_refs[i], *scratch[k * i:k * i + k]) for i, g in enumerate(geoms)]

    def phase1(self):
        for s in self.items:
            s.phase1()

    def phase2(self):
        for s in self.items:
            s.phase2()

    def phase2b(self):
        for s in self.items:
            s.phase2b()

    def phase3(self):
        for s in self.items:
            s.phase3()


PACK_W = 1024


def _all_reduce_small(dg1, dg2, dgf, dgrn, dlng, dlnb, dbs, loss_parts, dws, dcv):
    n_a = 3

    def body(dg1_ref, dg2_ref, dgf_ref, dgrn_ref, dlng_ref, dlnb_ref, dbs_ref, loss_ref, dws_ref, dcv_ref,
             rp_ref, rws_ref, rcv_ref, pack, rx_p, rx_ws, rx_cv, cs_p, cs_ws, cs_cv, g_p, g_ws, g_cv,
             s1_send, s1_recv, s2_send, s2_recv, s3_send, s3_recv):
        x, y, c = lax.axis_index("x"), lax.axis_index("y"), lax.axis_index("c")
        sibling = (x, y, 1 - c)
        chips = [(1 - x, y), (x, 1 - y), (1 - x, 1 - y)]
        pack[...] = jnp.zeros_like(pack)
        pack[0, 0:1, :] = dg1_ref[0:1, :]
        pack[0, 1:2, :] = dg2_ref[0:1, :]
        pack[0, 2:3, :] = dgf_ref[0:1, :]
        pack[0, 3:4, 0:RET_W] = dgrn_ref[0:1, :]
        lsum = loss_ref[0, 0:1, :]
        for i in range(1, N_TB):
            lsum = lsum + loss_ref[i, 0:1, :]
        pack[0, 3:4, RET_W:RET_W + 128] = lsum
        pack[1, 0:HEADS, 0:128] = dlng_ref[0:HEADS, :]
        pack[1, 0:HEADS, 128:256] = dlnb_ref[0:HEADS, :]
        pack[1, 0:HEADS, 256:384] = dbs_ref[0:HEADS, :]

        srcs = [pack, dws_ref, dcv_ref]
        outs = [rp_ref, rws_ref, rcv_ref]
        rxs = [rx_p, rx_ws, rx_cv]
        css = [cs_p, cs_ws, cs_cv]
        gs = [g_p, g_ws, g_cv]
        hl = [1, HEADS // 2, 1]

        def half(ref, a, h):
            return ref.at[pl.ds(h * hl[a], hl[a])]

        ex1 = [pltpu.make_async_remote_copy(src_ref=half(srcs[a], a, 1 - c), dst_ref=rxs[a], send_sem=s1_send.at[a],
                                            recv_sem=s1_recv.at[a], device_id=sibling, device_id_type=MESH) for a in range(n_a)]
        for cp in ex1:
            cp.start()
        ex2 = []
        for a in range(n_a):
            ex1[a].wait_recv()
            css[a][...] = half(srcs[a], a, c)[...] + rxs[a][...]
            for j, chip in enumerate(chips):
                cp = pltpu.make_async_remote_copy(src_ref=css[a], dst_ref=gs[a].at[j], send_sem=s2_send.at[a, j],
                                                  recv_sem=s2_recv.at[a, j], device_id=(*chip, c), device_id_type=MESH)
                cp.start()
                ex2.append(cp)
        ex3 = []
        for a in range(n_a):
            for j in range(3):
                ex2[3 * a + j].wait_recv()
            tot = None
            for q in range(4):
                k = jnp.where(x != (q >> 1), 1, 0) + jnp.where(y != (q & 1), 2, 0)
                term = jnp.where(k == 0, css[a][...], jnp.where(k == 1, gs[a][0], jnp.where(k == 2, gs[a][1], gs[a][2])))
                tot = term if tot is None else tot + term
            half(outs[a], a, c)[...] = tot
            cp = pltpu.make_async_remote_copy(src_ref=half(outs[a], a, c), dst_ref=half(outs[a], a, c), send_sem=s3_send.at[a],
                                              recv_sem=s3_recv.at[a], device_id=sibling, device_id_type=MESH)
            cp.start()
            ex3.append(cp)
        for a in range(n_a):
            pltpu.make_async_remote_copy(src_ref=half(outs[a], a, 1 - c), dst_ref=half(outs[a], a, 1 - c), send_sem=s3_send.at[a],
                                         recv_sem=s3_recv.at[a], device_id=sibling, device_id_type=MESH).wait_recv()
        for cp in ex1 + ex2 + ex3:
            cp.wait_send()

    vm = pl.BlockSpec(memory_space=pltpu.VMEM)
    full = [(2, 8, PACK_W), (HEADS, CHUNK, CHUNK), (2, 8, D_FF)]
    halves = [(s[0] // 2,) + s[1:] for s in full]
    return pl.pallas_call(
        body, name="ar_small",
        out_shape=tuple(jax.ShapeDtypeStruct(s, F32) for s in full),
        in_specs=[vm] * 10, out_specs=(vm,) * 3,
        scratch_shapes=[pltpu.VMEM(full[0], F32)] + [pltpu.VMEM(s, F32) for s in halves] + [pltpu.VMEM(s, F32) for s in halves]
        + [pltpu.VMEM((3,) + s, F32) for s in halves]
        + [pltpu.SemaphoreType.DMA((n_a,)), pltpu.SemaphoreType.DMA((n_a,)), pltpu.SemaphoreType.DMA((n_a, 3)),
           pltpu.SemaphoreType.DMA((n_a, 3)), pltpu.SemaphoreType.DMA((n_a,)), pltpu.SemaphoreType.DMA((n_a,))],
        compiler_params=_cparams(),
    )(dg1, dg2, dgf, dgrn, dlng, dlnb, dbs, loss_parts, dws, dcv)


def _adam_math(w, g, m, v):
    nm = ADAM_B1 * m + (1.0 - ADAM_B1) * g
    nv = ADAM_B2 * v + (1.0 - ADAM_B2) * (g * g)
    d = -ADAM_LR * ((nm / (1.0 - ADAM_B1 ** ADAM_STEP)) / (jnp.sqrt(nv / (1.0 - ADAM_B2 ** ADAM_STEP)) + ADAM_EPS) + ADAM_WD * w)
    return d, nm, nv


def _adamw(name, w, gs, m, v, rows):
    _, r, cdim = w.shape
    n_steps = r // rows
    half = n_steps // len(gs)

    def body(w_ref, *rest):
        g_refs, (m_ref, v_ref, go_ref, d_ref, nm_ref, nv_ref) = rest[:len(gs)], rest[len(gs):]
        gg = g_refs[0][...]
        if len(gs) == 2:
            gg = jnp.where(pl.program_id(0) < half, gg, g_refs[1][...])
        go_ref[0] = gg
        d, nm, nv = _adam_math(w_ref[0], gg, m_ref[0], v_ref[0])
        d_ref[0], nm_ref[0], nv_ref[0] = d, nm, nv

    spec3 = pl.BlockSpec((1, rows, cdim), lambda i: (0, i, 0))
    if len(gs) == 1:
        g_specs = [pl.BlockSpec((rows, cdim), lambda i: (i, 0))]
    else:
        g_specs = [pl.BlockSpec((rows, cdim), lambda i: (jnp.minimum(i, half - 1), 0)),
                   pl.BlockSpec((rows, cdim), lambda i: (jnp.maximum(i - half, 0), 0))]
    sh = jax.ShapeDtypeStruct((1, r, cdim), F32)
    return pl.pallas_call(
        body, name=name, grid=(n_steps,), out_shape=(sh, sh, sh, sh),
        in_specs=[spec3] + g_specs + [spec3, spec3], out_specs=(spec3,) * 4,
        compiler_params=_cparams(("parallel",)),
    )(w, *gs, m, v)


def _adamw_small(rp, rws, rcv, gcw, params):
    n_p = len(params)

    def body(*refs):
        rp_ref, rws_ref, rcv_ref, gcw_ref = refs[:4]
        ins = refs[4:4 + 3 * n_p]
        outs = refs[4 + 3 * n_p:]
        grads = [rp_ref[0, 0:1, :], rp_ref[0, 1:2, :], rp_ref[0, 2:3, :], rp_ref[0, 3:4, 0:RET_W],
                 rp_ref[1, 0:HEADS, 0:128], rp_ref[1, 0:HEADS, 128:256], rp_ref[1, 0:HEADS, 256:384],
                 rws_ref[...], gcw_ref[0], None]
        for p in range(n_p):
            w_ref, m_ref, v_ref = ins[3 * p:3 * p + 3]
            o = outs[4 * p:4 * p + 4]
            if p == n_p - 1:
                for hf in range(2):
                    cs = slice(hf * D_FF, (hf + 1) * D_FF)
                    g = rcv_ref[hf, 3:4, :]
                    res = (g,) + _adam_math(w_ref[:, cs], g, m_ref[:, cs], v_ref[:, cs])
                    for t in range(4):
                        o[t][:, cs] = res[t]
                continue
            lead = w_ref.ndim > grads[p].ndim
            rd = (lambda r: r[0]) if lead else (lambda r: r[...])
            res = (grads[p],) + _adam_math(rd(w_ref), grads[p], rd(m_ref), rd(v_ref))
            for t in range(4):
                if lead:
                    o[t][0] = res[t]
                else:
                    o[t][...] = res[t]

    vm = pl.BlockSpec(memory_space=pltpu.VMEM)
    flat = [a for tr in params for a in tr]
    out_shape = tuple(jax.ShapeDtypeStruct(tr[0].shape, F32) for tr in params for _ in range(4))
    res = pl.pallas_call(
        body, name="adamw_small", out_shape=out_shape, in_specs=[vm] * (4 + len(flat)), out_specs=(vm,) * len(out_shape),
        compiler_params=_cparams(),
    )(rp, rws, rcv, gcw, *flat)
    return [res[4 * p:4 * p + 4] for p in range(n_p)]


def kernel(x, mix_norm_g, w_in, ret_norm_g, sgu_ln_g, sgu_ln_b, sgu_w_s, sgu_b_s, w_out, ffn_norm_g, w_up, conv_w, conv_b, w_down, final_norm_g, loss_target, m_mix_norm_g, m_w_in, m_ret_norm_g, m_sgu_ln_g, m_sgu_ln_b, m_sgu_w_s, m_sgu_b_s, m_w_out, m_ffn_norm_g, m_w_up, m_conv_w, m_conv_b, m_w_down, m_final_norm_g, v_mix_norm_g, v_w_in, v_ret_norm_g, v_sgu_ln_g, v_sgu_ln_b, v_sgu_w_s, v_sgu_b_s, v_w_out, v_ffn_norm_g, v_w_up, v_conv_w, v_conv_b, v_w_down, v_final_norm_g):
    xs = x[0]
    tgt = loss_target[0]
    cos2, sin2 = _rope_tables()
    mask, qdec, kdec = _decay_tables()
    grn = ret_norm_g.reshape(1, RET_W)
    lng = sgu_ln_g.reshape(1, SGU_W)
    lnb = sgu_ln_b.reshape(1, SGU_W)
    ws = sgu_w_s[0]
    bsb = jnp.broadcast_to(sgu_b_s[0][:, :, None], (HEADS, CHUNK, HEAD_DIM))
    gf = final_norm_g.reshape(1, D_MODEL)
    me = 4 * lax.axis_index("x") + 2 * lax.axis_index("y") + lax.axis_index("c")
    tr = lambda a: jnp.transpose(a[0])[None]

    proj, h1, win_g, cw_sh, wout_g, wdn_g, su = _fwd_proj(xs, mix_norm_g, cos2, sin2, w_in[0], w_out[0], tr(w_up)[0], w_down[0],
                                                         conv_w[0])
    cw_g = jnp.transpose(cw_sh, (1, 0, 2)).reshape(8, 2 * D_FF)
    x2, mixcat, o, sprev, wup_g = _fwd_mix(xs, proj, wout_g, grn, lng, lnb, ws, bsb, mask, qdec, kdec, su)
    h2, up_pre, u_conv, act, x3, loss_parts = _fwd_ffn(x2, ffn_norm_g, wup_g, cw_g, conv_b, wdn_g, gf, tgt)

    dx3, dpre, dx2, dgf, dg2, dcv = _bwd_ffn(x3, tgt, gf, x2, ffn_norm_g, up_pre, u_conv, wup_g, cw_g, wdn_g)
    band = FF_SHARD // 2
    (gdn_p,) = _wgrad("wgrad_down", act, dx3, tm=FF_TILE)
    (gout_p,) = _wgrad("wgrad_out", mixcat, dx2, tn=512)
    gup_p, g_dn = _wgrad("wgrad_up", dpre, h2, tm=FF_TILE, hosted=[(W_DOWN, gdn_p)])
    dproj, dgrn, dlng, dlnb, dws, dbs, g_up_a, g_out = _bwd_mix(
        dx2, proj, o, sprev, wout_g, grn, lng, lnb, ws, bsb, mask, qdec, kdec, cos2, sin2,
        [((W_UP, 0, band), gup_p), (W_OUT, gout_p)])
    gin_p, g_up_b = _wgrad("wgrad_in", h1, dproj, tn=768, hosted=[((W_UP, band, band), gup_p)])
    grad_x, dg1, g_in = _bwd_proj(dproj, win_g, xs, mix_norm_g, dx2, gin_p)
    rp, rws, rcv = _all_reduce_small(dg1, dg2, dgf, dgrn, dlng, dlnb, dbs, loss_parts, dws, dcv)
    loss = rp[0, 3, RET_W]
    gcw = lax.dynamic_slice(rcv, (me // (N_DEV // 2), 0, (me % (N_DEV // 2)) * FF_SHARD), (1, 3, FF_SHARD))

    table = {}
    for name, w, gs, m, v, rows in (("w_in", w_in, [g_in], m_w_in, v_w_in, 256), ("w_out", w_out, [g_out], m_w_out, v_w_out, 128),
                                    ("w_up", tr(w_up), [g_up_a, g_up_b], tr(m_w_up), tr(v_w_up), 176),
                                    ("w_down", w_down, [g_dn], m_w_down, v_w_down, 88)):
        table[name] = _adamw("adamw_" + name, w, gs, m, v, rows)
    table["w_up"] = tuple(tr(a) for a in table["w_up"])
    row = lambda a: a.reshape(1, D_MODEL)
    names_small = ["mix_norm_g", "ffn_norm_g", "final_norm_g", "ret_norm_g", "sgu_ln_g", "sgu_ln_b", "sgu_b_s", "sgu_w_s",
                   "conv_w", "conv_b"]
    params = [(mix_norm_g, m_mix_norm_g, v_mix_norm_g), (ffn_norm_g, m_ffn_norm_g, v_ffn_norm_g),
              (row(final_norm_g), row(m_final_norm_g), row(v_final_norm_g)), (ret_norm_g, m_ret_norm_g, v_ret_norm_g),
              (sgu_ln_g, m_sgu_ln_g, v_sgu_ln_g), (sgu_ln_b, m_sgu_ln_b, v_sgu_ln_b), (sgu_b_s, m_sgu_b_s, v_sgu_b_s),
              (sgu_w_s, m_sgu_w_s, v_sgu_w_s), (conv_w, m_conv_w, v_conv_w), (conv_b, m_conv_b, v_conv_b)]
    for n, res in zip(names_small, _adamw_small(rp, rws, rcv, gcw, params)):
        table[n] = res
    table["final_norm_g"] = tuple(a.reshape(D_MODEL) for a in table["final_norm_g"])

    order = ["mix_norm_g", "w_in", "ret_norm_g", "sgu_ln_g", "sgu_ln_b", "sgu_w_s", "sgu_b_s", "w_out", "ffn_norm_g", "w_up",
             "conv_w", "conv_b", "w_down", "final_norm_g"]
    outs = [loss, grad_x[None]]
    for col in range(4):
        outs += [table[n][col] for n in order]
    return tuple(outs)
```

```python
import functools
import math

import jax
import jax.numpy as jnp
import numpy as np
from jax import lax
from jax.experimental import pallas as pl
from jax.experimental.pallas import tpu as pltpu

F32 = jnp.float32
BF16 = jnp.bfloat16
MESH = pl.DeviceIdType.MESH

N_DEV = 8
SEQ = 2048
D_MODEL = 1024
CHUNK = 128
N_CHUNK = SEQ // CHUNK
HEADS = 4
HEAD_DIM = 128
RET_W = 512
SGU_W = 512
PROJ_W = 3072
D_FF = 2816
FF_SHARD = 704
FF_TILE = 1408
FF_TILES = ((0, 1536), (1536, 1280))
IN_SHARD = PROJ_W // N_DEV
OUT_SHARD = D_MODEL // N_DEV
DOWN_SHARD = D_FF // N_DEV
TM = 256
N_TB = SEQ // TM
EPS = 1e-6
ROPE_BASE = 10000.0
K_SCALE = HEAD_DIM ** -0.5
INV_SQRT2 = 0.7071067811865476
INV_SQRT_2PI = 0.3989422804014327

ADAM_LR = 0.001
ADAM_B1 = 0.9
ADAM_B2 = 0.999
ADAM_EPS = 1e-08
ADAM_WD = 0.01
ADAM_STEP = 10

VMEM_LIMIT = 56 * 1024 * 1024


def _cparams(sem=None, vmem=VMEM_LIMIT):
    return pltpu.CompilerParams(dimension_semantics=sem, vmem_limit_bytes=vmem)


def _resident(shape):
    nd = len(shape)
    return pl.BlockSpec(shape, lambda *_: (0,) * nd, pipeline_mode=pl.Buffered(1))


def _dot(a, b):
    return jnp.dot(a, b, preferred_element_type=F32)


def _dot_nt(a, b):
    return lax.dot_general(a, b, (((1,), (1,)), ((), ())), preferred_element_type=F32)


def _dot_tn(a, b):
    return lax.dot_general(a, b, (((0,), (0,)), ((), ())), preferred_element_type=F32)


def _sigmoid(x):
    return 1.0 / (1.0 + jnp.exp(-x))


def _gelu(x):
    return 0.5 * x * (1.0 + lax.erf(x * INV_SQRT2))


def _gelu_grad(x):
    return 0.5 * (1.0 + lax.erf(x * INV_SQRT2)) + x * (jnp.exp(-0.5 * x * x) * INV_SQRT_2PI)


def _rot(xh, cos2, sin2):
    return xh * cos2 + pltpu.roll(xh, HEAD_DIM // 2, 1) * sin2


def _rot_t(dh, cos2, sin2):
    return dh * cos2 + pltpu.roll(dh * sin2, HEAD_DIM // 2, 1)


def _rope_tables():
    half = HEAD_DIM // 2
    inv_freq = jnp.power(ROPE_BASE, -jnp.arange(half, dtype=F32) / half)
    ang = jnp.arange(SEQ, dtype=F32)[:, None] * inv_freq[None, :]
    cos, sin = jnp.cos(ang), jnp.sin(ang)
    cos2 = jnp.concatenate([cos, cos], axis=-1)
    sin2 = jnp.concatenate([-sin, sin], axis=-1)
    return cos2, sin2


def _decay_tables():
    log_gamma = jnp.log(1.0 - jnp.power(2.0, -5.0 - jnp.arange(HEADS, dtype=F32)))
    pos = jnp.arange(CHUNK, dtype=F32)
    diff = pos[:, None] - pos[None, :]
    mask = jnp.where(diff >= 0.0, jnp.exp(log_gamma[:, None, None] * jnp.maximum(diff, 0.0)[None]), 0.0)
    k_decay = jnp.exp(log_gamma[:, None] * (CHUNK - 1.0 - pos)[None])
    q_decay = jnp.exp(log_gamma[:, None] * (pos + 1.0)[None])
    kd = jnp.broadcast_to(k_decay[:, :, None], (HEADS, CHUNK, HEAD_DIM))
    qd = jnp.broadcast_to(q_decay[:, :, None], (HEADS, CHUNK, HEAD_DIM))
    return mask.astype(F32), qd.astype(F32), kd.astype(F32)


def _chunk_decay():
    lg = np.log(np.float32(1.0) - np.power(np.float32(2.0), -5.0 - np.arange(HEADS, dtype=np.float32))).astype(np.float32)
    return [float(np.exp(lg[h] * np.float32(CHUNK))) for h in range(HEADS)]


W_IN, W_OUT, W_UP, W_DOWN, W_CONV = range(5)
GATHERED = {W_IN: ((D_MODEL, PROJ_W), BF16), W_OUT: ((D_MODEL, D_MODEL), BF16), W_UP: ((2 * D_FF, D_MODEL), BF16),
            W_DOWN: ((D_FF, D_MODEL), BF16), W_CONV: ((N_DEV, 8, FF_SHARD), F32)}
SHARD = {W_IN: (D_MODEL, IN_SHARD), W_OUT: (OUT_SHARD, D_MODEL), W_UP: (FF_SHARD, D_MODEL), W_DOWN: (DOWN_SHARD, D_MODEL),
         W_CONV: (8, FF_SHARD)}


class _Gather:
    N_SEMS = 9

    def __init__(self, ids, stages, gathered, send_sems, recv_sems, local_sems):
        self.ids, self.stages, self.gathered = ids, stages, gathered
        self.send_sems, self.recv_sems, self.local_sems = send_sems, recv_sems, local_sems
        self.x, self.y, self.c = lax.axis_index("x"), lax.axis_index("y"), lax.axis_index("c")
        self.me = (self.x, self.y, self.c)
        self.sibling = (self.x, self.y, 1 - self.c)
        self.chips = [(1 - self.x, self.y), (self.x, 1 - self.y), (1 - self.x, 1 - self.y)]

    def slot(self, n, px, py, pc):
        dev = 4 * px + 2 * py + pc
        w, g = self.ids[n], self.gathered[n]
        if w == W_IN:
            return g.at[:, pl.ds(pl.multiple_of(dev * IN_SHARD, 128), IN_SHARD)]
        if w == W_OUT:
            return g.at[pl.ds(pl.multiple_of(dev * OUT_SHARD, 128), OUT_SHARD), :]
        if w == W_DOWN:
            return g.at[pl.ds(pl.multiple_of(dev * DOWN_SHARD, 32), DOWN_SHARD), :]
        if w == W_UP:
            return g.at[pl.ds(pl.multiple_of(dev * FF_SHARD, 32), FF_SHARD), :]
        return g.at[dev]

    def half(self, n, px, py, pc, h):
        dev = 4 * px + 2 * py + pc
        w, g = self.ids[n], self.gathered[n]
        if w == W_IN:
            return g.at[pl.ds(h * (D_MODEL // 2), D_MODEL // 2), pl.ds(pl.multiple_of(dev * IN_SHARD, 128), IN_SHARD)]
        rows = SHARD[w][0] // 2
        return g.at[pl.ds(pl.multiple_of(dev * SHARD[w][0] + h * rows, 16), rows), :]

    def tree(self, n):
        return self.ids[n] != W_CONV

    def copy(self, n, k, block, to, src=None, h=None):
        ref = self.slot(n, *block) if h is None else self.half(n, *block, h)
        return pltpu.make_async_remote_copy(
            src_ref=ref if src is None else src, dst_ref=ref,
            send_sem=self.send_sems.at[n, k], recv_sem=self.recv_sems.at[n, k], device_id=to, device_id_type=MESH)

    def _mine(self):
        return [pltpu.make_async_copy(self.stages[n], self.slot(n, *self.me), self.local_sems.at[n]) for n in range(len(self.ids))]

    def _first(self):
        out = []
        for n in range(len(self.ids)):
            out.append(self.copy(n, 0, self.me, self.sibling, src=self.stages[n]))
            out += [self.copy(n, 1 + j, self.me, (*chip, self.c), src=self.stages[n])
                    for j, chip in enumerate(self.chips[:2] if self.tree(n) else self.chips)]
        return out

    def start(self):
        for cp in self._mine() + self._first():
            cp.start()

    def finish(self):
        self.forward_near()
        self.forward_far()
        self.wait_all()

    def _go(self, cp):
        cp.start()
        self.passed.append(cp)

    def forward_near(self):
        cx, cy, cd = [(*chip, self.c) for chip in self.chips]
        self.passed = []
        go = self._go
        for n in range(len(self.ids)):
            if self.tree(n):
                self.copy(n, 1, cx, self.me).wait_recv()
                go(self.copy(n, 3, cx, cy, h=0))
                go(self.copy(n, 5, cx, self.sibling))
                self.copy(n, 2, cy, self.me).wait_recv()
                go(self.copy(n, 4, cy, cx, h=1))
                go(self.copy(n, 6, cy, self.sibling))
            else:
                for j, dev in enumerate((cx, cy, cd)):
                    self.copy(n, 1 + j, dev, self.me).wait_recv()
                    go(self.copy(n, 4 + j, dev, self.sibling))

    def forward_far(self):
        cd = (*self.chips[2], self.c)
        go = self._go
        for n in range(len(self.ids)):
            if self.tree(n):
                self.copy(n, 3, cd, self.me, h=0).wait_recv()
                go(self.copy(n, 7, cd, self.sibling, h=0))
                self.copy(n, 4, cd, self.me, h=1).wait_recv()
                go(self.copy(n, 8, cd, self.sibling, h=1))

    def wait_all(self):
        ox, oy, od = [(*chip, 1 - self.c) for chip in self.chips]
        for n in range(len(self.ids)):
            self.copy(n, 0, self.sibling, self.me).wait_recv()
            if self.tree(n):
                self.copy(n, 5, ox, self.me).wait_recv()
                self.copy(n, 6, oy, self.me).wait_recv()
                self.copy(n, 7, od, self.me, h=0).wait_recv()
                self.copy(n, 8, od, self.me, h=1).wait_recv()
            else:
                for j, dev in enumerate((ox, oy, od)):
                    self.copy(n, 4 + j, dev, self.me).wait_recv()
        for cp in self._first() + self.passed:
            cp.wait_send()
        for cp in self._mine():
            cp.wait()


def _gather_scratch(n):
    return [pltpu.SemaphoreType.DMA((n, _Gather.N_SEMS)), pltpu.SemaphoreType.DMA((n, _Gather.N_SEMS)), pltpu.SemaphoreType.DMA((n,))]


def _gathered_shapes(ids):
    return tuple(jax.ShapeDtypeStruct(*GATHERED[w]) for w in ids)


def _fwd_proj(x, g1, cos2, sin2, w_in, w_out, w_up, w_down, conv_w):
    ids_a, ids_b = [W_IN, W_CONV], [W_OUT, W_DOWN]

    def body(x_ref, g_ref, cos_ref, sin_ref, in_hbm, out_hbm, up_hbm, dn_hbm, cw_ref,
             proj_ref, h1_ref, gin, gcw, gout, gdn, su_ref,
             w_vm, s_in, s_cw, s_out, s_dn, f_in, f_out, f_up, f_dn, ld_sems,
             a_send, a_recv, a_local, b_send, b_recv, b_local):
        ag_a = _Gather(ids_a, [s_in, s_cw], [gin, gcw], a_send, a_recv, a_local)
        ag_b = _Gather(ids_b, [s_out, s_dn], [gout, gdn], b_send, b_recv, b_local)

        @pl.when(pl.program_id(0) == 0)
        def _():
            loads = [pltpu.make_async_copy(src, dst, ld_sems.at[i])
                     for i, (src, dst) in enumerate(((in_hbm, f_in), (out_hbm, f_out), (dn_hbm, f_dn), (up_hbm, f_up)))]
            for cp in loads:
                cp.start()
            s_cw[...] = jnp.zeros_like(s_cw)
            s_cw[0:3, :] = cw_ref[...]
            loads[0].wait()
            s_in[...] = f_in[...].astype(BF16)
            ag_a.start()
            loads[1].wait()
            s_out[...] = f_out[...].astype(BF16)
            loads[2].wait()
            s_dn[...] = f_dn[...].astype(BF16)
            ag_a.forward_near()
            ag_b.start()
            loads[3].wait()
            su_ref[...] = f_up[...].astype(BF16)
            ag_a.forward_far()
            ag_a.wait_all()
            fill = pltpu.make_async_copy(gin, w_vm, ld_sems.at[4])
            fill.start()
            fill.wait()

        xb = x_ref[...]
        r = lax.rsqrt(jnp.mean(xb * xb, axis=-1, keepdims=True) + EPS)
        h = ((xb * r) * g_ref[...]).astype(BF16)
        h1_ref[...] = h
        p = _dot(h, w_vm[...])
        for hd in range(HEADS):
            sl = slice(hd * HEAD_DIM, (hd + 1) * HEAD_DIM)
            c2, s2 = cos_ref[...], sin_ref[...]
            proj_ref[:, sl] = _rot(p[:, sl], c2, s2)
            ks = slice(RET_W + hd * HEAD_DIM, RET_W + (hd + 1) * HEAD_DIM)
            proj_ref[:, ks] = _rot(p[:, ks], c2, s2) * K_SCALE
        proj_ref[:, 2 * RET_W:] = p[:, 2 * RET_W:]

        @pl.when(pl.program_id(0) == N_TB - 1)
        def _():
            ag_b.finish()

    tok = lambda w: pl.BlockSpec((TM, w), lambda i: (i, 0))
    hbm = pl.BlockSpec(memory_space=pl.ANY)
    vm = pl.BlockSpec(memory_space=pltpu.VMEM)
    return pl.pallas_call(
        body, name="fwd_proj", grid=(N_TB,),
        out_shape=(jax.ShapeDtypeStruct((SEQ, PROJ_W), F32), jax.ShapeDtypeStruct((SEQ, D_MODEL), BF16))
        + _gathered_shapes(ids_a + ids_b) + (jax.ShapeDtypeStruct(SHARD[W_UP], BF16),),
        in_specs=[tok(D_MODEL), _resident((1, D_MODEL)), tok(HEAD_DIM), tok(HEAD_DIM), hbm, hbm, hbm, hbm, vm],
        out_specs=(tok(PROJ_W), tok(D_MODEL), hbm, hbm, hbm, hbm, vm),
        scratch_shapes=[pltpu.VMEM((D_MODEL, PROJ_W), BF16), pltpu.VMEM(SHARD[W_IN], BF16), pltpu.VMEM(SHARD[W_CONV], F32),
                        pltpu.VMEM(SHARD[W_OUT], BF16), pltpu.VMEM(SHARD[W_DOWN], BF16),
                        pltpu.VMEM(SHARD[W_IN], F32), pltpu.VMEM(SHARD[W_OUT], F32), pltpu.VMEM(SHARD[W_UP], F32),
                        pltpu.VMEM(SHARD[W_DOWN], F32), pltpu.SemaphoreType.DMA((5,))]
        + _gather_scratch(len(ids_a)) + _gather_scratch(len(ids_b)),
        compiler_params=_cparams(("arbitrary",)),
    )(x, g1, cos2, sin2, w_in, w_out, w_up, w_down, conv_w)


def _causal(w):
    r = lax.broadcasted_iota(jnp.int32, (CHUNK, CHUNK), 0)
    c = lax.broadcasted_iota(jnp.int32, (CHUNK, CHUNK), 1)
    return jnp.where(r >= c, w, 0.0)


def _fwd_mix(x, proj, wout_g, grn, lng, lnb, ws, bsb, mask, qdec, kdec, su):
    cdec = _chunk_decay()
    ids = [W_UP]

    def body(x_ref, p_ref, w_ref, grn_ref, lng_ref, lnb_ref, ws_ref, bsb_ref, m_ref, qd_ref, kd_ref, su_ref,
             x2_ref, cat_ref, o_ref, sp_ref, gup, state, send_sems, recv_sems, local_sems):
        ag = _Gather(ids, [su_ref], [gup], send_sems, recv_sems, local_sems)

        @pl.when(pl.program_id(0) == 0)
        def _():
            state[...] = jnp.zeros_like(state)
            ag.start()

        for h in range(HEADS):
            sl = slice(h * HEAD_DIM, (h + 1) * HEAD_DIM)
            q = p_ref[:, sl]
            k = p_ref[:, RET_W + h * HEAD_DIM:RET_W + (h + 1) * HEAD_DIM]
            v = p_ref[:, 2 * RET_W + h * HEAD_DIM:2 * RET_W + (h + 1) * HEAD_DIM]
            g = p_ref[:, 3 * RET_W + h * HEAD_DIM:3 * RET_W + (h + 1) * HEAD_DIM]
            qb, kb, vb = q.astype(BF16), k.astype(BF16), v.astype(BF16)
            a = _dot_nt(qb, kb) * m_ref[h]
            spb = state[h].astype(BF16)
            sp_ref[0, h] = spb
            o = _dot(a.astype(BF16), vb) + _dot((q * qd_ref[h]).astype(BF16), spb)
            state[h] = state[h] * cdec[h] + _dot_tn((k * kd_ref[h]).astype(BF16), vb)
            o_ref[:, sl] = o
            rinv = lax.rsqrt(jnp.mean(o * o, axis=-1, keepdims=True) + EPS)
            rn = (o * rinv) * grn_ref[:, sl]
            cat_ref[:, sl] = ((g * _sigmoid(g)) * rn).astype(BF16)
        for gi in range(HEADS):
            sl = slice(gi * HEAD_DIM, (gi + 1) * HEAD_DIM)
            u = p_ref[:, 4 * RET_W + gi * HEAD_DIM:4 * RET_W + (gi + 1) * HEAD_DIM]
            sv = p_ref[:, 4 * RET_W + SGU_W + gi * HEAD_DIM:4 * RET_W + SGU_W + (gi + 1) * HEAD_DIM]
            gv = _gelu(sv)
            xc = gv - jnp.mean(gv, axis=-1, keepdims=True)
            vn = (xc * lax.rsqrt(jnp.mean(xc * xc, axis=-1, keepdims=True) + EPS)) * lng_ref[:, sl] + lnb_ref[:, sl]
            mixed = _dot(_causal(ws_ref[gi]).astype(BF16), vn.astype(BF16)) + bsb_ref[gi]
            cat_ref[:, RET_W + gi * HEAD_DIM:RET_W + (gi + 1) * HEAD_DIM] = (_gelu(u) * mixed).astype(BF16)
        x2_ref[...] = x_ref[...] + _dot(cat_ref[...], w_ref[...])

        @pl.when(pl.program_id(0) == N_CHUNK - 1)
        def _():
            ag.finish()

    ch = lambda w: pl.BlockSpec((CHUNK, w), lambda i: (i, 0))
    hcc = (HEADS, CHUNK, CHUNK)
    hbm = pl.BlockSpec(memory_space=pl.ANY)
    return pl.pallas_call(
        body, name="fwd_mix", grid=(N_CHUNK,),
        out_shape=(jax.ShapeDtypeStruct((SEQ, D_MODEL), F32), jax.ShapeDtypeStruct((SEQ, D_MODEL), BF16),
                   jax.ShapeDtypeStruct((SEQ, RET_W), F32), jax.ShapeDtypeStruct((N_CHUNK, HEADS, HEAD_DIM, HEAD_DIM), BF16))
        + _gathered_shapes(ids),
        in_specs=[ch(D_MODEL), ch(PROJ_W), _resident((D_MODEL, D_MODEL)), _resident((1, RET_W)), _resident((1, SGU_W)),
                  _resident((1, SGU_W)), _resident(hcc), _resident(hcc), _resident(hcc), _resident(hcc), _resident(hcc), hbm],
        out_specs=(ch(D_MODEL), ch(D_MODEL), ch(RET_W), pl.BlockSpec((1, HEADS, HEAD_DIM, HEAD_DIM), lambda i: (i, 0, 0, 0)), hbm),
        scratch_shapes=[pltpu.VMEM((HEADS, HEAD_DIM, HEAD_DIM), F32)] + _gather_scratch(len(ids)),
        compiler_params=_cparams(("arbitrary",)),
    )(x, proj, wout_g, grn, lng, lnb, ws, bsb, mask, qdec, kdec, su)


def _conv_taps(p, prev8):
    row = lax.broadcasted_iota(jnp.int32, p.shape, 0)
    p1 = jnp.where(row == 0, prev8[7:8, :], pltpu.roll(p, 1, 0))
    p2 = jnp.where(row == 0, prev8[6:7, :], jnp.where(row == 1, prev8[7:8, :], pltpu.roll(p, 2, 0)))
    return p1, p2


def _fwd_ffn(x2, g2, wup_g, cw_g, cb_g, wdn_g, gf, tgt):
    def body(x_ref, g_ref, wu_ref, cw_ref, cb_ref, wd_ref, gf_ref, t_ref, h2_ref, up_ref, u_ref, act_ref, x3_ref, loss_ref, carry):
        @pl.when(pl.program_id(0) == 0)
        def _():
            carry[...] = jnp.zeros_like(carry)

        xb = x_ref[...]
        r = lax.rsqrt(jnp.mean(xb * xb, axis=-1, keepdims=True) + EPS)
        h = ((xb * r) * g_ref[...]).astype(BF16)
        h2_ref[...] = h
        acc = xb
        for t0, tw in FF_TILES:
            u = []
            for c0 in (t0, D_FF + t0):
                cs = slice(c0, c0 + tw)
                p = _dot_nt(h, wu_ref[pl.ds(c0, tw), :])
                up_ref[:, cs] = p.astype(BF16)
                p1, p2 = _conv_taps(p, carry[:, cs])
                carry[:, cs] = p[TM - 8:, :]
                us = p2 * cw_ref[0:1, cs] + p1 * cw_ref[1:2, cs] + p * cw_ref[2:3, cs] + cb_ref[:, cs]
                u_ref[:, cs] = us.astype(BF16)
                u.append(us)
            a = ((u[0] * _sigmoid(u[0])) * u[1]).astype(BF16)
            act_ref[:, t0:t0 + tw] = a
            acc = acc + _dot(a, wd_ref[pl.ds(t0, tw), :])
        x3_ref[...] = acc
        r3 = lax.rsqrt(jnp.mean(acc * acc, axis=-1, keepdims=True) + EPS)
        diff = (acc * r3) * gf_ref[...] - t_ref[...]
        loss_ref[...] = jnp.full(loss_ref.shape, 0.5 * jnp.sum(jnp.mean(diff * diff, axis=-1)), F32)

    tok = lambda w: pl.BlockSpec((TM, w), lambda i: (i, 0))
    return pl.pallas_call(
        body, name="fwd_ffn", grid=(N_TB,),
        out_shape=(jax.ShapeDtypeStruct((SEQ, D_MODEL), BF16), jax.ShapeDtypeStruct((SEQ, 2 * D_FF), BF16),
                   jax.ShapeDtypeStruct((SEQ, 2 * D_FF), BF16),
                   jax.ShapeDtypeStruct((SEQ, D_FF), BF16), jax.ShapeDtypeStruct((SEQ, D_MODEL), F32),
                   jax.ShapeDtypeStruct((N_TB, 8, 128), F32)),
        in_specs=[tok(D_MODEL), _resident((1, D_MODEL)), _resident((2 * D_FF, D_MODEL)), _resident((8, 2 * D_FF)),
                  _resident((1, 2 * D_FF)), _resident((D_FF, D_MODEL)), _resident((1, D_MODEL)), tok(D_MODEL)],
        out_specs=(tok(D_MODEL), tok(2 * D_FF), tok(2 * D_FF), tok(D_FF), tok(D_MODEL),
                   pl.BlockSpec((1, 8, 128), lambda i: (i, 0, 0))),
        scratch_shapes=[pltpu.VMEM((8, 2 * D_FF), F32)],
        compiler_params=_cparams(("arbitrary",)),
    )(x2, g2, wup_g, cw_g, cb_g, wdn_g, gf, tgt)


def _bwd_ffn(x3, tgt, gf, x2, g2, up_pre, u_conv, wup_g, cw_g, wdn_g):
    def body(x3_ref, t_ref, gf_ref, x2_ref, g2_ref, up_ref, u_ref, wu_ref, cw_ref, wd_ref,
             dx3_ref, dpre_ref, dx2_ref, dgf_ref, dg2_ref, dcv_ref, nxt):
        i = pl.program_id(0)

        @pl.when(i == 0)
        def _():
            nxt[...] = jnp.zeros_like(nxt)
            dgf_ref[...] = jnp.zeros_like(dgf_ref)
            dg2_ref[...] = jnp.zeros_like(dg2_ref)
            dcv_ref[...] = jnp.zeros_like(dcv_ref)

        x3 = x3_ref[...]
        r3 = lax.rsqrt(jnp.mean(x3 * x3, axis=-1, keepdims=True) + EPS)
        xh3 = x3 * r3
        dy = (xh3 * gf_ref[...] - t_ref[...]) * (1.0 / D_MODEL)
        dgf_ref[0:1, :] += jnp.sum(dy * xh3, axis=0, keepdims=True)
        t3 = dy * gf_ref[...]
        dx3 = r3 * (t3 - xh3 * jnp.mean(t3 * xh3, axis=-1, keepdims=True))
        dx3b = dx3.astype(BF16)
        dx3_ref[...] = dx3b
        dh2 = jnp.zeros((TM, D_MODEL), F32)
        for t0, tw in FF_TILES:
            row = lax.broadcasted_iota(jnp.int32, (TM, tw), 0)
            ts = slice(t0, t0 + tw)
            dact = _dot_nt(dx3b, wd_ref[pl.ds(t0, tw), :])
            ua = u_ref[:, ts].astype(F32)
            ub = u_ref[:, D_FF + t0:D_FF + t0 + tw].astype(F32)
            sg = _sigmoid(ua)
            du = [dact * ub * (sg * (1.0 + ua * (1.0 - sg))), dact * (ua * sg)]
            for n in range(2):
                d = du[n]
                c0 = n * D_FF + t0
                cs = slice(c0, c0 + tw)
                nx = nxt[:, cs]
                n1 = jnp.where(row == TM - 1, nx[0:1, :], pltpu.roll(d, TM - 1, 0))
                n2 = jnp.where(row == TM - 2, nx[0:1, :], jnp.where(row == TM - 1, nx[1:2, :], pltpu.roll(d, TM - 2, 0)))
                nxt[:, cs] = d[0:8, :]
                dp = (d * cw_ref[2:3, cs] + n1 * cw_ref[1:2, cs] + n2 * cw_ref[0:1, cs]).astype(BF16)
                dpre_ref[:, cs] = dp
                p = up_ref[:, cs].astype(F32)
                dcv_ref[n, 0:1, ts] += jnp.sum(n2 * p, axis=0, keepdims=True)
                dcv_ref[n, 1:2, ts] += jnp.sum(n1 * p, axis=0, keepdims=True)
                dcv_ref[n, 2:3, ts] += jnp.sum(d * p, axis=0, keepdims=True)
                dcv_ref[n, 3:4, ts] += jnp.sum(d, axis=0, keepdims=True)
                dh2 = dh2 + _dot(dp, wu_ref[pl.ds(c0, tw), :])
        x2 = x2_ref[...]
        r2 = lax.rsqrt(jnp.mean(x2 * x2, axis=-1, keepdims=True) + EPS)
        xh2 = x2 * r2
        dg2_ref[0:1, :] += jnp.sum(dh2 * xh2, axis=0, keepdims=True)
        t2 = dh2 * g2_ref[...]
        dx2_ref[...] = dx3 + r2 * (t2 - xh2 * jnp.mean(t2 * xh2, axis=-1, keepdims=True))

    rev = lambda w: pl.BlockSpec((TM, w), lambda i: (N_TB - 1 - i, 0))
    acc = lambda s: pl.BlockSpec(s, lambda i: (0,) * len(s))
    return pl.pallas_call(
        body, name="bwd_ffn", grid=(N_TB,),
        out_shape=(jax.ShapeDtypeStruct((SEQ, D_MODEL), BF16), jax.ShapeDtypeStruct((SEQ, 2 * D_FF), BF16),
                   jax.ShapeDtypeStruct((SEQ, D_MODEL), F32), jax.ShapeDtypeStruct((8, D_MODEL), F32),
                   jax.ShapeDtypeStruct((8, D_MODEL), F32), jax.ShapeDtypeStruct((2, 8, D_FF), F32)),
        in_specs=[rev(D_MODEL), rev(D_MODEL), _resident((1, D_MODEL)), rev(D_MODEL), _resident((1, D_MODEL)), rev(2 * D_FF),
                  rev(2 * D_FF), _resident((2 * D_FF, D_MODEL)), _resident((8, 2 * D_FF)), _resident((D_FF, D_MODEL))],
        out_specs=(rev(D_MODEL), rev(2 * D_FF), rev(D_MODEL), acc((8, D_MODEL)), acc((8, D_MODEL)), acc((2, 8, D_FF))),
        scratch_shapes=[pltpu.VMEM((8, 2 * D_FF), F32)],
        compiler_params=_cparams(("arbitrary",)),
    )(x3, tgt, gf, x2, g2, up_pre, u_conv, wup_g, cw_g, wdn_g)


def _bwd_mix(dx2, proj, o, sprev, wout_g, grn, lng, lnb, ws, bsb, mask, qdec, kdec, cos2, sin2, hosted):
    cdec = _chunk_decay()
    geoms = [g for g, _ in hosted]
    n_h = len(hosted)

    def body(dx2_ref, p_ref, o_ref, sp_ref, w_ref, grn_ref, lng_ref, lnb_ref, ws_ref, bsb_ref, m_ref, qd_ref, kd_ref,
             cos_ref, sin_ref, *rest):
        dp_ref, dgrn_ref, dlng_ref, dlnb_ref, dws_ref, dbs_ref = rest[n_h:n_h + 6]
        dstate, dbs_acc = rest[2 * n_h + 6:2 * n_h + 8]
        i = pl.program_id(0)
        rs = _Scatters(geoms, rest[:n_h], rest[n_h + 6:2 * n_h + 6], rest[2 * n_h + 8:])
        pl.when(i == 0)(rs.phase1)
        pl.when(i == 3)(rs.phase2)
        pl.when(i == 6)(rs.phase2b)

        @pl.when(i == 0)
        def _():
            dstate[...] = jnp.zeros_like(dstate)
            dgrn_ref[...] = jnp.zeros_like(dgrn_ref)
            dlng_ref[...] = jnp.zeros_like(dlng_ref)
            dlnb_ref[...] = jnp.zeros_like(dlnb_ref)
            dws_ref[...] = jnp.zeros_like(dws_ref)
            dbs_ref[...] = jnp.zeros_like(dbs_ref)
            dbs_acc[...] = jnp.zeros_like(dbs_acc)

        dmix = _dot_nt(dx2_ref[...].astype(BF16), w_ref[...])
        for h in range(HEADS):
            sl = slice(h * HEAD_DIM, (h + 1) * HEAD_DIM)
            q = p_ref[:, sl]
            k = p_ref[:, RET_W + h * HEAD_DIM:RET_W + (h + 1) * HEAD_DIM]
            v = p_ref[:, 2 * RET_W + h * HEAD_DIM:2 * RET_W + (h + 1) * HEAD_DIM]
            g = p_ref[:, 3 * RET_W + h * HEAD_DIM:3 * RET_W + (h + 1) * HEAD_DIM]
            o = o_ref[:, sl]
            rinv = lax.rsqrt(jnp.mean(o * o, axis=-1, keepdims=True) + EPS)
            oh = o * rinv
            gr = grn_ref[:, sl]
            sg = _sigmoid(g)
            dret = dmix[:, sl]
            dp_ref[:, 3 * RET_W + h * HEAD_DIM:3 * RET_W + (h + 1) * HEAD_DIM] = (
                dret * (oh * gr) * (sg * (1.0 + g * (1.0 - sg)))).astype(BF16)
            drn = dret * (g * sg)
            dgrn_ref[0:1, sl] += jnp.sum(drn * oh, axis=0, keepdims=True)
            t = drn * gr
            do = rinv * (t - oh * jnp.mean(t * oh, axis=-1, keepdims=True))
            qb, kb, vb, dob = q.astype(BF16), k.astype(BF16), v.astype(BF16), do.astype(BF16)
            m = m_ref[h]
            ab = (_dot_nt(qb, kb) * m).astype(BF16)
            dab = (_dot_nt(dob, vb) * m).astype(BF16)
            spb = sp_ref[0, h]
            dsn = dstate[h]
            dsnb = dsn.astype(BF16)
            qdb = (q * qd_ref[h]).astype(BF16)
            kdb = (k * kd_ref[h]).astype(BF16)
            dq = _dot(dab, kb) + _dot_nt(dob, spb) * qd_ref[h]
            dk = _dot_tn(dab, qb) + _dot_nt(vb, dsnb) * kd_ref[h]
            dv = _dot_tn(ab, dob) + _dot(kdb, dsnb)
            dstate[h] = dsn * cdec[h] + _dot_tn(qdb, dob)
            c2, s2 = cos_ref[...], sin_ref[...]
            dp_ref[:, sl] = _rot_t(dq, c2, s2).astype(BF16)
            dp_ref[:, RET_W + h * HEAD_DIM:RET_W + (h + 1) * HEAD_DIM] = _rot_t(dk * K_SCALE, c2, s2).astype(BF16)
            dp_ref[:, 2 * RET_W + h * HEAD_DIM:2 * RET_W + (h + 1) * HEAD_DIM] = dv.astype(BF16)
        for gi in range(HEADS):
            sl = slice(gi * HEAD_DIM, (gi + 1) * HEAD_DIM)
            u = p_ref[:, 4 * RET_W + gi * HEAD_DIM:4 * RET_W + (gi + 1) * HEAD_DIM]
            sv = p_ref[:, 4 * RET_W + SGU_W + gi * HEAD_DIM:4 * RET_W + SGU_W + (gi + 1) * HEAD_DIM]
            gv = _gelu(sv)
            xc = gv - jnp.mean(gv, axis=-1, keepdims=True)
            rstd = lax.rsqrt(jnp.mean(xc * xc, axis=-1, keepdims=True) + EPS)
            xh = xc * rstd
            lg = lng_ref[:, sl]
            vnb = (xh * lg + lnb_ref[:, sl]).astype(BF16)
            wcb = _causal(ws_ref[gi]).astype(BF16)
            mixed = _dot(wcb, vnb) + bsb_ref[gi]
            dsgu = dmix[:, RET_W + gi * HEAD_DIM:RET_W + (gi + 1) * HEAD_DIM]
            dmixed = dsgu * _gelu(u)
            dmb = dmixed.astype(BF16)
            dws_ref[gi] += _causal(_dot_nt(dmb, vnb))
            dbs_acc[gi] += dmixed
            dvn = _dot_tn(wcb, dmb)
            dlng_ref[gi:gi + 1, :] += jnp.sum(dvn * xh, axis=0, keepdims=True)
            dlnb_ref[gi:gi + 1, :] += jnp.sum(dvn, axis=0, keepdims=True)
            dxh = dvn * lg
            dgv = rstd * (dxh - jnp.mean(dxh, axis=-1, keepdims=True) - xh * jnp.mean(dxh * xh, axis=-1, keepdims=True))
            dp_ref[:, 4 * RET_W + gi * HEAD_DIM:4 * RET_W + (gi + 1) * HEAD_DIM] = (dsgu * mixed * _gelu_grad(u)).astype(BF16)
            dp_ref[:, 4 * RET_W + SGU_W + gi * HEAD_DIM:4 * RET_W + SGU_W + (gi + 1) * HEAD_DIM] = (
                dgv * _gelu_grad(sv)).astype(BF16)

        @pl.when(i == N_CHUNK - 1)
        def _():
            for gi in range(HEADS):
                col = jnp.broadcast_to(jnp.sum(dbs_acc[gi], axis=-1, keepdims=True), (CHUNK, CHUNK))
                dbs_ref[gi:gi + 1, :] = jnp.transpose(col)[0:1, :]
            rs.phase3()

    rev = lambda w: pl.BlockSpec((CHUNK, w), lambda i: (N_CHUNK - 1 - i, 0))
    hcc = (HEADS, CHUNK, CHUNK)
    acc = lambda s: pl.BlockSpec(s, lambda i: (0,) * len(s))
    res = pl.pallas_call(
        body, name="bwd_mix", grid=(N_CHUNK,),
        out_shape=(jax.ShapeDtypeStruct((SEQ, PROJ_W), BF16), jax.ShapeDtypeStruct((8, RET_W), F32),
                   jax.ShapeDtypeStruct((8, HEAD_DIM), F32), jax.ShapeDtypeStruct((8, HEAD_DIM), F32),
                   jax.ShapeDtypeStruct(hcc, F32), jax.ShapeDtypeStruct((8, CHUNK), F32)) + _scatter_out_shapes(geoms),
        in_specs=[rev(D_MODEL), rev(PROJ_W), rev(RET_W),
                  pl.BlockSpec((1, HEADS, HEAD_DIM, HEAD_DIM), lambda i: (N_CHUNK - 1 - i, 0, 0, 0)),
                  _resident((D_MODEL, D_MODEL)), _resident((1, RET_W)), _resident((1, SGU_W)), _resident((1, SGU_W)),
                  _resident(hcc), _resident(hcc), _resident(hcc), _resident(hcc), _resident(hcc), rev(HEAD_DIM), rev(HEAD_DIM)]
        + [pl.BlockSpec(memory_space=pl.ANY)] * n_h,
        out_specs=(rev(PROJ_W), acc((8, RET_W)), acc((8, HEAD_DIM)), acc((8, HEAD_DIM)), acc(hcc), acc((8, CHUNK)))
        + _scatter_out_specs(geoms),
        scratch_shapes=[pltpu.VMEM((HEADS, HEAD_DIM, HEAD_DIM), F32), pltpu.VMEM((HEADS, CHUNK, CHUNK), F32)] + _scatter_scratch(geoms),
        compiler_params=_cparams(("arbitrary",)),
    )(dx2, proj, o, sprev, wout_g, grn, lng, lnb, ws, bsb, mask, qdec, kdec, cos2, sin2, *[p for _, p in hosted])
    return tuple(res[:6 + n_h])


def _bwd_proj(dproj, win_g, x, g1, dx2, gin_p):
    geoms = [W_IN]

    def body(dp_ref, w_ref, x_ref, g_ref, dx2_ref, gin_ref, dx_ref, dg_ref, rs_out, *rs_scratch):
        rs = _Scatters(geoms, [gin_ref], [rs_out], rs_scratch)
        pl.when(pl.program_id(0) == 0)(rs.phase1)
        pl.when(pl.program_id(0) == 2)(rs.phase2)
        pl.when(pl.program_id(0) == 4)(rs.phase2b)

        @pl.when(pl.program_id(0) == 0)
        def _():
            dg_ref[...] = jnp.zeros_like(dg_ref)

        dh = _dot_nt(dp_ref[...], w_ref[...])
        xb = x_ref[...]
        r = lax.rsqrt(jnp.mean(xb * xb, axis=-1, keepdims=True) + EPS)
        xh = xb * r
        dg_ref[0:1, :] += jnp.sum(dh * xh, axis=0, keepdims=True)
        t = dh * g_ref[...]
        dx_ref[...] = dx2_ref[...] + r * (t - xh * jnp.mean(t * xh, axis=-1, keepdims=True))
        pl.when(pl.program_id(0) == N_TB - 1)(rs.phase3)

    tok = lambda w: pl.BlockSpec((TM, w), lambda i: (i, 0))
    res = pl.pallas_call(
        body, name="bwd_proj", grid=(N_TB,),
        out_shape=(jax.ShapeDtypeStruct((SEQ, D_MODEL), F32), jax.ShapeDtypeStruct((8, D_MODEL), F32)) + _scatter_out_shapes(geoms),
        in_specs=[tok(PROJ_W), _resident((D_MODEL, PROJ_W)), tok(D_MODEL), _resident((1, D_MODEL)), tok(D_MODEL),
                  pl.BlockSpec(memory_space=pl.ANY)],
        out_specs=(tok(D_MODEL), pl.BlockSpec((8, D_MODEL), lambda i: (0, 0))) + _scatter_out_specs(geoms),
        scratch_shapes=_scatter_scratch(geoms),
        compiler_params=_cparams(("arbitrary",)),
    )(dproj, win_g, x, g1, dx2, gin_p)
    return res[:3]


def _wgrad(name, a, b, tm=None, tn=None, hosted=()):
    m_w, n_w = a.shape[-1], b.shape[-1]
    tm = m_w if tm is None else tm
    tn = n_w if tn is None else tn
    n_steps = (m_w // tm) * (n_w // tn)
    geoms = [g for g, _ in hosted]
    n_h = len(hosted)

    def body(a_ref, b_ref, *rest):
        o_ref = rest[n_h]
        if n_h:
            rs = _Scatters(geoms, rest[:n_h], rest[n_h + 1:2 * n_h + 1], rest[2 * n_h + 1:])
            step = pl.program_id(0) * (n_w // tn) + pl.program_id(1)
            pl.when(step == 0)(rs.phase1)
            pl.when(step == 1)(rs.phase2)
            pl.when(step == 2)(rs.phase2b)
        o_ref[...] = _dot_tn(a_ref[...].astype(BF16), b_ref[...].astype(BF16)).astype(BF16)
        if n_h:
            pl.when(step == n_steps - 1)(rs.phase3)

    assert not n_h or n_steps >= 4
    res = pl.pallas_call(
        body, name=name, grid=(m_w // tm, n_w // tn),
        out_shape=(jax.ShapeDtypeStruct((m_w, n_w), BF16),) + _scatter_out_shapes(geoms),
        in_specs=[pl.BlockSpec((SEQ, tm), lambda i, j: (0, i)), pl.BlockSpec((SEQ, tn), lambda i, j: (0, j))]
        + [pl.BlockSpec(memory_space=pl.ANY)] * n_h,
        out_specs=(pl.BlockSpec((tm, tn), lambda i, j: (i, j)),) + _scatter_out_specs(geoms),
        scratch_shapes=_scatter_scratch(geoms),
        compiler_params=_cparams(("arbitrary", "arbitrary") if n_h else ("parallel", "parallel")),
    )(a, b, *[p for _, p in hosted])
    return tuple(res[:1 + n_h])


def _row_step(half_rows):
    return max(s for s in range(16, 177, 16) if half_rows % s == 0)


class _Scatter:
    def __init__(self, geom, partial, out, land1, mine, stage2, land2, comb, s1_send, s1_recv, s2_send, s2_recv, ld_sems):
        self.w, self.row0, self.shape = _geom(geom)
        self.partial, self.out, self.land1 = partial, out, land1
        self.mine, self.stage2, self.land2, self.comb = mine, stage2, land2, comb
        self.hr = self.shape[0] // 2
        self.step = _row_step(self.hr)
        self.s1_send, self.s1_recv, self.s2_send, self.s2_recv, self.ld_sems = s1_send, s1_recv, s2_send, s2_recv, ld_sems
        self.x, self.y, self.c = lax.axis_index("x"), lax.axis_index("y"), lax.axis_index("c")
        self.sibling = (self.x, self.y, 1 - self.c)
        self.chips = [(self.x, self.y), (1 - self.x, self.y), (self.x, 1 - self.y), (1 - self.x, 1 - self.y)]

    def block(self, px, py, pc):
        dev = 4 * px + 2 * py + pc
        if self.w == W_IN:
            return self.partial.at[:, pl.ds(pl.multiple_of(dev * IN_SHARD, 128), IN_SHARD)]
        if self.w == W_OUT:
            return self.partial.at[pl.ds(pl.multiple_of(dev * OUT_SHARD, 128), OUT_SHARD), :]
        if self.w == W_DOWN:
            return self.partial.at[pl.ds(pl.multiple_of(dev * DOWN_SHARD, 32), DOWN_SHARD), :]
        return self.partial.at[pl.ds(pl.multiple_of(dev * FF_SHARD + self.row0, 32), self.shape[0]), :]

    def copy1(self, k):
        return pltpu.make_async_remote_copy(
            src_ref=self.block(*self.chips[k], 1 - self.c), dst_ref=self.land1.at[k],
            send_sem=self.s1_send.at[k], recv_sem=self.s1_recv.at[k], device_id=self.sibling, device_id_type=MESH)

    STAGE2 = [(1, 0, 1), (3, 0, 1), (2, 1, 2), (3, 1, 2), (1, 1, 1), (2, 0, 2)]

    def copy2(self, j):
        blk, h, to = self.STAGE2[j]
        src = self.comb.at[j - 4] if j >= 4 else self.stage2.at[blk - 1, pl.ds(h * self.hr, self.hr), :]
        return pltpu.make_async_remote_copy(
            src_ref=src, dst_ref=self.land2.at[j], send_sem=self.s2_send.at[j], recv_sem=self.s2_recv.at[j],
            device_id=(*self.chips[to], self.c), device_id_type=MESH)

    def _rows(self, h=None):
        step = self.step
        lo, n = (0, self.shape[0]) if h is None else (h * self.hr, self.hr)
        return [pl.ds(r0, step) for r0 in range(lo, lo + n, step)]

    def load(self, k):
        return pltpu.make_async_copy(self.block(*self.chips[k], self.c), self.mine.at[k], self.ld_sems.at[k])

    def phase1(self):
        for k in range(4):
            self.copy1(k).start()
        for k in range(4):
            self.load(k).start()

    def phase2(self):
        for k in (3, 1, 2, 0):
            self.copy1(k).wait_recv()
            self.load(k).wait()
            for rs in self._rows():
                s = self.mine[k, rs, :].astype(F32) + self.land1[k, rs, :].astype(F32)
                if k == 0:
                    self.out[rs, :] = s
                else:
                    self.stage2[k - 1, rs, :] = s.astype(BF16)
            for j in {3: (1, 3), 1: (0,), 2: (2,), 0: ()}[k]:
                self.copy2(j).start()

    def phase2b(self):
        for j, got in ((4, 3), (5, 1)):
            blk, h, _ = self.STAGE2[j]
            self.copy2(got).wait_recv()
            for i, rs in enumerate(self._rows(h)):
                lr = pl.ds(i * self.step, self.step)
                self.comb[j - 4, lr, :] = (self.stage2[blk - 1, rs, :].astype(F32) + self.land2[got, lr, :].astype(F32)).astype(BF16)
            self.copy2(j).start()

    def phase3(self):
        for j in (0, 5, 4, 2):
            self.copy2(j).wait_recv()
        for h, (first, second) in enumerate(((0, 5), (4, 2))):
            for i, rs in enumerate(self._rows(h)):
                lr = pl.ds(i * self.step, self.step)
                self.out[rs, :] = (self.out[rs, :] + self.land2[first, lr, :].astype(F32)) + self.land2[second, lr, :].astype(F32)
        for k in range(4):
            self.copy1(k).wait_send()
        for j in range(6):
            self.copy2(j).wait_send()


def _geom(geom):
    if isinstance(geom, tuple):
        w, row0, rows = geom
        assert w == W_UP
        return w, row0, (rows, SHARD[w][1])
    return geom, 0, SHARD[geom]


N_SCATTER_SCRATCH = 10


def _scatter_out_shapes(geoms):
    return tuple(jax.ShapeDtypeStruct(_geom(g)[2], F32) for g in geoms)


def _scatter_out_specs(geoms):
    return (pl.BlockSpec(memory_space=pltpu.VMEM),) * len(geoms)


def _scatter_scratch(geoms):
    out = []
    for g in geoms:
        s = _geom(g)[2]
        hs = (s[0] // 2, s[1])
        out += [pltpu.VMEM((4,) + s, BF16), pltpu.VMEM((4,) + s, BF16), pltpu.VMEM((3,) + s, BF16), pltpu.VMEM((6,) + hs, BF16),
                pltpu.VMEM((2,) + hs, BF16),
                pltpu.SemaphoreType.DMA((4,)), pltpu.SemaphoreType.DMA((4,)), pltpu.SemaphoreType.DMA((6,)),
                pltpu.SemaphoreType.DMA((6,)), pltpu.SemaphoreType.DMA((4,))]
    return out


class _Scatters:
    def __init__(self, geoms, p_refs, out_refs, scratch):
        k = N_SCATTER_SCRATCH
        self.items = [_Scatter(g, p_refs[i], out_refs[i], *scratch[k * i:k * i + k]) for i, g in enumerate(geoms)]

    def phase1(self):
        for s in self.items:
            s.phase1()

    def phase2(self):
        for s in self.items:
            s.phase2()

    def phase2b(self):
        for s in self.items:
            s.phase2b()

    def phase3(self):
        for s in self.items:
            s.phase3()


PACK_W = 1024


def _all_reduce_small(dg1, dg2, dgf, dgrn, dlng, dlnb, dbs, loss_parts, dws, dcv):
    n_a = 3

    def body(dg1_ref, dg2_ref, dgf_ref, dgrn_ref, dlng_ref, dlnb_ref, dbs_ref, loss_ref, dws_ref, dcv_ref,
             rp_ref, rws_ref, rcv_ref, pack, rx_p, rx_ws, rx_cv, cs_p, cs_ws, cs_cv, g_p, g_ws, g_cv,
             s1_send, s1_recv, s2_send, s2_recv, s3_send, s3_recv):
        x, y, c = lax.axis_index("x"), lax.axis_index("y"), lax.axis_index("c")
        sibling = (x, y, 1 - c)
        chips = [(1 - x, y), (x, 1 - y), (1 - x, 1 - y)]
        pack[...] = jnp.zeros_like(pack)
        pack[0, 0:1, :] = dg1_ref[0:1, :]
        pack[0, 1:2, :] = dg2_ref[0:1, :]
        pack[0, 2:3, :] = dgf_ref[0:1, :]
        pack[0, 3:4, 0:RET_W] = dgrn_ref[0:1, :]
        lsum = loss_ref[0, 0:1, :]
        for i in range(1, N_TB):
            lsum = lsum + loss_ref[i, 0:1, :]
        pack[0, 3:4, RET_W:RET_W + 128] = lsum
        pack[1, 0:HEADS, 0:128] = dlng_ref[0:HEADS, :]
        pack[1, 0:HEADS, 128:256] = dlnb_ref[0:HEADS, :]
        pack[1, 0:HEADS, 256:384] = dbs_ref[0:HEADS, :]

        srcs = [pack, dws_ref, dcv_ref]
        outs = [rp_ref, rws_ref, rcv_ref]
        rxs = [rx_p, rx_ws, rx_cv]
        css = [cs_p, cs_ws, cs_cv]
        gs = [g_p, g_ws, g_cv]
        hl = [1, HEADS // 2, 1]

        def half(ref, a, h):
            return ref.at[pl.ds(h * hl[a], hl[a])]

        ex1 = [pltpu.make_async_remote_copy(src_ref=half(srcs[a], a, 1 - c), dst_ref=rxs[a], send_sem=s1_send.at[a],
                                            recv_sem=s1_recv.at[a], device_id=sibling, device_id_type=MESH) for a in range(n_a)]
        for cp in ex1:
            cp.start()
        ex2 = []
        for a in range(n_a):
            ex1[a].wait_recv()
            css[a][...] = half(srcs[a], a, c)[...] + rxs[a][...]
            for j, chip in enumerate(chips):
                cp = pltpu.make_async_remote_copy(src_ref=css[a], dst_ref=gs[a].at[j], send_sem=s2_send.at[a, j],
                                                  recv_sem=s2_recv.at[a, j], device_id=(*chip, c), device_id_type=MESH)
                cp.start()
                ex2.append(cp)
        ex3 = []
        for a in range(n_a):
            for j in range(3):
                ex2[3 * a + j].wait_recv()
            tot = None
            for q in range(4):
                k = jnp.where(x != (q >> 1), 1, 0) + jnp.where(y != (q & 1), 2, 0)
                term = jnp.where(k == 0, css[a][...], jnp.where(k == 1, gs[a][0], jnp.where(k == 2, gs[a][1], gs[a][2])))
                tot = term if tot is None else tot + term
            half(outs[a], a, c)[...] = tot
            cp = pltpu.make_async_remote_copy(src_ref=half(outs[a], a, c), dst_ref=half(outs[a], a, c), send_sem=s3_send.at[a],
                                              recv_sem=s3_recv.at[a], device_id=sibling, device_id_type=MESH)
            cp.start()
            ex3.append(cp)
        for a in range(n_a):
            pltpu.make_async_remote_copy(src_ref=half(outs[a], a, 1 - c), dst_ref=half(outs[a], a, 1 - c), send_sem=s3_send.at[a],
                                         recv_sem=s3_recv.at[a], device_id=sibling, device_id_type=MESH).wait_recv()
        for cp in ex1 + ex2 + ex3:
            cp.wait_send()

    vm = pl.BlockSpec(memory_space=pltpu.VMEM)
    full = [(2, 8, PACK_W), (HEADS, CHUNK, CHUNK), (2, 8, D_FF)]
    halves = [(s[0] // 2,) + s[1:] for s in full]
    return pl.pallas_call(
        body, name="ar_small",
        out_shape=tuple(jax.ShapeDtypeStruct(s, F32) for s in full),
        in_specs=[vm] * 10, out_specs=(vm,) * 3,
        scratch_shapes=[pltpu.VMEM(full[0], F32)] + [pltpu.VMEM(s, F32) for s in halves] + [pltpu.VMEM(s, F32) for s in halves]
        + [pltpu.VMEM((3,) + s, F32) for s in halves]
        + [pltpu.SemaphoreType.DMA((n_a,)), pltpu.SemaphoreType.DMA((n_a,)), pltpu.SemaphoreType.DMA((n_a, 3)),
           pltpu.SemaphoreType.DMA((n_a, 3)), pltpu.SemaphoreType.DMA((n_a,)), pltpu.SemaphoreType.DMA((n_a,))],
        compiler_params=_cparams(),
    )(dg1, dg2, dgf, dgrn, dlng, dlnb, dbs, loss_parts, dws, dcv)


def _adam_math(w, g, m, v):
    nm = ADAM_B1 * m + (1.0 - ADAM_B1) * g
    nv = ADAM_B2 * v + (1.0 - ADAM_B2) * (g * g)
    d = -ADAM_LR * ((nm / (1.0 - ADAM_B1 ** ADAM_STEP)) / (jnp.sqrt(nv / (1.0 - ADAM_B2 ** ADAM_STEP)) + ADAM_EPS) + ADAM_WD * w)
    return d, nm, nv


def _adamw(name, w, gs, m, v, rows):
    _, r, cdim = w.shape
    n_steps = r // rows
    half = gs[0].shape[0] // rows

    def body(w_ref, *rest):
        g_refs, (m_ref, v_ref, go_ref, d_ref, nm_ref, nv_ref) = rest[:len(gs)], rest[len(gs):]
        gg = g_refs[0][...]
        if len(gs) == 2:
            gg = jnp.where(pl.program_id(0) < half, gg, g_refs[1][...])
        go_ref[0] = gg
        d, nm, nv = _adam_math(w_ref[0], gg, m_ref[0], v_ref[0])
        d_ref[0], nm_ref[0], nv_ref[0] = d, nm, nv

    spec3 = pl.BlockSpec((1, rows, cdim), lambda i: (0, i, 0))
    if len(gs) == 1:
        g_specs = [pl.BlockSpec((rows, cdim), lambda i: (i, 0))]
    else:
        g_specs = [pl.BlockSpec((rows, cdim), lambda i: (jnp.minimum(i, half - 1), 0)),
                   pl.BlockSpec((rows, cdim), lambda i: (jnp.maximum(i - half, 0), 0))]
    sh = jax.ShapeDtypeStruct((1, r, cdim), F32)
    return pl.pallas_call(
        body, name=name, grid=(n_steps,), out_shape=(sh, sh, sh, sh),
        in_specs=[spec3] + g_specs + [spec3, spec3], out_specs=(spec3,) * 4,
        compiler_params=_cparams(("parallel",)),
    )(w, *gs, m, v)


def _adamw_small(rp, rws, rcv, gcw, params):
    n_p = len(params)

    def body(*refs):
        rp_ref, rws_ref, rcv_ref, gcw_ref = refs[:4]
        ins = refs[4:4 + 3 * n_p]
        outs = refs[4 + 3 * n_p:]
        grads = [rp_ref[0, 0:1, :], rp_ref[0, 1:2, :], rp_ref[0, 2:3, :], rp_ref[0, 3:4, 0:RET_W],
                 rp_ref[1, 0:HEADS, 0:128], rp_ref[1, 0:HEADS, 128:256], rp_ref[1, 0:HEADS, 256:384],
                 rws_ref[...], gcw_ref[0], None]
        for p in range(n_p):
            w_ref, m_ref, v_ref = ins[3 * p:3 * p + 3]
            o = outs[4 * p:4 * p + 4]
            if p == n_p - 1:
                for hf in range(2):
                    cs = slice(hf * D_FF, (hf + 1) * D_FF)
                    g = rcv_ref[hf, 3:4, :]
                    res = (g,) + _adam_math(w_ref[:, cs], g, m_ref[:, cs], v_ref[:, cs])
                    for t in range(4):
                        o[t][:, cs] = res[t]
                continue
            lead = w_ref.ndim > grads[p].ndim
            rd = (lambda r: r[0]) if lead else (lambda r: r[...])
            res = (grads[p],) + _adam_math(rd(w_ref), grads[p], rd(m_ref), rd(v_ref))
            for t in range(4):
                if lead:
                    o[t][0] = res[t]
                else:
                    o[t][...] = res[t]

    vm = pl.BlockSpec(memory_space=pltpu.VMEM)
    flat = [a for tr in params for a in tr]
    out_shape = tuple(jax.ShapeDtypeStruct(tr[0].shape, F32) for tr in params for _ in range(4))
    res = pl.pallas_call(
        body, name="adamw_small", out_shape=out_shape, in_specs=[vm] * (4 + len(flat)), out_specs=(vm,) * len(out_shape),
        compiler_params=_cparams(),
    )(rp, rws, rcv, gcw, *flat)
    return [res[4 * p:4 * p + 4] for p in range(n_p)]


def kernel(x, mix_norm_g, w_in, ret_norm_g, sgu_ln_g, sgu_ln_b, sgu_w_s, sgu_b_s, w_out, ffn_norm_g, w_up, conv_w, conv_b, w_down, final_norm_g, loss_target, m_mix_norm_g, m_w_in, m_ret_norm_g, m_sgu_ln_g, m_sgu_ln_b, m_sgu_w_s, m_sgu_b_s, m_w_out, m_ffn_norm_g, m_w_up, m_conv_w, m_conv_b, m_w_down, m_final_norm_g, v_mix_norm_g, v_w_in, v_ret_norm_g, v_sgu_ln_g, v_sgu_ln_b, v_sgu_w_s, v_sgu_b_s, v_w_out, v_ffn_norm_g, v_w_up, v_conv_w, v_conv_b, v_w_down, v_final_norm_g):
    xs = x[0]
    tgt = loss_target[0]
    cos2, sin2 = _rope_tables()
    mask, qdec, kdec = _decay_tables()
    grn = ret_norm_g.reshape(1, RET_W)
    lng = sgu_ln_g.reshape(1, SGU_W)
    lnb = sgu_ln_b.reshape(1, SGU_W)
    ws = sgu_w_s[0]
    bsb = jnp.broadcast_to(sgu_b_s[0][:, :, None], (HEADS, CHUNK, HEAD_DIM))
    gf = final_norm_g.reshape(1, D_MODEL)
    me = 4 * lax.axis_index("x") + 2 * lax.axis_index("y") + lax.axis_index("c")
    tr = lambda a: jnp.transpose(a[0])[None]

    proj, h1, win_g, cw_sh, wout_g, wdn_g, su = _fwd_proj(xs, mix_norm_g, cos2, sin2, w_in[0], w_out[0], tr(w_up)[0], w_down[0],
                                                         conv_w[0])
    cw_g = jnp.transpose(cw_sh, (1, 0, 2)).reshape(8, 2 * D_FF)
    x2, mixcat, o, sprev, wup_g = _fwd_mix(xs, proj, wout_g, grn, lng, lnb, ws, bsb, mask, qdec, kdec, su)
    h2, up_pre, u_conv, act, x3, loss_parts = _fwd_ffn(x2, ffn_norm_g, wup_g, cw_g, conv_b, wdn_g, gf, tgt)

    dx3, dpre, dx2, dgf, dg2, dcv = _bwd_ffn(x3, tgt, gf, x2, ffn_norm_g, up_pre, u_conv, wup_g, cw_g, wdn_g)
    band = 512
    (gdn_p,) = _wgrad("wgrad_down", act, dx3, tm=FF_TILE)
    (gout_p,) = _wgrad("wgrad_out", mixcat, dx2, tn=512)
    gup_p, g_dn, g_out = _wgrad("wgrad_up", dpre, h2, tm=FF_TILE, hosted=[(W_DOWN, gdn_p), (W_OUT, gout_p)])
    dproj, dgrn, dlng, dlnb, dws, dbs, g_up_a = _bwd_mix(
        dx2, proj, o, sprev, wout_g, grn, lng, lnb, ws, bsb, mask, qdec, kdec, cos2, sin2, [((W_UP, 0, band), gup_p)])
    gin_p, g_up_b = _wgrad("wgrad_in", h1, dproj, tn=768, hosted=[((W_UP, band, FF_SHARD - band), gup_p)])
    grad_x, dg1, g_in = _bwd_proj(dproj, win_g, xs, mix_norm_g, dx2, gin_p)
    rp, rws, rcv = _all_reduce_small(dg1, dg2, dgf, dgrn, dlng, dlnb, dbs, loss_parts, dws, dcv)
    loss = rp[0, 3, RET_W]
    gcw = lax.dynamic_slice(rcv, (me // (N_DEV // 2), 0, (me % (N_DEV // 2)) * FF_SHARD), (1, 3, FF_SHARD))

    table = {}
    for name, w, gs, m, v, rows in (("w_in", w_in, [g_in], m_w_in, v_w_in, 256), ("w_out", w_out, [g_out], m_w_out, v_w_out, 128),
                                    ("w_up", tr(w_up), [g_up_a, g_up_b], tr(m_w_up), tr(v_w_up), 64),
                                    ("w_down", w_down, [g_dn], m_w_down, v_w_down, 88)):
        table[name] = _adamw("adamw_" + name, w, gs, m, v, rows)
    table["w_up"] = tuple(tr(a) for a in table["w_up"])
    row = lambda a: a.reshape(1, D_MODEL)
    names_small = ["mix_norm_g", "ffn_norm_g", "final_norm_g", "ret_norm_g", "sgu_ln_g", "sgu_ln_b", "sgu_b_s", "sgu_w_s",
                   "conv_w", "conv_b"]
    params = [(mix_norm_g, m_mix_norm_g, v_mix_norm_g), (ffn_norm_g, m_ffn_norm_g, v_ffn_norm_g),
              (row(final_norm_g), row(m_final_norm_g), row(v_final_norm_g)), (ret_norm_g, m_ret_norm_g, v_ret_norm_g),
              (sgu_ln_g, m_sgu_ln_g, v_sgu_ln_g), (sgu_ln_b, m_sgu_ln_b, v_sgu_ln_b), (sgu_b_s, m_sgu_b_s, v_sgu_b_s),
              (sgu_w_s, m_sgu_w_s, v_sgu_w_s), (conv_w, m_conv_w, v_conv_w), (conv_b, m_conv_b, v_conv_b)]
    for n, res in zip(names_small, _adamw_small(rp, rws, rcv, gcw, params)):
        table[n] = res
    table["final_norm_g"] = tuple(a.reshape(D_MODEL) for a in table["final_norm_g"])

    order = ["mix_norm_g", "w_in", "ret_norm_g", "sgu_ln_g", "sgu_ln_b", "sgu_w_s", "sgu_b_s", "w_out", "ffn_norm_g", "w_up",
             "conv_w", "conv_b", "w_down", "final_norm_g"]
    outs = [loss, grad_x[None]]
    for col in range(4):
        outs += [table[n][col] for n in order]
    return tuple(outs)
```

```python
import functools
import math

import jax
import jax.numpy as jnp
import numpy as np
from jax import lax
from jax.experimental import pallas as pl
from jax.experimental.pallas import tpu as pltpu

F32 = jnp.float32
BF16 = jnp.bfloat16
MESH = pl.DeviceIdType.MESH

N_DEV = 8
SEQ = 2048
D_MODEL = 1024
CHUNK = 128
N_CHUNK = SEQ // CHUNK
HEADS = 4
HEAD_DIM = 128
RET_W = 512
SGU_W = 512
PROJ_W = 3072
D_FF = 2816
FF_SHARD = 704
FF_TILE = 1408
FF_TILES = ((0, 1536), (1536, 1280))
IN_SHARD = PROJ_W // N_DEV
OUT_SHARD = D_MODEL // N_DEV
DOWN_SHARD = D_FF // N_DEV
TM = 256
N_TB = SEQ // TM
EPS = 1e-6
ROPE_BASE = 10000.0
K_SCALE = HEAD_DIM ** -0.5
INV_SQRT2 = 0.7071067811865476
INV_SQRT_2PI = 0.3989422804014327

ADAM_LR = 0.001
ADAM_B1 = 0.9
ADAM_B2 = 0.999
ADAM_EPS = 1e-08
ADAM_WD = 0.01
ADAM_STEP = 10

VMEM_LIMIT = 56 * 1024 * 1024


def _cparams(sem=None, vmem=VMEM_LIMIT):
    return pltpu.CompilerParams(dimension_semantics=sem, vmem_limit_bytes=vmem)


def _resident(shape):
    nd = len(shape)
    return pl.BlockSpec(shape, lambda *_: (0,) * nd, pipeline_mode=pl.Buffered(1))


def _dot(a, b):
    return jnp.dot(a, b, preferred_element_type=F32)


def _dot_nt(a, b):
    return lax.dot_general(a, b, (((1,), (1,)), ((), ())), preferred_element_type=F32)


def _dot_tn(a, b):
    return lax.dot_general(a, b, (((0,), (0,)), ((), ())), preferred_element_type=F32)


def _sigmoid(x):
    return 1.0 / (1.0 + jnp.exp(-x))


def _gelu(x):
    return 0.5 * x * (1.0 + lax.erf(x * INV_SQRT2))


def _gelu_grad(x):
    return 0.5 * (1.0 + lax.erf(x * INV_SQRT2)) + x * (jnp.exp(-0.5 * x * x) * INV_SQRT_2PI)


def _rot(xh, cos2, sin2):
    return xh * cos2 + pltpu.roll(xh, HEAD_DIM // 2, 1) * sin2


def _rot_t(dh, cos2, sin2):
    return dh * cos2 + pltpu.roll(dh * sin2, HEAD_DIM // 2, 1)


def _rope_tables():
    half = HEAD_DIM // 2
    inv_freq = jnp.power(ROPE_BASE, -jnp.arange(half, dtype=F32) / half)
    ang = jnp.arange(SEQ, dtype=F32)[:, None] * inv_freq[None, :]
    cos, sin = jnp.cos(ang), jnp.sin(ang)
    cos2 = jnp.concatenate([cos, cos], axis=-1)
    sin2 = jnp.concatenate([-sin, sin], axis=-1)
    return cos2, sin2


def _decay_tables():
    log_gamma = jnp.log(1.0 - jnp.power(2.0, -5.0 - jnp.arange(HEADS, dtype=F32)))
    pos = jnp.arange(CHUNK, dtype=F32)
    diff = pos[:, None] - pos[None, :]
    mask = jnp.where(diff >= 0.0, jnp.exp(log_gamma[:, None, None] * jnp.maximum(diff, 0.0)[None]), 0.0)
    k_decay = jnp.exp(log_gamma[:, None] * (CHUNK - 1.0 - pos)[None])
    q_decay = jnp.exp(log_gamma[:, None] * (pos + 1.0)[None])
    kd = jnp.broadcast_to(k_decay[:, :, None], (HEADS, CHUNK, HEAD_DIM))
    qd = jnp.broadcast_to(q_decay[:, :, None], (HEADS, CHUNK, HEAD_DIM))
    return mask.astype(F32), qd.astype(F32), kd.astype(F32)


def _chunk_decay():
    lg = np.log(np.float32(1.0) - np.power(np.float32(2.0), -5.0 - np.arange(HEADS, dtype=np.float32))).astype(np.float32)
    return [float(np.exp(lg[h] * np.float32(CHUNK))) for h in range(HEADS)]


W_IN, W_OUT, W_UP, W_DOWN, W_CONV = range(5)
GATHERED = {W_IN: ((D_MODEL, PROJ_W), BF16), W_OUT: ((D_MODEL, D_MODEL), BF16), W_UP: ((2 * D_FF, D_MODEL), BF16),
            W_DOWN: ((D_FF, D_MODEL), BF16), W_CONV: ((N_DEV, 8, FF_SHARD), F32)}
SHARD = {W_IN: (D_MODEL, IN_SHARD), W_OUT: (OUT_SHARD, D_MODEL), W_UP: (FF_SHARD, D_MODEL), W_DOWN: (DOWN_SHARD, D_MODEL),
         W_CONV: (8, FF_SHARD)}


class _Gather:
    N_SEMS = 9

    def __init__(self, ids, stages, gathered, send_sems, recv_sems, local_sems):
        self.ids, self.stages, self.gathered = ids, stages, gathered
        self.send_sems, self.recv_sems, self.local_sems = send_sems, recv_sems, local_sems
        self.x, self.y, self.c = lax.axis_index("x"), lax.axis_index("y"), lax.axis_index("c")
        self.me = (self.x, self.y, self.c)
        self.sibling = (self.x, self.y, 1 - self.c)
        self.chips = [(1 - self.x, self.y), (self.x, 1 - self.y), (1 - self.x, 1 - self.y)]

    def slot(self, n, px, py, pc):
        dev = 4 * px + 2 * py + pc
        w, g = self.ids[n], self.gathered[n]
        if w == W_IN:
            return g.at[:, pl.ds(pl.multiple_of(dev * IN_SHARD, 128), IN_SHARD)]
        if w == W_OUT:
            return g.at[pl.ds(pl.multiple_of(dev * OUT_SHARD, 128), OUT_SHARD), :]
        if w == W_DOWN:
            return g.at[pl.ds(pl.multiple_of(dev * DOWN_SHARD, 32), DOWN_SHARD), :]
        if w == W_UP:
            return g.at[pl.ds(pl.multiple_of(dev * FF_SHARD, 32), FF_SHARD), :]
        return g.at[dev]

    def half(self, n, px, py, pc, h):
        dev = 4 * px + 2 * py + pc
        w, g = self.ids[n], self.gathered[n]
        if w == W_IN:
            return g.at[pl.ds(h * (D_MODEL // 2), D_MODEL // 2), pl.ds(pl.multiple_of(dev * IN_SHARD, 128), IN_SHARD)]
        rows = SHARD[w][0] // 2
        return g.at[pl.ds(pl.multiple_of(dev * SHARD[w][0] + h * rows, 16), rows), :]

    def tree(self, n):
        return self.ids[n] != W_CONV

    def copy(self, n, k, block, to, src=None, h=None):
        ref = self.slot(n, *block) if h is None else self.half(n, *block, h)
        return pltpu.make_async_remote_copy(
            src_ref=ref if src is None else src, dst_ref=ref,
            send_sem=self.send_sems.at[n, k], recv_sem=self.recv_sems.at[n, k], device_id=to, device_id_type=MESH)

    def _mine(self):
        return [pltpu.make_async_copy(self.stages[n], self.slot(n, *self.me), self.local_sems.at[n]) for n in range(len(self.ids))]

    def _first(self):
        out = []
        for n in range(len(self.ids)):
            out.append(self.copy(n, 0, self.me, self.sibling, src=self.stages[n]))
            out += [self.copy(n, 1 + j, self.me, (*chip, self.c), src=self.stages[n])
                    for j, chip in enumerate(self.chips[:2] if self.tree(n) else self.chips)]
        return out

    def start(self):
        for cp in self._mine() + self._first():
            cp.start()

    def finish(self):
        self.forward_near()
        self.forward_far()
        self.wait_all()

    def _go(self, cp):
        cp.start()
        self.passed.append(cp)

    def forward_near(self):
        cx, cy, cd = [(*chip, self.c) for chip in self.chips]
        self.passed = []
        go = self._go
        for n in range(len(self.ids)):
            if self.tree(n):
                self.copy(n, 1, cx, self.me).wait_recv()
                go(self.copy(n, 3, cx, cy, h=0))
                go(self.copy(n, 5, cx, self.sibling))
                self.copy(n, 2, cy, self.me).wait_recv()
                go(self.copy(n, 4, cy, cx, h=1))
                go(self.copy(n, 6, cy, self.sibling))
            else:
                for j, dev in enumerate((cx, cy, cd)):
                    self.copy(n, 1 + j, dev, self.me).wait_recv()
                    go(self.copy(n, 4 + j, dev, self.sibling))

    def forward_far(self):
        cd = (*self.chips[2], self.c)
        go = self._go
        for n in range(len(self.ids)):
            if self.tree(n):
                self.copy(n, 3, cd, self.me, h=0).wait_recv()
                go(self.copy(n, 7, cd, self.sibling, h=0))
                self.copy(n, 4, cd, self.me, h=1).wait_recv()
                go(self.copy(n, 8, cd, self.sibling, h=1))

    def wait_all(self):
        ox, oy, od = [(*chip, 1 - self.c) for chip in self.chips]
        for n in range(len(self.ids)):
            self.copy(n, 0, self.sibling, self.me).wait_recv()
            if self.tree(n):
                self.copy(n, 5, ox, self.me).wait_recv()
                self.copy(n, 6, oy, self.me).wait_recv()
                self.copy(n, 7, od, self.me, h=0).wait_recv()
                self.copy(n, 8, od, self.me, h=1).wait_recv()
            else:
                for j, dev in enumerate((ox, oy, od)):
                    self.copy(n, 4 + j, dev, self.me).wait_recv()
        for cp in self._first() + self.passed:
            cp.wait_send()
        for cp in self._mine():
            cp.wait()


def _gather_scratch(n):
    return [pltpu.SemaphoreType.DMA((n, _Gather.N_SEMS)), pltpu.SemaphoreType.DMA((n, _Gather.N_SEMS)), pltpu.SemaphoreType.DMA((n,))]


def _gathered_shapes(ids):
    return tuple(jax.ShapeDtypeStruct(*GATHERED[w]) for w in ids)


def _fwd_proj(x, g1, cos2, sin2, w_in, w_out, w_up, w_down, conv_w):
    ids_a, ids_b = [W_IN, W_CONV], [W_OUT, W_DOWN]

    def body(x_ref, g_ref, cos_ref, sin_ref, in_hbm, out_hbm, up_hbm, dn_hbm, cw_ref,
             proj_ref, h1_ref, gin, gcw, gout, gdn, su_ref,
             w_vm, s_in, s_cw, s_out, s_dn, f_in, f_out, f_up, f_dn, ld_sems,
             a_send, a_recv, a_local, b_send, b_recv, b_local):
        ag_a = _Gather(ids_a, [s_in, s_cw], [gin, gcw], a_send, a_recv, a_local)
        ag_b = _Gather(ids_b, [s_out, s_dn], [gout, gdn], b_send, b_recv, b_local)

        @pl.when(pl.program_id(0) == 0)
        def _():
            loads = [pltpu.make_async_copy(src, dst, ld_sems.at[i])
                     for i, (src, dst) in enumerate(((in_hbm, f_in), (out_hbm, f_out), (dn_hbm, f_dn), (up_hbm, f_up)))]
            for cp in loads:
                cp.start()
            s_cw[...] = jnp.zeros_like(s_cw)
            s_cw[0:3, :] = cw_ref[...]
            loads[0].wait()
            s_in[...] = f_in[...].astype(BF16)
            ag_a.start()
            loads[1].wait()
            s_out[...] = f_out[...].astype(BF16)
            loads[2].wait()
            s_dn[...] = f_dn[...].astype(BF16)
            ag_a.forward_near()
            ag_b.start()
            loads[3].wait()
            su_ref[...] = f_up[...].astype(BF16)
            ag_a.forward_far()
            ag_a.wait_all()
            fill = pltpu.make_async_copy(gin, w_vm, ld_sems.at[4])
            fill.start()
            fill.wait()

        xb = x_ref[...]
        r = lax.rsqrt(jnp.mean(xb * xb, axis=-1, keepdims=True) + EPS)
        h = ((xb * r) * g_ref[...]).astype(BF16)
        h1_ref[...] = h
        p = _dot(h, w_vm[...])
        for hd in range(HEADS):
            sl = slice(hd * HEAD_DIM, (hd + 1) * HEAD_DIM)
            c2, s2 = cos_ref[...], sin_ref[...]
            proj_ref[:, sl] = _rot(p[:, sl], c2, s2)
            ks = slice(RET_W + hd * HEAD_DIM, RET_W + (hd + 1) * HEAD_DIM)
            proj_ref[:, ks] = _rot(p[:, ks], c2, s2) * K_SCALE
        proj_ref[:, 2 * RET_W:] = p[:, 2 * RET_W:]

        @pl.when(pl.program_id(0) == N_TB - 1)
        def _():
            ag_b.finish()

    tok = lambda w: pl.BlockSpec((TM, w), lambda i: (i, 0))
    hbm = pl.BlockSpec(memory_space=pl.ANY)
    vm = pl.BlockSpec(memory_space=pltpu.VMEM)
    return pl.pallas_call(
        body, name="fwd_proj", grid=(N_TB,),
        out_shape=(jax.ShapeDtypeStruct((SEQ, PROJ_W), F32), jax.ShapeDtypeStruct((SEQ, D_MODEL), BF16))
        + _gathered_shapes(ids_a + ids_b) + (jax.ShapeDtypeStruct(SHARD[W_UP], BF16),),
        in_specs=[tok(D_MODEL), _resident((1, D_MODEL)), tok(HEAD_DIM), tok(HEAD_DIM), hbm, hbm, hbm, hbm, vm],
        out_specs=(tok(PROJ_W), tok(D_MODEL), hbm, hbm, hbm, hbm, vm),
        scratch_shapes=[pltpu.VMEM((D_MODEL, PROJ_W), BF16), pltpu.VMEM(SHARD[W_IN], BF16), pltpu.VMEM(SHARD[W_CONV], F32),
                        pltpu.VMEM(SHARD[W_OUT], BF16), pltpu.VMEM(SHARD[W_DOWN], BF16),
                        pltpu.VMEM(SHARD[W_IN], F32), pltpu.VMEM(SHARD[W_OUT], F32), pltpu.VMEM(SHARD[W_UP], F32),
                        pltpu.VMEM(SHARD[W_DOWN], F32), pltpu.SemaphoreType.DMA((5,))]
        + _gather_scratch(len(ids_a)) + _gather_scratch(len(ids_b)),
        compiler_params=_cparams(("arbitrary",)),
    )(x, g1, cos2, sin2, w_in, w_out, w_up, w_down, conv_w)


def _causal(w):
    r = lax.broadcasted_iota(jnp.int32, (CHUNK, CHUNK), 0)
    c = lax.broadcasted_iota(jnp.int32, (CHUNK, CHUNK), 1)
    return jnp.where(r >= c, w, 0.0)


def _fwd_mix(x, proj, wout_g, grn, lng, lnb, ws, bsb, mask, qdec, kdec, su):
    cdec = _chunk_decay()
    ids = [W_UP]

    def body(x_ref, p_ref, w_ref, grn_ref, lng_ref, lnb_ref, ws_ref, bsb_ref, m_ref, qd_ref, kd_ref, su_ref,
             x2_ref, cat_ref, o_ref, sp_ref, gup, state, send_sems, recv_sems, local_sems):
        ag = _Gather(ids, [su_ref], [gup], send_sems, recv_sems, local_sems)

        @pl.when(pl.program_id(0) == 0)
        def _():
            state[...] = jnp.zeros_like(state)
            ag.start()

        for h in range(HEADS):
            sl = slice(h * HEAD_DIM, (h + 1) * HEAD_DIM)
            q = p_ref[:, sl]
            k = p_ref[:, RET_W + h * HEAD_DIM:RET_W + (h + 1) * HEAD_DIM]
            v = p_ref[:, 2 * RET_W + h * HEAD_DIM:2 * RET_W + (h + 1) * HEAD_DIM]
            g = p_ref[:, 3 * RET_W + h * HEAD_DIM:3 * RET_W + (h + 1) * HEAD_DIM]
            qb, kb, vb = q.astype(BF16), k.astype(BF16), v.astype(BF16)
            a = _dot_nt(qb, kb) * m_ref[h]
            spb = state[h].astype(BF16)
            sp_ref[0, h] = spb
            o = _dot(a.astype(BF16), vb) + _dot((q * qd_ref[h]).astype(BF16), spb)
            state[h] = state[h] * cdec[h] + _dot_tn((k * kd_ref[h]).astype(BF16), vb)
            o_ref[:, sl] = o
            rinv = lax.rsqrt(jnp.mean(o * o, axis=-1, keepdims=True) + EPS)
            rn = (o * rinv) * grn_ref[:, sl]
            cat_ref[:, sl] = ((g * _sigmoid(g)) * rn).astype(BF16)
        for gi in range(HEADS):
            sl = slice(gi * HEAD_DIM, (gi + 1) * HEAD_DIM)
            u = p_ref[:, 4 * RET_W + gi * HEAD_DIM:4 * RET_W + (gi + 1) * HEAD_DIM]
            sv = p_ref[:, 4 * RET_W + SGU_W + gi * HEAD_DIM:4 * RET_W + SGU_W + (gi + 1) * HEAD_DIM]
            gv = _gelu(sv)
            xc = gv - jnp.mean(gv, axis=-1, keepdims=True)
            vn = (xc * lax.rsqrt(jnp.mean(xc * xc, axis=-1, keepdims=True) + EPS)) * lng_ref[:, sl] + lnb_ref[:, sl]
            mixed = _dot(_causal(ws_ref[gi]).astype(BF16), vn.astype(BF16)) + bsb_ref[gi]
            cat_ref[:, RET_W + gi * HEAD_DIM:RET_W + (gi + 1) * HEAD_DIM] = (_gelu(u) * mixed).astype(BF16)
        x2_ref[...] = x_ref[...] + _dot(cat_ref[...], w_ref[...])

        @pl.when(pl.program_id(0) == N_CHUNK - 1)
        def _():
            ag.finish()

    ch = lambda w: pl.BlockSpec((CHUNK, w), lambda i: (i, 0))
    hcc = (HEADS, CHUNK, CHUNK)
    hbm = pl.BlockSpec(memory_space=pl.ANY)
    return pl.pallas_call(
        body, name="fwd_mix", grid=(N_CHUNK,),
        out_shape=(jax.ShapeDtypeStruct((SEQ, D_MODEL), F32), jax.ShapeDtypeStruct((SEQ, D_MODEL), BF16),
                   jax.ShapeDtypeStruct((SEQ, RET_W), F32), jax.ShapeDtypeStruct((N_CHUNK, HEADS, HEAD_DIM, HEAD_DIM), BF16))
        + _gathered_shapes(ids),
        in_specs=[ch(D_MODEL), ch(PROJ_W), _resident((D_MODEL, D_MODEL)), _resident((1, RET_W)), _resident((1, SGU_W)),
                  _resident((1, SGU_W)), _resident(hcc), _resident(hcc), _resident(hcc), _resident(hcc), _resident(hcc), hbm],
        out_specs=(ch(D_MODEL), ch(D_MODEL), ch(RET_W), pl.BlockSpec((1, HEADS, HEAD_DIM, HEAD_DIM), lambda i: (i, 0, 0, 0)), hbm),
        scratch_shapes=[pltpu.VMEM((HEADS, HEAD_DIM, HEAD_DIM), F32)] + _gather_scratch(len(ids)),
        compiler_params=_cparams(("arbitrary",)),
    )(x, proj, wout_g, grn, lng, lnb, ws, bsb, mask, qdec, kdec, su)


def _conv_taps(p, prev8):
    row = lax.broadcasted_iota(jnp.int32, p.shape, 0)
    p1 = jnp.where(row == 0, prev8[7:8, :], pltpu.roll(p, 1, 0))
    p2 = jnp.where(row == 0, prev8[6:7, :], jnp.where(row == 1, prev8[7:8, :], pltpu.roll(p, 2, 0)))
    return p1, p2


def _fwd_ffn(x2, g2, wup_g, cw_g, cb_g, wdn_g, gf, tgt):
    def body(x_ref, g_ref, wu_ref, cw_ref, cb_ref, wd_ref, gf_ref, t_ref, h2_ref, up_ref, u_ref, act_ref, x3_ref, loss_ref, carry):
        @pl.when(pl.program_id(0) == 0)
        def _():
            carry[...] = jnp.zeros_like(carry)

        xb = x_ref[...]
        r = lax.rsqrt(jnp.mean(xb * xb, axis=-1, keepdims=True) + EPS)
        h = ((xb * r) * g_ref[...]).astype(BF16)
        h2_ref[...] = h
        acc = xb
        for t0, tw in FF_TILES:
            u = []
            for c0 in (t0, D_FF + t0):
                cs = slice(c0, c0 + tw)
                p = _dot_nt(h, wu_ref[pl.ds(c0, tw), :])
                up_ref[:, cs] = p.astype(BF16)
                p1, p2 = _conv_taps(p, carry[:, cs])
                carry[:, cs] = p[TM - 8:, :]
                us = p2 * cw_ref[0:1, cs] + p1 * cw_ref[1:2, cs] + p * cw_ref[2:3, cs] + cb_ref[:, cs]
                u_ref[:, cs] = us.astype(BF16)
                u.append(us)
            a = ((u[0] * _sigmoid(u[0])) * u[1]).astype(BF16)
            act_ref[:, t0:t0 + tw] = a
            acc = acc + _dot(a, wd_ref[pl.ds(t0, tw), :])
        x3_ref[...] = acc
        r3 = lax.rsqrt(jnp.mean(acc * acc, axis=-1, keepdims=True) + EPS)
        diff = (acc * r3) * gf_ref[...] - t_ref[...]
        loss_ref[...] = jnp.full(loss_ref.shape, 0.5 * jnp.sum(jnp.mean(diff * diff, axis=-1)), F32)

    tok = lambda w: pl.BlockSpec((TM, w), lambda i: (i, 0))
    return pl.pallas_call(
        body, name="fwd_ffn", grid=(N_TB,),
        out_shape=(jax.ShapeDtypeStruct((SEQ, D_MODEL), BF16), jax.ShapeDtypeStruct((SEQ, 2 * D_FF), BF16),
                   jax.ShapeDtypeStruct((SEQ, 2 * D_FF), BF16),
                   jax.ShapeDtypeStruct((SEQ, D_FF), BF16), jax.ShapeDtypeStruct((SEQ, D_MODEL), F32),
                   jax.ShapeDtypeStruct((N_TB, 8, 128), F32)),
        in_specs=[tok(D_MODEL), _resident((1, D_MODEL)), _resident((2 * D_FF, D_MODEL)), _resident((8, 2 * D_FF)),
                  _resident((1, 2 * D_FF)), _resident((D_FF, D_MODEL)), _resident((1, D_MODEL)), tok(D_MODEL)],
        out_specs=(tok(D_MODEL), tok(2 * D_FF), tok(2 * D_FF), tok(D_FF), tok(D_MODEL),
                   pl.BlockSpec((1, 8, 128), lambda i: (i, 0, 0))),
        scratch_shapes=[pltpu.VMEM((8, 2 * D_FF), F32)],
        compiler_params=_cparams(("arbitrary",)),
    )(x2, g2, wup_g, cw_g, cb_g, wdn_g, gf, tgt)


def _bwd_ffn(x3, tgt, gf, x2, g2, up_pre, u_conv, wup_g, cw_g, wdn_g):
    def body(x3_ref, t_ref, gf_ref, x2_ref, g2_ref, up_ref, u_ref, wu_ref, cw_ref, wd_ref,
             dx3_ref, dpre_ref, dx2_ref, dgf_ref, dg2_ref, dcv_ref, nxt):
        i = pl.program_id(0)

        @pl.when(i == 0)
        def _():
            nxt[...] = jnp.zeros_like(nxt)
            dgf_ref[...] = jnp.zeros_like(dgf_ref)
            dg2_ref[...] = jnp.zeros_like(dg2_ref)
            dcv_ref[...] = jnp.zeros_like(dcv_ref)

        x3 = x3_ref[...]
        r3 = lax.rsqrt(jnp.mean(x3 * x3, axis=-1, keepdims=True) + EPS)
        xh3 = x3 * r3
        dy = (xh3 * gf_ref[...] - t_ref[...]) * (1.0 / D_MODEL)
        dgf_ref[0:1, :] += jnp.sum(dy * xh3, axis=0, keepdims=True)
        t3 = dy * gf_ref[...]
        dx3 = r3 * (t3 - xh3 * jnp.mean(t3 * xh3, axis=-1, keepdims=True))
        dx3b = dx3.astype(BF16)
        dx3_ref[...] = dx3b
        dh2 = jnp.zeros((TM, D_MODEL), F32)
        for t0, tw in FF_TILES:
            row = lax.broadcasted_iota(jnp.int32, (TM, tw), 0)
            ts = slice(t0, t0 + tw)
            dact = _dot_nt(dx3b, wd_ref[pl.ds(t0, tw), :])
            ua = u_ref[:, ts].astype(F32)
            ub = u_ref[:, D_FF + t0:D_FF + t0 + tw].astype(F32)
            sg = _sigmoid(ua)
            du = [dact * ub * (sg * (1.0 + ua * (1.0 - sg))), dact * (ua * sg)]
            for n in range(2):
                d = du[n]
                c0 = n * D_FF + t0
                cs = slice(c0, c0 + tw)
                nx = nxt[:, cs]
                n1 = jnp.where(row == TM - 1, nx[0:1, :], pltpu.roll(d, TM - 1, 0))
                n2 = jnp.where(row == TM - 2, nx[0:1, :], jnp.where(row == TM - 1, nx[1:2, :], pltpu.roll(d, TM - 2, 0)))
                nxt[:, cs] = d[0:8, :]
                dp = (d * cw_ref[2:3, cs] + n1 * cw_ref[1:2, cs] + n2 * cw_ref[0:1, cs]).astype(BF16)
                dpre_ref[:, cs] = dp
                p = up_ref[:, cs].astype(F32)
                dcv_ref[n, 0:1, ts] += jnp.sum(n2 * p, axis=0, keepdims=True)
                dcv_ref[n, 1:2, ts] += jnp.sum(n1 * p, axis=0, keepdims=True)
                dcv_ref[n, 2:3, ts] += jnp.sum(d * p, axis=0, keepdims=True)
                dcv_ref[n, 3:4, ts] += jnp.sum(d, axis=0, keepdims=True)
                dh2 = dh2 + _dot(dp, wu_ref[pl.ds(c0, tw), :])
        x2 = x2_ref[...]
        r2 = lax.rsqrt(jnp.mean(x2 * x2, axis=-1, keepdims=True) + EPS)
        xh2 = x2 * r2
        dg2_ref[0:1, :] += jnp.sum(dh2 * xh2, axis=0, keepdims=True)
        t2 = dh2 * g2_ref[...]
        dx2_ref[...] = dx3 + r2 * (t2 - xh2 * jnp.mean(t2 * xh2, axis=-1, keepdims=True))

    rev = lambda w: pl.BlockSpec((TM, w), lambda i: (N_TB - 1 - i, 0))
    acc = lambda s: pl.BlockSpec(s, lambda i: (0,) * len(s))
    return pl.pallas_call(
        body, name="bwd_ffn", grid=(N_TB,),
        out_shape=(jax.ShapeDtypeStruct((SEQ, D_MODEL), BF16), jax.ShapeDtypeStruct((SEQ, 2 * D_FF), BF16),
                   jax.ShapeDtypeStruct((SEQ, D_MODEL), F32), jax.ShapeDtypeStruct((8, D_MODEL), F32),
                   jax.ShapeDtypeStruct((8, D_MODEL), F32), jax.ShapeDtypeStruct((2, 8, D_FF), F32)),
        in_specs=[rev(D_MODEL), rev(D_MODEL), _resident((1, D_MODEL)), rev(D_MODEL), _resident((1, D_MODEL)), rev(2 * D_FF),
                  rev(2 * D_FF), _resident((2 * D_FF, D_MODEL)), _resident((8, 2 * D_FF)), _resident((D_FF, D_MODEL))],
        out_specs=(rev(D_MODEL), rev(2 * D_FF), rev(D_MODEL), acc((8, D_MODEL)), acc((8, D_MODEL)), acc((2, 8, D_FF))),
        scratch_shapes=[pltpu.VMEM((8, 2 * D_FF), F32)],
        compiler_params=_cparams(("arbitrary",)),
    )(x3, tgt, gf, x2, g2, up_pre, u_conv, wup_g, cw_g, wdn_g)


def _bwd_mix(dx2, proj, o, sprev, wout_g, grn, lng, lnb, ws, bsb, mask, qdec, kdec, cos2, sin2, hosted):
    cdec = _chunk_decay()
    geoms = [g for g, _ in hosted]
    n_h = len(hosted)

    def body(dx2_ref, p_ref, o_ref, sp_ref, w_ref, grn_ref, lng_ref, lnb_ref, ws_ref, bsb_ref, m_ref, qd_ref, kd_ref,
             cos_ref, sin_ref, *rest):
        dp_ref, dgrn_ref, dlng_ref, dlnb_ref, dws_ref, dbs_ref = rest[n_h:n_h + 6]
        dstate, dbs_acc = rest[2 * n_h + 6:2 * n_h + 8]
        i = pl.program_id(0)
        rs = _Scatters(geoms, rest[:n_h], rest[n_h + 6:2 * n_h + 6], rest[2 * n_h + 8:])
        pl.when(i == 0)(rs.phase1)
        pl.when(i == 3)(rs.phase2)
        pl.when(i == 6)(rs.phase2b)

        @pl.when(i == 0)
        def _():
            dstate[...] = jnp.zeros_like(dstate)
            dgrn_ref[...] = jnp.zeros_like(dgrn_ref)
            dlng_ref[...] = jnp.zeros_like(dlng_ref)
            dlnb_ref[...] = jnp.zeros_like(dlnb_ref)
            dws_ref[...] = jnp.zeros_like(dws_ref)
            dbs_ref[...] = jnp.zeros_like(dbs_ref)
            dbs_acc[...] = jnp.zeros_like(dbs_acc)

        dmix = _dot_nt(dx2_ref[...].astype(BF16), w_ref[...])
        for h in range(HEADS):
            sl = slice(h * HEAD_DIM, (h + 1) * HEAD_DIM)
            q = p_ref[:, sl]
            k = p_ref[:, RET_W + h * HEAD_DIM:RET_W + (h + 1) * HEAD_DIM]
            v = p_ref[:, 2 * RET_W + h * HEAD_DIM:2 * RET_W + (h + 1) * HEAD_DIM]
            g = p_ref[:, 3 * RET_W + h * HEAD_DIM:3 * RET_W + (h + 1) * HEAD_DIM]
            o = o_ref[:, sl]
            rinv = lax.rsqrt(jnp.mean(o * o, axis=-1, keepdims=True) + EPS)
            oh = o * rinv
            gr = grn_ref[:, sl]
            sg = _sigmoid(g)
            dret = dmix[:, sl]
            dp_ref[:, 3 * RET_W + h * HEAD_DIM:3 * RET_W + (h + 1) * HEAD_DIM] = (
                dret * (oh * gr) * (sg * (1.0 + g * (1.0 - sg)))).astype(BF16)
            drn = dret * (g * sg)
            dgrn_ref[0:1, sl] += jnp.sum(drn * oh, axis=0, keepdims=True)
            t = drn * gr
            do = rinv * (t - oh * jnp.mean(t * oh, axis=-1, keepdims=True))
            qb, kb, vb, dob = q.astype(BF16), k.astype(BF16), v.astype(BF16), do.astype(BF16)
            m = m_ref[h]
            ab = (_dot_nt(qb, kb) * m).astype(BF16)
            dab = (_dot_nt(dob, vb) * m).astype(BF16)
            spb = sp_ref[0, h]
            dsn = dstate[h]
            dsnb = dsn.astype(BF16)
            qdb = (q * qd_ref[h]).astype(BF16)
            kdb = (k * kd_ref[h]).astype(BF16)
            dq = _dot(dab, kb) + _dot_nt(dob, spb) * qd_ref[h]
            dk = _dot_tn(dab, qb) + _dot_nt(vb, dsnb) * kd_ref[h]
            dv = _dot_tn(ab, dob) + _dot(kdb, dsnb)
            dstate[h] = dsn * cdec[h] + _dot_tn(qdb, dob)
            c2, s2 = cos_ref[...], sin_ref[...]
            dp_ref[:, sl] = _rot_t(dq, c2, s2).astype(BF16)
            dp_ref[:, RET_W + h * HEAD_DIM:RET_W + (h + 1) * HEAD_DIM] = _rot_t(dk * K_SCALE, c2, s2).astype(BF16)
            dp_ref[:, 2 * RET_W + h * HEAD_DIM:2 * RET_W + (h + 1) * HEAD_DIM] = dv.astype(BF16)
        for gi in range(HEADS):
            sl = slice(gi * HEAD_DIM, (gi + 1) * HEAD_DIM)
            u = p_ref[:, 4 * RET_W + gi * HEAD_DIM:4 * RET_W + (gi + 1) * HEAD_DIM]
            sv = p_ref[:, 4 * RET_W + SGU_W + gi * HEAD_DIM:4 * RET_W + SGU_W + (gi + 1) * HEAD_DIM]
            gv = _gelu(sv)
            xc = gv - jnp.mean(gv, axis=-1, keepdims=True)
            rstd = lax.rsqrt(jnp.mean(xc * xc, axis=-1, keepdims=True) + EPS)
            xh = xc * rstd
            lg = lng_ref[:, sl]
            vnb = (xh * lg + lnb_ref[:, sl]).astype(BF16)
            wcb = _causal(ws_ref[gi]).astype(BF16)
            mixed = _dot(wcb, vnb) + bsb_ref[gi]
            dsgu = dmix[:, RET_W + gi * HEAD_DIM:RET_W + (gi + 1) * HEAD_DIM]
            dmixed = dsgu * _gelu(u)
            dmb = dmixed.astype(BF16)
            dws_ref[gi] += _causal(_dot_nt(dmb, vnb))
            dbs_acc[gi] += dmixed
            dvn = _dot_tn(wcb, dmb)
            dlng_ref[gi:gi + 1, :] += jnp.sum(dvn * xh, axis=0, keepdims=True)
            dlnb_ref[gi:gi + 1, :] += jnp.sum(dvn, axis=0, keepdims=True)
            dxh = dvn * lg
            dgv = rstd * (dxh - jnp.mean(dxh, axis=-1, keepdims=True) - xh * jnp.mean(dxh * xh, axis=-1, keepdims=True))
            dp_ref[:, 4 * RET_W + gi * HEAD_DIM:4 * RET_W + (gi + 1) * HEAD_DIM] = (dsgu * mixed * _gelu_grad(u)).astype(BF16)
            dp_ref[:, 4 * RET_W + SGU_W + gi * HEAD_DIM:4 * RET_W + SGU_W + (gi + 1) * HEAD_DIM] = (
                dgv * _gelu_grad(sv)).astype(BF16)

        @pl.when(i == N_CHUNK - 1)
        def _():
            for gi in range(HEADS):
                col = jnp.broadcast_to(jnp.sum(dbs_acc[gi], axis=-1, keepdims=True), (CHUNK, CHUNK))
                dbs_ref[gi:gi + 1, :] = jnp.transpose(col)[0:1, :]
            rs.phase3()

    rev = lambda w: pl.BlockSpec((CHUNK, w), lambda i: (N_CHUNK - 1 - i, 0))
    hcc = (HEADS, CHUNK, CHUNK)
    acc = lambda s: pl.BlockSpec(s, lambda i: (0,) * len(s))
    res = pl.pallas_call(
        body, name="bwd_mix", grid=(N_CHUNK,),
        out_shape=(jax.ShapeDtypeStruct((SEQ, PROJ_W), BF16), jax.ShapeDtypeStruct((8, RET_W), F32),
                   jax.ShapeDtypeStruct((8, HEAD_DIM), F32), jax.ShapeDtypeStruct((8, HEAD_DIM), F32),
                   jax.ShapeDtypeStruct(hcc, F32), jax.ShapeDtypeStruct((8, CHUNK), F32)) + _scatter_out_shapes(geoms),
        in_specs=[rev(D_MODEL), rev(PROJ_W), rev(RET_W),
                  pl.BlockSpec((1, HEADS, HEAD_DIM, HEAD_DIM), lambda i: (N_CHUNK - 1 - i, 0, 0, 0)),
                  _resident((D_MODEL, D_MODEL)), _resident((1, RET_W)), _resident((1, SGU_W)), _resident((1, SGU_W)),
                  _resident(hcc), _resident(hcc), _resident(hcc), _resident(hcc), _resident(hcc), rev(HEAD_DIM), rev(HEAD_DIM)]
        + [pl.BlockSpec(memory_space=pl.ANY)] * n_h,
        out_specs=(rev(PROJ_W), acc((8, RET_W)), acc((8, HEAD_DIM)), acc((8, HEAD_DIM)), acc(hcc), acc((8, CHUNK)))
        + _scatter_out_specs(geoms),
        scratch_shapes=[pltpu.VMEM((HEADS, HEAD_DIM, HEAD_DIM), F32), pltpu.VMEM((HEADS, CHUNK, CHUNK), F32)] + _scatter_scratch(geoms),
        compiler_params=_cparams(("arbitrary",)),
    )(dx2, proj, o, sprev, wout_g, grn, lng, lnb, ws, bsb, mask, qdec, kdec, cos2, sin2, *[p for _, p in hosted])
    return tuple(res[:6 + n_h])


def _bwd_proj(dproj, win_g, x, g1, dx2, gin_p, small):
    geoms = [W_IN]
    n_s = len(small)

    def body(dp_ref, w_ref, x_ref, g_ref, dx2_ref, gin_ref, *rest):
        small_refs = rest[:n_s]
        dx_ref, rs_out, rp_ref, rws_ref, rcv_ref, dg_ref = rest[n_s:n_s + 6]
        rs_scratch = rest[n_s + 6:n_s + 6 + N_SCATTER_SCRATCH]
        ar_scratch = rest[n_s + 6 + N_SCATTER_SCRATCH:]
        ar_res = ar_scratch[N_SMALL_SCRATCH:]
        ar = _SmallReduce((dg_ref,) + tuple(small_refs), ar_res, ar_scratch[:N_SMALL_SCRATCH])
        rs = _Scatters(geoms, [gin_ref], [rs_out], rs_scratch)
        pl.when(pl.program_id(0) == 0)(rs.phase1)
        pl.when(pl.program_id(0) == 2)(rs.phase2)
        pl.when(pl.program_id(0) == 4)(rs.phase2b)

        @pl.when(pl.program_id(0) == 0)
        def _():
            dg_ref[...] = jnp.zeros_like(dg_ref)

        dh = _dot_nt(dp_ref[...], w_ref[...])
        xb = x_ref[...]
        r = lax.rsqrt(jnp.mean(xb * xb, axis=-1, keepdims=True) + EPS)
        xh = xb * r
        dg_ref[0:1, :] += jnp.sum(dh * xh, axis=0, keepdims=True)
        t = dh * g_ref[...]
        dx_ref[...] = dx2_ref[...] + r * (t - xh * jnp.mean(t * xh, axis=-1, keepdims=True))

        @pl.when(pl.program_id(0) == N_TB - 1)
        def _():
            ar.begin()
            rs.phase3()
            ar.end()
            for o_ref, r_ref in zip((rp_ref, rws_ref, rcv_ref), ar_res):
                o_ref[...] = r_ref[...]

    tok = lambda w: pl.BlockSpec((TM, w), lambda i: (i, 0))
    vm = pl.BlockSpec(memory_space=pltpu.VMEM)
    res = pl.pallas_call(
        body, name="bwd_proj", grid=(N_TB,),
        out_shape=(jax.ShapeDtypeStruct((SEQ, D_MODEL), F32),) + _scatter_out_shapes(geoms)
        + tuple(jax.ShapeDtypeStruct(s, F32) for s in SMALL_FULL),
        in_specs=[tok(PROJ_W), _resident((D_MODEL, PROJ_W)), tok(D_MODEL), _resident((1, D_MODEL)), tok(D_MODEL),
                  pl.BlockSpec(memory_space=pl.ANY)] + [vm] * n_s,
        out_specs=(tok(D_MODEL),) + _scatter_out_specs(geoms) + (vm,) * len(SMALL_FULL),
        scratch_shapes=[pltpu.VMEM((8, D_MODEL), F32)] + _scatter_scratch(geoms) + _small_scratch()
        + [pltpu.VMEM(s, F32) for s in SMALL_FULL],
        compiler_params=_cparams(("arbitrary",)),
    )(dproj, win_g, x, g1, dx2, gin_p, *small)
    return res


def _wgrad(name, a, b, tm=None, tn=None, hosted=()):
    m_w, n_w = a.shape[-1], b.shape[-1]
    tm = m_w if tm is None else tm
    tn = n_w if tn is None else tn
    n_steps = (m_w // tm) * (n_w // tn)
    geoms = [g for g, _ in hosted]
    n_h = len(hosted)

    def body(a_ref, b_ref, *rest):
        o_ref = rest[n_h]
        if n_h:
            rs = _Scatters(geoms, rest[:n_h], rest[n_h + 1:2 * n_h + 1], rest[2 * n_h + 1:])
            step = pl.program_id(0) * (n_w // tn) + pl.program_id(1)
            pl.when(step == 0)(rs.phase1)
            pl.when(step == 1)(rs.phase2)
            pl.when(step == 2)(rs.phase2b)
        o_ref[...] = _dot_tn(a_ref[...].astype(BF16), b_ref[...].astype(BF16)).astype(BF16)
        if n_h:
            pl.when(step == n_steps - 1)(rs.phase3)

    assert not n_h or n_steps >= 4
    res = pl.pallas_call(
        body, name=name, grid=(m_w // tm, n_w // tn),
        out_shape=(jax.ShapeDtypeStruct((m_w, n_w), BF16),) + _scatter_out_shapes(geoms),
        in_specs=[pl.BlockSpec((SEQ, tm), lambda i, j: (0, i)), pl.BlockSpec((SEQ, tn), lambda i, j: (0, j))]
        + [pl.BlockSpec(memory_space=pl.ANY)] * n_h,
        out_specs=(pl.BlockSpec((tm, tn), lambda i, j: (i, j)),) + _scatter_out_specs(geoms),
        scratch_shapes=_scatter_scratch(geoms),
        compiler_params=_cparams(("arbitrary", "arbitrary") if n_h else ("parallel", "parallel")),
    )(a, b, *[p for _, p in hosted])
    return tuple(res[:1 + n_h])


def _row_step(half_rows):
    return max(s for s in range(16, 177, 16) if half_rows % s == 0)


class _Scatter:
    def __init__(self, geom, partial, out, land1, mine, stage2, land2, comb, s1_send, s1_recv, s2_send, s2_recv, ld_sems):
        self.w, self.row0, self.shape = _geom(geom)
        self.partial, self.out, self.land1 = partial, out, land1
        self.mine, self.stage2, self.land2, self.comb = mine, stage2, land2, comb
        self.hr = self.shape[0] // 2
        self.step = _row_step(self.hr)
        self.s1_send, self.s1_recv, self.s2_send, self.s2_recv, self.ld_sems = s1_send, s1_recv, s2_send, s2_recv, ld_sems
        self.x, self.y, self.c = lax.axis_index("x"), lax.axis_index("y"), lax.axis_index("c")
        self.sibling = (self.x, self.y, 1 - self.c)
        self.chips = [(self.x, self.y), (1 - self.x, self.y), (self.x, 1 - self.y), (1 - self.x, 1 - self.y)]

    def block(self, px, py, pc):
        dev = 4 * px + 2 * py + pc
        if self.w == W_IN:
            return self.partial.at[:, pl.ds(pl.multiple_of(dev * IN_SHARD, 128), IN_SHARD)]
        if self.w == W_OUT:
            return self.partial.at[pl.ds(pl.multiple_of(dev * OUT_SHARD, 128), OUT_SHARD), :]
        if self.w == W_DOWN:
            return self.partial.at[pl.ds(pl.multiple_of(dev * DOWN_SHARD, 32), DOWN_SHARD), :]
        return self.partial.at[pl.ds(pl.multiple_of(dev * FF_SHARD + self.row0, 32), self.shape[0]), :]

    def copy1(self, k):
        return pltpu.make_async_remote_copy(
            src_ref=self.block(*self.chips[k], 1 - self.c), dst_ref=self.land1.at[k],
            send_sem=self.s1_send.at[k], recv_sem=self.s1_recv.at[k], device_id=self.sibling, device_id_type=MESH)

    STAGE2 = [(1, 0, 1), (3, 0, 1), (2, 1, 2), (3, 1, 2), (1, 1, 1), (2, 0, 2)]

    def copy2(self, j):
        blk, h, to = self.STAGE2[j]
        src = self.comb.at[j - 4] if j >= 4 else self.stage2.at[blk - 1, pl.ds(h * self.hr, self.hr), :]
        return pltpu.make_async_remote_copy(
            src_ref=src, dst_ref=self.land2.at[j], send_sem=self.s2_send.at[j], recv_sem=self.s2_recv.at[j],
            device_id=(*self.chips[to], self.c), device_id_type=MESH)

    def _rows(self, h=None):
        step = self.step
        lo, n = (0, self.shape[0]) if h is None else (h * self.hr, self.hr)
        return [pl.ds(r0, step) for r0 in range(lo, lo + n, step)]

    def load(self, k):
        return pltpu.make_async_copy(self.block(*self.chips[k], self.c), self.mine.at[k], self.ld_sems.at[k])

    def phase1(self):
        for k in range(4):
            self.copy1(k).start()
        for k in range(4):
            self.load(k).start()

    def phase2(self):
        for k in (3, 1, 2, 0):
            self.copy1(k).wait_recv()
            self.load(k).wait()
            for rs in self._rows():
                s = self.mine[k, rs, :].astype(F32) + self.land1[k, rs, :].astype(F32)
                if k == 0:
                    self.out[rs, :] = s
                else:
                    self.stage2[k - 1, rs, :] = s.astype(BF16)
            for j in {3: (1, 3), 1: (0,), 2: (2,), 0: ()}[k]:
                self.copy2(j).start()

    def phase2b(self):
        for j, got in ((4, 3), (5, 1)):
            blk, h, _ = self.STAGE2[j]
            self.copy2(got).wait_recv()
            for i, rs in enumerate(self._rows(h)):
                lr = pl.ds(i * self.step, self.step)
                self.comb[j - 4, lr, :] = (self.stage2[blk - 1, rs, :].astype(F32) + self.land2[got, lr, :].astype(F32)).astype(BF16)
            self.copy2(j).start()

    def phase3(self):
        for j in (0, 5, 4, 2):
            self.copy2(j).wait_recv()
        for h, (first, second) in enumerate(((0, 5), (4, 2))):
            for i, rs in enumerate(self._rows(h)):
                lr = pl.ds(i * self.step, self.step)
                self.out[rs, :] = (self.out[rs, :] + self.land2[first, lr, :].astype(F32)) + self.land2[second, lr, :].astype(F32)
        for k in range(4):
            self.copy1(k).wait_send()
        for j in range(6):
            self.copy2(j).wait_send()


def _geom(geom):
    if isinstance(geom, tuple):
        w, row0, rows = geom
        assert w == W_UP
        return w, row0, (rows, SHARD[w][1])
    return geom, 0, SHARD[geom]


N_SCATTER_SCRATCH = 10


def _scatter_out_shapes(geoms):
    return tuple(jax.ShapeDtypeStruct(_geom(g)[2], F32) for g in geoms)


def _scatter_out_specs(geoms):
    return (pl.BlockSpec(memory_space=pltpu.VMEM),) * len(geoms)


def _scatter_scratch(geoms):
    out = []
    for g in geoms:
        s = _geom(g)[2]
        hs = (s[0] // 2, s[1])
        out += [pltpu.VMEM((4,) + s, BF16), pltpu.VMEM((4,) + s, BF16), pltpu.VMEM((3,) + s, BF16), pltpu.VMEM((6,) + hs, BF16),
                pltpu.VMEM((2,) + hs, BF16),
                pltpu.SemaphoreType.DMA((4,)), pltpu.SemaphoreType.DMA((4,)), pltpu.SemaphoreType.DMA((6,)),
                pltpu.SemaphoreType.DMA((6,)), pltpu.SemaphoreType.DMA((4,))]
    return out


class _Scatters:
    def __init__(self, geoms, p_refs, out_refs, scratch):
        k = N_SCATTER_SCRATCH
        self.items = [_Scatter(g, p_refs[i], out_refs[i], *scratch[k * i:k * i + k]) for i, g in enumerate(geoms)]

    def phase1(self):
        for s in self.items:
            s.phase1()

    def phase2(self):
        for s in self.items:
            s.phase2()

    def phase2b(self):
        for s in self.items:
            s.phase2b()

    def phase3(self):
        for s in self.items:
            s.phase3()


PACK_W = 1024


SMALL_FULL = [(2, 8, PACK_W), (HEADS, CHUNK, CHUNK), (2, 8, D_FF)]
SMALL_HALF = [(s[0] // 2,) + s[1:] for s in SMALL_FULL]
N_SMALL_SCRATCH = 16


def _small_scratch():
    n_a = len(SMALL_FULL)
    return ([pltpu.VMEM(SMALL_FULL[0], F32)] + [pltpu.VMEM(s, F32) for s in SMALL_HALF] + [pltpu.VMEM(s, F32) for s in SMALL_HALF]
            + [pltpu.VMEM((3,) + s, F32) for s in SMALL_HALF]
            + [pltpu.SemaphoreType.DMA((n_a,)), pltpu.SemaphoreType.DMA((n_a,)), pltpu.SemaphoreType.DMA((n_a, 3)),
               pltpu.SemaphoreType.DMA((n_a, 3)), pltpu.SemaphoreType.DMA((n_a,)), pltpu.SemaphoreType.DMA((n_a,))])


class _SmallReduce:
    def __init__(self, ins, outs, scratch):
        self.ins, self.outs = ins, outs
        (self.pack, *rest) = scratch
        self.rxs, self.css, self.gs = rest[0:3], rest[3:6], rest[6:9]
        self.s1_send, self.s1_recv, self.s2_send, self.s2_recv, self.s3_send, self.s3_recv = rest[9:]
        self.x, self.y, self.c = lax.axis_index("x"), lax.axis_index("y"), lax.axis_index("c")
        self.sibling = (self.x, self.y, 1 - self.c)
        self.chips = [(1 - self.x, self.y), (self.x, 1 - self.y), (1 - self.x, 1 - self.y)]
        self.hl = [s[0] for s in SMALL_HALF]

    def half(self, ref, a, h):
        return ref.at[pl.ds(h * self.hl[a], self.hl[a])]

    def begin(self):
        dg1_ref, dg2_ref, dgf_ref, dgrn_ref, dlng_ref, dlnb_ref, dbs_ref, loss_ref, dws_ref, dcv_ref = self.ins
        pack, c = self.pack, self.c
        pack[...] = jnp.zeros_like(pack)
        pack[0, 0:1, :] = dg1_ref[0:1, :]
        pack[0, 1:2, :] = dg2_ref[0:1, :]
        pack[0, 2:3, :] = dgf_ref[0:1, :]
        pack[0, 3:4, 0:RET_W] = dgrn_ref[0:1, :]
        lsum = loss_ref[0, 0:1, :]
        for i in range(1, N_TB):
            lsum = lsum + loss_ref[i, 0:1, :]
        pack[0, 3:4, RET_W:RET_W + 128] = lsum
        pack[1, 0:HEADS, 0:128] = dlng_ref[0:HEADS, :]
        pack[1, 0:HEADS, 128:256] = dlnb_ref[0:HEADS, :]
        pack[1, 0:HEADS, 256:384] = dbs_ref[0:HEADS, :]
        self.srcs = [pack, dws_ref, dcv_ref]
        n_a = len(self.srcs)
        self.ex1 = [pltpu.make_async_remote_copy(src_ref=self.half(self.srcs[a], a, 1 - c), dst_ref=self.rxs[a],
                                                 send_sem=self.s1_send.at[a], recv_sem=self.s1_recv.at[a],
                                                 device_id=self.sibling, device_id_type=MESH) for a in range(n_a)]
        for cp in self.ex1:
            cp.start()
        self.ex2 = []
        for a in range(n_a):
            self.ex1[a].wait_recv()
            self.css[a][...] = self.half(self.srcs[a], a, c)[...] + self.rxs[a][...]
            for j, chip in enumerate(self.chips):
                cp = pltpu.make_async_remote_copy(src_ref=self.css[a], dst_ref=self.gs[a].at[j], send_sem=self.s2_send.at[a, j],
                                                  recv_sem=self.s2_recv.at[a, j], device_id=(*chip, c), device_id_type=MESH)
                cp.start()
                self.ex2.append(cp)

    def end(self):
        c, x, y = self.c, self.x, self.y
        ex3 = []
        for a in range(len(self.srcs)):
            css, gs, out = self.css[a], self.gs[a], self.outs[a]
            for j in range(3):
                self.ex2[3 * a + j].wait_recv()
            tot = None
            for q in range(4):
                k = jnp.where(x != (q >> 1), 1, 0) + jnp.where(y != (q & 1), 2, 0)
                term = jnp.where(k == 0, css[...], jnp.where(k == 1, gs[0], jnp.where(k == 2, gs[1], gs[2])))
                tot = term if tot is None else tot + term
            self.half(out, a, c)[...] = tot
            cp = pltpu.make_async_remote_copy(src_ref=self.half(out, a, c), dst_ref=self.half(out, a, c), send_sem=self.s3_send.at[a],
                                              recv_sem=self.s3_recv.at[a], device_id=self.sibling, device_id_type=MESH)
            cp.start()
            ex3.append(cp)
        for a in range(len(self.srcs)):
            out = self.outs[a]
            pltpu.make_async_remote_copy(src_ref=self.half(out, a, 1 - c), dst_ref=self.half(out, a, 1 - c), send_sem=self.s3_send.at[a],
                                         recv_sem=self.s3_recv.at[a], device_id=self.sibling, device_id_type=MESH).wait_recv()
        for cp in self.ex1 + self.ex2 + ex3:
            cp.wait_send()


def _adam_math(w, g, m, v):
    nm = ADAM_B1 * m + (1.0 - ADAM_B1) * g
    nv = ADAM_B2 * v + (1.0 - ADAM_B2) * (g * g)
    d = -ADAM_LR * ((nm / (1.0 - ADAM_B1 ** ADAM_STEP)) / (jnp.sqrt(nv / (1.0 - ADAM_B2 ** ADAM_STEP)) + ADAM_EPS) + ADAM_WD * w)
    return d, nm, nv


def _adamw(name, w, gs, m, v, rows):
    _, r, cdim = w.shape
    n_steps = r // rows
    half = gs[0].shape[0] // rows

    def body(w_ref, *rest):
        g_refs, (m_ref, v_ref, go_ref, d_ref, nm_ref, nv_ref) = rest[:len(gs)], rest[len(gs):]
        gg = g_refs[0][...]
        if len(gs) == 2:
            gg = jnp.where(pl.program_id(0) < half, gg, g_refs[1][...])
        go_ref[0] = gg
        d, nm, nv = _adam_math(w_ref[0], gg, m_ref[0], v_ref[0])
        d_ref[0], nm_ref[0], nv_ref[0] = d, nm, nv

    spec3 = pl.BlockSpec((1, rows, cdim), lambda i: (0, i, 0))
    if len(gs) == 1:
        g_specs = [pl.BlockSpec((rows, cdim), lambda i: (i, 0))]
    else:
        g_specs = [pl.BlockSpec((rows, cdim), lambda i: (jnp.minimum(i, half - 1), 0)),
                   pl.BlockSpec((rows, cdim), lambda i: (jnp.maximum(i - half, 0), 0))]
    sh = jax.ShapeDtypeStruct((1, r, cdim), F32)
    return pl.pallas_call(
        body, name=name, grid=(n_steps,), out_shape=(sh, sh, sh, sh),
        in_specs=[spec3] + g_specs + [spec3, spec3], out_specs=(spec3,) * 4,
        compiler_params=_cparams(("parallel",)),
    )(w, *gs, m, v)


def _adamw_small(rp, rws, rcv, gcw, params):
    n_p = len(params)

    def body(*refs):
        rp_ref, rws_ref, rcv_ref, gcw_ref = refs[:4]
        ins = refs[4:4 + 3 * n_p]
        outs = refs[4 + 3 * n_p:]
        grads = [rp_ref[0, 0:1, :], rp_ref[0, 1:2, :], rp_ref[0, 2:3, :], rp_ref[0, 3:4, 0:RET_W],
                 rp_ref[1, 0:HEADS, 0:128], rp_ref[1, 0:HEADS, 128:256], rp_ref[1, 0:HEADS, 256:384],
                 rws_ref[...], gcw_ref[0], None]
        for p in range(n_p):
            w_ref, m_ref, v_ref = ins[3 * p:3 * p + 3]
            o = outs[4 * p:4 * p + 4]
            if p == n_p - 1:
                for hf in range(2):
                    cs = slice(hf * D_FF, (hf + 1) * D_FF)
                    g = rcv_ref[hf, 3:4, :]
                    res = (g,) + _adam_math(w_ref[:, cs], g, m_ref[:, cs], v_ref[:, cs])
                    for t in range(4):
                        o[t][:, cs] = res[t]
                continue
            lead = w_ref.ndim > grads[p].ndim
            rd = (lambda r: r[0]) if lead else (lambda r: r[...])
            res = (grads[p],) + _adam_math(rd(w_ref), grads[p], rd(m_ref), rd(v_ref))
            for t in range(4):
                if lead:
                    o[t][0] = res[t]
                else:
                    o[t][...] = res[t]

    vm = pl.BlockSpec(memory_space=pltpu.VMEM)
    flat = [a for tr in params for a in tr]
    out_shape = tuple(jax.ShapeDtypeStruct(tr[0].shape, F32) for tr in params for _ in range(4))
    res = pl.pallas_call(
        body, name="adamw_small", out_shape=out_shape, in_specs=[vm] * (4 + len(flat)), out_specs=(vm,) * len(out_shape),
        compiler_params=_cparams(),
    )(rp, rws, rcv, gcw, *flat)
    return [res[4 * p:4 * p + 4] for p in range(n_p)]


def kernel(x, mix_norm_g, w_in, ret_norm_g, sgu_ln_g, sgu_ln_b, sgu_w_s, sgu_b_s, w_out, ffn_norm_g, w_up, conv_w, conv_b, w_down, final_norm_g, loss_target, m_mix_norm_g, m_w_in, m_ret_norm_g, m_sgu_ln_g, m_sgu_ln_b, m_sgu_w_s, m_sgu_b_s, m_w_out, m_ffn_norm_g, m_w_up, m_conv_w, m_conv_b, m_w_down, m_final_norm_g, v_mix_norm_g, v_w_in, v_ret_norm_g, v_sgu_ln_g, v_sgu_ln_b, v_sgu_w_s, v_sgu_b_s, v_w_out, v_ffn_norm_g, v_w_up, v_conv_w, v_conv_b, v_w_down, v_final_norm_g):
    xs = x[0]
    tgt = loss_target[0]
    cos2, sin2 = _rope_tables()
    mask, qdec, kdec = _decay_tables()
    grn = ret_norm_g.reshape(1, RET_W)
    lng = sgu_ln_g.reshape(1, SGU_W)
    lnb = sgu_ln_b.reshape(1, SGU_W)
    ws = sgu_w_s[0]
    bsb = jnp.broadcast_to(sgu_b_s[0][:, :, None], (HEADS, CHUNK, HEAD_DIM))
    gf = final_norm_g.reshape(1, D_MODEL)
    me = 4 * lax.axis_index("x") + 2 * lax.axis_index("y") + lax.axis_index("c")
    tr = lambda a: jnp.transpose(a[0])[None]

    proj, h1, win_g, cw_sh, wout_g, wdn_g, su = _fwd_proj(xs, mix_norm_g, cos2, sin2, w_in[0], w_out[0], tr(w_up)[0], w_down[0],
                                                         conv_w[0])
    cw_g = jnp.transpose(cw_sh, (1, 0, 2)).reshape(8, 2 * D_FF)
    x2, mixcat, o, sprev, wup_g = _fwd_mix(xs, proj, wout_g, grn, lng, lnb, ws, bsb, mask, qdec, kdec, su)
    h2, up_pre, u_conv, act, x3, loss_parts = _fwd_ffn(x2, ffn_norm_g, wup_g, cw_g, conv_b, wdn_g, gf, tgt)

    dx3, dpre, dx2, dgf, dg2, dcv = _bwd_ffn(x3, tgt, gf, x2, ffn_norm_g, up_pre, u_conv, wup_g, cw_g, wdn_g)
    band = 512
    (gdn_p,) = _wgrad("wgrad_down", act, dx3, tm=FF_TILE)
    (gout_p,) = _wgrad("wgrad_out", mixcat, dx2, tn=512)
    gup_p, g_dn = _wgrad("wgrad_up", dpre, h2, tm=FF_TILE, hosted=[(W_DOWN, gdn_p)])
    dproj, dgrn, dlng, dlnb, dws, dbs, g_up_a = _bwd_mix(
        dx2, proj, o, sprev, wout_g, grn, lng, lnb, ws, bsb, mask, qdec, kdec, cos2, sin2, [((W_UP, 0, band), gup_p)])
    gin_p, g_up_b, g_out = _wgrad("wgrad_in", h1, dproj, tn=768,
                                  hosted=[((W_UP, band, FF_SHARD - band), gup_p), (W_OUT, gout_p)])
    grad_x, g_in, rp, rws, rcv = _bwd_proj(dproj, win_g, xs, mix_norm_g, dx2, gin_p,
                                           (dg2, dgf, dgrn, dlng, dlnb, dbs, loss_parts, dws, dcv))
    loss = rp[0, 3, RET_W]
    gcw = lax.dynamic_slice(rcv, (me // (N_DEV // 2), 0, (me % (N_DEV // 2)) * FF_SHARD), (1, 3, FF_SHARD))

    table = {}
    for name, w, gs, m, v, rows in (("w_in", w_in, [g_in], m_w_in, v_w_in, 256), ("w_out", w_out, [g_out], m_w_out, v_w_out, 128),
                                    ("w_up", tr(w_up), [g_up_a, g_up_b], tr(m_w_up), tr(v_w_up), 64),
                                    ("w_down", w_down, [g_dn], m_w_down, v_w_down, 88)):
        table[name] = _adamw("adamw_" + name, w, gs, m, v, rows)
    table["w_up"] = tuple(tr(a) for a in table["w_up"])
    row = lambda a: a.reshape(1, D_MODEL)
    names_small = ["mix_norm_g", "ffn_norm_g", "final_norm_g", "ret_norm_g", "sgu_ln_g", "sgu_ln_b", "sgu_b_s", "sgu_w_s",
                   "conv_w", "conv_b"]
    params = [(mix_norm_g, m_mix_norm_g, v_mix_norm_g), (ffn_norm_g, m_ffn_norm_g, v_ffn_norm_g),
              (row(final_norm_g), row(m_final_norm_g), row(v_final_norm_g)), (ret_norm_g, m_ret_norm_g, v_ret_norm_g),
              (sgu_ln_g, m_sgu_ln_g, v_sgu_ln_g), (sgu_ln_b, m_sgu_ln_b, v_sgu_ln_b), (sgu_b_s, m_sgu_b_s, v_sgu_b_s),
              (sgu_w_s, m_sgu_w_s, v_sgu_w_s), (conv_w, m_conv_w, v_conv_w), (conv_b, m_conv_b, v_conv_b)]
    for n, res in zip(names_small, _adamw_small(rp, rws, rcv, gcw, params)):
        table[n] = res
    table["final_norm_g"] = tuple(a.reshape(D_MODEL) for a in table["final_norm_g"])

    order = ["mix_norm_g", "w_in", "ret_norm_g", "sgu_ln_g", "sgu_ln_b", "sgu_w_s", "sgu_b_s", "w_out", "ffn_norm_g", "w_up",
             "conv_w", "conv_b", "w_down", "final_norm_g"]
    outs = [loss, grad_x[None]]
    for col in range(4):
        outs += [table[n][col] for n in order]
    return tuple(outs)
```

```python
import functools
import math

import jax
import jax.numpy as jnp
import numpy as np
from jax import lax
from jax.experimental import pallas as pl
from jax.experimental.pallas import tpu as pltpu

F32 = jnp.float32
BF16 = jnp.bfloat16
MESH = pl.DeviceIdType.MESH

N_DEV = 8
SEQ = 2048
D_MODEL = 1024
CHUNK = 128
N_CHUNK = SEQ // CHUNK
HEADS = 4
HEAD_DIM = 128
RET_W = 512
SGU_W = 512
PROJ_W = 3072
D_FF = 2816
FF_SHARD = 704
FF_TILE = 1408
FF_TILES = ((0, 1536), (1536, 1280))
IN_SHARD = PROJ_W // N_DEV
OUT_SHARD = D_MODEL // N_DEV
DOWN_SHARD = D_FF // N_DEV
TM = 256
N_TB = SEQ // TM
EPS = 1e-6
ROPE_BASE = 10000.0
K_SCALE = HEAD_DIM ** -0.5
INV_SQRT2 = 0.7071067811865476
INV_SQRT_2PI = 0.3989422804014327

ADAM_LR = 0.001
ADAM_B1 = 0.9
ADAM_B2 = 0.999
ADAM_EPS = 1e-08
ADAM_WD = 0.01
ADAM_STEP = 10

VMEM_LIMIT = 56 * 1024 * 1024


def _cparams(sem=None, vmem=VMEM_LIMIT):
    return pltpu.CompilerParams(dimension_semantics=sem, vmem_limit_bytes=vmem)


def _resident(shape):
    nd = len(shape)
    return pl.BlockSpec(shape, lambda *_: (0,) * nd, pipeline_mode=pl.Buffered(1))


def _dot(a, b):
    return jnp.dot(a, b, preferred_element_type=F32)


def _dot_nt(a, b):
    return lax.dot_general(a, b, (((1,), (1,)), ((), ())), preferred_element_type=F32)


def _dot_tn(a, b):
    return lax.dot_general(a, b, (((0,), (0,)), ((), ())), preferred_element_type=F32)


def _sigmoid(x):
    return 1.0 / (1.0 + jnp.exp(-x))


def _gelu(x):
    return 0.5 * x * (1.0 + lax.erf(x * INV_SQRT2))


def _gelu_grad(x):
    return 0.5 * (1.0 + lax.erf(x * INV_SQRT2)) + x * (jnp.exp(-0.5 * x * x) * INV_SQRT_2PI)


def _rot(xh, cos2, sin2):
    return xh * cos2 + pltpu.roll(xh, HEAD_DIM // 2, 1) * sin2


def _rot_t(dh, cos2, sin2):
    return dh * cos2 + pltpu.roll(dh * sin2, HEAD_DIM // 2, 1)


def _rope_tables():
    half = HEAD_DIM // 2
    inv_freq = jnp.power(ROPE_BASE, -jnp.arange(half, dtype=F32) / half)
    ang = jnp.arange(SEQ, dtype=F32)[:, None] * inv_freq[None, :]
    cos, sin = jnp.cos(ang), jnp.sin(ang)
    cos2 = jnp.concatenate([cos, cos], axis=-1)
    sin2 = jnp.concatenate([-sin, sin], axis=-1)
    return cos2, sin2


def _decay_tables():
    log_gamma = jnp.log(1.0 - jnp.power(2.0, -5.0 - jnp.arange(HEADS, dtype=F32)))
    pos = jnp.arange(CHUNK, dtype=F32)
    diff = pos[:, None] - pos[None, :]
    mask = jnp.where(diff >= 0.0, jnp.exp(log_gamma[:, None, None] * jnp.maximum(diff, 0.0)[None]), 0.0)
    k_decay = jnp.exp(log_gamma[:, None] * (CHUNK - 1.0 - pos)[None])
    q_decay = jnp.exp(log_gamma[:, None] * (pos + 1.0)[None])
    kd = jnp.broadcast_to(k_decay[:, :, None], (HEADS, CHUNK, HEAD_DIM))
    qd = jnp.broadcast_to(q_decay[:, :, None], (HEADS, CHUNK, HEAD_DIM))
    return mask.astype(F32), qd.astype(F32), kd.astype(F32)


def _chunk_decay():
    lg = np.log(np.float32(1.0) - np.power(np.float32(2.0), -5.0 - np.arange(HEADS, dtype=np.float32))).astype(np.float32)
    return [float(np.exp(lg[h] * np.float32(CHUNK))) for h in range(HEADS)]


W_IN, W_OUT, W_UP, W_DOWN, W_CONV = range(5)
GATHERED = {W_IN: ((D_MODEL, PROJ_W), BF16), W_OUT: ((D_MODEL, D_MODEL), BF16), W_UP: ((2 * D_FF, D_MODEL), BF16),
            W_DOWN: ((D_FF, D_MODEL), BF16), W_CONV: ((N_DEV, 8, FF_SHARD), F32)}
SHARD = {W_IN: (D_MODEL, IN_SHARD), W_OUT: (OUT_SHARD, D_MODEL), W_UP: (FF_SHARD, D_MODEL), W_DOWN: (DOWN_SHARD, D_MODEL),
         W_CONV: (8, FF_SHARD)}


class _Gather:
    N_SEMS = 9

    def __init__(self, ids, stages, gathered, send_sems, recv_sems, local_sems):
        self.ids, self.stages, self.gathered = ids, stages, gathered
        self.send_sems, self.recv_sems, self.local_sems = send_sems, recv_sems, local_sems
        self.x, self.y, self.c = lax.axis_index("x"), lax.axis_index("y"), lax.axis_index("c")
        self.me = (self.x, self.y, self.c)
        self.sibling = (self.x, self.y, 1 - self.c)
        self.chips = [(1 - self.x, self.y), (self.x, 1 - self.y), (1 - self.x, 1 - self.y)]

    def slot(self, n, px, py, pc):
        dev = 4 * px + 2 * py + pc
        w, g = self.ids[n], self.gathered[n]
        if w == W_IN:
            return g.at[:, pl.ds(pl.multiple_of(dev * IN_SHARD, 128), IN_SHARD)]
        if w == W_OUT:
            return g.at[pl.ds(pl.multiple_of(dev * OUT_SHARD, 128), OUT_SHARD), :]
        if w == W_DOWN:
            return g.at[pl.ds(pl.multiple_of(dev * DOWN_SHARD, 32), DOWN_SHARD), :]
        if w == W_UP:
            return g.at[pl.ds(pl.multiple_of(dev * FF_SHARD, 32), FF_SHARD), :]
        return g.at[dev]

    def half(self, n, px, py, pc, h):
        dev = 4 * px + 2 * py + pc
        w, g = self.ids[n], self.gathered[n]
        if w == W_IN:
            return g.at[pl.ds(h * (D_MODEL // 2), D_MODEL // 2), pl.ds(pl.multiple_of(dev * IN_SHARD, 128), IN_SHARD)]
        rows = SHARD[w][0] // 2
        return g.at[pl.ds(pl.multiple_of(dev * SHARD[w][0] + h * rows, 16), rows), :]

    def tree(self, n):
        return self.ids[n] != W_CONV

    def copy(self, n, k, block, to, src=None, h=None):
        ref = self.slot(n, *block) if h is None else self.half(n, *block, h)
        return pltpu.make_async_remote_copy(
            src_ref=ref if src is None else src, dst_ref=ref,
            send_sem=self.send_sems.at[n, k], recv_sem=self.recv_sems.at[n, k], device_id=to, device_id_type=MESH)

    def _mine(self):
        return [pltpu.make_async_copy(self.stages[n], self.slot(n, *self.me), self.local_sems.at[n]) for n in range(len(self.ids))]

    def _first(self):
        out = []
        for n in range(len(self.ids)):
            out.append(self.copy(n, 0, self.me, self.sibling, src=self.stages[n]))
            out += [self.copy(n, 1 + j, self.me, (*chip, self.c), src=self.stages[n])
                    for j, chip in enumerate(self.chips[:2] if self.tree(n) else self.chips)]
        return out

    def start(self):
        for cp in self._mine() + self._first():
            cp.start()

    def finish(self):
        self.forward_near()
        self.forward_far()
        self.wait_all()

    def _go(self, cp):
        cp.start()
        self.passed.append(cp)

    def forward_near(self):
        cx, cy, cd = [(*chip, self.c) for chip in self.chips]
        self.passed = []
        go = self._go
        for n in range(len(self.ids)):
            if self.tree(n):
                self.copy(n, 1, cx, self.me).wait_recv()
                go(self.copy(n, 3, cx, cy, h=0))
                go(self.copy(n, 5, cx, self.sibling))
                self.copy(n, 2, cy, self.me).wait_recv()
                go(self.copy(n, 4, cy, cx, h=1))
                go(self.copy(n, 6, cy, self.sibling))
            else:
                for j, dev in enumerate((cx, cy, cd)):
                    self.copy(n, 1 + j, dev, self.me).wait_recv()
                    go(self.copy(n, 4 + j, dev, self.sibling))

    def forward_far(self):
        cd = (*self.chips[2], self.c)
        go = self._go
        for n in range(len(self.ids)):
            if self.tree(n):
                self.copy(n, 3, cd, self.me, h=0).wait_recv()
                go(self.copy(n, 7, cd, self.sibling, h=0))
                self.copy(n, 4, cd, self.me, h=1).wait_recv()
                go(self.copy(n, 8, cd, self.sibling, h=1))

    def wait_all(self):
        ox, oy, od = [(*chip, 1 - self.c) for chip in self.chips]
        for n in range(len(self.ids)):
            self.copy(n, 0, self.sibling, self.me).wait_recv()
            if self.tree(n):
                self.copy(n, 5, ox, self.me).wait_recv()
                self.copy(n, 6, oy, self.me).wait_recv()
                self.copy(n, 7, od, self.me, h=0).wait_recv()
                self.copy(n, 8, od, self.me, h=1).wait_recv()
            else:
                for j, dev in enumerate((ox, oy, od)):
                    self.copy(n, 4 + j, dev, self.me).wait_recv()
        for cp in self._first() + self.passed:
            cp.wait_send()
        for cp in self._mine():
            cp.wait()


def _gather_scratch(n):
    return [pltpu.SemaphoreType.DMA((n, _Gather.N_SEMS)), pltpu.SemaphoreType.DMA((n, _Gather.N_SEMS)), pltpu.SemaphoreType.DMA((n,))]


def _gathered_shapes(ids):
    return tuple(jax.ShapeDtypeStruct(*GATHERED[w]) for w in ids)


def _fwd_proj(x, g1, cos2, sin2, w_in, w_out, w_up, w_down, conv_w):
    ids_a, ids_b = [W_IN, W_CONV], [W_OUT]

    def body(x_ref, g_ref, cos_ref, sin_ref, in_hbm, out_hbm, up_hbm, dn_hbm, cw_ref,
             proj_ref, h1_ref, gin, gcw, gout, su_ref, s_dn,
             w_vm, s_in, s_cw, s_out, f_in, f_out, f_up, f_dn, ld_sems,
             a_send, a_recv, a_local, b_send, b_recv, b_local):
        ag_a = _Gather(ids_a, [s_in, s_cw], [gin, gcw], a_send, a_recv, a_local)
        ag_b = _Gather(ids_b, [s_out], [gout], b_send, b_recv, b_local)

        @pl.when(pl.program_id(0) == 0)
        def _():
            loads = [pltpu.make_async_copy(src, dst, ld_sems.at[i])
                     for i, (src, dst) in enumerate(((in_hbm, f_in), (out_hbm, f_out), (dn_hbm, f_dn), (up_hbm, f_up)))]
            for cp in loads:
                cp.start()
            s_cw[...] = jnp.zeros_like(s_cw)
            s_cw[0:3, :] = cw_ref[...]
            loads[0].wait()
            s_in[...] = f_in[...].astype(BF16)
            ag_a.start()
            loads[1].wait()
            s_out[...] = f_out[...].astype(BF16)
            loads[2].wait()
            s_dn[...] = f_dn[...].astype(BF16)
            ag_a.forward_near()
            ag_b.start()
            loads[3].wait()
            su_ref[...] = f_up[...].astype(BF16)
            ag_a.forward_far()
            ag_a.wait_all()
            fill = pltpu.make_async_copy(gin, w_vm, ld_sems.at[4])
            fill.start()
            fill.wait()

        xb = x_ref[...]
        r = lax.rsqrt(jnp.mean(xb * xb, axis=-1, keepdims=True) + EPS)
        h = ((xb * r) * g_ref[...]).astype(BF16)
        h1_ref[...] = h
        p = _dot(h, w_vm[...])
        for hd in range(HEADS):
            sl = slice(hd * HEAD_DIM, (hd + 1) * HEAD_DIM)
            c2, s2 = cos_ref[...], sin_ref[...]
            proj_ref[:, sl] = _rot(p[:, sl], c2, s2)
            ks = slice(RET_W + hd * HEAD_DIM, RET_W + (hd + 1) * HEAD_DIM)
            proj_ref[:, ks] = _rot(p[:, ks], c2, s2) * K_SCALE
        proj_ref[:, 2 * RET_W:] = p[:, 2 * RET_W:]

        @pl.when(pl.program_id(0) == N_TB - 1)
        def _():
            ag_b.finish()

    tok = lambda w: pl.BlockSpec((TM, w), lambda i: (i, 0))
    hbm = pl.BlockSpec(memory_space=pl.ANY)
    vm = pl.BlockSpec(memory_space=pltpu.VMEM)
    return pl.pallas_call(
        body, name="fwd_proj", grid=(N_TB,),
        out_shape=(jax.ShapeDtypeStruct((SEQ, PROJ_W), F32), jax.ShapeDtypeStruct((SEQ, D_MODEL), BF16))
        + _gathered_shapes(ids_a + ids_b) + (jax.ShapeDtypeStruct(SHARD[W_UP], BF16), jax.ShapeDtypeStruct(SHARD[W_DOWN], BF16)),
        in_specs=[tok(D_MODEL), _resident((1, D_MODEL)), tok(HEAD_DIM), tok(HEAD_DIM), hbm, hbm, hbm, hbm, vm],
        out_specs=(tok(PROJ_W), tok(D_MODEL), hbm, hbm, hbm, vm, vm),
        scratch_shapes=[pltpu.VMEM((D_MODEL, PROJ_W), BF16), pltpu.VMEM(SHARD[W_IN], BF16), pltpu.VMEM(SHARD[W_CONV], F32),
                        pltpu.VMEM(SHARD[W_OUT], BF16),
                        pltpu.VMEM(SHARD[W_IN], F32), pltpu.VMEM(SHARD[W_OUT], F32), pltpu.VMEM(SHARD[W_UP], F32),
                        pltpu.VMEM(SHARD[W_DOWN], F32), pltpu.SemaphoreType.DMA((5,))]
        + _gather_scratch(len(ids_a)) + _gather_scratch(len(ids_b)),
        compiler_params=_cparams(("arbitrary",)),
    )(x, g1, cos2, sin2, w_in, w_out, w_up, w_down, conv_w)


def _causal(w):
    r = lax.broadcasted_iota(jnp.int32, (CHUNK, CHUNK), 0)
    c = lax.broadcasted_iota(jnp.int32, (CHUNK, CHUNK), 1)
    return jnp.where(r >= c, w, 0.0)


def _fwd_mix(x, proj, wout_g, grn, lng, lnb, ws, bsb, mask, qdec, kdec, su):
    cdec = _chunk_decay()
    ids = [W_UP]

    def body(x_ref, p_ref, w_ref, grn_ref, lng_ref, lnb_ref, ws_ref, bsb_ref, m_ref, qd_ref, kd_ref, su_ref,
             x2_ref, cat_ref, o_ref, sp_ref, gup, state, send_sems, recv_sems, local_sems):
        ag = _Gather(ids, [su_ref], [gup], send_sems, recv_sems, local_sems)

        @pl.when(pl.program_id(0) == 0)
        def _():
            state[...] = jnp.zeros_like(state)
            ag.start()

        for h in range(HEADS):
            sl = slice(h * HEAD_DIM, (h + 1) * HEAD_DIM)
            q = p_ref[:, sl]
            k = p_ref[:, RET_W + h * HEAD_DIM:RET_W + (h + 1) * HEAD_DIM]
            v = p_ref[:, 2 * RET_W + h * HEAD_DIM:2 * RET_W + (h + 1) * HEAD_DIM]
            g = p_ref[:, 3 * RET_W + h * HEAD_DIM:3 * RET_W + (h + 1) * HEAD_DIM]
            qb, kb, vb = q.astype(BF16), k.astype(BF16), v.astype(BF16)
            a = _dot_nt(qb, kb) * m_ref[h]
            spb = state[h].astype(BF16)
            sp_ref[0, h] = spb
            o = _dot(a.astype(BF16), vb) + _dot((q * qd_ref[h]).astype(BF16), spb)
            state[h] = state[h] * cdec[h] + _dot_tn((k * kd_ref[h]).astype(BF16), vb)
            o_ref[:, sl] = o
            rinv = lax.rsqrt(jnp.mean(o * o, axis=-1, keepdims=True) + EPS)
            rn = (o * rinv) * grn_ref[:, sl]
            cat_ref[:, sl] = ((g * _sigmoid(g)) * rn).astype(BF16)
        for gi in range(HEADS):
            sl = slice(gi * HEAD_DIM, (gi + 1) * HEAD_DIM)
            u = p_ref[:, 4 * RET_W + gi * HEAD_DIM:4 * RET_W + (gi + 1) * HEAD_DIM]
            sv = p_ref[:, 4 * RET_W + SGU_W + gi * HEAD_DIM:4 * RET_W + SGU_W + (gi + 1) * HEAD_DIM]
            gv = _gelu(sv)
            xc = gv - jnp.mean(gv, axis=-1, keepdims=True)
            vn = (xc * lax.rsqrt(jnp.mean(xc * xc, axis=-1, keepdims=True) + EPS)) * lng_ref[:, sl] + lnb_ref[:, sl]
            mixed = _dot(_causal(ws_ref[gi]).astype(BF16), vn.astype(BF16)) + bsb_ref[gi]
            cat_ref[:, RET_W + gi * HEAD_DIM:RET_W + (gi + 1) * HEAD_DIM] = (_gelu(u) * mixed).astype(BF16)
        x2_ref[...] = x_ref[...] + _dot(cat_ref[...], w_ref[...])

        @pl.when(pl.program_id(0) == N_CHUNK - 1)
        def _():
            ag.finish()

    ch = lambda w: pl.BlockSpec((CHUNK, w), lambda i: (i, 0))
    hcc = (HEADS, CHUNK, CHUNK)
    hbm = pl.BlockSpec(memory_space=pl.ANY)
    return pl.pallas_call(
        body, name="fwd_mix", grid=(N_CHUNK,),
        out_shape=(jax.ShapeDtypeStruct((SEQ, D_MODEL), F32), jax.ShapeDtypeStruct((SEQ, D_MODEL), BF16),
                   jax.ShapeDtypeStruct((SEQ, RET_W), F32), jax.ShapeDtypeStruct((N_CHUNK, HEADS, HEAD_DIM, HEAD_DIM), BF16))
        + _gathered_shapes(ids),
        in_specs=[ch(D_MODEL), ch(PROJ_W), _resident((D_MODEL, D_MODEL)), _resident((1, RET_W)), _resident((1, SGU_W)),
                  _resident((1, SGU_W)), _resident(hcc), _resident(hcc), _resident(hcc), _resident(hcc), _resident(hcc), hbm],
        out_specs=(ch(D_MODEL), ch(D_MODEL), ch(RET_W), pl.BlockSpec((1, HEADS, HEAD_DIM, HEAD_DIM), lambda i: (i, 0, 0, 0)), hbm),
        scratch_shapes=[pltpu.VMEM((HEADS, HEAD_DIM, HEAD_DIM), F32)] + _gather_scratch(len(ids)),
        compiler_params=_cparams(("arbitrary",)),
    )(x, proj, wout_g, grn, lng, lnb, ws, bsb, mask, qdec, kdec, su)


def _conv_taps(p, prev8):
    row = lax.broadcasted_iota(jnp.int32, p.shape, 0)
    p1 = jnp.where(row == 0, prev8[7:8, :], pltpu.roll(p, 1, 0))
    p2 = jnp.where(row == 0, prev8[6:7, :], jnp.where(row == 1, prev8[7:8, :], pltpu.roll(p, 2, 0)))
    return p1, p2


def _fwd_ffn_up(x2, g2, wup_g, cw_g, cb_g, sd):
    ids = [W_DOWN]

    def body(x_ref, g_ref, wu_ref, cw_ref, cb_ref, sd_ref, h2_ref, up_ref, u_ref, act_ref, gdn, carry, send_sems, recv_sems, local_sems):
        ag = _Gather(ids, [sd_ref], [gdn], send_sems, recv_sems, local_sems)

        @pl.when(pl.program_id(0) == 0)
        def _():
            carry[...] = jnp.zeros_like(carry)
            ag.start()

        xb = x_ref[...]
        r = lax.rsqrt(jnp.mean(xb * xb, axis=-1, keepdims=True) + EPS)
        h = ((xb * r) * g_ref[...]).astype(BF16)
        h2_ref[...] = h
        for t0, tw in FF_TILES:
            u = []
            for c0 in (t0, D_FF + t0):
                cs = slice(c0, c0 + tw)
                p = _dot_nt(h, wu_ref[pl.ds(c0, tw), :])
                up_ref[:, cs] = p.astype(BF16)
                p1, p2 = _conv_taps(p, carry[:, cs])
                carry[:, cs] = p[TM - 8:, :]
                us = p2 * cw_ref[0:1, cs] + p1 * cw_ref[1:2, cs] + p * cw_ref[2:3, cs] + cb_ref[:, cs]
                u_ref[:, cs] = us.astype(BF16)
                u.append(us)
            act_ref[:, t0:t0 + tw] = ((u[0] * _sigmoid(u[0])) * u[1]).astype(BF16)

        @pl.when(pl.program_id(0) == N_TB - 1)
        def _():
            ag.finish()

    tok = lambda w: pl.BlockSpec((TM, w), lambda i: (i, 0))
    hbm = pl.BlockSpec(memory_space=pl.ANY)
    return pl.pallas_call(
        body, name="fwd_ffn_up", grid=(N_TB,),
        out_shape=(jax.ShapeDtypeStruct((SEQ, D_MODEL), BF16), jax.ShapeDtypeStruct((SEQ, 2 * D_FF), BF16),
                   jax.ShapeDtypeStruct((SEQ, 2 * D_FF), BF16), jax.ShapeDtypeStruct((SEQ, D_FF), BF16)) + _gathered_shapes(ids),
        in_specs=[tok(D_MODEL), _resident((1, D_MODEL)), _resident((2 * D_FF, D_MODEL)), _resident((8, 2 * D_FF)),
                  _resident((1, 2 * D_FF)), hbm],
        out_specs=(tok(D_MODEL), tok(2 * D_FF), tok(2 * D_FF), tok(D_FF), hbm),
        scratch_shapes=[pltpu.VMEM((8, 2 * D_FF), F32)] + _gather_scratch(len(ids)),
        compiler_params=_cparams(("arbitrary",)),
    )(x2, g2, wup_g, cw_g, cb_g, sd)


def _fwd_ffn_down(x2, act, wdn_g, gf, tgt):
    def body(x_ref, a_ref, wd_ref, gf_ref, t_ref, x3_ref, loss_ref):
        acc = x_ref[...] + _dot(a_ref[...], wd_ref[...])
        x3_ref[...] = acc
        r3 = lax.rsqrt(jnp.mean(acc * acc, axis=-1, keepdims=True) + EPS)
        diff = (acc * r3) * gf_ref[...] - t_ref[...]
        loss_ref[...] = jnp.full(loss_ref.shape, 0.5 * jnp.sum(jnp.mean(diff * diff, axis=-1)), F32)

    tok = lambda w: pl.BlockSpec((TM, w), lambda i: (i, 0))
    return pl.pallas_call(
        body, name="fwd_ffn_down", grid=(N_TB,),
        out_shape=(jax.ShapeDtypeStruct((SEQ, D_MODEL), F32), jax.ShapeDtypeStruct((N_TB, 8, 128), F32)),
        in_specs=[tok(D_MODEL), tok(D_FF), _resident((D_FF, D_MODEL)), _resident((1, D_MODEL)), tok(D_MODEL)],
        out_specs=(tok(D_MODEL), pl.BlockSpec((1, 8, 128), lambda i: (i, 0, 0))),
        compiler_params=_cparams(("parallel",)),
    )(x2, act, wdn_g, gf, tgt)


def _bwd_ffn(x3, tgt, gf, x2, g2, up_pre, u_conv, wup_g, cw_g, wdn_g):
    def body(x3_ref, t_ref, gf_ref, x2_ref, g2_ref, up_ref, u_ref, wu_ref, cw_ref, wd_ref,
             dx3_ref, dpre_ref, dx2_ref, dgf_ref, dg2_ref, dcv_ref, nxt):
        i = pl.program_id(0)

        @pl.when(i == 0)
        def _():
            nxt[...] = jnp.zeros_like(nxt)
            dgf_ref[...] = jnp.zeros_like(dgf_ref)
            dg2_ref[...] = jnp.zeros_like(dg2_ref)
            dcv_ref[...] = jnp.zeros_like(dcv_ref)

        x3 = x3_ref[...]
        r3 = lax.rsqrt(jnp.mean(x3 * x3, axis=-1, keepdims=True) + EPS)
        xh3 = x3 * r3
        dy = (xh3 * gf_ref[...] - t_ref[...]) * (1.0 / D_MODEL)
        dgf_ref[0:1, :] += jnp.sum(dy * xh3, axis=0, keepdims=True)
        t3 = dy * gf_ref[...]
        dx3 = r3 * (t3 - xh3 * jnp.mean(t3 * xh3, axis=-1, keepdims=True))
        dx3b = dx3.astype(BF16)
        dx3_ref[...] = dx3b
        dh2 = jnp.zeros((TM, D_MODEL), F32)
        for t0, tw in FF_TILES:
            row = lax.broadcasted_iota(jnp.int32, (TM, tw), 0)
            ts = slice(t0, t0 + tw)
            dact = _dot_nt(dx3b, wd_ref[pl.ds(t0, tw), :])
            ua = u_ref[:, ts].astype(F32)
            ub = u_ref[:, D_FF + t0:D_FF + t0 + tw].astype(F32)
            sg = _sigmoid(ua)
            du = [dact * ub * (sg * (1.0 + ua * (1.0 - sg))), dact * (ua * sg)]
            for n in range(2):
                d = du[n]
                c0 = n * D_FF + t0
                cs = slice(c0, c0 + tw)
                nx = nxt[:, cs]
                n1 = jnp.where(row == TM - 1, nx[0:1, :], pltpu.roll(d, TM - 1, 0))
                n2 = jnp.where(row == TM - 2, nx[0:1, :], jnp.where(row == TM - 1, nx[1:2, :], pltpu.roll(d, TM - 2, 0)))
                nxt[:, cs] = d[0:8, :]
                dp = (d * cw_ref[2:3, cs] + n1 * cw_ref[1:2, cs] + n2 * cw_ref[0:1, cs]).astype(BF16)
                dpre_ref[:, cs] = dp
                p = up_ref[:, cs].astype(F32)
                dcv_ref[n, 0:1, ts] += jnp.sum(n2 * p, axis=0, keepdims=True)
                dcv_ref[n, 1:2, ts] += jnp.sum(n1 * p, axis=0, keepdims=True)
                dcv_ref[n, 2:3, ts] += jnp.sum(d * p, axis=0, keepdims=True)
                dcv_ref[n, 3:4, ts] += jnp.sum(d, axis=0, keepdims=True)
                dh2 = dh2 + _dot(dp, wu_ref[pl.ds(c0, tw), :])
        x2 = x2_ref[...]
        r2 = lax.rsqrt(jnp.mean(x2 * x2, axis=-1, keepdims=True) + EPS)
        xh2 = x2 * r2
        dg2_ref[0:1, :] += jnp.sum(dh2 * xh2, axis=0, keepdims=True)
        t2 = dh2 * g2_ref[...]
        dx2_ref[...] = dx3 + r2 * (t2 - xh2 * jnp.mean(t2 * xh2, axis=-1, keepdims=True))

    rev = lambda w: pl.BlockSpec((TM, w), lambda i: (N_TB - 1 - i, 0))
    acc = lambda s: pl.BlockSpec(s, lambda i: (0,) * len(s))
    return pl.pallas_call(
        body, name="bwd_ffn", grid=(N_TB,),
        out_shape=(jax.ShapeDtypeStruct((SEQ, D_MODEL), BF16), jax.ShapeDtypeStruct((SEQ, 2 * D_FF), BF16),
                   jax.ShapeDtypeStruct((SEQ, D_MODEL), F32), jax.ShapeDtypeStruct((8, D_MODEL), F32),
                   jax.ShapeDtypeStruct((8, D_MODEL), F32), jax.ShapeDtypeStruct((2, 8, D_FF), F32)),
        in_specs=[rev(D_MODEL), rev(D_MODEL), _resident((1, D_MODEL)), rev(D_MODEL), _resident((1, D_MODEL)), rev(2 * D_FF),
                  rev(2 * D_FF), _resident((2 * D_FF, D_MODEL)), _resident((8, 2 * D_FF)), _resident((D_FF, D_MODEL))],
        out_specs=(rev(D_MODEL), rev(2 * D_FF), rev(D_MODEL), acc((8, D_MODEL)), acc((8, D_MODEL)), acc((2, 8, D_FF))),
        scratch_shapes=[pltpu.VMEM((8, 2 * D_FF), F32)],
        compiler_params=_cparams(("arbitrary",)),
    )(x3, tgt, gf, x2, g2, up_pre, u_conv, wup_g, cw_g, wdn_g)


def _bwd_mix(dx2, proj, o, sprev, wout_g, grn, lng, lnb, ws, bsb, mask, qdec, kdec, cos2, sin2, hosted):
    cdec = _chunk_decay()
    geoms = [g for g, _ in hosted]
    n_h = len(hosted)

    def body(dx2_ref, p_ref, o_ref, sp_ref, w_ref, grn_ref, lng_ref, lnb_ref, ws_ref, bsb_ref, m_ref, qd_ref, kd_ref,
             cos_ref, sin_ref, *rest):
        dp_ref, dgrn_ref, dlng_ref, dlnb_ref, dws_ref, dbs_ref = rest[n_h:n_h + 6]
        dstate, dbs_acc = rest[2 * n_h + 6:2 * n_h + 8]
        i = pl.program_id(0)
        rs = _Scatters(geoms, rest[:n_h], rest[n_h + 6:2 * n_h + 6], rest[2 * n_h + 8:])
        pl.when(i == 0)(rs.phase1)
        pl.when(i == 3)(rs.phase2)
        pl.when(i == 6)(rs.phase2b)

        @pl.when(i == 0)
        def _():
            dstate[...] = jnp.zeros_like(dstate)
            dgrn_ref[...] = jnp.zeros_like(dgrn_ref)
            dlng_ref[...] = jnp.zeros_like(dlng_ref)
            dlnb_ref[...] = jnp.zeros_like(dlnb_ref)
            dws_ref[...] = jnp.zeros_like(dws_ref)
            dbs_ref[...] = jnp.zeros_like(dbs_ref)
            dbs_acc[...] = jnp.zeros_like(dbs_acc)

        dmix = _dot_nt(dx2_ref[...].astype(BF16), w_ref[...])
        for h in range(HEADS):
            sl = slice(h * HEAD_DIM, (h + 1) * HEAD_DIM)
            q = p_ref[:, sl]
            k = p_ref[:, RET_W + h * HEAD_DIM:RET_W + (h + 1) * HEAD_DIM]
            v = p_ref[:, 2 * RET_W + h * HEAD_DIM:2 * RET_W + (h + 1) * HEAD_DIM]
            g = p_ref[:, 3 * RET_W + h * HEAD_DIM:3 * RET_W + (h + 1) * HEAD_DIM]
            o = o_ref[:, sl]
            rinv = lax.rsqrt(jnp.mean(o * o, axis=-1, keepdims=True) + EPS)
            oh = o * rinv
            gr = grn_ref[:, sl]
            sg = _sigmoid(g)
            dret = dmix[:, sl]
            dp_ref[:, 3 * RET_W + h * HEAD_DIM:3 * RET_W + (h + 1) * HEAD_DIM] = (
                dret * (oh * gr) * (sg * (1.0 + g * (1.0 - sg)))).astype(BF16)
            drn = dret * (g * sg)
            dgrn_ref[0:1, sl] += jnp.sum(drn * oh, axis=0, keepdims=True)
            t = drn * gr
            do = rinv * (t - oh * jnp.mean(t * oh, axis=-1, keepdims=True))
            qb, kb, vb, dob = q.astype(BF16), k.astype(BF16), v.astype(BF16), do.astype(BF16)
            m = m_ref[h]
            ab = (_dot_nt(qb, kb) * m).astype(BF16)
            dab = (_dot_nt(dob, vb) * m).astype(BF16)
            spb = sp_ref[0, h]
            dsn = dstate[h]
            dsnb = dsn.astype(BF16)
            qdb = (q * qd_ref[h]).astype(BF16)
            kdb = (k * kd_ref[h]).astype(BF16)
            dq = _dot(dab, kb) + _dot_nt(dob, spb) * qd_ref[h]
            dk = _dot_tn(dab, qb) + _dot_nt(vb, dsnb) * kd_ref[h]
            dv = _dot_tn(ab, dob) + _dot(kdb, dsnb)
            dstate[h] = dsn * cdec[h] + _dot_tn(qdb, dob)
            c2, s2 = cos_ref[...], sin_ref[...]
            dp_ref[:, sl] = _rot_t(dq, c2, s2).astype(BF16)
            dp_ref[:, RET_W + h * HEAD_DIM:RET_W + (h + 1) * HEAD_DIM] = _rot_t(dk * K_SCALE, c2, s2).astype(BF16)
            dp_ref[:, 2 * RET_W + h * HEAD_DIM:2 * RET_W + (h + 1) * HEAD_DIM] = dv.astype(BF16)
        for gi in range(HEADS):
            sl = slice(gi * HEAD_DIM, (gi + 1) * HEAD_DIM)
            u = p_ref[:, 4 * RET_W + gi * HEAD_DIM:4 * RET_W + (gi + 1) * HEAD_DIM]
            sv = p_ref[:, 4 * RET_W + SGU_W + gi * HEAD_DIM:4 * RET_W + SGU_W + (gi + 1) * HEAD_DIM]
            gv = _gelu(sv)
            xc = gv - jnp.mean(gv, axis=-1, keepdims=True)
            rstd = lax.rsqrt(jnp.mean(xc * xc, axis=-1, keepdims=True) + EPS)
            xh = xc * rstd
            lg = lng_ref[:, sl]
            vnb = (xh * lg + lnb_ref[:, sl]).astype(BF16)
            wcb = _causal(ws_ref[gi]).astype(BF16)
            mixed = _dot(wcb, vnb) + bsb_ref[gi]
            dsgu = dmix[:, RET_W + gi * HEAD_DIM:RET_W + (gi + 1) * HEAD_DIM]
            dmixed = dsgu * _gelu(u)
            dmb = dmixed.astype(BF16)
            dws_ref[gi] += _causal(_dot_nt(dmb, vnb))
            dbs_acc[gi] += dmixed
            dvn = _dot_tn(wcb, dmb)
            dlng_ref[gi:gi + 1, :] += jnp.sum(dvn * xh, axis=0, keepdims=True)
            dlnb_ref[gi:gi + 1, :] += jnp.sum(dvn, axis=0, keepdims=True)
            dxh = dvn * lg
            dgv = rstd * (dxh - jnp.mean(dxh, axis=-1, keepdims=True) - xh * jnp.mean(dxh * xh, axis=-1, keepdims=True))
            dp_ref[:, 4 * RET_W + gi * HEAD_DIM:4 * RET_W + (gi + 1) * HEAD_DIM] = (dsgu * mixed * _gelu_grad(u)).astype(BF16)
            dp_ref[:, 4 * RET_W + SGU_W + gi * HEAD_DIM:4 * RET_W + SGU_W + (gi + 1) * HEAD_DIM] = (
                dgv * _gelu_grad(sv)).astype(BF16)

        @pl.when(i == N_CHUNK - 1)
        def _():
            for gi in range(HEADS):
                col = jnp.broadcast_to(jnp.sum(dbs_acc[gi], axis=-1, keepdims=True), (CHUNK, CHUNK))
                dbs_ref[gi:gi + 1, :] = jnp.transpose(col)[0:1, :]
            rs.phase3()

    rev = lambda w: pl.BlockSpec((CHUNK, w), lambda i: (N_CHUNK - 1 - i, 0))
    hcc = (HEADS, CHUNK, CHUNK)
    acc = lambda s: pl.BlockSpec(s, lambda i: (0,) * len(s))
    res = pl.pallas_call(
        body, name="bwd_mix", grid=(N_CHUNK,),
        out_shape=(jax.ShapeDtypeStruct((SEQ, PROJ_W), BF16), jax.ShapeDtypeStruct((8, RET_W), F32),
                   jax.ShapeDtypeStruct((8, HEAD_DIM), F32), jax.ShapeDtypeStruct((8, HEAD_DIM), F32),
                   jax.ShapeDtypeStruct(hcc, F32), jax.ShapeDtypeStruct((8, CHUNK), F32)) + _scatter_out_shapes(geoms),
        in_specs=[rev(D_MODEL), rev(PROJ_W), rev(RET_W),
                  pl.BlockSpec((1, HEADS, HEAD_DIM, HEAD_DIM), lambda i: (N_CHUNK - 1 - i, 0, 0, 0)),
                  _resident((D_MODEL, D_MODEL)), _resident((1, RET_W)), _resident((1, SGU_W)), _resident((1, SGU_W)),
                  _resident(hcc), _resident(hcc), _resident(hcc), _resident(hcc), _resident(hcc), rev(HEAD_DIM), rev(HEAD_DIM)]
        + [pl.BlockSpec(memory_space=pl.ANY)] * n_h,
        out_specs=(rev(PROJ_W), acc((8, RET_W)), acc((8, HEAD_DIM)), acc((8, HEAD_DIM)), acc(hcc), acc((8, CHUNK)))
        + _scatter_out_specs(geoms),
        scratch_shapes=[pltpu.VMEM((HEADS, HEAD_DIM, HEAD_DIM), F32), pltpu.VMEM((HEADS, CHUNK, CHUNK), F32)] + _scatter_scratch(geoms),
        compiler_params=_cparams(("arbitrary",)),
    )(dx2, proj, o, sprev, wout_g, grn, lng, lnb, ws, bsb, mask, qdec, kdec, cos2, sin2, *[p for _, p in hosted])
    return tuple(res[:6 + n_h])


def _bwd_proj(dproj, win_g, x, g1, dx2, gin_p, small):
    geoms = [W_IN]
    n_s = len(small)

    def body(dp_ref, w_ref, x_ref, g_ref, dx2_ref, gin_ref, *rest):
        small_refs = rest[:n_s]
        dx_ref, rs_out, rp_ref, rws_ref, rcv_ref, dg_ref = rest[n_s:n_s + 6]
        rs_scratch = rest[n_s + 6:n_s + 6 + N_SCATTER_SCRATCH]
        ar_scratch = rest[n_s + 6 + N_SCATTER_SCRATCH:]
        ar_res = ar_scratch[N_SMALL_SCRATCH:]
        ar = _SmallReduce((dg_ref,) + tuple(small_refs), ar_res, ar_scratch[:N_SMALL_SCRATCH])
        rs = _Scatters(geoms, [gin_ref], [rs_out], rs_scratch)
        pl.when(pl.program_id(0) == 0)(rs.phase1)
        pl.when(pl.program_id(0) == 2)(rs.phase2)
        pl.when(pl.program_id(0) == 4)(rs.phase2b)

        @pl.when(pl.program_id(0) == 0)
        def _():
            dg_ref[...] = jnp.zeros_like(dg_ref)

        dh = _dot_nt(dp_ref[...], w_ref[...])
        xb = x_ref[...]
        r = lax.rsqrt(jnp.mean(xb * xb, axis=-1, keepdims=True) + EPS)
        xh = xb * r
        dg_ref[0:1, :] += jnp.sum(dh * xh, axis=0, keepdims=True)
        t = dh * g_ref[...]
        dx_ref[...] = dx2_ref[...] + r * (t - xh * jnp.mean(t * xh, axis=-1, keepdims=True))

        @pl.when(pl.program_id(0) == N_TB - 1)
        def _():
            ar.begin()
            rs.phase3()
            ar.end()
            for o_ref, r_ref in zip((rp_ref, rws_ref, rcv_ref), ar_res):
                o_ref[...] = r_ref[...]

    tok = lambda w: pl.BlockSpec((TM, w), lambda i: (i, 0))
    vm = pl.BlockSpec(memory_space=pltpu.VMEM)
    res = pl.pallas_call(
        body, name="bwd_proj", grid=(N_TB,),
        out_shape=(jax.ShapeDtypeStruct((SEQ, D_MODEL), F32),) + _scatter_out_shapes(geoms)
        + tuple(jax.ShapeDtypeStruct(s, F32) for s in SMALL_FULL),
        in_specs=[tok(PROJ_W), _resident((D_MODEL, PROJ_W)), tok(D_MODEL), _resident((1, D_MODEL)), tok(D_MODEL),
                  pl.BlockSpec(memory_space=pl.ANY)] + [vm] * n_s,
        out_specs=(tok(D_MODEL),) + _scatter_out_specs(geoms) + (vm,) * len(SMALL_FULL),
        scratch_shapes=[pltpu.VMEM((8, D_MODEL), F32)] + _scatter_scratch(geoms) + _small_scratch()
        + [pltpu.VMEM(s, F32) for s in SMALL_FULL],
        compiler_params=_cparams(("arbitrary",)),
    )(dproj, win_g, x, g1, dx2, gin_p, *small)
    return res


def _wgrad(name, a, b, tm=None, tn=None, hosted=()):
    m_w, n_w = a.shape[-1], b.shape[-1]
    tm = m_w if tm is None else tm
    tn = n_w if tn is None else tn
    n_steps = (m_w // tm) * (n_w // tn)
    geoms = [g for g, _ in hosted]
    n_h = len(hosted)

    def body(a_ref, b_ref, *rest):
        o_ref = rest[n_h]
        if n_h:
            rs = _Scatters(geoms, rest[:n_h], rest[n_h + 1:2 * n_h + 1], rest[2 * n_h + 1:])
            step = pl.program_id(0) * (n_w // tn) + pl.program_id(1)
            pl.when(step == 0)(rs.phase1)
            pl.when(step == 1)(rs.phase2)
            pl.when(step == 2)(rs.phase2b)
        o_ref[...] = _dot_tn(a_ref[...].astype(BF16), b_ref[...].astype(BF16)).astype(BF16)
        if n_h:
            pl.when(step == n_steps - 1)(rs.phase3)

    assert not n_h or n_steps >= 4
    res = pl.pallas_call(
        body, name=name, grid=(m_w // tm, n_w // tn),
        out_shape=(jax.ShapeDtypeStruct((m_w, n_w), BF16),) + _scatter_out_shapes(geoms),
        in_specs=[pl.BlockSpec((SEQ, tm), lambda i, j: (0, i)), pl.BlockSpec((SEQ, tn), lambda i, j: (0, j))]
        + [pl.BlockSpec(memory_space=pl.ANY)] * n_h,
        out_specs=(pl.BlockSpec((tm, tn), lambda i, j: (i, j)),) + _scatter_out_specs(geoms),
        scratch_shapes=_scatter_scratch(geoms),
        compiler_params=_cparams(("arbitrary", "arbitrary") if n_h else ("parallel", "parallel")),
    )(a, b, *[p for _, p in hosted])
    return tuple(res[:1 + n_h])


def _row_step(half_rows):
    return max(s for s in range(16, 177, 16) if half_rows % s == 0)


class _Scatter:
    def __init__(self, geom, partial, out, land1, mine, stage2, land2, comb, s1_send, s1_recv, s2_send, s2_recv, ld_sems):
        self.w, self.row0, self.shape = _geom(geom)
        self.partial, self.out, self.land1 = partial, out, land1
        self.mine, self.stage2, self.land2, self.comb = mine, stage2, land2, comb
        self.hr = self.shape[0] // 2
        self.step = _row_step(self.hr)
        self.s1_send, self.s1_recv, self.s2_send, self.s2_recv, self.ld_sems = s1_send, s1_recv, s2_send, s2_recv, ld_sems
        self.x, self.y, self.c = lax.axis_index("x"), lax.axis_index("y"), lax.axis_index("c")
        self.sibling = (self.x, self.y, 1 - self.c)
        self.chips = [(self.x, self.y), (1 - self.x, self.y), (self.x, 1 - self.y), (1 - self.x, 1 - self.y)]

    def block(self, px, py, pc):
        dev = 4 * px + 2 * py + pc
        if self.w == W_IN:
            return self.partial.at[:, pl.ds(pl.multiple_of(dev * IN_SHARD, 128), IN_SHARD)]
        if self.w == W_OUT:
            return self.partial.at[pl.ds(pl.multiple_of(dev * OUT_SHARD, 128), OUT_SHARD), :]
        if self.w == W_DOWN:
            return self.partial.at[pl.ds(pl.multiple_of(dev * DOWN_SHARD, 32), DOWN_SHARD), :]
        return self.partial.at[pl.ds(pl.multiple_of(dev * FF_SHARD + self.row0, 32), self.shape[0]), :]

    def copy1(self, k):
        return pltpu.make_async_remote_copy(
            src_ref=self.block(*self.chips[k], 1 - self.c), dst_ref=self.land1.at[k],
            send_sem=self.s1_send.at[k], recv_sem=self.s1_recv.at[k], device_id=self.sibling, device_id_type=MESH)

    STAGE2 = [(1, 0, 1), (3, 0, 1), (2, 1, 2), (3, 1, 2), (1, 1, 1), (2, 0, 2)]

    def copy2(self, j):
        blk, h, to = self.STAGE2[j]
        src = self.comb.at[j - 4] if j >= 4 else self.stage2.at[blk - 1, pl.ds(h * self.hr, self.hr), :]
        return pltpu.make_async_remote_copy(
            src_ref=src, dst_ref=self.land2.at[j], send_sem=self.s2_send.at[j], recv_sem=self.s2_recv.at[j],
            device_id=(*self.chips[to], self.c), device_id_type=MESH)

    def _rows(self, h=None):
        step = self.step
        lo, n = (0, self.shape[0]) if h is None else (h * self.hr, self.hr)
        return [pl.ds(r0, step) for r0 in range(lo, lo + n, step)]

    def load(self, k):
        return pltpu.make_async_copy(self.block(*self.chips[k], self.c), self.mine.at[k], self.ld_sems.at[k])

    def phase1(self):
        for k in range(4):
            self.copy1(k).start()
        for k in range(4):
            self.load(k).start()

    def phase2(self):
        for k in (3, 1, 2, 0):
            self.copy1(k).wait_recv()
            self.load(k).wait()
            for rs in self._rows():
                s = self.mine[k, rs, :].astype(F32) + self.land1[k, rs, :].astype(F32)
                if k == 0:
                    self.out[rs, :] = s
                else:
                    self.stage2[k - 1, rs, :] = s.astype(BF16)
            for j in {3: (1, 3), 1: (0,), 2: (2,), 0: ()}[k]:
                self.copy2(j).start()

    def phase2b(self):
        for j, got in ((4, 3), (5, 1)):
            blk, h, _ = self.STAGE2[j]
            self.copy2(got).wait_recv()
            for i, rs in enumerate(self._rows(h)):
                lr = pl.ds(i * self.step, self.step)
                self.comb[j - 4, lr, :] = (self.stage2[blk - 1, rs, :].astype(F32) + self.land2[got, lr, :].astype(F32)).astype(BF16)
            self.copy2(j).start()

    def phase3(self):
        for j in (0, 5, 4, 2):
            self.copy2(j).wait_recv()
        for h, (first, second) in enumerate(((0, 5), (4, 2))):
            for i, rs in enumerate(self._rows(h)):
                lr = pl.ds(i * self.step, self.step)
                self.out[rs, :] = (self.out[rs, :] + self.land2[first, lr, :].astype(F32)) + self.land2[second, lr, :].astype(F32)
        for k in range(4):
            self.copy1(k).wait_send()
        for j in range(6):
            self.copy2(j).wait_send()


def _geom(geom):
    if isinstance(geom, tuple):
        w, row0, rows = geom
        assert w == W_UP
        return w, row0, (rows, SHARD[w][1])
    return geom, 0, SHARD[geom]


N_SCATTER_SCRATCH = 10


def _scatter_out_shapes(geoms):
    return tuple(jax.ShapeDtypeStruct(_geom(g)[2], F32) for g in geoms)


def _scatter_out_specs(geoms):
    return (pl.BlockSpec(memory_space=pltpu.VMEM),) * len(geoms)


def _scatter_scratch(geoms):
    out = []
    for g in geoms:
        s = _geom(g)[2]
        hs = (s[0] // 2, s[1])
        out += [pltpu.VMEM((4,) + s, BF16), pltpu.VMEM((4,) + s, BF16), pltpu.VMEM((3,) + s, BF16), pltpu.VMEM((6,) + hs, BF16),
                pltpu.VMEM((2,) + hs, BF16),
                pltpu.SemaphoreType.DMA((4,)), pltpu.SemaphoreType.DMA((4,)), pltpu.SemaphoreType.DMA((6,)),
                pltpu.SemaphoreType.DMA((6,)), pltpu.SemaphoreType.DMA((4,))]
    return out


class _Scatters:
    def __init__(self, geoms, p_refs, out_refs, scratch):
        k = N_SCATTER_SCRATCH
        self.items = [_Scatter(g, p_refs[i], out_refs[i], *scratch[k * i:k * i + k]) for i, g in enumerate(geoms)]

    def phase1(self):
        for s in self.items:
            s.phase1()

    def phase2(self):
        for s in self.items:
            s.phase2()

    def phase2b(self):
        for s in self.items:
            s.phase2b()

    def phase3(self):
        for s in self.items:
            s.phase3()


PACK_W = 1024


SMALL_FULL = [(2, 8, PACK_W), (HEADS, CHUNK, CHUNK), (2, 8, D_FF)]
SMALL_HALF = [(s[0] // 2,) + s[1:] for s in SMALL_FULL]
N_SMALL_SCRATCH = 16


def _small_scratch():
    n_a = len(SMALL_FULL)
    return ([pltpu.VMEM(SMALL_FULL[0], F32)] + [pltpu.VMEM(s, F32) for s in SMALL_HALF] + [pltpu.VMEM(s, F32) for s in SMALL_HALF]
            + [pltpu.VMEM((3,) + s, F32) for s in SMALL_HALF]
            + [pltpu.SemaphoreType.DMA((n_a,)), pltpu.SemaphoreType.DMA((n_a,)), pltpu.SemaphoreType.DMA((n_a, 3)),
               pltpu.SemaphoreType.DMA((n_a, 3)), pltpu.SemaphoreType.DMA((n_a,)), pltpu.SemaphoreType.DMA((n_a,))])


class _SmallReduce:
    def __init__(self, ins, outs, scratch):
        self.ins, self.outs = ins, outs
        (self.pack, *rest) = scratch
        self.rxs, self.css, self.gs = rest[0:3], rest[3:6], rest[6:9]
        self.s1_send, self.s1_recv, self.s2_send, self.s2_recv, self.s3_send, self.s3_recv = rest[9:]
        self.x, self.y, self.c = lax.axis_index("x"), lax.axis_index("y"), lax.axis_index("c")
        self.sibling = (self.x, self.y, 1 - self.c)
        self.chips = [(1 - self.x, self.y), (self.x, 1 - self.y), (1 - self.x, 1 - self.y)]
        self.hl = [s[0] for s in SMALL_HALF]

    def half(self, ref, a, h):
        return ref.at[pl.ds(h * self.hl[a], self.hl[a])]

    def begin(self):
        dg1_ref, dg2_ref, dgf_ref, dgrn_ref, dlng_ref, dlnb_ref, dbs_ref, loss_ref, dws_ref, dcv_ref = self.ins
        pack, c = self.pack, self.c
        pack[...] = jnp.zeros_like(pack)
        pack[0, 0:1, :] = dg1_ref[0:1, :]
        pack[0, 1:2, :] = dg2_ref[0:1, :]
        pack[0, 2:3, :] = dgf_ref[0:1, :]
        pack[0, 3:4, 0:RET_W] = dgrn_ref[0:1, :]
        lsum = loss_ref[0, 0:1, :]
        for i in range(1, N_TB):
            lsum = lsum + loss_ref[i, 0:1, :]
        pack[0, 3:4, RET_W:RET_W + 128] = lsum
        pack[1, 0:HEADS, 0:128] = dlng_ref[0:HEADS, :]
        pack[1, 0:HEADS, 128:256] = dlnb_ref[0:HEADS, :]
        pack[1, 0:HEADS, 256:384] = dbs_ref[0:HEADS, :]
        self.srcs = [pack, dws_ref, dcv_ref]
        n_a = len(self.srcs)
        self.ex1 = [pltpu.make_async_remote_copy(src_ref=self.half(self.srcs[a], a, 1 - c), dst_ref=self.rxs[a],
                                                 send_sem=self.s1_send.at[a], recv_sem=self.s1_recv.at[a],
                                                 device_id=self.sibling, device_id_type=MESH) for a in range(n_a)]
        for cp in self.ex1:
            cp.start()
        self.ex2 = []
        for a in range(n_a):
            self.ex1[a].wait_recv()
            self.css[a][...] = self.half(self.srcs[a], a, c)[...] + self.rxs[a][...]
            for j, chip in enumerate(self.chips):
                cp = pltpu.make_async_remote_copy(src_ref=self.css[a], dst_ref=self.gs[a].at[j], send_sem=self.s2_send.at[a, j],
                                                  recv_sem=self.s2_recv.at[a, j], device_id=(*chip, c), device_id_type=MESH)
                cp.start()
                self.ex2.append(cp)

    def end(self):
        c, x, y = self.c, self.x, self.y
        ex3 = []
        for a in range(len(self.srcs)):
            css, gs, out = self.css[a], self.gs[a], self.outs[a]
            for j in range(3):
                self.ex2[3 * a + j].wait_recv()
            tot = None
            for q in range(4):
                k = jnp.where(x != (q >> 1), 1, 0) + jnp.where(y != (q & 1), 2, 0)
                term = jnp.where(k == 0, css[...], jnp.where(k == 1, gs[0], jnp.where(k == 2, gs[1], gs[2])))
                tot = term if tot is None else tot + term
            self.half(out, a, c)[...] = tot
            cp = pltpu.make_async_remote_copy(src_ref=self.half(out, a, c), dst_ref=self.half(out, a, c), send_sem=self.s3_send.at[a],
                                              recv_sem=self.s3_recv.at[a], device_id=self.sibling, device_id_type=MESH)
            cp.start()
            ex3.append(cp)
        for a in range(len(self.srcs)):
            out = self.outs[a]
            pltpu.make_async_remote_copy(src_ref=self.half(out, a, 1 - c), dst_ref=self.half(out, a, 1 - c), send_sem=self.s3_send.at[a],
                                         recv_sem=self.s3_recv.at[a], device_id=self.sibling, device_id_type=MESH).wait_recv()
        for cp in self.ex1 + self.ex2 + ex3:
            cp.wait_send()


def _adam_math(w, g, m, v):
    nm = ADAM_B1 * m + (1.0 - ADAM_B1) * g
    nv = ADAM_B2 * v + (1.0 - ADAM_B2) * (g * g)
    d = -ADAM_LR * ((nm / (1.0 - ADAM_B1 ** ADAM_STEP)) / (jnp.sqrt(nv / (1.0 - ADAM_B2 ** ADAM_STEP)) + ADAM_EPS) + ADAM_WD * w)
    return d, nm, nv


def _adamw(name, w, gs, m, v, rows):
    _, r, cdim = w.shape
    n_steps = r // rows
    half = gs[0].shape[0] // rows

    def body(w_ref, *rest):
        g_refs, (m_ref, v_ref, go_ref, d_ref, nm_ref, nv_ref) = rest[:len(gs)], rest[len(gs):]
        gg = g_refs[0][...]
        if len(gs) == 2:
            gg = jnp.where(pl.program_id(0) < half, gg, g_refs[1][...])
        go_ref[0] = gg
        d, nm, nv = _adam_math(w_ref[0], gg, m_ref[0], v_ref[0])
        d_ref[0], nm_ref[0], nv_ref[0] = d, nm, nv

    spec3 = pl.BlockSpec((1, rows, cdim), lambda i: (0, i, 0))
    if len(gs) == 1:
        g_specs = [pl.BlockSpec((rows, cdim), lambda i: (i, 0))]
    else:
        g_specs = [pl.BlockSpec((rows, cdim), lambda i: (jnp.minimum(i, half - 1), 0)),
                   pl.BlockSpec((rows, cdim), lambda i: (jnp.maximum(i - half, 0), 0))]
    sh = jax.ShapeDtypeStruct((1, r, cdim), F32)
    return pl.pallas_call(
        body, name=name, grid=(n_steps,), out_shape=(sh, sh, sh, sh),
        in_specs=[spec3] + g_specs + [spec3, spec3], out_specs=(spec3,) * 4,
        compiler_params=_cparams(("parallel",)),
    )(w, *gs, m, v)


def _adamw_small(rp, rws, rcv, gcw, params):
    n_p = len(params)

    def body(*refs):
        rp_ref, rws_ref, rcv_ref, gcw_ref = refs[:4]
        ins = refs[4:4 + 3 * n_p]
        outs = refs[4 + 3 * n_p:]
        grads = [rp_ref[0, 0:1, :], rp_ref[0, 1:2, :], rp_ref[0, 2:3, :], rp_ref[0, 3:4, 0:RET_W],
                 rp_ref[1, 0:HEADS, 0:128], rp_ref[1, 0:HEADS, 128:256], rp_ref[1, 0:HEADS, 256:384],
                 rws_ref[...], gcw_ref[0], None]
        for p in range(n_p):
            w_ref, m_ref, v_ref = ins[3 * p:3 * p + 3]
            o = outs[4 * p:4 * p + 4]
            if p == n_p - 1:
                for hf in range(2):
                    cs = slice(hf * D_FF, (hf + 1) * D_FF)
                    g = rcv_ref[hf, 3:4, :]
                    res = (g,) + _adam_math(w_ref[:, cs], g, m_ref[:, cs], v_ref[:, cs])
                    for t in range(4):
                        o[t][:, cs] = res[t]
                continue
            lead = w_ref.ndim > grads[p].ndim
            rd = (lambda r: r[0]) if lead else (lambda r: r[...])
            res = (grads[p],) + _adam_math(rd(w_ref), grads[p], rd(m_ref), rd(v_ref))
            for t in range(4):
                if lead:
                    o[t][0] = res[t]
                else:
                    o[t][...] = res[t]

    vm = pl.BlockSpec(memory_space=pltpu.VMEM)
    flat = [a for tr in params for a in tr]
    out_shape = tuple(jax.ShapeDtypeStruct(tr[0].shape, F32) for tr in params for _ in range(4))
    res = pl.pallas_call(
        body, name="adamw_small", out_shape=out_shape, in_specs=[vm] * (4 + len(flat)), out_specs=(vm,) * len(out_shape),
        compiler_params=_cparams(),
    )(rp, rws, rcv, gcw, *flat)
    return [res[4 * p:4 * p + 4] for p in range(n_p)]


def kernel(x, mix_norm_g, w_in, ret_norm_g, sgu_ln_g, sgu_ln_b, sgu_w_s, sgu_b_s, w_out, ffn_norm_g, w_up, conv_w, conv_b, w_down, final_norm_g, loss_target, m_mix_norm_g, m_w_in, m_ret_norm_g, m_sgu_ln_g, m_sgu_ln_b, m_sgu_w_s, m_sgu_b_s, m_w_out, m_ffn_norm_g, m_w_up, m_conv_w, m_conv_b, m_w_down, m_final_norm_g, v_mix_norm_g, v_w_in, v_ret_norm_g, v_sgu_ln_g, v_sgu_ln_b, v_sgu_w_s, v_sgu_b_s, v_w_out, v_ffn_norm_g, v_w_up, v_conv_w, v_conv_b, v_w_down, v_final_norm_g):
    xs = x[0]
    tgt = loss_target[0]
    cos2, sin2 = _rope_tables()
    mask, qdec, kdec = _decay_tables()
    grn = ret_norm_g.reshape(1, RET_W)
    lng = sgu_ln_g.reshape(1, SGU_W)
    lnb = sgu_ln_b.reshape(1, SGU_W)
    ws = sgu_w_s[0]
    bsb = jnp.broadcast_to(sgu_b_s[0][:, :, None], (HEADS, CHUNK, HEAD_DIM))
    gf = final_norm_g.reshape(1, D_MODEL)
    me = 4 * lax.axis_index("x") + 2 * lax.axis_index("y") + lax.axis_index("c")
    tr = lambda a: jnp.transpose(a[0])[None]

    proj, h1, win_g, cw_sh, wout_g, su, sd = _fwd_proj(xs, mix_norm_g, cos2, sin2, w_in[0], w_out[0], tr(w_up)[0], w_down[0],
                                                      conv_w[0])
    cw_g = jnp.transpose(cw_sh, (1, 0, 2)).reshape(8, 2 * D_FF)
    x2, mixcat, o, sprev, wup_g = _fwd_mix(xs, proj, wout_g, grn, lng, lnb, ws, bsb, mask, qdec, kdec, su)
    h2, up_pre, u_conv, act, wdn_g = _fwd_ffn_up(x2, ffn_norm_g, wup_g, cw_g, conv_b, sd)
    x3, loss_parts = _fwd_ffn_down(x2, act, wdn_g, gf, tgt)

    dx3, dpre, dx2, dgf, dg2, dcv = _bwd_ffn(x3, tgt, gf, x2, ffn_norm_g, up_pre, u_conv, wup_g, cw_g, wdn_g)
    band = 512
    (gdn_p,) = _wgrad("wgrad_down", act, dx3, tm=FF_TILE)
    (gout_p,) = _wgrad("wgrad_out", mixcat, dx2, tn=512)
    gup_p, g_dn = _wgrad("wgrad_up", dpre, h2, tm=FF_TILE, hosted=[(W_DOWN, gdn_p)])
    dproj, dgrn, dlng, dlnb, dws, dbs, g_up_a = _bwd_mix(
        dx2, proj, o, sprev, wout_g, grn, lng, lnb, ws, bsb, mask, qdec, kdec, cos2, sin2, [((W_UP, 0, band), gup_p)])
    gin_p, g_up_b, g_out = _wgrad("wgrad_in", h1, dproj, tn=768,
                                  hosted=[((W_UP, band, FF_SHARD - band), gup_p), (W_OUT, gout_p)])
    grad_x, g_in, rp, rws, rcv = _bwd_proj(dproj, win_g, xs, mix_norm_g, dx2, gin_p,
                                           (dg2, dgf, dgrn, dlng, dlnb, dbs, loss_parts, dws, dcv))
    loss = rp[0, 3, RET_W]
    gcw = lax.dynamic_slice(rcv, (me // (N_DEV // 2), 0, (me % (N_DEV // 2)) * FF_SHARD), (1, 3, FF_SHARD))

    table = {}
    for name, w, gs, m, v, rows in (("w_in", w_in, [g_in], m_w_in, v_w_in, 256), ("w_out", w_out, [g_out], m_w_out, v_w_out, 128),
                                    ("w_up", tr(w_up), [g_up_a, g_up_b], tr(m_w_up), tr(v_w_up), 64),
                                    ("w_down", w_down, [g_dn], m_w_down, v_w_down, 88)):
        table[name] = _adamw("adamw_" + name, w, gs, m, v, rows)
    table["w_up"] = tuple(tr(a) for a in table["w_up"])
    row = lambda a: a.reshape(1, D_MODEL)
    names_small = ["mix_norm_g", "ffn_norm_g", "final_norm_g", "ret_norm_g", "sgu_ln_g", "sgu_ln_b", "sgu_b_s", "sgu_w_s",
                   "conv_w", "conv_b"]
    params = [(mix_norm_g, m_mix_norm_g, v_mix_norm_g), (ffn_norm_g, m_ffn_norm_g, v_ffn_norm_g),
              (row(final_norm_g), row(m_final_norm_g), row(v_final_norm_g)), (ret_norm_g, m_ret_norm_g, v_ret_norm_g),
              (sgu_ln_g, m_sgu_ln_g, v_sgu_ln_g), (sgu_ln_b, m_sgu_ln_b, v_sgu_ln_b), (sgu_b_s, m_sgu_b_s, v_sgu_b_s),
              (sgu_w_s, m_sgu_w_s, v_sgu_w_s), (conv_w, m_conv_w, v_conv_w), (conv_b, m_conv_b, v_conv_b)]
    for n, res in zip(names_small, _adamw_small(rp, rws, rcv, gcw, params)):
        table[n] = res
    table["final_norm_g"] = tuple(a.reshape(D_MODEL) for a in table["final_norm_g"])

    order = ["mix_norm_g", "w_in", "ret_norm_g", "sgu_ln_g", "sgu_ln_b", "sgu_w_s", "sgu_b_s", "w_out", "ffn_norm_g", "w_up",
             "conv_w", "conv_b", "w_down", "final_norm_g"]
    outs = [loss, grad_x[None]]
    for col in range(4):
        outs += [table[n][col] for n in order]
    return tuple(outs)
```

```python
import functools
import math

import jax
import jax.numpy as jnp
import numpy as np
from jax import lax
from jax.experimental import pallas as pl
from jax.experimental.pallas import tpu as pltpu

F32 = jnp.float32
BF16 = jnp.bfloat16
MESH = pl.DeviceIdType.MESH

N_DEV = 8
SEQ = 2048
D_MODEL = 1024
CHUNK = 128
N_CHUNK = SEQ // CHUNK
HEADS = 4
HEAD_DIM = 128
RET_W = 512
SGU_W = 512
PROJ_W = 3072
D_FF = 2816
FF_SHARD = 704
FF_TILE = 1408
FF_TILES = ((0, 1536), (1536, 1280))
IN_SHARD = PROJ_W // N_DEV
OUT_SHARD = D_MODEL // N_DEV
DOWN_SHARD = D_FF // N_DEV
TM = 256
N_TB = SEQ // TM
EPS = 1e-6
ROPE_BASE = 10000.0
K_SCALE = HEAD_DIM ** -0.5
INV_SQRT2 = 0.7071067811865476
INV_SQRT_2PI = 0.3989422804014327

ADAM_LR = 0.001
ADAM_B1 = 0.9
ADAM_B2 = 0.999
ADAM_EPS = 1e-08
ADAM_WD = 0.01
ADAM_STEP = 10

VMEM_LIMIT = 56 * 1024 * 1024


def _cparams(sem=None, vmem=VMEM_LIMIT):
    return pltpu.CompilerParams(dimension_semantics=sem, vmem_limit_bytes=vmem)


def _resident(shape):
    nd = len(shape)
    return pl.BlockSpec(shape, lambda *_: (0,) * nd, pipeline_mode=pl.Buffered(1))


def _dot(a, b):
    return jnp.dot(a, b, preferred_element_type=F32)


def _dot_nt(a, b):
    return lax.dot_general(a, b, (((1,), (1,)), ((), ())), preferred_element_type=F32)


def _dot_tn(a, b):
    return lax.dot_general(a, b, (((0,), (0,)), ((), ())), preferred_element_type=F32)


def _sigmoid(x):
    return 1.0 / (1.0 + jnp.exp(-x))


def _gelu(x):
    return 0.5 * x * (1.0 + lax.erf(x * INV_SQRT2))


def _gelu_grad(x):
    return 0.5 * (1.0 + lax.erf(x * INV_SQRT2)) + x * (jnp.exp(-0.5 * x * x) * INV_SQRT_2PI)


def _rot(xh, cos2, sin2):
    return xh * cos2 + pltpu.roll(xh, HEAD_DIM // 2, 1) * sin2


def _rot_t(dh, cos2, sin2):
    return dh * cos2 + pltpu.roll(dh * sin2, HEAD_DIM // 2, 1)


def _rope_tables():
    half = HEAD_DIM // 2
    inv_freq = jnp.power(ROPE_BASE, -jnp.arange(half, dtype=F32) / half)
    ang = jnp.arange(SEQ, dtype=F32)[:, None] * inv_freq[None, :]
    cos, sin = jnp.cos(ang), jnp.sin(ang)
    cos2 = jnp.concatenate([cos, cos], axis=-1)
    sin2 = jnp.concatenate([-sin, sin], axis=-1)
    return cos2, sin2


def _decay_tables():
    log_gamma = jnp.log(1.0 - jnp.power(2.0, -5.0 - jnp.arange(HEADS, dtype=F32)))
    pos = jnp.arange(CHUNK, dtype=F32)
    diff = pos[:, None] - pos[None, :]
    mask = jnp.where(diff >= 0.0, jnp.exp(log_gamma[:, None, None] * jnp.maximum(diff, 0.0)[None]), 0.0)
    k_decay = jnp.exp(log_gamma[:, None] * (CHUNK - 1.0 - pos)[None])
    q_decay = jnp.exp(log_gamma[:, None] * (pos + 1.0)[None])
    kd = jnp.broadcast_to(k_decay[:, :, None], (HEADS, CHUNK, HEAD_DIM))
    qd = jnp.broadcast_to(q_decay[:, :, None], (HEADS, CHUNK, HEAD_DIM))
    return mask.astype(F32), qd.astype(F32), kd.astype(F32)


def _chunk_decay():
    lg = np.log(np.float32(1.0) - np.power(np.float32(2.0), -5.0 - np.arange(HEADS, dtype=np.float32))).astype(np.float32)
    return [float(np.exp(lg[h] * np.float32(CHUNK))) for h in range(HEADS)]


W_IN, W_OUT, W_UP, W_DOWN, W_CONV = range(5)
GATHERED = {W_IN: ((D_MODEL, PROJ_W), BF16), W_OUT: ((D_MODEL, D_MODEL), BF16), W_UP: ((2 * D_FF, D_MODEL), BF16),
            W_DOWN: ((D_FF, D_MODEL), BF16), W_CONV: ((N_DEV, 8, FF_SHARD), F32)}
SHARD = {W_IN: (D_MODEL, IN_SHARD), W_OUT: (OUT_SHARD, D_MODEL), W_UP: (FF_SHARD, D_MODEL), W_DOWN: (DOWN_SHARD, D_MODEL),
         W_CONV: (8, FF_SHARD)}


class _Gather:
    N_SEMS = 9

    def __init__(self, ids, stages, gathered, send_sems, recv_sems, local_sems):
        self.ids, self.stages, self.gathered = ids, stages, gathered
        self.send_sems, self.recv_sems, self.local_sems = send_sems, recv_sems, local_sems
        self.x, self.y, self.c = lax.axis_index("x"), lax.axis_index("y"), lax.axis_index("c")
        self.me = (self.x, self.y, self.c)
        self.sibling = (self.x, self.y, 1 - self.c)
        self.chips = [(1 - self.x, self.y), (self.x, 1 - self.y), (1 - self.x, 1 - self.y)]

    def slot(self, n, px, py, pc):
        dev = 4 * px + 2 * py + pc
        w, g = self.ids[n], self.gathered[n]
        if w == W_IN:
            return g.at[:, pl.ds(pl.multiple_of(dev * IN_SHARD, 128), IN_SHARD)]
        if w == W_OUT:
            return g.at[pl.ds(pl.multiple_of(dev * OUT_SHARD, 128), OUT_SHARD), :]
        if w == W_DOWN:
            return g.at[pl.ds(pl.multiple_of(dev * DOWN_SHARD, 32), DOWN_SHARD), :]
        if w == W_UP:
            return g.at[pl.ds(pl.multiple_of(dev * FF_SHARD, 32), FF_SHARD), :]
        return g.at[dev]

    def half(self, n, px, py, pc, h):
        dev = 4 * px + 2 * py + pc
        w, g = self.ids[n], self.gathered[n]
        if w == W_IN:
            return g.at[pl.ds(h * (D_MODEL // 2), D_MODEL // 2), pl.ds(pl.multiple_of(dev * IN_SHARD, 128), IN_SHARD)]
        rows = SHARD[w][0] // 2
        return g.at[pl.ds(pl.multiple_of(dev * SHARD[w][0] + h * rows, 16), rows), :]

    def tree(self, n):
        return self.ids[n] != W_CONV

    def copy(self, n, k, block, to, src=None, h=None):
        ref = self.slot(n, *block) if h is None else self.half(n, *block, h)
        return pltpu.make_async_remote_copy(
            src_ref=ref if src is None else src, dst_ref=ref,
            send_sem=self.send_sems.at[n, k], recv_sem=self.recv_sems.at[n, k], device_id=to, device_id_type=MESH)

    def _mine(self):
        return [pltpu.make_async_copy(self.stages[n], self.slot(n, *self.me), self.local_sems.at[n]) for n in range(len(self.ids))]

    def _first(self):
        out = []
        for n in range(len(self.ids)):
            out.append(self.copy(n, 0, self.me, self.sibling, src=self.stages[n]))
            out += [self.copy(n, 1 + j, self.me, (*chip, self.c), src=self.stages[n])
                    for j, chip in enumerate(self.chips[:2] if self.tree(n) else self.chips)]
        return out

    def start(self):
        for cp in self._mine() + self._first():
            cp.start()

    def finish(self):
        self.near(0)
        self.near(1)
        self.far()
        for j in range(4):
            self.arrived(j)
        self.wait_sends()

    def _passed(self, j):
        dev = (*self.chips[j], self.c)
        out = []
        for n in range(len(self.ids)):
            if not self.tree(n):
                out.append(self.copy(n, 4 + j, dev, self.sibling))
            elif j < 2:
                out += [self.copy(n, 3 + j, dev, (*self.chips[1 - j], self.c), h=j), self.copy(n, 5 + j, dev, self.sibling)]
            else:
                out += [self.copy(n, 7, dev, self.sibling, h=0), self.copy(n, 8, dev, self.sibling, h=1)]
        return out

    def near(self, j):
        dev = (*self.chips[j], self.c)
        for n in range(len(self.ids)):
            self.copy(n, 1 + j, dev, self.me).wait_recv()
        for cp in self._passed(j):
            cp.start()

    def far(self):
        dev = (*self.chips[2], self.c)
        for n in range(len(self.ids)):
            if self.tree(n):
                self.copy(n, 3, dev, self.me, h=0).wait_recv()
                self.copy(n, 4, dev, self.me, h=1).wait_recv()
            else:
                self.copy(n, 3, dev, self.me).wait_recv()
        for cp in self._passed(2):
            cp.start()

    def arrived(self, j):
        for n in range(len(self.ids)):
            if j == 0:
                self.copy(n, 0, self.sibling, self.me).wait_recv()
                continue
            dev = (*self.chips[j - 1], 1 - self.c)
            if not self.tree(n):
                self.copy(n, 3 + j, dev, self.me).wait_recv()
            elif j < 3:
                self.copy(n, 4 + j, dev, self.me).wait_recv()
            else:
                self.copy(n, 7, dev, self.me, h=0).wait_recv()
                self.copy(n, 8, dev, self.me, h=1).wait_recv()
        if j == 0:
            for cp in self._mine():
                cp.wait()

    def wait_sends(self):
        for cp in self._first() + self._passed(0) + self._passed(1) + self._passed(2):
            cp.wait_send()


def _gather_scratch(n):
    return [pltpu.SemaphoreType.DMA((n, _Gather.N_SEMS)), pltpu.SemaphoreType.DMA((n, _Gather.N_SEMS)), pltpu.SemaphoreType.DMA((n,))]


def _gathered_shapes(ids):
    return tuple(jax.ShapeDtypeStruct(*GATHERED[w]) for w in ids)


def _fwd_proj(x, g1, cos2, sin2, w_in, w_out, w_up, w_down, conv_w):
    ids_a, ids_b = [W_IN, W_CONV], [W_OUT, W_DOWN]
    tp = 2 * TM
    cols = 2 * IN_SHARD
    blocks = cols // HEAD_DIM
    xi, yi = lax.axis_index("x"), lax.axis_index("y")
    order = jnp.stack([2 * xi + yi, 2 * (1 - xi) + yi, 2 * xi + (1 - yi), 2 * (1 - xi) + (1 - yi)]).astype(jnp.int32)

    def body(order_ref, x_ref, g_ref, cos_ref, sin_ref, in_hbm, out_hbm, up_hbm, dn_hbm, cw_ref,
             proj_ref, h1_ref, gin, gcw, gout, gdn, su_ref,
             w_vm, s_in, s_cw, s_out, s_dn, f_in, f_out, f_up, f_dn, ld_sems,
             a_send, a_recv, a_local, b_send, b_recv, b_local):
        ag_a = _Gather(ids_a, [s_in, s_cw], [gin, gcw], a_send, a_recv, a_local)
        ag_b = _Gather(ids_b, [s_out, s_dn], [gout, gdn], b_send, b_recv, b_local)
        p, i = pl.program_id(0), pl.program_id(1)
        chip = order_ref[p]

        def fill():
            cp = pltpu.make_async_copy(gin.at[:, pl.ds(pl.multiple_of(chip * cols, 128), cols)], w_vm, ld_sems.at[4])
            cp.start()
            cp.wait()

        @pl.when((i == 0) & (p == 0))
        def _():
            loads = [pltpu.make_async_copy(src, dst, ld_sems.at[k])
                     for k, (src, dst) in enumerate(((in_hbm, f_in), (out_hbm, f_out), (dn_hbm, f_dn), (up_hbm, f_up)))]
            for cp in loads:
                cp.start()
            s_cw[...] = jnp.zeros_like(s_cw)
            s_cw[0:3, :] = cw_ref[...]
            loads[0].wait()
            s_in[...] = f_in[...].astype(BF16)
            ag_a.start()
            loads[1].wait()
            s_out[...] = f_out[...].astype(BF16)
            loads[2].wait()
            s_dn[...] = f_dn[...].astype(BF16)
            loads[3].wait()
            su_ref[...] = f_up[...].astype(BF16)
            ag_a.arrived(0)
            fill()

        @pl.when((i == 0) & (p == 1))
        def _():
            ag_a.near(0)
            ag_a.arrived(1)
            fill()

        @pl.when((i == 0) & (p == 2))
        def _():
            ag_a.near(1)
            ag_b.start()
            ag_a.arrived(2)
            fill()

        @pl.when((i == 0) & (p == 3))
        def _():
            ag_a.far()
            ag_a.arrived(3)
            ag_a.wait_sends()
            fill()

        xb = x_ref[...]
        r = lax.rsqrt(jnp.mean(xb * xb, axis=-1, keepdims=True) + EPS)
        h = ((xb * r) * g_ref[...]).astype(BF16)

        @pl.when(p == 0)
        def _():
            h1_ref[...] = h

        pr = _dot(h, w_vm[...])
        c2, s2 = cos_ref[...], sin_ref[...]
        for j in range(blocks):
            gb = blocks * chip + j
            sl = slice(j * HEAD_DIM, (j + 1) * HEAD_DIM)
            rot = _rot(pr[:, sl], c2, s2)
            proj_ref[:, sl] = jnp.where(gb < HEADS, rot, jnp.where(gb < 2 * HEADS, rot * K_SCALE, pr[:, sl]))

        @pl.when((p == 3) & (i == SEQ // tp - 1))
        def _():
            ag_b.finish()

    tok = lambda w: pl.BlockSpec((tp, w), lambda p, i, o: (i, 0))
    hbm = pl.BlockSpec(memory_space=pl.ANY)
    vm = pl.BlockSpec(memory_space=pltpu.VMEM)
    return pl.pallas_call(
        body, name="fwd_proj",
        grid_spec=pltpu.PrefetchScalarGridSpec(
            num_scalar_prefetch=1, grid=(4, SEQ // tp),
            in_specs=[tok(D_MODEL), _resident((1, D_MODEL)), tok(HEAD_DIM), tok(HEAD_DIM), hbm, hbm, hbm, hbm, vm],
            out_specs=(pl.BlockSpec((tp, cols), lambda p, i, o: (i, o[p])),
                       pl.BlockSpec((tp, D_MODEL), lambda p, i, o: (jnp.where(p == 0, i, SEQ // tp - 1), 0)),
                       hbm, hbm, hbm, hbm, vm),
            scratch_shapes=[pltpu.VMEM((D_MODEL, cols), BF16), pltpu.VMEM(SHARD[W_IN], BF16), pltpu.VMEM(SHARD[W_CONV], F32),
                            pltpu.VMEM(SHARD[W_OUT], BF16), pltpu.VMEM(SHARD[W_DOWN], BF16),
                            pltpu.VMEM(SHARD[W_IN], F32), pltpu.VMEM(SHARD[W_OUT], F32), pltpu.VMEM(SHARD[W_UP], F32),
                            pltpu.VMEM(SHARD[W_DOWN], F32), pltpu.SemaphoreType.DMA((5,))]
            + _gather_scratch(len(ids_a)) + _gather_scratch(len(ids_b))),
        out_shape=(jax.ShapeDtypeStruct((SEQ, PROJ_W), F32), jax.ShapeDtypeStruct((SEQ, D_MODEL), BF16))
        + _gathered_shapes(ids_a + ids_b) + (jax.ShapeDtypeStruct(SHARD[W_UP], BF16),),
        compiler_params=_cparams(("arbitrary", "arbitrary")),
    )(order, x, g1, cos2, sin2, w_in, w_out, w_up, w_down, conv_w)


def _causal(w):
    r = lax.broadcasted_iota(jnp.int32, (CHUNK, CHUNK), 0)
    c = lax.broadcasted_iota(jnp.int32, (CHUNK, CHUNK), 1)
    return jnp.where(r >= c, w, 0.0)


def _fwd_mix(x, proj, wout_g, grn, lng, lnb, ws, bsb, mask, qdec, kdec, su):
    cdec = _chunk_decay()
    ids = [W_UP]

    def body(x_ref, p_ref, w_ref, grn_ref, lng_ref, lnb_ref, ws_ref, bsb_ref, m_ref, qd_ref, kd_ref, su_ref,
             x2_ref, cat_ref, o_ref, sp_ref, gup, state, send_sems, recv_sems, local_sems):
        ag = _Gather(ids, [su_ref], [gup], send_sems, recv_sems, local_sems)

        @pl.when(pl.program_id(0) == 0)
        def _():
            state[...] = jnp.zeros_like(state)
            ag.start()

        for h in range(HEADS):
            sl = slice(h * HEAD_DIM, (h + 1) * HEAD_DIM)
            q = p_ref[:, sl]
            k = p_ref[:, RET_W + h * HEAD_DIM:RET_W + (h + 1) * HEAD_DIM]
            v = p_ref[:, 2 * RET_W + h * HEAD_DIM:2 * RET_W + (h + 1) * HEAD_DIM]
            g = p_ref[:, 3 * RET_W + h * HEAD_DIM:3 * RET_W + (h + 1) * HEAD_DIM]
            qb, kb, vb = q.astype(BF16), k.astype(BF16), v.astype(BF16)
            a = _dot_nt(qb, kb) * m_ref[h]
            spb = state[h].astype(BF16)
            sp_ref[0, h] = spb
            o = _dot(a.astype(BF16), vb) + _dot((q * qd_ref[h]).astype(BF16), spb)
            state[h] = state[h] * cdec[h] + _dot_tn((k * kd_ref[h]).astype(BF16), vb)
            o_ref[:, sl] = o
            rinv = lax.rsqrt(jnp.mean(o * o, axis=-1, keepdims=True) + EPS)
            rn = (o * rinv) * grn_ref[:, sl]
            cat_ref[:, sl] = ((g * _sigmoid(g)) * rn).astype(BF16)
        for gi in range(HEADS):
            sl = slice(gi * HEAD_DIM, (gi + 1) * HEAD_DIM)
            u = p_ref[:, 4 * RET_W + gi * HEAD_DIM:4 * RET_W + (gi + 1) * HEAD_DIM]
            sv = p_ref[:, 4 * RET_W + SGU_W + gi * HEAD_DIM:4 * RET_W + SGU_W + (gi + 1) * HEAD_DIM]
            gv = _gelu(sv)
            xc = gv - jnp.mean(gv, axis=-1, keepdims=True)
            vn = (xc * lax.rsqrt(jnp.mean(xc * xc, axis=-1, keepdims=True) + EPS)) * lng_ref[:, sl] + lnb_ref[:, sl]
            mixed = _dot(_causal(ws_ref[gi]).astype(BF16), vn.astype(BF16)) + bsb_ref[gi]
            cat_ref[:, RET_W + gi * HEAD_DIM:RET_W + (gi + 1) * HEAD_DIM] = (_gelu(u) * mixed).astype(BF16)
        x2_ref[...] = x_ref[...] + _dot(cat_ref[...], w_ref[...])

        @pl.when(pl.program_id(0) == N_CHUNK - 1)
        def _():
            ag.finish()

    ch = lambda w: pl.BlockSpec((CHUNK, w), lambda i: (i, 0))
    hcc = (HEADS, CHUNK, CHUNK)
    hbm = pl.BlockSpec(memory_space=pl.ANY)
    return pl.pallas_call(
        body, name="fwd_mix", grid=(N_CHUNK,),
        out_shape=(jax.ShapeDtypeStruct((SEQ, D_MODEL), F32), jax.ShapeDtypeStruct((SEQ, D_MODEL), BF16),
                   jax.ShapeDtypeStruct((SEQ, RET_W), F32), jax.ShapeDtypeStruct((N_CHUNK, HEADS, HEAD_DIM, HEAD_DIM), BF16))
        + _gathered_shapes(ids),
        in_specs=[ch(D_MODEL), ch(PROJ_W), _resident((D_MODEL, D_MODEL)), _resident((1, RET_W)), _resident((1, SGU_W)),
                  _resident((1, SGU_W)), _resident(hcc), _resident(hcc), _resident(hcc), _resident(hcc), _resident(hcc), hbm],
        out_specs=(ch(D_MODEL), ch(D_MODEL), ch(RET_W), pl.BlockSpec((1, HEADS, HEAD_DIM, HEAD_DIM), lambda i: (i, 0, 0, 0)), hbm),
        scratch_shapes=[pltpu.VMEM((HEADS, HEAD_DIM, HEAD_DIM), F32)] + _gather_scratch(len(ids)),
        compiler_params=_cparams(("arbitrary",)),
    )(x, proj, wout_g, grn, lng, lnb, ws, bsb, mask, qdec, kdec, su)


def _conv_taps(p, prev8):
    row = lax.broadcasted_iota(jnp.int32, p.shape, 0)
    p1 = jnp.where(row == 0, prev8[7:8, :], pltpu.roll(p, 1, 0))
    p2 = jnp.where(row == 0, prev8[6:7, :], jnp.where(row == 1, prev8[7:8, :], pltpu.roll(p, 2, 0)))
    return p1, p2


def _fwd_ffn(x2, g2, wup_g, cw_g, cb_g, wdn_g, gf, tgt):
    def body(x_ref, g_ref, wu_ref, cw_ref, cb_ref, wd_ref, gf_ref, t_ref, h2_ref, up_ref, u_ref, act_ref, x3_ref, loss_ref, carry):
        @pl.when(pl.program_id(0) == 0)
        def _():
            carry[...] = jnp.zeros_like(carry)

        xb = x_ref[...]
        r = lax.rsqrt(jnp.mean(xb * xb, axis=-1, keepdims=True) + EPS)
        h = ((xb * r) * g_ref[...]).astype(BF16)
        h2_ref[...] = h
        acc = xb
        for t0, tw in FF_TILES:
            u = []
            for c0 in (t0, D_FF + t0):
                cs = slice(c0, c0 + tw)
                p = _dot_nt(h, wu_ref[pl.ds(c0, tw), :])
                up_ref[:, cs] = p.astype(BF16)
                p1, p2 = _conv_taps(p, carry[:, cs])
                carry[:, cs] = p[TM - 8:, :]
                us = p2 * cw_ref[0:1, cs] + p1 * cw_ref[1:2, cs] + p * cw_ref[2:3, cs] + cb_ref[:, cs]
                u_ref[:, cs] = us.astype(BF16)
                u.append(us)
            a = ((u[0] * _sigmoid(u[0])) * u[1]).astype(BF16)
            act_ref[:, t0:t0 + tw] = a
            acc = acc + _dot(a, wd_ref[pl.ds(t0, tw), :])
        x3_ref[...] = acc
        r3 = lax.rsqrt(jnp.mean(acc * acc, axis=-1, keepdims=True) + EPS)
        diff = (acc * r3) * gf_ref[...] - t_ref[...]
        loss_ref[...] = jnp.full(loss_ref.shape, 0.5 * jnp.sum(jnp.mean(diff * diff, axis=-1)), F32)

    tok = lambda w: pl.BlockSpec((TM, w), lambda i: (i, 0))
    return pl.pallas_call(
        body, name="fwd_ffn", grid=(N_TB,),
        out_shape=(jax.ShapeDtypeStruct((SEQ, D_MODEL), BF16), jax.ShapeDtypeStruct((SEQ, 2 * D_FF), BF16),
                   jax.ShapeDtypeStruct((SEQ, 2 * D_FF), BF16),
                   jax.ShapeDtypeStruct((SEQ, D_FF), BF16), jax.ShapeDtypeStruct((SEQ, D_MODEL), F32),
                   jax.ShapeDtypeStruct((N_TB, 8, 128), F32)),
        in_specs=[tok(D_MODEL), _resident((1, D_MODEL)), _resident((2 * D_FF, D_MODEL)), _resident((8, 2 * D_FF)),
                  _resident((1, 2 * D_FF)), _resident((D_FF, D_MODEL)), _resident((1, D_MODEL)), tok(D_MODEL)],
        out_specs=(tok(D_MODEL), tok(2 * D_FF), tok(2 * D_FF), tok(D_FF), tok(D_MODEL),
                   pl.BlockSpec((1, 8, 128), lambda i: (i, 0, 0))),
        scratch_shapes=[pltpu.VMEM((8, 2 * D_FF), F32)],
        compiler_params=_cparams(("arbitrary",)),
    )(x2, g2, wup_g, cw_g, cb_g, wdn_g, gf, tgt)


def _bwd_ffn(x3, tgt, gf, x2, g2, up_pre, u_conv, wup_g, cw_g, wdn_g):
    def body(x3_ref, t_ref, gf_ref, x2_ref, g2_ref, up_ref, u_ref, wu_ref, cw_ref, wd_ref,
             dx3_ref, dpre_ref, dx2_ref, dgf_ref, dg2_ref, dcv_ref, nxt):
        i = pl.program_id(0)

        @pl.when(i == 0)
        def _():
            nxt[...] = jnp.zeros_like(nxt)
            dgf_ref[...] = jnp.zeros_like(dgf_ref)
            dg2_ref[...] = jnp.zeros_like(dg2_ref)
            dcv_ref[...] = jnp.zeros_like(dcv_ref)

        x3 = x3_ref[...]
        r3 = lax.rsqrt(jnp.mean(x3 * x3, axis=-1, keepdims=True) + EPS)
        xh3 = x3 * r3
        dy = (xh3 * gf_ref[...] - t_ref[...]) * (1.0 / D_MODEL)
        dgf_ref[0:1, :] += jnp.sum(dy * xh3, axis=0, keepdims=True)
        t3 = dy * gf_ref[...]
        dx3 = r3 * (t3 - xh3 * jnp.mean(t3 * xh3, axis=-1, keepdims=True))
        dx3b = dx3.astype(BF16)
        dx3_ref[...] = dx3b
        dh2 = jnp.zeros((TM, D_MODEL), F32)
        for t0, tw in FF_TILES:
            row = lax.broadcasted_iota(jnp.int32, (TM, tw), 0)
            ts = slice(t0, t0 + tw)
            dact = _dot_nt(dx3b, wd_ref[pl.ds(t0, tw), :])
            ua = u_ref[:, ts].astype(F32)
            ub = u_ref[:, D_FF + t0:D_FF + t0 + tw].astype(F32)
            sg = _sigmoid(ua)
            du = [dact * ub * (sg * (1.0 + ua * (1.0 - sg))), dact * (ua * sg)]
            for n in range(2):
                d = du[n]
                c0 = n * D_FF + t0
                cs = slice(c0, c0 + tw)
                nx = nxt[:, cs]
                n1 = jnp.where(row == TM - 1, nx[0:1, :], pltpu.roll(d, TM - 1, 0))
                n2 = jnp.where(row == TM - 2, nx[0:1, :], jnp.where(row == TM - 1, nx[1:2, :], pltpu.roll(d, TM - 2, 0)))
                nxt[:, cs] = d[0:8, :]
                dp = (d * cw_ref[2:3, cs] + n1 * cw_ref[1:2, cs] + n2 * cw_ref[0:1, cs]).astype(BF16)
                dpre_ref[:, cs] = dp
                p = up_ref[:, cs].astype(F32)
                dcv_ref[n, 0:1, ts] += jnp.sum(n2 * p, axis=0, keepdims=True)
                dcv_ref[n, 1:2, ts] += jnp.sum(n1 * p, axis=0, keepdims=True)
                dcv_ref[n, 2:3, ts] += jnp.sum(d * p, axis=0, keepdims=True)
                dcv_ref[n, 3:4, ts] += jnp.sum(d, axis=0, keepdims=True)
                dh2 = dh2 + _dot(dp, wu_ref[pl.ds(c0, tw), :])
        x2 = x2_ref[...]
        r2 = lax.rsqrt(jnp.mean(x2 * x2, axis=-1, keepdims=True) + EPS)
        xh2 = x2 * r2
        dg2_ref[0:1, :] += jnp.sum(dh2 * xh2, axis=0, keepdims=True)
        t2 = dh2 * g2_ref[...]
        dx2_ref[...] = dx3 + r2 * (t2 - xh2 * jnp.mean(t2 * xh2, axis=-1, keepdims=True))

    rev = lambda w: pl.BlockSpec((TM, w), lambda i: (N_TB - 1 - i, 0))
    acc = lambda s: pl.BlockSpec(s, lambda i: (0,) * len(s))
    return pl.pallas_call(
        body, name="bwd_ffn", grid=(N_TB,),
        out_shape=(jax.ShapeDtypeStruct((SEQ, D_MODEL), BF16), jax.ShapeDtypeStruct((SEQ, 2 * D_FF), BF16),
                   jax.ShapeDtypeStruct((SEQ, D_MODEL), F32), jax.ShapeDtypeStruct((8, D_MODEL), F32),
                   jax.ShapeDtypeStruct((8, D_MODEL), F32), jax.ShapeDtypeStruct((2, 8, D_FF), F32)),
        in_specs=[rev(D_MODEL), rev(D_MODEL), _resident((1, D_MODEL)), rev(D_MODEL), _resident((1, D_MODEL)), rev(2 * D_FF),
                  rev(2 * D_FF), _resident((2 * D_FF, D_MODEL)), _resident((8, 2 * D_FF)), _resident((D_FF, D_MODEL))],
        out_specs=(rev(D_MODEL), rev(2 * D_FF), rev(D_MODEL), acc((8, D_MODEL)), acc((8, D_MODEL)), acc((2, 8, D_FF))),
        scratch_shapes=[pltpu.VMEM((8, 2 * D_FF), F32)],
        compiler_params=_cparams(("arbitrary",)),
    )(x3, tgt, gf, x2, g2, up_pre, u_conv, wup_g, cw_g, wdn_g)


def _bwd_mix(dx2, proj, o, sprev, wout_g, grn, lng, lnb, ws, bsb, mask, qdec, kdec, cos2, sin2, hosted):
    cdec = _chunk_decay()
    geoms = [g for g, _ in hosted]
    n_h = len(hosted)

    def body(dx2_ref, p_ref, o_ref, sp_ref, w_ref, grn_ref, lng_ref, lnb_ref, ws_ref, bsb_ref, m_ref, qd_ref, kd_ref,
             cos_ref, sin_ref, *rest):
        dp_ref, dgrn_ref, dlng_ref, dlnb_ref, dws_ref, dbs_ref = rest[n_h:n_h + 6]
        dstate, dbs_acc = rest[2 * n_h + 6:2 * n_h + 8]
        i = pl.program_id(0)
        rs = _Scatters(geoms, rest[:n_h], rest[n_h + 6:2 * n_h + 6], rest[2 * n_h + 8:])
        pl.when(i == 0)(rs.phase1)
        pl.when(i == 3)(rs.phase2)
        pl.when(i == 6)(rs.phase2b)

        @pl.when(i == 0)
        def _():
            dstate[...] = jnp.zeros_like(dstate)
            dgrn_ref[...] = jnp.zeros_like(dgrn_ref)
            dlng_ref[...] = jnp.zeros_like(dlng_ref)
            dlnb_ref[...] = jnp.zeros_like(dlnb_ref)
            dws_ref[...] = jnp.zeros_like(dws_ref)
            dbs_ref[...] = jnp.zeros_like(dbs_ref)
            dbs_acc[...] = jnp.zeros_like(dbs_acc)

        dmix = _dot_nt(dx2_ref[...].astype(BF16), w_ref[...])
        for h in range(HEADS):
            sl = slice(h * HEAD_DIM, (h + 1) * HEAD_DIM)
            q = p_ref[:, sl]
            k = p_ref[:, RET_W + h * HEAD_DIM:RET_W + (h + 1) * HEAD_DIM]
            v = p_ref[:, 2 * RET_W + h * HEAD_DIM:2 * RET_W + (h + 1) * HEAD_DIM]
            g = p_ref[:, 3 * RET_W + h * HEAD_DIM:3 * RET_W + (h + 1) * HEAD_DIM]
            o = o_ref[:, sl]
            rinv = lax.rsqrt(jnp.mean(o * o, axis=-1, keepdims=True) + EPS)
            oh = o * rinv
            gr = grn_ref[:, sl]
            sg = _sigmoid(g)
            dret = dmix[:, sl]
            dp_ref[:, 3 * RET_W + h * HEAD_DIM:3 * RET_W + (h + 1) * HEAD_DIM] = (
                dret * (oh * gr) * (sg * (1.0 + g * (1.0 - sg)))).astype(BF16)
            drn = dret * (g * sg)
            dgrn_ref[0:1, sl] += jnp.sum(drn * oh, axis=0, keepdims=True)
            t = drn * gr
            do = rinv * (t - oh * jnp.mean(t * oh, axis=-1, keepdims=True))
            qb, kb, vb, dob = q.astype(BF16), k.astype(BF16), v.astype(BF16), do.astype(BF16)
            m = m_ref[h]
            ab = (_dot_nt(qb, kb) * m).astype(BF16)
            dab = (_dot_nt(dob, vb) * m).astype(BF16)
            spb = sp_ref[0, h]
            dsn = dstate[h]
            dsnb = dsn.astype(BF16)
            qdb = (q * qd_ref[h]).astype(BF16)
            kdb = (k * kd_ref[h]).astype(BF16)
            dq = _dot(dab, kb) + _dot_nt(dob, spb) * qd_ref[h]
            dk = _dot_tn(dab, qb) + _dot_nt(vb, dsnb) * kd_ref[h]
            dv = _dot_tn(ab, dob) + _dot(kdb, dsnb)
            dstate[h] = dsn * cdec[h] + _dot_tn(qdb, dob)
            c2, s2 = cos_ref[...], sin_ref[...]
            dp_ref[:, sl] = _rot_t(dq, c2, s2).astype(BF16)
            dp_ref[:, RET_W + h * HEAD_DIM:RET_W + (h + 1) * HEAD_DIM] = _rot_t(dk * K_SCALE, c2, s2).astype(BF16)
            dp_ref[:, 2 * RET_W + h * HEAD_DIM:2 * RET_W + (h + 1) * HEAD_DIM] = dv.astype(BF16)
        for gi in range(HEADS):
            sl = slice(gi * HEAD_DIM, (gi + 1) * HEAD_DIM)
            u = p_ref[:, 4 * RET_W + gi * HEAD_DIM:4 * RET_W + (gi + 1) * HEAD_DIM]
            sv = p_ref[:, 4 * RET_W + SGU_W + gi * HEAD_DIM:4 * RET_W + SGU_W + (gi + 1) * HEAD_DIM]
            gv = _gelu(sv)
            xc = gv - jnp.mean(gv, axis=-1, keepdims=True)
            rstd = lax.rsqrt(jnp.mean(xc * xc, axis=-1, keepdims=True) + EPS)
            xh = xc * rstd
            lg = lng_ref[:, sl]
            vnb = (xh * lg + lnb_ref[:, sl]).astype(BF16)
            wcb = _causal(ws_ref[gi]).astype(BF16)
            mixed = _dot(wcb, vnb) + bsb_ref[gi]
            dsgu = dmix[:, RET_W + gi * HEAD_DIM:RET_W + (gi + 1) * HEAD_DIM]
            dmixed = dsgu * _gelu(u)
            dmb = dmixed.astype(BF16)
            dws_ref[gi] += _causal(_dot_nt(dmb, vnb))
            dbs_acc[gi] += dmixed
            dvn = _dot_tn(wcb, dmb)
            dlng_ref[gi:gi + 1, :] += jnp.sum(dvn * xh, axis=0, keepdims=True)
            dlnb_ref[gi:gi + 1, :] += jnp.sum(dvn, axis=0, keepdims=True)
            dxh = dvn * lg
            dgv = rstd * (dxh - jnp.mean(dxh, axis=-1, keepdims=True) - xh * jnp.mean(dxh * xh, axis=-1, keepdims=True))
            dp_ref[:, 4 * RET_W + gi * HEAD_DIM:4 * RET_W + (gi + 1) * HEAD_DIM] = (dsgu * mixed * _gelu_grad(u)).astype(BF16)
            dp_ref[:, 4 * RET_W + SGU_W + gi * HEAD_DIM:4 * RET_W + SGU_W + (gi + 1) * HEAD_DIM] = (
                dgv * _gelu_grad(sv)).astype(BF16)

        @pl.when(i == N_CHUNK - 1)
        def _():
            for gi in range(HEADS):
                col = jnp.broadcast_to(jnp.sum(dbs_acc[gi], axis=-1, keepdims=True), (CHUNK, CHUNK))
                dbs_ref[gi:gi + 1, :] = jnp.transpose(col)[0:1, :]
            rs.phase3()

    rev = lambda w: pl.BlockSpec((CHUNK, w), lambda i: (N_CHUNK - 1 - i, 0))
    hcc = (HEADS, CHUNK, CHUNK)
    acc = lambda s: pl.BlockSpec(s, lambda i: (0,) * len(s))
    res = pl.pallas_call(
        body, name="bwd_mix", grid=(N_CHUNK,),
        out_shape=(jax.ShapeDtypeStruct((SEQ, PROJ_W), BF16), jax.ShapeDtypeStruct((8, RET_W), F32),
                   jax.ShapeDtypeStruct((8, HEAD_DIM), F32), jax.ShapeDtypeStruct((8, HEAD_DIM), F32),
                   jax.ShapeDtypeStruct(hcc, F32), jax.ShapeDtypeStruct((8, CHUNK), F32)) + _scatter_out_shapes(geoms),
        in_specs=[rev(D_MODEL), rev(PROJ_W), rev(RET_W),
                  pl.BlockSpec((1, HEADS, HEAD_DIM, HEAD_DIM), lambda i: (N_CHUNK - 1 - i, 0, 0, 0)),
                  _resident((D_MODEL, D_MODEL)), _resident((1, RET_W)), _resident((1, SGU_W)), _resident((1, SGU_W)),
                  _resident(hcc), _resident(hcc), _resident(hcc), _resident(hcc), _resident(hcc), rev(HEAD_DIM), rev(HEAD_DIM)]
        + [pl.BlockSpec(memory_space=pl.ANY)] * n_h,
        out_specs=(rev(PROJ_W), acc((8, RET_W)), acc((8, HEAD_DIM)), acc((8, HEAD_DIM)), acc(hcc), acc((8, CHUNK)))
        + _scatter_out_specs(geoms),
        scratch_shapes=[pltpu.VMEM((HEADS, HEAD_DIM, HEAD_DIM), F32), pltpu.VMEM((HEADS, CHUNK, CHUNK), F32)] + _scatter_scratch(geoms),
        compiler_params=_cparams(("arbitrary",)),
    )(dx2, proj, o, sprev, wout_g, grn, lng, lnb, ws, bsb, mask, qdec, kdec, cos2, sin2, *[p for _, p in hosted])
    return tuple(res[:6 + n_h])


def _bwd_proj(dproj, win_g, x, g1, dx2, gin_p, small):
    geoms = [W_IN]
    n_s = len(small)

    def body(dp_ref, w_ref, x_ref, g_ref, dx2_ref, gin_ref, *rest):
        small_refs = rest[:n_s]
        dx_ref, rs_out, rp_ref, rws_ref, rcv_ref, dg_ref = rest[n_s:n_s + 6]
        rs_scratch = rest[n_s + 6:n_s + 6 + N_SCATTER_SCRATCH]
        ar_scratch = rest[n_s + 6 + N_SCATTER_SCRATCH:]
        ar_res = ar_scratch[N_SMALL_SCRATCH:]
        ar = _SmallReduce((dg_ref,) + tuple(small_refs), ar_res, ar_scratch[:N_SMALL_SCRATCH])
        rs = _Scatters(geoms, [gin_ref], [rs_out], rs_scratch)
        pl.when(pl.program_id(0) == 0)(rs.phase1)
        pl.when(pl.program_id(0) == 2)(rs.phase2)
        pl.when(pl.program_id(0) == 4)(rs.phase2b)

        @pl.when(pl.program_id(0) == 0)
        def _():
            dg_ref[...] = jnp.zeros_like(dg_ref)

        dh = _dot_nt(dp_ref[...], w_ref[...])
        xb = x_ref[...]
        r = lax.rsqrt(jnp.mean(xb * xb, axis=-1, keepdims=True) + EPS)
        xh = xb * r
        dg_ref[0:1, :] += jnp.sum(dh * xh, axis=0, keepdims=True)
        t = dh * g_ref[...]
        dx_ref[...] = dx2_ref[...] + r * (t - xh * jnp.mean(t * xh, axis=-1, keepdims=True))

        @pl.when(pl.program_id(0) == N_TB - 1)
        def _():
            ar.begin()
            rs.phase3()
            ar.end()
            for o_ref, r_ref in zip((rp_ref, rws_ref, rcv_ref), ar_res):
                o_ref[...] = r_ref[...]

    tok = lambda w: pl.BlockSpec((TM, w), lambda i: (i, 0))
    vm = pl.BlockSpec(memory_space=pltpu.VMEM)
    res = pl.pallas_call(
        body, name="bwd_proj", grid=(N_TB,),
        out_shape=(jax.ShapeDtypeStruct((SEQ, D_MODEL), F32),) + _scatter_out_shapes(geoms)
        + tuple(jax.ShapeDtypeStruct(s, F32) for s in SMALL_FULL),
        in_specs=[tok(PROJ_W), _resident((D_MODEL, PROJ_W)), tok(D_MODEL), _resident((1, D_MODEL)), tok(D_MODEL),
                  pl.BlockSpec(memory_space=pl.ANY)] + [vm] * n_s,
        out_specs=(tok(D_MODEL),) + _scatter_out_specs(geoms) + (vm,) * len(SMALL_FULL),
        scratch_shapes=[pltpu.VMEM((8, D_MODEL), F32)] + _scatter_scratch(geoms) + _small_scratch()
        + [pltpu.VMEM(s, F32) for s in SMALL_FULL],
        compiler_params=_cparams(("arbitrary",)),
    )(dproj, win_g, x, g1, dx2, gin_p, *small)
    return res


def _wgrad(name, a, b, tm=None, tn=None, hosted=()):
    m_w, n_w = a.shape[-1], b.shape[-1]
    tm = m_w if tm is None else tm
    tn = n_w if tn is None else tn
    n_steps = (m_w // tm) * (n_w // tn)
    geoms = [g for g, _ in hosted]
    n_h = len(hosted)

    def body(a_ref, b_ref, *rest):
        o_ref = rest[n_h]
        if n_h:
            rs = _Scatters(geoms, rest[:n_h], rest[n_h + 1:2 * n_h + 1], rest[2 * n_h + 1:])
            step = pl.program_id(0) * (n_w // tn) + pl.program_id(1)
            pl.when(step == 0)(rs.phase1)
            pl.when(step == 1)(rs.phase2)
            pl.when(step == 2)(rs.phase2b)
        o_ref[...] = _dot_tn(a_ref[...].astype(BF16), b_ref[...].astype(BF16)).astype(BF16)
        if n_h:
            pl.when(step == n_steps - 1)(rs.phase3)

    assert not n_h or n_steps >= 4
    res = pl.pallas_call(
        body, name=name, grid=(m_w // tm, n_w // tn),
        out_shape=(jax.ShapeDtypeStruct((m_w, n_w), BF16),) + _scatter_out_shapes(geoms),
        in_specs=[pl.BlockSpec((SEQ, tm), lambda i, j: (0, i)), pl.BlockSpec((SEQ, tn), lambda i, j: (0, j))]
        + [pl.BlockSpec(memory_space=pl.ANY)] * n_h,
        out_specs=(pl.BlockSpec((tm, tn), lambda i, j: (i, j)),) + _scatter_out_specs(geoms),
        scratch_shapes=_scatter_scratch(geoms),
        compiler_params=_cparams(("arbitrary", "arbitrary") if n_h else ("parallel", "parallel")),
    )(a, b, *[p for _, p in hosted])
    return tuple(res[:1 + n_h])


def _row_step(half_rows):
    return max(s for s in range(16, 177, 16) if half_rows % s == 0)


class _Scatter:
    def __init__(self, geom, partial, out, land1, mine, stage2, land2, comb, s1_send, s1_recv, s2_send, s2_recv, ld_sems):
        self.w, self.row0, self.shape = _geom(geom)
        self.partial, self.out, self.land1 = partial, out, land1
        self.mine, self.stage2, self.land2, self.comb = mine, stage2, land2, comb
        self.hr = self.shape[0] // 2
        self.step = _row_step(self.hr)
        self.s1_send, self.s1_recv, self.s2_send, self.s2_recv, self.ld_sems = s1_send, s1_recv, s2_send, s2_recv, ld_sems
        self.x, self.y, self.c = lax.axis_index("x"), lax.axis_index("y"), lax.axis_index("c")
        self.sibling = (self.x, self.y, 1 - self.c)
        self.chips = [(self.x, self.y), (1 - self.x, self.y), (self.x, 1 - self.y), (1 - self.x, 1 - self.y)]

    def block(self, px, py, pc):
        dev = 4 * px + 2 * py + pc
        if self.w == W_IN:
            return self.partial.at[:, pl.ds(pl.multiple_of(dev * IN_SHARD, 128), IN_SHARD)]
        if self.w == W_OUT:
            return self.partial.at[pl.ds(pl.multiple_of(dev * OUT_SHARD, 128), OUT_SHARD), :]
        if self.w == W_DOWN:
            return self.partial.at[pl.ds(pl.multiple_of(dev * DOWN_SHARD, 32), DOWN_SHARD), :]
        return self.partial.at[pl.ds(pl.multiple_of(dev * FF_SHARD + self.row0, 32), self.shape[0]), :]

    def copy1(self, k):
        return pltpu.make_async_remote_copy(
            src_ref=self.block(*self.chips[k], 1 - self.c), dst_ref=self.land1.at[k],
            send_sem=self.s1_send.at[k], recv_sem=self.s1_recv.at[k], device_id=self.sibling, device_id_type=MESH)

    STAGE2 = [(1, 0, 1), (3, 0, 1), (2, 1, 2), (3, 1, 2), (1, 1, 1), (2, 0, 2)]

    def copy2(self, j):
        blk, h, to = self.STAGE2[j]
        src = self.comb.at[j - 4] if j >= 4 else self.stage2.at[blk - 1, pl.ds(h * self.hr, self.hr), :]
        return pltpu.make_async_remote_copy(
            src_ref=src, dst_ref=self.land2.at[j], send_sem=self.s2_send.at[j], recv_sem=self.s2_recv.at[j],
            device_id=(*self.chips[to], self.c), device_id_type=MESH)

    def _rows(self, h=None):
        step = self.step
        lo, n = (0, self.shape[0]) if h is None else (h * self.hr, self.hr)
        return [pl.ds(r0, step) for r0 in range(lo, lo + n, step)]

    def load(self, k):
        return pltpu.make_async_copy(self.block(*self.chips[k], self.c), self.mine.at[k], self.ld_sems.at[k])

    def phase1(self):
        for k in range(4):
            self.copy1(k).start()
        for k in range(4):
            self.load(k).start()

    def phase2(self):
        for k in (3, 1, 2, 0):
            self.copy1(k).wait_recv()
            self.load(k).wait()
            for rs in self._rows():
                s = self.mine[k, rs, :].astype(F32) + self.land1[k, rs, :].astype(F32)
                if k == 0:
                    self.out[rs, :] = s
                else:
                    self.stage2[k - 1, rs, :] = s.astype(BF16)
            for j in {3: (1, 3), 1: (0,), 2: (2,), 0: ()}[k]:
                self.copy2(j).start()

    def phase2b(self):
        for j, got in ((4, 3), (5, 1)):
            blk, h, _ = self.STAGE2[j]
            self.copy2(got).wait_recv()
            for i, rs in enumerate(self._rows(h)):
                lr = pl.ds(i * self.step, self.step)
                self.comb[j - 4, lr, :] = (self.stage2[blk - 1, rs, :].astype(F32) + self.land2[got, lr, :].astype(F32)).astype(BF16)
            self.copy2(j).start()

    def phase3(self):
        for j in (0, 5, 4, 2):
            self.copy2(j).wait_recv()
        for h, (first, second) in enumerate(((0, 5), (4, 2))):
            for i, rs in enumerate(self._rows(h)):
                lr = pl.ds(i * self.step, self.step)
                self.out[rs, :] = (self.out[rs, :] + self.land2[first, lr, :].astype(F32)) + self.land2[second, lr, :].astype(F32)
        for k in range(4):
            self.copy1(k).wait_send()
        for j in range(6):
            self.copy2(j).wait_send()


def _geom(geom):
    if isinstance(geom, tuple):
        w, row0, rows = geom
        assert w == W_UP
        return w, row0, (rows, SHARD[w][1])
    return geom, 0, SHARD[geom]


N_SCATTER_SCRATCH = 10


def _scatter_out_shapes(geoms):
    return tuple(jax.ShapeDtypeStruct(_geom(g)[2], F32) for g in geoms)


def _scatter_out_specs(geoms):
    return (pl.BlockSpec(memory_space=pltpu.VMEM),) * len(geoms)


def _scatter_scratch(geoms):
    out = []
    for g in geoms:
        s = _geom(g)[2]
        hs = (s[0] // 2, s[1])
        out += [pltpu.VMEM((4,) + s, BF16), pltpu.VMEM((4,) + s, BF16), pltpu.VMEM((3,) + s, BF16), pltpu.VMEM((6,) + hs, BF16),
                pltpu.VMEM((2,) + hs, BF16),
                pltpu.SemaphoreType.DMA((4,)), pltpu.SemaphoreType.DMA((4,)), pltpu.SemaphoreType.DMA((6,)),
                pltpu.SemaphoreType.DMA((6,)), pltpu.SemaphoreType.DMA((4,))]
    return out


class _Scatters:
    def __init__(self, geoms, p_refs, out_refs, scratch):
        k = N_SCATTER_SCRATCH
        self.items = [_Scatter(g, p_refs[i], out_refs[i], *scratch[k * i:k * i + k]) for i, g in enumerate(geoms)]

    def phase1(self):
        for s in self.items:
            s.phase1()

    def phase2(self):
        for s in self.items:
            s.phase2()

    def phase2b(self):
        for s in self.items:
            s.phase2b()

    def phase3(self):
        for s in self.items:
            s.phase3()


PACK_W = 1024


SMALL_FULL = [(2, 8, PACK_W), (HEADS, CHUNK, CHUNK), (2, 8, D_FF)]
SMALL_HALF = [(s[0] // 2,) + s[1:] for s in SMALL_FULL]
N_SMALL_SCRATCH = 16


def _small_scratch():
    n_a = len(SMALL_FULL)
    return ([pltpu.VMEM(SMALL_FULL[0], F32)] + [pltpu.VMEM(s, F32) for s in SMALL_HALF] + [pltpu.VMEM(s, F32) for s in SMALL_HALF]
            + [pltpu.VMEM((3,) + s, F32) for s in SMALL_HALF]
            + [pltpu.SemaphoreType.DMA((n_a,)), pltpu.SemaphoreType.DMA((n_a,)), pltpu.SemaphoreType.DMA((n_a, 3)),
               pltpu.SemaphoreType.DMA((n_a, 3)), pltpu.SemaphoreType.DMA((n_a,)), pltpu.SemaphoreType.DMA((n_a,))])


class _SmallReduce:
    def __init__(self, ins, outs, scratch):
        self.ins, self.outs = ins, outs
        (self.pack, *rest) = scratch
        self.rxs, self.css, self.gs = rest[0:3], rest[3:6], rest[6:9]
        self.s1_send, self.s1_recv, self.s2_send, self.s2_recv, self.s3_send, self.s3_recv = rest[9:]
        self.x, self.y, self.c = lax.axis_index("x"), lax.axis_index("y"), lax.axis_index("c")
        self.sibling = (self.x, self.y, 1 - self.c)
        self.chips = [(1 - self.x, self.y), (self.x, 1 - self.y), (1 - self.x, 1 - self.y)]
        self.hl = [s[0] for s in SMALL_HALF]

    def half(self, ref, a, h):
        return ref.at[pl.ds(h * self.hl[a], self.hl[a])]

    def begin(self):
        dg1_ref, dg2_ref, dgf_ref, dgrn_ref, dlng_ref, dlnb_ref, dbs_ref, loss_ref, dws_ref, dcv_ref = self.ins
        pack, c = self.pack, self.c
        pack[...] = jnp.zeros_like(pack)
        pack[0, 0:1, :] = dg1_ref[0:1, :]
        pack[0, 1:2, :] = dg2_ref[0:1, :]
        pack[0, 2:3, :] = dgf_ref[0:1, :]
        pack[0, 3:4, 0:RET_W] = dgrn_ref[0:1, :]
        lsum = loss_ref[0, 0:1, :]
        for i in range(1, N_TB):
            lsum = lsum + loss_ref[i, 0:1, :]
        pack[0, 3:4, RET_W:RET_W + 128] = lsum
        pack[1, 0:HEADS, 0:128] = dlng_ref[0:HEADS, :]
        pack[1, 0:HEADS, 128:256] = dlnb_ref[0:HEADS, :]
        pack[1, 0:HEADS, 256:384] = dbs_ref[0:HEADS, :]
        self.srcs = [pack, dws_ref, dcv_ref]
        n_a = len(self.srcs)
        self.ex1 = [pltpu.make_async_remote_copy(src_ref=self.half(self.srcs[a], a, 1 - c), dst_ref=self.rxs[a],
                                                 send_sem=self.s1_send.at[a], recv_sem=self.s1_recv.at[a],
                                                 device_id=self.sibling, device_id_type=MESH) for a in range(n_a)]
        for cp in self.ex1:
            cp.start()
        self.ex2 = []
        for a in range(n_a):
            self.ex1[a].wait_recv()
            self.css[a][...] = self.half(self.srcs[a], a, c)[...] + self.rxs[a][...]
            for j, chip in enumerate(self.chips):
                cp = pltpu.make_async_remote_copy(src_ref=self.css[a], dst_ref=self.gs[a].at[j], send_sem=self.s2_send.at[a, j],
                                                  recv_sem=self.s2_recv.at[a, j], device_id=(*chip, c), device_id_type=MESH)
                cp.start()
                self.ex2.append(cp)

    def end(self):
        c, x, y = self.c, self.x, self.y
        ex3 = []
        for a in range(len(self.srcs)):
            css, gs, out = self.css[a], self.gs[a], self.outs[a]
            for j in range(3):
                self.ex2[3 * a + j].wait_recv()
            tot = None
            for q in range(4):
                k = jnp.where(x != (q >> 1), 1, 0) + jnp.where(y != (q & 1), 2, 0)
                term = jnp.where(k == 0, css[...], jnp.where(k == 1, gs[0], jnp.where(k == 2, gs[1], gs[2])))
                tot = term if tot is None else tot + term
            self.half(out, a, c)[...] = tot
            cp = pltpu.make_async_remote_copy(src_ref=self.half(out, a, c), dst_ref=self.half(out, a, c), send_sem=self.s3_send.at[a],
                                              recv_sem=self.s3_recv.at[a], device_id=self.sibling, device_id_type=MESH)
            cp.start()
            ex3.append(cp)
        for a in range(len(self.srcs)):
            out = self.outs[a]
            pltpu.make_async_remote_copy(src_ref=self.half(out, a, 1 - c), dst_ref=self.half(out, a, 1 - c), send_sem=self.s3_send.at[a],
                                         recv_sem=self.s3_recv.at[a], device_id=self.sibling, device_id_type=MESH).wait_recv()
        for cp in self.ex1 + self.ex2 + ex3:
            cp.wait_send()


def _adam_math(w, g, m, v):
    nm = ADAM_B1 * m + (1.0 - ADAM_B1) * g
    nv = ADAM_B2 * v + (1.0 - ADAM_B2) * (g * g)
    d = -ADAM_LR * ((nm / (1.0 - ADAM_B1 ** ADAM_STEP)) / (jnp.sqrt(nv / (1.0 - ADAM_B2 ** ADAM_STEP)) + ADAM_EPS) + ADAM_WD * w)
    return d, nm, nv


def _adamw(name, w, gs, m, v, rows):
    _, r, cdim = w.shape
    n_steps = r // rows
    half = gs[0].shape[0] // rows

    def body(w_ref, *rest):
        g_refs, (m_ref, v_ref, go_ref, d_ref, nm_ref, nv_ref) = rest[:len(gs)], rest[len(gs):]
        gg = g_refs[0][...]
        if len(gs) == 2:
            gg = jnp.where(pl.program_id(0) < half, gg, g_refs[1][...])
        go_ref[0] = gg
        d, nm, nv = _adam_math(w_ref[0], gg, m_ref[0], v_ref[0])
        d_ref[0], nm_ref[0], nv_ref[0] = d, nm, nv

    spec3 = pl.BlockSpec((1, rows, cdim), lambda i: (0, i, 0))
    if len(gs) == 1:
        g_specs = [pl.BlockSpec((rows, cdim), lambda i: (i, 0))]
    else:
        g_specs = [pl.BlockSpec((rows, cdim), lambda i: (jnp.minimum(i, half - 1), 0)),
                   pl.BlockSpec((rows, cdim), lambda i: (jnp.maximum(i - half, 0), 0))]
    sh = jax.ShapeDtypeStruct((1, r, cdim), F32)
    return pl.pallas_call(
        body, name=name, grid=(n_steps,), out_shape=(sh, sh, sh, sh),
        in_specs=[spec3] + g_specs + [spec3, spec3], out_specs=(spec3,) * 4,
        compiler_params=_cparams(("parallel",)),
    )(w, *gs, m, v)


def _adamw_small(rp, rws, rcv, gcw, params):
    n_p = len(params)

    def body(*refs):
        rp_ref, rws_ref, rcv_ref, gcw_ref = refs[:4]
        ins = refs[4:4 + 3 * n_p]
        outs = refs[4 + 3 * n_p:]
        grads = [rp_ref[0, 0:1, :], rp_ref[0, 1:2, :], rp_ref[0, 2:3, :], rp_ref[0, 3:4, 0:RET_W],
                 rp_ref[1, 0:HEADS, 0:128], rp_ref[1, 0:HEADS, 128:256], rp_ref[1, 0:HEADS, 256:384],
                 rws_ref[...], gcw_ref[0], None]
        for p in range(n_p):
            w_ref, m_ref, v_ref = ins[3 * p:3 * p + 3]
            o = outs[4 * p:4 * p + 4]
            if p == n_p - 1:
                for hf in range(2):
                    cs = slice(hf * D_FF, (hf + 1) * D_FF)
                    g = rcv_ref[hf, 3:4, :]
                    res = (g,) + _adam_math(w_ref[:, cs], g, m_ref[:, cs], v_ref[:, cs])
                    for t in range(4):
                        o[t][:, cs] = res[t]
                continue
            lead = w_ref.ndim > grads[p].ndim
            rd = (lambda r: r[0]) if lead else (lambda r: r[...])
            res = (grads[p],) + _adam_math(rd(w_ref), grads[p], rd(m_ref), rd(v_ref))
            for t in range(4):
                if lead:
                    o[t][0] = res[t]
                else:
                    o[t][...] = res[t]

    vm = pl.BlockSpec(memory_space=pltpu.VMEM)
    flat = [a for tr in params for a in tr]
    out_shape = tuple(jax.ShapeDtypeStruct(tr[0].shape, F32) for tr in params for _ in range(4))
    res = pl.pallas_call(
        body, name="adamw_small", out_shape=out_shape, in_specs=[vm] * (4 + len(flat)), out_specs=(vm,) * len(out_shape),
        compiler_params=_cparams(),
    )(rp, rws, rcv, gcw, *flat)
    return [res[4 * p:4 * p + 4] for p in range(n_p)]


def kernel(x, mix_norm_g, w_in, ret_norm_g, sgu_ln_g, sgu_ln_b, sgu_w_s, sgu_b_s, w_out, ffn_norm_g, w_up, conv_w, conv_b, w_down, final_norm_g, loss_target, m_mix_norm_g, m_w_in, m_ret_norm_g, m_sgu_ln_g, m_sgu_ln_b, m_sgu_w_s, m_sgu_b_s, m_w_out, m_ffn_norm_g, m_w_up, m_conv_w, m_conv_b, m_w_down, m_final_norm_g, v_mix_norm_g, v_w_in, v_ret_norm_g, v_sgu_ln_g, v_sgu_ln_b, v_sgu_w_s, v_sgu_b_s, v_w_out, v_ffn_norm_g, v_w_up, v_conv_w, v_conv_b, v_w_down, v_final_norm_g):
    xs = x[0]
    tgt = loss_target[0]
    cos2, sin2 = _rope_tables()
    mask, qdec, kdec = _decay_tables()
    grn = ret_norm_g.reshape(1, RET_W)
    lng = sgu_ln_g.reshape(1, SGU_W)
    lnb = sgu_ln_b.reshape(1, SGU_W)
    ws = sgu_w_s[0]
    bsb = jnp.broadcast_to(sgu_b_s[0][:, :, None], (HEADS, CHUNK, HEAD_DIM))
    gf = final_norm_g.reshape(1, D_MODEL)
    me = 4 * lax.axis_index("x") + 2 * lax.axis_index("y") + lax.axis_index("c")
    tr = lambda a: jnp.transpose(a[0])[None]

    proj, h1, win_g, cw_sh, wout_g, wdn_g, su = _fwd_proj(xs, mix_norm_g, cos2, sin2, w_in[0], w_out[0], tr(w_up)[0], w_down[0],
                                                         conv_w[0])
    cw_g = jnp.transpose(cw_sh, (1, 0, 2)).reshape(8, 2 * D_FF)
    x2, mixcat, o, sprev, wup_g = _fwd_mix(xs, proj, wout_g, grn, lng, lnb, ws, bsb, mask, qdec, kdec, su)
    h2, up_pre, u_conv, act, x3, loss_parts = _fwd_ffn(x2, ffn_norm_g, wup_g, cw_g, conv_b, wdn_g, gf, tgt)

    dx3, dpre, dx2, dgf, dg2, dcv = _bwd_ffn(x3, tgt, gf, x2, ffn_norm_g, up_pre, u_conv, wup_g, cw_g, wdn_g)
    band = 512
    (gdn_p,) = _wgrad("wgrad_down", act, dx3, tm=FF_TILE)
    (gout_p,) = _wgrad("wgrad_out", mixcat, dx2, tn=512)
    gup_p, g_dn = _wgrad("wgrad_up", dpre, h2, tm=FF_TILE, hosted=[(W_DOWN, gdn_p)])
    dproj, dgrn, dlng, dlnb, dws, dbs, g_up_a = _bwd_mix(
        dx2, proj, o, sprev, wout_g, grn, lng, lnb, ws, bsb, mask, qdec, kdec, cos2, sin2, [((W_UP, 0, band), gup_p)])
    gin_p, g_up_b, g_out = _wgrad("wgrad_in", h1, dproj, tn=768,
                                  hosted=[((W_UP, band, FF_SHARD - band), gup_p), (W_OUT, gout_p)])
    grad_x, g_in, rp, rws, rcv = _bwd_proj(dproj, win_g, xs, mix_norm_g, dx2, gin_p,
                                           (dg2, dgf, dgrn, dlng, dlnb, dbs, loss_parts, dws, dcv))
    loss = rp[0, 3, RET_W]
    gcw = lax.dynamic_slice(rcv, (me // (N_DEV // 2), 0, (me % (N_DEV // 2)) * FF_SHARD), (1, 3, FF_SHARD))

    table = {}
    for name, w, gs, m, v, rows in (("w_in", w_in, [g_in], m_w_in, v_w_in, 256), ("w_out", w_out, [g_out], m_w_out, v_w_out, 128),
                                    ("w_up", tr(w_up), [g_up_a, g_up_b], tr(m_w_up), tr(v_w_up), 64),
                                    ("w_down", w_down, [g_dn], m_w_down, v_w_down, 88)):
        table[name] = _adamw("adamw_" + name, w, gs, m, v, rows)
    table["w_up"] = tuple(tr(a) for a in table["w_up"])
    row = lambda a: a.reshape(1, D_MODEL)
    names_small = ["mix_norm_g", "ffn_norm_g", "final_norm_g", "ret_norm_g", "sgu_ln_g", "sgu_ln_b", "sgu_b_s", "sgu_w_s",
                   "conv_w", "conv_b"]
    params = [(mix_norm_g, m_mix_norm_g, v_mix_norm_g), (ffn_norm_g, m_ffn_norm_g, v_ffn_norm_g),
              (row(final_norm_g), row(m_final_norm_g), row(v_final_norm_g)), (ret_norm_g, m_ret_norm_g, v_ret_norm_g),
              (sgu_ln_g, m_sgu_ln_g, v_sgu_ln_g), (sgu_ln_b, m_sgu_ln_b, v_sgu_ln_b), (sgu_b_s, m_sgu_b_s, v_sgu_b_s),
              (sgu_w_s, m_sgu_w_s, v_sgu_w_s), (conv_w, m_conv_w, v_conv_w), (conv_b, m_conv_b, v_conv_b)]
    for n, res in zip(names_small, _adamw_small(rp, rws, rcv, gcw, params)):
        table[n] = res
    table["final_norm_g"] = tuple(a.reshape(D_MODEL) for a in table["final_norm_g"])

    order = ["mix_norm_g", "w_in", "ret_norm_g", "sgu_ln_g", "sgu_ln_b", "sgu_w_s", "sgu_b_s", "w_out", "ffn_norm_g", "w_up",
             "conv_w", "conv_b", "w_down", "final_norm_g"]
    outs = [loss, grad_x[None]]
    for col in range(4):
        outs += [table[n][col] for n in order]
    return tuple(outs)
```

```python
import functools
import math

import jax
import jax.numpy as jnp
import numpy as np
from jax import lax
from jax.experimental import pallas as pl
from jax.experimental.pallas import tpu as pltpu

F32 = jnp.float32
BF16 = jnp.bfloat16
MESH = pl.DeviceIdType.MESH

N_DEV = 8
SEQ = 2048
D_MODEL = 1024
CHUNK = 128
N_CHUNK = SEQ // CHUNK
HEADS = 4
HEAD_DIM = 128
RET_W = 512
SGU_W = 512
PROJ_W = 3072
D_FF = 2816
FF_SHARD = 704
FF_TILE = 1408
FF_TILES = ((0, 1536), (1536, 1280))
IN_SHARD = PROJ_W // N_DEV
OUT_SHARD = D_MODEL // N_DEV
DOWN_SHARD = D_FF // N_DEV
TM = 256
N_TB = SEQ // TM
FWD_PROJ_PASS_AT = 5
FWD_MIX_PASS_AT = 10
EPS = 1e-6
ROPE_BASE = 10000.0
K_SCALE = HEAD_DIM ** -0.5
INV_SQRT2 = 0.7071067811865476
INV_SQRT_2PI = 0.3989422804014327

ADAM_LR = 0.001
ADAM_B1 = 0.9
ADAM_B2 = 0.999
ADAM_EPS = 1e-08
ADAM_WD = 0.01
ADAM_STEP = 10

VMEM_LIMIT = 56 * 1024 * 1024


def _cparams(sem=None, vmem=VMEM_LIMIT):
    return pltpu.CompilerParams(dimension_semantics=sem, vmem_limit_bytes=vmem)


def _resident(shape):
    nd = len(shape)
    return pl.BlockSpec(shape, lambda *_: (0,) * nd, pipeline_mode=pl.Buffered(1))


def _dot(a, b):
    return jnp.dot(a, b, preferred_element_type=F32)


def _dot_nt(a, b):
    return lax.dot_general(a, b, (((1,), (1,)), ((), ())), preferred_element_type=F32)


def _dot_tn(a, b):
    return lax.dot_general(a, b, (((0,), (0,)), ((), ())), preferred_element_type=F32)


def _sigmoid(x):
    return 1.0 / (1.0 + jnp.exp(-x))


def _gelu(x):
    return 0.5 * x * (1.0 + lax.erf(x * INV_SQRT2))


def _gelu_grad(x):
    return 0.5 * (1.0 + lax.erf(x * INV_SQRT2)) + x * (jnp.exp(-0.5 * x * x) * INV_SQRT_2PI)


def _rot(xh, cos2, sin2):
    return xh * cos2 + pltpu.roll(xh, HEAD_DIM // 2, 1) * sin2


def _rot_t(dh, cos2, sin2):
    return dh * cos2 + pltpu.roll(dh * sin2, HEAD_DIM // 2, 1)


def _rope_tables():
    half = HEAD_DIM // 2
    inv_freq = jnp.power(ROPE_BASE, -jnp.arange(half, dtype=F32) / half)
    ang = jnp.arange(SEQ, dtype=F32)[:, None] * inv_freq[None, :]
    cos, sin = jnp.cos(ang), jnp.sin(ang)
    cos2 = jnp.concatenate([cos, cos], axis=-1)
    sin2 = jnp.concatenate([-sin, sin], axis=-1)
    return cos2, sin2


def _decay_tables():
    log_gamma = jnp.log(1.0 - jnp.power(2.0, -5.0 - jnp.arange(HEADS, dtype=F32)))
    pos = jnp.arange(CHUNK, dtype=F32)
    diff = pos[:, None] - pos[None, :]
    mask = jnp.where(diff >= 0.0, jnp.exp(log_gamma[:, None, None] * jnp.maximum(diff, 0.0)[None]), 0.0)
    k_decay = jnp.exp(log_gamma[:, None] * (CHUNK - 1.0 - pos)[None])
    q_decay = jnp.exp(log_gamma[:, None] * (pos + 1.0)[None])
    kd = jnp.broadcast_to(k_decay[:, :, None], (HEADS, CHUNK, HEAD_DIM))
    qd = jnp.broadcast_to(q_decay[:, :, None], (HEADS, CHUNK, HEAD_DIM))
    return mask.astype(F32), qd.astype(F32), kd.astype(F32)


def _chunk_decay():
    lg = np.log(np.float32(1.0) - np.power(np.float32(2.0), -5.0 - np.arange(HEADS, dtype=np.float32))).astype(np.float32)
    return [float(np.exp(lg[h] * np.float32(CHUNK))) for h in range(HEADS)]


W_IN, W_OUT, W_UP, W_DOWN, W_CONV = range(5)
GATHERED = {W_IN: ((D_MODEL, PROJ_W), BF16), W_OUT: ((D_MODEL, D_MODEL), BF16), W_UP: ((2 * D_FF, D_MODEL), BF16),
            W_DOWN: ((D_FF, D_MODEL), BF16), W_CONV: ((N_DEV, 8, FF_SHARD), F32)}
SHARD = {W_IN: (D_MODEL, IN_SHARD), W_OUT: (OUT_SHARD, D_MODEL), W_UP: (FF_SHARD, D_MODEL), W_DOWN: (DOWN_SHARD, D_MODEL),
         W_CONV: (8, FF_SHARD)}


class _Gather:
    N_SEMS = 9

    def __init__(self, ids, stages, gathered, send_sems, recv_sems, local_sems):
        self.ids, self.stages, self.gathered = ids, stages, gathered
        self.send_sems, self.recv_sems, self.local_sems = send_sems, recv_sems, local_sems
        self.x, self.y, self.c = lax.axis_index("x"), lax.axis_index("y"), lax.axis_index("c")
        self.me = (self.x, self.y, self.c)
        self.sibling = (self.x, self.y, 1 - self.c)
        self.chips = [(1 - self.x, self.y), (self.x, 1 - self.y), (1 - self.x, 1 - self.y)]

    def slot(self, n, px, py, pc):
        dev = 4 * px + 2 * py + pc
        w, g = self.ids[n], self.gathered[n]
        if w == W_IN:
            return g.at[:, pl.ds(pl.multiple_of(dev * IN_SHARD, 128), IN_SHARD)]
        if w == W_OUT:
            return g.at[pl.ds(pl.multiple_of(dev * OUT_SHARD, 128), OUT_SHARD), :]
        if w == W_DOWN:
            return g.at[pl.ds(pl.multiple_of(dev * DOWN_SHARD, 32), DOWN_SHARD), :]
        if w == W_UP:
            return g.at[pl.ds(pl.multiple_of(dev * FF_SHARD, 32), FF_SHARD), :]
        return g.at[dev]

    def half(self, n, px, py, pc, h):
        dev = 4 * px + 2 * py + pc
        w, g = self.ids[n], self.gathered[n]
        if w == W_IN:
            return g.at[pl.ds(h * (D_MODEL // 2), D_MODEL // 2), pl.ds(pl.multiple_of(dev * IN_SHARD, 128), IN_SHARD)]
        rows = SHARD[w][0] // 2
        return g.at[pl.ds(pl.multiple_of(dev * SHARD[w][0] + h * rows, 16), rows), :]

    def tree(self, n):
        return self.ids[n] != W_CONV

    def copy(self, n, k, block, to, src=None, h=None):
        ref = self.slot(n, *block) if h is None else self.half(n, *block, h)
        return pltpu.make_async_remote_copy(
            src_ref=ref if src is None else src, dst_ref=ref,
            send_sem=self.send_sems.at[n, k], recv_sem=self.recv_sems.at[n, k], device_id=to, device_id_type=MESH)

    def _mine(self):
        return [pltpu.make_async_copy(self.stages[n], self.slot(n, *self.me), self.local_sems.at[n]) for n in range(len(self.ids))]

    def _first(self):
        out = []
        for n in range(len(self.ids)):
            out.append(self.copy(n, 0, self.me, self.sibling, src=self.stages[n]))
            out += [self.copy(n, 1 + j, self.me, (*chip, self.c), src=self.stages[n])
                    for j, chip in enumerate(self.chips[:2] if self.tree(n) else self.chips)]
        return out

    def start(self):
        for cp in self._mine() + self._first():
            cp.start()

    def _passed(self, j):
        dev = (*self.chips[j], self.c)
        out = []
        for n in range(len(self.ids)):
            if not self.tree(n):
                out.append(self.copy(n, 4 + j, dev, self.sibling))
            elif j < 2:
                out += [self.copy(n, 3 + j, dev, (*self.chips[1 - j], self.c), h=j), self.copy(n, 5 + j, dev, self.sibling)]
            else:
                out += [self.copy(n, 7, dev, self.sibling, h=0), self.copy(n, 8, dev, self.sibling, h=1)]
        return out

    def near(self):
        for j in range(2):
            dev = (*self.chips[j], self.c)
            for n in range(len(self.ids)):
                self.copy(n, 1 + j, dev, self.me).wait_recv()
            for cp in self._passed(j):
                cp.start()

    def finish(self):
        dev = (*self.chips[2], self.c)
        for n in range(len(self.ids)):
            if self.tree(n):
                self.copy(n, 3, dev, self.me, h=0).wait_recv()
                self.copy(n, 4, dev, self.me, h=1).wait_recv()
            else:
                self.copy(n, 3, dev, self.me).wait_recv()
        for cp in self._passed(2):
            cp.start()
        for n in range(len(self.ids)):
            self.copy(n, 0, self.sibling, self.me).wait_recv()
            for j, chip in enumerate(self.chips):
                dev = (*chip, 1 - self.c)
                if not self.tree(n):
                    self.copy(n, 4 + j, dev, self.me).wait_recv()
                elif j < 2:
                    self.copy(n, 5 + j, dev, self.me).wait_recv()
                else:
                    self.copy(n, 7, dev, self.me, h=0).wait_recv()
                    self.copy(n, 8, dev, self.me, h=1).wait_recv()
        for cp in self._mine():
            cp.wait()
        for cp in self._first() + self._passed(0) + self._passed(1) + self._passed(2):
            cp.wait_send()


def _gather_scratch(n):
    return [pltpu.SemaphoreType.DMA((n, _Gather.N_SEMS)), pltpu.SemaphoreType.DMA((n, _Gather.N_SEMS)), pltpu.SemaphoreType.DMA((n,))]


def _gathered_shapes(ids):
    return tuple(jax.ShapeDtypeStruct(*GATHERED[w]) for w in ids)


def _fwd_proj(x, g1, cos2, sin2, w_in, w_out, w_up, w_down, conv_w):
    ids_a, ids_b = [W_IN, W_CONV], [W_OUT, W_DOWN]

    def body(x_ref, g_ref, cos_ref, sin_ref, in_hbm, out_hbm, up_hbm, dn_hbm, cw_ref,
             proj_ref, h1_ref, gin, gcw, gout, gdn, su_ref,
             w_vm, s_in, s_cw, s_out, s_dn, f_in, f_out, f_up, f_dn, ld_sems,
             a_send, a_recv, a_local, b_send, b_recv, b_local):
        ag_a = _Gather(ids_a, [s_in, s_cw], [gin, gcw], a_send, a_recv, a_local)
        ag_b = _Gather(ids_b, [s_out, s_dn], [gout, gdn], b_send, b_recv, b_local)

        @pl.when(pl.program_id(0) == 0)
        def _():
            loads = [pltpu.make_async_copy(src, dst, ld_sems.at[i])
                     for i, (src, dst) in enumerate(((in_hbm, f_in), (out_hbm, f_out), (dn_hbm, f_dn), (up_hbm, f_up)))]
            for cp in loads:
                cp.start()
            s_cw[...] = jnp.zeros_like(s_cw)
            s_cw[0:3, :] = cw_ref[...]
            loads[0].wait()
            s_in[...] = f_in[...].astype(BF16)
            ag_a.start()
            loads[1].wait()
            s_out[...] = f_out[...].astype(BF16)
            loads[2].wait()
            s_dn[...] = f_dn[...].astype(BF16)
            ag_a.near()
            ag_b.start()
            loads[3].wait()
            su_ref[...] = f_up[...].astype(BF16)
            ag_a.finish()
            fill = pltpu.make_async_copy(gin, w_vm, ld_sems.at[4])
            fill.start()
            fill.wait()

        pl.when(pl.program_id(0) == FWD_PROJ_PASS_AT)(ag_b.near)

        xb = x_ref[...]
        r = lax.rsqrt(jnp.mean(xb * xb, axis=-1, keepdims=True) + EPS)
        h = ((xb * r) * g_ref[...]).astype(BF16)
        h1_ref[...] = h
        p = _dot(h, w_vm[...])
        for hd in range(HEADS):
            sl = slice(hd * HEAD_DIM, (hd + 1) * HEAD_DIM)
            c2, s2 = cos_ref[...], sin_ref[...]
            proj_ref[:, sl] = _rot(p[:, sl], c2, s2)
            ks = slice(RET_W + hd * HEAD_DIM, RET_W + (hd + 1) * HEAD_DIM)
            proj_ref[:, ks] = _rot(p[:, ks], c2, s2) * K_SCALE
        proj_ref[:, 2 * RET_W:] = p[:, 2 * RET_W:]

        pl.when(pl.program_id(0) == N_TB - 1)(ag_b.finish)

    tok = lambda w: pl.BlockSpec((TM, w), lambda i: (i, 0))
    hbm = pl.BlockSpec(memory_space=pl.ANY)
    vm = pl.BlockSpec(memory_space=pltpu.VMEM)
    return pl.pallas_call(
        body, name="fwd_proj", grid=(N_TB,),
        out_shape=(jax.ShapeDtypeStruct((SEQ, PROJ_W), F32), jax.ShapeDtypeStruct((SEQ, D_MODEL), BF16))
        + _gathered_shapes(ids_a + ids_b) + (jax.ShapeDtypeStruct(SHARD[W_UP], BF16),),
        in_specs=[tok(D_MODEL), _resident((1, D_MODEL)), tok(HEAD_DIM), tok(HEAD_DIM), hbm, hbm, hbm, hbm, vm],
        out_specs=(tok(PROJ_W), tok(D_MODEL), hbm, hbm, hbm, hbm, vm),
        scratch_shapes=[pltpu.VMEM((D_MODEL, PROJ_W), BF16), pltpu.VMEM(SHARD[W_IN], BF16), pltpu.VMEM(SHARD[W_CONV], F32),
                        pltpu.VMEM(SHARD[W_OUT], BF16), pltpu.VMEM(SHARD[W_DOWN], BF16),
                        pltpu.VMEM(SHARD[W_IN], F32), pltpu.VMEM(SHARD[W_OUT], F32), pltpu.VMEM(SHARD[W_UP], F32),
                        pltpu.VMEM(SHARD[W_DOWN], F32), pltpu.SemaphoreType.DMA((5,))]
        + _gather_scratch(len(ids_a)) + _gather_scratch(len(ids_b)),
        compiler_params=_cparams(("arbitrary",)),
    )(x, g1, cos2, sin2, w_in, w_out, w_up, w_down, conv_w)


def _causal(w):
    r = lax.broadcasted_iota(jnp.int32, (CHUNK, CHUNK), 0)
    c = lax.broadcasted_iota(jnp.int32, (CHUNK, CHUNK), 1)
    return jnp.where(r >= c, w, 0.0)


def _fwd_mix(x, proj, wout_g, grn, lng, lnb, ws, bsb, mask, qdec, kdec, su):
    cdec = _chunk_decay()
    ids = [W_UP]

    def body(x_ref, p_ref, w_ref, grn_ref, lng_ref, lnb_ref, ws_ref, bsb_ref, m_ref, qd_ref, kd_ref, su_ref,
             x2_ref, cat_ref, o_ref, sp_ref, gup, state, send_sems, recv_sems, local_sems):
        ag = _Gather(ids, [su_ref], [gup], send_sems, recv_sems, local_sems)

        @pl.when(pl.program_id(0) == 0)
        def _():
            state[...] = jnp.zeros_like(state)
            ag.start()

        for h in range(HEADS):
            sl = slice(h * HEAD_DIM, (h + 1) * HEAD_DIM)
            q = p_ref[:, sl]
            k = p_ref[:, RET_W + h * HEAD_DIM:RET_W + (h + 1) * HEAD_DIM]
            v = p_ref[:, 2 * RET_W + h * HEAD_DIM:2 * RET_W + (h + 1) * HEAD_DIM]
            g = p_ref[:, 3 * RET_W + h * HEAD_DIM:3 * RET_W + (h + 1) * HEAD_DIM]
            qb, kb, vb = q.astype(BF16), k.astype(BF16), v.astype(BF16)
            a = _dot_nt(qb, kb) * m_ref[h]
            spb = state[h].astype(BF16)
            sp_ref[0, h] = spb
            o = _dot(a.astype(BF16), vb) + _dot((q * qd_ref[h]).astype(BF16), spb)
            state[h] = state[h] * cdec[h] + _dot_tn((k * kd_ref[h]).astype(BF16), vb)
            o_ref[:, sl] = o
            rinv = lax.rsqrt(jnp.mean(o * o, axis=-1, keepdims=True) + EPS)
            rn = (o * rinv) * grn_ref[:, sl]
            cat_ref[:, sl] = ((g * _sigmoid(g)) * rn).astype(BF16)
        for gi in range(HEADS):
            sl = slice(gi * HEAD_DIM, (gi + 1) * HEAD_DIM)
            u = p_ref[:, 4 * RET_W + gi * HEAD_DIM:4 * RET_W + (gi + 1) * HEAD_DIM]
            sv = p_ref[:, 4 * RET_W + SGU_W + gi * HEAD_DIM:4 * RET_W + SGU_W + (gi + 1) * HEAD_DIM]
            gv = _gelu(sv)
            xc = gv - jnp.mean(gv, axis=-1, keepdims=True)
            vn = (xc * lax.rsqrt(jnp.mean(xc * xc, axis=-1, keepdims=True) + EPS)) * lng_ref[:, sl] + lnb_ref[:, sl]
            mixed = _dot(_causal(ws_ref[gi]).astype(BF16), vn.astype(BF16)) + bsb_ref[gi]
            cat_ref[:, RET_W + gi * HEAD_DIM:RET_W + (gi + 1) * HEAD_DIM] = (_gelu(u) * mixed).astype(BF16)
        x2_ref[...] = x_ref[...] + _dot(cat_ref[...], w_ref[...])

        pl.when(pl.program_id(0) == FWD_MIX_PASS_AT)(ag.near)
        pl.when(pl.program_id(0) == N_CHUNK - 1)(ag.finish)

    ch = lambda w: pl.BlockSpec((CHUNK, w), lambda i: (i, 0))
    hcc = (HEADS, CHUNK, CHUNK)
    hbm = pl.BlockSpec(memory_space=pl.ANY)
    return pl.pallas_call(
        body, name="fwd_mix", grid=(N_CHUNK,),
        out_shape=(jax.ShapeDtypeStruct((SEQ, D_MODEL), F32), jax.ShapeDtypeStruct((SEQ, D_MODEL), BF16),
                   jax.ShapeDtypeStruct((SEQ, RET_W), F32), jax.ShapeDtypeStruct((N_CHUNK, HEADS, HEAD_DIM, HEAD_DIM), BF16))
        + _gathered_shapes(ids),
        in_specs=[ch(D_MODEL), ch(PROJ_W), _resident((D_MODEL, D_MODEL)), _resident((1, RET_W)), _resident((1, SGU_W)),
                  _resident((1, SGU_W)), _resident(hcc), _resident(hcc), _resident(hcc), _resident(hcc), _resident(hcc), hbm],
        out_specs=(ch(D_MODEL), ch(D_MODEL), ch(RET_W), pl.BlockSpec((1, HEADS, HEAD_DIM, HEAD_DIM), lambda i: (i, 0, 0, 0)), hbm),
        scratch_shapes=[pltpu.VMEM((HEADS, HEAD_DIM, HEAD_DIM), F32)] + _gather_scratch(len(ids)),
        compiler_params=_cparams(("arbitrary",)),
    )(x, proj, wout_g, grn, lng, lnb, ws, bsb, mask, qdec, kdec, su)


def _conv_taps(p, prev8):
    row = lax.broadcasted_iota(jnp.int32, p.shape, 0)
    p1 = jnp.where(row == 0, prev8[7:8, :], pltpu.roll(p, 1, 0))
    p2 = jnp.where(row == 0, prev8[6:7, :], jnp.where(row == 1, prev8[7:8, :], pltpu.roll(p, 2, 0)))
    return p1, p2


def _fwd_ffn(x2, g2, wup_g, cw_g, cb_g, wdn_g, gf, tgt):
    def body(x_ref, g_ref, wu_ref, cw_ref, cb_ref, wd_ref, gf_ref, t_ref, h2_ref, up_ref, u_ref, act_ref, x3_ref, loss_ref, carry):
        @pl.when(pl.program_id(0) == 0)
        def _():
            carry[...] = jnp.zeros_like(carry)

        xb = x_ref[...]
        r = lax.rsqrt(jnp.mean(xb * xb, axis=-1, keepdims=True) + EPS)
        h = ((xb * r) * g_ref[...]).astype(BF16)
        h2_ref[...] = h
        acc = xb
        for t0, tw in FF_TILES:
            u = []
            for c0 in (t0, D_FF + t0):
                cs = slice(c0, c0 + tw)
                p = _dot_nt(h, wu_ref[pl.ds(c0, tw), :])
                up_ref[:, cs] = p.astype(BF16)
                p1, p2 = _conv_taps(p, carry[:, cs])
                carry[:, cs] = p[TM - 8:, :]
                us = p2 * cw_ref[0:1, cs] + p1 * cw_ref[1:2, cs] + p * cw_ref[2:3, cs] + cb_ref[:, cs]
                u_ref[:, cs] = us.astype(BF16)
                u.append(us)
            a = ((u[0] * _sigmoid(u[0])) * u[1]).astype(BF16)
            act_ref[:, t0:t0 + tw] = a
            acc = acc + _dot(a, wd_ref[pl.ds(t0, tw), :])
        x3_ref[...] = acc
        r3 = lax.rsqrt(jnp.mean(acc * acc, axis=-1, keepdims=True) + EPS)
        diff = (acc * r3) * gf_ref[...] - t_ref[...]
        loss_ref[...] = jnp.full(loss_ref.shape, 0.5 * jnp.sum(jnp.mean(diff * diff, axis=-1)), F32)

    tok = lambda w: pl.BlockSpec((TM, w), lambda i: (i, 0))
    return pl.pallas_call(
        body, name="fwd_ffn", grid=(N_TB,),
        out_shape=(jax.ShapeDtypeStruct((SEQ, D_MODEL), BF16), jax.ShapeDtypeStruct((SEQ, 2 * D_FF), BF16),
                   jax.ShapeDtypeStruct((SEQ, 2 * D_FF), BF16),
                   jax.ShapeDtypeStruct((SEQ, D_FF), BF16), jax.ShapeDtypeStruct((SEQ, D_MODEL), F32),
                   jax.ShapeDtypeStruct((N_TB, 8, 128), F32)),
        in_specs=[tok(D_MODEL), _resident((1, D_MODEL)), _resident((2 * D_FF, D_MODEL)), _resident((8, 2 * D_FF)),
                  _resident((1, 2 * D_FF)), _resident((D_FF, D_MODEL)), _resident((1, D_MODEL)), tok(D_MODEL)],
        out_specs=(tok(D_MODEL), tok(2 * D_FF), tok(2 * D_FF), tok(D_FF), tok(D_MODEL),
                   pl.BlockSpec((1, 8, 128), lambda i: (i, 0, 0))),
        scratch_shapes=[pltpu.VMEM((8, 2 * D_FF), F32)],
        compiler_params=_cparams(("arbitrary",)),
    )(x2, g2, wup_g, cw_g, cb_g, wdn_g, gf, tgt)


def _bwd_ffn(x3, tgt, gf, x2, g2, up_pre, u_conv, wup_g, cw_g, wdn_g):
    def body(x3_ref, t_ref, gf_ref, x2_ref, g2_ref, up_ref, u_ref, wu_ref, cw_ref, wd_ref,
             dx3_ref, dpre_ref, dx2_ref, dgf_ref, dg2_ref, dcv_ref, nxt):
        i = pl.program_id(0)

        @pl.when(i == 0)
        def _():
            nxt[...] = jnp.zeros_like(nxt)
            dgf_ref[...] = jnp.zeros_like(dgf_ref)
            dg2_ref[...] = jnp.zeros_like(dg2_ref)
            dcv_ref[...] = jnp.zeros_like(dcv_ref)

        x3 = x3_ref[...]
        r3 = lax.rsqrt(jnp.mean(x3 * x3, axis=-1, keepdims=True) + EPS)
        xh3 = x3 * r3
        dy = (xh3 * gf_ref[...] - t_ref[...]) * (1.0 / D_MODEL)
        dgf_ref[0:1, :] += jnp.sum(dy * xh3, axis=0, keepdims=True)
        t3 = dy * gf_ref[...]
        dx3 = r3 * (t3 - xh3 * jnp.mean(t3 * xh3, axis=-1, keepdims=True))
        dx3b = dx3.astype(BF16)
        dx3_ref[...] = dx3b
        dh2 = jnp.zeros((TM, D_MODEL), F32)
        for t0, tw in FF_TILES:
            row = lax.broadcasted_iota(jnp.int32, (TM, tw), 0)
            ts = slice(t0, t0 + tw)
            dact = _dot_nt(dx3b, wd_ref[pl.ds(t0, tw), :])
            ua = u_ref[:, ts].astype(F32)
            ub = u_ref[:, D_FF + t0:D_FF + t0 + tw].astype(F32)
            sg = _sigmoid(ua)
            du = [dact * ub * (sg * (1.0 + ua * (1.0 - sg))), dact * (ua * sg)]
            for n in range(2):
                d = du[n]
                c0 = n * D_FF + t0
                cs = slice(c0, c0 + tw)
                nx = nxt[:, cs]
                n1 = jnp.where(row == TM - 1, nx[0:1, :], pltpu.roll(d, TM - 1, 0))
                n2 = jnp.where(row == TM - 2, nx[0:1, :], jnp.where(row == TM - 1, nx[1:2, :], pltpu.roll(d, TM - 2, 0)))
                nxt[:, cs] = d[0:8, :]
                dp = (d * cw_ref[2:3, cs] + n1 * cw_ref[1:2, cs] + n2 * cw_ref[0:1, cs]).astype(BF16)
                dpre_ref[:, cs] = dp
                p = up_ref[:, cs].astype(F32)
                dcv_ref[n, 0:1, ts] += jnp.sum(n2 * p, axis=0, keepdims=True)
                dcv_ref[n, 1:2, ts] += jnp.sum(n1 * p, axis=0, keepdims=True)
                dcv_ref[n, 2:3, ts] += jnp.sum(d * p, axis=0, keepdims=True)
                dcv_ref[n, 3:4, ts] += jnp.sum(d, axis=0, keepdims=True)
                dh2 = dh2 + _dot(dp, wu_ref[pl.ds(c0, tw), :])
        x2 = x2_ref[...]
        r2 = lax.rsqrt(jnp.mean(x2 * x2, axis=-1, keepdims=True) + EPS)
        xh2 = x2 * r2
        dg2_ref[0:1, :] += jnp.sum(dh2 * xh2, axis=0, keepdims=True)
        t2 = dh2 * g2_ref[...]
        dx2_ref[...] = dx3 + r2 * (t2 - xh2 * jnp.mean(t2 * xh2, axis=-1, keepdims=True))

    rev = lambda w: pl.BlockSpec((TM, w), lambda i: (N_TB - 1 - i, 0))
    acc = lambda s: pl.BlockSpec(s, lambda i: (0,) * len(s))
    return pl.pallas_call(
        body, name="bwd_ffn", grid=(N_TB,),
        out_shape=(jax.ShapeDtypeStruct((SEQ, D_MODEL), BF16), jax.ShapeDtypeStruct((SEQ, 2 * D_FF), BF16),
                   jax.ShapeDtypeStruct((SEQ, D_MODEL), F32), jax.ShapeDtypeStruct((8, D_MODEL), F32),
                   jax.ShapeDtypeStruct((8, D_MODEL), F32), jax.ShapeDtypeStruct((2, 8, D_FF), F32)),
        in_specs=[rev(D_MODEL), rev(D_MODEL), _resident((1, D_MODEL)), rev(D_MODEL), _resident((1, D_MODEL)), rev(2 * D_FF),
                  rev(2 * D_FF), _resident((2 * D_FF, D_MODEL)), _resident((8, 2 * D_FF)), _resident((D_FF, D_MODEL))],
        out_specs=(rev(D_MODEL), rev(2 * D_FF), rev(D_MODEL), acc((8, D_MODEL)), acc((8, D_MODEL)), acc((2, 8, D_FF))),
        scratch_shapes=[pltpu.VMEM((8, 2 * D_FF), F32)],
        compiler_params=_cparams(("arbitrary",)),
    )(x3, tgt, gf, x2, g2, up_pre, u_conv, wup_g, cw_g, wdn_g)


def _bwd_mix(dx2, proj, o, sprev, wout_g, grn, lng, lnb, ws, bsb, mask, qdec, kdec, cos2, sin2, hosted):
    cdec = _chunk_decay()
    geoms = [g for g, _ in hosted]
    n_h = len(hosted)

    def body(dx2_ref, p_ref, o_ref, sp_ref, w_ref, grn_ref, lng_ref, lnb_ref, ws_ref, bsb_ref, m_ref, qd_ref, kd_ref,
             cos_ref, sin_ref, *rest):
        dp_ref, dgrn_ref, dlng_ref, dlnb_ref, dws_ref, dbs_ref = rest[n_h:n_h + 6]
        dstate, dbs_acc = rest[2 * n_h + 6:2 * n_h + 8]
        i = pl.program_id(0)
        rs = _Scatters(geoms, rest[:n_h], rest[n_h + 6:2 * n_h + 6], rest[2 * n_h + 8:])
        pl.when(i == 0)(rs.phase1)
        pl.when(i == 3)(rs.phase2)
        pl.when(i == 6)(rs.phase2b)

        @pl.when(i == 0)
        def _():
            dstate[...] = jnp.zeros_like(dstate)
            dgrn_ref[...] = jnp.zeros_like(dgrn_ref)
            dlng_ref[...] = jnp.zeros_like(dlng_ref)
            dlnb_ref[...] = jnp.zeros_like(dlnb_ref)
            dws_ref[...] = jnp.zeros_like(dws_ref)
            dbs_ref[...] = jnp.zeros_like(dbs_ref)
            dbs_acc[...] = jnp.zeros_like(dbs_acc)

        dmix = _dot_nt(dx2_ref[...].astype(BF16), w_ref[...])
        for h in range(HEADS):
            sl = slice(h * HEAD_DIM, (h + 1) * HEAD_DIM)
            q = p_ref[:, sl]
            k = p_ref[:, RET_W + h * HEAD_DIM:RET_W + (h + 1) * HEAD_DIM]
            v = p_ref[:, 2 * RET_W + h * HEAD_DIM:2 * RET_W + (h + 1) * HEAD_DIM]
            g = p_ref[:, 3 * RET_W + h * HEAD_DIM:3 * RET_W + (h + 1) * HEAD_DIM]
            o = o_ref[:, sl]
            rinv = lax.rsqrt(jnp.mean(o * o, axis=-1, keepdims=True) + EPS)
            oh = o * rinv
            gr = grn_ref[:, sl]
            sg = _sigmoid(g)
            dret = dmix[:, sl]
            dp_ref[:, 3 * RET_W + h * HEAD_DIM:3 * RET_W + (h + 1) * HEAD_DIM] = (
                dret * (oh * gr) * (sg * (1.0 + g * (1.0 - sg)))).astype(BF16)
            drn = dret * (g * sg)
            dgrn_ref[0:1, sl] += jnp.sum(drn * oh, axis=0, keepdims=True)
            t = drn * gr
            do = rinv * (t - oh * jnp.mean(t * oh, axis=-1, keepdims=True))
            qb, kb, vb, dob = q.astype(BF16), k.astype(BF16), v.astype(BF16), do.astype(BF16)
            m = m_ref[h]
            ab = (_dot_nt(qb, kb) * m).astype(BF16)
            dab = (_dot_nt(dob, vb) * m).astype(BF16)
            spb = sp_ref[0, h]
            dsn = dstate[h]
            dsnb = dsn.astype(BF16)
            qdb = (q * qd_ref[h]).astype(BF16)
            kdb = (k * kd_ref[h]).astype(BF16)
            dq = _dot(dab, kb) + _dot_nt(dob, spb) * qd_ref[h]
            dk = _dot_tn(dab, qb) + _dot_nt(vb, dsnb) * kd_ref[h]
            dv = _dot_tn(ab, dob) + _dot(kdb, dsnb)
            dstate[h] = dsn * cdec[h] + _dot_tn(qdb, dob)
            c2, s2 = cos_ref[...], sin_ref[...]
            dp_ref[:, sl] = _rot_t(dq, c2, s2).astype(BF16)
            dp_ref[:, RET_W + h * HEAD_DIM:RET_W + (h + 1) * HEAD_DIM] = _rot_t(dk * K_SCALE, c2, s2).astype(BF16)
            dp_ref[:, 2 * RET_W + h * HEAD_DIM:2 * RET_W + (h + 1) * HEAD_DIM] = dv.astype(BF16)
        for gi in range(HEADS):
            sl = slice(gi * HEAD_DIM, (gi + 1) * HEAD_DIM)
            u = p_ref[:, 4 * RET_W + gi * HEAD_DIM:4 * RET_W + (gi + 1) * HEAD_DIM]
            sv = p_ref[:, 4 * RET_W + SGU_W + gi * HEAD_DIM:4 * RET_W + SGU_W + (gi + 1) * HEAD_DIM]
            gv = _gelu(sv)
            xc = gv - jnp.mean(gv, axis=-1, keepdims=True)
            rstd = lax.rsqrt(jnp.mean(xc * xc, axis=-1, keepdims=True) + EPS)
            xh = xc * rstd
            lg = lng_ref[:, sl]
            vnb = (xh * lg + lnb_ref[:, sl]).astype(BF16)
            wcb = _causal(ws_ref[gi]).astype(BF16)
            mixed = _dot(wcb, vnb) + bsb_ref[gi]
            dsgu = dmix[:, RET_W + gi * HEAD_DIM:RET_W + (gi + 1) * HEAD_DIM]
            dmixed = dsgu * _gelu(u)
            dmb = dmixed.astype(BF16)
            dws_ref[gi] += _causal(_dot_nt(dmb, vnb))
            dbs_acc[gi] += dmixed
            dvn = _dot_tn(wcb, dmb)
            dlng_ref[gi:gi + 1, :] += jnp.sum(dvn * xh, axis=0, keepdims=True)
            dlnb_ref[gi:gi + 1, :] += jnp.sum(dvn, axis=0, keepdims=True)
            dxh = dvn * lg
            dgv = rstd * (dxh - jnp.mean(dxh, axis=-1, keepdims=True) - xh * jnp.mean(dxh * xh, axis=-1, keepdims=True))
            dp_ref[:, 4 * RET_W + gi * HEAD_DIM:4 * RET_W + (gi + 1) * HEAD_DIM] = (dsgu * mixed * _gelu_grad(u)).astype(BF16)
            dp_ref[:, 4 * RET_W + SGU_W + gi * HEAD_DIM:4 * RET_W + SGU_W + (gi + 1) * HEAD_DIM] = (
                dgv * _gelu_grad(sv)).astype(BF16)

        @pl.when(i == N_CHUNK - 1)
        def _():
            for gi in range(HEADS):
                col = jnp.broadcast_to(jnp.sum(dbs_acc[gi], axis=-1, keepdims=True), (CHUNK, CHUNK))
                dbs_ref[gi:gi + 1, :] = jnp.transpose(col)[0:1, :]
            rs.phase3()

    rev = lambda w: pl.BlockSpec((CHUNK, w), lambda i: (N_CHUNK - 1 - i, 0))
    hcc = (HEADS, CHUNK, CHUNK)
    acc = lambda s: pl.BlockSpec(s, lambda i: (0,) * len(s))
    res = pl.pallas_call(
        body, name="bwd_mix", grid=(N_CHUNK,),
        out_shape=(jax.ShapeDtypeStruct((SEQ, PROJ_W), BF16), jax.ShapeDtypeStruct((8, RET_W), F32),
                   jax.ShapeDtypeStruct((8, HEAD_DIM), F32), jax.ShapeDtypeStruct((8, HEAD_DIM), F32),
                   jax.ShapeDtypeStruct(hcc, F32), jax.ShapeDtypeStruct((8, CHUNK), F32)) + _scatter_out_shapes(geoms),
        in_specs=[rev(D_MODEL), rev(PROJ_W), rev(RET_W),
                  pl.BlockSpec((1, HEADS, HEAD_DIM, HEAD_DIM), lambda i: (N_CHUNK - 1 - i, 0, 0, 0)),
                  _resident((D_MODEL, D_MODEL)), _resident((1, RET_W)), _resident((1, SGU_W)), _resident((1, SGU_W)),
                  _resident(hcc), _resident(hcc), _resident(hcc), _resident(hcc), _resident(hcc), rev(HEAD_DIM), rev(HEAD_DIM)]
        + [pl.BlockSpec(memory_space=pl.ANY)] * n_h,
        out_specs=(rev(PROJ_W), acc((8, RET_W)), acc((8, HEAD_DIM)), acc((8, HEAD_DIM)), acc(hcc), acc((8, CHUNK)))
        + _scatter_out_specs(geoms),
        scratch_shapes=[pltpu.VMEM((HEADS, HEAD_DIM, HEAD_DIM), F32), pltpu.VMEM((HEADS, CHUNK, CHUNK), F32)] + _scatter_scratch(geoms),
        compiler_params=_cparams(("arbitrary",)),
    )(dx2, proj, o, sprev, wout_g, grn, lng, lnb, ws, bsb, mask, qdec, kdec, cos2, sin2, *[p for _, p in hosted])
    return tuple(res[:6 + n_h])


def _bwd_proj(dproj, win_g, x, g1, dx2, gin_p, small):
    geoms = [W_IN]
    n_s = len(small)

    def body(dp_ref, w_ref, x_ref, g_ref, dx2_ref, gin_ref, *rest):
        small_refs = rest[:n_s]
        dx_ref, rs_out, rp_ref, rws_ref, rcv_ref, dg_ref = rest[n_s:n_s + 6]
        rs_scratch = rest[n_s + 6:n_s + 6 + N_SCATTER_SCRATCH]
        ar_scratch = rest[n_s + 6 + N_SCATTER_SCRATCH:]
        ar_res = ar_scratch[N_SMALL_SCRATCH:]
        ar = _SmallReduce((dg_ref,) + tuple(small_refs), ar_res, ar_scratch[:N_SMALL_SCRATCH])
        rs = _Scatters(geoms, [gin_ref], [rs_out], rs_scratch)
        pl.when(pl.program_id(0) == 0)(rs.phase1)
        pl.when(pl.program_id(0) == 2)(rs.phase2)
        pl.when(pl.program_id(0) == 4)(rs.phase2b)

        @pl.when(pl.program_id(0) == 0)
        def _():
            dg_ref[...] = jnp.zeros_like(dg_ref)

        dh = _dot_nt(dp_ref[...], w_ref[...])
        xb = x_ref[...]
        r = lax.rsqrt(jnp.mean(xb * xb, axis=-1, keepdims=True) + EPS)
        xh = xb * r
        dg_ref[0:1, :] += jnp.sum(dh * xh, axis=0, keepdims=True)
        t = dh * g_ref[...]
        dx_ref[...] = dx2_ref[...] + r * (t - xh * jnp.mean(t * xh, axis=-1, keepdims=True))

        @pl.when(pl.program_id(0) == N_TB - 1)
        def _():
            ar.begin()
            rs.phase3()
            ar.end()
            for o_ref, r_ref in zip((rp_ref, rws_ref, rcv_ref), ar_res):
                o_ref[...] = r_ref[...]

    tok = lambda w: pl.BlockSpec((TM, w), lambda i: (i, 0))
    vm = pl.BlockSpec(memory_space=pltpu.VMEM)
    res = pl.pallas_call(
        body, name="bwd_proj", grid=(N_TB,),
        out_shape=(jax.ShapeDtypeStruct((SEQ, D_MODEL), F32),) + _scatter_out_shapes(geoms)
        + tuple(jax.ShapeDtypeStruct(s, F32) for s in SMALL_FULL),
        in_specs=[tok(PROJ_W), _resident((D_MODEL, PROJ_W)), tok(D_MODEL), _resident((1, D_MODEL)), tok(D_MODEL),
                  pl.BlockSpec(memory_space=pl.ANY)] + [vm] * n_s,
        out_specs=(tok(D_MODEL),) + _scatter_out_specs(geoms) + (vm,) * len(SMALL_FULL),
        scratch_shapes=[pltpu.VMEM((8, D_MODEL), F32)] + _scatter_scratch(geoms) + _small_scratch()
        + [pltpu.VMEM(s, F32) for s in SMALL_FULL],
        compiler_params=_cparams(("arbitrary",)),
    )(dproj, win_g, x, g1, dx2, gin_p, *small)
    return res


def _wgrad(name, a, b, tm=None, tn=None, hosted=()):
    m_w, n_w = a.shape[-1], b.shape[-1]
    tm = m_w if tm is None else tm
    tn = n_w if tn is None else tn
    n_steps = (m_w // tm) * (n_w // tn)
    geoms = [g for g, _ in hosted]
    n_h = len(hosted)

    def body(a_ref, b_ref, *rest):
        o_ref = rest[n_h]
        if n_h:
            rs = _Scatters(geoms, rest[:n_h], rest[n_h + 1:2 * n_h + 1], rest[2 * n_h + 1:])
            step = pl.program_id(0) * (n_w // tn) + pl.program_id(1)
            pl.when(step == 0)(rs.phase1)
            pl.when(step == 1)(rs.phase2)
            pl.when(step == 2)(rs.phase2b)
        o_ref[...] = _dot_tn(a_ref[...].astype(BF16), b_ref[...].astype(BF16)).astype(BF16)
        if n_h:
            pl.when(step == n_steps - 1)(rs.phase3)

    assert not n_h or n_steps >= 4
    res = pl.pallas_call(
        body, name=name, grid=(m_w // tm, n_w // tn),
        out_shape=(jax.ShapeDtypeStruct((m_w, n_w), BF16),) + _scatter_out_shapes(geoms),
        in_specs=[pl.BlockSpec((SEQ, tm), lambda i, j: (0, i)), pl.BlockSpec((SEQ, tn), lambda i, j: (0, j))]
        + [pl.BlockSpec(memory_space=pl.ANY)] * n_h,
        out_specs=(pl.BlockSpec((tm, tn), lambda i, j: (i, j)),) + _scatter_out_specs(geoms),
        scratch_shapes=_scatter_scratch(geoms),
        compiler_params=_cparams(("arbitrary", "arbitrary") if n_h else ("parallel", "parallel")),
    )(a, b, *[p for _, p in hosted])
    return tuple(res[:1 + n_h])


def _row_step(half_rows):
    return max(s for s in range(16, 177, 16) if half_rows % s == 0)


class _Scatter:
    def __init__(self, geom, partial, out, land1, mine, stage2, land2, comb, s1_send, s1_recv, s2_send, s2_recv, ld_sems):
        self.w, self.row0, self.shape = _geom(geom)
        self.partial, self.out, self.land1 = partial, out, land1
        self.mine, self.stage2, self.land2, self.comb = mine, stage2, land2, comb
        self.hr = self.shape[0] // 2
        self.step = _row_step(self.hr)
        self.s1_send, self.s1_recv, self.s2_send, self.s2_recv, self.ld_sems = s1_send, s1_recv, s2_send, s2_recv, ld_sems
        self.x, self.y, self.c = lax.axis_index("x"), lax.axis_index("y"), lax.axis_index("c")
        self.sibling = (self.x, self.y, 1 - self.c)
        self.chips = [(self.x, self.y), (1 - self.x, self.y), (self.x, 1 - self.y), (1 - self.x, 1 - self.y)]

    def block(self, px, py, pc):
        dev = 4 * px + 2 * py + pc
        if self.w == W_IN:
            return self.partial.at[:, pl.ds(pl.multiple_of(dev * IN_SHARD, 128), IN_SHARD)]
        if self.w == W_OUT:
            return self.partial.at[pl.ds(pl.multiple_of(dev * OUT_SHARD, 128), OUT_SHARD), :]
        if self.w == W_DOWN:
            return self.partial.at[pl.ds(pl.multiple_of(dev * DOWN_SHARD, 32), DOWN_SHARD), :]
        return self.partial.at[pl.ds(pl.multiple_of(dev * FF_SHARD + self.row0, 32), self.shape[0]), :]

    def copy1(self, k):
        return pltpu.make_async_remote_copy(
            src_ref=self.block(*self.chips[k], 1 - self.c), dst_ref=self.land1.at[k],
            send_sem=self.s1_send.at[k], recv_sem=self.s1_recv.at[k], device_id=self.sibling, device_id_type=MESH)

    STAGE2 = [(1, 0, 1), (3, 0, 1), (2, 1, 2), (3, 1, 2), (1, 1, 1), (2, 0, 2)]

    def copy2(self, j):
        blk, h, to = self.STAGE2[j]
        src = self.comb.at[j - 4] if j >= 4 else self.stage2.at[blk - 1, pl.ds(h * self.hr, self.hr), :]
        return pltpu.make_async_remote_copy(
            src_ref=src, dst_ref=self.land2.at[j], send_sem=self.s2_send.at[j], recv_sem=self.s2_recv.at[j],
            device_id=(*self.chips[to], self.c), device_id_type=MESH)

    def _rows(self, h=None):
        step = self.step
        lo, n = (0, self.shape[0]) if h is None else (h * self.hr, self.hr)
        return [pl.ds(r0, step) for r0 in range(lo, lo + n, step)]

    def load(self, k):
        return pltpu.make_async_copy(self.block(*self.chips[k], self.c), self.mine.at[k], self.ld_sems.at[k])

    def phase1(self):
        for k in range(4):
            self.copy1(k).start()
        for k in range(4):
            self.load(k).start()

    def phase2(self):
        for k in (3, 1, 2, 0):
            self.copy1(k).wait_recv()
            self.load(k).wait()
            for rs in self._rows():
                s = self.mine[k, rs, :].astype(F32) + self.land1[k, rs, :].astype(F32)
                if k == 0:
                    self.out[rs, :] = s
                else:
                    self.stage2[k - 1, rs, :] = s.astype(BF16)
            for j in {3: (1, 3), 1: (0,), 2: (2,), 0: ()}[k]:
                self.copy2(j).start()

    def phase2b(self):
        for j, got in ((4, 3), (5, 1)):
            blk, h, _ = self.STAGE2[j]
            self.copy2(got).wait_recv()
            for i, rs in enumerate(self._rows(h)):
                lr = pl.ds(i * self.step, self.step)
                self.comb[j - 4, lr, :] = (self.stage2[blk - 1, rs, :].astype(F32) + self.land2[got, lr, :].astype(F32)).astype(BF16)
            self.copy2(j).start()

    def phase3(self):
        for j in (0, 5, 4, 2):
            self.copy2(j).wait_recv()
        for h, (first, second) in enumerate(((0, 5), (4, 2))):
            for i, rs in enumerate(self._rows(h)):
                lr = pl.ds(i * self.step, self.step)
                self.out[rs, :] = (self.out[rs, :] + self.land2[first, lr, :].astype(F32)) + self.land2[second, lr, :].astype(F32)
        for k in range(4):
            self.copy1(k).wait_send()
        for j in range(6):
            self.copy2(j).wait_send()


def _geom(geom):
    if isinstance(geom, tuple):
        w, row0, rows = geom
        assert w == W_UP
        return w, row0, (rows, SHARD[w][1])
    return geom, 0, SHARD[geom]


N_SCATTER_SCRATCH = 10


def _scatter_out_shapes(geoms):
    return tuple(jax.ShapeDtypeStruct(_geom(g)[2], F32) for g in geoms)


def _scatter_out_specs(geoms):
    return (pl.BlockSpec(memory_space=pltpu.VMEM),) * len(geoms)


def _scatter_scratch(geoms):
    out = []
    for g in geoms:
        s = _geom(g)[2]
        hs = (s[0] // 2, s[1])
        out += [pltpu.VMEM((4,) + s, BF16), pltpu.VMEM((4,) + s, BF16), pltpu.VMEM((3,) + s, BF16), pltpu.VMEM((6,) + hs, BF16),
                pltpu.VMEM((2,) + hs, BF16),
                pltpu.SemaphoreType.DMA((4,)), pltpu.SemaphoreType.DMA((4,)), pltpu.SemaphoreType.DMA((6,)),
                pltpu.SemaphoreType.DMA((6,)), pltpu.SemaphoreType.DMA((4,))]
    return out


class _Scatters:
    def __init__(self, geoms, p_refs, out_refs, scratch):
        k = N_SCATTER_SCRATCH
        self.items = [_Scatter(g, p_refs[i], out_refs[i], *scratch[k * i:k * i + k]) for i, g in enumerate(geoms)]

    def phase1(self):
        for s in self.items:
            s.phase1()

    def phase2(self):
        for s in self.items:
            s.phase2()

    def phase2b(self):
        for s in self.items:
            s.phase2b()

    def phase3(self):
        for s in self.items:
            s.phase3()


PACK_W = 1024


SMALL_FULL = [(2, 8, PACK_W), (HEADS, CHUNK, CHUNK), (2, 8, D_FF)]
SMALL_HALF = [(s[0] // 2,) + s[1:] for s in SMALL_FULL]
N_SMALL_SCRATCH = 16


def _small_scratch():
    n_a = len(SMALL_FULL)
    return ([pltpu.VMEM(SMALL_FULL[0], F32)] + [pltpu.VMEM(s, F32) for s in SMALL_HALF] + [pltpu.VMEM(s, F32) for s in SMALL_HALF]
            + [pltpu.VMEM((3,) + s, F32) for s in SMALL_HALF]
            + [pltpu.SemaphoreType.DMA((n_a,)), pltpu.SemaphoreType.DMA((n_a,)), pltpu.SemaphoreType.DMA((n_a, 3)),
               pltpu.SemaphoreType.DMA((n_a, 3)), pltpu.SemaphoreType.DMA((n_a,)), pltpu.SemaphoreType.DMA((n_a,))])


class _SmallReduce:
    def __init__(self, ins, outs, scratch):
        self.ins, self.outs = ins, outs
        (self.pack, *rest) = scratch
        self.rxs, self.css, self.gs = rest[0:3], rest[3:6], rest[6:9]
        self.s1_send, self.s1_recv, self.s2_send, self.s2_recv, self.s3_send, self.s3_recv = rest[9:]
        self.x, self.y, self.c = lax.axis_index("x"), lax.axis_index("y"), lax.axis_index("c")
        self.sibling = (self.x, self.y, 1 - self.c)
        self.chips = [(1 - self.x, self.y), (self.x, 1 - self.y), (1 - self.x, 1 - self.y)]
        self.hl = [s[0] for s in SMALL_HALF]

    def half(self, ref, a, h):
        return ref.at[pl.ds(h * self.hl[a], self.hl[a])]

    def begin(self):
        dg1_ref, dg2_ref, dgf_ref, dgrn_ref, dlng_ref, dlnb_ref, dbs_ref, loss_ref, dws_ref, dcv_ref = self.ins
        pack, c = self.pack, self.c
        pack[...] = jnp.zeros_like(pack)
        pack[0, 0:1, :] = dg1_ref[0:1, :]
        pack[0, 1:2, :] = dg2_ref[0:1, :]
        pack[0, 2:3, :] = dgf_ref[0:1, :]
        pack[0, 3:4, 0:RET_W] = dgrn_ref[0:1, :]
        lsum = loss_ref[0, 0:1, :]
        for i in range(1, N_TB):
            lsum = lsum + loss_ref[i, 0:1, :]
        pack[0, 3:4, RET_W:RET_W + 128] = lsum
        pack[1, 0:HEADS, 0:128] = dlng_ref[0:HEADS, :]
        pack[1, 0:HEADS, 128:256] = dlnb_ref[0:HEADS, :]
        pack[1, 0:HEADS, 256:384] = dbs_ref[0:HEADS, :]
        self.srcs = [pack, dws_ref, dcv_ref]
        n_a = len(self.srcs)
        self.ex1 = [pltpu.make_async_remote_copy(src_ref=self.half(self.srcs[a], a, 1 - c), dst_ref=self.rxs[a],
                                                 send_sem=self.s1_send.at[a], recv_sem=self.s1_recv.at[a],
                                                 device_id=self.sibling, device_id_type=MESH) for a in range(n_a)]
        for cp in self.ex1:
            cp.start()
        self.ex2 = []
        for a in range(n_a):
            self.ex1[a].wait_recv()
            self.css[a][...] = self.half(self.srcs[a], a, c)[...] + self.rxs[a][...]
            for j, chip in enumerate(self.chips):
                cp = pltpu.make_async_remote_copy(src_ref=self.css[a], dst_ref=self.gs[a].at[j], send_sem=self.s2_send.at[a, j],
                                                  recv_sem=self.s2_recv.at[a, j], device_id=(*chip, c), device_id_type=MESH)
                cp.start()
                self.ex2.append(cp)

    def end(self):
        c, x, y = self.c, self.x, self.y
        ex3 = []
        for a in range(len(self.srcs)):
            css, gs, out = self.css[a], self.gs[a], self.outs[a]
            for j in range(3):
                self.ex2[3 * a + j].wait_recv()
            tot = None
            for q in range(4):
                k = jnp.where(x != (q >> 1), 1, 0) + jnp.where(y != (q & 1), 2, 0)
                term = jnp.where(k == 0, css[...], jnp.where(k == 1, gs[0], jnp.where(k == 2, gs[1], gs[2])))
                tot = term if tot is None else tot + term
            self.half(out, a, c)[...] = tot
            cp = pltpu.make_async_remote_copy(src_ref=self.half(out, a, c), dst_ref=self.half(out, a, c), send_sem=self.s3_send.at[a],
                                              recv_sem=self.s3_recv.at[a], device_id=self.sibling, device_id_type=MESH)
            cp.start()
            ex3.append(cp)
        for a in range(len(self.srcs)):
            out = self.outs[a]
            pltpu.make_async_remote_copy(src_ref=self.half(out, a, 1 - c), dst_ref=self.half(out, a, 1 - c), send_sem=self.s3_send.at[a],
                                         recv_sem=self.s3_recv.at[a], device_id=self.sibling, device_id_type=MESH).wait_recv()
        for cp in self.ex1 + self.ex2 + ex3:
            cp.wait_send()


def _adam_math(w, g, m, v):
    nm = ADAM_B1 * m + (1.0 - ADAM_B1) * g
    nv = ADAM_B2 * v + (1.0 - ADAM_B2) * (g * g)
    d = -ADAM_LR * ((nm / (1.0 - ADAM_B1 ** ADAM_STEP)) / (jnp.sqrt(nv / (1.0 - ADAM_B2 ** ADAM_STEP)) + ADAM_EPS) + ADAM_WD * w)
    return d, nm, nv


def _adamw(name, w, gs, m, v, rows):
    _, r, cdim = w.shape
    n_steps = r // rows
    half = gs[0].shape[0] // rows

    def body(w_ref, *rest):
        g_refs, (m_ref, v_ref, go_ref, d_ref, nm_ref, nv_ref) = rest[:len(gs)], rest[len(gs):]
        gg = g_refs[0][...]
        if len(gs) == 2:
            gg = jnp.where(pl.program_id(0) < half, gg, g_refs[1][...])
        go_ref[0] = gg
        d, nm, nv = _adam_math(w_ref[0], gg, m_ref[0], v_ref[0])
        d_ref[0], nm_ref[0], nv_ref[0] = d, nm, nv

    spec3 = pl.BlockSpec((1, rows, cdim), lambda i: (0, i, 0))
    if len(gs) == 1:
        g_specs = [pl.BlockSpec((rows, cdim), lambda i: (i, 0))]
    else:
        g_specs = [pl.BlockSpec((rows, cdim), lambda i: (jnp.minimum(i, half - 1), 0)),
                   pl.BlockSpec((rows, cdim), lambda i: (jnp.maximum(i - half, 0), 0))]
    sh = jax.ShapeDtypeStruct((1, r, cdim), F32)
    return pl.pallas_call(
        body, name=name, grid=(n_steps,), out_shape=(sh, sh, sh, sh),
        in_specs=[spec3] + g_specs + [spec3, spec3], out_specs=(spec3,) * 4,
        compiler_params=_cparams(("parallel",)),
    )(w, *gs, m, v)


def _adamw_small(rp, rws, rcv, gcw, params):
    n_p = len(params)

    def body(*refs):
        rp_ref, rws_ref, rcv_ref, gcw_ref = refs[:4]
        ins = refs[4:4 + 3 * n_p]
        outs = refs[4 + 3 * n_p:]
        grads = [rp_ref[0, 0:1, :], rp_ref[0, 1:2, :], rp_ref[0, 2:3, :], rp_ref[0, 3:4, 0:RET_W],
                 rp_ref[1, 0:HEADS, 0:128], rp_ref[1, 0:HEADS, 128:256], rp_ref[1, 0:HEADS, 256:384],
                 rws_ref[...], gcw_ref[0], None]
        for p in range(n_p):
            w_ref, m_ref, v_ref = ins[3 * p:3 * p + 3]
            o = outs[4 * p:4 * p + 4]
            if p == n_p - 1:
                for hf in range(2):
                    cs = slice(hf * D_FF, (hf + 1) * D_FF)
                    g = rcv_ref[hf, 3:4, :]
                    res = (g,) + _adam_math(w_ref[:, cs], g, m_ref[:, cs], v_ref[:, cs])
                    for t in range(4):
                        o[t][:, cs] = res[t]
                continue
            lead = w_ref.ndim > grads[p].ndim
            rd = (lambda r: r[0]) if lead else (lambda r: r[...])
            res = (grads[p],) + _adam_math(rd(w_ref), grads[p], rd(m_ref), rd(v_ref))
            for t in range(4):
                if lead:
                    o[t][0] = res[t]
                else:
                    o[t][...] = res[t]

    vm = pl.BlockSpec(memory_space=pltpu.VMEM)
    flat = [a for tr in params for a in tr]
    out_shape = tuple(jax.ShapeDtypeStruct(tr[0].shape, F32) for tr in params for _ in range(4))
    res = pl.pallas_call(
        body, name="adamw_small", out_shape=out_shape, in_specs=[vm] * (4 + len(flat)), out_specs=(vm,) * len(out_shape),
        compiler_params=_cparams(),
    )(rp, rws, rcv, gcw, *flat)
    return [res[4 * p:4 * p + 4] for p in range(n_p)]


def kernel(x, mix_norm_g, w_in, ret_norm_g, sgu_ln_g, sgu_ln_b, sgu_w_s, sgu_b_s, w_out, ffn_norm_g, w_up, conv_w, conv_b, w_down, final_norm_g, loss_target, m_mix_norm_g, m_w_in, m_ret_norm_g, m_sgu_ln_g, m_sgu_ln_b, m_sgu_w_s, m_sgu_b_s, m_w_out, m_ffn_norm_g, m_w_up, m_conv_w, m_conv_b, m_w_down, m_final_norm_g, v_mix_norm_g, v_w_in, v_ret_norm_g, v_sgu_ln_g, v_sgu_ln_b, v_sgu_w_s, v_sgu_b_s, v_w_out, v_ffn_norm_g, v_w_up, v_conv_w, v_conv_b, v_w_down, v_final_norm_g):
    xs = x[0]
    tgt = loss_target[0]
    cos2, sin2 = _rope_tables()
    mask, qdec, kdec = _decay_tables()
    grn = ret_norm_g.reshape(1, RET_W)
    lng = sgu_ln_g.reshape(1, SGU_W)
    lnb = sgu_ln_b.reshape(1, SGU_W)
    ws = sgu_w_s[0]
    bsb = jnp.broadcast_to(sgu_b_s[0][:, :, None], (HEADS, CHUNK, HEAD_DIM))
    gf = final_norm_g.reshape(1, D_MODEL)
    me = 4 * lax.axis_index("x") + 2 * lax.axis_index("y") + lax.axis_index("c")
    tr = lambda a: jnp.transpose(a[0])[None]

    proj, h1, win_g, cw_sh, wout_g, wdn_g, su = _fwd_proj(xs, mix_norm_g, cos2, sin2, w_in[0], w_out[0], tr(w_up)[0], w_down[0],
                                                         conv_w[0])
    cw_g = jnp.transpose(cw_sh, (1, 0, 2)).reshape(8, 2 * D_FF)
    x2, mixcat, o, sprev, wup_g = _fwd_mix(xs, proj, wout_g, grn, lng, lnb, ws, bsb, mask, qdec, kdec, su)
    h2, up_pre, u_conv, act, x3, loss_parts = _fwd_ffn(x2, ffn_norm_g, wup_g, cw_g, conv_b, wdn_g, gf, tgt)

    dx3, dpre, dx2, dgf, dg2, dcv = _bwd_ffn(x3, tgt, gf, x2, ffn_norm_g, up_pre, u_conv, wup_g, cw_g, wdn_g)
    band = 512
    (gdn_p,) = _wgrad("wgrad_down", act, dx3, tm=FF_TILE)
    (gout_p,) = _wgrad("wgrad_out", mixcat, dx2, tn=512)
    gup_p, g_dn = _wgrad("wgrad_up", dpre, h2, tm=FF_TILE, hosted=[(W_DOWN, gdn_p)])
    dproj, dgrn, dlng, dlnb, dws, dbs, g_up_a = _bwd_mix(
        dx2, proj, o, sprev, wout_g, grn, lng, lnb, ws, bsb, mask, qdec, kdec, cos2, sin2, [((W_UP, 0, band), gup_p)])
    gin_p, g_up_b, g_out = _wgrad("wgrad_in", h1, dproj, tn=768,
                                  hosted=[((W_UP, band, FF_SHARD - band), gup_p), (W_OUT, gout_p)])
    grad_x, g_in, rp, rws, rcv = _bwd_proj(dproj, win_g, xs, mix_norm_g, dx2, gin_p,
                                           (dg2, dgf, dgrn, dlng, dlnb, dbs, loss_parts, dws, dcv))
    loss = rp[0, 3, RET_W]
    gcw = lax.dynamic_slice(rcv, (me // (N_DEV // 2), 0, (me % (N_DEV // 2)) * FF_SHARD), (1, 3, FF_SHARD))

    table = {}
    for name, w, gs, m, v, rows in (("w_in", w_in, [g_in], m_w_in, v_w_in, 256), ("w_out", w_out, [g_out], m_w_out, v_w_out, 128),
                                    ("w_up", tr(w_up), [g_up_a, g_up_b], tr(m_w_up), tr(v_w_up), 64),
                                    ("w_down", w_down, [g_dn], m_w_down, v_w_down, 88)):
        table[name] = _adamw("adamw_" + name, w, gs, m, v, rows)
    table["w_up"] = tuple(tr(a) for a in table["w_up"])
    row = lambda a: a.reshape(1, D_MODEL)
    names_small = ["mix_norm_g", "ffn_norm_g", "final_norm_g", "ret_norm_g", "sgu_ln_g", "sgu_ln_b", "sgu_b_s", "sgu_w_s",
                   "conv_w", "conv_b"]
    params = [(mix_norm_g, m_mix_norm_g, v_mix_norm_g), (ffn_norm_g, m_ffn_norm_g, v_ffn_norm_g),
              (row(final_norm_g), row(m_final_norm_g), row(v_final_norm_g)), (ret_norm_g, m_ret_norm_g, v_ret_norm_g),
              (sgu_ln_g, m_sgu_ln_g, v_sgu_ln_g), (sgu_ln_b, m_sgu_ln_b, v_sgu_ln_b), (sgu_b_s, m_sgu_b_s, v_sgu_b_s),
              (sgu_w_s, m_sgu_w_s, v_sgu_w_s), (conv_w, m_conv_w, v_conv_w), (conv_b, m_conv_b, v_conv_b)]
    for n, res in zip(names_small, _adamw_small(rp, rws, rcv, gcw, params)):
        table[n] = res
    table["final_norm_g"] = tuple(a.reshape(D_MODEL) for a in table["final_norm_g"])

    order = ["mix_norm_g", "w_in", "ret_norm_g", "sgu_ln_g", "sgu_ln_b", "sgu_w_s", "sgu_b_s", "w_out", "ffn_norm_g", "w_up",
             "conv_w", "conv_b", "w_down", "final_norm_g"]
    outs = [loss, grad_x[None]]
    for col in range(4):
        outs += [table[n][col] for n in order]
    return tuple(outs)
```

```python
import functools
import math

import jax
import jax.numpy as jnp
import numpy as np
from jax import lax
from jax.experimental import pallas as pl
from jax.experimental.pallas import tpu as pltpu

F32 = jnp.float32
BF16 = jnp.bfloat16
MESH = pl.DeviceIdType.MESH

N_DEV = 8
SEQ = 2048
D_MODEL = 1024
CHUNK = 128
N_CHUNK = SEQ // CHUNK
HEADS = 4
HEAD_DIM = 128
RET_W = 512
SGU_W = 512
PROJ_W = 3072
D_FF = 2816
FF_SHARD = 704
FF_TILE = 1408
FF_TILES = ((0, 1536), (1536, 1280))
IN_SHARD = PROJ_W // N_DEV
OUT_SHARD = D_MODEL // N_DEV
DOWN_SHARD = D_FF // N_DEV
TM = 256
N_TB = SEQ // TM
FWD_PROJ_PASS_AT = 5
FWD_MIX_PASS_AT = 10
EPS = 1e-6
ROPE_BASE = 10000.0
K_SCALE = HEAD_DIM ** -0.5
INV_SQRT2 = 0.7071067811865476
INV_SQRT_2PI = 0.3989422804014327

ADAM_LR = 0.001
ADAM_B1 = 0.9
ADAM_B2 = 0.999
ADAM_EPS = 1e-08
ADAM_WD = 0.01
ADAM_STEP = 10

VMEM_LIMIT = 56 * 1024 * 1024


def _cparams(sem=None, vmem=VMEM_LIMIT):
    return pltpu.CompilerParams(dimension_semantics=sem, vmem_limit_bytes=vmem)


def _resident(shape):
    nd = len(shape)
    return pl.BlockSpec(shape, lambda *_: (0,) * nd, pipeline_mode=pl.Buffered(1))


def _dot(a, b):
    return jnp.dot(a, b, preferred_element_type=F32)


def _dot_nt(a, b):
    return lax.dot_general(a, b, (((1,), (1,)), ((), ())), preferred_element_type=F32)


def _dot_tn(a, b):
    return lax.dot_general(a, b, (((0,), (0,)), ((), ())), preferred_element_type=F32)


def _sigmoid(x):
    return 1.0 / (1.0 + jnp.exp(-x))


def _gelu(x):
    return 0.5 * x * (1.0 + lax.erf(x * INV_SQRT2))


def _gelu_grad(x):
    return 0.5 * (1.0 + lax.erf(x * INV_SQRT2)) + x * (jnp.exp(-0.5 * x * x) * INV_SQRT_2PI)


def _rot(xh, cos2, sin2):
    return xh * cos2 + pltpu.roll(xh, HEAD_DIM // 2, 1) * sin2


def _rot_t(dh, cos2, sin2):
    return dh * cos2 + pltpu.roll(dh * sin2, HEAD_DIM // 2, 1)


def _rope_freq():
    half = HEAD_DIM // 2
    inv_freq = jnp.power(ROPE_BASE, -jnp.arange(half, dtype=F32) / half)
    return jnp.concatenate([inv_freq, inv_freq])[None, :]


def _rope_block(inv2, first_row):
    pos = (lax.broadcasted_iota(jnp.int32, (TM, HEAD_DIM), 0) + first_row).astype(F32)
    ang = pos * inv2
    sin = jnp.sin(ang)
    lane = lax.broadcasted_iota(jnp.int32, (TM, HEAD_DIM), 1)
    return jnp.cos(ang), jnp.where(lane < HEAD_DIM // 2, -sin, sin)


def _decay_tables():
    log_gamma = jnp.log(1.0 - jnp.power(2.0, -5.0 - jnp.arange(HEADS, dtype=F32)))
    pos = jnp.arange(CHUNK, dtype=F32)
    diff = pos[:, None] - pos[None, :]
    mask = jnp.where(diff >= 0.0, jnp.exp(log_gamma[:, None, None] * jnp.maximum(diff, 0.0)[None]), 0.0)
    k_decay = jnp.exp(log_gamma[:, None] * (CHUNK - 1.0 - pos)[None])
    q_decay = jnp.exp(log_gamma[:, None] * (pos + 1.0)[None])
    kd = jnp.broadcast_to(k_decay[:, :, None], (HEADS, CHUNK, HEAD_DIM))
    qd = jnp.broadcast_to(q_decay[:, :, None], (HEADS, CHUNK, HEAD_DIM))
    return mask.astype(F32), qd.astype(F32), kd.astype(F32)


def _chunk_decay():
    lg = np.log(np.float32(1.0) - np.power(np.float32(2.0), -5.0 - np.arange(HEADS, dtype=np.float32))).astype(np.float32)
    return [float(np.exp(lg[h] * np.float32(CHUNK))) for h in range(HEADS)]


W_IN, W_OUT, W_UP, W_DOWN, W_CONV = range(5)
GATHERED = {W_IN: ((D_MODEL, PROJ_W), BF16), W_OUT: ((D_MODEL, D_MODEL), BF16), W_UP: ((2 * D_FF, D_MODEL), BF16),
            W_DOWN: ((D_FF, D_MODEL), BF16), W_CONV: ((N_DEV, 8, FF_SHARD), F32)}
SHARD = {W_IN: (D_MODEL, IN_SHARD), W_OUT: (OUT_SHARD, D_MODEL), W_UP: (FF_SHARD, D_MODEL), W_DOWN: (DOWN_SHARD, D_MODEL),
         W_CONV: (8, FF_SHARD)}


class _Gather:
    N_SEMS = 9

    def __init__(self, ids, stages, gathered, send_sems, recv_sems, local_sems):
        self.ids, self.stages, self.gathered = ids, stages, gathered
        self.send_sems, self.recv_sems, self.local_sems = send_sems, recv_sems, local_sems
        self.x, self.y, self.c = lax.axis_index("x"), lax.axis_index("y"), lax.axis_index("c")
        self.me = (self.x, self.y, self.c)
        self.sibling = (self.x, self.y, 1 - self.c)
        self.chips = [(1 - self.x, self.y), (self.x, 1 - self.y), (1 - self.x, 1 - self.y)]

    def slot(self, n, px, py, pc):
        dev = 4 * px + 2 * py + pc
        w, g = self.ids[n], self.gathered[n]
        if w == W_IN:
            return g.at[:, pl.ds(pl.multiple_of(dev * IN_SHARD, 128), IN_SHARD)]
        if w == W_OUT:
            return g.at[pl.ds(pl.multiple_of(dev * OUT_SHARD, 128), OUT_SHARD), :]
        if w == W_DOWN:
            return g.at[pl.ds(pl.multiple_of(dev * DOWN_SHARD, 32), DOWN_SHARD), :]
        if w == W_UP:
            return g.at[pl.ds(pl.multiple_of(dev * FF_SHARD, 32), FF_SHARD), :]
        return g.at[dev]

    def half(self, n, px, py, pc, h):
        dev = 4 * px + 2 * py + pc
        w, g = self.ids[n], self.gathered[n]
        if w == W_IN:
            return g.at[pl.ds(h * (D_MODEL // 2), D_MODEL // 2), pl.ds(pl.multiple_of(dev * IN_SHARD, 128), IN_SHARD)]
        rows = SHARD[w][0] // 2
        return g.at[pl.ds(pl.multiple_of(dev * SHARD[w][0] + h * rows, 16), rows), :]

    def tree(self, n):
        return self.ids[n] != W_CONV

    def copy(self, n, k, block, to, src=None, h=None):
        ref = self.slot(n, *block) if h is None else self.half(n, *block, h)
        return pltpu.make_async_remote_copy(
            src_ref=ref if src is None else src, dst_ref=ref,
            send_sem=self.send_sems.at[n, k], recv_sem=self.recv_sems.at[n, k], device_id=to, device_id_type=MESH)

    def _mine(self):
        return [pltpu.make_async_copy(self.stages[n], self.slot(n, *self.me), self.local_sems.at[n]) for n in range(len(self.ids))]

    def _first(self):
        out = []
        for n in range(len(self.ids)):
            out.append(self.copy(n, 0, self.me, self.sibling, src=self.stages[n]))
            out += [self.copy(n, 1 + j, self.me, (*chip, self.c), src=self.stages[n])
                    for j, chip in enumerate(self.chips[:2] if self.tree(n) else self.chips)]
        return out

    def start(self):
        for cp in self._mine() + self._first():
            cp.start()

    def _passed(self, j):
        dev = (*self.chips[j], self.c)
        out = []
        for n in range(len(self.ids)):
            if not self.tree(n):
                out.append(self.copy(n, 4 + j, dev, self.sibling))
            elif j < 2:
                out += [self.copy(n, 3 + j, dev, (*self.chips[1 - j], self.c), h=j), self.copy(n, 5 + j, dev, self.sibling)]
            else:
                out += [self.copy(n, 7, dev, self.sibling, h=0), self.copy(n, 8, dev, self.sibling, h=1)]
        return out

    def near(self):
        for j in range(2):
            dev = (*self.chips[j], self.c)
            for n in range(len(self.ids)):
                self.copy(n, 1 + j, dev, self.me).wait_recv()
            for cp in self._passed(j):
                cp.start()

    def finish(self):
        dev = (*self.chips[2], self.c)
        for n in range(len(self.ids)):
            if self.tree(n):
                self.copy(n, 3, dev, self.me, h=0).wait_recv()
                self.copy(n, 4, dev, self.me, h=1).wait_recv()
            else:
                self.copy(n, 3, dev, self.me).wait_recv()
        for cp in self._passed(2):
            cp.start()
        for n in range(len(self.ids)):
            self.copy(n, 0, self.sibling, self.me).wait_recv()
            for j, chip in enumerate(self.chips):
                dev = (*chip, 1 - self.c)
                if not self.tree(n):
                    self.copy(n, 4 + j, dev, self.me).wait_recv()
                elif j < 2:
                    self.copy(n, 5 + j, dev, self.me).wait_recv()
                else:
                    self.copy(n, 7, dev, self.me, h=0).wait_recv()
                    self.copy(n, 8, dev, self.me, h=1).wait_recv()
        for cp in self._mine():
            cp.wait()
        for cp in self._first() + self._passed(0) + self._passed(1) + self._passed(2):
            cp.wait_send()


def _gather_scratch(n):
    return [pltpu.SemaphoreType.DMA((n, _Gather.N_SEMS)), pltpu.SemaphoreType.DMA((n, _Gather.N_SEMS)), pltpu.SemaphoreType.DMA((n,))]


def _gathered_shapes(ids):
    return tuple(jax.ShapeDtypeStruct(*GATHERED[w]) for w in ids)


def _fwd_proj(x, g1, inv2, w_in, w_out, w_up, w_down, conv_w):
    ids_a, ids_b = [W_IN, W_CONV], [W_OUT, W_DOWN]

    def body(x_ref, g_ref, inv_ref, in_hbm, out_hbm, up_hbm, dn_hbm, cw_ref,
             proj_ref, h1_ref, cos_ref, sin_ref, gin, gcw, gout, gdn, su_ref,
             w_vm, s_in, s_cw, s_out, s_dn, f_in, f_out, f_up, f_dn, ld_sems,
             a_send, a_recv, a_local, b_send, b_recv, b_local):
        ag_a = _Gather(ids_a, [s_in, s_cw], [gin, gcw], a_send, a_recv, a_local)
        ag_b = _Gather(ids_b, [s_out, s_dn], [gout, gdn], b_send, b_recv, b_local)

        @pl.when(pl.program_id(0) == 0)
        def _():
            loads = [pltpu.make_async_copy(src, dst, ld_sems.at[i])
                     for i, (src, dst) in enumerate(((in_hbm, f_in), (out_hbm, f_out), (dn_hbm, f_dn), (up_hbm, f_up)))]
            for cp in loads:
                cp.start()
            s_cw[...] = jnp.zeros_like(s_cw)
            for k in range(3):
                s_cw[k:k + 1, :] = cw_ref[k]
            loads[0].wait()
            s_in[...] = f_in[...].astype(BF16)
            ag_a.start()
            loads[1].wait()
            s_out[...] = f_out[...].astype(BF16)
            loads[2].wait()
            s_dn[...] = f_dn[...].astype(BF16)
            ag_a.near()
            ag_b.start()
            loads[3].wait()
            su_ref[...] = f_up[...].astype(BF16)
            ag_a.finish()
            fill = pltpu.make_async_copy(gin, w_vm, ld_sems.at[4])
            fill.start()
            fill.wait()

        pl.when(pl.program_id(0) == FWD_PROJ_PASS_AT)(ag_b.near)

        xb = x_ref[...]
        r = lax.rsqrt(jnp.mean(xb * xb, axis=-1, keepdims=True) + EPS)
        h = ((xb * r) * g_ref[...]).astype(BF16)
        h1_ref[...] = h
        p = _dot(h, w_vm[...])
        c2, s2 = _rope_block(inv_ref[...], pl.program_id(0) * TM)
        cos_ref[...], sin_ref[...] = c2, s2
        for hd in range(HEADS):
            sl = slice(hd * HEAD_DIM, (hd + 1) * HEAD_DIM)
            proj_ref[:, sl] = _rot(p[:, sl], c2, s2)
            ks = slice(RET_W + hd * HEAD_DIM, RET_W + (hd + 1) * HEAD_DIM)
            proj_ref[:, ks] = _rot(p[:, ks], c2, s2) * K_SCALE
        proj_ref[:, 2 * RET_W:] = p[:, 2 * RET_W:]

        pl.when(pl.program_id(0) == N_TB - 1)(ag_b.finish)

    tok = lambda w: pl.BlockSpec((TM, w), lambda i: (i, 0))
    hbm = pl.BlockSpec(memory_space=pl.ANY)
    vm = pl.BlockSpec(memory_space=pltpu.VMEM)
    return pl.pallas_call(
        body, name="fwd_proj", grid=(N_TB,),
        out_shape=(jax.ShapeDtypeStruct((SEQ, PROJ_W), F32), jax.ShapeDtypeStruct((SEQ, D_MODEL), BF16),
                   jax.ShapeDtypeStruct((SEQ, HEAD_DIM), F32), jax.ShapeDtypeStruct((SEQ, HEAD_DIM), F32))
        + _gathered_shapes(ids_a + ids_b) + (jax.ShapeDtypeStruct(SHARD[W_UP], BF16),),
        in_specs=[tok(D_MODEL), _resident((1, D_MODEL)), _resident((1, HEAD_DIM)), hbm, hbm, hbm, hbm, vm],
        out_specs=(tok(PROJ_W), tok(D_MODEL), tok(HEAD_DIM), tok(HEAD_DIM), hbm, hbm, hbm, hbm, vm),
        scratch_shapes=[pltpu.VMEM((D_MODEL, PROJ_W), BF16), pltpu.VMEM(SHARD[W_IN], BF16), pltpu.VMEM(SHARD[W_CONV], F32),
                        pltpu.VMEM(SHARD[W_OUT], BF16), pltpu.VMEM(SHARD[W_DOWN], BF16),
                        pltpu.VMEM(SHARD[W_IN], F32), pltpu.VMEM(SHARD[W_OUT], F32), pltpu.VMEM(SHARD[W_UP], F32),
                        pltpu.VMEM(SHARD[W_DOWN], F32), pltpu.SemaphoreType.DMA((5,))]
        + _gather_scratch(len(ids_a)) + _gather_scratch(len(ids_b)),
        compiler_params=_cparams(("arbitrary",)),
    )(x, g1, inv2, w_in, w_out, w_up, w_down, conv_w)


def _causal(w):
    r = lax.broadcasted_iota(jnp.int32, (CHUNK, CHUNK), 0)
    c = lax.broadcasted_iota(jnp.int32, (CHUNK, CHUNK), 1)
    return jnp.where(r >= c, w, 0.0)


def _fwd_mix(x, proj, wout_g, grn, lng, lnb, ws, bsb, mask, qdec, kdec, su):
    cdec = _chunk_decay()
    ids = [W_UP]

    def body(x_ref, p_ref, w_ref, grn_ref, lng_ref, lnb_ref, ws_ref, bsb_ref, m_ref, qd_ref, kd_ref, su_ref,
             x2_ref, cat_ref, o_ref, sp_ref, gup, state, send_sems, recv_sems, local_sems):
        ag = _Gather(ids, [su_ref], [gup], send_sems, recv_sems, local_sems)

        @pl.when(pl.program_id(0) == 0)
        def _():
            state[...] = jnp.zeros_like(state)
            ag.start()

        for h in range(HEADS):
            sl = slice(h * HEAD_DIM, (h + 1) * HEAD_DIM)
            q = p_ref[:, sl]
            k = p_ref[:, RET_W + h * HEAD_DIM:RET_W + (h + 1) * HEAD_DIM]
            v = p_ref[:, 2 * RET_W + h * HEAD_DIM:2 * RET_W + (h + 1) * HEAD_DIM]
            g = p_ref[:, 3 * RET_W + h * HEAD_DIM:3 * RET_W + (h + 1) * HEAD_DIM]
            qb, kb, vb = q.astype(BF16), k.astype(BF16), v.astype(BF16)
            a = _dot_nt(qb, kb) * m_ref[h]
            spb = state[h].astype(BF16)
            sp_ref[0, h] = spb
            o = _dot(a.astype(BF16), vb) + _dot((q * qd_ref[h]).astype(BF16), spb)
            state[h] = state[h] * cdec[h] + _dot_tn((k * kd_ref[h]).astype(BF16), vb)
            o_ref[:, sl] = o
            rinv = lax.rsqrt(jnp.mean(o * o, axis=-1, keepdims=True) + EPS)
            rn = (o * rinv) * grn_ref[:, sl]
            cat_ref[:, sl] = ((g * _sigmoid(g)) * rn).astype(BF16)
        for gi in range(HEADS):
            sl = slice(gi * HEAD_DIM, (gi + 1) * HEAD_DIM)
            u = p_ref[:, 4 * RET_W + gi * HEAD_DIM:4 * RET_W + (gi + 1) * HEAD_DIM]
            sv = p_ref[:, 4 * RET_W + SGU_W + gi * HEAD_DIM:4 * RET_W + SGU_W + (gi + 1) * HEAD_DIM]
            gv = _gelu(sv)
            xc = gv - jnp.mean(gv, axis=-1, keepdims=True)
            vn = (xc * lax.rsqrt(jnp.mean(xc * xc, axis=-1, keepdims=True) + EPS)) * lng_ref[:, sl] + lnb_ref[:, sl]
            mixed = _dot(_causal(ws_ref[gi]).astype(BF16), vn.astype(BF16)) + bsb_ref[gi]
            cat_ref[:, RET_W + gi * HEAD_DIM:RET_W + (gi + 1) * HEAD_DIM] = (_gelu(u) * mixed).astype(BF16)
        x2_ref[...] = x_ref[...] + _dot(cat_ref[...], w_ref[...])

        pl.when(pl.program_id(0) == FWD_MIX_PASS_AT)(ag.near)
        pl.when(pl.program_id(0) == N_CHUNK - 1)(ag.finish)

    ch = lambda w: pl.BlockSpec((CHUNK, w), lambda i: (i, 0))
    hcc = (HEADS, CHUNK, CHUNK)
    hbm = pl.BlockSpec(memory_space=pl.ANY)
    return pl.pallas_call(
        body, name="fwd_mix", grid=(N_CHUNK,),
        out_shape=(jax.ShapeDtypeStruct((SEQ, D_MODEL), F32), jax.ShapeDtypeStruct((SEQ, D_MODEL), BF16),
                   jax.ShapeDtypeStruct((SEQ, RET_W), F32), jax.ShapeDtypeStruct((N_CHUNK, HEADS, HEAD_DIM, HEAD_DIM), BF16))
        + _gathered_shapes(ids),
        in_specs=[ch(D_MODEL), ch(PROJ_W), _resident((D_MODEL, D_MODEL)), _resident((1, RET_W)), _resident((1, SGU_W)),
                  _resident((1, SGU_W)), _resident(hcc), _resident(hcc), _resident(hcc), _resident(hcc), _resident(hcc), hbm],
        out_specs=(ch(D_MODEL), ch(D_MODEL), ch(RET_W), pl.BlockSpec((1, HEADS, HEAD_DIM, HEAD_DIM), lambda i: (i, 0, 0, 0)), hbm),
        scratch_shapes=[pltpu.VMEM((HEADS, HEAD_DIM, HEAD_DIM), F32)] + _gather_scratch(len(ids)),
        compiler_params=_cparams(("arbitrary",)),
    )(x, proj, wout_g, grn, lng, lnb, ws, bsb, mask, qdec, kdec, su)


def _conv_taps(p, prev8):
    row = lax.broadcasted_iota(jnp.int32, p.shape, 0)
    p1 = jnp.where(row == 0, prev8[7:8, :], pltpu.roll(p, 1, 0))
    p2 = jnp.where(row == 0, prev8[6:7, :], jnp.where(row == 1, prev8[7:8, :], pltpu.roll(p, 2, 0)))
    return p1, p2


def _fwd_ffn(x2, g2, wup_g, cw_g, cb_g, wdn_g, gf, tgt):
    def body(x_ref, g_ref, wu_ref, cw_ref, cb_ref, wd_ref, gf_ref, t_ref, h2_ref, up_ref, u_ref, act_ref, x3_ref, loss_ref, carry):
        @pl.when(pl.program_id(0) == 0)
        def _():
            carry[...] = jnp.zeros_like(carry)

        xb = x_ref[...]
        r = lax.rsqrt(jnp.mean(xb * xb, axis=-1, keepdims=True) + EPS)
        h = ((xb * r) * g_ref[...]).astype(BF16)
        h2_ref[...] = h
        acc = xb
        for t0, tw in FF_TILES:
            u = []
            for c0 in (t0, D_FF + t0):
                cs = slice(c0, c0 + tw)
                p = _dot_nt(h, wu_ref[pl.ds(c0, tw), :])
                up_ref[:, cs] = p.astype(BF16)
                p1, p2 = _conv_taps(p, carry[:, cs])
                carry[:, cs] = p[TM - 8:, :]
                us = p2 * cw_ref[0:1, cs] + p1 * cw_ref[1:2, cs] + p * cw_ref[2:3, cs] + cb_ref[:, cs]
                u_ref[:, cs] = us.astype(BF16)
                u.append(us)
            a = ((u[0] * _sigmoid(u[0])) * u[1]).astype(BF16)
            act_ref[:, t0:t0 + tw] = a
            acc = acc + _dot(a, wd_ref[pl.ds(t0, tw), :])
        x3_ref[...] = acc
        r3 = lax.rsqrt(jnp.mean(acc * acc, axis=-1, keepdims=True) + EPS)
        diff = (acc * r3) * gf_ref[...] - t_ref[...]
        loss_ref[...] = jnp.full(loss_ref.shape, 0.5 * jnp.sum(jnp.mean(diff * diff, axis=-1)), F32)

    tok = lambda w: pl.BlockSpec((TM, w), lambda i: (i, 0))
    return pl.pallas_call(
        body, name="fwd_ffn", grid=(N_TB,),
        out_shape=(jax.ShapeDtypeStruct((SEQ, D_MODEL), BF16), jax.ShapeDtypeStruct((SEQ, 2 * D_FF), BF16),
                   jax.ShapeDtypeStruct((SEQ, 2 * D_FF), BF16),
                   jax.ShapeDtypeStruct((SEQ, D_FF), BF16), jax.ShapeDtypeStruct((SEQ, D_MODEL), F32),
                   jax.ShapeDtypeStruct((N_TB, 8, 128), F32)),
        in_specs=[tok(D_MODEL), _resident((1, D_MODEL)), _resident((2 * D_FF, D_MODEL)), _resident((8, 2 * D_FF)),
                  _resident((1, 2 * D_FF)), _resident((D_FF, D_MODEL)), _resident((1, D_MODEL)), tok(D_MODEL)],
        out_specs=(tok(D_MODEL), tok(2 * D_FF), tok(2 * D_FF), tok(D_FF), tok(D_MODEL),
                   pl.BlockSpec((1, 8, 128), lambda i: (i, 0, 0))),
        scratch_shapes=[pltpu.VMEM((8, 2 * D_FF), F32)],
        compiler_params=_cparams(("arbitrary",)),
    )(x2, g2, wup_g, cw_g, cb_g, wdn_g, gf, tgt)


def _bwd_ffn(x3, tgt, gf, x2, g2, up_pre, u_conv, wup_g, cw_g, wdn_g):
    def body(x3_ref, t_ref, gf_ref, x2_ref, g2_ref, up_ref, u_ref, wu_ref, cw_ref, wd_ref,
             dx3_ref, dpre_ref, dx2_ref, dgf_ref, dg2_ref, dcv_ref, nxt):
        i = pl.program_id(0)

        @pl.when(i == 0)
        def _():
            nxt[...] = jnp.zeros_like(nxt)
            dgf_ref[...] = jnp.zeros_like(dgf_ref)
            dg2_ref[...] = jnp.zeros_like(dg2_ref)
            dcv_ref[...] = jnp.zeros_like(dcv_ref)

        x3 = x3_ref[...]
        r3 = lax.rsqrt(jnp.mean(x3 * x3, axis=-1, keepdims=True) + EPS)
        xh3 = x3 * r3
        dy = (xh3 * gf_ref[...] - t_ref[...]) * (1.0 / D_MODEL)
        dgf_ref[0:1, :] += jnp.sum(dy * xh3, axis=0, keepdims=True)
        t3 = dy * gf_ref[...]
        dx3 = r3 * (t3 - xh3 * jnp.mean(t3 * xh3, axis=-1, keepdims=True))
        dx3b = dx3.astype(BF16)
        dx3_ref[...] = dx3b
        dh2 = jnp.zeros((TM, D_MODEL), F32)
        for t0, tw in FF_TILES:
            row = lax.broadcasted_iota(jnp.int32, (TM, tw), 0)
            ts = slice(t0, t0 + tw)
            dact = _dot_nt(dx3b, wd_ref[pl.ds(t0, tw), :])
            ua = u_ref[:, ts].astype(F32)
            ub = u_ref[:, D_FF + t0:D_FF + t0 + tw].astype(F32)
            sg = _sigmoid(ua)
            du = [dact * ub * (sg * (1.0 + ua * (1.0 - sg))), dact * (ua * sg)]
            for n in range(2):
                d = du[n]
                c0 = n * D_FF + t0
                cs = slice(c0, c0 + tw)
                nx = nxt[:, cs]
                n1 = jnp.where(row == TM - 1, nx[0:1, :], pltpu.roll(d, TM - 1, 0))
                n2 = jnp.where(row == TM - 2, nx[0:1, :], jnp.where(row == TM - 1, nx[1:2, :], pltpu.roll(d, TM - 2, 0)))
                nxt[:, cs] = d[0:8, :]
                dp = (d * cw_ref[2:3, cs] + n1 * cw_ref[1:2, cs] + n2 * cw_ref[0:1, cs]).astype(BF16)
                dpre_ref[:, cs] = dp
                p = up_ref[:, cs].astype(F32)
                dcv_ref[n, 0:1, ts] += jnp.sum(n2 * p, axis=0, keepdims=True)
                dcv_ref[n, 1:2, ts] += jnp.sum(n1 * p, axis=0, keepdims=True)
                dcv_ref[n, 2:3, ts] += jnp.sum(d * p, axis=0, keepdims=True)
                dcv_ref[n, 3:4, ts] += jnp.sum(d, axis=0, keepdims=True)
                dh2 = dh2 + _dot(dp, wu_ref[pl.ds(c0, tw), :])
        x2 = x2_ref[...]
        r2 = lax.rsqrt(jnp.mean(x2 * x2, axis=-1, keepdims=True) + EPS)
        xh2 = x2 * r2
        dg2_ref[0:1, :] += jnp.sum(dh2 * xh2, axis=0, keepdims=True)
        t2 = dh2 * g2_ref[...]
        dx2_ref[...] = dx3 + r2 * (t2 - xh2 * jnp.mean(t2 * xh2, axis=-1, keepdims=True))

    rev = lambda w: pl.BlockSpec((TM, w), lambda i: (N_TB - 1 - i, 0))
    acc = lambda s: pl.BlockSpec(s, lambda i: (0,) * len(s))
    return pl.pallas_call(
        body, name="bwd_ffn", grid=(N_TB,),
        out_shape=(jax.ShapeDtypeStruct((SEQ, D_MODEL), BF16), jax.ShapeDtypeStruct((SEQ, 2 * D_FF), BF16),
                   jax.ShapeDtypeStruct((SEQ, D_MODEL), F32), jax.ShapeDtypeStruct((8, D_MODEL), F32),
                   jax.ShapeDtypeStruct((8, D_MODEL), F32), jax.ShapeDtypeStruct((2, 8, D_FF), F32)),
        in_specs=[rev(D_MODEL), rev(D_MODEL), _resident((1, D_MODEL)), rev(D_MODEL), _resident((1, D_MODEL)), rev(2 * D_FF),
                  rev(2 * D_FF), _resident((2 * D_FF, D_MODEL)), _resident((8, 2 * D_FF)), _resident((D_FF, D_MODEL))],
        out_specs=(rev(D_MODEL), rev(2 * D_FF), rev(D_MODEL), acc((8, D_MODEL)), acc((8, D_MODEL)), acc((2, 8, D_FF))),
        scratch_shapes=[pltpu.VMEM((8, 2 * D_FF), F32)],
        compiler_params=_cparams(("arbitrary",)),
    )(x3, tgt, gf, x2, g2, up_pre, u_conv, wup_g, cw_g, wdn_g)


def _bwd_mix(dx2, proj, o, sprev, wout_g, grn, lng, lnb, ws, bsb, mask, qdec, kdec, cos2, sin2, hosted):
    cdec = _chunk_decay()
    geoms = [g for g, _ in hosted]
    n_h = len(hosted)

    def body(dx2_ref, p_ref, o_ref, sp_ref, w_ref, grn_ref, lng_ref, lnb_ref, ws_ref, bsb_ref, m_ref, qd_ref, kd_ref,
             cos_ref, sin_ref, *rest):
        dp_ref, dgrn_ref, dlng_ref, dlnb_ref, dws_ref, dbs_ref = rest[n_h:n_h + 6]
        dstate, dbs_acc = rest[2 * n_h + 6:2 * n_h + 8]
        i = pl.program_id(0)
        rs = _Scatters(geoms, rest[:n_h], rest[n_h + 6:2 * n_h + 6], rest[2 * n_h + 8:])
        pl.when(i == 0)(rs.phase1)
        pl.when(i == 3)(rs.phase2)
        pl.when(i == 6)(rs.phase2b)

        @pl.when(i == 0)
        def _():
            dstate[...] = jnp.zeros_like(dstate)
            dgrn_ref[...] = jnp.zeros_like(dgrn_ref)
            dlng_ref[...] = jnp.zeros_like(dlng_ref)
            dlnb_ref[...] = jnp.zeros_like(dlnb_ref)
            dws_ref[...] = jnp.zeros_like(dws_ref)
            dbs_ref[...] = jnp.zeros_like(dbs_ref)
            dbs_acc[...] = jnp.zeros_like(dbs_acc)

        dmix = _dot_nt(dx2_ref[...].astype(BF16), w_ref[...])
        for h in range(HEADS):
            sl = slice(h * HEAD_DIM, (h + 1) * HEAD_DIM)
            q = p_ref[:, sl]
            k = p_ref[:, RET_W + h * HEAD_DIM:RET_W + (h + 1) * HEAD_DIM]
            v = p_ref[:, 2 * RET_W + h * HEAD_DIM:2 * RET_W + (h + 1) * HEAD_DIM]
            g = p_ref[:, 3 * RET_W + h * HEAD_DIM:3 * RET_W + (h + 1) * HEAD_DIM]
            o = o_ref[:, sl]
            rinv = lax.rsqrt(jnp.mean(o * o, axis=-1, keepdims=True) + EPS)
            oh = o * rinv
            gr = grn_ref[:, sl]
            sg = _sigmoid(g)
            dret = dmix[:, sl]
            dp_ref[:, 3 * RET_W + h * HEAD_DIM:3 * RET_W + (h + 1) * HEAD_DIM] = (
                dret * (oh * gr) * (sg * (1.0 + g * (1.0 - sg)))).astype(BF16)
            drn = dret * (g * sg)
            dgrn_ref[0:1, sl] += jnp.sum(drn * oh, axis=0, keepdims=True)
            t = drn * gr
            do = rinv * (t - oh * jnp.mean(t * oh, axis=-1, keepdims=True))
            qb, kb, vb, dob = q.astype(BF16), k.astype(BF16), v.astype(BF16), do.astype(BF16)
            m = m_ref[h]
            ab = (_dot_nt(qb, kb) * m).astype(BF16)
            dab = (_dot_nt(dob, vb) * m).astype(BF16)
            spb = sp_ref[0, h]
            dsn = dstate[h]
            dsnb = dsn.astype(BF16)
            qdb = (q * qd_ref[h]).astype(BF16)
            kdb = (k * kd_ref[h]).astype(BF16)
            dq = _dot(dab, kb) + _dot_nt(dob, spb) * qd_ref[h]
            dk = _dot_tn(dab, qb) + _dot_nt(vb, dsnb) * kd_ref[h]
            dv = _dot_tn(ab, dob) + _dot(kdb, dsnb)
            dstate[h] = dsn * cdec[h] + _dot_tn(qdb, dob)
            c2, s2 = cos_ref[...], sin_ref[...]
            dp_ref[:, sl] = _rot_t(dq, c2, s2).astype(BF16)
            dp_ref[:, RET_W + h * HEAD_DIM:RET_W + (h + 1) * HEAD_DIM] = _rot_t(dk * K_SCALE, c2, s2).astype(BF16)
            dp_ref[:, 2 * RET_W + h * HEAD_DIM:2 * RET_W + (h + 1) * HEAD_DIM] = dv.astype(BF16)
        for gi in range(HEADS):
            sl = slice(gi * HEAD_DIM, (gi + 1) * HEAD_DIM)
            u = p_ref[:, 4 * RET_W + gi * HEAD_DIM:4 * RET_W + (gi + 1) * HEAD_DIM]
            sv = p_ref[:, 4 * RET_W + SGU_W + gi * HEAD_DIM:4 * RET_W + SGU_W + (gi + 1) * HEAD_DIM]
            gv = _gelu(sv)
            xc = gv - jnp.mean(gv, axis=-1, keepdims=True)
            rstd = lax.rsqrt(jnp.mean(xc * xc, axis=-1, keepdims=True) + EPS)
            xh = xc * rstd
            lg = lng_ref[:, sl]
            vnb = (xh * lg + lnb_ref[:, sl]).astype(BF16)
            wcb = _causal(ws_ref[gi]).astype(BF16)
            mixed = _dot(wcb, vnb) + bsb_ref[gi]
            dsgu = dmix[:, RET_W + gi * HEAD_DIM:RET_W + (gi + 1) * HEAD_DIM]
            dmixed = dsgu * _gelu(u)
            dmb = dmixed.astype(BF16)
            dws_ref[gi] += _causal(_dot_nt(dmb, vnb))
            dbs_acc[gi] += dmixed
            dvn = _dot_tn(wcb, dmb)
            dlng_ref[gi:gi + 1, :] += jnp.sum(dvn * xh, axis=0, keepdims=True)
            dlnb_ref[gi:gi + 1, :] += jnp.sum(dvn, axis=0, keepdims=True)
            dxh = dvn * lg
            dgv = rstd * (dxh - jnp.mean(dxh, axis=-1, keepdims=True) - xh * jnp.mean(dxh * xh, axis=-1, keepdims=True))
            dp_ref[:, 4 * RET_W + gi * HEAD_DIM:4 * RET_W + (gi + 1) * HEAD_DIM] = (dsgu * mixed * _gelu_grad(u)).astype(BF16)
            dp_ref[:, 4 * RET_W + SGU_W + gi * HEAD_DIM:4 * RET_W + SGU_W + (gi + 1) * HEAD_DIM] = (
                dgv * _gelu_grad(sv)).astype(BF16)

        @pl.when(i == N_CHUNK - 1)
        def _():
            for gi in range(HEADS):
                col = jnp.broadcast_to(jnp.sum(dbs_acc[gi], axis=-1, keepdims=True), (CHUNK, CHUNK))
                dbs_ref[gi:gi + 1, :] = jnp.transpose(col)[0:1, :]
            rs.phase3()

    rev = lambda w: pl.BlockSpec((CHUNK, w), lambda i: (N_CHUNK - 1 - i, 0))
    hcc = (HEADS, CHUNK, CHUNK)
    acc = lambda s: pl.BlockSpec(s, lambda i: (0,) * len(s))
    res = pl.pallas_call(
        body, name="bwd_mix", grid=(N_CHUNK,),
        out_shape=(jax.ShapeDtypeStruct((SEQ, PROJ_W), BF16), jax.ShapeDtypeStruct((8, RET_W), F32),
                   jax.ShapeDtypeStruct((8, HEAD_DIM), F32), jax.ShapeDtypeStruct((8, HEAD_DIM), F32),
                   jax.ShapeDtypeStruct(hcc, F32), jax.ShapeDtypeStruct((8, CHUNK), F32)) + _scatter_out_shapes(geoms),
        in_specs=[rev(D_MODEL), rev(PROJ_W), rev(RET_W),
                  pl.BlockSpec((1, HEADS, HEAD_DIM, HEAD_DIM), lambda i: (N_CHUNK - 1 - i, 0, 0, 0)),
                  _resident((D_MODEL, D_MODEL)), _resident((1, RET_W)), _resident((1, SGU_W)), _resident((1, SGU_W)),
                  _resident(hcc), _resident(hcc), _resident(hcc), _resident(hcc), _resident(hcc), rev(HEAD_DIM), rev(HEAD_DIM)]
        + [pl.BlockSpec(memory_space=pl.ANY)] * n_h,
        out_specs=(rev(PROJ_W), acc((8, RET_W)), acc((8, HEAD_DIM)), acc((8, HEAD_DIM)), acc(hcc), acc((8, CHUNK)))
        + _scatter_out_specs(geoms),
        scratch_shapes=[pltpu.VMEM((HEADS, HEAD_DIM, HEAD_DIM), F32), pltpu.VMEM((HEADS, CHUNK, CHUNK), F32)] + _scatter_scratch(geoms),
        compiler_params=_cparams(("arbitrary",)),
    )(dx2, proj, o, sprev, wout_g, grn, lng, lnb, ws, bsb, mask, qdec, kdec, cos2, sin2, *[p for _, p in hosted])
    return tuple(res[:6 + n_h])


def _bwd_proj(dproj, win_g, x, g1, dx2, gin_p, small):
    geoms = [W_IN]
    n_s = len(small)

    def body(dp_ref, w_ref, x_ref, g_ref, dx2_ref, gin_ref, *rest):
        small_refs = rest[:n_s]
        dx_ref, rs_out, rp_ref, rws_ref, rcv_ref, dg_ref = rest[n_s:n_s + 6]
        rs_scratch = rest[n_s + 6:n_s + 6 + N_SCATTER_SCRATCH]
        ar_scratch = rest[n_s + 6 + N_SCATTER_SCRATCH:]
        ar_res = ar_scratch[N_SMALL_SCRATCH:]
        ar = _SmallReduce((dg_ref,) + tuple(small_refs), ar_res, ar_scratch[:N_SMALL_SCRATCH])
        rs = _Scatters(geoms, [gin_ref], [rs_out], rs_scratch)
        pl.when(pl.program_id(0) == 0)(rs.phase1)
        pl.when(pl.program_id(0) == 2)(rs.phase2)
        pl.when(pl.program_id(0) == 4)(rs.phase2b)

        @pl.when(pl.program_id(0) == 0)
        def _():
            dg_ref[...] = jnp.zeros_like(dg_ref)

        dh = _dot_nt(dp_ref[...], w_ref[...])
        xb = x_ref[...]
        r = lax.rsqrt(jnp.mean(xb * xb, axis=-1, keepdims=True) + EPS)
        xh = xb * r
        dg_ref[0:1, :] += jnp.sum(dh * xh, axis=0, keepdims=True)
        t = dh * g_ref[...]
        dx_ref[...] = dx2_ref[...] + r * (t - xh * jnp.mean(t * xh, axis=-1, keepdims=True))

        @pl.when(pl.program_id(0) == N_TB - 1)
        def _():
            ar.begin()
            rs.phase3()
            ar.end()
            for o_ref, r_ref in zip((rp_ref, rws_ref, rcv_ref), ar_res):
                o_ref[...] = r_ref[...]

    tok = lambda w: pl.BlockSpec((TM, w), lambda i: (i, 0))
    vm = pl.BlockSpec(memory_space=pltpu.VMEM)
    res = pl.pallas_call(
        body, name="bwd_proj", grid=(N_TB,),
        out_shape=(jax.ShapeDtypeStruct((SEQ, D_MODEL), F32),) + _scatter_out_shapes(geoms)
        + tuple(jax.ShapeDtypeStruct(s, F32) for s in SMALL_FULL),
        in_specs=[tok(PROJ_W), _resident((D_MODEL, PROJ_W)), tok(D_MODEL), _resident((1, D_MODEL)), tok(D_MODEL),
                  pl.BlockSpec(memory_space=pl.ANY)] + [vm] * n_s,
        out_specs=(tok(D_MODEL),) + _scatter_out_specs(geoms) + (vm,) * len(SMALL_FULL),
        scratch_shapes=[pltpu.VMEM((8, D_MODEL), F32)] + _scatter_scratch(geoms) + _small_scratch()
        + [pltpu.VMEM(s, F32) for s in SMALL_FULL],
        compiler_params=_cparams(("arbitrary",)),
    )(dproj, win_g, x, g1, dx2, gin_p, *small)
    return res


def _wgrad(name, a, b, tm=None, tn=None, hosted=()):
    m_w, n_w = a.shape[-1], b.shape[-1]
    tm = m_w if tm is None else tm
    tn = n_w if tn is None else tn
    n_steps = (m_w // tm) * (n_w // tn)
    geoms = [g for g, _ in hosted]
    n_h = len(hosted)

    def body(a_ref, b_ref, *rest):
        o_ref = rest[n_h]
        if n_h:
            rs = _Scatters(geoms, rest[:n_h], rest[n_h + 1:2 * n_h + 1], rest[2 * n_h + 1:])
            step = pl.program_id(0) * (n_w // tn) + pl.program_id(1)
            pl.when(step == 0)(rs.phase1)
            pl.when(step == 1)(rs.phase2)
            pl.when(step == 2)(rs.phase2b)
        o_ref[...] = _dot_tn(a_ref[...].astype(BF16), b_ref[...].astype(BF16)).astype(BF16)
        if n_h:
            pl.when(step == n_steps - 1)(rs.phase3)

    assert not n_h or n_steps >= 4
    res = pl.pallas_call(
        body, name=name, grid=(m_w // tm, n_w // tn),
        out_shape=(jax.ShapeDtypeStruct((m_w, n_w), BF16),) + _scatter_out_shapes(geoms),
        in_specs=[pl.BlockSpec((SEQ, tm), lambda i, j: (0, i)), pl.BlockSpec((SEQ, tn), lambda i, j: (0, j))]
        + [pl.BlockSpec(memory_space=pl.ANY)] * n_h,
        out_specs=(pl.BlockSpec((tm, tn), lambda i, j: (i, j)),) + _scatter_out_specs(geoms),
        scratch_shapes=_scatter_scratch(geoms),
        compiler_params=_cparams(("arbitrary", "arbitrary") if n_h else ("parallel", "parallel")),
    )(a, b, *[p for _, p in hosted])
    return tuple(res[:1 + n_h])


def _row_step(half_rows):
    return max(s for s in range(16, 177, 16) if half_rows % s == 0)


class _Scatter:
    def __init__(self, geom, partial, out, land1, mine, stage2, land2, comb, s1_send, s1_recv, s2_send, s2_recv, ld_sems):
        self.w, self.row0, self.shape = _geom(geom)
        self.partial, self.out, self.land1 = partial, out, land1
        self.mine, self.stage2, self.land2, self.comb = mine, stage2, land2, comb
        self.hr = self.shape[0] // 2
        self.step = _row_step(self.hr)
        self.s1_send, self.s1_recv, self.s2_send, self.s2_recv, self.ld_sems = s1_send, s1_recv, s2_send, s2_recv, ld_sems
        self.x, self.y, self.c = lax.axis_index("x"), lax.axis_index("y"), lax.axis_index("c")
        self.sibling = (self.x, self.y, 1 - self.c)
        self.chips = [(self.x, self.y), (1 - self.x, self.y), (self.x, 1 - self.y), (1 - self.x, 1 - self.y)]

    def block(self, px, py, pc):
        dev = 4 * px + 2 * py + pc
        if self.w == W_IN:
            return self.partial.at[:, pl.ds(pl.multiple_of(dev * IN_SHARD, 128), IN_SHARD)]
        if self.w == W_OUT:
            return self.partial.at[pl.ds(pl.multiple_of(dev * OUT_SHARD, 128), OUT_SHARD), :]
        if self.w == W_DOWN:
            return self.partial.at[pl.ds(pl.multiple_of(dev * DOWN_SHARD, 32), DOWN_SHARD), :]
        return self.partial.at[pl.ds(pl.multiple_of(dev * FF_SHARD + self.row0, 32), self.shape[0]), :]

    def copy1(self, k):
        return pltpu.make_async_remote_copy(
            src_ref=self.block(*self.chips[k], 1 - self.c), dst_ref=self.land1.at[k],
            send_sem=self.s1_send.at[k], recv_sem=self.s1_recv.at[k], device_id=self.sibling, device_id_type=MESH)

    STAGE2 = [(1, 0, 1), (3, 0, 1), (2, 1, 2), (3, 1, 2), (1, 1, 1), (2, 0, 2)]

    def copy2(self, j):
        blk, h, to = self.STAGE2[j]
        src = self.comb.at[j - 4] if j >= 4 else self.stage2.at[blk - 1, pl.ds(h * self.hr, self.hr), :]
        return pltpu.make_async_remote_copy(
            src_ref=src, dst_ref=self.land2.at[j], send_sem=self.s2_send.at[j], recv_sem=self.s2_recv.at[j],
            device_id=(*self.chips[to], self.c), device_id_type=MESH)

    def _rows(self, h=None):
        step = self.step
        lo, n = (0, self.shape[0]) if h is None else (h * self.hr, self.hr)
        return [pl.ds(r0, step) for r0 in range(lo, lo + n, step)]

    def load(self, k):
        return pltpu.make_async_copy(self.block(*self.chips[k], self.c), self.mine.at[k], self.ld_sems.at[k])

    def phase1(self):
        for k in range(4):
            self.copy1(k).start()
        for k in range(4):
            self.load(k).start()

    def phase2(self):
        for k in (3, 1, 2, 0):
            self.copy1(k).wait_recv()
            self.load(k).wait()
            for rs in self._rows():
                s = self.mine[k, rs, :].astype(F32) + self.land1[k, rs, :].astype(F32)
                if k == 0:
                    self.out[rs, :] = s
                else:
                    self.stage2[k - 1, rs, :] = s.astype(BF16)
            for j in {3: (1, 3), 1: (0,), 2: (2,), 0: ()}[k]:
                self.copy2(j).start()

    def phase2b(self):
        for j, got in ((4, 3), (5, 1)):
            blk, h, _ = self.STAGE2[j]
            self.copy2(got).wait_recv()
            for i, rs in enumerate(self._rows(h)):
                lr = pl.ds(i * self.step, self.step)
                self.comb[j - 4, lr, :] = (self.stage2[blk - 1, rs, :].astype(F32) + self.land2[got, lr, :].astype(F32)).astype(BF16)
            self.copy2(j).start()

    def phase3(self):
        for j in (0, 5, 4, 2):
            self.copy2(j).wait_recv()
        for h, (first, second) in enumerate(((0, 5), (4, 2))):
            for i, rs in enumerate(self._rows(h)):
                lr = pl.ds(i * self.step, self.step)
                self.out[rs, :] = (self.out[rs, :] + self.land2[first, lr, :].astype(F32)) + self.land2[second, lr, :].astype(F32)
        for k in range(4):
            self.copy1(k).wait_send()
        for j in range(6):
            self.copy2(j).wait_send()


def _geom(geom):
    if isinstance(geom, tuple):
        w, row0, rows = geom
        assert w == W_UP
        return w, row0, (rows, SHARD[w][1])
    return geom, 0, SHARD[geom]


N_SCATTER_SCRATCH = 10


def _scatter_out_shapes(geoms):
    return tuple(jax.ShapeDtypeStruct(_geom(g)[2], F32) for g in geoms)


def _scatter_out_specs(geoms):
    return (pl.BlockSpec(memory_space=pltpu.VMEM),) * len(geoms)


def _scatter_scratch(geoms):
    out = []
    for g in geoms:
        s = _geom(g)[2]
        hs = (s[0] // 2, s[1])
        out += [pltpu.VMEM((4,) + s, BF16), pltpu.VMEM((4,) + s, BF16), pltpu.VMEM((3,) + s, BF16), pltpu.VMEM((6,) + hs, BF16),
                pltpu.VMEM((2,) + hs, BF16),
                pltpu.SemaphoreType.DMA((4,)), pltpu.SemaphoreType.DMA((4,)), pltpu.SemaphoreType.DMA((6,)),
                pltpu.SemaphoreType.DMA((6,)), pltpu.SemaphoreType.DMA((4,))]
    return out


class _Scatters:
    def __init__(self, geoms, p_refs, out_refs, scratch):
        k = N_SCATTER_SCRATCH
        self.items = [_Scatter(g, p_refs[i], out_refs[i], *scratch[k * i:k * i + k]) for i, g in enumerate(geoms)]

    def phase1(self):
        for s in self.items:
            s.phase1()

    def phase2(self):
        for s in self.items:
            s.phase2()

    def phase2b(self):
        for s in self.items:
            s.phase2b()

    def phase3(self):
        for s in self.items:
            s.phase3()


PACK_W = 1024


SMALL_FULL = [(2, 8, PACK_W), (HEADS, CHUNK, CHUNK), (2, 8, D_FF)]
SMALL_HALF = [(s[0] // 2,) + s[1:] for s in SMALL_FULL]
N_SMALL_SCRATCH = 16


def _small_scratch():
    n_a = len(SMALL_FULL)
    return ([pltpu.VMEM(SMALL_FULL[0], F32)] + [pltpu.VMEM(s, F32) for s in SMALL_HALF] + [pltpu.VMEM(s, F32) for s in SMALL_HALF]
            + [pltpu.VMEM((3,) + s, F32) for s in SMALL_HALF]
            + [pltpu.SemaphoreType.DMA((n_a,)), pltpu.SemaphoreType.DMA((n_a,)), pltpu.SemaphoreType.DMA((n_a, 3)),
               pltpu.SemaphoreType.DMA((n_a, 3)), pltpu.SemaphoreType.DMA((n_a,)), pltpu.SemaphoreType.DMA((n_a,))])


class _SmallReduce:
    def __init__(self, ins, outs, scratch):
        self.ins, self.outs = ins, outs
        (self.pack, *rest) = scratch
        self.rxs, self.css, self.gs = rest[0:3], rest[3:6], rest[6:9]
        self.s1_send, self.s1_recv, self.s2_send, self.s2_recv, self.s3_send, self.s3_recv = rest[9:]
        self.x, self.y, self.c = lax.axis_index("x"), lax.axis_index("y"), lax.axis_index("c")
        self.sibling = (self.x, self.y, 1 - self.c)
        self.chips = [(1 - self.x, self.y), (self.x, 1 - self.y), (1 - self.x, 1 - self.y)]
        self.hl = [s[0] for s in SMALL_HALF]

    def half(self, ref, a, h):
        return ref.at[pl.ds(h * self.hl[a], self.hl[a])]

    def begin(self):
        dg1_ref, dg2_ref, dgf_ref, dgrn_ref, dlng_ref, dlnb_ref, dbs_ref, loss_ref, dws_ref, dcv_ref = self.ins
        pack, c = self.pack, self.c
        pack[...] = jnp.zeros_like(pack)
        pack[0, 0:1, :] = dg1_ref[0:1, :]
        pack[0, 1:2, :] = dg2_ref[0:1, :]
        pack[0, 2:3, :] = dgf_ref[0:1, :]
        pack[0, 3:4, 0:RET_W] = dgrn_ref[0:1, :]
        lsum = loss_ref[0, 0:1, :]
        for i in range(1, N_TB):
            lsum = lsum + loss_ref[i, 0:1, :]
        pack[0, 3:4, RET_W:RET_W + 128] = lsum
        pack[1, 0:HEADS, 0:128] = dlng_ref[0:HEADS, :]
        pack[1, 0:HEADS, 128:256] = dlnb_ref[0:HEADS, :]
        pack[1, 0:HEADS, 256:384] = dbs_ref[0:HEADS, :]
        self.srcs = [pack, dws_ref, dcv_ref]
        n_a = len(self.srcs)
        self.ex1 = [pltpu.make_async_remote_copy(src_ref=self.half(self.srcs[a], a, 1 - c), dst_ref=self.rxs[a],
                                                 send_sem=self.s1_send.at[a], recv_sem=self.s1_recv.at[a],
                                                 device_id=self.sibling, device_id_type=MESH) for a in range(n_a)]
        for cp in self.ex1:
            cp.start()
        self.ex2 = []
        for a in range(n_a):
            self.ex1[a].wait_recv()
            self.css[a][...] = self.half(self.srcs[a], a, c)[...] + self.rxs[a][...]
            for j, chip in enumerate(self.chips):
                cp = pltpu.make_async_remote_copy(src_ref=self.css[a], dst_ref=self.gs[a].at[j], send_sem=self.s2_send.at[a, j],
                                                  recv_sem=self.s2_recv.at[a, j], device_id=(*chip, c), device_id_type=MESH)
                cp.start()
                self.ex2.append(cp)

    def end(self):
        c, x, y = self.c, self.x, self.y
        ex3 = []
        for a in range(len(self.srcs)):
            css, gs, out = self.css[a], self.gs[a], self.outs[a]
            for j in range(3):
                self.ex2[3 * a + j].wait_recv()
            tot = None
            for q in range(4):
                k = jnp.where(x != (q >> 1), 1, 0) + jnp.where(y != (q & 1), 2, 0)
                term = jnp.where(k == 0, css[...], jnp.where(k == 1, gs[0], jnp.where(k == 2, gs[1], gs[2])))
                tot = term if tot is None else tot + term
            self.half(out, a, c)[...] = tot
            cp = pltpu.make_async_remote_copy(src_ref=self.half(out, a, c), dst_ref=self.half(out, a, c), send_sem=self.s3_send.at[a],
                                              recv_sem=self.s3_recv.at[a], device_id=self.sibling, device_id_type=MESH)
            cp.start()
            ex3.append(cp)
        for a in range(len(self.srcs)):
            out = self.outs[a]
            pltpu.make_async_remote_copy(src_ref=self.half(out, a, 1 - c), dst_ref=self.half(out, a, 1 - c), send_sem=self.s3_send.at[a],
                                         recv_sem=self.s3_recv.at[a], device_id=self.sibling, device_id_type=MESH).wait_recv()
        for cp in self.ex1 + self.ex2 + ex3:
            cp.wait_send()


def _adam_math(w, g, m, v):
    nm = ADAM_B1 * m + (1.0 - ADAM_B1) * g
    nv = ADAM_B2 * v + (1.0 - ADAM_B2) * (g * g)
    d = -ADAM_LR * ((nm / (1.0 - ADAM_B1 ** ADAM_STEP)) / (jnp.sqrt(nv / (1.0 - ADAM_B2 ** ADAM_STEP)) + ADAM_EPS) + ADAM_WD * w)
    return d, nm, nv


def _adamw(name, w, gs, m, v, rows):
    _, r, cdim = w.shape
    n_steps = r // rows
    half = gs[0].shape[0] // rows

    def body(w_ref, *rest):
        g_refs, (m_ref, v_ref, go_ref, d_ref, nm_ref, nv_ref) = rest[:len(gs)], rest[len(gs):]
        gg = g_refs[0][...]
        if len(gs) == 2:
            gg = jnp.where(pl.program_id(0) < half, gg, g_refs[1][...])
        go_ref[0] = gg
        d, nm, nv = _adam_math(w_ref[0], gg, m_ref[0], v_ref[0])
        d_ref[0], nm_ref[0], nv_ref[0] = d, nm, nv

    spec3 = pl.BlockSpec((1, rows, cdim), lambda i: (0, i, 0))
    if len(gs) == 1:
        g_specs = [pl.BlockSpec((rows, cdim), lambda i: (i, 0))]
    else:
        g_specs = [pl.BlockSpec((rows, cdim), lambda i: (jnp.minimum(i, half - 1), 0)),
                   pl.BlockSpec((rows, cdim), lambda i: (jnp.maximum(i - half, 0), 0))]
    sh = jax.ShapeDtypeStruct((1, r, cdim), F32)
    return pl.pallas_call(
        body, name=name, grid=(n_steps,), out_shape=(sh, sh, sh, sh),
        in_specs=[spec3] + g_specs + [spec3, spec3], out_specs=(spec3,) * 4,
        compiler_params=_cparams(("parallel",)),
    )(w, *gs, m, v)


def _adamw_small(rp, rws, rcv, gcw, params):
    n_p = len(params)

    def body(*refs):
        rp_ref, rws_ref, rcv_ref, gcw_ref = refs[:4]
        ins = refs[4:4 + 3 * n_p]
        outs = refs[4 + 3 * n_p:]
        outs[4 * n_p][...] = rp_ref[0, 3:4, RET_W:RET_W + 1]
        grads = [rp_ref[0, 0:1, :], rp_ref[0, 1:2, :], rp_ref[0, 2:3, :], rp_ref[0, 3:4, 0:RET_W],
                 rp_ref[1, 0:HEADS, 0:128], rp_ref[1, 0:HEADS, 128:256], rp_ref[1, 0:HEADS, 256:384],
                 rws_ref[...], gcw_ref[...], None]
        for p in range(n_p):
            w_ref, m_ref, v_ref = ins[3 * p:3 * p + 3]
            o = outs[4 * p:4 * p + 4]
            if p == n_p - 1:
                for hf in range(2):
                    cs = slice(hf * D_FF, (hf + 1) * D_FF)
                    g = rcv_ref[hf, 3:4, :]
                    res = (g,) + _adam_math(w_ref[:, cs], g, m_ref[:, cs], v_ref[:, cs])
                    for t in range(4):
                        o[t][:, cs] = res[t]
                continue
            lead = w_ref.ndim > grads[p].ndim
            rd = (lambda r: r[0]) if lead else (lambda r: r[...])
            res = (grads[p],) + _adam_math(rd(w_ref), grads[p], rd(m_ref), rd(v_ref))
            for t in range(4):
                if lead:
                    o[t][0] = res[t]
                else:
                    o[t][...] = res[t]

    vm = pl.BlockSpec(memory_space=pltpu.VMEM)
    flat = [a for tr in params for a in tr]
    out_shape = tuple(jax.ShapeDtypeStruct(tr[0].shape, F32) for tr in params for _ in range(4)) + (jax.ShapeDtypeStruct((1, 1), F32),)
    res = pl.pallas_call(
        body, name="adamw_small", out_shape=out_shape, in_specs=[vm] * (4 + len(flat)), out_specs=(vm,) * len(out_shape),
        compiler_params=_cparams(),
    )(rp, rws, rcv, gcw, *flat)
    return [res[4 * p:4 * p + 4] for p in range(n_p)], res[4 * n_p]


def kernel(x, mix_norm_g, w_in, ret_norm_g, sgu_ln_g, sgu_ln_b, sgu_w_s, sgu_b_s, w_out, ffn_norm_g, w_up, conv_w, conv_b, w_down, final_norm_g, loss_target, m_mix_norm_g, m_w_in, m_ret_norm_g, m_sgu_ln_g, m_sgu_ln_b, m_sgu_w_s, m_sgu_b_s, m_w_out, m_ffn_norm_g, m_w_up, m_conv_w, m_conv_b, m_w_down, m_final_norm_g, v_mix_norm_g, v_w_in, v_ret_norm_g, v_sgu_ln_g, v_sgu_ln_b, v_sgu_w_s, v_sgu_b_s, v_w_out, v_ffn_norm_g, v_w_up, v_conv_w, v_conv_b, v_w_down, v_final_norm_g):
    xs = x[0]
    tgt = loss_target[0]
    mask, qdec, kdec = _decay_tables()
    grn = ret_norm_g.reshape(1, RET_W)
    lng = sgu_ln_g.reshape(1, SGU_W)
    lnb = sgu_ln_b.reshape(1, SGU_W)
    ws = sgu_w_s[0]
    bsb = jnp.broadcast_to(sgu_b_s[0][:, :, None], (HEADS, CHUNK, HEAD_DIM))
    gf = final_norm_g.reshape(1, D_MODEL)
    me = 4 * lax.axis_index("x") + 2 * lax.axis_index("y") + lax.axis_index("c")
    tr = lambda a: jnp.transpose(a[0])[None]
    tr_cw = lambda a: jnp.transpose(a, (1, 0, 2))

    proj, h1, cos2, sin2, win_g, cw_sh, wout_g, wdn_g, su = _fwd_proj(
        xs, mix_norm_g, _rope_freq(), w_in[0], w_out[0], tr(w_up)[0], w_down[0], tr_cw(conv_w))
    cw_g = jnp.transpose(cw_sh, (1, 0, 2)).reshape(8, 2 * D_FF)
    x2, mixcat, o, sprev, wup_g = _fwd_mix(xs, proj, wout_g, grn, lng, lnb, ws, bsb, mask, qdec, kdec, su)
    h2, up_pre, u_conv, act, x3, loss_parts = _fwd_ffn(x2, ffn_norm_g, wup_g, cw_g, conv_b, wdn_g, gf, tgt)

    dx3, dpre, dx2, dgf, dg2, dcv = _bwd_ffn(x3, tgt, gf, x2, ffn_norm_g, up_pre, u_conv, wup_g, cw_g, wdn_g)
    band = 512
    (gdn_p,) = _wgrad("wgrad_down", act, dx3, tm=FF_TILE)
    (gout_p,) = _wgrad("wgrad_out", mixcat, dx2, tn=512)
    gup_p, g_dn = _wgrad("wgrad_up", dpre, h2, tm=FF_TILE, hosted=[(W_DOWN, gdn_p)])
    dproj, dgrn, dlng, dlnb, dws, dbs, g_up_a = _bwd_mix(
        dx2, proj, o, sprev, wout_g, grn, lng, lnb, ws, bsb, mask, qdec, kdec, cos2, sin2, [((W_UP, 0, band), gup_p)])
    gin_p, g_up_b, g_out = _wgrad("wgrad_in", h1, dproj, tn=768,
                                  hosted=[((W_UP, band, FF_SHARD - band), gup_p), (W_OUT, gout_p)])
    grad_x, g_in, rp, rws, rcv = _bwd_proj(dproj, win_g, xs, mix_norm_g, dx2, gin_p,
                                           (dg2, dgf, dgrn, dlng, dlnb, dbs, loss_parts, dws, dcv))
    gcw = tr_cw(lax.dynamic_slice(rcv, (me // (N_DEV // 2), 0, (me % (N_DEV // 2)) * FF_SHARD), (1, 3, FF_SHARD)))

    table = {}
    for name, w, gs, m, v, rows in (("w_in", w_in, [g_in], m_w_in, v_w_in, 256), ("w_out", w_out, [g_out], m_w_out, v_w_out, 128),
                                    ("w_up", tr(w_up), [g_up_a, g_up_b], tr(m_w_up), tr(v_w_up), 64),
                                    ("w_down", w_down, [g_dn], m_w_down, v_w_down, 88)):
        table[name] = _adamw("adamw_" + name, w, gs, m, v, rows)
    table["w_up"] = tuple(tr(a) for a in table["w_up"])
    row = lambda a: a.reshape(1, D_MODEL)
    names_small = ["mix_norm_g", "ffn_norm_g", "final_norm_g", "ret_norm_g", "sgu_ln_g", "sgu_ln_b", "sgu_b_s", "sgu_w_s",
                   "conv_w", "conv_b"]
    params = [(mix_norm_g, m_mix_norm_g, v_mix_norm_g), (ffn_norm_g, m_ffn_norm_g, v_ffn_norm_g),
              (row(final_norm_g), row(m_final_norm_g), row(v_final_norm_g)), (ret_norm_g, m_ret_norm_g, v_ret_norm_g),
              (sgu_ln_g, m_sgu_ln_g, v_sgu_ln_g), (sgu_ln_b, m_sgu_ln_b, v_sgu_ln_b), (sgu_b_s, m_sgu_b_s, v_sgu_b_s),
              (sgu_w_s, m_sgu_w_s, v_sgu_w_s), (tr_cw(conv_w), tr_cw(m_conv_w), tr_cw(v_conv_w)), (conv_b, m_conv_b, v_conv_b)]
    small, loss = _adamw_small(rp, rws, rcv, gcw, params)
    for n, res in zip(names_small, small):
        table[n] = res
    table["final_norm_g"] = tuple(a.reshape(D_MODEL) for a in table["final_norm_g"])
    table["conv_w"] = tuple(tr_cw(a) for a in table["conv_w"])

    order = ["mix_norm_g", "w_in", "ret_norm_g", "sgu_ln_g", "sgu_ln_b", "sgu_w_s", "sgu_b_s", "w_out", "ffn_norm_g", "w_up",
             "conv_w", "conv_b", "w_down", "final_norm_g"]
    outs = [loss.reshape(()), grad_x[None]]
    for col in range(4):
        outs += [table[n][col] for n in order]
    return tuple(outs)
```

```python
import functools
import math

import jax
import jax.numpy as jnp
import numpy as np
from jax import lax
from jax.experimental import pallas as pl
from jax.experimental.pallas import tpu as pltpu

F32 = jnp.float32
BF16 = jnp.bfloat16
MESH = pl.DeviceIdType.MESH

N_DEV = 8
SEQ = 2048
D_MODEL = 1024
CHUNK = 128
N_CHUNK = SEQ // CHUNK
HEADS = 4
HEAD_DIM = 128
RET_W = 512
SGU_W = 512
PROJ_W = 3072
D_FF = 2816
FF_SHARD = 704
FF_TILE = 1408
FF_TILES = ((0, 1536), (1536, 1280))
IN_SHARD = PROJ_W // N_DEV
OUT_SHARD = D_MODEL // N_DEV
DOWN_SHARD = D_FF // N_DEV
TM = 256
N_TB = SEQ // TM
FWD_PROJ_PASS_AT = 5
FWD_MIX_PASS_AT = 10
EPS = 1e-6
ROPE_BASE = 10000.0
K_SCALE = HEAD_DIM ** -0.5
INV_SQRT2 = 0.7071067811865476
INV_SQRT_2PI = 0.3989422804014327

ADAM_LR = 0.001
ADAM_B1 = 0.9
ADAM_B2 = 0.999
ADAM_EPS = 1e-08
ADAM_WD = 0.01
ADAM_STEP = 10

VMEM_LIMIT = 56 * 1024 * 1024


def _cparams(sem=None, vmem=VMEM_LIMIT, collective=None):
    return pltpu.CompilerParams(dimension_semantics=sem, vmem_limit_bytes=vmem, collective_id=collective)


COLLECTIVE = {name: k for k, name in enumerate(("fwd_proj", "fwd_mix", "wgrad_up", "bwd_mix", "wgrad_in", "bwd_proj"))}


class _Meet:
    def __init__(self, diagonal):
        x, y, c = lax.axis_index("x"), lax.axis_index("y"), lax.axis_index("c")
        self.peers = [(x, y, 1 - c), (1 - x, y, c), (x, 1 - y, c)] + ([(1 - x, 1 - y, c)] if diagonal else [])

    def signal(self):
        for peer in self.peers:
            pl.semaphore_signal(pltpu.get_barrier_semaphore(), inc=1, device_id=peer, device_id_type=MESH)

    def wait(self):
        pl.semaphore_wait(pltpu.get_barrier_semaphore(), len(self.peers))


def _resident(shape):
    nd = len(shape)
    return pl.BlockSpec(shape, lambda *_: (0,) * nd, pipeline_mode=pl.Buffered(1))


def _dot(a, b):
    return jnp.dot(a, b, preferred_element_type=F32)


def _dot_nt(a, b):
    return lax.dot_general(a, b, (((1,), (1,)), ((), ())), preferred_element_type=F32)


def _dot_tn(a, b):
    return lax.dot_general(a, b, (((0,), (0,)), ((), ())), preferred_element_type=F32)


def _sigmoid(x):
    return 1.0 / (1.0 + jnp.exp(-x))


def _gelu(x):
    return 0.5 * x * (1.0 + lax.erf(x * INV_SQRT2))


def _gelu_grad(x):
    return 0.5 * (1.0 + lax.erf(x * INV_SQRT2)) + x * (jnp.exp(-0.5 * x * x) * INV_SQRT_2PI)


def _rot(xh, cos2, sin2):
    return xh * cos2 + pltpu.roll(xh, HEAD_DIM // 2, 1) * sin2


def _rot_t(dh, cos2, sin2):
    return dh * cos2 + pltpu.roll(dh * sin2, HEAD_DIM // 2, 1)


def _rope_freq():
    half = HEAD_DIM // 2
    inv_freq = jnp.power(ROPE_BASE, -jnp.arange(half, dtype=F32) / half)
    return jnp.concatenate([inv_freq, inv_freq])[None, :]


def _rope_block(inv2, first_row):
    pos = (lax.broadcasted_iota(jnp.int32, (TM, HEAD_DIM), 0) + first_row).astype(F32)
    ang = pos * inv2
    sin = jnp.sin(ang)
    lane = lax.broadcasted_iota(jnp.int32, (TM, HEAD_DIM), 1)
    return jnp.cos(ang), jnp.where(lane < HEAD_DIM // 2, -sin, sin)


def _decay_tables():
    log_gamma = jnp.log(1.0 - jnp.power(2.0, -5.0 - jnp.arange(HEADS, dtype=F32)))
    pos = jnp.arange(CHUNK, dtype=F32)
    diff = pos[:, None] - pos[None, :]
    mask = jnp.where(diff >= 0.0, jnp.exp(log_gamma[:, None, None] * jnp.maximum(diff, 0.0)[None]), 0.0)
    k_decay = jnp.exp(log_gamma[:, None] * (CHUNK - 1.0 - pos)[None])
    q_decay = jnp.exp(log_gamma[:, None] * (pos + 1.0)[None])
    kd = jnp.broadcast_to(k_decay[:, :, None], (HEADS, CHUNK, HEAD_DIM))
    qd = jnp.broadcast_to(q_decay[:, :, None], (HEADS, CHUNK, HEAD_DIM))
    return mask.astype(F32), qd.astype(F32), kd.astype(F32)


def _chunk_decay():
    lg = np.log(np.float32(1.0) - np.power(np.float32(2.0), -5.0 - np.arange(HEADS, dtype=np.float32))).astype(np.float32)
    return [float(np.exp(lg[h] * np.float32(CHUNK))) for h in range(HEADS)]


W_IN, W_OUT, W_UP, W_DOWN, W_CONV = range(5)
GATHERED = {W_IN: ((D_MODEL, PROJ_W), BF16), W_OUT: ((D_MODEL, D_MODEL), BF16), W_UP: ((2 * D_FF, D_MODEL), BF16),
            W_DOWN: ((D_FF, D_MODEL), BF16), W_CONV: ((N_DEV, 8, FF_SHARD), F32)}
SHARD = {W_IN: (D_MODEL, IN_SHARD), W_OUT: (OUT_SHARD, D_MODEL), W_UP: (FF_SHARD, D_MODEL), W_DOWN: (DOWN_SHARD, D_MODEL),
         W_CONV: (8, FF_SHARD)}


class _Gather:
    N_SEMS = 9

    def __init__(self, ids, stages, gathered, send_sems, recv_sems, local_sems):
        self.ids, self.stages, self.gathered = ids, stages, gathered
        self.send_sems, self.recv_sems, self.local_sems = send_sems, recv_sems, local_sems
        self.x, self.y, self.c = lax.axis_index("x"), lax.axis_index("y"), lax.axis_index("c")
        self.me = (self.x, self.y, self.c)
        self.sibling = (self.x, self.y, 1 - self.c)
        self.chips = [(1 - self.x, self.y), (self.x, 1 - self.y), (1 - self.x, 1 - self.y)]

    def slot(self, n, px, py, pc):
        dev = 4 * px + 2 * py + pc
        w, g = self.ids[n], self.gathered[n]
        if w == W_IN:
            return g.at[:, pl.ds(pl.multiple_of(dev * IN_SHARD, 128), IN_SHARD)]
        if w == W_OUT:
            return g.at[pl.ds(pl.multiple_of(dev * OUT_SHARD, 128), OUT_SHARD), :]
        if w == W_DOWN:
            return g.at[pl.ds(pl.multiple_of(dev * DOWN_SHARD, 32), DOWN_SHARD), :]
        if w == W_UP:
            return g.at[pl.ds(pl.multiple_of(dev * FF_SHARD, 32), FF_SHARD), :]
        return g.at[dev]

    def half(self, n, px, py, pc, h):
        dev = 4 * px + 2 * py + pc
        w, g = self.ids[n], self.gathered[n]
        if w == W_IN:
            return g.at[pl.ds(h * (D_MODEL // 2), D_MODEL // 2), pl.ds(pl.multiple_of(dev * IN_SHARD, 128), IN_SHARD)]
        rows = SHARD[w][0] // 2
        return g.at[pl.ds(pl.multiple_of(dev * SHARD[w][0] + h * rows, 16), rows), :]

    def tree(self, n):
        return self.ids[n] != W_CONV

    def copy(self, n, k, block, to, src=None, h=None):
        ref = self.slot(n, *block) if h is None else self.half(n, *block, h)
        return pltpu.make_async_remote_copy(
            src_ref=ref if src is None else src, dst_ref=ref,
            send_sem=self.send_sems.at[n, k], recv_sem=self.recv_sems.at[n, k], device_id=to, device_id_type=MESH)

    def _mine(self):
        return [pltpu.make_async_copy(self.stages[n], self.slot(n, *self.me), self.local_sems.at[n]) for n in range(len(self.ids))]

    def _first(self):
        out = []
        for n in range(len(self.ids)):
            out.append(self.copy(n, 0, self.me, self.sibling, src=self.stages[n]))
            out += [self.copy(n, 1 + j, self.me, (*chip, self.c), src=self.stages[n])
                    for j, chip in enumerate(self.chips[:2] if self.tree(n) else self.chips)]
        return out

    def start(self):
        for cp in self._mine() + self._first():
            cp.start()

    def _passed(self, j):
        dev = (*self.chips[j], self.c)
        out = []
        for n in range(len(self.ids)):
            if not self.tree(n):
                out.append(self.copy(n, 4 + j, dev, self.sibling))
            elif j < 2:
                out += [self.copy(n, 3 + j, dev, (*self.chips[1 - j], self.c), h=j), self.copy(n, 5 + j, dev, self.sibling)]
            else:
                out += [self.copy(n, 7, dev, self.sibling, h=0), self.copy(n, 8, dev, self.sibling, h=1)]
        return out

    def near(self):
        for j in range(2):
            dev = (*self.chips[j], self.c)
            for n in range(len(self.ids)):
                self.copy(n, 1 + j, dev, self.me).wait_recv()
            for cp in self._passed(j):
                cp.start()

    def finish(self):
        dev = (*self.chips[2], self.c)
        for n in range(len(self.ids)):
            if self.tree(n):
                self.copy(n, 3, dev, self.me, h=0).wait_recv()
                self.copy(n, 4, dev, self.me, h=1).wait_recv()
            else:
                self.copy(n, 3, dev, self.me).wait_recv()
        for cp in self._passed(2):
            cp.start()
        for n in range(len(self.ids)):
            self.copy(n, 0, self.sibling, self.me).wait_recv()
            for j, chip in enumerate(self.chips):
                dev = (*chip, 1 - self.c)
                if not self.tree(n):
                    self.copy(n, 4 + j, dev, self.me).wait_recv()
                elif j < 2:
                    self.copy(n, 5 + j, dev, self.me).wait_recv()
                else:
                    self.copy(n, 7, dev, self.me, h=0).wait_recv()
                    self.copy(n, 8, dev, self.me, h=1).wait_recv()
        for cp in self._mine():
            cp.wait()
        for cp in self._first() + self._passed(0) + self._passed(1) + self._passed(2):
            cp.wait_send()


def _gather_scratch(n):
    return [pltpu.SemaphoreType.DMA((n, _Gather.N_SEMS)), pltpu.SemaphoreType.DMA((n, _Gather.N_SEMS)), pltpu.SemaphoreType.DMA((n,))]


def _gathered_shapes(ids):
    return tuple(jax.ShapeDtypeStruct(*GATHERED[w]) for w in ids)


def _fwd_proj(x, g1, inv2, w_in, w_out, w_up, w_down, conv_w):
    ids_a, ids_b = [W_IN, W_CONV], [W_OUT, W_DOWN]

    def body(x_ref, g_ref, inv_ref, in_hbm, out_hbm, up_hbm, dn_hbm, cw_ref,
             proj_ref, h1_ref, cos_ref, sin_ref, gin, gcw, gout, gdn, su_ref,
             w_vm, s_in, s_cw, s_out, s_dn, f_in, f_out, f_up, f_dn, ld_sems,
             a_send, a_recv, a_local, b_send, b_recv, b_local):
        ag_a = _Gather(ids_a, [s_in, s_cw], [gin, gcw], a_send, a_recv, a_local)
        ag_b = _Gather(ids_b, [s_out, s_dn], [gout, gdn], b_send, b_recv, b_local)

        @pl.when(pl.program_id(0) == 0)
        def _():
            meet = _Meet(diagonal=True)
            meet.signal()
            loads = [pltpu.make_async_copy(src, dst, ld_sems.at[i])
                     for i, (src, dst) in enumerate(((in_hbm, f_in), (out_hbm, f_out), (dn_hbm, f_dn), (up_hbm, f_up)))]
            for cp in loads:
                cp.start()
            s_cw[...] = jnp.zeros_like(s_cw)
            for k in range(3):
                s_cw[k:k + 1, :] = cw_ref[k]
            loads[0].wait()
            s_in[...] = f_in[...].astype(BF16)
            meet.wait()
            ag_a.start()
            loads[1].wait()
            s_out[...] = f_out[...].astype(BF16)
            loads[2].wait()
            s_dn[...] = f_dn[...].astype(BF16)
            ag_a.near()
            ag_b.start()
            loads[3].wait()
            su_ref[...] = f_up[...].astype(BF16)
            ag_a.finish()
            fill = pltpu.make_async_copy(gin, w_vm, ld_sems.at[4])
            fill.start()
            fill.wait()

        pl.when(pl.program_id(0) == FWD_PROJ_PASS_AT)(ag_b.near)

        xb = x_ref[...]
        r = lax.rsqrt(jnp.mean(xb * xb, axis=-1, keepdims=True) + EPS)
        h = ((xb * r) * g_ref[...]).astype(BF16)
        h1_ref[...] = h
        p = _dot(h, w_vm[...])
        c2, s2 = _rope_block(inv_ref[...], pl.program_id(0) * TM)
        cos_ref[...], sin_ref[...] = c2, s2
        for hd in range(HEADS):
            sl = slice(hd * HEAD_DIM, (hd + 1) * HEAD_DIM)
            proj_ref[:, sl] = _rot(p[:, sl], c2, s2)
            ks = slice(RET_W + hd * HEAD_DIM, RET_W + (hd + 1) * HEAD_DIM)
            proj_ref[:, ks] = _rot(p[:, ks], c2, s2) * K_SCALE
        proj_ref[:, 2 * RET_W:] = p[:, 2 * RET_W:]

        pl.when(pl.program_id(0) == N_TB - 1)(ag_b.finish)

    tok = lambda w: pl.BlockSpec((TM, w), lambda i: (i, 0))
    hbm = pl.BlockSpec(memory_space=pl.ANY)
    vm = pl.BlockSpec(memory_space=pltpu.VMEM)
    return pl.pallas_call(
        body, name="fwd_proj", grid=(N_TB,),
        out_shape=(jax.ShapeDtypeStruct((SEQ, PROJ_W), F32), jax.ShapeDtypeStruct((SEQ, D_MODEL), BF16),
                   jax.ShapeDtypeStruct((SEQ, HEAD_DIM), F32), jax.ShapeDtypeStruct((SEQ, HEAD_DIM), F32))
        + _gathered_shapes(ids_a + ids_b) + (jax.ShapeDtypeStruct(SHARD[W_UP], BF16),),
        in_specs=[tok(D_MODEL), _resident((1, D_MODEL)), _resident((1, HEAD_DIM)), hbm, hbm, hbm, hbm, vm],
        out_specs=(tok(PROJ_W), tok(D_MODEL), tok(HEAD_DIM), tok(HEAD_DIM), hbm, hbm, hbm, hbm, vm),
        scratch_shapes=[pltpu.VMEM((D_MODEL, PROJ_W), BF16), pltpu.VMEM(SHARD[W_IN], BF16), pltpu.VMEM(SHARD[W_CONV], F32),
                        pltpu.VMEM(SHARD[W_OUT], BF16), pltpu.VMEM(SHARD[W_DOWN], BF16),
                        pltpu.VMEM(SHARD[W_IN], F32), pltpu.VMEM(SHARD[W_OUT], F32), pltpu.VMEM(SHARD[W_UP], F32),
                        pltpu.VMEM(SHARD[W_DOWN], F32), pltpu.SemaphoreType.DMA((5,))]
        + _gather_scratch(len(ids_a)) + _gather_scratch(len(ids_b)),
        compiler_params=_cparams(("arbitrary",), collective=COLLECTIVE["fwd_proj"]),
    )(x, g1, inv2, w_in, w_out, w_up, w_down, conv_w)


def _causal(w):
    r = lax.broadcasted_iota(jnp.int32, (CHUNK, CHUNK), 0)
    c = lax.broadcasted_iota(jnp.int32, (CHUNK, CHUNK), 1)
    return jnp.where(r >= c, w, 0.0)


def _fwd_mix(x, proj, wout_g, grn, lng, lnb, ws, bsb, mask, qdec, kdec, su):
    cdec = _chunk_decay()
    ids = [W_UP]

    def body(x_ref, p_ref, w_ref, grn_ref, lng_ref, lnb_ref, ws_ref, bsb_ref, m_ref, qd_ref, kd_ref, su_ref,
             x2_ref, cat_ref, o_ref, sp_ref, gup, state, send_sems, recv_sems, local_sems):
        ag = _Gather(ids, [su_ref], [gup], send_sems, recv_sems, local_sems)

        @pl.when(pl.program_id(0) == 0)
        def _():
            meet = _Meet(diagonal=False)
            meet.signal()
            state[...] = jnp.zeros_like(state)
            meet.wait()
            ag.start()

        for h in range(HEADS):
            sl = slice(h * HEAD_DIM, (h + 1) * HEAD_DIM)
            q = p_ref[:, sl]
            k = p_ref[:, RET_W + h * HEAD_DIM:RET_W + (h + 1) * HEAD_DIM]
            v = p_ref[:, 2 * RET_W + h * HEAD_DIM:2 * RET_W + (h + 1) * HEAD_DIM]
            g = p_ref[:, 3 * RET_W + h * HEAD_DIM:3 * RET_W + (h + 1) * HEAD_DIM]
            qb, kb, vb = q.astype(BF16), k.astype(BF16), v.astype(BF16)
            a = _dot_nt(qb, kb) * m_ref[h]
            spb = state[h].astype(BF16)
            sp_ref[0, h] = spb
            o = _dot(a.astype(BF16), vb) + _dot((q * qd_ref[h]).astype(BF16), spb)
            state[h] = state[h] * cdec[h] + _dot_tn((k * kd_ref[h]).astype(BF16), vb)
            o_ref[:, sl] = o
            rinv = lax.rsqrt(jnp.mean(o * o, axis=-1, keepdims=True) + EPS)
            rn = (o * rinv) * grn_ref[:, sl]
            cat_ref[:, sl] = ((g * _sigmoid(g)) * rn).astype(BF16)
        for gi in range(HEADS):
            sl = slice(gi * HEAD_DIM, (gi + 1) * HEAD_DIM)
            u = p_ref[:, 4 * RET_W + gi * HEAD_DIM:4 * RET_W + (gi + 1) * HEAD_DIM]
            sv = p_ref[:, 4 * RET_W + SGU_W + gi * HEAD_DIM:4 * RET_W + SGU_W + (gi + 1) * HEAD_DIM]
            gv = _gelu(sv)
            xc = gv - jnp.mean(gv, axis=-1, keepdims=True)
            vn = (xc * lax.rsqrt(jnp.mean(xc * xc, axis=-1, keepdims=True) + EPS)) * lng_ref[:, sl] + lnb_ref[:, sl]
            mixed = _dot(_causal(ws_ref[gi]).astype(BF16), vn.astype(BF16)) + bsb_ref[gi]
            cat_ref[:, RET_W + gi * HEAD_DIM:RET_W + (gi + 1) * HEAD_DIM] = (_gelu(u) * mixed).astype(BF16)
        x2_ref[...] = x_ref[...] + _dot(cat_ref[...], w_ref[...])

        pl.when(pl.program_id(0) == FWD_MIX_PASS_AT)(ag.near)
        pl.when(pl.program_id(0) == N_CHUNK - 1)(ag.finish)

    ch = lambda w: pl.BlockSpec((CHUNK, w), lambda i: (i, 0))
    hcc = (HEADS, CHUNK, CHUNK)
    hbm = pl.BlockSpec(memory_space=pl.ANY)
    return pl.pallas_call(
        body, name="fwd_mix", grid=(N_CHUNK,),
        out_shape=(jax.ShapeDtypeStruct((SEQ, D_MODEL), F32), jax.ShapeDtypeStruct((SEQ, D_MODEL), BF16),
                   jax.ShapeDtypeStruct((SEQ, RET_W), F32), jax.ShapeDtypeStruct((N_CHUNK, HEADS, HEAD_DIM, HEAD_DIM), BF16))
        + _gathered_shapes(ids),
        in_specs=[ch(D_MODEL), ch(PROJ_W), _resident((D_MODEL, D_MODEL)), _resident((1, RET_W)), _resident((1, SGU_W)),
                  _resident((1, SGU_W)), _resident(hcc), _resident(hcc), _resident(hcc), _resident(hcc), _resident(hcc), hbm],
        out_specs=(ch(D_MODEL), ch(D_MODEL), ch(RET_W), pl.BlockSpec((1, HEADS, HEAD_DIM, HEAD_DIM), lambda i: (i, 0, 0, 0)), hbm),
        scratch_shapes=[pltpu.VMEM((HEADS, HEAD_DIM, HEAD_DIM), F32)] + _gather_scratch(len(ids)),
        compiler_params=_cparams(("arbitrary",), collective=COLLECTIVE["fwd_mix"]),
    )(x, proj, wout_g, grn, lng, lnb, ws, bsb, mask, qdec, kdec, su)


def _conv_taps(p, prev8):
    row = lax.broadcasted_iota(jnp.int32, p.shape, 0)
    p1 = jnp.where(row == 0, prev8[7:8, :], pltpu.roll(p, 1, 0))
    p2 = jnp.where(row == 0, prev8[6:7, :], jnp.where(row == 1, prev8[7:8, :], pltpu.roll(p, 2, 0)))
    return p1, p2


def _fwd_ffn(x2, g2, wup_g, cw_g, cb_g, wdn_g, gf, tgt):
    def body(x_ref, g_ref, wu_ref, cw_ref, cb_ref, wd_ref, gf_ref, t_ref, h2_ref, up_ref, u_ref, act_ref, x3_ref, loss_ref, carry):
        @pl.when(pl.program_id(0) == 0)
        def _():
            carry[...] = jnp.zeros_like(carry)

        xb = x_ref[...]
        r = lax.rsqrt(jnp.mean(xb * xb, axis=-1, keepdims=True) + EPS)
        h = ((xb * r) * g_ref[...]).astype(BF16)
        h2_ref[...] = h
        acc = xb
        for t0, tw in FF_TILES:
            u = []
            for c0 in (t0, D_FF + t0):
                cs = slice(c0, c0 + tw)
                p = _dot_nt(h, wu_ref[pl.ds(c0, tw), :])
                up_ref[:, cs] = p.astype(BF16)
                p1, p2 = _conv_taps(p, carry[:, cs])
                carry[:, cs] = p[TM - 8:, :]
                us = p2 * cw_ref[0:1, cs] + p1 * cw_ref[1:2, cs] + p * cw_ref[2:3, cs] + cb_ref[:, cs]
                u_ref[:, cs] = us.astype(BF16)
                u.append(us)
            a = ((u[0] * _sigmoid(u[0])) * u[1]).astype(BF16)
            act_ref[:, t0:t0 + tw] = a
            acc = acc + _dot(a, wd_ref[pl.ds(t0, tw), :])
        x3_ref[...] = acc
        r3 = lax.rsqrt(jnp.mean(acc * acc, axis=-1, keepdims=True) + EPS)
        diff = (acc * r3) * gf_ref[...] - t_ref[...]
        loss_ref[...] = jnp.full(loss_ref.shape, 0.5 * jnp.sum(jnp.mean(diff * diff, axis=-1)), F32)

    tok = lambda w: pl.BlockSpec((TM, w), lambda i: (i, 0))
    return pl.pallas_call(
        body, name="fwd_ffn", grid=(N_TB,),
        out_shape=(jax.ShapeDtypeStruct((SEQ, D_MODEL), BF16), jax.ShapeDtypeStruct((SEQ, 2 * D_FF), BF16),
                   jax.ShapeDtypeStruct((SEQ, 2 * D_FF), BF16),
                   jax.ShapeDtypeStruct((SEQ, D_FF), BF16), jax.ShapeDtypeStruct((SEQ, D_MODEL), F32),
                   jax.ShapeDtypeStruct((N_TB, 8, 128), F32)),
        in_specs=[tok(D_MODEL), _resident((1, D_MODEL)), _resident((2 * D_FF, D_MODEL)), _resident((8, 2 * D_FF)),
                  _resident((1, 2 * D_FF)), _resident((D_FF, D_MODEL)), _resident((1, D_MODEL)), tok(D_MODEL)],
        out_specs=(tok(D_MODEL), tok(2 * D_FF), tok(2 * D_FF), tok(D_FF), tok(D_MODEL),
                   pl.BlockSpec((1, 8, 128), lambda i: (i, 0, 0))),
        scratch_shapes=[pltpu.VMEM((8, 2 * D_FF), F32)],
        compiler_params=_cparams(("arbitrary",)),
    )(x2, g2, wup_g, cw_g, cb_g, wdn_g, gf, tgt)


def _bwd_ffn(x3, tgt, gf, x2, g2, up_pre, u_conv, wup_g, cw_g, wdn_g):
    def body(x3_ref, t_ref, gf_ref, x2_ref, g2_ref, up_ref, u_ref, wu_ref, cw_ref, wd_ref,
             dx3_ref, dpre_ref, dx2_ref, dgf_ref, dg2_ref, dcv_ref, nxt):
        i = pl.program_id(0)

        @pl.when(i == 0)
        def _():
            nxt[...] = jnp.zeros_like(nxt)
            dgf_ref[...] = jnp.zeros_like(dgf_ref)
            dg2_ref[...] = jnp.zeros_like(dg2_ref)
            dcv_ref[...] = jnp.zeros_like(dcv_ref)

        x3 = x3_ref[...]
        r3 = lax.rsqrt(jnp.mean(x3 * x3, axis=-1, keepdims=True) + EPS)
        xh3 = x3 * r3
        dy = (xh3 * gf_ref[...] - t_ref[...]) * (1.0 / D_MODEL)
        dgf_ref[0:1, :] += jnp.sum(dy * xh3, axis=0, keepdims=True)
        t3 = dy * gf_ref[...]
        dx3 = r3 * (t3 - xh3 * jnp.mean(t3 * xh3, axis=-1, keepdims=True))
        dx3b = dx3.astype(BF16)
        dx3_ref[...] = dx3b
        dh2 = jnp.zeros((TM, D_MODEL), F32)
        for t0, tw in FF_TILES:
            row = lax.broadcasted_iota(jnp.int32, (TM, tw), 0)
            ts = slice(t0, t0 + tw)
            dact = _dot_nt(dx3b, wd_ref[pl.ds(t0, tw), :])
            ua = u_ref[:, ts].astype(F32)
            ub = u_ref[:, D_FF + t0:D_FF + t0 + tw].astype(F32)
            sg = _sigmoid(ua)
            du = [dact * ub * (sg * (1.0 + ua * (1.0 - sg))), dact * (ua * sg)]
            for n in range(2):
                d = du[n]
                c0 = n * D_FF + t0
                cs = slice(c0, c0 + tw)
                nx = nxt[:, cs]
                n1 = jnp.where(row == TM - 1, nx[0:1, :], pltpu.roll(d, TM - 1, 0))
                n2 = jnp.where(row == TM - 2, nx[0:1, :], jnp.where(row == TM - 1, nx[1:2, :], pltpu.roll(d, TM - 2, 0)))
                nxt[:, cs] = d[0:8, :]
                dp = (d * cw_ref[2:3, cs] + n1 * cw_ref[1:2, cs] + n2 * cw_ref[0:1, cs]).astype(BF16)
                dpre_ref[:, cs] = dp
                p = up_ref[:, cs].astype(F32)
                dcv_ref[n, 0:1, ts] += jnp.sum(n2 * p, axis=0, keepdims=True)
                dcv_ref[n, 1:2, ts] += jnp.sum(n1 * p, axis=0, keepdims=True)
                dcv_ref[n, 2:3, ts] += jnp.sum(d * p, axis=0, keepdims=True)
                dcv_ref[n, 3:4, ts] += jnp.sum(d, axis=0, keepdims=True)
                dh2 = dh2 + _dot(dp, wu_ref[pl.ds(c0, tw), :])
        x2 = x2_ref[...]
        r2 = lax.rsqrt(jnp.mean(x2 * x2, axis=-1, keepdims=True) + EPS)
        xh2 = x2 * r2
        dg2_ref[0:1, :] += jnp.sum(dh2 * xh2, axis=0, keepdims=True)
        t2 = dh2 * g2_ref[...]
        dx2_ref[...] = dx3 + r2 * (t2 - xh2 * jnp.mean(t2 * xh2, axis=-1, keepdims=True))

    rev = lambda w: pl.BlockSpec((TM, w), lambda i: (N_TB - 1 - i, 0))
    acc = lambda s: pl.BlockSpec(s, lambda i: (0,) * len(s))
    return pl.pallas_call(
        body, name="bwd_ffn", grid=(N_TB,),
        out_shape=(jax.ShapeDtypeStruct((SEQ, D_MODEL), BF16), jax.ShapeDtypeStruct((SEQ, 2 * D_FF), BF16),
                   jax.ShapeDtypeStruct((SEQ, D_MODEL), F32), jax.ShapeDtypeStruct((8, D_MODEL), F32),
                   jax.ShapeDtypeStruct((8, D_MODEL), F32), jax.ShapeDtypeStruct((2, 8, D_FF), F32)),
        in_specs=[rev(D_MODEL), rev(D_MODEL), _resident((1, D_MODEL)), rev(D_MODEL), _resident((1, D_MODEL)), rev(2 * D_FF),
                  rev(2 * D_FF), _resident((2 * D_FF, D_MODEL)), _resident((8, 2 * D_FF)), _resident((D_FF, D_MODEL))],
        out_specs=(rev(D_MODEL), rev(2 * D_FF), rev(D_MODEL), acc((8, D_MODEL)), acc((8, D_MODEL)), acc((2, 8, D_FF))),
        scratch_shapes=[pltpu.VMEM((8, 2 * D_FF), F32)],
        compiler_params=_cparams(("arbitrary",)),
    )(x3, tgt, gf, x2, g2, up_pre, u_conv, wup_g, cw_g, wdn_g)


def _bwd_mix(dx2, proj, o, sprev, wout_g, grn, lng, lnb, ws, bsb, mask, qdec, kdec, cos2, sin2, hosted):
    cdec = _chunk_decay()
    geoms = [g for g, _ in hosted]
    n_h = len(hosted)

    def body(dx2_ref, p_ref, o_ref, sp_ref, w_ref, grn_ref, lng_ref, lnb_ref, ws_ref, bsb_ref, m_ref, qd_ref, kd_ref,
             cos_ref, sin_ref, *rest):
        dp_ref, dgrn_ref, dlng_ref, dlnb_ref, dws_ref, dbs_ref = rest[n_h:n_h + 6]
        dstate, dbs_acc = rest[2 * n_h + 6:2 * n_h + 8]
        i = pl.program_id(0)
        rs = _Scatters(geoms, rest[:n_h], rest[n_h + 6:2 * n_h + 6], rest[2 * n_h + 8:])
        pl.when(i == 0)(rs.phase1)
        pl.when(i == 3)(rs.phase2)
        pl.when(i == 6)(rs.phase2b)

        @pl.when(i == 0)
        def _():
            dstate[...] = jnp.zeros_like(dstate)
            dgrn_ref[...] = jnp.zeros_like(dgrn_ref)
            dlng_ref[...] = jnp.zeros_like(dlng_ref)
            dlnb_ref[...] = jnp.zeros_like(dlnb_ref)
            dws_ref[...] = jnp.zeros_like(dws_ref)
            dbs_ref[...] = jnp.zeros_like(dbs_ref)
            dbs_acc[...] = jnp.zeros_like(dbs_acc)

        dmix = _dot_nt(dx2_ref[...].astype(BF16), w_ref[...])
        for h in range(HEADS):
            sl = slice(h * HEAD_DIM, (h + 1) * HEAD_DIM)
            q = p_ref[:, sl]
            k = p_ref[:, RET_W + h * HEAD_DIM:RET_W + (h + 1) * HEAD_DIM]
            v = p_ref[:, 2 * RET_W + h * HEAD_DIM:2 * RET_W + (h + 1) * HEAD_DIM]
            g = p_ref[:, 3 * RET_W + h * HEAD_DIM:3 * RET_W + (h + 1) * HEAD_DIM]
            o = o_ref[:, sl]
            rinv = lax.rsqrt(jnp.mean(o * o, axis=-1, keepdims=True) + EPS)
            oh = o * rinv
            gr = grn_ref[:, sl]
            sg = _sigmoid(g)
            dret = dmix[:, sl]
            dp_ref[:, 3 * RET_W + h * HEAD_DIM:3 * RET_W + (h + 1) * HEAD_DIM] = (
                dret * (oh * gr) * (sg * (1.0 + g * (1.0 - sg)))).astype(BF16)
            drn = dret * (g * sg)
            dgrn_ref[0:1, sl] += jnp.sum(drn * oh, axis=0, keepdims=True)
            t = drn * gr
            do = rinv * (t - oh * jnp.mean(t * oh, axis=-1, keepdims=True))
            qb, kb, vb, dob = q.astype(BF16), k.astype(BF16), v.astype(BF16), do.astype(BF16)
            m = m_ref[h]
            ab = (_dot_nt(qb, kb) * m).astype(BF16)
            dab = (_dot_nt(dob, vb) * m).astype(BF16)
            spb = sp_ref[0, h]
            dsn = dstate[h]
            dsnb = dsn.astype(BF16)
            qdb = (q * qd_ref[h]).astype(BF16)
            kdb = (k * kd_ref[h]).astype(BF16)
            dq = _dot(dab, kb) + _dot_nt(dob, spb) * qd_ref[h]
            dk = _dot_tn(dab, qb) + _dot_nt(vb, dsnb) * kd_ref[h]
            dv = _dot_tn(ab, dob) + _dot(kdb, dsnb)
            dstate[h] = dsn * cdec[h] + _dot_tn(qdb, dob)
            c2, s2 = cos_ref[...], sin_ref[...]
            dp_ref[:, sl] = _rot_t(dq, c2, s2).astype(BF16)
            dp_ref[:, RET_W + h * HEAD_DIM:RET_W + (h + 1) * HEAD_DIM] = _rot_t(dk * K_SCALE, c2, s2).astype(BF16)
            dp_ref[:, 2 * RET_W + h * HEAD_DIM:2 * RET_W + (h + 1) * HEAD_DIM] = dv.astype(BF16)
        for gi in range(HEADS):
            sl = slice(gi * HEAD_DIM, (gi + 1) * HEAD_DIM)
            u = p_ref[:, 4 * RET_W + gi * HEAD_DIM:4 * RET_W + (gi + 1) * HEAD_DIM]
            sv = p_ref[:, 4 * RET_W + SGU_W + gi * HEAD_DIM:4 * RET_W + SGU_W + (gi + 1) * HEAD_DIM]
            gv = _gelu(sv)
            xc = gv - jnp.mean(gv, axis=-1, keepdims=True)
            rstd = lax.rsqrt(jnp.mean(xc * xc, axis=-1, keepdims=True) + EPS)
            xh = xc * rstd
            lg = lng_ref[:, sl]
            vnb = (xh * lg + lnb_ref[:, sl]).astype(BF16)
            wcb = _causal(ws_ref[gi]).astype(BF16)
            mixed = _dot(wcb, vnb) + bsb_ref[gi]
            dsgu = dmix[:, RET_W + gi * HEAD_DIM:RET_W + (gi + 1) * HEAD_DIM]
            dmixed = dsgu * _gelu(u)
            dmb = dmixed.astype(BF16)
            dws_ref[gi] += _causal(_dot_nt(dmb, vnb))
            dbs_acc[gi] += dmixed
            dvn = _dot_tn(wcb, dmb)
            dlng_ref[gi:gi + 1, :] += jnp.sum(dvn * xh, axis=0, keepdims=True)
            dlnb_ref[gi:gi + 1, :] += jnp.sum(dvn, axis=0, keepdims=True)
            dxh = dvn * lg
            dgv = rstd * (dxh - jnp.mean(dxh, axis=-1, keepdims=True) - xh * jnp.mean(dxh * xh, axis=-1, keepdims=True))
            dp_ref[:, 4 * RET_W + gi * HEAD_DIM:4 * RET_W + (gi + 1) * HEAD_DIM] = (dsgu * mixed * _gelu_grad(u)).astype(BF16)
            dp_ref[:, 4 * RET_W + SGU_W + gi * HEAD_DIM:4 * RET_W + SGU_W + (gi + 1) * HEAD_DIM] = (
                dgv * _gelu_grad(sv)).astype(BF16)

        @pl.when(i == N_CHUNK - 1)
        def _():
            for gi in range(HEADS):
                col = jnp.broadcast_to(jnp.sum(dbs_acc[gi], axis=-1, keepdims=True), (CHUNK, CHUNK))
                dbs_ref[gi:gi + 1, :] = jnp.transpose(col)[0:1, :]
            rs.phase3()

    rev = lambda w: pl.BlockSpec((CHUNK, w), lambda i: (N_CHUNK - 1 - i, 0))
    hcc = (HEADS, CHUNK, CHUNK)
    acc = lambda s: pl.BlockSpec(s, lambda i: (0,) * len(s))
    res = pl.pallas_call(
        body, name="bwd_mix", grid=(N_CHUNK,),
        out_shape=(jax.ShapeDtypeStruct((SEQ, PROJ_W), BF16), jax.ShapeDtypeStruct((8, RET_W), F32),
                   jax.ShapeDtypeStruct((8, HEAD_DIM), F32), jax.ShapeDtypeStruct((8, HEAD_DIM), F32),
                   jax.ShapeDtypeStruct(hcc, F32), jax.ShapeDtypeStruct((8, CHUNK), F32)) + _scatter_out_shapes(geoms),
        in_specs=[rev(D_MODEL), rev(PROJ_W), rev(RET_W),
                  pl.BlockSpec((1, HEADS, HEAD_DIM, HEAD_DIM), lambda i: (N_CHUNK - 1 - i, 0, 0, 0)),
                  _resident((D_MODEL, D_MODEL)), _resident((1, RET_W)), _resident((1, SGU_W)), _resident((1, SGU_W)),
                  _resident(hcc), _resident(hcc), _resident(hcc), _resident(hcc), _resident(hcc), rev(HEAD_DIM), rev(HEAD_DIM)]
        + [pl.BlockSpec(memory_space=pl.ANY)] * n_h,
        out_specs=(rev(PROJ_W), acc((8, RET_W)), acc((8, HEAD_DIM)), acc((8, HEAD_DIM)), acc(hcc), acc((8, CHUNK)))
        + _scatter_out_specs(geoms),
        scratch_shapes=[pltpu.VMEM((HEADS, HEAD_DIM, HEAD_DIM), F32), pltpu.VMEM((HEADS, CHUNK, CHUNK), F32)] + _scatter_scratch(geoms),
        compiler_params=_cparams(("arbitrary",), collective=COLLECTIVE["bwd_mix"]),
    )(dx2, proj, o, sprev, wout_g, grn, lng, lnb, ws, bsb, mask, qdec, kdec, cos2, sin2, *[p for _, p in hosted])
    return tuple(res[:6 + n_h])


def _bwd_proj(dproj, win_g, x, g1, dx2, gin_p, small):
    geoms = [W_IN]
    n_s = len(small)

    def body(dp_ref, w_ref, x_ref, g_ref, dx2_ref, gin_ref, *rest):
        small_refs = rest[:n_s]
        dx_ref, rs_out, rp_ref, rws_ref, rcv_ref, dg_ref = rest[n_s:n_s + 6]
        rs_scratch = rest[n_s + 6:n_s + 6 + N_SCATTER_SCRATCH]
        ar_scratch = rest[n_s + 6 + N_SCATTER_SCRATCH:]
        ar_res = ar_scratch[N_SMALL_SCRATCH:]
        ar = _SmallReduce((dg_ref,) + tuple(small_refs), ar_res, ar_scratch[:N_SMALL_SCRATCH])
        rs = _Scatters(geoms, [gin_ref], [rs_out], rs_scratch)
        pl.when(pl.program_id(0) == 0)(lambda: rs.phase1(diagonal=True))
        pl.when(pl.program_id(0) == 2)(rs.phase2)
        pl.when(pl.program_id(0) == 4)(rs.phase2b)

        @pl.when(pl.program_id(0) == 0)
        def _():
            dg_ref[...] = jnp.zeros_like(dg_ref)

        dh = _dot_nt(dp_ref[...], w_ref[...])
        xb = x_ref[...]
        r = lax.rsqrt(jnp.mean(xb * xb, axis=-1, keepdims=True) + EPS)
        xh = xb * r
        dg_ref[0:1, :] += jnp.sum(dh * xh, axis=0, keepdims=True)
        t = dh * g_ref[...]
        dx_ref[...] = dx2_ref[...] + r * (t - xh * jnp.mean(t * xh, axis=-1, keepdims=True))

        @pl.when(pl.program_id(0) == N_TB - 1)
        def _():
            ar.begin()
            rs.phase3()
            ar.end()
            for o_ref, r_ref in zip((rp_ref, rws_ref, rcv_ref), ar_res):
                o_ref[...] = r_ref[...]

    tok = lambda w: pl.BlockSpec((TM, w), lambda i: (i, 0))
    vm = pl.BlockSpec(memory_space=pltpu.VMEM)
    res = pl.pallas_call(
        body, name="bwd_proj", grid=(N_TB,),
        out_shape=(jax.ShapeDtypeStruct((SEQ, D_MODEL), F32),) + _scatter_out_shapes(geoms)
        + tuple(jax.ShapeDtypeStruct(s, F32) for s in SMALL_FULL),
        in_specs=[tok(PROJ_W), _resident((D_MODEL, PROJ_W)), tok(D_MODEL), _resident((1, D_MODEL)), tok(D_MODEL),
                  pl.BlockSpec(memory_space=pl.ANY)] + [vm] * n_s,
        out_specs=(tok(D_MODEL),) + _scatter_out_specs(geoms) + (vm,) * len(SMALL_FULL),
        scratch_shapes=[pltpu.VMEM((8, D_MODEL), F32)] + _scatter_scratch(geoms) + _small_scratch()
        + [pltpu.VMEM(s, F32) for s in SMALL_FULL],
        compiler_params=_cparams(("arbitrary",), collective=COLLECTIVE["bwd_proj"]),
    )(dproj, win_g, x, g1, dx2, gin_p, *small)
    return res


def _wgrad(name, a, b, tm=None, tn=None, hosted=()):
    m_w, n_w = a.shape[-1], b.shape[-1]
    tm = m_w if tm is None else tm
    tn = n_w if tn is None else tn
    n_steps = (m_w // tm) * (n_w // tn)
    geoms = [g for g, _ in hosted]
    n_h = len(hosted)

    def body(a_ref, b_ref, *rest):
        o_ref = rest[n_h]
        if n_h:
            rs = _Scatters(geoms, rest[:n_h], rest[n_h + 1:2 * n_h + 1], rest[2 * n_h + 1:])
            step = pl.program_id(0) * (n_w // tn) + pl.program_id(1)
            pl.when(step == 0)(rs.phase1)
            pl.when(step == 1)(rs.phase2)
            pl.when(step == 2)(rs.phase2b)
        o_ref[...] = _dot_tn(a_ref[...].astype(BF16), b_ref[...].astype(BF16)).astype(BF16)
        if n_h:
            pl.when(step == n_steps - 1)(rs.phase3)

    assert not n_h or n_steps >= 4
    res = pl.pallas_call(
        body, name=name, grid=(m_w // tm, n_w // tn),
        out_shape=(jax.ShapeDtypeStruct((m_w, n_w), BF16),) + _scatter_out_shapes(geoms),
        in_specs=[pl.BlockSpec((SEQ, tm), lambda i, j: (0, i)), pl.BlockSpec((SEQ, tn), lambda i, j: (0, j))]
        + [pl.BlockSpec(memory_space=pl.ANY)] * n_h,
        out_specs=(pl.BlockSpec((tm, tn), lambda i, j: (i, j)),) + _scatter_out_specs(geoms),
        scratch_shapes=_scatter_scratch(geoms),
        compiler_params=_cparams(("arbitrary", "arbitrary"), collective=COLLECTIVE[name]) if n_h else _cparams(("parallel", "parallel")),
    )(a, b, *[p for _, p in hosted])
    return tuple(res[:1 + n_h])


def _row_step(half_rows):
    return max(s for s in range(16, 177, 16) if half_rows % s == 0)


class _Scatter:
    def __init__(self, geom, partial, out, land1, mine, stage2, land2, comb, s1_send, s1_recv, s2_send, s2_recv, ld_sems):
        self.w, self.row0, self.shape = _geom(geom)
        self.partial, self.out, self.land1 = partial, out, land1
        self.mine, self.stage2, self.land2, self.comb = mine, stage2, land2, comb
        self.hr = self.shape[0] // 2
        self.step = _row_step(self.hr)
        self.s1_send, self.s1_recv, self.s2_send, self.s2_recv, self.ld_sems = s1_send, s1_recv, s2_send, s2_recv, ld_sems
        self.x, self.y, self.c = lax.axis_index("x"), lax.axis_index("y"), lax.axis_index("c")
        self.sibling = (self.x, self.y, 1 - self.c)
        self.chips = [(self.x, self.y), (1 - self.x, self.y), (self.x, 1 - self.y), (1 - self.x, 1 - self.y)]

    def block(self, px, py, pc):
        dev = 4 * px + 2 * py + pc
        if self.w == W_IN:
            return self.partial.at[:, pl.ds(pl.multiple_of(dev * IN_SHARD, 128), IN_SHARD)]
        if self.w == W_OUT:
            return self.partial.at[pl.ds(pl.multiple_of(dev * OUT_SHARD, 128), OUT_SHARD), :]
        if self.w == W_DOWN:
            return self.partial.at[pl.ds(pl.multiple_of(dev * DOWN_SHARD, 32), DOWN_SHARD), :]
        return self.partial.at[pl.ds(pl.multiple_of(dev * FF_SHARD + self.row0, 32), self.shape[0]), :]

    def copy1(self, k):
        return pltpu.make_async_remote_copy(
            src_ref=self.block(*self.chips[k], 1 - self.c), dst_ref=self.land1.at[k],
            send_sem=self.s1_send.at[k], recv_sem=self.s1_recv.at[k], device_id=self.sibling, device_id_type=MESH)

    STAGE2 = [(1, 0, 1), (3, 0, 1), (2, 1, 2), (3, 1, 2), (1, 1, 1), (2, 0, 2)]

    def copy2(self, j):
        blk, h, to = self.STAGE2[j]
        src = self.comb.at[j - 4] if j >= 4 else self.stage2.at[blk - 1, pl.ds(h * self.hr, self.hr), :]
        return pltpu.make_async_remote_copy(
            src_ref=src, dst_ref=self.land2.at[j], send_sem=self.s2_send.at[j], recv_sem=self.s2_recv.at[j],
            device_id=(*self.chips[to], self.c), device_id_type=MESH)

    def _rows(self, h=None):
        step = self.step
        lo, n = (0, self.shape[0]) if h is None else (h * self.hr, self.hr)
        return [pl.ds(r0, step) for r0 in range(lo, lo + n, step)]

    def load(self, k):
        return pltpu.make_async_copy(self.block(*self.chips[k], self.c), self.mine.at[k], self.ld_sems.at[k])

    def load_mine(self):
        for k in range(4):
            self.load(k).start()

    def phase1(self):
        for k in range(4):
            self.copy1(k).start()

    def phase2(self):
        for k in (3, 1, 2, 0):
            self.copy1(k).wait_recv()
            self.load(k).wait()
            for rs in self._rows():
                s = self.mine[k, rs, :].astype(F32) + self.land1[k, rs, :].astype(F32)
                if k == 0:
                    self.out[rs, :] = s
                else:
                    self.stage2[k - 1, rs, :] = s.astype(BF16)
            for j in {3: (1, 3), 1: (0,), 2: (2,), 0: ()}[k]:
                self.copy2(j).start()

    def phase2b(self):
        for j, got in ((4, 3), (5, 1)):
            blk, h, _ = self.STAGE2[j]
            self.copy2(got).wait_recv()
            for i, rs in enumerate(self._rows(h)):
                lr = pl.ds(i * self.step, self.step)
                self.comb[j - 4, lr, :] = (self.stage2[blk - 1, rs, :].astype(F32) + self.land2[got, lr, :].astype(F32)).astype(BF16)
            self.copy2(j).start()

    def phase3(self):
        for j in (0, 5, 4, 2):
            self.copy2(j).wait_recv()
        for h, (first, second) in enumerate(((0, 5), (4, 2))):
            for i, rs in enumerate(self._rows(h)):
                lr = pl.ds(i * self.step, self.step)
                self.out[rs, :] = (self.out[rs, :] + self.land2[first, lr, :].astype(F32)) + self.land2[second, lr, :].astype(F32)
        for k in range(4):
            self.copy1(k).wait_send()
        for j in range(6):
            self.copy2(j).wait_send()


def _geom(geom):
    if isinstance(geom, tuple):
        w, row0, rows = geom
        assert w == W_UP
        return w, row0, (rows, SHARD[w][1])
    return geom, 0, SHARD[geom]


N_SCATTER_SCRATCH = 10


def _scatter_out_shapes(geoms):
    return tuple(jax.ShapeDtypeStruct(_geom(g)[2], F32) for g in geoms)


def _scatter_out_specs(geoms):
    return (pl.BlockSpec(memory_space=pltpu.VMEM),) * len(geoms)


def _scatter_scratch(geoms):
    out = []
    for g in geoms:
        s = _geom(g)[2]
        hs = (s[0] // 2, s[1])
        out += [pltpu.VMEM((4,) + s, BF16), pltpu.VMEM((4,) + s, BF16), pltpu.VMEM((3,) + s, BF16), pltpu.VMEM((6,) + hs, BF16),
                pltpu.VMEM((2,) + hs, BF16),
                pltpu.SemaphoreType.DMA((4,)), pltpu.SemaphoreType.DMA((4,)), pltpu.SemaphoreType.DMA((6,)),
                pltpu.SemaphoreType.DMA((6,)), pltpu.SemaphoreType.DMA((4,))]
    return out


class _Scatters:
    def __init__(self, geoms, p_refs, out_refs, scratch):
        k = N_SCATTER_SCRATCH
        self.items = [_Scatter(g, p_refs[i], out_refs[i], *scratch[k * i:k * i + k]) for i, g in enumerate(geoms)]

    def phase1(self, diagonal=False):
        meet = _Meet(diagonal)
        meet.signal()
        for s in self.items:
            s.load_mine()
        meet.wait()
        for s in self.items:
            s.phase1()

    def phase2(self):
        for s in self.items:
            s.phase2()

    def phase2b(self):
        for s in self.items:
            s.phase2b()

    def phase3(self):
        for s in self.items:
            s.phase3()


PACK_W = 1024


SMALL_FULL = [(2, 8, PACK_W), (HEADS, CHUNK, CHUNK), (2, 8, D_FF)]
SMALL_HALF = [(s[0] // 2,) + s[1:] for s in SMALL_FULL]
N_SMALL_SCRATCH = 16


def _small_scratch():
    n_a = len(SMALL_FULL)
    return ([pltpu.VMEM(SMALL_FULL[0], F32)] + [pltpu.VMEM(s, F32) for s in SMALL_HALF] + [pltpu.VMEM(s, F32) for s in SMALL_HALF]
            + [pltpu.VMEM((3,) + s, F32) for s in SMALL_HALF]
            + [pltpu.SemaphoreType.DMA((n_a,)), pltpu.SemaphoreType.DMA((n_a,)), pltpu.SemaphoreType.DMA((n_a, 3)),
               pltpu.SemaphoreType.DMA((n_a, 3)), pltpu.SemaphoreType.DMA((n_a,)), pltpu.SemaphoreType.DMA((n_a,))])


class _SmallReduce:
    def __init__(self, ins, outs, scratch):
        self.ins, self.outs = ins, outs
        (self.pack, *rest) = scratch
        self.rxs, self.css, self.gs = rest[0:3], rest[3:6], rest[6:9]
        self.s1_send, self.s1_recv, self.s2_send, self.s2_recv, self.s3_send, self.s3_recv = rest[9:]
        self.x, self.y, self.c = lax.axis_index("x"), lax.axis_index("y"), lax.axis_index("c")
        self.sibling = (self.x, self.y, 1 - self.c)
        self.chips = [(1 - self.x, self.y), (self.x, 1 - self.y), (1 - self.x, 1 - self.y)]
        self.hl = [s[0] for s in SMALL_HALF]

    def half(self, ref, a, h):
        return ref.at[pl.ds(h * self.hl[a], self.hl[a])]

    def begin(self):
        dg1_ref, dg2_ref, dgf_ref, dgrn_ref, dlng_ref, dlnb_ref, dbs_ref, loss_ref, dws_ref, dcv_ref = self.ins
        pack, c = self.pack, self.c
        pack[...] = jnp.zeros_like(pack)
        pack[0, 0:1, :] = dg1_ref[0:1, :]
        pack[0, 1:2, :] = dg2_ref[0:1, :]
        pack[0, 2:3, :] = dgf_ref[0:1, :]
        pack[0, 3:4, 0:RET_W] = dgrn_ref[0:1, :]
        lsum = loss_ref[0, 0:1, :]
        for i in range(1, N_TB):
            lsum = lsum + loss_ref[i, 0:1, :]
        pack[0, 3:4, RET_W:RET_W + 128] = lsum
        pack[1, 0:HEADS, 0:128] = dlng_ref[0:HEADS, :]
        pack[1, 0:HEADS, 128:256] = dlnb_ref[0:HEADS, :]
        pack[1, 0:HEADS, 256:384] = dbs_ref[0:HEADS, :]
        self.srcs = [pack, dws_ref, dcv_ref]
        n_a = len(self.srcs)
        self.ex1 = [pltpu.make_async_remote_copy(src_ref=self.half(self.srcs[a], a, 1 - c), dst_ref=self.rxs[a],
                                                 send_sem=self.s1_send.at[a], recv_sem=self.s1_recv.at[a],
                                                 device_id=self.sibling, device_id_type=MESH) for a in range(n_a)]
        for cp in self.ex1:
            cp.start()
        self.ex2 = []
        for a in range(n_a):
            self.ex1[a].wait_recv()
            self.css[a][...] = self.half(self.srcs[a], a, c)[...] + self.rxs[a][...]
            for j, chip in enumerate(self.chips):
                cp = pltpu.make_async_remote_copy(src_ref=self.css[a], dst_ref=self.gs[a].at[j], send_sem=self.s2_send.at[a, j],
                                                  recv_sem=self.s2_recv.at[a, j], device_id=(*chip, c), device_id_type=MESH)
                cp.start()
                self.ex2.append(cp)

    def end(self):
        c, x, y = self.c, self.x, self.y
        ex3 = []
        for a in range(len(self.srcs)):
            css, gs, out = self.css[a], self.gs[a], self.outs[a]
            for j in range(3):
                self.ex2[3 * a + j].wait_recv()
            tot = None
            for q in range(4):
                k = jnp.where(x != (q >> 1), 1, 0) + jnp.where(y != (q & 1), 2, 0)
                term = jnp.where(k == 0, css[...], jnp.where(k == 1, gs[0], jnp.where(k == 2, gs[1], gs[2])))
                tot = term if tot is None else tot + term
            self.half(out, a, c)[...] = tot
            cp = pltpu.make_async_remote_copy(src_ref=self.half(out, a, c), dst_ref=self.half(out, a, c), send_sem=self.s3_send.at[a],
                                              recv_sem=self.s3_recv.at[a], device_id=self.sibling, device_id_type=MESH)
            cp.start()
            ex3.append(cp)
        for a in range(len(self.srcs)):
            out = self.outs[a]
            pltpu.make_async_remote_copy(src_ref=self.half(out, a, 1 - c), dst_ref=self.half(out, a, 1 - c), send_sem=self.s3_send.at[a],
                                         recv_sem=self.s3_recv.at[a], device_id=self.sibling, device_id_type=MESH).wait_recv()
        for cp in self.ex1 + self.ex2 + ex3:
            cp.wait_send()


def _adam_math(w, g, m, v):
    nm = ADAM_B1 * m + (1.0 - ADAM_B1) * g
    nv = ADAM_B2 * v + (1.0 - ADAM_B2) * (g * g)
    d = -ADAM_LR * ((nm / (1.0 - ADAM_B1 ** ADAM_STEP)) / (jnp.sqrt(nv / (1.0 - ADAM_B2 ** ADAM_STEP)) + ADAM_EPS) + ADAM_WD * w)
    return d, nm, nv


def _adamw(name, w, gs, m, v, rows):
    _, r, cdim = w.shape
    n_steps = r // rows
    half = gs[0].shape[0] // rows

    def body(w_ref, *rest):
        g_refs, (m_ref, v_ref, go_ref, d_ref, nm_ref, nv_ref) = rest[:len(gs)], rest[len(gs):]
        gg = g_refs[0][...]
        if len(gs) == 2:
            gg = jnp.where(pl.program_id(0) < half, gg, g_refs[1][...])
        go_ref[0] = gg
        d, nm, nv = _adam_math(w_ref[0], gg, m_ref[0], v_ref[0])
        d_ref[0], nm_ref[0], nv_ref[0] = d, nm, nv

    spec3 = pl.BlockSpec((1, rows, cdim), lambda i: (0, i, 0))
    if len(gs) == 1:
        g_specs = [pl.BlockSpec((rows, cdim), lambda i: (i, 0))]
    else:
        g_specs = [pl.BlockSpec((rows, cdim), lambda i: (jnp.minimum(i, half - 1), 0)),
                   pl.BlockSpec((rows, cdim), lambda i: (jnp.maximum(i - half, 0), 0))]
    sh = jax.ShapeDtypeStruct((1, r, cdim), F32)
    return pl.pallas_call(
        body, name=name, grid=(n_steps,), out_shape=(sh, sh, sh, sh),
        in_specs=[spec3] + g_specs + [spec3, spec3], out_specs=(spec3,) * 4,
        compiler_params=_cparams(("parallel",)),
    )(w, *gs, m, v)


def _adamw_small(rp, rws, rcv, gcw, params):
    n_p = len(params)

    def body(*refs):
        rp_ref, rws_ref, rcv_ref, gcw_ref = refs[:4]
        ins = refs[4:4 + 3 * n_p]
        outs = refs[4 + 3 * n_p:]
        outs[4 * n_p][...] = rp_ref[0, 3:4, RET_W:RET_W + 1]
        grads = [rp_ref[0, 0:1, :], rp_ref[0, 1:2, :], rp_ref[0, 2:3, :], rp_ref[0, 3:4, 0:RET_W],
                 rp_ref[1, 0:HEADS, 0:128], rp_ref[1, 0:HEADS, 128:256], rp_ref[1, 0:HEADS, 256:384],
                 rws_ref[...], gcw_ref[...], None]
        for p in range(n_p):
            w_ref, m_ref, v_ref = ins[3 * p:3 * p + 3]
            o = outs[4 * p:4 * p + 4]
            if p == n_p - 1:
                for hf in range(2):
                    cs = slice(hf * D_FF, (hf + 1) * D_FF)
                    g = rcv_ref[hf, 3:4, :]
                    res = (g,) + _adam_math(w_ref[:, cs], g, m_ref[:, cs], v_ref[:, cs])
                    for t in range(4):
                        o[t][:, cs] = res[t]
                continue
            lead = w_ref.ndim > grads[p].ndim
            rd = (lambda r: r[0]) if lead else (lambda r: r[...])
            res = (grads[p],) + _adam_math(rd(w_ref), grads[p], rd(m_ref), rd(v_ref))
            for t in range(4):
                if lead:
                    o[t][0] = res[t]
                else:
                    o[t][...] = res[t]

    vm = pl.BlockSpec(memory_space=pltpu.VMEM)
    flat = [a for tr in params for a in tr]
    out_shape = tuple(jax.ShapeDtypeStruct(tr[0].shape, F32) for tr in params for _ in range(4)) + (jax.ShapeDtypeStruct((1, 1), F32),)
    res = pl.pallas_call(
        body, name="adamw_small", out_shape=out_shape, in_specs=[vm] * (4 + len(flat)), out_specs=(vm,) * len(out_shape),
        compiler_params=_cparams(),
    )(rp, rws, rcv, gcw, *flat)
    return [res[4 * p:4 * p + 4] for p in range(n_p)], res[4 * n_p]


def kernel(x, mix_norm_g, w_in, ret_norm_g, sgu_ln_g, sgu_ln_b, sgu_w_s, sgu_b_s, w_out, ffn_norm_g, w_up, conv_w, conv_b, w_down, final_norm_g, loss_target, m_mix_norm_g, m_w_in, m_ret_norm_g, m_sgu_ln_g, m_sgu_ln_b, m_sgu_w_s, m_sgu_b_s, m_w_out, m_ffn_norm_g, m_w_up, m_conv_w, m_conv_b, m_w_down, m_final_norm_g, v_mix_norm_g, v_w_in, v_ret_norm_g, v_sgu_ln_g, v_sgu_ln_b, v_sgu_w_s, v_sgu_b_s, v_w_out, v_ffn_norm_g, v_w_up, v_conv_w, v_conv_b, v_w_down, v_final_norm_g):
    xs = x[0]
    tgt = loss_target[0]
    mask, qdec, kdec = _decay_tables()
    grn = ret_norm_g.reshape(1, RET_W)
    lng = sgu_ln_g.reshape(1, SGU_W)
    lnb = sgu_ln_b.reshape(1, SGU_W)
    ws = sgu_w_s[0]
    bsb = jnp.broadcast_to(sgu_b_s[0][:, :, None], (HEADS, CHUNK, HEAD_DIM))
    gf = final_norm_g.reshape(1, D_MODEL)
    me = 4 * lax.axis_index("x") + 2 * lax.axis_index("y") + lax.axis_index("c")
    tr = lambda a: jnp.transpose(a[0])[None]
    tr_cw = lambda a: jnp.transpose(a, (1, 0, 2))

    proj, h1, cos2, sin2, win_g, cw_sh, wout_g, wdn_g, su = _fwd_proj(
        xs, mix_norm_g, _rope_freq(), w_in[0], w_out[0], tr(w_up)[0], w_down[0], tr_cw(conv_w))
    cw_g = jnp.transpose(cw_sh, (1, 0, 2)).reshape(8, 2 * D_FF)
    x2, mixcat, o, sprev, wup_g = _fwd_mix(xs, proj, wout_g, grn, lng, lnb, ws, bsb, mask, qdec, kdec, su)
    h2, up_pre, u_conv, act, x3, loss_parts = _fwd_ffn(x2, ffn_norm_g, wup_g, cw_g, conv_b, wdn_g, gf, tgt)

    dx3, dpre, dx2, dgf, dg2, dcv = _bwd_ffn(x3, tgt, gf, x2, ffn_norm_g, up_pre, u_conv, wup_g, cw_g, wdn_g)
    band = 512
    (gdn_p,) = _wgrad("wgrad_down", act, dx3, tm=FF_TILE)
    (gout_p,) = _wgrad("wgrad_out", mixcat, dx2, tn=512)
    gup_p, g_dn = _wgrad("wgrad_up", dpre, h2, tm=FF_TILE, hosted=[(W_DOWN, gdn_p)])
    dproj, dgrn, dlng, dlnb, dws, dbs, g_up_a = _bwd_mix(
        dx2, proj, o, sprev, wout_g, grn, lng, lnb, ws, bsb, mask, qdec, kdec, cos2, sin2, [((W_UP, 0, band), gup_p)])
    gin_p, g_up_b, g_out = _wgrad("wgrad_in", h1, dproj, tn=768,
                                  hosted=[((W_UP, band, FF_SHARD - band), gup_p), (W_OUT, gout_p)])
    grad_x, g_in, rp, rws, rcv = _bwd_proj(dproj, win_g, xs, mix_norm_g, dx2, gin_p,
                                           (dg2, dgf, dgrn, dlng, dlnb, dbs, loss_parts, dws, dcv))
    gcw = tr_cw(lax.dynamic_slice(rcv, (me // (N_DEV // 2), 0, (me % (N_DEV // 2)) * FF_SHARD), (1, 3, FF_SHARD)))

    table = {}
    for name, w, gs, m, v, rows in (("w_in", w_in, [g_in], m_w_in, v_w_in, 256), ("w_out", w_out, [g_out], m_w_out, v_w_out, 128),
                                    ("w_up", tr(w_up), [g_up_a, g_up_b], tr(m_w_up), tr(v_w_up), 64),
                                    ("w_down", w_down, [g_dn], m_w_down, v_w_down, 88)):
        table[name] = _adamw("adamw_" + name, w, gs, m, v, rows)
    table["w_up"] = tuple(tr(a) for a in table["w_up"])
    row = lambda a: a.reshape(1, D_MODEL)
    names_small = ["mix_norm_g", "ffn_norm_g", "final_norm_g", "ret_norm_g", "sgu_ln_g", "sgu_ln_b", "sgu_b_s", "sgu_w_s",
                   "conv_w", "conv_b"]
    params = [(mix_norm_g, m_mix_norm_g, v_mix_norm_g), (ffn_norm_g, m_ffn_norm_g, v_ffn_norm_g),
              (row(final_norm_g), row(m_final_norm_g), row(v_final_norm_g)), (ret_norm_g, m_ret_norm_g, v_ret_norm_g),
              (sgu_ln_g, m_sgu_ln_g, v_sgu_ln_g), (sgu_ln_b, m_sgu_ln_b, v_sgu_ln_b), (sgu_b_s, m_sgu_b_s, v_sgu_b_s),
              (sgu_w_s, m_sgu_w_s, v_sgu_w_s), (tr_cw(conv_w), tr_cw(m_conv_w), tr_cw(v_conv_w)), (conv_b, m_conv_b, v_conv_b)]
    small, loss = _adamw_small(rp, rws, rcv, gcw, params)
    for n, res in zip(names_small, small):
        table[n] = res
    table["final_norm_g"] = tuple(a.reshape(D_MODEL) for a in table["final_norm_g"])
    table["conv_w"] = tuple(tr_cw(a) for a in table["conv_w"])

    order = ["mix_norm_g", "w_in", "ret_norm_g", "sgu_ln_g", "sgu_ln_b", "sgu_w_s", "sgu_b_s", "w_out", "ffn_norm_g", "w_up",
             "conv_w", "conv_b", "w_down", "final_norm_g"]
    outs = [loss.reshape(()), grad_x[None]]
    for col in range(4):
        outs += [table[n][col] for n in order]
    return tuple(outs)
```

```python
import functools
import math

import jax
import jax.numpy as jnp
import numpy as np
from jax import lax
from jax.experimental import pallas as pl
from jax.experimental.pallas import tpu as pltpu

F32 = jnp.float32
BF16 = jnp.bfloat16
MESH = pl.DeviceIdType.MESH

N_DEV = 8
SEQ = 2048
D_MODEL = 1024
CHUNK = 128
N_CHUNK = SEQ // CHUNK
HEADS = 4
HEAD_DIM = 128
RET_W = 512
SGU_W = 512
PROJ_W = 3072
D_FF = 2816
FF_SHARD = 704
FF_TILE = 1408
FF_TILES = ((0, 1536), (1536, 1280))
IN_SHARD = PROJ_W // N_DEV
OUT_SHARD = D_MODEL // N_DEV
DOWN_SHARD = D_FF // N_DEV
TM = 256
N_TB = SEQ // TM
FWD_PROJ_PASS_AT = 5
FWD_MIX_PASS_AT = 10
EPS = 1e-6
ROPE_BASE = 10000.0
K_SCALE = HEAD_DIM ** -0.5
INV_SQRT2 = 0.7071067811865476
INV_SQRT_2PI = 0.3989422804014327

ADAM_LR = 0.001
ADAM_B1 = 0.9
ADAM_B2 = 0.999
ADAM_EPS = 1e-08
ADAM_WD = 0.01
ADAM_STEP = 10

VMEM_LIMIT = 56 * 1024 * 1024


def _cparams(sem=None, vmem=VMEM_LIMIT, collective=None):
    return pltpu.CompilerParams(dimension_semantics=sem, vmem_limit_bytes=vmem, collective_id=collective)


COLLECTIVE = {name: k for k, name in enumerate(("fwd_proj", "fwd_mix", "wgrad_up", "bwd_mix", "wgrad_in", "bwd_proj"))}


class _Meet:
    def __init__(self, diagonal):
        x, y, c = lax.axis_index("x"), lax.axis_index("y"), lax.axis_index("c")
        self.peers = [(x, y, 1 - c), (1 - x, y, c), (x, 1 - y, c)] + ([(1 - x, 1 - y, c)] if diagonal else [])

    def signal(self):
        for peer in self.peers:
            pl.semaphore_signal(pltpu.get_barrier_semaphore(), inc=1, device_id=peer, device_id_type=MESH)

    def wait(self):
        pl.semaphore_wait(pltpu.get_barrier_semaphore(), len(self.peers))


def _resident(shape):
    nd = len(shape)
    return pl.BlockSpec(shape, lambda *_: (0,) * nd, pipeline_mode=pl.Buffered(1))


def _dot(a, b):
    return jnp.dot(a, b, preferred_element_type=F32)


def _dot_nt(a, b):
    return lax.dot_general(a, b, (((1,), (1,)), ((), ())), preferred_element_type=F32)


def _dot_tn(a, b):
    return lax.dot_general(a, b, (((0,), (0,)), ((), ())), preferred_element_type=F32)


def _sigmoid(x):
    return 1.0 / (1.0 + jnp.exp(-x))


def _gelu(x):
    return 0.5 * x * (1.0 + lax.erf(x * INV_SQRT2))


def _gelu_grad(x):
    return 0.5 * (1.0 + lax.erf(x * INV_SQRT2)) + x * (jnp.exp(-0.5 * x * x) * INV_SQRT_2PI)


def _rot(xh, cos2, sin2):
    return xh * cos2 + pltpu.roll(xh, HEAD_DIM // 2, 1) * sin2


def _rot_t(dh, cos2, sin2):
    return dh * cos2 + pltpu.roll(dh * sin2, HEAD_DIM // 2, 1)


def _rope_freq():
    half = HEAD_DIM // 2
    inv_freq = jnp.power(ROPE_BASE, -jnp.arange(half, dtype=F32) / half)
    return jnp.concatenate([inv_freq, inv_freq])[None, :]


def _rope_block(inv2, first_row):
    pos = (lax.broadcasted_iota(jnp.int32, (TM, HEAD_DIM), 0) + first_row).astype(F32)
    ang = pos * inv2
    sin = jnp.sin(ang)
    lane = lax.broadcasted_iota(jnp.int32, (TM, HEAD_DIM), 1)
    return jnp.cos(ang), jnp.where(lane < HEAD_DIM // 2, -sin, sin)


def _decay_tables():
    log_gamma = jnp.log(1.0 - jnp.power(2.0, -5.0 - jnp.arange(HEADS, dtype=F32)))
    pos = jnp.arange(CHUNK, dtype=F32)
    diff = pos[:, None] - pos[None, :]
    mask = jnp.where(diff >= 0.0, jnp.exp(log_gamma[:, None, None] * jnp.maximum(diff, 0.0)[None]), 0.0)
    k_decay = jnp.exp(log_gamma[:, None] * (CHUNK - 1.0 - pos)[None])
    q_decay = jnp.exp(log_gamma[:, None] * (pos + 1.0)[None])
    kd = jnp.broadcast_to(k_decay[:, :, None], (HEADS, CHUNK, HEAD_DIM))
    qd = jnp.broadcast_to(q_decay[:, :, None], (HEADS, CHUNK, HEAD_DIM))
    return mask.astype(F32), qd.astype(F32), kd.astype(F32)


def _chunk_decay():
    lg = np.log(np.float32(1.0) - np.power(np.float32(2.0), -5.0 - np.arange(HEADS, dtype=np.float32))).astype(np.float32)
    return [float(np.exp(lg[h] * np.float32(CHUNK))) for h in range(HEADS)]


W_IN, W_OUT, W_UP, W_DOWN, W_CONV = range(5)
GATHERED = {W_IN: ((D_MODEL, PROJ_W), BF16), W_OUT: ((D_MODEL, D_MODEL), BF16), W_UP: ((2 * D_FF, D_MODEL), BF16),
            W_DOWN: ((D_FF, D_MODEL), BF16), W_CONV: ((N_DEV, 8, FF_SHARD), F32)}
SHARD = {W_IN: (D_MODEL, IN_SHARD), W_OUT: (OUT_SHARD, D_MODEL), W_UP: (FF_SHARD, D_MODEL), W_DOWN: (DOWN_SHARD, D_MODEL),
         W_CONV: (8, FF_SHARD)}


class _Gather:
    N_SEMS = 9

    def __init__(self, ids, stages, gathered, send_sems, recv_sems, local_sems):
        self.ids, self.stages, self.gathered = ids, stages, gathered
        self.send_sems, self.recv_sems, self.local_sems = send_sems, recv_sems, local_sems
        self.x, self.y, self.c = lax.axis_index("x"), lax.axis_index("y"), lax.axis_index("c")
        self.me = (self.x, self.y, self.c)
        self.sibling = (self.x, self.y, 1 - self.c)
        self.chips = [(1 - self.x, self.y), (self.x, 1 - self.y), (1 - self.x, 1 - self.y)]

    def slot(self, n, px, py, pc):
        dev = 4 * px + 2 * py + pc
        w, g = self.ids[n], self.gathered[n]
        if w == W_IN:
            return g.at[:, pl.ds(pl.multiple_of(dev * IN_SHARD, 128), IN_SHARD)]
        if w == W_OUT:
            return g.at[pl.ds(pl.multiple_of(dev * OUT_SHARD, 128), OUT_SHARD), :]
        if w == W_DOWN:
            return g.at[pl.ds(pl.multiple_of(dev * DOWN_SHARD, 32), DOWN_SHARD), :]
        if w == W_UP:
            return g.at[pl.ds(pl.multiple_of(dev * FF_SHARD, 32), FF_SHARD), :]
        return g.at[dev]

    def half(self, n, px, py, pc, h):
        dev = 4 * px + 2 * py + pc
        w, g = self.ids[n], self.gathered[n]
        if w == W_IN:
            return g.at[pl.ds(h * (D_MODEL // 2), D_MODEL // 2), pl.ds(pl.multiple_of(dev * IN_SHARD, 128), IN_SHARD)]
        rows = SHARD[w][0] // 2
        return g.at[pl.ds(pl.multiple_of(dev * SHARD[w][0] + h * rows, 16), rows), :]

    def tree(self, n):
        return self.ids[n] != W_CONV

    def copy(self, n, k, block, to, src=None, h=None):
        ref = self.slot(n, *block) if h is None else self.half(n, *block, h)
        return pltpu.make_async_remote_copy(
            src_ref=ref if src is None else src, dst_ref=ref,
            send_sem=self.send_sems.at[n, k], recv_sem=self.recv_sems.at[n, k], device_id=to, device_id_type=MESH)

    def _mine(self):
        return [pltpu.make_async_copy(self.stages[n], self.slot(n, *self.me), self.local_sems.at[n]) for n in range(len(self.ids))]

    def _first(self):
        out = []
        for n in range(len(self.ids)):
            out.append(self.copy(n, 0, self.me, self.sibling, src=self.stages[n]))
            out += [self.copy(n, 1 + j, self.me, (*chip, self.c), src=self.stages[n])
                    for j, chip in enumerate(self.chips[:2] if self.tree(n) else self.chips)]
        return out

    def start(self):
        for cp in self._mine() + self._first():
            cp.start()

    def _passed(self, j):
        dev = (*self.chips[j], self.c)
        out = []
        for n in range(len(self.ids)):
            if not self.tree(n):
                out.append(self.copy(n, 4 + j, dev, self.sibling))
            elif j < 2:
                out += [self.copy(n, 3 + j, dev, (*self.chips[1 - j], self.c), h=j), self.copy(n, 5 + j, dev, self.sibling)]
            else:
                out += [self.copy(n, 7, dev, self.sibling, h=0), self.copy(n, 8, dev, self.sibling, h=1)]
        return out

    def near(self):
        for j in range(2):
            dev = (*self.chips[j], self.c)
            for n in range(len(self.ids)):
                self.copy(n, 1 + j, dev, self.me).wait_recv()
            for cp in self._passed(j):
                cp.start()

    def finish(self):
        dev = (*self.chips[2], self.c)
        for n in range(len(self.ids)):
            if self.tree(n):
                self.copy(n, 3, dev, self.me, h=0).wait_recv()
                self.copy(n, 4, dev, self.me, h=1).wait_recv()
            else:
                self.copy(n, 3, dev, self.me).wait_recv()
        for cp in self._passed(2):
            cp.start()
        for n in range(len(self.ids)):
            self.copy(n, 0, self.sibling, self.me).wait_recv()
            for j, chip in enumerate(self.chips):
                dev = (*chip, 1 - self.c)
                if not self.tree(n):
                    self.copy(n, 4 + j, dev, self.me).wait_recv()
                elif j < 2:
                    self.copy(n, 5 + j, dev, self.me).wait_recv()
                else:
                    self.copy(n, 7, dev, self.me, h=0).wait_recv()
                    self.copy(n, 8, dev, self.me, h=1).wait_recv()
        for cp in self._mine():
            cp.wait()
        for cp in self._first() + self._passed(0) + self._passed(1) + self._passed(2):
            cp.wait_send()


def _gather_scratch(n):
    return [pltpu.SemaphoreType.DMA((n, _Gather.N_SEMS)), pltpu.SemaphoreType.DMA((n, _Gather.N_SEMS)), pltpu.SemaphoreType.DMA((n,))]


def _gathered_shapes(ids):
    return tuple(jax.ShapeDtypeStruct(*GATHERED[w]) for w in ids)


def _fwd_proj(x, g1, inv2, w_in, w_out, w_up, w_down, conv_w):
    ids_a, ids_b = [W_IN, W_CONV], [W_OUT, W_DOWN]

    def body(x_ref, g_ref, inv_ref, in_hbm, out_hbm, up_hbm, dn_hbm, cw_ref,
             proj_ref, h1_ref, cos_ref, sin_ref, gin, gcw, gout, gdn, su_ref,
             w_vm, s_in, s_cw, s_out, s_dn, f_in, f_out, f_up, f_dn, ld_sems,
             a_send, a_recv, a_local, b_send, b_recv, b_local):
        ag_a = _Gather(ids_a, [s_in, s_cw], [gin, gcw], a_send, a_recv, a_local)
        ag_b = _Gather(ids_b, [s_out, s_dn], [gout, gdn], b_send, b_recv, b_local)

        @pl.when(pl.program_id(0) == 0)
        def _():
            meet = _Meet(diagonal=True)
            meet.signal()
            loads = [pltpu.make_async_copy(src, dst, ld_sems.at[i])
                     for i, (src, dst) in enumerate(((in_hbm, f_in), (out_hbm, f_out), (dn_hbm, f_dn), (up_hbm, f_up)))]
            for cp in loads:
                cp.start()
            s_cw[...] = jnp.zeros_like(s_cw)
            for k in range(3):
                s_cw[k:k + 1, :] = cw_ref[k]
            loads[0].wait()
            s_in[...] = f_in[...].astype(BF16)
            meet.wait()
            ag_a.start()
            loads[1].wait()
            s_out[...] = f_out[...].astype(BF16)
            loads[2].wait()
            s_dn[...] = f_dn[...].astype(BF16)
            ag_a.near()
            ag_b.start()
            loads[3].wait()
            su_ref[...] = f_up[...].astype(BF16)
            ag_a.finish()
            fill = pltpu.make_async_copy(gin, w_vm, ld_sems.at[4])
            fill.start()
            fill.wait()

        pl.when(pl.program_id(0) == FWD_PROJ_PASS_AT)(ag_b.near)

        xb = x_ref[...]
        r = lax.rsqrt(jnp.mean(xb * xb, axis=-1, keepdims=True) + EPS)
        h = ((xb * r) * g_ref[...]).astype(BF16)
        h1_ref[...] = h
        p = _dot(h, w_vm[...])
        c2, s2 = _rope_block(inv_ref[...], pl.program_id(0) * TM)
        cos_ref[...], sin_ref[...] = c2, s2
        for hd in range(HEADS):
            sl = slice(hd * HEAD_DIM, (hd + 1) * HEAD_DIM)
            proj_ref[:, sl] = _rot(p[:, sl], c2, s2)
            ks = slice(RET_W + hd * HEAD_DIM, RET_W + (hd + 1) * HEAD_DIM)
            proj_ref[:, ks] = _rot(p[:, ks], c2, s2) * K_SCALE
        proj_ref[:, 2 * RET_W:] = p[:, 2 * RET_W:]

        pl.when(pl.program_id(0) == N_TB - 1)(ag_b.finish)

    tok = lambda w: pl.BlockSpec((TM, w), lambda i: (i, 0))
    hbm = pl.BlockSpec(memory_space=pl.ANY)
    vm = pl.BlockSpec(memory_space=pltpu.VMEM)
    return pl.pallas_call(
        body, name="fwd_proj", grid=(N_TB,),
        out_shape=(jax.ShapeDtypeStruct((SEQ, PROJ_W), F32), jax.ShapeDtypeStruct((SEQ, D_MODEL), BF16),
                   jax.ShapeDtypeStruct((SEQ, HEAD_DIM), F32), jax.ShapeDtypeStruct((SEQ, HEAD_DIM), F32))
        + _gathered_shapes(ids_a + ids_b) + (jax.ShapeDtypeStruct(SHARD[W_UP], BF16),),
        in_specs=[tok(D_MODEL), _resident((1, D_MODEL)), _resident((1, HEAD_DIM)), hbm, hbm, hbm, hbm, vm],
        out_specs=(tok(PROJ_W), tok(D_MODEL), tok(HEAD_DIM), tok(HEAD_DIM), hbm, hbm, hbm, hbm, vm),
        scratch_shapes=[pltpu.VMEM((D_MODEL, PROJ_W), BF16), pltpu.VMEM(SHARD[W_IN], BF16), pltpu.VMEM(SHARD[W_CONV], F32),
                        pltpu.VMEM(SHARD[W_OUT], BF16), pltpu.VMEM(SHARD[W_DOWN], BF16),
                        pltpu.VMEM(SHARD[W_IN], F32), pltpu.VMEM(SHARD[W_OUT], F32), pltpu.VMEM(SHARD[W_UP], F32),
                        pltpu.VMEM(SHARD[W_DOWN], F32), pltpu.SemaphoreType.DMA((5,))]
        + _gather_scratch(len(ids_a)) + _gather_scratch(len(ids_b)),
        compiler_params=_cparams(("arbitrary",), collective=COLLECTIVE["fwd_proj"]),
    )(x, g1, inv2, w_in, w_out, w_up, w_down, conv_w)


def _causal(w):
    r = lax.broadcasted_iota(jnp.int32, (CHUNK, CHUNK), 0)
    c = lax.broadcasted_iota(jnp.int32, (CHUNK, CHUNK), 1)
    return jnp.where(r >= c, w, 0.0)


def _fwd_mix(x, proj, wout_g, grn, lng, lnb, ws, bsb, mask, qdec, kdec, su):
    cdec = _chunk_decay()
    ids = [W_UP]

    def body(x_ref, p_ref, w_ref, grn_ref, lng_ref, lnb_ref, ws_ref, bsb_ref, m_ref, qd_ref, kd_ref, su_ref,
             x2_ref, cat_ref, o_ref, sp_ref, gup, state, send_sems, recv_sems, local_sems):
        ag = _Gather(ids, [su_ref], [gup], send_sems, recv_sems, local_sems)

        @pl.when(pl.program_id(0) == 0)
        def _():
            meet = _Meet(diagonal=False)
            meet.signal()
            state[...] = jnp.zeros_like(state)
            meet.wait()
            ag.start()

        for h in range(HEADS):
            sl = slice(h * HEAD_DIM, (h + 1) * HEAD_DIM)
            q = p_ref[:, sl]
            k = p_ref[:, RET_W + h * HEAD_DIM:RET_W + (h + 1) * HEAD_DIM]
            v = p_ref[:, 2 * RET_W + h * HEAD_DIM:2 * RET_W + (h + 1) * HEAD_DIM]
            g = p_ref[:, 3 * RET_W + h * HEAD_DIM:3 * RET_W + (h + 1) * HEAD_DIM]
            qb, kb, vb = q.astype(BF16), k.astype(BF16), v.astype(BF16)
            a = _dot_nt(qb, kb) * m_ref[h]
            spb = state[h].astype(BF16)
            sp_ref[0, h] = spb
            o = _dot(a.astype(BF16), vb) + _dot((q * qd_ref[h]).astype(BF16), spb)
            state[h] = state[h] * cdec[h] + _dot_tn((k * kd_ref[h]).astype(BF16), vb)
            o_ref[:, sl] = o
            rinv = lax.rsqrt(jnp.mean(o * o, axis=-1, keepdims=True) + EPS)
            rn = (o * rinv) * grn_ref[:, sl]
            cat_ref[:, sl] = ((g * _sigmoid(g)) * rn).astype(BF16)
        for gi in range(HEADS):
            sl = slice(gi * HEAD_DIM, (gi + 1) * HEAD_DIM)
            u = p_ref[:, 4 * RET_W + gi * HEAD_DIM:4 * RET_W + (gi + 1) * HEAD_DIM]
            sv = p_ref[:, 4 * RET_W + SGU_W + gi * HEAD_DIM:4 * RET_W + SGU_W + (gi + 1) * HEAD_DIM]
            gv = _gelu(sv)
            xc = gv - jnp.mean(gv, axis=-1, keepdims=True)
            vn = (xc * lax.rsqrt(jnp.mean(xc * xc, axis=-1, keepdims=True) + EPS)) * lng_ref[:, sl] + lnb_ref[:, sl]
            mixed = _dot(_causal(ws_ref[gi]).astype(BF16), vn.astype(BF16)) + bsb_ref[gi]
            cat_ref[:, RET_W + gi * HEAD_DIM:RET_W + (gi + 1) * HEAD_DIM] = (_gelu(u) * mixed).astype(BF16)
        x2_ref[...] = x_ref[...] + _dot(cat_ref[...], w_ref[...])

        pl.when(pl.program_id(0) == FWD_MIX_PASS_AT)(ag.near)
        pl.when(pl.program_id(0) == N_CHUNK - 1)(ag.finish)

    ch = lambda w: pl.BlockSpec((CHUNK, w), lambda i: (i, 0))
    hcc = (HEADS, CHUNK, CHUNK)
    hbm = pl.BlockSpec(memory_space=pl.ANY)
    return pl.pallas_call(
        body, name="fwd_mix", grid=(N_CHUNK,),
        out_shape=(jax.ShapeDtypeStruct((SEQ, D_MODEL), F32), jax.ShapeDtypeStruct((SEQ, D_MODEL), BF16),
                   jax.ShapeDtypeStruct((SEQ, RET_W), F32), jax.ShapeDtypeStruct((N_CHUNK, HEADS, HEAD_DIM, HEAD_DIM), BF16))
        + _gathered_shapes(ids),
        in_specs=[ch(D_MODEL), ch(PROJ_W), _resident((D_MODEL, D_MODEL)), _resident((1, RET_W)), _resident((1, SGU_W)),
                  _resident((1, SGU_W)), _resident(hcc), _resident(hcc), _resident(hcc), _resident(hcc), _resident(hcc), hbm],
        out_specs=(ch(D_MODEL), ch(D_MODEL), ch(RET_W), pl.BlockSpec((1, HEADS, HEAD_DIM, HEAD_DIM), lambda i: (i, 0, 0, 0)), hbm),
        scratch_shapes=[pltpu.VMEM((HEADS, HEAD_DIM, HEAD_DIM), F32)] + _gather_scratch(len(ids)),
        compiler_params=_cparams(("arbitrary",), collective=COLLECTIVE["fwd_mix"]),
    )(x, proj, wout_g, grn, lng, lnb, ws, bsb, mask, qdec, kdec, su)


def _conv_taps(p, prev8):
    row = lax.broadcasted_iota(jnp.int32, p.shape, 0)
    p1 = jnp.where(row == 0, prev8[7:8, :], pltpu.roll(p, 1, 0))
    p2 = jnp.where(row == 0, prev8[6:7, :], jnp.where(row == 1, prev8[7:8, :], pltpu.roll(p, 2, 0)))
    return p1, p2


def _fwd_ffn(x2, g2, wup_g, cw_g, cb_g, wdn_g, gf, tgt):
    def body(x_ref, g_ref, wu_ref, cw_ref, cb_ref, wd_ref, gf_ref, t_ref, h2_ref, up_ref, u_ref, act_ref, x3_ref, loss_ref, carry):
        @pl.when(pl.program_id(0) == 0)
        def _():
            carry[...] = jnp.zeros_like(carry)

        xb = x_ref[...]
        r = lax.rsqrt(jnp.mean(xb * xb, axis=-1, keepdims=True) + EPS)
        h = ((xb * r) * g_ref[...]).astype(BF16)
        h2_ref[...] = h
        acc = xb
        for t0, tw in FF_TILES:
            u = []
            for c0 in (t0, D_FF + t0):
                cs = slice(c0, c0 + tw)
                p = _dot_nt(h, wu_ref[pl.ds(c0, tw), :])
                up_ref[:, cs] = p.astype(BF16)
                p1, p2 = _conv_taps(p, carry[:, cs])
                carry[:, cs] = p[TM - 8:, :]
                us = p2 * cw_ref[0:1, cs] + p1 * cw_ref[1:2, cs] + p * cw_ref[2:3, cs] + cb_ref[:, cs]
                u_ref[:, cs] = us.astype(BF16)
                u.append(us)
            a = ((u[0] * _sigmoid(u[0])) * u[1]).astype(BF16)
            act_ref[:, t0:t0 + tw] = a
            acc = acc + _dot(a, wd_ref[pl.ds(t0, tw), :])
        x3_ref[...] = acc
        r3 = lax.rsqrt(jnp.mean(acc * acc, axis=-1, keepdims=True) + EPS)
        diff = (acc * r3) * gf_ref[...] - t_ref[...]
        loss_ref[...] = jnp.full(loss_ref.shape, 0.5 * jnp.sum(jnp.mean(diff * diff, axis=-1)), F32)

    tok = lambda w: pl.BlockSpec((TM, w), lambda i: (i, 0))
    return pl.pallas_call(
        body, name="fwd_ffn", grid=(N_TB,),
        out_shape=(jax.ShapeDtypeStruct((SEQ, D_MODEL), BF16), jax.ShapeDtypeStruct((SEQ, 2 * D_FF), BF16),
                   jax.ShapeDtypeStruct((SEQ, 2 * D_FF), BF16),
                   jax.ShapeDtypeStruct((SEQ, D_FF), BF16), jax.ShapeDtypeStruct((SEQ, D_MODEL), F32),
                   jax.ShapeDtypeStruct((N_TB, 8, 128), F32)),
        in_specs=[tok(D_MODEL), _resident((1, D_MODEL)), _resident((2 * D_FF, D_MODEL)), _resident((8, 2 * D_FF)),
                  _resident((1, 2 * D_FF)), _resident((D_FF, D_MODEL)), _resident((1, D_MODEL)), tok(D_MODEL)],
        out_specs=(tok(D_MODEL), tok(2 * D_FF), tok(2 * D_FF), tok(D_FF), tok(D_MODEL),
                   pl.BlockSpec((1, 8, 128), lambda i: (i, 0, 0))),
        scratch_shapes=[pltpu.VMEM((8, 2 * D_FF), F32)],
        compiler_params=_cparams(("arbitrary",)),
    )(x2, g2, wup_g, cw_g, cb_g, wdn_g, gf, tgt)


def _bwd_ffn(x3, tgt, gf, x2, g2, up_pre, u_conv, wup_g, cw_g, wdn_g):
    def body(x3_ref, t_ref, gf_ref, x2_ref, g2_ref, up_ref, u_ref, wu_ref, cw_ref, wd_ref,
             dx3_ref, dpre_ref, dx2_ref, dgf_ref, dg2_ref, dcv_ref, nxt):
        i = pl.program_id(0)

        @pl.when(i == 0)
        def _():
            nxt[...] = jnp.zeros_like(nxt)
            dgf_ref[...] = jnp.zeros_like(dgf_ref)
            dg2_ref[...] = jnp.zeros_like(dg2_ref)
            dcv_ref[...] = jnp.zeros_like(dcv_ref)

        x3 = x3_ref[...]
        r3 = lax.rsqrt(jnp.mean(x3 * x3, axis=-1, keepdims=True) + EPS)
        xh3 = x3 * r3
        dy = (xh3 * gf_ref[...] - t_ref[...]) * (1.0 / D_MODEL)
        dgf_ref[0:1, :] += jnp.sum(dy * xh3, axis=0, keepdims=True)
        t3 = dy * gf_ref[...]
        dx3 = r3 * (t3 - xh3 * jnp.mean(t3 * xh3, axis=-1, keepdims=True))
        dx3b = dx3.astype(BF16)
        dx3_ref[...] = dx3b
        dh2 = jnp.zeros((TM, D_MODEL), F32)
        for t0, tw in FF_TILES:
            row = lax.broadcasted_iota(jnp.int32, (TM, tw), 0)
            ts = slice(t0, t0 + tw)
            dact = _dot_nt(dx3b, wd_ref[pl.ds(t0, tw), :])
            ua = u_ref[:, ts].astype(F32)
            ub = u_ref[:, D_FF + t0:D_FF + t0 + tw].astype(F32)
            sg = _sigmoid(ua)
            du = [dact * ub * (sg * (1.0 + ua * (1.0 - sg))), dact * (ua * sg)]
            for n in range(2):
                d = du[n]
                c0 = n * D_FF + t0
                cs = slice(c0, c0 + tw)
                nx = nxt[:, cs]
                n1 = jnp.where(row == TM - 1, nx[0:1, :], pltpu.roll(d, TM - 1, 0))
                n2 = jnp.where(row == TM - 2, nx[0:1, :], jnp.where(row == TM - 1, nx[1:2, :], pltpu.roll(d, TM - 2, 0)))
                nxt[:, cs] = d[0:8, :]
                dp = (d * cw_ref[2:3, cs] + n1 * cw_ref[1:2, cs] + n2 * cw_ref[0:1, cs]).astype(BF16)
                dpre_ref[:, cs] = dp
                p = up_ref[:, cs].astype(F32)
                dcv_ref[n, 0:1, ts] += jnp.sum(n2 * p, axis=0, keepdims=True)
                dcv_ref[n, 1:2, ts] += jnp.sum(n1 * p, axis=0, keepdims=True)
                dcv_ref[n, 2:3, ts] += jnp.sum(d * p, axis=0, keepdims=True)
                dcv_ref[n, 3:4, ts] += jnp.sum(d, axis=0, keepdims=True)
                dh2 = dh2 + _dot(dp, wu_ref[pl.ds(c0, tw), :])
        x2 = x2_ref[...]
        r2 = lax.rsqrt(jnp.mean(x2 * x2, axis=-1, keepdims=True) + EPS)
        xh2 = x2 * r2
        dg2_ref[0:1, :] += jnp.sum(dh2 * xh2, axis=0, keepdims=True)
        t2 = dh2 * g2_ref[...]
        dx2_ref[...] = dx3 + r2 * (t2 - xh2 * jnp.mean(t2 * xh2, axis=-1, keepdims=True))

    rev = lambda w: pl.BlockSpec((TM, w), lambda i: (N_TB - 1 - i, 0))
    acc = lambda s: pl.BlockSpec(s, lambda i: (0,) * len(s))
    return pl.pallas_call(
        body, name="bwd_ffn", grid=(N_TB,),
        out_shape=(jax.ShapeDtypeStruct((SEQ, D_MODEL), BF16), jax.ShapeDtypeStruct((SEQ, 2 * D_FF), BF16),
                   jax.ShapeDtypeStruct((SEQ, D_MODEL), F32), jax.ShapeDtypeStruct((8, D_MODEL), F32),
                   jax.ShapeDtypeStruct((8, D_MODEL), F32), jax.ShapeDtypeStruct((2, 8, D_FF), F32)),
        in_specs=[rev(D_MODEL), rev(D_MODEL), _resident((1, D_MODEL)), rev(D_MODEL), _resident((1, D_MODEL)), rev(2 * D_FF),
                  rev(2 * D_FF), _resident((2 * D_FF, D_MODEL)), _resident((8, 2 * D_FF)), _resident((D_FF, D_MODEL))],
        out_specs=(rev(D_MODEL), rev(2 * D_FF), rev(D_MODEL), acc((8, D_MODEL)), acc((8, D_MODEL)), acc((2, 8, D_FF))),
        scratch_shapes=[pltpu.VMEM((8, 2 * D_FF), F32)],
        compiler_params=_cparams(("arbitrary",)),
    )(x3, tgt, gf, x2, g2, up_pre, u_conv, wup_g, cw_g, wdn_g)


def _bwd_mix(dx2, proj, o, sprev, wout_g, grn, lng, lnb, ws, bsb, mask, qdec, kdec, cos2, sin2, hosted):
    cdec = _chunk_decay()
    geoms = [g for g, _ in hosted]
    n_h = len(hosted)

    def body(dx2_ref, p_ref, o_ref, sp_ref, w_ref, grn_ref, lng_ref, lnb_ref, ws_ref, bsb_ref, m_ref, qd_ref, kd_ref,
             cos_ref, sin_ref, *rest):
        dp_ref, dgrn_ref, dlng_ref, dlnb_ref, dws_ref, dbs_ref = rest[n_h:n_h + 6]
        dstate, dbs_acc = rest[2 * n_h + 6:2 * n_h + 8]
        i = pl.program_id(0)
        rs = _Scatters(geoms, rest[:n_h], rest[n_h + 6:2 * n_h + 6], rest[2 * n_h + 8:])
        pl.when(i == 0)(rs.phase1)
        pl.when(i == 3)(rs.phase2)
        pl.when(i == 6)(rs.phase2b)

        @pl.when(i == 0)
        def _():
            dstate[...] = jnp.zeros_like(dstate)
            dgrn_ref[...] = jnp.zeros_like(dgrn_ref)
            dlng_ref[...] = jnp.zeros_like(dlng_ref)
            dlnb_ref[...] = jnp.zeros_like(dlnb_ref)
            dws_ref[...] = jnp.zeros_like(dws_ref)
            dbs_ref[...] = jnp.zeros_like(dbs_ref)
            dbs_acc[...] = jnp.zeros_like(dbs_acc)

        dmix = _dot_nt(dx2_ref[...].astype(BF16), w_ref[...])
        for h in range(HEADS):
            sl = slice(h * HEAD_DIM, (h + 1) * HEAD_DIM)
            q = p_ref[:, sl]
            k = p_ref[:, RET_W + h * HEAD_DIM:RET_W + (h + 1) * HEAD_DIM]
            v = p_ref[:, 2 * RET_W + h * HEAD_DIM:2 * RET_W + (h + 1) * HEAD_DIM]
            g = p_ref[:, 3 * RET_W + h * HEAD_DIM:3 * RET_W + (h + 1) * HEAD_DIM]
            o = o_ref[:, sl]
            rinv = lax.rsqrt(jnp.mean(o * o, axis=-1, keepdims=True) + EPS)
            oh = o * rinv
            gr = grn_ref[:, sl]
            sg = _sigmoid(g)
            dret = dmix[:, sl]
            dp_ref[:, 3 * RET_W + h * HEAD_DIM:3 * RET_W + (h + 1) * HEAD_DIM] = (
                dret * (oh * gr) * (sg * (1.0 + g * (1.0 - sg)))).astype(BF16)
            drn = dret * (g * sg)
            dgrn_ref[0:1, sl] += jnp.sum(drn * oh, axis=0, keepdims=True)
            t = drn * gr
            do = rinv * (t - oh * jnp.mean(t * oh, axis=-1, keepdims=True))
            qb, kb, vb, dob = q.astype(BF16), k.astype(BF16), v.astype(BF16), do.astype(BF16)
            m = m_ref[h]
            ab = (_dot_nt(qb, kb) * m).astype(BF16)
            dab = (_dot_nt(dob, vb) * m).astype(BF16)
            spb = sp_ref[0, h]
            dsn = dstate[h]
            dsnb = dsn.astype(BF16)
            qdb = (q * qd_ref[h]).astype(BF16)
            kdb = (k * kd_ref[h]).astype(BF16)
            dq = _dot(dab, kb) + _dot_nt(dob, spb) * qd_ref[h]
            dk = _dot_tn(dab, qb) + _dot_nt(vb, dsnb) * kd_ref[h]
            dv = _dot_tn(ab, dob) + _dot(kdb, dsnb)
            dstate[h] = dsn * cdec[h] + _dot_tn(qdb, dob)
            c2, s2 = cos_ref[...], sin_ref[...]
            dp_ref[:, sl] = _rot_t(dq, c2, s2).astype(BF16)
            dp_ref[:, RET_W + h * HEAD_DIM:RET_W + (h + 1) * HEAD_DIM] = _rot_t(dk * K_SCALE, c2, s2).astype(BF16)
            dp_ref[:, 2 * RET_W + h * HEAD_DIM:2 * RET_W + (h + 1) * HEAD_DIM] = dv.astype(BF16)
        for gi in range(HEADS):
            sl = slice(gi * HEAD_DIM, (gi + 1) * HEAD_DIM)
            u = p_ref[:, 4 * RET_W + gi * HEAD_DIM:4 * RET_W + (gi + 1) * HEAD_DIM]
            sv = p_ref[:, 4 * RET_W + SGU_W + gi * HEAD_DIM:4 * RET_W + SGU_W + (gi + 1) * HEAD_DIM]
            gv = _gelu(sv)
            xc = gv - jnp.mean(gv, axis=-1, keepdims=True)
            rstd = lax.rsqrt(jnp.mean(xc * xc, axis=-1, keepdims=True) + EPS)
            xh = xc * rstd
            lg = lng_ref[:, sl]
            vnb = (xh * lg + lnb_ref[:, sl]).astype(BF16)
            wcb = _causal(ws_ref[gi]).astype(BF16)
            mixed = _dot(wcb, vnb) + bsb_ref[gi]
            dsgu = dmix[:, RET_W + gi * HEAD_DIM:RET_W + (gi + 1) * HEAD_DIM]
            dmixed = dsgu * _gelu(u)
            dmb = dmixed.astype(BF16)
            dws_ref[gi] += _causal(_dot_nt(dmb, vnb))
            dbs_acc[gi] += dmixed
            dvn = _dot_tn(wcb, dmb)
            dlng_ref[gi:gi + 1, :] += jnp.sum(dvn * xh, axis=0, keepdims=True)
            dlnb_ref[gi:gi + 1, :] += jnp.sum(dvn, axis=0, keepdims=True)
            dxh = dvn * lg
            dgv = rstd * (dxh - jnp.mean(dxh, axis=-1, keepdims=True) - xh * jnp.mean(dxh * xh, axis=-1, keepdims=True))
            dp_ref[:, 4 * RET_W + gi * HEAD_DIM:4 * RET_W + (gi + 1) * HEAD_DIM] = (dsgu * mixed * _gelu_grad(u)).astype(BF16)
            dp_ref[:, 4 * RET_W + SGU_W + gi * HEAD_DIM:4 * RET_W + SGU_W + (gi + 1) * HEAD_DIM] = (
                dgv * _gelu_grad(sv)).astype(BF16)

        @pl.when(i == N_CHUNK - 1)
        def _():
            for gi in range(HEADS):
                col = jnp.broadcast_to(jnp.sum(dbs_acc[gi], axis=-1, keepdims=True), (CHUNK, CHUNK))
                dbs_ref[gi:gi + 1, :] = jnp.transpose(col)[0:1, :]
            rs.phase3()

    rev = lambda w: pl.BlockSpec((CHUNK, w), lambda i: (N_CHUNK - 1 - i, 0))
    hcc = (HEADS, CHUNK, CHUNK)
    acc = lambda s: pl.BlockSpec(s, lambda i: (0,) * len(s))
    res = pl.pallas_call(
        body, name="bwd_mix", grid=(N_CHUNK,),
        out_shape=(jax.ShapeDtypeStruct((SEQ, PROJ_W), BF16), jax.ShapeDtypeStruct((8, RET_W), F32),
                   jax.ShapeDtypeStruct((8, HEAD_DIM), F32), jax.ShapeDtypeStruct((8, HEAD_DIM), F32),
                   jax.ShapeDtypeStruct(hcc, F32), jax.ShapeDtypeStruct((8, CHUNK), F32)) + _scatter_out_shapes(geoms),
        in_specs=[rev(D_MODEL), rev(PROJ_W), rev(RET_W),
                  pl.BlockSpec((1, HEADS, HEAD_DIM, HEAD_DIM), lambda i: (N_CHUNK - 1 - i, 0, 0, 0)),
                  _resident((D_MODEL, D_MODEL)), _resident((1, RET_W)), _resident((1, SGU_W)), _resident((1, SGU_W)),
                  _resident(hcc), _resident(hcc), _resident(hcc), _resident(hcc), _resident(hcc), rev(HEAD_DIM), rev(HEAD_DIM)]
        + [pl.BlockSpec(memory_space=pl.ANY)] * n_h,
        out_specs=(rev(PROJ_W), acc((8, RET_W)), acc((8, HEAD_DIM)), acc((8, HEAD_DIM)), acc(hcc), acc((8, CHUNK)))
        + _scatter_out_specs(geoms),
        scratch_shapes=[pltpu.VMEM((HEADS, HEAD_DIM, HEAD_DIM), F32), pltpu.VMEM((HEADS, CHUNK, CHUNK), F32)] + _scatter_scratch(geoms),
        compiler_params=_cparams(("arbitrary",), collective=COLLECTIVE["bwd_mix"]),
    )(dx2, proj, o, sprev, wout_g, grn, lng, lnb, ws, bsb, mask, qdec, kdec, cos2, sin2, *[p for _, p in hosted])
    return tuple(res[:6 + n_h])


def _bwd_proj(dproj, win_g, x, g1, dx2, gin_p, small):
    geoms = [W_IN]
    n_s = len(small)

    def body(dp_ref, w_ref, x_ref, g_ref, dx2_ref, gin_ref, *rest):
        small_refs = rest[:n_s]
        dx_ref, rs_out, rp_ref, rws_ref, rcv_ref, dg_ref = rest[n_s:n_s + 6]
        rs_scratch = rest[n_s + 6:n_s + 6 + N_SCATTER_SCRATCH]
        ar_scratch = rest[n_s + 6 + N_SCATTER_SCRATCH:]
        ar_res = ar_scratch[N_SMALL_SCRATCH:]
        ar = _SmallReduce((dg_ref,) + tuple(small_refs), ar_res, ar_scratch[:N_SMALL_SCRATCH])
        rs = _Scatters(geoms, [gin_ref], [rs_out], rs_scratch)
        pl.when(pl.program_id(0) == 0)(lambda: rs.phase1(diagonal=True))
        pl.when(pl.program_id(0) == 2)(rs.phase2)
        pl.when(pl.program_id(0) == 4)(rs.phase2b)

        @pl.when(pl.program_id(0) == 0)
        def _():
            dg_ref[...] = jnp.zeros_like(dg_ref)

        dh = _dot_nt(dp_ref[...], w_ref[...])
        xb = x_ref[...]
        r = lax.rsqrt(jnp.mean(xb * xb, axis=-1, keepdims=True) + EPS)
        xh = xb * r
        dg_ref[0:1, :] += jnp.sum(dh * xh, axis=0, keepdims=True)
        t = dh * g_ref[...]
        dx_ref[...] = dx2_ref[...] + r * (t - xh * jnp.mean(t * xh, axis=-1, keepdims=True))

        @pl.when(pl.program_id(0) == N_TB - 1)
        def _():
            ar.begin()
            rs.phase3()
            ar.end()
            for o_ref, r_ref in zip((rp_ref, rws_ref, rcv_ref), ar_res):
                o_ref[...] = r_ref[...]

    tok = lambda w: pl.BlockSpec((TM, w), lambda i: (i, 0))
    vm = pl.BlockSpec(memory_space=pltpu.VMEM)
    res = pl.pallas_call(
        body, name="bwd_proj", grid=(N_TB,),
        out_shape=(jax.ShapeDtypeStruct((SEQ, D_MODEL), F32),) + _scatter_out_shapes(geoms)
        + tuple(jax.ShapeDtypeStruct(s, F32) for s in SMALL_FULL),
        in_specs=[tok(PROJ_W), _resident((D_MODEL, PROJ_W)), tok(D_MODEL), _resident((1, D_MODEL)), tok(D_MODEL),
                  pl.BlockSpec(memory_space=pl.ANY)] + [vm] * n_s,
        out_specs=(tok(D_MODEL),) + _scatter_out_specs(geoms) + (vm,) * len(SMALL_FULL),
        scratch_shapes=[pltpu.VMEM((8, D_MODEL), F32)] + _scatter_scratch(geoms) + _small_scratch()
        + [pltpu.VMEM(s, F32) for s in SMALL_FULL],
        compiler_params=_cparams(("arbitrary",), collective=COLLECTIVE["bwd_proj"]),
    )(dproj, win_g, x, g1, dx2, gin_p, *small)
    return res


def _wgrad(name, a, b, tm=None, tn=None, hosted=()):
    m_w, n_w = a.shape[-1], b.shape[-1]
    tm = m_w if tm is None else tm
    tn = n_w if tn is None else tn
    n_steps = (m_w // tm) * (n_w // tn)
    geoms = [g for g, _ in hosted]
    n_h = len(hosted)

    def body(a_ref, b_ref, *rest):
        o_ref = rest[n_h]
        if n_h:
            rs = _Scatters(geoms, rest[:n_h], rest[n_h + 1:2 * n_h + 1], rest[2 * n_h + 1:])
            step = pl.program_id(0) * (n_w // tn) + pl.program_id(1)
            pl.when(step == 0)(rs.phase1)
            pl.when(step == 1)(rs.phase2)
            pl.when(step == 2)(rs.phase2b)
        o_ref[...] = _dot_tn(a_ref[...].astype(BF16), b_ref[...].astype(BF16)).astype(BF16)
        if n_h:
            pl.when(step == n_steps - 1)(rs.phase3)

    assert not n_h or n_steps >= 4
    res = pl.pallas_call(
        body, name=name, grid=(m_w // tm, n_w // tn),
        out_shape=(jax.ShapeDtypeStruct((m_w, n_w), BF16),) + _scatter_out_shapes(geoms),
        in_specs=[pl.BlockSpec((SEQ, tm), lambda i, j: (0, i)), pl.BlockSpec((SEQ, tn), lambda i, j: (0, j))]
        + [pl.BlockSpec(memory_space=pl.ANY)] * n_h,
        out_specs=(pl.BlockSpec((tm, tn), lambda i, j: (i, j)),) + _scatter_out_specs(geoms),
        scratch_shapes=_scatter_scratch(geoms),
        compiler_params=_cparams(("arbitrary", "arbitrary"), collective=COLLECTIVE[name]) if n_h else _cparams(("parallel", "parallel")),
    )(a, b, *[p for _, p in hosted])
    return tuple(res[:1 + n_h])


def _row_step(half_rows):
    return max(s for s in range(16, 177, 16) if half_rows % s == 0)


class _Scatter:
    def __init__(self, geom, partial, out, land1, mine, stage2, land2, comb, s1_send, s1_recv, s2_send, s2_recv, ld_sems):
        self.w, self.row0, self.shape = _geom(geom)
        self.partial, self.out, self.land1 = partial, out, land1
        self.mine, self.stage2, self.land2, self.comb = mine, stage2, land2, comb
        self.hr = self.shape[0] // 2
        self.step = _row_step(self.hr)
        self.s1_send, self.s1_recv, self.s2_send, self.s2_recv, self.ld_sems = s1_send, s1_recv, s2_send, s2_recv, ld_sems
        self.x, self.y, self.c = lax.axis_index("x"), lax.axis_index("y"), lax.axis_index("c")
        self.sibling = (self.x, self.y, 1 - self.c)
        self.chips = [(self.x, self.y), (1 - self.x, self.y), (self.x, 1 - self.y), (1 - self.x, 1 - self.y)]

    def block(self, px, py, pc):
        dev = 4 * px + 2 * py + pc
        if self.w == W_IN:
            return self.partial.at[:, pl.ds(pl.multiple_of(dev * IN_SHARD, 128), IN_SHARD)]
        if self.w == W_OUT:
            return self.partial.at[pl.ds(pl.multiple_of(dev * OUT_SHARD, 128), OUT_SHARD), :]
        if self.w == W_DOWN:
            return self.partial.at[pl.ds(pl.multiple_of(dev * DOWN_SHARD, 32), DOWN_SHARD), :]
        return self.partial.at[pl.ds(pl.multiple_of(dev * FF_SHARD + self.row0, 32), self.shape[0]), :]

    def copy1(self, k):
        return pltpu.make_async_remote_copy(
            src_ref=self.block(*self.chips[k], 1 - self.c), dst_ref=self.land1.at[k],
            send_sem=self.s1_send.at[k], recv_sem=self.s1_recv.at[k], device_id=self.sibling, device_id_type=MESH)

    STAGE2 = [(1, 0, 1), (3, 0, 1), (2, 1, 2), (3, 1, 2), (1, 1, 1), (2, 0, 2)]

    def copy2(self, j):
        blk, h, to = self.STAGE2[j]
        src = self.comb.at[j - 4] if j >= 4 else self.stage2.at[blk - 1, pl.ds(h * self.hr, self.hr), :]
        return pltpu.make_async_remote_copy(
            src_ref=src, dst_ref=self.land2.at[j], send_sem=self.s2_send.at[j], recv_sem=self.s2_recv.at[j],
            device_id=(*self.chips[to], self.c), device_id_type=MESH)

    def _rows(self, h=None):
        step = self.step
        lo, n = (0, self.shape[0]) if h is None else (h * self.hr, self.hr)
        return [pl.ds(r0, step) for r0 in range(lo, lo + n, step)]

    def load(self, k):
        return pltpu.make_async_copy(self.block(*self.chips[k], self.c), self.mine.at[k], self.ld_sems.at[k])

    def load_mine(self):
        for k in range(4):
            self.load(k).start()

    def phase1(self):
        for k in range(4):
            self.copy1(k).start()

    def phase2(self):
        for k in (3, 1, 2, 0):
            self.copy1(k).wait_recv()
            self.load(k).wait()
            for rs in self._rows():
                s = self.mine[k, rs, :].astype(F32) + self.land1[k, rs, :].astype(F32)
                if k == 0:
                    self.out[rs, :] = s
                else:
                    self.stage2[k - 1, rs, :] = s.astype(BF16)
            for j in {3: (1, 3), 1: (0,), 2: (2,), 0: ()}[k]:
                self.copy2(j).start()

    def phase2b(self):
        for j, got in ((4, 3), (5, 1)):
            blk, h, _ = self.STAGE2[j]
            self.copy2(got).wait_recv()
            for i, rs in enumerate(self._rows(h)):
                lr = pl.ds(i * self.step, self.step)
                self.comb[j - 4, lr, :] = (self.stage2[blk - 1, rs, :].astype(F32) + self.land2[got, lr, :].astype(F32)).astype(BF16)
            self.copy2(j).start()

    def phase3(self):
        for j in (0, 5, 4, 2):
            self.copy2(j).wait_recv()
        for h, (first, second) in enumerate(((0, 5), (4, 2))):
            for i, rs in enumerate(self._rows(h)):
                lr = pl.ds(i * self.step, self.step)
                self.out[rs, :] = (self.out[rs, :] + self.land2[first, lr, :].astype(F32)) + self.land2[second, lr, :].astype(F32)
        for k in range(4):
            self.copy1(k).wait_send()
        for j in range(6):
            self.copy2(j).wait_send()


def _geom(geom):
    if isinstance(geom, tuple):
        w, row0, rows = geom
        assert w == W_UP
        return w, row0, (rows, SHARD[w][1])
    return geom, 0, SHARD[geom]


N_SCATTER_SCRATCH = 10


def _scatter_out_shapes(geoms):
    return tuple(jax.ShapeDtypeStruct(_geom(g)[2], F32) for g in geoms)


def _scatter_out_specs(geoms):
    return (pl.BlockSpec(memory_space=pltpu.VMEM),) * len(geoms)


def _scatter_scratch(geoms):
    out = []
    for g in geoms:
        s = _geom(g)[2]
        hs = (s[0] // 2, s[1])
        out += [pltpu.VMEM((4,) + s, BF16), pltpu.VMEM((4,) + s, BF16), pltpu.VMEM((3,) + s, BF16), pltpu.VMEM((6,) + hs, BF16),
                pltpu.VMEM((2,) + hs, BF16),
                pltpu.SemaphoreType.DMA((4,)), pltpu.SemaphoreType.DMA((4,)), pltpu.SemaphoreType.DMA((6,)),
                pltpu.SemaphoreType.DMA((6,)), pltpu.SemaphoreType.DMA((4,))]
    return out


class _Scatters:
    def __init__(self, geoms, p_refs, out_refs, scratch):
        k = N_SCATTER_SCRATCH
        self.items = [_Scatter(g, p_refs[i], out_refs[i], *scratch[k * i:k * i + k]) for i, g in enumerate(geoms)]

    def phase1(self, diagonal=False):
        meet = _Meet(diagonal)
        meet.signal()
        for s in self.items:
            s.load_mine()
        meet.wait()
        for s in self.items:
            s.phase1()

    def phase2(self):
        for s in self.items:
            s.phase2()

    def phase2b(self):
        for s in self.items:
            s.phase2b()

    def phase3(self):
        for s in self.items:
            s.phase3()


PACK_W = 1024


SMALL_FULL = [(2, 8, PACK_W), (HEADS, CHUNK, CHUNK), (2, 8, D_FF)]
SMALL_HALF = [(s[0] // 2,) + s[1:] for s in SMALL_FULL]
N_SMALL_SCRATCH = 16


def _small_scratch():
    n_a = len(SMALL_FULL)
    return ([pltpu.VMEM(SMALL_FULL[0], F32)] + [pltpu.VMEM(s, F32) for s in SMALL_HALF] + [pltpu.VMEM(s, F32) for s in SMALL_HALF]
            + [pltpu.VMEM((3,) + s, F32) for s in SMALL_HALF]
            + [pltpu.SemaphoreType.DMA((n_a,)), pltpu.SemaphoreType.DMA((n_a,)), pltpu.SemaphoreType.DMA((n_a, 3)),
               pltpu.SemaphoreType.DMA((n_a, 3)), pltpu.SemaphoreType.DMA((n_a,)), pltpu.SemaphoreType.DMA((n_a,))])


class _SmallReduce:
    def __init__(self, ins, outs, scratch):
        self.ins, self.outs = ins, outs
        (self.pack, *rest) = scratch
        self.rxs, self.css, self.gs = rest[0:3], rest[3:6], rest[6:9]
        self.s1_send, self.s1_recv, self.s2_send, self.s2_recv, self.s3_send, self.s3_recv = rest[9:]
        self.x, self.y, self.c = lax.axis_index("x"), lax.axis_index("y"), lax.axis_index("c")
        self.sibling = (self.x, self.y, 1 - self.c)
        self.chips = [(1 - self.x, self.y), (self.x, 1 - self.y), (1 - self.x, 1 - self.y)]
        self.hl = [s[0] for s in SMALL_HALF]

    def half(self, ref, a, h):
        return ref.at[pl.ds(h * self.hl[a], self.hl[a])]

    def begin(self):
        dg1_ref, dg2_ref, dgf_ref, dgrn_ref, dlng_ref, dlnb_ref, dbs_ref, loss_ref, dws_ref, dcv_ref = self.ins
        pack, c = self.pack, self.c
        pack[...] = jnp.zeros_like(pack)
        pack[0, 0:1, :] = dg1_ref[0:1, :]
        pack[0, 1:2, :] = dg2_ref[0:1, :]
        pack[0, 2:3, :] = dgf_ref[0:1, :]
        pack[0, 3:4, 0:RET_W] = dgrn_ref[0:1, :]
        lsum = loss_ref[0, 0:1, :]
        for i in range(1, N_TB):
            lsum = lsum + loss_ref[i, 0:1, :]
        pack[0, 3:4, RET_W:RET_W + 128] = lsum
        pack[1, 0:HEADS, 0:128] = dlng_ref[0:HEADS, :]
        pack[1, 0:HEADS, 128:256] = dlnb_ref[0:HEADS, :]
        pack[1, 0:HEADS, 256:384] = dbs_ref[0:HEADS, :]
        self.srcs = [pack, dws_ref, dcv_ref]
        n_a = len(self.srcs)
        self.ex1 = [pltpu.make_async_remote_copy(src_ref=self.half(self.srcs[a], a, 1 - c), dst_ref=self.rxs[a],
                                                 send_sem=self.s1_send.at[a], recv_sem=self.s1_recv.at[a],
                                                 device_id=self.sibling, device_id_type=MESH) for a in range(n_a)]
        for cp in self.ex1:
            cp.start()
        self.ex2 = []
        for a in range(n_a):
            self.ex1[a].wait_recv()
            self.css[a][...] = self.half(self.srcs[a], a, c)[...] + self.rxs[a][...]
            for j, chip in enumerate(self.chips):
                cp = pltpu.make_async_remote_copy(src_ref=self.css[a], dst_ref=self.gs[a].at[j], send_sem=self.s2_send.at[a, j],
                                                  recv_sem=self.s2_recv.at[a, j], device_id=(*chip, c), device_id_type=MESH)
                cp.start()
                self.ex2.append(cp)

    def end(self):
        c, x, y = self.c, self.x, self.y
        ex3 = []
        for a in range(len(self.srcs)):
            css, gs, out = self.css[a], self.gs[a], self.outs[a]
            for j in range(3):
                self.ex2[3 * a + j].wait_recv()
            tot = None
            for q in range(4):
                k = jnp.where(x != (q >> 1), 1, 0) + jnp.where(y != (q & 1), 2, 0)
                term = jnp.where(k == 0, css[...], jnp.where(k == 1, gs[0], jnp.where(k == 2, gs[1], gs[2])))
                tot = term if tot is None else tot + term
            self.half(out, a, c)[...] = tot
            cp = pltpu.make_async_remote_copy(src_ref=self.half(out, a, c), dst_ref=self.half(out, a, c), send_sem=self.s3_send.at[a],
                                              recv_sem=self.s3_recv.at[a], device_id=self.sibling, device_id_type=MESH)
            cp.start()
            ex3.append(cp)
        for a in range(len(self.srcs)):
            out = self.outs[a]
            pltpu.make_async_remote_copy(src_ref=self.half(out, a, 1 - c), dst_ref=self.half(out, a, 1 - c), send_sem=self.s3_send.at[a],
                                         recv_sem=self.s3_recv.at[a], device_id=self.sibling, device_id_type=MESH).wait_recv()
        for cp in self.ex1 + self.ex2 + ex3:
            cp.wait_send()


def _adam_math(w, g, m, v):
    nm = ADAM_B1 * m + (1.0 - ADAM_B1) * g
    nv = ADAM_B2 * v + (1.0 - ADAM_B2) * (g * g)
    d = -ADAM_LR * ((nm / (1.0 - ADAM_B1 ** ADAM_STEP)) / (jnp.sqrt(nv / (1.0 - ADAM_B2 ** ADAM_STEP)) + ADAM_EPS) + ADAM_WD * w)
    return d, nm, nv


def _adamw(name, w, gs, m, v, rows):
    _, r, cdim = w.shape
    n_steps = r // rows
    half = gs[0].shape[0] // rows

    def body(w_ref, *rest):
        g_refs, (m_ref, v_ref, go_ref, d_ref, nm_ref, nv_ref) = rest[:len(gs)], rest[len(gs):]
        gg = g_refs[0][...]
        if len(gs) == 2:
            gg = jnp.where(pl.program_id(0) < half, gg, g_refs[1][...])
        go_ref[0] = gg
        d, nm, nv = _adam_math(w_ref[0], gg, m_ref[0], v_ref[0])
        d_ref[0], nm_ref[0], nv_ref[0] = d, nm, nv

    spec3 = pl.BlockSpec((1, rows, cdim), lambda i: (0, i, 0))
    if len(gs) == 1:
        g_specs = [pl.BlockSpec((rows, cdim), lambda i: (i, 0))]
    else:
        g_specs = [pl.BlockSpec((rows, cdim), lambda i: (jnp.minimum(i, half - 1), 0)),
                   pl.BlockSpec((rows, cdim), lambda i: (jnp.maximum(i - half, 0), 0))]
    sh = jax.ShapeDtypeStruct((1, r, cdim), F32)
    return pl.pallas_call(
        body, name=name, grid=(n_steps,), out_shape=(sh, sh, sh, sh),
        in_specs=[spec3] + g_specs + [spec3, spec3], out_specs=(spec3,) * 4,
        compiler_params=_cparams(("parallel",)),
    )(w, *gs, m, v)


def _adamw_small(rp, rws, rcv, gcw, params):
    n_p = len(params)

    def body(*refs):
        rp_ref, rws_ref, rcv_ref, gcw_ref = refs[:4]
        ins = refs[4:4 + 3 * n_p]
        outs = refs[4 + 3 * n_p:]
        outs[4 * n_p][...] = rp_ref[0, 3:4, RET_W:RET_W + 1]
        grads = [rp_ref[0, 0:1, :], rp_ref[0, 1:2, :], rp_ref[0, 2:3, :], rp_ref[0, 3:4, 0:RET_W],
                 rp_ref[1, 0:HEADS, 0:128], rp_ref[1, 0:HEADS, 128:256], rp_ref[1, 0:HEADS, 256:384],
                 rws_ref[...], gcw_ref[...], None]
        for p in range(n_p):
            w_ref, m_ref, v_ref = ins[3 * p:3 * p + 3]
            o = outs[4 * p:4 * p + 4]
            if p == n_p - 1:
                for hf in range(2):
                    cs = slice(hf * D_FF, (hf + 1) * D_FF)
                    g = rcv_ref[hf, 3:4, :]
                    res = (g,) + _adam_math(w_ref[:, cs], g, m_ref[:, cs], v_ref[:, cs])
                    for t in range(4):
                        o[t][:, cs] = res[t]
                continue
            lead = w_ref.ndim > grads[p].ndim
            rd = (lambda r: r[0]) if lead else (lambda r: r[...])
            res = (grads[p],) + _adam_math(rd(w_ref), grads[p], rd(m_ref), rd(v_ref))
            for t in range(4):
                if lead:
                    o[t][0] = res[t]
                else:
                    o[t][...] = res[t]

    vm = pl.BlockSpec(memory_space=pltpu.VMEM)
    flat = [a for tr in params for a in tr]
    out_shape = tuple(jax.ShapeDtypeStruct(tr[0].shape, F32) for tr in params for _ in range(4)) + (jax.ShapeDtypeStruct((1, 1), F32),)
    res = pl.pallas_call(
        body, name="adamw_small", out_shape=out_shape, in_specs=[vm] * (4 + len(flat)), out_specs=(vm,) * len(out_shape),
        compiler_params=_cparams(),
    )(rp, rws, rcv, gcw, *flat)
    return [res[4 * p:4 * p + 4] for p in range(n_p)], res[4 * n_p]


def kernel(x, mix_norm_g, w_in, ret_norm_g, sgu_ln_g, sgu_ln_b, sgu_w_s, sgu_b_s, w_out, ffn_norm_g, w_up, conv_w, conv_b, w_down, final_norm_g, loss_target, m_mix_norm_g, m_w_in, m_ret_norm_g, m_sgu_ln_g, m_sgu_ln_b, m_sgu_w_s, m_sgu_b_s, m_w_out, m_ffn_norm_g, m_w_up, m_conv_w, m_conv_b, m_w_down, m_final_norm_g, v_mix_norm_g, v_w_in, v_ret_norm_g, v_sgu_ln_g, v_sgu_ln_b, v_sgu_w_s, v_sgu_b_s, v_w_out, v_ffn_norm_g, v_w_up, v_conv_w, v_conv_b, v_w_down, v_final_norm_g):
    xs = x[0]
    tgt = loss_target[0]
    mask, qdec, kdec = _decay_tables()
    grn = ret_norm_g.reshape(1, RET_W)
    lng = sgu_ln_g.reshape(1, SGU_W)
    lnb = sgu_ln_b.reshape(1, SGU_W)
    ws = sgu_w_s[0]
    bsb = jnp.broadcast_to(sgu_b_s[0][:, :, None], (HEADS, CHUNK, HEAD_DIM))
    gf = final_norm_g.reshape(1, D_MODEL)
    me = 4 * lax.axis_index("x") + 2 * lax.axis_index("y") + lax.axis_index("c")
    tr = lambda a: jnp.transpose(a[0])[None]
    tr_cw = lambda a: jnp.transpose(a, (1, 0, 2))

    proj, h1, cos2, sin2, win_g, cw_sh, wout_g, wdn_g, su = _fwd_proj(
        xs, mix_norm_g, _rope_freq(), w_in[0], w_out[0], tr(w_up)[0], w_down[0], tr_cw(conv_w))
    cw_g = jnp.transpose(cw_sh, (1, 0, 2)).reshape(8, 2 * D_FF)
    x2, mixcat, o, sprev, wup_g = _fwd_mix(xs, proj, wout_g, grn, lng, lnb, ws, bsb, mask, qdec, kdec, su)
    h2, up_pre, u_conv, act, x3, loss_parts = _fwd_ffn(x2, ffn_norm_g, wup_g, cw_g, conv_b, wdn_g, gf, tgt)

    dx3, dpre, dx2, dgf, dg2, dcv = _bwd_ffn(x3, tgt, gf, x2, ffn_norm_g, up_pre, u_conv, wup_g, cw_g, wdn_g)
    band = 512
    (gdn_p,) = _wgrad("wgrad_down", act, dx3, tm=FF_TILE)
    (gout_p,) = _wgrad("wgrad_out", mixcat, dx2, tn=512)
    gup_p, g_dn = _wgrad("wgrad_up", dpre, h2, tm=FF_TILE, hosted=[(W_DOWN, gdn_p)])
    dproj, dgrn, dlng, dlnb, dws, dbs, g_up_a, g_out = _bwd_mix(
        dx2, proj, o, sprev, wout_g, grn, lng, lnb, ws, bsb, mask, qdec, kdec, cos2, sin2,
        [((W_UP, 0, band), gup_p), (W_OUT, gout_p)])
    gin_p, g_up_b = _wgrad("wgrad_in", h1, dproj, tn=768, hosted=[((W_UP, band, FF_SHARD - band), gup_p)])
    grad_x, g_in, rp, rws, rcv = _bwd_proj(dproj, win_g, xs, mix_norm_g, dx2, gin_p,
                                           (dg2, dgf, dgrn, dlng, dlnb, dbs, loss_parts, dws, dcv))
    gcw = tr_cw(lax.dynamic_slice(rcv, (me // (N_DEV // 2), 0, (me % (N_DEV // 2)) * FF_SHARD), (1, 3, FF_SHARD)))

    table = {}
    for name, w, gs, m, v, rows in (("w_in", w_in, [g_in], m_w_in, v_w_in, 256), ("w_out", w_out, [g_out], m_w_out, v_w_out, 128),
                                    ("w_up", tr(w_up), [g_up_a, g_up_b], tr(m_w_up), tr(v_w_up), 64),
                                    ("w_down", w_down, [g_dn], m_w_down, v_w_down, 88)):
        table[name] = _adamw("adamw_" + name, w, gs, m, v, rows)
    table["w_up"] = tuple(tr(a) for a in table["w_up"])
    row = lambda a: a.reshape(1, D_MODEL)
    names_small = ["mix_norm_g", "ffn_norm_g", "final_norm_g", "ret_norm_g", "sgu_ln_g", "sgu_ln_b", "sgu_b_s", "sgu_w_s",
                   "conv_w", "conv_b"]
    params = [(mix_norm_g, m_mix_norm_g, v_mix_norm_g), (ffn_norm_g, m_ffn_norm_g, v_ffn_norm_g),
              (row(final_norm_g), row(m_final_norm_g), row(v_final_norm_g)), (ret_norm_g, m_ret_norm_g, v_ret_norm_g),
              (sgu_ln_g, m_sgu_ln_g, v_sgu_ln_g), (sgu_ln_b, m_sgu_ln_b, v_sgu_ln_b), (sgu_b_s, m_sgu_b_s, v_sgu_b_s),
              (sgu_w_s, m_sgu_w_s, v_sgu_w_s), (tr_cw(conv_w), tr_cw(m_conv_w), tr_cw(v_conv_w)), (conv_b, m_conv_b, v_conv_b)]
    small, loss = _adamw_small(rp, rws, rcv, gcw, params)
    for n, res in zip(names_small, small):
        table[n] = res
    table["final_norm_g"] = tuple(a.reshape(D_MODEL) for a in table["final_norm_g"])
    table["conv_w"] = tuple(tr_cw(a) for a in table["conv_w"])

    order = ["mix_norm_g", "w_in", "ret_norm_g", "sgu_ln_g", "sgu_ln_b", "sgu_w_s", "sgu_b_s", "w_out", "ffn_norm_g", "w_up",
             "conv_w", "conv_b", "w_down", "final_norm_g"]
    outs = [loss.reshape(()), grad_x[None]]
    for col in range(4):
        outs += [table[n][col] for n in order]
    return tuple(outs)
```

```python
import functools
import math

import jax
import jax.numpy as jnp
import numpy as np
from jax import lax
from jax.experimental import pallas as pl
from jax.experimental.pallas import tpu as pltpu

F32 = jnp.float32
BF16 = jnp.bfloat16
MESH = pl.DeviceIdType.MESH

N_DEV = 8
SEQ = 2048
D_MODEL = 1024
CHUNK = 128
N_CHUNK = SEQ // CHUNK
HEADS = 4
HEAD_DIM = 128
RET_W = 512
SGU_W = 512
PROJ_W = 3072
D_FF = 2816
FF_SHARD = 704
FF_TILE = 1408
FF_TILES = ((0, 1536), (1536, 1280))
IN_SHARD = PROJ_W // N_DEV
OUT_SHARD = D_MODEL // N_DEV
DOWN_SHARD = D_FF // N_DEV
TM = 256
N_TB = SEQ // TM
FWD_PROJ_PASS_AT = 5
FWD_MIX_PASS_AT = 10
EPS = 1e-6
ROPE_BASE = 10000.0
K_SCALE = HEAD_DIM ** -0.5
INV_SQRT2 = 0.7071067811865476
INV_SQRT_2PI = 0.3989422804014327

ADAM_LR = 0.001
ADAM_B1 = 0.9
ADAM_B2 = 0.999
ADAM_EPS = 1e-08
ADAM_WD = 0.01
ADAM_STEP = 10

VMEM_LIMIT = 56 * 1024 * 1024


def _cparams(sem=None, vmem=VMEM_LIMIT, collective=None):
    return pltpu.CompilerParams(dimension_semantics=sem, vmem_limit_bytes=vmem, collective_id=collective)


COLLECTIVE = {name: k for k, name in enumerate(("fwd_proj", "fwd_mix", "wgrad_up", "bwd_mix", "wgrad_in", "bwd_proj"))}


class _Meet:
    def __init__(self, diagonal):
        x, y, c = lax.axis_index("x"), lax.axis_index("y"), lax.axis_index("c")
        self.peers = [(x, y, 1 - c), (1 - x, y, c), (x, 1 - y, c)] + ([(1 - x, 1 - y, c)] if diagonal else [])

    def signal(self):
        for peer in self.peers:
            pl.semaphore_signal(pltpu.get_barrier_semaphore(), inc=1, device_id=peer, device_id_type=MESH)

    def wait(self):
        pl.semaphore_wait(pltpu.get_barrier_semaphore(), len(self.peers))


def _resident(shape):
    nd = len(shape)
    return pl.BlockSpec(shape, lambda *_: (0,) * nd, pipeline_mode=pl.Buffered(1))


def _dot(a, b):
    return jnp.dot(a, b, preferred_element_type=F32)


def _dot_nt(a, b):
    return lax.dot_general(a, b, (((1,), (1,)), ((), ())), preferred_element_type=F32)


def _dot_tn(a, b):
    return lax.dot_general(a, b, (((0,), (0,)), ((), ())), preferred_element_type=F32)


def _sigmoid(x):
    return 1.0 / (1.0 + jnp.exp(-x))


def _gelu(x):
    return 0.5 * x * (1.0 + lax.erf(x * INV_SQRT2))


def _gelu_grad(x):
    return 0.5 * (1.0 + lax.erf(x * INV_SQRT2)) + x * (jnp.exp(-0.5 * x * x) * INV_SQRT_2PI)


def _rot(xh, cos2, sin2):
    return xh * cos2 + pltpu.roll(xh, HEAD_DIM // 2, 1) * sin2


def _rot_t(dh, cos2, sin2):
    return dh * cos2 + pltpu.roll(dh * sin2, HEAD_DIM // 2, 1)


def _rope_freq():
    half = HEAD_DIM // 2
    inv_freq = jnp.power(ROPE_BASE, -jnp.arange(half, dtype=F32) / half)
    return jnp.concatenate([inv_freq, inv_freq])[None, :]


def _rope_block(inv2, first_row):
    pos = (lax.broadcasted_iota(jnp.int32, (TM, HEAD_DIM), 0) + first_row).astype(F32)
    ang = pos * inv2
    sin = jnp.sin(ang)
    lane = lax.broadcasted_iota(jnp.int32, (TM, HEAD_DIM), 1)
    return jnp.cos(ang), jnp.where(lane < HEAD_DIM // 2, -sin, sin)


def _decay_tables():
    log_gamma = jnp.log(1.0 - jnp.power(2.0, -5.0 - jnp.arange(HEADS, dtype=F32)))
    pos = jnp.arange(CHUNK, dtype=F32)
    diff = pos[:, None] - pos[None, :]
    mask = jnp.where(diff >= 0.0, jnp.exp(log_gamma[:, None, None] * jnp.maximum(diff, 0.0)[None]), 0.0)
    k_decay = jnp.exp(log_gamma[:, None] * (CHUNK - 1.0 - pos)[None])
    q_decay = jnp.exp(log_gamma[:, None] * (pos + 1.0)[None])
    kd = jnp.broadcast_to(k_decay[:, :, None], (HEADS, CHUNK, HEAD_DIM))
    qd = jnp.broadcast_to(q_decay[:, :, None], (HEADS, CHUNK, HEAD_DIM))
    return mask.astype(F32), qd.astype(F32), kd.astype(F32)


def _chunk_decay():
    lg = np.log(np.float32(1.0) - np.power(np.float32(2.0), -5.0 - np.arange(HEADS, dtype=np.float32))).astype(np.float32)
    return [float(np.exp(lg[h] * np.float32(CHUNK))) for h in range(HEADS)]


W_IN, W_OUT, W_UP, W_DOWN, W_CONV = range(5)
GATHERED = {W_IN: ((D_MODEL, PROJ_W), BF16), W_OUT: ((D_MODEL, D_MODEL), BF16), W_UP: ((2 * D_FF, D_MODEL), BF16),
            W_DOWN: ((D_FF, D_MODEL), BF16), W_CONV: ((N_DEV, 8, FF_SHARD), F32)}
SHARD = {W_IN: (D_MODEL, IN_SHARD), W_OUT: (OUT_SHARD, D_MODEL), W_UP: (FF_SHARD, D_MODEL), W_DOWN: (DOWN_SHARD, D_MODEL),
         W_CONV: (8, FF_SHARD)}


class _Gather:
    N_SEMS = 9

    def __init__(self, ids, stages, gathered, send_sems, recv_sems, local_sems):
        self.ids, self.stages, self.gathered = ids, stages, gathered
        self.send_sems, self.recv_sems, self.local_sems = send_sems, recv_sems, local_sems
        self.x, self.y, self.c = lax.axis_index("x"), lax.axis_index("y"), lax.axis_index("c")
        self.me = (self.x, self.y, self.c)
        self.sibling = (self.x, self.y, 1 - self.c)
        self.chips = [(1 - self.x, self.y), (self.x, 1 - self.y), (1 - self.x, 1 - self.y)]

    def slot(self, n, px, py, pc):
        dev = 4 * px + 2 * py + pc
        w, g = self.ids[n], self.gathered[n]
        if w == W_IN:
            return g.at[:, pl.ds(pl.multiple_of(dev * IN_SHARD, 128), IN_SHARD)]
        if w == W_OUT:
            return g.at[pl.ds(pl.multiple_of(dev * OUT_SHARD, 128), OUT_SHARD), :]
        if w == W_DOWN:
            return g.at[pl.ds(pl.multiple_of(dev * DOWN_SHARD, 32), DOWN_SHARD), :]
        if w == W_UP:
            return g.at[pl.ds(pl.multiple_of(dev * FF_SHARD, 32), FF_SHARD), :]
        return g.at[dev]

    def half(self, n, px, py, pc, h):
        dev = 4 * px + 2 * py + pc
        w, g = self.ids[n], self.gathered[n]
        if w == W_IN:
            return g.at[pl.ds(h * (D_MODEL // 2), D_MODEL // 2), pl.ds(pl.multiple_of(dev * IN_SHARD, 128), IN_SHARD)]
        rows = SHARD[w][0] // 2
        return g.at[pl.ds(pl.multiple_of(dev * SHARD[w][0] + h * rows, 16), rows), :]

    def tree(self, n):
        return self.ids[n] != W_CONV

    def copy(self, n, k, block, to, src=None, h=None):
        ref = self.slot(n, *block) if h is None else self.half(n, *block, h)
        return pltpu.make_async_remote_copy(
            src_ref=ref if src is None else src, dst_ref=ref,
            send_sem=self.send_sems.at[n, k], recv_sem=self.recv_sems.at[n, k], device_id=to, device_id_type=MESH)

    def _mine(self):
        return [pltpu.make_async_copy(self.stages[n], self.slot(n, *self.me), self.local_sems.at[n]) for n in range(len(self.ids))]

    def _first(self):
        out = []
        for n in range(len(self.ids)):
            out.append(self.copy(n, 0, self.me, self.sibling, src=self.stages[n]))
            out += [self.copy(n, 1 + j, self.me, (*chip, self.c), src=self.stages[n])
                    for j, chip in enumerate(self.chips[:2] if self.tree(n) else self.chips)]
        return out

    def start(self):
        for cp in self._mine() + self._first():
            cp.start()

    def _passed(self, j):
        dev = (*self.chips[j], self.c)
        out = []
        for n in range(len(self.ids)):
            if not self.tree(n):
                out.append(self.copy(n, 4 + j, dev, self.sibling))
            elif j < 2:
                out += [self.copy(n, 3 + j, dev, (*self.chips[1 - j], self.c), h=j), self.copy(n, 5 + j, dev, self.sibling)]
            else:
                out += [self.copy(n, 7, dev, self.sibling, h=0), self.copy(n, 8, dev, self.sibling, h=1)]
        return out

    def near(self):
        for j in range(2):
            dev = (*self.chips[j], self.c)
            for n in range(len(self.ids)):
                self.copy(n, 1 + j, dev, self.me).wait_recv()
            for cp in self._passed(j):
                cp.start()

    def finish(self):
        dev = (*self.chips[2], self.c)
        for n in range(len(self.ids)):
            if self.tree(n):
                self.copy(n, 3, dev, self.me, h=0).wait_recv()
                self.copy(n, 4, dev, self.me, h=1).wait_recv()
            else:
                self.copy(n, 3, dev, self.me).wait_recv()
        for cp in self._passed(2):
            cp.start()
        for n in range(len(self.ids)):
            self.copy(n, 0, self.sibling, self.me).wait_recv()
            for j, chip in enumerate(self.chips):
                dev = (*chip, 1 - self.c)
                if not self.tree(n):
                    self.copy(n, 4 + j, dev, self.me).wait_recv()
                elif j < 2:
                    self.copy(n, 5 + j, dev, self.me).wait_recv()
                else:
                    self.copy(n, 7, dev, self.me, h=0).wait_recv()
                    self.copy(n, 8, dev, self.me, h=1).wait_recv()
        for cp in self._mine():
            cp.wait()
        for cp in self._first() + self._passed(0) + self._passed(1) + self._passed(2):
            cp.wait_send()


def _gather_scratch(n):
    return [pltpu.SemaphoreType.DMA((n, _Gather.N_SEMS)), pltpu.SemaphoreType.DMA((n, _Gather.N_SEMS)), pltpu.SemaphoreType.DMA((n,))]


def _gathered_shapes(ids):
    return tuple(jax.ShapeDtypeStruct(*GATHERED[w]) for w in ids)


def _fwd_proj(x, g1, inv2, w_in, w_out, w_up, w_down, conv_w):
    ids_a, ids_b = [W_IN, W_CONV], [W_OUT, W_DOWN]

    def body(x_ref, g_ref, inv_ref, in_hbm, out_hbm, up_hbm, dn_hbm, cw_ref,
             proj_ref, h1_ref, cos_ref, sin_ref, gin, gcw, gout, gdn, su_ref,
             w_vm, s_in, s_cw, s_out, s_dn, f_in, f_out, f_up, f_dn, ld_sems,
             a_send, a_recv, a_local, b_send, b_recv, b_local):
        ag_a = _Gather(ids_a, [s_in, s_cw], [gin, gcw], a_send, a_recv, a_local)
        ag_b = _Gather(ids_b, [s_out, s_dn], [gout, gdn], b_send, b_recv, b_local)

        @pl.when(pl.program_id(0) == 0)
        def _():
            meet = _Meet(diagonal=True)
            meet.signal()
            loads = [pltpu.make_async_copy(src, dst, ld_sems.at[i])
                     for i, (src, dst) in enumerate(((in_hbm, f_in), (out_hbm, f_out), (dn_hbm, f_dn), (up_hbm, f_up)))]
            for cp in loads:
                cp.start()
            s_cw[...] = jnp.zeros_like(s_cw)
            for k in range(3):
                s_cw[k:k + 1, :] = cw_ref[k]
            loads[0].wait()
            s_in[...] = f_in[...].astype(BF16)
            meet.wait()
            ag_a.start()
            loads[1].wait()
            s_out[...] = f_out[...].astype(BF16)
            loads[2].wait()
            s_dn[...] = f_dn[...].astype(BF16)
            ag_a.near()
            ag_b.start()
            loads[3].wait()
            su_ref[...] = f_up[...].astype(BF16)
            ag_a.finish()
            fill = pltpu.make_async_copy(gin, w_vm, ld_sems.at[4])
            fill.start()
            fill.wait()

        pl.when(pl.program_id(0) == FWD_PROJ_PASS_AT)(ag_b.near)

        xb = x_ref[...]
        r = lax.rsqrt(jnp.mean(xb * xb, axis=-1, keepdims=True) + EPS)
        h = ((xb * r) * g_ref[...]).astype(BF16)
        h1_ref[...] = h
        p = _dot(h, w_vm[...])
        c2, s2 = _rope_block(inv_ref[...], pl.program_id(0) * TM)
        cos_ref[...], sin_ref[...] = c2, s2
        for hd in range(HEADS):
            sl = slice(hd * HEAD_DIM, (hd + 1) * HEAD_DIM)
            proj_ref[:, sl] = _rot(p[:, sl], c2, s2)
            ks = slice(RET_W + hd * HEAD_DIM, RET_W + (hd + 1) * HEAD_DIM)
            proj_ref[:, ks] = _rot(p[:, ks], c2, s2) * K_SCALE
        proj_ref[:, 2 * RET_W:] = p[:, 2 * RET_W:]

        pl.when(pl.program_id(0) == N_TB - 1)(ag_b.finish)

    tok = lambda w: pl.BlockSpec((TM, w), lambda i: (i, 0))
    hbm = pl.BlockSpec(memory_space=pl.ANY)
    vm = pl.BlockSpec(memory_space=pltpu.VMEM)
    return pl.pallas_call(
        body, name="fwd_proj", grid=(N_TB,),
        out_shape=(jax.ShapeDtypeStruct((SEQ, PROJ_W), F32), jax.ShapeDtypeStruct((SEQ, D_MODEL), BF16),
                   jax.ShapeDtypeStruct((SEQ, HEAD_DIM), F32), jax.ShapeDtypeStruct((SEQ, HEAD_DIM), F32))
        + _gathered_shapes(ids_a + ids_b) + (jax.ShapeDtypeStruct(SHARD[W_UP], BF16),),
        in_specs=[tok(D_MODEL), _resident((1, D_MODEL)), _resident((1, HEAD_DIM)), hbm, hbm, hbm, hbm, vm],
        out_specs=(tok(PROJ_W), tok(D_MODEL), tok(HEAD_DIM), tok(HEAD_DIM), hbm, hbm, hbm, hbm, vm),
        scratch_shapes=[pltpu.VMEM((D_MODEL, PROJ_W), BF16), pltpu.VMEM(SHARD[W_IN], BF16), pltpu.VMEM(SHARD[W_CONV], F32),
                        pltpu.VMEM(SHARD[W_OUT], BF16), pltpu.VMEM(SHARD[W_DOWN], BF16),
                        pltpu.VMEM(SHARD[W_IN], F32), pltpu.VMEM(SHARD[W_OUT], F32), pltpu.VMEM(SHARD[W_UP], F32),
                        pltpu.VMEM(SHARD[W_DOWN], F32), pltpu.SemaphoreType.DMA((5,))]
        + _gather_scratch(len(ids_a)) + _gather_scratch(len(ids_b)),
        compiler_params=_cparams(("arbitrary",), collective=COLLECTIVE["fwd_proj"]),
    )(x, g1, inv2, w_in, w_out, w_up, w_down, conv_w)


def _causal(w):
    r = lax.broadcasted_iota(jnp.int32, (CHUNK, CHUNK), 0)
    c = lax.broadcasted_iota(jnp.int32, (CHUNK, CHUNK), 1)
    return jnp.where(r >= c, w, 0.0)


def _fwd_mix(x, proj, wout_g, grn, lng, lnb, ws, bsb, mask, qdec, kdec, su):
    cdec = _chunk_decay()
    ids = [W_UP]

    def body(x_ref, p_ref, w_ref, grn_ref, lng_ref, lnb_ref, ws_ref, bsb_ref, m_ref, qd_ref, kd_ref, su_ref,
             x2_ref, cat_ref, o_ref, sp_ref, gup, state, send_sems, recv_sems, local_sems):
        ag = _Gather(ids, [su_ref], [gup], send_sems, recv_sems, local_sems)

        @pl.when(pl.program_id(0) == 0)
        def _():
            meet = _Meet(diagonal=False)
            meet.signal()
            state[...] = jnp.zeros_like(state)
            meet.wait()
            ag.start()

        for h in range(HEADS):
            sl = slice(h * HEAD_DIM, (h + 1) * HEAD_DIM)
            q = p_ref[:, sl]
            k = p_ref[:, RET_W + h * HEAD_DIM:RET_W + (h + 1) * HEAD_DIM]
            v = p_ref[:, 2 * RET_W + h * HEAD_DIM:2 * RET_W + (h + 1) * HEAD_DIM]
            g = p_ref[:, 3 * RET_W + h * HEAD_DIM:3 * RET_W + (h + 1) * HEAD_DIM]
            qb, kb, vb = q.astype(BF16), k.astype(BF16), v.astype(BF16)
            a = _dot_nt(qb, kb) * m_ref[h]
            spb = state[h].astype(BF16)
            sp_ref[0, h] = spb
            o = _dot(a.astype(BF16), vb) + _dot((q * qd_ref[h]).astype(BF16), spb)
            state[h] = state[h] * cdec[h] + _dot_tn((k * kd_ref[h]).astype(BF16), vb)
            o_ref[:, sl] = o
            rinv = lax.rsqrt(jnp.mean(o * o, axis=-1, keepdims=True) + EPS)
            rn = (o * rinv) * grn_ref[:, sl]
            cat_ref[:, sl] = ((g * _sigmoid(g)) * rn).astype(BF16)
        for gi in range(HEADS):
            sl = slice(gi * HEAD_DIM, (gi + 1) * HEAD_DIM)
            u = p_ref[:, 4 * RET_W + gi * HEAD_DIM:4 * RET_W + (gi + 1) * HEAD_DIM]
            sv = p_ref[:, 4 * RET_W + SGU_W + gi * HEAD_DIM:4 * RET_W + SGU_W + (gi + 1) * HEAD_DIM]
            gv = _gelu(sv)
            xc = gv - jnp.mean(gv, axis=-1, keepdims=True)
            vn = (xc * lax.rsqrt(jnp.mean(xc * xc, axis=-1, keepdims=True) + EPS)) * lng_ref[:, sl] + lnb_ref[:, sl]
            mixed = _dot(_causal(ws_ref[gi]).astype(BF16), vn.astype(BF16)) + bsb_ref[gi]
            cat_ref[:, RET_W + gi * HEAD_DIM:RET_W + (gi + 1) * HEAD_DIM] = (_gelu(u) * mixed).astype(BF16)
        x2_ref[...] = x_ref[...] + _dot(cat_ref[...], w_ref[...])

        pl.when(pl.program_id(0) == FWD_MIX_PASS_AT)(ag.near)
        pl.when(pl.program_id(0) == N_CHUNK - 1)(ag.finish)

    ch = lambda w: pl.BlockSpec((CHUNK, w), lambda i: (i, 0))
    hcc = (HEADS, CHUNK, CHUNK)
    hbm = pl.BlockSpec(memory_space=pl.ANY)
    return pl.pallas_call(
        body, name="fwd_mix", grid=(N_CHUNK,),
        out_shape=(jax.ShapeDtypeStruct((SEQ, D_MODEL), F32), jax.ShapeDtypeStruct((SEQ, D_MODEL), BF16),
                   jax.ShapeDtypeStruct((SEQ, RET_W), F32), jax.ShapeDtypeStruct((N_CHUNK, HEADS, HEAD_DIM, HEAD_DIM), BF16))
        + _gathered_shapes(ids),
        in_specs=[ch(D_MODEL), ch(PROJ_W), _resident((D_MODEL, D_MODEL)), _resident((1, RET_W)), _resident((1, SGU_W)),
                  _resident((1, SGU_W)), _resident(hcc), _resident(hcc), _resident(hcc), _resident(hcc), _resident(hcc), hbm],
        out_specs=(ch(D_MODEL), ch(D_MODEL), ch(RET_W), pl.BlockSpec((1, HEADS, HEAD_DIM, HEAD_DIM), lambda i: (i, 0, 0, 0)), hbm),
        scratch_shapes=[pltpu.VMEM((HEADS, HEAD_DIM, HEAD_DIM), F32)] + _gather_scratch(len(ids)),
        compiler_params=_cparams(("arbitrary",), collective=COLLECTIVE["fwd_mix"]),
    )(x, proj, wout_g, grn, lng, lnb, ws, bsb, mask, qdec, kdec, su)


def _conv_taps(p, prev8):
    row = lax.broadcasted_iota(jnp.int32, p.shape, 0)
    p1 = jnp.where(row == 0, prev8[7:8, :], pltpu.roll(p, 1, 0))
    p2 = jnp.where(row == 0, prev8[6:7, :], jnp.where(row == 1, prev8[7:8, :], pltpu.roll(p, 2, 0)))
    return p1, p2


def _fwd_ffn(x2, g2, wup_g, cw_g, cb_g, wdn_g, gf, tgt):
    def body(x_ref, g_ref, wu_ref, cw_ref, cb_ref, wd_ref, gf_ref, t_ref, h2_ref, up_ref, u_ref, act_ref, x3_ref, loss_ref, carry):
        @pl.when(pl.program_id(0) == 0)
        def _():
            carry[...] = jnp.zeros_like(carry)

        xb = x_ref[...]
        r = lax.rsqrt(jnp.mean(xb * xb, axis=-1, keepdims=True) + EPS)
        h = ((xb * r) * g_ref[...]).astype(BF16)
        h2_ref[...] = h
        acc = xb
        for t0, tw in FF_TILES:
            u = []
            for c0 in (t0, D_FF + t0):
                cs = slice(c0, c0 + tw)
                p = _dot_nt(h, wu_ref[pl.ds(c0, tw), :])
                up_ref[:, cs] = p.astype(BF16)
                p1, p2 = _conv_taps(p, carry[:, cs])
                carry[:, cs] = p[TM - 8:, :]
                us = p2 * cw_ref[0:1, cs] + p1 * cw_ref[1:2, cs] + p * cw_ref[2:3, cs] + cb_ref[:, cs]
                u_ref[:, cs] = us.astype(BF16)
                u.append(us)
            a = ((u[0] * _sigmoid(u[0])) * u[1]).astype(BF16)
            act_ref[:, t0:t0 + tw] = a
            acc = acc + _dot(a, wd_ref[pl.ds(t0, tw), :])
        x3_ref[...] = acc
        r3 = lax.rsqrt(jnp.mean(acc * acc, axis=-1, keepdims=True) + EPS)
        diff = (acc * r3) * gf_ref[...] - t_ref[...]
        loss_ref[...] = jnp.full(loss_ref.shape, 0.5 * jnp.sum(jnp.mean(diff * diff, axis=-1)), F32)

    tok = lambda w: pl.BlockSpec((TM, w), lambda i: (i, 0))
    return pl.pallas_call(
        body, name="fwd_ffn", grid=(N_TB,),
        out_shape=(jax.ShapeDtypeStruct((SEQ, D_MODEL), BF16), jax.ShapeDtypeStruct((SEQ, 2 * D_FF), BF16),
                   jax.ShapeDtypeStruct((SEQ, 2 * D_FF), BF16),
                   jax.ShapeDtypeStruct((SEQ, D_FF), BF16), jax.ShapeDtypeStruct((SEQ, D_MODEL), F32),
                   jax.ShapeDtypeStruct((N_TB, 8, 128), F32)),
        in_specs=[tok(D_MODEL), _resident((1, D_MODEL)), _resident((2 * D_FF, D_MODEL)), _resident((8, 2 * D_FF)),
                  _resident((1, 2 * D_FF)), _resident((D_FF, D_MODEL)), _resident((1, D_MODEL)), tok(D_MODEL)],
        out_specs=(tok(D_MODEL), tok(2 * D_FF), tok(2 * D_FF), tok(D_FF), tok(D_MODEL),
                   pl.BlockSpec((1, 8, 128), lambda i: (i, 0, 0))),
        scratch_shapes=[pltpu.VMEM((8, 2 * D_FF), F32)],
        compiler_params=_cparams(("arbitrary",)),
    )(x2, g2, wup_g, cw_g, cb_g, wdn_g, gf, tgt)


def _bwd_ffn(x3, tgt, gf, x2, g2, up_pre, u_conv, wup_g, cw_g, wdn_g):
    def body(x3_ref, t_ref, gf_ref, x2_ref, g2_ref, up_ref, u_ref, wu_ref, cw_ref, wd_ref,
             dx3_ref, dpre_ref, dx2_ref, dgf_ref, dg2_ref, dcv_ref, nxt):
        i = pl.program_id(0)

        @pl.when(i == 0)
        def _():
            nxt[...] = jnp.zeros_like(nxt)
            dgf_ref[...] = jnp.zeros_like(dgf_ref)
            dg2_ref[...] = jnp.zeros_like(dg2_ref)
            dcv_ref[...] = jnp.zeros_like(dcv_ref)

        x3 = x3_ref[...]
        r3 = lax.rsqrt(jnp.mean(x3 * x3, axis=-1, keepdims=True) + EPS)
        xh3 = x3 * r3
        dy = (xh3 * gf_ref[...] - t_ref[...]) * (1.0 / D_MODEL)
        dgf_ref[0:1, :] += jnp.sum(dy * xh3, axis=0, keepdims=True)
        t3 = dy * gf_ref[...]
        dx3 = r3 * (t3 - xh3 * jnp.mean(t3 * xh3, axis=-1, keepdims=True))
        dx3b = dx3.astype(BF16)
        dx3_ref[...] = dx3b
        dh2 = jnp.zeros((TM, D_MODEL), F32)
        for t0, tw in FF_TILES:
            row = lax.broadcasted_iota(jnp.int32, (TM, tw), 0)
            ts = slice(t0, t0 + tw)
            dact = _dot_nt(dx3b, wd_ref[pl.ds(t0, tw), :])
            ua = u_ref[:, ts].astype(F32)
            ub = u_ref[:, D_FF + t0:D_FF + t0 + tw].astype(F32)
            sg = _sigmoid(ua)
            du = [dact * ub * (sg * (1.0 + ua * (1.0 - sg))), dact * (ua * sg)]
            for n in range(2):
                d = du[n]
                c0 = n * D_FF + t0
                cs = slice(c0, c0 + tw)
                nx = nxt[:, cs]
                n1 = jnp.where(row == TM - 1, nx[0:1, :], pltpu.roll(d, TM - 1, 0))
                n2 = jnp.where(row == TM - 2, nx[0:1, :], jnp.where(row == TM - 1, nx[1:2, :], pltpu.roll(d, TM - 2, 0)))
                nxt[:, cs] = d[0:8, :]
                dp = (d * cw_ref[2:3, cs] + n1 * cw_ref[1:2, cs] + n2 * cw_ref[0:1, cs]).astype(BF16)
                dpre_ref[:, cs] = dp
                p = up_ref[:, cs].astype(F32)
                dcv_ref[n, 0:1, ts] += jnp.sum(n2 * p, axis=0, keepdims=True)
                dcv_ref[n, 1:2, ts] += jnp.sum(n1 * p, axis=0, keepdims=True)
                dcv_ref[n, 2:3, ts] += jnp.sum(d * p, axis=0, keepdims=True)
                dcv_ref[n, 3:4, ts] += jnp.sum(d, axis=0, keepdims=True)
                dh2 = dh2 + _dot(dp, wu_ref[pl.ds(c0, tw), :])
        x2 = x2_ref[...]
        r2 = lax.rsqrt(jnp.mean(x2 * x2, axis=-1, keepdims=True) + EPS)
        xh2 = x2 * r2
        dg2_ref[0:1, :] += jnp.sum(dh2 * xh2, axis=0, keepdims=True)
        t2 = dh2 * g2_ref[...]
        dx2_ref[...] = dx3 + r2 * (t2 - xh2 * jnp.mean(t2 * xh2, axis=-1, keepdims=True))

    rev = lambda w: pl.BlockSpec((TM, w), lambda i: (N_TB - 1 - i, 0))
    acc = lambda s: pl.BlockSpec(s, lambda i: (0,) * len(s))
    return pl.pallas_call(
        body, name="bwd_ffn", grid=(N_TB,),
        out_shape=(jax.ShapeDtypeStruct((SEQ, D_MODEL), BF16), jax.ShapeDtypeStruct((SEQ, 2 * D_FF), BF16),
                   jax.ShapeDtypeStruct((SEQ, D_MODEL), F32), jax.ShapeDtypeStruct((8, D_MODEL), F32),
                   jax.ShapeDtypeStruct((8, D_MODEL), F32), jax.ShapeDtypeStruct((2, 8, D_FF), F32)),
        in_specs=[rev(D_MODEL), rev(D_MODEL), _resident((1, D_MODEL)), rev(D_MODEL), _resident((1, D_MODEL)), rev(2 * D_FF),
                  rev(2 * D_FF), _resident((2 * D_FF, D_MODEL)), _resident((8, 2 * D_FF)), _resident((D_FF, D_MODEL))],
        out_specs=(rev(D_MODEL), rev(2 * D_FF), rev(D_MODEL), acc((8, D_MODEL)), acc((8, D_MODEL)), acc((2, 8, D_FF))),
        scratch_shapes=[pltpu.VMEM((8, 2 * D_FF), F32)],
        compiler_params=_cparams(("arbitrary",)),
    )(x3, tgt, gf, x2, g2, up_pre, u_conv, wup_g, cw_g, wdn_g)


def _bwd_mix(dx2, proj, o, sprev, wout_g, grn, lng, lnb, ws, bsb, mask, qdec, kdec, cos2, sin2, hosted):
    cdec = _chunk_decay()
    geoms = [g for g, _ in hosted]
    n_h = len(hosted)

    def body(dx2_ref, p_ref, o_ref, sp_ref, w_ref, grn_ref, lng_ref, lnb_ref, ws_ref, bsb_ref, m_ref, qd_ref, kd_ref,
             cos_ref, sin_ref, *rest):
        dp_ref, dgrn_ref, dlng_ref, dlnb_ref, dws_ref, dbs_ref = rest[n_h:n_h + 6]
        dstate, dbs_acc = rest[2 * n_h + 6:2 * n_h + 8]
        i = pl.program_id(0)
        rs = _Scatters(geoms, rest[:n_h], rest[n_h + 6:2 * n_h + 6], rest[2 * n_h + 8:])
        pl.when(i == 0)(rs.phase1)
        pl.when(i == 3)(rs.phase2)
        pl.when(i == 6)(rs.phase2b)

        @pl.when(i == 0)
        def _():
            dstate[...] = jnp.zeros_like(dstate)
            dgrn_ref[...] = jnp.zeros_like(dgrn_ref)
            dlng_ref[...] = jnp.zeros_like(dlng_ref)
            dlnb_ref[...] = jnp.zeros_like(dlnb_ref)
            dws_ref[...] = jnp.zeros_like(dws_ref)
            dbs_ref[...] = jnp.zeros_like(dbs_ref)
            dbs_acc[...] = jnp.zeros_like(dbs_acc)

        dmix = _dot_nt(dx2_ref[...].astype(BF16), w_ref[...])
        for h in range(HEADS):
            sl = slice(h * HEAD_DIM, (h + 1) * HEAD_DIM)
            q = p_ref[:, sl]
            k = p_ref[:, RET_W + h * HEAD_DIM:RET_W + (h + 1) * HEAD_DIM]
            v = p_ref[:, 2 * RET_W + h * HEAD_DIM:2 * RET_W + (h + 1) * HEAD_DIM]
            g = p_ref[:, 3 * RET_W + h * HEAD_DIM:3 * RET_W + (h + 1) * HEAD_DIM]
            o = o_ref[:, sl]
            rinv = lax.rsqrt(jnp.mean(o * o, axis=-1, keepdims=True) + EPS)
            oh = o * rinv
            gr = grn_ref[:, sl]
            sg = _sigmoid(g)
            dret = dmix[:, sl]
            dp_ref[:, 3 * RET_W + h * HEAD_DIM:3 * RET_W + (h + 1) * HEAD_DIM] = (
                dret * (oh * gr) * (sg * (1.0 + g * (1.0 - sg)))).astype(BF16)
            drn = dret * (g * sg)
            dgrn_ref[0:1, sl] += jnp.sum(drn * oh, axis=0, keepdims=True)
            t = drn * gr
            do = rinv * (t - oh * jnp.mean(t * oh, axis=-1, keepdims=True))
            qb, kb, vb, dob = q.astype(BF16), k.astype(BF16), v.astype(BF16), do.astype(BF16)
            m = m_ref[h]
            ab = (_dot_nt(qb, kb) * m).astype(BF16)
            dab = (_dot_nt(dob, vb) * m).astype(BF16)
            spb = sp_ref[0, h]
            dsn = dstate[h]
            dsnb = dsn.astype(BF16)
            qdb = (q * qd_ref[h]).astype(BF16)
            kdb = (k * kd_ref[h]).astype(BF16)
            dq = _dot(dab, kb) + _dot_nt(dob, spb) * qd_ref[h]
            dk = _dot_tn(dab, qb) + _dot_nt(vb, dsnb) * kd_ref[h]
            dv = _dot_tn(ab, dob) + _dot(kdb, dsnb)
            dstate[h] = dsn * cdec[h] + _dot_tn(qdb, dob)
            c2, s2 = cos_ref[...], sin_ref[...]
            dp_ref[:, sl] = _rot_t(dq, c2, s2).astype(BF16)
            dp_ref[:, RET_W + h * HEAD_DIM:RET_W + (h + 1) * HEAD_DIM] = _rot_t(dk * K_SCALE, c2, s2).astype(BF16)
            dp_ref[:, 2 * RET_W + h * HEAD_DIM:2 * RET_W + (h + 1) * HEAD_DIM] = dv.astype(BF16)
        for gi in range(HEADS):
            sl = slice(gi * HEAD_DIM, (gi + 1) * HEAD_DIM)
            u = p_ref[:, 4 * RET_W + gi * HEAD_DIM:4 * RET_W + (gi + 1) * HEAD_DIM]
            sv = p_ref[:, 4 * RET_W + SGU_W + gi * HEAD_DIM:4 * RET_W + SGU_W + (gi + 1) * HEAD_DIM]
            gv = _gelu(sv)
            xc = gv - jnp.mean(gv, axis=-1, keepdims=True)
            rstd = lax.rsqrt(jnp.mean(xc * xc, axis=-1, keepdims=True) + EPS)
            xh = xc * rstd
            lg = lng_ref[:, sl]
            vnb = (xh * lg + lnb_ref[:, sl]).astype(BF16)
            wcb = _causal(ws_ref[gi]).astype(BF16)
            mixed = _dot(wcb, vnb) + bsb_ref[gi]
            dsgu = dmix[:, RET_W + gi * HEAD_DIM:RET_W + (gi + 1) * HEAD_DIM]
            dmixed = dsgu * _gelu(u)
            dmb = dmixed.astype(BF16)
            dws_ref[gi] += _causal(_dot_nt(dmb, vnb))
            dbs_acc[gi] += dmixed
            dvn = _dot_tn(wcb, dmb)
            dlng_ref[gi:gi + 1, :] += jnp.sum(dvn * xh, axis=0, keepdims=True)
            dlnb_ref[gi:gi + 1, :] += jnp.sum(dvn, axis=0, keepdims=True)
            dxh = dvn * lg
            dgv = rstd * (dxh - jnp.mean(dxh, axis=-1, keepdims=True) - xh * jnp.mean(dxh * xh, axis=-1, keepdims=True))
            dp_ref[:, 4 * RET_W + gi * HEAD_DIM:4 * RET_W + (gi + 1) * HEAD_DIM] = (dsgu * mixed * _gelu_grad(u)).astype(BF16)
            dp_ref[:, 4 * RET_W + SGU_W + gi * HEAD_DIM:4 * RET_W + SGU_W + (gi + 1) * HEAD_DIM] = (
                dgv * _gelu_grad(sv)).astype(BF16)

        @pl.when(i == N_CHUNK - 1)
        def _():
            for gi in range(HEADS):
                col = jnp.broadcast_to(jnp.sum(dbs_acc[gi], axis=-1, keepdims=True), (CHUNK, CHUNK))
                dbs_ref[gi:gi + 1, :] = jnp.transpose(col)[0:1, :]
            rs.phase3()

    rev = lambda w: pl.BlockSpec((CHUNK, w), lambda i: (N_CHUNK - 1 - i, 0))
    hcc = (HEADS, CHUNK, CHUNK)
    acc = lambda s: pl.BlockSpec(s, lambda i: (0,) * len(s))
    res = pl.pallas_call(
        body, name="bwd_mix", grid=(N_CHUNK,),
        out_shape=(jax.ShapeDtypeStruct((SEQ, PROJ_W), BF16), jax.ShapeDtypeStruct((8, RET_W), F32),
                   jax.ShapeDtypeStruct((8, HEAD_DIM), F32), jax.ShapeDtypeStruct((8, HEAD_DIM), F32),
                   jax.ShapeDtypeStruct(hcc, F32), jax.ShapeDtypeStruct((8, CHUNK), F32)) + _scatter_out_shapes(geoms),
        in_specs=[rev(D_MODEL), rev(PROJ_W), rev(RET_W),
                  pl.BlockSpec((1, HEADS, HEAD_DIM, HEAD_DIM), lambda i: (N_CHUNK - 1 - i, 0, 0, 0)),
                  _resident((D_MODEL, D_MODEL)), _resident((1, RET_W)), _resident((1, SGU_W)), _resident((1, SGU_W)),
                  _resident(hcc), _resident(hcc), _resident(hcc), _resident(hcc), _resident(hcc), rev(HEAD_DIM), rev(HEAD_DIM)]
        + [pl.BlockSpec(memory_space=pl.ANY)] * n_h,
        out_specs=(rev(PROJ_W), acc((8, RET_W)), acc((8, HEAD_DIM)), acc((8, HEAD_DIM)), acc(hcc), acc((8, CHUNK)))
        + _scatter_out_specs(geoms),
        scratch_shapes=[pltpu.VMEM((HEADS, HEAD_DIM, HEAD_DIM), F32), pltpu.VMEM((HEADS, CHUNK, CHUNK), F32)] + _scatter_scratch(geoms),
        compiler_params=_cparams(("arbitrary",), collective=COLLECTIVE["bwd_mix"]),
    )(dx2, proj, o, sprev, wout_g, grn, lng, lnb, ws, bsb, mask, qdec, kdec, cos2, sin2, *[p for _, p in hosted])
    return tuple(res[:6 + n_h])


def _bwd_proj(dproj, win_g, x, g1, dx2, gin_p, small):
    geoms = [W_IN]
    n_s = len(small)

    def body(dp_ref, w_ref, x_ref, g_ref, dx2_ref, gin_ref, *rest):
        small_refs = rest[:n_s]
        dx_ref, rs_out, rp_ref, rws_ref, rcv_ref, dg_ref = rest[n_s:n_s + 6]
        rs_scratch = rest[n_s + 6:n_s + 6 + N_SCATTER_SCRATCH]
        ar_scratch = rest[n_s + 6 + N_SCATTER_SCRATCH:]
        ar_res = ar_scratch[N_SMALL_SCRATCH:]
        ar = _SmallReduce((dg_ref,) + tuple(small_refs), ar_res, ar_scratch[:N_SMALL_SCRATCH])
        rs = _Scatters(geoms, [gin_ref], [rs_out], rs_scratch)
        pl.when(pl.program_id(0) == 0)(lambda: rs.phase1(diagonal=True))
        pl.when(pl.program_id(0) == 2)(rs.phase2)
        pl.when(pl.program_id(0) == 4)(rs.phase2b)

        @pl.when(pl.program_id(0) == 0)
        def _():
            dg_ref[...] = jnp.zeros_like(dg_ref)

        dh = _dot_nt(dp_ref[...], w_ref[...])
        xb = x_ref[...]
        r = lax.rsqrt(jnp.mean(xb * xb, axis=-1, keepdims=True) + EPS)
        xh = xb * r
        dg_ref[0:1, :] += jnp.sum(dh * xh, axis=0, keepdims=True)
        t = dh * g_ref[...]
        dx_ref[...] = dx2_ref[...] + r * (t - xh * jnp.mean(t * xh, axis=-1, keepdims=True))

        @pl.when(pl.program_id(0) == N_TB - 1)
        def _():
            ar.begin()
            rs.phase3()
            ar.end()
            for o_ref, r_ref in zip((rp_ref, rws_ref, rcv_ref), ar_res):
                o_ref[...] = r_ref[...]

    tok = lambda w: pl.BlockSpec((TM, w), lambda i: (i, 0))
    vm = pl.BlockSpec(memory_space=pltpu.VMEM)
    res = pl.pallas_call(
        body, name="bwd_proj", grid=(N_TB,),
        out_shape=(jax.ShapeDtypeStruct((SEQ, D_MODEL), F32),) + _scatter_out_shapes(geoms)
        + tuple(jax.ShapeDtypeStruct(s, F32) for s in SMALL_FULL),
        in_specs=[tok(PROJ_W), _resident((D_MODEL, PROJ_W)), tok(D_MODEL), _resident((1, D_MODEL)), tok(D_MODEL),
                  pl.BlockSpec(memory_space=pl.ANY)] + [vm] * n_s,
        out_specs=(tok(D_MODEL),) + _scatter_out_specs(geoms) + (vm,) * len(SMALL_FULL),
        scratch_shapes=[pltpu.VMEM((8, D_MODEL), F32)] + _scatter_scratch(geoms) + _small_scratch()
        + [pltpu.VMEM(s, F32) for s in SMALL_FULL],
        compiler_params=_cparams(("arbitrary",), collective=COLLECTIVE["bwd_proj"]),
    )(dproj, win_g, x, g1, dx2, gin_p, *small)
    return res


def _wgrad(name, a, b, tm=None, tn=None, hosted=()):
    m_w, n_w = a.shape[-1], b.shape[-1]
    tm = m_w if tm is None else tm
    tn = n_w if tn is None else tn
    n_steps = (m_w // tm) * (n_w // tn)
    geoms = [g for g, _ in hosted]
    n_h = len(hosted)

    def body(a_ref, b_ref, *rest):
        o_ref = rest[n_h]
        if n_h:
            rs = _Scatters(geoms, rest[:n_h], rest[n_h + 1:2 * n_h + 1], rest[2 * n_h + 1:])
            step = pl.program_id(0) * (n_w // tn) + pl.program_id(1)
            pl.when(step == 0)(rs.phase1)
            pl.when(step == 1)(rs.phase2)
            pl.when(step == 2)(rs.phase2b)
        o_ref[...] = _dot_tn(a_ref[...].astype(BF16), b_ref[...].astype(BF16)).astype(BF16)
        if n_h:
            pl.when(step == n_steps - 1)(rs.phase3)

    assert not n_h or n_steps >= 4
    res = pl.pallas_call(
        body, name=name, grid=(m_w // tm, n_w // tn),
        out_shape=(jax.ShapeDtypeStruct((m_w, n_w), BF16),) + _scatter_out_shapes(geoms),
        in_specs=[pl.BlockSpec((SEQ, tm), lambda i, j: (0, i)), pl.BlockSpec((SEQ, tn), lambda i, j: (0, j))]
        + [pl.BlockSpec(memory_space=pl.ANY)] * n_h,
        out_specs=(pl.BlockSpec((tm, tn), lambda i, j: (i, j)),) + _scatter_out_specs(geoms),
        scratch_shapes=_scatter_scratch(geoms),
        compiler_params=_cparams(("arbitrary", "arbitrary"), collective=COLLECTIVE[name]) if n_h else _cparams(("parallel", "parallel")),
    )(a, b, *[p for _, p in hosted])
    return tuple(res[:1 + n_h])


def _row_step(half_rows):
    return max(s for s in range(16, 177, 16) if half_rows % s == 0)


class _Scatter:
    def __init__(self, geom, partial, out, land1, mine, stage2, land2, comb, s1_send, s1_recv, s2_send, s2_recv, ld_sems):
        self.w, self.row0, self.shape = _geom(geom)
        self.partial, self.out, self.land1 = partial, out, land1
        self.mine, self.stage2, self.land2, self.comb = mine, stage2, land2, comb
        self.hr = self.shape[0] // 2
        self.step = _row_step(self.hr)
        self.s1_send, self.s1_recv, self.s2_send, self.s2_recv, self.ld_sems = s1_send, s1_recv, s2_send, s2_recv, ld_sems
        self.x, self.y, self.c = lax.axis_index("x"), lax.axis_index("y"), lax.axis_index("c")
        self.sibling = (self.x, self.y, 1 - self.c)
        self.chips = [(self.x, self.y), (1 - self.x, self.y), (self.x, 1 - self.y), (1 - self.x, 1 - self.y)]

    def block(self, px, py, pc):
        dev = 4 * px + 2 * py + pc
        if self.w == W_IN:
            return self.partial.at[:, pl.ds(pl.multiple_of(dev * IN_SHARD, 128), IN_SHARD)]
        if self.w == W_OUT:
            return self.partial.at[pl.ds(pl.multiple_of(dev * OUT_SHARD, 128), OUT_SHARD), :]
        if self.w == W_DOWN:
            return self.partial.at[pl.ds(pl.multiple_of(dev * DOWN_SHARD, 32), DOWN_SHARD), :]
        return self.partial.at[pl.ds(pl.multiple_of(dev * FF_SHARD + self.row0, 32), self.shape[0]), :]

    def copy1(self, k):
        return pltpu.make_async_remote_copy(
            src_ref=self.block(*self.chips[k], 1 - self.c), dst_ref=self.land1.at[k],
            send_sem=self.s1_send.at[k], recv_sem=self.s1_recv.at[k], device_id=self.sibling, device_id_type=MESH)

    STAGE2 = [(1, 0, 1), (3, 0, 1), (2, 1, 2), (3, 1, 2), (1, 1, 1), (2, 0, 2)]

    def copy2(self, j):
        blk, h, to = self.STAGE2[j]
        src = self.comb.at[j - 4] if j >= 4 else self.stage2.at[blk - 1, pl.ds(h * self.hr, self.hr), :]
        return pltpu.make_async_remote_copy(
            src_ref=src, dst_ref=self.land2.at[j], send_sem=self.s2_send.at[j], recv_sem=self.s2_recv.at[j],
            device_id=(*self.chips[to], self.c), device_id_type=MESH)

    def _rows(self, h=None):
        step = self.step
        lo, n = (0, self.shape[0]) if h is None else (h * self.hr, self.hr)
        return [pl.ds(r0, step) for r0 in range(lo, lo + n, step)]

    def load(self, k):
        return pltpu.make_async_copy(self.block(*self.chips[k], self.c), self.mine.at[k], self.ld_sems.at[k])

    def load_mine(self):
        for k in range(4):
            self.load(k).start()

    def phase1(self):
        for k in range(4):
            self.copy1(k).start()

    def phase2(self, k):
        self.copy1(k).wait_recv()
        self.load(k).wait()
        for rs in self._rows():
            s = self.mine[k, rs, :].astype(F32) + self.land1[k, rs, :].astype(F32)
            if k == 0:
                self.out[rs, :] = s
            else:
                self.stage2[k - 1, rs, :] = s.astype(BF16)
        for j in {3: (1, 3), 1: (0,), 2: (2,), 0: ()}[k]:
            self.copy2(j).start()

    def phase2b(self):
        for j, got in ((4, 3), (5, 1)):
            blk, h, _ = self.STAGE2[j]
            self.copy2(got).wait_recv()
            for i, rs in enumerate(self._rows(h)):
                lr = pl.ds(i * self.step, self.step)
                self.comb[j - 4, lr, :] = (self.stage2[blk - 1, rs, :].astype(F32) + self.land2[got, lr, :].astype(F32)).astype(BF16)
            self.copy2(j).start()

    def phase3(self):
        for j in (0, 5, 4, 2):
            self.copy2(j).wait_recv()
        for h, (first, second) in enumerate(((0, 5), (4, 2))):
            for i, rs in enumerate(self._rows(h)):
                lr = pl.ds(i * self.step, self.step)
                self.out[rs, :] = (self.out[rs, :] + self.land2[first, lr, :].astype(F32)) + self.land2[second, lr, :].astype(F32)
        for k in range(4):
            self.copy1(k).wait_send()
        for j in range(6):
            self.copy2(j).wait_send()


def _geom(geom):
    if isinstance(geom, tuple):
        w, row0, rows = geom
        assert w == W_UP
        return w, row0, (rows, SHARD[w][1])
    return geom, 0, SHARD[geom]


N_SCATTER_SCRATCH = 10


def _scatter_out_shapes(geoms):
    return tuple(jax.ShapeDtypeStruct(_geom(g)[2], F32) for g in geoms)


def _scatter_out_specs(geoms):
    return (pl.BlockSpec(memory_space=pltpu.VMEM),) * len(geoms)


def _scatter_scratch(geoms):
    out = []
    for g in geoms:
        s = _geom(g)[2]
        hs = (s[0] // 2, s[1])
        out += [pltpu.VMEM((4,) + s, BF16), pltpu.VMEM((4,) + s, BF16), pltpu.VMEM((3,) + s, BF16), pltpu.VMEM((6,) + hs, BF16),
                pltpu.VMEM((2,) + hs, BF16),
                pltpu.SemaphoreType.DMA((4,)), pltpu.SemaphoreType.DMA((4,)), pltpu.SemaphoreType.DMA((6,)),
                pltpu.SemaphoreType.DMA((6,)), pltpu.SemaphoreType.DMA((4,))]
    return out


class _Scatters:
    def __init__(self, geoms, p_refs, out_refs, scratch):
        k = N_SCATTER_SCRATCH
        self.items = [_Scatter(g, p_refs[i], out_refs[i], *scratch[k * i:k * i + k]) for i, g in enumerate(geoms)]

    def phase1(self, diagonal=False):
        meet = _Meet(diagonal)
        meet.signal()
        for s in self.items:
            s.load_mine()
        meet.wait()
        for s in self.items:
            s.phase1()

    def phase2(self):
        for k in (3, 1, 2, 0):
            for s in self.items:
                s.phase2(k)

    def phase2b(self):
        for s in self.items:
            s.phase2b()

    def phase3(self):
        for s in self.items:
            s.phase3()


PACK_W = 1024


SMALL_FULL = [(2, 8, PACK_W), (HEADS, CHUNK, CHUNK), (2, 8, D_FF)]
SMALL_HALF = [(s[0] // 2,) + s[1:] for s in SMALL_FULL]
N_SMALL_SCRATCH = 16


def _small_scratch():
    n_a = len(SMALL_FULL)
    return ([pltpu.VMEM(SMALL_FULL[0], F32)] + [pltpu.VMEM(s, F32) for s in SMALL_HALF] + [pltpu.VMEM(s, F32) for s in SMALL_HALF]
            + [pltpu.VMEM((3,) + s, F32) for s in SMALL_HALF]
            + [pltpu.SemaphoreType.DMA((n_a,)), pltpu.SemaphoreType.DMA((n_a,)), pltpu.SemaphoreType.DMA((n_a, 3)),
               pltpu.SemaphoreType.DMA((n_a, 3)), pltpu.SemaphoreType.DMA((n_a,)), pltpu.SemaphoreType.DMA((n_a,))])


class _SmallReduce:
    def __init__(self, ins, outs, scratch):
        self.ins, self.outs = ins, outs
        (self.pack, *rest) = scratch
        self.rxs, self.css, self.gs = rest[0:3], rest[3:6], rest[6:9]
        self.s1_send, self.s1_recv, self.s2_send, self.s2_recv, self.s3_send, self.s3_recv = rest[9:]
        self.x, self.y, self.c = lax.axis_index("x"), lax.axis_index("y"), lax.axis_index("c")
        self.sibling = (self.x, self.y, 1 - self.c)
        self.chips = [(1 - self.x, self.y), (self.x, 1 - self.y), (1 - self.x, 1 - self.y)]
        self.hl = [s[0] for s in SMALL_HALF]

    def half(self, ref, a, h):
        return ref.at[pl.ds(h * self.hl[a], self.hl[a])]

    def begin(self):
        dg1_ref, dg2_ref, dgf_ref, dgrn_ref, dlng_ref, dlnb_ref, dbs_ref, loss_ref, dws_ref, dcv_ref = self.ins
        pack, c = self.pack, self.c
        pack[...] = jnp.zeros_like(pack)
        pack[0, 0:1, :] = dg1_ref[0:1, :]
        pack[0, 1:2, :] = dg2_ref[0:1, :]
        pack[0, 2:3, :] = dgf_ref[0:1, :]
        pack[0, 3:4, 0:RET_W] = dgrn_ref[0:1, :]
        lsum = loss_ref[0, 0:1, :]
        for i in range(1, N_TB):
            lsum = lsum + loss_ref[i, 0:1, :]
        pack[0, 3:4, RET_W:RET_W + 128] = lsum
        pack[1, 0:HEADS, 0:128] = dlng_ref[0:HEADS, :]
        pack[1, 0:HEADS, 128:256] = dlnb_ref[0:HEADS, :]
        pack[1, 0:HEADS, 256:384] = dbs_ref[0:HEADS, :]
        self.srcs = [pack, dws_ref, dcv_ref]
        n_a = len(self.srcs)
        self.ex1 = [pltpu.make_async_remote_copy(src_ref=self.half(self.srcs[a], a, 1 - c), dst_ref=self.rxs[a],
                                                 send_sem=self.s1_send.at[a], recv_sem=self.s1_recv.at[a],
                                                 device_id=self.sibling, device_id_type=MESH) for a in range(n_a)]
        for cp in self.ex1:
            cp.start()
        self.ex2 = []
        for a in range(n_a):
            self.ex1[a].wait_recv()
            self.css[a][...] = self.half(self.srcs[a], a, c)[...] + self.rxs[a][...]
            for j, chip in enumerate(self.chips):
                cp = pltpu.make_async_remote_copy(src_ref=self.css[a], dst_ref=self.gs[a].at[j], send_sem=self.s2_send.at[a, j],
                                                  recv_sem=self.s2_recv.at[a, j], device_id=(*chip, c), device_id_type=MESH)
                cp.start()
                self.ex2.append(cp)

    def end(self):
        c, x, y = self.c, self.x, self.y
        ex3 = []
        for a in range(len(self.srcs)):
            css, gs, out = self.css[a], self.gs[a], self.outs[a]
            for j in range(3):
                self.ex2[3 * a + j].wait_recv()
            tot = None
            for q in range(4):
                k = jnp.where(x != (q >> 1), 1, 0) + jnp.where(y != (q & 1), 2, 0)
                term = jnp.where(k == 0, css[...], jnp.where(k == 1, gs[0], jnp.where(k == 2, gs[1], gs[2])))
                tot = term if tot is None else tot + term
            self.half(out, a, c)[...] = tot
            cp = pltpu.make_async_remote_copy(src_ref=self.half(out, a, c), dst_ref=self.half(out, a, c), send_sem=self.s3_send.at[a],
                                              recv_sem=self.s3_recv.at[a], device_id=self.sibling, device_id_type=MESH)
            cp.start()
            ex3.append(cp)
        for a in range(len(self.srcs)):
            out = self.outs[a]
            pltpu.make_async_remote_copy(src_ref=self.half(out, a, 1 - c), dst_ref=self.half(out, a, 1 - c), send_sem=self.s3_send.at[a],
                                         recv_sem=self.s3_recv.at[a], device_id=self.sibling, device_id_type=MESH).wait_recv()
        for cp in self.ex1 + self.ex2 + ex3:
            cp.wait_send()


def _adam_math(w, g, m, v):
    nm = ADAM_B1 * m + (1.0 - ADAM_B1) * g
    nv = ADAM_B2 * v + (1.0 - ADAM_B2) * (g * g)
    d = -ADAM_LR * ((nm / (1.0 - ADAM_B1 ** ADAM_STEP)) / (jnp.sqrt(nv / (1.0 - ADAM_B2 ** ADAM_STEP)) + ADAM_EPS) + ADAM_WD * w)
    return d, nm, nv


def _adamw(name, w, gs, m, v, rows):
    _, r, cdim = w.shape
    n_steps = r // rows
    half = gs[0].shape[0] // rows

    def body(w_ref, *rest):
        g_refs, (m_ref, v_ref, go_ref, d_ref, nm_ref, nv_ref) = rest[:len(gs)], rest[len(gs):]
        gg = g_refs[0][...]
        if len(gs) == 2:
            gg = jnp.where(pl.program_id(0) < half, gg, g_refs[1][...])
        go_ref[0] = gg
        d, nm, nv = _adam_math(w_ref[0], gg, m_ref[0], v_ref[0])
        d_ref[0], nm_ref[0], nv_ref[0] = d, nm, nv

    spec3 = pl.BlockSpec((1, rows, cdim), lambda i: (0, i, 0))
    if len(gs) == 1:
        g_specs = [pl.BlockSpec((rows, cdim), lambda i: (i, 0))]
    else:
        g_specs = [pl.BlockSpec((rows, cdim), lambda i: (jnp.minimum(i, half - 1), 0)),
                   pl.BlockSpec((rows, cdim), lambda i: (jnp.maximum(i - half, 0), 0))]
    sh = jax.ShapeDtypeStruct((1, r, cdim), F32)
    return pl.pallas_call(
        body, name=name, grid=(n_steps,), out_shape=(sh, sh, sh, sh),
        in_specs=[spec3] + g_specs + [spec3, spec3], out_specs=(spec3,) * 4,
        compiler_params=_cparams(("parallel",)),
    )(w, *gs, m, v)


def _adamw_small(rp, rws, rcv, gcw, params):
    n_p = len(params)

    def body(*refs):
        rp_ref, rws_ref, rcv_ref, gcw_ref = refs[:4]
        ins = refs[4:4 + 3 * n_p]
        outs = refs[4 + 3 * n_p:]
        outs[4 * n_p][...] = rp_ref[0, 3:4, RET_W:RET_W + 1]
        grads = [rp_ref[0, 0:1, :], rp_ref[0, 1:2, :], rp_ref[0, 2:3, :], rp_ref[0, 3:4, 0:RET_W],
                 rp_ref[1, 0:HEADS, 0:128], rp_ref[1, 0:HEADS, 128:256], rp_ref[1, 0:HEADS, 256:384],
                 rws_ref[...], gcw_ref[...], None]
        for p in range(n_p):
            w_ref, m_ref, v_ref = ins[3 * p:3 * p + 3]
            o = outs[4 * p:4 * p + 4]
            if p == n_p - 1:
                for hf in range(2):
                    cs = slice(hf * D_FF, (hf + 1) * D_FF)
                    g = rcv_ref[hf, 3:4, :]
                    res = (g,) + _adam_math(w_ref[:, cs], g, m_ref[:, cs], v_ref[:, cs])
                    for t in range(4):
                        o[t][:, cs] = res[t]
                continue
            lead = w_ref.ndim > grads[p].ndim
            rd = (lambda r: r[0]) if lead else (lambda r: r[...])
            res = (grads[p],) + _adam_math(rd(w_ref), grads[p], rd(m_ref), rd(v_ref))
            for t in range(4):
                if lead:
                    o[t][0] = res[t]
                else:
                    o[t][...] = res[t]

    vm = pl.BlockSpec(memory_space=pltpu.VMEM)
    flat = [a for tr in params for a in tr]
    out_shape = tuple(jax.ShapeDtypeStruct(tr[0].shape, F32) for tr in params for _ in range(4)) + (jax.ShapeDtypeStruct((1, 1), F32),)
    res = pl.pallas_call(
        body, name="adamw_small", out_shape=out_shape, in_specs=[vm] * (4 + len(flat)), out_specs=(vm,) * len(out_shape),
        compiler_params=_cparams(),
    )(rp, rws, rcv, gcw, *flat)
    return [res[4 * p:4 * p + 4] for p in range(n_p)], res[4 * n_p]


def kernel(x, mix_norm_g, w_in, ret_norm_g, sgu_ln_g, sgu_ln_b, sgu_w_s, sgu_b_s, w_out, ffn_norm_g, w_up, conv_w, conv_b, w_down, final_norm_g, loss_target, m_mix_norm_g, m_w_in, m_ret_norm_g, m_sgu_ln_g, m_sgu_ln_b, m_sgu_w_s, m_sgu_b_s, m_w_out, m_ffn_norm_g, m_w_up, m_conv_w, m_conv_b, m_w_down, m_final_norm_g, v_mix_norm_g, v_w_in, v_ret_norm_g, v_sgu_ln_g, v_sgu_ln_b, v_sgu_w_s, v_sgu_b_s, v_w_out, v_ffn_norm_g, v_w_up, v_conv_w, v_conv_b, v_w_down, v_final_norm_g):
    xs = x[0]
    tgt = loss_target[0]
    mask, qdec, kdec = _decay_tables()
    grn = ret_norm_g.reshape(1, RET_W)
    lng = sgu_ln_g.reshape(1, SGU_W)
    lnb = sgu_ln_b.reshape(1, SGU_W)
    ws = sgu_w_s[0]
    bsb = jnp.broadcast_to(sgu_b_s[0][:, :, None], (HEADS, CHUNK, HEAD_DIM))
    gf = final_norm_g.reshape(1, D_MODEL)
    me = 4 * lax.axis_index("x") + 2 * lax.axis_index("y") + lax.axis_index("c")
    tr = lambda a: jnp.transpose(a[0])[None]
    tr_cw = lambda a: jnp.transpose(a, (1, 0, 2))

    proj, h1, cos2, sin2, win_g, cw_sh, wout_g, wdn_g, su = _fwd_proj(
        xs, mix_norm_g, _rope_freq(), w_in[0], w_out[0], tr(w_up)[0], w_down[0], tr_cw(conv_w))
    cw_g = jnp.transpose(cw_sh, (1, 0, 2)).reshape(8, 2 * D_FF)
    x2, mixcat, o, sprev, wup_g = _fwd_mix(xs, proj, wout_g, grn, lng, lnb, ws, bsb, mask, qdec, kdec, su)
    h2, up_pre, u_conv, act, x3, loss_parts = _fwd_ffn(x2, ffn_norm_g, wup_g, cw_g, conv_b, wdn_g, gf, tgt)

    dx3, dpre, dx2, dgf, dg2, dcv = _bwd_ffn(x3, tgt, gf, x2, ffn_norm_g, up_pre, u_conv, wup_g, cw_g, wdn_g)
    band = 512
    (gdn_p,) = _wgrad("wgrad_down", act, dx3, tm=FF_TILE)
    (gout_p,) = _wgrad("wgrad_out", mixcat, dx2, tn=512)
    gup_p, g_dn = _wgrad("wgrad_up", dpre, h2, tm=FF_TILE, hosted=[(W_DOWN, gdn_p)])
    dproj, dgrn, dlng, dlnb, dws, dbs, g_up_a, g_out = _bwd_mix(
        dx2, proj, o, sprev, wout_g, grn, lng, lnb, ws, bsb, mask, qdec, kdec, cos2, sin2,
        [((W_UP, 0, band), gup_p), (W_OUT, gout_p)])
    gin_p, g_up_b = _wgrad("wgrad_in", h1, dproj, tn=768, hosted=[((W_UP, band, FF_SHARD - band), gup_p)])
    grad_x, g_in, rp, rws, rcv = _bwd_proj(dproj, win_g, xs, mix_norm_g, dx2, gin_p,
                                           (dg2, dgf, dgrn, dlng, dlnb, dbs, loss_parts, dws, dcv))
    gcw = tr_cw(lax.dynamic_slice(rcv, (me // (N_DEV // 2), 0, (me % (N_DEV // 2)) * FF_SHARD), (1, 3, FF_SHARD)))

    table = {}
    for name, w, gs, m, v, rows in (("w_in", w_in, [g_in], m_w_in, v_w_in, 256), ("w_out", w_out, [g_out], m_w_out, v_w_out, 128),
                                    ("w_up", tr(w_up), [g_up_a, g_up_b], tr(m_w_up), tr(v_w_up), 64),
                                    ("w_down", w_down, [g_dn], m_w_down, v_w_down, 88)):
        table[name] = _adamw("adamw_" + name, w, gs, m, v, rows)
    table["w_up"] = tuple(tr(a) for a in table["w_up"])
    row = lambda a: a.reshape(1, D_MODEL)
    names_small = ["mix_norm_g", "ffn_norm_g", "final_norm_g", "ret_norm_g", "sgu_ln_g", "sgu_ln_b", "sgu_b_s", "sgu_w_s",
                   "conv_w", "conv_b"]
    params = [(mix_norm_g, m_mix_norm_g, v_mix_norm_g), (ffn_norm_g, m_ffn_norm_g, v_ffn_norm_g),
              (row(final_norm_g), row(m_final_norm_g), row(v_final_norm_g)), (ret_norm_g, m_ret_norm_g, v_ret_norm_g),
              (sgu_ln_g, m_sgu_ln_g, v_sgu_ln_g), (sgu_ln_b, m_sgu_ln_b, v_sgu_ln_b), (sgu_b_s, m_sgu_b_s, v_sgu_b_s),
              (sgu_w_s, m_sgu_w_s, v_sgu_w_s), (tr_cw(conv_w), tr_cw(m_conv_w), tr_cw(v_conv_w)), (conv_b, m_conv_b, v_conv_b)]
    small, loss = _adamw_small(rp, rws, rcv, gcw, params)
    for n, res in zip(names_small, small):
        table[n] = res
    table["final_norm_g"] = tuple(a.reshape(D_MODEL) for a in table["final_norm_g"])
    table["conv_w"] = tuple(tr_cw(a) for a in table["conv_w"])

    order = ["mix_norm_g", "w_in", "ret_norm_g", "sgu_ln_g", "sgu_ln_b", "sgu_w_s", "sgu_b_s", "w_out", "ffn_norm_g", "w_up",
             "conv_w", "conv_b", "w_down", "final_norm_g"]
    outs = [loss.reshape(()), grad_x[None]]
    for col in range(4):
        outs += [table[n][col] for n in order]
    return tuple(outs)
```

```python
import functools
import math

import jax
import jax.numpy as jnp
import numpy as np
from jax import lax
from jax.experimental import pallas as pl
from jax.experimental.pallas import tpu as pltpu

F32 = jnp.float32
BF16 = jnp.bfloat16
MESH = pl.DeviceIdType.MESH

N_DEV = 8
SEQ = 2048
D_MODEL = 1024
CHUNK = 128
N_CHUNK = SEQ // CHUNK
HEADS = 4
HEAD_DIM = 128
RET_W = 512
SGU_W = 512
PROJ_W = 3072
D_FF = 2816
FF_SHARD = 704
FF_TILE = 1408
FF_TILES = ((0, 1536), (1536, 1280))
IN_SHARD = PROJ_W // N_DEV
OUT_SHARD = D_MODEL // N_DEV
DOWN_SHARD = D_FF // N_DEV
TM = 256
N_TB = SEQ // TM
FWD_PROJ_PASS_AT = 5
FWD_MIX_PASS_AT = 10
EPS = 1e-6
ROPE_BASE = 10000.0
K_SCALE = HEAD_DIM ** -0.5
INV_SQRT2 = 0.7071067811865476
INV_SQRT_2PI = 0.3989422804014327

ADAM_LR = 0.001
ADAM_B1 = 0.9
ADAM_B2 = 0.999
ADAM_EPS = 1e-08
ADAM_WD = 0.01
ADAM_STEP = 10

VMEM_LIMIT = 56 * 1024 * 1024


def _cparams(sem=None, vmem=VMEM_LIMIT, collective=None):
    return pltpu.CompilerParams(dimension_semantics=sem, vmem_limit_bytes=vmem, collective_id=collective)


COLLECTIVE = {name: k for k, name in enumerate(("fwd_proj", "fwd_mix", "wgrad_up", "bwd_mix", "wgrad_in", "bwd_proj"))}


class _Meet:
    def __init__(self, diagonal):
        x, y, c = lax.axis_index("x"), lax.axis_index("y"), lax.axis_index("c")
        self.peers = [(x, y, 1 - c), (1 - x, y, c), (x, 1 - y, c)] + ([(1 - x, 1 - y, c)] if diagonal else [])

    def signal(self):
        for peer in self.peers:
            pl.semaphore_signal(pltpu.get_barrier_semaphore(), inc=1, device_id=peer, device_id_type=MESH)

    def wait(self):
        pl.semaphore_wait(pltpu.get_barrier_semaphore(), len(self.peers))


def _resident(shape):
    nd = len(shape)
    return pl.BlockSpec(shape, lambda *_: (0,) * nd, pipeline_mode=pl.Buffered(1))


def _dot(a, b):
    return jnp.dot(a, b, preferred_element_type=F32)


def _dot_nt(a, b):
    return lax.dot_general(a, b, (((1,), (1,)), ((), ())), preferred_element_type=F32)


def _dot_tn(a, b):
    return lax.dot_general(a, b, (((0,), (0,)), ((), ())), preferred_element_type=F32)


def _sigmoid(x):
    return 1.0 / (1.0 + jnp.exp(-x))


def _gelu(x):
    return 0.5 * x * (1.0 + lax.erf(x * INV_SQRT2))


def _gelu_grad(x):
    return 0.5 * (1.0 + lax.erf(x * INV_SQRT2)) + x * (jnp.exp(-0.5 * x * x) * INV_SQRT_2PI)


def _rot(xh, cos2, sin2):
    return xh * cos2 + pltpu.roll(xh, HEAD_DIM // 2, 1) * sin2


def _rot_t(dh, cos2, sin2):
    return dh * cos2 + pltpu.roll(dh * sin2, HEAD_DIM // 2, 1)


def _rope_freq():
    half = HEAD_DIM // 2
    inv_freq = jnp.power(ROPE_BASE, -jnp.arange(half, dtype=F32) / half)
    return jnp.concatenate([inv_freq, inv_freq])[None, :]


def _rope_block(inv2, first_row):
    pos = (lax.broadcasted_iota(jnp.int32, (TM, HEAD_DIM), 0) + first_row).astype(F32)
    ang = pos * inv2
    sin = jnp.sin(ang)
    lane = lax.broadcasted_iota(jnp.int32, (TM, HEAD_DIM), 1)
    return jnp.cos(ang), jnp.where(lane < HEAD_DIM // 2, -sin, sin)


def _decay_tables():
    log_gamma = jnp.log(1.0 - jnp.power(2.0, -5.0 - jnp.arange(HEADS, dtype=F32)))
    pos = jnp.arange(CHUNK, dtype=F32)
    diff = pos[:, None] - pos[None, :]
    mask = jnp.where(diff >= 0.0, jnp.exp(log_gamma[:, None, None] * jnp.maximum(diff, 0.0)[None]), 0.0)
    k_decay = jnp.exp(log_gamma[:, None] * (CHUNK - 1.0 - pos)[None])
    q_decay = jnp.exp(log_gamma[:, None] * (pos + 1.0)[None])
    kd = jnp.broadcast_to(k_decay[:, :, None], (HEADS, CHUNK, HEAD_DIM))
    qd = jnp.broadcast_to(q_decay[:, :, None], (HEADS, CHUNK, HEAD_DIM))
    return mask.astype(F32), qd.astype(F32), kd.astype(F32)


def _chunk_decay():
    lg = np.log(np.float32(1.0) - np.power(np.float32(2.0), -5.0 - np.arange(HEADS, dtype=np.float32))).astype(np.float32)
    return [float(np.exp(lg[h] * np.float32(CHUNK))) for h in range(HEADS)]


W_IN, W_OUT, W_UP, W_DOWN, W_CONV = range(5)
GATHERED = {W_IN: ((D_MODEL, PROJ_W), BF16), W_OUT: ((D_MODEL, D_MODEL), BF16), W_UP: ((2 * D_FF, D_MODEL), BF16),
            W_DOWN: ((D_FF, D_MODEL), BF16), W_CONV: ((N_DEV, 8, FF_SHARD), F32)}
SHARD = {W_IN: (D_MODEL, IN_SHARD), W_OUT: (OUT_SHARD, D_MODEL), W_UP: (FF_SHARD, D_MODEL), W_DOWN: (DOWN_SHARD, D_MODEL),
         W_CONV: (8, FF_SHARD)}


class _Gather:
    N_SEMS = 9

    def __init__(self, ids, stages, gathered, send_sems, recv_sems, local_sems):
        self.ids, self.stages, self.gathered = ids, stages, gathered
        self.send_sems, self.recv_sems, self.local_sems = send_sems, recv_sems, local_sems
        self.x, self.y, self.c = lax.axis_index("x"), lax.axis_index("y"), lax.axis_index("c")
        self.me = (self.x, self.y, self.c)
        self.sibling = (self.x, self.y, 1 - self.c)
        self.chips = [(1 - self.x, self.y), (self.x, 1 - self.y), (1 - self.x, 1 - self.y)]

    def slot(self, n, px, py, pc):
        dev = 4 * px + 2 * py + pc
        w, g = self.ids[n], self.gathered[n]
        if w == W_IN:
            return g.at[:, pl.ds(pl.multiple_of(dev * IN_SHARD, 128), IN_SHARD)]
        if w == W_OUT:
            return g.at[pl.ds(pl.multiple_of(dev * OUT_SHARD, 128), OUT_SHARD), :]
        if w == W_DOWN:
            return g.at[pl.ds(pl.multiple_of(dev * DOWN_SHARD, 32), DOWN_SHARD), :]
        if w == W_UP:
            return g.at[pl.ds(pl.multiple_of(dev * FF_SHARD, 32), FF_SHARD), :]
        return g.at[dev]

    def half(self, n, px, py, pc, h):
        dev = 4 * px + 2 * py + pc
        w, g = self.ids[n], self.gathered[n]
        if w == W_IN:
            return g.at[pl.ds(h * (D_MODEL // 2), D_MODEL // 2), pl.ds(pl.multiple_of(dev * IN_SHARD, 128), IN_SHARD)]
        rows = SHARD[w][0] // 2
        return g.at[pl.ds(pl.multiple_of(dev * SHARD[w][0] + h * rows, 16), rows), :]

    def tree(self, n):
        return self.ids[n] != W_CONV

    def copy(self, n, k, block, to, src=None, h=None):
        ref = self.slot(n, *block) if h is None else self.half(n, *block, h)
        return pltpu.make_async_remote_copy(
            src_ref=ref if src is None else src, dst_ref=ref,
            send_sem=self.send_sems.at[n, k], recv_sem=self.recv_sems.at[n, k], device_id=to, device_id_type=MESH)

    def _mine(self):
        return [pltpu.make_async_copy(self.stages[n], self.slot(n, *self.me), self.local_sems.at[n]) for n in range(len(self.ids))]

    def _first(self):
        out = []
        for n in range(len(self.ids)):
            out.append(self.copy(n, 0, self.me, self.sibling, src=self.stages[n]))
            out += [self.copy(n, 1 + j, self.me, (*chip, self.c), src=self.stages[n])
                    for j, chip in enumerate(self.chips[:2] if self.tree(n) else self.chips)]
        return out

    def start(self):
        for cp in self._mine() + self._first():
            cp.start()

    def _passed(self, j):
        dev = (*self.chips[j], self.c)
        out = []
        for n in range(len(self.ids)):
            if not self.tree(n):
                out.append(self.copy(n, 4 + j, dev, self.sibling))
            elif j < 2:
                out += [self.copy(n, 3 + j, dev, (*self.chips[1 - j], self.c), h=j), self.copy(n, 5 + j, dev, self.sibling)]
            else:
                out += [self.copy(n, 7, dev, self.sibling, h=0), self.copy(n, 8, dev, self.sibling, h=1)]
        return out

    def near(self):
        for j in range(2):
            dev = (*self.chips[j], self.c)
            for n in range(len(self.ids)):
                self.copy(n, 1 + j, dev, self.me).wait_recv()
            for cp in self._passed(j):
                cp.start()

    def finish(self):
        dev = (*self.chips[2], self.c)
        for n in range(len(self.ids)):
            if self.tree(n):
                self.copy(n, 3, dev, self.me, h=0).wait_recv()
                self.copy(n, 4, dev, self.me, h=1).wait_recv()
            else:
                self.copy(n, 3, dev, self.me).wait_recv()
        for cp in self._passed(2):
            cp.start()
        for n in range(len(self.ids)):
            self.copy(n, 0, self.sibling, self.me).wait_recv()
            for j, chip in enumerate(self.chips):
                dev = (*chip, 1 - self.c)
                if not self.tree(n):
                    self.copy(n, 4 + j, dev, self.me).wait_recv()
                elif j < 2:
                    self.copy(n, 5 + j, dev, self.me).wait_recv()
                else:
                    self.copy(n, 7, dev, self.me, h=0).wait_recv()
                    self.copy(n, 8, dev, self.me, h=1).wait_recv()
        for cp in self._mine():
            cp.wait()
        for cp in self._first() + self._passed(0) + self._passed(1) + self._passed(2):
            cp.wait_send()


def _gather_scratch(n):
    return [pltpu.SemaphoreType.DMA((n, _Gather.N_SEMS)), pltpu.SemaphoreType.DMA((n, _Gather.N_SEMS)), pltpu.SemaphoreType.DMA((n,))]


def _gathered_shapes(ids):
    return tuple(jax.ShapeDtypeStruct(*GATHERED[w]) for w in ids)


def _fwd_proj(x, g1, inv2, w_in, w_out, w_up, w_down, conv_w):
    ids_a, ids_b = [W_IN, W_CONV], [W_OUT, W_DOWN]

    def body(x_ref, g_ref, inv_ref, in_hbm, out_hbm, up_hbm, dn_hbm, cw_ref,
             proj_ref, h1_ref, cos_ref, sin_ref, gin, gcw, gout, gdn, su_ref,
             w_vm, s_in, s_cw, s_out, s_dn, f_in, f_out, f_up, f_dn, ld_sems,
             a_send, a_recv, a_local, b_send, b_recv, b_local):
        ag_a = _Gather(ids_a, [s_in, s_cw], [gin, gcw], a_send, a_recv, a_local)
        ag_b = _Gather(ids_b, [s_out, s_dn], [gout, gdn], b_send, b_recv, b_local)

        @pl.when(pl.program_id(0) == 0)
        def _():
            meet = _Meet(diagonal=True)
            meet.signal()
            loads = [pltpu.make_async_copy(src, dst, ld_sems.at[i])
                     for i, (src, dst) in enumerate(((in_hbm, f_in), (out_hbm, f_out), (dn_hbm, f_dn), (up_hbm, f_up)))]
            for cp in loads:
                cp.start()
            s_cw[...] = jnp.zeros_like(s_cw)
            for k in range(3):
                s_cw[k:k + 1, :] = cw_ref[k]
            loads[0].wait()
            s_in[...] = f_in[...].astype(BF16)
            meet.wait()
            ag_a.start()
            loads[1].wait()
            s_out[...] = f_out[...].astype(BF16)
            loads[2].wait()
            s_dn[...] = f_dn[...].astype(BF16)
            ag_a.near()
            ag_b.start()
            loads[3].wait()
            su_ref[...] = f_up[...].astype(BF16)
            ag_a.finish()
            fill = pltpu.make_async_copy(gin, w_vm, ld_sems.at[4])
            fill.start()
            fill.wait()

        pl.when(pl.program_id(0) == FWD_PROJ_PASS_AT)(ag_b.near)

        xb = x_ref[...]
        r = lax.rsqrt(jnp.mean(xb * xb, axis=-1, keepdims=True) + EPS)
        h = ((xb * r) * g_ref[...]).astype(BF16)
        h1_ref[...] = h
        p = _dot(h, w_vm[...])
        c2, s2 = _rope_block(inv_ref[...], pl.program_id(0) * TM)
        cos_ref[...], sin_ref[...] = c2, s2
        for hd in range(HEADS):
            sl = slice(hd * HEAD_DIM, (hd + 1) * HEAD_DIM)
            proj_ref[:, sl] = _rot(p[:, sl], c2, s2)
            ks = slice(RET_W + hd * HEAD_DIM, RET_W + (hd + 1) * HEAD_DIM)
            proj_ref[:, ks] = _rot(p[:, ks], c2, s2) * K_SCALE
        proj_ref[:, 2 * RET_W:] = p[:, 2 * RET_W:]

        pl.when(pl.program_id(0) == N_TB - 1)(ag_b.finish)

    tok = lambda w: pl.BlockSpec((TM, w), lambda i: (i, 0))
    hbm = pl.BlockSpec(memory_space=pl.ANY)
    vm = pl.BlockSpec(memory_space=pltpu.VMEM)
    return pl.pallas_call(
        body, name="fwd_proj", grid=(N_TB,),
        out_shape=(jax.ShapeDtypeStruct((SEQ, PROJ_W), F32), jax.ShapeDtypeStruct((SEQ, D_MODEL), BF16),
                   jax.ShapeDtypeStruct((SEQ, HEAD_DIM), F32), jax.ShapeDtypeStruct((SEQ, HEAD_DIM), F32))
        + _gathered_shapes(ids_a + ids_b) + (jax.ShapeDtypeStruct(SHARD[W_UP], BF16),),
        in_specs=[tok(D_MODEL), _resident((1, D_MODEL)), _resident((1, HEAD_DIM)), hbm, hbm, hbm, hbm, vm],
        out_specs=(tok(PROJ_W), tok(D_MODEL), tok(HEAD_DIM), tok(HEAD_DIM), hbm, hbm, hbm, hbm, vm),
        scratch_shapes=[pltpu.VMEM((D_MODEL, PROJ_W), BF16), pltpu.VMEM(SHARD[W_IN], BF16), pltpu.VMEM(SHARD[W_CONV], F32),
                        pltpu.VMEM(SHARD[W_OUT], BF16), pltpu.VMEM(SHARD[W_DOWN], BF16),
                        pltpu.VMEM(SHARD[W_IN], F32), pltpu.VMEM(SHARD[W_OUT], F32), pltpu.VMEM(SHARD[W_UP], F32),
                        pltpu.VMEM(SHARD[W_DOWN], F32), pltpu.SemaphoreType.DMA((5,))]
        + _gather_scratch(len(ids_a)) + _gather_scratch(len(ids_b)),
        compiler_params=_cparams(("arbitrary",), collective=COLLECTIVE["fwd_proj"]),
    )(x, g1, inv2, w_in, w_out, w_up, w_down, conv_w)


def _causal(w):
    r = lax.broadcasted_iota(jnp.int32, (CHUNK, CHUNK), 0)
    c = lax.broadcasted_iota(jnp.int32, (CHUNK, CHUNK), 1)
    return jnp.where(r >= c, w, 0.0)


def _fwd_mix(x, proj, wout_g, grn, lng, lnb, ws, bsb, mask, qdec, kdec, su):
    cdec = _chunk_decay()
    ids = [W_UP]

    def body(x_ref, p_ref, w_ref, grn_ref, lng_ref, lnb_ref, ws_ref, bsb_ref, m_ref, qd_ref, kd_ref, su_ref,
             x2_ref, cat_ref, o_ref, sp_ref, gup, state, send_sems, recv_sems, local_sems):
        ag = _Gather(ids, [su_ref], [gup], send_sems, recv_sems, local_sems)

        @pl.when(pl.program_id(0) == 0)
        def _():
            meet = _Meet(diagonal=False)
            meet.signal()
            state[...] = jnp.zeros_like(state)
            meet.wait()
            ag.start()

        for h in range(HEADS):
            sl = slice(h * HEAD_DIM, (h + 1) * HEAD_DIM)
            q = p_ref[:, sl]
            k = p_ref[:, RET_W + h * HEAD_DIM:RET_W + (h + 1) * HEAD_DIM]
            v = p_ref[:, 2 * RET_W + h * HEAD_DIM:2 * RET_W + (h + 1) * HEAD_DIM]
            g = p_ref[:, 3 * RET_W + h * HEAD_DIM:3 * RET_W + (h + 1) * HEAD_DIM]
            qb, kb, vb = q.astype(BF16), k.astype(BF16), v.astype(BF16)
            a = _dot_nt(qb, kb) * m_ref[h]
            spb = state[h].astype(BF16)
            sp_ref[0, h] = spb
            o = _dot(a.astype(BF16), vb) + _dot((q * qd_ref[h]).astype(BF16), spb)
            state[h] = state[h] * cdec[h] + _dot_tn((k * kd_ref[h]).astype(BF16), vb)
            o_ref[:, sl] = o
            rinv = lax.rsqrt(jnp.mean(o * o, axis=-1, keepdims=True) + EPS)
            rn = (o * rinv) * grn_ref[:, sl]
            cat_ref[:, sl] = ((g * _sigmoid(g)) * rn).astype(BF16)
        for gi in range(HEADS):
            sl = slice(gi * HEAD_DIM, (gi + 1) * HEAD_DIM)
            u = p_ref[:, 4 * RET_W + gi * HEAD_DIM:4 * RET_W + (gi + 1) * HEAD_DIM]
            sv = p_ref[:, 4 * RET_W + SGU_W + gi * HEAD_DIM:4 * RET_W + SGU_W + (gi + 1) * HEAD_DIM]
            gv = _gelu(sv)
            xc = gv - jnp.mean(gv, axis=-1, keepdims=True)
            vn = (xc * lax.rsqrt(jnp.mean(xc * xc, axis=-1, keepdims=True) + EPS)) * lng_ref[:, sl] + lnb_ref[:, sl]
            mixed = _dot(_causal(ws_ref[gi]).astype(BF16), vn.astype(BF16)) + bsb_ref[gi]
            cat_ref[:, RET_W + gi * HEAD_DIM:RET_W + (gi + 1) * HEAD_DIM] = (_gelu(u) * mixed).astype(BF16)
        x2_ref[...] = x_ref[...] + _dot(cat_ref[...], w_ref[...])

        pl.when(pl.program_id(0) == FWD_MIX_PASS_AT)(ag.near)
        pl.when(pl.program_id(0) == N_CHUNK - 1)(ag.finish)

    ch = lambda w: pl.BlockSpec((CHUNK, w), lambda i: (i, 0))
    hcc = (HEADS, CHUNK, CHUNK)
    hbm = pl.BlockSpec(memory_space=pl.ANY)
    return pl.pallas_call(
        body, name="fwd_mix", grid=(N_CHUNK,),
        out_shape=(jax.ShapeDtypeStruct((SEQ, D_MODEL), F32), jax.ShapeDtypeStruct((SEQ, D_MODEL), BF16),
                   jax.ShapeDtypeStruct((SEQ, RET_W), F32), jax.ShapeDtypeStruct((N_CHUNK, HEADS, HEAD_DIM, HEAD_DIM), BF16))
        + _gathered_shapes(ids),
        in_specs=[ch(D_MODEL), ch(PROJ_W), _resident((D_MODEL, D_MODEL)), _resident((1, RET_W)), _resident((1, SGU_W)),
                  _resident((1, SGU_W)), _resident(hcc), _resident(hcc), _resident(hcc), _resident(hcc), _resident(hcc), hbm],
        out_specs=(ch(D_MODEL), ch(D_MODEL), ch(RET_W), pl.BlockSpec((1, HEADS, HEAD_DIM, HEAD_DIM), lambda i: (i, 0, 0, 0)), hbm),
        scratch_shapes=[pltpu.VMEM((HEADS, HEAD_DIM, HEAD_DIM), F32)] + _gather_scratch(len(ids)),
        compiler_params=_cparams(("arbitrary",), collective=COLLECTIVE["fwd_mix"]),
    )(x, proj, wout_g, grn, lng, lnb, ws, bsb, mask, qdec, kdec, su)


def _conv_taps(p, prev8):
    row = lax.broadcasted_iota(jnp.int32, p.shape, 0)
    p1 = jnp.where(row == 0, prev8[7:8, :], pltpu.roll(p, 1, 0))
    p2 = jnp.where(row == 0, prev8[6:7, :], jnp.where(row == 1, prev8[7:8, :], pltpu.roll(p, 2, 0)))
    return p1, p2


def _fwd_ffn(x2, g2, wup_g, cw_g, cb_g, wdn_g, gf, tgt):
    def body(x_ref, g_ref, wu_ref, cw_ref, cb_ref, wd_ref, gf_ref, t_ref, h2_ref, up_ref, u_ref, act_ref, x3_ref, loss_ref, carry):
        @pl.when(pl.program_id(0) == 0)
        def _():
            carry[...] = jnp.zeros_like(carry)

        xb = x_ref[...]
        r = lax.rsqrt(jnp.mean(xb * xb, axis=-1, keepdims=True) + EPS)
        h = ((xb * r) * g_ref[...]).astype(BF16)
        h2_ref[...] = h
        acc = xb
        for t0, tw in FF_TILES:
            u = []
            for c0 in (t0, D_FF + t0):
                cs = slice(c0, c0 + tw)
                p = _dot_nt(h, wu_ref[pl.ds(c0, tw), :])
                up_ref[:, cs] = p.astype(BF16)
                p1, p2 = _conv_taps(p, carry[:, cs])
                carry[:, cs] = p[TM - 8:, :]
                us = p2 * cw_ref[0:1, cs] + p1 * cw_ref[1:2, cs] + p * cw_ref[2:3, cs] + cb_ref[:, cs]
                u_ref[:, cs] = us.astype(BF16)
                u.append(us)
            a = ((u[0] * _sigmoid(u[0])) * u[1]).astype(BF16)
            act_ref[:, t0:t0 + tw] = a
            acc = acc + _dot(a, wd_ref[pl.ds(t0, tw), :])
        x3_ref[...] = acc
        r3 = lax.rsqrt(jnp.mean(acc * acc, axis=-1, keepdims=True) + EPS)
        diff = (acc * r3) * gf_ref[...] - t_ref[...]
        loss_ref[...] = jnp.full(loss_ref.shape, 0.5 * jnp.sum(jnp.mean(diff * diff, axis=-1)), F32)

    tok = lambda w: pl.BlockSpec((TM, w), lambda i: (i, 0))
    return pl.pallas_call(
        body, name="fwd_ffn", grid=(N_TB,),
        out_shape=(jax.ShapeDtypeStruct((SEQ, D_MODEL), BF16), jax.ShapeDtypeStruct((SEQ, 2 * D_FF), BF16),
                   jax.ShapeDtypeStruct((SEQ, 2 * D_FF), BF16),
                   jax.ShapeDtypeStruct((SEQ, D_FF), BF16), jax.ShapeDtypeStruct((SEQ, D_MODEL), F32),
                   jax.ShapeDtypeStruct((N_TB, 8, 128), F32)),
        in_specs=[tok(D_MODEL), _resident((1, D_MODEL)), _resident((2 * D_FF, D_MODEL)), _resident((8, 2 * D_FF)),
                  _resident((1, 2 * D_FF)), _resident((D_FF, D_MODEL)), _resident((1, D_MODEL)), tok(D_MODEL)],
        out_specs=(tok(D_MODEL), tok(2 * D_FF), tok(2 * D_FF), tok(D_FF), tok(D_MODEL),
                   pl.BlockSpec((1, 8, 128), lambda i: (i, 0, 0))),
        scratch_shapes=[pltpu.VMEM((8, 2 * D_FF), F32)],
        compiler_params=_cparams(("arbitrary",)),
    )(x2, g2, wup_g, cw_g, cb_g, wdn_g, gf, tgt)


def _bwd_ffn(x3, tgt, gf, x2, g2, up_pre, u_conv, wup_g, cw_g, wdn_g):
    def body(x3_ref, t_ref, gf_ref, x2_ref, g2_ref, up_ref, u_ref, wu_ref, cw_ref, wd_ref,
             dx3_ref, dpre_ref, dx2_ref, dgf_ref, dg2_ref, dcv_ref, nxt):
        i = pl.program_id(0)

        @pl.when(i == 0)
        def _():
            nxt[...] = jnp.zeros_like(nxt)
            dgf_ref[...] = jnp.zeros_like(dgf_ref)
            dg2_ref[...] = jnp.zeros_like(dg2_ref)
            dcv_ref[...] = jnp.zeros_like(dcv_ref)

        x3 = x3_ref[...]
        r3 = lax.rsqrt(jnp.mean(x3 * x3, axis=-1, keepdims=True) + EPS)
        xh3 = x3 * r3
        dy = (xh3 * gf_ref[...] - t_ref[...]) * (1.0 / D_MODEL)
        dgf_ref[0:1, :] += jnp.sum(dy * xh3, axis=0, keepdims=True)
        t3 = dy * gf_ref[...]
        dx3 = r3 * (t3 - xh3 * jnp.mean(t3 * xh3, axis=-1, keepdims=True))
        dx3b = dx3.astype(BF16)
        dx3_ref[...] = dx3b
        dh2 = jnp.zeros((TM, D_MODEL), F32)
        for t0, tw in FF_TILES:
            row = lax.broadcasted_iota(jnp.int32, (TM, tw), 0)
            ts = slice(t0, t0 + tw)
            dact = _dot_nt(dx3b, wd_ref[pl.ds(t0, tw), :])
            ua = u_ref[:, ts].astype(F32)
            ub = u_ref[:, D_FF + t0:D_FF + t0 + tw].astype(F32)
            sg = _sigmoid(ua)
            du = [dact * ub * (sg * (1.0 + ua * (1.0 - sg))), dact * (ua * sg)]
            for n in range(2):
                d = du[n]
                c0 = n * D_FF + t0
                cs = slice(c0, c0 + tw)
                nx = nxt[:, cs]
                n1 = jnp.where(row == TM - 1, nx[0:1, :], pltpu.roll(d, TM - 1, 0))
                n2 = jnp.where(row == TM - 2, nx[0:1, :], jnp.where(row == TM - 1, nx[1:2, :], pltpu.roll(d, TM - 2, 0)))
                nxt[:, cs] = d[0:8, :]
                dp = (d * cw_ref[2:3, cs] + n1 * cw_ref[1:2, cs] + n2 * cw_ref[0:1, cs]).astype(BF16)
                dpre_ref[:, cs] = dp
                p = up_ref[:, cs].astype(F32)
                dcv_ref[n, 0:1, ts] += jnp.sum(n2 * p, axis=0, keepdims=True)
                dcv_ref[n, 1:2, ts] += jnp.sum(n1 * p, axis=0, keepdims=True)
                dcv_ref[n, 2:3, ts] += jnp.sum(d * p, axis=0, keepdims=True)
                dcv_ref[n, 3:4, ts] += jnp.sum(d, axis=0, keepdims=True)
                dh2 = dh2 + _dot(dp, wu_ref[pl.ds(c0, tw), :])
        x2 = x2_ref[...]
        r2 = lax.rsqrt(jnp.mean(x2 * x2, axis=-1, keepdims=True) + EPS)
        xh2 = x2 * r2
        dg2_ref[0:1, :] += jnp.sum(dh2 * xh2, axis=0, keepdims=True)
        t2 = dh2 * g2_ref[...]
        dx2_ref[...] = dx3 + r2 * (t2 - xh2 * jnp.mean(t2 * xh2, axis=-1, keepdims=True))

    rev = lambda w: pl.BlockSpec((TM, w), lambda i: (N_TB - 1 - i, 0))
    acc = lambda s: pl.BlockSpec(s, lambda i: (0,) * len(s))
    return pl.pallas_call(
        body, name="bwd_ffn", grid=(N_TB,),
        out_shape=(jax.ShapeDtypeStruct((SEQ, D_MODEL), BF16), jax.ShapeDtypeStruct((SEQ, 2 * D_FF), BF16),
                   jax.ShapeDtypeStruct((SEQ, D_MODEL), F32), jax.ShapeDtypeStruct((8, D_MODEL), F32),
                   jax.ShapeDtypeStruct((8, D_MODEL), F32), jax.ShapeDtypeStruct((2, 8, D_FF), F32)),
        in_specs=[rev(D_MODEL), rev(D_MODEL), _resident((1, D_MODEL)), rev(D_MODEL), _resident((1, D_MODEL)), rev(2 * D_FF),
                  rev(2 * D_FF), _resident((2 * D_FF, D_MODEL)), _resident((8, 2 * D_FF)), _resident((D_FF, D_MODEL))],
        out_specs=(rev(D_MODEL), rev(2 * D_FF), rev(D_MODEL), acc((8, D_MODEL)), acc((8, D_MODEL)), acc((2, 8, D_FF))),
        scratch_shapes=[pltpu.VMEM((8, 2 * D_FF), F32)],
        compiler_params=_cparams(("arbitrary",)),
    )(x3, tgt, gf, x2, g2, up_pre, u_conv, wup_g, cw_g, wdn_g)


def _bwd_mix(dx2, proj, o, sprev, wout_g, grn, lng, lnb, ws, bsb, mask, qdec, kdec, cos2, sin2, hosted):
    cdec = _chunk_decay()
    geoms = [g for g, _ in hosted]
    n_h = len(hosted)

    def body(dx2_ref, p_ref, o_ref, sp_ref, w_ref, grn_ref, lng_ref, lnb_ref, ws_ref, bsb_ref, m_ref, qd_ref, kd_ref,
             cos_ref, sin_ref, *rest):
        dp_ref, dgrn_ref, dlng_ref, dlnb_ref, dws_ref, dbs_ref = rest[n_h:n_h + 6]
        dstate, dbs_acc = rest[2 * n_h + 6:2 * n_h + 8]
        i = pl.program_id(0)
        rs = _Scatters(geoms, rest[:n_h], rest[n_h + 6:2 * n_h + 6], rest[2 * n_h + 8:])
        pl.when(i == 0)(rs.phase1)
        pl.when(i == 3)(rs.phase2)
        pl.when(i == 8)(rs.phase2b)

        @pl.when(i == 0)
        def _():
            dstate[...] = jnp.zeros_like(dstate)
            dgrn_ref[...] = jnp.zeros_like(dgrn_ref)
            dlng_ref[...] = jnp.zeros_like(dlng_ref)
            dlnb_ref[...] = jnp.zeros_like(dlnb_ref)
            dws_ref[...] = jnp.zeros_like(dws_ref)
            dbs_ref[...] = jnp.zeros_like(dbs_ref)
            dbs_acc[...] = jnp.zeros_like(dbs_acc)

        dmix = _dot_nt(dx2_ref[...].astype(BF16), w_ref[...])
        for h in range(HEADS):
            sl = slice(h * HEAD_DIM, (h + 1) * HEAD_DIM)
            q = p_ref[:, sl]
            k = p_ref[:, RET_W + h * HEAD_DIM:RET_W + (h + 1) * HEAD_DIM]
            v = p_ref[:, 2 * RET_W + h * HEAD_DIM:2 * RET_W + (h + 1) * HEAD_DIM]
            g = p_ref[:, 3 * RET_W + h * HEAD_DIM:3 * RET_W + (h + 1) * HEAD_DIM]
            o = o_ref[:, sl]
            rinv = lax.rsqrt(jnp.mean(o * o, axis=-1, keepdims=True) + EPS)
            oh = o * rinv
            gr = grn_ref[:, sl]
            sg = _sigmoid(g)
            dret = dmix[:, sl]
            dp_ref[:, 3 * RET_W + h * HEAD_DIM:3 * RET_W + (h + 1) * HEAD_DIM] = (
                dret * (oh * gr) * (sg * (1.0 + g * (1.0 - sg)))).astype(BF16)
            drn = dret * (g * sg)
            dgrn_ref[0:1, sl] += jnp.sum(drn * oh, axis=0, keepdims=True)
            t = drn * gr
            do = rinv * (t - oh * jnp.mean(t * oh, axis=-1, keepdims=True))
            qb, kb, vb, dob = q.astype(BF16), k.astype(BF16), v.astype(BF16), do.astype(BF16)
            m = m_ref[h]
            ab = (_dot_nt(qb, kb) * m).astype(BF16)
            dab = (_dot_nt(dob, vb) * m).astype(BF16)
            spb = sp_ref[0, h]
            dsn = dstate[h]
            dsnb = dsn.astype(BF16)
            qdb = (q * qd_ref[h]).astype(BF16)
            kdb = (k * kd_ref[h]).astype(BF16)
            dq = _dot(dab, kb) + _dot_nt(dob, spb) * qd_ref[h]
            dk = _dot_tn(dab, qb) + _dot_nt(vb, dsnb) * kd_ref[h]
            dv = _dot_tn(ab, dob) + _dot(kdb, dsnb)
            dstate[h] = dsn * cdec[h] + _dot_tn(qdb, dob)
            c2, s2 = cos_ref[...], sin_ref[...]
            dp_ref[:, sl] = _rot_t(dq, c2, s2).astype(BF16)
            dp_ref[:, RET_W + h * HEAD_DIM:RET_W + (h + 1) * HEAD_DIM] = _rot_t(dk * K_SCALE, c2, s2).astype(BF16)
            dp_ref[:, 2 * RET_W + h * HEAD_DIM:2 * RET_W + (h + 1) * HEAD_DIM] = dv.astype(BF16)
        for gi in range(HEADS):
            sl = slice(gi * HEAD_DIM, (gi + 1) * HEAD_DIM)
            u = p_ref[:, 4 * RET_W + gi * HEAD_DIM:4 * RET_W + (gi + 1) * HEAD_DIM]
            sv = p_ref[:, 4 * RET_W + SGU_W + gi * HEAD_DIM:4 * RET_W + SGU_W + (gi + 1) * HEAD_DIM]
            gv = _gelu(sv)
            xc = gv - jnp.mean(gv, axis=-1, keepdims=True)
            rstd = lax.rsqrt(jnp.mean(xc * xc, axis=-1, keepdims=True) + EPS)
            xh = xc * rstd
            lg = lng_ref[:, sl]
            vnb = (xh * lg + lnb_ref[:, sl]).astype(BF16)
            wcb = _causal(ws_ref[gi]).astype(BF16)
            mixed = _dot(wcb, vnb) + bsb_ref[gi]
            dsgu = dmix[:, RET_W + gi * HEAD_DIM:RET_W + (gi + 1) * HEAD_DIM]
            dmixed = dsgu * _gelu(u)
            dmb = dmixed.astype(BF16)
            dws_ref[gi] += _causal(_dot_nt(dmb, vnb))
            dbs_acc[gi] += dmixed
            dvn = _dot_tn(wcb, dmb)
            dlng_ref[gi:gi + 1, :] += jnp.sum(dvn * xh, axis=0, keepdims=True)
            dlnb_ref[gi:gi + 1, :] += jnp.sum(dvn, axis=0, keepdims=True)
            dxh = dvn * lg
            dgv = rstd * (dxh - jnp.mean(dxh, axis=-1, keepdims=True) - xh * jnp.mean(dxh * xh, axis=-1, keepdims=True))
            dp_ref[:, 4 * RET_W + gi * HEAD_DIM:4 * RET_W + (gi + 1) * HEAD_DIM] = (dsgu * mixed * _gelu_grad(u)).astype(BF16)
            dp_ref[:, 4 * RET_W + SGU_W + gi * HEAD_DIM:4 * RET_W + SGU_W + (gi + 1) * HEAD_DIM] = (
                dgv * _gelu_grad(sv)).astype(BF16)

        @pl.when(i == N_CHUNK - 1)
        def _():
            for gi in range(HEADS):
                col = jnp.broadcast_to(jnp.sum(dbs_acc[gi], axis=-1, keepdims=True), (CHUNK, CHUNK))
                dbs_ref[gi:gi + 1, :] = jnp.transpose(col)[0:1, :]
            rs.phase3()

    rev = lambda w: pl.BlockSpec((CHUNK, w), lambda i: (N_CHUNK - 1 - i, 0))
    hcc = (HEADS, CHUNK, CHUNK)
    acc = lambda s: pl.BlockSpec(s, lambda i: (0,) * len(s))
    res = pl.pallas_call(
        body, name="bwd_mix", grid=(N_CHUNK,),
        out_shape=(jax.ShapeDtypeStruct((SEQ, PROJ_W), BF16), jax.ShapeDtypeStruct((8, RET_W), F32),
                   jax.ShapeDtypeStruct((8, HEAD_DIM), F32), jax.ShapeDtypeStruct((8, HEAD_DIM), F32),
                   jax.ShapeDtypeStruct(hcc, F32), jax.ShapeDtypeStruct((8, CHUNK), F32)) + _scatter_out_shapes(geoms),
        in_specs=[rev(D_MODEL), rev(PROJ_W), rev(RET_W),
                  pl.BlockSpec((1, HEADS, HEAD_DIM, HEAD_DIM), lambda i: (N_CHUNK - 1 - i, 0, 0, 0)),
                  _resident((D_MODEL, D_MODEL)), _resident((1, RET_W)), _resident((1, SGU_W)), _resident((1, SGU_W)),
                  _resident(hcc), _resident(hcc), _resident(hcc), _resident(hcc), _resident(hcc), rev(HEAD_DIM), rev(HEAD_DIM)]
        + [pl.BlockSpec(memory_space=pl.ANY)] * n_h,
        out_specs=(rev(PROJ_W), acc((8, RET_W)), acc((8, HEAD_DIM)), acc((8, HEAD_DIM)), acc(hcc), acc((8, CHUNK)))
        + _scatter_out_specs(geoms),
        scratch_shapes=[pltpu.VMEM((HEADS, HEAD_DIM, HEAD_DIM), F32), pltpu.VMEM((HEADS, CHUNK, CHUNK), F32)] + _scatter_scratch(geoms),
        compiler_params=_cparams(("arbitrary",), collective=COLLECTIVE["bwd_mix"]),
    )(dx2, proj, o, sprev, wout_g, grn, lng, lnb, ws, bsb, mask, qdec, kdec, cos2, sin2, *[p for _, p in hosted])
    return tuple(res[:6 + n_h])


def _bwd_proj(dproj, win_g, x, g1, dx2, gin_p, small):
    geoms = [W_IN]
    n_s = len(small)

    def body(dp_ref, w_ref, x_ref, g_ref, dx2_ref, gin_ref, *rest):
        small_refs = rest[:n_s]
        dx_ref, rs_out, rp_ref, rws_ref, rcv_ref, dg_ref = rest[n_s:n_s + 6]
        rs_scratch = rest[n_s + 6:n_s + 6 + N_SCATTER_SCRATCH]
        ar_scratch = rest[n_s + 6 + N_SCATTER_SCRATCH:]
        ar_res = ar_scratch[N_SMALL_SCRATCH:]
        ar = _SmallReduce((dg_ref,) + tuple(small_refs), ar_res, ar_scratch[:N_SMALL_SCRATCH])
        rs = _Scatters(geoms, [gin_ref], [rs_out], rs_scratch)
        pl.when(pl.program_id(0) == 0)(lambda: rs.phase1(diagonal=True))
        pl.when(pl.program_id(0) == 1)(rs.phase2)
        pl.when(pl.program_id(0) == 5)(rs.phase2b)

        @pl.when(pl.program_id(0) == 0)
        def _():
            dg_ref[...] = jnp.zeros_like(dg_ref)

        dh = _dot_nt(dp_ref[...], w_ref[...])
        xb = x_ref[...]
        r = lax.rsqrt(jnp.mean(xb * xb, axis=-1, keepdims=True) + EPS)
        xh = xb * r
        dg_ref[0:1, :] += jnp.sum(dh * xh, axis=0, keepdims=True)
        t = dh * g_ref[...]
        dx_ref[...] = dx2_ref[...] + r * (t - xh * jnp.mean(t * xh, axis=-1, keepdims=True))

        @pl.when(pl.program_id(0) == N_TB - 1)
        def _():
            ar.begin()
            rs.phase3()
            ar.end()
            for o_ref, r_ref in zip((rp_ref, rws_ref, rcv_ref), ar_res):
                o_ref[...] = r_ref[...]

    tok = lambda w: pl.BlockSpec((TM, w), lambda i: (i, 0))
    vm = pl.BlockSpec(memory_space=pltpu.VMEM)
    res = pl.pallas_call(
        body, name="bwd_proj", grid=(N_TB,),
        out_shape=(jax.ShapeDtypeStruct((SEQ, D_MODEL), F32),) + _scatter_out_shapes(geoms)
        + tuple(jax.ShapeDtypeStruct(s, F32) for s in SMALL_FULL),
        in_specs=[tok(PROJ_W), _resident((D_MODEL, PROJ_W)), tok(D_MODEL), _resident((1, D_MODEL)), tok(D_MODEL),
                  pl.BlockSpec(memory_space=pl.ANY)] + [vm] * n_s,
        out_specs=(tok(D_MODEL),) + _scatter_out_specs(geoms) + (vm,) * len(SMALL_FULL),
        scratch_shapes=[pltpu.VMEM((8, D_MODEL), F32)] + _scatter_scratch(geoms) + _small_scratch()
        + [pltpu.VMEM(s, F32) for s in SMALL_FULL],
        compiler_params=_cparams(("arbitrary",), collective=COLLECTIVE["bwd_proj"]),
    )(dproj, win_g, x, g1, dx2, gin_p, *small)
    return res


def _wgrad(name, a, b, tm=None, tn=None, hosted=()):
    m_w, n_w = a.shape[-1], b.shape[-1]
    tm = m_w if tm is None else tm
    tn = n_w if tn is None else tn
    n_steps = (m_w // tm) * (n_w // tn)
    geoms = [g for g, _ in hosted]
    n_h = len(hosted)

    def body(a_ref, b_ref, *rest):
        o_ref = rest[n_h]
        if n_h:
            rs = _Scatters(geoms, rest[:n_h], rest[n_h + 1:2 * n_h + 1], rest[2 * n_h + 1:])
            step = pl.program_id(0) * (n_w // tn) + pl.program_id(1)
            pl.when(step == 0)(rs.phase1)
            pl.when(step == 1)(rs.phase2)
            pl.when(step == n_steps // 2)(rs.phase2b)
        o_ref[...] = _dot_tn(a_ref[...].astype(BF16), b_ref[...].astype(BF16)).astype(BF16)
        if n_h:
            pl.when(step == n_steps - 1)(rs.phase3)

    assert not n_h or n_steps >= 4
    res = pl.pallas_call(
        body, name=name, grid=(m_w // tm, n_w // tn),
        out_shape=(jax.ShapeDtypeStruct((m_w, n_w), BF16),) + _scatter_out_shapes(geoms),
        in_specs=[pl.BlockSpec((SEQ, tm), lambda i, j: (0, i)), pl.BlockSpec((SEQ, tn), lambda i, j: (0, j))]
        + [pl.BlockSpec(memory_space=pl.ANY)] * n_h,
        out_specs=(pl.BlockSpec((tm, tn), lambda i, j: (i, j)),) + _scatter_out_specs(geoms),
        scratch_shapes=_scatter_scratch(geoms),
        compiler_params=_cparams(("arbitrary", "arbitrary"), collective=COLLECTIVE[name]) if n_h else _cparams(("parallel", "parallel")),
    )(a, b, *[p for _, p in hosted])
    return tuple(res[:1 + n_h])


def _row_step(half_rows):
    return max(s for s in range(16, 177, 16) if half_rows % s == 0)


class _Scatter:
    def __init__(self, geom, partial, out, land1, mine, stage2, land2, comb, s1_send, s1_recv, s2_send, s2_recv, ld_sems):
        self.w, self.row0, self.shape = _geom(geom)
        self.partial, self.out, self.land1 = partial, out, land1
        self.mine, self.stage2, self.land2, self.comb = mine, stage2, land2, comb
        self.hr = self.shape[0] // 2
        self.step = _row_step(self.hr)
        self.s1_send, self.s1_recv, self.s2_send, self.s2_recv, self.ld_sems = s1_send, s1_recv, s2_send, s2_recv, ld_sems
        self.x, self.y, self.c = lax.axis_index("x"), lax.axis_index("y"), lax.axis_index("c")
        self.sibling = (self.x, self.y, 1 - self.c)
        self.chips = [(self.x, self.y), (1 - self.x, self.y), (self.x, 1 - self.y), (1 - self.x, 1 - self.y)]

    def block(self, px, py, pc):
        dev = 4 * px + 2 * py + pc
        if self.w == W_IN:
            return self.partial.at[:, pl.ds(pl.multiple_of(dev * IN_SHARD, 128), IN_SHARD)]
        if self.w == W_OUT:
            return self.partial.at[pl.ds(pl.multiple_of(dev * OUT_SHARD, 128), OUT_SHARD), :]
        if self.w == W_DOWN:
            return self.partial.at[pl.ds(pl.multiple_of(dev * DOWN_SHARD, 32), DOWN_SHARD), :]
        return self.partial.at[pl.ds(pl.multiple_of(dev * FF_SHARD + self.row0, 32), self.shape[0]), :]

    def copy1(self, k):
        return pltpu.make_async_remote_copy(
            src_ref=self.block(*self.chips[k], 1 - self.c), dst_ref=self.land1.at[k],
            send_sem=self.s1_send.at[k], recv_sem=self.s1_recv.at[k], device_id=self.sibling, device_id_type=MESH)

    STAGE2 = [(1, 0, 1), (3, 0, 1), (2, 1, 2), (3, 1, 2), (1, 1, 1), (2, 0, 2)]

    def copy2(self, j):
        blk, h, to = self.STAGE2[j]
        src = self.comb.at[j - 4] if j >= 4 else self.stage2.at[blk - 1, pl.ds(h * self.hr, self.hr), :]
        return pltpu.make_async_remote_copy(
            src_ref=src, dst_ref=self.land2.at[j], send_sem=self.s2_send.at[j], recv_sem=self.s2_recv.at[j],
            device_id=(*self.chips[to], self.c), device_id_type=MESH)

    def _rows(self, h=None):
        step = self.step
        lo, n = (0, self.shape[0]) if h is None else (h * self.hr, self.hr)
        return [pl.ds(r0, step) for r0 in range(lo, lo + n, step)]

    def load(self, k):
        return pltpu.make_async_copy(self.block(*self.chips[k], self.c), self.mine.at[k], self.ld_sems.at[k])

    def load_mine(self):
        for k in range(4):
            self.load(k).start()

    def phase1(self):
        for k in range(4):
            self.copy1(k).start()

    def phase2(self, k):
        self.copy1(k).wait_recv()
        self.load(k).wait()
        for rs in self._rows():
            s = self.mine[k, rs, :].astype(F32) + self.land1[k, rs, :].astype(F32)
            if k == 0:
                self.out[rs, :] = s
            else:
                self.stage2[k - 1, rs, :] = s.astype(BF16)
        for j in {3: (1, 3), 1: (0,), 2: (2,), 0: ()}[k]:
            self.copy2(j).start()

    def phase2b(self):
        for j, got in ((4, 3), (5, 1)):
            blk, h, _ = self.STAGE2[j]
            self.copy2(got).wait_recv()
            for i, rs in enumerate(self._rows(h)):
                lr = pl.ds(i * self.step, self.step)
                self.comb[j - 4, lr, :] = (self.stage2[blk - 1, rs, :].astype(F32) + self.land2[got, lr, :].astype(F32)).astype(BF16)
            self.copy2(j).start()

    def phase3(self):
        for j in (0, 5, 4, 2):
            self.copy2(j).wait_recv()
        for h, (first, second) in enumerate(((0, 5), (4, 2))):
            for i, rs in enumerate(self._rows(h)):
                lr = pl.ds(i * self.step, self.step)
                self.out[rs, :] = (self.out[rs, :] + self.land2[first, lr, :].astype(F32)) + self.land2[second, lr, :].astype(F32)
        for k in range(4):
            self.copy1(k).wait_send()
        for j in range(6):
            self.copy2(j).wait_send()


def _geom(geom):
    if isinstance(geom, tuple):
        w, row0, rows = geom
        assert w == W_UP
        return w, row0, (rows, SHARD[w][1])
    return geom, 0, SHARD[geom]


N_SCATTER_SCRATCH = 10


def _scatter_out_shapes(geoms):
    return tuple(jax.ShapeDtypeStruct(_geom(g)[2], F32) for g in geoms)


def _scatter_out_specs(geoms):
    return (pl.BlockSpec(memory_space=pltpu.VMEM),) * len(geoms)


def _scatter_scratch(geoms):
    out = []
    for g in geoms:
        s = _geom(g)[2]
        hs = (s[0] // 2, s[1])
        out += [pltpu.VMEM((4,) + s, BF16), pltpu.VMEM((4,) + s, BF16), pltpu.VMEM((3,) + s, BF16), pltpu.VMEM((6,) + hs, BF16),
                pltpu.VMEM((2,) + hs, BF16),
                pltpu.SemaphoreType.DMA((4,)), pltpu.SemaphoreType.DMA((4,)), pltpu.SemaphoreType.DMA((6,)),
                pltpu.SemaphoreType.DMA((6,)), pltpu.SemaphoreType.DMA((4,))]
    return out


class _Scatters:
    def __init__(self, geoms, p_refs, out_refs, scratch):
        k = N_SCATTER_SCRATCH
        self.items = [_Scatter(g, p_refs[i], out_refs[i], *scratch[k * i:k * i + k]) for i, g in enumerate(geoms)]

    def phase1(self, diagonal=False):
        meet = _Meet(diagonal)
        meet.signal()
        for s in self.items:
            s.load_mine()
        meet.wait()
        for s in self.items:
            s.phase1()

    def phase2(self):
        for k in (3, 1, 2, 0):
            for s in self.items:
                s.phase2(k)

    def phase2b(self):
        for s in self.items:
            s.phase2b()

    def phase3(self):
        for s in self.items:
            s.phase3()


PACK_W = 1024


SMALL_FULL = [(2, 8, PACK_W), (HEADS, CHUNK, CHUNK), (2, 8, D_FF)]
SMALL_HALF = [(s[0] // 2,) + s[1:] for s in SMALL_FULL]
N_SMALL_SCRATCH = 16


def _small_scratch():
    n_a = len(SMALL_FULL)
    return ([pltpu.VMEM(SMALL_FULL[0], F32)] + [pltpu.VMEM(s, F32) for s in SMALL_HALF] + [pltpu.VMEM(s, F32) for s in SMALL_HALF]
            + [pltpu.VMEM((3,) + s, F32) for s in SMALL_HALF]
            + [pltpu.SemaphoreType.DMA((n_a,)), pltpu.SemaphoreType.DMA((n_a,)), pltpu.SemaphoreType.DMA((n_a, 3)),
               pltpu.SemaphoreType.DMA((n_a, 3)), pltpu.SemaphoreType.DMA((n_a,)), pltpu.SemaphoreType.DMA((n_a,))])


class _SmallReduce:
    def __init__(self, ins, outs, scratch):
        self.ins, self.outs = ins, outs
        (self.pack, *rest) = scratch
        self.rxs, self.css, self.gs = rest[0:3], rest[3:6], rest[6:9]
        self.s1_send, self.s1_recv, self.s2_send, self.s2_recv, self.s3_send, self.s3_recv = rest[9:]
        self.x, self.y, self.c = lax.axis_index("x"), lax.axis_index("y"), lax.axis_index("c")
        self.sibling = (self.x, self.y, 1 - self.c)
        self.chips = [(1 - self.x, self.y), (self.x, 1 - self.y), (1 - self.x, 1 - self.y)]
        self.hl = [s[0] for s in SMALL_HALF]

    def half(self, ref, a, h):
        return ref.at[pl.ds(h * self.hl[a], self.hl[a])]

    def begin(self):
        dg1_ref, dg2_ref, dgf_ref, dgrn_ref, dlng_ref, dlnb_ref, dbs_ref, loss_ref, dws_ref, dcv_ref = self.ins
        pack, c = self.pack, self.c
        pack[...] = jnp.zeros_like(pack)
        pack[0, 0:1, :] = dg1_ref[0:1, :]
        pack[0, 1:2, :] = dg2_ref[0:1, :]
        pack[0, 2:3, :] = dgf_ref[0:1, :]
        pack[0, 3:4, 0:RET_W] = dgrn_ref[0:1, :]
        lsum = loss_ref[0, 0:1, :]
        for i in range(1, N_TB):
            lsum = lsum + loss_ref[i, 0:1, :]
        pack[0, 3:4, RET_W:RET_W + 128] = lsum
        pack[1, 0:HEADS, 0:128] = dlng_ref[0:HEADS, :]
        pack[1, 0:HEADS, 128:256] = dlnb_ref[0:HEADS, :]
        pack[1, 0:HEADS, 256:384] = dbs_ref[0:HEADS, :]
        self.srcs = [pack, dws_ref, dcv_ref]
        n_a = len(self.srcs)
        self.ex1 = [pltpu.make_async_remote_copy(src_ref=self.half(self.srcs[a], a, 1 - c), dst_ref=self.rxs[a],
                                                 send_sem=self.s1_send.at[a], recv_sem=self.s1_recv.at[a],
                                                 device_id=self.sibling, device_id_type=MESH) for a in range(n_a)]
        for cp in self.ex1:
            cp.start()
        self.ex2 = []
        for a in range(n_a):
            self.ex1[a].wait_recv()
            self.css[a][...] = self.half(self.srcs[a], a, c)[...] + self.rxs[a][...]
            for j, chip in enumerate(self.chips):
                cp = pltpu.make_async_remote_copy(src_ref=self.css[a], dst_ref=self.gs[a].at[j], send_sem=self.s2_send.at[a, j],
                                                  recv_sem=self.s2_recv.at[a, j], device_id=(*chip, c), device_id_type=MESH)
                cp.start()
                self.ex2.append(cp)

    def end(self):
        c, x, y = self.c, self.x, self.y
        ex3 = []
        for a in range(len(self.srcs)):
            css, gs, out = self.css[a], self.gs[a], self.outs[a]
            for j in range(3):
                self.ex2[3 * a + j].wait_recv()
            tot = None
            for q in range(4):
                k = jnp.where(x != (q >> 1), 1, 0) + jnp.where(y != (q & 1), 2, 0)
                term = jnp.where(k == 0, css[...], jnp.where(k == 1, gs[0], jnp.where(k == 2, gs[1], gs[2])))
                tot = term if tot is None else tot + term
            self.half(out, a, c)[...] = tot
            cp = pltpu.make_async_remote_copy(src_ref=self.half(out, a, c), dst_ref=self.half(out, a, c), send_sem=self.s3_send.at[a],
                                              recv_sem=self.s3_recv.at[a], device_id=self.sibling, device_id_type=MESH)
            cp.start()
            ex3.append(cp)
        for a in range(len(self.srcs)):
            out = self.outs[a]
            pltpu.make_async_remote_copy(src_ref=self.half(out, a, 1 - c), dst_ref=self.half(out, a, 1 - c), send_sem=self.s3_send.at[a],
                                         recv_sem=self.s3_recv.at[a], device_id=self.sibling, device_id_type=MESH).wait_recv()
        for cp in self.ex1 + self.ex2 + ex3:
            cp.wait_send()


def _adam_math(w, g, m, v):
    nm = ADAM_B1 * m + (1.0 - ADAM_B1) * g
    nv = ADAM_B2 * v + (1.0 - ADAM_B2) * (g * g)
    d = -ADAM_LR * ((nm / (1.0 - ADAM_B1 ** ADAM_STEP)) / (jnp.sqrt(nv / (1.0 - ADAM_B2 ** ADAM_STEP)) + ADAM_EPS) + ADAM_WD * w)
    return d, nm, nv


def _adamw(name, w, gs, m, v, rows):
    _, r, cdim = w.shape
    n_steps = r // rows
    half = gs[0].shape[0] // rows

    def body(w_ref, *rest):
        g_refs, (m_ref, v_ref, go_ref, d_ref, nm_ref, nv_ref) = rest[:len(gs)], rest[len(gs):]
        gg = g_refs[0][...]
        if len(gs) == 2:
            gg = jnp.where(pl.program_id(0) < half, gg, g_refs[1][...])
        go_ref[0] = gg
        d, nm, nv = _adam_math(w_ref[0], gg, m_ref[0], v_ref[0])
        d_ref[0], nm_ref[0], nv_ref[0] = d, nm, nv

    spec3 = pl.BlockSpec((1, rows, cdim), lambda i: (0, i, 0))
    if len(gs) == 1:
        g_specs = [pl.BlockSpec((rows, cdim), lambda i: (i, 0))]
    else:
        g_specs = [pl.BlockSpec((rows, cdim), lambda i: (jnp.minimum(i, half - 1), 0)),
                   pl.BlockSpec((rows, cdim), lambda i: (jnp.maximum(i - half, 0), 0))]
    sh = jax.ShapeDtypeStruct((1, r, cdim), F32)
    return pl.pallas_call(
        body, name=name, grid=(n_steps,), out_shape=(sh, sh, sh, sh),
        in_specs=[spec3] + g_specs + [spec3, spec3], out_specs=(spec3,) * 4,
        compiler_params=_cparams(("parallel",)),
    )(w, *gs, m, v)


def _adamw_small(rp, rws, rcv, gcw, params):
    n_p = len(params)

    def body(*refs):
        rp_ref, rws_ref, rcv_ref, gcw_ref = refs[:4]
        ins = refs[4:4 + 3 * n_p]
        outs = refs[4 + 3 * n_p:]
        outs[4 * n_p][...] = rp_ref[0, 3:4, RET_W:RET_W + 1]
        grads = [rp_ref[0, 0:1, :], rp_ref[0, 1:2, :], rp_ref[0, 2:3, :], rp_ref[0, 3:4, 0:RET_W],
                 rp_ref[1, 0:HEADS, 0:128], rp_ref[1, 0:HEADS, 128:256], rp_ref[1, 0:HEADS, 256:384],
                 rws_ref[...], gcw_ref[...], None]
        for p in range(n_p):
            w_ref, m_ref, v_ref = ins[3 * p:3 * p + 3]
            o = outs[4 * p:4 * p + 4]
            if p == n_p - 1:
                for hf in range(2):
                    cs = slice(hf * D_FF, (hf + 1) * D_FF)
                    g = rcv_ref[hf, 3:4, :]
                    res = (g,) + _adam_math(w_ref[:, cs], g, m_ref[:, cs], v_ref[:, cs])
                    for t in range(4):
                        o[t][:, cs] = res[t]
                continue
            lead = w_ref.ndim > grads[p].ndim
            rd = (lambda r: r[0]) if lead else (lambda r: r[...])
            res = (grads[p],) + _adam_math(rd(w_ref), grads[p], rd(m_ref), rd(v_ref))
            for t in range(4):
                if lead:
                    o[t][0] = res[t]
                else:
                    o[t][...] = res[t]

    vm = pl.BlockSpec(memory_space=pltpu.VMEM)
    flat = [a for tr in params for a in tr]
    out_shape = tuple(jax.ShapeDtypeStruct(tr[0].shape, F32) for tr in params for _ in range(4)) + (jax.ShapeDtypeStruct((1, 1), F32),)
    res = pl.pallas_call(
        body, name="adamw_small", out_shape=out_shape, in_specs=[vm] * (4 + len(flat)), out_specs=(vm,) * len(out_shape),
        compiler_params=_cparams(),
    )(rp, rws, rcv, gcw, *flat)
    return [res[4 * p:4 * p + 4] for p in range(n_p)], res[4 * n_p]


def kernel(x, mix_norm_g, w_in, ret_norm_g, sgu_ln_g, sgu_ln_b, sgu_w_s, sgu_b_s, w_out, ffn_norm_g, w_up, conv_w, conv_b, w_down, final_norm_g, loss_target, m_mix_norm_g, m_w_in, m_ret_norm_g, m_sgu_ln_g, m_sgu_ln_b, m_sgu_w_s, m_sgu_b_s, m_w_out, m_ffn_norm_g, m_w_up, m_conv_w, m_conv_b, m_w_down, m_final_norm_g, v_mix_norm_g, v_w_in, v_ret_norm_g, v_sgu_ln_g, v_sgu_ln_b, v_sgu_w_s, v_sgu_b_s, v_w_out, v_ffn_norm_g, v_w_up, v_conv_w, v_conv_b, v_w_down, v_final_norm_g):
    xs = x[0]
    tgt = loss_target[0]
    mask, qdec, kdec = _decay_tables()
    grn = ret_norm_g.reshape(1, RET_W)
    lng = sgu_ln_g.reshape(1, SGU_W)
    lnb = sgu_ln_b.reshape(1, SGU_W)
    ws = sgu_w_s[0]
    bsb = jnp.broadcast_to(sgu_b_s[0][:, :, None], (HEADS, CHUNK, HEAD_DIM))
    gf = final_norm_g.reshape(1, D_MODEL)
    me = 4 * lax.axis_index("x") + 2 * lax.axis_index("y") + lax.axis_index("c")
    tr = lambda a: jnp.transpose(a[0])[None]
    tr_cw = lambda a: jnp.transpose(a, (1, 0, 2))

    proj, h1, cos2, sin2, win_g, cw_sh, wout_g, wdn_g, su = _fwd_proj(
        xs, mix_norm_g, _rope_freq(), w_in[0], w_out[0], tr(w_up)[0], w_down[0], tr_cw(conv_w))
    cw_g = jnp.transpose(cw_sh, (1, 0, 2)).reshape(8, 2 * D_FF)
    x2, mixcat, o, sprev, wup_g = _fwd_mix(xs, proj, wout_g, grn, lng, lnb, ws, bsb, mask, qdec, kdec, su)
    h2, up_pre, u_conv, act, x3, loss_parts = _fwd_ffn(x2, ffn_norm_g, wup_g, cw_g, conv_b, wdn_g, gf, tgt)

    dx3, dpre, dx2, dgf, dg2, dcv = _bwd_ffn(x3, tgt, gf, x2, ffn_norm_g, up_pre, u_conv, wup_g, cw_g, wdn_g)
    band = 512
    (gdn_p,) = _wgrad("wgrad_down", act, dx3, tm=FF_TILE)
    (gout_p,) = _wgrad("wgrad_out", mixcat, dx2, tn=512)
    gup_p, g_dn = _wgrad("wgrad_up", dpre, h2, tm=FF_TILE, tn=512, hosted=[(W_DOWN, gdn_p)])
    dproj, dgrn, dlng, dlnb, dws, dbs, g_up_a, g_out = _bwd_mix(
        dx2, proj, o, sprev, wout_g, grn, lng, lnb, ws, bsb, mask, qdec, kdec, cos2, sin2,
        [((W_UP, 0, band), gup_p), (W_OUT, gout_p)])
    gin_p, g_up_b = _wgrad("wgrad_in", h1, dproj, tm=512, tn=768, hosted=[((W_UP, band, FF_SHARD - band), gup_p)])
    grad_x, g_in, rp, rws, rcv = _bwd_proj(dproj, win_g, xs, mix_norm_g, dx2, gin_p,
                                           (dg2, dgf, dgrn, dlng, dlnb, dbs, loss_parts, dws, dcv))
    gcw = tr_cw(lax.dynamic_slice(rcv, (me // (N_DEV // 2), 0, (me % (N_DEV // 2)) * FF_SHARD), (1, 3, FF_SHARD)))

    table = {}
    for name, w, gs, m, v, rows in (("w_in", w_in, [g_in], m_w_in, v_w_in, 256), ("w_out", w_out, [g_out], m_w_out, v_w_out, 128),
                                    ("w_up", tr(w_up), [g_up_a, g_up_b], tr(m_w_up), tr(v_w_up), 64),
                                    ("w_down", w_down, [g_dn], m_w_down, v_w_down, 88)):
        table[name] = _adamw("adamw_" + name, w, gs, m, v, rows)
    table["w_up"] = tuple(tr(a) for a in table["w_up"])
    row = lambda a: a.reshape(1, D_MODEL)
    names_small = ["mix_norm_g", "ffn_norm_g", "final_norm_g", "ret_norm_g", "sgu_ln_g", "sgu_ln_b", "sgu_b_s", "sgu_w_s",
                   "conv_w", "conv_b"]
    params = [(mix_norm_g, m_mix_norm_g, v_mix_norm_g), (ffn_norm_g, m_ffn_norm_g, v_ffn_norm_g),
              (row(final_norm_g), row(m_final_norm_g), row(v_final_norm_g)), (ret_norm_g, m_ret_norm_g, v_ret_norm_g),
              (sgu_ln_g, m_sgu_ln_g, v_sgu_ln_g), (sgu_ln_b, m_sgu_ln_b, v_sgu_ln_b), (sgu_b_s, m_sgu_b_s, v_sgu_b_s),
              (sgu_w_s, m_sgu_w_s, v_sgu_w_s), (tr_cw(conv_w), tr_cw(m_conv_w), tr_cw(v_conv_w)), (conv_b, m_conv_b, v_conv_b)]
    small, loss = _adamw_small(rp, rws, rcv, gcw, params)
    for n, res in zip(names_small, small):
        table[n] = res
    table["final_norm_g"] = tuple(a.reshape(D_MODEL) for a in table["final_norm_g"])
    table["conv_w"] = tuple(tr_cw(a) for a in table["conv_w"])

    order = ["mix_norm_g", "w_in", "ret_norm_g", "sgu_ln_g", "sgu_ln_b", "sgu_w_s", "sgu_b_s", "w_out", "ffn_norm_g", "w_up",
             "conv_w", "conv_b", "w_down", "final_norm_g"]
    outs = [loss.reshape(()), grad_x[None]]
    for col in range(4):
        outs += [table[n][col] for n in order]
    return tuple(outs)
```

```python
import functools
import math

import jax
import jax.numpy as jnp
import numpy as np
from jax import lax
from jax.experimental import pallas as pl
from jax.experimental.pallas import tpu as pltpu

F32 = jnp.float32
BF16 = jnp.bfloat16
MESH = pl.DeviceIdType.MESH

N_DEV = 8
SEQ = 2048
D_MODEL = 1024
CHUNK = 128
N_CHUNK = SEQ // CHUNK
HEADS = 4
HEAD_DIM = 128
RET_W = 512
SGU_W = 512
PROJ_W = 3072
D_FF = 2816
FF_SHARD = 704
FF_TILE = 1408
FF_TILES = ((0, 1536), (1536, 1280))
IN_SHARD = PROJ_W // N_DEV
OUT_SHARD = D_MODEL // N_DEV
DOWN_SHARD = D_FF // N_DEV
TM = 256
N_TB = SEQ // TM
FWD_PROJ_PASS_AT = 5
FWD_MIX_PASS_AT = 10
EPS = 1e-6
ROPE_BASE = 10000.0
K_SCALE = HEAD_DIM ** -0.5
INV_SQRT2 = 0.7071067811865476
INV_SQRT_2PI = 0.3989422804014327

ADAM_LR = 0.001
ADAM_B1 = 0.9
ADAM_B2 = 0.999
ADAM_EPS = 1e-08
ADAM_WD = 0.01
ADAM_STEP = 10

VMEM_LIMIT = 56 * 1024 * 1024


def _cparams(sem=None, vmem=VMEM_LIMIT, collective=None):
    return pltpu.CompilerParams(dimension_semantics=sem, vmem_limit_bytes=vmem, collective_id=collective)


COLLECTIVE = {name: k for k, name in enumerate(("fwd_proj", "fwd_mix", "wgrad_up", "bwd_mix", "wgrad_in", "bwd_proj"))}


class _Meet:
    def __init__(self, diagonal):
        x, y, c = lax.axis_index("x"), lax.axis_index("y"), lax.axis_index("c")
        self.peers = [(x, y, 1 - c), (1 - x, y, c), (x, 1 - y, c)] + ([(1 - x, 1 - y, c)] if diagonal else [])

    def signal(self):
        for peer in self.peers:
            pl.semaphore_signal(pltpu.get_barrier_semaphore(), inc=1, device_id=peer, device_id_type=MESH)

    def wait(self):
        pl.semaphore_wait(pltpu.get_barrier_semaphore(), len(self.peers))


def _resident(shape):
    nd = len(shape)
    return pl.BlockSpec(shape, lambda *_: (0,) * nd, pipeline_mode=pl.Buffered(1))


def _dot(a, b):
    return jnp.dot(a, b, preferred_element_type=F32)


def _dot_nt(a, b):
    return lax.dot_general(a, b, (((1,), (1,)), ((), ())), preferred_element_type=F32)


def _dot_tn(a, b):
    return lax.dot_general(a, b, (((0,), (0,)), ((), ())), preferred_element_type=F32)


def _sigmoid(x):
    return 1.0 / (1.0 + jnp.exp(-x))


def _gelu(x):
    return 0.5 * x * (1.0 + lax.erf(x * INV_SQRT2))


def _gelu_grad(x):
    return 0.5 * (1.0 + lax.erf(x * INV_SQRT2)) + x * (jnp.exp(-0.5 * x * x) * INV_SQRT_2PI)


def _rot(xh, cos2, sin2):
    return xh * cos2 + pltpu.roll(xh, HEAD_DIM // 2, 1) * sin2


def _rot_t(dh, cos2, sin2):
    return dh * cos2 + pltpu.roll(dh * sin2, HEAD_DIM // 2, 1)


def _rope_freq():
    half = HEAD_DIM // 2
    inv_freq = jnp.power(ROPE_BASE, -jnp.arange(half, dtype=F32) / half)
    return jnp.concatenate([inv_freq, inv_freq])[None, :]


def _rope_block(inv2, first_row):
    pos = (lax.broadcasted_iota(jnp.int32, (TM, HEAD_DIM), 0) + first_row).astype(F32)
    ang = pos * inv2
    sin = jnp.sin(ang)
    lane = lax.broadcasted_iota(jnp.int32, (TM, HEAD_DIM), 1)
    return jnp.cos(ang), jnp.where(lane < HEAD_DIM // 2, -sin, sin)


def _decay_tables():
    log_gamma = jnp.log(1.0 - jnp.power(2.0, -5.0 - jnp.arange(HEADS, dtype=F32)))
    pos = jnp.arange(CHUNK, dtype=F32)
    diff = pos[:, None] - pos[None, :]
    mask = jnp.where(diff >= 0.0, jnp.exp(log_gamma[:, None, None] * jnp.maximum(diff, 0.0)[None]), 0.0)
    k_decay = jnp.exp(log_gamma[:, None] * (CHUNK - 1.0 - pos)[None])
    q_decay = jnp.exp(log_gamma[:, None] * (pos + 1.0)[None])
    kd = jnp.broadcast_to(k_decay[:, :, None], (HEADS, CHUNK, HEAD_DIM))
    qd = jnp.broadcast_to(q_decay[:, :, None], (HEADS, CHUNK, HEAD_DIM))
    return mask.astype(F32), qd.astype(F32), kd.astype(F32)


def _chunk_decay():
    lg = np.log(np.float32(1.0) - np.power(np.float32(2.0), -5.0 - np.arange(HEADS, dtype=np.float32))).astype(np.float32)
    return [float(np.exp(lg[h] * np.float32(CHUNK))) for h in range(HEADS)]


W_IN, W_OUT, W_UP, W_DOWN, W_CONV = range(5)
GATHERED = {W_IN: ((D_MODEL, PROJ_W), BF16), W_OUT: ((D_MODEL, D_MODEL), BF16), W_UP: ((2 * D_FF, D_MODEL), BF16),
            W_DOWN: ((D_FF, D_MODEL), BF16), W_CONV: ((N_DEV, 8, FF_SHARD), F32)}
SHARD = {W_IN: (D_MODEL, IN_SHARD), W_OUT: (OUT_SHARD, D_MODEL), W_UP: (FF_SHARD, D_MODEL), W_DOWN: (DOWN_SHARD, D_MODEL),
         W_CONV: (8, FF_SHARD)}


class _Gather:
    N_SEMS = 9

    def __init__(self, ids, stages, gathered, send_sems, recv_sems, local_sems):
        self.ids, self.stages, self.gathered = ids, stages, gathered
        self.send_sems, self.recv_sems, self.local_sems = send_sems, recv_sems, local_sems
        self.x, self.y, self.c = lax.axis_index("x"), lax.axis_index("y"), lax.axis_index("c")
        self.me = (self.x, self.y, self.c)
        self.sibling = (self.x, self.y, 1 - self.c)
        self.chips = [(1 - self.x, self.y), (self.x, 1 - self.y), (1 - self.x, 1 - self.y)]

    def slot(self, n, px, py, pc):
        dev = 4 * px + 2 * py + pc
        w, g = self.ids[n], self.gathered[n]
        if w == W_IN:
            return g.at[:, pl.ds(pl.multiple_of(dev * IN_SHARD, 128), IN_SHARD)]
        if w == W_OUT:
            return g.at[pl.ds(pl.multiple_of(dev * OUT_SHARD, 128), OUT_SHARD), :]
        if w == W_DOWN:
            return g.at[pl.ds(pl.multiple_of(dev * DOWN_SHARD, 32), DOWN_SHARD), :]
        if w == W_UP:
            return g.at[pl.ds(pl.multiple_of(dev * FF_SHARD, 32), FF_SHARD), :]
        return g.at[dev]

    def half(self, n, px, py, pc, h):
        dev = 4 * px + 2 * py + pc
        w, g = self.ids[n], self.gathered[n]
        if w == W_IN:
            return g.at[pl.ds(h * (D_MODEL // 2), D_MODEL // 2), pl.ds(pl.multiple_of(dev * IN_SHARD, 128), IN_SHARD)]
        rows = SHARD[w][0] // 2
        return g.at[pl.ds(pl.multiple_of(dev * SHARD[w][0] + h * rows, 16), rows), :]

    def tree(self, n):
        return self.ids[n] != W_CONV

    def copy(self, n, k, block, to, src=None, h=None):
        ref = self.slot(n, *block) if h is None else self.half(n, *block, h)
        return pltpu.make_async_remote_copy(
            src_ref=ref if src is None else src, dst_ref=ref,
            send_sem=self.send_sems.at[n, k], recv_sem=self.recv_sems.at[n, k], device_id=to, device_id_type=MESH)

    def _mine(self):
        return [pltpu.make_async_copy(self.stages[n], self.slot(n, *self.me), self.local_sems.at[n]) for n in range(len(self.ids))]

    def _first(self):
        out = []
        for n in range(len(self.ids)):
            out.append(self.copy(n, 0, self.me, self.sibling, src=self.stages[n]))
            out += [self.copy(n, 1 + j, self.me, (*chip, self.c), src=self.stages[n])
                    for j, chip in enumerate(self.chips[:2] if self.tree(n) else self.chips)]
        return out

    def start(self):
        for cp in self._mine() + self._first():
            cp.start()

    def _passed(self, j):
        dev = (*self.chips[j], self.c)
        out = []
        for n in range(len(self.ids)):
            if not self.tree(n):
                out.append(self.copy(n, 4 + j, dev, self.sibling))
            elif j < 2:
                out += [self.copy(n, 3 + j, dev, (*self.chips[1 - j], self.c), h=j), self.copy(n, 5 + j, dev, self.sibling)]
            else:
                out += [self.copy(n, 7, dev, self.sibling, h=0), self.copy(n, 8, dev, self.sibling, h=1)]
        return out

    def near(self):
        for j in range(2):
            dev = (*self.chips[j], self.c)
            for n in range(len(self.ids)):
                self.copy(n, 1 + j, dev, self.me).wait_recv()
            for cp in self._passed(j):
                cp.start()

    def finish(self):
        dev = (*self.chips[2], self.c)
        for n in range(len(self.ids)):
            if self.tree(n):
                self.copy(n, 3, dev, self.me, h=0).wait_recv()
                self.copy(n, 4, dev, self.me, h=1).wait_recv()
            else:
                self.copy(n, 3, dev, self.me).wait_recv()
        for cp in self._passed(2):
            cp.start()
        for n in range(len(self.ids)):
            self.copy(n, 0, self.sibling, self.me).wait_recv()
            for j, chip in enumerate(self.chips):
                dev = (*chip, 1 - self.c)
                if not self.tree(n):
                    self.copy(n, 4 + j, dev, self.me).wait_recv()
                elif j < 2:
                    self.copy(n, 5 + j, dev, self.me).wait_recv()
                else:
                    self.copy(n, 7, dev, self.me, h=0).wait_recv()
                    self.copy(n, 8, dev, self.me, h=1).wait_recv()
        for cp in self._mine():
            cp.wait()
        for cp in self._first() + self._passed(0) + self._passed(1) + self._passed(2):
            cp.wait_send()


def _gather_scratch(n):
    return [pltpu.SemaphoreType.DMA((n, _Gather.N_SEMS)), pltpu.SemaphoreType.DMA((n, _Gather.N_SEMS)), pltpu.SemaphoreType.DMA((n,))]


def _gathered_shapes(ids):
    return tuple(jax.ShapeDtypeStruct(*GATHERED[w]) for w in ids)


def _fwd_proj(x, g1, inv2, w_in, w_out, w_up, w_down, conv_w):
    ids_a, ids_b = [W_IN, W_CONV], [W_OUT, W_DOWN]

    def body(x_ref, g_ref, inv_ref, in_hbm, out_hbm, up_hbm, dn_hbm, cw_ref,
             proj_ref, h1_ref, cos_ref, sin_ref, gin, gcw, gout, gdn, su_ref,
             w_vm, s_in, s_cw, s_out, s_dn, f_in, f_out, f_up, f_dn, ld_sems,
             a_send, a_recv, a_local, b_send, b_recv, b_local):
        ag_a = _Gather(ids_a, [s_in, s_cw], [gin, gcw], a_send, a_recv, a_local)
        ag_b = _Gather(ids_b, [s_out, s_dn], [gout, gdn], b_send, b_recv, b_local)

        @pl.when(pl.program_id(0) == 0)
        def _():
            meet = _Meet(diagonal=True)
            meet.signal()
            loads = [pltpu.make_async_copy(src, dst, ld_sems.at[i])
                     for i, (src, dst) in enumerate(((in_hbm, f_in), (out_hbm, f_out), (dn_hbm, f_dn), (up_hbm, f_up)))]
            for cp in loads:
                cp.start()
            s_cw[...] = jnp.zeros_like(s_cw)
            for k in range(3):
                s_cw[k:k + 1, :] = cw_ref[k]
            loads[0].wait()
            s_in[...] = f_in[...].astype(BF16)
            meet.wait()
            ag_a.start()
            loads[1].wait()
            s_out[...] = f_out[...].astype(BF16)
            loads[2].wait()
            s_dn[...] = f_dn[...].astype(BF16)
            ag_a.near()
            ag_b.start()
            loads[3].wait()
            su_ref[...] = f_up[...].astype(BF16)
            ag_a.finish()
            fill = pltpu.make_async_copy(gin, w_vm, ld_sems.at[4])
            fill.start()
            fill.wait()

        pl.when(pl.program_id(0) == FWD_PROJ_PASS_AT)(ag_b.near)

        xb = x_ref[...]
        r = lax.rsqrt(jnp.mean(xb * xb, axis=-1, keepdims=True) + EPS)
        h = ((xb * r) * g_ref[...]).astype(BF16)
        h1_ref[...] = h
        p = _dot(h, w_vm[...])
        c2, s2 = _rope_block(inv_ref[...], pl.program_id(0) * TM)
        cos_ref[...], sin_ref[...] = c2, s2
        for hd in range(HEADS):
            sl = slice(hd * HEAD_DIM, (hd + 1) * HEAD_DIM)
            proj_ref[:, sl] = _rot(p[:, sl], c2, s2)
            ks = slice(RET_W + hd * HEAD_DIM, RET_W + (hd + 1) * HEAD_DIM)
            proj_ref[:, ks] = _rot(p[:, ks], c2, s2) * K_SCALE
        proj_ref[:, 2 * RET_W:] = p[:, 2 * RET_W:]

        pl.when(pl.program_id(0) == N_TB - 1)(ag_b.finish)

    tok = lambda w: pl.BlockSpec((TM, w), lambda i: (i, 0))
    hbm = pl.BlockSpec(memory_space=pl.ANY)
    vm = pl.BlockSpec(memory_space=pltpu.VMEM)
    return pl.pallas_call(
        body, name="fwd_proj", grid=(N_TB,),
        out_shape=(jax.ShapeDtypeStruct((SEQ, PROJ_W), F32), jax.ShapeDtypeStruct((SEQ, D_MODEL), BF16),
                   jax.ShapeDtypeStruct((SEQ, HEAD_DIM), F32), jax.ShapeDtypeStruct((SEQ, HEAD_DIM), F32))
        + _gathered_shapes(ids_a + ids_b) + (jax.ShapeDtypeStruct(SHARD[W_UP], BF16),),
        in_specs=[tok(D_MODEL), _resident((1, D_MODEL)), _resident((1, HEAD_DIM)), hbm, hbm, hbm, hbm, vm],
        out_specs=(tok(PROJ_W), tok(D_MODEL), tok(HEAD_DIM), tok(HEAD_DIM), hbm, hbm, hbm, hbm, vm),
        scratch_shapes=[pltpu.VMEM((D_MODEL, PROJ_W), BF16), pltpu.VMEM(SHARD[W_IN], BF16), pltpu.VMEM(SHARD[W_CONV], F32),
                        pltpu.VMEM(SHARD[W_OUT], BF16), pltpu.VMEM(SHARD[W_DOWN], BF16),
                        pltpu.VMEM(SHARD[W_IN], F32), pltpu.VMEM(SHARD[W_OUT], F32), pltpu.VMEM(SHARD[W_UP], F32),
                        pltpu.VMEM(SHARD[W_DOWN], F32), pltpu.SemaphoreType.DMA((5,))]
        + _gather_scratch(len(ids_a)) + _gather_scratch(len(ids_b)),
        compiler_params=_cparams(("arbitrary",), collective=COLLECTIVE["fwd_proj"]),
    )(x, g1, inv2, w_in, w_out, w_up, w_down, conv_w)


def _causal(w):
    r = lax.broadcasted_iota(jnp.int32, (CHUNK, CHUNK), 0)
    c = lax.broadcasted_iota(jnp.int32, (CHUNK, CHUNK), 1)
    return jnp.where(r >= c, w, 0.0)


def _fwd_mix(x, proj, wout_g, grn, lng, lnb, ws, bsb, mask, qdec, kdec, su):
    cdec = _chunk_decay()
    ids = [W_UP]

    def body(x_ref, p_ref, w_ref, grn_ref, lng_ref, lnb_ref, ws_ref, bsb_ref, m_ref, qd_ref, kd_ref, su_ref,
             x2_ref, cat_ref, o_ref, sp_ref, gup, state, send_sems, recv_sems, local_sems):
        ag = _Gather(ids, [su_ref], [gup], send_sems, recv_sems, local_sems)

        @pl.when(pl.program_id(0) == 0)
        def _():
            meet = _Meet(diagonal=False)
            meet.signal()
            state[...] = jnp.zeros_like(state)
            meet.wait()
            ag.start()

        for h in range(HEADS):
            sl = slice(h * HEAD_DIM, (h + 1) * HEAD_DIM)
            q = p_ref[:, sl]
            k = p_ref[:, RET_W + h * HEAD_DIM:RET_W + (h + 1) * HEAD_DIM]
            v = p_ref[:, 2 * RET_W + h * HEAD_DIM:2 * RET_W + (h + 1) * HEAD_DIM]
            g = p_ref[:, 3 * RET_W + h * HEAD_DIM:3 * RET_W + (h + 1) * HEAD_DIM]
            qb, kb, vb = q.astype(BF16), k.astype(BF16), v.astype(BF16)
            a = _dot_nt(qb, kb) * m_ref[h]
            spb = state[h].astype(BF16)
            sp_ref[0, h] = spb
            o = _dot(a.astype(BF16), vb) + _dot((q * qd_ref[h]).astype(BF16), spb)
            state[h] = state[h] * cdec[h] + _dot_tn((k * kd_ref[h]).astype(BF16), vb)
            o_ref[:, sl] = o
            rinv = lax.rsqrt(jnp.mean(o * o, axis=-1, keepdims=True) + EPS)
            rn = (o * rinv) * grn_ref[:, sl]
            cat_ref[:, sl] = ((g * _sigmoid(g)) * rn).astype(BF16)
        for gi in range(HEADS):
            sl = slice(gi * HEAD_DIM, (gi + 1) * HEAD_DIM)
            u = p_ref[:, 4 * RET_W + gi * HEAD_DIM:4 * RET_W + (gi + 1) * HEAD_DIM]
            sv = p_ref[:, 4 * RET_W + SGU_W + gi * HEAD_DIM:4 * RET_W + SGU_W + (gi + 1) * HEAD_DIM]
            gv = _gelu(sv)
            xc = gv - jnp.mean(gv, axis=-1, keepdims=True)
            vn = (xc * lax.rsqrt(jnp.mean(xc * xc, axis=-1, keepdims=True) + EPS)) * lng_ref[:, sl] + lnb_ref[:, sl]
            mixed = _dot(_causal(ws_ref[gi]).astype(BF16), vn.astype(BF16)) + bsb_ref[gi]
            cat_ref[:, RET_W + gi * HEAD_DIM:RET_W + (gi + 1) * HEAD_DIM] = (_gelu(u) * mixed).astype(BF16)
        x2_ref[...] = x_ref[...] + _dot(cat_ref[...], w_ref[...])

        pl.when(pl.program_id(0) == FWD_MIX_PASS_AT)(ag.near)
        pl.when(pl.program_id(0) == N_CHUNK - 1)(ag.finish)

    ch = lambda w: pl.BlockSpec((CHUNK, w), lambda i: (i, 0))
    hcc = (HEADS, CHUNK, CHUNK)
    hbm = pl.BlockSpec(memory_space=pl.ANY)
    return pl.pallas_call(
        body, name="fwd_mix", grid=(N_CHUNK,),
        out_shape=(jax.ShapeDtypeStruct((SEQ, D_MODEL), F32), jax.ShapeDtypeStruct((SEQ, D_MODEL), BF16),
                   jax.ShapeDtypeStruct((SEQ, RET_W), F32), jax.ShapeDtypeStruct((N_CHUNK, HEADS, HEAD_DIM, HEAD_DIM), BF16))
        + _gathered_shapes(ids),
        in_specs=[ch(D_MODEL), ch(PROJ_W), _resident((D_MODEL, D_MODEL)), _resident((1, RET_W)), _resident((1, SGU_W)),
                  _resident((1, SGU_W)), _resident(hcc), _resident(hcc), _resident(hcc), _resident(hcc), _resident(hcc), hbm],
        out_specs=(ch(D_MODEL), ch(D_MODEL), ch(RET_W), pl.BlockSpec((1, HEADS, HEAD_DIM, HEAD_DIM), lambda i: (i, 0, 0, 0)), hbm),
        scratch_shapes=[pltpu.VMEM((HEADS, HEAD_DIM, HEAD_DIM), F32)] + _gather_scratch(len(ids)),
        compiler_params=_cparams(("arbitrary",), collective=COLLECTIVE["fwd_mix"]),
    )(x, proj, wout_g, grn, lng, lnb, ws, bsb, mask, qdec, kdec, su)


def _conv_taps(p, prev8):
    row = lax.broadcasted_iota(jnp.int32, p.shape, 0)
    p1 = jnp.where(row == 0, prev8[7:8, :], pltpu.roll(p, 1, 0))
    p2 = jnp.where(row == 0, prev8[6:7, :], jnp.where(row == 1, prev8[7:8, :], pltpu.roll(p, 2, 0)))
    return p1, p2


class _FfnWeights:
    N_SEMS = 3 * len(FF_TILES)

    def __init__(self, wu_hbm, wd_hbm, wu_vm, wd_vm, sems):
        self.tiles = []
        for n, (t0, tw) in enumerate(FF_TILES):
            rows = [(wu_hbm, wu_vm, t0), (wu_hbm, wu_vm, D_FF + t0), (wd_hbm, wd_vm, t0)]
            self.tiles.append([pltpu.make_async_copy(src.at[pl.ds(r0, tw), :], dst.at[pl.ds(r0, tw), :], sems.at[3 * n + k])
                               for k, (src, dst, r0) in enumerate(rows)])

    def start(self):
        for tile in self.tiles:
            for cp in tile:
                cp.start()

    def wait(self, n):
        for cp in self.tiles[n]:
            cp.wait()


def _ffn_weight_scratch():
    return [pltpu.VMEM((2 * D_FF, D_MODEL), BF16), pltpu.VMEM((D_FF, D_MODEL), BF16), pltpu.SemaphoreType.DMA((_FfnWeights.N_SEMS,))]


def _fwd_ffn(x2, g2, wup_g, cw_g, cb_g, wdn_g, gf, tgt):
    def body(x_ref, g_ref, wu_hbm, cw_ref, cb_ref, wd_hbm, gf_ref, t_ref, h2_ref, up_ref, u_ref, act_ref, x3_ref, loss_ref,
             carry, wu_ref, wd_ref, w_sems):
        first = pl.program_id(0) == 0
        weights = _FfnWeights(wu_hbm, wd_hbm, wu_ref, wd_ref, w_sems)

        @pl.when(first)
        def _():
            weights.start()
            carry[...] = jnp.zeros_like(carry)
            weights.wait(0)

        xb = x_ref[...]
        r = lax.rsqrt(jnp.mean(xb * xb, axis=-1, keepdims=True) + EPS)
        h = ((xb * r) * g_ref[...]).astype(BF16)
        h2_ref[...] = h
        acc = xb
        for n, (t0, tw) in enumerate(FF_TILES):
            if n:
                pl.when(first)(functools.partial(weights.wait, n))
            u = []
            for c0 in (t0, D_FF + t0):
                cs = slice(c0, c0 + tw)
                p = _dot_nt(h, wu_ref[pl.ds(c0, tw), :])
                up_ref[:, cs] = p.astype(BF16)
                p1, p2 = _conv_taps(p, carry[:, cs])
                carry[:, cs] = p[TM - 8:, :]
                us = p2 * cw_ref[0:1, cs] + p1 * cw_ref[1:2, cs] + p * cw_ref[2:3, cs] + cb_ref[:, cs]
                u_ref[:, cs] = us.astype(BF16)
                u.append(us)
            a = ((u[0] * _sigmoid(u[0])) * u[1]).astype(BF16)
            act_ref[:, t0:t0 + tw] = a
            acc = acc + _dot(a, wd_ref[pl.ds(t0, tw), :])
        x3_ref[...] = acc
        r3 = lax.rsqrt(jnp.mean(acc * acc, axis=-1, keepdims=True) + EPS)
        diff = (acc * r3) * gf_ref[...] - t_ref[...]
        loss_ref[...] = jnp.full(loss_ref.shape, 0.5 * jnp.sum(jnp.mean(diff * diff, axis=-1)), F32)

    tok = lambda w: pl.BlockSpec((TM, w), lambda i: (i, 0))
    hbm = pl.BlockSpec(memory_space=pl.ANY)
    return pl.pallas_call(
        body, name="fwd_ffn", grid=(N_TB,),
        out_shape=(jax.ShapeDtypeStruct((SEQ, D_MODEL), BF16), jax.ShapeDtypeStruct((SEQ, 2 * D_FF), BF16),
                   jax.ShapeDtypeStruct((SEQ, 2 * D_FF), BF16),
                   jax.ShapeDtypeStruct((SEQ, D_FF), BF16), jax.ShapeDtypeStruct((SEQ, D_MODEL), F32),
                   jax.ShapeDtypeStruct((N_TB, 8, 128), F32)),
        in_specs=[tok(D_MODEL), _resident((1, D_MODEL)), hbm, _resident((8, 2 * D_FF)),
                  _resident((1, 2 * D_FF)), hbm, _resident((1, D_MODEL)), tok(D_MODEL)],
        out_specs=(tok(D_MODEL), tok(2 * D_FF), tok(2 * D_FF), tok(D_FF), tok(D_MODEL),
                   pl.BlockSpec((1, 8, 128), lambda i: (i, 0, 0))),
        scratch_shapes=[pltpu.VMEM((8, 2 * D_FF), F32)] + _ffn_weight_scratch(),
        compiler_params=_cparams(("arbitrary",)),
    )(x2, g2, wup_g, cw_g, cb_g, wdn_g, gf, tgt)


def _bwd_ffn(x3, tgt, gf, x2, g2, up_pre, u_conv, wup_g, cw_g, wdn_g):
    def body(x3_ref, t_ref, gf_ref, x2_ref, g2_ref, up_ref, u_ref, wu_hbm, cw_ref, wd_hbm,
             dx3_ref, dpre_ref, dx2_ref, dgf_ref, dg2_ref, dcv_ref, nxt, wu_ref, wd_ref, w_sems):
        i = pl.program_id(0)
        weights = _FfnWeights(wu_hbm, wd_hbm, wu_ref, wd_ref, w_sems)

        @pl.when(i == 0)
        def _():
            weights.start()
            nxt[...] = jnp.zeros_like(nxt)
            dgf_ref[...] = jnp.zeros_like(dgf_ref)
            dg2_ref[...] = jnp.zeros_like(dg2_ref)
            dcv_ref[...] = jnp.zeros_like(dcv_ref)

        x3 = x3_ref[...]
        r3 = lax.rsqrt(jnp.mean(x3 * x3, axis=-1, keepdims=True) + EPS)
        xh3 = x3 * r3
        dy = (xh3 * gf_ref[...] - t_ref[...]) * (1.0 / D_MODEL)
        dgf_ref[0:1, :] += jnp.sum(dy * xh3, axis=0, keepdims=True)
        t3 = dy * gf_ref[...]
        dx3 = r3 * (t3 - xh3 * jnp.mean(t3 * xh3, axis=-1, keepdims=True))
        dx3b = dx3.astype(BF16)
        dx3_ref[...] = dx3b
        dh2 = jnp.zeros((TM, D_MODEL), F32)
        for n, (t0, tw) in enumerate(FF_TILES):
            pl.when(i == 0)(functools.partial(weights.wait, n))
            row = lax.broadcasted_iota(jnp.int32, (TM, tw), 0)
            ts = slice(t0, t0 + tw)
            dact = _dot_nt(dx3b, wd_ref[pl.ds(t0, tw), :])
            ua = u_ref[:, ts].astype(F32)
            ub = u_ref[:, D_FF + t0:D_FF + t0 + tw].astype(F32)
            sg = _sigmoid(ua)
            du = [dact * ub * (sg * (1.0 + ua * (1.0 - sg))), dact * (ua * sg)]
            for n in range(2):
                d = du[n]
                c0 = n * D_FF + t0
                cs = slice(c0, c0 + tw)
                nx = nxt[:, cs]
                n1 = jnp.where(row == TM - 1, nx[0:1, :], pltpu.roll(d, TM - 1, 0))
                n2 = jnp.where(row == TM - 2, nx[0:1, :], jnp.where(row == TM - 1, nx[1:2, :], pltpu.roll(d, TM - 2, 0)))
                nxt[:, cs] = d[0:8, :]
                dp = (d * cw_ref[2:3, cs] + n1 * cw_ref[1:2, cs] + n2 * cw_ref[0:1, cs]).astype(BF16)
                dpre_ref[:, cs] = dp
                p = up_ref[:, cs].astype(F32)
                dcv_ref[n, 0:1, ts] += jnp.sum(n2 * p, axis=0, keepdims=True)
                dcv_ref[n, 1:2, ts] += jnp.sum(n1 * p, axis=0, keepdims=True)
                dcv_ref[n, 2:3, ts] += jnp.sum(d * p, axis=0, keepdims=True)
                dcv_ref[n, 3:4, ts] += jnp.sum(d, axis=0, keepdims=True)
                dh2 = dh2 + _dot(dp, wu_ref[pl.ds(c0, tw), :])
        x2 = x2_ref[...]
        r2 = lax.rsqrt(jnp.mean(x2 * x2, axis=-1, keepdims=True) + EPS)
        xh2 = x2 * r2
        dg2_ref[0:1, :] += jnp.sum(dh2 * xh2, axis=0, keepdims=True)
        t2 = dh2 * g2_ref[...]
        dx2_ref[...] = dx3 + r2 * (t2 - xh2 * jnp.mean(t2 * xh2, axis=-1, keepdims=True))

    rev = lambda w: pl.BlockSpec((TM, w), lambda i: (N_TB - 1 - i, 0))
    acc = lambda s: pl.BlockSpec(s, lambda i: (0,) * len(s))
    return pl.pallas_call(
        body, name="bwd_ffn", grid=(N_TB,),
        out_shape=(jax.ShapeDtypeStruct((SEQ, D_MODEL), BF16), jax.ShapeDtypeStruct((SEQ, 2 * D_FF), BF16),
                   jax.ShapeDtypeStruct((SEQ, D_MODEL), F32), jax.ShapeDtypeStruct((8, D_MODEL), F32),
                   jax.ShapeDtypeStruct((8, D_MODEL), F32), jax.ShapeDtypeStruct((2, 8, D_FF), F32)),
        in_specs=[rev(D_MODEL), rev(D_MODEL), _resident((1, D_MODEL)), rev(D_MODEL), _resident((1, D_MODEL)), rev(2 * D_FF),
                  rev(2 * D_FF), pl.BlockSpec(memory_space=pl.ANY), _resident((8, 2 * D_FF)), pl.BlockSpec(memory_space=pl.ANY)],
        out_specs=(rev(D_MODEL), rev(2 * D_FF), rev(D_MODEL), acc((8, D_MODEL)), acc((8, D_MODEL)), acc((2, 8, D_FF))),
        scratch_shapes=[pltpu.VMEM((8, 2 * D_FF), F32)] + _ffn_weight_scratch(),
        compiler_params=_cparams(("arbitrary",)),
    )(x3, tgt, gf, x2, g2, up_pre, u_conv, wup_g, cw_g, wdn_g)


def _bwd_mix(dx2, proj, o, sprev, wout_g, grn, lng, lnb, ws, bsb, mask, qdec, kdec, cos2, sin2, hosted):
    cdec = _chunk_decay()
    geoms = [g for g, _ in hosted]
    n_h = len(hosted)

    def body(dx2_ref, p_ref, o_ref, sp_ref, w_ref, grn_ref, lng_ref, lnb_ref, ws_ref, bsb_ref, m_ref, qd_ref, kd_ref,
             cos_ref, sin_ref, *rest):
        dp_ref, dgrn_ref, dlng_ref, dlnb_ref, dws_ref, dbs_ref = rest[n_h:n_h + 6]
        dstate, dbs_acc = rest[2 * n_h + 6:2 * n_h + 8]
        i = pl.program_id(0)
        rs = _Scatters(geoms, rest[:n_h], rest[n_h + 6:2 * n_h + 6], rest[2 * n_h + 8:])
        pl.when(i == 0)(rs.phase1)
        pl.when(i == 3)(rs.phase2)
        pl.when(i == 8)(rs.phase2b)

        @pl.when(i == 0)
        def _():
            dstate[...] = jnp.zeros_like(dstate)
            dgrn_ref[...] = jnp.zeros_like(dgrn_ref)
            dlng_ref[...] = jnp.zeros_like(dlng_ref)
            dlnb_ref[...] = jnp.zeros_like(dlnb_ref)
            dws_ref[...] = jnp.zeros_like(dws_ref)
            dbs_ref[...] = jnp.zeros_like(dbs_ref)
            dbs_acc[...] = jnp.zeros_like(dbs_acc)

        dmix = _dot_nt(dx2_ref[...].astype(BF16), w_ref[...])
        for h in range(HEADS):
            sl = slice(h * HEAD_DIM, (h + 1) * HEAD_DIM)
            q = p_ref[:, sl]
            k = p_ref[:, RET_W + h * HEAD_DIM:RET_W + (h + 1) * HEAD_DIM]
            v = p_ref[:, 2 * RET_W + h * HEAD_DIM:2 * RET_W + (h + 1) * HEAD_DIM]
            g = p_ref[:, 3 * RET_W + h * HEAD_DIM:3 * RET_W + (h + 1) * HEAD_DIM]
            o = o_ref[:, sl]
            rinv = lax.rsqrt(jnp.mean(o * o, axis=-1, keepdims=True) + EPS)
            oh = o * rinv
            gr = grn_ref[:, sl]
            sg = _sigmoid(g)
            dret = dmix[:, sl]
            dp_ref[:, 3 * RET_W + h * HEAD_DIM:3 * RET_W + (h + 1) * HEAD_DIM] = (
                dret * (oh * gr) * (sg * (1.0 + g * (1.0 - sg)))).astype(BF16)
            drn = dret * (g * sg)
            dgrn_ref[0:1, sl] += jnp.sum(drn * oh, axis=0, keepdims=True)
            t = drn * gr
            do = rinv * (t - oh * jnp.mean(t * oh, axis=-1, keepdims=True))
            qb, kb, vb, dob = q.astype(BF16), k.astype(BF16), v.astype(BF16), do.astype(BF16)
            m = m_ref[h]
            ab = (_dot_nt(qb, kb) * m).astype(BF16)
            dab = (_dot_nt(dob, vb) * m).astype(BF16)
            spb = sp_ref[0, h]
            dsn = dstate[h]
            dsnb = dsn.astype(BF16)
            qdb = (q * qd_ref[h]).astype(BF16)
            kdb = (k * kd_ref[h]).astype(BF16)
            dq = _dot(dab, kb) + _dot_nt(dob, spb) * qd_ref[h]
            dk = _dot_tn(dab, qb) + _dot_nt(vb, dsnb) * kd_ref[h]
            dv = _dot_tn(ab, dob) + _dot(kdb, dsnb)
            dstate[h] = dsn * cdec[h] + _dot_tn(qdb, dob)
            c2, s2 = cos_ref[...], sin_ref[...]
            dp_ref[:, sl] = _rot_t(dq, c2, s2).astype(BF16)
            dp_ref[:, RET_W + h * HEAD_DIM:RET_W + (h + 1) * HEAD_DIM] = _rot_t(dk * K_SCALE, c2, s2).astype(BF16)
            dp_ref[:, 2 * RET_W + h * HEAD_DIM:2 * RET_W + (h + 1) * HEAD_DIM] = dv.astype(BF16)
        for gi in range(HEADS):
            sl = slice(gi * HEAD_DIM, (gi + 1) * HEAD_DIM)
            u = p_ref[:, 4 * RET_W + gi * HEAD_DIM:4 * RET_W + (gi + 1) * HEAD_DIM]
            sv = p_ref[:, 4 * RET_W + SGU_W + gi * HEAD_DIM:4 * RET_W + SGU_W + (gi + 1) * HEAD_DIM]
            gv = _gelu(sv)
            xc = gv - jnp.mean(gv, axis=-1, keepdims=True)
            rstd = lax.rsqrt(jnp.mean(xc * xc, axis=-1, keepdims=True) + EPS)
            xh = xc * rstd
            lg = lng_ref[:, sl]
            vnb = (xh * lg + lnb_ref[:, sl]).astype(BF16)
            wcb = _causal(ws_ref[gi]).astype(BF16)
            mixed = _dot(wcb, vnb) + bsb_ref[gi]
            dsgu = dmix[:, RET_W + gi * HEAD_DIM:RET_W + (gi + 1) * HEAD_DIM]
            dmixed = dsgu * _gelu(u)
            dmb = dmixed.astype(BF16)
            dws_ref[gi] += _causal(_dot_nt(dmb, vnb))
            dbs_acc[gi] += dmixed
            dvn = _dot_tn(wcb, dmb)
            dlng_ref[gi:gi + 1, :] += jnp.sum(dvn * xh, axis=0, keepdims=True)
            dlnb_ref[gi:gi + 1, :] += jnp.sum(dvn, axis=0, keepdims=True)
            dxh = dvn * lg
            dgv = rstd * (dxh - jnp.mean(dxh, axis=-1, keepdims=True) - xh * jnp.mean(dxh * xh, axis=-1, keepdims=True))
            dp_ref[:, 4 * RET_W + gi * HEAD_DIM:4 * RET_W + (gi + 1) * HEAD_DIM] = (dsgu * mixed * _gelu_grad(u)).astype(BF16)
            dp_ref[:, 4 * RET_W + SGU_W + gi * HEAD_DIM:4 * RET_W + SGU_W + (gi + 1) * HEAD_DIM] = (
                dgv * _gelu_grad(sv)).astype(BF16)

        @pl.when(i == N_CHUNK - 1)
        def _():
            for gi in range(HEADS):
                col = jnp.broadcast_to(jnp.sum(dbs_acc[gi], axis=-1, keepdims=True), (CHUNK, CHUNK))
                dbs_ref[gi:gi + 1, :] = jnp.transpose(col)[0:1, :]
            rs.phase3()

    rev = lambda w: pl.BlockSpec((CHUNK, w), lambda i: (N_CHUNK - 1 - i, 0))
    hcc = (HEADS, CHUNK, CHUNK)
    acc = lambda s: pl.BlockSpec(s, lambda i: (0,) * len(s))
    res = pl.pallas_call(
        body, name="bwd_mix", grid=(N_CHUNK,),
        out_shape=(jax.ShapeDtypeStruct((SEQ, PROJ_W), BF16), jax.ShapeDtypeStruct((8, RET_W), F32),
                   jax.ShapeDtypeStruct((8, HEAD_DIM), F32), jax.ShapeDtypeStruct((8, HEAD_DIM), F32),
                   jax.ShapeDtypeStruct(hcc, F32), jax.ShapeDtypeStruct((8, CHUNK), F32)) + _scatter_out_shapes(geoms),
        in_specs=[rev(D_MODEL), rev(PROJ_W), rev(RET_W),
                  pl.BlockSpec((1, HEADS, HEAD_DIM, HEAD_DIM), lambda i: (N_CHUNK - 1 - i, 0, 0, 0)),
                  _resident((D_MODEL, D_MODEL)), _resident((1, RET_W)), _resident((1, SGU_W)), _resident((1, SGU_W)),
                  _resident(hcc), _resident(hcc), _resident(hcc), _resident(hcc), _resident(hcc), rev(HEAD_DIM), rev(HEAD_DIM)]
        + [pl.BlockSpec(memory_space=pl.ANY)] * n_h,
        out_specs=(rev(PROJ_W), acc((8, RET_W)), acc((8, HEAD_DIM)), acc((8, HEAD_DIM)), acc(hcc), acc((8, CHUNK)))
        + _scatter_out_specs(geoms),
        scratch_shapes=[pltpu.VMEM((HEADS, HEAD_DIM, HEAD_DIM), F32), pltpu.VMEM((HEADS, CHUNK, CHUNK), F32)] + _scatter_scratch(geoms),
        compiler_params=_cparams(("arbitrary",), collective=COLLECTIVE["bwd_mix"]),
    )(dx2, proj, o, sprev, wout_g, grn, lng, lnb, ws, bsb, mask, qdec, kdec, cos2, sin2, *[p for _, p in hosted])
    return tuple(res[:6 + n_h])


def _bwd_proj(dproj, win_g, x, g1, dx2, gin_p, small):
    geoms = [W_IN]
    n_s = len(small)

    def body(dp_ref, w_ref, x_ref, g_ref, dx2_ref, gin_ref, *rest):
        small_refs = rest[:n_s]
        dx_ref, rs_out, rp_ref, rws_ref, rcv_ref, dg_ref = rest[n_s:n_s + 6]
        rs_scratch = rest[n_s + 6:n_s + 6 + N_SCATTER_SCRATCH]
        ar_scratch = rest[n_s + 6 + N_SCATTER_SCRATCH:]
        ar_res = ar_scratch[N_SMALL_SCRATCH:]
        ar = _SmallReduce((dg_ref,) + tuple(small_refs), ar_res, ar_scratch[:N_SMALL_SCRATCH])
        rs = _Scatters(geoms, [gin_ref], [rs_out], rs_scratch)
        pl.when(pl.program_id(0) == 0)(lambda: rs.phase1(diagonal=True))
        pl.when(pl.program_id(0) == 1)(rs.phase2)
        pl.when(pl.program_id(0) == 5)(rs.phase2b)

        @pl.when(pl.program_id(0) == 0)
        def _():
            dg_ref[...] = jnp.zeros_like(dg_ref)

        dh = _dot_nt(dp_ref[...], w_ref[...])
        xb = x_ref[...]
        r = lax.rsqrt(jnp.mean(xb * xb, axis=-1, keepdims=True) + EPS)
        xh = xb * r
        dg_ref[0:1, :] += jnp.sum(dh * xh, axis=0, keepdims=True)
        t = dh * g_ref[...]
        dx_ref[...] = dx2_ref[...] + r * (t - xh * jnp.mean(t * xh, axis=-1, keepdims=True))

        @pl.when(pl.program_id(0) == N_TB - 1)
        def _():
            ar.begin()
            rs.phase3()
            ar.end()
            for o_ref, r_ref in zip((rp_ref, rws_ref, rcv_ref), ar_res):
                o_ref[...] = r_ref[...]

    tok = lambda w: pl.BlockSpec((TM, w), lambda i: (i, 0))
    vm = pl.BlockSpec(memory_space=pltpu.VMEM)
    res = pl.pallas_call(
        body, name="bwd_proj", grid=(N_TB,),
        out_shape=(jax.ShapeDtypeStruct((SEQ, D_MODEL), F32),) + _scatter_out_shapes(geoms)
        + tuple(jax.ShapeDtypeStruct(s, F32) for s in SMALL_FULL),
        in_specs=[tok(PROJ_W), _resident((D_MODEL, PROJ_W)), tok(D_MODEL), _resident((1, D_MODEL)), tok(D_MODEL),
                  pl.BlockSpec(memory_space=pl.ANY)] + [vm] * n_s,
        out_specs=(tok(D_MODEL),) + _scatter_out_specs(geoms) + (vm,) * len(SMALL_FULL),
        scratch_shapes=[pltpu.VMEM((8, D_MODEL), F32)] + _scatter_scratch(geoms) + _small_scratch()
        + [pltpu.VMEM(s, F32) for s in SMALL_FULL],
        compiler_params=_cparams(("arbitrary",), collective=COLLECTIVE["bwd_proj"]),
    )(dproj, win_g, x, g1, dx2, gin_p, *small)
    return res


def _wgrad(name, a, b, tm=None, tn=None, hosted=()):
    m_w, n_w = a.shape[-1], b.shape[-1]
    tm = m_w if tm is None else tm
    tn = n_w if tn is None else tn
    n_steps = (m_w // tm) * (n_w // tn)
    geoms = [g for g, _ in hosted]
    n_h = len(hosted)

    def body(a_ref, b_ref, *rest):
        o_ref = rest[n_h]
        if n_h:
            rs = _Scatters(geoms, rest[:n_h], rest[n_h + 1:2 * n_h + 1], rest[2 * n_h + 1:])
            step = pl.program_id(0) * (n_w // tn) + pl.program_id(1)
            pl.when(step == 0)(rs.phase1)
            pl.when(step == 1)(rs.phase2)
            pl.when(step == n_steps // 2)(rs.phase2b)
        o_ref[...] = _dot_tn(a_ref[...].astype(BF16), b_ref[...].astype(BF16)).astype(BF16)
        if n_h:
            pl.when(step == n_steps - 1)(rs.phase3)

    assert not n_h or n_steps >= 4
    res = pl.pallas_call(
        body, name=name, grid=(m_w // tm, n_w // tn),
        out_shape=(jax.ShapeDtypeStruct((m_w, n_w), BF16),) + _scatter_out_shapes(geoms),
        in_specs=[pl.BlockSpec((SEQ, tm), lambda i, j: (0, i)), pl.BlockSpec((SEQ, tn), lambda i, j: (0, j))]
        + [pl.BlockSpec(memory_space=pl.ANY)] * n_h,
        out_specs=(pl.BlockSpec((tm, tn), lambda i, j: (i, j)),) + _scatter_out_specs(geoms),
        scratch_shapes=_scatter_scratch(geoms),
        compiler_params=_cparams(("arbitrary", "arbitrary"), collective=COLLECTIVE[name]) if n_h else _cparams(("parallel", "parallel")),
    )(a, b, *[p for _, p in hosted])
    return tuple(res[:1 + n_h])


def _row_step(half_rows):
    return max(s for s in range(16, 177, 16) if half_rows % s == 0)


class _Scatter:
    def __init__(self, geom, partial, out, land1, mine, stage2, land2, comb, s1_send, s1_recv, s2_send, s2_recv, ld_sems):
        self.w, self.row0, self.shape = _geom(geom)
        self.partial, self.out, self.land1 = partial, out, land1
        self.mine, self.stage2, self.land2, self.comb = mine, stage2, land2, comb
        self.hr = self.shape[0] // 2
        self.step = _row_step(self.hr)
        self.s1_send, self.s1_recv, self.s2_send, self.s2_recv, self.ld_sems = s1_send, s1_recv, s2_send, s2_recv, ld_sems
        self.x, self.y, self.c = lax.axis_index("x"), lax.axis_index("y"), lax.axis_index("c")
        self.sibling = (self.x, self.y, 1 - self.c)
        self.chips = [(self.x, self.y), (1 - self.x, self.y), (self.x, 1 - self.y), (1 - self.x, 1 - self.y)]

    def block(self, px, py, pc):
        dev = 4 * px + 2 * py + pc
        if self.w == W_IN:
            return self.partial.at[:, pl.ds(pl.multiple_of(dev * IN_SHARD, 128), IN_SHARD)]
        if self.w == W_OUT:
            return self.partial.at[pl.ds(pl.multiple_of(dev * OUT_SHARD, 128), OUT_SHARD), :]
        if self.w == W_DOWN:
            return self.partial.at[pl.ds(pl.multiple_of(dev * DOWN_SHARD, 32), DOWN_SHARD), :]
        return self.partial.at[pl.ds(pl.multiple_of(dev * FF_SHARD + self.row0, 32), self.shape[0]), :]

    def copy1(self, k):
        return pltpu.make_async_remote_copy(
            src_ref=self.block(*self.chips[k], 1 - self.c), dst_ref=self.land1.at[k],
            send_sem=self.s1_send.at[k], recv_sem=self.s1_recv.at[k], device_id=self.sibling, device_id_type=MESH)

    STAGE2 = [(1, 0, 1), (3, 0, 1), (2, 1, 2), (3, 1, 2), (1, 1, 1), (2, 0, 2)]

    def copy2(self, j):
        blk, h, to = self.STAGE2[j]
        src = self.comb.at[j - 4] if j >= 4 else self.stage2.at[blk - 1, pl.ds(h * self.hr, self.hr), :]
        return pltpu.make_async_remote_copy(
            src_ref=src, dst_ref=self.land2.at[j], send_sem=self.s2_send.at[j], recv_sem=self.s2_recv.at[j],
            device_id=(*self.chips[to], self.c), device_id_type=MESH)

    def _rows(self, h=None):
        step = self.step
        lo, n = (0, self.shape[0]) if h is None else (h * self.hr, self.hr)
        return [pl.ds(r0, step) for r0 in range(lo, lo + n, step)]

    def load(self, k):
        return pltpu.make_async_copy(self.block(*self.chips[k], self.c), self.mine.at[k], self.ld_sems.at[k])

    def load_mine(self):
        for k in range(4):
            self.load(k).start()

    def phase1(self):
        for k in range(4):
            self.copy1(k).start()

    def phase2(self, k):
        self.copy1(k).wait_recv()
        self.load(k).wait()
        for rs in self._rows():
            s = self.mine[k, rs, :].astype(F32) + self.land1[k, rs, :].astype(F32)
            if k == 0:
                self.out[rs, :] = s
            else:
                self.stage2[k - 1, rs, :] = s.astype(BF16)
        for j in {3: (1, 3), 1: (0,), 2: (2,), 0: ()}[k]:
            self.copy2(j).start()

    def phase2b(self):
        for j, got in ((4, 3), (5, 1)):
            blk, h, _ = self.STAGE2[j]
            self.copy2(got).wait_recv()
            for i, rs in enumerate(self._rows(h)):
                lr = pl.ds(i * self.step, self.step)
                self.comb[j - 4, lr, :] = (self.stage2[blk - 1, rs, :].astype(F32) + self.land2[got, lr, :].astype(F32)).astype(BF16)
            self.copy2(j).start()

    def phase3(self):
        for j in (0, 5, 4, 2):
            self.copy2(j).wait_recv()
        for h, (first, second) in enumerate(((0, 5), (4, 2))):
            for i, rs in enumerate(self._rows(h)):
                lr = pl.ds(i * self.step, self.step)
                self.out[rs, :] = (self.out[rs, :] + self.land2[first, lr, :].astype(F32)) + self.land2[second, lr, :].astype(F32)
        for k in range(4):
            self.copy1(k).wait_send()
        for j in range(6):
            self.copy2(j).wait_send()


def _geom(geom):
    if isinstance(geom, tuple):
        w, row0, rows = geom
        assert w == W_UP
        return w, row0, (rows, SHARD[w][1])
    return geom, 0, SHARD[geom]


N_SCATTER_SCRATCH = 10


def _scatter_out_shapes(geoms):
    return tuple(jax.ShapeDtypeStruct(_geom(g)[2], F32) for g in geoms)


def _scatter_out_specs(geoms):
    return (pl.BlockSpec(memory_space=pltpu.VMEM),) * len(geoms)


def _scatter_scratch(geoms):
    out = []
    for g in geoms:
        s = _geom(g)[2]
        hs = (s[0] // 2, s[1])
        out += [pltpu.VMEM((4,) + s, BF16), pltpu.VMEM((4,) + s, BF16), pltpu.VMEM((3,) + s, BF16), pltpu.VMEM((6,) + hs, BF16),
                pltpu.VMEM((2,) + hs, BF16),
                pltpu.SemaphoreType.DMA((4,)), pltpu.SemaphoreType.DMA((4,)), pltpu.SemaphoreType.DMA((6,)),
                pltpu.SemaphoreType.DMA((6,)), pltpu.SemaphoreType.DMA((4,))]
    return out


class _Scatters:
    def __init__(self, geoms, p_refs, out_refs, scratch):
        k = N_SCATTER_SCRATCH
        self.items = [_Scatter(g, p_refs[i], out_refs[i], *scratch[k * i:k * i + k]) for i, g in enumerate(geoms)]

    def phase1(self, diagonal=False):
        meet = _Meet(diagonal)
        meet.signal()
        for s in self.items:
            s.load_mine()
        meet.wait()
        for s in self.items:
            s.phase1()

    def phase2(self):
        for k in (3, 1, 2, 0):
            for s in self.items:
                s.phase2(k)

    def phase2b(self):
        for s in self.items:
            s.phase2b()

    def phase3(self):
        for s in self.items:
            s.phase3()


PACK_W = 1024


SMALL_FULL = [(2, 8, PACK_W), (HEADS, CHUNK, CHUNK), (2, 8, D_FF)]
SMALL_HALF = [(s[0] // 2,) + s[1:] for s in SMALL_FULL]
N_SMALL_SCRATCH = 16


def _small_scratch():
    n_a = len(SMALL_FULL)
    return ([pltpu.VMEM(SMALL_FULL[0], F32)] + [pltpu.VMEM(s, F32) for s in SMALL_HALF] + [pltpu.VMEM(s, F32) for s in SMALL_HALF]
            + [pltpu.VMEM((3,) + s, F32) for s in SMALL_HALF]
            + [pltpu.SemaphoreType.DMA((n_a,)), pltpu.SemaphoreType.DMA((n_a,)), pltpu.SemaphoreType.DMA((n_a, 3)),
               pltpu.SemaphoreType.DMA((n_a, 3)), pltpu.SemaphoreType.DMA((n_a,)), pltpu.SemaphoreType.DMA((n_a,))])


class _SmallReduce:
    def __init__(self, ins, outs, scratch):
        self.ins, self.outs = ins, outs
        (self.pack, *rest) = scratch
        self.rxs, self.css, self.gs = rest[0:3], rest[3:6], rest[6:9]
        self.s1_send, self.s1_recv, self.s2_send, self.s2_recv, self.s3_send, self.s3_recv = rest[9:]
        self.x, self.y, self.c = lax.axis_index("x"), lax.axis_index("y"), lax.axis_index("c")
        self.sibling = (self.x, self.y, 1 - self.c)
        self.chips = [(1 - self.x, self.y), (self.x, 1 - self.y), (1 - self.x, 1 - self.y)]
        self.hl = [s[0] for s in SMALL_HALF]

    def half(self, ref, a, h):
        return ref.at[pl.ds(h * self.hl[a], self.hl[a])]

    def begin(self):
        dg1_ref, dg2_ref, dgf_ref, dgrn_ref, dlng_ref, dlnb_ref, dbs_ref, loss_ref, dws_ref, dcv_ref = self.ins
        pack, c = self.pack, self.c
        pack[...] = jnp.zeros_like(pack)
        pack[0, 0:1, :] = dg1_ref[0:1, :]
        pack[0, 1:2, :] = dg2_ref[0:1, :]
        pack[0, 2:3, :] = dgf_ref[0:1, :]
        pack[0, 3:4, 0:RET_W] = dgrn_ref[0:1, :]
        lsum = loss_ref[0, 0:1, :]
        for i in range(1, N_TB):
            lsum = lsum + loss_ref[i, 0:1, :]
        pack[0, 3:4, RET_W:RET_W + 128] = lsum
        pack[1, 0:HEADS, 0:128] = dlng_ref[0:HEADS, :]
        pack[1, 0:HEADS, 128:256] = dlnb_ref[0:HEADS, :]
        pack[1, 0:HEADS, 256:384] = dbs_ref[0:HEADS, :]
        self.srcs = [pack, dws_ref, dcv_ref]
        n_a = len(self.srcs)
        self.ex1 = [pltpu.make_async_remote_copy(src_ref=self.half(self.srcs[a], a, 1 - c), dst_ref=self.rxs[a],
                                                 send_sem=self.s1_send.at[a], recv_sem=self.s1_recv.at[a],
                                                 device_id=self.sibling, device_id_type=MESH) for a in range(n_a)]
        for cp in self.ex1:
            cp.start()
        self.ex2 = []
        for a in range(n_a):
            self.ex1[a].wait_recv()
            self.css[a][...] = self.half(self.srcs[a], a, c)[...] + self.rxs[a][...]
            for j, chip in enumerate(self.chips):
                cp = pltpu.make_async_remote_copy(src_ref=self.css[a], dst_ref=self.gs[a].at[j], send_sem=self.s2_send.at[a, j],
                                                  recv_sem=self.s2_recv.at[a, j], device_id=(*chip, c), device_id_type=MESH)
                cp.start()
                self.ex2.append(cp)

    def end(self):
        c, x, y = self.c, self.x, self.y
        ex3 = []
        for a in range(len(self.srcs)):
            css, gs, out = self.css[a], self.gs[a], self.outs[a]
            for j in range(3):
                self.ex2[3 * a + j].wait_recv()
            tot = None
            for q in range(4):
                k = jnp.where(x != (q >> 1), 1, 0) + jnp.where(y != (q & 1), 2, 0)
                term = jnp.where(k == 0, css[...], jnp.where(k == 1, gs[0], jnp.where(k == 2, gs[1], gs[2])))
                tot = term if tot is None else tot + term
            self.half(out, a, c)[...] = tot
            cp = pltpu.make_async_remote_copy(src_ref=self.half(out, a, c), dst_ref=self.half(out, a, c), send_sem=self.s3_send.at[a],
                                              recv_sem=self.s3_recv.at[a], device_id=self.sibling, device_id_type=MESH)
            cp.start()
            ex3.append(cp)
        for a in range(len(self.srcs)):
            out = self.outs[a]
            pltpu.make_async_remote_copy(src_ref=self.half(out, a, 1 - c), dst_ref=self.half(out, a, 1 - c), send_sem=self.s3_send.at[a],
                                         recv_sem=self.s3_recv.at[a], device_id=self.sibling, device_id_type=MESH).wait_recv()
        for cp in self.ex1 + self.ex2 + ex3:
            cp.wait_send()


def _adam_math(w, g, m, v):
    nm = ADAM_B1 * m + (1.0 - ADAM_B1) * g
    nv = ADAM_B2 * v + (1.0 - ADAM_B2) * (g * g)
    d = -ADAM_LR * ((nm / (1.0 - ADAM_B1 ** ADAM_STEP)) / (jnp.sqrt(nv / (1.0 - ADAM_B2 ** ADAM_STEP)) + ADAM_EPS) + ADAM_WD * w)
    return d, nm, nv


def _adamw(name, w, gs, m, v, rows):
    _, r, cdim = w.shape
    n_steps = r // rows
    half = gs[0].shape[0] // rows

    def body(w_ref, *rest):
        g_refs, (m_ref, v_ref, go_ref, d_ref, nm_ref, nv_ref) = rest[:len(gs)], rest[len(gs):]
        gg = g_refs[0][...]
        if len(gs) == 2:
            gg = jnp.where(pl.program_id(0) < half, gg, g_refs[1][...])
        go_ref[0] = gg
        d, nm, nv = _adam_math(w_ref[0], gg, m_ref[0], v_ref[0])
        d_ref[0], nm_ref[0], nv_ref[0] = d, nm, nv

    spec3 = pl.BlockSpec((1, rows, cdim), lambda i: (0, i, 0))
    if len(gs) == 1:
        g_specs = [pl.BlockSpec((rows, cdim), lambda i: (i, 0))]
    else:
        g_specs = [pl.BlockSpec((rows, cdim), lambda i: (jnp.minimum(i, half - 1), 0)),
                   pl.BlockSpec((rows, cdim), lambda i: (jnp.maximum(i - half, 0), 0))]
    sh = jax.ShapeDtypeStruct((1, r, cdim), F32)
    return pl.pallas_call(
        body, name=name, grid=(n_steps,), out_shape=(sh, sh, sh, sh),
        in_specs=[spec3] + g_specs + [spec3, spec3], out_specs=(spec3,) * 4,
        compiler_params=_cparams(("parallel",)),
    )(w, *gs, m, v)


def _adamw_small(rp, rws, rcv, gcw, params):
    n_p = len(params)

    def body(*refs):
        rp_ref, rws_ref, rcv_ref, gcw_ref = refs[:4]
        ins = refs[4:4 + 3 * n_p]
        outs = refs[4 + 3 * n_p:]
        outs[4 * n_p][...] = rp_ref[0, 3:4, RET_W:RET_W + 1]
        grads = [rp_ref[0, 0:1, :], rp_ref[0, 1:2, :], rp_ref[0, 2:3, :], rp_ref[0, 3:4, 0:RET_W],
                 rp_ref[1, 0:HEADS, 0:128], rp_ref[1, 0:HEADS, 128:256], rp_ref[1, 0:HEADS, 256:384],
                 rws_ref[...], gcw_ref[...], None]
        for p in range(n_p):
            w_ref, m_ref, v_ref = ins[3 * p:3 * p + 3]
            o = outs[4 * p:4 * p + 4]
            if p == n_p - 1:
                for hf in range(2):
                    cs = slice(hf * D_FF, (hf + 1) * D_FF)
                    g = rcv_ref[hf, 3:4, :]
                    res = (g,) + _adam_math(w_ref[:, cs], g, m_ref[:, cs], v_ref[:, cs])
                    for t in range(4):
                        o[t][:, cs] = res[t]
                continue
            lead = w_ref.ndim > grads[p].ndim
            rd = (lambda r: r[0]) if lead else (lambda r: r[...])
            res = (grads[p],) + _adam_math(rd(w_ref), grads[p], rd(m_ref), rd(v_ref))
            for t in range(4):
                if lead:
                    o[t][0] = res[t]
                else:
                    o[t][...] = res[t]

    vm = pl.BlockSpec(memory_space=pltpu.VMEM)
    flat = [a for tr in params for a in tr]
    out_shape = tuple(jax.ShapeDtypeStruct(tr[0].shape, F32) for tr in params for _ in range(4)) + (jax.ShapeDtypeStruct((1, 1), F32),)
    res = pl.pallas_call(
        body, name="adamw_small", out_shape=out_shape, in_specs=[vm] * (4 + len(flat)), out_specs=(vm,) * len(out_shape),
        compiler_params=_cparams(),
    )(rp, rws, rcv, gcw, *flat)
    return [res[4 * p:4 * p + 4] for p in range(n_p)], res[4 * n_p]


def kernel(x, mix_norm_g, w_in, ret_norm_g, sgu_ln_g, sgu_ln_b, sgu_w_s, sgu_b_s, w_out, ffn_norm_g, w_up, conv_w, conv_b, w_down, final_norm_g, loss_target, m_mix_norm_g, m_w_in, m_ret_norm_g, m_sgu_ln_g, m_sgu_ln_b, m_sgu_w_s, m_sgu_b_s, m_w_out, m_ffn_norm_g, m_w_up, m_conv_w, m_conv_b, m_w_down, m_final_norm_g, v_mix_norm_g, v_w_in, v_ret_norm_g, v_sgu_ln_g, v_sgu_ln_b, v_sgu_w_s, v_sgu_b_s, v_w_out, v_ffn_norm_g, v_w_up, v_conv_w, v_conv_b, v_w_down, v_final_norm_g):
    xs = x[0]
    tgt = loss_target[0]
    mask, qdec, kdec = _decay_tables()
    grn = ret_norm_g.reshape(1, RET_W)
    lng = sgu_ln_g.reshape(1, SGU_W)
    lnb = sgu_ln_b.reshape(1, SGU_W)
    ws = sgu_w_s[0]
    bsb = jnp.broadcast_to(sgu_b_s[0][:, :, None], (HEADS, CHUNK, HEAD_DIM))
    gf = final_norm_g.reshape(1, D_MODEL)
    me = 4 * lax.axis_index("x") + 2 * lax.axis_index("y") + lax.axis_index("c")
    tr = lambda a: jnp.transpose(a[0])[None]
    tr_cw = lambda a: jnp.transpose(a, (1, 0, 2))

    proj, h1, cos2, sin2, win_g, cw_sh, wout_g, wdn_g, su = _fwd_proj(
        xs, mix_norm_g, _rope_freq(), w_in[0], w_out[0], tr(w_up)[0], w_down[0], tr_cw(conv_w))
    cw_g = jnp.transpose(cw_sh, (1, 0, 2)).reshape(8, 2 * D_FF)
    x2, mixcat, o, sprev, wup_g = _fwd_mix(xs, proj, wout_g, grn, lng, lnb, ws, bsb, mask, qdec, kdec, su)
    h2, up_pre, u_conv, act, x3, loss_parts = _fwd_ffn(x2, ffn_norm_g, wup_g, cw_g, conv_b, wdn_g, gf, tgt)

    dx3, dpre, dx2, dgf, dg2, dcv = _bwd_ffn(x3, tgt, gf, x2, ffn_norm_g, up_pre, u_conv, wup_g, cw_g, wdn_g)
    band = 512
    (gdn_p,) = _wgrad("wgrad_down", act, dx3, tm=FF_TILE)
    (gout_p,) = _wgrad("wgrad_out", mixcat, dx2, tn=512)
    gup_p, g_dn = _wgrad("wgrad_up", dpre, h2, tm=FF_TILE, tn=512, hosted=[(W_DOWN, gdn_p)])
    dproj, dgrn, dlng, dlnb, dws, dbs, g_up_a, g_out = _bwd_mix(
        dx2, proj, o, sprev, wout_g, grn, lng, lnb, ws, bsb, mask, qdec, kdec, cos2, sin2,
        [((W_UP, 0, band), gup_p), (W_OUT, gout_p)])
    gin_p, g_up_b = _wgrad("wgrad_in", h1, dproj, tm=512, tn=768, hosted=[((W_UP, band, FF_SHARD - band), gup_p)])
    grad_x, g_in, rp, rws, rcv = _bwd_proj(dproj, win_g, xs, mix_norm_g, dx2, gin_p,
                                           (dg2, dgf, dgrn, dlng, dlnb, dbs, loss_parts, dws, dcv))
    gcw = tr_cw(lax.dynamic_slice(rcv, (me // (N_DEV // 2), 0, (me % (N_DEV // 2)) * FF_SHARD), (1, 3, FF_SHARD)))

    table = {}
    for name, w, gs, m, v, rows in (("w_in", w_in, [g_in], m_w_in, v_w_in, 256), ("w_out", w_out, [g_out], m_w_out, v_w_out, 128),
                                    ("w_up", tr(w_up), [g_up_a, g_up_b], tr(m_w_up), tr(v_w_up), 64),
                                    ("w_down", w_down, [g_dn], m_w_down, v_w_down, 88)):
        table[name] = _adamw("adamw_" + name, w, gs, m, v, rows)
    table["w_up"] = tuple(tr(a) for a in table["w_up"])
    row = lambda a: a.reshape(1, D_MODEL)
    names_small = ["mix_norm_g", "ffn_norm_g", "final_norm_g", "ret_norm_g", "sgu_ln_g", "sgu_ln_b", "sgu_b_s", "sgu_w_s",
                   "conv_w", "conv_b"]
    params = [(mix_norm_g, m_mix_norm_g, v_mix_norm_g), (ffn_norm_g, m_ffn_norm_g, v_ffn_norm_g),
              (row(final_norm_g), row(m_final_norm_g), row(v_final_norm_g)), (ret_norm_g, m_ret_norm_g, v_ret_norm_g),
              (sgu_ln_g, m_sgu_ln_g, v_sgu_ln_g), (sgu_ln_b, m_sgu_ln_b, v_sgu_ln_b), (sgu_b_s, m_sgu_b_s, v_sgu_b_s),
              (sgu_w_s, m_sgu_w_s, v_sgu_w_s), (tr_cw(conv_w), tr_cw(m_conv_w), tr_cw(v_conv_w)), (conv_b, m_conv_b, v_conv_b)]
    small, loss = _adamw_small(rp, rws, rcv, gcw, params)
    for n, res in zip(names_small, small):
        table[n] = res
    table["final_norm_g"] = tuple(a.reshape(D_MODEL) for a in table["final_norm_g"])
    table["conv_w"] = tuple(tr_cw(a) for a in table["conv_w"])

    order = ["mix_norm_g", "w_in", "ret_norm_g", "sgu_ln_g", "sgu_ln_b", "sgu_w_s", "sgu_b_s", "w_out", "ffn_norm_g", "w_up",
             "conv_w", "conv_b", "w_down", "final_norm_g"]
    outs = [loss.reshape(()), grad_x[None]]
    for col in range(4):
        outs += [table[n][col] for n in order]
    return tuple(outs)
```

```python
import functools
import math

import jax
import jax.numpy as jnp
import numpy as np
from jax import lax
from jax.experimental import pallas as pl
from jax.experimental.pallas import tpu as pltpu

F32 = jnp.float32
BF16 = jnp.bfloat16
MESH = pl.DeviceIdType.MESH

N_DEV = 8
SEQ = 2048
D_MODEL = 1024
CHUNK = 128
N_CHUNK = SEQ // CHUNK
HEADS = 4
HEAD_DIM = 128
RET_W = 512
SGU_W = 512
PROJ_W = 3072
D_FF = 2816
FF_SHARD = 704
FF_TILE = 1408
FF_TILES = ((0, 1536), (1536, 1280))
IN_SHARD = PROJ_W // N_DEV
OUT_SHARD = D_MODEL // N_DEV
DOWN_SHARD = D_FF // N_DEV
TM = 256
N_TB = SEQ // TM
FWD_PROJ_PASS_AT = 5
FWD_MIX_PASS_AT = 10
EPS = 1e-6
ROPE_BASE = 10000.0
K_SCALE = HEAD_DIM ** -0.5
INV_SQRT2 = 0.7071067811865476
INV_SQRT_2PI = 0.3989422804014327

ADAM_LR = 0.001
ADAM_B1 = 0.9
ADAM_B2 = 0.999
ADAM_EPS = 1e-08
ADAM_WD = 0.01
ADAM_STEP = 10

VMEM_LIMIT = 56 * 1024 * 1024


def _cparams(sem=None, vmem=VMEM_LIMIT, collective=None):
    return pltpu.CompilerParams(dimension_semantics=sem, vmem_limit_bytes=vmem, collective_id=collective)


COLLECTIVE = {name: k for k, name in enumerate(("fwd_proj", "fwd_mix", "wgrad_up", "bwd_mix", "wgrad_in", "bwd_proj"))}


class _Meet:
    def __init__(self, diagonal):
        x, y, c = lax.axis_index("x"), lax.axis_index("y"), lax.axis_index("c")
        self.peers = [(x, y, 1 - c), (1 - x, y, c), (x, 1 - y, c)] + ([(1 - x, 1 - y, c)] if diagonal else [])

    def signal(self):
        for peer in self.peers:
            pl.semaphore_signal(pltpu.get_barrier_semaphore(), inc=1, device_id=peer, device_id_type=MESH)

    def wait(self):
        pl.semaphore_wait(pltpu.get_barrier_semaphore(), len(self.peers))


def _resident(shape):
    nd = len(shape)
    return pl.BlockSpec(shape, lambda *_: (0,) * nd, pipeline_mode=pl.Buffered(1))


def _dot(a, b):
    return jnp.dot(a, b, preferred_element_type=F32)


def _dot_nt(a, b):
    return lax.dot_general(a, b, (((1,), (1,)), ((), ())), preferred_element_type=F32)


def _dot_tn(a, b):
    return lax.dot_general(a, b, (((0,), (0,)), ((), ())), preferred_element_type=F32)


def _sigmoid(x):
    return 1.0 / (1.0 + jnp.exp(-x))


def _gelu(x):
    return 0.5 * x * (1.0 + lax.erf(x * INV_SQRT2))


def _gelu_grad(x):
    return 0.5 * (1.0 + lax.erf(x * INV_SQRT2)) + x * (jnp.exp(-0.5 * x * x) * INV_SQRT_2PI)


def _rot(xh, cos2, sin2):
    return xh * cos2 + pltpu.roll(xh, HEAD_DIM // 2, 1) * sin2


def _rot_t(dh, cos2, sin2):
    return dh * cos2 + pltpu.roll(dh * sin2, HEAD_DIM // 2, 1)


def _rope_freq():
    half = HEAD_DIM // 2
    inv_freq = jnp.power(ROPE_BASE, -jnp.arange(half, dtype=F32) / half)
    return jnp.concatenate([inv_freq, inv_freq])[None, :]


def _rope_block(inv2, first_row):
    pos = (lax.broadcasted_iota(jnp.int32, (TM, HEAD_DIM), 0) + first_row).astype(F32)
    ang = pos * inv2
    sin = jnp.sin(ang)
    lane = lax.broadcasted_iota(jnp.int32, (TM, HEAD_DIM), 1)
    return jnp.cos(ang), jnp.where(lane < HEAD_DIM // 2, -sin, sin)


def _decay_tables():
    log_gamma = jnp.log(1.0 - jnp.power(2.0, -5.0 - jnp.arange(HEADS, dtype=F32)))
    pos = jnp.arange(CHUNK, dtype=F32)
    diff = pos[:, None] - pos[None, :]
    mask = jnp.where(diff >= 0.0, jnp.exp(log_gamma[:, None, None] * jnp.maximum(diff, 0.0)[None]), 0.0)
    k_decay = jnp.exp(log_gamma[:, None] * (CHUNK - 1.0 - pos)[None])
    q_decay = jnp.exp(log_gamma[:, None] * (pos + 1.0)[None])
    kd = jnp.broadcast_to(k_decay[:, :, None], (HEADS, CHUNK, HEAD_DIM))
    qd = jnp.broadcast_to(q_decay[:, :, None], (HEADS, CHUNK, HEAD_DIM))
    return mask.astype(F32), qd.astype(F32), kd.astype(F32)


def _chunk_decay():
    lg = np.log(np.float32(1.0) - np.power(np.float32(2.0), -5.0 - np.arange(HEADS, dtype=np.float32))).astype(np.float32)
    return [float(np.exp(lg[h] * np.float32(CHUNK))) for h in range(HEADS)]


W_IN, W_OUT, W_UP, W_DOWN, W_CONV = range(5)
GATHERED = {W_IN: ((D_MODEL, PROJ_W), BF16), W_OUT: ((D_MODEL, D_MODEL), BF16), W_UP: ((2 * D_FF, D_MODEL), BF16),
            W_DOWN: ((D_FF, D_MODEL), BF16), W_CONV: ((N_DEV, 8, FF_SHARD), F32)}
SHARD = {W_IN: (D_MODEL, IN_SHARD), W_OUT: (OUT_SHARD, D_MODEL), W_UP: (FF_SHARD, D_MODEL), W_DOWN: (DOWN_SHARD, D_MODEL),
         W_CONV: (8, FF_SHARD)}


class _Gather:
    N_SEMS = 9

    def __init__(self, ids, stages, gathered, send_sems, recv_sems, local_sems):
        self.ids, self.stages, self.gathered = ids, stages, gathered
        self.send_sems, self.recv_sems, self.local_sems = send_sems, recv_sems, local_sems
        self.x, self.y, self.c = lax.axis_index("x"), lax.axis_index("y"), lax.axis_index("c")
        self.me = (self.x, self.y, self.c)
        self.sibling = (self.x, self.y, 1 - self.c)
        self.chips = [(1 - self.x, self.y), (self.x, 1 - self.y), (1 - self.x, 1 - self.y)]

    def slot(self, n, px, py, pc):
        dev = 4 * px + 2 * py + pc
        w, g = self.ids[n], self.gathered[n]
        if w == W_IN:
            return g.at[:, pl.ds(pl.multiple_of(dev * IN_SHARD, 128), IN_SHARD)]
        if w == W_OUT:
            return g.at[pl.ds(pl.multiple_of(dev * OUT_SHARD, 128), OUT_SHARD), :]
        if w == W_DOWN:
            return g.at[pl.ds(pl.multiple_of(dev * DOWN_SHARD, 32), DOWN_SHARD), :]
        if w == W_UP:
            return g.at[pl.ds(pl.multiple_of(dev * FF_SHARD, 32), FF_SHARD), :]
        return g.at[dev]

    def half(self, n, px, py, pc, h):
        dev = 4 * px + 2 * py + pc
        w, g = self.ids[n], self.gathered[n]
        if w == W_IN:
            return g.at[pl.ds(h * (D_MODEL // 2), D_MODEL // 2), pl.ds(pl.multiple_of(dev * IN_SHARD, 128), IN_SHARD)]
        rows = SHARD[w][0] // 2
        return g.at[pl.ds(pl.multiple_of(dev * SHARD[w][0] + h * rows, 16), rows), :]

    def tree(self, n):
        return self.ids[n] != W_CONV

    def copy(self, n, k, block, to, src=None, h=None):
        ref = self.slot(n, *block) if h is None else self.half(n, *block, h)
        return pltpu.make_async_remote_copy(
            src_ref=ref if src is None else src, dst_ref=ref,
            send_sem=self.send_sems.at[n, k], recv_sem=self.recv_sems.at[n, k], device_id=to, device_id_type=MESH)

    def _mine(self):
        return [pltpu.make_async_copy(self.stages[n], self.slot(n, *self.me), self.local_sems.at[n]) for n in range(len(self.ids))]

    def _first(self):
        out = []
        for n in range(len(self.ids)):
            out.append(self.copy(n, 0, self.me, self.sibling, src=self.stages[n]))
            out += [self.copy(n, 1 + j, self.me, (*chip, self.c), src=self.stages[n])
                    for j, chip in enumerate(self.chips[:2] if self.tree(n) else self.chips)]
        return out

    def start(self):
        for cp in self._mine() + self._first():
            cp.start()

    def _passed(self, j):
        dev = (*self.chips[j], self.c)
        out = []
        for n in range(len(self.ids)):
            if not self.tree(n):
                out.append(self.copy(n, 4 + j, dev, self.sibling))
            elif j < 2:
                out += [self.copy(n, 3 + j, dev, (*self.chips[1 - j], self.c), h=j), self.copy(n, 5 + j, dev, self.sibling)]
            else:
                out += [self.copy(n, 7, dev, self.sibling, h=0), self.copy(n, 8, dev, self.sibling, h=1)]
        return out

    def near(self):
        for j in range(2):
            dev = (*self.chips[j], self.c)
            for n in range(len(self.ids)):
                self.copy(n, 1 + j, dev, self.me).wait_recv()
            for cp in self._passed(j):
                cp.start()

    def finish(self):
        dev = (*self.chips[2], self.c)
        for n in range(len(self.ids)):
            if self.tree(n):
                self.copy(n, 3, dev, self.me, h=0).wait_recv()
                self.copy(n, 4, dev, self.me, h=1).wait_recv()
            else:
                self.copy(n, 3, dev, self.me).wait_recv()
        for cp in self._passed(2):
            cp.start()
        for n in range(len(self.ids)):
            self.copy(n, 0, self.sibling, self.me).wait_recv()
            for j, chip in enumerate(self.chips):
                dev = (*chip, 1 - self.c)
                if not self.tree(n):
                    self.copy(n, 4 + j, dev, self.me).wait_recv()
                elif j < 2:
                    self.copy(n, 5 + j, dev, self.me).wait_recv()
                else:
                    self.copy(n, 7, dev, self.me, h=0).wait_recv()
                    self.copy(n, 8, dev, self.me, h=1).wait_recv()
        for cp in self._mine():
            cp.wait()
        for cp in self._first() + self._passed(0) + self._passed(1) + self._passed(2):
            cp.wait_send()


def _gather_scratch(n):
    return [pltpu.SemaphoreType.DMA((n, _Gather.N_SEMS)), pltpu.SemaphoreType.DMA((n, _Gather.N_SEMS)), pltpu.SemaphoreType.DMA((n,))]


def _gathered_shapes(ids):
    return tuple(jax.ShapeDtypeStruct(*GATHERED[w]) for w in ids)


def _fwd_proj(x, g1, inv2, w_in, w_out, w_up, w_down, conv_w):
    ids_a, ids_b = [W_IN, W_CONV], [W_OUT, W_DOWN]

    def body(x_ref, g_ref, inv_ref, in_hbm, out_hbm, up_hbm, dn_hbm, cw_ref,
             proj_ref, h1_ref, cos_ref, sin_ref, gin, gcw, gout, gdn, su_ref,
             w_vm, s_in, s_cw, s_out, s_dn, f_in, f_out, f_up, f_dn, ld_sems,
             a_send, a_recv, a_local, b_send, b_recv, b_local):
        ag_a = _Gather(ids_a, [s_in, s_cw], [gin, gcw], a_send, a_recv, a_local)
        ag_b = _Gather(ids_b, [s_out, s_dn], [gout, gdn], b_send, b_recv, b_local)

        @pl.when(pl.program_id(0) == 0)
        def _():
            meet = _Meet(diagonal=True)
            meet.signal()
            loads = [pltpu.make_async_copy(src, dst, ld_sems.at[i])
                     for i, (src, dst) in enumerate(((in_hbm, f_in), (out_hbm, f_out), (dn_hbm, f_dn), (up_hbm, f_up)))]
            for cp in loads:
                cp.start()
            s_cw[...] = jnp.zeros_like(s_cw)
            for k in range(3):
                s_cw[k:k + 1, :] = cw_ref[k]
            loads[0].wait()
            s_in[...] = f_in[...].astype(BF16)
            meet.wait()
            ag_a.start()
            loads[1].wait()
            s_out[...] = f_out[...].astype(BF16)
            loads[2].wait()
            s_dn[...] = f_dn[...].astype(BF16)
            ag_a.near()
            ag_b.start()
            loads[3].wait()
            su_ref[...] = f_up[...].astype(BF16)
            ag_a.finish()
            fill = pltpu.make_async_copy(gin, w_vm, ld_sems.at[4])
            fill.start()
            fill.wait()

        pl.when(pl.program_id(0) == FWD_PROJ_PASS_AT)(ag_b.near)

        xb = x_ref[...]
        r = lax.rsqrt(jnp.mean(xb * xb, axis=-1, keepdims=True) + EPS)
        h = ((xb * r) * g_ref[...]).astype(BF16)
        h1_ref[...] = h
        p = _dot(h, w_vm[...])
        c2, s2 = _rope_block(inv_ref[...], pl.program_id(0) * TM)
        cos_ref[...], sin_ref[...] = c2, s2
        for hd in range(HEADS):
            sl = slice(hd * HEAD_DIM, (hd + 1) * HEAD_DIM)
            proj_ref[:, sl] = _rot(p[:, sl], c2, s2)
            ks = slice(RET_W + hd * HEAD_DIM, RET_W + (hd + 1) * HEAD_DIM)
            proj_ref[:, ks] = _rot(p[:, ks], c2, s2) * K_SCALE
        proj_ref[:, 2 * RET_W:] = p[:, 2 * RET_W:]

        pl.when(pl.program_id(0) == N_TB - 1)(ag_b.finish)

    tok = lambda w: pl.BlockSpec((TM, w), lambda i: (i, 0))
    hbm = pl.BlockSpec(memory_space=pl.ANY)
    vm = pl.BlockSpec(memory_space=pltpu.VMEM)
    return pl.pallas_call(
        body, name="fwd_proj", grid=(N_TB,),
        out_shape=(jax.ShapeDtypeStruct((SEQ, PROJ_W), F32), jax.ShapeDtypeStruct((SEQ, D_MODEL), BF16),
                   jax.ShapeDtypeStruct((SEQ, HEAD_DIM), F32), jax.ShapeDtypeStruct((SEQ, HEAD_DIM), F32))
        + _gathered_shapes(ids_a + ids_b) + (jax.ShapeDtypeStruct(SHARD[W_UP], BF16),),
        in_specs=[tok(D_MODEL), _resident((1, D_MODEL)), _resident((1, HEAD_DIM)), hbm, hbm, hbm, hbm, vm],
        out_specs=(tok(PROJ_W), tok(D_MODEL), tok(HEAD_DIM), tok(HEAD_DIM), hbm, hbm, hbm, hbm, vm),
        scratch_shapes=[pltpu.VMEM((D_MODEL, PROJ_W), BF16), pltpu.VMEM(SHARD[W_IN], BF16), pltpu.VMEM(SHARD[W_CONV], F32),
                        pltpu.VMEM(SHARD[W_OUT], BF16), pltpu.VMEM(SHARD[W_DOWN], BF16),
                        pltpu.VMEM(SHARD[W_IN], F32), pltpu.VMEM(SHARD[W_OUT], F32), pltpu.VMEM(SHARD[W_UP], F32),
                        pltpu.VMEM(SHARD[W_DOWN], F32), pltpu.SemaphoreType.DMA((5,))]
        + _gather_scratch(len(ids_a)) + _gather_scratch(len(ids_b)),
        compiler_params=_cparams(("arbitrary",), collective=COLLECTIVE["fwd_proj"]),
    )(x, g1, inv2, w_in, w_out, w_up, w_down, conv_w)


def _causal(w):
    r = lax.broadcasted_iota(jnp.int32, (CHUNK, CHUNK), 0)
    c = lax.broadcasted_iota(jnp.int32, (CHUNK, CHUNK), 1)
    return jnp.where(r >= c, w, 0.0)


def _fwd_mix(x, proj, wout_g, grn, lng, lnb, ws, bsb, mask, qdec, kdec, su):
    cdec = _chunk_decay()
    ids = [W_UP]

    def body(x_ref, p_ref, w_ref, grn_ref, lng_ref, lnb_ref, ws_ref, bsb_ref, m_ref, qd_ref, kd_ref, su_ref,
             x2_ref, cat_ref, o_ref, sp_ref, gup, state, send_sems, recv_sems, local_sems):
        ag = _Gather(ids, [su_ref], [gup], send_sems, recv_sems, local_sems)

        @pl.when(pl.program_id(0) == 0)
        def _():
            meet = _Meet(diagonal=False)
            meet.signal()
            state[...] = jnp.zeros_like(state)
            meet.wait()
            ag.start()

        for h in range(HEADS):
            sl = slice(h * HEAD_DIM, (h + 1) * HEAD_DIM)
            q = p_ref[:, sl]
            k = p_ref[:, RET_W + h * HEAD_DIM:RET_W + (h + 1) * HEAD_DIM]
            v = p_ref[:, 2 * RET_W + h * HEAD_DIM:2 * RET_W + (h + 1) * HEAD_DIM]
            g = p_ref[:, 3 * RET_W + h * HEAD_DIM:3 * RET_W + (h + 1) * HEAD_DIM]
            qb, kb, vb = q.astype(BF16), k.astype(BF16), v.astype(BF16)
            a = _dot_nt(qb, kb) * m_ref[h]
            spb = state[h].astype(BF16)
            sp_ref[0, h] = spb
            o = _dot(a.astype(BF16), vb) + _dot((q * qd_ref[h]).astype(BF16), spb)
            state[h] = state[h] * cdec[h] + _dot_tn((k * kd_ref[h]).astype(BF16), vb)
            o_ref[:, sl] = o
            rinv = lax.rsqrt(jnp.mean(o * o, axis=-1, keepdims=True) + EPS)
            rn = (o * rinv) * grn_ref[:, sl]
            cat_ref[:, sl] = ((g * _sigmoid(g)) * rn).astype(BF16)
        for gi in range(HEADS):
            sl = slice(gi * HEAD_DIM, (gi + 1) * HEAD_DIM)
            u = p_ref[:, 4 * RET_W + gi * HEAD_DIM:4 * RET_W + (gi + 1) * HEAD_DIM]
            sv = p_ref[:, 4 * RET_W + SGU_W + gi * HEAD_DIM:4 * RET_W + SGU_W + (gi + 1) * HEAD_DIM]
            gv = _gelu(sv)
            xc = gv - jnp.mean(gv, axis=-1, keepdims=True)
            vn = (xc * lax.rsqrt(jnp.mean(xc * xc, axis=-1, keepdims=True) + EPS)) * lng_ref[:, sl] + lnb_ref[:, sl]
            mixed = _dot(_causal(ws_ref[gi]).astype(BF16), vn.astype(BF16)) + bsb_ref[gi]
            cat_ref[:, RET_W + gi * HEAD_DIM:RET_W + (gi + 1) * HEAD_DIM] = (_gelu(u) * mixed).astype(BF16)
        x2_ref[...] = x_ref[...] + _dot(cat_ref[...], w_ref[...])

        pl.when(pl.program_id(0) == FWD_MIX_PASS_AT)(ag.near)
        pl.when(pl.program_id(0) == N_CHUNK - 1)(ag.finish)

    ch = lambda w: pl.BlockSpec((CHUNK, w), lambda i: (i, 0))
    hcc = (HEADS, CHUNK, CHUNK)
    hbm = pl.BlockSpec(memory_space=pl.ANY)
    return pl.pallas_call(
        body, name="fwd_mix", grid=(N_CHUNK,),
        out_shape=(jax.ShapeDtypeStruct((SEQ, D_MODEL), F32), jax.ShapeDtypeStruct((SEQ, D_MODEL), BF16),
                   jax.ShapeDtypeStruct((SEQ, RET_W), F32), jax.ShapeDtypeStruct((N_CHUNK, HEADS, HEAD_DIM, HEAD_DIM), BF16))
        + _gathered_shapes(ids),
        in_specs=[ch(D_MODEL), ch(PROJ_W), _resident((D_MODEL, D_MODEL)), _resident((1, RET_W)), _resident((1, SGU_W)),
                  _resident((1, SGU_W)), _resident(hcc), _resident(hcc), _resident(hcc), _resident(hcc), _resident(hcc), hbm],
        out_specs=(ch(D_MODEL), ch(D_MODEL), ch(RET_W), pl.BlockSpec((1, HEADS, HEAD_DIM, HEAD_DIM), lambda i: (i, 0, 0, 0)), hbm),
        scratch_shapes=[pltpu.VMEM((HEADS, HEAD_DIM, HEAD_DIM), F32)] + _gather_scratch(len(ids)),
        compiler_params=_cparams(("arbitrary",), collective=COLLECTIVE["fwd_mix"]),
    )(x, proj, wout_g, grn, lng, lnb, ws, bsb, mask, qdec, kdec, su)


def _conv_taps(p, prev8):
    row = lax.broadcasted_iota(jnp.int32, p.shape, 0)
    p1 = jnp.where(row == 0, prev8[7:8, :], pltpu.roll(p, 1, 0))
    p2 = jnp.where(row == 0, prev8[6:7, :], jnp.where(row == 1, prev8[7:8, :], pltpu.roll(p, 2, 0)))
    return p1, p2


def _fwd_ffn(x2, g2, wup_g, cw_g, cb_g, wdn_g, gf, tgt):
    def body(x_ref, g_ref, wu_ref, cw_ref, cb_ref, wd_ref, gf_ref, t_ref, h2_ref, up_ref, u_ref, act_ref, x3_ref, loss_ref, carry):
        @pl.when(pl.program_id(0) == 0)
        def _():
            carry[...] = jnp.zeros_like(carry)

        xb = x_ref[...]
        r = lax.rsqrt(jnp.mean(xb * xb, axis=-1, keepdims=True) + EPS)
        h = ((xb * r) * g_ref[...]).astype(BF16)
        h2_ref[...] = h
        acc = xb
        for t0, tw in FF_TILES:
            u = []
            for c0 in (t0, D_FF + t0):
                cs = slice(c0, c0 + tw)
                p = _dot_nt(h, wu_ref[pl.ds(c0, tw), :])
                up_ref[:, cs] = p.astype(BF16)
                p1, p2 = _conv_taps(p, carry[:, cs])
                carry[:, cs] = p[TM - 8:, :]
                us = p2 * cw_ref[0:1, cs] + p1 * cw_ref[1:2, cs] + p * cw_ref[2:3, cs] + cb_ref[:, cs]
                u_ref[:, cs] = us.astype(BF16)
                u.append(us)
            a = ((u[0] * _sigmoid(u[0])) * u[1]).astype(BF16)
            act_ref[:, t0:t0 + tw] = a
            acc = acc + _dot(a, wd_ref[pl.ds(t0, tw), :])
        x3_ref[...] = acc
        r3 = lax.rsqrt(jnp.mean(acc * acc, axis=-1, keepdims=True) + EPS)
        diff = (acc * r3) * gf_ref[...] - t_ref[...]
        loss_ref[...] = jnp.full(loss_ref.shape, 0.5 * jnp.sum(jnp.mean(diff * diff, axis=-1)), F32)

    tok = lambda w: pl.BlockSpec((TM, w), lambda i: (i, 0))
    return pl.pallas_call(
        body, name="fwd_ffn", grid=(N_TB,),
        out_shape=(jax.ShapeDtypeStruct((SEQ, D_MODEL), BF16), jax.ShapeDtypeStruct((SEQ, 2 * D_FF), BF16),
                   jax.ShapeDtypeStruct((SEQ, 2 * D_FF), BF16),
                   jax.ShapeDtypeStruct((SEQ, D_FF), BF16), jax.ShapeDtypeStruct((SEQ, D_MODEL), F32),
                   jax.ShapeDtypeStruct((N_TB, 8, 128), F32)),
        in_specs=[tok(D_MODEL), _resident((1, D_MODEL)), _resident((2 * D_FF, D_MODEL)), _resident((8, 2 * D_FF)),
                  _resident((1, 2 * D_FF)), _resident((D_FF, D_MODEL)), _resident((1, D_MODEL)), tok(D_MODEL)],
        out_specs=(tok(D_MODEL), tok(2 * D_FF), tok(2 * D_FF), tok(D_FF), tok(D_MODEL),
                   pl.BlockSpec((1, 8, 128), lambda i: (i, 0, 0))),
        scratch_shapes=[pltpu.VMEM((8, 2 * D_FF), F32)],
        compiler_params=_cparams(("arbitrary",)),
    )(x2, g2, wup_g, cw_g, cb_g, wdn_g, gf, tgt)


def _bwd_ffn(x3, tgt, gf, x2, g2, up_pre, u_conv, wup_g, cw_g, wdn_g):
    def body(x3_ref, t_ref, gf_ref, x2_ref, g2_ref, up_ref, u_ref, wu_ref, cw_ref, wd_ref,
             dx3_ref, dpre_ref, dx2_ref, dgf_ref, dg2_ref, dcv_ref, nxt):
        i = pl.program_id(0)

        @pl.when(i == 0)
        def _():
            nxt[...] = jnp.zeros_like(nxt)
            dgf_ref[...] = jnp.zeros_like(dgf_ref)
            dg2_ref[...] = jnp.zeros_like(dg2_ref)
            dcv_ref[...] = jnp.zeros_like(dcv_ref)

        x3 = x3_ref[...]
        r3 = lax.rsqrt(jnp.mean(x3 * x3, axis=-1, keepdims=True) + EPS)
        xh3 = x3 * r3
        dy = (xh3 * gf_ref[...] - t_ref[...]) * (1.0 / D_MODEL)
        dgf_ref[0:1, :] += jnp.sum(dy * xh3, axis=0, keepdims=True)
        t3 = dy * gf_ref[...]
        dx3 = r3 * (t3 - xh3 * jnp.mean(t3 * xh3, axis=-1, keepdims=True))
        dx3b = dx3.astype(BF16)
        dx3_ref[...] = dx3b
        dh2 = jnp.zeros((TM, D_MODEL), F32)
        for t0, tw in FF_TILES:
            row = lax.broadcasted_iota(jnp.int32, (TM, tw), 0)
            ts = slice(t0, t0 + tw)
            dact = _dot_nt(dx3b, wd_ref[pl.ds(t0, tw), :])
            ua = u_ref[:, ts].astype(F32)
            ub = u_ref[:, D_FF + t0:D_FF + t0 + tw].astype(F32)
            sg = _sigmoid(ua)
            du = [dact * ub * (sg * (1.0 + ua * (1.0 - sg))), dact * (ua * sg)]
            for n in range(2):
                d = du[n]
                c0 = n * D_FF + t0
                cs = slice(c0, c0 + tw)
                nx = nxt[:, cs]
                n1 = jnp.where(row == TM - 1, nx[0:1, :], pltpu.roll(d, TM - 1, 0))
                n2 = jnp.where(row == TM - 2, nx[0:1, :], jnp.where(row == TM - 1, nx[1:2, :], pltpu.roll(d, TM - 2, 0)))
                nxt[:, cs] = d[0:8, :]
                dp = (d * cw_ref[2:3, cs] + n1 * cw_ref[1:2, cs] + n2 * cw_ref[0:1, cs]).astype(BF16)
                dpre_ref[:, cs] = dp
                p = up_ref[:, cs].astype(F32)
                dcv_ref[n, 0:1, ts] += jnp.sum(n2 * p, axis=0, keepdims=True)
                dcv_ref[n, 1:2, ts] += jnp.sum(n1 * p, axis=0, keepdims=True)
                dcv_ref[n, 2:3, ts] += jnp.sum(d * p, axis=0, keepdims=True)
                dcv_ref[n, 3:4, ts] += jnp.sum(d, axis=0, keepdims=True)
                dh2 = dh2 + _dot(dp, wu_ref[pl.ds(c0, tw), :])
        x2 = x2_ref[...]
        r2 = lax.rsqrt(jnp.mean(x2 * x2, axis=-1, keepdims=True) + EPS)
        xh2 = x2 * r2
        dg2_ref[0:1, :] += jnp.sum(dh2 * xh2, axis=0, keepdims=True)
        t2 = dh2 * g2_ref[...]
        dx2_ref[...] = dx3 + r2 * (t2 - xh2 * jnp.mean(t2 * xh2, axis=-1, keepdims=True))

    rev = lambda w: pl.BlockSpec((TM, w), lambda i: (N_TB - 1 - i, 0))
    acc = lambda s: pl.BlockSpec(s, lambda i: (0,) * len(s))
    return pl.pallas_call(
        body, name="bwd_ffn", grid=(N_TB,),
        out_shape=(jax.ShapeDtypeStruct((SEQ, D_MODEL), BF16), jax.ShapeDtypeStruct((SEQ, 2 * D_FF), BF16),
                   jax.ShapeDtypeStruct((SEQ, D_MODEL), F32), jax.ShapeDtypeStruct((8, D_MODEL), F32),
                   jax.ShapeDtypeStruct((8, D_MODEL), F32), jax.ShapeDtypeStruct((2, 8, D_FF), F32)),
        in_specs=[rev(D_MODEL), rev(D_MODEL), _resident((1, D_MODEL)), rev(D_MODEL), _resident((1, D_MODEL)), rev(2 * D_FF),
                  rev(2 * D_FF), _resident((2 * D_FF, D_MODEL)), _resident((8, 2 * D_FF)), _resident((D_FF, D_MODEL))],
        out_specs=(rev(D_MODEL), rev(2 * D_FF), rev(D_MODEL), acc((8, D_MODEL)), acc((8, D_MODEL)), acc((2, 8, D_FF))),
        scratch_shapes=[pltpu.VMEM((8, 2 * D_FF), F32)],
        compiler_params=_cparams(("arbitrary",)),
    )(x3, tgt, gf, x2, g2, up_pre, u_conv, wup_g, cw_g, wdn_g)


def _bwd_mix(dx2, proj, o, sprev, wout_g, grn, lng, lnb, ws, bsb, mask, qdec, kdec, cos2, sin2, hosted):
    cdec = _chunk_decay()
    geoms = [g for g, _ in hosted]
    n_h = len(hosted)

    def body(dx2_ref, p_ref, o_ref, sp_ref, w_ref, grn_ref, lng_ref, lnb_ref, ws_ref, bsb_ref, m_ref, qd_ref, kd_ref,
             cos_ref, sin_ref, *rest):
        dp_ref, dgrn_ref, dlng_ref, dlnb_ref, dws_ref, dbs_ref = rest[n_h:n_h + 6]
        dstate, dbs_acc = rest[2 * n_h + 6:2 * n_h + 8]
        i = pl.program_id(0)
        rs = _Scatters(geoms, rest[:n_h], rest[n_h + 6:2 * n_h + 6], rest[2 * n_h + 8:])
        pl.when(i == 0)(rs.phase1)
        pl.when(i == 3)(rs.phase2)
        pl.when(i == 8)(rs.phase2b)

        @pl.when(i == 0)
        def _():
            dstate[...] = jnp.zeros_like(dstate)
            dgrn_ref[...] = jnp.zeros_like(dgrn_ref)
            dlng_ref[...] = jnp.zeros_like(dlng_ref)
            dlnb_ref[...] = jnp.zeros_like(dlnb_ref)
            dws_ref[...] = jnp.zeros_like(dws_ref)
            dbs_ref[...] = jnp.zeros_like(dbs_ref)
            dbs_acc[...] = jnp.zeros_like(dbs_acc)

        dmix = _dot_nt(dx2_ref[...].astype(BF16), w_ref[...])
        for h in range(HEADS):
            sl = slice(h * HEAD_DIM, (h + 1) * HEAD_DIM)
            q = p_ref[:, sl]
            k = p_ref[:, RET_W + h * HEAD_DIM:RET_W + (h + 1) * HEAD_DIM]
            v = p_ref[:, 2 * RET_W + h * HEAD_DIM:2 * RET_W + (h + 1) * HEAD_DIM]
            g = p_ref[:, 3 * RET_W + h * HEAD_DIM:3 * RET_W + (h + 1) * HEAD_DIM]
            o = o_ref[:, sl]
            rinv = lax.rsqrt(jnp.mean(o * o, axis=-1, keepdims=True) + EPS)
            oh = o * rinv
            gr = grn_ref[:, sl]
            sg = _sigmoid(g)
            dret = dmix[:, sl]
            dp_ref[:, 3 * RET_W + h * HEAD_DIM:3 * RET_W + (h + 1) * HEAD_DIM] = (
                dret * (oh * gr) * (sg * (1.0 + g * (1.0 - sg)))).astype(BF16)
            drn = dret * (g * sg)
            dgrn_ref[0:1, sl] += jnp.sum(drn * oh, axis=0, keepdims=True)
            t = drn * gr
            do = rinv * (t - oh * jnp.mean(t * oh, axis=-1, keepdims=True))
            qb, kb, vb, dob = q.astype(BF16), k.astype(BF16), v.astype(BF16), do.astype(BF16)
            m = m_ref[h]
            ab = (_dot_nt(qb, kb) * m).astype(BF16)
            dab = (_dot_nt(dob, vb) * m).astype(BF16)
            spb = sp_ref[0, h]
            dsn = dstate[h]
            dsnb = dsn.astype(BF16)
            qdb = (q * qd_ref[h]).astype(BF16)
            kdb = (k * kd_ref[h]).astype(BF16)
            dq = _dot(dab, kb) + _dot_nt(dob, spb) * qd_ref[h]
            dk = _dot_tn(dab, qb) + _dot_nt(vb, dsnb) * kd_ref[h]
            dv = _dot_tn(ab, dob) + _dot(kdb, dsnb)
            dstate[h] = dsn * cdec[h] + _dot_tn(qdb, dob)
            c2, s2 = cos_ref[...], sin_ref[...]
            dp_ref[:, sl] = _rot_t(dq, c2, s2).astype(BF16)
            dp_ref[:, RET_W + h * HEAD_DIM:RET_W + (h + 1) * HEAD_DIM] = _rot_t(dk * K_SCALE, c2, s2).astype(BF16)
            dp_ref[:, 2 * RET_W + h * HEAD_DIM:2 * RET_W + (h + 1) * HEAD_DIM] = dv.astype(BF16)
        for gi in range(HEADS):
            sl = slice(gi * HEAD_DIM, (gi + 1) * HEAD_DIM)
            u = p_ref[:, 4 * RET_W + gi * HEAD_DIM:4 * RET_W + (gi + 1) * HEAD_DIM]
            sv = p_ref[:, 4 * RET_W + SGU_W + gi * HEAD_DIM:4 * RET_W + SGU_W + (gi + 1) * HEAD_DIM]
            gv = _gelu(sv)
            xc = gv - jnp.mean(gv, axis=-1, keepdims=True)
            rstd = lax.rsqrt(jnp.mean(xc * xc, axis=-1, keepdims=True) + EPS)
            xh = xc * rstd
            lg = lng_ref[:, sl]
            vnb = (xh * lg + lnb_ref[:, sl]).astype(BF16)
            wcb = _causal(ws_ref[gi]).astype(BF16)
            mixed = _dot(wcb, vnb) + bsb_ref[gi]
            dsgu = dmix[:, RET_W + gi * HEAD_DIM:RET_W + (gi + 1) * HEAD_DIM]
            dmixed = dsgu * _gelu(u)
            dmb = dmixed.astype(BF16)
            dws_ref[gi] += _causal(_dot_nt(dmb, vnb))
            dbs_acc[gi] += dmixed
            dvn = _dot_tn(wcb, dmb)
            dlng_ref[gi:gi + 1, :] += jnp.sum(dvn * xh, axis=0, keepdims=True)
            dlnb_ref[gi:gi + 1, :] += jnp.sum(dvn, axis=0, keepdims=True)
            dxh = dvn * lg
            dgv = rstd * (dxh - jnp.mean(dxh, axis=-1, keepdims=True) - xh * jnp.mean(dxh * xh, axis=-1, keepdims=True))
            dp_ref[:, 4 * RET_W + gi * HEAD_DIM:4 * RET_W + (gi + 1) * HEAD_DIM] = (dsgu * mixed * _gelu_grad(u)).astype(BF16)
            dp_ref[:, 4 * RET_W + SGU_W + gi * HEAD_DIM:4 * RET_W + SGU_W + (gi + 1) * HEAD_DIM] = (
                dgv * _gelu_grad(sv)).astype(BF16)

        @pl.when(i == N_CHUNK - 1)
        def _():
            for gi in range(HEADS):
                col = jnp.broadcast_to(jnp.sum(dbs_acc[gi], axis=-1, keepdims=True), (CHUNK, CHUNK))
                dbs_ref[gi:gi + 1, :] = jnp.transpose(col)[0:1, :]
            rs.phase3()

    rev = lambda w: pl.BlockSpec((CHUNK, w), lambda i: (N_CHUNK - 1 - i, 0))
    hcc = (HEADS, CHUNK, CHUNK)
    acc = lambda s: pl.BlockSpec(s, lambda i: (0,) * len(s))
    res = pl.pallas_call(
        body, name="bwd_mix", grid=(N_CHUNK,),
        out_shape=(jax.ShapeDtypeStruct((SEQ, PROJ_W), BF16), jax.ShapeDtypeStruct((8, RET_W), F32),
                   jax.ShapeDtypeStruct((8, HEAD_DIM), F32), jax.ShapeDtypeStruct((8, HEAD_DIM), F32),
                   jax.ShapeDtypeStruct(hcc, F32), jax.ShapeDtypeStruct((8, CHUNK), F32)) + _scatter_out_shapes(geoms),
        in_specs=[rev(D_MODEL), rev(PROJ_W), rev(RET_W),
                  pl.BlockSpec((1, HEADS, HEAD_DIM, HEAD_DIM), lambda i: (N_CHUNK - 1 - i, 0, 0, 0)),
                  _resident((D_MODEL, D_MODEL)), _resident((1, RET_W)), _resident((1, SGU_W)), _resident((1, SGU_W)),
                  _resident(hcc), _resident(hcc), _resident(hcc), _resident(hcc), _resident(hcc), rev(HEAD_DIM), rev(HEAD_DIM)]
        + [pl.BlockSpec(memory_space=pl.ANY)] * n_h,
        out_specs=(rev(PROJ_W), acc((8, RET_W)), acc((8, HEAD_DIM)), acc((8, HEAD_DIM)), acc(hcc), acc((8, CHUNK)))
        + _scatter_out_specs(geoms),
        scratch_shapes=[pltpu.VMEM((HEADS, HEAD_DIM, HEAD_DIM), F32), pltpu.VMEM((HEADS, CHUNK, CHUNK), F32)] + _scatter_scratch(geoms),
        compiler_params=_cparams(("arbitrary",), collective=COLLECTIVE["bwd_mix"]),
    )(dx2, proj, o, sprev, wout_g, grn, lng, lnb, ws, bsb, mask, qdec, kdec, cos2, sin2, *[p for _, p in hosted])
    return tuple(res[:6 + n_h])


def _bwd_proj(dproj, win_g, x, g1, dx2, gin_p, small):
    geoms = [W_IN]
    n_s = len(small)

    def body(dp_ref, w_ref, x_ref, g_ref, dx2_ref, gin_ref, *rest):
        small_refs = rest[:n_s]
        dx_ref, rs_out, rp_ref, rws_ref, rcv_ref, dg_ref = rest[n_s:n_s + 6]
        rs_scratch = rest[n_s + 6:n_s + 6 + N_SCATTER_SCRATCH]
        ar_scratch = rest[n_s + 6 + N_SCATTER_SCRATCH:]
        ar_res = ar_scratch[N_SMALL_SCRATCH:]
        ar = _SmallReduce((dg_ref,) + tuple(small_refs), ar_res, ar_scratch[:N_SMALL_SCRATCH])
        rs = _Scatters(geoms, [gin_ref], [rs_out], rs_scratch)
        pl.when(pl.program_id(0) == 0)(lambda: rs.phase1(diagonal=True))
        pl.when(pl.program_id(0) == 1)(rs.phase2)
        pl.when(pl.program_id(0) == 5)(rs.phase2b)

        @pl.when(pl.program_id(0) == 0)
        def _():
            dg_ref[...] = jnp.zeros_like(dg_ref)

        dh = _dot_nt(dp_ref[...], w_ref[...])
        xb = x_ref[...]
        r = lax.rsqrt(jnp.mean(xb * xb, axis=-1, keepdims=True) + EPS)
        xh = xb * r
        dg_ref[0:1, :] += jnp.sum(dh * xh, axis=0, keepdims=True)
        t = dh * g_ref[...]
        dx_ref[...] = dx2_ref[...] + r * (t - xh * jnp.mean(t * xh, axis=-1, keepdims=True))

        @pl.when(pl.program_id(0) == N_TB - 1)
        def _():
            ar.begin()
            rs.phase3()
            ar.end()
            for o_ref, r_ref in zip((rp_ref, rws_ref, rcv_ref), ar_res):
                o_ref[...] = r_ref[...]

    tok = lambda w: pl.BlockSpec((TM, w), lambda i: (i, 0))
    vm = pl.BlockSpec(memory_space=pltpu.VMEM)
    res = pl.pallas_call(
        body, name="bwd_proj", grid=(N_TB,),
        out_shape=(jax.ShapeDtypeStruct((SEQ, D_MODEL), F32),) + _scatter_out_shapes(geoms)
        + tuple(jax.ShapeDtypeStruct(s, F32) for s in SMALL_FULL),
        in_specs=[tok(PROJ_W), _resident((D_MODEL, PROJ_W)), tok(D_MODEL), _resident((1, D_MODEL)), tok(D_MODEL),
                  pl.BlockSpec(memory_space=pl.ANY)] + [vm] * n_s,
        out_specs=(tok(D_MODEL),) + _scatter_out_specs(geoms) + (vm,) * len(SMALL_FULL),
        scratch_shapes=[pltpu.VMEM((8, D_MODEL), F32)] + _scatter_scratch(geoms) + _small_scratch()
        + [pltpu.VMEM(s, F32) for s in SMALL_FULL],
        compiler_params=_cparams(("arbitrary",), collective=COLLECTIVE["bwd_proj"]),
    )(dproj, win_g, x, g1, dx2, gin_p, *small)
    return res


def _wgrad(name, a, b, tm=None, tn=None, hosted=()):
    m_w, n_w = a.shape[-1], b.shape[-1]
    tm = m_w if tm is None else tm
    tn = n_w if tn is None else tn
    n_steps = (m_w // tm) * (n_w // tn)
    geoms = [g for g, _ in hosted]
    n_h = len(hosted)

    def body(a_ref, b_ref, *rest):
        o_ref = rest[n_h]
        if n_h:
            rs = _Scatters(geoms, rest[:n_h], rest[n_h + 1:2 * n_h + 1], rest[2 * n_h + 1:])
            step = pl.program_id(0) * (n_w // tn) + pl.program_id(1)
            pl.when(step == 0)(rs.phase1)
            pl.when(step == 1)(rs.phase2)
            pl.when(step == n_steps // 2)(rs.phase2b)
        o_ref[...] = _dot_tn(a_ref[...].astype(BF16), b_ref[...].astype(BF16)).astype(BF16)
        if n_h:
            pl.when(step == n_steps - 1)(rs.phase3)

    assert not n_h or n_steps >= 4
    res = pl.pallas_call(
        body, name=name, grid=(m_w // tm, n_w // tn),
        out_shape=(jax.ShapeDtypeStruct((m_w, n_w), BF16),) + _scatter_out_shapes(geoms),
        in_specs=[pl.BlockSpec((SEQ, tm), lambda i, j: (0, i)), pl.BlockSpec((SEQ, tn), lambda i, j: (0, j))]
        + [pl.BlockSpec(memory_space=pl.ANY)] * n_h,
        out_specs=(pl.BlockSpec((tm, tn), lambda i, j: (i, j)),) + _scatter_out_specs(geoms),
        scratch_shapes=_scatter_scratch(geoms),
        compiler_params=_cparams(("arbitrary", "arbitrary"), collective=COLLECTIVE[name]) if n_h else _cparams(("parallel", "parallel")),
    )(a, b, *[p for _, p in hosted])
    return tuple(res[:1 + n_h])


def _row_step(half_rows):
    return max(s for s in range(16, 177, 16) if half_rows % s == 0)


class _Scatter:
    def __init__(self, geom, partial, out, land1, mine, stage2, land2, comb, s1_send, s1_recv, s2_send, s2_recv, ld_sems):
        self.w, self.row0, self.shape = _geom(geom)
        self.partial, self.out, self.land1 = partial, out, land1
        self.mine, self.stage2, self.land2, self.comb = mine, stage2, land2, comb
        self.hr = self.shape[0] // 2
        self.step = _row_step(self.hr)
        self.s1_send, self.s1_recv, self.s2_send, self.s2_recv, self.ld_sems = s1_send, s1_recv, s2_send, s2_recv, ld_sems
        self.x, self.y, self.c = lax.axis_index("x"), lax.axis_index("y"), lax.axis_index("c")
        self.sibling = (self.x, self.y, 1 - self.c)
        self.chips = [(self.x, self.y), (1 - self.x, self.y), (self.x, 1 - self.y), (1 - self.x, 1 - self.y)]

    def block(self, px, py, pc):
        dev = 4 * px + 2 * py + pc
        if self.w == W_IN:
            return self.partial.at[:, pl.ds(pl.multiple_of(dev * IN_SHARD, 128), IN_SHARD)]
        if self.w == W_OUT:
            return self.partial.at[pl.ds(pl.multiple_of(dev * OUT_SHARD, 128), OUT_SHARD), :]
        if self.w == W_DOWN:
            return self.partial.at[pl.ds(pl.multiple_of(dev * DOWN_SHARD, 32), DOWN_SHARD), :]
        return self.partial.at[pl.ds(pl.multiple_of(dev * FF_SHARD + self.row0, 32), self.shape[0]), :]

    def copy1(self, k):
        return pltpu.make_async_remote_copy(
            src_ref=self.block(*self.chips[k], 1 - self.c), dst_ref=self.land1.at[k],
            send_sem=self.s1_send.at[k], recv_sem=self.s1_recv.at[k], device_id=self.sibling, device_id_type=MESH)

    STAGE2 = [(1, 0, 1), (3, 0, 1), (2, 1, 2), (3, 1, 2), (1, 1, 1), (2, 0, 2)]

    def copy2(self, j):
        blk, h, to = self.STAGE2[j]
        src = self.comb.at[j - 4] if j >= 4 else self.stage2.at[blk - 1, pl.ds(h * self.hr, self.hr), :]
        return pltpu.make_async_remote_copy(
            src_ref=src, dst_ref=self.land2.at[j], send_sem=self.s2_send.at[j], recv_sem=self.s2_recv.at[j],
            device_id=(*self.chips[to], self.c), device_id_type=MESH)

    def _rows(self, h=None):
        step = self.step
        lo, n = (0, self.shape[0]) if h is None else (h * self.hr, self.hr)
        return [pl.ds(r0, step) for r0 in range(lo, lo + n, step)]

    def load(self, k):
        return pltpu.make_async_copy(self.block(*self.chips[k], self.c), self.mine.at[k], self.ld_sems.at[k])

    def load_mine(self):
        for k in range(4):
            self.load(k).start()

    def phase1(self):
        for k in range(4):
            self.copy1(k).start()

    def phase2(self, k):
        self.copy1(k).wait_recv()
        self.load(k).wait()
        for rs in self._rows():
            s = self.mine[k, rs, :].astype(F32) + self.land1[k, rs, :].astype(F32)
            if k == 0:
                self.out[rs, :] = s
            else:
                self.stage2[k - 1, rs, :] = s.astype(BF16)
        for j in {3: (1, 3), 1: (0,), 2: (2,), 0: ()}[k]:
            self.copy2(j).start()

    def phase2b(self):
        for j, got in ((4, 3), (5, 1)):
            blk, h, _ = self.STAGE2[j]
            self.copy2(got).wait_recv()
            for i, rs in enumerate(self._rows(h)):
                lr = pl.ds(i * self.step, self.step)
                self.comb[j - 4, lr, :] = (self.stage2[blk - 1, rs, :].astype(F32) + self.land2[got, lr, :].astype(F32)).astype(BF16)
            self.copy2(j).start()

    def phase3(self):
        for j in (0, 5, 4, 2):
            self.copy2(j).wait_recv()
        for h, (first, second) in enumerate(((0, 5), (4, 2))):
            for i, rs in enumerate(self._rows(h)):
                lr = pl.ds(i * self.step, self.step)
                self.out[rs, :] = (self.out[rs, :] + self.land2[first, lr, :].astype(F32)) + self.land2[second, lr, :].astype(F32)
        for k in range(4):
            self.copy1(k).wait_send()
        for j in range(6):
            self.copy2(j).wait_send()


def _geom(geom):
    if isinstance(geom, tuple):
        w, row0, rows = geom
        assert w == W_UP
        return w, row0, (rows, SHARD[w][1])
    return geom, 0, SHARD[geom]


N_SCATTER_SCRATCH = 10


def _scatter_out_shapes(geoms):
    return tuple(jax.ShapeDtypeStruct(_geom(g)[2], F32) for g in geoms)


def _scatter_out_specs(geoms):
    return (pl.BlockSpec(memory_space=pltpu.VMEM),) * len(geoms)


def _scatter_scratch(geoms):
    out = []
    for g in geoms:
        s = _geom(g)[2]
        hs = (s[0] // 2, s[1])
        out += [pltpu.VMEM((4,) + s, BF16), pltpu.VMEM((4,) + s, BF16), pltpu.VMEM((3,) + s, BF16), pltpu.VMEM((6,) + hs, BF16),
                pltpu.VMEM((2,) + hs, BF16),
                pltpu.SemaphoreType.DMA((4,)), pltpu.SemaphoreType.DMA((4,)), pltpu.SemaphoreType.DMA((6,)),
                pltpu.SemaphoreType.DMA((6,)), pltpu.SemaphoreType.DMA((4,))]
    return out


class _Scatters:
    def __init__(self, geoms, p_refs, out_refs, scratch):
        k = N_SCATTER_SCRATCH
        self.items = [_Scatter(g, p_refs[i], out_refs[i], *scratch[k * i:k * i + k]) for i, g in enumerate(geoms)]

    def phase1(self, diagonal=False):
        meet = _Meet(diagonal)
        meet.signal()
        for s in self.items:
            s.load_mine()
        meet.wait()
        for s in self.items:
            s.phase1()

    def phase2(self):
        for k in (3, 1, 2, 0):
            for s in self.items:
                s.phase2(k)

    def phase2b(self):
        for s in self.items:
            s.phase2b()

    def phase3(self):
        for s in self.items:
            s.phase3()


PACK_W = 1024


SMALL_FULL = [(2, 8, PACK_W), (HEADS, CHUNK, CHUNK), (2, 8, D_FF)]
SMALL_HALF = [(s[0] // 2,) + s[1:] for s in SMALL_FULL]
N_SMALL_SCRATCH = 16


def _small_scratch():
    n_a = len(SMALL_FULL)
    return ([pltpu.VMEM(SMALL_FULL[0], F32)] + [pltpu.VMEM(s, F32) for s in SMALL_HALF] + [pltpu.VMEM(s, F32) for s in SMALL_HALF]
            + [pltpu.VMEM((3,) + s, F32) for s in SMALL_HALF]
            + [pltpu.SemaphoreType.DMA((n_a,)), pltpu.SemaphoreType.DMA((n_a,)), pltpu.SemaphoreType.DMA((n_a, 3)),
               pltpu.SemaphoreType.DMA((n_a, 3)), pltpu.SemaphoreType.DMA((n_a,)), pltpu.SemaphoreType.DMA((n_a,))])


class _SmallReduce:
    def __init__(self, ins, outs, scratch):
        self.ins, self.outs = ins, outs
        (self.pack, *rest) = scratch
        self.rxs, self.css, self.gs = rest[0:3], rest[3:6], rest[6:9]
        self.s1_send, self.s1_recv, self.s2_send, self.s2_recv, self.s3_send, self.s3_recv = rest[9:]
        self.x, self.y, self.c = lax.axis_index("x"), lax.axis_index("y"), lax.axis_index("c")
        self.sibling = (self.x, self.y, 1 - self.c)
        self.chips = [(1 - self.x, self.y), (self.x, 1 - self.y), (1 - self.x, 1 - self.y)]
        self.hl = [s[0] for s in SMALL_HALF]

    def half(self, ref, a, h):
        return ref.at[pl.ds(h * self.hl[a], self.hl[a])]

    def begin(self):
        dg1_ref, dg2_ref, dgf_ref, dgrn_ref, dlng_ref, dlnb_ref, dbs_ref, loss_ref, dws_ref, dcv_ref = self.ins
        pack, c = self.pack, self.c
        pack[...] = jnp.zeros_like(pack)
        pack[0, 0:1, :] = dg1_ref[0:1, :]
        pack[0, 1:2, :] = dg2_ref[0:1, :]
        pack[0, 2:3, :] = dgf_ref[0:1, :]
        pack[0, 3:4, 0:RET_W] = dgrn_ref[0:1, :]
        lsum = loss_ref[0, 0:1, :]
        for i in range(1, N_TB):
            lsum = lsum + loss_ref[i, 0:1, :]
        pack[0, 3:4, RET_W:RET_W + 128] = lsum
        pack[1, 0:HEADS, 0:128] = dlng_ref[0:HEADS, :]
        pack[1, 0:HEADS, 128:256] = dlnb_ref[0:HEADS, :]
        pack[1, 0:HEADS, 256:384] = dbs_ref[0:HEADS, :]
        self.srcs = [pack, dws_ref, dcv_ref]
        n_a = len(self.srcs)
        self.ex1 = [pltpu.make_async_remote_copy(src_ref=self.half(self.srcs[a], a, 1 - c), dst_ref=self.rxs[a],
                                                 send_sem=self.s1_send.at[a], recv_sem=self.s1_recv.at[a],
                                                 device_id=self.sibling, device_id_type=MESH) for a in range(n_a)]
        for cp in self.ex1:
            cp.start()
        self.ex2 = []
        for a in range(n_a):
            self.ex1[a].wait_recv()
            self.css[a][...] = self.half(self.srcs[a], a, c)[...] + self.rxs[a][...]
            for j, chip in enumerate(self.chips):
                cp = pltpu.make_async_remote_copy(src_ref=self.css[a], dst_ref=self.gs[a].at[j], send_sem=self.s2_send.at[a, j],
                                                  recv_sem=self.s2_recv.at[a, j], device_id=(*chip, c), device_id_type=MESH)
                cp.start()
                self.ex2.append(cp)

    def end(self):
        c, x, y = self.c, self.x, self.y
        ex3 = []
        for a in range(len(self.srcs)):
            css, gs, out = self.css[a], self.gs[a], self.outs[a]
            for j in range(3):
                self.ex2[3 * a + j].wait_recv()
            tot = None
            for q in range(4):
                k = jnp.where(x != (q >> 1), 1, 0) + jnp.where(y != (q & 1), 2, 0)
                term = jnp.where(k == 0, css[...], jnp.where(k == 1, gs[0], jnp.where(k == 2, gs[1], gs[2])))
                tot = term if tot is None else tot + term
            self.half(out, a, c)[...] = tot
            cp = pltpu.make_async_remote_copy(src_ref=self.half(out, a, c), dst_ref=self.half(out, a, c), send_sem=self.s3_send.at[a],
                                              recv_sem=self.s3_recv.at[a], device_id=self.sibling, device_id_type=MESH)
            cp.start()
            ex3.append(cp)
        for a in range(len(self.srcs)):
            out = self.outs[a]
            pltpu.make_async_remote_copy(src_ref=self.half(out, a, 1 - c), dst_ref=self.half(out, a, 1 - c), send_sem=self.s3_send.at[a],
                                         recv_sem=self.s3_recv.at[a], device_id=self.sibling, device_id_type=MESH).wait_recv()
        for cp in self.ex1 + self.ex2 + ex3:
            cp.wait_send()


def _adam_math(w, g, m, v):
    nm = ADAM_B1 * m + (1.0 - ADAM_B1) * g
    nv = ADAM_B2 * v + (1.0 - ADAM_B2) * (g * g)
    d = -ADAM_LR * ((nm / (1.0 - ADAM_B1 ** ADAM_STEP)) / (jnp.sqrt(nv / (1.0 - ADAM_B2 ** ADAM_STEP)) + ADAM_EPS) + ADAM_WD * w)
    return d, nm, nv


def _adamw(name, w, gs, m, v, rows, thru=()):
    _, r, cdim = w.shape
    n_steps = r // rows
    half = gs[0].shape[0] // rows
    n_g, n_t = len(gs), len(thru)

    def body(w_ref, *rest):
        g_refs, (m_ref, v_ref), t_refs = rest[:n_g], rest[n_g:n_g + 2], rest[n_g + 2:n_g + 2 + n_t]
        go_ref, d_ref, nm_ref, nv_ref, *to_refs = rest[n_g + 2 + n_t:]
        gg = g_refs[0][...]
        if n_g == 2:
            gg = jnp.where(pl.program_id(0) < half, gg, g_refs[1][...])
        go_ref[0] = gg
        d, nm, nv = _adam_math(w_ref[0], gg, m_ref[0], v_ref[0])
        d_ref[0], nm_ref[0], nv_ref[0] = d, nm, nv
        for t_ref, to_ref in zip(t_refs, to_refs):
            to_ref[...] = t_ref[...]

    spec3 = pl.BlockSpec((1, rows, cdim), lambda i: (0, i, 0))
    if n_g == 1:
        g_specs = [pl.BlockSpec((rows, cdim), lambda i: (i, 0))]
    else:
        g_specs = [pl.BlockSpec((rows, cdim), lambda i: (jnp.minimum(i, half - 1), 0)),
                   pl.BlockSpec((rows, cdim), lambda i: (jnp.maximum(i - half, 0), 0))]
    t_specs = [pl.BlockSpec((t.shape[0] // n_steps, t.shape[1]), lambda i: (i, 0)) for t in thru]
    sh = jax.ShapeDtypeStruct((1, r, cdim), F32)
    return pl.pallas_call(
        body, name=name, grid=(n_steps,), out_shape=(sh, sh, sh, sh) + tuple(jax.ShapeDtypeStruct(t.shape, t.dtype) for t in thru),
        in_specs=[spec3] + g_specs + [spec3, spec3] + t_specs, out_specs=(spec3,) * 4 + tuple(t_specs),
        compiler_params=_cparams(("parallel",)),
    )(w, *gs, m, v, *thru)


def _adamw_small(rp, rws, rcv, gcw, params):
    n_p = len(params)

    def body(*refs):
        rp_ref, rws_ref, rcv_ref, gcw_ref = refs[:4]
        ins = refs[4:4 + 3 * n_p]
        outs = refs[4 + 3 * n_p:]
        outs[4 * n_p][...] = rp_ref[0, 3:4, RET_W:RET_W + 1]
        grads = [rp_ref[0, 0:1, :], rp_ref[0, 1:2, :], rp_ref[0, 2:3, :], rp_ref[0, 3:4, 0:RET_W],
                 rp_ref[1, 0:HEADS, 0:128], rp_ref[1, 0:HEADS, 128:256], rp_ref[1, 0:HEADS, 256:384],
                 rws_ref[...], gcw_ref[...], None]
        for p in range(n_p):
            w_ref, m_ref, v_ref = ins[3 * p:3 * p + 3]
            o = outs[4 * p:4 * p + 4]
            if p == n_p - 1:
                for hf in range(2):
                    cs = slice(hf * D_FF, (hf + 1) * D_FF)
                    g = rcv_ref[hf, 3:4, :]
                    res = (g,) + _adam_math(w_ref[:, cs], g, m_ref[:, cs], v_ref[:, cs])
                    for t in range(4):
                        o[t][:, cs] = res[t]
                continue
            lead = w_ref.ndim > grads[p].ndim
            rd = (lambda r: r[0]) if lead else (lambda r: r[...])
            res = (grads[p],) + _adam_math(rd(w_ref), grads[p], rd(m_ref), rd(v_ref))
            for t in range(4):
                if lead:
                    o[t][0] = res[t]
                else:
                    o[t][...] = res[t]

    vm = pl.BlockSpec(memory_space=pltpu.VMEM)
    flat = [a for tr in params for a in tr]
    out_shape = tuple(jax.ShapeDtypeStruct(tr[0].shape, F32) for tr in params for _ in range(4)) + (jax.ShapeDtypeStruct((1, 1), F32),)
    res = pl.pallas_call(
        body, name="adamw_small", out_shape=out_shape, in_specs=[vm] * (4 + len(flat)), out_specs=(vm,) * len(out_shape),
        compiler_params=_cparams(),
    )(rp, rws, rcv, gcw, *flat)
    return [res[4 * p:4 * p + 4] for p in range(n_p)], res[4 * n_p]


def kernel(x, mix_norm_g, w_in, ret_norm_g, sgu_ln_g, sgu_ln_b, sgu_w_s, sgu_b_s, w_out, ffn_norm_g, w_up, conv_w, conv_b, w_down, final_norm_g, loss_target, m_mix_norm_g, m_w_in, m_ret_norm_g, m_sgu_ln_g, m_sgu_ln_b, m_sgu_w_s, m_sgu_b_s, m_w_out, m_ffn_norm_g, m_w_up, m_conv_w, m_conv_b, m_w_down, m_final_norm_g, v_mix_norm_g, v_w_in, v_ret_norm_g, v_sgu_ln_g, v_sgu_ln_b, v_sgu_w_s, v_sgu_b_s, v_w_out, v_ffn_norm_g, v_w_up, v_conv_w, v_conv_b, v_w_down, v_final_norm_g):
    xs = x[0]
    tgt = loss_target[0]
    mask, qdec, kdec = _decay_tables()
    grn = ret_norm_g.reshape(1, RET_W)
    lng = sgu_ln_g.reshape(1, SGU_W)
    lnb = sgu_ln_b.reshape(1, SGU_W)
    ws = sgu_w_s[0]
    bsb = jnp.broadcast_to(sgu_b_s[0][:, :, None], (HEADS, CHUNK, HEAD_DIM))
    gf = final_norm_g.reshape(1, D_MODEL)
    me = 4 * lax.axis_index("x") + 2 * lax.axis_index("y") + lax.axis_index("c")
    tr = lambda a: jnp.transpose(a[0])[None]
    tr_cw = lambda a: jnp.transpose(a, (1, 0, 2))

    proj, h1, cos2, sin2, win_g, cw_sh, wout_g, wdn_g, su = _fwd_proj(
        xs, mix_norm_g, _rope_freq(), w_in[0], w_out[0], tr(w_up)[0], w_down[0], tr_cw(conv_w))
    cw_g = jnp.transpose(cw_sh, (1, 0, 2)).reshape(8, 2 * D_FF)
    x2, mixcat, o, sprev, wup_g = _fwd_mix(xs, proj, wout_g, grn, lng, lnb, ws, bsb, mask, qdec, kdec, su)
    h2, up_pre, u_conv, act, x3, loss_parts = _fwd_ffn(x2, ffn_norm_g, wup_g, cw_g, conv_b, wdn_g, gf, tgt)

    dx3, dpre, dx2, dgf, dg2, dcv = _bwd_ffn(x3, tgt, gf, x2, ffn_norm_g, up_pre, u_conv, wup_g, cw_g, wdn_g)
    band = 512
    (gdn_p,) = _wgrad("wgrad_down", act, dx3, tm=FF_TILE)
    (gout_p,) = _wgrad("wgrad_out", mixcat, dx2, tn=512)
    gup_p, g_dn = _wgrad("wgrad_up", dpre, h2, tm=FF_TILE, tn=512, hosted=[(W_DOWN, gdn_p)])
    dproj, dgrn, dlng, dlnb, dws, dbs, g_up_a, g_out = _bwd_mix(
        dx2, proj, o, sprev, wout_g, grn, lng, lnb, ws, bsb, mask, qdec, kdec, cos2, sin2,
        [((W_UP, 0, band), gup_p), (W_OUT, gout_p)])
    gin_p, g_up_b = _wgrad("wgrad_in", h1, dproj, tm=512, tn=768, hosted=[((W_UP, band, FF_SHARD - band), gup_p)])
    grad_x, g_in, rp, rws, rcv = _bwd_proj(dproj, win_g, xs, mix_norm_g, dx2, gin_p,
                                           (dg2, dgf, dgrn, dlng, dlnb, dbs, loss_parts, dws, dcv))
    gcw = tr_cw(lax.dynamic_slice(rcv, (me // (N_DEV // 2), 0, (me % (N_DEV // 2)) * FF_SHARD), (1, 3, FF_SHARD)))

    table = {}
    *table["w_in"], grad_x = _adamw("adamw_w_in", w_in, [g_in], m_w_in, v_w_in, 256, thru=[grad_x])
    for name, w, gs, m, v, rows in (("w_out", w_out, [g_out], m_w_out, v_w_out, 128),
                                    ("w_up", tr(w_up), [g_up_a, g_up_b], tr(m_w_up), tr(v_w_up), 64),
                                    ("w_down", w_down, [g_dn], m_w_down, v_w_down, 88)):
        table[name] = _adamw("adamw_" + name, w, gs, m, v, rows)
    table["w_up"] = tuple(tr(a) for a in table["w_up"])
    row = lambda a: a.reshape(1, D_MODEL)
    names_small = ["mix_norm_g", "ffn_norm_g", "final_norm_g", "ret_norm_g", "sgu_ln_g", "sgu_ln_b", "sgu_b_s", "sgu_w_s",
                   "conv_w", "conv_b"]
    params = [(mix_norm_g, m_mix_norm_g, v_mix_norm_g), (ffn_norm_g, m_ffn_norm_g, v_ffn_norm_g),
              (row(final_norm_g), row(m_final_norm_g), row(v_final_norm_g)), (ret_norm_g, m_ret_norm_g, v_ret_norm_g),
              (sgu_ln_g, m_sgu_ln_g, v_sgu_ln_g), (sgu_ln_b, m_sgu_ln_b, v_sgu_ln_b), (sgu_b_s, m_sgu_b_s, v_sgu_b_s),
              (sgu_w_s, m_sgu_w_s, v_sgu_w_s), (tr_cw(conv_w), tr_cw(m_conv_w), tr_cw(v_conv_w)), (conv_b, m_conv_b, v_conv_b)]
    small, loss = _adamw_small(rp, rws, rcv, gcw, params)
    for n, res in zip(names_small, small):
        table[n] = res
    table["final_norm_g"] = tuple(a.reshape(D_MODEL) for a in table["final_norm_g"])
    table["conv_w"] = tuple(tr_cw(a) for a in table["conv_w"])

    order = ["mix_norm_g", "w_in", "ret_norm_g", "sgu_ln_g", "sgu_ln_b", "sgu_w_s", "sgu_b_s", "w_out", "ffn_norm_g", "w_up",
             "conv_w", "conv_b", "w_down", "final_norm_g"]
    outs = [loss.reshape(()), grad_x[None]]
    for col in range(4):
        outs += [table[n][col] for n in order]
    return tuple(outs)
```

```python
import functools
import math

import jax
import jax.numpy as jnp
import numpy as np
from jax import lax
from jax.experimental import pallas as pl
from jax.experimental.pallas import tpu as pltpu

F32 = jnp.float32
BF16 = jnp.bfloat16
MESH = pl.DeviceIdType.MESH

N_DEV = 8
SEQ = 2048
D_MODEL = 1024
CHUNK = 128
N_CHUNK = SEQ // CHUNK
HEADS = 4
HEAD_DIM = 128
RET_W = 512
SGU_W = 512
PROJ_W = 3072
D_FF = 2816
FF_SHARD = 704
FF_TILE = 1408
FF_TILES = ((0, 1536), (1536, 1280))
IN_SHARD = PROJ_W // N_DEV
OUT_SHARD = D_MODEL // N_DEV
DOWN_SHARD = D_FF // N_DEV
TM = 256
N_TB = SEQ // TM
FWD_PROJ_PASS_AT = 5
FWD_MIX_PASS_AT = 10
FWD_FFN_PASS_AT = 3
FWD_FFN_LAG = 6
EPS = 1e-6
ROPE_BASE = 10000.0
K_SCALE = HEAD_DIM ** -0.5
INV_SQRT2 = 0.7071067811865476
INV_SQRT_2PI = 0.3989422804014327

ADAM_LR = 0.001
ADAM_B1 = 0.9
ADAM_B2 = 0.999
ADAM_EPS = 1e-08
ADAM_WD = 0.01
ADAM_STEP = 10

VMEM_LIMIT = 56 * 1024 * 1024


def _cparams(sem=None, vmem=VMEM_LIMIT, collective=None):
    return pltpu.CompilerParams(dimension_semantics=sem, vmem_limit_bytes=vmem, collective_id=collective)


COLLECTIVE = {name: k for k, name in enumerate(("fwd_proj", "fwd_mix", "fwd_ffn", "wgrad_up", "bwd_mix", "wgrad_in", "bwd_proj"))}


class _Meet:
    def __init__(self, diagonal):
        x, y, c = lax.axis_index("x"), lax.axis_index("y"), lax.axis_index("c")
        self.peers = [(x, y, 1 - c), (1 - x, y, c), (x, 1 - y, c)] + ([(1 - x, 1 - y, c)] if diagonal else [])

    def signal(self):
        for peer in self.peers:
            pl.semaphore_signal(pltpu.get_barrier_semaphore(), inc=1, device_id=peer, device_id_type=MESH)

    def wait(self):
        pl.semaphore_wait(pltpu.get_barrier_semaphore(), len(self.peers))


def _resident(shape):
    nd = len(shape)
    return pl.BlockSpec(shape, lambda *_: (0,) * nd, pipeline_mode=pl.Buffered(1))


def _dot(a, b):
    return jnp.dot(a, b, preferred_element_type=F32)


def _dot_nt(a, b):
    return lax.dot_general(a, b, (((1,), (1,)), ((), ())), preferred_element_type=F32)


def _dot_tn(a, b):
    return lax.dot_general(a, b, (((0,), (0,)), ((), ())), preferred_element_type=F32)


def _sigmoid(x):
    return 1.0 / (1.0 + jnp.exp(-x))


def _gelu(x):
    return 0.5 * x * (1.0 + lax.erf(x * INV_SQRT2))


def _gelu_grad(x):
    return 0.5 * (1.0 + lax.erf(x * INV_SQRT2)) + x * (jnp.exp(-0.5 * x * x) * INV_SQRT_2PI)


def _rot(xh, cos2, sin2):
    return xh * cos2 + pltpu.roll(xh, HEAD_DIM // 2, 1) * sin2


def _rot_t(dh, cos2, sin2):
    return dh * cos2 + pltpu.roll(dh * sin2, HEAD_DIM // 2, 1)


def _rope_freq():
    half = HEAD_DIM // 2
    inv_freq = jnp.power(ROPE_BASE, -jnp.arange(half, dtype=F32) / half)
    return jnp.concatenate([inv_freq, inv_freq])[None, :]


def _rope_block(inv2, first_row):
    pos = (lax.broadcasted_iota(jnp.int32, (TM, HEAD_DIM), 0) + first_row).astype(F32)
    ang = pos * inv2
    sin = jnp.sin(ang)
    lane = lax.broadcasted_iota(jnp.int32, (TM, HEAD_DIM), 1)
    return jnp.cos(ang), jnp.where(lane < HEAD_DIM // 2, -sin, sin)


def _decay_tables():
    log_gamma = jnp.log(1.0 - jnp.power(2.0, -5.0 - jnp.arange(HEADS, dtype=F32)))
    pos = jnp.arange(CHUNK, dtype=F32)
    diff = pos[:, None] - pos[None, :]
    mask = jnp.where(diff >= 0.0, jnp.exp(log_gamma[:, None, None] * jnp.maximum(diff, 0.0)[None]), 0.0)
    k_decay = jnp.exp(log_gamma[:, None] * (CHUNK - 1.0 - pos)[None])
    q_decay = jnp.exp(log_gamma[:, None] * (pos + 1.0)[None])
    kd = jnp.broadcast_to(k_decay[:, :, None], (HEADS, CHUNK, HEAD_DIM))
    qd = jnp.broadcast_to(q_decay[:, :, None], (HEADS, CHUNK, HEAD_DIM))
    return mask.astype(F32), qd.astype(F32), kd.astype(F32)


def _chunk_decay():
    lg = np.log(np.float32(1.0) - np.power(np.float32(2.0), -5.0 - np.arange(HEADS, dtype=np.float32))).astype(np.float32)
    return [float(np.exp(lg[h] * np.float32(CHUNK))) for h in range(HEADS)]


W_IN, W_OUT, W_UP, W_DOWN, W_CONV = range(5)
GATHERED = {W_IN: ((D_MODEL, PROJ_W), BF16), W_OUT: ((D_MODEL, D_MODEL), BF16), W_UP: ((2 * D_FF, D_MODEL), BF16),
            W_DOWN: ((D_FF, D_MODEL), BF16), W_CONV: ((N_DEV, 8, FF_SHARD), F32)}
SHARD = {W_IN: (D_MODEL, IN_SHARD), W_OUT: (OUT_SHARD, D_MODEL), W_UP: (FF_SHARD, D_MODEL), W_DOWN: (DOWN_SHARD, D_MODEL),
         W_CONV: (8, FF_SHARD)}


class _Gather:
    N_SEMS = 9

    def __init__(self, ids, stages, gathered, send_sems, recv_sems, local_sems):
        self.ids, self.stages, self.gathered = ids, stages, gathered
        self.send_sems, self.recv_sems, self.local_sems = send_sems, recv_sems, local_sems
        self.x, self.y, self.c = lax.axis_index("x"), lax.axis_index("y"), lax.axis_index("c")
        self.me = (self.x, self.y, self.c)
        self.sibling = (self.x, self.y, 1 - self.c)
        self.chips = [(1 - self.x, self.y), (self.x, 1 - self.y), (1 - self.x, 1 - self.y)]

    def slot(self, n, px, py, pc):
        dev = 4 * px + 2 * py + pc
        w, g = self.ids[n], self.gathered[n]
        if w == W_IN:
            return g.at[:, pl.ds(pl.multiple_of(dev * IN_SHARD, 128), IN_SHARD)]
        if w == W_OUT:
            return g.at[pl.ds(pl.multiple_of(dev * OUT_SHARD, 128), OUT_SHARD), :]
        if w == W_DOWN:
            return g.at[pl.ds(pl.multiple_of(dev * DOWN_SHARD, 32), DOWN_SHARD), :]
        if w == W_UP:
            return g.at[pl.ds(pl.multiple_of(dev * FF_SHARD, 32), FF_SHARD), :]
        return g.at[dev]

    def half(self, n, px, py, pc, h):
        dev = 4 * px + 2 * py + pc
        w, g = self.ids[n], self.gathered[n]
        if w == W_IN:
            return g.at[pl.ds(h * (D_MODEL // 2), D_MODEL // 2), pl.ds(pl.multiple_of(dev * IN_SHARD, 128), IN_SHARD)]
        rows = SHARD[w][0] // 2
        return g.at[pl.ds(pl.multiple_of(dev * SHARD[w][0] + h * rows, 16), rows), :]

    def tree(self, n):
        return self.ids[n] != W_CONV

    def copy(self, n, k, block, to, src=None, h=None):
        ref = self.slot(n, *block) if h is None else self.half(n, *block, h)
        return pltpu.make_async_remote_copy(
            src_ref=ref if src is None else src, dst_ref=ref,
            send_sem=self.send_sems.at[n, k], recv_sem=self.recv_sems.at[n, k], device_id=to, device_id_type=MESH)

    def _mine(self):
        return [pltpu.make_async_copy(self.stages[n], self.slot(n, *self.me), self.local_sems.at[n]) for n in range(len(self.ids))]

    def _first(self):
        out = []
        for n in range(len(self.ids)):
            out.append(self.copy(n, 0, self.me, self.sibling, src=self.stages[n]))
            out += [self.copy(n, 1 + j, self.me, (*chip, self.c), src=self.stages[n])
                    for j, chip in enumerate(self.chips[:2] if self.tree(n) else self.chips)]
        return out

    def start(self):
        for cp in self._mine() + self._first():
            cp.start()

    def _passed(self, j):
        dev = (*self.chips[j], self.c)
        out = []
        for n in range(len(self.ids)):
            if not self.tree(n):
                out.append(self.copy(n, 4 + j, dev, self.sibling))
            elif j < 2:
                out += [self.copy(n, 3 + j, dev, (*self.chips[1 - j], self.c), h=j), self.copy(n, 5 + j, dev, self.sibling)]
            else:
                out += [self.copy(n, 7, dev, self.sibling, h=0), self.copy(n, 8, dev, self.sibling, h=1)]
        return out

    def near(self):
        for j in range(2):
            dev = (*self.chips[j], self.c)
            for n in range(len(self.ids)):
                self.copy(n, 1 + j, dev, self.me).wait_recv()
            for cp in self._passed(j):
                cp.start()

    def finish(self):
        dev = (*self.chips[2], self.c)
        for n in range(len(self.ids)):
            if self.tree(n):
                self.copy(n, 3, dev, self.me, h=0).wait_recv()
                self.copy(n, 4, dev, self.me, h=1).wait_recv()
            else:
                self.copy(n, 3, dev, self.me).wait_recv()
        for cp in self._passed(2):
            cp.start()
        for n in range(len(self.ids)):
            self.copy(n, 0, self.sibling, self.me).wait_recv()
            for j, chip in enumerate(self.chips):
                dev = (*chip, 1 - self.c)
                if not self.tree(n):
                    self.copy(n, 4 + j, dev, self.me).wait_recv()
                elif j < 2:
                    self.copy(n, 5 + j, dev, self.me).wait_recv()
                else:
                    self.copy(n, 7, dev, self.me, h=0).wait_recv()
                    self.copy(n, 8, dev, self.me, h=1).wait_recv()
        for cp in self._mine():
            cp.wait()
        for cp in self._first() + self._passed(0) + self._passed(1) + self._passed(2):
            cp.wait_send()


def _gather_scratch(n):
    return [pltpu.SemaphoreType.DMA((n, _Gather.N_SEMS)), pltpu.SemaphoreType.DMA((n, _Gather.N_SEMS)), pltpu.SemaphoreType.DMA((n,))]


def _gathered_shapes(ids):
    return tuple(jax.ShapeDtypeStruct(*GATHERED[w]) for w in ids)


def _fwd_proj(x, g1, inv2, w_in, w_out, w_up, w_down, conv_w):
    ids_a, ids_b = [W_IN, W_CONV], [W_OUT]

    def body(x_ref, g_ref, inv_ref, in_hbm, out_hbm, up_hbm, dn_hbm, cw_ref,
             proj_ref, h1_ref, cos_ref, sin_ref, gin, gcw, gout, su_ref, s_dn,
             w_vm, s_in, s_cw, s_out, f_in, f_out, f_up, f_dn, ld_sems,
             a_send, a_recv, a_local, b_send, b_recv, b_local):
        ag_a = _Gather(ids_a, [s_in, s_cw], [gin, gcw], a_send, a_recv, a_local)
        ag_b = _Gather(ids_b, [s_out], [gout], b_send, b_recv, b_local)

        @pl.when(pl.program_id(0) == 0)
        def _():
            meet = _Meet(diagonal=True)
            meet.signal()
            loads = [pltpu.make_async_copy(src, dst, ld_sems.at[i])
                     for i, (src, dst) in enumerate(((in_hbm, f_in), (out_hbm, f_out), (dn_hbm, f_dn), (up_hbm, f_up)))]
            for cp in loads:
                cp.start()
            s_cw[...] = jnp.zeros_like(s_cw)
            for k in range(3):
                s_cw[k:k + 1, :] = cw_ref[k]
            loads[0].wait()
            s_in[...] = f_in[...].astype(BF16)
            meet.wait()
            ag_a.start()
            loads[1].wait()
            s_out[...] = f_out[...].astype(BF16)
            ag_a.near()
            ag_b.start()
            loads[2].wait()
            s_dn[...] = f_dn[...].astype(BF16)
            loads[3].wait()
            su_ref[...] = f_up[...].astype(BF16)
            ag_a.finish()
            fill = pltpu.make_async_copy(gin, w_vm, ld_sems.at[4])
            fill.start()
            fill.wait()

        pl.when(pl.program_id(0) == FWD_PROJ_PASS_AT)(ag_b.near)

        xb = x_ref[...]
        r = lax.rsqrt(jnp.mean(xb * xb, axis=-1, keepdims=True) + EPS)
        h = ((xb * r) * g_ref[...]).astype(BF16)
        h1_ref[...] = h
        p = _dot(h, w_vm[...])
        c2, s2 = _rope_block(inv_ref[...], pl.program_id(0) * TM)
        cos_ref[...], sin_ref[...] = c2, s2
        for hd in range(HEADS):
            sl = slice(hd * HEAD_DIM, (hd + 1) * HEAD_DIM)
            proj_ref[:, sl] = _rot(p[:, sl], c2, s2)
            ks = slice(RET_W + hd * HEAD_DIM, RET_W + (hd + 1) * HEAD_DIM)
            proj_ref[:, ks] = _rot(p[:, ks], c2, s2) * K_SCALE
        proj_ref[:, 2 * RET_W:] = p[:, 2 * RET_W:]

        pl.when(pl.program_id(0) == N_TB - 1)(ag_b.finish)

    tok = lambda w: pl.BlockSpec((TM, w), lambda i: (i, 0))
    hbm = pl.BlockSpec(memory_space=pl.ANY)
    vm = pl.BlockSpec(memory_space=pltpu.VMEM)
    return pl.pallas_call(
        body, name="fwd_proj", grid=(N_TB,),
        out_shape=(jax.ShapeDtypeStruct((SEQ, PROJ_W), F32), jax.ShapeDtypeStruct((SEQ, D_MODEL), BF16),
                   jax.ShapeDtypeStruct((SEQ, HEAD_DIM), F32), jax.ShapeDtypeStruct((SEQ, HEAD_DIM), F32))
        + _gathered_shapes(ids_a + ids_b) + (jax.ShapeDtypeStruct(SHARD[W_UP], BF16), jax.ShapeDtypeStruct(SHARD[W_DOWN], BF16)),
        in_specs=[tok(D_MODEL), _resident((1, D_MODEL)), _resident((1, HEAD_DIM)), hbm, hbm, hbm, hbm, vm],
        out_specs=(tok(PROJ_W), tok(D_MODEL), tok(HEAD_DIM), tok(HEAD_DIM), hbm, hbm, hbm, vm, vm),
        scratch_shapes=[pltpu.VMEM((D_MODEL, PROJ_W), BF16), pltpu.VMEM(SHARD[W_IN], BF16), pltpu.VMEM(SHARD[W_CONV], F32),
                        pltpu.VMEM(SHARD[W_OUT], BF16),
                        pltpu.VMEM(SHARD[W_IN], F32), pltpu.VMEM(SHARD[W_OUT], F32), pltpu.VMEM(SHARD[W_UP], F32),
                        pltpu.VMEM(SHARD[W_DOWN], F32), pltpu.SemaphoreType.DMA((5,))]
        + _gather_scratch(len(ids_a)) + _gather_scratch(len(ids_b)),
        compiler_params=_cparams(("arbitrary",), collective=COLLECTIVE["fwd_proj"]),
    )(x, g1, inv2, w_in, w_out, w_up, w_down, conv_w)


def _causal(w):
    r = lax.broadcasted_iota(jnp.int32, (CHUNK, CHUNK), 0)
    c = lax.broadcasted_iota(jnp.int32, (CHUNK, CHUNK), 1)
    return jnp.where(r >= c, w, 0.0)


def _fwd_mix(x, proj, wout_g, grn, lng, lnb, ws, bsb, mask, qdec, kdec, su):
    cdec = _chunk_decay()
    ids = [W_UP]

    def body(x_ref, p_ref, w_ref, grn_ref, lng_ref, lnb_ref, ws_ref, bsb_ref, m_ref, qd_ref, kd_ref, su_ref,
             x2_ref, cat_ref, o_ref, sp_ref, gup, state, send_sems, recv_sems, local_sems):
        ag = _Gather(ids, [su_ref], [gup], send_sems, recv_sems, local_sems)

        @pl.when(pl.program_id(0) == 0)
        def _():
            meet = _Meet(diagonal=False)
            meet.signal()
            state[...] = jnp.zeros_like(state)
            meet.wait()
            ag.start()

        for h in range(HEADS):
            sl = slice(h * HEAD_DIM, (h + 1) * HEAD_DIM)
            q = p_ref[:, sl]
            k = p_ref[:, RET_W + h * HEAD_DIM:RET_W + (h + 1) * HEAD_DIM]
            v = p_ref[:, 2 * RET_W + h * HEAD_DIM:2 * RET_W + (h + 1) * HEAD_DIM]
            g = p_ref[:, 3 * RET_W + h * HEAD_DIM:3 * RET_W + (h + 1) * HEAD_DIM]
            qb, kb, vb = q.astype(BF16), k.astype(BF16), v.astype(BF16)
            a = _dot_nt(qb, kb) * m_ref[h]
            spb = state[h].astype(BF16)
            sp_ref[0, h] = spb
            o = _dot(a.astype(BF16), vb) + _dot((q * qd_ref[h]).astype(BF16), spb)
            state[h] = state[h] * cdec[h] + _dot_tn((k * kd_ref[h]).astype(BF16), vb)
            o_ref[:, sl] = o
            rinv = lax.rsqrt(jnp.mean(o * o, axis=-1, keepdims=True) + EPS)
            rn = (o * rinv) * grn_ref[:, sl]
            cat_ref[:, sl] = ((g * _sigmoid(g)) * rn).astype(BF16)
        for gi in range(HEADS):
            sl = slice(gi * HEAD_DIM, (gi + 1) * HEAD_DIM)
            u = p_ref[:, 4 * RET_W + gi * HEAD_DIM:4 * RET_W + (gi + 1) * HEAD_DIM]
            sv = p_ref[:, 4 * RET_W + SGU_W + gi * HEAD_DIM:4 * RET_W + SGU_W + (gi + 1) * HEAD_DIM]
            gv = _gelu(sv)
            xc = gv - jnp.mean(gv, axis=-1, keepdims=True)
            vn = (xc * lax.rsqrt(jnp.mean(xc * xc, axis=-1, keepdims=True) + EPS)) * lng_ref[:, sl] + lnb_ref[:, sl]
            mixed = _dot(_causal(ws_ref[gi]).astype(BF16), vn.astype(BF16)) + bsb_ref[gi]
            cat_ref[:, RET_W + gi * HEAD_DIM:RET_W + (gi + 1) * HEAD_DIM] = (_gelu(u) * mixed).astype(BF16)
        x2_ref[...] = x_ref[...] + _dot(cat_ref[...], w_ref[...])

        pl.when(pl.program_id(0) == FWD_MIX_PASS_AT)(ag.near)
        pl.when(pl.program_id(0) == N_CHUNK - 1)(ag.finish)

    ch = lambda w: pl.BlockSpec((CHUNK, w), lambda i: (i, 0))
    hcc = (HEADS, CHUNK, CHUNK)
    hbm = pl.BlockSpec(memory_space=pl.ANY)
    return pl.pallas_call(
        body, name="fwd_mix", grid=(N_CHUNK,),
        out_shape=(jax.ShapeDtypeStruct((SEQ, D_MODEL), F32), jax.ShapeDtypeStruct((SEQ, D_MODEL), BF16),
                   jax.ShapeDtypeStruct((SEQ, RET_W), F32), jax.ShapeDtypeStruct((N_CHUNK, HEADS, HEAD_DIM, HEAD_DIM), BF16))
        + _gathered_shapes(ids),
        in_specs=[ch(D_MODEL), ch(PROJ_W), _resident((D_MODEL, D_MODEL)), _resident((1, RET_W)), _resident((1, SGU_W)),
                  _resident((1, SGU_W)), _resident(hcc), _resident(hcc), _resident(hcc), _resident(hcc), _resident(hcc), hbm],
        out_specs=(ch(D_MODEL), ch(D_MODEL), ch(RET_W), pl.BlockSpec((1, HEADS, HEAD_DIM, HEAD_DIM), lambda i: (i, 0, 0, 0)), hbm),
        scratch_shapes=[pltpu.VMEM((HEADS, HEAD_DIM, HEAD_DIM), F32)] + _gather_scratch(len(ids)),
        compiler_params=_cparams(("arbitrary",), collective=COLLECTIVE["fwd_mix"]),
    )(x, proj, wout_g, grn, lng, lnb, ws, bsb, mask, qdec, kdec, su)


def _conv_taps(p, prev8):
    row = lax.broadcasted_iota(jnp.int32, p.shape, 0)
    p1 = jnp.where(row == 0, prev8[7:8, :], pltpu.roll(p, 1, 0))
    p2 = jnp.where(row == 0, prev8[6:7, :], jnp.where(row == 1, prev8[7:8, :], pltpu.roll(p, 2, 0)))
    return p1, p2


def _fwd_ffn(x2, g2, wup_g, cw_g, cb_g, sdn, gf, tgt):
    ids = [W_DOWN]
    lag, ring_n = FWD_FFN_LAG, FWD_FFN_LAG + 1

    def body(x_ref, g_ref, wu_ref, cw_ref, cb_ref, sdn_ref, gf_ref, t_ref, xd_ref,
             h2_ref, up_ref, u_ref, act_ref, x3_ref, loss_ref, gdn,
             carry, ring, wd_ref, fill_sem, send_sems, recv_sems, local_sems):
        i = pl.program_id(0)
        ag = _Gather(ids, [sdn_ref], [gdn], send_sems, recv_sems, local_sems)

        @pl.when(i == 0)
        def _():
            meet = _Meet(diagonal=False)
            meet.signal()
            carry[...] = jnp.zeros_like(carry)
            meet.wait()
            ag.start()

        pl.when(i == FWD_FFN_PASS_AT)(ag.near)

        @pl.when(i == lag)
        def _():
            ag.finish()
            fill = pltpu.make_async_copy(gdn, wd_ref, fill_sem)
            fill.start()
            fill.wait()

        @pl.when(i < N_TB)
        def _():
            xb = x_ref[...]
            r = lax.rsqrt(jnp.mean(xb * xb, axis=-1, keepdims=True) + EPS)
            h = ((xb * r) * g_ref[...]).astype(BF16)
            h2_ref[...] = h
            for t0, tw in FF_TILES:
                u = []
                for c0 in (t0, D_FF + t0):
                    cs = slice(c0, c0 + tw)
                    p = _dot_nt(h, wu_ref[pl.ds(c0, tw), :])
                    up_ref[:, cs] = p.astype(BF16)
                    p1, p2 = _conv_taps(p, carry[:, cs])
                    carry[:, cs] = p[TM - 8:, :]
                    us = p2 * cw_ref[0:1, cs] + p1 * cw_ref[1:2, cs] + p * cw_ref[2:3, cs] + cb_ref[:, cs]
                    u_ref[:, cs] = us.astype(BF16)
                    u.append(us)
                a = ((u[0] * _sigmoid(u[0])) * u[1]).astype(BF16)
                act_ref[:, t0:t0 + tw] = a
                ring[i % ring_n, :, t0:t0 + tw] = a

        @pl.when(i >= lag)
        def _():
            acc = xd_ref[...]
            for t0, tw in FF_TILES:
                acc = acc + _dot(ring[(i - lag) % ring_n, :, t0:t0 + tw], wd_ref[pl.ds(t0, tw), :])
            x3_ref[...] = acc
            r3 = lax.rsqrt(jnp.mean(acc * acc, axis=-1, keepdims=True) + EPS)
            diff = (acc * r3) * gf_ref[...] - t_ref[...]
            loss_ref[...] = jnp.full(loss_ref.shape, 0.5 * jnp.sum(jnp.mean(diff * diff, axis=-1)), F32)

    lead = lambda w: pl.BlockSpec((TM, w), lambda i: (jnp.minimum(i, N_TB - 1), 0))
    late = lambda w: pl.BlockSpec((TM, w), lambda i: (jnp.maximum(i - lag, 0), 0))
    vm = pl.BlockSpec(memory_space=pltpu.VMEM)
    return pl.pallas_call(
        body, name="fwd_ffn", grid=(N_TB + lag,),
        out_shape=(jax.ShapeDtypeStruct((SEQ, D_MODEL), BF16), jax.ShapeDtypeStruct((SEQ, 2 * D_FF), BF16),
                   jax.ShapeDtypeStruct((SEQ, 2 * D_FF), BF16),
                   jax.ShapeDtypeStruct((SEQ, D_FF), BF16), jax.ShapeDtypeStruct((SEQ, D_MODEL), F32),
                   jax.ShapeDtypeStruct((N_TB, 8, 128), F32)) + _gathered_shapes(ids),
        in_specs=[lead(D_MODEL), _resident((1, D_MODEL)), _resident((2 * D_FF, D_MODEL)), _resident((8, 2 * D_FF)),
                  _resident((1, 2 * D_FF)), vm, _resident((1, D_MODEL)), late(D_MODEL), late(D_MODEL)],
        out_specs=(lead(D_MODEL), lead(2 * D_FF), lead(2 * D_FF), lead(D_FF), late(D_MODEL),
                   pl.BlockSpec((1, 8, 128), lambda i: (jnp.maximum(i - lag, 0), 0, 0)), pl.BlockSpec(memory_space=pl.ANY)),
        scratch_shapes=[pltpu.VMEM((8, 2 * D_FF), F32), pltpu.VMEM((ring_n, TM, D_FF), BF16), pltpu.VMEM((D_FF, D_MODEL), BF16),
                        pltpu.SemaphoreType.DMA] + _gather_scratch(len(ids)),
        compiler_params=_cparams(("arbitrary",), collective=COLLECTIVE["fwd_ffn"]),
    )(x2, g2, wup_g, cw_g, cb_g, sdn, gf, tgt, x2)


def _bwd_ffn(x3, tgt, gf, x2, g2, up_pre, u_conv, wup_g, cw_g, wdn_g):
    def body(x3_ref, t_ref, gf_ref, x2_ref, g2_ref, up_ref, u_ref, wu_ref, cw_ref, wd_ref,
             dx3_ref, dpre_ref, dx2_ref, dgf_ref, dg2_ref, dcv_ref, nxt):
        i = pl.program_id(0)

        @pl.when(i == 0)
        def _():
            nxt[...] = jnp.zeros_like(nxt)
            dgf_ref[...] = jnp.zeros_like(dgf_ref)
            dg2_ref[...] = jnp.zeros_like(dg2_ref)
            dcv_ref[...] = jnp.zeros_like(dcv_ref)

        x3 = x3_ref[...]
        r3 = lax.rsqrt(jnp.mean(x3 * x3, axis=-1, keepdims=True) + EPS)
        xh3 = x3 * r3
        dy = (xh3 * gf_ref[...] - t_ref[...]) * (1.0 / D_MODEL)
        dgf_ref[0:1, :] += jnp.sum(dy * xh3, axis=0, keepdims=True)
        t3 = dy * gf_ref[...]
        dx3 = r3 * (t3 - xh3 * jnp.mean(t3 * xh3, axis=-1, keepdims=True))
        dx3b = dx3.astype(BF16)
        dx3_ref[...] = dx3b
        dh2 = jnp.zeros((TM, D_MODEL), F32)
        for t0, tw in FF_TILES:
            row = lax.broadcasted_iota(jnp.int32, (TM, tw), 0)
            ts = slice(t0, t0 + tw)
            dact = _dot_nt(dx3b, wd_ref[pl.ds(t0, tw), :])
            ua = u_ref[:, ts].astype(F32)
            ub = u_ref[:, D_FF + t0:D_FF + t0 + tw].astype(F32)
            sg = _sigmoid(ua)
            du = [dact * ub * (sg * (1.0 + ua * (1.0 - sg))), dact * (ua * sg)]
            for n in range(2):
                d = du[n]
                c0 = n * D_FF + t0
                cs = slice(c0, c0 + tw)
                nx = nxt[:, cs]
                n1 = jnp.where(row == TM - 1, nx[0:1, :], pltpu.roll(d, TM - 1, 0))
                n2 = jnp.where(row == TM - 2, nx[0:1, :], jnp.where(row == TM - 1, nx[1:2, :], pltpu.roll(d, TM - 2, 0)))
                nxt[:, cs] = d[0:8, :]
                dp = (d * cw_ref[2:3, cs] + n1 * cw_ref[1:2, cs] + n2 * cw_ref[0:1, cs]).astype(BF16)
                dpre_ref[:, cs] = dp
                p = up_ref[:, cs].astype(F32)
                dcv_ref[n, 0:1, ts] += jnp.sum(n2 * p, axis=0, keepdims=True)
                dcv_ref[n, 1:2, ts] += jnp.sum(n1 * p, axis=0, keepdims=True)
                dcv_ref[n, 2:3, ts] += jnp.sum(d * p, axis=0, keepdims=True)
                dcv_ref[n, 3:4, ts] += jnp.sum(d, axis=0, keepdims=True)
                dh2 = dh2 + _dot(dp, wu_ref[pl.ds(c0, tw), :])
        x2 = x2_ref[...]
        r2 = lax.rsqrt(jnp.mean(x2 * x2, axis=-1, keepdims=True) + EPS)
        xh2 = x2 * r2
        dg2_ref[0:1, :] += jnp.sum(dh2 * xh2, axis=0, keepdims=True)
        t2 = dh2 * g2_ref[...]
        dx2_ref[...] = dx3 + r2 * (t2 - xh2 * jnp.mean(t2 * xh2, axis=-1, keepdims=True))

    rev = lambda w: pl.BlockSpec((TM, w), lambda i: (N_TB - 1 - i, 0))
    acc = lambda s: pl.BlockSpec(s, lambda i: (0,) * len(s))
    return pl.pallas_call(
        body, name="bwd_ffn", grid=(N_TB,),
        out_shape=(jax.ShapeDtypeStruct((SEQ, D_MODEL), BF16), jax.ShapeDtypeStruct((SEQ, 2 * D_FF), BF16),
                   jax.ShapeDtypeStruct((SEQ, D_MODEL), F32), jax.ShapeDtypeStruct((8, D_MODEL), F32),
                   jax.ShapeDtypeStruct((8, D_MODEL), F32), jax.ShapeDtypeStruct((2, 8, D_FF), F32)),
        in_specs=[rev(D_MODEL), rev(D_MODEL), _resident((1, D_MODEL)), rev(D_MODEL), _resident((1, D_MODEL)), rev(2 * D_FF),
                  rev(2 * D_FF), _resident((2 * D_FF, D_MODEL)), _resident((8, 2 * D_FF)), _resident((D_FF, D_MODEL))],
        out_specs=(rev(D_MODEL), rev(2 * D_FF), rev(D_MODEL), acc((8, D_MODEL)), acc((8, D_MODEL)), acc((2, 8, D_FF))),
        scratch_shapes=[pltpu.VMEM((8, 2 * D_FF), F32)],
        compiler_params=_cparams(("arbitrary",)),
    )(x3, tgt, gf, x2, g2, up_pre, u_conv, wup_g, cw_g, wdn_g)


def _bwd_mix(dx2, proj, o, sprev, wout_g, grn, lng, lnb, ws, bsb, mask, qdec, kdec, cos2, sin2, hosted):
    cdec = _chunk_decay()
    geoms = [g for g, _ in hosted]
    n_h = len(hosted)

    def body(dx2_ref, p_ref, o_ref, sp_ref, w_ref, grn_ref, lng_ref, lnb_ref, ws_ref, bsb_ref, m_ref, qd_ref, kd_ref,
             cos_ref, sin_ref, *rest):
        dp_ref, dgrn_ref, dlng_ref, dlnb_ref, dws_ref, dbs_ref = rest[n_h:n_h + 6]
        dstate, dbs_acc = rest[2 * n_h + 6:2 * n_h + 8]
        i = pl.program_id(0)
        rs = _Scatters(geoms, rest[:n_h], rest[n_h + 6:2 * n_h + 6], rest[2 * n_h + 8:])
        pl.when(i == 0)(rs.phase1)
        pl.when(i == 3)(rs.phase2)
        pl.when(i == 8)(rs.phase2b)

        @pl.when(i == 0)
        def _():
            dstate[...] = jnp.zeros_like(dstate)
            dgrn_ref[...] = jnp.zeros_like(dgrn_ref)
            dlng_ref[...] = jnp.zeros_like(dlng_ref)
            dlnb_ref[...] = jnp.zeros_like(dlnb_ref)
            dws_ref[...] = jnp.zeros_like(dws_ref)
            dbs_ref[...] = jnp.zeros_like(dbs_ref)
            dbs_acc[...] = jnp.zeros_like(dbs_acc)

        dmix = _dot_nt(dx2_ref[...].astype(BF16), w_ref[...])
        for h in range(HEADS):
            sl = slice(h * HEAD_DIM, (h + 1) * HEAD_DIM)
            q = p_ref[:, sl]
            k = p_ref[:, RET_W + h * HEAD_DIM:RET_W + (h + 1) * HEAD_DIM]
            v = p_ref[:, 2 * RET_W + h * HEAD_DIM:2 * RET_W + (h + 1) * HEAD_DIM]
            g = p_ref[:, 3 * RET_W + h * HEAD_DIM:3 * RET_W + (h + 1) * HEAD_DIM]
            o = o_ref[:, sl]
            rinv = lax.rsqrt(jnp.mean(o * o, axis=-1, keepdims=True) + EPS)
            oh = o * rinv
            gr = grn_ref[:, sl]
            sg = _sigmoid(g)
            dret = dmix[:, sl]
            dp_ref[:, 3 * RET_W + h * HEAD_DIM:3 * RET_W + (h + 1) * HEAD_DIM] = (
                dret * (oh * gr) * (sg * (1.0 + g * (1.0 - sg)))).astype(BF16)
            drn = dret * (g * sg)
            dgrn_ref[0:1, sl] += jnp.sum(drn * oh, axis=0, keepdims=True)
            t = drn * gr
            do = rinv * (t - oh * jnp.mean(t * oh, axis=-1, keepdims=True))
            qb, kb, vb, dob = q.astype(BF16), k.astype(BF16), v.astype(BF16), do.astype(BF16)
            m = m_ref[h]
            ab = (_dot_nt(qb, kb) * m).astype(BF16)
            dab = (_dot_nt(dob, vb) * m).astype(BF16)
            spb = sp_ref[0, h]
            dsn = dstate[h]
            dsnb = dsn.astype(BF16)
            qdb = (q * qd_ref[h]).astype(BF16)
            kdb = (k * kd_ref[h]).astype(BF16)
            dq = _dot(dab, kb) + _dot_nt(dob, spb) * qd_ref[h]
            dk = _dot_tn(dab, qb) + _dot_nt(vb, dsnb) * kd_ref[h]
            dv = _dot_tn(ab, dob) + _dot(kdb, dsnb)
            dstate[h] = dsn * cdec[h] + _dot_tn(qdb, dob)
            c2, s2 = cos_ref[...], sin_ref[...]
            dp_ref[:, sl] = _rot_t(dq, c2, s2).astype(BF16)
            dp_ref[:, RET_W + h * HEAD_DIM:RET_W + (h + 1) * HEAD_DIM] = _rot_t(dk * K_SCALE, c2, s2).astype(BF16)
            dp_ref[:, 2 * RET_W + h * HEAD_DIM:2 * RET_W + (h + 1) * HEAD_DIM] = dv.astype(BF16)
        for gi in range(HEADS):
            sl = slice(gi * HEAD_DIM, (gi + 1) * HEAD_DIM)
            u = p_ref[:, 4 * RET_W + gi * HEAD_DIM:4 * RET_W + (gi + 1) * HEAD_DIM]
            sv = p_ref[:, 4 * RET_W + SGU_W + gi * HEAD_DIM:4 * RET_W + SGU_W + (gi + 1) * HEAD_DIM]
            gv = _gelu(sv)
            xc = gv - jnp.mean(gv, axis=-1, keepdims=True)
            rstd = lax.rsqrt(jnp.mean(xc * xc, axis=-1, keepdims=True) + EPS)
            xh = xc * rstd
            lg = lng_ref[:, sl]
            vnb = (xh * lg + lnb_ref[:, sl]).astype(BF16)
            wcb = _causal(ws_ref[gi]).astype(BF16)
            mixed = _dot(wcb, vnb) + bsb_ref[gi]
            dsgu = dmix[:, RET_W + gi * HEAD_DIM:RET_W + (gi + 1) * HEAD_DIM]
            dmixed = dsgu * _gelu(u)
            dmb = dmixed.astype(BF16)
            dws_ref[gi] += _causal(_dot_nt(dmb, vnb))
            dbs_acc[gi] += dmixed
            dvn = _dot_tn(wcb, dmb)
            dlng_ref[gi:gi + 1, :] += jnp.sum(dvn * xh, axis=0, keepdims=True)
            dlnb_ref[gi:gi + 1, :] += jnp.sum(dvn, axis=0, keepdims=True)
            dxh = dvn * lg
            dgv = rstd * (dxh - jnp.mean(dxh, axis=-1, keepdims=True) - xh * jnp.mean(dxh * xh, axis=-1, keepdims=True))
            dp_ref[:, 4 * RET_W + gi * HEAD_DIM:4 * RET_W + (gi + 1) * HEAD_DIM] = (dsgu * mixed * _gelu_grad(u)).astype(BF16)
            dp_ref[:, 4 * RET_W + SGU_W + gi * HEAD_DIM:4 * RET_W + SGU_W + (gi + 1) * HEAD_DIM] = (
                dgv * _gelu_grad(sv)).astype(BF16)

        @pl.when(i == N_CHUNK - 1)
        def _():
            for gi in range(HEADS):
                col = jnp.broadcast_to(jnp.sum(dbs_acc[gi], axis=-1, keepdims=True), (CHUNK, CHUNK))
                dbs_ref[gi:gi + 1, :] = jnp.transpose(col)[0:1, :]
            rs.phase3()

    rev = lambda w: pl.BlockSpec((CHUNK, w), lambda i: (N_CHUNK - 1 - i, 0))
    hcc = (HEADS, CHUNK, CHUNK)
    acc = lambda s: pl.BlockSpec(s, lambda i: (0,) * len(s))
    res = pl.pallas_call(
        body, name="bwd_mix", grid=(N_CHUNK,),
        out_shape=(jax.ShapeDtypeStruct((SEQ, PROJ_W), BF16), jax.ShapeDtypeStruct((8, RET_W), F32),
                   jax.ShapeDtypeStruct((8, HEAD_DIM), F32), jax.ShapeDtypeStruct((8, HEAD_DIM), F32),
                   jax.ShapeDtypeStruct(hcc, F32), jax.ShapeDtypeStruct((8, CHUNK), F32)) + _scatter_out_shapes(geoms),
        in_specs=[rev(D_MODEL), rev(PROJ_W), rev(RET_W),
                  pl.BlockSpec((1, HEADS, HEAD_DIM, HEAD_DIM), lambda i: (N_CHUNK - 1 - i, 0, 0, 0)),
                  _resident((D_MODEL, D_MODEL)), _resident((1, RET_W)), _resident((1, SGU_W)), _resident((1, SGU_W)),
                  _resident(hcc), _resident(hcc), _resident(hcc), _resident(hcc), _resident(hcc), rev(HEAD_DIM), rev(HEAD_DIM)]
        + [pl.BlockSpec(memory_space=pl.ANY)] * n_h,
        out_specs=(rev(PROJ_W), acc((8, RET_W)), acc((8, HEAD_DIM)), acc((8, HEAD_DIM)), acc(hcc), acc((8, CHUNK)))
        + _scatter_out_specs(geoms),
        scratch_shapes=[pltpu.VMEM((HEADS, HEAD_DIM, HEAD_DIM), F32), pltpu.VMEM((HEADS, CHUNK, CHUNK), F32)] + _scatter_scratch(geoms),
        compiler_params=_cparams(("arbitrary",), collective=COLLECTIVE["bwd_mix"]),
    )(dx2, proj, o, sprev, wout_g, grn, lng, lnb, ws, bsb, mask, qdec, kdec, cos2, sin2, *[p for _, p in hosted])
    return tuple(res[:6 + n_h])


def _bwd_proj(dproj, win_g, x, g1, dx2, gin_p, small):
    geoms = [W_IN]
    n_s = len(small)

    def body(dp_ref, w_ref, x_ref, g_ref, dx2_ref, gin_ref, *rest):
        small_refs = rest[:n_s]
        dx_ref, rs_out, rp_ref, rws_ref, rcv_ref, dg_ref = rest[n_s:n_s + 6]
        rs_scratch = rest[n_s + 6:n_s + 6 + N_SCATTER_SCRATCH]
        ar_scratch = rest[n_s + 6 + N_SCATTER_SCRATCH:]
        ar_res = ar_scratch[N_SMALL_SCRATCH:]
        ar = _SmallReduce((dg_ref,) + tuple(small_refs), ar_res, ar_scratch[:N_SMALL_SCRATCH])
        rs = _Scatters(geoms, [gin_ref], [rs_out], rs_scratch)
        pl.when(pl.program_id(0) == 0)(lambda: rs.phase1(diagonal=True))
        pl.when(pl.program_id(0) == 1)(rs.phase2)
        pl.when(pl.program_id(0) == 5)(rs.phase2b)

        @pl.when(pl.program_id(0) == 0)
        def _():
            dg_ref[...] = jnp.zeros_like(dg_ref)

        dh = _dot_nt(dp_ref[...], w_ref[...])
        xb = x_ref[...]
        r = lax.rsqrt(jnp.mean(xb * xb, axis=-1, keepdims=True) + EPS)
        xh = xb * r
        dg_ref[0:1, :] += jnp.sum(dh * xh, axis=0, keepdims=True)
        t = dh * g_ref[...]
        dx_ref[...] = dx2_ref[...] + r * (t - xh * jnp.mean(t * xh, axis=-1, keepdims=True))

        @pl.when(pl.program_id(0) == N_TB - 1)
        def _():
            ar.begin()
            rs.phase3()
            ar.end()
            for o_ref, r_ref in zip((rp_ref, rws_ref, rcv_ref), ar_res):
                o_ref[...] = r_ref[...]

    tok = lambda w: pl.BlockSpec((TM, w), lambda i: (i, 0))
    vm = pl.BlockSpec(memory_space=pltpu.VMEM)
    res = pl.pallas_call(
        body, name="bwd_proj", grid=(N_TB,),
        out_shape=(jax.ShapeDtypeStruct((SEQ, D_MODEL), F32),) + _scatter_out_shapes(geoms)
        + tuple(jax.ShapeDtypeStruct(s, F32) for s in SMALL_FULL),
        in_specs=[tok(PROJ_W), _resident((D_MODEL, PROJ_W)), tok(D_MODEL), _resident((1, D_MODEL)), tok(D_MODEL),
                  pl.BlockSpec(memory_space=pl.ANY)] + [vm] * n_s,
        out_specs=(tok(D_MODEL),) + _scatter_out_specs(geoms) + (vm,) * len(SMALL_FULL),
        scratch_shapes=[pltpu.VMEM((8, D_MODEL), F32)] + _scatter_scratch(geoms) + _small_scratch()
        + [pltpu.VMEM(s, F32) for s in SMALL_FULL],
        compiler_params=_cparams(("arbitrary",), collective=COLLECTIVE["bwd_proj"]),
    )(dproj, win_g, x, g1, dx2, gin_p, *small)
    return res


def _wgrad(name, a, b, tm=None, tn=None, hosted=()):
    m_w, n_w = a.shape[-1], b.shape[-1]
    tm = m_w if tm is None else tm
    tn = n_w if tn is None else tn
    n_steps = (m_w // tm) * (n_w // tn)
    geoms = [g for g, _ in hosted]
    n_h = len(hosted)

    def body(a_ref, b_ref, *rest):
        o_ref = rest[n_h]
        if n_h:
            rs = _Scatters(geoms, rest[:n_h], rest[n_h + 1:2 * n_h + 1], rest[2 * n_h + 1:])
            step = pl.program_id(0) * (n_w // tn) + pl.program_id(1)
            pl.when(step == 0)(rs.phase1)
            pl.when(step == 1)(rs.phase2)
            pl.when(step == n_steps // 2)(rs.phase2b)
        o_ref[...] = _dot_tn(a_ref[...].astype(BF16), b_ref[...].astype(BF16)).astype(BF16)
        if n_h:
            pl.when(step == n_steps - 1)(rs.phase3)

    assert not n_h or n_steps >= 4
    res = pl.pallas_call(
        body, name=name, grid=(m_w // tm, n_w // tn),
        out_shape=(jax.ShapeDtypeStruct((m_w, n_w), BF16),) + _scatter_out_shapes(geoms),
        in_specs=[pl.BlockSpec((SEQ, tm), lambda i, j: (0, i)), pl.BlockSpec((SEQ, tn), lambda i, j: (0, j))]
        + [pl.BlockSpec(memory_space=pl.ANY)] * n_h,
        out_specs=(pl.BlockSpec((tm, tn), lambda i, j: (i, j)),) + _scatter_out_specs(geoms),
        scratch_shapes=_scatter_scratch(geoms),
        compiler_params=_cparams(("arbitrary", "arbitrary"), collective=COLLECTIVE[name]) if n_h else _cparams(("parallel", "parallel")),
    )(a, b, *[p for _, p in hosted])
    return tuple(res[:1 + n_h])


def _row_step(half_rows):
    return max(s for s in range(16, 177, 16) if half_rows % s == 0)


class _Scatter:
    def __init__(self, geom, partial, out, land1, mine, stage2, land2, comb, s1_send, s1_recv, s2_send, s2_recv, ld_sems):
        self.w, self.row0, self.shape = _geom(geom)
        self.partial, self.out, self.land1 = partial, out, land1
        self.mine, self.stage2, self.land2, self.comb = mine, stage2, land2, comb
        self.hr = self.shape[0] // 2
        self.step = _row_step(self.hr)
        self.s1_send, self.s1_recv, self.s2_send, self.s2_recv, self.ld_sems = s1_send, s1_recv, s2_send, s2_recv, ld_sems
        self.x, self.y, self.c = lax.axis_index("x"), lax.axis_index("y"), lax.axis_index("c")
        self.sibling = (self.x, self.y, 1 - self.c)
        self.chips = [(self.x, self.y), (1 - self.x, self.y), (self.x, 1 - self.y), (1 - self.x, 1 - self.y)]

    def block(self, px, py, pc):
        dev = 4 * px + 2 * py + pc
        if self.w == W_IN:
            return self.partial.at[:, pl.ds(pl.multiple_of(dev * IN_SHARD, 128), IN_SHARD)]
        if self.w == W_OUT:
            return self.partial.at[pl.ds(pl.multiple_of(dev * OUT_SHARD, 128), OUT_SHARD), :]
        if self.w == W_DOWN:
            return self.partial.at[pl.ds(pl.multiple_of(dev * DOWN_SHARD, 32), DOWN_SHARD), :]
        return self.partial.at[pl.ds(pl.multiple_of(dev * FF_SHARD + self.row0, 32), self.shape[0]), :]

    def copy1(self, k):
        return pltpu.make_async_remote_copy(
            src_ref=self.block(*self.chips[k], 1 - self.c), dst_ref=self.land1.at[k],
            send_sem=self.s1_send.at[k], recv_sem=self.s1_recv.at[k], device_id=self.sibling, device_id_type=MESH)

    STAGE2 = [(1, 0, 1), (3, 0, 1), (2, 1, 2), (3, 1, 2), (1, 1, 1), (2, 0, 2)]

    def copy2(self, j):
        blk, h, to = self.STAGE2[j]
        src = self.comb.at[j - 4] if j >= 4 else self.stage2.at[blk - 1, pl.ds(h * self.hr, self.hr), :]
        return pltpu.make_async_remote_copy(
            src_ref=src, dst_ref=self.land2.at[j], send_sem=self.s2_send.at[j], recv_sem=self.s2_recv.at[j],
            device_id=(*self.chips[to], self.c), device_id_type=MESH)

    def _rows(self, h=None):
        step = self.step
        lo, n = (0, self.shape[0]) if h is None else (h * self.hr, self.hr)
        return [pl.ds(r0, step) for r0 in range(lo, lo + n, step)]

    def load(self, k):
        return pltpu.make_async_copy(self.block(*self.chips[k], self.c), self.mine.at[k], self.ld_sems.at[k])

    def load_mine(self):
        for k in range(4):
            self.load(k).start()

    def phase1(self):
        for k in range(4):
            self.copy1(k).start()

    def phase2(self, k):
        self.copy1(k).wait_recv()
        self.load(k).wait()
        for rs in self._rows():
            s = self.mine[k, rs, :].astype(F32) + self.land1[k, rs, :].astype(F32)
            if k == 0:
                self.out[rs, :] = s
            else:
                self.stage2[k - 1, rs, :] = s.astype(BF16)
        for j in {3: (1, 3), 1: (0,), 2: (2,), 0: ()}[k]:
            self.copy2(j).start()

    def phase2b(self):
        for j, got in ((4, 3), (5, 1)):
            blk, h, _ = self.STAGE2[j]
            self.copy2(got).wait_recv()
            for i, rs in enumerate(self._rows(h)):
                lr = pl.ds(i * self.step, self.step)
                self.comb[j - 4, lr, :] = (self.stage2[blk - 1, rs, :].astype(F32) + self.land2[got, lr, :].astype(F32)).astype(BF16)
            self.copy2(j).start()

    def phase3(self):
        for j in (0, 5, 4, 2):
            self.copy2(j).wait_recv()
        for h, (first, second) in enumerate(((0, 5), (4, 2))):
            for i, rs in enumerate(self._rows(h)):
                lr = pl.ds(i * self.step, self.step)
                self.out[rs, :] = (self.out[rs, :] + self.land2[first, lr, :].astype(F32)) + self.land2[second, lr, :].astype(F32)
        for k in range(4):
            self.copy1(k).wait_send()
        for j in range(6):
            self.copy2(j).wait_send()


def _geom(geom):
    if isinstance(geom, tuple):
        w, row0, rows = geom
        assert w == W_UP
        return w, row0, (rows, SHARD[w][1])
    return geom, 0, SHARD[geom]


N_SCATTER_SCRATCH = 10


def _scatter_out_shapes(geoms):
    return tuple(jax.ShapeDtypeStruct(_geom(g)[2], F32) for g in geoms)


def _scatter_out_specs(geoms):
    return (pl.BlockSpec(memory_space=pltpu.VMEM),) * len(geoms)


def _scatter_scratch(geoms):
    out = []
    for g in geoms:
        s = _geom(g)[2]
        hs = (s[0] // 2, s[1])
        out += [pltpu.VMEM((4,) + s, BF16), pltpu.VMEM((4,) + s, BF16), pltpu.VMEM((3,) + s, BF16), pltpu.VMEM((6,) + hs, BF16),
                pltpu.VMEM((2,) + hs, BF16),
                pltpu.SemaphoreType.DMA((4,)), pltpu.SemaphoreType.DMA((4,)), pltpu.SemaphoreType.DMA((6,)),
                pltpu.SemaphoreType.DMA((6,)), pltpu.SemaphoreType.DMA((4,))]
    return out


class _Scatters:
    def __init__(self, geoms, p_refs, out_refs, scratch):
        k = N_SCATTER_SCRATCH
        self.items = [_Scatter(g, p_refs[i], out_refs[i], *scratch[k * i:k * i + k]) for i, g in enumerate(geoms)]

    def phase1(self, diagonal=False):
        meet = _Meet(diagonal)
        meet.signal()
        for s in self.items:
            s.load_mine()
        meet.wait()
        for s in self.items:
            s.phase1()

    def phase2(self):
        for k in (3, 1, 2, 0):
            for s in self.items:
                s.phase2(k)

    def phase2b(self):
        for s in self.items:
            s.phase2b()

    def phase3(self):
        for s in self.items:
            s.phase3()


PACK_W = 1024


SMALL_FULL = [(2, 8, PACK_W), (HEADS, CHUNK, CHUNK), (2, 8, D_FF)]
SMALL_HALF = [(s[0] // 2,) + s[1:] for s in SMALL_FULL]
N_SMALL_SCRATCH = 16


def _small_scratch():
    n_a = len(SMALL_FULL)
    return ([pltpu.VMEM(SMALL_FULL[0], F32)] + [pltpu.VMEM(s, F32) for s in SMALL_HALF] + [pltpu.VMEM(s, F32) for s in SMALL_HALF]
            + [pltpu.VMEM((3,) + s, F32) for s in SMALL_HALF]
            + [pltpu.SemaphoreType.DMA((n_a,)), pltpu.SemaphoreType.DMA((n_a,)), pltpu.SemaphoreType.DMA((n_a, 3)),
               pltpu.SemaphoreType.DMA((n_a, 3)), pltpu.SemaphoreType.DMA((n_a,)), pltpu.SemaphoreType.DMA((n_a,))])


class _SmallReduce:
    def __init__(self, ins, outs, scratch):
        self.ins, self.outs = ins, outs
        (self.pack, *rest) = scratch
        self.rxs, self.css, self.gs = rest[0:3], rest[3:6], rest[6:9]
        self.s1_send, self.s1_recv, self.s2_send, self.s2_recv, self.s3_send, self.s3_recv = rest[9:]
        self.x, self.y, self.c = lax.axis_index("x"), lax.axis_index("y"), lax.axis_index("c")
        self.sibling = (self.x, self.y, 1 - self.c)
        self.chips = [(1 - self.x, self.y), (self.x, 1 - self.y), (1 - self.x, 1 - self.y)]
        self.hl = [s[0] for s in SMALL_HALF]

    def half(self, ref, a, h):
        return ref.at[pl.ds(h * self.hl[a], self.hl[a])]

    def begin(self):
        dg1_ref, dg2_ref, dgf_ref, dgrn_ref, dlng_ref, dlnb_ref, dbs_ref, loss_ref, dws_ref, dcv_ref = self.ins
        pack, c = self.pack, self.c
        pack[...] = jnp.zeros_like(pack)
        pack[0, 0:1, :] = dg1_ref[0:1, :]
        pack[0, 1:2, :] = dg2_ref[0:1, :]
        pack[0, 2:3, :] = dgf_ref[0:1, :]
        pack[0, 3:4, 0:RET_W] = dgrn_ref[0:1, :]
        lsum = loss_ref[0, 0:1, :]
        for i in range(1, N_TB):
            lsum = lsum + loss_ref[i, 0:1, :]
        pack[0, 3:4, RET_W:RET_W + 128] = lsum
        pack[1, 0:HEADS, 0:128] = dlng_ref[0:HEADS, :]
        pack[1, 0:HEADS, 128:256] = dlnb_ref[0:HEADS, :]
        pack[1, 0:HEADS, 256:384] = dbs_ref[0:HEADS, :]
        self.srcs = [pack, dws_ref, dcv_ref]
        n_a = len(self.srcs)
        self.ex1 = [pltpu.make_async_remote_copy(src_ref=self.half(self.srcs[a], a, 1 - c), dst_ref=self.rxs[a],
                                                 send_sem=self.s1_send.at[a], recv_sem=self.s1_recv.at[a],
                                                 device_id=self.sibling, device_id_type=MESH) for a in range(n_a)]
        for cp in self.ex1:
            cp.start()
        self.ex2 = []
        for a in range(n_a):
            self.ex1[a].wait_recv()
            self.css[a][...] = self.half(self.srcs[a], a, c)[...] + self.rxs[a][...]
            for j, chip in enumerate(self.chips):
                cp = pltpu.make_async_remote_copy(src_ref=self.css[a], dst_ref=self.gs[a].at[j], send_sem=self.s2_send.at[a, j],
                                                  recv_sem=self.s2_recv.at[a, j], device_id=(*chip, c), device_id_type=MESH)
                cp.start()
                self.ex2.append(cp)

    def end(self):
        c, x, y = self.c, self.x, self.y
        ex3 = []
        for a in range(len(self.srcs)):
            css, gs, out = self.css[a], self.gs[a], self.outs[a]
            for j in range(3):
                self.ex2[3 * a + j].wait_recv()
            tot = None
            for q in range(4):
                k = jnp.where(x != (q >> 1), 1, 0) + jnp.where(y != (q & 1), 2, 0)
                term = jnp.where(k == 0, css[...], jnp.where(k == 1, gs[0], jnp.where(k == 2, gs[1], gs[2])))
                tot = term if tot is None else tot + term
            self.half(out, a, c)[...] = tot
            cp = pltpu.make_async_remote_copy(src_ref=self.half(out, a, c), dst_ref=self.half(out, a, c), send_sem=self.s3_send.at[a],
                                              recv_sem=self.s3_recv.at[a], device_id=self.sibling, device_id_type=MESH)
            cp.start()
            ex3.append(cp)
        for a in range(len(self.srcs)):
            out = self.outs[a]
            pltpu.make_async_remote_copy(src_ref=self.half(out, a, 1 - c), dst_ref=self.half(out, a, 1 - c), send_sem=self.s3_send.at[a],
                                         recv_sem=self.s3_recv.at[a], device_id=self.sibling, device_id_type=MESH).wait_recv()
        for cp in self.ex1 + self.ex2 + ex3:
            cp.wait_send()


def _adam_math(w, g, m, v):
    nm = ADAM_B1 * m + (1.0 - ADAM_B1) * g
    nv = ADAM_B2 * v + (1.0 - ADAM_B2) * (g * g)
    d = -ADAM_LR * ((nm / (1.0 - ADAM_B1 ** ADAM_STEP)) / (jnp.sqrt(nv / (1.0 - ADAM_B2 ** ADAM_STEP)) + ADAM_EPS) + ADAM_WD * w)
    return d, nm, nv


def _adamw(name, w, gs, m, v, rows, thru=()):
    _, r, cdim = w.shape
    n_steps = r // rows
    half = gs[0].shape[0] // rows
    n_g, n_t = len(gs), len(thru)

    def body(w_ref, *rest):
        g_refs, (m_ref, v_ref), t_refs = rest[:n_g], rest[n_g:n_g + 2], rest[n_g + 2:n_g + 2 + n_t]
        go_ref, d_ref, nm_ref, nv_ref, *to_refs = rest[n_g + 2 + n_t:]
        gg = g_refs[0][...]
        if n_g == 2:
            gg = jnp.where(pl.program_id(0) < half, gg, g_refs[1][...])
        go_ref[0] = gg
        d, nm, nv = _adam_math(w_ref[0], gg, m_ref[0], v_ref[0])
        d_ref[0], nm_ref[0], nv_ref[0] = d, nm, nv
        for t_ref, to_ref in zip(t_refs, to_refs):
            to_ref[...] = t_ref[...]

    spec3 = pl.BlockSpec((1, rows, cdim), lambda i: (0, i, 0))
    if n_g == 1:
        g_specs = [pl.BlockSpec((rows, cdim), lambda i: (i, 0))]
    else:
        g_specs = [pl.BlockSpec((rows, cdim), lambda i: (jnp.minimum(i, half - 1), 0)),
                   pl.BlockSpec((rows, cdim), lambda i: (jnp.maximum(i - half, 0), 0))]
    t_specs = [pl.BlockSpec((t.shape[0] // n_steps, t.shape[1]), lambda i: (i, 0)) for t in thru]
    sh = jax.ShapeDtypeStruct((1, r, cdim), F32)
    return pl.pallas_call(
        body, name=name, grid=(n_steps,), out_shape=(sh, sh, sh, sh) + tuple(jax.ShapeDtypeStruct(t.shape, t.dtype) for t in thru),
        in_specs=[spec3] + g_specs + [spec3, spec3] + t_specs, out_specs=(spec3,) * 4 + tuple(t_specs),
        compiler_params=_cparams(("parallel",)),
    )(w, *gs, m, v, *thru)


def _adamw_small(rp, rws, rcv, gcw, params):
    n_p = len(params)

    def body(*refs):
        rp_ref, rws_ref, rcv_ref, gcw_ref = refs[:4]
        ins = refs[4:4 + 3 * n_p]
        outs = refs[4 + 3 * n_p:]
        outs[4 * n_p][...] = rp_ref[0, 3:4, RET_W:RET_W + 1]
        grads = [rp_ref[0, 0:1, :], rp_ref[0, 1:2, :], rp_ref[0, 2:3, :], rp_ref[0, 3:4, 0:RET_W],
                 rp_ref[1, 0:HEADS, 0:128], rp_ref[1, 0:HEADS, 128:256], rp_ref[1, 0:HEADS, 256:384],
                 rws_ref[...], gcw_ref[...], None]
        for p in range(n_p):
            w_ref, m_ref, v_ref = ins[3 * p:3 * p + 3]
            o = outs[4 * p:4 * p + 4]
            if p == n_p - 1:
                for hf in range(2):
                    cs = slice(hf * D_FF, (hf + 1) * D_FF)
                    g = rcv_ref[hf, 3:4, :]
                    res = (g,) + _adam_math(w_ref[:, cs], g, m_ref[:, cs], v_ref[:, cs])
                    for t in range(4):
                        o[t][:, cs] = res[t]
                continue
            lead = w_ref.ndim > grads[p].ndim
            rd = (lambda r: r[0]) if lead else (lambda r: r[...])
            res = (grads[p],) + _adam_math(rd(w_ref), grads[p], rd(m_ref), rd(v_ref))
            for t in range(4):
                if lead:
                    o[t][0] = res[t]
                else:
                    o[t][...] = res[t]

    vm = pl.BlockSpec(memory_space=pltpu.VMEM)
    flat = [a for tr in params for a in tr]
    out_shape = tuple(jax.ShapeDtypeStruct(tr[0].shape, F32) for tr in params for _ in range(4)) + (jax.ShapeDtypeStruct((1, 1), F32),)
    res = pl.pallas_call(
        body, name="adamw_small", out_shape=out_shape, in_specs=[vm] * (4 + len(flat)), out_specs=(vm,) * len(out_shape),
        compiler_params=_cparams(),
    )(rp, rws, rcv, gcw, *flat)
    return [res[4 * p:4 * p + 4] for p in range(n_p)], res[4 * n_p]


def kernel(x, mix_norm_g, w_in, ret_norm_g, sgu_ln_g, sgu_ln_b, sgu_w_s, sgu_b_s, w_out, ffn_norm_g, w_up, conv_w, conv_b, w_down, final_norm_g, loss_target, m_mix_norm_g, m_w_in, m_ret_norm_g, m_sgu_ln_g, m_sgu_ln_b, m_sgu_w_s, m_sgu_b_s, m_w_out, m_ffn_norm_g, m_w_up, m_conv_w, m_conv_b, m_w_down, m_final_norm_g, v_mix_norm_g, v_w_in, v_ret_norm_g, v_sgu_ln_g, v_sgu_ln_b, v_sgu_w_s, v_sgu_b_s, v_w_out, v_ffn_norm_g, v_w_up, v_conv_w, v_conv_b, v_w_down, v_final_norm_g):
    xs = x[0]
    tgt = loss_target[0]
    mask, qdec, kdec = _decay_tables()
    grn = ret_norm_g.reshape(1, RET_W)
    lng = sgu_ln_g.reshape(1, SGU_W)
    lnb = sgu_ln_b.reshape(1, SGU_W)
    ws = sgu_w_s[0]
    bsb = jnp.broadcast_to(sgu_b_s[0][:, :, None], (HEADS, CHUNK, HEAD_DIM))
    gf = final_norm_g.reshape(1, D_MODEL)
    me = 4 * lax.axis_index("x") + 2 * lax.axis_index("y") + lax.axis_index("c")
    tr = lambda a: jnp.transpose(a[0])[None]
    tr_cw = lambda a: jnp.transpose(a, (1, 0, 2))

    proj, h1, cos2, sin2, win_g, cw_sh, wout_g, su, sdn = _fwd_proj(
        xs, mix_norm_g, _rope_freq(), w_in[0], w_out[0], tr(w_up)[0], w_down[0], tr_cw(conv_w))
    cw_g = jnp.transpose(cw_sh, (1, 0, 2)).reshape(8, 2 * D_FF)
    x2, mixcat, o, sprev, wup_g = _fwd_mix(xs, proj, wout_g, grn, lng, lnb, ws, bsb, mask, qdec, kdec, su)
    h2, up_pre, u_conv, act, x3, loss_parts, wdn_g = _fwd_ffn(x2, ffn_norm_g, wup_g, cw_g, conv_b, sdn, gf, tgt)

    dx3, dpre, dx2, dgf, dg2, dcv = _bwd_ffn(x3, tgt, gf, x2, ffn_norm_g, up_pre, u_conv, wup_g, cw_g, wdn_g)
    band = 512
    (gdn_p,) = _wgrad("wgrad_down", act, dx3, tm=FF_TILE)
    (gout_p,) = _wgrad("wgrad_out", mixcat, dx2, tn=512)
    gup_p, g_dn = _wgrad("wgrad_up", dpre, h2, tm=FF_TILE, tn=512, hosted=[(W_DOWN, gdn_p)])
    dproj, dgrn, dlng, dlnb, dws, dbs, g_up_a, g_out = _bwd_mix(
        dx2, proj, o, sprev, wout_g, grn, lng, lnb, ws, bsb, mask, qdec, kdec, cos2, sin2,
        [((W_UP, 0, band), gup_p), (W_OUT, gout_p)])
    gin_p, g_up_b = _wgrad("wgrad_in", h1, dproj, tm=512, tn=768, hosted=[((W_UP, band, FF_SHARD - band), gup_p)])
    grad_x, g_in, rp, rws, rcv = _bwd_proj(dproj, win_g, xs, mix_norm_g, dx2, gin_p,
                                           (dg2, dgf, dgrn, dlng, dlnb, dbs, loss_parts, dws, dcv))
    gcw = tr_cw(lax.dynamic_slice(rcv, (me // (N_DEV // 2), 0, (me % (N_DEV // 2)) * FF_SHARD), (1, 3, FF_SHARD)))

    table = {}
    *table["w_in"], grad_x = _adamw("adamw_w_in", w_in, [g_in], m_w_in, v_w_in, 256, thru=[grad_x])
    for name, w, gs, m, v, rows in (("w_out", w_out, [g_out], m_w_out, v_w_out, 128),
                                    ("w_up", tr(w_up), [g_up_a, g_up_b], tr(m_w_up), tr(v_w_up), 64),
                                    ("w_down", w_down, [g_dn], m_w_down, v_w_down, 88)):
        table[name] = _adamw("adamw_" + name, w, gs, m, v, rows)
    table["w_up"] = tuple(tr(a) for a in table["w_up"])
    row = lambda a: a.reshape(1, D_MODEL)
    names_small = ["mix_norm_g", "ffn_norm_g", "final_norm_g", "ret_norm_g", "sgu_ln_g", "sgu_ln_b", "sgu_b_s", "sgu_w_s",
                   "conv_w", "conv_b"]
    params = [(mix_norm_g, m_mix_norm_g, v_mix_norm_g), (ffn_norm_g, m_ffn_norm_g, v_ffn_norm_g),
              (row(final_norm_g), row(m_final_norm_g), row(v_final_norm_g)), (ret_norm_g, m_ret_norm_g, v_ret_norm_g),
              (sgu_ln_g, m_sgu_ln_g, v_sgu_ln_g), (sgu_ln_b, m_sgu_ln_b, v_sgu_ln_b), (sgu_b_s, m_sgu_b_s, v_sgu_b_s),
              (sgu_w_s, m_sgu_w_s, v_sgu_w_s), (tr_cw(conv_w), tr_cw(m_conv_w), tr_cw(v_conv_w)), (conv_b, m_conv_b, v_conv_b)]
    small, loss = _adamw_small(rp, rws, rcv, gcw, params)
    for n, res in zip(names_small, small):
        table[n] = res
    table["final_norm_g"] = tuple(a.reshape(D_MODEL) for a in table["final_norm_g"])
    table["conv_w"] = tuple(tr_cw(a) for a in table["conv_w"])

    order = ["mix_norm_g", "w_in", "ret_norm_g", "sgu_ln_g", "sgu_ln_b", "sgu_w_s", "sgu_b_s", "w_out", "ffn_norm_g", "w_up",
             "conv_w", "conv_b", "w_down", "final_norm_g"]
    outs = [loss.reshape(()), grad_x[None]]
    for col in range(4):
        outs += [table[n][col] for n in order]
    return tuple(outs)
```

```python
import functools
import math

import jax
import jax.numpy as jnp
import numpy as np
from jax import lax
from jax.experimental import pallas as pl
from jax.experimental.pallas import tpu as pltpu

F32 = jnp.float32
BF16 = jnp.bfloat16
MESH = pl.DeviceIdType.MESH

N_DEV = 8
SEQ = 2048
D_MODEL = 1024
CHUNK = 128
N_CHUNK = SEQ // CHUNK
HEADS = 4
HEAD_DIM = 128
RET_W = 512
SGU_W = 512
PROJ_W = 3072
D_FF = 2816
FF_SHARD = 704
FF_TILE = 1408
FF_TILES = ((0, 1536), (1536, 1280))
IN_SHARD = PROJ_W // N_DEV
OUT_SHARD = D_MODEL // N_DEV
DOWN_SHARD = D_FF // N_DEV
TM = 256
N_TB = SEQ // TM
FWD_MIX_PASS_AT = 10
EPS = 1e-6
ROPE_BASE = 10000.0
K_SCALE = HEAD_DIM ** -0.5
INV_SQRT2 = 0.7071067811865476
INV_SQRT_2PI = 0.3989422804014327

ADAM_LR = 0.001
ADAM_B1 = 0.9
ADAM_B2 = 0.999
ADAM_EPS = 1e-08
ADAM_WD = 0.01
ADAM_STEP = 10

VMEM_LIMIT = 56 * 1024 * 1024


def _cparams(sem=None, vmem=VMEM_LIMIT, collective=None):
    return pltpu.CompilerParams(dimension_semantics=sem, vmem_limit_bytes=vmem, collective_id=collective)


COLLECTIVE = {name: k for k, name in enumerate(("fwd_proj", "fwd_mix", "wgrad_up", "bwd_mix", "wgrad_in", "bwd_proj"))}


class _Meet:
    def __init__(self, diagonal):
        x, y, c = lax.axis_index("x"), lax.axis_index("y"), lax.axis_index("c")
        self.peers = [(x, y, 1 - c), (1 - x, y, c), (x, 1 - y, c)] + ([(1 - x, 1 - y, c)] if diagonal else [])

    def signal(self):
        for peer in self.peers:
            pl.semaphore_signal(pltpu.get_barrier_semaphore(), inc=1, device_id=peer, device_id_type=MESH)

    def wait(self):
        pl.semaphore_wait(pltpu.get_barrier_semaphore(), len(self.peers))


def _resident(shape):
    nd = len(shape)
    return pl.BlockSpec(shape, lambda *_: (0,) * nd, pipeline_mode=pl.Buffered(1))


def _dot(a, b):
    return jnp.dot(a, b, preferred_element_type=F32)


def _dot_nt(a, b):
    return lax.dot_general(a, b, (((1,), (1,)), ((), ())), preferred_element_type=F32)


def _dot_tn(a, b):
    return lax.dot_general(a, b, (((0,), (0,)), ((), ())), preferred_element_type=F32)


def _sigmoid(x):
    return 1.0 / (1.0 + jnp.exp(-x))


def _gelu(x):
    return 0.5 * x * (1.0 + lax.erf(x * INV_SQRT2))


def _gelu_grad(x):
    return 0.5 * (1.0 + lax.erf(x * INV_SQRT2)) + x * (jnp.exp(-0.5 * x * x) * INV_SQRT_2PI)


def _rot(xh, cos2, sin2):
    return xh * cos2 + pltpu.roll(xh, HEAD_DIM // 2, 1) * sin2


def _rot_t(dh, cos2, sin2):
    return dh * cos2 + pltpu.roll(dh * sin2, HEAD_DIM // 2, 1)


def _rope_freq():
    half = HEAD_DIM // 2
    inv_freq = jnp.power(ROPE_BASE, -jnp.arange(half, dtype=F32) / half)
    return jnp.concatenate([inv_freq, inv_freq])[None, :]


def _rope_block(inv2, first_row, rows):
    pos = (lax.broadcasted_iota(jnp.int32, (rows, HEAD_DIM), 0) + first_row).astype(F32)
    ang = pos * inv2
    sin = jnp.sin(ang)
    lane = lax.broadcasted_iota(jnp.int32, (rows, HEAD_DIM), 1)
    return jnp.cos(ang), jnp.where(lane < HEAD_DIM // 2, -sin, sin)


def _decay_tables():
    log_gamma = jnp.log(1.0 - jnp.power(2.0, -5.0 - jnp.arange(HEADS, dtype=F32)))
    pos = jnp.arange(CHUNK, dtype=F32)
    diff = pos[:, None] - pos[None, :]
    mask = jnp.where(diff >= 0.0, jnp.exp(log_gamma[:, None, None] * jnp.maximum(diff, 0.0)[None]), 0.0)
    k_decay = jnp.exp(log_gamma[:, None] * (CHUNK - 1.0 - pos)[None])
    q_decay = jnp.exp(log_gamma[:, None] * (pos + 1.0)[None])
    kd = jnp.broadcast_to(k_decay[:, :, None], (HEADS, CHUNK, HEAD_DIM))
    qd = jnp.broadcast_to(q_decay[:, :, None], (HEADS, CHUNK, HEAD_DIM))
    return mask.astype(F32), qd.astype(F32), kd.astype(F32)


def _chunk_decay():
    lg = np.log(np.float32(1.0) - np.power(np.float32(2.0), -5.0 - np.arange(HEADS, dtype=np.float32))).astype(np.float32)
    return [float(np.exp(lg[h] * np.float32(CHUNK))) for h in range(HEADS)]


W_IN, W_OUT, W_UP, W_DOWN, W_CONV = range(5)
GATHERED = {W_IN: ((D_MODEL, PROJ_W), BF16), W_OUT: ((D_MODEL, D_MODEL), BF16), W_UP: ((2 * D_FF, D_MODEL), BF16),
            W_DOWN: ((D_FF, D_MODEL), BF16), W_CONV: ((N_DEV, 8, FF_SHARD), F32)}
SHARD = {W_IN: (D_MODEL, IN_SHARD), W_OUT: (OUT_SHARD, D_MODEL), W_UP: (FF_SHARD, D_MODEL), W_DOWN: (DOWN_SHARD, D_MODEL),
         W_CONV: (8, FF_SHARD)}


class _Gather:
    N_SEMS = 9

    def __init__(self, ids, stages, gathered, send_sems, recv_sems, local_sems):
        self.ids, self.stages, self.gathered = ids, stages, gathered
        self.send_sems, self.recv_sems, self.local_sems = send_sems, recv_sems, local_sems
        self.x, self.y, self.c = lax.axis_index("x"), lax.axis_index("y"), lax.axis_index("c")
        self.me = (self.x, self.y, self.c)
        self.sibling = (self.x, self.y, 1 - self.c)
        self.chips = [(1 - self.x, self.y), (self.x, 1 - self.y), (1 - self.x, 1 - self.y)]

    def slot(self, n, px, py, pc):
        dev = 4 * px + 2 * py + pc
        w, g = self.ids[n], self.gathered[n]
        if w == W_IN:
            return g.at[:, pl.ds(pl.multiple_of(dev * IN_SHARD, 128), IN_SHARD)]
        if w == W_OUT:
            return g.at[pl.ds(pl.multiple_of(dev * OUT_SHARD, 128), OUT_SHARD), :]
        if w == W_DOWN:
            return g.at[pl.ds(pl.multiple_of(dev * DOWN_SHARD, 32), DOWN_SHARD), :]
        if w == W_UP:
            return g.at[pl.ds(pl.multiple_of(dev * FF_SHARD, 32), FF_SHARD), :]
        return g.at[dev]

    def half(self, n, px, py, pc, h):
        dev = 4 * px + 2 * py + pc
        w, g = self.ids[n], self.gathered[n]
        if w == W_IN:
            return g.at[pl.ds(h * (D_MODEL // 2), D_MODEL // 2), pl.ds(pl.multiple_of(dev * IN_SHARD, 128), IN_SHARD)]
        rows = SHARD[w][0] // 2
        return g.at[pl.ds(pl.multiple_of(dev * SHARD[w][0] + h * rows, 16), rows), :]

    def tree(self, n):
        return self.ids[n] != W_CONV

    def copy(self, n, k, block, to, src=None, h=None):
        ref = self.slot(n, *block) if h is None else self.half(n, *block, h)
        return pltpu.make_async_remote_copy(
            src_ref=ref if src is None else src, dst_ref=ref,
            send_sem=self.send_sems.at[n, k], recv_sem=self.recv_sems.at[n, k], device_id=to, device_id_type=MESH)

    def _mine(self):
        return [pltpu.make_async_copy(self.stages[n], self.slot(n, *self.me), self.local_sems.at[n]) for n in range(len(self.ids))]

    def _first(self):
        out = []
        for n in range(len(self.ids)):
            out.append(self.copy(n, 0, self.me, self.sibling, src=self.stages[n]))
            out += [self.copy(n, 1 + j, self.me, (*chip, self.c), src=self.stages[n])
                    for j, chip in enumerate(self.chips[:2] if self.tree(n) else self.chips)]
        return out

    def start(self):
        for cp in self._mine() + self._first():
            cp.start()

    def _passed(self, j):
        dev = (*self.chips[j], self.c)
        out = []
        for n in range(len(self.ids)):
            if not self.tree(n):
                out.append(self.copy(n, 4 + j, dev, self.sibling))
            elif j < 2:
                out += [self.copy(n, 3 + j, dev, (*self.chips[1 - j], self.c), h=j), self.copy(n, 5 + j, dev, self.sibling)]
            else:
                out += [self.copy(n, 7, dev, self.sibling, h=0), self.copy(n, 8, dev, self.sibling, h=1)]
        return out

    def near(self):
        for j in range(2):
            dev = (*self.chips[j], self.c)
            for n in range(len(self.ids)):
                self.copy(n, 1 + j, dev, self.me).wait_recv()
            for cp in self._passed(j):
                cp.start()

    def home(self):
        for n in range(len(self.ids)):
            self.copy(n, 0, self.sibling, self.me).wait_recv()
        for cp in self._mine():
            cp.wait()

    def mid(self):
        for n in range(len(self.ids)):
            for j, chip in enumerate(self.chips[:2]):
                self.copy(n, (5 if self.tree(n) else 4) + j, (*chip, 1 - self.c), self.me).wait_recv()

    def far(self):
        dev, other = (*self.chips[2], self.c), (*self.chips[2], 1 - self.c)
        for n in range(len(self.ids)):
            if self.tree(n):
                self.copy(n, 3, dev, self.me, h=0).wait_recv()
                self.copy(n, 4, dev, self.me, h=1).wait_recv()
            else:
                self.copy(n, 3, dev, self.me).wait_recv()
        for cp in self._passed(2):
            cp.start()
        for n in range(len(self.ids)):
            if self.tree(n):
                self.copy(n, 7, other, self.me, h=0).wait_recv()
                self.copy(n, 8, other, self.me, h=1).wait_recv()
            else:
                self.copy(n, 6, other, self.me).wait_recv()
        for cp in self._first() + self._passed(0) + self._passed(1) + self._passed(2):
            cp.wait_send()

    def finish(self):
        self.home()
        self.mid()
        self.far()


def _gather_scratch(n):
    return [pltpu.SemaphoreType.DMA((n, _Gather.N_SEMS)), pltpu.SemaphoreType.DMA((n, _Gather.N_SEMS)), pltpu.SemaphoreType.DMA((n,))]


def _gathered_shapes(ids):
    return tuple(jax.ShapeDtypeStruct(*GATHERED[w]) for w in ids)


def _fwd_proj(x, g1, inv2, w_in, w_out, w_up, w_down, conv_w):
    ids_a, ids_b = [W_IN, W_CONV], [W_OUT, W_DOWN]
    tp = 2 * TM
    n_tp = SEQ // tp
    cols = 2 * IN_SHARD
    blocks = cols // HEAD_DIM
    xi, yi = lax.axis_index("x"), lax.axis_index("y")
    order = jnp.stack([2 * xi + yi, 2 * (1 - xi) + yi, 2 * xi + (1 - yi), 2 * (1 - xi) + (1 - yi)]).astype(jnp.int32)

    def body(order_ref, x_ref, g_ref, inv_ref, in_hbm, out_hbm, up_hbm, dn_hbm, cw_ref,
             proj_ref, h1_ref, cos_ref, sin_ref, gin, gcw, gout, gdn, su_ref,
             w_vm, s_in, s_cw, s_out, s_dn, f_in, f_out, f_up, f_dn, ld_sems,
             a_send, a_recv, a_local, b_send, b_recv, b_local):
        ag_a = _Gather(ids_a, [s_in, s_cw], [gin, gcw], a_send, a_recv, a_local)
        ag_b = _Gather(ids_b, [s_out, s_dn], [gout, gdn], b_send, b_recv, b_local)
        p, i = pl.program_id(0), pl.program_id(1)
        chip = order_ref[p]

        def fill():
            cp = pltpu.make_async_copy(gin.at[:, pl.ds(pl.multiple_of(chip * cols, 128), cols)], w_vm, ld_sems.at[4])
            cp.start()
            cp.wait()

        @pl.when((p == 0) & (i == 0))
        def _():
            meet = _Meet(diagonal=True)
            meet.signal()
            loads = [pltpu.make_async_copy(src, dst, ld_sems.at[i])
                     for i, (src, dst) in enumerate(((in_hbm, f_in), (out_hbm, f_out), (dn_hbm, f_dn), (up_hbm, f_up)))]
            for cp in loads:
                cp.start()
            s_cw[...] = jnp.zeros_like(s_cw)
            for k in range(3):
                s_cw[k:k + 1, :] = cw_ref[k]
            loads[0].wait()
            s_in[...] = f_in[...].astype(BF16)
            meet.wait()
            ag_a.start()
            loads[1].wait()
            s_out[...] = f_out[...].astype(BF16)
            loads[2].wait()
            s_dn[...] = f_dn[...].astype(BF16)
            loads[3].wait()
            su_ref[...] = f_up[...].astype(BF16)
            ag_a.home()
            fill()

        @pl.when((p == 1) & (i == 0))
        def _():
            ag_a.near()
            ag_b.start()
            ag_a.mid()
            fill()

        pl.when((p == 2) & (i == 0))(fill)

        @pl.when((p == 3) & (i == 0))
        def _():
            ag_a.far()
            fill()

        xb = x_ref[...]
        r = lax.rsqrt(jnp.mean(xb * xb, axis=-1, keepdims=True) + EPS)
        h = ((xb * r) * g_ref[...]).astype(BF16)
        c2, s2 = _rope_block(inv_ref[...], i * tp, tp)

        @pl.when(p == 0)
        def _():
            h1_ref[...] = h
            cos_ref[...], sin_ref[...] = c2, s2

        pr = _dot(h, w_vm[...])
        for j in range(blocks):
            gb = blocks * chip + j
            sl = slice(j * HEAD_DIM, (j + 1) * HEAD_DIM)
            rot = _rot(pr[:, sl], c2, s2)
            proj_ref[:, sl] = jnp.where(gb < HEADS, rot, jnp.where(gb < 2 * HEADS, rot * K_SCALE, pr[:, sl]))

        @pl.when((p == 3) & (i == n_tp - 1))
        def _():
            ag_b.near()
            ag_b.finish()

    tok = lambda w: pl.BlockSpec((tp, w), lambda p, i, o: (i, 0))
    once = lambda w: pl.BlockSpec((tp, w), lambda p, i, o: (jnp.where(p == 0, i, n_tp - 1), 0))
    hbm = pl.BlockSpec(memory_space=pl.ANY)
    vm = pl.BlockSpec(memory_space=pltpu.VMEM)
    return pl.pallas_call(
        body, name="fwd_proj",
        grid_spec=pltpu.PrefetchScalarGridSpec(
            num_scalar_prefetch=1, grid=(4, n_tp),
            in_specs=[tok(D_MODEL), _resident((1, D_MODEL)), _resident((1, HEAD_DIM)), hbm, hbm, hbm, hbm, vm],
            out_specs=(pl.BlockSpec((tp, cols), lambda p, i, o: (i, o[p])), once(D_MODEL), once(HEAD_DIM), once(HEAD_DIM),
                       hbm, hbm, hbm, hbm, vm),
            scratch_shapes=[pltpu.VMEM((D_MODEL, cols), BF16), pltpu.VMEM(SHARD[W_IN], BF16), pltpu.VMEM(SHARD[W_CONV], F32),
                            pltpu.VMEM(SHARD[W_OUT], BF16), pltpu.VMEM(SHARD[W_DOWN], BF16),
                            pltpu.VMEM(SHARD[W_IN], F32), pltpu.VMEM(SHARD[W_OUT], F32), pltpu.VMEM(SHARD[W_UP], F32),
                            pltpu.VMEM(SHARD[W_DOWN], F32), pltpu.SemaphoreType.DMA((5,))]
            + _gather_scratch(len(ids_a)) + _gather_scratch(len(ids_b))),
        out_shape=(jax.ShapeDtypeStruct((SEQ, PROJ_W), F32), jax.ShapeDtypeStruct((SEQ, D_MODEL), BF16),
                   jax.ShapeDtypeStruct((SEQ, HEAD_DIM), F32), jax.ShapeDtypeStruct((SEQ, HEAD_DIM), F32))
        + _gathered_shapes(ids_a + ids_b) + (jax.ShapeDtypeStruct(SHARD[W_UP], BF16),),
        compiler_params=_cparams(("arbitrary", "arbitrary"), collective=COLLECTIVE["fwd_proj"]),
    )(order, x, g1, inv2, w_in, w_out, w_up, w_down, conv_w)


def _causal(w):
    r = lax.broadcasted_iota(jnp.int32, (CHUNK, CHUNK), 0)
    c = lax.broadcasted_iota(jnp.int32, (CHUNK, CHUNK), 1)
    return jnp.where(r >= c, w, 0.0)


def _fwd_mix(x, proj, wout_g, grn, lng, lnb, ws, bsb, mask, qdec, kdec, su):
    cdec = _chunk_decay()
    ids = [W_UP]

    def body(x_ref, p_ref, w_ref, grn_ref, lng_ref, lnb_ref, ws_ref, bsb_ref, m_ref, qd_ref, kd_ref, su_ref,
             x2_ref, cat_ref, o_ref, sp_ref, gup, state, send_sems, recv_sems, local_sems):
        ag = _Gather(ids, [su_ref], [gup], send_sems, recv_sems, local_sems)

        @pl.when(pl.program_id(0) == 0)
        def _():
            meet = _Meet(diagonal=False)
            meet.signal()
            state[...] = jnp.zeros_like(state)
            meet.wait()
            ag.start()

        for h in range(HEADS):
            sl = slice(h * HEAD_DIM, (h + 1) * HEAD_DIM)
            q = p_ref[:, sl]
            k = p_ref[:, RET_W + h * HEAD_DIM:RET_W + (h + 1) * HEAD_DIM]
            v = p_ref[:, 2 * RET_W + h * HEAD_DIM:2 * RET_W + (h + 1) * HEAD_DIM]
            g = p_ref[:, 3 * RET_W + h * HEAD_DIM:3 * RET_W + (h + 1) * HEAD_DIM]
            qb, kb, vb = q.astype(BF16), k.astype(BF16), v.astype(BF16)
            a = _dot_nt(qb, kb) * m_ref[h]
            spb = state[h].astype(BF16)
            sp_ref[0, h] = spb
            o = _dot(a.astype(BF16), vb) + _dot((q * qd_ref[h]).astype(BF16), spb)
            state[h] = state[h] * cdec[h] + _dot_tn((k * kd_ref[h]).astype(BF16), vb)
            o_ref[:, sl] = o
            rinv = lax.rsqrt(jnp.mean(o * o, axis=-1, keepdims=True) + EPS)
            rn = (o * rinv) * grn_ref[:, sl]
            cat_ref[:, sl] = ((g * _sigmoid(g)) * rn).astype(BF16)
        for gi in range(HEADS):
            sl = slice(gi * HEAD_DIM, (gi + 1) * HEAD_DIM)
            u = p_ref[:, 4 * RET_W + gi * HEAD_DIM:4 * RET_W + (gi + 1) * HEAD_DIM]
            sv = p_ref[:, 4 * RET_W + SGU_W + gi * HEAD_DIM:4 * RET_W + SGU_W + (gi + 1) * HEAD_DIM]
            gv = _gelu(sv)
            xc = gv - jnp.mean(gv, axis=-1, keepdims=True)
            vn = (xc * lax.rsqrt(jnp.mean(xc * xc, axis=-1, keepdims=True) + EPS)) * lng_ref[:, sl] + lnb_ref[:, sl]
            mixed = _dot(_causal(ws_ref[gi]).astype(BF16), vn.astype(BF16)) + bsb_ref[gi]
            cat_ref[:, RET_W + gi * HEAD_DIM:RET_W + (gi + 1) * HEAD_DIM] = (_gelu(u) * mixed).astype(BF16)
        x2_ref[...] = x_ref[...] + _dot(cat_ref[...], w_ref[...])

        pl.when(pl.program_id(0) == FWD_MIX_PASS_AT)(ag.near)
        pl.when(pl.program_id(0) == N_CHUNK - 1)(ag.finish)

    ch = lambda w: pl.BlockSpec((CHUNK, w), lambda i: (i, 0))
    hcc = (HEADS, CHUNK, CHUNK)
    hbm = pl.BlockSpec(memory_space=pl.ANY)
    return pl.pallas_call(
        body, name="fwd_mix", grid=(N_CHUNK,),
        out_shape=(jax.ShapeDtypeStruct((SEQ, D_MODEL), F32), jax.ShapeDtypeStruct((SEQ, D_MODEL), BF16),
                   jax.ShapeDtypeStruct((SEQ, RET_W), F32), jax.ShapeDtypeStruct((N_CHUNK, HEADS, HEAD_DIM, HEAD_DIM), BF16))
        + _gathered_shapes(ids),
        in_specs=[ch(D_MODEL), ch(PROJ_W), _resident((D_MODEL, D_MODEL)), _resident((1, RET_W)), _resident((1, SGU_W)),
                  _resident((1, SGU_W)), _resident(hcc), _resident(hcc), _resident(hcc), _resident(hcc), _resident(hcc), hbm],
        out_specs=(ch(D_MODEL), ch(D_MODEL), ch(RET_W), pl.BlockSpec((1, HEADS, HEAD_DIM, HEAD_DIM), lambda i: (i, 0, 0, 0)), hbm),
        scratch_shapes=[pltpu.VMEM((HEADS, HEAD_DIM, HEAD_DIM), F32)] + _gather_scratch(len(ids)),
        compiler_params=_cparams(("arbitrary",), collective=COLLECTIVE["fwd_mix"]),
    )(x, proj, wout_g, grn, lng, lnb, ws, bsb, mask, qdec, kdec, su)


def _conv_taps(p, prev8):
    row = lax.broadcasted_iota(jnp.int32, p.shape, 0)
    p1 = jnp.where(row == 0, prev8[7:8, :], pltpu.roll(p, 1, 0))
    p2 = jnp.where(row == 0, prev8[6:7, :], jnp.where(row == 1, prev8[7:8, :], pltpu.roll(p, 2, 0)))
    return p1, p2


def _fwd_ffn(x2, g2, wup_g, cw_g, cb_g, wdn_g, gf, tgt):
    def body(x_ref, g_ref, wu_ref, cw_ref, cb_ref, wd_ref, gf_ref, t_ref, h2_ref, up_ref, u_ref, act_ref, x3_ref, loss_ref, carry):
        @pl.when(pl.program_id(0) == 0)
        def _():
            carry[...] = jnp.zeros_like(carry)

        xb = x_ref[...]
        r = lax.rsqrt(jnp.mean(xb * xb, axis=-1, keepdims=True) + EPS)
        h = ((xb * r) * g_ref[...]).astype(BF16)
        h2_ref[...] = h
        acc = xb
        for t0, tw in FF_TILES:
            u = []
            for c0 in (t0, D_FF + t0):
                cs = slice(c0, c0 + tw)
                p = _dot_nt(h, wu_ref[pl.ds(c0, tw), :])
                up_ref[:, cs] = p.astype(BF16)
                p1, p2 = _conv_taps(p, carry[:, cs])
                carry[:, cs] = p[TM - 8:, :]
                us = p2 * cw_ref[0:1, cs] + p1 * cw_ref[1:2, cs] + p * cw_ref[2:3, cs] + cb_ref[:, cs]
                u_ref[:, cs] = us.astype(BF16)
                u.append(us)
            a = ((u[0] * _sigmoid(u[0])) * u[1]).astype(BF16)
            act_ref[:, t0:t0 + tw] = a
            acc = acc + _dot(a, wd_ref[pl.ds(t0, tw), :])
        x3_ref[...] = acc
        r3 = lax.rsqrt(jnp.mean(acc * acc, axis=-1, keepdims=True) + EPS)
        diff = (acc * r3) * gf_ref[...] - t_ref[...]
        loss_ref[...] = jnp.full(loss_ref.shape, 0.5 * jnp.sum(jnp.mean(diff * diff, axis=-1)), F32)

    tok = lambda w: pl.BlockSpec((TM, w), lambda i: (i, 0))
    return pl.pallas_call(
        body, name="fwd_ffn", grid=(N_TB,),
        out_shape=(jax.ShapeDtypeStruct((SEQ, D_MODEL), BF16), jax.ShapeDtypeStruct((SEQ, 2 * D_FF), BF16),
                   jax.ShapeDtypeStruct((SEQ, 2 * D_FF), BF16),
                   jax.ShapeDtypeStruct((SEQ, D_FF), BF16), jax.ShapeDtypeStruct((SEQ, D_MODEL), F32),
                   jax.ShapeDtypeStruct((N_TB, 8, 128), F32)),
        in_specs=[tok(D_MODEL), _resident((1, D_MODEL)), _resident((2 * D_FF, D_MODEL)), _resident((8, 2 * D_FF)),
                  _resident((1, 2 * D_FF)), _resident((D_FF, D_MODEL)), _resident((1, D_MODEL)), tok(D_MODEL)],
        out_specs=(tok(D_MODEL), tok(2 * D_FF), tok(2 * D_FF), tok(D_FF), tok(D_MODEL),
                   pl.BlockSpec((1, 8, 128), lambda i: (i, 0, 0))),
        scratch_shapes=[pltpu.VMEM((8, 2 * D_FF), F32)],
        compiler_params=_cparams(("arbitrary",)),
    )(x2, g2, wup_g, cw_g, cb_g, wdn_g, gf, tgt)


def _bwd_ffn(x3, tgt, gf, x2, g2, up_pre, u_conv, wup_g, cw_g, wdn_g):
    def body(x3_ref, t_ref, gf_ref, x2_ref, g2_ref, up_ref, u_ref, wu_ref, cw_ref, wd_ref,
             dx3_ref, dpre_ref, dx2_ref, dgf_ref, dg2_ref, dcv_ref, nxt):
        i = pl.program_id(0)

        @pl.when(i == 0)
        def _():
            nxt[...] = jnp.zeros_like(nxt)
            dgf_ref[...] = jnp.zeros_like(dgf_ref)
            dg2_ref[...] = jnp.zeros_like(dg2_ref)
            dcv_ref[...] = jnp.zeros_like(dcv_ref)

        x3 = x3_ref[...]
        r3 = lax.rsqrt(jnp.mean(x3 * x3, axis=-1, keepdims=True) + EPS)
        xh3 = x3 * r3
        dy = (xh3 * gf_ref[...] - t_ref[...]) * (1.0 / D_MODEL)
        dgf_ref[0:1, :] += jnp.sum(dy * xh3, axis=0, keepdims=True)
        t3 = dy * gf_ref[...]
        dx3 = r3 * (t3 - xh3 * jnp.mean(t3 * xh3, axis=-1, keepdims=True))
        dx3b = dx3.astype(BF16)
        dx3_ref[...] = dx3b
        dh2 = jnp.zeros((TM, D_MODEL), F32)
        for t0, tw in FF_TILES:
            row = lax.broadcasted_iota(jnp.int32, (TM, tw), 0)
            ts = slice(t0, t0 + tw)
            dact = _dot_nt(dx3b, wd_ref[pl.ds(t0, tw), :])
            ua = u_ref[:, ts].astype(F32)
            ub = u_ref[:, D_FF + t0:D_FF + t0 + tw].astype(F32)
            sg = _sigmoid(ua)
            du = [dact * ub * (sg * (1.0 + ua * (1.0 - sg))), dact * (ua * sg)]
            for n in range(2):
                d = du[n]
                c0 = n * D_FF + t0
                cs = slice(c0, c0 + tw)
                nx = nxt[:, cs]
                n1 = jnp.where(row == TM - 1, nx[0:1, :], pltpu.roll(d, TM - 1, 0))
                n2 = jnp.where(row == TM - 2, nx[0:1, :], jnp.where(row == TM - 1, nx[1:2, :], pltpu.roll(d, TM - 2, 0)))
                nxt[:, cs] = d[0:8, :]
                dp = (d * cw_ref[2:3, cs] + n1 * cw_ref[1:2, cs] + n2 * cw_ref[0:1, cs]).astype(BF16)
                dpre_ref[:, cs] = dp
                p = up_ref[:, cs].astype(F32)
                dcv_ref[n, 0:1, ts] += jnp.sum(n2 * p, axis=0, keepdims=True)
                dcv_ref[n, 1:2, ts] += jnp.sum(n1 * p, axis=0, keepdims=True)
                dcv_ref[n, 2:3, ts] += jnp.sum(d * p, axis=0, keepdims=True)
                dcv_ref[n, 3:4, ts] += jnp.sum(d, axis=0, keepdims=True)
                dh2 = dh2 + _dot(dp, wu_ref[pl.ds(c0, tw), :])
        x2 = x2_ref[...]
        r2 = lax.rsqrt(jnp.mean(x2 * x2, axis=-1, keepdims=True) + EPS)
        xh2 = x2 * r2
        dg2_ref[0:1, :] += jnp.sum(dh2 * xh2, axis=0, keepdims=True)
        t2 = dh2 * g2_ref[...]
        dx2_ref[...] = dx3 + r2 * (t2 - xh2 * jnp.mean(t2 * xh2, axis=-1, keepdims=True))

    rev = lambda w: pl.BlockSpec((TM, w), lambda i: (N_TB - 1 - i, 0))
    acc = lambda s: pl.BlockSpec(s, lambda i: (0,) * len(s))
    return pl.pallas_call(
        body, name="bwd_ffn", grid=(N_TB,),
        out_shape=(jax.ShapeDtypeStruct((SEQ, D_MODEL), BF16), jax.ShapeDtypeStruct((SEQ, 2 * D_FF), BF16),
                   jax.ShapeDtypeStruct((SEQ, D_MODEL), F32), jax.ShapeDtypeStruct((8, D_MODEL), F32),
                   jax.ShapeDtypeStruct((8, D_MODEL), F32), jax.ShapeDtypeStruct((2, 8, D_FF), F32)),
        in_specs=[rev(D_MODEL), rev(D_MODEL), _resident((1, D_MODEL)), rev(D_MODEL), _resident((1, D_MODEL)), rev(2 * D_FF),
                  rev(2 * D_FF), _resident((2 * D_FF, D_MODEL)), _resident((8, 2 * D_FF)), _resident((D_FF, D_MODEL))],
        out_specs=(rev(D_MODEL), rev(2 * D_FF), rev(D_MODEL), acc((8, D_MODEL)), acc((8, D_MODEL)), acc((2, 8, D_FF))),
        scratch_shapes=[pltpu.VMEM((8, 2 * D_FF), F32)],
        compiler_params=_cparams(("arbitrary",)),
    )(x3, tgt, gf, x2, g2, up_pre, u_conv, wup_g, cw_g, wdn_g)


def _bwd_mix(dx2, proj, o, sprev, wout_g, grn, lng, lnb, ws, bsb, mask, qdec, kdec, cos2, sin2, hosted):
    cdec = _chunk_decay()
    geoms = [g for g, _ in hosted]
    n_h = len(hosted)

    def body(dx2_ref, p_ref, o_ref, sp_ref, w_ref, grn_ref, lng_ref, lnb_ref, ws_ref, bsb_ref, m_ref, qd_ref, kd_ref,
             cos_ref, sin_ref, *rest):
        dp_ref, dgrn_ref, dlng_ref, dlnb_ref, dws_ref, dbs_ref = rest[n_h:n_h + 6]
        dstate, dbs_acc = rest[2 * n_h + 6:2 * n_h + 8]
        i = pl.program_id(0)
        rs = _Scatters(geoms, rest[:n_h], rest[n_h + 6:2 * n_h + 6], rest[2 * n_h + 8:])
        pl.when(i == 0)(rs.phase1)
        pl.when(i == 3)(rs.phase2)
        pl.when(i == 8)(rs.phase2b)

        @pl.when(i == 0)
        def _():
            dstate[...] = jnp.zeros_like(dstate)
            dgrn_ref[...] = jnp.zeros_like(dgrn_ref)
            dlng_ref[...] = jnp.zeros_like(dlng_ref)
            dlnb_ref[...] = jnp.zeros_like(dlnb_ref)
            dws_ref[...] = jnp.zeros_like(dws_ref)
            dbs_ref[...] = jnp.zeros_like(dbs_ref)
            dbs_acc[...] = jnp.zeros_like(dbs_acc)

        dmix = _dot_nt(dx2_ref[...].astype(BF16), w_ref[...])
        for h in range(HEADS):
            sl = slice(h * HEAD_DIM, (h + 1) * HEAD_DIM)
            q = p_ref[:, sl]
            k = p_ref[:, RET_W + h * HEAD_DIM:RET_W + (h + 1) * HEAD_DIM]
            v = p_ref[:, 2 * RET_W + h * HEAD_DIM:2 * RET_W + (h + 1) * HEAD_DIM]
            g = p_ref[:, 3 * RET_W + h * HEAD_DIM:3 * RET_W + (h + 1) * HEAD_DIM]
            o = o_ref[:, sl]
            rinv = lax.rsqrt(jnp.mean(o * o, axis=-1, keepdims=True) + EPS)
            oh = o * rinv
            gr = grn_ref[:, sl]
            sg = _sigmoid(g)
            dret = dmix[:, sl]
            dp_ref[:, 3 * RET_W + h * HEAD_DIM:3 * RET_W + (h + 1) * HEAD_DIM] = (
                dret * (oh * gr) * (sg * (1.0 + g * (1.0 - sg)))).astype(BF16)
            drn = dret * (g * sg)
            dgrn_ref[0:1, sl] += jnp.sum(drn * oh, axis=0, keepdims=True)
            t = drn * gr
            do = rinv * (t - oh * jnp.mean(t * oh, axis=-1, keepdims=True))
            qb, kb, vb, dob = q.astype(BF16), k.astype(BF16), v.astype(BF16), do.astype(BF16)
            m = m_ref[h]
            ab = (_dot_nt(qb, kb) * m).astype(BF16)
            dab = (_dot_nt(dob, vb) * m).astype(BF16)
            spb = sp_ref[0, h]
            dsn = dstate[h]
            dsnb = dsn.astype(BF16)
            qdb = (q * qd_ref[h]).astype(BF16)
            kdb = (k * kd_ref[h]).astype(BF16)
            dq = _dot(dab, kb) + _dot_nt(dob, spb) * qd_ref[h]
            dk = _dot_tn(dab, qb) + _dot_nt(vb, dsnb) * kd_ref[h]
            dv = _dot_tn(ab, dob) + _dot(kdb, dsnb)
            dstate[h] = dsn * cdec[h] + _dot_tn(qdb, dob)
            c2, s2 = cos_ref[...], sin_ref[...]
            dp_ref[:, sl] = _rot_t(dq, c2, s2).astype(BF16)
            dp_ref[:, RET_W + h * HEAD_DIM:RET_W + (h + 1) * HEAD_DIM] = _rot_t(dk * K_SCALE, c2, s2).astype(BF16)
            dp_ref[:, 2 * RET_W + h * HEAD_DIM:2 * RET_W + (h + 1) * HEAD_DIM] = dv.astype(BF16)
        for gi in range(HEADS):
            sl = slice(gi * HEAD_DIM, (gi + 1) * HEAD_DIM)
            u = p_ref[:, 4 * RET_W + gi * HEAD_DIM:4 * RET_W + (gi + 1) * HEAD_DIM]
            sv = p_ref[:, 4 * RET_W + SGU_W + gi * HEAD_DIM:4 * RET_W + SGU_W + (gi + 1) * HEAD_DIM]
            gv = _gelu(sv)
            xc = gv - jnp.mean(gv, axis=-1, keepdims=True)
            rstd = lax.rsqrt(jnp.mean(xc * xc, axis=-1, keepdims=True) + EPS)
            xh = xc * rstd
            lg = lng_ref[:, sl]
            vnb = (xh * lg + lnb_ref[:, sl]).astype(BF16)
            wcb = _causal(ws_ref[gi]).astype(BF16)
            mixed = _dot(wcb, vnb) + bsb_ref[gi]
            dsgu = dmix[:, RET_W + gi * HEAD_DIM:RET_W + (gi + 1) * HEAD_DIM]
            dmixed = dsgu * _gelu(u)
            dmb = dmixed.astype(BF16)
            dws_ref[gi] += _causal(_dot_nt(dmb, vnb))
            dbs_acc[gi] += dmixed
            dvn = _dot_tn(wcb, dmb)
            dlng_ref[gi:gi + 1, :] += jnp.sum(dvn * xh, axis=0, keepdims=True)
            dlnb_ref[gi:gi + 1, :] += jnp.sum(dvn, axis=0, keepdims=True)
            dxh = dvn * lg
            dgv = rstd * (dxh - jnp.mean(dxh, axis=-1, keepdims=True) - xh * jnp.mean(dxh * xh, axis=-1, keepdims=True))
            dp_ref[:, 4 * RET_W + gi * HEAD_DIM:4 * RET_W + (gi + 1) * HEAD_DIM] = (dsgu * mixed * _gelu_grad(u)).astype(BF16)
            dp_ref[:, 4 * RET_W + SGU_W + gi * HEAD_DIM:4 * RET_W + SGU_W + (gi + 1) * HEAD_DIM] = (
                dgv * _gelu_grad(sv)).astype(BF16)

        @pl.when(i == N_CHUNK - 1)
        def _():
            for gi in range(HEADS):
                col = jnp.broadcast_to(jnp.sum(dbs_acc[gi], axis=-1, keepdims=True), (CHUNK, CHUNK))
                dbs_ref[gi:gi + 1, :] = jnp.transpose(col)[0:1, :]
            rs.phase3()

    rev = lambda w: pl.BlockSpec((CHUNK, w), lambda i: (N_CHUNK - 1 - i, 0))
    hcc = (HEADS, CHUNK, CHUNK)
    acc = lambda s: pl.BlockSpec(s, lambda i: (0,) * len(s))
    res = pl.pallas_call(
        body, name="bwd_mix", grid=(N_CHUNK,),
        out_shape=(jax.ShapeDtypeStruct((SEQ, PROJ_W), BF16), jax.ShapeDtypeStruct((8, RET_W), F32),
                   jax.ShapeDtypeStruct((8, HEAD_DIM), F32), jax.ShapeDtypeStruct((8, HEAD_DIM), F32),
                   jax.ShapeDtypeStruct(hcc, F32), jax.ShapeDtypeStruct((8, CHUNK), F32)) + _scatter_out_shapes(geoms),
        in_specs=[rev(D_MODEL), rev(PROJ_W), rev(RET_W),
                  pl.BlockSpec((1, HEADS, HEAD_DIM, HEAD_DIM), lambda i: (N_CHUNK - 1 - i, 0, 0, 0)),
                  _resident((D_MODEL, D_MODEL)), _resident((1, RET_W)), _resident((1, SGU_W)), _resident((1, SGU_W)),
                  _resident(hcc), _resident(hcc), _resident(hcc), _resident(hcc), _resident(hcc), rev(HEAD_DIM), rev(HEAD_DIM)]
        + [pl.BlockSpec(memory_space=pl.ANY)] * n_h,
        out_specs=(rev(PROJ_W), acc((8, RET_W)), acc((8, HEAD_DIM)), acc((8, HEAD_DIM)), acc(hcc), acc((8, CHUNK)))
        + _scatter_out_specs(geoms),
        scratch_shapes=[pltpu.VMEM((HEADS, HEAD_DIM, HEAD_DIM), F32), pltpu.VMEM((HEADS, CHUNK, CHUNK), F32)] + _scatter_scratch(geoms),
        compiler_params=_cparams(("arbitrary",), collective=COLLECTIVE["bwd_mix"]),
    )(dx2, proj, o, sprev, wout_g, grn, lng, lnb, ws, bsb, mask, qdec, kdec, cos2, sin2, *[p for _, p in hosted])
    return tuple(res[:6 + n_h])


def _bwd_proj(dproj, win_g, x, g1, dx2, gin_p, small):
    geoms = [W_IN]
    n_s = len(small)

    def body(dp_ref, w_ref, x_ref, g_ref, dx2_ref, gin_ref, *rest):
        small_refs = rest[:n_s]
        dx_ref, rs_out, rp_ref, rws_ref, rcv_ref, dg_ref = rest[n_s:n_s + 6]
        rs_scratch = rest[n_s + 6:n_s + 6 + N_SCATTER_SCRATCH]
        ar_scratch = rest[n_s + 6 + N_SCATTER_SCRATCH:]
        ar_res = ar_scratch[N_SMALL_SCRATCH:]
        ar = _SmallReduce((dg_ref,) + tuple(small_refs), ar_res, ar_scratch[:N_SMALL_SCRATCH])
        rs = _Scatters(geoms, [gin_ref], [rs_out], rs_scratch)
        pl.when(pl.program_id(0) == 0)(lambda: rs.phase1(diagonal=True))
        pl.when(pl.program_id(0) == 1)(rs.phase2)
        pl.when(pl.program_id(0) == 5)(rs.phase2b)

        @pl.when(pl.program_id(0) == 0)
        def _():
            dg_ref[...] = jnp.zeros_like(dg_ref)

        dh = _dot_nt(dp_ref[...], w_ref[...])
        xb = x_ref[...]
        r = lax.rsqrt(jnp.mean(xb * xb, axis=-1, keepdims=True) + EPS)
        xh = xb * r
        dg_ref[0:1, :] += jnp.sum(dh * xh, axis=0, keepdims=True)
        t = dh * g_ref[...]
        dx_ref[...] = dx2_ref[...] + r * (t - xh * jnp.mean(t * xh, axis=-1, keepdims=True))

        @pl.when(pl.program_id(0) == N_TB - 1)
        def _():
            ar.begin()
            rs.phase3()
            ar.end()
            for o_ref, r_ref in zip((rp_ref, rws_ref, rcv_ref), ar_res):
                o_ref[...] = r_ref[...]

    tok = lambda w: pl.BlockSpec((TM, w), lambda i: (i, 0))
    vm = pl.BlockSpec(memory_space=pltpu.VMEM)
    res = pl.pallas_call(
        body, name="bwd_proj", grid=(N_TB,),
        out_shape=(jax.ShapeDtypeStruct((SEQ, D_MODEL), F32),) + _scatter_out_shapes(geoms)
        + tuple(jax.ShapeDtypeStruct(s, F32) for s in SMALL_FULL),
        in_specs=[tok(PROJ_W), _resident((D_MODEL, PROJ_W)), tok(D_MODEL), _resident((1, D_MODEL)), tok(D_MODEL),
                  pl.BlockSpec(memory_space=pl.ANY)] + [vm] * n_s,
        out_specs=(tok(D_MODEL),) + _scatter_out_specs(geoms) + (vm,) * len(SMALL_FULL),
        scratch_shapes=[pltpu.VMEM((8, D_MODEL), F32)] + _scatter_scratch(geoms) + _small_scratch()
        + [pltpu.VMEM(s, F32) for s in SMALL_FULL],
        compiler_params=_cparams(("arbitrary",), collective=COLLECTIVE["bwd_proj"]),
    )(dproj, win_g, x, g1, dx2, gin_p, *small)
    return res


def _wgrad(name, a, b, tm=None, tn=None, hosted=()):
    m_w, n_w = a.shape[-1], b.shape[-1]
    tm = m_w if tm is None else tm
    tn = n_w if tn is None else tn
    n_steps = (m_w // tm) * (n_w // tn)
    geoms = [g for g, _ in hosted]
    n_h = len(hosted)

    def body(a_ref, b_ref, *rest):
        o_ref = rest[n_h]
        if n_h:
            rs = _Scatters(geoms, rest[:n_h], rest[n_h + 1:2 * n_h + 1], rest[2 * n_h + 1:])
            step = pl.program_id(0) * (n_w // tn) + pl.program_id(1)
            pl.when(step == 0)(rs.phase1)
            pl.when(step == 1)(rs.phase2)
            pl.when(step == n_steps // 2)(rs.phase2b)
        o_ref[...] = _dot_tn(a_ref[...].astype(BF16), b_ref[...].astype(BF16)).astype(BF16)
        if n_h:
            pl.when(step == n_steps - 1)(rs.phase3)

    assert not n_h or n_steps >= 4
    res = pl.pallas_call(
        body, name=name, grid=(m_w // tm, n_w // tn),
        out_shape=(jax.ShapeDtypeStruct((m_w, n_w), BF16),) + _scatter_out_shapes(geoms),
        in_specs=[pl.BlockSpec((SEQ, tm), lambda i, j: (0, i)), pl.BlockSpec((SEQ, tn), lambda i, j: (0, j))]
        + [pl.BlockSpec(memory_space=pl.ANY)] * n_h,
        out_specs=(pl.BlockSpec((tm, tn), lambda i, j: (i, j)),) + _scatter_out_specs(geoms),
        scratch_shapes=_scatter_scratch(geoms),
        compiler_params=_cparams(("arbitrary", "arbitrary"), collective=COLLECTIVE[name]) if n_h else _cparams(("parallel", "parallel")),
    )(a, b, *[p for _, p in hosted])
    return tuple(res[:1 + n_h])


def _row_step(half_rows):
    return max(s for s in range(16, 177, 16) if half_rows % s == 0)


class _Scatter:
    def __init__(self, geom, partial, out, land1, mine, stage2, land2, comb, s1_send, s1_recv, s2_send, s2_recv, ld_sems):
        self.w, self.row0, self.shape = _geom(geom)
        self.partial, self.out, self.land1 = partial, out, land1
        self.mine, self.stage2, self.land2, self.comb = mine, stage2, land2, comb
        self.hr = self.shape[0] // 2
        self.step = _row_step(self.hr)
        self.s1_send, self.s1_recv, self.s2_send, self.s2_recv, self.ld_sems = s1_send, s1_recv, s2_send, s2_recv, ld_sems
        self.x, self.y, self.c = lax.axis_index("x"), lax.axis_index("y"), lax.axis_index("c")
        self.sibling = (self.x, self.y, 1 - self.c)
        self.chips = [(self.x, self.y), (1 - self.x, self.y), (self.x, 1 - self.y), (1 - self.x, 1 - self.y)]

    def block(self, px, py, pc):
        dev = 4 * px + 2 * py + pc
        if self.w == W_IN:
            return self.partial.at[:, pl.ds(pl.multiple_of(dev * IN_SHARD, 128), IN_SHARD)]
        if self.w == W_OUT:
            return self.partial.at[pl.ds(pl.multiple_of(dev * OUT_SHARD, 128), OUT_SHARD), :]
        if self.w == W_DOWN:
            return self.partial.at[pl.ds(pl.multiple_of(dev * DOWN_SHARD, 32), DOWN_SHARD), :]
        return self.partial.at[pl.ds(pl.multiple_of(dev * FF_SHARD + self.row0, 32), self.shape[0]), :]

    def copy1(self, k):
        return pltpu.make_async_remote_copy(
            src_ref=self.block(*self.chips[k], 1 - self.c), dst_ref=self.land1.at[k],
            send_sem=self.s1_send.at[k], recv_sem=self.s1_recv.at[k], device_id=self.sibling, device_id_type=MESH)

    STAGE2 = [(1, 0, 1), (3, 0, 1), (2, 1, 2), (3, 1, 2), (1, 1, 1), (2, 0, 2)]

    def copy2(self, j):
        blk, h, to = self.STAGE2[j]
        src = self.comb.at[j - 4] if j >= 4 else self.stage2.at[blk - 1, pl.ds(h * self.hr, self.hr), :]
        return pltpu.make_async_remote_copy(
            src_ref=src, dst_ref=self.land2.at[j], send_sem=self.s2_send.at[j], recv_sem=self.s2_recv.at[j],
            device_id=(*self.chips[to], self.c), device_id_type=MESH)

    def _rows(self, h=None):
        step = self.step
        lo, n = (0, self.shape[0]) if h is None else (h * self.hr, self.hr)
        return [pl.ds(r0, step) for r0 in range(lo, lo + n, step)]

    def load(self, k):
        return pltpu.make_async_copy(self.block(*self.chips[k], self.c), self.mine.at[k], self.ld_sems.at[k])

    def load_mine(self):
        for k in range(4):
            self.load(k).start()

    def phase1(self):
        for k in range(4):
            self.copy1(k).start()

    def phase2(self, k):
        self.copy1(k).wait_recv()
        self.load(k).wait()
        for rs in self._rows():
            s = self.mine[k, rs, :].astype(F32) + self.land1[k, rs, :].astype(F32)
            if k == 0:
                self.out[rs, :] = s
            else:
                self.stage2[k - 1, rs, :] = s.astype(BF16)
        for j in {3: (1, 3), 1: (0,), 2: (2,), 0: ()}[k]:
            self.copy2(j).start()

    def phase2b(self):
        for j, got in ((4, 3), (5, 1)):
            blk, h, _ = self.STAGE2[j]
            self.copy2(got).wait_recv()
            for i, rs in enumerate(self._rows(h)):
                lr = pl.ds(i * self.step, self.step)
                self.comb[j - 4, lr, :] = (self.stage2[blk - 1, rs, :].astype(F32) + self.land2[got, lr, :].astype(F32)).astype(BF16)
            self.copy2(j).start()

    def phase3(self):
        for j in (0, 5, 4, 2):
            self.copy2(j).wait_recv()
        for h, (first, second) in enumerate(((0, 5), (4, 2))):
            for i, rs in enumerate(self._rows(h)):
                lr = pl.ds(i * self.step, self.step)
                self.out[rs, :] = (self.out[rs, :] + self.land2[first, lr, :].astype(F32)) + self.land2[second, lr, :].astype(F32)
        for k in range(4):
            self.copy1(k).wait_send()
        for j in range(6):
            self.copy2(j).wait_send()


def _geom(geom):
    if isinstance(geom, tuple):
        w, row0, rows = geom
        assert w == W_UP
        return w, row0, (rows, SHARD[w][1])
    return geom, 0, SHARD[geom]


N_SCATTER_SCRATCH = 10


def _scatter_out_shapes(geoms):
    return tuple(jax.ShapeDtypeStruct(_geom(g)[2], F32) for g in geoms)


def _scatter_out_specs(geoms):
    return (pl.BlockSpec(memory_space=pltpu.VMEM),) * len(geoms)


def _scatter_scratch(geoms):
    out = []
    for g in geoms:
        s = _geom(g)[2]
        hs = (s[0] // 2, s[1])
        out += [pltpu.VMEM((4,) + s, BF16), pltpu.VMEM((4,) + s, BF16), pltpu.VMEM((3,) + s, BF16), pltpu.VMEM((6,) + hs, BF16),
                pltpu.VMEM((2,) + hs, BF16),
                pltpu.SemaphoreType.DMA((4,)), pltpu.SemaphoreType.DMA((4,)), pltpu.SemaphoreType.DMA((6,)),
                pltpu.SemaphoreType.DMA((6,)), pltpu.SemaphoreType.DMA((4,))]
    return out


class _Scatters:
    def __init__(self, geoms, p_refs, out_refs, scratch):
        k = N_SCATTER_SCRATCH
        self.items = [_Scatter(g, p_refs[i], out_refs[i], *scratch[k * i:k * i + k]) for i, g in enumerate(geoms)]

    def phase1(self, diagonal=False):
        meet = _Meet(diagonal)
        meet.signal()
        for s in self.items:
            s.load_mine()
        meet.wait()
        for s in self.items:
            s.phase1()

    def phase2(self):
        for k in (3, 1, 2, 0):
            for s in self.items:
                s.phase2(k)

    def phase2b(self):
        for s in self.items:
            s.phase2b()

    def phase3(self):
        for s in self.items:
            s.phase3()


PACK_W = 1024


SMALL_FULL = [(2, 8, PACK_W), (HEADS, CHUNK, CHUNK), (2, 8, D_FF)]
SMALL_HALF = [(s[0] // 2,) + s[1:] for s in SMALL_FULL]
N_SMALL_SCRATCH = 16


def _small_scratch():
    n_a = len(SMALL_FULL)
    return ([pltpu.VMEM(SMALL_FULL[0], F32)] + [pltpu.VMEM(s, F32) for s in SMALL_HALF] + [pltpu.VMEM(s, F32) for s in SMALL_HALF]
            + [pltpu.VMEM((3,) + s, F32) for s in SMALL_HALF]
            + [pltpu.SemaphoreType.DMA((n_a,)), pltpu.SemaphoreType.DMA((n_a,)), pltpu.SemaphoreType.DMA((n_a, 3)),
               pltpu.SemaphoreType.DMA((n_a, 3)), pltpu.SemaphoreType.DMA((n_a,)), pltpu.SemaphoreType.DMA((n_a,))])


class _SmallReduce:
    def __init__(self, ins, outs, scratch):
        self.ins, self.outs = ins, outs
        (self.pack, *rest) = scratch
        self.rxs, self.css, self.gs = rest[0:3], rest[3:6], rest[6:9]
        self.s1_send, self.s1_recv, self.s2_send, self.s2_recv, self.s3_send, self.s3_recv = rest[9:]
        self.x, self.y, self.c = lax.axis_index("x"), lax.axis_index("y"), lax.axis_index("c")
        self.sibling = (self.x, self.y, 1 - self.c)
        self.chips = [(1 - self.x, self.y), (self.x, 1 - self.y), (1 - self.x, 1 - self.y)]
        self.hl = [s[0] for s in SMALL_HALF]

    def half(self, ref, a, h):
        return ref.at[pl.ds(h * self.hl[a], self.hl[a])]

    def begin(self):
        dg1_ref, dg2_ref, dgf_ref, dgrn_ref, dlng_ref, dlnb_ref, dbs_ref, loss_ref, dws_ref, dcv_ref = self.ins
        pack, c = self.pack, self.c
        pack[...] = jnp.zeros_like(pack)
        pack[0, 0:1, :] = dg1_ref[0:1, :]
        pack[0, 1:2, :] = dg2_ref[0:1, :]
        pack[0, 2:3, :] = dgf_ref[0:1, :]
        pack[0, 3:4, 0:RET_W] = dgrn_ref[0:1, :]
        lsum = loss_ref[0, 0:1, :]
        for i in range(1, N_TB):
            lsum = lsum + loss_ref[i, 0:1, :]
        pack[0, 3:4, RET_W:RET_W + 128] = lsum
        pack[1, 0:HEADS, 0:128] = dlng_ref[0:HEADS, :]
        pack[1, 0:HEADS, 128:256] = dlnb_ref[0:HEADS, :]
        pack[1, 0:HEADS, 256:384] = dbs_ref[0:HEADS, :]
        self.srcs = [pack, dws_ref, dcv_ref]
        n_a = len(self.srcs)
        self.ex1 = [pltpu.make_async_remote_copy(src_ref=self.half(self.srcs[a], a, 1 - c), dst_ref=self.rxs[a],
                                                 send_sem=self.s1_send.at[a], recv_sem=self.s1_recv.at[a],
                                                 device_id=self.sibling, device_id_type=MESH) for a in range(n_a)]
        for cp in self.ex1:
            cp.start()
        self.ex2 = []
        for a in range(n_a):
            self.ex1[a].wait_recv()
            self.css[a][...] = self.half(self.srcs[a], a, c)[...] + self.rxs[a][...]
            for j, chip in enumerate(self.chips):
                cp = pltpu.make_async_remote_copy(src_ref=self.css[a], dst_ref=self.gs[a].at[j], send_sem=self.s2_send.at[a, j],
                                                  recv_sem=self.s2_recv.at[a, j], device_id=(*chip, c), device_id_type=MESH)
                cp.start()
                self.ex2.append(cp)

    def end(self):
        c, x, y = self.c, self.x, self.y
        ex3 = []
        for a in range(len(self.srcs)):
            css, gs, out = self.css[a], self.gs[a], self.outs[a]
            for j in range(3):
                self.ex2[3 * a + j].wait_recv()
            tot = None
            for q in range(4):
                k = jnp.where(x != (q >> 1), 1, 0) + jnp.where(y != (q & 1), 2, 0)
                term = jnp.where(k == 0, css[...], jnp.where(k == 1, gs[0], jnp.where(k == 2, gs[1], gs[2])))
                tot = term if tot is None else tot + term
            self.half(out, a, c)[...] = tot
            cp = pltpu.make_async_remote_copy(src_ref=self.half(out, a, c), dst_ref=self.half(out, a, c), send_sem=self.s3_send.at[a],
                                              recv_sem=self.s3_recv.at[a], device_id=self.sibling, device_id_type=MESH)
            cp.start()
            ex3.append(cp)
        for a in range(len(self.srcs)):
            out = self.outs[a]
            pltpu.make_async_remote_copy(src_ref=self.half(out, a, 1 - c), dst_ref=self.half(out, a, 1 - c), send_sem=self.s3_send.at[a],
                                         recv_sem=self.s3_recv.at[a], device_id=self.sibling, device_id_type=MESH).wait_recv()
        for cp in self.ex1 + self.ex2 + ex3:
            cp.wait_send()


def _adam_math(w, g, m, v):
    nm = ADAM_B1 * m + (1.0 - ADAM_B1) * g
    nv = ADAM_B2 * v + (1.0 - ADAM_B2) * (g * g)
    d = -ADAM_LR * ((nm / (1.0 - ADAM_B1 ** ADAM_STEP)) / (jnp.sqrt(nv / (1.0 - ADAM_B2 ** ADAM_STEP)) + ADAM_EPS) + ADAM_WD * w)
    return d, nm, nv


def _adamw(name, w, gs, m, v, rows, thru=()):
    _, r, cdim = w.shape
    n_steps = r // rows
    half = gs[0].shape[0] // rows
    n_g, n_t = len(gs), len(thru)

    def body(w_ref, *rest):
        g_refs, (m_ref, v_ref), t_refs = rest[:n_g], rest[n_g:n_g + 2], rest[n_g + 2:n_g + 2 + n_t]
        go_ref, d_ref, nm_ref, nv_ref, *to_refs = rest[n_g + 2 + n_t:]
        gg = g_refs[0][...]
        if n_g == 2:
            gg = jnp.where(pl.program_id(0) < half, gg, g_refs[1][...])
        go_ref[0] = gg
        d, nm, nv = _adam_math(w_ref[0], gg, m_ref[0], v_ref[0])
        d_ref[0], nm_ref[0], nv_ref[0] = d, nm, nv
        for t_ref, to_ref in zip(t_refs, to_refs):
            to_ref[...] = t_ref[...]

    spec3 = pl.BlockSpec((1, rows, cdim), lambda i: (0, i, 0))
    if n_g == 1:
        g_specs = [pl.BlockSpec((rows, cdim), lambda i: (i, 0))]
    else:
        g_specs = [pl.BlockSpec((rows, cdim), lambda i: (jnp.minimum(i, half - 1), 0)),
                   pl.BlockSpec((rows, cdim), lambda i: (jnp.maximum(i - half, 0), 0))]
    t_specs = [pl.BlockSpec((t.shape[0] // n_steps, t.shape[1]), lambda i: (i, 0)) for t in thru]
    sh = jax.ShapeDtypeStruct((1, r, cdim), F32)
    return pl.pallas_call(
        body, name=name, grid=(n_steps,), out_shape=(sh, sh, sh, sh) + tuple(jax.ShapeDtypeStruct(t.shape, t.dtype) for t in thru),
        in_specs=[spec3] + g_specs + [spec3, spec3] + t_specs, out_specs=(spec3,) * 4 + tuple(t_specs),
        compiler_params=_cparams(("parallel",)),
    )(w, *gs, m, v, *thru)


def _adamw_small(rp, rws, rcv, gcw, params):
    n_p = len(params)

    def body(*refs):
        rp_ref, rws_ref, rcv_ref, gcw_ref = refs[:4]
        ins = refs[4:4 + 3 * n_p]
        outs = refs[4 + 3 * n_p:]
        outs[4 * n_p][...] = rp_ref[0, 3:4, RET_W:RET_W + 1]
        grads = [rp_ref[0, 0:1, :], rp_ref[0, 1:2, :], rp_ref[0, 2:3, :], rp_ref[0, 3:4, 0:RET_W],
                 rp_ref[1, 0:HEADS, 0:128], rp_ref[1, 0:HEADS, 128:256], rp_ref[1, 0:HEADS, 256:384],
                 rws_ref[...], gcw_ref[...], None]
        for p in range(n_p):
            w_ref, m_ref, v_ref = ins[3 * p:3 * p + 3]
            o = outs[4 * p:4 * p + 4]
            if p == n_p - 1:
                for hf in range(2):
                    cs = slice(hf * D_FF, (hf + 1) * D_FF)
                    g = rcv_ref[hf, 3:4, :]
                    res = (g,) + _adam_math(w_ref[:, cs], g, m_ref[:, cs], v_ref[:, cs])
                    for t in range(4):
                        o[t][:, cs] = res[t]
                continue
            lead = w_ref.ndim > grads[p].ndim
            rd = (lambda r: r[0]) if lead else (lambda r: r[...])
            res = (grads[p],) + _adam_math(rd(w_ref), grads[p], rd(m_ref), rd(v_ref))
            for t in range(4):
                if lead:
                    o[t][0] = res[t]
                else:
                    o[t][...] = res[t]

    vm = pl.BlockSpec(memory_space=pltpu.VMEM)
    flat = [a for tr in params for a in tr]
    out_shape = tuple(jax.ShapeDtypeStruct(tr[0].shape, F32) for tr in params for _ in range(4)) + (jax.ShapeDtypeStruct((1, 1), F32),)
    res = pl.pallas_call(
        body, name="adamw_small", out_shape=out_shape, in_specs=[vm] * (4 + len(flat)), out_specs=(vm,) * len(out_shape),
        compiler_params=_cparams(),
    )(rp, rws, rcv, gcw, *flat)
    return [res[4 * p:4 * p + 4] for p in range(n_p)], res[4 * n_p]


def kernel(x, mix_norm_g, w_in, ret_norm_g, sgu_ln_g, sgu_ln_b, sgu_w_s, sgu_b_s, w_out, ffn_norm_g, w_up, conv_w, conv_b, w_down, final_norm_g, loss_target, m_mix_norm_g, m_w_in, m_ret_norm_g, m_sgu_ln_g, m_sgu_ln_b, m_sgu_w_s, m_sgu_b_s, m_w_out, m_ffn_norm_g, m_w_up, m_conv_w, m_conv_b, m_w_down, m_final_norm_g, v_mix_norm_g, v_w_in, v_ret_norm_g, v_sgu_ln_g, v_sgu_ln_b, v_sgu_w_s, v_sgu_b_s, v_w_out, v_ffn_norm_g, v_w_up, v_conv_w, v_conv_b, v_w_down, v_final_norm_g):
    xs = x[0]
    tgt = loss_target[0]
    mask, qdec, kdec = _decay_tables()
    grn = ret_norm_g.reshape(1, RET_W)
    lng = sgu_ln_g.reshape(1, SGU_W)
    lnb = sgu_ln_b.reshape(1, SGU_W)
    ws = sgu_w_s[0]
    bsb = jnp.broadcast_to(sgu_b_s[0][:, :, None], (HEADS, CHUNK, HEAD_DIM))
    gf = final_norm_g.reshape(1, D_MODEL)
    me = 4 * lax.axis_index("x") + 2 * lax.axis_index("y") + lax.axis_index("c")
    tr = lambda a: jnp.transpose(a[0])[None]
    tr_cw = lambda a: jnp.transpose(a, (1, 0, 2))

    proj, h1, cos2, sin2, win_g, cw_sh, wout_g, wdn_g, su = _fwd_proj(
        xs, mix_norm_g, _rope_freq(), w_in[0], w_out[0], tr(w_up)[0], w_down[0], tr_cw(conv_w))
    cw_g = jnp.transpose(cw_sh, (1, 0, 2)).reshape(8, 2 * D_FF)
    x2, mixcat, o, sprev, wup_g = _fwd_mix(xs, proj, wout_g, grn, lng, lnb, ws, bsb, mask, qdec, kdec, su)
    h2, up_pre, u_conv, act, x3, loss_parts = _fwd_ffn(x2, ffn_norm_g, wup_g, cw_g, conv_b, wdn_g, gf, tgt)

    dx3, dpre, dx2, dgf, dg2, dcv = _bwd_ffn(x3, tgt, gf, x2, ffn_norm_g, up_pre, u_conv, wup_g, cw_g, wdn_g)
    band = 512
    (gdn_p,) = _wgrad("wgrad_down", act, dx3, tm=FF_TILE)
    (gout_p,) = _wgrad("wgrad_out", mixcat, dx2, tn=512)
    gup_p, g_dn = _wgrad("wgrad_up", dpre, h2, tm=FF_TILE, tn=512, hosted=[(W_DOWN, gdn_p)])
    dproj, dgrn, dlng, dlnb, dws, dbs, g_up_a, g_out = _bwd_mix(
        dx2, proj, o, sprev, wout_g, grn, lng, lnb, ws, bsb, mask, qdec, kdec, cos2, sin2,
        [((W_UP, 0, band), gup_p), (W_OUT, gout_p)])
    gin_p, g_up_b = _wgrad("wgrad_in", h1, dproj, tm=512, tn=768, hosted=[((W_UP, band, FF_SHARD - band), gup_p)])
    grad_x, g_in, rp, rws, rcv = _bwd_proj(dproj, win_g, xs, mix_norm_g, dx2, gin_p,
                                           (dg2, dgf, dgrn, dlng, dlnb, dbs, loss_parts, dws, dcv))
    gcw = tr_cw(lax.dynamic_slice(rcv, (me // (N_DEV // 2), 0, (me % (N_DEV // 2)) * FF_SHARD), (1, 3, FF_SHARD)))

    table = {}
    *table["w_in"], grad_x = _adamw("adamw_w_in", w_in, [g_in], m_w_in, v_w_in, 256, thru=[grad_x])
    for name, w, gs, m, v, rows in (("w_out", w_out, [g_out], m_w_out, v_w_out, 128),
                                    ("w_up", tr(w_up), [g_up_a, g_up_b], tr(m_w_up), tr(v_w_up), 64),
                                    ("w_down", w_down, [g_dn], m_w_down, v_w_down, 88)):
        table[name] = _adamw("adamw_" + name, w, gs, m, v, rows)
    table["w_up"] = tuple(tr(a) for a in table["w_up"])
    row = lambda a: a.reshape(1, D_MODEL)
    names_small = ["mix_norm_g", "ffn_norm_g", "final_norm_g", "ret_norm_g", "sgu_ln_g", "sgu_ln_b", "sgu_b_s", "sgu_w_s",
                   "conv_w", "conv_b"]
    params = [(mix_norm_g, m_mix_norm_g, v_mix_norm_g), (ffn_norm_g, m_ffn_norm_g, v_ffn_norm_g),
              (row(final_norm_g), row(m_final_norm_g), row(v_final_norm_g)), (ret_norm_g, m_ret_norm_g, v_ret_norm_g),
              (sgu_ln_g, m_sgu_ln_g, v_sgu_ln_g), (sgu_ln_b, m_sgu_ln_b, v_sgu_ln_b), (sgu_b_s, m_sgu_b_s, v_sgu_b_s),
              (sgu_w_s, m_sgu_w_s, v_sgu_w_s), (tr_cw(conv_w), tr_cw(m_conv_w), tr_cw(v_conv_w)), (conv_b, m_conv_b, v_conv_b)]
    small, loss = _adamw_small(rp, rws, rcv, gcw, params)
    for n, res in zip(names_small, small):
        table[n] = res
    table["final_norm_g"] = tuple(a.reshape(D_MODEL) for a in table["final_norm_g"])
    table["conv_w"] = tuple(tr_cw(a) for a in table["conv_w"])

    order = ["mix_norm_g", "w_in", "ret_norm_g", "sgu_ln_g", "sgu_ln_b", "sgu_w_s", "sgu_b_s", "w_out", "ffn_norm_g", "w_up",
             "conv_w", "conv_b", "w_down", "final_norm_g"]
    outs = [loss.reshape(()), grad_x[None]]
    for col in range(4):
        outs += [table[n][col] for n in order]
    return tuple(outs)
```

```python
import functools
import math

import jax
import jax.numpy as jnp
import numpy as np
from jax import lax
from jax.experimental import pallas as pl
from jax.experimental.pallas import tpu as pltpu

F32 = jnp.float32
BF16 = jnp.bfloat16
MESH = pl.DeviceIdType.MESH

N_DEV = 8
SEQ = 2048
D_MODEL = 1024
CHUNK = 128
N_CHUNK = SEQ // CHUNK
HEADS = 4
HEAD_DIM = 128
RET_W = 512
SGU_W = 512
PROJ_W = 3072
D_FF = 2816
FF_SHARD = 704
FF_TILE = 1408
FF_TILES = ((0, 1536), (1536, 1280))
IN_SHARD = PROJ_W // N_DEV
OUT_SHARD = D_MODEL // N_DEV
DOWN_SHARD = D_FF // N_DEV
TM = 256
N_TB = SEQ // TM
FWD_MIX_PASS_AT = 10
EPS = 1e-6
ROPE_BASE = 10000.0
K_SCALE = HEAD_DIM ** -0.5
INV_SQRT2 = 0.7071067811865476
INV_SQRT_2PI = 0.3989422804014327

ADAM_LR = 0.001
ADAM_B1 = 0.9
ADAM_B2 = 0.999
ADAM_EPS = 1e-08
ADAM_WD = 0.01
ADAM_STEP = 10

VMEM_LIMIT = 56 * 1024 * 1024


def _cparams(sem=None, vmem=VMEM_LIMIT, collective=None):
    return pltpu.CompilerParams(dimension_semantics=sem, vmem_limit_bytes=vmem, collective_id=collective)


COLLECTIVE = {name: k for k, name in enumerate(("fwd_proj", "fwd_mix", "wgrad_up", "bwd_mix", "wgrad_in", "bwd_proj"))}


class _Meet:
    def __init__(self, diagonal):
        x, y, c = lax.axis_index("x"), lax.axis_index("y"), lax.axis_index("c")
        self.peers = [(x, y, 1 - c), (1 - x, y, c), (x, 1 - y, c)] + ([(1 - x, 1 - y, c)] if diagonal else [])

    def signal(self):
        for peer in self.peers:
            pl.semaphore_signal(pltpu.get_barrier_semaphore(), inc=1, device_id=peer, device_id_type=MESH)

    def wait(self):
        pl.semaphore_wait(pltpu.get_barrier_semaphore(), len(self.peers))


def _resident(shape):
    nd = len(shape)
    return pl.BlockSpec(shape, lambda *_: (0,) * nd, pipeline_mode=pl.Buffered(1))


def _dot(a, b):
    return jnp.dot(a, b, preferred_element_type=F32)


def _dot_nt(a, b):
    return lax.dot_general(a, b, (((1,), (1,)), ((), ())), preferred_element_type=F32)


def _dot_tn(a, b):
    return lax.dot_general(a, b, (((0,), (0,)), ((), ())), preferred_element_type=F32)


def _sigmoid(x):
    return 1.0 / (1.0 + jnp.exp(-x))


def _gelu(x):
    return 0.5 * x * (1.0 + lax.erf(x * INV_SQRT2))


def _gelu_grad(x):
    return 0.5 * (1.0 + lax.erf(x * INV_SQRT2)) + x * (jnp.exp(-0.5 * x * x) * INV_SQRT_2PI)


def _rot(xh, cos2, sin2):
    return xh * cos2 + pltpu.roll(xh, HEAD_DIM // 2, 1) * sin2


def _rot_t(dh, cos2, sin2):
    return dh * cos2 + pltpu.roll(dh * sin2, HEAD_DIM // 2, 1)


def _rope_freq():
    half = HEAD_DIM // 2
    inv_freq = jnp.power(ROPE_BASE, -jnp.arange(half, dtype=F32) / half)
    return jnp.concatenate([inv_freq, inv_freq])[None, :]


def _rope_block(inv2, first_row, rows):
    pos = (lax.broadcasted_iota(jnp.int32, (rows, HEAD_DIM), 0) + first_row).astype(F32)
    ang = pos * inv2
    sin = jnp.sin(ang)
    lane = lax.broadcasted_iota(jnp.int32, (rows, HEAD_DIM), 1)
    return jnp.cos(ang), jnp.where(lane < HEAD_DIM // 2, -sin, sin)


def _decay_tables():
    log_gamma = jnp.log(1.0 - jnp.power(2.0, -5.0 - jnp.arange(HEADS, dtype=F32)))
    pos = jnp.arange(CHUNK, dtype=F32)
    diff = pos[:, None] - pos[None, :]
    mask = jnp.where(diff >= 0.0, jnp.exp(log_gamma[:, None, None] * jnp.maximum(diff, 0.0)[None]), 0.0)
    k_decay = jnp.exp(log_gamma[:, None] * (CHUNK - 1.0 - pos)[None])
    q_decay = jnp.exp(log_gamma[:, None] * (pos + 1.0)[None])
    kd = jnp.broadcast_to(k_decay[:, :, None], (HEADS, CHUNK, HEAD_DIM))
    qd = jnp.broadcast_to(q_decay[:, :, None], (HEADS, CHUNK, HEAD_DIM))
    return mask.astype(F32), qd.astype(F32), kd.astype(F32)


def _chunk_decay():
    lg = np.log(np.float32(1.0) - np.power(np.float32(2.0), -5.0 - np.arange(HEADS, dtype=np.float32))).astype(np.float32)
    return [float(np.exp(lg[h] * np.float32(CHUNK))) for h in range(HEADS)]


W_IN, W_OUT, W_UP, W_DOWN, W_CONV = range(5)
GATHERED = {W_IN: ((D_MODEL, PROJ_W), BF16), W_OUT: ((D_MODEL, D_MODEL), BF16), W_UP: ((2 * D_FF, D_MODEL), BF16),
            W_DOWN: ((D_FF, D_MODEL), BF16), W_CONV: ((N_DEV, 8, FF_SHARD), F32)}
SHARD = {W_IN: (D_MODEL, IN_SHARD), W_OUT: (OUT_SHARD, D_MODEL), W_UP: (FF_SHARD, D_MODEL), W_DOWN: (DOWN_SHARD, D_MODEL),
         W_CONV: (8, FF_SHARD)}


class _Gather:
    N_SEMS = 9

    def __init__(self, ids, stages, gathered, send_sems, recv_sems, local_sems):
        self.ids, self.stages, self.gathered = ids, stages, gathered
        self.send_sems, self.recv_sems, self.local_sems = send_sems, recv_sems, local_sems
        self.x, self.y, self.c = lax.axis_index("x"), lax.axis_index("y"), lax.axis_index("c")
        self.me = (self.x, self.y, self.c)
        self.sibling = (self.x, self.y, 1 - self.c)
        self.chips = [(1 - self.x, self.y), (self.x, 1 - self.y), (1 - self.x, 1 - self.y)]

    def slot(self, n, px, py, pc):
        dev = 4 * px + 2 * py + pc
        w, g = self.ids[n], self.gathered[n]
        if w == W_IN:
            return g.at[:, pl.ds(pl.multiple_of(dev * IN_SHARD, 128), IN_SHARD)]
        if w == W_OUT:
            return g.at[pl.ds(pl.multiple_of(dev * OUT_SHARD, 128), OUT_SHARD), :]
        if w == W_DOWN:
            return g.at[pl.ds(pl.multiple_of(dev * DOWN_SHARD, 32), DOWN_SHARD), :]
        if w == W_UP:
            return g.at[pl.ds(pl.multiple_of(dev * FF_SHARD, 32), FF_SHARD), :]
        return g.at[dev]

    def half(self, n, px, py, pc, h):
        dev = 4 * px + 2 * py + pc
        w, g = self.ids[n], self.gathered[n]
        if w == W_IN:
            return g.at[pl.ds(h * (D_MODEL // 2), D_MODEL // 2), pl.ds(pl.multiple_of(dev * IN_SHARD, 128), IN_SHARD)]
        rows = SHARD[w][0] // 2
        return g.at[pl.ds(pl.multiple_of(dev * SHARD[w][0] + h * rows, 16), rows), :]

    def tree(self, n):
        return self.ids[n] != W_CONV

    def copy(self, n, k, block, to, src=None, h=None):
        ref = self.slot(n, *block) if h is None else self.half(n, *block, h)
        return pltpu.make_async_remote_copy(
            src_ref=ref if src is None else src, dst_ref=ref,
            send_sem=self.send_sems.at[n, k], recv_sem=self.recv_sems.at[n, k], device_id=to, device_id_type=MESH)

    def _mine(self):
        return [pltpu.make_async_copy(self.stages[n], self.slot(n, *self.me), self.local_sems.at[n]) for n in range(len(self.ids))]

    def _first(self):
        out = []
        for n in range(len(self.ids)):
            out.append(self.copy(n, 0, self.me, self.sibling, src=self.stages[n]))
            out += [self.copy(n, 1 + j, self.me, (*chip, self.c), src=self.stages[n])
                    for j, chip in enumerate(self.chips[:2] if self.tree(n) else self.chips)]
        return out

    def start(self):
        for cp in self._mine() + self._first():
            cp.start()

    def _passed(self, j):
        dev = (*self.chips[j], self.c)
        out = []
        for n in range(len(self.ids)):
            if not self.tree(n):
                out.append(self.copy(n, 4 + j, dev, self.sibling))
            elif j < 2:
                out += [self.copy(n, 3 + j, dev, (*self.chips[1 - j], self.c), h=j), self.copy(n, 5 + j, dev, self.sibling)]
            else:
                out += [self.copy(n, 7, dev, self.sibling, h=0), self.copy(n, 8, dev, self.sibling, h=1)]
        return out

    def near(self):
        for j in range(2):
            dev = (*self.chips[j], self.c)
            for n in range(len(self.ids)):
                self.copy(n, 1 + j, dev, self.me).wait_recv()
            for cp in self._passed(j):
                cp.start()

    def home(self):
        for n in range(len(self.ids)):
            self.copy(n, 0, self.sibling, self.me).wait_recv()
        for cp in self._mine():
            cp.wait()

    def mid(self):
        for n in range(len(self.ids)):
            for j, chip in enumerate(self.chips[:2]):
                self.copy(n, (5 if self.tree(n) else 4) + j, (*chip, 1 - self.c), self.me).wait_recv()

    def far(self):
        dev, other = (*self.chips[2], self.c), (*self.chips[2], 1 - self.c)
        for n in range(len(self.ids)):
            if self.tree(n):
                self.copy(n, 3, dev, self.me, h=0).wait_recv()
                self.copy(n, 4, dev, self.me, h=1).wait_recv()
            else:
                self.copy(n, 3, dev, self.me).wait_recv()
        for cp in self._passed(2):
            cp.start()
        for n in range(len(self.ids)):
            if self.tree(n):
                self.copy(n, 7, other, self.me, h=0).wait_recv()
                self.copy(n, 8, other, self.me, h=1).wait_recv()
            else:
                self.copy(n, 6, other, self.me).wait_recv()
        for cp in self._first() + self._passed(0) + self._passed(1) + self._passed(2):
            cp.wait_send()

    def finish(self):
        self.home()
        self.mid()
        self.far()


def _gather_scratch(n):
    return [pltpu.SemaphoreType.DMA((n, _Gather.N_SEMS)), pltpu.SemaphoreType.DMA((n, _Gather.N_SEMS)), pltpu.SemaphoreType.DMA((n,))]


def _gathered_shapes(ids):
    return tuple(jax.ShapeDtypeStruct(*GATHERED[w]) for w in ids)


def _fwd_proj(x, g1, inv2, w_in, w_out, w_up, w_down, conv_w):
    ids_a, ids_b = [W_IN, W_CONV], [W_OUT, W_DOWN]
    tp = 2 * TM
    n_tp = SEQ // tp
    cols = 2 * IN_SHARD
    blocks = cols // HEAD_DIM
    xi, yi = lax.axis_index("x"), lax.axis_index("y")
    order = jnp.stack([2 * xi + yi, 2 * (1 - xi) + yi, 2 * xi + (1 - yi), 2 * (1 - xi) + (1 - yi)]).astype(jnp.int32)

    def body(order_ref, x_ref, g_ref, inv_ref, in_hbm, out_hbm, up_hbm, dn_hbm, cw_ref,
             proj_ref, h1_ref, cos_ref, sin_ref, gin, gcw, gout, gdn, su_ref,
             h_all, cos_all, sin_all, pr_vm, w_vm, s_in, s_cw, s_out, s_dn, f_in, f_out, f_up, f_dn, ld_sems,
             a_send, a_recv, a_local, b_send, b_recv, b_local):
        ag_a = _Gather(ids_a, [s_in, s_cw], [gin, gcw], a_send, a_recv, a_local)
        ag_b = _Gather(ids_b, [s_out, s_dn], [gout, gdn], b_send, b_recv, b_local)
        p, i = pl.program_id(0), pl.program_id(1)
        chip = order_ref[p]

        def fill():
            cp = pltpu.make_async_copy(gin.at[:, pl.ds(pl.multiple_of(chip * cols, 128), cols)], w_vm, ld_sems.at[4])
            cp.start()
            cp.wait()

        @pl.when((p == 0) & (i == 0))
        def _():
            meet = _Meet(diagonal=True)
            meet.signal()
            loads = [pltpu.make_async_copy(src, dst, ld_sems.at[i])
                     for i, (src, dst) in enumerate(((in_hbm, f_in), (out_hbm, f_out), (dn_hbm, f_dn), (up_hbm, f_up)))]
            for cp in loads:
                cp.start()
            s_cw[...] = jnp.zeros_like(s_cw)
            for k in range(3):
                s_cw[k:k + 1, :] = cw_ref[k]
            loads[0].wait()
            s_in[...] = f_in[...].astype(BF16)
            meet.wait()
            ag_a.start()
            loads[1].wait()
            s_out[...] = f_out[...].astype(BF16)
            loads[2].wait()
            s_dn[...] = f_dn[...].astype(BF16)
            loads[3].wait()
            su_ref[...] = f_up[...].astype(BF16)
            ag_a.home()
            fill()

        @pl.when((p == 1) & (i == 0))
        def _():
            ag_a.near()
            ag_b.start()
            ag_a.mid()
            fill()

        pl.when((p == 2) & (i == 0))(fill)

        @pl.when((p == 3) & (i == 0))
        def _():
            ag_a.far()
            fill()

        rows = pl.ds(pl.multiple_of(i * tp, tp), tp)

        @pl.when(p == 0)
        def _():
            xb = x_ref[...]
            r = lax.rsqrt(jnp.mean(xb * xb, axis=-1, keepdims=True) + EPS)
            h = ((xb * r) * g_ref[...]).astype(BF16)
            c2, s2 = _rope_block(inv_ref[...], i * tp, tp)
            h_all[rows, :], cos_all[rows, :], sin_all[rows, :] = h, c2, s2
            h1_ref[...], cos_ref[...], sin_ref[...] = h, c2, s2

        pr_vm[...] = _dot(h_all[rows, :], w_vm[...])
        c2, s2 = cos_all[rows, :], sin_all[rows, :]
        blk = lambda j: pr_vm[:, j * HEAD_DIM:(j + 1) * HEAD_DIM]

        def store(first_k, first_plain):
            for j in range(blocks):
                sl = slice(j * HEAD_DIM, (j + 1) * HEAD_DIM)
                if j < first_k:
                    proj_ref[:, sl] = _rot(blk(j), c2, s2)
                elif j < first_plain:
                    proj_ref[:, sl] = _rot(blk(j), c2, s2) * K_SCALE
                else:
                    proj_ref[:, sl] = blk(j)

        pl.when(chip == 0)(functools.partial(store, HEADS, blocks))
        pl.when(chip == 1)(functools.partial(store, 0, 2 * HEADS - blocks))
        pl.when(chip >= 2)(functools.partial(store, 0, 0))

        @pl.when((p == 3) & (i == n_tp - 1))
        def _():
            ag_b.near()
            ag_b.finish()

    once = lambda w: pl.BlockSpec((tp, w), lambda p, i, o: (jnp.where(p == 0, i, n_tp - 1), 0))
    hbm = pl.BlockSpec(memory_space=pl.ANY)
    vm = pl.BlockSpec(memory_space=pltpu.VMEM)
    return pl.pallas_call(
        body, name="fwd_proj",
        grid_spec=pltpu.PrefetchScalarGridSpec(
            num_scalar_prefetch=1, grid=(4, n_tp),
            in_specs=[once(D_MODEL), _resident((1, D_MODEL)), _resident((1, HEAD_DIM)), hbm, hbm, hbm, hbm, vm],
            out_specs=(pl.BlockSpec((tp, cols), lambda p, i, o: (i, o[p])), once(D_MODEL), once(HEAD_DIM), once(HEAD_DIM),
                       hbm, hbm, hbm, hbm, vm),
            scratch_shapes=[pltpu.VMEM((SEQ, D_MODEL), BF16), pltpu.VMEM((SEQ, HEAD_DIM), F32), pltpu.VMEM((SEQ, HEAD_DIM), F32),
                            pltpu.VMEM((tp, cols), F32),
                            pltpu.VMEM((D_MODEL, cols), BF16), pltpu.VMEM(SHARD[W_IN], BF16), pltpu.VMEM(SHARD[W_CONV], F32),
                            pltpu.VMEM(SHARD[W_OUT], BF16), pltpu.VMEM(SHARD[W_DOWN], BF16),
                            pltpu.VMEM(SHARD[W_IN], F32), pltpu.VMEM(SHARD[W_OUT], F32), pltpu.VMEM(SHARD[W_UP], F32),
                            pltpu.VMEM(SHARD[W_DOWN], F32), pltpu.SemaphoreType.DMA((5,))]
            + _gather_scratch(len(ids_a)) + _gather_scratch(len(ids_b))),
        out_shape=(jax.ShapeDtypeStruct((SEQ, PROJ_W), F32), jax.ShapeDtypeStruct((SEQ, D_MODEL), BF16),
                   jax.ShapeDtypeStruct((SEQ, HEAD_DIM), F32), jax.ShapeDtypeStruct((SEQ, HEAD_DIM), F32))
        + _gathered_shapes(ids_a + ids_b) + (jax.ShapeDtypeStruct(SHARD[W_UP], BF16),),
        compiler_params=_cparams(("arbitrary", "arbitrary"), collective=COLLECTIVE["fwd_proj"]),
    )(order, x, g1, inv2, w_in, w_out, w_up, w_down, conv_w)


def _causal(w):
    r = lax.broadcasted_iota(jnp.int32, (CHUNK, CHUNK), 0)
    c = lax.broadcasted_iota(jnp.int32, (CHUNK, CHUNK), 1)
    return jnp.where(r >= c, w, 0.0)


def _fwd_mix(x, proj, wout_g, grn, lng, lnb, ws, bsb, mask, qdec, kdec, su):
    cdec = _chunk_decay()
    ids = [W_UP]

    def body(x_ref, p_ref, w_ref, grn_ref, lng_ref, lnb_ref, ws_ref, bsb_ref, m_ref, qd_ref, kd_ref, su_ref,
             x2_ref, cat_ref, o_ref, sp_ref, gup, state, send_sems, recv_sems, local_sems):
        ag = _Gather(ids, [su_ref], [gup], send_sems, recv_sems, local_sems)

        @pl.when(pl.program_id(0) == 0)
        def _():
            meet = _Meet(diagonal=False)
            meet.signal()
            state[...] = jnp.zeros_like(state)
            meet.wait()
            ag.start()

        for h in range(HEADS):
            sl = slice(h * HEAD_DIM, (h + 1) * HEAD_DIM)
            q = p_ref[:, sl]
            k = p_ref[:, RET_W + h * HEAD_DIM:RET_W + (h + 1) * HEAD_DIM]
            v = p_ref[:, 2 * RET_W + h * HEAD_DIM:2 * RET_W + (h + 1) * HEAD_DIM]
            g = p_ref[:, 3 * RET_W + h * HEAD_DIM:3 * RET_W + (h + 1) * HEAD_DIM]
            qb, kb, vb = q.astype(BF16), k.astype(BF16), v.astype(BF16)
            a = _dot_nt(qb, kb) * m_ref[h]
            spb = state[h].astype(BF16)
            sp_ref[0, h] = spb
            o = _dot(a.astype(BF16), vb) + _dot((q * qd_ref[h]).astype(BF16), spb)
            state[h] = state[h] * cdec[h] + _dot_tn((k * kd_ref[h]).astype(BF16), vb)
            o_ref[:, sl] = o
            rinv = lax.rsqrt(jnp.mean(o * o, axis=-1, keepdims=True) + EPS)
            rn = (o * rinv) * grn_ref[:, sl]
            cat_ref[:, sl] = ((g * _sigmoid(g)) * rn).astype(BF16)
        for gi in range(HEADS):
            sl = slice(gi * HEAD_DIM, (gi + 1) * HEAD_DIM)
            u = p_ref[:, 4 * RET_W + gi * HEAD_DIM:4 * RET_W + (gi + 1) * HEAD_DIM]
            sv = p_ref[:, 4 * RET_W + SGU_W + gi * HEAD_DIM:4 * RET_W + SGU_W + (gi + 1) * HEAD_DIM]
            gv = _gelu(sv)
            xc = gv - jnp.mean(gv, axis=-1, keepdims=True)
            vn = (xc * lax.rsqrt(jnp.mean(xc * xc, axis=-1, keepdims=True) + EPS)) * lng_ref[:, sl] + lnb_ref[:, sl]
            mixed = _dot(_causal(ws_ref[gi]).astype(BF16), vn.astype(BF16)) + bsb_ref[gi]
            cat_ref[:, RET_W + gi * HEAD_DIM:RET_W + (gi + 1) * HEAD_DIM] = (_gelu(u) * mixed).astype(BF16)
        x2_ref[...] = x_ref[...] + _dot(cat_ref[...], w_ref[...])

        pl.when(pl.program_id(0) == FWD_MIX_PASS_AT)(ag.near)
        pl.when(pl.program_id(0) == N_CHUNK - 1)(ag.finish)

    ch = lambda w: pl.BlockSpec((CHUNK, w), lambda i: (i, 0))
    hcc = (HEADS, CHUNK, CHUNK)
    hbm = pl.BlockSpec(memory_space=pl.ANY)
    return pl.pallas_call(
        body, name="fwd_mix", grid=(N_CHUNK,),
        out_shape=(jax.ShapeDtypeStruct((SEQ, D_MODEL), F32), jax.ShapeDtypeStruct((SEQ, D_MODEL), BF16),
                   jax.ShapeDtypeStruct((SEQ, RET_W), F32), jax.ShapeDtypeStruct((N_CHUNK, HEADS, HEAD_DIM, HEAD_DIM), BF16))
        + _gathered_shapes(ids),
        in_specs=[ch(D_MODEL), ch(PROJ_W), _resident((D_MODEL, D_MODEL)), _resident((1, RET_W)), _resident((1, SGU_W)),
                  _resident((1, SGU_W)), _resident(hcc), _resident(hcc), _resident(hcc), _resident(hcc), _resident(hcc), hbm],
        out_specs=(ch(D_MODEL), ch(D_MODEL), ch(RET_W), pl.BlockSpec((1, HEADS, HEAD_DIM, HEAD_DIM), lambda i: (i, 0, 0, 0)), hbm),
        scratch_shapes=[pltpu.VMEM((HEADS, HEAD_DIM, HEAD_DIM), F32)] + _gather_scratch(len(ids)),
        compiler_params=_cparams(("arbitrary",), collective=COLLECTIVE["fwd_mix"]),
    )(x, proj, wout_g, grn, lng, lnb, ws, bsb, mask, qdec, kdec, su)


def _conv_taps(p, prev8):
    row = lax.broadcasted_iota(jnp.int32, p.shape, 0)
    p1 = jnp.where(row == 0, prev8[7:8, :], pltpu.roll(p, 1, 0))
    p2 = jnp.where(row == 0, prev8[6:7, :], jnp.where(row == 1, prev8[7:8, :], pltpu.roll(p, 2, 0)))
    return p1, p2


def _fwd_ffn(x2, g2, wup_g, cw_g, cb_g, wdn_g, gf, tgt):
    def body(x_ref, g_ref, wu_ref, cw_ref, cb_ref, wd_ref, gf_ref, t_ref, h2_ref, up_ref, u_ref, act_ref, x3_ref, loss_ref, carry):
        @pl.when(pl.program_id(0) == 0)
        def _():
            carry[...] = jnp.zeros_like(carry)

        xb = x_ref[...]
        r = lax.rsqrt(jnp.mean(xb * xb, axis=-1, keepdims=True) + EPS)
        h = ((xb * r) * g_ref[...]).astype(BF16)
        h2_ref[...] = h
        acc = xb
        for t0, tw in FF_TILES:
            u = []
            for c0 in (t0, D_FF + t0):
                cs = slice(c0, c0 + tw)
                p = _dot_nt(h, wu_ref[pl.ds(c0, tw), :])
                up_ref[:, cs] = p.astype(BF16)
                p1, p2 = _conv_taps(p, carry[:, cs])
                carry[:, cs] = p[TM - 8:, :]
                us = p2 * cw_ref[0:1, cs] + p1 * cw_ref[1:2, cs] + p * cw_ref[2:3, cs] + cb_ref[:, cs]
                u_ref[:, cs] = us.astype(BF16)
                u.append(us)
            a = ((u[0] * _sigmoid(u[0])) * u[1]).astype(BF16)
            act_ref[:, t0:t0 + tw] = a
            acc = acc + _dot(a, wd_ref[pl.ds(t0, tw), :])
        x3_ref[...] = acc
        r3 = lax.rsqrt(jnp.mean(acc * acc, axis=-1, keepdims=True) + EPS)
        diff = (acc * r3) * gf_ref[...] - t_ref[...]
        loss_ref[...] = jnp.full(loss_ref.shape, 0.5 * jnp.sum(jnp.mean(diff * diff, axis=-1)), F32)

    tok = lambda w: pl.BlockSpec((TM, w), lambda i: (i, 0))
    return pl.pallas_call(
        body, name="fwd_ffn", grid=(N_TB,),
        out_shape=(jax.ShapeDtypeStruct((SEQ, D_MODEL), BF16), jax.ShapeDtypeStruct((SEQ, 2 * D_FF), BF16),
                   jax.ShapeDtypeStruct((SEQ, 2 * D_FF), BF16),
                   jax.ShapeDtypeStruct((SEQ, D_FF), BF16), jax.ShapeDtypeStruct((SEQ, D_MODEL), F32),
                   jax.ShapeDtypeStruct((N_TB, 8, 128), F32)),
        in_specs=[tok(D_MODEL), _resident((1, D_MODEL)), _resident((2 * D_FF, D_MODEL)), _resident((8, 2 * D_FF)),
                  _resident((1, 2 * D_FF)), _resident((D_FF, D_MODEL)), _resident((1, D_MODEL)), tok(D_MODEL)],
        out_specs=(tok(D_MODEL), tok(2 * D_FF), tok(2 * D_FF), tok(D_FF), tok(D_MODEL),
                   pl.BlockSpec((1, 8, 128), lambda i: (i, 0, 0))),
        scratch_shapes=[pltpu.VMEM((8, 2 * D_FF), F32)],
        compiler_params=_cparams(("arbitrary",)),
    )(x2, g2, wup_g, cw_g, cb_g, wdn_g, gf, tgt)


def _bwd_ffn(x3, tgt, gf, x2, g2, up_pre, u_conv, wup_g, cw_g, wdn_g):
    def body(x3_ref, t_ref, gf_ref, x2_ref, g2_ref, up_ref, u_ref, wu_ref, cw_ref, wd_ref,
             dx3_ref, dpre_ref, dx2_ref, dgf_ref, dg2_ref, dcv_ref, nxt):
        i = pl.program_id(0)

        @pl.when(i == 0)
        def _():
            nxt[...] = jnp.zeros_like(nxt)
            dgf_ref[...] = jnp.zeros_like(dgf_ref)
            dg2_ref[...] = jnp.zeros_like(dg2_ref)
            dcv_ref[...] = jnp.zeros_like(dcv_ref)

        x3 = x3_ref[...]
        r3 = lax.rsqrt(jnp.mean(x3 * x3, axis=-1, keepdims=True) + EPS)
        xh3 = x3 * r3
        dy = (xh3 * gf_ref[...] - t_ref[...]) * (1.0 / D_MODEL)
        dgf_ref[0:1, :] += jnp.sum(dy * xh3, axis=0, keepdims=True)
        t3 = dy * gf_ref[...]
        dx3 = r3 * (t3 - xh3 * jnp.mean(t3 * xh3, axis=-1, keepdims=True))
        dx3b = dx3.astype(BF16)
        dx3_ref[...] = dx3b
        dh2 = jnp.zeros((TM, D_MODEL), F32)
        for t0, tw in FF_TILES:
            row = lax.broadcasted_iota(jnp.int32, (TM, tw), 0)
            ts = slice(t0, t0 + tw)
            dact = _dot_nt(dx3b, wd_ref[pl.ds(t0, tw), :])
            ua = u_ref[:, ts].astype(F32)
            ub = u_ref[:, D_FF + t0:D_FF + t0 + tw].astype(F32)
            sg = _sigmoid(ua)
            du = [dact * ub * (sg * (1.0 + ua * (1.0 - sg))), dact * (ua * sg)]
            for n in range(2):
                d = du[n]
                c0 = n * D_FF + t0
                cs = slice(c0, c0 + tw)
                nx = nxt[:, cs]
                n1 = jnp.where(row == TM - 1, nx[0:1, :], pltpu.roll(d, TM - 1, 0))
                n2 = jnp.where(row == TM - 2, nx[0:1, :], jnp.where(row == TM - 1, nx[1:2, :], pltpu.roll(d, TM - 2, 0)))
                nxt[:, cs] = d[0:8, :]
                dp = (d * cw_ref[2:3, cs] + n1 * cw_ref[1:2, cs] + n2 * cw_ref[0:1, cs]).astype(BF16)
                dpre_ref[:, cs] = dp
                p = up_ref[:, cs].astype(F32)
                dcv_ref[n, 0:1, ts] += jnp.sum(n2 * p, axis=0, keepdims=True)
                dcv_ref[n, 1:2, ts] += jnp.sum(n1 * p, axis=0, keepdims=True)
                dcv_ref[n, 2:3, ts] += jnp.sum(d * p, axis=0, keepdims=True)
                dcv_ref[n, 3:4, ts] += jnp.sum(d, axis=0, keepdims=True)
                dh2 = dh2 + _dot(dp, wu_ref[pl.ds(c0, tw), :])
        x2 = x2_ref[...]
        r2 = lax.rsqrt(jnp.mean(x2 * x2, axis=-1, keepdims=True) + EPS)
        xh2 = x2 * r2
        dg2_ref[0:1, :] += jnp.sum(dh2 * xh2, axis=0, keepdims=True)
        t2 = dh2 * g2_ref[...]
        dx2_ref[...] = dx3 + r2 * (t2 - xh2 * jnp.mean(t2 * xh2, axis=-1, keepdims=True))

    rev = lambda w: pl.BlockSpec((TM, w), lambda i: (N_TB - 1 - i, 0))
    acc = lambda s: pl.BlockSpec(s, lambda i: (0,) * len(s))
    return pl.pallas_call(
        body, name="bwd_ffn", grid=(N_TB,),
        out_shape=(jax.ShapeDtypeStruct((SEQ, D_MODEL), BF16), jax.ShapeDtypeStruct((SEQ, 2 * D_FF), BF16),
                   jax.ShapeDtypeStruct((SEQ, D_MODEL), F32), jax.ShapeDtypeStruct((8, D_MODEL), F32),
                   jax.ShapeDtypeStruct((8, D_MODEL), F32), jax.ShapeDtypeStruct((2, 8, D_FF), F32)),
        in_specs=[rev(D_MODEL), rev(D_MODEL), _resident((1, D_MODEL)), rev(D_MODEL), _resident((1, D_MODEL)), rev(2 * D_FF),
                  rev(2 * D_FF), _resident((2 * D_FF, D_MODEL)), _resident((8, 2 * D_FF)), _resident((D_FF, D_MODEL))],
        out_specs=(rev(D_MODEL), rev(2 * D_FF), rev(D_MODEL), acc((8, D_MODEL)), acc((8, D_MODEL)), acc((2, 8, D_FF))),
        scratch_shapes=[pltpu.VMEM((8, 2 * D_FF), F32)],
        compiler_params=_cparams(("arbitrary",)),
    )(x3, tgt, gf, x2, g2, up_pre, u_conv, wup_g, cw_g, wdn_g)


def _bwd_mix(dx2, proj, o, sprev, wout_g, grn, lng, lnb, ws, bsb, mask, qdec, kdec, cos2, sin2, hosted):
    cdec = _chunk_decay()
    geoms = [g for g, _ in hosted]
    n_h = len(hosted)

    def body(dx2_ref, p_ref, o_ref, sp_ref, w_ref, grn_ref, lng_ref, lnb_ref, ws_ref, bsb_ref, m_ref, qd_ref, kd_ref,
             cos_ref, sin_ref, *rest):
        dp_ref, dgrn_ref, dlng_ref, dlnb_ref, dws_ref, dbs_ref = rest[n_h:n_h + 6]
        dstate, dbs_acc = rest[2 * n_h + 6:2 * n_h + 8]
        i = pl.program_id(0)
        rs = _Scatters(geoms, rest[:n_h], rest[n_h + 6:2 * n_h + 6], rest[2 * n_h + 8:])
        pl.when(i == 0)(rs.phase1)
        pl.when(i == 3)(rs.phase2)
        pl.when(i == 8)(rs.phase2b)

        @pl.when(i == 0)
        def _():
            dstate[...] = jnp.zeros_like(dstate)
            dgrn_ref[...] = jnp.zeros_like(dgrn_ref)
            dlng_ref[...] = jnp.zeros_like(dlng_ref)
            dlnb_ref[...] = jnp.zeros_like(dlnb_ref)
            dws_ref[...] = jnp.zeros_like(dws_ref)
            dbs_ref[...] = jnp.zeros_like(dbs_ref)
            dbs_acc[...] = jnp.zeros_like(dbs_acc)

        dmix = _dot_nt(dx2_ref[...].astype(BF16), w_ref[...])
        for h in range(HEADS):
            sl = slice(h * HEAD_DIM, (h + 1) * HEAD_DIM)
            q = p_ref[:, sl]
            k = p_ref[:, RET_W + h * HEAD_DIM:RET_W + (h + 1) * HEAD_DIM]
            v = p_ref[:, 2 * RET_W + h * HEAD_DIM:2 * RET_W + (h + 1) * HEAD_DIM]
            g = p_ref[:, 3 * RET_W + h * HEAD_DIM:3 * RET_W + (h + 1) * HEAD_DIM]
            o = o_ref[:, sl]
            rinv = lax.rsqrt(jnp.mean(o * o, axis=-1, keepdims=True) + EPS)
            oh = o * rinv
            gr = grn_ref[:, sl]
            sg = _sigmoid(g)
            dret = dmix[:, sl]
            dp_ref[:, 3 * RET_W + h * HEAD_DIM:3 * RET_W + (h + 1) * HEAD_DIM] = (
                dret * (oh * gr) * (sg * (1.0 + g * (1.0 - sg)))).astype(BF16)
            drn = dret * (g * sg)
            dgrn_ref[0:1, sl] += jnp.sum(drn * oh, axis=0, keepdims=True)
            t = drn * gr
            do = rinv * (t - oh * jnp.mean(t * oh, axis=-1, keepdims=True))
            qb, kb, vb, dob = q.astype(BF16), k.astype(BF16), v.astype(BF16), do.astype(BF16)
            m = m_ref[h]
            ab = (_dot_nt(qb, kb) * m).astype(BF16)
            dab = (_dot_nt(dob, vb) * m).astype(BF16)
            spb = sp_ref[0, h]
            dsn = dstate[h]
            dsnb = dsn.astype(BF16)
            qdb = (q * qd_ref[h]).astype(BF16)
            kdb = (k * kd_ref[h]).astype(BF16)
            dq = _dot(dab, kb) + _dot_nt(dob, spb) * qd_ref[h]
            dk = _dot_tn(dab, qb) + _dot_nt(vb, dsnb) * kd_ref[h]
            dv = _dot_tn(ab, dob) + _dot(kdb, dsnb)
            dstate[h] = dsn * cdec[h] + _dot_tn(qdb, dob)
            c2, s2 = cos_ref[...], sin_ref[...]
            dp_ref[:, sl] = _rot_t(dq, c2, s2).astype(BF16)
            dp_ref[:, RET_W + h * HEAD_DIM:RET_W + (h + 1) * HEAD_DIM] = _rot_t(dk * K_SCALE, c2, s2).astype(BF16)
            dp_ref[:, 2 * RET_W + h * HEAD_DIM:2 * RET_W + (h + 1) * HEAD_DIM] = dv.astype(BF16)
        for gi in range(HEADS):
            sl = slice(gi * HEAD_DIM, (gi + 1) * HEAD_DIM)
            u = p_ref[:, 4 * RET_W + gi * HEAD_DIM:4 * RET_W + (gi + 1) * HEAD_DIM]
            sv = p_ref[:, 4 * RET_W + SGU_W + gi * HEAD_DIM:4 * RET_W + SGU_W + (gi + 1) * HEAD_DIM]
            gv = _gelu(sv)
            xc = gv - jnp.mean(gv, axis=-1, keepdims=True)
            rstd = lax.rsqrt(jnp.mean(xc * xc, axis=-1, keepdims=True) + EPS)
            xh = xc * rstd
            lg = lng_ref[:, sl]
            vnb = (xh * lg + lnb_ref[:, sl]).astype(BF16)
            wcb = _causal(ws_ref[gi]).astype(BF16)
            mixed = _dot(wcb, vnb) + bsb_ref[gi]
            dsgu = dmix[:, RET_W + gi * HEAD_DIM:RET_W + (gi + 1) * HEAD_DIM]
            dmixed = dsgu * _gelu(u)
            dmb = dmixed.astype(BF16)
            dws_ref[gi] += _causal(_dot_nt(dmb, vnb))
            dbs_acc[gi] += dmixed
            dvn = _dot_tn(wcb, dmb)
            dlng_ref[gi:gi + 1, :] += jnp.sum(dvn * xh, axis=0, keepdims=True)
            dlnb_ref[gi:gi + 1, :] += jnp.sum(dvn, axis=0, keepdims=True)
            dxh = dvn * lg
            dgv = rstd * (dxh - jnp.mean(dxh, axis=-1, keepdims=True) - xh * jnp.mean(dxh * xh, axis=-1, keepdims=True))
            dp_ref[:, 4 * RET_W + gi * HEAD_DIM:4 * RET_W + (gi + 1) * HEAD_DIM] = (dsgu * mixed * _gelu_grad(u)).astype(BF16)
            dp_ref[:, 4 * RET_W + SGU_W + gi * HEAD_DIM:4 * RET_W + SGU_W + (gi + 1) * HEAD_DIM] = (
                dgv * _gelu_grad(sv)).astype(BF16)

        @pl.when(i == N_CHUNK - 1)
        def _():
            for gi in range(HEADS):
                col = jnp.broadcast_to(jnp.sum(dbs_acc[gi], axis=-1, keepdims=True), (CHUNK, CHUNK))
                dbs_ref[gi:gi + 1, :] = jnp.transpose(col)[0:1, :]
            rs.phase3()

    rev = lambda w: pl.BlockSpec((CHUNK, w), lambda i: (N_CHUNK - 1 - i, 0))
    hcc = (HEADS, CHUNK, CHUNK)
    acc = lambda s: pl.BlockSpec(s, lambda i: (0,) * len(s))
    res = pl.pallas_call(
        body, name="bwd_mix", grid=(N_CHUNK,),
        out_shape=(jax.ShapeDtypeStruct((SEQ, PROJ_W), BF16), jax.ShapeDtypeStruct((8, RET_W), F32),
                   jax.ShapeDtypeStruct((8, HEAD_DIM), F32), jax.ShapeDtypeStruct((8, HEAD_DIM), F32),
                   jax.ShapeDtypeStruct(hcc, F32), jax.ShapeDtypeStruct((8, CHUNK), F32)) + _scatter_out_shapes(geoms),
        in_specs=[rev(D_MODEL), rev(PROJ_W), rev(RET_W),
                  pl.BlockSpec((1, HEADS, HEAD_DIM, HEAD_DIM), lambda i: (N_CHUNK - 1 - i, 0, 0, 0)),
                  _resident((D_MODEL, D_MODEL)), _resident((1, RET_W)), _resident((1, SGU_W)), _resident((1, SGU_W)),
                  _resident(hcc), _resident(hcc), _resident(hcc), _resident(hcc), _resident(hcc), rev(HEAD_DIM), rev(HEAD_DIM)]
        + [pl.BlockSpec(memory_space=pl.ANY)] * n_h,
        out_specs=(rev(PROJ_W), acc((8, RET_W)), acc((8, HEAD_DIM)), acc((8, HEAD_DIM)), acc(hcc), acc((8, CHUNK)))
        + _scatter_out_specs(geoms),
        scratch_shapes=[pltpu.VMEM((HEADS, HEAD_DIM, HEAD_DIM), F32), pltpu.VMEM((HEADS, CHUNK, CHUNK), F32)] + _scatter_scratch(geoms),
        compiler_params=_cparams(("arbitrary",), collective=COLLECTIVE["bwd_mix"]),
    )(dx2, proj, o, sprev, wout_g, grn, lng, lnb, ws, bsb, mask, qdec, kdec, cos2, sin2, *[p for _, p in hosted])
    return tuple(res[:6 + n_h])


def _bwd_proj(dproj, win_g, x, g1, dx2, gin_p, small):
    geoms = [W_IN]
    n_s = len(small)

    def body(dp_ref, w_ref, x_ref, g_ref, dx2_ref, gin_ref, *rest):
        small_refs = rest[:n_s]
        dx_ref, rs_out, rp_ref, rws_ref, rcv_ref, dg_ref = rest[n_s:n_s + 6]
        rs_scratch = rest[n_s + 6:n_s + 6 + N_SCATTER_SCRATCH]
        ar_scratch = rest[n_s + 6 + N_SCATTER_SCRATCH:]
        ar_res = ar_scratch[N_SMALL_SCRATCH:]
        ar = _SmallReduce((dg_ref,) + tuple(small_refs), ar_res, ar_scratch[:N_SMALL_SCRATCH])
        rs = _Scatters(geoms, [gin_ref], [rs_out], rs_scratch)
        pl.when(pl.program_id(0) == 0)(lambda: rs.phase1(diagonal=True))
        pl.when(pl.program_id(0) == 1)(rs.phase2)
        pl.when(pl.program_id(0) == 5)(rs.phase2b)

        @pl.when(pl.program_id(0) == 0)
        def _():
            dg_ref[...] = jnp.zeros_like(dg_ref)

        dh = _dot_nt(dp_ref[...], w_ref[...])
        xb = x_ref[...]
        r = lax.rsqrt(jnp.mean(xb * xb, axis=-1, keepdims=True) + EPS)
        xh = xb * r
        dg_ref[0:1, :] += jnp.sum(dh * xh, axis=0, keepdims=True)
        t = dh * g_ref[...]
        dx_ref[...] = dx2_ref[...] + r * (t - xh * jnp.mean(t * xh, axis=-1, keepdims=True))

        @pl.when(pl.program_id(0) == N_TB - 1)
        def _():
            ar.begin()
            rs.phase3()
            ar.end()
            for o_ref, r_ref in zip((rp_ref, rws_ref, rcv_ref), ar_res):
                o_ref[...] = r_ref[...]

    tok = lambda w: pl.BlockSpec((TM, w), lambda i: (i, 0))
    vm = pl.BlockSpec(memory_space=pltpu.VMEM)
    res = pl.pallas_call(
        body, name="bwd_proj", grid=(N_TB,),
        out_shape=(jax.ShapeDtypeStruct((SEQ, D_MODEL), F32),) + _scatter_out_shapes(geoms)
        + tuple(jax.ShapeDtypeStruct(s, F32) for s in SMALL_FULL),
        in_specs=[tok(PROJ_W), _resident((D_MODEL, PROJ_W)), tok(D_MODEL), _resident((1, D_MODEL)), tok(D_MODEL),
                  pl.BlockSpec(memory_space=pl.ANY)] + [vm] * n_s,
        out_specs=(tok(D_MODEL),) + _scatter_out_specs(geoms) + (vm,) * len(SMALL_FULL),
        scratch_shapes=[pltpu.VMEM((8, D_MODEL), F32)] + _scatter_scratch(geoms) + _small_scratch()
        + [pltpu.VMEM(s, F32) for s in SMALL_FULL],
        compiler_params=_cparams(("arbitrary",), collective=COLLECTIVE["bwd_proj"]),
    )(dproj, win_g, x, g1, dx2, gin_p, *small)
    return res


def _wgrad(name, a, b, tm=None, tn=None, hosted=()):
    m_w, n_w = a.shape[-1], b.shape[-1]
    tm = m_w if tm is None else tm
    tn = n_w if tn is None else tn
    n_steps = (m_w // tm) * (n_w // tn)
    geoms = [g for g, _ in hosted]
    n_h = len(hosted)

    def body(a_ref, b_ref, *rest):
        o_ref = rest[n_h]
        if n_h:
            rs = _Scatters(geoms, rest[:n_h], rest[n_h + 1:2 * n_h + 1], rest[2 * n_h + 1:])
            step = pl.program_id(0) * (n_w // tn) + pl.program_id(1)
            pl.when(step == 0)(rs.phase1)
            pl.when(step == 1)(rs.phase2)
            pl.when(step == n_steps // 2)(rs.phase2b)
        o_ref[...] = _dot_tn(a_ref[...].astype(BF16), b_ref[...].astype(BF16)).astype(BF16)
        if n_h:
            pl.when(step == n_steps - 1)(rs.phase3)

    assert not n_h or n_steps >= 4
    res = pl.pallas_call(
        body, name=name, grid=(m_w // tm, n_w // tn),
        out_shape=(jax.ShapeDtypeStruct((m_w, n_w), BF16),) + _scatter_out_shapes(geoms),
        in_specs=[pl.BlockSpec((SEQ, tm), lambda i, j: (0, i)), pl.BlockSpec((SEQ, tn), lambda i, j: (0, j))]
        + [pl.BlockSpec(memory_space=pl.ANY)] * n_h,
        out_specs=(pl.BlockSpec((tm, tn), lambda i, j: (i, j)),) + _scatter_out_specs(geoms),
        scratch_shapes=_scatter_scratch(geoms),
        compiler_params=_cparams(("arbitrary", "arbitrary"), collective=COLLECTIVE[name]) if n_h else _cparams(("parallel", "parallel")),
    )(a, b, *[p for _, p in hosted])
    return tuple(res[:1 + n_h])


def _row_step(half_rows):
    return max(s for s in range(16, 177, 16) if half_rows % s == 0)


class _Scatter:
    def __init__(self, geom, partial, out, land1, mine, stage2, land2, comb, s1_send, s1_recv, s2_send, s2_recv, ld_sems):
        self.w, self.row0, self.shape = _geom(geom)
        self.partial, self.out, self.land1 = partial, out, land1
        self.mine, self.stage2, self.land2, self.comb = mine, stage2, land2, comb
        self.hr = self.shape[0] // 2
        self.step = _row_step(self.hr)
        self.s1_send, self.s1_recv, self.s2_send, self.s2_recv, self.ld_sems = s1_send, s1_recv, s2_send, s2_recv, ld_sems
        self.x, self.y, self.c = lax.axis_index("x"), lax.axis_index("y"), lax.axis_index("c")
        self.sibling = (self.x, self.y, 1 - self.c)
        self.chips = [(self.x, self.y), (1 - self.x, self.y), (self.x, 1 - self.y), (1 - self.x, 1 - self.y)]

    def block(self, px, py, pc):
        dev = 4 * px + 2 * py + pc
        if self.w == W_IN:
            return self.partial.at[:, pl.ds(pl.multiple_of(dev * IN_SHARD, 128), IN_SHARD)]
        if self.w == W_OUT:
            return self.partial.at[pl.ds(pl.multiple_of(dev * OUT_SHARD, 128), OUT_SHARD), :]
        if self.w == W_DOWN:
            return self.partial.at[pl.ds(pl.multiple_of(dev * DOWN_SHARD, 32), DOWN_SHARD), :]
        return self.partial.at[pl.ds(pl.multiple_of(dev * FF_SHARD + self.row0, 32), self.shape[0]), :]

    def copy1(self, k):
        return pltpu.make_async_remote_copy(
            src_ref=self.block(*self.chips[k], 1 - self.c), dst_ref=self.land1.at[k],
            send_sem=self.s1_send.at[k], recv_sem=self.s1_recv.at[k], device_id=self.sibling, device_id_type=MESH)

    STAGE2 = [(1, 0, 1), (3, 0, 1), (2, 1, 2), (3, 1, 2), (1, 1, 1), (2, 0, 2)]

    def copy2(self, j):
        blk, h, to = self.STAGE2[j]
        src = self.comb.at[j - 4] if j >= 4 else self.stage2.at[blk - 1, pl.ds(h * self.hr, self.hr), :]
        return pltpu.make_async_remote_copy(
            src_ref=src, dst_ref=self.land2.at[j], send_sem=self.s2_send.at[j], recv_sem=self.s2_recv.at[j],
            device_id=(*self.chips[to], self.c), device_id_type=MESH)

    def _rows(self, h=None):
        step = self.step
        lo, n = (0, self.shape[0]) if h is None else (h * self.hr, self.hr)
        return [pl.ds(r0, step) for r0 in range(lo, lo + n, step)]

    def load(self, k):
        return pltpu.make_async_copy(self.block(*self.chips[k], self.c), self.mine.at[k], self.ld_sems.at[k])

    def load_mine(self):
        for k in range(4):
            self.load(k).start()

    def phase1(self):
        for k in range(4):
            self.copy1(k).start()

    def phase2(self, k):
        self.copy1(k).wait_recv()
        self.load(k).wait()
        for rs in self._rows():
            s = self.mine[k, rs, :].astype(F32) + self.land1[k, rs, :].astype(F32)
            if k == 0:
                self.out[rs, :] = s
            else:
                self.stage2[k - 1, rs, :] = s.astype(BF16)
        for j in {3: (1, 3), 1: (0,), 2: (2,), 0: ()}[k]:
            self.copy2(j).start()

    def phase2b(self):
        for j, got in ((4, 3), (5, 1)):
            blk, h, _ = self.STAGE2[j]
            self.copy2(got).wait_recv()
            for i, rs in enumerate(self._rows(h)):
                lr = pl.ds(i * self.step, self.step)
                self.comb[j - 4, lr, :] = (self.stage2[blk - 1, rs, :].astype(F32) + self.land2[got, lr, :].astype(F32)).astype(BF16)
            self.copy2(j).start()

    def phase3(self):
        for j in (0, 5, 4, 2):
            self.copy2(j).wait_recv()
        for h, (first, second) in enumerate(((0, 5), (4, 2))):
            for i, rs in enumerate(self._rows(h)):
                lr = pl.ds(i * self.step, self.step)
                self.out[rs, :] = (self.out[rs, :] + self.land2[first, lr, :].astype(F32)) + self.land2[second, lr, :].astype(F32)
        for k in range(4):
            self.copy1(k).wait_send()
        for j in range(6):
            self.copy2(j).wait_send()


def _geom(geom):
    if isinstance(geom, tuple):
        w, row0, rows = geom
        assert w == W_UP
        return w, row0, (rows, SHARD[w][1])
    return geom, 0, SHARD[geom]


N_SCATTER_SCRATCH = 10


def _scatter_out_shapes(geoms):
    return tuple(jax.ShapeDtypeStruct(_geom(g)[2], F32) for g in geoms)


def _scatter_out_specs(geoms):
    return (pl.BlockSpec(memory_space=pltpu.VMEM),) * len(geoms)


def _scatter_scratch(geoms):
    out = []
    for g in geoms:
        s = _geom(g)[2]
        hs = (s[0] // 2, s[1])
        out += [pltpu.VMEM((4,) + s, BF16), pltpu.VMEM((4,) + s, BF16), pltpu.VMEM((3,) + s, BF16), pltpu.VMEM((6,) + hs, BF16),
                pltpu.VMEM((2,) + hs, BF16),
                pltpu.SemaphoreType.DMA((4,)), pltpu.SemaphoreType.DMA((4,)), pltpu.SemaphoreType.DMA((6,)),
                pltpu.SemaphoreType.DMA((6,)), pltpu.SemaphoreType.DMA((4,))]
    return out


class _Scatters:
    def __init__(self, geoms, p_refs, out_refs, scratch):
        k = N_SCATTER_SCRATCH
        self.items = [_Scatter(g, p_refs[i], out_refs[i], *scratch[k * i:k * i + k]) for i, g in enumerate(geoms)]

    def phase1(self, diagonal=False):
        meet = _Meet(diagonal)
        meet.signal()
        for s in self.items:
            s.load_mine()
        meet.wait()
        for s in self.items:
            s.phase1()

    def phase2(self):
        for k in (3, 1, 2, 0):
            for s in self.items:
                s.phase2(k)

    def phase2b(self):
        for s in self.items:
            s.phase2b()

    def phase3(self):
        for s in self.items:
            s.phase3()


PACK_W = 1024


SMALL_FULL = [(2, 8, PACK_W), (HEADS, CHUNK, CHUNK), (2, 8, D_FF)]
SMALL_HALF = [(s[0] // 2,) + s[1:] for s in SMALL_FULL]
N_SMALL_SCRATCH = 16


def _small_scratch():
    n_a = len(SMALL_FULL)
    return ([pltpu.VMEM(SMALL_FULL[0], F32)] + [pltpu.VMEM(s, F32) for s in SMALL_HALF] + [pltpu.VMEM(s, F32) for s in SMALL_HALF]
            + [pltpu.VMEM((3,) + s, F32) for s in SMALL_HALF]
            + [pltpu.SemaphoreType.DMA((n_a,)), pltpu.SemaphoreType.DMA((n_a,)), pltpu.SemaphoreType.DMA((n_a, 3)),
               pltpu.SemaphoreType.DMA((n_a, 3)), pltpu.SemaphoreType.DMA((n_a,)), pltpu.SemaphoreType.DMA((n_a,))])


class _SmallReduce:
    def __init__(self, ins, outs, scratch):
        self.ins, self.outs = ins, outs
        (self.pack, *rest) = scratch
        self.rxs, self.css, self.gs = rest[0:3], rest[3:6], rest[6:9]
        self.s1_send, self.s1_recv, self.s2_send, self.s2_recv, self.s3_send, self.s3_recv = rest[9:]
        self.x, self.y, self.c = lax.axis_index("x"), lax.axis_index("y"), lax.axis_index("c")
        self.sibling = (self.x, self.y, 1 - self.c)
        self.chips = [(1 - self.x, self.y), (self.x, 1 - self.y), (1 - self.x, 1 - self.y)]
        self.hl = [s[0] for s in SMALL_HALF]

    def half(self, ref, a, h):
        return ref.at[pl.ds(h * self.hl[a], self.hl[a])]

    def begin(self):
        dg1_ref, dg2_ref, dgf_ref, dgrn_ref, dlng_ref, dlnb_ref, dbs_ref, loss_ref, dws_ref, dcv_ref = self.ins
        pack, c = self.pack, self.c
        pack[...] = jnp.zeros_like(pack)
        pack[0, 0:1, :] = dg1_ref[0:1, :]
        pack[0, 1:2, :] = dg2_ref[0:1, :]
        pack[0, 2:3, :] = dgf_ref[0:1, :]
        pack[0, 3:4, 0:RET_W] = dgrn_ref[0:1, :]
        lsum = loss_ref[0, 0:1, :]
        for i in range(1, N_TB):
            lsum = lsum + loss_ref[i, 0:1, :]
        pack[0, 3:4, RET_W:RET_W + 128] = lsum
        pack[1, 0:HEADS, 0:128] = dlng_ref[0:HEADS, :]
        pack[1, 0:HEADS, 128:256] = dlnb_ref[0:HEADS, :]
        pack[1, 0:HEADS, 256:384] = dbs_ref[0:HEADS, :]
        self.srcs = [pack, dws_ref, dcv_ref]
        n_a = len(self.srcs)
        self.ex1 = [pltpu.make_async_remote_copy(src_ref=self.half(self.srcs[a], a, 1 - c), dst_ref=self.rxs[a],
                                                 send_sem=self.s1_send.at[a], recv_sem=self.s1_recv.at[a],
                                                 device_id=self.sibling, device_id_type=MESH) for a in range(n_a)]
        for cp in self.ex1:
            cp.start()
        self.ex2 = []
        for a in range(n_a):
            self.ex1[a].wait_recv()
            self.css[a][...] = self.half(self.srcs[a], a, c)[...] + self.rxs[a][...]
            for j, chip in enumerate(self.chips):
                cp = pltpu.make_async_remote_copy(src_ref=self.css[a], dst_ref=self.gs[a].at[j], send_sem=self.s2_send.at[a, j],
                                                  recv_sem=self.s2_recv.at[a, j], device_id=(*chip, c), device_id_type=MESH)
                cp.start()
                self.ex2.append(cp)

    def end(self):
        c, x, y = self.c, self.x, self.y
        ex3 = []
        for a in range(len(self.srcs)):
            css, gs, out = self.css[a], self.gs[a], self.outs[a]
            for j in range(3):
                self.ex2[3 * a + j].wait_recv()
            tot = None
            for q in range(4):
                k = jnp.where(x != (q >> 1), 1, 0) + jnp.where(y != (q & 1), 2, 0)
                term = jnp.where(k == 0, css[...], jnp.where(k == 1, gs[0], jnp.where(k == 2, gs[1], gs[2])))
                tot = term if tot is None else tot + term
            self.half(out, a, c)[...] = tot
            cp = pltpu.make_async_remote_copy(src_ref=self.half(out, a, c), dst_ref=self.half(out, a, c), send_sem=self.s3_send.at[a],
                                              recv_sem=self.s3_recv.at[a], device_id=self.sibling, device_id_type=MESH)
            cp.start()
            ex3.append(cp)
        for a in range(len(self.srcs)):
            out = self.outs[a]
            pltpu.make_async_remote_copy(src_ref=self.half(out, a, 1 - c), dst_ref=self.half(out, a, 1 - c), send_sem=self.s3_send.at[a],
                                         recv_sem=self.s3_recv.at[a], device_id=self.sibling, device_id_type=MESH).wait_recv()
        for cp in self.ex1 + self.ex2 + ex3:
            cp.wait_send()


def _adam_math(w, g, m, v):
    nm = ADAM_B1 * m + (1.0 - ADAM_B1) * g
    nv = ADAM_B2 * v + (1.0 - ADAM_B2) * (g * g)
    d = -ADAM_LR * ((nm / (1.0 - ADAM_B1 ** ADAM_STEP)) / (jnp.sqrt(nv / (1.0 - ADAM_B2 ** ADAM_STEP)) + ADAM_EPS) + ADAM_WD * w)
    return d, nm, nv


def _adamw(name, w, gs, m, v, rows, thru=()):
    _, r, cdim = w.shape
    n_steps = r // rows
    half = gs[0].shape[0] // rows
    n_g, n_t = len(gs), len(thru)

    def body(w_ref, *rest):
        g_refs, (m_ref, v_ref), t_refs = rest[:n_g], rest[n_g:n_g + 2], rest[n_g + 2:n_g + 2 + n_t]
        go_ref, d_ref, nm_ref, nv_ref, *to_refs = rest[n_g + 2 + n_t:]
        gg = g_refs[0][...]
        if n_g == 2:
            gg = jnp.where(pl.program_id(0) < half, gg, g_refs[1][...])
        go_ref[0] = gg
        d, nm, nv = _adam_math(w_ref[0], gg, m_ref[0], v_ref[0])
        d_ref[0], nm_ref[0], nv_ref[0] = d, nm, nv
        for t_ref, to_ref in zip(t_refs, to_refs):
            to_ref[...] = t_ref[...]

    spec3 = pl.BlockSpec((1, rows, cdim), lambda i: (0, i, 0))
    if n_g == 1:
        g_specs = [pl.BlockSpec((rows, cdim), lambda i: (i, 0))]
    else:
        g_specs = [pl.BlockSpec((rows, cdim), lambda i: (jnp.minimum(i, half - 1), 0)),
                   pl.BlockSpec((rows, cdim), lambda i: (jnp.maximum(i - half, 0), 0))]
    t_specs = [pl.BlockSpec((t.shape[0] // n_steps, t.shape[1]), lambda i: (i, 0)) for t in thru]
    sh = jax.ShapeDtypeStruct((1, r, cdim), F32)
    return pl.pallas_call(
        body, name=name, grid=(n_steps,), out_shape=(sh, sh, sh, sh) + tuple(jax.ShapeDtypeStruct(t.shape, t.dtype) for t in thru),
        in_specs=[spec3] + g_specs + [spec3, spec3] + t_specs, out_specs=(spec3,) * 4 + tuple(t_specs),
        compiler_params=_cparams(("parallel",)),
    )(w, *gs, m, v, *thru)


def _adamw_small(rp, rws, rcv, gcw, params):
    n_p = len(params)

    def body(*refs):
        rp_ref, rws_ref, rcv_ref, gcw_ref = refs[:4]
        ins = refs[4:4 + 3 * n_p]
        outs = refs[4 + 3 * n_p:]
        outs[4 * n_p][...] = rp_ref[0, 3:4, RET_W:RET_W + 1]
        grads = [rp_ref[0, 0:1, :], rp_ref[0, 1:2, :], rp_ref[0, 2:3, :], rp_ref[0, 3:4, 0:RET_W],
                 rp_ref[1, 0:HEADS, 0:128], rp_ref[1, 0:HEADS, 128:256], rp_ref[1, 0:HEADS, 256:384],
                 rws_ref[...], gcw_ref[...], None]
        for p in range(n_p):
            w_ref, m_ref, v_ref = ins[3 * p:3 * p + 3]
            o = outs[4 * p:4 * p + 4]
            if p == n_p - 1:
                for hf in range(2):
                    cs = slice(hf * D_FF, (hf + 1) * D_FF)
                    g = rcv_ref[hf, 3:4, :]
                    res = (g,) + _adam_math(w_ref[:, cs], g, m_ref[:, cs], v_ref[:, cs])
                    for t in range(4):
                        o[t][:, cs] = res[t]
                continue
            lead = w_ref.ndim > grads[p].ndim
            rd = (lambda r: r[0]) if lead else (lambda r: r[...])
            res = (grads[p],) + _adam_math(rd(w_ref), grads[p], rd(m_ref), rd(v_ref))
            for t in range(4):
                if lead:
                    o[t][0] = res[t]
                else:
                    o[t][...] = res[t]

    vm = pl.BlockSpec(memory_space=pltpu.VMEM)
    flat = [a for tr in params for a in tr]
    out_shape = tuple(jax.ShapeDtypeStruct(tr[0].shape, F32) for tr in params for _ in range(4)) + (jax.ShapeDtypeStruct((1, 1), F32),)
    res = pl.pallas_call(
        body, name="adamw_small", out_shape=out_shape, in_specs=[vm] * (4 + len(flat)), out_specs=(vm,) * len(out_shape),
        compiler_params=_cparams(),
    )(rp, rws, rcv, gcw, *flat)
    return [res[4 * p:4 * p + 4] for p in range(n_p)], res[4 * n_p]


def kernel(x, mix_norm_g, w_in, ret_norm_g, sgu_ln_g, sgu_ln_b, sgu_w_s, sgu_b_s, w_out, ffn_norm_g, w_up, conv_w, conv_b, w_down, final_norm_g, loss_target, m_mix_norm_g, m_w_in, m_ret_norm_g, m_sgu_ln_g, m_sgu_ln_b, m_sgu_w_s, m_sgu_b_s, m_w_out, m_ffn_norm_g, m_w_up, m_conv_w, m_conv_b, m_w_down, m_final_norm_g, v_mix_norm_g, v_w_in, v_ret_norm_g, v_sgu_ln_g, v_sgu_ln_b, v_sgu_w_s, v_sgu_b_s, v_w_out, v_ffn_norm_g, v_w_up, v_conv_w, v_conv_b, v_w_down, v_final_norm_g):
    xs = x[0]
    tgt = loss_target[0]
    mask, qdec, kdec = _decay_tables()
    grn = ret_norm_g.reshape(1, RET_W)
    lng = sgu_ln_g.reshape(1, SGU_W)
    lnb = sgu_ln_b.reshape(1, SGU_W)
    ws = sgu_w_s[0]
    bsb = jnp.broadcast_to(sgu_b_s[0][:, :, None], (HEADS, CHUNK, HEAD_DIM))
    gf = final_norm_g.reshape(1, D_MODEL)
    me = 4 * lax.axis_index("x") + 2 * lax.axis_index("y") + lax.axis_index("c")
    tr = lambda a: jnp.transpose(a[0])[None]
    tr_cw = lambda a: jnp.transpose(a, (1, 0, 2))

    proj, h1, cos2, sin2, win_g, cw_sh, wout_g, wdn_g, su = _fwd_proj(
        xs, mix_norm_g, _rope_freq(), w_in[0], w_out[0], tr(w_up)[0], w_down[0], tr_cw(conv_w))
    cw_g = jnp.transpose(cw_sh, (1, 0, 2)).reshape(8, 2 * D_FF)
    x2, mixcat, o, sprev, wup_g = _fwd_mix(xs, proj, wout_g, grn, lng, lnb, ws, bsb, mask, qdec, kdec, su)
    h2, up_pre, u_conv, act, x3, loss_parts = _fwd_ffn(x2, ffn_norm_g, wup_g, cw_g, conv_b, wdn_g, gf, tgt)

    dx3, dpre, dx2, dgf, dg2, dcv = _bwd_ffn(x3, tgt, gf, x2, ffn_norm_g, up_pre, u_conv, wup_g, cw_g, wdn_g)
    band = 512
    (gdn_p,) = _wgrad("wgrad_down", act, dx3, tm=FF_TILE)
    (gout_p,) = _wgrad("wgrad_out", mixcat, dx2, tn=512)
    gup_p, g_dn = _wgrad("wgrad_up", dpre, h2, tm=FF_TILE, tn=512, hosted=[(W_DOWN, gdn_p)])
    dproj, dgrn, dlng, dlnb, dws, dbs, g_up_a, g_out = _bwd_mix(
        dx2, proj, o, sprev, wout_g, grn, lng, lnb, ws, bsb, mask, qdec, kdec, cos2, sin2,
        [((W_UP, 0, band), gup_p), (W_OUT, gout_p)])
    gin_p, g_up_b = _wgrad("wgrad_in", h1, dproj, tm=512, tn=768, hosted=[((W_UP, band, FF_SHARD - band), gup_p)])
    grad_x, g_in, rp, rws, rcv = _bwd_proj(dproj, win_g, xs, mix_norm_g, dx2, gin_p,
                                           (dg2, dgf, dgrn, dlng, dlnb, dbs, loss_parts, dws, dcv))
    gcw = tr_cw(lax.dynamic_slice(rcv, (me // (N_DEV // 2), 0, (me % (N_DEV // 2)) * FF_SHARD), (1, 3, FF_SHARD)))

    table = {}
    *table["w_in"], grad_x = _adamw("adamw_w_in", w_in, [g_in], m_w_in, v_w_in, 256, thru=[grad_x])
    for name, w, gs, m, v, rows in (("w_out", w_out, [g_out], m_w_out, v_w_out, 128),
                                    ("w_up", tr(w_up), [g_up_a, g_up_b], tr(m_w_up), tr(v_w_up), 64),
                                    ("w_down", w_down, [g_dn], m_w_down, v_w_down, 88)):
        table[name] = _adamw("adamw_" + name, w, gs, m, v, rows)
    table["w_up"] = tuple(tr(a) for a in table["w_up"])
    row = lambda a: a.reshape(1, D_MODEL)
    names_small = ["mix_norm_g", "ffn_norm_g", "final_norm_g", "ret_norm_g", "sgu_ln_g", "sgu_ln_b", "sgu_b_s", "sgu_w_s",
                   "conv_w", "conv_b"]
    params = [(mix_norm_g, m_mix_norm_g, v_mix_norm_g), (ffn_norm_g, m_ffn_norm_g, v_ffn_norm_g),
              (row(final_norm_g), row(m_final_norm_g), row(v_final_norm_g)), (ret_norm_g, m_ret_norm_g, v_ret_norm_g),
              (sgu_ln_g, m_sgu_ln_g, v_sgu_ln_g), (sgu_ln_b, m_sgu_ln_b, v_sgu_ln_b), (sgu_b_s, m_sgu_b_s, v_sgu_b_s),
              (sgu_w_s, m_sgu_w_s, v_sgu_w_s), (tr_cw(conv_w), tr_cw(m_conv_w), tr_cw(v_conv_w)), (conv_b, m_conv_b, v_conv_b)]
    small, loss = _adamw_small(rp, rws, rcv, gcw, params)
    for n, res in zip(names_small, small):
        table[n] = res
    table["final_norm_g"] = tuple(a.reshape(D_MODEL) for a in table["final_norm_g"])
    table["conv_w"] = tuple(tr_cw(a) for a in table["conv_w"])

    order = ["mix_norm_g", "w_in", "ret_norm_g", "sgu_ln_g", "sgu_ln_b", "sgu_w_s", "sgu_b_s", "w_out", "ffn_norm_g", "w_up",
             "conv_w", "conv_b", "w_down", "final_norm_g"]
    outs = [loss.reshape(()), grad_x[None]]
    for col in range(4):
        outs += [table[n][col] for n in order]
    return tuple(outs)
```

```python
import functools
import math

import jax
import jax.numpy as jnp
import numpy as np
from jax import lax
from jax.experimental import pallas as pl
from jax.experimental.pallas import tpu as pltpu

F32 = jnp.float32
BF16 = jnp.bfloat16
MESH = pl.DeviceIdType.MESH

N_DEV = 8
SEQ = 2048
D_MODEL = 1024
CHUNK = 128
N_CHUNK = SEQ // CHUNK
HEADS = 4
HEAD_DIM = 128
RET_W = 512
SGU_W = 512
PROJ_W = 3072
D_FF = 2816
FF_SHARD = 704
FF_TILE = 1408
FF_TILES = ((0, 1536), (1536, 1280))
IN_SHARD = PROJ_W // N_DEV
OUT_SHARD = D_MODEL // N_DEV
DOWN_SHARD = D_FF // N_DEV
TM = 256
N_TB = SEQ // TM
FWD_PROJ_PASS_AT = 5
FWD_MIX_PASS_AT = 10
EPS = 1e-6
ROPE_BASE = 10000.0
K_SCALE = HEAD_DIM ** -0.5
INV_SQRT2 = 0.7071067811865476
INV_SQRT_2PI = 0.3989422804014327

ADAM_LR = 0.001
ADAM_B1 = 0.9
ADAM_B2 = 0.999
ADAM_EPS = 1e-08
ADAM_WD = 0.01
ADAM_STEP = 10

VMEM_LIMIT = 56 * 1024 * 1024


def _cparams(sem=None, vmem=VMEM_LIMIT, collective=None):
    return pltpu.CompilerParams(dimension_semantics=sem, vmem_limit_bytes=vmem, collective_id=collective)


COLLECTIVE = {name: k for k, name in enumerate(("fwd_proj", "fwd_mix", "wgrad_up", "bwd_mix", "wgrad_in", "bwd_proj"))}


class _Meet:
    def __init__(self, diagonal):
        x, y, c = lax.axis_index("x"), lax.axis_index("y"), lax.axis_index("c")
        self.peers = [(x, y, 1 - c), (1 - x, y, c), (x, 1 - y, c)] + ([(1 - x, 1 - y, c)] if diagonal else [])

    def signal(self):
        for peer in self.peers:
            pl.semaphore_signal(pltpu.get_barrier_semaphore(), inc=1, device_id=peer, device_id_type=MESH)

    def wait(self):
        pl.semaphore_wait(pltpu.get_barrier_semaphore(), len(self.peers))


def _resident(shape):
    nd = len(shape)
    return pl.BlockSpec(shape, lambda *_: (0,) * nd, pipeline_mode=pl.Buffered(1))


def _dot(a, b):
    return jnp.dot(a, b, preferred_element_type=F32)


def _dot_nt(a, b):
    return lax.dot_general(a, b, (((1,), (1,)), ((), ())), preferred_element_type=F32)


def _dot_tn(a, b):
    return lax.dot_general(a, b, (((0,), (0,)), ((), ())), preferred_element_type=F32)


def _sigmoid(x):
    return 1.0 / (1.0 + jnp.exp(-x))


def _gelu(x):
    return 0.5 * x * (1.0 + lax.erf(x * INV_SQRT2))


def _gelu_grad(x):
    return 0.5 * (1.0 + lax.erf(x * INV_SQRT2)) + x * (jnp.exp(-0.5 * x * x) * INV_SQRT_2PI)


def _rot(xh, cos2, sin2):
    return xh * cos2 + pltpu.roll(xh, HEAD_DIM // 2, 1) * sin2


def _rot_t(dh, cos2, sin2):
    return dh * cos2 + pltpu.roll(dh * sin2, HEAD_DIM // 2, 1)


def _rope_freq():
    half = HEAD_DIM // 2
    inv_freq = jnp.power(ROPE_BASE, -jnp.arange(half, dtype=F32) / half)
    return jnp.concatenate([inv_freq, inv_freq])[None, :]


def _rope_block(inv2, first_row):
    pos = (lax.broadcasted_iota(jnp.int32, (TM, HEAD_DIM), 0) + first_row).astype(F32)
    ang = pos * inv2
    sin = jnp.sin(ang)
    lane = lax.broadcasted_iota(jnp.int32, (TM, HEAD_DIM), 1)
    return jnp.cos(ang), jnp.where(lane < HEAD_DIM // 2, -sin, sin)


def _decay_tables():
    log_gamma = jnp.log(1.0 - jnp.power(2.0, -5.0 - jnp.arange(HEADS, dtype=F32)))
    pos = jnp.arange(CHUNK, dtype=F32)
    diff = pos[:, None] - pos[None, :]
    mask = jnp.where(diff >= 0.0, jnp.exp(log_gamma[:, None, None] * jnp.maximum(diff, 0.0)[None]), 0.0)
    k_decay = jnp.exp(log_gamma[:, None] * (CHUNK - 1.0 - pos)[None])
    q_decay = jnp.exp(log_gamma[:, None] * (pos + 1.0)[None])
    kd = jnp.broadcast_to(k_decay[:, :, None], (HEADS, CHUNK, HEAD_DIM))
    qd = jnp.broadcast_to(q_decay[:, :, None], (HEADS, CHUNK, HEAD_DIM))
    return mask.astype(F32), qd.astype(F32), kd.astype(F32)


def _chunk_decay():
    lg = np.log(np.float32(1.0) - np.power(np.float32(2.0), -5.0 - np.arange(HEADS, dtype=np.float32))).astype(np.float32)
    return [float(np.exp(lg[h] * np.float32(CHUNK))) for h in range(HEADS)]


W_IN, W_OUT, W_UP, W_DOWN, W_CONV = range(5)
GATHERED = {W_IN: ((D_MODEL, PROJ_W), BF16), W_OUT: ((D_MODEL, D_MODEL), BF16), W_UP: ((2 * D_FF, D_MODEL), BF16),
            W_DOWN: ((D_FF, D_MODEL), BF16), W_CONV: ((N_DEV, 8, FF_SHARD), F32)}
SHARD = {W_IN: (D_MODEL, IN_SHARD), W_OUT: (OUT_SHARD, D_MODEL), W_UP: (FF_SHARD, D_MODEL), W_DOWN: (DOWN_SHARD, D_MODEL),
         W_CONV: (8, FF_SHARD)}


class _Gather:
    N_SEMS = 9

    def __init__(self, ids, stages, gathered, send_sems, recv_sems, local_sems):
        self.ids, self.stages, self.gathered = ids, stages, gathered
        self.send_sems, self.recv_sems, self.local_sems = send_sems, recv_sems, local_sems
        self.x, self.y, self.c = lax.axis_index("x"), lax.axis_index("y"), lax.axis_index("c")
        self.me = (self.x, self.y, self.c)
        self.sibling = (self.x, self.y, 1 - self.c)
        self.chips = [(1 - self.x, self.y), (self.x, 1 - self.y), (1 - self.x, 1 - self.y)]

    def slot(self, n, px, py, pc):
        dev = 4 * px + 2 * py + pc
        w, g = self.ids[n], self.gathered[n]
        if w == W_IN:
            return g.at[:, pl.ds(pl.multiple_of(dev * IN_SHARD, 128), IN_SHARD)]
        if w == W_OUT:
            return g.at[pl.ds(pl.multiple_of(dev * OUT_SHARD, 128), OUT_SHARD), :]
        if w == W_DOWN:
            return g.at[pl.ds(pl.multiple_of(dev * DOWN_SHARD, 32), DOWN_SHARD), :]
        if w == W_UP:
            return g.at[pl.ds(pl.multiple_of(dev * FF_SHARD, 32), FF_SHARD), :]
        return g.at[dev]

    def half(self, n, px, py, pc, h):
        dev = 4 * px + 2 * py + pc
        w, g = self.ids[n], self.gathered[n]
        if w == W_IN:
            return g.at[pl.ds(h * (D_MODEL // 2), D_MODEL // 2), pl.ds(pl.multiple_of(dev * IN_SHARD, 128), IN_SHARD)]
        rows = SHARD[w][0] // 2
        return g.at[pl.ds(pl.multiple_of(dev * SHARD[w][0] + h * rows, 16), rows), :]

    def tree(self, n):
        return self.ids[n] != W_CONV

    def copy(self, n, k, block, to, src=None, h=None):
        ref = self.slot(n, *block) if h is None else self.half(n, *block, h)
        return pltpu.make_async_remote_copy(
            src_ref=ref if src is None else src, dst_ref=ref,
            send_sem=self.send_sems.at[n, k], recv_sem=self.recv_sems.at[n, k], device_id=to, device_id_type=MESH)

    def _mine(self):
        return [pltpu.make_async_copy(self.stages[n], self.slot(n, *self.me), self.local_sems.at[n]) for n in range(len(self.ids))]

    def _first(self):
        out = []
        for n in range(len(self.ids)):
            out.append(self.copy(n, 0, self.me, self.sibling, src=self.stages[n]))
            out += [self.copy(n, 1 + j, self.me, (*chip, self.c), src=self.stages[n])
                    for j, chip in enumerate(self.chips[:2] if self.tree(n) else self.chips)]
        return out

    def start(self):
        for cp in self._mine() + self._first():
            cp.start()

    def _passed(self, j):
        dev = (*self.chips[j], self.c)
        out = []
        for n in range(len(self.ids)):
            if not self.tree(n):
                out.append(self.copy(n, 4 + j, dev, self.sibling))
            elif j < 2:
                out += [self.copy(n, 3 + j, dev, (*self.chips[1 - j], self.c), h=j), self.copy(n, 5 + j, dev, self.sibling)]
            else:
                out += [self.copy(n, 7, dev, self.sibling, h=0), self.copy(n, 8, dev, self.sibling, h=1)]
        return out

    def near(self):
        for j in range(2):
            dev = (*self.chips[j], self.c)
            for n in range(len(self.ids)):
                self.copy(n, 1 + j, dev, self.me).wait_recv()
            for cp in self._passed(j):
                cp.start()

    def finish(self):
        dev = (*self.chips[2], self.c)
        for n in range(len(self.ids)):
            if self.tree(n):
                self.copy(n, 3, dev, self.me, h=0).wait_recv()
                self.copy(n, 4, dev, self.me, h=1).wait_recv()
            else:
                self.copy(n, 3, dev, self.me).wait_recv()
        for cp in self._passed(2):
            cp.start()
        for n in range(len(self.ids)):
            self.copy(n, 0, self.sibling, self.me).wait_recv()
            for j, chip in enumerate(self.chips):
                dev = (*chip, 1 - self.c)
                if not self.tree(n):
                    self.copy(n, 4 + j, dev, self.me).wait_recv()
                elif j < 2:
                    self.copy(n, 5 + j, dev, self.me).wait_recv()
                else:
                    self.copy(n, 7, dev, self.me, h=0).wait_recv()
                    self.copy(n, 8, dev, self.me, h=1).wait_recv()
        for cp in self._mine():
            cp.wait()
        for cp in self._first() + self._passed(0) + self._passed(1) + self._passed(2):
            cp.wait_send()


def _gather_scratch(n):
    return [pltpu.SemaphoreType.DMA((n, _Gather.N_SEMS)), pltpu.SemaphoreType.DMA((n, _Gather.N_SEMS)), pltpu.SemaphoreType.DMA((n,))]


def _gathered_shapes(ids):
    return tuple(jax.ShapeDtypeStruct(*GATHERED[w]) for w in ids)


def _fwd_proj(x, g1, inv2, w_in, w_out, w_up, w_down, conv_w):
    ids_a, ids_b = [W_IN, W_CONV], [W_OUT, W_DOWN]

    def body(x_ref, g_ref, inv_ref, in_hbm, out_hbm, up_hbm, dn_hbm, cw_ref,
             proj_ref, h1_ref, cos_ref, sin_ref, gin, gcw, gout, gdn, su_ref,
             w_vm, s_in, s_cw, s_out, s_dn, f_in, f_out, f_up, f_dn, ld_sems,
             a_send, a_recv, a_local, b_send, b_recv, b_local):
        ag_a = _Gather(ids_a, [s_in, s_cw], [gin, gcw], a_send, a_recv, a_local)
        ag_b = _Gather(ids_b, [s_out, s_dn], [gout, gdn], b_send, b_recv, b_local)

        @pl.when(pl.program_id(0) == 0)
        def _():
            meet = _Meet(diagonal=True)
            meet.signal()
            loads = [pltpu.make_async_copy(src, dst, ld_sems.at[i])
                     for i, (src, dst) in enumerate(((in_hbm, f_in), (out_hbm, f_out), (dn_hbm, f_dn), (up_hbm, f_up)))]
            for cp in loads:
                cp.start()
            s_cw[...] = jnp.zeros_like(s_cw)
            for k in range(3):
                s_cw[k:k + 1, :] = cw_ref[k]
            loads[0].wait()
            s_in[...] = f_in[...].astype(BF16)
            meet.wait()
            ag_a.start()
            loads[1].wait()
            s_out[...] = f_out[...].astype(BF16)
            loads[2].wait()
            s_dn[...] = f_dn[...].astype(BF16)
            ag_a.near()
            ag_b.start()
            loads[3].wait()
            su_ref[...] = f_up[...].astype(BF16)
            ag_a.finish()
            fill = pltpu.make_async_copy(gin, w_vm, ld_sems.at[4])
            fill.start()
            fill.wait()

        pl.when(pl.program_id(0) == FWD_PROJ_PASS_AT)(ag_b.near)

        xb = x_ref[...]
        r = lax.rsqrt(jnp.mean(xb * xb, axis=-1, keepdims=True) + EPS)
        h = ((xb * r) * g_ref[...]).astype(BF16)
        h1_ref[...] = h
        p = _dot(h, w_vm[...])
        c2, s2 = _rope_block(inv_ref[...], pl.program_id(0) * TM)
        cos_ref[...], sin_ref[...] = c2, s2
        for hd in range(HEADS):
            sl = slice(hd * HEAD_DIM, (hd + 1) * HEAD_DIM)
            proj_ref[:, sl] = _rot(p[:, sl], c2, s2)
            ks = slice(RET_W + hd * HEAD_DIM, RET_W + (hd + 1) * HEAD_DIM)
            proj_ref[:, ks] = _rot(p[:, ks], c2, s2) * K_SCALE
        proj_ref[:, 2 * RET_W:] = p[:, 2 * RET_W:]

        pl.when(pl.program_id(0) == N_TB - 1)(ag_b.finish)

    tok = lambda w: pl.BlockSpec((TM, w), lambda i: (i, 0))
    hbm = pl.BlockSpec(memory_space=pl.ANY)
    vm = pl.BlockSpec(memory_space=pltpu.VMEM)
    return pl.pallas_call(
        body, name="fwd_proj", grid=(N_TB,),
        out_shape=(jax.ShapeDtypeStruct((SEQ, PROJ_W), F32), jax.ShapeDtypeStruct((SEQ, D_MODEL), BF16),
                   jax.ShapeDtypeStruct((SEQ, HEAD_DIM), F32), jax.ShapeDtypeStruct((SEQ, HEAD_DIM), F32))
        + _gathered_shapes(ids_a + ids_b) + (jax.ShapeDtypeStruct(SHARD[W_UP], BF16),),
        in_specs=[tok(D_MODEL), _resident((1, D_MODEL)), _resident((1, HEAD_DIM)), hbm, hbm, hbm, hbm, vm],
        out_specs=(tok(PROJ_W), tok(D_MODEL), tok(HEAD_DIM), tok(HEAD_DIM), hbm, hbm, hbm, hbm, vm),
        scratch_shapes=[pltpu.VMEM((D_MODEL, PROJ_W), BF16), pltpu.VMEM(SHARD[W_IN], BF16), pltpu.VMEM(SHARD[W_CONV], F32),
                        pltpu.VMEM(SHARD[W_OUT], BF16), pltpu.VMEM(SHARD[W_DOWN], BF16),
                        pltpu.VMEM(SHARD[W_IN], F32), pltpu.VMEM(SHARD[W_OUT], F32), pltpu.VMEM(SHARD[W_UP], F32),
                        pltpu.VMEM(SHARD[W_DOWN], F32), pltpu.SemaphoreType.DMA((5,))]
        + _gather_scratch(len(ids_a)) + _gather_scratch(len(ids_b)),
        compiler_params=_cparams(("arbitrary",), collective=COLLECTIVE["fwd_proj"]),
    )(x, g1, inv2, w_in, w_out, w_up, w_down, conv_w)


def _causal(w):
    r = lax.broadcasted_iota(jnp.int32, (CHUNK, CHUNK), 0)
    c = lax.broadcasted_iota(jnp.int32, (CHUNK, CHUNK), 1)
    return jnp.where(r >= c, w, 0.0)


def _fwd_mix(x, proj, wout_g, grn, lng, lnb, ws, bsb, mask, qdec, kdec, su):
    cdec = _chunk_decay()
    ids = [W_UP]

    def body(x_ref, p_ref, w_ref, grn_ref, lng_ref, lnb_ref, ws_ref, bsb_ref, m_ref, qd_ref, kd_ref, su_ref,
             x2_ref, cat_ref, o_ref, sp_ref, gup, state, send_sems, recv_sems, local_sems):
        ag = _Gather(ids, [su_ref], [gup], send_sems, recv_sems, local_sems)

        @pl.when(pl.program_id(0) == 0)
        def _():
            meet = _Meet(diagonal=False)
            meet.signal()
            state[...] = jnp.zeros_like(state)
            meet.wait()
            ag.start()

        for h in range(HEADS):
            sl = slice(h * HEAD_DIM, (h + 1) * HEAD_DIM)
            q = p_ref[:, sl]
            k = p_ref[:, RET_W + h * HEAD_DIM:RET_W + (h + 1) * HEAD_DIM]
            v = p_ref[:, 2 * RET_W + h * HEAD_DIM:2 * RET_W + (h + 1) * HEAD_DIM]
            g = p_ref[:, 3 * RET_W + h * HEAD_DIM:3 * RET_W + (h + 1) * HEAD_DIM]
            qb, kb, vb = q.astype(BF16), k.astype(BF16), v.astype(BF16)
            a = _dot_nt(qb, kb) * m_ref[h]
            spb = state[h].astype(BF16)
            sp_ref[0, h] = spb
            o = _dot(a.astype(BF16), vb) + _dot((q * qd_ref[h]).astype(BF16), spb)
            state[h] = state[h] * cdec[h] + _dot_tn((k * kd_ref[h]).astype(BF16), vb)
            o_ref[:, sl] = o
            rinv = lax.rsqrt(jnp.mean(o * o, axis=-1, keepdims=True) + EPS)
            rn = (o * rinv) * grn_ref[:, sl]
            cat_ref[:, sl] = ((g * _sigmoid(g)) * rn).astype(BF16)
        for gi in range(HEADS):
            sl = slice(gi * HEAD_DIM, (gi + 1) * HEAD_DIM)
            u = p_ref[:, 4 * RET_W + gi * HEAD_DIM:4 * RET_W + (gi + 1) * HEAD_DIM]
            sv = p_ref[:, 4 * RET_W + SGU_W + gi * HEAD_DIM:4 * RET_W + SGU_W + (gi + 1) * HEAD_DIM]
            gv = _gelu(sv)
            xc = gv - jnp.mean(gv, axis=-1, keepdims=True)
            vn = (xc * lax.rsqrt(jnp.mean(xc * xc, axis=-1, keepdims=True) + EPS)) * lng_ref[:, sl] + lnb_ref[:, sl]
            mixed = _dot(_causal(ws_ref[gi]).astype(BF16), vn.astype(BF16)) + bsb_ref[gi]
            cat_ref[:, RET_W + gi * HEAD_DIM:RET_W + (gi + 1) * HEAD_DIM] = (_gelu(u) * mixed).astype(BF16)
        x2_ref[...] = x_ref[...] + _dot(cat_ref[...], w_ref[...])

        pl.when(pl.program_id(0) == FWD_MIX_PASS_AT)(ag.near)
        pl.when(pl.program_id(0) == N_CHUNK - 1)(ag.finish)

    ch = lambda w: pl.BlockSpec((CHUNK, w), lambda i: (i, 0))
    hcc = (HEADS, CHUNK, CHUNK)
    hbm = pl.BlockSpec(memory_space=pl.ANY)
    return pl.pallas_call(
        body, name="fwd_mix", grid=(N_CHUNK,),
        out_shape=(jax.ShapeDtypeStruct((SEQ, D_MODEL), F32), jax.ShapeDtypeStruct((SEQ, D_MODEL), BF16),
                   jax.ShapeDtypeStruct((SEQ, RET_W), F32), jax.ShapeDtypeStruct((N_CHUNK, HEADS, HEAD_DIM, HEAD_DIM), BF16))
        + _gathered_shapes(ids),
        in_specs=[ch(D_MODEL), ch(PROJ_W), _resident((D_MODEL, D_MODEL)), _resident((1, RET_W)), _resident((1, SGU_W)),
                  _resident((1, SGU_W)), _resident(hcc), _resident(hcc), _resident(hcc), _resident(hcc), _resident(hcc), hbm],
        out_specs=(ch(D_MODEL), ch(D_MODEL), ch(RET_W), pl.BlockSpec((1, HEADS, HEAD_DIM, HEAD_DIM), lambda i: (i, 0, 0, 0)), hbm),
        scratch_shapes=[pltpu.VMEM((HEADS, HEAD_DIM, HEAD_DIM), F32)] + _gather_scratch(len(ids)),
        compiler_params=_cparams(("arbitrary",), collective=COLLECTIVE["fwd_mix"]),
    )(x, proj, wout_g, grn, lng, lnb, ws, bsb, mask, qdec, kdec, su)


def _conv_taps(p, prev8):
    row = lax.broadcasted_iota(jnp.int32, p.shape, 0)
    p1 = jnp.where(row == 0, prev8[7:8, :], pltpu.roll(p, 1, 0))
    p2 = jnp.where(row == 0, prev8[6:7, :], jnp.where(row == 1, prev8[7:8, :], pltpu.roll(p, 2, 0)))
    return p1, p2


def _fwd_ffn(x2, g2, wup_g, cw_g, cb_g, wdn_g, gf, tgt):
    def body(x_ref, g_ref, wu_ref, cw_ref, cb_ref, wd_ref, gf_ref, t_ref, h2_ref, up_ref, u_ref, act_ref, x3_ref, loss_ref, carry):
        @pl.when(pl.program_id(0) == 0)
        def _():
            carry[...] = jnp.zeros_like(carry)

        xb = x_ref[...]
        r = lax.rsqrt(jnp.mean(xb * xb, axis=-1, keepdims=True) + EPS)
        h = ((xb * r) * g_ref[...]).astype(BF16)
        h2_ref[...] = h
        acc = xb
        for t0, tw in FF_TILES:
            u = []
            for c0 in (t0, D_FF + t0):
                cs = slice(c0, c0 + tw)
                p = _dot_nt(h, wu_ref[pl.ds(c0, tw), :])
                up_ref[:, cs] = p.astype(BF16)
                p1, p2 = _conv_taps(p, carry[:, cs])
                carry[:, cs] = p[TM - 8:, :]
                us = p2 * cw_ref[0:1, cs] + p1 * cw_ref[1:2, cs] + p * cw_ref[2:3, cs] + cb_ref[:, cs]
                u_ref[:, cs] = us.astype(BF16)
                u.append(us)
            a = ((u[0] * _sigmoid(u[0])) * u[1]).astype(BF16)
            act_ref[:, t0:t0 + tw] = a
            acc = acc + _dot(a, wd_ref[pl.ds(t0, tw), :])
        x3_ref[...] = acc
        r3 = lax.rsqrt(jnp.mean(acc * acc, axis=-1, keepdims=True) + EPS)
        diff = (acc * r3) * gf_ref[...] - t_ref[...]
        loss_ref[...] = jnp.full(loss_ref.shape, 0.5 * jnp.sum(jnp.mean(diff * diff, axis=-1)), F32)

    tok = lambda w: pl.BlockSpec((TM, w), lambda i: (i, 0))
    return pl.pallas_call(
        body, name="fwd_ffn", grid=(N_TB,),
        out_shape=(jax.ShapeDtypeStruct((SEQ, D_MODEL), BF16), jax.ShapeDtypeStruct((SEQ, 2 * D_FF), BF16),
                   jax.ShapeDtypeStruct((SEQ, 2 * D_FF), BF16),
                   jax.ShapeDtypeStruct((SEQ, D_FF), BF16), jax.ShapeDtypeStruct((SEQ, D_MODEL), F32),
                   jax.ShapeDtypeStruct((N_TB, 8, 128), F32)),
        in_specs=[tok(D_MODEL), _resident((1, D_MODEL)), _resident((2 * D_FF, D_MODEL)), _resident((8, 2 * D_FF)),
                  _resident((1, 2 * D_FF)), _resident((D_FF, D_MODEL)), _resident((1, D_MODEL)), tok(D_MODEL)],
        out_specs=(tok(D_MODEL), tok(2 * D_FF), tok(2 * D_FF), tok(D_FF), tok(D_MODEL),
                   pl.BlockSpec((1, 8, 128), lambda i: (i, 0, 0))),
        scratch_shapes=[pltpu.VMEM((8, 2 * D_FF), F32)],
        compiler_params=_cparams(("arbitrary",)),
    )(x2, g2, wup_g, cw_g, cb_g, wdn_g, gf, tgt)


def _bwd_ffn(x3, tgt, gf, x2, g2, up_pre, u_conv, wup_g, cw_g, wdn_g):
    def body(x3_ref, t_ref, gf_ref, x2_ref, g2_ref, up_ref, u_ref, wu_ref, cw_ref, wd_ref,
             dx3_ref, dpre_ref, dx2_ref, dgf_ref, dg2_ref, dcv_ref, nxt):
        i = pl.program_id(0)

        @pl.when(i == 0)
        def _():
            nxt[...] = jnp.zeros_like(nxt)
            dgf_ref[...] = jnp.zeros_like(dgf_ref)
            dg2_ref[...] = jnp.zeros_like(dg2_ref)
            dcv_ref[...] = jnp.zeros_like(dcv_ref)

        x3 = x3_ref[...]
        r3 = lax.rsqrt(jnp.mean(x3 * x3, axis=-1, keepdims=True) + EPS)
        xh3 = x3 * r3
        dy = (xh3 * gf_ref[...] - t_ref[...]) * (1.0 / D_MODEL)
        dgf_ref[0:1, :] += jnp.sum(dy * xh3, axis=0, keepdims=True)
        t3 = dy * gf_ref[...]
        dx3 = r3 * (t3 - xh3 * jnp.mean(t3 * xh3, axis=-1, keepdims=True))
        dx3b = dx3.astype(BF16)
        dx3_ref[...] = dx3b
        dh2 = jnp.zeros((TM, D_MODEL), F32)
        for t0, tw in FF_TILES:
            row = lax.broadcasted_iota(jnp.int32, (TM, tw), 0)
            ts = slice(t0, t0 + tw)
            dact = _dot_nt(dx3b, wd_ref[pl.ds(t0, tw), :])
            ua = u_ref[:, ts].astype(F32)
            ub = u_ref[:, D_FF + t0:D_FF + t0 + tw].astype(F32)
            sg = _sigmoid(ua)
            du = [dact * ub * (sg * (1.0 + ua * (1.0 - sg))), dact * (ua * sg)]
            for n in range(2):
                d = du[n]
                c0 = n * D_FF + t0
                cs = slice(c0, c0 + tw)
                nx = nxt[:, cs]
                n1 = jnp.where(row == TM - 1, nx[0:1, :], pltpu.roll(d, TM - 1, 0))
                n2 = jnp.where(row == TM - 2, nx[0:1, :], jnp.where(row == TM - 1, nx[1:2, :], pltpu.roll(d, TM - 2, 0)))
                nxt[:, cs] = d[0:8, :]
                dp = (d * cw_ref[2:3, cs] + n1 * cw_ref[1:2, cs] + n2 * cw_ref[0:1, cs]).astype(BF16)
                dpre_ref[:, cs] = dp
                p = up_ref[:, cs].astype(F32)
                dcv_ref[n, 0:1, ts] += jnp.sum(n2 * p, axis=0, keepdims=True)
                dcv_ref[n, 1:2, ts] += jnp.sum(n1 * p, axis=0, keepdims=True)
                dcv_ref[n, 2:3, ts] += jnp.sum(d * p, axis=0, keepdims=True)
                dcv_ref[n, 3:4, ts] += jnp.sum(d, axis=0, keepdims=True)
                dh2 = dh2 + _dot(dp, wu_ref[pl.ds(c0, tw), :])
        x2 = x2_ref[...]
        r2 = lax.rsqrt(jnp.mean(x2 * x2, axis=-1, keepdims=True) + EPS)
        xh2 = x2 * r2
        dg2_ref[0:1, :] += jnp.sum(dh2 * xh2, axis=0, keepdims=True)
        t2 = dh2 * g2_ref[...]
        dx2_ref[...] = dx3 + r2 * (t2 - xh2 * jnp.mean(t2 * xh2, axis=-1, keepdims=True))

    rev = lambda w: pl.BlockSpec((TM, w), lambda i: (N_TB - 1 - i, 0))
    acc = lambda s: pl.BlockSpec(s, lambda i: (0,) * len(s))
    return pl.pallas_call(
        body, name="bwd_ffn", grid=(N_TB,),
        out_shape=(jax.ShapeDtypeStruct((SEQ, D_MODEL), BF16), jax.ShapeDtypeStruct((SEQ, 2 * D_FF), BF16),
                   jax.ShapeDtypeStruct((SEQ, D_MODEL), F32), jax.ShapeDtypeStruct((8, D_MODEL), F32),
                   jax.ShapeDtypeStruct((8, D_MODEL), F32), jax.ShapeDtypeStruct((2, 8, D_FF), F32)),
        in_specs=[rev(D_MODEL), rev(D_MODEL), _resident((1, D_MODEL)), rev(D_MODEL), _resident((1, D_MODEL)), rev(2 * D_FF),
                  rev(2 * D_FF), _resident((2 * D_FF, D_MODEL)), _resident((8, 2 * D_FF)), _resident((D_FF, D_MODEL))],
        out_specs=(rev(D_MODEL), rev(2 * D_FF), rev(D_MODEL), acc((8, D_MODEL)), acc((8, D_MODEL)), acc((2, 8, D_FF))),
        scratch_shapes=[pltpu.VMEM((8, 2 * D_FF), F32)],
        compiler_params=_cparams(("arbitrary",)),
    )(x3, tgt, gf, x2, g2, up_pre, u_conv, wup_g, cw_g, wdn_g)


def _bwd_mix(dx2, proj, o, sprev, wout_g, grn, lng, lnb, ws, bsb, mask, qdec, kdec, cos2, sin2, hosted):
    cdec = _chunk_decay()
    geoms = [g for g, _ in hosted]
    n_h = len(hosted)

    def body(dx2_ref, p_ref, o_ref, sp_ref, w_ref, grn_ref, lng_ref, lnb_ref, ws_ref, bsb_ref, m_ref, qd_ref, kd_ref,
             cos_ref, sin_ref, *rest):
        dp_ref, dgrn_ref, dlng_ref, dlnb_ref, dws_ref, dbs_ref = rest[n_h:n_h + 6]
        dstate, dbs_acc = rest[2 * n_h + 6:2 * n_h + 8]
        i = pl.program_id(0)
        rs = _Scatters(geoms, rest[:n_h], rest[n_h + 6:2 * n_h + 6], rest[2 * n_h + 8:])
        pl.when(i == 0)(rs.phase1)
        pl.when(i == 3)(rs.phase2)
        pl.when(i == 8)(rs.phase2b)

        @pl.when(i == 0)
        def _():
            dstate[...] = jnp.zeros_like(dstate)
            dgrn_ref[...] = jnp.zeros_like(dgrn_ref)
            dlng_ref[...] = jnp.zeros_like(dlng_ref)
            dlnb_ref[...] = jnp.zeros_like(dlnb_ref)
            dws_ref[...] = jnp.zeros_like(dws_ref)
            dbs_ref[...] = jnp.zeros_like(dbs_ref)
            dbs_acc[...] = jnp.zeros_like(dbs_acc)

        dmix = _dot_nt(dx2_ref[...].astype(BF16), w_ref[...])
        for h in range(HEADS):
            sl = slice(h * HEAD_DIM, (h + 1) * HEAD_DIM)
            q = p_ref[:, sl]
            k = p_ref[:, RET_W + h * HEAD_DIM:RET_W + (h + 1) * HEAD_DIM]
            v = p_ref[:, 2 * RET_W + h * HEAD_DIM:2 * RET_W + (h + 1) * HEAD_DIM]
            g = p_ref[:, 3 * RET_W + h * HEAD_DIM:3 * RET_W + (h + 1) * HEAD_DIM]
            o = o_ref[:, sl]
            rinv = lax.rsqrt(jnp.mean(o * o, axis=-1, keepdims=True) + EPS)
            oh = o * rinv
            gr = grn_ref[:, sl]
            sg = _sigmoid(g)
            dret = dmix[:, sl]
            dp_ref[:, 3 * RET_W + h * HEAD_DIM:3 * RET_W + (h + 1) * HEAD_DIM] = (
                dret * (oh * gr) * (sg * (1.0 + g * (1.0 - sg)))).astype(BF16)
            drn = dret * (g * sg)
            dgrn_ref[0:1, sl] += jnp.sum(drn * oh, axis=0, keepdims=True)
            t = drn * gr
            do = rinv * (t - oh * jnp.mean(t * oh, axis=-1, keepdims=True))
            qb, kb, vb, dob = q.astype(BF16), k.astype(BF16), v.astype(BF16), do.astype(BF16)
            m = m_ref[h]
            ab = (_dot_nt(qb, kb) * m).astype(BF16)
            dab = (_dot_nt(dob, vb) * m).astype(BF16)
            spb = sp_ref[0, h]
            dsn = dstate[h]
            dsnb = dsn.astype(BF16)
            qdb = (q * qd_ref[h]).astype(BF16)
            kdb = (k * kd_ref[h]).astype(BF16)
            dq = _dot(dab, kb) + _dot_nt(dob, spb) * qd_ref[h]
            dk = _dot_tn(dab, qb) + _dot_nt(vb, dsnb) * kd_ref[h]
            dv = _dot_tn(ab, dob) + _dot(kdb, dsnb)
            dstate[h] = dsn * cdec[h] + _dot_tn(qdb, dob)
            c2, s2 = cos_ref[...], sin_ref[...]
            dp_ref[:, sl] = _rot_t(dq, c2, s2).astype(BF16)
            dp_ref[:, RET_W + h * HEAD_DIM:RET_W + (h + 1) * HEAD_DIM] = _rot_t(dk * K_SCALE, c2, s2).astype(BF16)
            dp_ref[:, 2 * RET_W + h * HEAD_DIM:2 * RET_W + (h + 1) * HEAD_DIM] = dv.astype(BF16)
        for gi in range(HEADS):
            sl = slice(gi * HEAD_DIM, (gi + 1) * HEAD_DIM)
            u = p_ref[:, 4 * RET_W + gi * HEAD_DIM:4 * RET_W + (gi + 1) * HEAD_DIM]
            sv = p_ref[:, 4 * RET_W + SGU_W + gi * HEAD_DIM:4 * RET_W + SGU_W + (gi + 1) * HEAD_DIM]
            gv = _gelu(sv)
            xc = gv - jnp.mean(gv, axis=-1, keepdims=True)
            rstd = lax.rsqrt(jnp.mean(xc * xc, axis=-1, keepdims=True) + EPS)
            xh = xc * rstd
            lg = lng_ref[:, sl]
            vnb = (xh * lg + lnb_ref[:, sl]).astype(BF16)
            wcb = _causal(ws_ref[gi]).astype(BF16)
            mixed = _dot(wcb, vnb) + bsb_ref[gi]
            dsgu = dmix[:, RET_W + gi * HEAD_DIM:RET_W + (gi + 1) * HEAD_DIM]
            dmixed = dsgu * _gelu(u)
            dmb = dmixed.astype(BF16)
            dws_ref[gi] += _causal(_dot_nt(dmb, vnb))
            dbs_acc[gi] += dmixed
            dvn = _dot_tn(wcb, dmb)
            dlng_ref[gi:gi + 1, :] += jnp.sum(dvn * xh, axis=0, keepdims=True)
            dlnb_ref[gi:gi + 1, :] += jnp.sum(dvn, axis=0, keepdims=True)
            dxh = dvn * lg
            dgv = rstd * (dxh - jnp.mean(dxh, axis=-1, keepdims=True) - xh * jnp.mean(dxh * xh, axis=-1, keepdims=True))
            dp_ref[:, 4 * RET_W + gi * HEAD_DIM:4 * RET_W + (gi + 1) * HEAD_DIM] = (dsgu * mixed * _gelu_grad(u)).astype(BF16)
            dp_ref[:, 4 * RET_W + SGU_W + gi * HEAD_DIM:4 * RET_W + SGU_W + (gi + 1) * HEAD_DIM] = (
                dgv * _gelu_grad(sv)).astype(BF16)

        @pl.when(i == N_CHUNK - 1)
        def _():
            for gi in range(HEADS):
                col = jnp.broadcast_to(jnp.sum(dbs_acc[gi], axis=-1, keepdims=True), (CHUNK, CHUNK))
                dbs_ref[gi:gi + 1, :] = jnp.transpose(col)[0:1, :]
            rs.phase3()

    rev = lambda w: pl.BlockSpec((CHUNK, w), lambda i: (N_CHUNK - 1 - i, 0))
    hcc = (HEADS, CHUNK, CHUNK)
    acc = lambda s: pl.BlockSpec(s, lambda i: (0,) * len(s))
    res = pl.pallas_call(
        body, name="bwd_mix", grid=(N_CHUNK,),
        out_shape=(jax.ShapeDtypeStruct((SEQ, PROJ_W), BF16), jax.ShapeDtypeStruct((8, RET_W), F32),
                   jax.ShapeDtypeStruct((8, HEAD_DIM), F32), jax.ShapeDtypeStruct((8, HEAD_DIM), F32),
                   jax.ShapeDtypeStruct(hcc, F32), jax.ShapeDtypeStruct((8, CHUNK), F32)) + _scatter_out_shapes(geoms),
        in_specs=[rev(D_MODEL), rev(PROJ_W), rev(RET_W),
                  pl.BlockSpec((1, HEADS, HEAD_DIM, HEAD_DIM), lambda i: (N_CHUNK - 1 - i, 0, 0, 0)),
                  _resident((D_MODEL, D_MODEL)), _resident((1, RET_W)), _resident((1, SGU_W)), _resident((1, SGU_W)),
                  _resident(hcc), _resident(hcc), _resident(hcc), _resident(hcc), _resident(hcc), rev(HEAD_DIM), rev(HEAD_DIM)]
        + [pl.BlockSpec(memory_space=pl.ANY)] * n_h,
        out_specs=(rev(PROJ_W), acc((8, RET_W)), acc((8, HEAD_DIM)), acc((8, HEAD_DIM)), acc(hcc), acc((8, CHUNK)))
        + _scatter_out_specs(geoms),
        scratch_shapes=[pltpu.VMEM((HEADS, HEAD_DIM, HEAD_DIM), F32), pltpu.VMEM((HEADS, CHUNK, CHUNK), F32)] + _scatter_scratch(geoms),
        compiler_params=_cparams(("arbitrary",), collective=COLLECTIVE["bwd_mix"]),
    )(dx2, proj, o, sprev, wout_g, grn, lng, lnb, ws, bsb, mask, qdec, kdec, cos2, sin2, *[p for _, p in hosted])
    return tuple(res[:6 + n_h])


def _bwd_proj(dproj, win_g, x, g1, dx2, gin_p, small):
    geoms = [W_IN]
    n_s = len(small)

    def body(dp_ref, w_ref, x_ref, g_ref, dx2_ref, gin_ref, *rest):
        small_refs = rest[:n_s]
        dx_ref, rs_out, rp_ref, rws_ref, rcv_ref, dg_ref = rest[n_s:n_s + 6]
        rs_scratch = rest[n_s + 6:n_s + 6 + N_SCATTER_SCRATCH]
        ar_scratch = rest[n_s + 6 + N_SCATTER_SCRATCH:]
        ar_res = ar_scratch[N_SMALL_SCRATCH:]
        ar = _SmallReduce((dg_ref,) + tuple(small_refs), ar_res, ar_scratch[:N_SMALL_SCRATCH])
        rs = _Scatters(geoms, [gin_ref], [rs_out], rs_scratch)
        pl.when(pl.program_id(0) == 0)(lambda: rs.phase1(diagonal=True))
        pl.when(pl.program_id(0) == 1)(rs.phase2)
        pl.when(pl.program_id(0) == 5)(rs.phase2b)

        @pl.when(pl.program_id(0) == 0)
        def _():
            dg_ref[...] = jnp.zeros_like(dg_ref)

        dh = _dot_nt(dp_ref[...], w_ref[...])
        xb = x_ref[...]
        r = lax.rsqrt(jnp.mean(xb * xb, axis=-1, keepdims=True) + EPS)
        xh = xb * r
        dg_ref[0:1, :] += jnp.sum(dh * xh, axis=0, keepdims=True)
        t = dh * g_ref[...]
        dx_ref[...] = dx2_ref[...] + r * (t - xh * jnp.mean(t * xh, axis=-1, keepdims=True))

        @pl.when(pl.program_id(0) == N_TB - 1)
        def _():
            ar.begin()
            rs.phase3()
            ar.end()
            for o_ref, r_ref in zip((rp_ref, rws_ref, rcv_ref), ar_res):
                o_ref[...] = r_ref[...]

    tok = lambda w: pl.BlockSpec((TM, w), lambda i: (i, 0))
    vm = pl.BlockSpec(memory_space=pltpu.VMEM)
    res = pl.pallas_call(
        body, name="bwd_proj", grid=(N_TB,),
        out_shape=(jax.ShapeDtypeStruct((SEQ, D_MODEL), F32),) + _scatter_out_shapes(geoms)
        + tuple(jax.ShapeDtypeStruct(s, F32) for s in SMALL_FULL),
        in_specs=[tok(PROJ_W), _resident((D_MODEL, PROJ_W)), tok(D_MODEL), _resident((1, D_MODEL)), tok(D_MODEL),
                  pl.BlockSpec(memory_space=pl.ANY)] + [vm] * n_s,
        out_specs=(tok(D_MODEL),) + _scatter_out_specs(geoms) + (vm,) * len(SMALL_FULL),
        scratch_shapes=[pltpu.VMEM((8, D_MODEL), F32)] + _scatter_scratch(geoms) + _small_scratch()
        + [pltpu.VMEM(s, F32) for s in SMALL_FULL],
        compiler_params=_cparams(("arbitrary",), collective=COLLECTIVE["bwd_proj"]),
    )(dproj, win_g, x, g1, dx2, gin_p, *small)
    return res


def _wgrad(name, a, b, tm=None, tn=None, hosted=()):
    m_w, n_w = a.shape[-1], b.shape[-1]
    tm = m_w if tm is None else tm
    tn = n_w if tn is None else tn
    n_steps = (m_w // tm) * (n_w // tn)
    geoms = [g for g, _ in hosted]
    n_h = len(hosted)

    def body(a_ref, b_ref, *rest):
        o_ref = rest[n_h]
        if n_h:
            rs = _Scatters(geoms, rest[:n_h], rest[n_h + 1:2 * n_h + 1], rest[2 * n_h + 1:])
            step = pl.program_id(0) * (n_w // tn) + pl.program_id(1)
            pl.when(step == 0)(rs.phase1)
            pl.when(step == 1)(rs.phase2)
            pl.when(step == n_steps // 2)(rs.phase2b)
        o_ref[...] = _dot_tn(a_ref[...].astype(BF16), b_ref[...].astype(BF16)).astype(BF16)
        if n_h:
            pl.when(step == n_steps - 1)(rs.phase3)

    assert not n_h or n_steps >= 4
    res = pl.pallas_call(
        body, name=name, grid=(m_w // tm, n_w // tn),
        out_shape=(jax.ShapeDtypeStruct((m_w, n_w), BF16),) + _scatter_out_shapes(geoms),
        in_specs=[pl.BlockSpec((SEQ, tm), lambda i, j: (0, i)), pl.BlockSpec((SEQ, tn), lambda i, j: (0, j))]
        + [pl.BlockSpec(memory_space=pl.ANY)] * n_h,
        out_specs=(pl.BlockSpec((tm, tn), lambda i, j: (i, j)),) + _scatter_out_specs(geoms),
        scratch_shapes=_scatter_scratch(geoms),
        compiler_params=_cparams(("arbitrary", "arbitrary"), collective=COLLECTIVE[name]) if n_h else _cparams(("parallel", "parallel")),
    )(a, b, *[p for _, p in hosted])
    return tuple(res[:1 + n_h])


def _row_step(half_rows):
    return max(s for s in range(16, 177, 16) if half_rows % s == 0)


class _Scatter:
    def __init__(self, geom, partial, out, land1, mine, stage2, land2, comb, s1_send, s1_recv, s2_send, s2_recv, ld_sems):
        self.w, self.row0, self.shape = _geom(geom)
        self.partial, self.out, self.land1 = partial, out, land1
        self.mine, self.stage2, self.land2, self.comb = mine, stage2, land2, comb
        self.hr = self.shape[0] // 2
        self.step = _row_step(self.hr)
        self.s1_send, self.s1_recv, self.s2_send, self.s2_recv, self.ld_sems = s1_send, s1_recv, s2_send, s2_recv, ld_sems
        self.x, self.y, self.c = lax.axis_index("x"), lax.axis_index("y"), lax.axis_index("c")
        self.sibling = (self.x, self.y, 1 - self.c)
        self.chips = [(self.x, self.y), (1 - self.x, self.y), (self.x, 1 - self.y), (1 - self.x, 1 - self.y)]

    def block(self, px, py, pc):
        dev = 4 * px + 2 * py + pc
        if self.w == W_IN:
            return self.partial.at[:, pl.ds(pl.multiple_of(dev * IN_SHARD, 128), IN_SHARD)]
        if self.w == W_OUT:
            return self.partial.at[pl.ds(pl.multiple_of(dev * OUT_SHARD, 128), OUT_SHARD), :]
        if self.w == W_DOWN:
            return self.partial.at[pl.ds(pl.multiple_of(dev * DOWN_SHARD, 32), DOWN_SHARD), :]
        return self.partial.at[pl.ds(pl.multiple_of(dev * FF_SHARD + self.row0, 32), self.shape[0]), :]

    def copy1(self, k):
        return pltpu.make_async_remote_copy(
            src_ref=self.block(*self.chips[k], 1 - self.c), dst_ref=self.land1.at[k],
            send_sem=self.s1_send.at[k], recv_sem=self.s1_recv.at[k], device_id=self.sibling, device_id_type=MESH)

    STAGE2 = [(1, 0, 1), (3, 0, 1), (2, 1, 2), (3, 1, 2), (1, 1, 1), (2, 0, 2)]

    def copy2(self, j):
        blk, h, to = self.STAGE2[j]
        src = self.comb.at[j - 4] if j >= 4 else self.stage2.at[blk - 1, pl.ds(h * self.hr, self.hr), :]
        return pltpu.make_async_remote_copy(
            src_ref=src, dst_ref=self.land2.at[j], send_sem=self.s2_send.at[j], recv_sem=self.s2_recv.at[j],
            device_id=(*self.chips[to], self.c), device_id_type=MESH)

    def _rows(self, h=None):
        step = self.step
        lo, n = (0, self.shape[0]) if h is None else (h * self.hr, self.hr)
        return [pl.ds(r0, step) for r0 in range(lo, lo + n, step)]

    def load(self, k):
        return pltpu.make_async_copy(self.block(*self.chips[k], self.c), self.mine.at[k], self.ld_sems.at[k])

    def load_mine(self):
        for k in range(4):
            self.load(k).start()

    def phase1(self):
        for k in range(4):
            self.copy1(k).start()

    def phase2(self, k):
        self.copy1(k).wait_recv()
        self.load(k).wait()
        for rs in self._rows():
            s = self.mine[k, rs, :].astype(F32) + self.land1[k, rs, :].astype(F32)
            if k == 0:
                self.out[rs, :] = s
            else:
                self.stage2[k - 1, rs, :] = s.astype(BF16)
        for j in {3: (1, 3), 1: (0,), 2: (2,), 0: ()}[k]:
            self.copy2(j).start()

    def phase2b(self):
        for j, got in ((4, 3), (5, 1)):
            blk, h, _ = self.STAGE2[j]
            self.copy2(got).wait_recv()
            for i, rs in enumerate(self._rows(h)):
                lr = pl.ds(i * self.step, self.step)
                self.comb[j - 4, lr, :] = (self.stage2[blk - 1, rs, :].astype(F32) + self.land2[got, lr, :].astype(F32)).astype(BF16)
            self.copy2(j).start()

    def phase3(self):
        for j in (0, 5, 4, 2):
            self.copy2(j).wait_recv()
        for h, (first, second) in enumerate(((0, 5), (4, 2))):
            for i, rs in enumerate(self._rows(h)):
                lr = pl.ds(i * self.step, self.step)
                self.out[rs, :] = (self.out[rs, :] + self.land2[first, lr, :].astype(F32)) + self.land2[second, lr, :].astype(F32)
        for k in range(4):
            self.copy1(k).wait_send()
        for j in range(6):
            self.copy2(j).wait_send()


def _geom(geom):
    if isinstance(geom, tuple):
        w, row0, rows = geom
        assert w == W_UP
        return w, row0, (rows, SHARD[w][1])
    return geom, 0, SHARD[geom]


N_SCATTER_SCRATCH = 10


def _scatter_out_shapes(geoms):
    return tuple(jax.ShapeDtypeStruct(_geom(g)[2], F32) for g in geoms)


def _scatter_out_specs(geoms):
    return (pl.BlockSpec(memory_space=pltpu.VMEM),) * len(geoms)


def _scatter_scratch(geoms):
    out = []
    for g in geoms:
        s = _geom(g)[2]
        hs = (s[0] // 2, s[1])
        out += [pltpu.VMEM((4,) + s, BF16), pltpu.VMEM((4,) + s, BF16), pltpu.VMEM((3,) + s, BF16), pltpu.VMEM((6,) + hs, BF16),
                pltpu.VMEM((2,) + hs, BF16),
                pltpu.SemaphoreType.DMA((4,)), pltpu.SemaphoreType.DMA((4,)), pltpu.SemaphoreType.DMA((6,)),
                pltpu.SemaphoreType.DMA((6,)), pltpu.SemaphoreType.DMA((4,))]
    return out


class _Scatters:
    def __init__(self, geoms, p_refs, out_refs, scratch):
        k = N_SCATTER_SCRATCH
        self.items = [_Scatter(g, p_refs[i], out_refs[i], *scratch[k * i:k * i + k]) for i, g in enumerate(geoms)]

    def phase1(self, diagonal=False):
        meet = _Meet(diagonal)
        meet.signal()
        for s in self.items:
            s.load_mine()
        meet.wait()
        for s in self.items:
            s.phase1()

    def phase2(self):
        for k in (3, 1, 2, 0):
            for s in self.items:
                s.phase2(k)

    def phase2b(self):
        for s in self.items:
            s.phase2b()

    def phase3(self):
        for s in self.items:
            s.phase3()


PACK_W = 1024


SMALL_FULL = [(2, 8, PACK_W), (HEADS, CHUNK, CHUNK), (2, 8, D_FF)]
SMALL_HALF = [(s[0] // 2,) + s[1:] for s in SMALL_FULL]
N_SMALL_SCRATCH = 16


def _small_scratch():
    n_a = len(SMALL_FULL)
    return ([pltpu.VMEM(SMALL_FULL[0], F32)] + [pltpu.VMEM(s, F32) for s in SMALL_HALF] + [pltpu.VMEM(s, F32) for s in SMALL_HALF]
            + [pltpu.VMEM((3,) + s, F32) for s in SMALL_HALF]
            + [pltpu.SemaphoreType.DMA((n_a,)), pltpu.SemaphoreType.DMA((n_a,)), pltpu.SemaphoreType.DMA((n_a, 3)),
               pltpu.SemaphoreType.DMA((n_a, 3)), pltpu.SemaphoreType.DMA((n_a,)), pltpu.SemaphoreType.DMA((n_a,))])


class _SmallReduce:
    def __init__(self, ins, outs, scratch):
        self.ins, self.outs = ins, outs
        (self.pack, *rest) = scratch
        self.rxs, self.css, self.gs = rest[0:3], rest[3:6], rest[6:9]
        self.s1_send, self.s1_recv, self.s2_send, self.s2_recv, self.s3_send, self.s3_recv = rest[9:]
        self.x, self.y, self.c = lax.axis_index("x"), lax.axis_index("y"), lax.axis_index("c")
        self.sibling = (self.x, self.y, 1 - self.c)
        self.chips = [(1 - self.x, self.y), (self.x, 1 - self.y), (1 - self.x, 1 - self.y)]
        self.hl = [s[0] for s in SMALL_HALF]

    def half(self, ref, a, h):
        return ref.at[pl.ds(h * self.hl[a], self.hl[a])]

    def begin(self):
        dg1_ref, dg2_ref, dgf_ref, dgrn_ref, dlng_ref, dlnb_ref, dbs_ref, loss_ref, dws_ref, dcv_ref = self.ins
        pack, c = self.pack, self.c
        pack[...] = jnp.zeros_like(pack)
        pack[0, 0:1, :] = dg1_ref[0:1, :]
        pack[0, 1:2, :] = dg2_ref[0:1, :]
        pack[0, 2:3, :] = dgf_ref[0:1, :]
        pack[0, 3:4, 0:RET_W] = dgrn_ref[0:1, :]
        lsum = loss_ref[0, 0:1, :]
        for i in range(1, N_TB):
            lsum = lsum + loss_ref[i, 0:1, :]
        pack[0, 3:4, RET_W:RET_W + 128] = lsum
        pack[1, 0:HEADS, 0:128] = dlng_ref[0:HEADS, :]
        pack[1, 0:HEADS, 128:256] = dlnb_ref[0:HEADS, :]
        pack[1, 0:HEADS, 256:384] = dbs_ref[0:HEADS, :]
        self.srcs = [pack, dws_ref, dcv_ref]
        n_a = len(self.srcs)
        self.ex1 = [pltpu.make_async_remote_copy(src_ref=self.half(self.srcs[a], a, 1 - c), dst_ref=self.rxs[a],
                                                 send_sem=self.s1_send.at[a], recv_sem=self.s1_recv.at[a],
                                                 device_id=self.sibling, device_id_type=MESH) for a in range(n_a)]
        for cp in self.ex1:
            cp.start()
        self.ex2 = []
        for a in range(n_a):
            self.ex1[a].wait_recv()
            self.css[a][...] = self.half(self.srcs[a], a, c)[...] + self.rxs[a][...]
            for j, chip in enumerate(self.chips):
                cp = pltpu.make_async_remote_copy(src_ref=self.css[a], dst_ref=self.gs[a].at[j], send_sem=self.s2_send.at[a, j],
                                                  recv_sem=self.s2_recv.at[a, j], device_id=(*chip, c), device_id_type=MESH)
                cp.start()
                self.ex2.append(cp)

    def end(self):
        c, x, y = self.c, self.x, self.y
        ex3 = []
        for a in range(len(self.srcs)):
            css, gs, out = self.css[a], self.gs[a], self.outs[a]
            for j in range(3):
                self.ex2[3 * a + j].wait_recv()
            tot = None
            for q in range(4):
                k = jnp.where(x != (q >> 1), 1, 0) + jnp.where(y != (q & 1), 2, 0)
                term = jnp.where(k == 0, css[...], jnp.where(k == 1, gs[0], jnp.where(k == 2, gs[1], gs[2])))
                tot = term if tot is None else tot + term
            self.half(out, a, c)[...] = tot
            cp = pltpu.make_async_remote_copy(src_ref=self.half(out, a, c), dst_ref=self.half(out, a, c), send_sem=self.s3_send.at[a],
                                              recv_sem=self.s3_recv.at[a], device_id=self.sibling, device_id_type=MESH)
            cp.start()
            ex3.append(cp)
        for a in range(len(self.srcs)):
            out = self.outs[a]
            pltpu.make_async_remote_copy(src_ref=self.half(out, a, 1 - c), dst_ref=self.half(out, a, 1 - c), send_sem=self.s3_send.at[a],
                                         recv_sem=self.s3_recv.at[a], device_id=self.sibling, device_id_type=MESH).wait_recv()
        for cp in self.ex1 + self.ex2 + ex3:
            cp.wait_send()


def _adam_math(w, g, m, v):
    nm = ADAM_B1 * m + (1.0 - ADAM_B1) * g
    nv = ADAM_B2 * v + (1.0 - ADAM_B2) * (g * g)
    d = -ADAM_LR * ((nm / (1.0 - ADAM_B1 ** ADAM_STEP)) / (jnp.sqrt(nv / (1.0 - ADAM_B2 ** ADAM_STEP)) + ADAM_EPS) + ADAM_WD * w)
    return d, nm, nv


def _adamw(name, w, gs, m, v, n_steps, thru=()):
    _, r, cdim = w.shape
    n_g, n_t = len(gs), len(thru)
    edges = [0, r // n_steps] if n_g == 1 else [sum(g.shape[0] for g in gs[:k]) for k in range(n_g + 1)]

    def body(w_ref, *rest):
        g_refs, (m_ref, v_ref), t_refs = rest[:n_g], rest[n_g:n_g + 2], rest[n_g + 2:n_g + 2 + n_t]
        go_ref, d_ref, nm_ref, nv_ref, *to_refs = rest[n_g + 2 + n_t:]
        for g_ref, lo, hi in zip(g_refs, edges[:-1], edges[1:]):
            gg = g_ref[...]
            go_ref[0, lo:hi, :] = gg
            d_ref[0, lo:hi, :], nm_ref[0, lo:hi, :], nv_ref[0, lo:hi, :] = _adam_math(
                w_ref[0, lo:hi, :], gg, m_ref[0, lo:hi, :], v_ref[0, lo:hi, :])
        for t_ref, to_ref in zip(t_refs, to_refs):
            to_ref[...] = t_ref[...]

    if n_g == 1:
        spec3 = pl.BlockSpec((1, r // n_steps, cdim), lambda i: (0, i, 0))
        g_specs = [pl.BlockSpec((r // n_steps, cdim), lambda i: (i, 0))]
    else:
        spec3 = pl.BlockSpec((1, r, cdim // n_steps), lambda i: (0, 0, i))
        g_specs = [pl.BlockSpec((g.shape[0], cdim // n_steps), lambda i: (0, i)) for g in gs]
    t_specs = [pl.BlockSpec((t.shape[0] // n_steps, t.shape[1]), lambda i: (i, 0)) for t in thru]
    sh = jax.ShapeDtypeStruct((1, r, cdim), F32)
    return pl.pallas_call(
        body, name=name, grid=(n_steps,), out_shape=(sh, sh, sh, sh) + tuple(jax.ShapeDtypeStruct(t.shape, t.dtype) for t in thru),
        in_specs=[spec3] + g_specs + [spec3, spec3] + t_specs, out_specs=(spec3,) * 4 + tuple(t_specs),
        compiler_params=_cparams(("parallel",)),
    )(w, *gs, m, v, *thru)


def _adamw_small(rp, rws, rcv, gcw, params):
    n_p = len(params)

    def body(*refs):
        rp_ref, rws_ref, rcv_ref, gcw_ref = refs[:4]
        ins = refs[4:4 + 3 * n_p]
        outs = refs[4 + 3 * n_p:]
        outs[4 * n_p][...] = rp_ref[0, 3:4, RET_W:RET_W + 1]
        grads = [rp_ref[0, 0:1, :], rp_ref[0, 1:2, :], rp_ref[0, 2:3, :], rp_ref[0, 3:4, 0:RET_W],
                 rp_ref[1, 0:HEADS, 0:128], rp_ref[1, 0:HEADS, 128:256], rp_ref[1, 0:HEADS, 256:384],
                 rws_ref[...], gcw_ref[...], None]
        for p in range(n_p):
            w_ref, m_ref, v_ref = ins[3 * p:3 * p + 3]
            o = outs[4 * p:4 * p + 4]
            if p == n_p - 1:
                for hf in range(2):
                    cs = slice(hf * D_FF, (hf + 1) * D_FF)
                    g = rcv_ref[hf, 3:4, :]
                    res = (g,) + _adam_math(w_ref[:, cs], g, m_ref[:, cs], v_ref[:, cs])
                    for t in range(4):
                        o[t][:, cs] = res[t]
                continue
            lead = w_ref.ndim > grads[p].ndim
            rd = (lambda r: r[0]) if lead else (lambda r: r[...])
            res = (grads[p],) + _adam_math(rd(w_ref), grads[p], rd(m_ref), rd(v_ref))
            for t in range(4):
                if lead:
                    o[t][0] = res[t]
                else:
                    o[t][...] = res[t]

    vm = pl.BlockSpec(memory_space=pltpu.VMEM)
    flat = [a for tr in params for a in tr]
    out_shape = tuple(jax.ShapeDtypeStruct(tr[0].shape, F32) for tr in params for _ in range(4)) + (jax.ShapeDtypeStruct((1, 1), F32),)
    res = pl.pallas_call(
        body, name="adamw_small", out_shape=out_shape, in_specs=[vm] * (4 + len(flat)), out_specs=(vm,) * len(out_shape),
        compiler_params=_cparams(),
    )(rp, rws, rcv, gcw, *flat)
    return [res[4 * p:4 * p + 4] for p in range(n_p)], res[4 * n_p]


def kernel(x, mix_norm_g, w_in, ret_norm_g, sgu_ln_g, sgu_ln_b, sgu_w_s, sgu_b_s, w_out, ffn_norm_g, w_up, conv_w, conv_b, w_down, final_norm_g, loss_target, m_mix_norm_g, m_w_in, m_ret_norm_g, m_sgu_ln_g, m_sgu_ln_b, m_sgu_w_s, m_sgu_b_s, m_w_out, m_ffn_norm_g, m_w_up, m_conv_w, m_conv_b, m_w_down, m_final_norm_g, v_mix_norm_g, v_w_in, v_ret_norm_g, v_sgu_ln_g, v_sgu_ln_b, v_sgu_w_s, v_sgu_b_s, v_w_out, v_ffn_norm_g, v_w_up, v_conv_w, v_conv_b, v_w_down, v_final_norm_g):
    xs = x[0]
    tgt = loss_target[0]
    mask, qdec, kdec = _decay_tables()
    grn = ret_norm_g.reshape(1, RET_W)
    lng = sgu_ln_g.reshape(1, SGU_W)
    lnb = sgu_ln_b.reshape(1, SGU_W)
    ws = sgu_w_s[0]
    bsb = jnp.broadcast_to(sgu_b_s[0][:, :, None], (HEADS, CHUNK, HEAD_DIM))
    gf = final_norm_g.reshape(1, D_MODEL)
    me = 4 * lax.axis_index("x") + 2 * lax.axis_index("y") + lax.axis_index("c")
    tr = lambda a: jnp.transpose(a[0])[None]
    tr_cw = lambda a: jnp.transpose(a, (1, 0, 2))

    proj, h1, cos2, sin2, win_g, cw_sh, wout_g, wdn_g, su = _fwd_proj(
        xs, mix_norm_g, _rope_freq(), w_in[0], w_out[0], tr(w_up)[0], w_down[0], tr_cw(conv_w))
    cw_g = jnp.transpose(cw_sh, (1, 0, 2)).reshape(8, 2 * D_FF)
    x2, mixcat, o, sprev, wup_g = _fwd_mix(xs, proj, wout_g, grn, lng, lnb, ws, bsb, mask, qdec, kdec, su)
    h2, up_pre, u_conv, act, x3, loss_parts = _fwd_ffn(x2, ffn_norm_g, wup_g, cw_g, conv_b, wdn_g, gf, tgt)

    dx3, dpre, dx2, dgf, dg2, dcv = _bwd_ffn(x3, tgt, gf, x2, ffn_norm_g, up_pre, u_conv, wup_g, cw_g, wdn_g)
    band = 512
    (gdn_p,) = _wgrad("wgrad_down", act, dx3, tm=FF_TILE)
    (gout_p,) = _wgrad("wgrad_out", mixcat, dx2, tn=512)
    gup_p, g_dn = _wgrad("wgrad_up", dpre, h2, tm=FF_TILE, tn=512, hosted=[(W_DOWN, gdn_p)])
    dproj, dgrn, dlng, dlnb, dws, dbs, g_up_a, g_out = _bwd_mix(
        dx2, proj, o, sprev, wout_g, grn, lng, lnb, ws, bsb, mask, qdec, kdec, cos2, sin2,
        [((W_UP, 0, band), gup_p), (W_OUT, gout_p)])
    gin_p, g_up_b = _wgrad("wgrad_in", h1, dproj, tm=512, tn=768, hosted=[((W_UP, band, FF_SHARD - band), gup_p)])
    grad_x, g_in, rp, rws, rcv = _bwd_proj(dproj, win_g, xs, mix_norm_g, dx2, gin_p,
                                           (dg2, dgf, dgrn, dlng, dlnb, dbs, loss_parts, dws, dcv))
    gcw = tr_cw(lax.dynamic_slice(rcv, (me // (N_DEV // 2), 0, (me % (N_DEV // 2)) * FF_SHARD), (1, 3, FF_SHARD)))

    table = {}
    *table["w_in"], grad_x = _adamw("adamw_w_in", w_in, [g_in], m_w_in, v_w_in, 4, thru=[grad_x])
    for name, w, gs, m, v, n_steps in (("w_out", w_out, [g_out], m_w_out, v_w_out, 1),
                                       ("w_up", tr(w_up), [g_up_a, g_up_b], tr(m_w_up), tr(v_w_up), 4),
                                       ("w_down", w_down, [g_dn], m_w_down, v_w_down, 4)):
        table[name] = _adamw("adamw_" + name, w, gs, m, v, n_steps)
    table["w_up"] = tuple(tr(a) for a in table["w_up"])
    row = lambda a: a.reshape(1, D_MODEL)
    names_small = ["mix_norm_g", "ffn_norm_g", "final_norm_g", "ret_norm_g", "sgu_ln_g", "sgu_ln_b", "sgu_b_s", "sgu_w_s",
                   "conv_w", "conv_b"]
    params = [(mix_norm_g, m_mix_norm_g, v_mix_norm_g), (ffn_norm_g, m_ffn_norm_g, v_ffn_norm_g),
              (row(final_norm_g), row(m_final_norm_g), row(v_final_norm_g)), (ret_norm_g, m_ret_norm_g, v_ret_norm_g),
              (sgu_ln_g, m_sgu_ln_g, v_sgu_ln_g), (sgu_ln_b, m_sgu_ln_b, v_sgu_ln_b), (sgu_b_s, m_sgu_b_s, v_sgu_b_s),
              (sgu_w_s, m_sgu_w_s, v_sgu_w_s), (tr_cw(conv_w), tr_cw(m_conv_w), tr_cw(v_conv_w)), (conv_b, m_conv_b, v_conv_b)]
    small, loss = _adamw_small(rp, rws, rcv, gcw, params)
    for n, res in zip(names_small, small):
        table[n] = res
    table["final_norm_g"] = tuple(a.reshape(D_MODEL) for a in table["final_norm_g"])
    table["conv_w"] = tuple(tr_cw(a) for a in table["conv_w"])

    order = ["mix_norm_g", "w_in", "ret_norm_g", "sgu_ln_g", "sgu_ln_b", "sgu_w_s", "sgu_b_s", "w_out", "ffn_norm_g", "w_up",
             "conv_w", "conv_b", "w_down", "final_norm_g"]
    outs = [loss.reshape(()), grad_x[None]]
    for col in range(4):
        outs += [table[n][col] for n in order]
    return tuple(outs)
```

```python
import functools
import math

import jax
import jax.numpy as jnp
import numpy as np
from jax import lax
from jax.experimental import pallas as pl
from jax.experimental.pallas import tpu as pltpu

F32 = jnp.float32
BF16 = jnp.bfloat16
MESH = pl.DeviceIdType.MESH

N_DEV = 8
SEQ = 2048
D_MODEL = 1024
CHUNK = 128
N_CHUNK = SEQ // CHUNK
HEADS = 4
HEAD_DIM = 128
RET_W = 512
SGU_W = 512
PROJ_W = 3072
D_FF = 2816
FF_SHARD = 704
FF_TILE = 1408
FF_TILES = ((0, 1536), (1536, 1280))
IN_SHARD = PROJ_W // N_DEV
OUT_SHARD = D_MODEL // N_DEV
DOWN_SHARD = D_FF // N_DEV
TM = 256
N_TB = SEQ // TM
FWD_PROJ_PASS_AT = 5
FWD_MIX_PASS_AT = 10
EPS = 1e-6
ROPE_BASE = 10000.0
K_SCALE = HEAD_DIM ** -0.5
INV_SQRT2 = 0.7071067811865476
INV_SQRT_2PI = 0.3989422804014327

ADAM_LR = 0.001
ADAM_B1 = 0.9
ADAM_B2 = 0.999
ADAM_EPS = 1e-08
ADAM_WD = 0.01
ADAM_STEP = 10

VMEM_LIMIT = 56 * 1024 * 1024


def _cparams(sem=None, vmem=VMEM_LIMIT, collective=None):
    return pltpu.CompilerParams(dimension_semantics=sem, vmem_limit_bytes=vmem, collective_id=collective)


COLLECTIVE = {name: k for k, name in enumerate(("fwd_proj", "fwd_mix", "wgrad_up", "bwd_mix", "wgrad_in", "bwd_proj"))}


class _Meet:
    def __init__(self, diagonal):
        x, y, c = lax.axis_index("x"), lax.axis_index("y"), lax.axis_index("c")
        self.peers = [(x, y, 1 - c), (1 - x, y, c), (x, 1 - y, c)] + ([(1 - x, 1 - y, c)] if diagonal else [])

    def signal(self):
        for peer in self.peers:
            pl.semaphore_signal(pltpu.get_barrier_semaphore(), inc=1, device_id=peer, device_id_type=MESH)

    def wait(self):
        pl.semaphore_wait(pltpu.get_barrier_semaphore(), len(self.peers))


def _resident(shape):
    nd = len(shape)
    return pl.BlockSpec(shape, lambda *_: (0,) * nd, pipeline_mode=pl.Buffered(1))


def _dot(a, b):
    return jnp.dot(a, b, preferred_element_type=F32)


def _dot_nt(a, b):
    return lax.dot_general(a, b, (((1,), (1,)), ((), ())), preferred_element_type=F32)


def _dot_tn(a, b):
    return lax.dot_general(a, b, (((0,), (0,)), ((), ())), preferred_element_type=F32)


def _sigmoid(x):
    return 1.0 / (1.0 + jnp.exp(-x))


def _gelu(x):
    return 0.5 * x * (1.0 + lax.erf(x * INV_SQRT2))


def _gelu_grad(x):
    return 0.5 * (1.0 + lax.erf(x * INV_SQRT2)) + x * (jnp.exp(-0.5 * x * x) * INV_SQRT_2PI)


def _rot(xh, cos2, sin2):
    return xh * cos2 + pltpu.roll(xh, HEAD_DIM // 2, 1) * sin2


def _rot_t(dh, cos2, sin2):
    return dh * cos2 + pltpu.roll(dh * sin2, HEAD_DIM // 2, 1)


def _rope_freq():
    half = HEAD_DIM // 2
    inv_freq = jnp.power(ROPE_BASE, -jnp.arange(half, dtype=F32) / half)
    return jnp.concatenate([inv_freq, inv_freq])[None, :]


def _rope_block(inv2, first_row):
    pos = (lax.broadcasted_iota(jnp.int32, (TM, HEAD_DIM), 0) + first_row).astype(F32)
    ang = pos * inv2
    sin = jnp.sin(ang)
    lane = lax.broadcasted_iota(jnp.int32, (TM, HEAD_DIM), 1)
    return jnp.cos(ang), jnp.where(lane < HEAD_DIM // 2, -sin, sin)


def _decay_tables():
    log_gamma = jnp.log(1.0 - jnp.power(2.0, -5.0 - jnp.arange(HEADS, dtype=F32)))
    pos = jnp.arange(CHUNK, dtype=F32)
    diff = pos[:, None] - pos[None, :]
    mask = jnp.where(diff >= 0.0, jnp.exp(log_gamma[:, None, None] * jnp.maximum(diff, 0.0)[None]), 0.0)
    k_decay = jnp.exp(log_gamma[:, None] * (CHUNK - 1.0 - pos)[None])
    q_decay = jnp.exp(log_gamma[:, None] * (pos + 1.0)[None])
    kd = jnp.broadcast_to(k_decay[:, :, None], (HEADS, CHUNK, HEAD_DIM))
    qd = jnp.broadcast_to(q_decay[:, :, None], (HEADS, CHUNK, HEAD_DIM))
    return mask.astype(F32), qd.astype(F32), kd.astype(F32)


def _chunk_decay():
    lg = np.log(np.float32(1.0) - np.power(np.float32(2.0), -5.0 - np.arange(HEADS, dtype=np.float32))).astype(np.float32)
    return [float(np.exp(lg[h] * np.float32(CHUNK))) for h in range(HEADS)]


W_IN, W_OUT, W_UP, W_DOWN, W_CONV = range(5)
GATHERED = {W_IN: ((D_MODEL, PROJ_W), BF16), W_OUT: ((D_MODEL, D_MODEL), BF16), W_UP: ((2 * D_FF, D_MODEL), BF16),
            W_DOWN: ((D_FF, D_MODEL), BF16), W_CONV: ((N_DEV, 8, FF_SHARD), F32)}
SHARD = {W_IN: (D_MODEL, IN_SHARD), W_OUT: (OUT_SHARD, D_MODEL), W_UP: (FF_SHARD, D_MODEL), W_DOWN: (DOWN_SHARD, D_MODEL),
         W_CONV: (8, FF_SHARD)}


class _Gather:
    N_SEMS = 9

    def __init__(self, ids, stages, gathered, send_sems, recv_sems, local_sems):
        self.ids, self.stages, self.gathered = ids, stages, gathered
        self.send_sems, self.recv_sems, self.local_sems = send_sems, recv_sems, local_sems
        self.x, self.y, self.c = lax.axis_index("x"), lax.axis_index("y"), lax.axis_index("c")
        self.me = (self.x, self.y, self.c)
        self.sibling = (self.x, self.y, 1 - self.c)
        self.chips = [(1 - self.x, self.y), (self.x, 1 - self.y), (1 - self.x, 1 - self.y)]

    def slot(self, n, px, py, pc):
        dev = 4 * px + 2 * py + pc
        w, g = self.ids[n], self.gathered[n]
        if w == W_IN:
            return g.at[:, pl.ds(pl.multiple_of(dev * IN_SHARD, 128), IN_SHARD)]
        if w == W_OUT:
            return g.at[pl.ds(pl.multiple_of(dev * OUT_SHARD, 128), OUT_SHARD), :]
        if w == W_DOWN:
            return g.at[pl.ds(pl.multiple_of(dev * DOWN_SHARD, 32), DOWN_SHARD), :]
        if w == W_UP:
            return g.at[pl.ds(pl.multiple_of(dev * FF_SHARD, 32), FF_SHARD), :]
        return g.at[dev]

    def half(self, n, px, py, pc, h):
        dev = 4 * px + 2 * py + pc
        w, g = self.ids[n], self.gathered[n]
        if w == W_IN:
            return g.at[pl.ds(h * (D_MODEL // 2), D_MODEL // 2), pl.ds(pl.multiple_of(dev * IN_SHARD, 128), IN_SHARD)]
        rows = SHARD[w][0] // 2
        return g.at[pl.ds(pl.multiple_of(dev * SHARD[w][0] + h * rows, 16), rows), :]

    def tree(self, n):
        return self.ids[n] != W_CONV

    def copy(self, n, k, block, to, src=None, h=None):
        ref = self.slot(n, *block) if h is None else self.half(n, *block, h)
        return pltpu.make_async_remote_copy(
            src_ref=ref if src is None else src, dst_ref=ref,
            send_sem=self.send_sems.at[n, k], recv_sem=self.recv_sems.at[n, k], device_id=to, device_id_type=MESH)

    def _mine(self):
        return [pltpu.make_async_copy(self.stages[n], self.slot(n, *self.me), self.local_sems.at[n]) for n in range(len(self.ids))]

    def _first(self):
        out = []
        for n in range(len(self.ids)):
            out.append(self.copy(n, 0, self.me, self.sibling, src=self.stages[n]))
            out += [self.copy(n, 1 + j, self.me, (*chip, self.c), src=self.stages[n])
                    for j, chip in enumerate(self.chips[:2] if self.tree(n) else self.chips)]
        return out

    def start(self):
        for cp in self._mine() + self._first():
            cp.start()

    def _passed(self, j):
        dev = (*self.chips[j], self.c)
        out = []
        for n in range(len(self.ids)):
            if not self.tree(n):
                out.append(self.copy(n, 4 + j, dev, self.sibling))
            elif j < 2:
                out += [self.copy(n, 3 + j, dev, (*self.chips[1 - j], self.c), h=j), self.copy(n, 5 + j, dev, self.sibling)]
            else:
                out += [self.copy(n, 7, dev, self.sibling, h=0), self.copy(n, 8, dev, self.sibling, h=1)]
        return out

    def near(self):
        for j in range(2):
            dev = (*self.chips[j], self.c)
            for n in range(len(self.ids)):
                self.copy(n, 1 + j, dev, self.me).wait_recv()
            for cp in self._passed(j):
                cp.start()

    def finish(self):
        dev = (*self.chips[2], self.c)
        for n in range(len(self.ids)):
            if self.tree(n):
                self.copy(n, 3, dev, self.me, h=0).wait_recv()
                self.copy(n, 4, dev, self.me, h=1).wait_recv()
            else:
                self.copy(n, 3, dev, self.me).wait_recv()
        for cp in self._passed(2):
            cp.start()
        for n in range(len(self.ids)):
            self.copy(n, 0, self.sibling, self.me).wait_recv()
            for j, chip in enumerate(self.chips):
                dev = (*chip, 1 - self.c)
                if not self.tree(n):
                    self.copy(n, 4 + j, dev, self.me).wait_recv()
                elif j < 2:
                    self.copy(n, 5 + j, dev, self.me).wait_recv()
                else:
                    self.copy(n, 7, dev, self.me, h=0).wait_recv()
                    self.copy(n, 8, dev, self.me, h=1).wait_recv()
        for cp in self._mine():
            cp.wait()
        for cp in self._first() + self._passed(0) + self._passed(1) + self._passed(2):
            cp.wait_send()


def _gather_scratch(n):
    return [pltpu.SemaphoreType.DMA((n, _Gather.N_SEMS)), pltpu.SemaphoreType.DMA((n, _Gather.N_SEMS)), pltpu.SemaphoreType.DMA((n,))]


def _gathered_shapes(ids):
    return tuple(jax.ShapeDtypeStruct(*GATHERED[w]) for w in ids)


def _fwd_proj(x, g1, inv2, w_in, w_out, w_up, w_down, conv_w):
    ids_a, ids_b = [W_IN, W_CONV], [W_OUT, W_DOWN]

    def body(x_ref, g_ref, inv_ref, in_hbm, out_hbm, up_hbm, dn_hbm, cw_ref,
             proj_ref, h1_ref, cos_ref, sin_ref, gin, gcw, gout, gdn, su_ref,
             w_vm, s_in, s_cw, s_out, s_dn, f_in, f_out, f_up, f_dn, ld_sems,
             a_send, a_recv, a_local, b_send, b_recv, b_local):
        ag_a = _Gather(ids_a, [s_in, s_cw], [gin, gcw], a_send, a_recv, a_local)
        ag_b = _Gather(ids_b, [s_out, s_dn], [gout, gdn], b_send, b_recv, b_local)

        @pl.when(pl.program_id(0) == 0)
        def _():
            meet = _Meet(diagonal=True)
            meet.signal()
            loads = [pltpu.make_async_copy(src, dst, ld_sems.at[i])
                     for i, (src, dst) in enumerate(((in_hbm, f_in), (out_hbm, f_out), (dn_hbm, f_dn), (up_hbm, f_up)))]
            for cp in loads:
                cp.start()
            s_cw[...] = jnp.zeros_like(s_cw)
            for k in range(3):
                s_cw[k:k + 1, :] = cw_ref[k]
            loads[0].wait()
            s_in[...] = f_in[...].astype(BF16)
            meet.wait()
            ag_a.start()
            loads[1].wait()
            s_out[...] = f_out[...].astype(BF16)
            loads[2].wait()
            s_dn[...] = f_dn[...].astype(BF16)
            ag_a.near()
            ag_b.start()
            loads[3].wait()
            su_ref[...] = f_up[...].astype(BF16)
            ag_a.finish()
            fill = pltpu.make_async_copy(gin, w_vm, ld_sems.at[4])
            fill.start()
            fill.wait()

        pl.when(pl.program_id(0) == FWD_PROJ_PASS_AT)(ag_b.near)

        xb = x_ref[...]
        r = lax.rsqrt(jnp.mean(xb * xb, axis=-1, keepdims=True) + EPS)
        h = ((xb * r) * g_ref[...]).astype(BF16)
        h1_ref[...] = h
        p = _dot(h, w_vm[...])
        c2, s2 = _rope_block(inv_ref[...], pl.program_id(0) * TM)
        cos_ref[...], sin_ref[...] = c2, s2
        for hd in range(HEADS):
            sl = slice(hd * HEAD_DIM, (hd + 1) * HEAD_DIM)
            proj_ref[:, sl] = _rot(p[:, sl], c2, s2)
            ks = slice(RET_W + hd * HEAD_DIM, RET_W + (hd + 1) * HEAD_DIM)
            proj_ref[:, ks] = _rot(p[:, ks], c2, s2) * K_SCALE
        proj_ref[:, 2 * RET_W:] = p[:, 2 * RET_W:]

        pl.when(pl.program_id(0) == N_TB - 1)(ag_b.finish)

    tok = lambda w: pl.BlockSpec((TM, w), lambda i: (i, 0))
    hbm = pl.BlockSpec(memory_space=pl.ANY)
    vm = pl.BlockSpec(memory_space=pltpu.VMEM)
    return pl.pallas_call(
        body, name="fwd_proj", grid=(N_TB,),
        out_shape=(jax.ShapeDtypeStruct((SEQ, PROJ_W), F32), jax.ShapeDtypeStruct((SEQ, D_MODEL), BF16),
                   jax.ShapeDtypeStruct((SEQ, HEAD_DIM), F32), jax.ShapeDtypeStruct((SEQ, HEAD_DIM), F32))
        + _gathered_shapes(ids_a + ids_b) + (jax.ShapeDtypeStruct(SHARD[W_UP], BF16),),
        in_specs=[tok(D_MODEL), _resident((1, D_MODEL)), _resident((1, HEAD_DIM)), hbm, hbm, hbm, hbm, vm],
        out_specs=(tok(PROJ_W), tok(D_MODEL), tok(HEAD_DIM), tok(HEAD_DIM), hbm, hbm, hbm, hbm, vm),
        scratch_shapes=[pltpu.VMEM((D_MODEL, PROJ_W), BF16), pltpu.VMEM(SHARD[W_IN], BF16), pltpu.VMEM(SHARD[W_CONV], F32),
                        pltpu.VMEM(SHARD[W_OUT], BF16), pltpu.VMEM(SHARD[W_DOWN], BF16),
                        pltpu.VMEM(SHARD[W_IN], F32), pltpu.VMEM(SHARD[W_OUT], F32), pltpu.VMEM(SHARD[W_UP], F32),
                        pltpu.VMEM(SHARD[W_DOWN], F32), pltpu.SemaphoreType.DMA((5,))]
        + _gather_scratch(len(ids_a)) + _gather_scratch(len(ids_b)),
        compiler_params=_cparams(("arbitrary",), collective=COLLECTIVE["fwd_proj"]),
    )(x, g1, inv2, w_in, w_out, w_up, w_down, conv_w)


def _causal(w):
    r = lax.broadcasted_iota(jnp.int32, (CHUNK, CHUNK), 0)
    c = lax.broadcasted_iota(jnp.int32, (CHUNK, CHUNK), 1)
    return jnp.where(r >= c, w, 0.0)


def _fwd_mix(x, proj, wout_g, grn, lng, lnb, ws, bsb, mask, qdec, kdec, su):
    cdec = _chunk_decay()
    ids = [W_UP]

    def body(x_ref, p_ref, w_ref, grn_ref, lng_ref, lnb_ref, ws_ref, bsb_ref, m_ref, qd_ref, kd_ref, su_ref,
             x2_ref, cat_ref, o_ref, sp_ref, gup, state, send_sems, recv_sems, local_sems):
        ag = _Gather(ids, [su_ref], [gup], send_sems, recv_sems, local_sems)

        @pl.when(pl.program_id(0) == 0)
        def _():
            meet = _Meet(diagonal=False)
            meet.signal()
            state[...] = jnp.zeros_like(state)
            meet.wait()
            ag.start()

        for h in range(HEADS):
            sl = slice(h * HEAD_DIM, (h + 1) * HEAD_DIM)
            q = p_ref[:, sl]
            k = p_ref[:, RET_W + h * HEAD_DIM:RET_W + (h + 1) * HEAD_DIM]
            v = p_ref[:, 2 * RET_W + h * HEAD_DIM:2 * RET_W + (h + 1) * HEAD_DIM]
            g = p_ref[:, 3 * RET_W + h * HEAD_DIM:3 * RET_W + (h + 1) * HEAD_DIM]
            qb, kb, vb = q.astype(BF16), k.astype(BF16), v.astype(BF16)
            a = _dot_nt(qb, kb) * m_ref[h]
            spb = state[h].astype(BF16)
            sp_ref[0, h] = spb
            o = _dot(a.astype(BF16), vb) + _dot((q * qd_ref[h]).astype(BF16), spb)
            state[h] = state[h] * cdec[h] + _dot_tn((k * kd_ref[h]).astype(BF16), vb)
            o_ref[:, sl] = o
            rinv = lax.rsqrt(jnp.mean(o * o, axis=-1, keepdims=True) + EPS)
            rn = (o * rinv) * grn_ref[:, sl]
            cat_ref[:, sl] = ((g * _sigmoid(g)) * rn).astype(BF16)
        for gi in range(HEADS):
            sl = slice(gi * HEAD_DIM, (gi + 1) * HEAD_DIM)
            u = p_ref[:, 4 * RET_W + gi * HEAD_DIM:4 * RET_W + (gi + 1) * HEAD_DIM]
            sv = p_ref[:, 4 * RET_W + SGU_W + gi * HEAD_DIM:4 * RET_W + SGU_W + (gi + 1) * HEAD_DIM]
            gv = _gelu(sv)
            xc = gv - jnp.mean(gv, axis=-1, keepdims=True)
            vn = (xc * lax.rsqrt(jnp.mean(xc * xc, axis=-1, keepdims=True) + EPS)) * lng_ref[:, sl] + lnb_ref[:, sl]
            mixed = _dot(_causal(ws_ref[gi]).astype(BF16), vn.astype(BF16)) + bsb_ref[gi]
            cat_ref[:, RET_W + gi * HEAD_DIM:RET_W + (gi + 1) * HEAD_DIM] = (_gelu(u) * mixed).astype(BF16)
        x2_ref[...] = x_ref[...] + _dot(cat_ref[...], w_ref[...])

        pl.when(pl.program_id(0) == FWD_MIX_PASS_AT)(ag.near)
        pl.when(pl.program_id(0) == N_CHUNK - 1)(ag.finish)

    ch = lambda w: pl.BlockSpec((CHUNK, w), lambda i: (i, 0))
    hcc = (HEADS, CHUNK, CHUNK)
    hbm = pl.BlockSpec(memory_space=pl.ANY)
    return pl.pallas_call(
        body, name="fwd_mix", grid=(N_CHUNK,),
        out_shape=(jax.ShapeDtypeStruct((SEQ, D_MODEL), F32), jax.ShapeDtypeStruct((SEQ, D_MODEL), BF16),
                   jax.ShapeDtypeStruct((SEQ, RET_W), F32), jax.ShapeDtypeStruct((N_CHUNK, HEADS, HEAD_DIM, HEAD_DIM), BF16))
        + _gathered_shapes(ids),
        in_specs=[ch(D_MODEL), ch(PROJ_W), _resident((D_MODEL, D_MODEL)), _resident((1, RET_W)), _resident((1, SGU_W)),
                  _resident((1, SGU_W)), _resident(hcc), _resident(hcc), _resident(hcc), _resident(hcc), _resident(hcc), hbm],
        out_specs=(ch(D_MODEL), ch(D_MODEL), ch(RET_W), pl.BlockSpec((1, HEADS, HEAD_DIM, HEAD_DIM), lambda i: (i, 0, 0, 0)), hbm),
        scratch_shapes=[pltpu.VMEM((HEADS, HEAD_DIM, HEAD_DIM), F32)] + _gather_scratch(len(ids)),
        compiler_params=_cparams(("arbitrary",), collective=COLLECTIVE["fwd_mix"]),
    )(x, proj, wout_g, grn, lng, lnb, ws, bsb, mask, qdec, kdec, su)


def _conv_taps(p, prev8):
    row = lax.broadcasted_iota(jnp.int32, p.shape, 0)
    p1 = jnp.where(row == 0, prev8[7:8, :], pltpu.roll(p, 1, 0))
    p2 = jnp.where(row == 0, prev8[6:7, :], jnp.where(row == 1, prev8[7:8, :], pltpu.roll(p, 2, 0)))
    return p1, p2


def _fwd_ffn(x2, g2, wup_g, cw_g, cb_g, wdn_g, gf, tgt):
    def body(x_ref, g_ref, wu_ref, cw_ref, cb_ref, wd_ref, gf_ref, t_ref, h2_ref, up_ref, u_ref, act_ref, x3_ref, loss_ref, carry):
        @pl.when(pl.program_id(0) == 0)
        def _():
            carry[...] = jnp.zeros_like(carry)

        xb = x_ref[...]
        r = lax.rsqrt(jnp.mean(xb * xb, axis=-1, keepdims=True) + EPS)
        h = ((xb * r) * g_ref[...]).astype(BF16)
        h2_ref[...] = h
        acc = xb
        for t0, tw in FF_TILES:
            u = []
            for c0 in (t0, D_FF + t0):
                cs = slice(c0, c0 + tw)
                p = _dot_nt(h, wu_ref[pl.ds(c0, tw), :])
                up_ref[:, cs] = p.astype(BF16)
                p1, p2 = _conv_taps(p, carry[:, cs])
                carry[:, cs] = p[TM - 8:, :]
                us = p2 * cw_ref[0:1, cs] + p1 * cw_ref[1:2, cs] + p * cw_ref[2:3, cs] + cb_ref[:, cs]
                u_ref[:, cs] = us.astype(BF16)
                u.append(us)
            a = ((u[0] * _sigmoid(u[0])) * u[1]).astype(BF16)
            act_ref[:, t0:t0 + tw] = a
            acc = acc + _dot(a, wd_ref[pl.ds(t0, tw), :])
        x3_ref[...] = acc
        r3 = lax.rsqrt(jnp.mean(acc * acc, axis=-1, keepdims=True) + EPS)
        diff = (acc * r3) * gf_ref[...] - t_ref[...]
        loss_ref[...] = jnp.full(loss_ref.shape, 0.5 * jnp.sum(jnp.mean(diff * diff, axis=-1)), F32)

    tok = lambda w: pl.BlockSpec((TM, w), lambda i: (i, 0))
    return pl.pallas_call(
        body, name="fwd_ffn", grid=(N_TB,),
        out_shape=(jax.ShapeDtypeStruct((SEQ, D_MODEL), BF16), jax.ShapeDtypeStruct((SEQ, 2 * D_FF), BF16),
                   jax.ShapeDtypeStruct((SEQ, 2 * D_FF), BF16),
                   jax.ShapeDtypeStruct((SEQ, D_FF), BF16), jax.ShapeDtypeStruct((SEQ, D_MODEL), F32),
                   jax.ShapeDtypeStruct((N_TB, 8, 128), F32)),
        in_specs=[tok(D_MODEL), _resident((1, D_MODEL)), _resident((2 * D_FF, D_MODEL)), _resident((8, 2 * D_FF)),
                  _resident((1, 2 * D_FF)), _resident((D_FF, D_MODEL)), _resident((1, D_MODEL)), tok(D_MODEL)],
        out_specs=(tok(D_MODEL), tok(2 * D_FF), tok(2 * D_FF), tok(D_FF), tok(D_MODEL),
                   pl.BlockSpec((1, 8, 128), lambda i: (i, 0, 0))),
        scratch_shapes=[pltpu.VMEM((8, 2 * D_FF), F32)],
        compiler_params=_cparams(("arbitrary",)),
    )(x2, g2, wup_g, cw_g, cb_g, wdn_g, gf, tgt)


def _bwd_ffn(x3, tgt, gf, x2, g2, up_pre, u_conv, wup_g, cw_g, wdn_g):
    def body(x3_ref, t_ref, gf_ref, x2_ref, g2_ref, up_ref, u_ref, wu_ref, cw_ref, wd_ref,
             dx3_ref, dpre_ref, dx2_ref, dgf_ref, dg2_ref, dcv_ref, nxt):
        i = pl.program_id(0)

        @pl.when(i == 0)
        def _():
            nxt[...] = jnp.zeros_like(nxt)
            dgf_ref[...] = jnp.zeros_like(dgf_ref)
            dg2_ref[...] = jnp.zeros_like(dg2_ref)
            dcv_ref[...] = jnp.zeros_like(dcv_ref)

        x3 = x3_ref[...]
        r3 = lax.rsqrt(jnp.mean(x3 * x3, axis=-1, keepdims=True) + EPS)
        xh3 = x3 * r3
        dy = (xh3 * gf_ref[...] - t_ref[...]) * (1.0 / D_MODEL)
        dgf_ref[0:1, :] += jnp.sum(dy * xh3, axis=0, keepdims=True)
        t3 = dy * gf_ref[...]
        dx3 = r3 * (t3 - xh3 * jnp.mean(t3 * xh3, axis=-1, keepdims=True))
        dx3b = dx3.astype(BF16)
        dx3_ref[...] = dx3b
        dh2 = jnp.zeros((TM, D_MODEL), F32)
        for t0, tw in FF_TILES:
            row = lax.broadcasted_iota(jnp.int32, (TM, tw), 0)
            ts = slice(t0, t0 + tw)
            dact = _dot_nt(dx3b, wd_ref[pl.ds(t0, tw), :])
            ua = u_ref[:, ts].astype(F32)
            ub = u_ref[:, D_FF + t0:D_FF + t0 + tw].astype(F32)
            sg = _sigmoid(ua)
            du = [dact * ub * (sg * (1.0 + ua * (1.0 - sg))), dact * (ua * sg)]
            for n in range(2):
                d = du[n]
                c0 = n * D_FF + t0
                cs = slice(c0, c0 + tw)
                nx = nxt[:, cs]
                n1 = jnp.where(row == TM - 1, nx[0:1, :], pltpu.roll(d, TM - 1, 0))
                n2 = jnp.where(row == TM - 2, nx[0:1, :], jnp.where(row == TM - 1, nx[1:2, :], pltpu.roll(d, TM - 2, 0)))
                nxt[:, cs] = d[0:8, :]
                dp = (d * cw_ref[2:3, cs] + n1 * cw_ref[1:2, cs] + n2 * cw_ref[0:1, cs]).astype(BF16)
                dpre_ref[:, cs] = dp
                p = up_ref[:, cs].astype(F32)
                dcv_ref[n, 0:1, ts] += jnp.sum(n2 * p, axis=0, keepdims=True)
                dcv_ref[n, 1:2, ts] += jnp.sum(n1 * p, axis=0, keepdims=True)
                dcv_ref[n, 2:3, ts] += jnp.sum(d * p, axis=0, keepdims=True)
                dcv_ref[n, 3:4, ts] += jnp.sum(d, axis=0, keepdims=True)
                dh2 = dh2 + _dot(dp, wu_ref[pl.ds(c0, tw), :])
        x2 = x2_ref[...]
        r2 = lax.rsqrt(jnp.mean(x2 * x2, axis=-1, keepdims=True) + EPS)
        xh2 = x2 * r2
        dg2_ref[0:1, :] += jnp.sum(dh2 * xh2, axis=0, keepdims=True)
        t2 = dh2 * g2_ref[...]
        dx2_ref[...] = dx3 + r2 * (t2 - xh2 * jnp.mean(t2 * xh2, axis=-1, keepdims=True))

    rev = lambda w: pl.BlockSpec((TM, w), lambda i: (N_TB - 1 - i, 0))
    acc = lambda s: pl.BlockSpec(s, lambda i: (0,) * len(s))
    return pl.pallas_call(
        body, name="bwd_ffn", grid=(N_TB,),
        out_shape=(jax.ShapeDtypeStruct((SEQ, D_MODEL), BF16), jax.ShapeDtypeStruct((SEQ, 2 * D_FF), BF16),
                   jax.ShapeDtypeStruct((SEQ, D_MODEL), F32), jax.ShapeDtypeStruct((8, D_MODEL), F32),
                   jax.ShapeDtypeStruct((8, D_MODEL), F32), jax.ShapeDtypeStruct((2, 8, D_FF), F32)),
        in_specs=[rev(D_MODEL), rev(D_MODEL), _resident((1, D_MODEL)), rev(D_MODEL), _resident((1, D_MODEL)), rev(2 * D_FF),
                  rev(2 * D_FF), _resident((2 * D_FF, D_MODEL)), _resident((8, 2 * D_FF)), _resident((D_FF, D_MODEL))],
        out_specs=(rev(D_MODEL), rev(2 * D_FF), rev(D_MODEL), acc((8, D_MODEL)), acc((8, D_MODEL)), acc((2, 8, D_FF))),
        scratch_shapes=[pltpu.VMEM((8, 2 * D_FF), F32)],
        compiler_params=_cparams(("arbitrary",)),
    )(x3, tgt, gf, x2, g2, up_pre, u_conv, wup_g, cw_g, wdn_g)


def _bwd_mix(dx2, proj, o, sprev, wout_g, grn, lng, lnb, ws, bsb, mask, qdec, kdec, cos2, sin2, hosted):
    cdec = _chunk_decay()
    geoms = [g for g, _ in hosted]
    n_h = len(hosted)

    def body(dx2_ref, p_ref, o_ref, sp_ref, w_ref, grn_ref, lng_ref, lnb_ref, ws_ref, bsb_ref, m_ref, qd_ref, kd_ref,
             cos_ref, sin_ref, *rest):
        dp_ref, dgrn_ref, dlng_ref, dlnb_ref, dws_ref, dbs_ref = rest[n_h:n_h + 6]
        dstate, dbs_acc = rest[2 * n_h + 6:2 * n_h + 8]
        i = pl.program_id(0)
        rs = _Scatters(geoms, rest[:n_h], rest[n_h + 6:2 * n_h + 6], rest[2 * n_h + 8:])
        pl.when(i == 0)(rs.phase1)
        pl.when(i == 3)(rs.phase2)
        pl.when(i == 8)(rs.phase2b)

        @pl.when(i == 0)
        def _():
            dstate[...] = jnp.zeros_like(dstate)
            dgrn_ref[...] = jnp.zeros_like(dgrn_ref)
            dlng_ref[...] = jnp.zeros_like(dlng_ref)
            dlnb_ref[...] = jnp.zeros_like(dlnb_ref)
            dws_ref[...] = jnp.zeros_like(dws_ref)
            dbs_ref[...] = jnp.zeros_like(dbs_ref)
            dbs_acc[...] = jnp.zeros_like(dbs_acc)

        dmix = _dot_nt(dx2_ref[...].astype(BF16), w_ref[...])
        for h in range(HEADS):
            sl = slice(h * HEAD_DIM, (h + 1) * HEAD_DIM)
            q = p_ref[:, sl]
            k = p_ref[:, RET_W + h * HEAD_DIM:RET_W + (h + 1) * HEAD_DIM]
            v = p_ref[:, 2 * RET_W + h * HEAD_DIM:2 * RET_W + (h + 1) * HEAD_DIM]
            g = p_ref[:, 3 * RET_W + h * HEAD_DIM:3 * RET_W + (h + 1) * HEAD_DIM]
            o = o_ref[:, sl]
            rinv = lax.rsqrt(jnp.mean(o * o, axis=-1, keepdims=True) + EPS)
            oh = o * rinv
            gr = grn_ref[:, sl]
            sg = _sigmoid(g)
            dret = dmix[:, sl]
            dp_ref[:, 3 * RET_W + h * HEAD_DIM:3 * RET_W + (h + 1) * HEAD_DIM] = (
                dret * (oh * gr) * (sg * (1.0 + g * (1.0 - sg)))).astype(BF16)
            drn = dret * (g * sg)
            dgrn_ref[0:1, sl] += jnp.sum(drn * oh, axis=0, keepdims=True)
            t = drn * gr
            do = rinv * (t - oh * jnp.mean(t * oh, axis=-1, keepdims=True))
            qb, kb, vb, dob = q.astype(BF16), k.astype(BF16), v.astype(BF16), do.astype(BF16)
            m = m_ref[h]
            ab = (_dot_nt(qb, kb) * m).astype(BF16)
            dab = (_dot_nt(dob, vb) * m).astype(BF16)
            spb = sp_ref[0, h]
            dsn = dstate[h]
            dsnb = dsn.astype(BF16)
            qdb = (q * qd_ref[h]).astype(BF16)
            kdb = (k * kd_ref[h]).astype(BF16)
            dq = _dot(dab, kb) + _dot_nt(dob, spb) * qd_ref[h]
            dk = _dot_tn(dab, qb) + _dot_nt(vb, dsnb) * kd_ref[h]
            dv = _dot_tn(ab, dob) + _dot(kdb, dsnb)
            dstate[h] = dsn * cdec[h] + _dot_tn(qdb, dob)
            c2, s2 = cos_ref[...], sin_ref[...]
            dp_ref[:, sl] = _rot_t(dq, c2, s2).astype(BF16)
            dp_ref[:, RET_W + h * HEAD_DIM:RET_W + (h + 1) * HEAD_DIM] = _rot_t(dk * K_SCALE, c2, s2).astype(BF16)
            dp_ref[:, 2 * RET_W + h * HEAD_DIM:2 * RET_W + (h + 1) * HEAD_DIM] = dv.astype(BF16)
        for gi in range(HEADS):
            sl = slice(gi * HEAD_DIM, (gi + 1) * HEAD_DIM)
            u = p_ref[:, 4 * RET_W + gi * HEAD_DIM:4 * RET_W + (gi + 1) * HEAD_DIM]
            sv = p_ref[:, 4 * RET_W + SGU_W + gi * HEAD_DIM:4 * RET_W + SGU_W + (gi + 1) * HEAD_DIM]
            gv = _gelu(sv)
            xc = gv - jnp.mean(gv, axis=-1, keepdims=True)
            rstd = lax.rsqrt(jnp.mean(xc * xc, axis=-1, keepdims=True) + EPS)
            xh = xc * rstd
            lg = lng_ref[:, sl]
            vnb = (xh * lg + lnb_ref[:, sl]).astype(BF16)
            wcb = _causal(ws_ref[gi]).astype(BF16)
            mixed = _dot(wcb, vnb) + bsb_ref[gi]
            dsgu = dmix[:, RET_W + gi * HEAD_DIM:RET_W + (gi + 1) * HEAD_DIM]
            dmixed = dsgu * _gelu(u)
            dmb = dmixed.astype(BF16)
            dws_ref[gi] += _causal(_dot_nt(dmb, vnb))
            dbs_acc[gi] += dmixed
            dvn = _dot_tn(wcb, dmb)
            dlng_ref[gi:gi + 1, :] += jnp.sum(dvn * xh, axis=0, keepdims=True)
            dlnb_ref[gi:gi + 1, :] += jnp.sum(dvn, axis=0, keepdims=True)
            dxh = dvn * lg
            dgv = rstd * (dxh - jnp.mean(dxh, axis=-1, keepdims=True) - xh * jnp.mean(dxh * xh, axis=-1, keepdims=True))
            dp_ref[:, 4 * RET_W + gi * HEAD_DIM:4 * RET_W + (gi + 1) * HEAD_DIM] = (dsgu * mixed * _gelu_grad(u)).astype(BF16)
            dp_ref[:, 4 * RET_W + SGU_W + gi * HEAD_DIM:4 * RET_W + SGU_W + (gi + 1) * HEAD_DIM] = (
                dgv * _gelu_grad(sv)).astype(BF16)

        @pl.when(i == N_CHUNK - 1)
        def _():
            for gi in range(HEADS):
                col = jnp.broadcast_to(jnp.sum(dbs_acc[gi], axis=-1, keepdims=True), (CHUNK, CHUNK))
                dbs_ref[gi:gi + 1, :] = jnp.transpose(col)[0:1, :]
            rs.phase3()

    rev = lambda w: pl.BlockSpec((CHUNK, w), lambda i: (N_CHUNK - 1 - i, 0))
    hcc = (HEADS, CHUNK, CHUNK)
    acc = lambda s: pl.BlockSpec(s, lambda i: (0,) * len(s))
    res = pl.pallas_call(
        body, name="bwd_mix", grid=(N_CHUNK,),
        out_shape=(jax.ShapeDtypeStruct((SEQ, PROJ_W), BF16), jax.ShapeDtypeStruct((8, RET_W), F32),
                   jax.ShapeDtypeStruct((8, HEAD_DIM), F32), jax.ShapeDtypeStruct((8, HEAD_DIM), F32),
                   jax.ShapeDtypeStruct(hcc, F32), jax.ShapeDtypeStruct((8, CHUNK), F32)) + _scatter_out_shapes(geoms),
        in_specs=[rev(D_MODEL), rev(PROJ_W), rev(RET_W),
                  pl.BlockSpec((1, HEADS, HEAD_DIM, HEAD_DIM), lambda i: (N_CHUNK - 1 - i, 0, 0, 0)),
                  _resident((D_MODEL, D_MODEL)), _resident((1, RET_W)), _resident((1, SGU_W)), _resident((1, SGU_W)),
                  _resident(hcc), _resident(hcc), _resident(hcc), _resident(hcc), _resident(hcc), rev(HEAD_DIM), rev(HEAD_DIM)]
        + [pl.BlockSpec(memory_space=pl.ANY)] * n_h,
        out_specs=(rev(PROJ_W), acc((8, RET_W)), acc((8, HEAD_DIM)), acc((8, HEAD_DIM)), acc(hcc), acc((8, CHUNK)))
        + _scatter_out_specs(geoms),
        scratch_shapes=[pltpu.VMEM((HEADS, HEAD_DIM, HEAD_DIM), F32), pltpu.VMEM((HEADS, CHUNK, CHUNK), F32)] + _scatter_scratch(geoms),
        compiler_params=_cparams(("arbitrary",), collective=COLLECTIVE["bwd_mix"]),
    )(dx2, proj, o, sprev, wout_g, grn, lng, lnb, ws, bsb, mask, qdec, kdec, cos2, sin2, *[p for _, p in hosted])
    return tuple(res[:6 + n_h])


def _bwd_proj(dproj, win_g, x, g1, dx2, gin_p, small):
    geoms = [W_IN]
    n_s = len(small)

    def body(dp_ref, w_ref, x_ref, g_ref, dx2_ref, gin_ref, *rest):
        small_refs = rest[:n_s]
        dx_ref, rs_out, rp_ref, rws_ref, rcv_ref, dg_ref = rest[n_s:n_s + 6]
        rs_scratch = rest[n_s + 6:n_s + 6 + N_SCATTER_SCRATCH]
        ar_scratch = rest[n_s + 6 + N_SCATTER_SCRATCH:]
        ar_res = ar_scratch[N_SMALL_SCRATCH:]
        ar = _SmallReduce((dg_ref,) + tuple(small_refs), ar_res, ar_scratch[:N_SMALL_SCRATCH])
        rs = _Scatters(geoms, [gin_ref], [rs_out], rs_scratch)
        pl.when(pl.program_id(0) == 0)(lambda: rs.phase1(diagonal=True))
        pl.when(pl.program_id(0) == 1)(rs.phase2)
        pl.when(pl.program_id(0) == 5)(rs.phase2b)

        @pl.when(pl.program_id(0) == 0)
        def _():
            dg_ref[...] = jnp.zeros_like(dg_ref)

        dh = _dot_nt(dp_ref[...], w_ref[...])
        xb = x_ref[...]
        r = lax.rsqrt(jnp.mean(xb * xb, axis=-1, keepdims=True) + EPS)
        xh = xb * r
        dg_ref[0:1, :] += jnp.sum(dh * xh, axis=0, keepdims=True)
        t = dh * g_ref[...]
        dx_ref[...] = dx2_ref[...] + r * (t - xh * jnp.mean(t * xh, axis=-1, keepdims=True))

        @pl.when(pl.program_id(0) == N_TB - 1)
        def _():
            ar.begin()
            rs.phase3()
            ar.end()
            for o_ref, r_ref in zip((rp_ref, rws_ref, rcv_ref), ar_res):
                o_ref[...] = r_ref[...]

    tok = lambda w: pl.BlockSpec((TM, w), lambda i: (i, 0))
    vm = pl.BlockSpec(memory_space=pltpu.VMEM)
    res = pl.pallas_call(
        body, name="bwd_proj", grid=(N_TB,),
        out_shape=(jax.ShapeDtypeStruct((SEQ, D_MODEL), F32),) + _scatter_out_shapes(geoms)
        + tuple(jax.ShapeDtypeStruct(s, F32) for s in SMALL_FULL),
        in_specs=[tok(PROJ_W), _resident((D_MODEL, PROJ_W)), tok(D_MODEL), _resident((1, D_MODEL)), tok(D_MODEL),
                  pl.BlockSpec(memory_space=pl.ANY)] + [vm] * n_s,
        out_specs=(tok(D_MODEL),) + _scatter_out_specs(geoms) + (vm,) * len(SMALL_FULL),
        scratch_shapes=[pltpu.VMEM((8, D_MODEL), F32)] + _scatter_scratch(geoms) + _small_scratch()
        + [pltpu.VMEM(s, F32) for s in SMALL_FULL],
        compiler_params=_cparams(("arbitrary",), collective=COLLECTIVE["bwd_proj"]),
    )(dproj, win_g, x, g1, dx2, gin_p, *small)
    return res


def _wgrad(name, a, b, tm=None, tn=None, hosted=()):
    m_w, n_w = a.shape[-1], b.shape[-1]
    tm = m_w if tm is None else tm
    tn = n_w if tn is None else tn
    n_steps = (m_w // tm) * (n_w // tn)
    geoms = [g for g, _ in hosted]
    n_h = len(hosted)

    def body(a_ref, b_ref, *rest):
        o_ref = rest[n_h]
        if n_h:
            rs = _Scatters(geoms, rest[:n_h], rest[n_h + 1:2 * n_h + 1], rest[2 * n_h + 1:])
            step = pl.program_id(0) * (n_w // tn) + pl.program_id(1)
            pl.when(step == 0)(rs.phase1)
            pl.when(step == 1)(rs.phase2)
            pl.when(step == n_steps // 2)(rs.phase2b)
        o_ref[...] = _dot_tn(a_ref[...].astype(BF16), b_ref[...].astype(BF16)).astype(BF16)
        if n_h:
            pl.when(step == n_steps - 1)(rs.phase3)

    assert not n_h or n_steps >= 4
    res = pl.pallas_call(
        body, name=name, grid=(m_w // tm, n_w // tn),
        out_shape=(jax.ShapeDtypeStruct((m_w, n_w), BF16),) + _scatter_out_shapes(geoms),
        in_specs=[pl.BlockSpec((SEQ, tm), lambda i, j: (0, i)), pl.BlockSpec((SEQ, tn), lambda i, j: (0, j))]
        + [pl.BlockSpec(memory_space=pl.ANY)] * n_h,
        out_specs=(pl.BlockSpec((tm, tn), lambda i, j: (i, j)),) + _scatter_out_specs(geoms),
        scratch_shapes=_scatter_scratch(geoms),
        compiler_params=_cparams(("arbitrary", "arbitrary"), collective=COLLECTIVE[name]) if n_h else _cparams(("parallel", "parallel")),
    )(a, b, *[p for _, p in hosted])
    return tuple(res[:1 + n_h])


def _row_step(half_rows):
    return max(s for s in range(16, 177, 16) if half_rows % s == 0)


class _Scatter:
    def __init__(self, geom, partial, out, land1, mine, stage2, land2, comb, s1_send, s1_recv, s2_send, s2_recv, ld_sems):
        self.w, self.row0, self.shape = _geom(geom)
        self.partial, self.out, self.land1 = partial, out, land1
        self.mine, self.stage2, self.land2, self.comb = mine, stage2, land2, comb
        self.hr = self.shape[0] // 2
        self.step = _row_step(self.hr)
        self.s1_send, self.s1_recv, self.s2_send, self.s2_recv, self.ld_sems = s1_send, s1_recv, s2_send, s2_recv, ld_sems
        self.x, self.y, self.c = lax.axis_index("x"), lax.axis_index("y"), lax.axis_index("c")
        self.sibling = (self.x, self.y, 1 - self.c)
        self.chips = [(self.x, self.y), (1 - self.x, self.y), (self.x, 1 - self.y), (1 - self.x, 1 - self.y)]

    def block(self, px, py, pc):
        dev = 4 * px + 2 * py + pc
        if self.w == W_IN:
            return self.partial.at[:, pl.ds(pl.multiple_of(dev * IN_SHARD, 128), IN_SHARD)]
        if self.w == W_OUT:
            return self.partial.at[pl.ds(pl.multiple_of(dev * OUT_SHARD, 128), OUT_SHARD), :]
        if self.w == W_DOWN:
            return self.partial.at[pl.ds(pl.multiple_of(dev * DOWN_SHARD, 32), DOWN_SHARD), :]
        return self.partial.at[pl.ds(pl.multiple_of(dev * FF_SHARD + self.row0, 32), self.shape[0]), :]

    def copy1(self, k):
        return pltpu.make_async_remote_copy(
            src_ref=self.block(*self.chips[k], 1 - self.c), dst_ref=self.land1.at[k],
            send_sem=self.s1_send.at[k], recv_sem=self.s1_recv.at[k], device_id=self.sibling, device_id_type=MESH)

    STAGE2 = [(1, 0, 1), (3, 0, 1), (2, 1, 2), (3, 1, 2), (1, 1, 1), (2, 0, 2)]

    def copy2(self, j):
        blk, h, to = self.STAGE2[j]
        src = self.comb.at[j - 4] if j >= 4 else self.stage2.at[blk - 1, pl.ds(h * self.hr, self.hr), :]
        return pltpu.make_async_remote_copy(
            src_ref=src, dst_ref=self.land2.at[j], send_sem=self.s2_send.at[j], recv_sem=self.s2_recv.at[j],
            device_id=(*self.chips[to], self.c), device_id_type=MESH)

    def _rows(self, h=None):
        step = self.step
        lo, n = (0, self.shape[0]) if h is None else (h * self.hr, self.hr)
        return [pl.ds(r0, step) for r0 in range(lo, lo + n, step)]

    def load(self, k):
        return pltpu.make_async_copy(self.block(*self.chips[k], self.c), self.mine.at[k], self.ld_sems.at[k])

    def load_mine(self):
        for k in range(4):
            self.load(k).start()

    def phase1(self):
        for k in range(4):
            self.copy1(k).start()

    def phase2(self, k):
        self.copy1(k).wait_recv()
        self.load(k).wait()
        for rs in self._rows():
            s = self.mine[k, rs, :].astype(F32) + self.land1[k, rs, :].astype(F32)
            if k == 0:
                self.out[rs, :] = s
            else:
                self.stage2[k - 1, rs, :] = s.astype(BF16)
        for j in {3: (1, 3), 1: (0,), 2: (2,), 0: ()}[k]:
            self.copy2(j).start()

    def phase2b(self):
        for j, got in ((4, 3), (5, 1)):
            blk, h, _ = self.STAGE2[j]
            self.copy2(got).wait_recv()
            for i, rs in enumerate(self._rows(h)):
                lr = pl.ds(i * self.step, self.step)
                self.comb[j - 4, lr, :] = (self.stage2[blk - 1, rs, :].astype(F32) + self.land2[got, lr, :].astype(F32)).astype(BF16)
            self.copy2(j).start()

    def phase3(self):
        for j in (0, 5, 4, 2):
            self.copy2(j).wait_recv()
        for h, (first, second) in enumerate(((0, 5), (4, 2))):
            for i, rs in enumerate(self._rows(h)):
                lr = pl.ds(i * self.step, self.step)
                self.out[rs, :] = (self.out[rs, :] + self.land2[first, lr, :].astype(F32)) + self.land2[second, lr, :].astype(F32)
        for k in range(4):
            self.copy1(k).wait_send()
        for j in range(6):
            self.copy2(j).wait_send()


def _geom(geom):
    if isinstance(geom, tuple):
        w, row0, rows = geom
        assert w == W_UP
        return w, row0, (rows, SHARD[w][1])
    return geom, 0, SHARD[geom]


N_SCATTER_SCRATCH = 10


def _scatter_out_shapes(geoms):
    return tuple(jax.ShapeDtypeStruct(_geom(g)[2], F32) for g in geoms)


def _scatter_out_specs(geoms):
    return (pl.BlockSpec(memory_space=pltpu.VMEM),) * len(geoms)


def _scatter_scratch(geoms):
    out = []
    for g in geoms:
        s = _geom(g)[2]
        hs = (s[0] // 2, s[1])
        out += [pltpu.VMEM((4,) + s, BF16), pltpu.VMEM((4,) + s, BF16), pltpu.VMEM((3,) + s, BF16), pltpu.VMEM((6,) + hs, BF16),
                pltpu.VMEM((2,) + hs, BF16),
                pltpu.SemaphoreType.DMA((4,)), pltpu.SemaphoreType.DMA((4,)), pltpu.SemaphoreType.DMA((6,)),
                pltpu.SemaphoreType.DMA((6,)), pltpu.SemaphoreType.DMA((4,))]
    return out


class _Scatters:
    def __init__(self, geoms, p_refs, out_refs, scratch):
        k = N_SCATTER_SCRATCH
        self.items = [_Scatter(g, p_refs[i], out_refs[i], *scratch[k * i:k * i + k]) for i, g in enumerate(geoms)]

    def phase1(self, diagonal=False):
        meet = _Meet(diagonal)
        meet.signal()
        for s in self.items:
            s.load_mine()
        meet.wait()
        for s in self.items:
            s.phase1()

    def phase2(self):
        for k in (3, 1, 2, 0):
            for s in self.items:
                s.phase2(k)

    def phase2b(self):
        for s in self.items:
            s.phase2b()

    def phase3(self):
        for s in self.items:
            s.phase3()


PACK_W = 1024


SMALL_FULL = [(2, 8, PACK_W), (HEADS, CHUNK, CHUNK), (2, 8, D_FF)]
SMALL_HALF = [(s[0] // 2,) + s[1:] for s in SMALL_FULL]
N_SMALL_SCRATCH = 16


def _small_scratch():
    n_a = len(SMALL_FULL)
    return ([pltpu.VMEM(SMALL_FULL[0], F32)] + [pltpu.VMEM(s, F32) for s in SMALL_HALF] + [pltpu.VMEM(s, F32) for s in SMALL_HALF]
            + [pltpu.VMEM((3,) + s, F32) for s in SMALL_HALF]
            + [pltpu.SemaphoreType.DMA((n_a,)), pltpu.SemaphoreType.DMA((n_a,)), pltpu.SemaphoreType.DMA((n_a, 3)),
               pltpu.SemaphoreType.DMA((n_a, 3)), pltpu.SemaphoreType.DMA((n_a,)), pltpu.SemaphoreType.DMA((n_a,))])


class _SmallReduce:
    def __init__(self, ins, outs, scratch):
        self.ins, self.outs = ins, outs
        (self.pack, *rest) = scratch
        self.rxs, self.css, self.gs = rest[0:3], rest[3:6], rest[6:9]
        self.s1_send, self.s1_recv, self.s2_send, self.s2_recv, self.s3_send, self.s3_recv = rest[9:]
        self.x, self.y, self.c = lax.axis_index("x"), lax.axis_index("y"), lax.axis_index("c")
        self.sibling = (self.x, self.y, 1 - self.c)
        self.chips = [(1 - self.x, self.y), (self.x, 1 - self.y), (1 - self.x, 1 - self.y)]
        self.hl = [s[0] for s in SMALL_HALF]

    def half(self, ref, a, h):
        return ref.at[pl.ds(h * self.hl[a], self.hl[a])]

    def begin(self):
        dg1_ref, dg2_ref, dgf_ref, dgrn_ref, dlng_ref, dlnb_ref, dbs_ref, loss_ref, dws_ref, dcv_ref = self.ins
        pack, c = self.pack, self.c
        pack[...] = jnp.zeros_like(pack)
        pack[0, 0:1, :] = dg1_ref[0:1, :]
        pack[0, 1:2, :] = dg2_ref[0:1, :]
        pack[0, 2:3, :] = dgf_ref[0:1, :]
        pack[0, 3:4, 0:RET_W] = dgrn_ref[0:1, :]
        lsum = loss_ref[0, 0:1, :]
        for i in range(1, N_TB):
            lsum = lsum + loss_ref[i, 0:1, :]
        pack[0, 3:4, RET_W:RET_W + 128] = lsum
        pack[1, 0:HEADS, 0:128] = dlng_ref[0:HEADS, :]
        pack[1, 0:HEADS, 128:256] = dlnb_ref[0:HEADS, :]
        pack[1, 0:HEADS, 256:384] = dbs_ref[0:HEADS, :]
        self.srcs = [pack, dws_ref, dcv_ref]
        n_a = len(self.srcs)
        self.ex1 = [pltpu.make_async_remote_copy(src_ref=self.half(self.srcs[a], a, 1 - c), dst_ref=self.rxs[a],
                                                 send_sem=self.s1_send.at[a], recv_sem=self.s1_recv.at[a],
                                                 device_id=self.sibling, device_id_type=MESH) for a in range(n_a)]
        for cp in self.ex1:
            cp.start()
        self.ex2 = []
        for a in range(n_a):
            self.ex1[a].wait_recv()
            self.css[a][...] = self.half(self.srcs[a], a, c)[...] + self.rxs[a][...]
            for j, chip in enumerate(self.chips):
                cp = pltpu.make_async_remote_copy(src_ref=self.css[a], dst_ref=self.gs[a].at[j], send_sem=self.s2_send.at[a, j],
                                                  recv_sem=self.s2_recv.at[a, j], device_id=(*chip, c), device_id_type=MESH)
                cp.start()
                self.ex2.append(cp)

    def end(self):
        c, x, y = self.c, self.x, self.y
        ex3 = []
        for a in range(len(self.srcs)):
            css, gs, out = self.css[a], self.gs[a], self.outs[a]
            for j in range(3):
                self.ex2[3 * a + j].wait_recv()
            tot = None
            for q in range(4):
                k = jnp.where(x != (q >> 1), 1, 0) + jnp.where(y != (q & 1), 2, 0)
                term = jnp.where(k == 0, css[...], jnp.where(k == 1, gs[0], jnp.where(k == 2, gs[1], gs[2])))
                tot = term if tot is None else tot + term
            self.half(out, a, c)[...] = tot
            cp = pltpu.make_async_remote_copy(src_ref=self.half(out, a, c), dst_ref=self.half(out, a, c), send_sem=self.s3_send.at[a],
                                              recv_sem=self.s3_recv.at[a], device_id=self.sibling, device_id_type=MESH)
            cp.start()
            ex3.append(cp)
        for a in range(len(self.srcs)):
            out = self.outs[a]
            pltpu.make_async_remote_copy(src_ref=self.half(out, a, 1 - c), dst_ref=self.half(out, a, 1 - c), send_sem=self.s3_send.at[a],
                                         recv_sem=self.s3_recv.at[a], device_id=self.sibling, device_id_type=MESH).wait_recv()
        for cp in self.ex1 + self.ex2 + ex3:
            cp.wait_send()


def _adam_math(w, g, m, v):
    nm = ADAM_B1 * m + (1.0 - ADAM_B1) * g
    nv = ADAM_B2 * v + (1.0 - ADAM_B2) * (g * g)
    d = -ADAM_LR * ((nm / (1.0 - ADAM_B1 ** ADAM_STEP)) / (jnp.sqrt(nv / (1.0 - ADAM_B2 ** ADAM_STEP)) + ADAM_EPS) + ADAM_WD * w)
    return d, nm, nv


def _adamw(params, thru, n_steps):
    plan = []
    for w, gs, _, _ in params:
        _, r, cdim = w.shape
        if len(gs) == 1:
            edges = [0, r // n_steps]
            spec3 = pl.BlockSpec((1, r // n_steps, cdim), lambda i: (0, i, 0))
            g_specs = [pl.BlockSpec((r // n_steps, cdim), lambda i: (i, 0))]
        else:
            edges = [sum(g.shape[0] for g in gs[:k]) for k in range(len(gs) + 1)]
            spec3 = pl.BlockSpec((1, r, cdim // n_steps), lambda i: (0, 0, i))
            g_specs = [pl.BlockSpec((g.shape[0], cdim // n_steps), lambda i: (0, i)) for g in gs]
        plan.append((len(gs), edges, spec3, g_specs))
    n_in = sum(n_g + 3 for n_g, _, _, _ in plan)
    n_t = len(thru)

    def body(*refs):
        ins, outs = refs[:n_in], refs[n_in + n_t:]
        for n_g, edges, _, _ in plan:
            (w_ref, *g_refs, m_ref, v_ref), ins = ins[:n_g + 3], ins[n_g + 3:]
            (go_ref, d_ref, nm_ref, nv_ref), outs = outs[:4], outs[4:]
            for g_ref, lo, hi in zip(g_refs, edges[:-1], edges[1:]):
                gg = g_ref[...]
                go_ref[0, lo:hi, :] = gg
                d_ref[0, lo:hi, :], nm_ref[0, lo:hi, :], nv_ref[0, lo:hi, :] = _adam_math(
                    w_ref[0, lo:hi, :], gg, m_ref[0, lo:hi, :], v_ref[0, lo:hi, :])
        for t_ref, to_ref in zip(refs[n_in:n_in + n_t], outs):
            to_ref[...] = t_ref[...]

    t_specs = [pl.BlockSpec((t.shape[0] // n_steps, t.shape[1]), lambda i: (i, 0)) for t in thru]
    in_specs, out_specs, out_shape, args = [], [], [], []
    for (w, gs, m, v), (_, _, spec3, g_specs) in zip(params, plan):
        in_specs += [spec3] + g_specs + [spec3, spec3]
        out_specs += [spec3] * 4
        out_shape += [jax.ShapeDtypeStruct(w.shape, F32)] * 4
        args += [w, *gs, m, v]
    res = pl.pallas_call(
        body, name="adamw", grid=(n_steps,), out_shape=tuple(out_shape) + tuple(jax.ShapeDtypeStruct(t.shape, t.dtype) for t in thru),
        in_specs=in_specs + t_specs, out_specs=tuple(out_specs) + tuple(t_specs),
        compiler_params=_cparams(("parallel",)),
    )(*args, *thru)
    return [res[4 * k:4 * k + 4] for k in range(len(params))], res[4 * len(params):]


def _adamw_small(rp, rws, rcv, gcw, params):
    n_p = len(params)

    def body(*refs):
        rp_ref, rws_ref, rcv_ref, gcw_ref = refs[:4]
        ins = refs[4:4 + 3 * n_p]
        outs = refs[4 + 3 * n_p:]
        outs[4 * n_p][...] = rp_ref[0, 3:4, RET_W:RET_W + 1]
        grads = [rp_ref[0, 0:1, :], rp_ref[0, 1:2, :], rp_ref[0, 2:3, :], rp_ref[0, 3:4, 0:RET_W],
                 rp_ref[1, 0:HEADS, 0:128], rp_ref[1, 0:HEADS, 128:256], rp_ref[1, 0:HEADS, 256:384],
                 rws_ref[...], gcw_ref[...], None]
        for p in range(n_p):
            w_ref, m_ref, v_ref = ins[3 * p:3 * p + 3]
            o = outs[4 * p:4 * p + 4]
            if p == n_p - 1:
                for hf in range(2):
                    cs = slice(hf * D_FF, (hf + 1) * D_FF)
                    g = rcv_ref[hf, 3:4, :]
                    res = (g,) + _adam_math(w_ref[:, cs], g, m_ref[:, cs], v_ref[:, cs])
                    for t in range(4):
                        o[t][:, cs] = res[t]
                continue
            lead = w_ref.ndim > grads[p].ndim
            rd = (lambda r: r[0]) if lead else (lambda r: r[...])
            res = (grads[p],) + _adam_math(rd(w_ref), grads[p], rd(m_ref), rd(v_ref))
            for t in range(4):
                if lead:
                    o[t][0] = res[t]
                else:
                    o[t][...] = res[t]

    vm = pl.BlockSpec(memory_space=pltpu.VMEM)
    flat = [a for tr in params for a in tr]
    out_shape = tuple(jax.ShapeDtypeStruct(tr[0].shape, F32) for tr in params for _ in range(4)) + (jax.ShapeDtypeStruct((1, 1), F32),)
    res = pl.pallas_call(
        body, name="adamw_small", out_shape=out_shape, in_specs=[vm] * (4 + len(flat)), out_specs=(vm,) * len(out_shape),
        compiler_params=_cparams(),
    )(rp, rws, rcv, gcw, *flat)
    return [res[4 * p:4 * p + 4] for p in range(n_p)], res[4 * n_p]


def kernel(x, mix_norm_g, w_in, ret_norm_g, sgu_ln_g, sgu_ln_b, sgu_w_s, sgu_b_s, w_out, ffn_norm_g, w_up, conv_w, conv_b, w_down, final_norm_g, loss_target, m_mix_norm_g, m_w_in, m_ret_norm_g, m_sgu_ln_g, m_sgu_ln_b, m_sgu_w_s, m_sgu_b_s, m_w_out, m_ffn_norm_g, m_w_up, m_conv_w, m_conv_b, m_w_down, m_final_norm_g, v_mix_norm_g, v_w_in, v_ret_norm_g, v_sgu_ln_g, v_sgu_ln_b, v_sgu_w_s, v_sgu_b_s, v_w_out, v_ffn_norm_g, v_w_up, v_conv_w, v_conv_b, v_w_down, v_final_norm_g):
    xs = x[0]
    tgt = loss_target[0]
    mask, qdec, kdec = _decay_tables()
    grn = ret_norm_g.reshape(1, RET_W)
    lng = sgu_ln_g.reshape(1, SGU_W)
    lnb = sgu_ln_b.reshape(1, SGU_W)
    ws = sgu_w_s[0]
    bsb = jnp.broadcast_to(sgu_b_s[0][:, :, None], (HEADS, CHUNK, HEAD_DIM))
    gf = final_norm_g.reshape(1, D_MODEL)
    me = 4 * lax.axis_index("x") + 2 * lax.axis_index("y") + lax.axis_index("c")
    tr = lambda a: jnp.transpose(a[0])[None]
    tr_cw = lambda a: jnp.transpose(a, (1, 0, 2))

    proj, h1, cos2, sin2, win_g, cw_sh, wout_g, wdn_g, su = _fwd_proj(
        xs, mix_norm_g, _rope_freq(), w_in[0], w_out[0], tr(w_up)[0], w_down[0], tr_cw(conv_w))
    cw_g = jnp.transpose(cw_sh, (1, 0, 2)).reshape(8, 2 * D_FF)
    x2, mixcat, o, sprev, wup_g = _fwd_mix(xs, proj, wout_g, grn, lng, lnb, ws, bsb, mask, qdec, kdec, su)
    h2, up_pre, u_conv, act, x3, loss_parts = _fwd_ffn(x2, ffn_norm_g, wup_g, cw_g, conv_b, wdn_g, gf, tgt)

    dx3, dpre, dx2, dgf, dg2, dcv = _bwd_ffn(x3, tgt, gf, x2, ffn_norm_g, up_pre, u_conv, wup_g, cw_g, wdn_g)
    band = 512
    (gdn_p,) = _wgrad("wgrad_down", act, dx3, tm=FF_TILE)
    (gout_p,) = _wgrad("wgrad_out", mixcat, dx2, tn=512)
    gup_p, g_dn = _wgrad("wgrad_up", dpre, h2, tm=FF_TILE, tn=512, hosted=[(W_DOWN, gdn_p)])
    dproj, dgrn, dlng, dlnb, dws, dbs, g_up_a, g_out = _bwd_mix(
        dx2, proj, o, sprev, wout_g, grn, lng, lnb, ws, bsb, mask, qdec, kdec, cos2, sin2,
        [((W_UP, 0, band), gup_p), (W_OUT, gout_p)])
    gin_p, g_up_b = _wgrad("wgrad_in", h1, dproj, tm=512, tn=768, hosted=[((W_UP, band, FF_SHARD - band), gup_p)])
    grad_x, g_in, rp, rws, rcv = _bwd_proj(dproj, win_g, xs, mix_norm_g, dx2, gin_p,
                                           (dg2, dgf, dgrn, dlng, dlnb, dbs, loss_parts, dws, dcv))
    gcw = tr_cw(lax.dynamic_slice(rcv, (me // (N_DEV // 2), 0, (me % (N_DEV // 2)) * FF_SHARD), (1, 3, FF_SHARD)))

    table = {}
    big, (grad_x,) = _adamw([(w_in, [g_in], m_w_in, v_w_in), (w_out, [g_out], m_w_out, v_w_out),
                             (tr(w_up), [g_up_a, g_up_b], tr(m_w_up), tr(v_w_up)), (w_down, [g_dn], m_w_down, v_w_down)],
                            [grad_x], n_steps=4)
    table.update(zip(("w_in", "w_out", "w_up", "w_down"), big))
    table["w_up"] = tuple(tr(a) for a in table["w_up"])
    row = lambda a: a.reshape(1, D_MODEL)
    names_small = ["mix_norm_g", "ffn_norm_g", "final_norm_g", "ret_norm_g", "sgu_ln_g", "sgu_ln_b", "sgu_b_s", "sgu_w_s",
                   "conv_w", "conv_b"]
    params = [(mix_norm_g, m_mix_norm_g, v_mix_norm_g), (ffn_norm_g, m_ffn_norm_g, v_ffn_norm_g),
              (row(final_norm_g), row(m_final_norm_g), row(v_final_norm_g)), (ret_norm_g, m_ret_norm_g, v_ret_norm_g),
              (sgu_ln_g, m_sgu_ln_g, v_sgu_ln_g), (sgu_ln_b, m_sgu_ln_b, v_sgu_ln_b), (sgu_b_s, m_sgu_b_s, v_sgu_b_s),
              (sgu_w_s, m_sgu_w_s, v_sgu_w_s), (tr_cw(conv_w), tr_cw(m_conv_w), tr_cw(v_conv_w)), (conv_b, m_conv_b, v_conv_b)]
    small, loss = _adamw_small(rp, rws, rcv, gcw, params)
    for n, res in zip(names_small, small):
        table[n] = res
    table["final_norm_g"] = tuple(a.reshape(D_MODEL) for a in table["final_norm_g"])
    table["conv_w"] = tuple(tr_cw(a) for a in table["conv_w"])

    order = ["mix_norm_g", "w_in", "ret_norm_g", "sgu_ln_g", "sgu_ln_b", "sgu_w_s", "sgu_b_s", "w_out", "ffn_norm_g", "w_up",
             "conv_w", "conv_b", "w_down", "final_norm_g"]
    outs = [loss.reshape(()), grad_x[None]]
    for col in range(4):
        outs += [table[n][col] for n in order]
    return tuple(outs)
```

```python
import functools
import math

import jax
import jax.numpy as jnp
import numpy as np
from jax import lax
from jax.experimental import pallas as pl
from jax.experimental.pallas import tpu as pltpu

F32 = jnp.float32
BF16 = jnp.bfloat16
MESH = pl.DeviceIdType.MESH

N_DEV = 8
SEQ = 2048
D_MODEL = 1024
CHUNK = 128
N_CHUNK = SEQ // CHUNK
HEADS = 4
HEAD_DIM = 128
RET_W = 512
SGU_W = 512
PROJ_W = 3072
D_FF = 2816
FF_SHARD = 704
FF_TILE = 1408
FF_TILES = ((0, 1536), (1536, 1280))
IN_SHARD = PROJ_W // N_DEV
OUT_SHARD = D_MODEL // N_DEV
DOWN_SHARD = D_FF // N_DEV
TM = 256
N_TB = SEQ // TM
FWD_PROJ_PASS_AT = 5
FWD_MIX_PASS_AT = 10
EPS = 1e-6
ROPE_BASE = 10000.0
K_SCALE = HEAD_DIM ** -0.5
INV_SQRT2 = 0.7071067811865476
INV_SQRT_2PI = 0.3989422804014327

ADAM_LR = 0.001
ADAM_B1 = 0.9
ADAM_B2 = 0.999
ADAM_EPS = 1e-08
ADAM_WD = 0.01
ADAM_STEP = 10

VMEM_LIMIT = 56 * 1024 * 1024


def _cparams(sem=None, vmem=VMEM_LIMIT, collective=None):
    return pltpu.CompilerParams(dimension_semantics=sem, vmem_limit_bytes=vmem, collective_id=collective)


COLLECTIVE = {name: k for k, name in enumerate(("fwd_proj", "fwd_mix", "wgrad_up", "bwd_mix", "wgrad_in", "bwd_proj"))}


class _Meet:
    def __init__(self, diagonal):
        x, y, c = lax.axis_index("x"), lax.axis_index("y"), lax.axis_index("c")
        self.peers = [(x, y, 1 - c), (1 - x, y, c), (x, 1 - y, c)] + ([(1 - x, 1 - y, c)] if diagonal else [])

    def signal(self):
        for peer in self.peers:
            pl.semaphore_signal(pltpu.get_barrier_semaphore(), inc=1, device_id=peer, device_id_type=MESH)

    def wait(self):
        pl.semaphore_wait(pltpu.get_barrier_semaphore(), len(self.peers))


def _resident(shape):
    nd = len(shape)
    return pl.BlockSpec(shape, lambda *_: (0,) * nd, pipeline_mode=pl.Buffered(1))


def _dot(a, b):
    return jnp.dot(a, b, preferred_element_type=F32)


def _dot_nt(a, b):
    return lax.dot_general(a, b, (((1,), (1,)), ((), ())), preferred_element_type=F32)


def _dot_tn(a, b):
    return lax.dot_general(a, b, (((0,), (0,)), ((), ())), preferred_element_type=F32)


def _sigmoid(x):
    return 1.0 / (1.0 + jnp.exp(-x))


def _gelu(x):
    return 0.5 * x * (1.0 + lax.erf(x * INV_SQRT2))


def _gelu_grad(x):
    return 0.5 * (1.0 + lax.erf(x * INV_SQRT2)) + x * (jnp.exp(-0.5 * x * x) * INV_SQRT_2PI)


def _rot(xh, cos2, sin2):
    return xh * cos2 + pltpu.roll(xh, HEAD_DIM // 2, 1) * sin2


def _rot_t(dh, cos2, sin2):
    return dh * cos2 + pltpu.roll(dh * sin2, HEAD_DIM // 2, 1)


def _rope_freq():
    half = HEAD_DIM // 2
    inv_freq = jnp.power(ROPE_BASE, -jnp.arange(half, dtype=F32) / half)
    return jnp.concatenate([inv_freq, inv_freq])[None, :]


def _rope_block(inv2, first_row):
    pos = (lax.broadcasted_iota(jnp.int32, (TM, HEAD_DIM), 0) + first_row).astype(F32)
    ang = pos * inv2
    sin = jnp.sin(ang)
    lane = lax.broadcasted_iota(jnp.int32, (TM, HEAD_DIM), 1)
    return jnp.cos(ang), jnp.where(lane < HEAD_DIM // 2, -sin, sin)


def _log_gamma():
    return np.log(np.float32(1.0) - np.power(np.float32(2.0), -5.0 - np.arange(HEADS, dtype=np.float32))).astype(np.float32)


def _fill_decay(mask_ref, qd_ref, kd_ref):
    assert HEAD_DIM == CHUNK
    lg = _log_gamma()
    t = lax.broadcasted_iota(jnp.int32, (CHUNK, CHUNK), 0).astype(F32)
    diff = t - lax.broadcasted_iota(jnp.int32, (CHUNK, CHUNK), 1).astype(F32)
    for h in range(HEADS):
        mask_ref[h] = jnp.where(diff >= 0.0, jnp.exp(float(lg[h]) * jnp.maximum(diff, 0.0)), 0.0)
        qd_ref[h] = jnp.exp(float(lg[h]) * (t + 1.0))
        kd_ref[h] = jnp.exp(float(lg[h]) * (CHUNK - 1.0 - t))


def _chunk_decay():
    lg = _log_gamma()
    return [float(np.exp(lg[h] * np.float32(CHUNK))) for h in range(HEADS)]


W_IN, W_OUT, W_UP, W_DOWN, W_CONV = range(5)
GATHERED = {W_IN: ((D_MODEL, PROJ_W), BF16), W_OUT: ((D_MODEL, D_MODEL), BF16), W_UP: ((2 * D_FF, D_MODEL), BF16),
            W_DOWN: ((D_FF, D_MODEL), BF16), W_CONV: ((N_DEV, 8, FF_SHARD), F32)}
SHARD = {W_IN: (D_MODEL, IN_SHARD), W_OUT: (OUT_SHARD, D_MODEL), W_UP: (FF_SHARD, D_MODEL), W_DOWN: (DOWN_SHARD, D_MODEL),
         W_CONV: (8, FF_SHARD)}


class _Gather:
    N_SEMS = 9

    def __init__(self, ids, stages, gathered, send_sems, recv_sems, local_sems):
        self.ids, self.stages, self.gathered = ids, stages, gathered
        self.send_sems, self.recv_sems, self.local_sems = send_sems, recv_sems, local_sems
        self.x, self.y, self.c = lax.axis_index("x"), lax.axis_index("y"), lax.axis_index("c")
        self.me = (self.x, self.y, self.c)
        self.sibling = (self.x, self.y, 1 - self.c)
        self.chips = [(1 - self.x, self.y), (self.x, 1 - self.y), (1 - self.x, 1 - self.y)]

    def slot(self, n, px, py, pc):
        dev = 4 * px + 2 * py + pc
        w, g = self.ids[n], self.gathered[n]
        if w == W_IN:
            return g.at[:, pl.ds(pl.multiple_of(dev * IN_SHARD, 128), IN_SHARD)]
        if w == W_OUT:
            return g.at[pl.ds(pl.multiple_of(dev * OUT_SHARD, 128), OUT_SHARD), :]
        if w == W_DOWN:
            return g.at[pl.ds(pl.multiple_of(dev * DOWN_SHARD, 32), DOWN_SHARD), :]
        if w == W_UP:
            return g.at[pl.ds(pl.multiple_of(dev * FF_SHARD, 32), FF_SHARD), :]
        return g.at[dev]

    def half(self, n, px, py, pc, h):
        dev = 4 * px + 2 * py + pc
        w, g = self.ids[n], self.gathered[n]
        if w == W_IN:
            return g.at[pl.ds(h * (D_MODEL // 2), D_MODEL // 2), pl.ds(pl.multiple_of(dev * IN_SHARD, 128), IN_SHARD)]
        rows = SHARD[w][0] // 2
        return g.at[pl.ds(pl.multiple_of(dev * SHARD[w][0] + h * rows, 16), rows), :]

    def tree(self, n):
        return self.ids[n] != W_CONV

    def copy(self, n, k, block, to, src=None, h=None):
        ref = self.slot(n, *block) if h is None else self.half(n, *block, h)
        return pltpu.make_async_remote_copy(
            src_ref=ref if src is None else src, dst_ref=ref,
            send_sem=self.send_sems.at[n, k], recv_sem=self.recv_sems.at[n, k], device_id=to, device_id_type=MESH)

    def _mine(self):
        return [pltpu.make_async_copy(self.stages[n], self.slot(n, *self.me), self.local_sems.at[n]) for n in range(len(self.ids))]

    def _first(self):
        out = []
        for n in range(len(self.ids)):
            out.append(self.copy(n, 0, self.me, self.sibling, src=self.stages[n]))
            out += [self.copy(n, 1 + j, self.me, (*chip, self.c), src=self.stages[n])
                    for j, chip in enumerate(self.chips[:2] if self.tree(n) else self.chips)]
        return out

    def start(self):
        for cp in self._mine() + self._first():
            cp.start()

    def _passed(self, j):
        dev = (*self.chips[j], self.c)
        out = []
        for n in range(len(self.ids)):
            if not self.tree(n):
                out.append(self.copy(n, 4 + j, dev, self.sibling))
            elif j < 2:
                out += [self.copy(n, 3 + j, dev, (*self.chips[1 - j], self.c), h=j), self.copy(n, 5 + j, dev, self.sibling)]
            else:
                out += [self.copy(n, 7, dev, self.sibling, h=0), self.copy(n, 8, dev, self.sibling, h=1)]
        return out

    def near(self):
        for j in range(2):
            dev = (*self.chips[j], self.c)
            for n in range(len(self.ids)):
                self.copy(n, 1 + j, dev, self.me).wait_recv()
            for cp in self._passed(j):
                cp.start()

    def finish(self):
        dev = (*self.chips[2], self.c)
        for n in range(len(self.ids)):
            if self.tree(n):
                self.copy(n, 3, dev, self.me, h=0).wait_recv()
                self.copy(n, 4, dev, self.me, h=1).wait_recv()
            else:
                self.copy(n, 3, dev, self.me).wait_recv()
        for cp in self._passed(2):
            cp.start()
        for n in range(len(self.ids)):
            self.copy(n, 0, self.sibling, self.me).wait_recv()
            for j, chip in enumerate(self.chips):
                dev = (*chip, 1 - self.c)
                if not self.tree(n):
                    self.copy(n, 4 + j, dev, self.me).wait_recv()
                elif j < 2:
                    self.copy(n, 5 + j, dev, self.me).wait_recv()
                else:
                    self.copy(n, 7, dev, self.me, h=0).wait_recv()
                    self.copy(n, 8, dev, self.me, h=1).wait_recv()
        for cp in self._mine():
            cp.wait()
        for cp in self._first() + self._passed(0) + self._passed(1) + self._passed(2):
            cp.wait_send()


def _gather_scratch(n):
    return [pltpu.SemaphoreType.DMA((n, _Gather.N_SEMS)), pltpu.SemaphoreType.DMA((n, _Gather.N_SEMS)), pltpu.SemaphoreType.DMA((n,))]


def _gathered_shapes(ids):
    return tuple(jax.ShapeDtypeStruct(*GATHERED[w]) for w in ids)


def _fwd_proj(x, g1, inv2, w_in, w_out, w_up, w_down, conv_w):
    ids_a, ids_b = [W_IN, W_CONV], [W_OUT, W_DOWN]

    def body(x_ref, g_ref, inv_ref, in_hbm, out_hbm, up_hbm, dn_hbm, cw_ref,
             proj_ref, h1_ref, cos_ref, sin_ref, gin, gcw, gout, gdn, su_ref,
             w_vm, s_in, s_cw, s_out, s_dn, f_in, f_out, f_up, f_dn, ld_sems,
             a_send, a_recv, a_local, b_send, b_recv, b_local):
        ag_a = _Gather(ids_a, [s_in, s_cw], [gin, gcw], a_send, a_recv, a_local)
        ag_b = _Gather(ids_b, [s_out, s_dn], [gout, gdn], b_send, b_recv, b_local)

        @pl.when(pl.program_id(0) == 0)
        def _():
            meet = _Meet(diagonal=True)
            meet.signal()
            loads = [pltpu.make_async_copy(src, dst, ld_sems.at[i])
                     for i, (src, dst) in enumerate(((in_hbm, f_in), (out_hbm, f_out), (dn_hbm, f_dn), (up_hbm, f_up)))]
            for cp in loads:
                cp.start()
            s_cw[...] = jnp.zeros_like(s_cw)
            for k in range(3):
                s_cw[k:k + 1, :] = cw_ref[k]
            loads[0].wait()
            s_in[...] = f_in[...].astype(BF16)
            meet.wait()
            ag_a.start()
            loads[1].wait()
            s_out[...] = f_out[...].astype(BF16)
            loads[2].wait()
            s_dn[...] = f_dn[...].astype(BF16)
            ag_a.near()
            ag_b.start()
            loads[3].wait()
            su_ref[...] = f_up[...].astype(BF16)
            ag_a.finish()
            fill = pltpu.make_async_copy(gin, w_vm, ld_sems.at[4])
            fill.start()
            fill.wait()

        pl.when(pl.program_id(0) == FWD_PROJ_PASS_AT)(ag_b.near)

        xb = x_ref[...]
        r = lax.rsqrt(jnp.mean(xb * xb, axis=-1, keepdims=True) + EPS)
        h = ((xb * r) * g_ref[...]).astype(BF16)
        h1_ref[...] = h
        p = _dot(h, w_vm[...])
        c2, s2 = _rope_block(inv_ref[...], pl.program_id(0) * TM)
        cos_ref[...], sin_ref[...] = c2, s2
        for hd in range(HEADS):
            sl = slice(hd * HEAD_DIM, (hd + 1) * HEAD_DIM)
            proj_ref[:, sl] = _rot(p[:, sl], c2, s2)
            ks = slice(RET_W + hd * HEAD_DIM, RET_W + (hd + 1) * HEAD_DIM)
            proj_ref[:, ks] = _rot(p[:, ks], c2, s2) * K_SCALE
        proj_ref[:, 2 * RET_W:] = p[:, 2 * RET_W:]

        pl.when(pl.program_id(0) == N_TB - 1)(ag_b.finish)

    tok = lambda w: pl.BlockSpec((TM, w), lambda i: (i, 0))
    hbm = pl.BlockSpec(memory_space=pl.ANY)
    vm = pl.BlockSpec(memory_space=pltpu.VMEM)
    return pl.pallas_call(
        body, name="fwd_proj", grid=(N_TB,),
        out_shape=(jax.ShapeDtypeStruct((SEQ, PROJ_W), F32), jax.ShapeDtypeStruct((SEQ, D_MODEL), BF16),
                   jax.ShapeDtypeStruct((SEQ, HEAD_DIM), F32), jax.ShapeDtypeStruct((SEQ, HEAD_DIM), F32))
        + _gathered_shapes(ids_a + ids_b) + (jax.ShapeDtypeStruct(SHARD[W_UP], BF16),),
        in_specs=[tok(D_MODEL), _resident((1, D_MODEL)), _resident((1, HEAD_DIM)), hbm, hbm, hbm, hbm, vm],
        out_specs=(tok(PROJ_W), tok(D_MODEL), tok(HEAD_DIM), tok(HEAD_DIM), hbm, hbm, hbm, hbm, vm),
        scratch_shapes=[pltpu.VMEM((D_MODEL, PROJ_W), BF16), pltpu.VMEM(SHARD[W_IN], BF16), pltpu.VMEM(SHARD[W_CONV], F32),
                        pltpu.VMEM(SHARD[W_OUT], BF16), pltpu.VMEM(SHARD[W_DOWN], BF16),
                        pltpu.VMEM(SHARD[W_IN], F32), pltpu.VMEM(SHARD[W_OUT], F32), pltpu.VMEM(SHARD[W_UP], F32),
                        pltpu.VMEM(SHARD[W_DOWN], F32), pltpu.SemaphoreType.DMA((5,))]
        + _gather_scratch(len(ids_a)) + _gather_scratch(len(ids_b)),
        compiler_params=_cparams(("arbitrary",), collective=COLLECTIVE["fwd_proj"]),
    )(x, g1, inv2, w_in, w_out, w_up, w_down, conv_w)


def _causal(w):
    r = lax.broadcasted_iota(jnp.int32, (CHUNK, CHUNK), 0)
    c = lax.broadcasted_iota(jnp.int32, (CHUNK, CHUNK), 1)
    return jnp.where(r >= c, w, 0.0)


def _fwd_mix(x, proj, wout_g, grn, lng, lnb, ws, bsb, su):
    cdec = _chunk_decay()
    ids = [W_UP]

    def body(x_ref, p_ref, w_ref, grn_ref, lng_ref, lnb_ref, ws_ref, bsb_ref, su_ref,
             x2_ref, cat_ref, o_ref, sp_ref, gup, state, m_ref, qd_ref, kd_ref, send_sems, recv_sems, local_sems):
        ag = _Gather(ids, [su_ref], [gup], send_sems, recv_sems, local_sems)

        @pl.when(pl.program_id(0) == 0)
        def _():
            meet = _Meet(diagonal=False)
            meet.signal()
            state[...] = jnp.zeros_like(state)
            _fill_decay(m_ref, qd_ref, kd_ref)
            meet.wait()
            ag.start()

        for h in range(HEADS):
            sl = slice(h * HEAD_DIM, (h + 1) * HEAD_DIM)
            q = p_ref[:, sl]
            k = p_ref[:, RET_W + h * HEAD_DIM:RET_W + (h + 1) * HEAD_DIM]
            v = p_ref[:, 2 * RET_W + h * HEAD_DIM:2 * RET_W + (h + 1) * HEAD_DIM]
            g = p_ref[:, 3 * RET_W + h * HEAD_DIM:3 * RET_W + (h + 1) * HEAD_DIM]
            qb, kb, vb = q.astype(BF16), k.astype(BF16), v.astype(BF16)
            a = _dot_nt(qb, kb) * m_ref[h]
            spb = state[h].astype(BF16)
            sp_ref[0, h] = spb
            o = _dot(a.astype(BF16), vb) + _dot((q * qd_ref[h]).astype(BF16), spb)
            state[h] = state[h] * cdec[h] + _dot_tn((k * kd_ref[h]).astype(BF16), vb)
            o_ref[:, sl] = o
            rinv = lax.rsqrt(jnp.mean(o * o, axis=-1, keepdims=True) + EPS)
            rn = (o * rinv) * grn_ref[:, sl]
            cat_ref[:, sl] = ((g * _sigmoid(g)) * rn).astype(BF16)
        for gi in range(HEADS):
            sl = slice(gi * HEAD_DIM, (gi + 1) * HEAD_DIM)
            u = p_ref[:, 4 * RET_W + gi * HEAD_DIM:4 * RET_W + (gi + 1) * HEAD_DIM]
            sv = p_ref[:, 4 * RET_W + SGU_W + gi * HEAD_DIM:4 * RET_W + SGU_W + (gi + 1) * HEAD_DIM]
            gv = _gelu(sv)
            xc = gv - jnp.mean(gv, axis=-1, keepdims=True)
            vn = (xc * lax.rsqrt(jnp.mean(xc * xc, axis=-1, keepdims=True) + EPS)) * lng_ref[:, sl] + lnb_ref[:, sl]
            mixed = _dot(_causal(ws_ref[gi]).astype(BF16), vn.astype(BF16)) + bsb_ref[gi]
            cat_ref[:, RET_W + gi * HEAD_DIM:RET_W + (gi + 1) * HEAD_DIM] = (_gelu(u) * mixed).astype(BF16)
        x2_ref[...] = x_ref[...] + _dot(cat_ref[...], w_ref[...])

        pl.when(pl.program_id(0) == FWD_MIX_PASS_AT)(ag.near)
        pl.when(pl.program_id(0) == N_CHUNK - 1)(ag.finish)

    ch = lambda w: pl.BlockSpec((CHUNK, w), lambda i: (i, 0))
    hcc = (HEADS, CHUNK, CHUNK)
    hbm = pl.BlockSpec(memory_space=pl.ANY)
    return pl.pallas_call(
        body, name="fwd_mix", grid=(N_CHUNK,),
        out_shape=(jax.ShapeDtypeStruct((SEQ, D_MODEL), F32), jax.ShapeDtypeStruct((SEQ, D_MODEL), BF16),
                   jax.ShapeDtypeStruct((SEQ, RET_W), F32), jax.ShapeDtypeStruct((N_CHUNK, HEADS, HEAD_DIM, HEAD_DIM), BF16))
        + _gathered_shapes(ids),
        in_specs=[ch(D_MODEL), ch(PROJ_W), _resident((D_MODEL, D_MODEL)), _resident((1, RET_W)), _resident((1, SGU_W)),
                  _resident((1, SGU_W)), _resident(hcc), _resident(hcc), hbm],
        out_specs=(ch(D_MODEL), ch(D_MODEL), ch(RET_W), pl.BlockSpec((1, HEADS, HEAD_DIM, HEAD_DIM), lambda i: (i, 0, 0, 0)), hbm),
        scratch_shapes=[pltpu.VMEM((HEADS, HEAD_DIM, HEAD_DIM), F32)] + [pltpu.VMEM(hcc, F32)] * 3 + _gather_scratch(len(ids)),
        compiler_params=_cparams(("arbitrary",), collective=COLLECTIVE["fwd_mix"]),
    )(x, proj, wout_g, grn, lng, lnb, ws, bsb, su)


def _conv_taps(p, prev8):
    row = lax.broadcasted_iota(jnp.int32, p.shape, 0)
    p1 = jnp.where(row == 0, prev8[7:8, :], pltpu.roll(p, 1, 0))
    p2 = jnp.where(row == 0, prev8[6:7, :], jnp.where(row == 1, prev8[7:8, :], pltpu.roll(p, 2, 0)))
    return p1, p2


def _fwd_ffn(x2, g2, wup_g, cw_g, cb_g, wdn_g, gf, tgt):
    def body(x_ref, g_ref, wu_ref, cw_ref, cb_ref, wd_ref, gf_ref, t_ref, h2_ref, up_ref, u_ref, act_ref, x3_ref, loss_ref, carry):
        @pl.when(pl.program_id(0) == 0)
        def _():
            carry[...] = jnp.zeros_like(carry)

        xb = x_ref[...]
        r = lax.rsqrt(jnp.mean(xb * xb, axis=-1, keepdims=True) + EPS)
        h = ((xb * r) * g_ref[...]).astype(BF16)
        h2_ref[...] = h
        acc = xb
        for t0, tw in FF_TILES:
            u = []
            for c0 in (t0, D_FF + t0):
                cs = slice(c0, c0 + tw)
                p = _dot_nt(h, wu_ref[pl.ds(c0, tw), :])
                up_ref[:, cs] = p.astype(BF16)
                p1, p2 = _conv_taps(p, carry[:, cs])
                carry[:, cs] = p[TM - 8:, :]
                us = p2 * cw_ref[0:1, cs] + p1 * cw_ref[1:2, cs] + p * cw_ref[2:3, cs] + cb_ref[:, cs]
                u_ref[:, cs] = us.astype(BF16)
                u.append(us)
            a = ((u[0] * _sigmoid(u[0])) * u[1]).astype(BF16)
            act_ref[:, t0:t0 + tw] = a
            acc = acc + _dot(a, wd_ref[pl.ds(t0, tw), :])
        x3_ref[...] = acc
        r3 = lax.rsqrt(jnp.mean(acc * acc, axis=-1, keepdims=True) + EPS)
        diff = (acc * r3) * gf_ref[...] - t_ref[...]
        loss_ref[...] = jnp.full(loss_ref.shape, 0.5 * jnp.sum(jnp.mean(diff * diff, axis=-1)), F32)

    tok = lambda w: pl.BlockSpec((TM, w), lambda i: (i, 0))
    return pl.pallas_call(
        body, name="fwd_ffn", grid=(N_TB,),
        out_shape=(jax.ShapeDtypeStruct((SEQ, D_MODEL), BF16), jax.ShapeDtypeStruct((SEQ, 2 * D_FF), BF16),
                   jax.ShapeDtypeStruct((SEQ, 2 * D_FF), BF16),
                   jax.ShapeDtypeStruct((SEQ, D_FF), BF16), jax.ShapeDtypeStruct((SEQ, D_MODEL), F32),
                   jax.ShapeDtypeStruct((N_TB, 8, 128), F32)),
        in_specs=[tok(D_MODEL), _resident((1, D_MODEL)), _resident((2 * D_FF, D_MODEL)), _resident((8, 2 * D_FF)),
                  _resident((1, 2 * D_FF)), _resident((D_FF, D_MODEL)), _resident((1, D_MODEL)), tok(D_MODEL)],
        out_specs=(tok(D_MODEL), tok(2 * D_FF), tok(2 * D_FF), tok(D_FF), tok(D_MODEL),
                   pl.BlockSpec((1, 8, 128), lambda i: (i, 0, 0))),
        scratch_shapes=[pltpu.VMEM((8, 2 * D_FF), F32)],
        compiler_params=_cparams(("arbitrary",)),
    )(x2, g2, wup_g, cw_g, cb_g, wdn_g, gf, tgt)


def _bwd_ffn(x3, tgt, gf, x2, g2, up_pre, u_conv, wup_g, cw_g, wdn_g):
    def body(x3_ref, t_ref, gf_ref, x2_ref, g2_ref, up_ref, u_ref, wu_ref, cw_ref, wd_ref,
             dx3_ref, dpre_ref, dx2_ref, dgf_ref, dg2_ref, dcv_ref, nxt):
        i = pl.program_id(0)

        @pl.when(i == 0)
        def _():
            nxt[...] = jnp.zeros_like(nxt)
            dgf_ref[...] = jnp.zeros_like(dgf_ref)
            dg2_ref[...] = jnp.zeros_like(dg2_ref)
            dcv_ref[...] = jnp.zeros_like(dcv_ref)

        x3 = x3_ref[...]
        r3 = lax.rsqrt(jnp.mean(x3 * x3, axis=-1, keepdims=True) + EPS)
        xh3 = x3 * r3
        dy = (xh3 * gf_ref[...] - t_ref[...]) * (1.0 / D_MODEL)
        dgf_ref[0:1, :] += jnp.sum(dy * xh3, axis=0, keepdims=True)
        t3 = dy * gf_ref[...]
        dx3 = r3 * (t3 - xh3 * jnp.mean(t3 * xh3, axis=-1, keepdims=True))
        dx3b = dx3.astype(BF16)
        dx3_ref[...] = dx3b
        dh2 = jnp.zeros((TM, D_MODEL), F32)
        for t0, tw in FF_TILES:
            row = lax.broadcasted_iota(jnp.int32, (TM, tw), 0)
            ts = slice(t0, t0 + tw)
            dact = _dot_nt(dx3b, wd_ref[pl.ds(t0, tw), :])
            ua = u_ref[:, ts].astype(F32)
            ub = u_ref[:, D_FF + t0:D_FF + t0 + tw].astype(F32)
            sg = _sigmoid(ua)
            du = [dact * ub * (sg * (1.0 + ua * (1.0 - sg))), dact * (ua * sg)]
            for n in range(2):
                d = du[n]
                c0 = n * D_FF + t0
                cs = slice(c0, c0 + tw)
                nx = nxt[:, cs]
                n1 = jnp.where(row == TM - 1, nx[0:1, :], pltpu.roll(d, TM - 1, 0))
                n2 = jnp.where(row == TM - 2, nx[0:1, :], jnp.where(row == TM - 1, nx[1:2, :], pltpu.roll(d, TM - 2, 0)))
                nxt[:, cs] = d[0:8, :]
                dp = (d * cw_ref[2:3, cs] + n1 * cw_ref[1:2, cs] + n2 * cw_ref[0:1, cs]).astype(BF16)
                dpre_ref[:, cs] = dp
                p = up_ref[:, cs].astype(F32)
                dcv_ref[n, 0:1, ts] += jnp.sum(n2 * p, axis=0, keepdims=True)
                dcv_ref[n, 1:2, ts] += jnp.sum(n1 * p, axis=0, keepdims=True)
                dcv_ref[n, 2:3, ts] += jnp.sum(d * p, axis=0, keepdims=True)
                dcv_ref[n, 3:4, ts] += jnp.sum(d, axis=0, keepdims=True)
                dh2 = dh2 + _dot(dp, wu_ref[pl.ds(c0, tw), :])
        x2 = x2_ref[...]
        r2 = lax.rsqrt(jnp.mean(x2 * x2, axis=-1, keepdims=True) + EPS)
        xh2 = x2 * r2
        dg2_ref[0:1, :] += jnp.sum(dh2 * xh2, axis=0, keepdims=True)
        t2 = dh2 * g2_ref[...]
        dx2_ref[...] = dx3 + r2 * (t2 - xh2 * jnp.mean(t2 * xh2, axis=-1, keepdims=True))

    rev = lambda w: pl.BlockSpec((TM, w), lambda i: (N_TB - 1 - i, 0))
    acc = lambda s: pl.BlockSpec(s, lambda i: (0,) * len(s))
    return pl.pallas_call(
        body, name="bwd_ffn", grid=(N_TB,),
        out_shape=(jax.ShapeDtypeStruct((SEQ, D_MODEL), BF16), jax.ShapeDtypeStruct((SEQ, 2 * D_FF), BF16),
                   jax.ShapeDtypeStruct((SEQ, D_MODEL), F32), jax.ShapeDtypeStruct((8, D_MODEL), F32),
                   jax.ShapeDtypeStruct((8, D_MODEL), F32), jax.ShapeDtypeStruct((2, 8, D_FF), F32)),
        in_specs=[rev(D_MODEL), rev(D_MODEL), _resident((1, D_MODEL)), rev(D_MODEL), _resident((1, D_MODEL)), rev(2 * D_FF),
                  rev(2 * D_FF), _resident((2 * D_FF, D_MODEL)), _resident((8, 2 * D_FF)), _resident((D_FF, D_MODEL))],
        out_specs=(rev(D_MODEL), rev(2 * D_FF), rev(D_MODEL), acc((8, D_MODEL)), acc((8, D_MODEL)), acc((2, 8, D_FF))),
        scratch_shapes=[pltpu.VMEM((8, 2 * D_FF), F32)],
        compiler_params=_cparams(("arbitrary",)),
    )(x3, tgt, gf, x2, g2, up_pre, u_conv, wup_g, cw_g, wdn_g)


def _bwd_mix(dx2, proj, o, sprev, wout_g, grn, lng, lnb, ws, bsb, cos2, sin2, hosted):
    cdec = _chunk_decay()
    geoms = [g for g, _ in hosted]
    n_h = len(hosted)

    def body(dx2_ref, p_ref, o_ref, sp_ref, w_ref, grn_ref, lng_ref, lnb_ref, ws_ref, bsb_ref, cos_ref, sin_ref, *rest):
        dp_ref, dgrn_ref, dlng_ref, dlnb_ref, dws_ref, dbs_ref = rest[n_h:n_h + 6]
        dstate, dbs_acc, m_ref, qd_ref, kd_ref = rest[2 * n_h + 6:2 * n_h + 11]
        i = pl.program_id(0)
        rs = _Scatters(geoms, rest[:n_h], rest[n_h + 6:2 * n_h + 6], rest[2 * n_h + 11:])
        pl.when(i == 0)(rs.phase1)
        pl.when(i == 3)(rs.phase2)
        pl.when(i == 8)(rs.phase2b)

        @pl.when(i == 0)
        def _():
            _fill_decay(m_ref, qd_ref, kd_ref)
            dstate[...] = jnp.zeros_like(dstate)
            dgrn_ref[...] = jnp.zeros_like(dgrn_ref)
            dlng_ref[...] = jnp.zeros_like(dlng_ref)
            dlnb_ref[...] = jnp.zeros_like(dlnb_ref)
            dws_ref[...] = jnp.zeros_like(dws_ref)
            dbs_ref[...] = jnp.zeros_like(dbs_ref)
            dbs_acc[...] = jnp.zeros_like(dbs_acc)

        dmix = _dot_nt(dx2_ref[...].astype(BF16), w_ref[...])
        for h in range(HEADS):
            sl = slice(h * HEAD_DIM, (h + 1) * HEAD_DIM)
            q = p_ref[:, sl]
            k = p_ref[:, RET_W + h * HEAD_DIM:RET_W + (h + 1) * HEAD_DIM]
            v = p_ref[:, 2 * RET_W + h * HEAD_DIM:2 * RET_W + (h + 1) * HEAD_DIM]
            g = p_ref[:, 3 * RET_W + h * HEAD_DIM:3 * RET_W + (h + 1) * HEAD_DIM]
            o = o_ref[:, sl]
            rinv = lax.rsqrt(jnp.mean(o * o, axis=-1, keepdims=True) + EPS)
            oh = o * rinv
            gr = grn_ref[:, sl]
            sg = _sigmoid(g)
            dret = dmix[:, sl]
            dp_ref[:, 3 * RET_W + h * HEAD_DIM:3 * RET_W + (h + 1) * HEAD_DIM] = (
                dret * (oh * gr) * (sg * (1.0 + g * (1.0 - sg)))).astype(BF16)
            drn = dret * (g * sg)
            dgrn_ref[0:1, sl] += jnp.sum(drn * oh, axis=0, keepdims=True)
            t = drn * gr
            do = rinv * (t - oh * jnp.mean(t * oh, axis=-1, keepdims=True))
            qb, kb, vb, dob = q.astype(BF16), k.astype(BF16), v.astype(BF16), do.astype(BF16)
            m = m_ref[h]
            ab = (_dot_nt(qb, kb) * m).astype(BF16)
            dab = (_dot_nt(dob, vb) * m).astype(BF16)
            spb = sp_ref[0, h]
            dsn = dstate[h]
            dsnb = dsn.astype(BF16)
            qdb = (q * qd_ref[h]).astype(BF16)
            kdb = (k * kd_ref[h]).astype(BF16)
            dq = _dot(dab, kb) + _dot_nt(dob, spb) * qd_ref[h]
            dk = _dot_tn(dab, qb) + _dot_nt(vb, dsnb) * kd_ref[h]
            dv = _dot_tn(ab, dob) + _dot(kdb, dsnb)
            dstate[h] = dsn * cdec[h] + _dot_tn(qdb, dob)
            c2, s2 = cos_ref[...], sin_ref[...]
            dp_ref[:, sl] = _rot_t(dq, c2, s2).astype(BF16)
            dp_ref[:, RET_W + h * HEAD_DIM:RET_W + (h + 1) * HEAD_DIM] = _rot_t(dk * K_SCALE, c2, s2).astype(BF16)
            dp_ref[:, 2 * RET_W + h * HEAD_DIM:2 * RET_W + (h + 1) * HEAD_DIM] = dv.astype(BF16)
        for gi in range(HEADS):
            sl = slice(gi * HEAD_DIM, (gi + 1) * HEAD_DIM)
            u = p_ref[:, 4 * RET_W + gi * HEAD_DIM:4 * RET_W + (gi + 1) * HEAD_DIM]
            sv = p_ref[:, 4 * RET_W + SGU_W + gi * HEAD_DIM:4 * RET_W + SGU_W + (gi + 1) * HEAD_DIM]
            gv = _gelu(sv)
            xc = gv - jnp.mean(gv, axis=-1, keepdims=True)
            rstd = lax.rsqrt(jnp.mean(xc * xc, axis=-1, keepdims=True) + EPS)
            xh = xc * rstd
            lg = lng_ref[:, sl]
            vnb = (xh * lg + lnb_ref[:, sl]).astype(BF16)
            wcb = _causal(ws_ref[gi]).astype(BF16)
            mixed = _dot(wcb, vnb) + bsb_ref[gi]
            dsgu = dmix[:, RET_W + gi * HEAD_DIM:RET_W + (gi + 1) * HEAD_DIM]
            dmixed = dsgu * _gelu(u)
            dmb = dmixed.astype(BF16)
            dws_ref[gi] += _causal(_dot_nt(dmb, vnb))
            dbs_acc[gi] += dmixed
            dvn = _dot_tn(wcb, dmb)
            dlng_ref[gi:gi + 1, :] += jnp.sum(dvn * xh, axis=0, keepdims=True)
            dlnb_ref[gi:gi + 1, :] += jnp.sum(dvn, axis=0, keepdims=True)
            dxh = dvn * lg
            dgv = rstd * (dxh - jnp.mean(dxh, axis=-1, keepdims=True) - xh * jnp.mean(dxh * xh, axis=-1, keepdims=True))
            dp_ref[:, 4 * RET_W + gi * HEAD_DIM:4 * RET_W + (gi + 1) * HEAD_DIM] = (dsgu * mixed * _gelu_grad(u)).astype(BF16)
            dp_ref[:, 4 * RET_W + SGU_W + gi * HEAD_DIM:4 * RET_W + SGU_W + (gi + 1) * HEAD_DIM] = (
                dgv * _gelu_grad(sv)).astype(BF16)

        @pl.when(i == N_CHUNK - 1)
        def _():
            for gi in range(HEADS):
                col = jnp.broadcast_to(jnp.sum(dbs_acc[gi], axis=-1, keepdims=True), (CHUNK, CHUNK))
                dbs_ref[gi:gi + 1, :] = jnp.transpose(col)[0:1, :]
            rs.phase3()

    rev = lambda w: pl.BlockSpec((CHUNK, w), lambda i: (N_CHUNK - 1 - i, 0))
    hcc = (HEADS, CHUNK, CHUNK)
    acc = lambda s: pl.BlockSpec(s, lambda i: (0,) * len(s))
    res = pl.pallas_call(
        body, name="bwd_mix", grid=(N_CHUNK,),
        out_shape=(jax.ShapeDtypeStruct((SEQ, PROJ_W), BF16), jax.ShapeDtypeStruct((8, RET_W), F32),
                   jax.ShapeDtypeStruct((8, HEAD_DIM), F32), jax.ShapeDtypeStruct((8, HEAD_DIM), F32),
                   jax.ShapeDtypeStruct(hcc, F32), jax.ShapeDtypeStruct((8, CHUNK), F32)) + _scatter_out_shapes(geoms),
        in_specs=[rev(D_MODEL), rev(PROJ_W), rev(RET_W),
                  pl.BlockSpec((1, HEADS, HEAD_DIM, HEAD_DIM), lambda i: (N_CHUNK - 1 - i, 0, 0, 0)),
                  _resident((D_MODEL, D_MODEL)), _resident((1, RET_W)), _resident((1, SGU_W)), _resident((1, SGU_W)),
                  _resident(hcc), _resident(hcc), rev(HEAD_DIM), rev(HEAD_DIM)]
        + [pl.BlockSpec(memory_space=pl.ANY)] * n_h,
        out_specs=(rev(PROJ_W), acc((8, RET_W)), acc((8, HEAD_DIM)), acc((8, HEAD_DIM)), acc(hcc), acc((8, CHUNK)))
        + _scatter_out_specs(geoms),
        scratch_shapes=[pltpu.VMEM((HEADS, HEAD_DIM, HEAD_DIM), F32), pltpu.VMEM((HEADS, CHUNK, CHUNK), F32)]
        + [pltpu.VMEM(hcc, F32)] * 3 + _scatter_scratch(geoms),
        compiler_params=_cparams(("arbitrary",), collective=COLLECTIVE["bwd_mix"]),
    )(dx2, proj, o, sprev, wout_g, grn, lng, lnb, ws, bsb, cos2, sin2, *[p for _, p in hosted])
    return tuple(res[:6 + n_h])


def _bwd_proj(dproj, win_g, x, g1, dx2, gin_p, small):
    geoms = [W_IN]
    n_s = len(small)

    def body(dp_ref, w_ref, x_ref, g_ref, dx2_ref, gin_ref, *rest):
        small_refs = rest[:n_s]
        dx_ref, rs_out, rp_ref, rws_ref, rcv_ref, dg_ref = rest[n_s:n_s + 6]
        rs_scratch = rest[n_s + 6:n_s + 6 + N_SCATTER_SCRATCH]
        ar_scratch = rest[n_s + 6 + N_SCATTER_SCRATCH:]
        ar_res = ar_scratch[N_SMALL_SCRATCH:]
        ar = _SmallReduce((dg_ref,) + tuple(small_refs), ar_res, ar_scratch[:N_SMALL_SCRATCH])
        rs = _Scatters(geoms, [gin_ref], [rs_out], rs_scratch)
        pl.when(pl.program_id(0) == 0)(lambda: rs.phase1(diagonal=True))
        pl.when(pl.program_id(0) == 1)(rs.phase2)
        pl.when(pl.program_id(0) == 5)(rs.phase2b)

        @pl.when(pl.program_id(0) == 0)
        def _():
            dg_ref[...] = jnp.zeros_like(dg_ref)

        dh = _dot_nt(dp_ref[...], w_ref[...])
        xb = x_ref[...]
        r = lax.rsqrt(jnp.mean(xb * xb, axis=-1, keepdims=True) + EPS)
        xh = xb * r
        dg_ref[0:1, :] += jnp.sum(dh * xh, axis=0, keepdims=True)
        t = dh * g_ref[...]
        dx_ref[...] = dx2_ref[...] + r * (t - xh * jnp.mean(t * xh, axis=-1, keepdims=True))

        @pl.when(pl.program_id(0) == N_TB - 1)
        def _():
            ar.begin()
            rs.phase3()
            ar.end()
            for o_ref, r_ref in zip((rp_ref, rws_ref, rcv_ref), ar_res):
                o_ref[...] = r_ref[...]

    tok = lambda w: pl.BlockSpec((TM, w), lambda i: (i, 0))
    vm = pl.BlockSpec(memory_space=pltpu.VMEM)
    res = pl.pallas_call(
        body, name="bwd_proj", grid=(N_TB,),
        out_shape=(jax.ShapeDtypeStruct((SEQ, D_MODEL), F32),) + _scatter_out_shapes(geoms)
        + tuple(jax.ShapeDtypeStruct(s, F32) for s in SMALL_FULL),
        in_specs=[tok(PROJ_W), _resident((D_MODEL, PROJ_W)), tok(D_MODEL), _resident((1, D_MODEL)), tok(D_MODEL),
                  pl.BlockSpec(memory_space=pl.ANY)] + [vm] * n_s,
        out_specs=(tok(D_MODEL),) + _scatter_out_specs(geoms) + (vm,) * len(SMALL_FULL),
        scratch_shapes=[pltpu.VMEM((8, D_MODEL), F32)] + _scatter_scratch(geoms) + _small_scratch()
        + [pltpu.VMEM(s, F32) for s in SMALL_FULL],
        compiler_params=_cparams(("arbitrary",), collective=COLLECTIVE["bwd_proj"]),
    )(dproj, win_g, x, g1, dx2, gin_p, *small)
    return res


def _wgrad(name, a, b, tm=None, tn=None, hosted=()):
    m_w, n_w = a.shape[-1], b.shape[-1]
    tm = m_w if tm is None else tm
    tn = n_w if tn is None else tn
    n_steps = (m_w // tm) * (n_w // tn)
    geoms = [g for g, _ in hosted]
    n_h = len(hosted)

    def body(a_ref, b_ref, *rest):
        o_ref = rest[n_h]
        if n_h:
            rs = _Scatters(geoms, rest[:n_h], rest[n_h + 1:2 * n_h + 1], rest[2 * n_h + 1:])
            step = pl.program_id(0) * (n_w // tn) + pl.program_id(1)
            pl.when(step == 0)(rs.phase1)
            pl.when(step == 1)(rs.phase2)
            pl.when(step == n_steps // 2)(rs.phase2b)
        o_ref[...] = _dot_tn(a_ref[...].astype(BF16), b_ref[...].astype(BF16)).astype(BF16)
        if n_h:
            pl.when(step == n_steps - 1)(rs.phase3)

    assert not n_h or n_steps >= 4
    res = pl.pallas_call(
        body, name=name, grid=(m_w // tm, n_w // tn),
        out_shape=(jax.ShapeDtypeStruct((m_w, n_w), BF16),) + _scatter_out_shapes(geoms),
        in_specs=[pl.BlockSpec((SEQ, tm), lambda i, j: (0, i)), pl.BlockSpec((SEQ, tn), lambda i, j: (0, j))]
        + [pl.BlockSpec(memory_space=pl.ANY)] * n_h,
        out_specs=(pl.BlockSpec((tm, tn), lambda i, j: (i, j)),) + _scatter_out_specs(geoms),
        scratch_shapes=_scatter_scratch(geoms),
        compiler_params=_cparams(("arbitrary", "arbitrary"), collective=COLLECTIVE[name]) if n_h else _cparams(("parallel", "parallel")),
    )(a, b, *[p for _, p in hosted])
    return tuple(res[:1 + n_h])


def _row_step(half_rows):
    return max(s for s in range(16, 177, 16) if half_rows % s == 0)


class _Scatter:
    def __init__(self, geom, partial, out, land1, mine, stage2, land2, comb, s1_send, s1_recv, s2_send, s2_recv, ld_sems):
        self.w, self.row0, self.shape = _geom(geom)
        self.partial, self.out, self.land1 = partial, out, land1
        self.mine, self.stage2, self.land2, self.comb = mine, stage2, land2, comb
        self.hr = self.shape[0] // 2
        self.step = _row_step(self.hr)
        self.s1_send, self.s1_recv, self.s2_send, self.s2_recv, self.ld_sems = s1_send, s1_recv, s2_send, s2_recv, ld_sems
        self.x, self.y, self.c = lax.axis_index("x"), lax.axis_index("y"), lax.axis_index("c")
        self.sibling = (self.x, self.y, 1 - self.c)
        self.chips = [(self.x, self.y), (1 - self.x, self.y), (self.x, 1 - self.y), (1 - self.x, 1 - self.y)]

    def block(self, px, py, pc):
        dev = 4 * px + 2 * py + pc
        if self.w == W_IN:
            return self.partial.at[:, pl.ds(pl.multiple_of(dev * IN_SHARD, 128), IN_SHARD)]
        if self.w == W_OUT:
            return self.partial.at[pl.ds(pl.multiple_of(dev * OUT_SHARD, 128), OUT_SHARD), :]
        if self.w == W_DOWN:
            return self.partial.at[pl.ds(pl.multiple_of(dev * DOWN_SHARD, 32), DOWN_SHARD), :]
        return self.partial.at[pl.ds(pl.multiple_of(dev * FF_SHARD + self.row0, 32), self.shape[0]), :]

    def copy1(self, k):
        return pltpu.make_async_remote_copy(
            src_ref=self.block(*self.chips[k], 1 - self.c), dst_ref=self.land1.at[k],
            send_sem=self.s1_send.at[k], recv_sem=self.s1_recv.at[k], device_id=self.sibling, device_id_type=MESH)

    STAGE2 = [(1, 0, 1), (3, 0, 1), (2, 1, 2), (3, 1, 2), (1, 1, 1), (2, 0, 2)]

    def copy2(self, j):
        blk, h, to = self.STAGE2[j]
        src = self.comb.at[j - 4] if j >= 4 else self.stage2.at[blk - 1, pl.ds(h * self.hr, self.hr), :]
        return pltpu.make_async_remote_copy(
            src_ref=src, dst_ref=self.land2.at[j], send_sem=self.s2_send.at[j], recv_sem=self.s2_recv.at[j],
            device_id=(*self.chips[to], self.c), device_id_type=MESH)

    def _rows(self, h=None):
        step = self.step
        lo, n = (0, self.shape[0]) if h is None else (h * self.hr, self.hr)
        return [pl.ds(r0, step) for r0 in range(lo, lo + n, step)]

    def load(self, k):
        return pltpu.make_async_copy(self.block(*self.chips[k], self.c), self.mine.at[k], self.ld_sems.at[k])

    def load_mine(self):
        for k in range(4):
            self.load(k).start()

    def phase1(self):
        for k in range(4):
            self.copy1(k).start()

    def phase2(self, k):
        self.copy1(k).wait_recv()
        self.load(k).wait()
        for rs in self._rows():
            s = self.mine[k, rs, :].astype(F32) + self.land1[k, rs, :].astype(F32)
            if k == 0:
                self.out[rs, :] = s
            else:
                self.stage2[k - 1, rs, :] = s.astype(BF16)
        for j in {3: (1, 3), 1: (0,), 2: (2,), 0: ()}[k]:
            self.copy2(j).start()

    def phase2b(self):
        for j, got in ((4, 3), (5, 1)):
            blk, h, _ = self.STAGE2[j]
            self.copy2(got).wait_recv()
            for i, rs in enumerate(self._rows(h)):
                lr = pl.ds(i * self.step, self.step)
                self.comb[j - 4, lr, :] = (self.stage2[blk - 1, rs, :].astype(F32) + self.land2[got, lr, :].astype(F32)).astype(BF16)
            self.copy2(j).start()

    def phase3(self):
        for j in (0, 5, 4, 2):
            self.copy2(j).wait_recv()
        for h, (first, second) in enumerate(((0, 5), (4, 2))):
            for i, rs in enumerate(self._rows(h)):
                lr = pl.ds(i * self.step, self.step)
                self.out[rs, :] = (self.out[rs, :] + self.land2[first, lr, :].astype(F32)) + self.land2[second, lr, :].astype(F32)
        for k in range(4):
            self.copy1(k).wait_send()
        for j in range(6):
            self.copy2(j).wait_send()


def _geom(geom):
    if isinstance(geom, tuple):
        w, row0, rows = geom
        assert w == W_UP
        return w, row0, (rows, SHARD[w][1])
    return geom, 0, SHARD[geom]


N_SCATTER_SCRATCH = 10


def _scatter_out_shapes(geoms):
    return tuple(jax.ShapeDtypeStruct(_geom(g)[2], F32) for g in geoms)


def _scatter_out_specs(geoms):
    return (pl.BlockSpec(memory_space=pltpu.VMEM),) * len(geoms)


def _scatter_scratch(geoms):
    out = []
    for g in geoms:
        s = _geom(g)[2]
        hs = (s[0] // 2, s[1])
        out += [pltpu.VMEM((4,) + s, BF16), pltpu.VMEM((4,) + s, BF16), pltpu.VMEM((3,) + s, BF16), pltpu.VMEM((6,) + hs, BF16),
                pltpu.VMEM((2,) + hs, BF16),
                pltpu.SemaphoreType.DMA((4,)), pltpu.SemaphoreType.DMA((4,)), pltpu.SemaphoreType.DMA((6,)),
                pltpu.SemaphoreType.DMA((6,)), pltpu.SemaphoreType.DMA((4,))]
    return out


class _Scatters:
    def __init__(self, geoms, p_refs, out_refs, scratch):
        k = N_SCATTER_SCRATCH
        self.items = [_Scatter(g, p_refs[i], out_refs[i], *scratch[k * i:k * i + k]) for i, g in enumerate(geoms)]

    def phase1(self, diagonal=False):
        meet = _Meet(diagonal)
        meet.signal()
        for s in self.items:
            s.load_mine()
        meet.wait()
        for s in self.items:
            s.phase1()

    def phase2(self):
        for k in (3, 1, 2, 0):
            for s in self.items:
                s.phase2(k)

    def phase2b(self):
        for s in self.items:
            s.phase2b()

    def phase3(self):
        for s in self.items:
            s.phase3()


PACK_W = 1024


SMALL_FULL = [(2, 8, PACK_W), (HEADS, CHUNK, CHUNK), (2, 8, D_FF)]
SMALL_HALF = [(s[0] // 2,) + s[1:] for s in SMALL_FULL]
N_SMALL_SCRATCH = 16


def _small_scratch():
    n_a = len(SMALL_FULL)
    return ([pltpu.VMEM(SMALL_FULL[0], F32)] + [pltpu.VMEM(s, F32) for s in SMALL_HALF] + [pltpu.VMEM(s, F32) for s in SMALL_HALF]
            + [pltpu.VMEM((3,) + s, F32) for s in SMALL_HALF]
            + [pltpu.SemaphoreType.DMA((n_a,)), pltpu.SemaphoreType.DMA((n_a,)), pltpu.SemaphoreType.DMA((n_a, 3)),
               pltpu.SemaphoreType.DMA((n_a, 3)), pltpu.SemaphoreType.DMA((n_a,)), pltpu.SemaphoreType.DMA((n_a,))])


class _SmallReduce:
    def __init__(self, ins, outs, scratch):
        self.ins, self.outs = ins, outs
        (self.pack, *rest) = scratch
        self.rxs, self.css, self.gs = rest[0:3], rest[3:6], rest[6:9]
        self.s1_send, self.s1_recv, self.s2_send, self.s2_recv, self.s3_send, self.s3_recv = rest[9:]
        self.x, self.y, self.c = lax.axis_index("x"), lax.axis_index("y"), lax.axis_index("c")
        self.sibling = (self.x, self.y, 1 - self.c)
        self.chips = [(1 - self.x, self.y), (self.x, 1 - self.y), (1 - self.x, 1 - self.y)]
        self.hl = [s[0] for s in SMALL_HALF]

    def half(self, ref, a, h):
        return ref.at[pl.ds(h * self.hl[a], self.hl[a])]

    def begin(self):
        dg1_ref, dg2_ref, dgf_ref, dgrn_ref, dlng_ref, dlnb_ref, dbs_ref, loss_ref, dws_ref, dcv_ref = self.ins
        pack, c = self.pack, self.c
        pack[...] = jnp.zeros_like(pack)
        pack[0, 0:1, :] = dg1_ref[0:1, :]
        pack[0, 1:2, :] = dg2_ref[0:1, :]
        pack[0, 2:3, :] = dgf_ref[0:1, :]
        pack[0, 3:4, 0:RET_W] = dgrn_ref[0:1, :]
        lsum = loss_ref[0, 0:1, :]
        for i in range(1, N_TB):
            lsum = lsum + loss_ref[i, 0:1, :]
        pack[0, 3:4, RET_W:RET_W + 128] = lsum
        pack[1, 0:HEADS, 0:128] = dlng_ref[0:HEADS, :]
        pack[1, 0:HEADS, 128:256] = dlnb_ref[0:HEADS, :]
        pack[1, 0:HEADS, 256:384] = dbs_ref[0:HEADS, :]
        self.srcs = [pack, dws_ref, dcv_ref]
        n_a = len(self.srcs)
        self.ex1 = [pltpu.make_async_remote_copy(src_ref=self.half(self.srcs[a], a, 1 - c), dst_ref=self.rxs[a],
                                                 send_sem=self.s1_send.at[a], recv_sem=self.s1_recv.at[a],
                                                 device_id=self.sibling, device_id_type=MESH) for a in range(n_a)]
        for cp in self.ex1:
            cp.start()
        self.ex2 = []
        for a in range(n_a):
            self.ex1[a].wait_recv()
            self.css[a][...] = self.half(self.srcs[a], a, c)[...] + self.rxs[a][...]
            for j, chip in enumerate(self.chips):
                cp = pltpu.make_async_remote_copy(src_ref=self.css[a], dst_ref=self.gs[a].at[j], send_sem=self.s2_send.at[a, j],
                                                  recv_sem=self.s2_recv.at[a, j], device_id=(*chip, c), device_id_type=MESH)
                cp.start()
                self.ex2.append(cp)

    def end(self):
        c, x, y = self.c, self.x, self.y
        ex3 = []
        for a in range(len(self.srcs)):
            css, gs, out = self.css[a], self.gs[a], self.outs[a]
            for j in range(3):
                self.ex2[3 * a + j].wait_recv()
            tot = None
            for q in range(4):
                k = jnp.where(x != (q >> 1), 1, 0) + jnp.where(y != (q & 1), 2, 0)
                term = jnp.where(k == 0, css[...], jnp.where(k == 1, gs[0], jnp.where(k == 2, gs[1], gs[2])))
                tot = term if tot is None else tot + term
            self.half(out, a, c)[...] = tot
            cp = pltpu.make_async_remote_copy(src_ref=self.half(out, a, c), dst_ref=self.half(out, a, c), send_sem=self.s3_send.at[a],
                                              recv_sem=self.s3_recv.at[a], device_id=self.sibling, device_id_type=MESH)
            cp.start()
            ex3.append(cp)
        for a in range(len(self.srcs)):
            out = self.outs[a]
            pltpu.make_async_remote_copy(src_ref=self.half(out, a, 1 - c), dst_ref=self.half(out, a, 1 - c), send_sem=self.s3_send.at[a],
                                         recv_sem=self.s3_recv.at[a], device_id=self.sibling, device_id_type=MESH).wait_recv()
        for cp in self.ex1 + self.ex2 + ex3:
            cp.wait_send()


def _adam_math(w, g, m, v):
    nm = ADAM_B1 * m + (1.0 - ADAM_B1) * g
    nv = ADAM_B2 * v + (1.0 - ADAM_B2) * (g * g)
    d = -ADAM_LR * ((nm / (1.0 - ADAM_B1 ** ADAM_STEP)) / (jnp.sqrt(nv / (1.0 - ADAM_B2 ** ADAM_STEP)) + ADAM_EPS) + ADAM_WD * w)
    return d, nm, nv


def _adamw(params, thru, n_steps):
    plan = []
    for w, gs, _, _ in params:
        _, r, cdim = w.shape
        if len(gs) == 1:
            edges = [0, r // n_steps]
            spec3 = pl.BlockSpec((1, r // n_steps, cdim), lambda i: (0, i, 0))
            g_specs = [pl.BlockSpec((r // n_steps, cdim), lambda i: (i, 0))]
        else:
            edges = [sum(g.shape[0] for g in gs[:k]) for k in range(len(gs) + 1)]
            spec3 = pl.BlockSpec((1, r, cdim // n_steps), lambda i: (0, 0, i))
            g_specs = [pl.BlockSpec((g.shape[0], cdim // n_steps), lambda i: (0, i)) for g in gs]
        plan.append((len(gs), edges, spec3, g_specs))
    n_in = sum(n_g + 3 for n_g, _, _, _ in plan)
    n_t = len(thru)

    def body(*refs):
        ins, outs = refs[:n_in], refs[n_in + n_t:]
        for n_g, edges, _, _ in plan:
            (w_ref, *g_refs, m_ref, v_ref), ins = ins[:n_g + 3], ins[n_g + 3:]
            (go_ref, d_ref, nm_ref, nv_ref), outs = outs[:4], outs[4:]
            for g_ref, lo, hi in zip(g_refs, edges[:-1], edges[1:]):
                gg = g_ref[...]
                go_ref[0, lo:hi, :] = gg
                d_ref[0, lo:hi, :], nm_ref[0, lo:hi, :], nv_ref[0, lo:hi, :] = _adam_math(
                    w_ref[0, lo:hi, :], gg, m_ref[0, lo:hi, :], v_ref[0, lo:hi, :])
        for t_ref, to_ref in zip(refs[n_in:n_in + n_t], outs):
            to_ref[...] = t_ref[...]

    t_specs = [pl.BlockSpec((t.shape[0] // n_steps, t.shape[1]), lambda i: (i, 0)) for t in thru]
    in_specs, out_specs, out_shape, args = [], [], [], []
    for (w, gs, m, v), (_, _, spec3, g_specs) in zip(params, plan):
        in_specs += [spec3] + g_specs + [spec3, spec3]
        out_specs += [spec3] * 4
        out_shape += [jax.ShapeDtypeStruct(w.shape, F32)] * 4
        args += [w, *gs, m, v]
    res = pl.pallas_call(
        body, name="adamw", grid=(n_steps,), out_shape=tuple(out_shape) + tuple(jax.ShapeDtypeStruct(t.shape, t.dtype) for t in thru),
        in_specs=in_specs + t_specs, out_specs=tuple(out_specs) + tuple(t_specs),
        compiler_params=_cparams(("parallel",)),
    )(*args, *thru)
    return [res[4 * k:4 * k + 4] for k in range(len(params))], res[4 * len(params):]


def _adamw_small(rp, rws, rcv, gcw, params):
    n_p = len(params)

    def body(*refs):
        rp_ref, rws_ref, rcv_ref, gcw_ref = refs[:4]
        ins = refs[4:4 + 3 * n_p]
        outs = refs[4 + 3 * n_p:]
        outs[4 * n_p][...] = rp_ref[0, 3:4, RET_W:RET_W + 1]
        grads = [rp_ref[0, 0:1, :], rp_ref[0, 1:2, :], rp_ref[0, 2:3, :], rp_ref[0, 3:4, 0:RET_W],
                 rp_ref[1, 0:HEADS, 0:128], rp_ref[1, 0:HEADS, 128:256], rp_ref[1, 0:HEADS, 256:384],
                 rws_ref[...], gcw_ref[...], None]
        for p in range(n_p):
            w_ref, m_ref, v_ref = ins[3 * p:3 * p + 3]
            o = outs[4 * p:4 * p + 4]
            if p == n_p - 1:
                for hf in range(2):
                    cs = slice(hf * D_FF, (hf + 1) * D_FF)
                    g = rcv_ref[hf, 3:4, :]
                    res = (g,) + _adam_math(w_ref[:, cs], g, m_ref[:, cs], v_ref[:, cs])
                    for t in range(4):
                        o[t][:, cs] = res[t]
                continue
            lead = w_ref.ndim > grads[p].ndim
            rd = (lambda r: r[0]) if lead else (lambda r: r[...])
            res = (grads[p],) + _adam_math(rd(w_ref), grads[p], rd(m_ref), rd(v_ref))
            for t in range(4):
                if lead:
                    o[t][0] = res[t]
                else:
                    o[t][...] = res[t]

    vm = pl.BlockSpec(memory_space=pltpu.VMEM)
    flat = [a for tr in params for a in tr]
    out_shape = tuple(jax.ShapeDtypeStruct(tr[0].shape, F32) for tr in params for _ in range(4)) + (jax.ShapeDtypeStruct((1, 1), F32),)
    res = pl.pallas_call(
        body, name="adamw_small", out_shape=out_shape, in_specs=[vm] * (4 + len(flat)), out_specs=(vm,) * len(out_shape),
        compiler_params=_cparams(),
    )(rp, rws, rcv, gcw, *flat)
    return [res[4 * p:4 * p + 4] for p in range(n_p)], res[4 * n_p]


def kernel(x, mix_norm_g, w_in, ret_norm_g, sgu_ln_g, sgu_ln_b, sgu_w_s, sgu_b_s, w_out, ffn_norm_g, w_up, conv_w, conv_b, w_down, final_norm_g, loss_target, m_mix_norm_g, m_w_in, m_ret_norm_g, m_sgu_ln_g, m_sgu_ln_b, m_sgu_w_s, m_sgu_b_s, m_w_out, m_ffn_norm_g, m_w_up, m_conv_w, m_conv_b, m_w_down, m_final_norm_g, v_mix_norm_g, v_w_in, v_ret_norm_g, v_sgu_ln_g, v_sgu_ln_b, v_sgu_w_s, v_sgu_b_s, v_w_out, v_ffn_norm_g, v_w_up, v_conv_w, v_conv_b, v_w_down, v_final_norm_g):
    xs = x[0]
    tgt = loss_target[0]
    grn = ret_norm_g.reshape(1, RET_W)
    lng = sgu_ln_g.reshape(1, SGU_W)
    lnb = sgu_ln_b.reshape(1, SGU_W)
    ws = sgu_w_s[0]
    bsb = jnp.broadcast_to(sgu_b_s[0][:, :, None], (HEADS, CHUNK, HEAD_DIM))
    gf = final_norm_g.reshape(1, D_MODEL)
    me = 4 * lax.axis_index("x") + 2 * lax.axis_index("y") + lax.axis_index("c")
    tr = lambda a: jnp.transpose(a[0])[None]
    tr_cw = lambda a: jnp.transpose(a, (1, 0, 2))

    proj, h1, cos2, sin2, win_g, cw_sh, wout_g, wdn_g, su = _fwd_proj(
        xs, mix_norm_g, _rope_freq(), w_in[0], w_out[0], tr(w_up)[0], w_down[0], tr_cw(conv_w))
    cw_g = jnp.transpose(cw_sh, (1, 0, 2)).reshape(8, 2 * D_FF)
    x2, mixcat, o, sprev, wup_g = _fwd_mix(xs, proj, wout_g, grn, lng, lnb, ws, bsb, su)
    h2, up_pre, u_conv, act, x3, loss_parts = _fwd_ffn(x2, ffn_norm_g, wup_g, cw_g, conv_b, wdn_g, gf, tgt)

    dx3, dpre, dx2, dgf, dg2, dcv = _bwd_ffn(x3, tgt, gf, x2, ffn_norm_g, up_pre, u_conv, wup_g, cw_g, wdn_g)
    band = 512
    (gdn_p,) = _wgrad("wgrad_down", act, dx3, tm=FF_TILE)
    (gout_p,) = _wgrad("wgrad_out", mixcat, dx2, tn=512)
    gup_p, g_dn = _wgrad("wgrad_up", dpre, h2, tm=FF_TILE, tn=512, hosted=[(W_DOWN, gdn_p)])
    dproj, dgrn, dlng, dlnb, dws, dbs, g_up_a, g_out = _bwd_mix(
        dx2, proj, o, sprev, wout_g, grn, lng, lnb, ws, bsb, cos2, sin2,
        [((W_UP, 0, band), gup_p), (W_OUT, gout_p)])
    gin_p, g_up_b = _wgrad("wgrad_in", h1, dproj, tm=512, tn=768, hosted=[((W_UP, band, FF_SHARD - band), gup_p)])
    grad_x, g_in, rp, rws, rcv = _bwd_proj(dproj, win_g, xs, mix_norm_g, dx2, gin_p,
                                           (dg2, dgf, dgrn, dlng, dlnb, dbs, loss_parts, dws, dcv))
    gcw = tr_cw(lax.dynamic_slice(rcv, (me // (N_DEV // 2), 0, (me % (N_DEV // 2)) * FF_SHARD), (1, 3, FF_SHARD)))

    table = {}
    big, (grad_x,) = _adamw([(w_in, [g_in], m_w_in, v_w_in), (w_out, [g_out], m_w_out, v_w_out),
                             (tr(w_up), [g_up_a, g_up_b], tr(m_w_up), tr(v_w_up)), (w_down, [g_dn], m_w_down, v_w_down)],
                            [grad_x], n_steps=4)
    table.update(zip(("w_in", "w_out", "w_up", "w_down"), big))
    table["w_up"] = tuple(tr(a) for a in table["w_up"])
    row = lambda a: a.reshape(1, D_MODEL)
    names_small = ["mix_norm_g", "ffn_norm_g", "final_norm_g", "ret_norm_g", "sgu_ln_g", "sgu_ln_b", "sgu_b_s", "sgu_w_s",
                   "conv_w", "conv_b"]
    params = [(mix_norm_g, m_mix_norm_g, v_mix_norm_g), (ffn_norm_g, m_ffn_norm_g, v_ffn_norm_g),
              (row(final_norm_g), row(m_final_norm_g), row(v_final_norm_g)), (ret_norm_g, m_ret_norm_g, v_ret_norm_g),
              (sgu_ln_g, m_sgu_ln_g, v_sgu_ln_g), (sgu_ln_b, m_sgu_ln_b, v_sgu_ln_b), (sgu_b_s, m_sgu_b_s, v_sgu_b_s),
              (sgu_w_s, m_sgu_w_s, v_sgu_w_s), (tr_cw(conv_w), tr_cw(m_conv_w), tr_cw(v_conv_w)), (conv_b, m_conv_b, v_conv_b)]
    small, loss = _adamw_small(rp, rws, rcv, gcw, params)
    for n, res in zip(names_small, small):
        table[n] = res
    table["final_norm_g"] = tuple(a.reshape(D_MODEL) for a in table["final_norm_g"])
    table["conv_w"] = tuple(tr_cw(a) for a in table["conv_w"])

    order = ["mix_norm_g", "w_in", "ret_norm_g", "sgu_ln_g", "sgu_ln_b", "sgu_w_s", "sgu_b_s", "w_out", "ffn_norm_g", "w_up",
             "conv_w", "conv_b", "w_down", "final_norm_g"]
    outs = [loss.reshape(()), grad_x[None]]
    for col in range(4):
        outs += [table[n][col] for n in order]
    return tuple(outs)
```

```python
import functools
import math

import jax
import jax.numpy as jnp
import numpy as np
from jax import lax
from jax.experimental import pallas as pl
from jax.experimental.pallas import tpu as pltpu

F32 = jnp.float32
BF16 = jnp.bfloat16
MESH = pl.DeviceIdType.MESH

N_DEV = 8
SEQ = 2048
D_MODEL = 1024
CHUNK = 128
N_CHUNK = SEQ // CHUNK
HEADS = 4
HEAD_DIM = 128
RET_W = 512
SGU_W = 512
PROJ_W = 3072
D_FF = 2816
FF_SHARD = 704
FF_TILE = 1408
FF_TILES = ((0, 1536), (1536, 1280))
IN_SHARD = PROJ_W // N_DEV
OUT_SHARD = D_MODEL // N_DEV
DOWN_SHARD = D_FF // N_DEV
TM = 256
N_TB = SEQ // TM
FWD_PROJ_PASS_AT = 5
FWD_MIX_PASS_AT = 10
EPS = 1e-6
ROPE_BASE = 10000.0
K_SCALE = HEAD_DIM ** -0.5
INV_SQRT2 = 0.7071067811865476
INV_SQRT_2PI = 0.3989422804014327

ADAM_LR = 0.001
ADAM_B1 = 0.9
ADAM_B2 = 0.999
ADAM_EPS = 1e-08
ADAM_WD = 0.01
ADAM_STEP = 10

VMEM_LIMIT = 56 * 1024 * 1024


def _cparams(sem=None, vmem=VMEM_LIMIT, collective=None):
    return pltpu.CompilerParams(dimension_semantics=sem, vmem_limit_bytes=vmem, collective_id=collective)


COLLECTIVE = {name: k for k, name in enumerate(("fwd_proj", "fwd_mix", "wgrad_up", "bwd_mix", "wgrad_in", "bwd_proj"))}


class _Meet:
    def __init__(self, diagonal):
        x, y, c = lax.axis_index("x"), lax.axis_index("y"), lax.axis_index("c")
        self.peers = [(x, y, 1 - c), (1 - x, y, c), (x, 1 - y, c)] + ([(1 - x, 1 - y, c)] if diagonal else [])

    def signal(self):
        for peer in self.peers:
            pl.semaphore_signal(pltpu.get_barrier_semaphore(), inc=1, device_id=peer, device_id_type=MESH)

    def wait(self):
        pl.semaphore_wait(pltpu.get_barrier_semaphore(), len(self.peers))


def _resident(shape):
    nd = len(shape)
    return pl.BlockSpec(shape, lambda *_: (0,) * nd, pipeline_mode=pl.Buffered(1))


def _dot(a, b):
    return jnp.dot(a, b, preferred_element_type=F32)


def _dot_nt(a, b):
    return lax.dot_general(a, b, (((1,), (1,)), ((), ())), preferred_element_type=F32)


def _dot_tn(a, b):
    return lax.dot_general(a, b, (((0,), (0,)), ((), ())), preferred_element_type=F32)


def _sigmoid(x):
    return 1.0 / (1.0 + jnp.exp(-x))


def _gelu(x):
    return 0.5 * x * (1.0 + lax.erf(x * INV_SQRT2))


def _gelu_grad(x):
    return 0.5 * (1.0 + lax.erf(x * INV_SQRT2)) + x * (jnp.exp(-0.5 * x * x) * INV_SQRT_2PI)


def _rot(xh, cos2, sin2):
    return xh * cos2 + pltpu.roll(xh, HEAD_DIM // 2, 1) * sin2


def _rot_t(dh, cos2, sin2):
    return dh * cos2 + pltpu.roll(dh * sin2, HEAD_DIM // 2, 1)


def _rope_freq():
    half = HEAD_DIM // 2
    inv_freq = jnp.power(ROPE_BASE, -jnp.arange(half, dtype=F32) / half)
    return jnp.concatenate([inv_freq, inv_freq])[None, :]


def _rope_block(inv2, first_row):
    pos = (lax.broadcasted_iota(jnp.int32, (TM, HEAD_DIM), 0) + first_row).astype(F32)
    ang = pos * inv2
    sin = jnp.sin(ang)
    lane = lax.broadcasted_iota(jnp.int32, (TM, HEAD_DIM), 1)
    return jnp.cos(ang), jnp.where(lane < HEAD_DIM // 2, -sin, sin)


def _log_gamma():
    return np.log(np.float32(1.0) - np.power(np.float32(2.0), -5.0 - np.arange(HEADS, dtype=np.float32))).astype(np.float32)


def _fill_decay(mask_ref, qd_ref, kd_ref):
    assert HEAD_DIM == CHUNK
    lg = _log_gamma()
    t = lax.broadcasted_iota(jnp.int32, (CHUNK, CHUNK), 0).astype(F32)
    diff = t - lax.broadcasted_iota(jnp.int32, (CHUNK, CHUNK), 1).astype(F32)
    for h in range(HEADS):
        mask_ref[h] = jnp.where(diff >= 0.0, jnp.exp(float(lg[h]) * jnp.maximum(diff, 0.0)), 0.0)
        qd_ref[h] = jnp.exp(float(lg[h]) * (t + 1.0))
        kd_ref[h] = jnp.exp(float(lg[h]) * (CHUNK - 1.0 - t))


def _chunk_decay():
    lg = _log_gamma()
    return [float(np.exp(lg[h] * np.float32(CHUNK))) for h in range(HEADS)]


W_IN, W_OUT, W_UP, W_DOWN, W_CONV = range(5)
GATHERED = {W_IN: ((D_MODEL, PROJ_W), BF16), W_OUT: ((D_MODEL, D_MODEL), BF16), W_UP: ((2 * D_FF, D_MODEL), BF16),
            W_DOWN: ((D_FF, D_MODEL), BF16), W_CONV: ((N_DEV, 8, FF_SHARD), F32)}
SHARD = {W_IN: (D_MODEL, IN_SHARD), W_OUT: (OUT_SHARD, D_MODEL), W_UP: (FF_SHARD, D_MODEL), W_DOWN: (DOWN_SHARD, D_MODEL),
         W_CONV: (8, FF_SHARD)}


class _Gather:
    N_SEMS = 9

    def __init__(self, ids, stages, gathered, send_sems, recv_sems, local_sems):
        self.ids, self.stages, self.gathered = ids, stages, gathered
        self.send_sems, self.recv_sems, self.local_sems = send_sems, recv_sems, local_sems
        self.x, self.y, self.c = lax.axis_index("x"), lax.axis_index("y"), lax.axis_index("c")
        self.me = (self.x, self.y, self.c)
        self.sibling = (self.x, self.y, 1 - self.c)
        self.chips = [(1 - self.x, self.y), (self.x, 1 - self.y), (1 - self.x, 1 - self.y)]

    def slot(self, n, px, py, pc):
        dev = 4 * px + 2 * py + pc
        w, g = self.ids[n], self.gathered[n]
        if w == W_IN:
            return g.at[:, pl.ds(pl.multiple_of(dev * IN_SHARD, 128), IN_SHARD)]
        if w == W_OUT:
            return g.at[pl.ds(pl.multiple_of(dev * OUT_SHARD, 128), OUT_SHARD), :]
        if w == W_DOWN:
            return g.at[pl.ds(pl.multiple_of(dev * DOWN_SHARD, 32), DOWN_SHARD), :]
        if w == W_UP:
            return g.at[pl.ds(pl.multiple_of(dev * FF_SHARD, 32), FF_SHARD), :]
        return g.at[dev]

    def half(self, n, px, py, pc, h):
        dev = 4 * px + 2 * py + pc
        w, g = self.ids[n], self.gathered[n]
        if w == W_IN:
            return g.at[pl.ds(h * (D_MODEL // 2), D_MODEL // 2), pl.ds(pl.multiple_of(dev * IN_SHARD, 128), IN_SHARD)]
        rows = SHARD[w][0] // 2
        return g.at[pl.ds(pl.multiple_of(dev * SHARD[w][0] + h * rows, 16), rows), :]

    def tree(self, n):
        return self.ids[n] != W_CONV

    def copy(self, n, k, block, to, src=None, h=None):
        ref = self.slot(n, *block) if h is None else self.half(n, *block, h)
        return pltpu.make_async_remote_copy(
            src_ref=ref if src is None else src, dst_ref=ref,
            send_sem=self.send_sems.at[n, k], recv_sem=self.recv_sems.at[n, k], device_id=to, device_id_type=MESH)

    def _mine(self):
        return [pltpu.make_async_copy(self.stages[n], self.slot(n, *self.me), self.local_sems.at[n]) for n in range(len(self.ids))]

    def _first(self):
        out = []
        for n in range(len(self.ids)):
            out.append(self.copy(n, 0, self.me, self.sibling, src=self.stages[n]))
            out += [self.copy(n, 1 + j, self.me, (*chip, self.c), src=self.stages[n])
                    for j, chip in enumerate(self.chips[:2] if self.tree(n) else self.chips)]
        return out

    def start(self):
        for cp in self._mine() + self._first():
            cp.start()

    def _passed(self, j):
        dev = (*self.chips[j], self.c)
        out = []
        for n in range(len(self.ids)):
            if not self.tree(n):
                out.append(self.copy(n, 4 + j, dev, self.sibling))
            elif j < 2:
                out += [self.copy(n, 3 + j, dev, (*self.chips[1 - j], self.c), h=j), self.copy(n, 5 + j, dev, self.sibling)]
            else:
                out += [self.copy(n, 7, dev, self.sibling, h=0), self.copy(n, 8, dev, self.sibling, h=1)]
        return out

    def near(self):
        for j in range(2):
            dev = (*self.chips[j], self.c)
            for n in range(len(self.ids)):
                self.copy(n, 1 + j, dev, self.me).wait_recv()
            for cp in self._passed(j):
                cp.start()

    def finish(self):
        dev = (*self.chips[2], self.c)
        for n in range(len(self.ids)):
            if self.tree(n):
                self.copy(n, 3, dev, self.me, h=0).wait_recv()
                self.copy(n, 4, dev, self.me, h=1).wait_recv()
            else:
                self.copy(n, 3, dev, self.me).wait_recv()
        for cp in self._passed(2):
            cp.start()
        for n in range(len(self.ids)):
            self.copy(n, 0, self.sibling, self.me).wait_recv()
            for j, chip in enumerate(self.chips):
                dev = (*chip, 1 - self.c)
                if not self.tree(n):
                    self.copy(n, 4 + j, dev, self.me).wait_recv()
                elif j < 2:
                    self.copy(n, 5 + j, dev, self.me).wait_recv()
                else:
                    self.copy(n, 7, dev, self.me, h=0).wait_recv()
                    self.copy(n, 8, dev, self.me, h=1).wait_recv()
        for cp in self._mine():
            cp.wait()
        for cp in self._first() + self._passed(0) + self._passed(1) + self._passed(2):
            cp.wait_send()


def _gather_scratch(n):
    return [pltpu.SemaphoreType.DMA((n, _Gather.N_SEMS)), pltpu.SemaphoreType.DMA((n, _Gather.N_SEMS)), pltpu.SemaphoreType.DMA((n,))]


def _gathered_shapes(ids):
    return tuple(jax.ShapeDtypeStruct(*GATHERED[w]) for w in ids)


def _fwd_proj(x, g1, inv2, w_in, w_out, w_up, w_down, conv_w):
    ids_a, ids_b = [W_IN, W_CONV], [W_OUT, W_DOWN]

    def body(x_ref, g_ref, inv_ref, in_hbm, out_hbm, up_hbm, dn_hbm, cw_ref,
             proj_ref, h1_ref, cos_ref, sin_ref, gin, gcw, gout, gdn, su_ref,
             w_vm, s_in, s_cw, s_out, s_dn, f_in, f_out, f_up, f_dn, ld_sems,
             a_send, a_recv, a_local, b_send, b_recv, b_local):
        ag_a = _Gather(ids_a, [s_in, s_cw], [gin, gcw], a_send, a_recv, a_local)
        ag_b = _Gather(ids_b, [s_out, s_dn], [gout, gdn], b_send, b_recv, b_local)

        @pl.when(pl.program_id(0) == 0)
        def _():
            meet = _Meet(diagonal=True)
            meet.signal()
            loads = [pltpu.make_async_copy(src, dst, ld_sems.at[i])
                     for i, (src, dst) in enumerate(((in_hbm, f_in), (out_hbm, f_out), (dn_hbm, f_dn), (up_hbm, f_up)))]
            for cp in loads:
                cp.start()
            s_cw[...] = jnp.zeros_like(s_cw)
            for k in range(3):
                s_cw[k:k + 1, :] = cw_ref[k]
            loads[0].wait()
            s_in[...] = f_in[...].astype(BF16)
            meet.wait()
            ag_a.start()
            loads[1].wait()
            s_out[...] = f_out[...].astype(BF16)
            loads[2].wait()
            s_dn[...] = f_dn[...].astype(BF16)
            ag_a.near()
            ag_b.start()
            loads[3].wait()
            su_ref[...] = f_up[...].astype(BF16)
            ag_a.finish()
            fill = pltpu.make_async_copy(gin, w_vm, ld_sems.at[4])
            fill.start()
            fill.wait()

        pl.when(pl.program_id(0) == FWD_PROJ_PASS_AT)(ag_b.near)

        xb = x_ref[...]
        r = lax.rsqrt(jnp.mean(xb * xb, axis=-1, keepdims=True) + EPS)
        h = ((xb * r) * g_ref[...]).astype(BF16)
        h1_ref[...] = h
        p = _dot(h, w_vm[...])
        c2, s2 = _rope_block(inv_ref[...], pl.program_id(0) * TM)
        cos_ref[...], sin_ref[...] = c2, s2
        for hd in range(HEADS):
            sl = slice(hd * HEAD_DIM, (hd + 1) * HEAD_DIM)
            proj_ref[:, sl] = _rot(p[:, sl], c2, s2)
            ks = slice(RET_W + hd * HEAD_DIM, RET_W + (hd + 1) * HEAD_DIM)
            proj_ref[:, ks] = _rot(p[:, ks], c2, s2) * K_SCALE
        proj_ref[:, 2 * RET_W:] = p[:, 2 * RET_W:]

        pl.when(pl.program_id(0) == N_TB - 1)(ag_b.finish)

    tok = lambda w: pl.BlockSpec((TM, w), lambda i: (i, 0))
    hbm = pl.BlockSpec(memory_space=pl.ANY)
    vm = pl.BlockSpec(memory_space=pltpu.VMEM)
    return pl.pallas_call(
        body, name="fwd_proj", grid=(N_TB,),
        out_shape=(jax.ShapeDtypeStruct((SEQ, PROJ_W), F32), jax.ShapeDtypeStruct((SEQ, D_MODEL), BF16),
                   jax.ShapeDtypeStruct((SEQ, HEAD_DIM), F32), jax.ShapeDtypeStruct((SEQ, HEAD_DIM), F32))
        + _gathered_shapes(ids_a + ids_b) + (jax.ShapeDtypeStruct(SHARD[W_UP], BF16),),
        in_specs=[tok(D_MODEL), _resident((1, D_MODEL)), _resident((1, HEAD_DIM)), hbm, hbm, hbm, hbm, vm],
        out_specs=(tok(PROJ_W), tok(D_MODEL), tok(HEAD_DIM), tok(HEAD_DIM), hbm, hbm, hbm, hbm, vm),
        scratch_shapes=[pltpu.VMEM((D_MODEL, PROJ_W), BF16), pltpu.VMEM(SHARD[W_IN], BF16), pltpu.VMEM(SHARD[W_CONV], F32),
                        pltpu.VMEM(SHARD[W_OUT], BF16), pltpu.VMEM(SHARD[W_DOWN], BF16),
                        pltpu.VMEM(SHARD[W_IN], F32), pltpu.VMEM(SHARD[W_OUT], F32), pltpu.VMEM(SHARD[W_UP], F32),
                        pltpu.VMEM(SHARD[W_DOWN], F32), pltpu.SemaphoreType.DMA((5,))]
        + _gather_scratch(len(ids_a)) + _gather_scratch(len(ids_b)),
        compiler_params=_cparams(("arbitrary",), collective=COLLECTIVE["fwd_proj"]),
    )(x, g1, inv2, w_in, w_out, w_up, w_down, conv_w)


def _causal(w):
    r = lax.broadcasted_iota(jnp.int32, (CHUNK, CHUNK), 0)
    c = lax.broadcasted_iota(jnp.int32, (CHUNK, CHUNK), 1)
    return jnp.where(r >= c, w, 0.0)


def _fwd_mix(x, proj, wout_g, grn, lng, lnb, ws, bsb, su):
    cdec = _chunk_decay()
    ids = [W_UP]

    def body(x_ref, p_ref, w_ref, grn_ref, lng_ref, lnb_ref, ws_ref, bsb_ref, su_ref,
             x2_ref, cat_ref, o_ref, sp_ref, gup, state, m_ref, qd_ref, kd_ref, send_sems, recv_sems, local_sems):
        ag = _Gather(ids, [su_ref], [gup], send_sems, recv_sems, local_sems)

        @pl.when(pl.program_id(0) == 0)
        def _():
            meet = _Meet(diagonal=False)
            meet.signal()
            state[...] = jnp.zeros_like(state)
            _fill_decay(m_ref, qd_ref, kd_ref)
            meet.wait()
            ag.start()

        for h in range(HEADS):
            sl = slice(h * HEAD_DIM, (h + 1) * HEAD_DIM)
            q = p_ref[:, sl]
            k = p_ref[:, RET_W + h * HEAD_DIM:RET_W + (h + 1) * HEAD_DIM]
            v = p_ref[:, 2 * RET_W + h * HEAD_DIM:2 * RET_W + (h + 1) * HEAD_DIM]
            g = p_ref[:, 3 * RET_W + h * HEAD_DIM:3 * RET_W + (h + 1) * HEAD_DIM]
            qb, kb, vb = q.astype(BF16), k.astype(BF16), v.astype(BF16)
            a = _dot_nt(qb, kb) * m_ref[h]
            spb = state[h].astype(BF16)
            sp_ref[0, h] = spb
            o = _dot(a.astype(BF16), vb) + _dot((q * qd_ref[h]).astype(BF16), spb)
            state[h] = state[h] * cdec[h] + _dot_tn((k * kd_ref[h]).astype(BF16), vb)
            o_ref[:, sl] = o
            rinv = lax.rsqrt(jnp.mean(o * o, axis=-1, keepdims=True) + EPS)
            rn = (o * rinv) * grn_ref[:, sl]
            cat_ref[:, sl] = ((g * _sigmoid(g)) * rn).astype(BF16)
        for gi in range(HEADS):
            sl = slice(gi * HEAD_DIM, (gi + 1) * HEAD_DIM)
            u = p_ref[:, 4 * RET_W + gi * HEAD_DIM:4 * RET_W + (gi + 1) * HEAD_DIM]
            sv = p_ref[:, 4 * RET_W + SGU_W + gi * HEAD_DIM:4 * RET_W + SGU_W + (gi + 1) * HEAD_DIM]
            gv = _gelu(sv)
            xc = gv - jnp.mean(gv, axis=-1, keepdims=True)
            vn = (xc * lax.rsqrt(jnp.mean(xc * xc, axis=-1, keepdims=True) + EPS)) * lng_ref[:, sl] + lnb_ref[:, sl]
            mixed = _dot(_causal(ws_ref[gi]).astype(BF16), vn.astype(BF16)) + bsb_ref[gi]
            cat_ref[:, RET_W + gi * HEAD_DIM:RET_W + (gi + 1) * HEAD_DIM] = (_gelu(u) * mixed).astype(BF16)
        x2_ref[...] = x_ref[...] + _dot(cat_ref[...], w_ref[...])

        pl.when(pl.program_id(0) == FWD_MIX_PASS_AT)(ag.near)
        pl.when(pl.program_id(0) == N_CHUNK - 1)(ag.finish)

    ch = lambda w: pl.BlockSpec((CHUNK, w), lambda i: (i, 0))
    hcc = (HEADS, CHUNK, CHUNK)
    hbm = pl.BlockSpec(memory_space=pl.ANY)
    return pl.pallas_call(
        body, name="fwd_mix", grid=(N_CHUNK,),
        out_shape=(jax.ShapeDtypeStruct((SEQ, D_MODEL), F32), jax.ShapeDtypeStruct((SEQ, D_MODEL), BF16),
                   jax.ShapeDtypeStruct((SEQ, RET_W), F32), jax.ShapeDtypeStruct((N_CHUNK, HEADS, HEAD_DIM, HEAD_DIM), BF16))
        + _gathered_shapes(ids),
        in_specs=[ch(D_MODEL), ch(PROJ_W), _resident((D_MODEL, D_MODEL)), _resident((1, RET_W)), _resident((1, SGU_W)),
                  _resident((1, SGU_W)), _resident(hcc), _resident(hcc), hbm],
        out_specs=(ch(D_MODEL), ch(D_MODEL), ch(RET_W), pl.BlockSpec((1, HEADS, HEAD_DIM, HEAD_DIM), lambda i: (i, 0, 0, 0)), hbm),
        scratch_shapes=[pltpu.VMEM((HEADS, HEAD_DIM, HEAD_DIM), F32)] + [pltpu.VMEM(hcc, F32)] * 3 + _gather_scratch(len(ids)),
        compiler_params=_cparams(("arbitrary",), collective=COLLECTIVE["fwd_mix"]),
    )(x, proj, wout_g, grn, lng, lnb, ws, bsb, su)


def _conv_taps(p, prev8):
    row = lax.broadcasted_iota(jnp.int32, p.shape, 0)
    p1 = jnp.where(row == 0, prev8[7:8, :], pltpu.roll(p, 1, 0))
    p2 = jnp.where(row == 0, prev8[6:7, :], jnp.where(row == 1, prev8[7:8, :], pltpu.roll(p, 2, 0)))
    return p1, p2


def _fwd_ffn(x2, g2, wup_g, cw_g, cb_g, wdn_g, gf, tgt):
    def body(x_ref, g_ref, wu_ref, cw_ref, cb_ref, wd_ref, gf_ref, t_ref, h2_ref, up_ref, u_ref, act_ref, x3_ref, loss_ref, carry):
        @pl.when(pl.program_id(0) == 0)
        def _():
            carry[...] = jnp.zeros_like(carry)

        xb = x_ref[...]
        r = lax.rsqrt(jnp.mean(xb * xb, axis=-1, keepdims=True) + EPS)
        h = ((xb * r) * g_ref[...]).astype(BF16)
        h2_ref[...] = h
        acc = xb
        for t0, tw in FF_TILES:
            u = []
            for c0 in (t0, D_FF + t0):
                cs = slice(c0, c0 + tw)
                p = _dot_nt(h, wu_ref[pl.ds(c0, tw), :])
                up_ref[:, cs] = p.astype(BF16)
                p1, p2 = _conv_taps(p, carry[:, cs])
                carry[:, cs] = p[TM - 8:, :]
                us = p2 * cw_ref[0:1, cs] + p1 * cw_ref[1:2, cs] + p * cw_ref[2:3, cs] + cb_ref[:, cs]
                u_ref[:, cs] = us.astype(BF16)
                u.append(us)
            a = ((u[0] * _sigmoid(u[0])) * u[1]).astype(BF16)
            act_ref[:, t0:t0 + tw] = a
            acc = acc + _dot(a, wd_ref[pl.ds(t0, tw), :])
        x3_ref[...] = acc
        r3 = lax.rsqrt(jnp.mean(acc * acc, axis=-1, keepdims=True) + EPS)
        diff = (acc * r3) * gf_ref[...] - t_ref[...]
        loss_ref[...] = jnp.full(loss_ref.shape, 0.5 * jnp.sum(jnp.mean(diff * diff, axis=-1)), F32)

    tok = lambda w: pl.BlockSpec((TM, w), lambda i: (i, 0))
    return pl.pallas_call(
        body, name="fwd_ffn", grid=(N_TB,),
        out_shape=(jax.ShapeDtypeStruct((SEQ, D_MODEL), BF16), jax.ShapeDtypeStruct((SEQ, 2 * D_FF), BF16),
                   jax.ShapeDtypeStruct((SEQ, 2 * D_FF), BF16),
                   jax.ShapeDtypeStruct((SEQ, D_FF), BF16), jax.ShapeDtypeStruct((SEQ, D_MODEL), F32),
                   jax.ShapeDtypeStruct((N_TB, 8, 128), F32)),
        in_specs=[tok(D_MODEL), _resident((1, D_MODEL)), _resident((2 * D_FF, D_MODEL)), _resident((8, 2 * D_FF)),
                  _resident((1, 2 * D_FF)), _resident((D_FF, D_MODEL)), _resident((1, D_MODEL)), tok(D_MODEL)],
        out_specs=(tok(D_MODEL), tok(2 * D_FF), tok(2 * D_FF), tok(D_FF), tok(D_MODEL),
                   pl.BlockSpec((1, 8, 128), lambda i: (i, 0, 0))),
        scratch_shapes=[pltpu.VMEM((8, 2 * D_FF), F32)],
        compiler_params=_cparams(("arbitrary",)),
    )(x2, g2, wup_g, cw_g, cb_g, wdn_g, gf, tgt)


def _bwd_ffn(x3, tgt, gf, x2, g2, up_pre, u_conv, wup_g, cw_g, wdn_g):
    def body(x3_ref, t_ref, gf_ref, x2_ref, g2_ref, up_ref, u_ref, wu_ref, cw_ref, wd_ref,
             dx3_ref, dpre_ref, dx2_ref, dgf_ref, dg2_ref, dcv_ref, nxt):
        i = pl.program_id(0)

        @pl.when(i == 0)
        def _():
            nxt[...] = jnp.zeros_like(nxt)
            dgf_ref[...] = jnp.zeros_like(dgf_ref)
            dg2_ref[...] = jnp.zeros_like(dg2_ref)
            dcv_ref[...] = jnp.zeros_like(dcv_ref)

        x3 = x3_ref[...]
        r3 = lax.rsqrt(jnp.mean(x3 * x3, axis=-1, keepdims=True) + EPS)
        xh3 = x3 * r3
        dy = (xh3 * gf_ref[...] - t_ref[...]) * (1.0 / D_MODEL)
        dgf_ref[0:1, :] += jnp.sum(dy * xh3, axis=0, keepdims=True)
        t3 = dy * gf_ref[...]
        dx3 = r3 * (t3 - xh3 * jnp.mean(t3 * xh3, axis=-1, keepdims=True))
        dx3b = dx3.astype(BF16)
        dx3_ref[...] = dx3b
        dh2 = jnp.zeros((TM, D_MODEL), F32)
        for t0, tw in FF_TILES:
            row = lax.broadcasted_iota(jnp.int32, (TM, tw), 0)
            ts = slice(t0, t0 + tw)
            dact = _dot_nt(dx3b, wd_ref[pl.ds(t0, tw), :])
            ua = u_ref[:, ts].astype(F32)
            ub = u_ref[:, D_FF + t0:D_FF + t0 + tw].astype(F32)
            sg = _sigmoid(ua)
            du = [dact * ub * (sg * (1.0 + ua * (1.0 - sg))), dact * (ua * sg)]
            for n in range(2):
                d = du[n]
                c0 = n * D_FF + t0
                cs = slice(c0, c0 + tw)
                nx = nxt[:, cs]
                n1 = jnp.where(row == TM - 1, nx[0:1, :], pltpu.roll(d, TM - 1, 0))
                n2 = jnp.where(row == TM - 2, nx[0:1, :], jnp.where(row == TM - 1, nx[1:2, :], pltpu.roll(d, TM - 2, 0)))
                nxt[:, cs] = d[0:8, :]
                dp = (d * cw_ref[2:3, cs] + n1 * cw_ref[1:2, cs] + n2 * cw_ref[0:1, cs]).astype(BF16)
                dpre_ref[:, cs] = dp
                p = up_ref[:, cs].astype(F32)
                dcv_ref[n, 0:1, ts] += jnp.sum(n2 * p, axis=0, keepdims=True)
                dcv_ref[n, 1:2, ts] += jnp.sum(n1 * p, axis=0, keepdims=True)
                dcv_ref[n, 2:3, ts] += jnp.sum(d * p, axis=0, keepdims=True)
                dcv_ref[n, 3:4, ts] += jnp.sum(d, axis=0, keepdims=True)
                dh2 = dh2 + _dot(dp, wu_ref[pl.ds(c0, tw), :])
        x2 = x2_ref[...]
        r2 = lax.rsqrt(jnp.mean(x2 * x2, axis=-1, keepdims=True) + EPS)
        xh2 = x2 * r2
        dg2_ref[0:1, :] += jnp.sum(dh2 * xh2, axis=0, keepdims=True)
        t2 = dh2 * g2_ref[...]
        dx2_ref[...] = dx3 + r2 * (t2 - xh2 * jnp.mean(t2 * xh2, axis=-1, keepdims=True))

    rev = lambda w: pl.BlockSpec((TM, w), lambda i: (N_TB - 1 - i, 0))
    acc = lambda s: pl.BlockSpec(s, lambda i: (0,) * len(s))
    return pl.pallas_call(
        body, name="bwd_ffn", grid=(N_TB,),
        out_shape=(jax.ShapeDtypeStruct((SEQ, D_MODEL), BF16), jax.ShapeDtypeStruct((SEQ, 2 * D_FF), BF16),
                   jax.ShapeDtypeStruct((SEQ, D_MODEL), F32), jax.ShapeDtypeStruct((8, D_MODEL), F32),
                   jax.ShapeDtypeStruct((8, D_MODEL), F32), jax.ShapeDtypeStruct((2, 8, D_FF), F32)),
        in_specs=[rev(D_MODEL), rev(D_MODEL), _resident((1, D_MODEL)), rev(D_MODEL), _resident((1, D_MODEL)), rev(2 * D_FF),
                  rev(2 * D_FF), _resident((2 * D_FF, D_MODEL)), _resident((8, 2 * D_FF)), _resident((D_FF, D_MODEL))],
        out_specs=(rev(D_MODEL), rev(2 * D_FF), rev(D_MODEL), acc((8, D_MODEL)), acc((8, D_MODEL)), acc((2, 8, D_FF))),
        scratch_shapes=[pltpu.VMEM((8, 2 * D_FF), F32)],
        compiler_params=_cparams(("arbitrary",)),
    )(x3, tgt, gf, x2, g2, up_pre, u_conv, wup_g, cw_g, wdn_g)


def _bwd_mix(dx2, proj, o, sprev, wout_g, grn, lng, lnb, ws, bsb, cos2, sin2, hosted):
    cdec = _chunk_decay()
    geoms = [g for g, _ in hosted]
    n_h = len(hosted)

    def body(dx2_ref, p_ref, o_ref, sp_ref, w_ref, grn_ref, lng_ref, lnb_ref, ws_ref, bsb_ref, cos_ref, sin_ref, *rest):
        dp_ref, dgrn_ref, dlng_ref, dlnb_ref, dws_ref, dbs_ref = rest[n_h:n_h + 6]
        dstate, dbs_acc, m_ref, qd_ref, kd_ref = rest[2 * n_h + 6:2 * n_h + 11]
        i = pl.program_id(0)
        rs = _Scatters(geoms, rest[:n_h], rest[n_h + 6:2 * n_h + 6], rest[2 * n_h + 11:])
        pl.when(i == 0)(rs.phase1)
        pl.when(i == 3)(rs.phase2)
        pl.when(i == 8)(rs.phase2b)

        @pl.when(i == 0)
        def _():
            _fill_decay(m_ref, qd_ref, kd_ref)
            dstate[...] = jnp.zeros_like(dstate)
            dgrn_ref[...] = jnp.zeros_like(dgrn_ref)
            dlng_ref[...] = jnp.zeros_like(dlng_ref)
            dlnb_ref[...] = jnp.zeros_like(dlnb_ref)
            dws_ref[...] = jnp.zeros_like(dws_ref)
            dbs_ref[...] = jnp.zeros_like(dbs_ref)
            dbs_acc[...] = jnp.zeros_like(dbs_acc)

        dmix = _dot_nt(dx2_ref[...].astype(BF16), w_ref[...])
        for h in range(HEADS):
            sl = slice(h * HEAD_DIM, (h + 1) * HEAD_DIM)
            q = p_ref[:, sl]
            k = p_ref[:, RET_W + h * HEAD_DIM:RET_W + (h + 1) * HEAD_DIM]
            v = p_ref[:, 2 * RET_W + h * HEAD_DIM:2 * RET_W + (h + 1) * HEAD_DIM]
            g = p_ref[:, 3 * RET_W + h * HEAD_DIM:3 * RET_W + (h + 1) * HEAD_DIM]
            o = o_ref[:, sl]
            rinv = lax.rsqrt(jnp.mean(o * o, axis=-1, keepdims=True) + EPS)
            oh = o * rinv
            gr = grn_ref[:, sl]
            sg = _sigmoid(g)
            dret = dmix[:, sl]
            dp_ref[:, 3 * RET_W + h * HEAD_DIM:3 * RET_W + (h + 1) * HEAD_DIM] = (
                dret * (oh * gr) * (sg * (1.0 + g * (1.0 - sg)))).astype(BF16)
            drn = dret * (g * sg)
            dgrn_ref[0:1, sl] += jnp.sum(drn * oh, axis=0, keepdims=True)
            t = drn * gr
            do = rinv * (t - oh * jnp.mean(t * oh, axis=-1, keepdims=True))
            qb, kb, vb, dob = q.astype(BF16), k.astype(BF16), v.astype(BF16), do.astype(BF16)
            m = m_ref[h]
            ab = (_dot_nt(qb, kb) * m).astype(BF16)
            dab = (_dot_nt(dob, vb) * m).astype(BF16)
            spb = sp_ref[0, h]
            dsn = dstate[h]
            dsnb = dsn.astype(BF16)
            qdb = (q * qd_ref[h]).astype(BF16)
            kdb = (k * kd_ref[h]).astype(BF16)
            dq = _dot(dab, kb) + _dot_nt(dob, spb) * qd_ref[h]
            dk = _dot_tn(dab, qb) + _dot_nt(vb, dsnb) * kd_ref[h]
            dv = _dot_tn(ab, dob) + _dot(kdb, dsnb)
            dstate[h] = dsn * cdec[h] + _dot_tn(qdb, dob)
            c2, s2 = cos_ref[...], sin_ref[...]
            dp_ref[:, sl] = _rot_t(dq, c2, s2).astype(BF16)
            dp_ref[:, RET_W + h * HEAD_DIM:RET_W + (h + 1) * HEAD_DIM] = _rot_t(dk * K_SCALE, c2, s2).astype(BF16)
            dp_ref[:, 2 * RET_W + h * HEAD_DIM:2 * RET_W + (h + 1) * HEAD_DIM] = dv.astype(BF16)
        for gi in range(HEADS):
            sl = slice(gi * HEAD_DIM, (gi + 1) * HEAD_DIM)
            u = p_ref[:, 4 * RET_W + gi * HEAD_DIM:4 * RET_W + (gi + 1) * HEAD_DIM]
            sv = p_ref[:, 4 * RET_W + SGU_W + gi * HEAD_DIM:4 * RET_W + SGU_W + (gi + 1) * HEAD_DIM]
            gv = _gelu(sv)
            xc = gv - jnp.mean(gv, axis=-1, keepdims=True)
            rstd = lax.rsqrt(jnp.mean(xc * xc, axis=-1, keepdims=True) + EPS)
            xh = xc * rstd
            lg = lng_ref[:, sl]
            vnb = (xh * lg + lnb_ref[:, sl]).astype(BF16)
            wcb = _causal(ws_ref[gi]).astype(BF16)
            mixed = _dot(wcb, vnb) + bsb_ref[gi]
            dsgu = dmix[:, RET_W + gi * HEAD_DIM:RET_W + (gi + 1) * HEAD_DIM]
            dmixed = dsgu * _gelu(u)
            dmb = dmixed.astype(BF16)
            dws_ref[gi] += _causal(_dot_nt(dmb, vnb))
            dbs_acc[gi] += dmixed
            dvn = _dot_tn(wcb, dmb)
            dlng_ref[gi:gi + 1, :] += jnp.sum(dvn * xh, axis=0, keepdims=True)
            dlnb_ref[gi:gi + 1, :] += jnp.sum(dvn, axis=0, keepdims=True)
            dxh = dvn * lg
            dgv = rstd * (dxh - jnp.mean(dxh, axis=-1, keepdims=True) - xh * jnp.mean(dxh * xh, axis=-1, keepdims=True))
            dp_ref[:, 4 * RET_W + gi * HEAD_DIM:4 * RET_W + (gi + 1) * HEAD_DIM] = (dsgu * mixed * _gelu_grad(u)).astype(BF16)
            dp_ref[:, 4 * RET_W + SGU_W + gi * HEAD_DIM:4 * RET_W + SGU_W + (gi + 1) * HEAD_DIM] = (
                dgv * _gelu_grad(sv)).astype(BF16)

        @pl.when(i == N_CHUNK - 1)
        def _():
            for gi in range(HEADS):
                col = jnp.broadcast_to(jnp.sum(dbs_acc[gi], axis=-1, keepdims=True), (CHUNK, CHUNK))
                dbs_ref[gi:gi + 1, :] = jnp.transpose(col)[0:1, :]
            rs.phase3()

    rev = lambda w: pl.BlockSpec((CHUNK, w), lambda i: (N_CHUNK - 1 - i, 0))
    hcc = (HEADS, CHUNK, CHUNK)
    acc = lambda s: pl.BlockSpec(s, lambda i: (0,) * len(s))
    res = pl.pallas_call(
        body, name="bwd_mix", grid=(N_CHUNK,),
        out_shape=(jax.ShapeDtypeStruct((SEQ, PROJ_W), BF16), jax.ShapeDtypeStruct((8, RET_W), F32),
                   jax.ShapeDtypeStruct((8, HEAD_DIM), F32), jax.ShapeDtypeStruct((8, HEAD_DIM), F32),
                   jax.ShapeDtypeStruct(hcc, F32), jax.ShapeDtypeStruct((8, CHUNK), F32)) + _scatter_out_shapes(geoms),
        in_specs=[rev(D_MODEL), rev(PROJ_W), rev(RET_W),
                  pl.BlockSpec((1, HEADS, HEAD_DIM, HEAD_DIM), lambda i: (N_CHUNK - 1 - i, 0, 0, 0)),
                  _resident((D_MODEL, D_MODEL)), _resident((1, RET_W)), _resident((1, SGU_W)), _resident((1, SGU_W)),
                  _resident(hcc), _resident(hcc), rev(HEAD_DIM), rev(HEAD_DIM)]
        + [pl.BlockSpec(memory_space=pl.ANY)] * n_h,
        out_specs=(rev(PROJ_W), acc((8, RET_W)), acc((8, HEAD_DIM)), acc((8, HEAD_DIM)), acc(hcc), acc((8, CHUNK)))
        + _scatter_out_specs(geoms),
        scratch_shapes=[pltpu.VMEM((HEADS, HEAD_DIM, HEAD_DIM), F32), pltpu.VMEM((HEADS, CHUNK, CHUNK), F32)]
        + [pltpu.VMEM(hcc, F32)] * 3 + _scatter_scratch(geoms),
        compiler_params=_cparams(("arbitrary",), collective=COLLECTIVE["bwd_mix"]),
    )(dx2, proj, o, sprev, wout_g, grn, lng, lnb, ws, bsb, cos2, sin2, *[p for _, p in hosted])
    return tuple(res[:6 + n_h])


def _bwd_proj(dproj, win_g, x, g1, dx2, gin_p, small):
    geoms = [W_IN]
    n_s = len(small)

    def body(dp_ref, w_ref, x_ref, g_ref, dx2_ref, gin_ref, *rest):
        small_refs = rest[:n_s]
        dx_ref, rs_out, rp_ref, rws_ref, rcv_ref, dg_ref = rest[n_s:n_s + 6]
        rs_scratch = rest[n_s + 6:n_s + 6 + N_SCATTER_SCRATCH]
        ar_scratch = rest[n_s + 6 + N_SCATTER_SCRATCH:]
        ar_res = ar_scratch[N_SMALL_SCRATCH:]
        ar = _SmallReduce((dg_ref,) + tuple(small_refs), ar_res, ar_scratch[:N_SMALL_SCRATCH])
        rs = _Scatters(geoms, [gin_ref], [rs_out], rs_scratch)
        pl.when(pl.program_id(0) == 0)(lambda: rs.phase1(diagonal=True))
        pl.when(pl.program_id(0) == 1)(rs.phase2)
        pl.when(pl.program_id(0) == 5)(rs.phase2b)

        @pl.when(pl.program_id(0) == 0)
        def _():
            dg_ref[...] = jnp.zeros_like(dg_ref)

        dh = _dot_nt(dp_ref[...], w_ref[...])
        xb = x_ref[...]
        r = lax.rsqrt(jnp.mean(xb * xb, axis=-1, keepdims=True) + EPS)
        xh = xb * r
        dg_ref[0:1, :] += jnp.sum(dh * xh, axis=0, keepdims=True)
        t = dh * g_ref[...]
        dx_ref[...] = dx2_ref[...] + r * (t - xh * jnp.mean(t * xh, axis=-1, keepdims=True))

        @pl.when(pl.program_id(0) == N_TB - 1)
        def _():
            ar.begin()
            rs.phase3()
            ar.end()
            for o_ref, r_ref in zip((rp_ref, rws_ref, rcv_ref), ar_res):
                o_ref[...] = r_ref[...]

    tok = lambda w: pl.BlockSpec((TM, w), lambda i: (i, 0))
    vm = pl.BlockSpec(memory_space=pltpu.VMEM)
    res = pl.pallas_call(
        body, name="bwd_proj", grid=(N_TB,),
        out_shape=(jax.ShapeDtypeStruct((SEQ, D_MODEL), F32),) + _scatter_out_shapes(geoms)
        + tuple(jax.ShapeDtypeStruct(s, F32) for s in SMALL_FULL),
        in_specs=[tok(PROJ_W), _resident((D_MODEL, PROJ_W)), tok(D_MODEL), _resident((1, D_MODEL)), tok(D_MODEL),
                  pl.BlockSpec(memory_space=pl.ANY)] + [vm] * n_s,
        out_specs=(tok(D_MODEL),) + _scatter_out_specs(geoms) + (vm,) * len(SMALL_FULL),
        scratch_shapes=[pltpu.VMEM((8, D_MODEL), F32)] + _scatter_scratch(geoms) + _small_scratch()
        + [pltpu.VMEM(s, F32) for s in SMALL_FULL],
        compiler_params=_cparams(("arbitrary",), collective=COLLECTIVE["bwd_proj"]),
    )(dproj, win_g, x, g1, dx2, gin_p, *small)
    return res


def _wgrad(name, a, b, tm=None, tn=None, hosted=()):
    m_w, n_w = a.shape[-1], b.shape[-1]
    tm = m_w if tm is None else tm
    tn = n_w if tn is None else tn
    n_steps = (m_w // tm) * (n_w // tn)
    geoms = [g for g, _ in hosted]
    n_h = len(hosted)

    def body(a_ref, b_ref, *rest):
        o_ref = rest[n_h]
        if n_h:
            rs = _Scatters(geoms, rest[:n_h], rest[n_h + 1:2 * n_h + 1], rest[2 * n_h + 1:])
            step = pl.program_id(0) * (n_w // tn) + pl.program_id(1)
            pl.when(step == 0)(rs.phase1)
            pl.when(step == 1)(rs.phase2)
            pl.when(step == n_steps // 2)(rs.phase2b)
        o_ref[...] = _dot_tn(a_ref[...].astype(BF16), b_ref[...].astype(BF16)).astype(BF16)
        if n_h:
            pl.when(step == n_steps - 1)(rs.phase3)

    assert not n_h or n_steps >= 4
    res = pl.pallas_call(
        body, name=name, grid=(m_w // tm, n_w // tn),
        out_shape=(jax.ShapeDtypeStruct((m_w, n_w), BF16),) + _scatter_out_shapes(geoms),
        in_specs=[pl.BlockSpec((SEQ, tm), lambda i, j: (0, i)), pl.BlockSpec((SEQ, tn), lambda i, j: (0, j))]
        + [pl.BlockSpec(memory_space=pl.ANY)] * n_h,
        out_specs=(pl.BlockSpec((tm, tn), lambda i, j: (i, j)),) + _scatter_out_specs(geoms),
        scratch_shapes=_scatter_scratch(geoms),
        compiler_params=_cparams(("arbitrary", "arbitrary"), collective=COLLECTIVE[name]) if n_h else _cparams(("parallel", "parallel")),
    )(a, b, *[p for _, p in hosted])
    return tuple(res[:1 + n_h])


def _row_step(half_rows):
    return max(s for s in range(16, 177, 16) if half_rows % s == 0)


class _Scatter:
    def __init__(self, geom, partial, out, land1, mine, stage2, land2, comb, s1_send, s1_recv, s2_send, s2_recv, ld_sems):
        self.w, self.row0, self.shape = _geom(geom)
        self.partial, self.out, self.land1 = partial, out, land1
        self.mine, self.stage2, self.land2, self.comb = mine, stage2, land2, comb
        self.hr = self.shape[0] // 2
        self.step = _row_step(self.hr)
        self.s1_send, self.s1_recv, self.s2_send, self.s2_recv, self.ld_sems = s1_send, s1_recv, s2_send, s2_recv, ld_sems
        self.x, self.y, self.c = lax.axis_index("x"), lax.axis_index("y"), lax.axis_index("c")
        self.sibling = (self.x, self.y, 1 - self.c)
        self.chips = [(self.x, self.y), (1 - self.x, self.y), (self.x, 1 - self.y), (1 - self.x, 1 - self.y)]

    def block(self, px, py, pc):
        dev = 4 * px + 2 * py + pc
        if self.w == W_IN:
            return self.partial.at[:, pl.ds(pl.multiple_of(dev * IN_SHARD, 128), IN_SHARD)]
        if self.w == W_OUT:
            return self.partial.at[pl.ds(pl.multiple_of(dev * OUT_SHARD, 128), OUT_SHARD), :]
        if self.w == W_DOWN:
            return self.partial.at[pl.ds(pl.multiple_of(dev * DOWN_SHARD, 32), DOWN_SHARD), :]
        return self.partial.at[pl.ds(pl.multiple_of(dev * FF_SHARD + self.row0, 32), self.shape[0]), :]

    def copy1(self, k):
        return pltpu.make_async_remote_copy(
            src_ref=self.block(*self.chips[k], 1 - self.c), dst_ref=self.land1.at[k],
            send_sem=self.s1_send.at[k], recv_sem=self.s1_recv.at[k], device_id=self.sibling, device_id_type=MESH)

    STAGE2 = [(1, 0, 1), (3, 0, 1), (2, 1, 2), (3, 1, 2), (1, 1, 1), (2, 0, 2)]

    def copy2(self, j):
        blk, h, to = self.STAGE2[j]
        src = self.comb.at[j - 4] if j >= 4 else self.stage2.at[blk - 1, pl.ds(h * self.hr, self.hr), :]
        return pltpu.make_async_remote_copy(
            src_ref=src, dst_ref=self.land2.at[j], send_sem=self.s2_send.at[j], recv_sem=self.s2_recv.at[j],
            device_id=(*self.chips[to], self.c), device_id_type=MESH)

    def _rows(self, h=None):
        step = self.step
        lo, n = (0, self.shape[0]) if h is None else (h * self.hr, self.hr)
        return [pl.ds(r0, step) for r0 in range(lo, lo + n, step)]

    def load(self, k):
        return pltpu.make_async_copy(self.block(*self.chips[k], self.c), self.mine.at[k], self.ld_sems.at[k])

    def load_mine(self):
        for k in range(4):
            self.load(k).start()

    def phase1(self):
        for k in range(4):
            self.copy1(k).start()

    def phase2(self, k):
        self.copy1(k).wait_recv()
        self.load(k).wait()
        for rs in self._rows():
            s = self.mine[k, rs, :].astype(F32) + self.land1[k, rs, :].astype(F32)
            if k == 0:
                self.out[rs, :] = s
            else:
                self.stage2[k - 1, rs, :] = s.astype(BF16)
        for j in {3: (1, 3), 1: (0,), 2: (2,), 0: ()}[k]:
            self.copy2(j).start()

    def phase2b(self):
        for j, got in ((4, 3), (5, 1)):
            blk, h, _ = self.STAGE2[j]
            self.copy2(got).wait_recv()
            for i, rs in enumerate(self._rows(h)):
                lr = pl.ds(i * self.step, self.step)
                self.comb[j - 4, lr, :] = (self.stage2[blk - 1, rs, :].astype(F32) + self.land2[got, lr, :].astype(F32)).astype(BF16)
            self.copy2(j).start()

    def phase3(self):
        for j in (0, 5, 4, 2):
            self.copy2(j).wait_recv()
        for h, (first, second) in enumerate(((0, 5), (4, 2))):
            for i, rs in enumerate(self._rows(h)):
                lr = pl.ds(i * self.step, self.step)
                self.out[rs, :] = (self.out[rs, :] + self.land2[first, lr, :].astype(F32)) + self.land2[second, lr, :].astype(F32)
        for k in range(4):
            self.copy1(k).wait_send()
        for j in range(6):
            self.copy2(j).wait_send()


def _geom(geom):
    if isinstance(geom, tuple):
        w, row0, rows = geom
        assert w == W_UP
        return w, row0, (rows, SHARD[w][1])
    return geom, 0, SHARD[geom]


N_SCATTER_SCRATCH = 10


def _scatter_out_shapes(geoms):
    return tuple(jax.ShapeDtypeStruct(_geom(g)[2], F32) for g in geoms)


def _scatter_out_specs(geoms):
    return (pl.BlockSpec(memory_space=pltpu.VMEM),) * len(geoms)


def _scatter_scratch(geoms):
    out = []
    for g in geoms:
        s = _geom(g)[2]
        hs = (s[0] // 2, s[1])
        out += [pltpu.VMEM((4,) + s, BF16), pltpu.VMEM((4,) + s, BF16), pltpu.VMEM((3,) + s, BF16), pltpu.VMEM((6,) + hs, BF16),
                pltpu.VMEM((2,) + hs, BF16),
                pltpu.SemaphoreType.DMA((4,)), pltpu.SemaphoreType.DMA((4,)), pltpu.SemaphoreType.DMA((6,)),
                pltpu.SemaphoreType.DMA((6,)), pltpu.SemaphoreType.DMA((4,))]
    return out


class _Scatters:
    def __init__(self, geoms, p_refs, out_refs, scratch):
        k = N_SCATTER_SCRATCH
        self.items = [_Scatter(g, p_refs[i], out_refs[i], *scratch[k * i:k * i + k]) for i, g in enumerate(geoms)]

    def phase1(self, diagonal=False):
        meet = _Meet(diagonal)
        meet.signal()
        for s in self.items:
            s.load_mine()
        meet.wait()
        for s in self.items:
            s.phase1()

    def phase2(self):
        for k in (3, 1, 2, 0):
            for s in self.items:
                s.phase2(k)

    def phase2b(self):
        for s in self.items:
            s.phase2b()

    def phase3(self):
        for s in self.items:
            s.phase3()


PACK_W = 1024


SMALL_FULL = [(2, 8, PACK_W), (HEADS, CHUNK, CHUNK), (2, 8, D_FF)]
SMALL_HALF = [(s[0] // 2,) + s[1:] for s in SMALL_FULL]
N_SMALL_SCRATCH = 16


def _small_scratch():
    n_a = len(SMALL_FULL)
    return ([pltpu.VMEM(SMALL_FULL[0], F32)] + [pltpu.VMEM(s, F32) for s in SMALL_HALF] + [pltpu.VMEM(s, F32) for s in SMALL_HALF]
            + [pltpu.VMEM((3,) + s, F32) for s in SMALL_HALF]
            + [pltpu.SemaphoreType.DMA((n_a,)), pltpu.SemaphoreType.DMA((n_a,)), pltpu.SemaphoreType.DMA((n_a, 3)),
               pltpu.SemaphoreType.DMA((n_a, 3)), pltpu.SemaphoreType.DMA((n_a,)), pltpu.SemaphoreType.DMA((n_a,))])


class _SmallReduce:
    def __init__(self, ins, outs, scratch):
        self.ins, self.outs = ins, outs
        (self.pack, *rest) = scratch
        self.rxs, self.css, self.gs = rest[0:3], rest[3:6], rest[6:9]
        self.s1_send, self.s1_recv, self.s2_send, self.s2_recv, self.s3_send, self.s3_recv = rest[9:]
        self.x, self.y, self.c = lax.axis_index("x"), lax.axis_index("y"), lax.axis_index("c")
        self.sibling = (self.x, self.y, 1 - self.c)
        self.chips = [(1 - self.x, self.y), (self.x, 1 - self.y), (1 - self.x, 1 - self.y)]
        self.hl = [s[0] for s in SMALL_HALF]

    def half(self, ref, a, h):
        return ref.at[pl.ds(h * self.hl[a], self.hl[a])]

    def begin(self):
        dg1_ref, dg2_ref, dgf_ref, dgrn_ref, dlng_ref, dlnb_ref, dbs_ref, loss_ref, dws_ref, dcv_ref = self.ins
        pack, c = self.pack, self.c
        pack[...] = jnp.zeros_like(pack)
        pack[0, 0:1, :] = dg1_ref[0:1, :]
        pack[0, 1:2, :] = dg2_ref[0:1, :]
        pack[0, 2:3, :] = dgf_ref[0:1, :]
        pack[0, 3:4, 0:RET_W] = dgrn_ref[0:1, :]
        lsum = loss_ref[0, 0:1, :]
        for i in range(1, N_TB):
            lsum = lsum + loss_ref[i, 0:1, :]
        pack[0, 3:4, RET_W:RET_W + 128] = lsum
        pack[1, 0:HEADS, 0:128] = dlng_ref[0:HEADS, :]
        pack[1, 0:HEADS, 128:256] = dlnb_ref[0:HEADS, :]
        pack[1, 0:HEADS, 256:384] = dbs_ref[0:HEADS, :]
        self.srcs = [pack, dws_ref, dcv_ref]
        n_a = len(self.srcs)
        self.ex1 = [pltpu.make_async_remote_copy(src_ref=self.half(self.srcs[a], a, 1 - c), dst_ref=self.rxs[a],
                                                 send_sem=self.s1_send.at[a], recv_sem=self.s1_recv.at[a],
                                                 device_id=self.sibling, device_id_type=MESH) for a in range(n_a)]
        for cp in self.ex1:
            cp.start()
        self.ex2 = []
        for a in range(n_a):
            self.ex1[a].wait_recv()
            self.css[a][...] = self.half(self.srcs[a], a, c)[...] + self.rxs[a][...]
            for j, chip in enumerate(self.chips):
                cp = pltpu.make_async_remote_copy(src_ref=self.css[a], dst_ref=self.gs[a].at[j], send_sem=self.s2_send.at[a, j],
                                                  recv_sem=self.s2_recv.at[a, j], device_id=(*chip, c), device_id_type=MESH)
                cp.start()
                self.ex2.append(cp)

    def end(self):
        c, x, y = self.c, self.x, self.y
        ex3 = []
        for a in range(len(self.srcs)):
            css, gs, out = self.css[a], self.gs[a], self.outs[a]
            for j in range(3):
                self.ex2[3 * a + j].wait_recv()
            tot = None
            for q in range(4):
                k = jnp.where(x != (q >> 1), 1, 0) + jnp.where(y != (q & 1), 2, 0)
                term = jnp.where(k == 0, css[...], jnp.where(k == 1, gs[0], jnp.where(k == 2, gs[1], gs[2])))
                tot = term if tot is None else tot + term
            self.half(out, a, c)[...] = tot
            cp = pltpu.make_async_remote_copy(src_ref=self.half(out, a, c), dst_ref=self.half(out, a, c), send_sem=self.s3_send.at[a],
                                              recv_sem=self.s3_recv.at[a], device_id=self.sibling, device_id_type=MESH)
            cp.start()
            ex3.append(cp)
        for a in range(len(self.srcs)):
            out = self.outs[a]
            pltpu.make_async_remote_copy(src_ref=self.half(out, a, 1 - c), dst_ref=self.half(out, a, 1 - c), send_sem=self.s3_send.at[a],
                                         recv_sem=self.s3_recv.at[a], device_id=self.sibling, device_id_type=MESH).wait_recv()
        for cp in self.ex1 + self.ex2 + ex3:
            cp.wait_send()


def _adam_math(w, g, m, v):
    nm = ADAM_B1 * m + (1.0 - ADAM_B1) * g
    nv = ADAM_B2 * v + (1.0 - ADAM_B2) * (g * g)
    d = -ADAM_LR * ((nm / (1.0 - ADAM_B1 ** ADAM_STEP)) / (jnp.sqrt(nv / (1.0 - ADAM_B2 ** ADAM_STEP)) + ADAM_EPS) + ADAM_WD * w)
    return d, nm, nv


def _adamw(params, thru, n_steps):
    plan = []
    for w, gs, _, _ in params:
        _, r, cdim = w.shape
        if len(gs) == 1:
            edges = [0, r // n_steps]
            spec3 = pl.BlockSpec((1, r // n_steps, cdim), lambda i: (0, i, 0))
            g_specs = [pl.BlockSpec((r // n_steps, cdim), lambda i: (i, 0))]
        else:
            edges = [sum(g.shape[0] for g in gs[:k]) for k in range(len(gs) + 1)]
            spec3 = pl.BlockSpec((1, r, cdim // n_steps), lambda i: (0, 0, i))
            g_specs = [pl.BlockSpec((g.shape[0], cdim // n_steps), lambda i: (0, i)) for g in gs]
        plan.append((len(gs), edges, spec3, g_specs))
    n_in = sum(n_g + 3 for n_g, _, _, _ in plan)
    n_t = len(thru)

    def body(*refs):
        ins, outs = refs[:n_in], refs[n_in + n_t:]
        for n_g, edges, _, _ in plan:
            (w_ref, *g_refs, m_ref, v_ref), ins = ins[:n_g + 3], ins[n_g + 3:]
            (go_ref, d_ref, nm_ref, nv_ref), outs = outs[:4], outs[4:]
            for g_ref, lo, hi in zip(g_refs, edges[:-1], edges[1:]):
                gg = g_ref[...]
                go_ref[0, lo:hi, :] = gg
                d_ref[0, lo:hi, :], nm_ref[0, lo:hi, :], nv_ref[0, lo:hi, :] = _adam_math(
                    w_ref[0, lo:hi, :], gg, m_ref[0, lo:hi, :], v_ref[0, lo:hi, :])
        for t_ref, to_ref in zip(refs[n_in:n_in + n_t], outs):
            to_ref[...] = t_ref[...]

    t_specs = [pl.BlockSpec((t.shape[0] // n_steps, t.shape[1]), lambda i: (i, 0)) for t in thru]
    in_specs, out_specs, out_shape, args = [], [], [], []
    for (w, gs, m, v), (_, _, spec3, g_specs) in zip(params, plan):
        in_specs += [spec3] + g_specs + [spec3, spec3]
        out_specs += [spec3] * 4
        out_shape += [jax.ShapeDtypeStruct(w.shape, F32)] * 4
        args += [w, *gs, m, v]
    res = pl.pallas_call(
        body, name="adamw", grid=(n_steps,), out_shape=tuple(out_shape) + tuple(jax.ShapeDtypeStruct(t.shape, t.dtype) for t in thru),
        in_specs=in_specs + t_specs, out_specs=tuple(out_specs) + tuple(t_specs),
        compiler_params=_cparams(("parallel",)),
    )(*args, *thru)
    return [res[4 * k:4 * k + 4] for k in range(len(params))], res[4 * len(params):]


def _adamw_small(rp, rws, rcv, gcw, params):
    n_p = len(params)

    def body(*refs):
        rp_ref, rws_ref, rcv_ref, gcw_ref = refs[:4]
        ins = refs[4:4 + 3 * n_p]
        outs = refs[4 + 3 * n_p:]
        outs[4 * n_p][...] = rp_ref[0, 3:4, RET_W:RET_W + 1]
        grads = [rp_ref[0, 0:1, :], rp_ref[0, 1:2, :], rp_ref[0, 2:3, :], rp_ref[0, 3:4, 0:RET_W],
                 rp_ref[1, 0:HEADS, 0:128], rp_ref[1, 0:HEADS, 128:256], rp_ref[1, 0:HEADS, 256:384],
                 rws_ref[...], gcw_ref[...], None]
        for p in range(n_p):
            w_ref, m_ref, v_ref = ins[3 * p:3 * p + 3]
            o = outs[4 * p:4 * p + 4]
            if p == n_p - 1:
                for hf in range(2):
                    cs = slice(hf * D_FF, (hf + 1) * D_FF)
                    g = rcv_ref[hf, 3:4, :]
                    res = (g,) + _adam_math(w_ref[:, cs], g, m_ref[:, cs], v_ref[:, cs])
                    for t in range(4):
                        o[t][:, cs] = res[t]
                continue
            lead = w_ref.ndim > grads[p].ndim
            rd = (lambda r: r[0]) if lead else (lambda r: r[...])
            res = (grads[p],) + _adam_math(rd(w_ref), grads[p], rd(m_ref), rd(v_ref))
            for t in range(4):
                if lead:
                    o[t][0] = res[t]
                else:
                    o[t][...] = res[t]

    vm = pl.BlockSpec(memory_space=pltpu.VMEM)
    flat = [a for tr in params for a in tr]
    out_shape = tuple(jax.ShapeDtypeStruct(tr[0].shape, F32) for tr in params for _ in range(4)) + (jax.ShapeDtypeStruct((1, 1), F32),)
    res = pl.pallas_call(
        body, name="adamw_small", out_shape=out_shape, in_specs=[vm] * (4 + len(flat)), out_specs=(vm,) * len(out_shape),
        compiler_params=_cparams(),
    )(rp, rws, rcv, gcw, *flat)
    return [res[4 * p:4 * p + 4] for p in range(n_p)], res[4 * n_p]


def kernel(x, mix_norm_g, w_in, ret_norm_g, sgu_ln_g, sgu_ln_b, sgu_w_s, sgu_b_s, w_out, ffn_norm_g, w_up, conv_w, conv_b, w_down, final_norm_g, loss_target, m_mix_norm_g, m_w_in, m_ret_norm_g, m_sgu_ln_g, m_sgu_ln_b, m_sgu_w_s, m_sgu_b_s, m_w_out, m_ffn_norm_g, m_w_up, m_conv_w, m_conv_b, m_w_down, m_final_norm_g, v_mix_norm_g, v_w_in, v_ret_norm_g, v_sgu_ln_g, v_sgu_ln_b, v_sgu_w_s, v_sgu_b_s, v_w_out, v_ffn_norm_g, v_w_up, v_conv_w, v_conv_b, v_w_down, v_final_norm_g):
    xs = x[0]
    tgt = loss_target[0]
    grn = ret_norm_g.reshape(1, RET_W)
    lng = sgu_ln_g.reshape(1, SGU_W)
    lnb = sgu_ln_b.reshape(1, SGU_W)
    ws = sgu_w_s[0]
    bsb = jnp.broadcast_to(sgu_b_s[0][:, :, None], (HEADS, CHUNK, HEAD_DIM))
    gf = final_norm_g.reshape(1, D_MODEL)
    me = 4 * lax.axis_index("x") + 2 * lax.axis_index("y") + lax.axis_index("c")
    tr = lambda a: jnp.transpose(a[0])[None]
    tr_cw = lambda a: jnp.transpose(a, (1, 0, 2))

    proj, h1, cos2, sin2, win_g, cw_sh, wout_g, wdn_g, su = _fwd_proj(
        xs, mix_norm_g, _rope_freq(), w_in[0], w_out[0], tr(w_up)[0], w_down[0], tr_cw(conv_w))
    cw_g = jnp.transpose(cw_sh, (1, 0, 2)).reshape(8, 2 * D_FF)
    x2, mixcat, o, sprev, wup_g = _fwd_mix(xs, proj, wout_g, grn, lng, lnb, ws, bsb, su)
    h2, up_pre, u_conv, act, x3, loss_parts = _fwd_ffn(x2, ffn_norm_g, wup_g, cw_g, conv_b, wdn_g, gf, tgt)

    dx3, dpre, dx2, dgf, dg2, dcv = _bwd_ffn(x3, tgt, gf, x2, ffn_norm_g, up_pre, u_conv, wup_g, cw_g, wdn_g)
    band = 512
    (gdn_p,) = _wgrad("wgrad_down", act, dx3, tm=FF_TILE, tn=512)
    (gout_p,) = _wgrad("wgrad_out", mixcat, dx2, tm=512, tn=512)
    gup_p, g_dn = _wgrad("wgrad_up", dpre, h2, tm=FF_TILE, tn=512, hosted=[(W_DOWN, gdn_p)])
    dproj, dgrn, dlng, dlnb, dws, dbs, g_up_a, g_out = _bwd_mix(
        dx2, proj, o, sprev, wout_g, grn, lng, lnb, ws, bsb, cos2, sin2,
        [((W_UP, 0, band), gup_p), (W_OUT, gout_p)])
    gin_p, g_up_b = _wgrad("wgrad_in", h1, dproj, tm=512, tn=768, hosted=[((W_UP, band, FF_SHARD - band), gup_p)])
    grad_x, g_in, rp, rws, rcv = _bwd_proj(dproj, win_g, xs, mix_norm_g, dx2, gin_p,
                                           (dg2, dgf, dgrn, dlng, dlnb, dbs, loss_parts, dws, dcv))
    gcw = tr_cw(lax.dynamic_slice(rcv, (me // (N_DEV // 2), 0, (me % (N_DEV // 2)) * FF_SHARD), (1, 3, FF_SHARD)))

    table = {}
    big, (grad_x,) = _adamw([(w_in, [g_in], m_w_in, v_w_in), (w_out, [g_out], m_w_out, v_w_out),
                             (tr(w_up), [g_up_a, g_up_b], tr(m_w_up), tr(v_w_up)), (w_down, [g_dn], m_w_down, v_w_down)],
                            [grad_x], n_steps=4)
    table.update(zip(("w_in", "w_out", "w_up", "w_down"), big))
    table["w_up"] = tuple(tr(a) for a in table["w_up"])
    row = lambda a: a.reshape(1, D_MODEL)
    names_small = ["mix_norm_g", "ffn_norm_g", "final_norm_g", "ret_norm_g", "sgu_ln_g", "sgu_ln_b", "sgu_b_s", "sgu_w_s",
                   "conv_w", "conv_b"]
    params = [(mix_norm_g, m_mix_norm_g, v_mix_norm_g), (ffn_norm_g, m_ffn_norm_g, v_ffn_norm_g),
              (row(final_norm_g), row(m_final_norm_g), row(v_final_norm_g)), (ret_norm_g, m_ret_norm_g, v_ret_norm_g),
              (sgu_ln_g, m_sgu_ln_g, v_sgu_ln_g), (sgu_ln_b, m_sgu_ln_b, v_sgu_ln_b), (sgu_b_s, m_sgu_b_s, v_sgu_b_s),
              (sgu_w_s, m_sgu_w_s, v_sgu_w_s), (tr_cw(conv_w), tr_cw(m_conv_w), tr_cw(v_conv_w)), (conv_b, m_conv_b, v_conv_b)]
    small, loss = _adamw_small(rp, rws, rcv, gcw, params)
    for n, res in zip(names_small, small):
        table[n] = res
    table["final_norm_g"] = tuple(a.reshape(D_MODEL) for a in table["final_norm_g"])
    table["conv_w"] = tuple(tr_cw(a) for a in table["conv_w"])

    order = ["mix_norm_g", "w_in", "ret_norm_g", "sgu_ln_g", "sgu_ln_b", "sgu_w_s", "sgu_b_s", "w_out", "ffn_norm_g", "w_up",
             "conv_w", "conv_b", "w_down", "final_norm_g"]
    outs = [loss.reshape(()), grad_x[None]]
    for col in range(4):
        outs += [table[n][col] for n in order]
    return tuple(outs)
```

```python
import functools
import math

import jax
import jax.numpy as jnp
import numpy as np
from jax import lax
from jax.experimental import pallas as pl
from jax.experimental.pallas import tpu as pltpu

F32 = jnp.float32
BF16 = jnp.bfloat16
MESH = pl.DeviceIdType.MESH

N_DEV = 8
SEQ = 2048
D_MODEL = 1024
CHUNK = 128
N_CHUNK = SEQ // CHUNK
HEADS = 4
HEAD_DIM = 128
RET_W = 512
SGU_W = 512
PROJ_W = 3072
D_FF = 2816
FF_SHARD = 704
FF_TILE = 1408
FF_TILES = ((0, 2816),)
IN_SHARD = PROJ_W // N_DEV
OUT_SHARD = D_MODEL // N_DEV
DOWN_SHARD = D_FF // N_DEV
TM = 256
N_TB = SEQ // TM
FWD_PROJ_PASS_AT = 5
FWD_MIX_PASS_AT = 10
EPS = 1e-6
ROPE_BASE = 10000.0
K_SCALE = HEAD_DIM ** -0.5
INV_SQRT2 = 0.7071067811865476
INV_SQRT_2PI = 0.3989422804014327

ADAM_LR = 0.001
ADAM_B1 = 0.9
ADAM_B2 = 0.999
ADAM_EPS = 1e-08
ADAM_WD = 0.01
ADAM_STEP = 10

VMEM_LIMIT = 56 * 1024 * 1024


def _cparams(sem=None, vmem=VMEM_LIMIT, collective=None):
    return pltpu.CompilerParams(dimension_semantics=sem, vmem_limit_bytes=vmem, collective_id=collective)


COLLECTIVE = {name: k for k, name in enumerate(("fwd_proj", "fwd_mix", "wgrad_up", "bwd_mix", "wgrad_in", "bwd_proj"))}


class _Meet:
    def __init__(self, diagonal):
        x, y, c = lax.axis_index("x"), lax.axis_index("y"), lax.axis_index("c")
        self.peers = [(x, y, 1 - c), (1 - x, y, c), (x, 1 - y, c)] + ([(1 - x, 1 - y, c)] if diagonal else [])

    def signal(self):
        for peer in self.peers:
            pl.semaphore_signal(pltpu.get_barrier_semaphore(), inc=1, device_id=peer, device_id_type=MESH)

    def wait(self):
        pl.semaphore_wait(pltpu.get_barrier_semaphore(), len(self.peers))


def _resident(shape):
    nd = len(shape)
    return pl.BlockSpec(shape, lambda *_: (0,) * nd, pipeline_mode=pl.Buffered(1))


def _dot(a, b):
    return jnp.dot(a, b, preferred_element_type=F32)


def _dot_nt(a, b):
    return lax.dot_general(a, b, (((1,), (1,)), ((), ())), preferred_element_type=F32)


def _dot_tn(a, b):
    return lax.dot_general(a, b, (((0,), (0,)), ((), ())), preferred_element_type=F32)


def _sigmoid(x):
    return 1.0 / (1.0 + jnp.exp(-x))


def _gelu(x):
    return 0.5 * x * (1.0 + lax.erf(x * INV_SQRT2))


def _gelu_grad(x):
    return 0.5 * (1.0 + lax.erf(x * INV_SQRT2)) + x * (jnp.exp(-0.5 * x * x) * INV_SQRT_2PI)


def _rot(xh, cos2, sin2):
    return xh * cos2 + pltpu.roll(xh, HEAD_DIM // 2, 1) * sin2


def _rot_t(dh, cos2, sin2):
    return dh * cos2 + pltpu.roll(dh * sin2, HEAD_DIM // 2, 1)


def _rope_freq():
    half = HEAD_DIM // 2
    inv_freq = jnp.power(ROPE_BASE, -jnp.arange(half, dtype=F32) / half)
    return jnp.concatenate([inv_freq, inv_freq])[None, :]


def _rope_block(inv2, first_row):
    pos = (lax.broadcasted_iota(jnp.int32, (TM, HEAD_DIM), 0) + first_row).astype(F32)
    ang = pos * inv2
    sin = jnp.sin(ang)
    lane = lax.broadcasted_iota(jnp.int32, (TM, HEAD_DIM), 1)
    return jnp.cos(ang), jnp.where(lane < HEAD_DIM // 2, -sin, sin)


def _log_gamma():
    return np.log(np.float32(1.0) - np.power(np.float32(2.0), -5.0 - np.arange(HEADS, dtype=np.float32))).astype(np.float32)


def _fill_decay(mask_ref, qd_ref, kd_ref):
    assert HEAD_DIM == CHUNK
    lg = _log_gamma()
    t = lax.broadcasted_iota(jnp.int32, (CHUNK, CHUNK), 0).astype(F32)
    diff = t - lax.broadcasted_iota(jnp.int32, (CHUNK, CHUNK), 1).astype(F32)
    for h in range(HEADS):
        mask_ref[h] = jnp.where(diff >= 0.0, jnp.exp(float(lg[h]) * jnp.maximum(diff, 0.0)), 0.0)
        qd_ref[h] = jnp.exp(float(lg[h]) * (t + 1.0))
        kd_ref[h] = jnp.exp(float(lg[h]) * (CHUNK - 1.0 - t))


def _chunk_decay():
    lg = _log_gamma()
    return [float(np.exp(lg[h] * np.float32(CHUNK))) for h in range(HEADS)]


W_IN, W_OUT, W_UP, W_DOWN, W_CONV = range(5)
GATHERED = {W_IN: ((D_MODEL, PROJ_W), BF16), W_OUT: ((D_MODEL, D_MODEL), BF16), W_UP: ((2 * D_FF, D_MODEL), BF16),
            W_DOWN: ((D_FF, D_MODEL), BF16), W_CONV: ((N_DEV, 8, FF_SHARD), F32)}
SHARD = {W_IN: (D_MODEL, IN_SHARD), W_OUT: (OUT_SHARD, D_MODEL), W_UP: (FF_SHARD, D_MODEL), W_DOWN: (DOWN_SHARD, D_MODEL),
         W_CONV: (8, FF_SHARD)}


class _Gather:
    N_SEMS = 9

    def __init__(self, ids, stages, gathered, send_sems, recv_sems, local_sems):
        self.ids, self.stages, self.gathered = ids, stages, gathered
        self.send_sems, self.recv_sems, self.local_sems = send_sems, recv_sems, local_sems
        self.x, self.y, self.c = lax.axis_index("x"), lax.axis_index("y"), lax.axis_index("c")
        self.me = (self.x, self.y, self.c)
        self.sibling = (self.x, self.y, 1 - self.c)
        self.chips = [(1 - self.x, self.y), (self.x, 1 - self.y), (1 - self.x, 1 - self.y)]

    def slot(self, n, px, py, pc):
        dev = 4 * px + 2 * py + pc
        w, g = self.ids[n], self.gathered[n]
        if w == W_IN:
            return g.at[:, pl.ds(pl.multiple_of(dev * IN_SHARD, 128), IN_SHARD)]
        if w == W_OUT:
            return g.at[pl.ds(pl.multiple_of(dev * OUT_SHARD, 128), OUT_SHARD), :]
        if w == W_DOWN:
            return g.at[pl.ds(pl.multiple_of(dev * DOWN_SHARD, 32), DOWN_SHARD), :]
        if w == W_UP:
            return g.at[pl.ds(pl.multiple_of(dev * FF_SHARD, 32), FF_SHARD), :]
        return g.at[dev]

    def half(self, n, px, py, pc, h):
        dev = 4 * px + 2 * py + pc
        w, g = self.ids[n], self.gathered[n]
        if w == W_IN:
            return g.at[pl.ds(h * (D_MODEL // 2), D_MODEL // 2), pl.ds(pl.multiple_of(dev * IN_SHARD, 128), IN_SHARD)]
        rows = SHARD[w][0] // 2
        return g.at[pl.ds(pl.multiple_of(dev * SHARD[w][0] + h * rows, 16), rows), :]

    def tree(self, n):
        return self.ids[n] != W_CONV

    def copy(self, n, k, block, to, src=None, h=None):
        ref = self.slot(n, *block) if h is None else self.half(n, *block, h)
        return pltpu.make_async_remote_copy(
            src_ref=ref if src is None else src, dst_ref=ref,
            send_sem=self.send_sems.at[n, k], recv_sem=self.recv_sems.at[n, k], device_id=to, device_id_type=MESH)

    def _mine(self):
        return [pltpu.make_async_copy(self.stages[n], self.slot(n, *self.me), self.local_sems.at[n]) for n in range(len(self.ids))]

    def _first(self):
        out = []
        for n in range(len(self.ids)):
            out.append(self.copy(n, 0, self.me, self.sibling, src=self.stages[n]))
            out += [self.copy(n, 1 + j, self.me, (*chip, self.c), src=self.stages[n])
                    for j, chip in enumerate(self.chips[:2] if self.tree(n) else self.chips)]
        return out

    def start(self):
        for cp in self._mine() + self._first():
            cp.start()

    def _passed(self, j):
        dev = (*self.chips[j], self.c)
        out = []
        for n in range(len(self.ids)):
            if not self.tree(n):
                out.append(self.copy(n, 4 + j, dev, self.sibling))
            elif j < 2:
                out += [self.copy(n, 3 + j, dev, (*self.chips[1 - j], self.c), h=j), self.copy(n, 5 + j, dev, self.sibling)]
            else:
                out += [self.copy(n, 7, dev, self.sibling, h=0), self.copy(n, 8, dev, self.sibling, h=1)]
        return out

    def near(self):
        for j in range(2):
            dev = (*self.chips[j], self.c)
            for n in range(len(self.ids)):
                self.copy(n, 1 + j, dev, self.me).wait_recv()
            for cp in self._passed(j):
                cp.start()

    def finish(self):
        dev = (*self.chips[2], self.c)
        for n in range(len(self.ids)):
            if self.tree(n):
                self.copy(n, 3, dev, self.me, h=0).wait_recv()
                self.copy(n, 4, dev, self.me, h=1).wait_recv()
            else:
                self.copy(n, 3, dev, self.me).wait_recv()
        for cp in self._passed(2):
            cp.start()
        for n in range(len(self.ids)):
            self.copy(n, 0, self.sibling, self.me).wait_recv()
            for j, chip in enumerate(self.chips):
                dev = (*chip, 1 - self.c)
                if not self.tree(n):
                    self.copy(n, 4 + j, dev, self.me).wait_recv()
                elif j < 2:
                    self.copy(n, 5 + j, dev, self.me).wait_recv()
                else:
                    self.copy(n, 7, dev, self.me, h=0).wait_recv()
                    self.copy(n, 8, dev, self.me, h=1).wait_recv()
        for cp in self._mine():
            cp.wait()
        for cp in self._first() + self._passed(0) + self._passed(1) + self._passed(2):
            cp.wait_send()


def _gather_scratch(n):
    return [pltpu.SemaphoreType.DMA((n, _Gather.N_SEMS)), pltpu.SemaphoreType.DMA((n, _Gather.N_SEMS)), pltpu.SemaphoreType.DMA((n,))]


def _gathered_shapes(ids):
    return tuple(jax.ShapeDtypeStruct(*GATHERED[w]) for w in ids)


def _fwd_proj(x, g1, inv2, w_in, w_out, w_up, w_down, conv_w):
    ids_a, ids_b = [W_IN, W_CONV], [W_OUT, W_DOWN]

    def body(x_ref, g_ref, inv_ref, in_hbm, out_hbm, up_hbm, dn_hbm, cw_ref,
             proj_ref, h1_ref, cos_ref, sin_ref, gin, gcw, gout, gdn, su_ref,
             w_vm, s_in, s_cw, s_out, s_dn, f_in, f_out, f_up, f_dn, ld_sems,
             a_send, a_recv, a_local, b_send, b_recv, b_local):
        ag_a = _Gather(ids_a, [s_in, s_cw], [gin, gcw], a_send, a_recv, a_local)
        ag_b = _Gather(ids_b, [s_out, s_dn], [gout, gdn], b_send, b_recv, b_local)

        @pl.when(pl.program_id(0) == 0)
        def _():
            meet = _Meet(diagonal=True)
            meet.signal()
            loads = [pltpu.make_async_copy(src, dst, ld_sems.at[i])
                     for i, (src, dst) in enumerate(((in_hbm, f_in), (out_hbm, f_out), (dn_hbm, f_dn), (up_hbm, f_up)))]
            for cp in loads:
                cp.start()
            s_cw[...] = jnp.zeros_like(s_cw)
            for k in range(3):
                s_cw[k:k + 1, :] = cw_ref[k]
            loads[0].wait()
            s_in[...] = f_in[...].astype(BF16)
            meet.wait()
            ag_a.start()
            loads[1].wait()
            s_out[...] = f_out[...].astype(BF16)
            loads[2].wait()
            s_dn[...] = f_dn[...].astype(BF16)
            ag_a.near()
            ag_b.start()
            loads[3].wait()
            su_ref[...] = f_up[...].astype(BF16)
            ag_a.finish()
            fill = pltpu.make_async_copy(gin, w_vm, ld_sems.at[4])
            fill.start()
            fill.wait()

        pl.when(pl.program_id(0) == FWD_PROJ_PASS_AT)(ag_b.near)

        xb = x_ref[...]
        r = lax.rsqrt(jnp.mean(xb * xb, axis=-1, keepdims=True) + EPS)
        h = ((xb * r) * g_ref[...]).astype(BF16)
        h1_ref[...] = h
        p = _dot(h, w_vm[...])
        c2, s2 = _rope_block(inv_ref[...], pl.program_id(0) * TM)
        cos_ref[...], sin_ref[...] = c2, s2
        for hd in range(HEADS):
            sl = slice(hd * HEAD_DIM, (hd + 1) * HEAD_DIM)
            proj_ref[:, sl] = _rot(p[:, sl], c2, s2)
            ks = slice(RET_W + hd * HEAD_DIM, RET_W + (hd + 1) * HEAD_DIM)
            proj_ref[:, ks] = _rot(p[:, ks], c2, s2) * K_SCALE
        proj_ref[:, 2 * RET_W:] = p[:, 2 * RET_W:]

        pl.when(pl.program_id(0) == N_TB - 1)(ag_b.finish)

    tok = lambda w: pl.BlockSpec((TM, w), lambda i: (i, 0))
    hbm = pl.BlockSpec(memory_space=pl.ANY)
    vm = pl.BlockSpec(memory_space=pltpu.VMEM)
    return pl.pallas_call(
        body, name="fwd_proj", grid=(N_TB,),
        out_shape=(jax.ShapeDtypeStruct((SEQ, PROJ_W), F32), jax.ShapeDtypeStruct((SEQ, D_MODEL), BF16),
                   jax.ShapeDtypeStruct((SEQ, HEAD_DIM), F32), jax.ShapeDtypeStruct((SEQ, HEAD_DIM), F32))
        + _gathered_shapes(ids_a + ids_b) + (jax.ShapeDtypeStruct(SHARD[W_UP], BF16),),
        in_specs=[tok(D_MODEL), _resident((1, D_MODEL)), _resident((1, HEAD_DIM)), hbm, hbm, hbm, hbm, vm],
        out_specs=(tok(PROJ_W), tok(D_MODEL), tok(HEAD_DIM), tok(HEAD_DIM), hbm, hbm, hbm, hbm, vm),
        scratch_shapes=[pltpu.VMEM((D_MODEL, PROJ_W), BF16), pltpu.VMEM(SHARD[W_IN], BF16), pltpu.VMEM(SHARD[W_CONV], F32),
                        pltpu.VMEM(SHARD[W_OUT], BF16), pltpu.VMEM(SHARD[W_DOWN], BF16),
                        pltpu.VMEM(SHARD[W_IN], F32), pltpu.VMEM(SHARD[W_OUT], F32), pltpu.VMEM(SHARD[W_UP], F32),
                        pltpu.VMEM(SHARD[W_DOWN], F32), pltpu.SemaphoreType.DMA((5,))]
        + _gather_scratch(len(ids_a)) + _gather_scratch(len(ids_b)),
        compiler_params=_cparams(("arbitrary",), collective=COLLECTIVE["fwd_proj"]),
    )(x, g1, inv2, w_in, w_out, w_up, w_down, conv_w)


def _causal(w):
    r = lax.broadcasted_iota(jnp.int32, (CHUNK, CHUNK), 0)
    c = lax.broadcasted_iota(jnp.int32, (CHUNK, CHUNK), 1)
    return jnp.where(r >= c, w, 0.0)


def _fwd_mix(x, proj, wout_g, grn, lng, lnb, ws, bsb, su):
    cdec = _chunk_decay()
    ids = [W_UP]

    def body(x_ref, p_ref, w_ref, grn_ref, lng_ref, lnb_ref, ws_ref, bsb_ref, su_ref,
             x2_ref, cat_ref, o_ref, sp_ref, gup, state, m_ref, qd_ref, kd_ref, send_sems, recv_sems, local_sems):
        ag = _Gather(ids, [su_ref], [gup], send_sems, recv_sems, local_sems)

        @pl.when(pl.program_id(0) == 0)
        def _():
            meet = _Meet(diagonal=False)
            meet.signal()
            state[...] = jnp.zeros_like(state)
            _fill_decay(m_ref, qd_ref, kd_ref)
            meet.wait()
            ag.start()

        for h in range(HEADS):
            sl = slice(h * HEAD_DIM, (h + 1) * HEAD_DIM)
            q = p_ref[:, sl]
            k = p_ref[:, RET_W + h * HEAD_DIM:RET_W + (h + 1) * HEAD_DIM]
            v = p_ref[:, 2 * RET_W + h * HEAD_DIM:2 * RET_W + (h + 1) * HEAD_DIM]
            g = p_ref[:, 3 * RET_W + h * HEAD_DIM:3 * RET_W + (h + 1) * HEAD_DIM]
            qb, kb, vb = q.astype(BF16), k.astype(BF16), v.astype(BF16)
            a = _dot_nt(qb, kb) * m_ref[h]
            spb = state[h].astype(BF16)
            sp_ref[0, h] = spb
            o = _dot(a.astype(BF16), vb) + _dot((q * qd_ref[h]).astype(BF16), spb)
            state[h] = state[h] * cdec[h] + _dot_tn((k * kd_ref[h]).astype(BF16), vb)
            o_ref[:, sl] = o
            rinv = lax.rsqrt(jnp.mean(o * o, axis=-1, keepdims=True) + EPS)
            rn = (o * rinv) * grn_ref[:, sl]
            cat_ref[:, sl] = ((g * _sigmoid(g)) * rn).astype(BF16)
        for gi in range(HEADS):
            sl = slice(gi * HEAD_DIM, (gi + 1) * HEAD_DIM)
            u = p_ref[:, 4 * RET_W + gi * HEAD_DIM:4 * RET_W + (gi + 1) * HEAD_DIM]
            sv = p_ref[:, 4 * RET_W + SGU_W + gi * HEAD_DIM:4 * RET_W + SGU_W + (gi + 1) * HEAD_DIM]
            gv = _gelu(sv)
            xc = gv - jnp.mean(gv, axis=-1, keepdims=True)
            vn = (xc * lax.rsqrt(jnp.mean(xc * xc, axis=-1, keepdims=True) + EPS)) * lng_ref[:, sl] + lnb_ref[:, sl]
            mixed = _dot(_causal(ws_ref[gi]).astype(BF16), vn.astype(BF16)) + bsb_ref[gi]
            cat_ref[:, RET_W + gi * HEAD_DIM:RET_W + (gi + 1) * HEAD_DIM] = (_gelu(u) * mixed).astype(BF16)
        x2_ref[...] = x_ref[...] + _dot(cat_ref[...], w_ref[...])

        pl.when(pl.program_id(0) == FWD_MIX_PASS_AT)(ag.near)
        pl.when(pl.program_id(0) == N_CHUNK - 1)(ag.finish)

    ch = lambda w: pl.BlockSpec((CHUNK, w), lambda i: (i, 0))
    hcc = (HEADS, CHUNK, CHUNK)
    hbm = pl.BlockSpec(memory_space=pl.ANY)
    return pl.pallas_call(
        body, name="fwd_mix", grid=(N_CHUNK,),
        out_shape=(jax.ShapeDtypeStruct((SEQ, D_MODEL), F32), jax.ShapeDtypeStruct((SEQ, D_MODEL), BF16),
                   jax.ShapeDtypeStruct((SEQ, RET_W), F32), jax.ShapeDtypeStruct((N_CHUNK, HEADS, HEAD_DIM, HEAD_DIM), BF16))
        + _gathered_shapes(ids),
        in_specs=[ch(D_MODEL), ch(PROJ_W), _resident((D_MODEL, D_MODEL)), _resident((1, RET_W)), _resident((1, SGU_W)),
                  _resident((1, SGU_W)), _resident(hcc), _resident(hcc), hbm],
        out_specs=(ch(D_MODEL), ch(D_MODEL), ch(RET_W), pl.BlockSpec((1, HEADS, HEAD_DIM, HEAD_DIM), lambda i: (i, 0, 0, 0)), hbm),
        scratch_shapes=[pltpu.VMEM((HEADS, HEAD_DIM, HEAD_DIM), F32)] + [pltpu.VMEM(hcc, F32)] * 3 + _gather_scratch(len(ids)),
        compiler_params=_cparams(("arbitrary",), collective=COLLECTIVE["fwd_mix"]),
    )(x, proj, wout_g, grn, lng, lnb, ws, bsb, su)


def _conv_taps(p, prev8):
    row = lax.broadcasted_iota(jnp.int32, p.shape, 0)
    p1 = jnp.where(row == 0, prev8[7:8, :], pltpu.roll(p, 1, 0))
    p2 = jnp.where(row == 0, prev8[6:7, :], jnp.where(row == 1, prev8[7:8, :], pltpu.roll(p, 2, 0)))
    return p1, p2


def _fwd_ffn(x2, g2, wup_g, cw_g, cb_g, wdn_g, gf, tgt):
    def body(x_ref, g_ref, wu_ref, cw_ref, cb_ref, wd_ref, gf_ref, t_ref, h2_ref, up_ref, u_ref, act_ref, x3_ref, loss_ref, carry):
        @pl.when(pl.program_id(0) == 0)
        def _():
            carry[...] = jnp.zeros_like(carry)

        xb = x_ref[...]
        r = lax.rsqrt(jnp.mean(xb * xb, axis=-1, keepdims=True) + EPS)
        h = ((xb * r) * g_ref[...]).astype(BF16)
        h2_ref[...] = h
        acc = xb
        for t0, tw in FF_TILES:
            u = []
            for c0 in (t0, D_FF + t0):
                cs = slice(c0, c0 + tw)
                p = _dot_nt(h, wu_ref[pl.ds(c0, tw), :])
                up_ref[:, cs] = p.astype(BF16)
                p1, p2 = _conv_taps(p, carry[:, cs])
                carry[:, cs] = p[TM - 8:, :]
                us = p2 * cw_ref[0:1, cs] + p1 * cw_ref[1:2, cs] + p * cw_ref[2:3, cs] + cb_ref[:, cs]
                u_ref[:, cs] = us.astype(BF16)
                u.append(us)
            a = ((u[0] * _sigmoid(u[0])) * u[1]).astype(BF16)
            act_ref[:, t0:t0 + tw] = a
            acc = acc + _dot(a, wd_ref[pl.ds(t0, tw), :])
        x3_ref[...] = acc
        r3 = lax.rsqrt(jnp.mean(acc * acc, axis=-1, keepdims=True) + EPS)
        diff = (acc * r3) * gf_ref[...] - t_ref[...]
        loss_ref[...] = jnp.full(loss_ref.shape, 0.5 * jnp.sum(jnp.mean(diff * diff, axis=-1)), F32)

    tok = lambda w: pl.BlockSpec((TM, w), lambda i: (i, 0))
    return pl.pallas_call(
        body, name="fwd_ffn", grid=(N_TB,),
        out_shape=(jax.ShapeDtypeStruct((SEQ, D_MODEL), BF16), jax.ShapeDtypeStruct((SEQ, 2 * D_FF), BF16),
                   jax.ShapeDtypeStruct((SEQ, 2 * D_FF), BF16),
                   jax.ShapeDtypeStruct((SEQ, D_FF), BF16), jax.ShapeDtypeStruct((SEQ, D_MODEL), F32),
                   jax.ShapeDtypeStruct((N_TB, 8, 128), F32)),
        in_specs=[tok(D_MODEL), _resident((1, D_MODEL)), _resident((2 * D_FF, D_MODEL)), _resident((8, 2 * D_FF)),
                  _resident((1, 2 * D_FF)), _resident((D_FF, D_MODEL)), _resident((1, D_MODEL)), tok(D_MODEL)],
        out_specs=(tok(D_MODEL), tok(2 * D_FF), tok(2 * D_FF), tok(D_FF), tok(D_MODEL),
                   pl.BlockSpec((1, 8, 128), lambda i: (i, 0, 0))),
        scratch_shapes=[pltpu.VMEM((8, 2 * D_FF), F32)],
        compiler_params=_cparams(("arbitrary",)),
    )(x2, g2, wup_g, cw_g, cb_g, wdn_g, gf, tgt)


def _bwd_ffn(x3, tgt, gf, x2, g2, up_pre, u_conv, wup_g, cw_g, wdn_g):
    def body(x3_ref, t_ref, gf_ref, x2_ref, g2_ref, up_ref, u_ref, wu_ref, cw_ref, wd_ref,
             dx3_ref, dpre_ref, dx2_ref, dgf_ref, dg2_ref, dcv_ref, nxt):
        i = pl.program_id(0)

        @pl.when(i == 0)
        def _():
            nxt[...] = jnp.zeros_like(nxt)
            dgf_ref[...] = jnp.zeros_like(dgf_ref)
            dg2_ref[...] = jnp.zeros_like(dg2_ref)
            dcv_ref[...] = jnp.zeros_like(dcv_ref)

        x3 = x3_ref[...]
        r3 = lax.rsqrt(jnp.mean(x3 * x3, axis=-1, keepdims=True) + EPS)
        xh3 = x3 * r3
        dy = (xh3 * gf_ref[...] - t_ref[...]) * (1.0 / D_MODEL)
        dgf_ref[0:1, :] += jnp.sum(dy * xh3, axis=0, keepdims=True)
        t3 = dy * gf_ref[...]
        dx3 = r3 * (t3 - xh3 * jnp.mean(t3 * xh3, axis=-1, keepdims=True))
        dx3b = dx3.astype(BF16)
        dx3_ref[...] = dx3b
        dh2 = jnp.zeros((TM, D_MODEL), F32)
        for t0, tw in FF_TILES:
            row = lax.broadcasted_iota(jnp.int32, (TM, tw), 0)
            ts = slice(t0, t0 + tw)
            dact = _dot_nt(dx3b, wd_ref[pl.ds(t0, tw), :])
            ua = u_ref[:, ts].astype(F32)
            ub = u_ref[:, D_FF + t0:D_FF + t0 + tw].astype(F32)
            sg = _sigmoid(ua)
            du = [dact * ub * (sg * (1.0 + ua * (1.0 - sg))), dact * (ua * sg)]
            for n in range(2):
                d = du[n]
                c0 = n * D_FF + t0
                cs = slice(c0, c0 + tw)
                nx = nxt[:, cs]
                n1 = jnp.where(row == TM - 1, nx[0:1, :], pltpu.roll(d, TM - 1, 0))
                n2 = jnp.where(row == TM - 2, nx[0:1, :], jnp.where(row == TM - 1, nx[1:2, :], pltpu.roll(d, TM - 2, 0)))
                nxt[:, cs] = d[0:8, :]
                dp = (d * cw_ref[2:3, cs] + n1 * cw_ref[1:2, cs] + n2 * cw_ref[0:1, cs]).astype(BF16)
                dpre_ref[:, cs] = dp
                p = up_ref[:, cs].astype(F32)
                dcv_ref[n, 0:1, ts] += jnp.sum(n2 * p, axis=0, keepdims=True)
                dcv_ref[n, 1:2, ts] += jnp.sum(n1 * p, axis=0, keepdims=True)
                dcv_ref[n, 2:3, ts] += jnp.sum(d * p, axis=0, keepdims=True)
                dcv_ref[n, 3:4, ts] += jnp.sum(d, axis=0, keepdims=True)
                dh2 = dh2 + _dot(dp, wu_ref[pl.ds(c0, tw), :])
        x2 = x2_ref[...]
        r2 = lax.rsqrt(jnp.mean(x2 * x2, axis=-1, keepdims=True) + EPS)
        xh2 = x2 * r2
        dg2_ref[0:1, :] += jnp.sum(dh2 * xh2, axis=0, keepdims=True)
        t2 = dh2 * g2_ref[...]
        dx2_ref[...] = dx3 + r2 * (t2 - xh2 * jnp.mean(t2 * xh2, axis=-1, keepdims=True))

    rev = lambda w: pl.BlockSpec((TM, w), lambda i: (N_TB - 1 - i, 0))
    acc = lambda s: pl.BlockSpec(s, lambda i: (0,) * len(s))
    return pl.pallas_call(
        body, name="bwd_ffn", grid=(N_TB,),
        out_shape=(jax.ShapeDtypeStruct((SEQ, D_MODEL), BF16), jax.ShapeDtypeStruct((SEQ, 2 * D_FF), BF16),
                   jax.ShapeDtypeStruct((SEQ, D_MODEL), F32), jax.ShapeDtypeStruct((8, D_MODEL), F32),
                   jax.ShapeDtypeStruct((8, D_MODEL), F32), jax.ShapeDtypeStruct((2, 8, D_FF), F32)),
        in_specs=[rev(D_MODEL), rev(D_MODEL), _resident((1, D_MODEL)), rev(D_MODEL), _resident((1, D_MODEL)), rev(2 * D_FF),
                  rev(2 * D_FF), _resident((2 * D_FF, D_MODEL)), _resident((8, 2 * D_FF)), _resident((D_FF, D_MODEL))],
        out_specs=(rev(D_MODEL), rev(2 * D_FF), rev(D_MODEL), acc((8, D_MODEL)), acc((8, D_MODEL)), acc((2, 8, D_FF))),
        scratch_shapes=[pltpu.VMEM((8, 2 * D_FF), F32)],
        compiler_params=_cparams(("arbitrary",)),
    )(x3, tgt, gf, x2, g2, up_pre, u_conv, wup_g, cw_g, wdn_g)


def _bwd_mix(dx2, proj, o, sprev, wout_g, grn, lng, lnb, ws, bsb, cos2, sin2, hosted):
    cdec = _chunk_decay()
    geoms = [g for g, _ in hosted]
    n_h = len(hosted)

    def body(dx2_ref, p_ref, o_ref, sp_ref, w_ref, grn_ref, lng_ref, lnb_ref, ws_ref, bsb_ref, cos_ref, sin_ref, *rest):
        dp_ref, dgrn_ref, dlng_ref, dlnb_ref, dws_ref, dbs_ref = rest[n_h:n_h + 6]
        dstate, dbs_acc, m_ref, qd_ref, kd_ref = rest[2 * n_h + 6:2 * n_h + 11]
        i = pl.program_id(0)
        rs = _Scatters(geoms, rest[:n_h], rest[n_h + 6:2 * n_h + 6], rest[2 * n_h + 11:])
        pl.when(i == 0)(rs.phase1)
        pl.when(i == 3)(rs.phase2)
        pl.when(i == 8)(rs.phase2b)

        @pl.when(i == 0)
        def _():
            _fill_decay(m_ref, qd_ref, kd_ref)
            dstate[...] = jnp.zeros_like(dstate)
            dgrn_ref[...] = jnp.zeros_like(dgrn_ref)
            dlng_ref[...] = jnp.zeros_like(dlng_ref)
            dlnb_ref[...] = jnp.zeros_like(dlnb_ref)
            dws_ref[...] = jnp.zeros_like(dws_ref)
            dbs_ref[...] = jnp.zeros_like(dbs_ref)
            dbs_acc[...] = jnp.zeros_like(dbs_acc)

        dmix = _dot_nt(dx2_ref[...].astype(BF16), w_ref[...])
        for h in range(HEADS):
            sl = slice(h * HEAD_DIM, (h + 1) * HEAD_DIM)
            q = p_ref[:, sl]
            k = p_ref[:, RET_W + h * HEAD_DIM:RET_W + (h + 1) * HEAD_DIM]
            v = p_ref[:, 2 * RET_W + h * HEAD_DIM:2 * RET_W + (h + 1) * HEAD_DIM]
            g = p_ref[:, 3 * RET_W + h * HEAD_DIM:3 * RET_W + (h + 1) * HEAD_DIM]
            o = o_ref[:, sl]
            rinv = lax.rsqrt(jnp.mean(o * o, axis=-1, keepdims=True) + EPS)
            oh = o * rinv
            gr = grn_ref[:, sl]
            sg = _sigmoid(g)
            dret = dmix[:, sl]
            dp_ref[:, 3 * RET_W + h * HEAD_DIM:3 * RET_W + (h + 1) * HEAD_DIM] = (
                dret * (oh * gr) * (sg * (1.0 + g * (1.0 - sg)))).astype(BF16)
            drn = dret * (g * sg)
            dgrn_ref[0:1, sl] += jnp.sum(drn * oh, axis=0, keepdims=True)
            t = drn * gr
            do = rinv * (t - oh * jnp.mean(t * oh, axis=-1, keepdims=True))
            qb, kb, vb, dob = q.astype(BF16), k.astype(BF16), v.astype(BF16), do.astype(BF16)
            m = m_ref[h]
            ab = (_dot_nt(qb, kb) * m).astype(BF16)
            dab = (_dot_nt(dob, vb) * m).astype(BF16)
            spb = sp_ref[0, h]
            dsn = dstate[h]
            dsnb = dsn.astype(BF16)
            qdb = (q * qd_ref[h]).astype(BF16)
            kdb = (k * kd_ref[h]).astype(BF16)
            dq = _dot(dab, kb) + _dot_nt(dob, spb) * qd_ref[h]
            dk = _dot_tn(dab, qb) + _dot_nt(vb, dsnb) * kd_ref[h]
            dv = _dot_tn(ab, dob) + _dot(kdb, dsnb)
            dstate[h] = dsn * cdec[h] + _dot_tn(qdb, dob)
            c2, s2 = cos_ref[...], sin_ref[...]
            dp_ref[:, sl] = _rot_t(dq, c2, s2).astype(BF16)
            dp_ref[:, RET_W + h * HEAD_DIM:RET_W + (h + 1) * HEAD_DIM] = _rot_t(dk * K_SCALE, c2, s2).astype(BF16)
            dp_ref[:, 2 * RET_W + h * HEAD_DIM:2 * RET_W + (h + 1) * HEAD_DIM] = dv.astype(BF16)
        for gi in range(HEADS):
            sl = slice(gi * HEAD_DIM, (gi + 1) * HEAD_DIM)
            u = p_ref[:, 4 * RET_W + gi * HEAD_DIM:4 * RET_W + (gi + 1) * HEAD_DIM]
            sv = p_ref[:, 4 * RET_W + SGU_W + gi * HEAD_DIM:4 * RET_W + SGU_W + (gi + 1) * HEAD_DIM]
            gv = _gelu(sv)
            xc = gv - jnp.mean(gv, axis=-1, keepdims=True)
            rstd = lax.rsqrt(jnp.mean(xc * xc, axis=-1, keepdims=True) + EPS)
            xh = xc * rstd
            lg = lng_ref[:, sl]
            vnb = (xh * lg + lnb_ref[:, sl]).astype(BF16)
            wcb = _causal(ws_ref[gi]).astype(BF16)
            mixed = _dot(wcb, vnb) + bsb_ref[gi]
            dsgu = dmix[:, RET_W + gi * HEAD_DIM:RET_W + (gi + 1) * HEAD_DIM]
            dmixed = dsgu * _gelu(u)
            dmb = dmixed.astype(BF16)
            dws_ref[gi] += _causal(_dot_nt(dmb, vnb))
            dbs_acc[gi] += dmixed
            dvn = _dot_tn(wcb, dmb)
            dlng_ref[gi:gi + 1, :] += jnp.sum(dvn * xh, axis=0, keepdims=True)
            dlnb_ref[gi:gi + 1, :] += jnp.sum(dvn, axis=0, keepdims=True)
            dxh = dvn * lg
            dgv = rstd * (dxh - jnp.mean(dxh, axis=-1, keepdims=True) - xh * jnp.mean(dxh * xh, axis=-1, keepdims=True))
            dp_ref[:, 4 * RET_W + gi * HEAD_DIM:4 * RET_W + (gi + 1) * HEAD_DIM] = (dsgu * mixed * _gelu_grad(u)).astype(BF16)
            dp_ref[:, 4 * RET_W + SGU_W + gi * HEAD_DIM:4 * RET_W + SGU_W + (gi + 1) * HEAD_DIM] = (
                dgv * _gelu_grad(sv)).astype(BF16)

        @pl.when(i == N_CHUNK - 1)
        def _():
            for gi in range(HEADS):
                col = jnp.broadcast_to(jnp.sum(dbs_acc[gi], axis=-1, keepdims=True), (CHUNK, CHUNK))
                dbs_ref[gi:gi + 1, :] = jnp.transpose(col)[0:1, :]
            rs.phase3()

    rev = lambda w: pl.BlockSpec((CHUNK, w), lambda i: (N_CHUNK - 1 - i, 0))
    hcc = (HEADS, CHUNK, CHUNK)
    acc = lambda s: pl.BlockSpec(s, lambda i: (0,) * len(s))
    res = pl.pallas_call(
        body, name="bwd_mix", grid=(N_CHUNK,),
        out_shape=(jax.ShapeDtypeStruct((SEQ, PROJ_W), BF16), jax.ShapeDtypeStruct((8, RET_W), F32),
                   jax.ShapeDtypeStruct((8, HEAD_DIM), F32), jax.ShapeDtypeStruct((8, HEAD_DIM), F32),
                   jax.ShapeDtypeStruct(hcc, F32), jax.ShapeDtypeStruct((8, CHUNK), F32)) + _scatter_out_shapes(geoms),
        in_specs=[rev(D_MODEL), rev(PROJ_W), rev(RET_W),
                  pl.BlockSpec((1, HEADS, HEAD_DIM, HEAD_DIM), lambda i: (N_CHUNK - 1 - i, 0, 0, 0)),
                  _resident((D_MODEL, D_MODEL)), _resident((1, RET_W)), _resident((1, SGU_W)), _resident((1, SGU_W)),
                  _resident(hcc), _resident(hcc), rev(HEAD_DIM), rev(HEAD_DIM)]
        + [pl.BlockSpec(memory_space=pl.ANY)] * n_h,
        out_specs=(rev(PROJ_W), acc((8, RET_W)), acc((8, HEAD_DIM)), acc((8, HEAD_DIM)), acc(hcc), acc((8, CHUNK)))
        + _scatter_out_specs(geoms),
        scratch_shapes=[pltpu.VMEM((HEADS, HEAD_DIM, HEAD_DIM), F32), pltpu.VMEM((HEADS, CHUNK, CHUNK), F32)]
        + [pltpu.VMEM(hcc, F32)] * 3 + _scatter_scratch(geoms),
        compiler_params=_cparams(("arbitrary",), collective=COLLECTIVE["bwd_mix"]),
    )(dx2, proj, o, sprev, wout_g, grn, lng, lnb, ws, bsb, cos2, sin2, *[p for _, p in hosted])
    return tuple(res[:6 + n_h])


def _bwd_proj(dproj, win_g, x, g1, dx2, gin_p, small):
    geoms = [W_IN]
    n_s = len(small)

    def body(dp_ref, w_ref, x_ref, g_ref, dx2_ref, gin_ref, *rest):
        small_refs = rest[:n_s]
        dx_ref, rs_out, rp_ref, rws_ref, rcv_ref, dg_ref = rest[n_s:n_s + 6]
        rs_scratch = rest[n_s + 6:n_s + 6 + N_SCATTER_SCRATCH]
        ar_scratch = rest[n_s + 6 + N_SCATTER_SCRATCH:]
        ar_res = ar_scratch[N_SMALL_SCRATCH:]
        ar = _SmallReduce((dg_ref,) + tuple(small_refs), ar_res, ar_scratch[:N_SMALL_SCRATCH])
        rs = _Scatters(geoms, [gin_ref], [rs_out], rs_scratch)
        pl.when(pl.program_id(0) == 0)(lambda: rs.phase1(diagonal=True))
        pl.when(pl.program_id(0) == 1)(rs.phase2)
        pl.when(pl.program_id(0) == 5)(rs.phase2b)

        @pl.when(pl.program_id(0) == 0)
        def _():
            dg_ref[...] = jnp.zeros_like(dg_ref)

        dh = _dot_nt(dp_ref[...], w_ref[...])
        xb = x_ref[...]
        r = lax.rsqrt(jnp.mean(xb * xb, axis=-1, keepdims=True) + EPS)
        xh = xb * r
        dg_ref[0:1, :] += jnp.sum(dh * xh, axis=0, keepdims=True)
        t = dh * g_ref[...]
        dx_ref[...] = dx2_ref[...] + r * (t - xh * jnp.mean(t * xh, axis=-1, keepdims=True))

        @pl.when(pl.program_id(0) == N_TB - 1)
        def _():
            ar.begin()
            rs.phase3()
            ar.end()
            for o_ref, r_ref in zip((rp_ref, rws_ref, rcv_ref), ar_res):
                o_ref[...] = r_ref[...]

    tok = lambda w: pl.BlockSpec((TM, w), lambda i: (i, 0))
    vm = pl.BlockSpec(memory_space=pltpu.VMEM)
    res = pl.pallas_call(
        body, name="bwd_proj", grid=(N_TB,),
        out_shape=(jax.ShapeDtypeStruct((SEQ, D_MODEL), F32),) + _scatter_out_shapes(geoms)
        + tuple(jax.ShapeDtypeStruct(s, F32) for s in SMALL_FULL),
        in_specs=[tok(PROJ_W), _resident((D_MODEL, PROJ_W)), tok(D_MODEL), _resident((1, D_MODEL)), tok(D_MODEL),
                  pl.BlockSpec(memory_space=pl.ANY)] + [vm] * n_s,
        out_specs=(tok(D_MODEL),) + _scatter_out_specs(geoms) + (vm,) * len(SMALL_FULL),
        scratch_shapes=[pltpu.VMEM((8, D_MODEL), F32)] + _scatter_scratch(geoms) + _small_scratch()
        + [pltpu.VMEM(s, F32) for s in SMALL_FULL],
        compiler_params=_cparams(("arbitrary",), collective=COLLECTIVE["bwd_proj"]),
    )(dproj, win_g, x, g1, dx2, gin_p, *small)
    return res


def _wgrad(name, a, b, tm=None, tn=None, hosted=()):
    m_w, n_w = a.shape[-1], b.shape[-1]
    tm = m_w if tm is None else tm
    tn = n_w if tn is None else tn
    n_steps = (m_w // tm) * (n_w // tn)
    geoms = [g for g, _ in hosted]
    n_h = len(hosted)

    def body(a_ref, b_ref, *rest):
        o_ref = rest[n_h]
        if n_h:
            rs = _Scatters(geoms, rest[:n_h], rest[n_h + 1:2 * n_h + 1], rest[2 * n_h + 1:])
            step = pl.program_id(0) * (n_w // tn) + pl.program_id(1)
            pl.when(step == 0)(rs.phase1)
            pl.when(step == 1)(rs.phase2)
            pl.when(step == n_steps // 2)(rs.phase2b)
        o_ref[...] = _dot_tn(a_ref[...].astype(BF16), b_ref[...].astype(BF16)).astype(BF16)
        if n_h:
            pl.when(step == n_steps - 1)(rs.phase3)

    assert not n_h or n_steps >= 4
    res = pl.pallas_call(
        body, name=name, grid=(m_w // tm, n_w // tn),
        out_shape=(jax.ShapeDtypeStruct((m_w, n_w), BF16),) + _scatter_out_shapes(geoms),
        in_specs=[pl.BlockSpec((SEQ, tm), lambda i, j: (0, i)), pl.BlockSpec((SEQ, tn), lambda i, j: (0, j))]
        + [pl.BlockSpec(memory_space=pl.ANY)] * n_h,
        out_specs=(pl.BlockSpec((tm, tn), lambda i, j: (i, j)),) + _scatter_out_specs(geoms),
        scratch_shapes=_scatter_scratch(geoms),
        compiler_params=_cparams(("arbitrary", "arbitrary"), collective=COLLECTIVE[name]) if n_h else _cparams(("parallel", "parallel")),
    )(a, b, *[p for _, p in hosted])
    return tuple(res[:1 + n_h])


def _row_step(half_rows):
    return max(s for s in range(16, 177, 16) if half_rows % s == 0)


class _Scatter:
    def __init__(self, geom, partial, out, land1, mine, stage2, land2, comb, s1_send, s1_recv, s2_send, s2_recv, ld_sems):
        self.w, self.row0, self.shape = _geom(geom)
        self.partial, self.out, self.land1 = partial, out, land1
        self.mine, self.stage2, self.land2, self.comb = mine, stage2, land2, comb
        self.hr = self.shape[0] // 2
        self.step = _row_step(self.hr)
        self.s1_send, self.s1_recv, self.s2_send, self.s2_recv, self.ld_sems = s1_send, s1_recv, s2_send, s2_recv, ld_sems
        self.x, self.y, self.c = lax.axis_index("x"), lax.axis_index("y"), lax.axis_index("c")
        self.sibling = (self.x, self.y, 1 - self.c)
        self.chips = [(self.x, self.y), (1 - self.x, self.y), (self.x, 1 - self.y), (1 - self.x, 1 - self.y)]

    def block(self, px, py, pc):
        dev = 4 * px + 2 * py + pc
        if self.w == W_IN:
            return self.partial.at[:, pl.ds(pl.multiple_of(dev * IN_SHARD, 128), IN_SHARD)]
        if self.w == W_OUT:
            return self.partial.at[pl.ds(pl.multiple_of(dev * OUT_SHARD, 128), OUT_SHARD), :]
        if self.w == W_DOWN:
            return self.partial.at[pl.ds(pl.multiple_of(dev * DOWN_SHARD, 32), DOWN_SHARD), :]
        return self.partial.at[pl.ds(pl.multiple_of(dev * FF_SHARD + self.row0, 32), self.shape[0]), :]

    def copy1(self, k):
        return pltpu.make_async_remote_copy(
            src_ref=self.block(*self.chips[k], 1 - self.c), dst_ref=self.land1.at[k],
            send_sem=self.s1_send.at[k], recv_sem=self.s1_recv.at[k], device_id=self.sibling, device_id_type=MESH)

    STAGE2 = [(1, 0, 1), (3, 0, 1), (2, 1, 2), (3, 1, 2), (1, 1, 1), (2, 0, 2)]

    def copy2(self, j):
        blk, h, to = self.STAGE2[j]
        src = self.comb.at[j - 4] if j >= 4 else self.stage2.at[blk - 1, pl.ds(h * self.hr, self.hr), :]
        return pltpu.make_async_remote_copy(
            src_ref=src, dst_ref=self.land2.at[j], send_sem=self.s2_send.at[j], recv_sem=self.s2_recv.at[j],
            device_id=(*self.chips[to], self.c), device_id_type=MESH)

    def _rows(self, h=None):
        step = self.step
        lo, n = (0, self.shape[0]) if h is None else (h * self.hr, self.hr)
        return [pl.ds(r0, step) for r0 in range(lo, lo + n, step)]

    def load(self, k):
        return pltpu.make_async_copy(self.block(*self.chips[k], self.c), self.mine.at[k], self.ld_sems.at[k])

    def load_mine(self):
        for k in range(4):
            self.load(k).start()

    def phase1(self):
        for k in range(4):
            self.copy1(k).start()

    def phase2(self, k):
        self.copy1(k).wait_recv()
        self.load(k).wait()
        for rs in self._rows():
            s = self.mine[k, rs, :].astype(F32) + self.land1[k, rs, :].astype(F32)
            if k == 0:
                self.out[rs, :] = s
            else:
                self.stage2[k - 1, rs, :] = s.astype(BF16)
        for j in {3: (1, 3), 1: (0,), 2: (2,), 0: ()}[k]:
            self.copy2(j).start()

    def phase2b(self):
        for j, got in ((4, 3), (5, 1)):
            blk, h, _ = self.STAGE2[j]
            self.copy2(got).wait_recv()
            for i, rs in enumerate(self._rows(h)):
                lr = pl.ds(i * self.step, self.step)
                self.comb[j - 4, lr, :] = (self.stage2[blk - 1, rs, :].astype(F32) + self.land2[got, lr, :].astype(F32)).astype(BF16)
            self.copy2(j).start()

    def phase3(self):
        for j in (0, 5, 4, 2):
            self.copy2(j).wait_recv()
        for h, (first, second) in enumerate(((0, 5), (4, 2))):
            for i, rs in enumerate(self._rows(h)):
                lr = pl.ds(i * self.step, self.step)
                self.out[rs, :] = (self.out[rs, :] + self.land2[first, lr, :].astype(F32)) + self.land2[second, lr, :].astype(F32)
        for k in range(4):
            self.copy1(k).wait_send()
        for j in range(6):
            self.copy2(j).wait_send()


def _geom(geom):
    if isinstance(geom, tuple):
        w, row0, rows = geom
        assert w == W_UP
        return w, row0, (rows, SHARD[w][1])
    return geom, 0, SHARD[geom]


N_SCATTER_SCRATCH = 10


def _scatter_out_shapes(geoms):
    return tuple(jax.ShapeDtypeStruct(_geom(g)[2], F32) for g in geoms)


def _scatter_out_specs(geoms):
    return (pl.BlockSpec(memory_space=pltpu.VMEM),) * len(geoms)


def _scatter_scratch(geoms):
    out = []
    for g in geoms:
        s = _geom(g)[2]
        hs = (s[0] // 2, s[1])
        out += [pltpu.VMEM((4,) + s, BF16), pltpu.VMEM((4,) + s, BF16), pltpu.VMEM((3,) + s, BF16), pltpu.VMEM((6,) + hs, BF16),
                pltpu.VMEM((2,) + hs, BF16),
                pltpu.SemaphoreType.DMA((4,)), pltpu.SemaphoreType.DMA((4,)), pltpu.SemaphoreType.DMA((6,)),
                pltpu.SemaphoreType.DMA((6,)), pltpu.SemaphoreType.DMA((4,))]
    return out


class _Scatters:
    def __init__(self, geoms, p_refs, out_refs, scratch):
        k = N_SCATTER_SCRATCH
        self.items = [_Scatter(g, p_refs[i], out_refs[i], *scratch[k * i:k * i + k]) for i, g in enumerate(geoms)]

    def phase1(self, diagonal=False):
        meet = _Meet(diagonal)
        meet.signal()
        for s in self.items:
            s.load_mine()
        meet.wait()
        for s in self.items:
            s.phase1()

    def phase2(self):
        for k in (3, 1, 2, 0):
            for s in self.items:
                s.phase2(k)

    def phase2b(self):
        for s in self.items:
            s.phase2b()

    def phase3(self):
        for s in self.items:
            s.phase3()


PACK_W = 1024


SMALL_FULL = [(2, 8, PACK_W), (HEADS, CHUNK, CHUNK), (2, 8, D_FF)]
SMALL_HALF = [(s[0] // 2,) + s[1:] for s in SMALL_FULL]
N_SMALL_SCRATCH = 16


def _small_scratch():
    n_a = len(SMALL_FULL)
    return ([pltpu.VMEM(SMALL_FULL[0], F32)] + [pltpu.VMEM(s, F32) for s in SMALL_HALF] + [pltpu.VMEM(s, F32) for s in SMALL_HALF]
            + [pltpu.VMEM((3,) + s, F32) for s in SMALL_HALF]
            + [pltpu.SemaphoreType.DMA((n_a,)), pltpu.SemaphoreType.DMA((n_a,)), pltpu.SemaphoreType.DMA((n_a, 3)),
               pltpu.SemaphoreType.DMA((n_a, 3)), pltpu.SemaphoreType.DMA((n_a,)), pltpu.SemaphoreType.DMA((n_a,))])


class _SmallReduce:
    def __init__(self, ins, outs, scratch):
        self.ins, self.outs = ins, outs
        (self.pack, *rest) = scratch
        self.rxs, self.css, self.gs = rest[0:3], rest[3:6], rest[6:9]
        self.s1_send, self.s1_recv, self.s2_send, self.s2_recv, self.s3_send, self.s3_recv = rest[9:]
        self.x, self.y, self.c = lax.axis_index("x"), lax.axis_index("y"), lax.axis_index("c")
        self.sibling = (self.x, self.y, 1 - self.c)
        self.chips = [(1 - self.x, self.y), (self.x, 1 - self.y), (1 - self.x, 1 - self.y)]
        self.hl = [s[0] for s in SMALL_HALF]

    def half(self, ref, a, h):
        return ref.at[pl.ds(h * self.hl[a], self.hl[a])]

    def begin(self):
        dg1_ref, dg2_ref, dgf_ref, dgrn_ref, dlng_ref, dlnb_ref, dbs_ref, loss_ref, dws_ref, dcv_ref = self.ins
        pack, c = self.pack, self.c
        pack[...] = jnp.zeros_like(pack)
        pack[0, 0:1, :] = dg1_ref[0:1, :]
        pack[0, 1:2, :] = dg2_ref[0:1, :]
        pack[0, 2:3, :] = dgf_ref[0:1, :]
        pack[0, 3:4, 0:RET_W] = dgrn_ref[0:1, :]
        lsum = loss_ref[0, 0:1, :]
        for i in range(1, N_TB):
            lsum = lsum + loss_ref[i, 0:1, :]
        pack[0, 3:4, RET_W:RET_W + 128] = lsum
        pack[1, 0:HEADS, 0:128] = dlng_ref[0:HEADS, :]
        pack[1, 0:HEADS, 128:256] = dlnb_ref[0:HEADS, :]
        pack[1, 0:HEADS, 256:384] = dbs_ref[0:HEADS, :]
        self.srcs = [pack, dws_ref, dcv_ref]
        n_a = len(self.srcs)
        self.ex1 = [pltpu.make_async_remote_copy(src_ref=self.half(self.srcs[a], a, 1 - c), dst_ref=self.rxs[a],
                                                 send_sem=self.s1_send.at[a], recv_sem=self.s1_recv.at[a],
                                                 device_id=self.sibling, device_id_type=MESH) for a in range(n_a)]
        for cp in self.ex1:
            cp.start()
        self.ex2 = []
        for a in range(n_a):
            self.ex1[a].wait_recv()
            self.css[a][...] = self.half(self.srcs[a], a, c)[...] + self.rxs[a][...]
            for j, chip in enumerate(self.chips):
                cp = pltpu.make_async_remote_copy(src_ref=self.css[a], dst_ref=self.gs[a].at[j], send_sem=self.s2_send.at[a, j],
                                                  recv_sem=self.s2_recv.at[a, j], device_id=(*chip, c), device_id_type=MESH)
                cp.start()
                self.ex2.append(cp)

    def end(self):
        c, x, y = self.c, self.x, self.y
        ex3 = []
        for a in range(len(self.srcs)):
            css, gs, out = self.css[a], self.gs[a], self.outs[a]
            for j in range(3):
                self.ex2[3 * a + j].wait_recv()
            tot = None
            for q in range(4):
                k = jnp.where(x != (q >> 1), 1, 0) + jnp.where(y != (q & 1), 2, 0)
                term = jnp.where(k == 0, css[...], jnp.where(k == 1, gs[0], jnp.where(k == 2, gs[1], gs[2])))
                tot = term if tot is None else tot + term
            self.half(out, a, c)[...] = tot
            cp = pltpu.make_async_remote_copy(src_ref=self.half(out, a, c), dst_ref=self.half(out, a, c), send_sem=self.s3_send.at[a],
                                              recv_sem=self.s3_recv.at[a], device_id=self.sibling, device_id_type=MESH)
            cp.start()
            ex3.append(cp)
        for a in range(len(self.srcs)):
            out = self.outs[a]
            pltpu.make_async_remote_copy(src_ref=self.half(out, a, 1 - c), dst_ref=self.half(out, a, 1 - c), send_sem=self.s3_send.at[a],
                                         recv_sem=self.s3_recv.at[a], device_id=self.sibling, device_id_type=MESH).wait_recv()
        for cp in self.ex1 + self.ex2 + ex3:
            cp.wait_send()


def _adam_math(w, g, m, v):
    nm = ADAM_B1 * m + (1.0 - ADAM_B1) * g
    nv = ADAM_B2 * v + (1.0 - ADAM_B2) * (g * g)
    d = -ADAM_LR * ((nm / (1.0 - ADAM_B1 ** ADAM_STEP)) / (jnp.sqrt(nv / (1.0 - ADAM_B2 ** ADAM_STEP)) + ADAM_EPS) + ADAM_WD * w)
    return d, nm, nv


def _adamw(params, thru, n_steps):
    plan = []
    for w, gs, _, _ in params:
        _, r, cdim = w.shape
        if len(gs) == 1:
            edges = [0, r // n_steps]
            spec3 = pl.BlockSpec((1, r // n_steps, cdim), lambda i: (0, i, 0))
            g_specs = [pl.BlockSpec((r // n_steps, cdim), lambda i: (i, 0))]
        else:
            edges = [sum(g.shape[0] for g in gs[:k]) for k in range(len(gs) + 1)]
            spec3 = pl.BlockSpec((1, r, cdim // n_steps), lambda i: (0, 0, i))
            g_specs = [pl.BlockSpec((g.shape[0], cdim // n_steps), lambda i: (0, i)) for g in gs]
        plan.append((len(gs), edges, spec3, g_specs))
    n_in = sum(n_g + 3 for n_g, _, _, _ in plan)
    n_t = len(thru)

    def body(*refs):
        ins, outs = refs[:n_in], refs[n_in + n_t:]
        for n_g, edges, _, _ in plan:
            (w_ref, *g_refs, m_ref, v_ref), ins = ins[:n_g + 3], ins[n_g + 3:]
            (go_ref, d_ref, nm_ref, nv_ref), outs = outs[:4], outs[4:]
            for g_ref, lo, hi in zip(g_refs, edges[:-1], edges[1:]):
                gg = g_ref[...]
                go_ref[0, lo:hi, :] = gg
                d_ref[0, lo:hi, :], nm_ref[0, lo:hi, :], nv_ref[0, lo:hi, :] = _adam_math(
                    w_ref[0, lo:hi, :], gg, m_ref[0, lo:hi, :], v_ref[0, lo:hi, :])
        for t_ref, to_ref in zip(refs[n_in:n_in + n_t], outs):
            to_ref[...] = t_ref[...]

    t_specs = [pl.BlockSpec((t.shape[0] // n_steps, t.shape[1]), lambda i: (i, 0)) for t in thru]
    in_specs, out_specs, out_shape, args = [], [], [], []
    for (w, gs, m, v), (_, _, spec3, g_specs) in zip(params, plan):
        in_specs += [spec3] + g_specs + [spec3, spec3]
        out_specs += [spec3] * 4
        out_shape += [jax.ShapeDtypeStruct(w.shape, F32)] * 4
        args += [w, *gs, m, v]
    res = pl.pallas_call(
        body, name="adamw", grid=(n_steps,), out_shape=tuple(out_shape) + tuple(jax.ShapeDtypeStruct(t.shape, t.dtype) for t in thru),
        in_specs=in_specs + t_specs, out_specs=tuple(out_specs) + tuple(t_specs),
        compiler_params=_cparams(("parallel",)),
    )(*args, *thru)
    return [res[4 * k:4 * k + 4] for k in range(len(params))], res[4 * len(params):]


def _adamw_small(rp, rws, rcv, gcw, params):
    n_p = len(params)

    def body(*refs):
        rp_ref, rws_ref, rcv_ref, gcw_ref = refs[:4]
        ins = refs[4:4 + 3 * n_p]
        outs = refs[4 + 3 * n_p:]
        outs[4 * n_p][...] = rp_ref[0, 3:4, RET_W:RET_W + 1]
        grads = [rp_ref[0, 0:1, :], rp_ref[0, 1:2, :], rp_ref[0, 2:3, :], rp_ref[0, 3:4, 0:RET_W],
                 rp_ref[1, 0:HEADS, 0:128], rp_ref[1, 0:HEADS, 128:256], rp_ref[1, 0:HEADS, 256:384],
                 rws_ref[...], gcw_ref[...], None]
        for p in range(n_p):
            w_ref, m_ref, v_ref = ins[3 * p:3 * p + 3]
            o = outs[4 * p:4 * p + 4]
            if p == n_p - 1:
                for hf in range(2):
                    cs = slice(hf * D_FF, (hf + 1) * D_FF)
                    g = rcv_ref[hf, 3:4, :]
                    res = (g,) + _adam_math(w_ref[:, cs], g, m_ref[:, cs], v_ref[:, cs])
                    for t in range(4):
                        o[t][:, cs] = res[t]
                continue
            lead = w_ref.ndim > grads[p].ndim
            rd = (lambda r: r[0]) if lead else (lambda r: r[...])
            res = (grads[p],) + _adam_math(rd(w_ref), grads[p], rd(m_ref), rd(v_ref))
            for t in range(4):
                if lead:
                    o[t][0] = res[t]
                else:
                    o[t][...] = res[t]

    vm = pl.BlockSpec(memory_space=pltpu.VMEM)
    flat = [a for tr in params for a in tr]
    out_shape = tuple(jax.ShapeDtypeStruct(tr[0].shape, F32) for tr in params for _ in range(4)) + (jax.ShapeDtypeStruct((1, 1), F32),)
    res = pl.pallas_call(
        body, name="adamw_small", out_shape=out_shape, in_specs=[vm] * (4 + len(flat)), out_specs=(vm,) * len(out_shape),
        compiler_params=_cparams(),
    )(rp, rws, rcv, gcw, *flat)
    return [res[4 * p:4 * p + 4] for p in range(n_p)], res[4 * n_p]


def kernel(x, mix_norm_g, w_in, ret_norm_g, sgu_ln_g, sgu_ln_b, sgu_w_s, sgu_b_s, w_out, ffn_norm_g, w_up, conv_w, conv_b, w_down, final_norm_g, loss_target, m_mix_norm_g, m_w_in, m_ret_norm_g, m_sgu_ln_g, m_sgu_ln_b, m_sgu_w_s, m_sgu_b_s, m_w_out, m_ffn_norm_g, m_w_up, m_conv_w, m_conv_b, m_w_down, m_final_norm_g, v_mix_norm_g, v_w_in, v_ret_norm_g, v_sgu_ln_g, v_sgu_ln_b, v_sgu_w_s, v_sgu_b_s, v_w_out, v_ffn_norm_g, v_w_up, v_conv_w, v_conv_b, v_w_down, v_final_norm_g):
    xs = x[0]
    tgt = loss_target[0]
    grn = ret_norm_g.reshape(1, RET_W)
    lng = sgu_ln_g.reshape(1, SGU_W)
    lnb = sgu_ln_b.reshape(1, SGU_W)
    ws = sgu_w_s[0]
    bsb = jnp.broadcast_to(sgu_b_s[0][:, :, None], (HEADS, CHUNK, HEAD_DIM))
    gf = final_norm_g.reshape(1, D_MODEL)
    me = 4 * lax.axis_index("x") + 2 * lax.axis_index("y") + lax.axis_index("c")
    tr = lambda a: jnp.transpose(a[0])[None]
    tr_cw = lambda a: jnp.transpose(a, (1, 0, 2))

    proj, h1, cos2, sin2, win_g, cw_sh, wout_g, wdn_g, su = _fwd_proj(
        xs, mix_norm_g, _rope_freq(), w_in[0], w_out[0], tr(w_up)[0], w_down[0], tr_cw(conv_w))
    cw_g = jnp.transpose(cw_sh, (1, 0, 2)).reshape(8, 2 * D_FF)
    x2, mixcat, o, sprev, wup_g = _fwd_mix(xs, proj, wout_g, grn, lng, lnb, ws, bsb, su)
    h2, up_pre, u_conv, act, x3, loss_parts = _fwd_ffn(x2, ffn_norm_g, wup_g, cw_g, conv_b, wdn_g, gf, tgt)

    dx3, dpre, dx2, dgf, dg2, dcv = _bwd_ffn(x3, tgt, gf, x2, ffn_norm_g, up_pre, u_conv, wup_g, cw_g, wdn_g)
    band = 512
    (gdn_p,) = _wgrad("wgrad_down", act, dx3, tm=FF_TILE)
    (gout_p,) = _wgrad("wgrad_out", mixcat, dx2, tn=512)
    gup_p, g_dn = _wgrad("wgrad_up", dpre, h2, tm=FF_TILE, tn=512, hosted=[(W_DOWN, gdn_p)])
    dproj, dgrn, dlng, dlnb, dws, dbs, g_up_a, g_out = _bwd_mix(
        dx2, proj, o, sprev, wout_g, grn, lng, lnb, ws, bsb, cos2, sin2,
        [((W_UP, 0, band), gup_p), (W_OUT, gout_p)])
    gin_p, g_up_b = _wgrad("wgrad_in", h1, dproj, tm=512, tn=768, hosted=[((W_UP, band, FF_SHARD - band), gup_p)])
    grad_x, g_in, rp, rws, rcv = _bwd_proj(dproj, win_g, xs, mix_norm_g, dx2, gin_p,
                                           (dg2, dgf, dgrn, dlng, dlnb, dbs, loss_parts, dws, dcv))
    gcw = tr_cw(lax.dynamic_slice(rcv, (me // (N_DEV // 2), 0, (me % (N_DEV // 2)) * FF_SHARD), (1, 3, FF_SHARD)))

    table = {}
    big, (grad_x,) = _adamw([(w_in, [g_in], m_w_in, v_w_in), (w_out, [g_out], m_w_out, v_w_out),
                             (tr(w_up), [g_up_a, g_up_b], tr(m_w_up), tr(v_w_up)), (w_down, [g_dn], m_w_down, v_w_down)],
                            [grad_x], n_steps=4)
    table.update(zip(("w_in", "w_out", "w_up", "w_down"), big))
    table["w_up"] = tuple(tr(a) for a in table["w_up"])
    row = lambda a: a.reshape(1, D_MODEL)
    names_small = ["mix_norm_g", "ffn_norm_g", "final_norm_g", "ret_norm_g", "sgu_ln_g", "sgu_ln_b", "sgu_b_s", "sgu_w_s",
                   "conv_w", "conv_b"]
    params = [(mix_norm_g, m_mix_norm_g, v_mix_norm_g), (ffn_norm_g, m_ffn_norm_g, v_ffn_norm_g),
              (row(final_norm_g), row(m_final_norm_g), row(v_final_norm_g)), (ret_norm_g, m_ret_norm_g, v_ret_norm_g),
              (sgu_ln_g, m_sgu_ln_g, v_sgu_ln_g), (sgu_ln_b, m_sgu_ln_b, v_sgu_ln_b), (sgu_b_s, m_sgu_b_s, v_sgu_b_s),
              (sgu_w_s, m_sgu_w_s, v_sgu_w_s), (tr_cw(conv_w), tr_cw(m_conv_w), tr_cw(v_conv_w)), (conv_b, m_conv_b, v_conv_b)]
    small, loss = _adamw_small(rp, rws, rcv, gcw, params)
    for n, res in zip(names_small, small):
        table[n] = res
    table["final_norm_g"] = tuple(a.reshape(D_MODEL) for a in table["final_norm_g"])
    table["conv_w"] = tuple(tr_cw(a) for a in table["conv_w"])

    order = ["mix_norm_g", "w_in", "ret_norm_g", "sgu_ln_g", "sgu_ln_b", "sgu_w_s", "sgu_b_s", "w_out", "ffn_norm_g", "w_up",
             "conv_w", "conv_b", "w_down", "final_norm_g"]
    outs = [loss.reshape(()), grad_x[None]]
    for col in range(4):
        outs += [table[n][col] for n in order]
    return tuple(outs)
```

```python
import functools
import math

import jax
import jax.numpy as jnp
import numpy as np
from jax import lax
from jax.experimental import pallas as pl
from jax.experimental.pallas import tpu as pltpu

F32 = jnp.float32
BF16 = jnp.bfloat16
MESH = pl.DeviceIdType.MESH

N_DEV = 8
SEQ = 2048
D_MODEL = 1024
CHUNK = 128
N_CHUNK = SEQ // CHUNK
HEADS = 4
HEAD_DIM = 128
RET_W = 512
SGU_W = 512
PROJ_W = 3072
D_FF = 2816
FF_SHARD = 704
FF_TILE = 1408
FF_TILES = ((0, D_FF),)
IN_SHARD = PROJ_W // N_DEV
OUT_SHARD = D_MODEL // N_DEV
DOWN_SHARD = D_FF // N_DEV
TM = 256
N_TB = SEQ // TM
FWD_PROJ_PASS_AT = 5
FWD_MIX_PASS_AT = 10
EPS = 1e-6
ROPE_BASE = 10000.0
K_SCALE = HEAD_DIM ** -0.5
INV_SQRT2 = 0.7071067811865476
INV_SQRT_2PI = 0.3989422804014327

ADAM_LR = 0.001
ADAM_B1 = 0.9
ADAM_B2 = 0.999
ADAM_EPS = 1e-08
ADAM_WD = 0.01
ADAM_STEP = 10

VMEM_LIMIT = 56 * 1024 * 1024


def _cparams(sem=None, vmem=VMEM_LIMIT, collective=None):
    return pltpu.CompilerParams(dimension_semantics=sem, vmem_limit_bytes=vmem, collective_id=collective)


COLLECTIVE = {name: k for k, name in enumerate(("fwd_proj", "fwd_mix", "wgrad_up", "bwd_mix", "wgrad_in", "bwd_proj"))}


class _Meet:
    def __init__(self, diagonal):
        x, y, c = lax.axis_index("x"), lax.axis_index("y"), lax.axis_index("c")
        self.peers = [(x, y, 1 - c), (1 - x, y, c), (x, 1 - y, c)] + ([(1 - x, 1 - y, c)] if diagonal else [])

    def signal(self):
        for peer in self.peers:
            pl.semaphore_signal(pltpu.get_barrier_semaphore(), inc=1, device_id=peer, device_id_type=MESH)

    def wait(self):
        pl.semaphore_wait(pltpu.get_barrier_semaphore(), len(self.peers))


def _resident(shape):
    nd = len(shape)
    return pl.BlockSpec(shape, lambda *_: (0,) * nd, pipeline_mode=pl.Buffered(1))


def _dot(a, b):
    return jnp.dot(a, b, preferred_element_type=F32)


def _dot_nt(a, b):
    return lax.dot_general(a, b, (((1,), (1,)), ((), ())), preferred_element_type=F32)


def _dot_tn(a, b):
    return lax.dot_general(a, b, (((0,), (0,)), ((), ())), preferred_element_type=F32)


def _sigmoid(x):
    return 1.0 / (1.0 + jnp.exp(-x))


def _gelu(x):
    return 0.5 * x * (1.0 + lax.erf(x * INV_SQRT2))


def _gelu_grad(x):
    return 0.5 * (1.0 + lax.erf(x * INV_SQRT2)) + x * (jnp.exp(-0.5 * x * x) * INV_SQRT_2PI)


def _rot(xh, cos2, sin2):
    return xh * cos2 + pltpu.roll(xh, HEAD_DIM // 2, 1) * sin2


def _rot_t(dh, cos2, sin2):
    return dh * cos2 + pltpu.roll(dh * sin2, HEAD_DIM // 2, 1)


def _rope_freq():
    half = HEAD_DIM // 2
    inv_freq = jnp.power(ROPE_BASE, -jnp.arange(half, dtype=F32) / half)
    return jnp.concatenate([inv_freq, inv_freq])[None, :]


def _rope_block(inv2, first_row):
    pos = (lax.broadcasted_iota(jnp.int32, (TM, HEAD_DIM), 0) + first_row).astype(F32)
    ang = pos * inv2
    sin = jnp.sin(ang)
    lane = lax.broadcasted_iota(jnp.int32, (TM, HEAD_DIM), 1)
    return jnp.cos(ang), jnp.where(lane < HEAD_DIM // 2, -sin, sin)


def _log_gamma():
    return np.log(np.float32(1.0) - np.power(np.float32(2.0), -5.0 - np.arange(HEADS, dtype=np.float32))).astype(np.float32)


def _fill_decay(mask_ref, qd_ref, kd_ref):
    assert HEAD_DIM == CHUNK
    lg = _log_gamma()
    t = lax.broadcasted_iota(jnp.int32, (CHUNK, CHUNK), 0).astype(F32)
    diff = t - lax.broadcasted_iota(jnp.int32, (CHUNK, CHUNK), 1).astype(F32)
    for h in range(HEADS):
        mask_ref[h] = jnp.where(diff >= 0.0, jnp.exp(float(lg[h]) * jnp.maximum(diff, 0.0)), 0.0)
        qd_ref[h] = jnp.exp(float(lg[h]) * (t + 1.0))
        kd_ref[h] = jnp.exp(float(lg[h]) * (CHUNK - 1.0 - t))


def _chunk_decay():
    lg = _log_gamma()
    return [float(np.exp(lg[h] * np.float32(CHUNK))) for h in range(HEADS)]


W_IN, W_OUT, W_UP, W_DOWN, W_CONV = range(5)
GATHERED = {W_IN: ((D_MODEL, PROJ_W), BF16), W_OUT: ((D_MODEL, D_MODEL), BF16), W_UP: ((2 * D_FF, D_MODEL), BF16),
            W_DOWN: ((D_FF, D_MODEL), BF16), W_CONV: ((N_DEV, 8, FF_SHARD), F32)}
SHARD = {W_IN: (D_MODEL, IN_SHARD), W_OUT: (OUT_SHARD, D_MODEL), W_UP: (FF_SHARD, D_MODEL), W_DOWN: (DOWN_SHARD, D_MODEL),
         W_CONV: (8, FF_SHARD)}


class _Gather:
    N_SEMS = 9

    def __init__(self, ids, stages, gathered, send_sems, recv_sems, local_sems):
        self.ids, self.stages, self.gathered = ids, stages, gathered
        self.send_sems, self.recv_sems, self.local_sems = send_sems, recv_sems, local_sems
        self.x, self.y, self.c = lax.axis_index("x"), lax.axis_index("y"), lax.axis_index("c")
        self.me = (self.x, self.y, self.c)
        self.sibling = (self.x, self.y, 1 - self.c)
        self.chips = [(1 - self.x, self.y), (self.x, 1 - self.y), (1 - self.x, 1 - self.y)]

    def slot(self, n, px, py, pc):
        dev = 4 * px + 2 * py + pc
        w, g = self.ids[n], self.gathered[n]
        if w == W_IN:
            return g.at[:, pl.ds(pl.multiple_of(dev * IN_SHARD, 128), IN_SHARD)]
        if w == W_OUT:
            return g.at[pl.ds(pl.multiple_of(dev * OUT_SHARD, 128), OUT_SHARD), :]
        if w == W_DOWN:
            return g.at[pl.ds(pl.multiple_of(dev * DOWN_SHARD, 32), DOWN_SHARD), :]
        if w == W_UP:
            return g.at[pl.ds(pl.multiple_of(dev * FF_SHARD, 32), FF_SHARD), :]
        return g.at[dev]

    def half(self, n, px, py, pc, h):
        dev = 4 * px + 2 * py + pc
        w, g = self.ids[n], self.gathered[n]
        if w == W_IN:
            return g.at[pl.ds(h * (D_MODEL // 2), D_MODEL // 2), pl.ds(pl.multiple_of(dev * IN_SHARD, 128), IN_SHARD)]
        rows = SHARD[w][0] // 2
        return g.at[pl.ds(pl.multiple_of(dev * SHARD[w][0] + h * rows, 16), rows), :]

    def tree(self, n):
        return self.ids[n] != W_CONV

    def copy(self, n, k, block, to, src=None, h=None):
        ref = self.slot(n, *block) if h is None else self.half(n, *block, h)
        return pltpu.make_async_remote_copy(
            src_ref=ref if src is None else src, dst_ref=ref,
            send_sem=self.send_sems.at[n, k], recv_sem=self.recv_sems.at[n, k], device_id=to, device_id_type=MESH)

    def _mine(self):
        return [pltpu.make_async_copy(self.stages[n], self.slot(n, *self.me), self.local_sems.at[n]) for n in range(len(self.ids))]

    def _first(self):
        out = []
        for n in range(len(self.ids)):
            out.append(self.copy(n, 0, self.me, self.sibling, src=self.stages[n]))
            out += [self.copy(n, 1 + j, self.me, (*chip, self.c), src=self.stages[n])
                    for j, chip in enumerate(self.chips[:2] if self.tree(n) else self.chips)]
        return out

    def start(self):
        for cp in self._mine() + self._first():
            cp.start()

    def _passed(self, j):
        dev = (*self.chips[j], self.c)
        out = []
        for n in range(len(self.ids)):
            if not self.tree(n):
                out.append(self.copy(n, 4 + j, dev, self.sibling))
            elif j < 2:
                out += [self.copy(n, 3 + j, dev, (*self.chips[1 - j], self.c), h=j), self.copy(n, 5 + j, dev, self.sibling)]
            else:
                out += [self.copy(n, 7, dev, self.sibling, h=0), self.copy(n, 8, dev, self.sibling, h=1)]
        return out

    def near(self):
        for j in range(2):
            dev = (*self.chips[j], self.c)
            for n in range(len(self.ids)):
                self.copy(n, 1 + j, dev, self.me).wait_recv()
            for cp in self._passed(j):
                cp.start()

    def finish(self):
        dev = (*self.chips[2], self.c)
        for n in range(len(self.ids)):
            if self.tree(n):
                self.copy(n, 3, dev, self.me, h=0).wait_recv()
                self.copy(n, 4, dev, self.me, h=1).wait_recv()
            else:
                self.copy(n, 3, dev, self.me).wait_recv()
        for cp in self._passed(2):
            cp.start()
        for n in range(len(self.ids)):
            self.copy(n, 0, self.sibling, self.me).wait_recv()
            for j, chip in enumerate(self.chips):
                dev = (*chip, 1 - self.c)
                if not self.tree(n):
                    self.copy(n, 4 + j, dev, self.me).wait_recv()
                elif j < 2:
                    self.copy(n, 5 + j, dev, self.me).wait_recv()
                else:
                    self.copy(n, 7, dev, self.me, h=0).wait_recv()
                    self.copy(n, 8, dev, self.me, h=1).wait_recv()
        for cp in self._mine():
            cp.wait()
        for cp in self._first() + self._passed(0) + self._passed(1) + self._passed(2):
            cp.wait_send()


def _gather_scratch(n):
    return [pltpu.SemaphoreType.DMA((n, _Gather.N_SEMS)), pltpu.SemaphoreType.DMA((n, _Gather.N_SEMS)), pltpu.SemaphoreType.DMA((n,))]


def _gathered_shapes(ids):
    return tuple(jax.ShapeDtypeStruct(*GATHERED[w]) for w in ids)


def _fwd_proj(x, g1, inv2, w_in, w_out, w_up, w_down, conv_w):
    ids_a, ids_b = [W_IN, W_CONV], [W_OUT, W_DOWN]

    def body(x_ref, g_ref, inv_ref, in_hbm, out_hbm, up_hbm, dn_hbm, cw_ref,
             proj_ref, h1_ref, cos_ref, sin_ref, gin, gcw, gout, gdn, su_ref,
             w_vm, s_in, s_cw, s_out, s_dn, f_in, f_out, f_up, f_dn, ld_sems,
             a_send, a_recv, a_local, b_send, b_recv, b_local):
        ag_a = _Gather(ids_a, [s_in, s_cw], [gin, gcw], a_send, a_recv, a_local)
        ag_b = _Gather(ids_b, [s_out, s_dn], [gout, gdn], b_send, b_recv, b_local)

        @pl.when(pl.program_id(0) == 0)
        def _():
            meet = _Meet(diagonal=True)
            meet.signal()
            loads = [pltpu.make_async_copy(src, dst, ld_sems.at[i])
                     for i, (src, dst) in enumerate(((in_hbm, f_in), (out_hbm, f_out), (dn_hbm, f_dn), (up_hbm, f_up)))]
            for cp in loads:
                cp.start()
            s_cw[...] = jnp.zeros_like(s_cw)
            for k in range(3):
                s_cw[k:k + 1, :] = cw_ref[k]
            loads[0].wait()
            s_in[...] = f_in[...].astype(BF16)
            meet.wait()
            ag_a.start()
            loads[1].wait()
            s_out[...] = f_out[...].astype(BF16)
            loads[2].wait()
            s_dn[...] = f_dn[...].astype(BF16)
            ag_a.near()
            ag_b.start()
            loads[3].wait()
            su_ref[...] = f_up[...].astype(BF16)
            ag_a.finish()
            fill = pltpu.make_async_copy(gin, w_vm, ld_sems.at[4])
            fill.start()
            fill.wait()

        pl.when(pl.program_id(0) == FWD_PROJ_PASS_AT)(ag_b.near)

        xb = x_ref[...]
        r = lax.rsqrt(jnp.mean(xb * xb, axis=-1, keepdims=True) + EPS)
        h = ((xb * r) * g_ref[...]).astype(BF16)
        h1_ref[...] = h
        p = _dot(h, w_vm[...])
        c2, s2 = _rope_block(inv_ref[...], pl.program_id(0) * TM)
        cos_ref[...], sin_ref[...] = c2, s2
        for hd in range(HEADS):
            sl = slice(hd * HEAD_DIM, (hd + 1) * HEAD_DIM)
            proj_ref[:, sl] = _rot(p[:, sl], c2, s2)
            ks = slice(RET_W + hd * HEAD_DIM, RET_W + (hd + 1) * HEAD_DIM)
            proj_ref[:, ks] = _rot(p[:, ks], c2, s2) * K_SCALE
        proj_ref[:, 2 * RET_W:] = p[:, 2 * RET_W:]

        pl.when(pl.program_id(0) == N_TB - 1)(ag_b.finish)

    tok = lambda w: pl.BlockSpec((TM, w), lambda i: (i, 0))
    hbm = pl.BlockSpec(memory_space=pl.ANY)
    vm = pl.BlockSpec(memory_space=pltpu.VMEM)
    return pl.pallas_call(
        body, name="fwd_proj", grid=(N_TB,),
        out_shape=(jax.ShapeDtypeStruct((SEQ, PROJ_W), F32), jax.ShapeDtypeStruct((SEQ, D_MODEL), BF16),
                   jax.ShapeDtypeStruct((SEQ, HEAD_DIM), F32), jax.ShapeDtypeStruct((SEQ, HEAD_DIM), F32))
        + _gathered_shapes(ids_a + ids_b) + (jax.ShapeDtypeStruct(SHARD[W_UP], BF16),),
        in_specs=[tok(D_MODEL), _resident((1, D_MODEL)), _resident((1, HEAD_DIM)), hbm, hbm, hbm, hbm, vm],
        out_specs=(tok(PROJ_W), tok(D_MODEL), tok(HEAD_DIM), tok(HEAD_DIM), hbm, hbm, hbm, hbm, vm),
        scratch_shapes=[pltpu.VMEM((D_MODEL, PROJ_W), BF16), pltpu.VMEM(SHARD[W_IN], BF16), pltpu.VMEM(SHARD[W_CONV], F32),
                        pltpu.VMEM(SHARD[W_OUT], BF16), pltpu.VMEM(SHARD[W_DOWN], BF16),
                        pltpu.VMEM(SHARD[W_IN], F32), pltpu.VMEM(SHARD[W_OUT], F32), pltpu.VMEM(SHARD[W_UP], F32),
                        pltpu.VMEM(SHARD[W_DOWN], F32), pltpu.SemaphoreType.DMA((5,))]
        + _gather_scratch(len(ids_a)) + _gather_scratch(len(ids_b)),
        compiler_params=_cparams(("arbitrary",), collective=COLLECTIVE["fwd_proj"]),
    )(x, g1, inv2, w_in, w_out, w_up, w_down, conv_w)


def _causal(w):
    r = lax.broadcasted_iota(jnp.int32, (CHUNK, CHUNK), 0)
    c = lax.broadcasted_iota(jnp.int32, (CHUNK, CHUNK), 1)
    return jnp.where(r >= c, w, 0.0)


def _fwd_mix(x, proj, wout_g, grn, lng, lnb, ws, bsb, su):
    cdec = _chunk_decay()
    ids = [W_UP]

    def body(x_ref, p_ref, w_ref, grn_ref, lng_ref, lnb_ref, ws_ref, bsb_ref, su_ref,
             x2_ref, cat_ref, o_ref, sp_ref, gup, state, m_ref, qd_ref, kd_ref, send_sems, recv_sems, local_sems):
        ag = _Gather(ids, [su_ref], [gup], send_sems, recv_sems, local_sems)

        @pl.when(pl.program_id(0) == 0)
        def _():
            meet = _Meet(diagonal=False)
            meet.signal()
            state[...] = jnp.zeros_like(state)
            _fill_decay(m_ref, qd_ref, kd_ref)
            meet.wait()
            ag.start()

        for h in range(HEADS):
            sl = slice(h * HEAD_DIM, (h + 1) * HEAD_DIM)
            q = p_ref[:, sl]
            k = p_ref[:, RET_W + h * HEAD_DIM:RET_W + (h + 1) * HEAD_DIM]
            v = p_ref[:, 2 * RET_W + h * HEAD_DIM:2 * RET_W + (h + 1) * HEAD_DIM]
            g = p_ref[:, 3 * RET_W + h * HEAD_DIM:3 * RET_W + (h + 1) * HEAD_DIM]
            qb, kb, vb = q.astype(BF16), k.astype(BF16), v.astype(BF16)
            a = _dot_nt(qb, kb) * m_ref[h]
            spb = state[h].astype(BF16)
            sp_ref[0, h] = spb
            o = _dot(a.astype(BF16), vb) + _dot((q * qd_ref[h]).astype(BF16), spb)
            state[h] = state[h] * cdec[h] + _dot_tn((k * kd_ref[h]).astype(BF16), vb)
            o_ref[:, sl] = o
            rinv = lax.rsqrt(jnp.mean(o * o, axis=-1, keepdims=True) + EPS)
            rn = (o * rinv) * grn_ref[:, sl]
            cat_ref[:, sl] = ((g * _sigmoid(g)) * rn).astype(BF16)
        for gi in range(HEADS):
            sl = slice(gi * HEAD_DIM, (gi + 1) * HEAD_DIM)
            u = p_ref[:, 4 * RET_W + gi * HEAD_DIM:4 * RET_W + (gi + 1) * HEAD_DIM]
            sv = p_ref[:, 4 * RET_W + SGU_W + gi * HEAD_DIM:4 * RET_W + SGU_W + (gi + 1) * HEAD_DIM]
            gv = _gelu(sv)
            xc = gv - jnp.mean(gv, axis=-1, keepdims=True)
            vn = (xc * lax.rsqrt(jnp.mean(xc * xc, axis=-1, keepdims=True) + EPS)) * lng_ref[:, sl] + lnb_ref[:, sl]
            mixed = _dot(_causal(ws_ref[gi]).astype(BF16), vn.astype(BF16)) + bsb_ref[gi]
            cat_ref[:, RET_W + gi * HEAD_DIM:RET_W + (gi + 1) * HEAD_DIM] = (_gelu(u) * mixed).astype(BF16)
        x2_ref[...] = x_ref[...] + _dot(cat_ref[...], w_ref[...])

        pl.when(pl.program_id(0) == FWD_MIX_PASS_AT)(ag.near)
        pl.when(pl.program_id(0) == N_CHUNK - 1)(ag.finish)

    ch = lambda w: pl.BlockSpec((CHUNK, w), lambda i: (i, 0))
    hcc = (HEADS, CHUNK, CHUNK)
    hbm = pl.BlockSpec(memory_space=pl.ANY)
    return pl.pallas_call(
        body, name="fwd_mix", grid=(N_CHUNK,),
        out_shape=(jax.ShapeDtypeStruct((SEQ, D_MODEL), F32), jax.ShapeDtypeStruct((SEQ, D_MODEL), BF16),
                   jax.ShapeDtypeStruct((SEQ, RET_W), F32), jax.ShapeDtypeStruct((N_CHUNK, HEADS, HEAD_DIM, HEAD_DIM), BF16))
        + _gathered_shapes(ids),
        in_specs=[ch(D_MODEL), ch(PROJ_W), _resident((D_MODEL, D_MODEL)), _resident((1, RET_W)), _resident((1, SGU_W)),
                  _resident((1, SGU_W)), _resident(hcc), _resident(hcc), hbm],
        out_specs=(ch(D_MODEL), ch(D_MODEL), ch(RET_W), pl.BlockSpec((1, HEADS, HEAD_DIM, HEAD_DIM), lambda i: (i, 0, 0, 0)), hbm),
        scratch_shapes=[pltpu.VMEM((HEADS, HEAD_DIM, HEAD_DIM), F32)] + [pltpu.VMEM(hcc, F32)] * 3 + _gather_scratch(len(ids)),
        compiler_params=_cparams(("arbitrary",), collective=COLLECTIVE["fwd_mix"]),
    )(x, proj, wout_g, grn, lng, lnb, ws, bsb, su)


def _conv_taps(p, prev8):
    row = lax.broadcasted_iota(jnp.int32, p.shape, 0)
    p1 = jnp.where(row == 0, prev8[7:8, :], pltpu.roll(p, 1, 0))
    p2 = jnp.where(row == 0, prev8[6:7, :], jnp.where(row == 1, prev8[7:8, :], pltpu.roll(p, 2, 0)))
    return p1, p2


def _fwd_ffn(x2, g2, wup_g, cw_g, cb_g, wdn_g, gf, tgt):
    def body(x_ref, g_ref, wu_ref, cw_ref, cb_ref, wd_ref, gf_ref, t_ref, h2_ref, up_ref, u_ref, act_ref, x3_ref, loss_ref, carry):
        @pl.when(pl.program_id(0) == 0)
        def _():
            carry[...] = jnp.zeros_like(carry)

        xb = x_ref[...]
        r = lax.rsqrt(jnp.mean(xb * xb, axis=-1, keepdims=True) + EPS)
        h = ((xb * r) * g_ref[...]).astype(BF16)
        h2_ref[...] = h
        acc = xb
        for t0, tw in FF_TILES:
            u = []
            for c0 in (t0, D_FF + t0):
                cs = slice(c0, c0 + tw)
                p = _dot_nt(h, wu_ref[pl.ds(c0, tw), :])
                up_ref[:, cs] = p.astype(BF16)
                p1, p2 = _conv_taps(p, carry[:, cs])
                carry[:, cs] = p[TM - 8:, :]
                us = p2 * cw_ref[0:1, cs] + p1 * cw_ref[1:2, cs] + p * cw_ref[2:3, cs] + cb_ref[:, cs]
                u_ref[:, cs] = us.astype(BF16)
                u.append(us)
            a = ((u[0] * _sigmoid(u[0])) * u[1]).astype(BF16)
            act_ref[:, t0:t0 + tw] = a
            acc = acc + _dot(a, wd_ref[pl.ds(t0, tw), :])
        x3_ref[...] = acc
        r3 = lax.rsqrt(jnp.mean(acc * acc, axis=-1, keepdims=True) + EPS)
        diff = (acc * r3) * gf_ref[...] - t_ref[...]
        loss_ref[...] = jnp.full(loss_ref.shape, 0.5 * jnp.sum(jnp.mean(diff * diff, axis=-1)), F32)

    tok = lambda w: pl.BlockSpec((TM, w), lambda i: (i, 0))
    return pl.pallas_call(
        body, name="fwd_ffn", grid=(N_TB,),
        out_shape=(jax.ShapeDtypeStruct((SEQ, D_MODEL), BF16), jax.ShapeDtypeStruct((SEQ, 2 * D_FF), BF16),
                   jax.ShapeDtypeStruct((SEQ, 2 * D_FF), BF16),
                   jax.ShapeDtypeStruct((SEQ, D_FF), BF16), jax.ShapeDtypeStruct((SEQ, D_MODEL), F32),
                   jax.ShapeDtypeStruct((N_TB, 8, 128), F32)),
        in_specs=[tok(D_MODEL), _resident((1, D_MODEL)), _resident((2 * D_FF, D_MODEL)), _resident((8, 2 * D_FF)),
                  _resident((1, 2 * D_FF)), _resident((D_FF, D_MODEL)), _resident((1, D_MODEL)), tok(D_MODEL)],
        out_specs=(tok(D_MODEL), tok(2 * D_FF), tok(2 * D_FF), tok(D_FF), tok(D_MODEL),
                   pl.BlockSpec((1, 8, 128), lambda i: (i, 0, 0))),
        scratch_shapes=[pltpu.VMEM((8, 2 * D_FF), F32)],
        compiler_params=_cparams(("arbitrary",)),
    )(x2, g2, wup_g, cw_g, cb_g, wdn_g, gf, tgt)


def _bwd_ffn(x3, tgt, gf, x2, g2, up_pre, u_conv, wup_g, cw_g, wdn_g):
    def body(x3_ref, t_ref, gf_ref, x2_ref, g2_ref, up_ref, u_ref, wu_ref, cw_ref, wd_ref,
             dx3_ref, dpre_ref, dx2_ref, dgf_ref, dg2_ref, dcv_ref, nxt):
        i = pl.program_id(0)

        @pl.when(i == 0)
        def _():
            nxt[...] = jnp.zeros_like(nxt)
            dgf_ref[...] = jnp.zeros_like(dgf_ref)
            dg2_ref[...] = jnp.zeros_like(dg2_ref)
            dcv_ref[...] = jnp.zeros_like(dcv_ref)

        x3 = x3_ref[...]
        r3 = lax.rsqrt(jnp.mean(x3 * x3, axis=-1, keepdims=True) + EPS)
        xh3 = x3 * r3
        dy = (xh3 * gf_ref[...] - t_ref[...]) * (1.0 / D_MODEL)
        dgf_ref[0:1, :] += jnp.sum(dy * xh3, axis=0, keepdims=True)
        t3 = dy * gf_ref[...]
        dx3 = r3 * (t3 - xh3 * jnp.mean(t3 * xh3, axis=-1, keepdims=True))
        dx3b = dx3.astype(BF16)
        dx3_ref[...] = dx3b
        dh2 = jnp.zeros((TM, D_MODEL), F32)
        for t0, tw in FF_TILES:
            row = lax.broadcasted_iota(jnp.int32, (TM, tw), 0)
            ts = slice(t0, t0 + tw)
            dact = _dot_nt(dx3b, wd_ref[pl.ds(t0, tw), :])
            ua = u_ref[:, ts].astype(F32)
            ub = u_ref[:, D_FF + t0:D_FF + t0 + tw].astype(F32)
            sg = _sigmoid(ua)
            du = [dact * ub * (sg * (1.0 + ua * (1.0 - sg))), dact * (ua * sg)]
            for n in range(2):
                d = du[n]
                c0 = n * D_FF + t0
                cs = slice(c0, c0 + tw)
                nx = nxt[:, cs]
                n1 = jnp.where(row == TM - 1, nx[0:1, :], pltpu.roll(d, TM - 1, 0))
                n2 = jnp.where(row == TM - 2, nx[0:1, :], jnp.where(row == TM - 1, nx[1:2, :], pltpu.roll(d, TM - 2, 0)))
                nxt[:, cs] = d[0:8, :]
                dp = (d * cw_ref[2:3, cs] + n1 * cw_ref[1:2, cs] + n2 * cw_ref[0:1, cs]).astype(BF16)
                dpre_ref[:, cs] = dp
                p = up_ref[:, cs].astype(F32)
                dcv_ref[n, 0:1, ts] += jnp.sum(n2 * p, axis=0, keepdims=True)
                dcv_ref[n, 1:2, ts] += jnp.sum(n1 * p, axis=0, keepdims=True)
                dcv_ref[n, 2:3, ts] += jnp.sum(d * p, axis=0, keepdims=True)
                dcv_ref[n, 3:4, ts] += jnp.sum(d, axis=0, keepdims=True)
                dh2 = dh2 + _dot(dp, wu_ref[pl.ds(c0, tw), :])
        x2 = x2_ref[...]
        r2 = lax.rsqrt(jnp.mean(x2 * x2, axis=-1, keepdims=True) + EPS)
        xh2 = x2 * r2
        dg2_ref[0:1, :] += jnp.sum(dh2 * xh2, axis=0, keepdims=True)
        t2 = dh2 * g2_ref[...]
        dx2_ref[...] = dx3 + r2 * (t2 - xh2 * jnp.mean(t2 * xh2, axis=-1, keepdims=True))

    rev = lambda w: pl.BlockSpec((TM, w), lambda i: (N_TB - 1 - i, 0))
    acc = lambda s: pl.BlockSpec(s, lambda i: (0,) * len(s))
    return pl.pallas_call(
        body, name="bwd_ffn", grid=(N_TB,),
        out_shape=(jax.ShapeDtypeStruct((SEQ, D_MODEL), BF16), jax.ShapeDtypeStruct((SEQ, 2 * D_FF), BF16),
                   jax.ShapeDtypeStruct((SEQ, D_MODEL), F32), jax.ShapeDtypeStruct((8, D_MODEL), F32),
                   jax.ShapeDtypeStruct((8, D_MODEL), F32), jax.ShapeDtypeStruct((2, 8, D_FF), F32)),
        in_specs=[rev(D_MODEL), rev(D_MODEL), _resident((1, D_MODEL)), rev(D_MODEL), _resident((1, D_MODEL)), rev(2 * D_FF),
                  rev(2 * D_FF), _resident((2 * D_FF, D_MODEL)), _resident((8, 2 * D_FF)), _resident((D_FF, D_MODEL))],
        out_specs=(rev(D_MODEL), rev(2 * D_FF), rev(D_MODEL), acc((8, D_MODEL)), acc((8, D_MODEL)), acc((2, 8, D_FF))),
        scratch_shapes=[pltpu.VMEM((8, 2 * D_FF), F32)],
        compiler_params=_cparams(("arbitrary",)),
    )(x3, tgt, gf, x2, g2, up_pre, u_conv, wup_g, cw_g, wdn_g)


def _bwd_mix(dx2, proj, o, sprev, wout_g, grn, lng, lnb, ws, bsb, cos2, sin2, hosted):
    cdec = _chunk_decay()
    geoms = [g for g, _ in hosted]
    n_h = len(hosted)

    def body(dx2_ref, p_ref, o_ref, sp_ref, w_ref, grn_ref, lng_ref, lnb_ref, ws_ref, bsb_ref, cos_ref, sin_ref, *rest):
        dp_ref, dgrn_ref, dlng_ref, dlnb_ref, dws_ref, dbs_ref = rest[n_h:n_h + 6]
        dstate, dbs_acc, m_ref, qd_ref, kd_ref = rest[2 * n_h + 6:2 * n_h + 11]
        i = pl.program_id(0)
        rs = _Scatters(geoms, rest[:n_h], rest[n_h + 6:2 * n_h + 6], rest[2 * n_h + 11:])
        pl.when(i == 0)(rs.phase1)
        pl.when(i == 3)(rs.phase2)
        pl.when(i == 8)(rs.phase2b)

        @pl.when(i == 0)
        def _():
            _fill_decay(m_ref, qd_ref, kd_ref)
            dstate[...] = jnp.zeros_like(dstate)
            dgrn_ref[...] = jnp.zeros_like(dgrn_ref)
            dlng_ref[...] = jnp.zeros_like(dlng_ref)
            dlnb_ref[...] = jnp.zeros_like(dlnb_ref)
            dws_ref[...] = jnp.zeros_like(dws_ref)
            dbs_ref[...] = jnp.zeros_like(dbs_ref)
            dbs_acc[...] = jnp.zeros_like(dbs_acc)

        dmix = _dot_nt(dx2_ref[...].astype(BF16), w_ref[...])
        for h in range(HEADS):
            sl = slice(h * HEAD_DIM, (h + 1) * HEAD_DIM)
            q = p_ref[:, sl]
            k = p_ref[:, RET_W + h * HEAD_DIM:RET_W + (h + 1) * HEAD_DIM]
            v = p_ref[:, 2 * RET_W + h * HEAD_DIM:2 * RET_W + (h + 1) * HEAD_DIM]
            g = p_ref[:, 3 * RET_W + h * HEAD_DIM:3 * RET_W + (h + 1) * HEAD_DIM]
            o = o_ref[:, sl]
            rinv = lax.rsqrt(jnp.mean(o * o, axis=-1, keepdims=True) + EPS)
            oh = o * rinv
            gr = grn_ref[:, sl]
            sg = _sigmoid(g)
            dret = dmix[:, sl]
            dp_ref[:, 3 * RET_W + h * HEAD_DIM:3 * RET_W + (h + 1) * HEAD_DIM] = (
                dret * (oh * gr) * (sg * (1.0 + g * (1.0 - sg)))).astype(BF16)
            drn = dret * (g * sg)
            dgrn_ref[0:1, sl] += jnp.sum(drn * oh, axis=0, keepdims=True)
            t = drn * gr
            do = rinv * (t - oh * jnp.mean(t * oh, axis=-1, keepdims=True))
            qb, kb, vb, dob = q.astype(BF16), k.astype(BF16), v.astype(BF16), do.astype(BF16)
            m = m_ref[h]
            ab = (_dot_nt(qb, kb) * m).astype(BF16)
            dab = (_dot_nt(dob, vb) * m).astype(BF16)
            spb = sp_ref[0, h]
            dsn = dstate[h]
            dsnb = dsn.astype(BF16)
            qdb = (q * qd_ref[h]).astype(BF16)
            kdb = (k * kd_ref[h]).astype(BF16)
            dq = _dot(dab, kb) + _dot_nt(dob, spb) * qd_ref[h]
            dk = _dot_tn(dab, qb) + _dot_nt(vb, dsnb) * kd_ref[h]
            dv = _dot_tn(ab, dob) + _dot(kdb, dsnb)
            dstate[h] = dsn * cdec[h] + _dot_tn(qdb, dob)
            c2, s2 = cos_ref[...], sin_ref[...]
            dp_ref[:, sl] = _rot_t(dq, c2, s2).astype(BF16)
            dp_ref[:, RET_W + h * HEAD_DIM:RET_W + (h + 1) * HEAD_DIM] = _rot_t(dk * K_SCALE, c2, s2).astype(BF16)
            dp_ref[:, 2 * RET_W + h * HEAD_DIM:2 * RET_W + (h + 1) * HEAD_DIM] = dv.astype(BF16)
        for gi in range(HEADS):
            sl = slice(gi * HEAD_DIM, (gi + 1) * HEAD_DIM)
            u = p_ref[:, 4 * RET_W + gi * HEAD_DIM:4 * RET_W + (gi + 1) * HEAD_DIM]
            sv = p_ref[:, 4 * RET_W + SGU_W + gi * HEAD_DIM:4 * RET_W + SGU_W + (gi + 1) * HEAD_DIM]
            gv = _gelu(sv)
            xc = gv - jnp.mean(gv, axis=-1, keepdims=True)
            rstd = lax.rsqrt(jnp.mean(xc * xc, axis=-1, keepdims=True) + EPS)
            xh = xc * rstd
            lg = lng_ref[:, sl]
            vnb = (xh * lg + lnb_ref[:, sl]).astype(BF16)
            wcb = _causal(ws_ref[gi]).astype(BF16)
            mixed = _dot(wcb, vnb) + bsb_ref[gi]
            dsgu = dmix[:, RET_W + gi * HEAD_DIM:RET_W + (gi + 1) * HEAD_DIM]
            dmixed = dsgu * _gelu(u)
            dmb = dmixed.astype(BF16)
            dws_ref[gi] += _causal(_dot_nt(dmb, vnb))
            dbs_acc[gi] += dmixed
            dvn = _dot_tn(wcb, dmb)
            dlng_ref[gi:gi + 1, :] += jnp.sum(dvn * xh, axis=0, keepdims=True)
            dlnb_ref[gi:gi + 1, :] += jnp.sum(dvn, axis=0, keepdims=True)
            dxh = dvn * lg
            dgv = rstd * (dxh - jnp.mean(dxh, axis=-1, keepdims=True) - xh * jnp.mean(dxh * xh, axis=-1, keepdims=True))
            dp_ref[:, 4 * RET_W + gi * HEAD_DIM:4 * RET_W + (gi + 1) * HEAD_DIM] = (dsgu * mixed * _gelu_grad(u)).astype(BF16)
            dp_ref[:, 4 * RET_W + SGU_W + gi * HEAD_DIM:4 * RET_W + SGU_W + (gi + 1) * HEAD_DIM] = (
                dgv * _gelu_grad(sv)).astype(BF16)

        @pl.when(i == N_CHUNK - 1)
        def _():
            for gi in range(HEADS):
                col = jnp.broadcast_to(jnp.sum(dbs_acc[gi], axis=-1, keepdims=True), (CHUNK, CHUNK))
                dbs_ref[gi:gi + 1, :] = jnp.transpose(col)[0:1, :]
            rs.phase3()

    rev = lambda w: pl.BlockSpec((CHUNK, w), lambda i: (N_CHUNK - 1 - i, 0))
    hcc = (HEADS, CHUNK, CHUNK)
    acc = lambda s: pl.BlockSpec(s, lambda i: (0,) * len(s))
    res = pl.pallas_call(
        body, name="bwd_mix", grid=(N_CHUNK,),
        out_shape=(jax.ShapeDtypeStruct((SEQ, PROJ_W), BF16), jax.ShapeDtypeStruct((8, RET_W), F32),
                   jax.ShapeDtypeStruct((8, HEAD_DIM), F32), jax.ShapeDtypeStruct((8, HEAD_DIM), F32),
                   jax.ShapeDtypeStruct(hcc, F32), jax.ShapeDtypeStruct((8, CHUNK), F32)) + _scatter_out_shapes(geoms),
        in_specs=[rev(D_MODEL), rev(PROJ_W), rev(RET_W),
                  pl.BlockSpec((1, HEADS, HEAD_DIM, HEAD_DIM), lambda i: (N_CHUNK - 1 - i, 0, 0, 0)),
                  _resident((D_MODEL, D_MODEL)), _resident((1, RET_W)), _resident((1, SGU_W)), _resident((1, SGU_W)),
                  _resident(hcc), _resident(hcc), rev(HEAD_DIM), rev(HEAD_DIM)]
        + [pl.BlockSpec(memory_space=pl.ANY)] * n_h,
        out_specs=(rev(PROJ_W), acc((8, RET_W)), acc((8, HEAD_DIM)), acc((8, HEAD_DIM)), acc(hcc), acc((8, CHUNK)))
        + _scatter_out_specs(geoms),
        scratch_shapes=[pltpu.VMEM((HEADS, HEAD_DIM, HEAD_DIM), F32), pltpu.VMEM((HEADS, CHUNK, CHUNK), F32)]
        + [pltpu.VMEM(hcc, F32)] * 3 + _scatter_scratch(geoms),
        compiler_params=_cparams(("arbitrary",), collective=COLLECTIVE["bwd_mix"]),
    )(dx2, proj, o, sprev, wout_g, grn, lng, lnb, ws, bsb, cos2, sin2, *[p for _, p in hosted])
    return tuple(res[:6 + n_h])


def _bwd_proj(dproj, win_g, x, g1, dx2, gin_p, small):
    geoms = [W_IN]
    n_s = len(small)

    def body(dp_ref, w_ref, x_ref, g_ref, dx2_ref, gin_ref, *rest):
        small_refs = rest[:n_s]
        dx_ref, rs_out, rp_ref, rws_ref, rcv_ref, dg_ref = rest[n_s:n_s + 6]
        rs_scratch = rest[n_s + 6:n_s + 6 + N_SCATTER_SCRATCH]
        ar_scratch = rest[n_s + 6 + N_SCATTER_SCRATCH:]
        ar_res = ar_scratch[N_SMALL_SCRATCH:]
        ar = _SmallReduce((dg_ref,) + tuple(small_refs), ar_res, ar_scratch[:N_SMALL_SCRATCH])
        rs = _Scatters(geoms, [gin_ref], [rs_out], rs_scratch)
        pl.when(pl.program_id(0) == 0)(lambda: rs.phase1(diagonal=True))
        pl.when(pl.program_id(0) == 1)(rs.phase2)
        pl.when(pl.program_id(0) == 5)(rs.phase2b)

        @pl.when(pl.program_id(0) == 0)
        def _():
            dg_ref[...] = jnp.zeros_like(dg_ref)

        dh = _dot_nt(dp_ref[...], w_ref[...])
        xb = x_ref[...]
        r = lax.rsqrt(jnp.mean(xb * xb, axis=-1, keepdims=True) + EPS)
        xh = xb * r
        dg_ref[0:1, :] += jnp.sum(dh * xh, axis=0, keepdims=True)
        t = dh * g_ref[...]
        dx_ref[...] = dx2_ref[...] + r * (t - xh * jnp.mean(t * xh, axis=-1, keepdims=True))

        @pl.when(pl.program_id(0) == N_TB - 1)
        def _():
            ar.begin()
            rs.phase3()
            ar.end()
            for o_ref, r_ref in zip((rp_ref, rws_ref, rcv_ref), ar_res):
                o_ref[...] = r_ref[...]

    tok = lambda w: pl.BlockSpec((TM, w), lambda i: (i, 0))
    vm = pl.BlockSpec(memory_space=pltpu.VMEM)
    res = pl.pallas_call(
        body, name="bwd_proj", grid=(N_TB,),
        out_shape=(jax.ShapeDtypeStruct((SEQ, D_MODEL), F32),) + _scatter_out_shapes(geoms)
        + tuple(jax.ShapeDtypeStruct(s, F32) for s in SMALL_FULL),
        in_specs=[tok(PROJ_W), _resident((D_MODEL, PROJ_W)), tok(D_MODEL), _resident((1, D_MODEL)), tok(D_MODEL),
                  pl.BlockSpec(memory_space=pl.ANY)] + [vm] * n_s,
        out_specs=(tok(D_MODEL),) + _scatter_out_specs(geoms) + (vm,) * len(SMALL_FULL),
        scratch_shapes=[pltpu.VMEM((8, D_MODEL), F32)] + _scatter_scratch(geoms) + _small_scratch()
        + [pltpu.VMEM(s, F32) for s in SMALL_FULL],
        compiler_params=_cparams(("arbitrary",), collective=COLLECTIVE["bwd_proj"]),
    )(dproj, win_g, x, g1, dx2, gin_p, *small)
    return res


def _wgrad(name, a, b, tm=None, tn=None, hosted=(), col_block=None):
    m_w, n_w = a.shape[-1], b.shape[-1]
    tm = m_w if tm is None else tm
    tn = n_w if tn is None else tn
    n_steps = (m_w // tm) * (n_w // tn)
    geoms = [g for g, _ in hosted]
    n_h = len(hosted)

    def body(a_ref, b_ref, *rest):
        o_ref = rest[n_h]
        if n_h:
            rs = _Scatters(geoms, rest[:n_h], rest[n_h + 1:2 * n_h + 1], rest[2 * n_h + 1:])
            step = pl.program_id(0) * (n_w // tn) + pl.program_id(1)
            pl.when(step == 0)(rs.phase1)
            pl.when(step == 1)(rs.phase2)
            pl.when(step == n_steps // 2)(rs.phase2b)
        res = _dot_tn(a_ref[...].astype(BF16), b_ref[...].astype(BF16)).astype(BF16)
        if col_block is None:
            o_ref[...] = res
        else:
            for k in range(tn // col_block):
                o_ref[k] = res[:, k * col_block:(k + 1) * col_block]
        if n_h:
            pl.when(step == n_steps - 1)(rs.phase3)

    assert not n_h or n_steps >= 4
    if col_block is None:
        o_shape, o_spec = (m_w, n_w), pl.BlockSpec((tm, tn), lambda i, j: (i, j))
    else:
        o_shape, o_spec = (n_w // col_block, m_w, col_block), pl.BlockSpec((tn // col_block, tm, col_block), lambda i, j: (j, i, 0))
    res = pl.pallas_call(
        body, name=name, grid=(m_w // tm, n_w // tn),
        out_shape=(jax.ShapeDtypeStruct(o_shape, BF16),) + _scatter_out_shapes(geoms),
        in_specs=[pl.BlockSpec((SEQ, tm), lambda i, j: (0, i)), pl.BlockSpec((SEQ, tn), lambda i, j: (0, j))]
        + [pl.BlockSpec(memory_space=pl.ANY)] * n_h,
        out_specs=(o_spec,) + _scatter_out_specs(geoms),
        scratch_shapes=_scatter_scratch(geoms),
        compiler_params=_cparams(("arbitrary", "arbitrary"), collective=COLLECTIVE[name]) if n_h else _cparams(("parallel", "parallel")),
    )(a, b, *[p for _, p in hosted])
    return tuple(res[:1 + n_h])


def _row_step(half_rows):
    return max(s for s in range(16, 177, 16) if half_rows % s == 0)


class _Scatter:
    def __init__(self, geom, partial, out, land1, mine, stage2, land2, comb, s1_send, s1_recv, s2_send, s2_recv, ld_sems):
        self.w, self.row0, self.shape = _geom(geom)
        self.partial, self.out, self.land1 = partial, out, land1
        self.mine, self.stage2, self.land2, self.comb = mine, stage2, land2, comb
        self.hr = self.shape[0] // 2
        self.step = _row_step(self.hr)
        self.s1_send, self.s1_recv, self.s2_send, self.s2_recv, self.ld_sems = s1_send, s1_recv, s2_send, s2_recv, ld_sems
        self.x, self.y, self.c = lax.axis_index("x"), lax.axis_index("y"), lax.axis_index("c")
        self.sibling = (self.x, self.y, 1 - self.c)
        self.chips = [(self.x, self.y), (1 - self.x, self.y), (self.x, 1 - self.y), (1 - self.x, 1 - self.y)]

    def block(self, px, py, pc):
        dev = 4 * px + 2 * py + pc
        if self.w == W_IN:
            return self.partial.at[dev]
        if self.w == W_OUT:
            return self.partial.at[pl.ds(pl.multiple_of(dev * OUT_SHARD, 128), OUT_SHARD), :]
        if self.w == W_DOWN:
            return self.partial.at[pl.ds(pl.multiple_of(dev * DOWN_SHARD, 32), DOWN_SHARD), :]
        return self.partial.at[pl.ds(pl.multiple_of(dev * FF_SHARD + self.row0, 32), self.shape[0]), :]

    def copy1(self, k):
        return pltpu.make_async_remote_copy(
            src_ref=self.block(*self.chips[k], 1 - self.c), dst_ref=self.land1.at[k],
            send_sem=self.s1_send.at[k], recv_sem=self.s1_recv.at[k], device_id=self.sibling, device_id_type=MESH)

    STAGE2 = [(1, 0, 1), (3, 0, 1), (2, 1, 2), (3, 1, 2), (1, 1, 1), (2, 0, 2)]

    def copy2(self, j):
        blk, h, to = self.STAGE2[j]
        src = self.comb.at[j - 4] if j >= 4 else self.stage2.at[blk - 1, pl.ds(h * self.hr, self.hr), :]
        return pltpu.make_async_remote_copy(
            src_ref=src, dst_ref=self.land2.at[j], send_sem=self.s2_send.at[j], recv_sem=self.s2_recv.at[j],
            device_id=(*self.chips[to], self.c), device_id_type=MESH)

    def _rows(self, h=None):
        step = self.step
        lo, n = (0, self.shape[0]) if h is None else (h * self.hr, self.hr)
        return [pl.ds(r0, step) for r0 in range(lo, lo + n, step)]

    def load(self, k):
        return pltpu.make_async_copy(self.block(*self.chips[k], self.c), self.mine.at[k], self.ld_sems.at[k])

    def load_mine(self):
        for k in range(4):
            self.load(k).start()

    def phase1(self):
        for k in range(4):
            self.copy1(k).start()

    def phase2(self, k):
        self.copy1(k).wait_recv()
        self.load(k).wait()
        for rs in self._rows():
            s = self.mine[k, rs, :].astype(F32) + self.land1[k, rs, :].astype(F32)
            if k == 0:
                self.out[rs, :] = s
            else:
                self.stage2[k - 1, rs, :] = s.astype(BF16)
        for j in {3: (1, 3), 1: (0,), 2: (2,), 0: ()}[k]:
            self.copy2(j).start()

    def phase2b(self):
        for j, got in ((4, 3), (5, 1)):
            blk, h, _ = self.STAGE2[j]
            self.copy2(got).wait_recv()
            for i, rs in enumerate(self._rows(h)):
                lr = pl.ds(i * self.step, self.step)
                self.comb[j - 4, lr, :] = (self.stage2[blk - 1, rs, :].astype(F32) + self.land2[got, lr, :].astype(F32)).astype(BF16)
            self.copy2(j).start()

    def phase3(self):
        for j in (0, 5, 4, 2):
            self.copy2(j).wait_recv()
        for h, (first, second) in enumerate(((0, 5), (4, 2))):
            for i, rs in enumerate(self._rows(h)):
                lr = pl.ds(i * self.step, self.step)
                self.out[rs, :] = (self.out[rs, :] + self.land2[first, lr, :].astype(F32)) + self.land2[second, lr, :].astype(F32)
        for k in range(4):
            self.copy1(k).wait_send()
        for j in range(6):
            self.copy2(j).wait_send()


def _geom(geom):
    if isinstance(geom, tuple):
        w, row0, rows = geom
        assert w == W_UP
        return w, row0, (rows, SHARD[w][1])
    return geom, 0, SHARD[geom]


N_SCATTER_SCRATCH = 10


def _scatter_out_shapes(geoms):
    return tuple(jax.ShapeDtypeStruct(_geom(g)[2], F32) for g in geoms)


def _scatter_out_specs(geoms):
    return (pl.BlockSpec(memory_space=pltpu.VMEM),) * len(geoms)


def _scatter_scratch(geoms):
    out = []
    for g in geoms:
        s = _geom(g)[2]
        hs = (s[0] // 2, s[1])
        out += [pltpu.VMEM((4,) + s, BF16), pltpu.VMEM((4,) + s, BF16), pltpu.VMEM((3,) + s, BF16), pltpu.VMEM((6,) + hs, BF16),
                pltpu.VMEM((2,) + hs, BF16),
                pltpu.SemaphoreType.DMA((4,)), pltpu.SemaphoreType.DMA((4,)), pltpu.SemaphoreType.DMA((6,)),
                pltpu.SemaphoreType.DMA((6,)), pltpu.SemaphoreType.DMA((4,))]
    return out


class _Scatters:
    def __init__(self, geoms, p_refs, out_refs, scratch):
        k = N_SCATTER_SCRATCH
        self.items = [_Scatter(g, p_refs[i], out_refs[i], *scratch[k * i:k * i + k]) for i, g in enumerate(geoms)]

    def phase1(self, diagonal=False):
        meet = _Meet(diagonal)
        meet.signal()
        for s in self.items:
            s.load_mine()
        meet.wait()
        for s in self.items:
            s.phase1()

    def phase2(self):
        for k in (3, 1, 2, 0):
            for s in self.items:
                s.phase2(k)

    def phase2b(self):
        for s in self.items:
            s.phase2b()

    def phase3(self):
        for s in self.items:
            s.phase3()


PACK_W = 1024


SMALL_FULL = [(2, 8, PACK_W), (HEADS, CHUNK, CHUNK), (2, 8, D_FF)]
SMALL_HALF = [(s[0] // 2,) + s[1:] for s in SMALL_FULL]
N_SMALL_SCRATCH = 16


def _small_scratch():
    n_a = len(SMALL_FULL)
    return ([pltpu.VMEM(SMALL_FULL[0], F32)] + [pltpu.VMEM(s, F32) for s in SMALL_HALF] + [pltpu.VMEM(s, F32) for s in SMALL_HALF]
            + [pltpu.VMEM((3,) + s, F32) for s in SMALL_HALF]
            + [pltpu.SemaphoreType.DMA((n_a,)), pltpu.SemaphoreType.DMA((n_a,)), pltpu.SemaphoreType.DMA((n_a, 3)),
               pltpu.SemaphoreType.DMA((n_a, 3)), pltpu.SemaphoreType.DMA((n_a,)), pltpu.SemaphoreType.DMA((n_a,))])


class _SmallReduce:
    def __init__(self, ins, outs, scratch):
        self.ins, self.outs = ins, outs
        (self.pack, *rest) = scratch
        self.rxs, self.css, self.gs = rest[0:3], rest[3:6], rest[6:9]
        self.s1_send, self.s1_recv, self.s2_send, self.s2_recv, self.s3_send, self.s3_recv = rest[9:]
        self.x, self.y, self.c = lax.axis_index("x"), lax.axis_index("y"), lax.axis_index("c")
        self.sibling = (self.x, self.y, 1 - self.c)
        self.chips = [(1 - self.x, self.y), (self.x, 1 - self.y), (1 - self.x, 1 - self.y)]
        self.hl = [s[0] for s in SMALL_HALF]

    def half(self, ref, a, h):
        return ref.at[pl.ds(h * self.hl[a], self.hl[a])]

    def begin(self):
        dg1_ref, dg2_ref, dgf_ref, dgrn_ref, dlng_ref, dlnb_ref, dbs_ref, loss_ref, dws_ref, dcv_ref = self.ins
        pack, c = self.pack, self.c
        pack[...] = jnp.zeros_like(pack)
        pack[0, 0:1, :] = dg1_ref[0:1, :]
        pack[0, 1:2, :] = dg2_ref[0:1, :]
        pack[0, 2:3, :] = dgf_ref[0:1, :]
        pack[0, 3:4, 0:RET_W] = dgrn_ref[0:1, :]
        lsum = loss_ref[0, 0:1, :]
        for i in range(1, N_TB):
            lsum = lsum + loss_ref[i, 0:1, :]
        pack[0, 3:4, RET_W:RET_W + 128] = lsum
        pack[1, 0:HEADS, 0:128] = dlng_ref[0:HEADS, :]
        pack[1, 0:HEADS, 128:256] = dlnb_ref[0:HEADS, :]
        pack[1, 0:HEADS, 256:384] = dbs_ref[0:HEADS, :]
        self.srcs = [pack, dws_ref, dcv_ref]
        n_a = len(self.srcs)
        self.ex1 = [pltpu.make_async_remote_copy(src_ref=self.half(self.srcs[a], a, 1 - c), dst_ref=self.rxs[a],
                                                 send_sem=self.s1_send.at[a], recv_sem=self.s1_recv.at[a],
                                                 device_id=self.sibling, device_id_type=MESH) for a in range(n_a)]
        for cp in self.ex1:
            cp.start()
        self.ex2 = []
        for a in range(n_a):
            self.ex1[a].wait_recv()
            self.css[a][...] = self.half(self.srcs[a], a, c)[...] + self.rxs[a][...]
            for j, chip in enumerate(self.chips):
                cp = pltpu.make_async_remote_copy(src_ref=self.css[a], dst_ref=self.gs[a].at[j], send_sem=self.s2_send.at[a, j],
                                                  recv_sem=self.s2_recv.at[a, j], device_id=(*chip, c), device_id_type=MESH)
                cp.start()
                self.ex2.append(cp)

    def end(self):
        c, x, y = self.c, self.x, self.y
        ex3 = []
        for a in range(len(self.srcs)):
            css, gs, out = self.css[a], self.gs[a], self.outs[a]
            for j in range(3):
                self.ex2[3 * a + j].wait_recv()
            tot = None
            for q in range(4):
                k = jnp.where(x != (q >> 1), 1, 0) + jnp.where(y != (q & 1), 2, 0)
                term = jnp.where(k == 0, css[...], jnp.where(k == 1, gs[0], jnp.where(k == 2, gs[1], gs[2])))
                tot = term if tot is None else tot + term
            self.half(out, a, c)[...] = tot
            cp = pltpu.make_async_remote_copy(src_ref=self.half(out, a, c), dst_ref=self.half(out, a, c), send_sem=self.s3_send.at[a],
                                              recv_sem=self.s3_recv.at[a], device_id=self.sibling, device_id_type=MESH)
            cp.start()
            ex3.append(cp)
        for a in range(len(self.srcs)):
            out = self.outs[a]
            pltpu.make_async_remote_copy(src_ref=self.half(out, a, 1 - c), dst_ref=self.half(out, a, 1 - c), send_sem=self.s3_send.at[a],
                                         recv_sem=self.s3_recv.at[a], device_id=self.sibling, device_id_type=MESH).wait_recv()
        for cp in self.ex1 + self.ex2 + ex3:
            cp.wait_send()


def _adam_math(w, g, m, v):
    nm = ADAM_B1 * m + (1.0 - ADAM_B1) * g
    nv = ADAM_B2 * v + (1.0 - ADAM_B2) * (g * g)
    d = -ADAM_LR * ((nm / (1.0 - ADAM_B1 ** ADAM_STEP)) / (jnp.sqrt(nv / (1.0 - ADAM_B2 ** ADAM_STEP)) + ADAM_EPS) + ADAM_WD * w)
    return d, nm, nv


def _adamw(params, thru, n_steps):
    plan = []
    for w, gs, _, _ in params:
        _, r, cdim = w.shape
        if len(gs) == 1:
            edges = [0, r // n_steps]
            spec3 = pl.BlockSpec((1, r // n_steps, cdim), lambda i: (0, i, 0))
            g_specs = [pl.BlockSpec((r // n_steps, cdim), lambda i: (i, 0))]
        else:
            edges = [sum(g.shape[0] for g in gs[:k]) for k in range(len(gs) + 1)]
            spec3 = pl.BlockSpec((1, r, cdim // n_steps), lambda i: (0, 0, i))
            g_specs = [pl.BlockSpec((g.shape[0], cdim // n_steps), lambda i: (0, i)) for g in gs]
        plan.append((len(gs), edges, spec3, g_specs))
    n_in = sum(n_g + 3 for n_g, _, _, _ in plan)
    n_t = len(thru)

    def body(*refs):
        ins, outs = refs[:n_in], refs[n_in + n_t:]
        for n_g, edges, _, _ in plan:
            (w_ref, *g_refs, m_ref, v_ref), ins = ins[:n_g + 3], ins[n_g + 3:]
            (go_ref, d_ref, nm_ref, nv_ref), outs = outs[:4], outs[4:]
            for g_ref, lo, hi in zip(g_refs, edges[:-1], edges[1:]):
                gg = g_ref[...]
                go_ref[0, lo:hi, :] = gg
                d_ref[0, lo:hi, :], nm_ref[0, lo:hi, :], nv_ref[0, lo:hi, :] = _adam_math(
                    w_ref[0, lo:hi, :], gg, m_ref[0, lo:hi, :], v_ref[0, lo:hi, :])
        for t_ref, to_ref in zip(refs[n_in:n_in + n_t], outs):
            to_ref[...] = t_ref[...]

    t_specs = [pl.BlockSpec((t.shape[0] // n_steps, t.shape[1]), lambda i: (i, 0)) for t in thru]
    in_specs, out_specs, out_shape, args = [], [], [], []
    for (w, gs, m, v), (_, _, spec3, g_specs) in zip(params, plan):
        in_specs += [spec3] + g_specs + [spec3, spec3]
        out_specs += [spec3] * 4
        out_shape += [jax.ShapeDtypeStruct(w.shape, F32)] * 4
        args += [w, *gs, m, v]
    res = pl.pallas_call(
        body, name="adamw", grid=(n_steps,), out_shape=tuple(out_shape) + tuple(jax.ShapeDtypeStruct(t.shape, t.dtype) for t in thru),
        in_specs=in_specs + t_specs, out_specs=tuple(out_specs) + tuple(t_specs),
        compiler_params=_cparams(("parallel",)),
    )(*args, *thru)
    return [res[4 * k:4 * k + 4] for k in range(len(params))], res[4 * len(params):]


def _adamw_small(rp, rws, rcv, gcw, params):
    n_p = len(params)

    def body(*refs):
        rp_ref, rws_ref, rcv_ref, gcw_ref = refs[:4]
        ins = refs[4:4 + 3 * n_p]
        outs = refs[4 + 3 * n_p:]
        outs[4 * n_p][...] = rp_ref[0, 3:4, RET_W:RET_W + 1]
        grads = [rp_ref[0, 0:1, :], rp_ref[0, 1:2, :], rp_ref[0, 2:3, :], rp_ref[0, 3:4, 0:RET_W],
                 rp_ref[1, 0:HEADS, 0:128], rp_ref[1, 0:HEADS, 128:256], rp_ref[1, 0:HEADS, 256:384],
                 rws_ref[...], gcw_ref[...], None]
        for p in range(n_p):
            w_ref, m_ref, v_ref = ins[3 * p:3 * p + 3]
            o = outs[4 * p:4 * p + 4]
            if p == n_p - 1:
                for hf in range(2):
                    cs = slice(hf * D_FF, (hf + 1) * D_FF)
                    g = rcv_ref[hf, 3:4, :]
                    res = (g,) + _adam_math(w_ref[:, cs], g, m_ref[:, cs], v_ref[:, cs])
                    for t in range(4):
                        o[t][:, cs] = res[t]
                continue
            lead = w_ref.ndim > grads[p].ndim
            rd = (lambda r: r[0]) if lead else (lambda r: r[...])
            res = (grads[p],) + _adam_math(rd(w_ref), grads[p], rd(m_ref), rd(v_ref))
            for t in range(4):
                if lead:
                    o[t][0] = res[t]
                else:
                    o[t][...] = res[t]

    vm = pl.BlockSpec(memory_space=pltpu.VMEM)
    flat = [a for tr in params for a in tr]
    out_shape = tuple(jax.ShapeDtypeStruct(tr[0].shape, F32) for tr in params for _ in range(4)) + (jax.ShapeDtypeStruct((1, 1), F32),)
    res = pl.pallas_call(
        body, name="adamw_small", out_shape=out_shape, in_specs=[vm] * (4 + len(flat)), out_specs=(vm,) * len(out_shape),
        compiler_params=_cparams(),
    )(rp, rws, rcv, gcw, *flat)
    return [res[4 * p:4 * p + 4] for p in range(n_p)], res[4 * n_p]


def kernel(x, mix_norm_g, w_in, ret_norm_g, sgu_ln_g, sgu_ln_b, sgu_w_s, sgu_b_s, w_out, ffn_norm_g, w_up, conv_w, conv_b, w_down, final_norm_g, loss_target, m_mix_norm_g, m_w_in, m_ret_norm_g, m_sgu_ln_g, m_sgu_ln_b, m_sgu_w_s, m_sgu_b_s, m_w_out, m_ffn_norm_g, m_w_up, m_conv_w, m_conv_b, m_w_down, m_final_norm_g, v_mix_norm_g, v_w_in, v_ret_norm_g, v_sgu_ln_g, v_sgu_ln_b, v_sgu_w_s, v_sgu_b_s, v_w_out, v_ffn_norm_g, v_w_up, v_conv_w, v_conv_b, v_w_down, v_final_norm_g):
    xs = x[0]
    tgt = loss_target[0]
    grn = ret_norm_g.reshape(1, RET_W)
    lng = sgu_ln_g.reshape(1, SGU_W)
    lnb = sgu_ln_b.reshape(1, SGU_W)
    ws = sgu_w_s[0]
    bsb = jnp.broadcast_to(sgu_b_s[0][:, :, None], (HEADS, CHUNK, HEAD_DIM))
    gf = final_norm_g.reshape(1, D_MODEL)
    me = 4 * lax.axis_index("x") + 2 * lax.axis_index("y") + lax.axis_index("c")
    tr = lambda a: jnp.transpose(a[0])[None]
    tr_cw = lambda a: jnp.transpose(a, (1, 0, 2))

    proj, h1, cos2, sin2, win_g, cw_sh, wout_g, wdn_g, su = _fwd_proj(
        xs, mix_norm_g, _rope_freq(), w_in[0], w_out[0], tr(w_up)[0], w_down[0], tr_cw(conv_w))
    cw_g = jnp.transpose(cw_sh, (1, 0, 2)).reshape(8, 2 * D_FF)
    x2, mixcat, o, sprev, wup_g = _fwd_mix(xs, proj, wout_g, grn, lng, lnb, ws, bsb, su)
    h2, up_pre, u_conv, act, x3, loss_parts = _fwd_ffn(x2, ffn_norm_g, wup_g, cw_g, conv_b, wdn_g, gf, tgt)

    dx3, dpre, dx2, dgf, dg2, dcv = _bwd_ffn(x3, tgt, gf, x2, ffn_norm_g, up_pre, u_conv, wup_g, cw_g, wdn_g)
    band = 512
    (gdn_p,) = _wgrad("wgrad_down", act, dx3, tm=FF_TILE)
    (gout_p,) = _wgrad("wgrad_out", mixcat, dx2, tn=512)
    gup_p, g_dn = _wgrad("wgrad_up", dpre, h2, tm=FF_TILE, tn=512, hosted=[(W_DOWN, gdn_p)])
    dproj, dgrn, dlng, dlnb, dws, dbs, g_up_a, g_out = _bwd_mix(
        dx2, proj, o, sprev, wout_g, grn, lng, lnb, ws, bsb, cos2, sin2,
        [((W_UP, 0, band), gup_p), (W_OUT, gout_p)])
    gin_p, g_up_b = _wgrad("wgrad_in", h1, dproj, tm=512, tn=768, hosted=[((W_UP, band, FF_SHARD - band), gup_p)],
                           col_block=IN_SHARD)
    grad_x, g_in, rp, rws, rcv = _bwd_proj(dproj, win_g, xs, mix_norm_g, dx2, gin_p,
                                           (dg2, dgf, dgrn, dlng, dlnb, dbs, loss_parts, dws, dcv))
    gcw = tr_cw(lax.dynamic_slice(rcv, (me // (N_DEV // 2), 0, (me % (N_DEV // 2)) * FF_SHARD), (1, 3, FF_SHARD)))

    table = {}
    big, (grad_x,) = _adamw([(w_in, [g_in], m_w_in, v_w_in), (w_out, [g_out], m_w_out, v_w_out),
                             (tr(w_up), [g_up_a, g_up_b], tr(m_w_up), tr(v_w_up)), (w_down, [g_dn], m_w_down, v_w_down)],
                            [grad_x], n_steps=4)
    table.update(zip(("w_in", "w_out", "w_up", "w_down"), big))
    table["w_up"] = tuple(tr(a) for a in table["w_up"])
    row = lambda a: a.reshape(1, D_MODEL)
    names_small = ["mix_norm_g", "ffn_norm_g", "final_norm_g", "ret_norm_g", "sgu_ln_g", "sgu_ln_b", "sgu_b_s", "sgu_w_s",
                   "conv_w", "conv_b"]
    params = [(mix_norm_g, m_mix_norm_g, v_mix_norm_g), (ffn_norm_g, m_ffn_norm_g, v_ffn_norm_g),
              (row(final_norm_g), row(m_final_norm_g), row(v_final_norm_g)), (ret_norm_g, m_ret_norm_g, v_ret_norm_g),
              (sgu_ln_g, m_sgu_ln_g, v_sgu_ln_g), (sgu_ln_b, m_sgu_ln_b, v_sgu_ln_b), (sgu_b_s, m_sgu_b_s, v_sgu_b_s),
              (sgu_w_s, m_sgu_w_s, v_sgu_w_s), (tr_cw(conv_w), tr_cw(m_conv_w), tr_cw(v_conv_w)), (conv_b, m_conv_b, v_conv_b)]
    small, loss = _adamw_small(rp, rws, rcv, gcw, params)
    for n, res in zip(names_small, small):
        table[n] = res
    table["final_norm_g"] = tuple(a.reshape(D_MODEL) for a in table["final_norm_g"])
    table["conv_w"] = tuple(tr_cw(a) for a in table["conv_w"])

    order = ["mix_norm_g", "w_in", "ret_norm_g", "sgu_ln_g", "sgu_ln_b", "sgu_w_s", "sgu_b_s", "w_out", "ffn_norm_g", "w_up",
             "conv_w", "conv_b", "w_down", "final_norm_g"]
    outs = [loss.reshape(()), grad_x[None]]
    for col in range(4):
        outs += [table[n][col] for n in order]
    return tuple(outs)
```

```python
import functools
import math

import jax
import jax.numpy as jnp
import numpy as np
from jax import lax
from jax.experimental import pallas as pl
from jax.experimental.pallas import tpu as pltpu

F32 = jnp.float32
BF16 = jnp.bfloat16
MESH = pl.DeviceIdType.MESH

N_DEV = 8
SEQ = 2048
D_MODEL = 1024
CHUNK = 128
N_CHUNK = SEQ // CHUNK
HEADS = 4
HEAD_DIM = 128
RET_W = 512
SGU_W = 512
PROJ_W = 3072
D_FF = 2816
FF_SHARD = 704
FF_TILE = 1408
FF_TILES = ((0, D_FF),)
IN_SHARD = PROJ_W // N_DEV
OUT_SHARD = D_MODEL // N_DEV
DOWN_SHARD = D_FF // N_DEV
TM = 256
N_TB = SEQ // TM
FWD_PROJ_PASS_AT = 5
FWD_MIX_PASS_AT = 10
EPS = 1e-6
ROPE_BASE = 10000.0
K_SCALE = HEAD_DIM ** -0.5
INV_SQRT2 = 0.7071067811865476
INV_SQRT_2PI = 0.3989422804014327

ADAM_LR = 0.001
ADAM_B1 = 0.9
ADAM_B2 = 0.999
ADAM_EPS = 1e-08
ADAM_WD = 0.01
ADAM_STEP = 10

VMEM_LIMIT = 56 * 1024 * 1024


def _cparams(sem=None, vmem=VMEM_LIMIT, collective=None):
    return pltpu.CompilerParams(dimension_semantics=sem, vmem_limit_bytes=vmem, collective_id=collective)


COLLECTIVE = {name: k for k, name in enumerate(("fwd_proj", "fwd_mix", "wgrad_up", "bwd_mix", "wgrad_in", "bwd_proj"))}


class _Meet:
    def __init__(self, diagonal):
        x, y, c = lax.axis_index("x"), lax.axis_index("y"), lax.axis_index("c")
        self.peers = [(x, y, 1 - c), (1 - x, y, c), (x, 1 - y, c)] + ([(1 - x, 1 - y, c)] if diagonal else [])

    def signal(self):
        for peer in self.peers:
            pl.semaphore_signal(pltpu.get_barrier_semaphore(), inc=1, device_id=peer, device_id_type=MESH)

    def wait(self):
        pl.semaphore_wait(pltpu.get_barrier_semaphore(), len(self.peers))


def _resident(shape):
    nd = len(shape)
    return pl.BlockSpec(shape, lambda *_: (0,) * nd, pipeline_mode=pl.Buffered(1))


def _dot(a, b):
    return jnp.dot(a, b, preferred_element_type=F32)


def _dot_nt(a, b):
    return lax.dot_general(a, b, (((1,), (1,)), ((), ())), preferred_element_type=F32)


def _dot_tn(a, b):
    return lax.dot_general(a, b, (((0,), (0,)), ((), ())), preferred_element_type=F32)


def _sigmoid(x):
    return 1.0 / (1.0 + jnp.exp(-x))


def _gelu(x):
    return 0.5 * x * (1.0 + lax.erf(x * INV_SQRT2))


def _gelu_grad(x):
    return 0.5 * (1.0 + lax.erf(x * INV_SQRT2)) + x * (jnp.exp(-0.5 * x * x) * INV_SQRT_2PI)


def _rot(xh, cos2, sin2):
    return xh * cos2 + pltpu.roll(xh, HEAD_DIM // 2, 1) * sin2


def _rot_t(dh, cos2, sin2):
    return dh * cos2 + pltpu.roll(dh * sin2, HEAD_DIM // 2, 1)


def _rope_freq():
    half = HEAD_DIM // 2
    inv_freq = jnp.power(ROPE_BASE, -jnp.arange(half, dtype=F32) / half)
    return jnp.concatenate([inv_freq, inv_freq])[None, :]


def _rope_block(inv2, first_row):
    pos = (lax.broadcasted_iota(jnp.int32, (TM, HEAD_DIM), 0) + first_row).astype(F32)
    ang = pos * inv2
    sin = jnp.sin(ang)
    lane = lax.broadcasted_iota(jnp.int32, (TM, HEAD_DIM), 1)
    return jnp.cos(ang), jnp.where(lane < HEAD_DIM // 2, -sin, sin)


def _log_gamma():
    return np.log(np.float32(1.0) - np.power(np.float32(2.0), -5.0 - np.arange(HEADS, dtype=np.float32))).astype(np.float32)


def _fill_decay(mask_ref, qd_ref, kd_ref):
    assert HEAD_DIM == CHUNK
    lg = _log_gamma()
    t = lax.broadcasted_iota(jnp.int32, (CHUNK, CHUNK), 0).astype(F32)
    diff = t - lax.broadcasted_iota(jnp.int32, (CHUNK, CHUNK), 1).astype(F32)
    for h in range(HEADS):
        mask_ref[h] = jnp.where(diff >= 0.0, jnp.exp(float(lg[h]) * jnp.maximum(diff, 0.0)), 0.0)
        qd_ref[h] = jnp.exp(float(lg[h]) * (t + 1.0))
        kd_ref[h] = jnp.exp(float(lg[h]) * (CHUNK - 1.0 - t))


def _chunk_decay():
    lg = _log_gamma()
    return [float(np.exp(lg[h] * np.float32(CHUNK))) for h in range(HEADS)]


W_IN, W_OUT, W_UP, W_DOWN, W_CONV = range(5)
GATHERED = {W_IN: ((D_MODEL, PROJ_W), BF16), W_OUT: ((D_MODEL, D_MODEL), BF16), W_UP: ((2 * D_FF, D_MODEL), BF16),
            W_DOWN: ((D_FF, D_MODEL), BF16), W_CONV: ((N_DEV, 8, FF_SHARD), F32)}
SHARD = {W_IN: (D_MODEL, IN_SHARD), W_OUT: (OUT_SHARD, D_MODEL), W_UP: (FF_SHARD, D_MODEL), W_DOWN: (DOWN_SHARD, D_MODEL),
         W_CONV: (8, FF_SHARD)}


class _Gather:
    N_SEMS = 9

    def __init__(self, ids, stages, gathered, send_sems, recv_sems, local_sems):
        self.ids, self.stages, self.gathered = ids, stages, gathered
        self.send_sems, self.recv_sems, self.local_sems = send_sems, recv_sems, local_sems
        self.x, self.y, self.c = lax.axis_index("x"), lax.axis_index("y"), lax.axis_index("c")
        self.me = (self.x, self.y, self.c)
        self.sibling = (self.x, self.y, 1 - self.c)
        self.chips = [(1 - self.x, self.y), (self.x, 1 - self.y), (1 - self.x, 1 - self.y)]

    def slot(self, n, px, py, pc):
        dev = 4 * px + 2 * py + pc
        w, g = self.ids[n], self.gathered[n]
        if w == W_IN:
            return g.at[:, pl.ds(pl.multiple_of(dev * IN_SHARD, 128), IN_SHARD)]
        if w == W_OUT:
            return g.at[pl.ds(pl.multiple_of(dev * OUT_SHARD, 128), OUT_SHARD), :]
        if w == W_DOWN:
            return g.at[pl.ds(pl.multiple_of(dev * DOWN_SHARD, 32), DOWN_SHARD), :]
        if w == W_UP:
            return g.at[pl.ds(pl.multiple_of(dev * FF_SHARD, 32), FF_SHARD), :]
        return g.at[dev]

    def half(self, n, px, py, pc, h):
        dev = 4 * px + 2 * py + pc
        w, g = self.ids[n], self.gathered[n]
        if w == W_IN:
            return g.at[pl.ds(h * (D_MODEL // 2), D_MODEL // 2), pl.ds(pl.multiple_of(dev * IN_SHARD, 128), IN_SHARD)]
        rows = SHARD[w][0] // 2
        return g.at[pl.ds(pl.multiple_of(dev * SHARD[w][0] + h * rows, 16), rows), :]

    def tree(self, n):
        return self.ids[n] != W_CONV

    def copy(self, n, k, block, to, src=None, h=None):
        ref = self.slot(n, *block) if h is None else self.half(n, *block, h)
        return pltpu.make_async_remote_copy(
            src_ref=ref if src is None else src, dst_ref=ref,
            send_sem=self.send_sems.at[n, k], recv_sem=self.recv_sems.at[n, k], device_id=to, device_id_type=MESH)

    def _mine(self):
        return [pltpu.make_async_copy(self.stages[n], self.slot(n, *self.me), self.local_sems.at[n]) for n in range(len(self.ids))]

    def _first(self):
        out = []
        for n in range(len(self.ids)):
            out.append(self.copy(n, 0, self.me, self.sibling, src=self.stages[n]))
            out += [self.copy(n, 1 + j, self.me, (*chip, self.c), src=self.stages[n])
                    for j, chip in enumerate(self.chips[:2] if self.tree(n) else self.chips)]
        return out

    def start(self):
        for cp in self._mine() + self._first():
            cp.start()

    def _passed(self, j):
        dev = (*self.chips[j], self.c)
        out = []
        for n in range(len(self.ids)):
            if not self.tree(n):
                out.append(self.copy(n, 4 + j, dev, self.sibling))
            elif j < 2:
                out += [self.copy(n, 3 + j, dev, (*self.chips[1 - j], self.c), h=j), self.copy(n, 5 + j, dev, self.sibling)]
            else:
                out += [self.copy(n, 7, dev, self.sibling, h=0), self.copy(n, 8, dev, self.sibling, h=1)]
        return out

    def near(self):
        for j in range(2):
            dev = (*self.chips[j], self.c)
            for n in range(len(self.ids)):
                self.copy(n, 1 + j, dev, self.me).wait_recv()
            for cp in self._passed(j):
                cp.start()

    def finish(self):
        dev = (*self.chips[2], self.c)
        for n in range(len(self.ids)):
            if self.tree(n):
                self.copy(n, 3, dev, self.me, h=0).wait_recv()
                self.copy(n, 4, dev, self.me, h=1).wait_recv()
            else:
                self.copy(n, 3, dev, self.me).wait_recv()
        for cp in self._passed(2):
            cp.start()
        for n in range(len(self.ids)):
            self.copy(n, 0, self.sibling, self.me).wait_recv()
            for j, chip in enumerate(self.chips):
                dev = (*chip, 1 - self.c)
                if not self.tree(n):
                    self.copy(n, 4 + j, dev, self.me).wait_recv()
                elif j < 2:
                    self.copy(n, 5 + j, dev, self.me).wait_recv()
                else:
                    self.copy(n, 7, dev, self.me, h=0).wait_recv()
                    self.copy(n, 8, dev, self.me, h=1).wait_recv()
        for cp in self._mine():
            cp.wait()
        for cp in self._first() + self._passed(0) + self._passed(1) + self._passed(2):
            cp.wait_send()


def _gather_scratch(n):
    return [pltpu.SemaphoreType.DMA((n, _Gather.N_SEMS)), pltpu.SemaphoreType.DMA((n, _Gather.N_SEMS)), pltpu.SemaphoreType.DMA((n,))]


def _gathered_shapes(ids):
    return tuple(jax.ShapeDtypeStruct(*GATHERED[w]) for w in ids)


def _fwd_proj(x, g1, inv2, w_in, w_out, w_up, w_down, conv_w):
    ids_a, ids_b = [W_IN, W_CONV], [W_OUT, W_DOWN]

    def body(x_ref, g_ref, inv_ref, in_hbm, out_hbm, up_hbm, dn_hbm, cw_ref,
             proj_ref, h1_ref, cos_ref, sin_ref, gin, gcw, gout, gdn, su_ref,
             w_vm, s_in, s_cw, s_out, s_dn, f_in, f_out, f_up, f_dn, ld_sems,
             a_send, a_recv, a_local, b_send, b_recv, b_local):
        ag_a = _Gather(ids_a, [s_in, s_cw], [gin, gcw], a_send, a_recv, a_local)
        ag_b = _Gather(ids_b, [s_out, s_dn], [gout, gdn], b_send, b_recv, b_local)

        @pl.when(pl.program_id(0) == 0)
        def _():
            meet = _Meet(diagonal=True)
            meet.signal()
            loads = [pltpu.make_async_copy(src, dst, ld_sems.at[i])
                     for i, (src, dst) in enumerate(((in_hbm, f_in), (out_hbm, f_out), (dn_hbm, f_dn), (up_hbm, f_up)))]
            for cp in loads:
                cp.start()
            s_cw[...] = jnp.zeros_like(s_cw)
            for k in range(3):
                s_cw[k:k + 1, :] = cw_ref[k]
            loads[0].wait()
            s_in[...] = f_in[...].astype(BF16)
            meet.wait()
            ag_a.start()
            loads[1].wait()
            s_out[...] = f_out[...].astype(BF16)
            loads[2].wait()
            s_dn[...] = f_dn[...].astype(BF16)
            ag_a.near()
            ag_b.start()
            loads[3].wait()
            su_ref[...] = f_up[...].astype(BF16)
            ag_a.finish()
            fill = pltpu.make_async_copy(gin, w_vm, ld_sems.at[4])
            fill.start()
            fill.wait()

        pl.when(pl.program_id(0) == FWD_PROJ_PASS_AT)(ag_b.near)

        xb = x_ref[...]
        r = lax.rsqrt(jnp.mean(xb * xb, axis=-1, keepdims=True) + EPS)
        h = ((xb * r) * g_ref[...]).astype(BF16)
        h1_ref[...] = h
        p = _dot(h, w_vm[...])
        c2, s2 = _rope_block(inv_ref[...], pl.program_id(0) * TM)
        cos_ref[...], sin_ref[...] = c2, s2
        for hd in range(HEADS):
            sl = slice(hd * HEAD_DIM, (hd + 1) * HEAD_DIM)
            proj_ref[:, sl] = _rot(p[:, sl], c2, s2)
            ks = slice(RET_W + hd * HEAD_DIM, RET_W + (hd + 1) * HEAD_DIM)
            proj_ref[:, ks] = _rot(p[:, ks], c2, s2) * K_SCALE
        proj_ref[:, 2 * RET_W:] = p[:, 2 * RET_W:]

        pl.when(pl.program_id(0) == N_TB - 1)(ag_b.finish)

    tok = lambda w: pl.BlockSpec((TM, w), lambda i: (i, 0))
    hbm = pl.BlockSpec(memory_space=pl.ANY)
    vm = pl.BlockSpec(memory_space=pltpu.VMEM)
    return pl.pallas_call(
        body, name="fwd_proj", grid=(N_TB,),
        out_shape=(jax.ShapeDtypeStruct((SEQ, PROJ_W), F32), jax.ShapeDtypeStruct((SEQ, D_MODEL), BF16),
                   jax.ShapeDtypeStruct((SEQ, HEAD_DIM), F32), jax.ShapeDtypeStruct((SEQ, HEAD_DIM), F32))
        + _gathered_shapes(ids_a + ids_b) + (jax.ShapeDtypeStruct(SHARD[W_UP], BF16),),
        in_specs=[tok(D_MODEL), _resident((1, D_MODEL)), _resident((1, HEAD_DIM)), hbm, hbm, hbm, hbm, vm],
        out_specs=(tok(PROJ_W), tok(D_MODEL), tok(HEAD_DIM), tok(HEAD_DIM), hbm, hbm, hbm, hbm, vm),
        scratch_shapes=[pltpu.VMEM((D_MODEL, PROJ_W), BF16), pltpu.VMEM(SHARD[W_IN], BF16), pltpu.VMEM(SHARD[W_CONV], F32),
                        pltpu.VMEM(SHARD[W_OUT], BF16), pltpu.VMEM(SHARD[W_DOWN], BF16),
                        pltpu.VMEM(SHARD[W_IN], F32), pltpu.VMEM(SHARD[W_OUT], F32), pltpu.VMEM(SHARD[W_UP], F32),
                        pltpu.VMEM(SHARD[W_DOWN], F32), pltpu.SemaphoreType.DMA((5,))]
        + _gather_scratch(len(ids_a)) + _gather_scratch(len(ids_b)),
        compiler_params=_cparams(("arbitrary",), collective=COLLECTIVE["fwd_proj"]),
    )(x, g1, inv2, w_in, w_out, w_up, w_down, conv_w)


def _causal(w):
    r = lax.broadcasted_iota(jnp.int32, (CHUNK, CHUNK), 0)
    c = lax.broadcasted_iota(jnp.int32, (CHUNK, CHUNK), 1)
    return jnp.where(r >= c, w, 0.0)


def _fwd_mix(x, proj, wout_g, grn, lng, lnb, ws, bsb, su):
    cdec = _chunk_decay()
    ids = [W_UP]

    def body(x_ref, p_ref, w_ref, grn_ref, lng_ref, lnb_ref, ws_ref, bsb_ref, su_ref,
             x2_ref, cat_ref, o_ref, sp_ref, gup, state, m_ref, qd_ref, kd_ref, send_sems, recv_sems, local_sems):
        ag = _Gather(ids, [su_ref], [gup], send_sems, recv_sems, local_sems)

        @pl.when(pl.program_id(0) == 0)
        def _():
            meet = _Meet(diagonal=False)
            meet.signal()
            state[...] = jnp.zeros_like(state)
            _fill_decay(m_ref, qd_ref, kd_ref)
            meet.wait()
            ag.start()

        for h in range(HEADS):
            sl = slice(h * HEAD_DIM, (h + 1) * HEAD_DIM)
            q = p_ref[:, sl]
            k = p_ref[:, RET_W + h * HEAD_DIM:RET_W + (h + 1) * HEAD_DIM]
            v = p_ref[:, 2 * RET_W + h * HEAD_DIM:2 * RET_W + (h + 1) * HEAD_DIM]
            g = p_ref[:, 3 * RET_W + h * HEAD_DIM:3 * RET_W + (h + 1) * HEAD_DIM]
            qb, kb, vb = q.astype(BF16), k.astype(BF16), v.astype(BF16)
            a = _dot_nt(qb, kb) * m_ref[h]
            spb = state[h].astype(BF16)
            sp_ref[0, h] = spb
            o = _dot(a.astype(BF16), vb) + _dot((q * qd_ref[h]).astype(BF16), spb)
            state[h] = state[h] * cdec[h] + _dot_tn((k * kd_ref[h]).astype(BF16), vb)
            o_ref[:, sl] = o
            rinv = lax.rsqrt(jnp.mean(o * o, axis=-1, keepdims=True) + EPS)
            rn = (o * rinv) * grn_ref[:, sl]
            cat_ref[:, sl] = ((g * _sigmoid(g)) * rn).astype(BF16)
        for gi in range(HEADS):
            sl = slice(gi * HEAD_DIM, (gi + 1) * HEAD_DIM)
            u = p_ref[:, 4 * RET_W + gi * HEAD_DIM:4 * RET_W + (gi + 1) * HEAD_DIM]
            sv = p_ref[:, 4 * RET_W + SGU_W + gi * HEAD_DIM:4 * RET_W + SGU_W + (gi + 1) * HEAD_DIM]
            gv = _gelu(sv)
            xc = gv - jnp.mean(gv, axis=-1, keepdims=True)
            vn = (xc * lax.rsqrt(jnp.mean(xc * xc, axis=-1, keepdims=True) + EPS)) * lng_ref[:, sl] + lnb_ref[:, sl]
            mixed = _dot(_causal(ws_ref[gi]).astype(BF16), vn.astype(BF16)) + bsb_ref[gi]
            cat_ref[:, RET_W + gi * HEAD_DIM:RET_W + (gi + 1) * HEAD_DIM] = (_gelu(u) * mixed).astype(BF16)
        x2_ref[...] = x_ref[...] + _dot(cat_ref[...], w_ref[...])

        pl.when(pl.program_id(0) == FWD_MIX_PASS_AT)(ag.near)
        pl.when(pl.program_id(0) == N_CHUNK - 1)(ag.finish)

    ch = lambda w: pl.BlockSpec((CHUNK, w), lambda i: (i, 0))
    hcc = (HEADS, CHUNK, CHUNK)
    hbm = pl.BlockSpec(memory_space=pl.ANY)
    return pl.pallas_call(
        body, name="fwd_mix", grid=(N_CHUNK,),
        out_shape=(jax.ShapeDtypeStruct((SEQ, D_MODEL), F32), jax.ShapeDtypeStruct((SEQ, D_MODEL), BF16),
                   jax.ShapeDtypeStruct((SEQ, RET_W), F32), jax.ShapeDtypeStruct((N_CHUNK, HEADS, HEAD_DIM, HEAD_DIM), BF16))
        + _gathered_shapes(ids),
        in_specs=[ch(D_MODEL), ch(PROJ_W), _resident((D_MODEL, D_MODEL)), _resident((1, RET_W)), _resident((1, SGU_W)),
                  _resident((1, SGU_W)), _resident(hcc), _resident(hcc), hbm],
        out_specs=(ch(D_MODEL), ch(D_MODEL), ch(RET_W), pl.BlockSpec((1, HEADS, HEAD_DIM, HEAD_DIM), lambda i: (i, 0, 0, 0)), hbm),
        scratch_shapes=[pltpu.VMEM((HEADS, HEAD_DIM, HEAD_DIM), F32)] + [pltpu.VMEM(hcc, F32)] * 3 + _gather_scratch(len(ids)),
        compiler_params=_cparams(("arbitrary",), collective=COLLECTIVE["fwd_mix"]),
    )(x, proj, wout_g, grn, lng, lnb, ws, bsb, su)


def _conv_taps(p, prev8):
    row = lax.broadcasted_iota(jnp.int32, p.shape, 0)
    p1 = jnp.where(row == 0, prev8[7:8, :], pltpu.roll(p, 1, 0))
    p2 = jnp.where(row == 0, prev8[6:7, :], jnp.where(row == 1, prev8[7:8, :], pltpu.roll(p, 2, 0)))
    return p1, p2


def _fwd_ffn(x2, g2, wup_g, cw_g, cb_g, wdn_g, gf, tgt):
    def body(x_ref, g_ref, wu_ref, cw_ref, cb_ref, wd_ref, gf_ref, t_ref, h2_ref, up_ref, u_ref, act_ref, x3_ref, loss_ref, carry):
        @pl.when(pl.program_id(0) == 0)
        def _():
            carry[...] = jnp.zeros_like(carry)

        xb = x_ref[...]
        r = lax.rsqrt(jnp.mean(xb * xb, axis=-1, keepdims=True) + EPS)
        h = ((xb * r) * g_ref[...]).astype(BF16)
        h2_ref[...] = h
        acc = xb
        for t0, tw in FF_TILES:
            u = []
            for c0 in (t0, D_FF + t0):
                cs = slice(c0, c0 + tw)
                p = _dot_nt(h, wu_ref[pl.ds(c0, tw), :])
                up_ref[:, cs] = p.astype(BF16)
                p1, p2 = _conv_taps(p, carry[:, cs])
                carry[:, cs] = p[TM - 8:, :]
                us = p2 * cw_ref[0:1, cs] + p1 * cw_ref[1:2, cs] + p * cw_ref[2:3, cs] + cb_ref[:, cs]
                u_ref[:, cs] = us.astype(BF16)
                u.append(us)
            a = ((u[0] * _sigmoid(u[0])) * u[1]).astype(BF16)
            act_ref[:, t0:t0 + tw] = a
            acc = acc + _dot(a, wd_ref[pl.ds(t0, tw), :])
        x3_ref[...] = acc
        r3 = lax.rsqrt(jnp.mean(acc * acc, axis=-1, keepdims=True) + EPS)
        diff = (acc * r3) * gf_ref[...] - t_ref[...]
        loss_ref[...] = jnp.full(loss_ref.shape, 0.5 * jnp.sum(jnp.mean(diff * diff, axis=-1)), F32)

    tok = lambda w: pl.BlockSpec((TM, w), lambda i: (i, 0))
    return pl.pallas_call(
        body, name="fwd_ffn", grid=(N_TB,),
        out_shape=(jax.ShapeDtypeStruct((SEQ, D_MODEL), BF16), jax.ShapeDtypeStruct((SEQ, 2 * D_FF), BF16),
                   jax.ShapeDtypeStruct((SEQ, 2 * D_FF), BF16),
                   jax.ShapeDtypeStruct((SEQ, D_FF), BF16), jax.ShapeDtypeStruct((SEQ, D_MODEL), F32),
                   jax.ShapeDtypeStruct((N_TB, 8, 128), F32)),
        in_specs=[tok(D_MODEL), _resident((1, D_MODEL)), _resident((2 * D_FF, D_MODEL)), _resident((8, 2 * D_FF)),
                  _resident((1, 2 * D_FF)), _resident((D_FF, D_MODEL)), _resident((1, D_MODEL)), tok(D_MODEL)],
        out_specs=(tok(D_MODEL), tok(2 * D_FF), tok(2 * D_FF), tok(D_FF), tok(D_MODEL),
                   pl.BlockSpec((1, 8, 128), lambda i: (i, 0, 0))),
        scratch_shapes=[pltpu.VMEM((8, 2 * D_FF), F32)],
        compiler_params=_cparams(("arbitrary",)),
    )(x2, g2, wup_g, cw_g, cb_g, wdn_g, gf, tgt)


def _bwd_ffn(x3, tgt, gf, x2, g2, up_pre, u_conv, wup_g, cw_g, wdn_g):
    def body(x3_ref, t_ref, gf_ref, x2_ref, g2_ref, up_ref, u_ref, wu_ref, cw_ref, wd_ref,
             dx3_ref, dpre_ref, dx2_ref, dgf_ref, dg2_ref, dcv_ref, nxt):
        i = pl.program_id(0)

        @pl.when(i == 0)
        def _():
            nxt[...] = jnp.zeros_like(nxt)
            dgf_ref[...] = jnp.zeros_like(dgf_ref)
            dg2_ref[...] = jnp.zeros_like(dg2_ref)
            dcv_ref[...] = jnp.zeros_like(dcv_ref)

        x3 = x3_ref[...]
        r3 = lax.rsqrt(jnp.mean(x3 * x3, axis=-1, keepdims=True) + EPS)
        xh3 = x3 * r3
        dy = (xh3 * gf_ref[...] - t_ref[...]) * (1.0 / D_MODEL)
        dgf_ref[0:1, :] += jnp.sum(dy * xh3, axis=0, keepdims=True)
        t3 = dy * gf_ref[...]
        dx3 = r3 * (t3 - xh3 * jnp.mean(t3 * xh3, axis=-1, keepdims=True))
        dx3b = dx3.astype(BF16)
        dx3_ref[...] = dx3b
        dh2 = jnp.zeros((TM, D_MODEL), F32)
        for t0, tw in FF_TILES:
            row = lax.broadcasted_iota(jnp.int32, (TM, tw), 0)
            ts = slice(t0, t0 + tw)
            dact = _dot_nt(dx3b, wd_ref[pl.ds(t0, tw), :])
            ua = u_ref[:, ts].astype(F32)
            ub = u_ref[:, D_FF + t0:D_FF + t0 + tw].astype(F32)
            sg = _sigmoid(ua)
            du = [dact * ub * (sg * (1.0 + ua * (1.0 - sg))), dact * (ua * sg)]
            for n in range(2):
                d = du[n]
                c0 = n * D_FF + t0
                cs = slice(c0, c0 + tw)
                nx = nxt[:, cs]
                n1 = jnp.where(row == TM - 1, nx[0:1, :], pltpu.roll(d, TM - 1, 0))
                n2 = jnp.where(row == TM - 2, nx[0:1, :], jnp.where(row == TM - 1, nx[1:2, :], pltpu.roll(d, TM - 2, 0)))
                nxt[:, cs] = d[0:8, :]
                dp = (d * cw_ref[2:3, cs] + n1 * cw_ref[1:2, cs] + n2 * cw_ref[0:1, cs]).astype(BF16)
                dpre_ref[:, cs] = dp
                p = up_ref[:, cs].astype(F32)
                dcv_ref[n, 0:1, ts] += jnp.sum(n2 * p, axis=0, keepdims=True)
                dcv_ref[n, 1:2, ts] += jnp.sum(n1 * p, axis=0, keepdims=True)
                dcv_ref[n, 2:3, ts] += jnp.sum(d * p, axis=0, keepdims=True)
                dcv_ref[n, 3:4, ts] += jnp.sum(d, axis=0, keepdims=True)
                dh2 = dh2 + _dot(dp, wu_ref[pl.ds(c0, tw), :])
        x2 = x2_ref[...]
        r2 = lax.rsqrt(jnp.mean(x2 * x2, axis=-1, keepdims=True) + EPS)
        xh2 = x2 * r2
        dg2_ref[0:1, :] += jnp.sum(dh2 * xh2, axis=0, keepdims=True)
        t2 = dh2 * g2_ref[...]
        dx2_ref[...] = dx3 + r2 * (t2 - xh2 * jnp.mean(t2 * xh2, axis=-1, keepdims=True))

    rev = lambda w: pl.BlockSpec((TM, w), lambda i: (N_TB - 1 - i, 0))
    acc = lambda s: pl.BlockSpec(s, lambda i: (0,) * len(s))
    return pl.pallas_call(
        body, name="bwd_ffn", grid=(N_TB,),
        out_shape=(jax.ShapeDtypeStruct((SEQ, D_MODEL), BF16), jax.ShapeDtypeStruct((SEQ, 2 * D_FF), BF16),
                   jax.ShapeDtypeStruct((SEQ, D_MODEL), F32), jax.ShapeDtypeStruct((8, D_MODEL), F32),
                   jax.ShapeDtypeStruct((8, D_MODEL), F32), jax.ShapeDtypeStruct((2, 8, D_FF), F32)),
        in_specs=[rev(D_MODEL), rev(D_MODEL), _resident((1, D_MODEL)), rev(D_MODEL), _resident((1, D_MODEL)), rev(2 * D_FF),
                  rev(2 * D_FF), _resident((2 * D_FF, D_MODEL)), _resident((8, 2 * D_FF)), _resident((D_FF, D_MODEL))],
        out_specs=(rev(D_MODEL), rev(2 * D_FF), rev(D_MODEL), acc((8, D_MODEL)), acc((8, D_MODEL)), acc((2, 8, D_FF))),
        scratch_shapes=[pltpu.VMEM((8, 2 * D_FF), F32)],
        compiler_params=_cparams(("arbitrary",)),
    )(x3, tgt, gf, x2, g2, up_pre, u_conv, wup_g, cw_g, wdn_g)


def _bwd_mix(dx2, proj, o, sprev, wout_g, grn, lng, lnb, ws, bsb, cos2, sin2, hosted):
    cdec = _chunk_decay()
    geoms = [g for g, _ in hosted]
    n_h = len(hosted)

    def body(dx2_ref, p_ref, o_ref, sp_ref, w_ref, grn_ref, lng_ref, lnb_ref, ws_ref, bsb_ref, cos_ref, sin_ref, *rest):
        dp_ref, dgrn_ref, dlng_ref, dlnb_ref, dws_ref, dbs_ref = rest[n_h:n_h + 6]
        dstate, dbs_acc, m_ref, qd_ref, kd_ref = rest[2 * n_h + 6:2 * n_h + 11]
        i = pl.program_id(0)
        rs = _Scatters(geoms, rest[:n_h], rest[n_h + 6:2 * n_h + 6], rest[2 * n_h + 11:])
        pl.when(i == 0)(rs.phase1)
        pl.when(i == 3)(rs.phase2)
        pl.when(i == 8)(rs.phase2b)

        @pl.when(i == 0)
        def _():
            _fill_decay(m_ref, qd_ref, kd_ref)
            dstate[...] = jnp.zeros_like(dstate)
            dgrn_ref[...] = jnp.zeros_like(dgrn_ref)
            dlng_ref[...] = jnp.zeros_like(dlng_ref)
            dlnb_ref[...] = jnp.zeros_like(dlnb_ref)
            dws_ref[...] = jnp.zeros_like(dws_ref)
            dbs_ref[...] = jnp.zeros_like(dbs_ref)
            dbs_acc[...] = jnp.zeros_like(dbs_acc)

        dmix = _dot_nt(dx2_ref[...].astype(BF16), w_ref[...])
        for h in range(HEADS):
            sl = slice(h * HEAD_DIM, (h + 1) * HEAD_DIM)
            q = p_ref[:, sl]
            k = p_ref[:, RET_W + h * HEAD_DIM:RET_W + (h + 1) * HEAD_DIM]
            v = p_ref[:, 2 * RET_W + h * HEAD_DIM:2 * RET_W + (h + 1) * HEAD_DIM]
            g = p_ref[:, 3 * RET_W + h * HEAD_DIM:3 * RET_W + (h + 1) * HEAD_DIM]
            o = o_ref[:, sl]
            rinv = lax.rsqrt(jnp.mean(o * o, axis=-1, keepdims=True) + EPS)
            oh = o * rinv
            gr = grn_ref[:, sl]
            sg = _sigmoid(g)
            dret = dmix[:, sl]
            dp_ref[:, 3 * RET_W + h * HEAD_DIM:3 * RET_W + (h + 1) * HEAD_DIM] = (
                dret * (oh * gr) * (sg * (1.0 + g * (1.0 - sg)))).astype(BF16)
            drn = dret * (g * sg)
            dgrn_ref[0:1, sl] += jnp.sum(drn * oh, axis=0, keepdims=True)
            t = drn * gr
            do = rinv * (t - oh * jnp.mean(t * oh, axis=-1, keepdims=True))
            qb, kb, vb, dob = q.astype(BF16), k.astype(BF16), v.astype(BF16), do.astype(BF16)
            m = m_ref[h]
            ab = (_dot_nt(qb, kb) * m).astype(BF16)
            dab = (_dot_nt(dob, vb) * m).astype(BF16)
            spb = sp_ref[0, h]
            dsn = dstate[h]
            dsnb = dsn.astype(BF16)
            qdb = (q * qd_ref[h]).astype(BF16)
            kdb = (k * kd_ref[h]).astype(BF16)
            dq = _dot(dab, kb) + _dot_nt(dob, spb) * qd_ref[h]
            dk = _dot_tn(dab, qb) + _dot_nt(vb, dsnb) * kd_ref[h]
            dv = _dot_tn(ab, dob) + _dot(kdb, dsnb)
            dstate[h] = dsn * cdec[h] + _dot_tn(qdb, dob)
            c2, s2 = cos_ref[...], sin_ref[...]
            dp_ref[:, sl] = _rot_t(dq, c2, s2).astype(BF16)
            dp_ref[:, RET_W + h * HEAD_DIM:RET_W + (h + 1) * HEAD_DIM] = _rot_t(dk * K_SCALE, c2, s2).astype(BF16)
            dp_ref[:, 2 * RET_W + h * HEAD_DIM:2 * RET_W + (h + 1) * HEAD_DIM] = dv.astype(BF16)
        for gi in range(HEADS):
            sl = slice(gi * HEAD_DIM, (gi + 1) * HEAD_DIM)
            u = p_ref[:, 4 * RET_W + gi * HEAD_DIM:4 * RET_W + (gi + 1) * HEAD_DIM]
            sv = p_ref[:, 4 * RET_W + SGU_W + gi * HEAD_DIM:4 * RET_W + SGU_W + (gi + 1) * HEAD_DIM]
            gv = _gelu(sv)
            xc = gv - jnp.mean(gv, axis=-1, keepdims=True)
            rstd = lax.rsqrt(jnp.mean(xc * xc, axis=-1, keepdims=True) + EPS)
            xh = xc * rstd
            lg = lng_ref[:, sl]
            vnb = (xh * lg + lnb_ref[:, sl]).astype(BF16)
            wcb = _causal(ws_ref[gi]).astype(BF16)
            mixed = _dot(wcb, vnb) + bsb_ref[gi]
            dsgu = dmix[:, RET_W + gi * HEAD_DIM:RET_W + (gi + 1) * HEAD_DIM]
            dmixed = dsgu * _gelu(u)
            dmb = dmixed.astype(BF16)
            dws_ref[gi] += _causal(_dot_nt(dmb, vnb))
            dbs_acc[gi] += dmixed
            dvn = _dot_tn(wcb, dmb)
            dlng_ref[gi:gi + 1, :] += jnp.sum(dvn * xh, axis=0, keepdims=True)
            dlnb_ref[gi:gi + 1, :] += jnp.sum(dvn, axis=0, keepdims=True)
            dxh = dvn * lg
            dgv = rstd * (dxh - jnp.mean(dxh, axis=-1, keepdims=True) - xh * jnp.mean(dxh * xh, axis=-1, keepdims=True))
            dp_ref[:, 4 * RET_W + gi * HEAD_DIM:4 * RET_W + (gi + 1) * HEAD_DIM] = (dsgu * mixed * _gelu_grad(u)).astype(BF16)
            dp_ref[:, 4 * RET_W + SGU_W + gi * HEAD_DIM:4 * RET_W + SGU_W + (gi + 1) * HEAD_DIM] = (
                dgv * _gelu_grad(sv)).astype(BF16)

        @pl.when(i == N_CHUNK - 1)
        def _():
            for gi in range(HEADS):
                col = jnp.broadcast_to(jnp.sum(dbs_acc[gi], axis=-1, keepdims=True), (CHUNK, CHUNK))
                dbs_ref[gi:gi + 1, :] = jnp.transpose(col)[0:1, :]
            rs.phase3()

    rev = lambda w: pl.BlockSpec((CHUNK, w), lambda i: (N_CHUNK - 1 - i, 0))
    hcc = (HEADS, CHUNK, CHUNK)
    acc = lambda s: pl.BlockSpec(s, lambda i: (0,) * len(s))
    res = pl.pallas_call(
        body, name="bwd_mix", grid=(N_CHUNK,),
        out_shape=(jax.ShapeDtypeStruct((SEQ, PROJ_W), BF16), jax.ShapeDtypeStruct((8, RET_W), F32),
                   jax.ShapeDtypeStruct((8, HEAD_DIM), F32), jax.ShapeDtypeStruct((8, HEAD_DIM), F32),
                   jax.ShapeDtypeStruct(hcc, F32), jax.ShapeDtypeStruct((8, CHUNK), F32)) + _scatter_out_shapes(geoms),
        in_specs=[rev(D_MODEL), rev(PROJ_W), rev(RET_W),
                  pl.BlockSpec((1, HEADS, HEAD_DIM, HEAD_DIM), lambda i: (N_CHUNK - 1 - i, 0, 0, 0)),
                  _resident((D_MODEL, D_MODEL)), _resident((1, RET_W)), _resident((1, SGU_W)), _resident((1, SGU_W)),
                  _resident(hcc), _resident(hcc), rev(HEAD_DIM), rev(HEAD_DIM)]
        + [pl.BlockSpec(memory_space=pl.ANY)] * n_h,
        out_specs=(rev(PROJ_W), acc((8, RET_W)), acc((8, HEAD_DIM)), acc((8, HEAD_DIM)), acc(hcc), acc((8, CHUNK)))
        + _scatter_out_specs(geoms),
        scratch_shapes=[pltpu.VMEM((HEADS, HEAD_DIM, HEAD_DIM), F32), pltpu.VMEM((HEADS, CHUNK, CHUNK), F32)]
        + [pltpu.VMEM(hcc, F32)] * 3 + _scatter_scratch(geoms),
        compiler_params=_cparams(("arbitrary",), collective=COLLECTIVE["bwd_mix"]),
    )(dx2, proj, o, sprev, wout_g, grn, lng, lnb, ws, bsb, cos2, sin2, *[p for _, p in hosted])
    return tuple(res[:6 + n_h])


def _bwd_proj(dproj, win_g, x, g1, dx2, gin_p, small):
    geoms = [W_IN]
    n_s = len(small)

    def body(dp_ref, w_ref, x_ref, g_ref, dx2_ref, gin_ref, *rest):
        small_refs = rest[:n_s]
        dx_ref, rs_out, rp_ref, rws_ref, rcv_ref, dg_ref = rest[n_s:n_s + 6]
        rs_scratch = rest[n_s + 6:n_s + 6 + N_SCATTER_SCRATCH]
        ar_scratch = rest[n_s + 6 + N_SCATTER_SCRATCH:]
        ar_res = ar_scratch[N_SMALL_SCRATCH:]
        ar = _SmallReduce((dg_ref,) + tuple(small_refs), ar_res, ar_scratch[:N_SMALL_SCRATCH])
        rs = _Scatters(geoms, [gin_ref], [rs_out], rs_scratch)
        pl.when(pl.program_id(0) == 0)(lambda: rs.phase1(diagonal=True))
        pl.when(pl.program_id(0) == 1)(rs.phase2)
        pl.when(pl.program_id(0) == 5)(rs.phase2b)

        @pl.when(pl.program_id(0) == 0)
        def _():
            dg_ref[...] = jnp.zeros_like(dg_ref)

        dh = _dot_nt(dp_ref[...], w_ref[...])
        xb = x_ref[...]
        r = lax.rsqrt(jnp.mean(xb * xb, axis=-1, keepdims=True) + EPS)
        xh = xb * r
        dg_ref[0:1, :] += jnp.sum(dh * xh, axis=0, keepdims=True)
        t = dh * g_ref[...]
        dx_ref[...] = dx2_ref[...] + r * (t - xh * jnp.mean(t * xh, axis=-1, keepdims=True))

        @pl.when(pl.program_id(0) == N_TB - 1)
        def _():
            ar.begin()
            rs.phase3()
            ar.end()
            for o_ref, r_ref in zip((rp_ref, rws_ref, rcv_ref), ar_res):
                o_ref[...] = r_ref[...]

    tok = lambda w: pl.BlockSpec((TM, w), lambda i: (i, 0))
    vm = pl.BlockSpec(memory_space=pltpu.VMEM)
    res = pl.pallas_call(
        body, name="bwd_proj", grid=(N_TB,),
        out_shape=(jax.ShapeDtypeStruct((SEQ, D_MODEL), F32),) + _scatter_out_shapes(geoms)
        + tuple(jax.ShapeDtypeStruct(s, F32) for s in SMALL_FULL),
        in_specs=[tok(PROJ_W), _resident((D_MODEL, PROJ_W)), tok(D_MODEL), _resident((1, D_MODEL)), tok(D_MODEL),
                  pl.BlockSpec(memory_space=pl.ANY)] + [vm] * n_s,
        out_specs=(tok(D_MODEL),) + _scatter_out_specs(geoms) + (vm,) * len(SMALL_FULL),
        scratch_shapes=[pltpu.VMEM((8, D_MODEL), F32)] + _scatter_scratch(geoms) + _small_scratch()
        + [pltpu.VMEM(s, F32) for s in SMALL_FULL],
        compiler_params=_cparams(("arbitrary",), collective=COLLECTIVE["bwd_proj"]),
    )(dproj, win_g, x, g1, dx2, gin_p, *small)
    return res


def _wgrad(name, a, b, tm=None, tn=None, hosted=()):
    m_w, n_w = a.shape[-1], b.shape[-1]
    tm = m_w if tm is None else tm
    tn = n_w if tn is None else tn
    n_steps = (m_w // tm) * (n_w // tn)
    geoms = [g for g, _ in hosted]
    n_h = len(hosted)

    def body(a_ref, b_ref, *rest):
        o_ref = rest[n_h]
        if n_h:
            rs = _Scatters(geoms, rest[:n_h], rest[n_h + 1:2 * n_h + 1], rest[2 * n_h + 1:])
            step = pl.program_id(0) * (n_w // tn) + pl.program_id(1)
            pl.when(step == 0)(rs.phase1)
            pl.when(step == 1)(rs.phase2)
            pl.when(step == n_steps // 2)(rs.phase2b)
        o_ref[...] = _dot_tn(a_ref[...].astype(BF16), b_ref[...].astype(BF16)).astype(BF16)
        if n_h:
            pl.when(step == n_steps - 1)(rs.phase3)

    assert not n_h or n_steps >= 4
    res = pl.pallas_call(
        body, name=name, grid=(m_w // tm, n_w // tn),
        out_shape=(jax.ShapeDtypeStruct((m_w, n_w), BF16),) + _scatter_out_shapes(geoms),
        in_specs=[pl.BlockSpec((SEQ, tm), lambda i, j: (0, i)), pl.BlockSpec((SEQ, tn), lambda i, j: (0, j))]
        + [pl.BlockSpec(memory_space=pl.ANY)] * n_h,
        out_specs=(pl.BlockSpec((tm, tn), lambda i, j: (i, j)),) + _scatter_out_specs(geoms),
        scratch_shapes=_scatter_scratch(geoms),
        compiler_params=_cparams(("arbitrary", "arbitrary"), collective=COLLECTIVE[name]) if n_h else _cparams(("parallel", "parallel")),
    )(a, b, *[p for _, p in hosted])
    return tuple(res[:1 + n_h])


def _row_step(half_rows):
    return max(s for s in range(16, 177, 16) if half_rows % s == 0)


class _Scatter:
    def __init__(self, geom, partial, out, land1, mine, stage2, land2, comb, s1_send, s1_recv, s2_send, s2_recv, ld_sems):
        self.w, self.row0, self.shape = _geom(geom)
        self.partial, self.out, self.land1 = partial, out, land1
        self.mine, self.stage2, self.land2, self.comb = mine, stage2, land2, comb
        self.hr = self.shape[0] // 2
        self.step = _row_step(self.hr)
        self.s1_send, self.s1_recv, self.s2_send, self.s2_recv, self.ld_sems = s1_send, s1_recv, s2_send, s2_recv, ld_sems
        self.x, self.y, self.c = lax.axis_index("x"), lax.axis_index("y"), lax.axis_index("c")
        self.sibling = (self.x, self.y, 1 - self.c)
        self.chips = [(self.x, self.y), (1 - self.x, self.y), (self.x, 1 - self.y), (1 - self.x, 1 - self.y)]

    def block(self, px, py, pc):
        dev = 4 * px + 2 * py + pc
        if self.w == W_IN:
            return self.partial.at[:, pl.ds(pl.multiple_of(dev * IN_SHARD, 128), IN_SHARD)]
        if self.w == W_OUT:
            return self.partial.at[pl.ds(pl.multiple_of(dev * OUT_SHARD, 128), OUT_SHARD), :]
        if self.w == W_DOWN:
            return self.partial.at[pl.ds(pl.multiple_of(dev * DOWN_SHARD, 32), DOWN_SHARD), :]
        return self.partial.at[pl.ds(pl.multiple_of(dev * FF_SHARD + self.row0, 32), self.shape[0]), :]

    def copy1(self, k):
        return pltpu.make_async_remote_copy(
            src_ref=self.block(*self.chips[k], 1 - self.c), dst_ref=self.land1.at[k],
            send_sem=self.s1_send.at[k], recv_sem=self.s1_recv.at[k], device_id=self.sibling, device_id_type=MESH)

    STAGE2 = [(1, 0, 1), (3, 0, 1), (2, 1, 2), (3, 1, 2), (1, 1, 1), (2, 0, 2)]

    def copy2(self, j):
        blk, h, to = self.STAGE2[j]
        src = self.comb.at[j - 4] if j >= 4 else self.stage2.at[blk - 1, pl.ds(h * self.hr, self.hr), :]
        return pltpu.make_async_remote_copy(
            src_ref=src, dst_ref=self.land2.at[j], send_sem=self.s2_send.at[j], recv_sem=self.s2_recv.at[j],
            device_id=(*self.chips[to], self.c), device_id_type=MESH)

    def _rows(self, h=None):
        step = self.step
        lo, n = (0, self.shape[0]) if h is None else (h * self.hr, self.hr)
        return [pl.ds(r0, step) for r0 in range(lo, lo + n, step)]

    def load(self, k):
        return pltpu.make_async_copy(self.block(*self.chips[k], self.c), self.mine.at[k], self.ld_sems.at[k])

    def load_mine(self):
        for k in range(4):
            self.load(k).start()

    def phase1(self):
        for k in range(4):
            self.copy1(k).start()

    def phase2(self, k):
        self.copy1(k).wait_recv()
        self.load(k).wait()
        for rs in self._rows():
            s = self.mine[k, rs, :].astype(F32) + self.land1[k, rs, :].astype(F32)
            if k == 0:
                self.out[rs, :] = s
            else:
                self.stage2[k - 1, rs, :] = s.astype(BF16)
        for j in {3: (1, 3), 1: (0,), 2: (2,), 0: ()}[k]:
            self.copy2(j).start()

    def phase2b(self):
        for j, got in ((4, 3), (5, 1)):
            blk, h, _ = self.STAGE2[j]
            self.copy2(got).wait_recv()
            for i, rs in enumerate(self._rows(h)):
                lr = pl.ds(i * self.step, self.step)
                self.comb[j - 4, lr, :] = (self.stage2[blk - 1, rs, :].astype(F32) + self.land2[got, lr, :].astype(F32)).astype(BF16)
            self.copy2(j).start()

    def phase3(self):
        for j in (0, 5, 4, 2):
            self.copy2(j).wait_recv()
        for h, (first, second) in enumerate(((0, 5), (4, 2))):
            for i, rs in enumerate(self._rows(h)):
                lr = pl.ds(i * self.step, self.step)
                self.out[rs, :] = (self.out[rs, :] + self.land2[first, lr, :].astype(F32)) + self.land2[second, lr, :].astype(F32)
        for k in range(4):
            self.copy1(k).wait_send()
        for j in range(6):
            self.copy2(j).wait_send()


def _geom(geom):
    if isinstance(geom, tuple):
        w, row0, rows = geom
        assert w == W_UP
        return w, row0, (rows, SHARD[w][1])
    return geom, 0, SHARD[geom]


N_SCATTER_SCRATCH = 10


def _scatter_out_shapes(geoms):
    return tuple(jax.ShapeDtypeStruct(_geom(g)[2], F32) for g in geoms)


def _scatter_out_specs(geoms):
    return (pl.BlockSpec(memory_space=pltpu.VMEM),) * len(geoms)


def _scatter_scratch(geoms):
    out = []
    for g in geoms:
        s = _geom(g)[2]
        hs = (s[0] // 2, s[1])
        out += [pltpu.VMEM((4,) + s, BF16), pltpu.VMEM((4,) + s, BF16), pltpu.VMEM((3,) + s, BF16), pltpu.VMEM((6,) + hs, BF16),
                pltpu.VMEM((2,) + hs, BF16),
                pltpu.SemaphoreType.DMA((4,)), pltpu.SemaphoreType.DMA((4,)), pltpu.SemaphoreType.DMA((6,)),
                pltpu.SemaphoreType.DMA((6,)), pltpu.SemaphoreType.DMA((4,))]
    return out


class _Scatters:
    def __init__(self, geoms, p_refs, out_refs, scratch):
        k = N_SCATTER_SCRATCH
        self.items = [_Scatter(g, p_refs[i], out_refs[i], *scratch[k * i:k * i + k]) for i, g in enumerate(geoms)]

    def phase1(self, diagonal=False):
        meet = _Meet(diagonal)
        meet.signal()
        for s in self.items:
            s.load_mine()
        meet.wait()
        for s in self.items:
            s.phase1()

    def phase2(self):
        for k in (3, 1, 2, 0):
            for s in self.items:
                s.phase2(k)

    def phase2b(self):
        for s in self.items:
            s.phase2b()

    def phase3(self):
        for s in self.items:
            s.phase3()


PACK_W = 1024


SMALL_FULL = [(2, 8, PACK_W), (HEADS, CHUNK, CHUNK), (2, 8, D_FF)]
SMALL_HALF = [(s[0] // 2,) + s[1:] for s in SMALL_FULL]
N_SMALL_SCRATCH = 16


def _small_scratch():
    n_a = len(SMALL_FULL)
    return ([pltpu.VMEM(SMALL_FULL[0], F32)] + [pltpu.VMEM(s, F32) for s in SMALL_HALF] + [pltpu.VMEM(s, F32) for s in SMALL_HALF]
            + [pltpu.VMEM((3,) + s, F32) for s in SMALL_HALF]
            + [pltpu.SemaphoreType.DMA((n_a,)), pltpu.SemaphoreType.DMA((n_a,)), pltpu.SemaphoreType.DMA((n_a, 3)),
               pltpu.SemaphoreType.DMA((n_a, 3)), pltpu.SemaphoreType.DMA((n_a,)), pltpu.SemaphoreType.DMA((n_a,))])


class _SmallReduce:
    def __init__(self, ins, outs, scratch):
        self.ins, self.outs = ins, outs
        (self.pack, *rest) = scratch
        self.rxs, self.css, self.gs = rest[0:3], rest[3:6], rest[6:9]
        self.s1_send, self.s1_recv, self.s2_send, self.s2_recv, self.s3_send, self.s3_recv = rest[9:]
        self.x, self.y, self.c = lax.axis_index("x"), lax.axis_index("y"), lax.axis_index("c")
        self.sibling = (self.x, self.y, 1 - self.c)
        self.chips = [(1 - self.x, self.y), (self.x, 1 - self.y), (1 - self.x, 1 - self.y)]
        self.hl = [s[0] for s in SMALL_HALF]

    def half(self, ref, a, h):
        return ref.at[pl.ds(h * self.hl[a], self.hl[a])]

    def begin(self):
        dg1_ref, dg2_ref, dgf_ref, dgrn_ref, dlng_ref, dlnb_ref, dbs_ref, loss_ref, dws_ref, dcv_ref = self.ins
        pack, c = self.pack, self.c
        pack[...] = jnp.zeros_like(pack)
        pack[0, 0:1, :] = dg1_ref[0:1, :]
        pack[0, 1:2, :] = dg2_ref[0:1, :]
        pack[0, 2:3, :] = dgf_ref[0:1, :]
        pack[0, 3:4, 0:RET_W] = dgrn_ref[0:1, :]
        lsum = loss_ref[0, 0:1, :]
        for i in range(1, N_TB):
            lsum = lsum + loss_ref[i, 0:1, :]
        pack[0, 3:4, RET_W:RET_W + 128] = lsum
        pack[1, 0:HEADS, 0:128] = dlng_ref[0:HEADS, :]
        pack[1, 0:HEADS, 128:256] = dlnb_ref[0:HEADS, :]
        pack[1, 0:HEADS, 256:384] = dbs_ref[0:HEADS, :]
        self.srcs = [pack, dws_ref, dcv_ref]
        n_a = len(self.srcs)
        self.ex1 = [pltpu.make_async_remote_copy(src_ref=self.half(self.srcs[a], a, 1 - c), dst_ref=self.rxs[a],
                                                 send_sem=self.s1_send.at[a], recv_sem=self.s1_recv.at[a],
                                                 device_id=self.sibling, device_id_type=MESH) for a in range(n_a)]
        for cp in self.ex1:
            cp.start()
        self.ex2 = []
        for a in range(n_a):
            self.ex1[a].wait_recv()
            self.css[a][...] = self.half(self.srcs[a], a, c)[...] + self.rxs[a][...]
            for j, chip in enumerate(self.chips):
                cp = pltpu.make_async_remote_copy(src_ref=self.css[a], dst_ref=self.gs[a].at[j], send_sem=self.s2_send.at[a, j],
                                                  recv_sem=self.s2_recv.at[a, j], device_id=(*chip, c), device_id_type=MESH)
                cp.start()
                self.ex2.append(cp)

    def end(self):
        c, x, y = self.c, self.x, self.y
        ex3 = []
        for a in range(len(self.srcs)):
            css, gs, out = self.css[a], self.gs[a], self.outs[a]
            for j in range(3):
                self.ex2[3 * a + j].wait_recv()
            tot = None
            for q in range(4):
                k = jnp.where(x != (q >> 1), 1, 0) + jnp.where(y != (q & 1), 2, 0)
                term = jnp.where(k == 0, css[...], jnp.where(k == 1, gs[0], jnp.where(k == 2, gs[1], gs[2])))
                tot = term if tot is None else tot + term
            self.half(out, a, c)[...] = tot
            cp = pltpu.make_async_remote_copy(src_ref=self.half(out, a, c), dst_ref=self.half(out, a, c), send_sem=self.s3_send.at[a],
                                              recv_sem=self.s3_recv.at[a], device_id=self.sibling, device_id_type=MESH)
            cp.start()
            ex3.append(cp)
        for a in range(len(self.srcs)):
            out = self.outs[a]
            pltpu.make_async_remote_copy(src_ref=self.half(out, a, 1 - c), dst_ref=self.half(out, a, 1 - c), send_sem=self.s3_send.at[a],
                                         recv_sem=self.s3_recv.at[a], device_id=self.sibling, device_id_type=MESH).wait_recv()
        for cp in self.ex1 + self.ex2 + ex3:
            cp.wait_send()


def _adam_math(w, g, m, v):
    nm = ADAM_B1 * m + (1.0 - ADAM_B1) * g
    nv = ADAM_B2 * v + (1.0 - ADAM_B2) * (g * g)
    d = -ADAM_LR * ((nm / (1.0 - ADAM_B1 ** ADAM_STEP)) / (jnp.sqrt(nv / (1.0 - ADAM_B2 ** ADAM_STEP)) + ADAM_EPS) + ADAM_WD * w)
    return d, nm, nv


def _adamw(params, thru, n_steps):
    plan = []
    for w, gs, _, _ in params:
        _, r, cdim = w.shape
        if len(gs) == 1:
            edges = [0, r // n_steps]
            spec3 = pl.BlockSpec((1, r // n_steps, cdim), lambda i: (0, i, 0))
            g_specs = [pl.BlockSpec((r // n_steps, cdim), lambda i: (i, 0))]
        else:
            edges = [sum(g.shape[0] for g in gs[:k]) for k in range(len(gs) + 1)]
            spec3 = pl.BlockSpec((1, r, cdim // n_steps), lambda i: (0, 0, i))
            g_specs = [pl.BlockSpec((g.shape[0], cdim // n_steps), lambda i: (0, i)) for g in gs]
        plan.append((len(gs), edges, spec3, g_specs))
    n_in = sum(n_g + 3 for n_g, _, _, _ in plan)
    n_t = len(thru)

    def body(*refs):
        ins, outs = refs[:n_in], refs[n_in + n_t:]
        for n_g, edges, _, _ in plan:
            (w_ref, *g_refs, m_ref, v_ref), ins = ins[:n_g + 3], ins[n_g + 3:]
            (go_ref, d_ref, nm_ref, nv_ref), outs = outs[:4], outs[4:]
            for g_ref, lo, hi in zip(g_refs, edges[:-1], edges[1:]):
                gg = g_ref[...]
                go_ref[0, lo:hi, :] = gg
                d_ref[0, lo:hi, :], nm_ref[0, lo:hi, :], nv_ref[0, lo:hi, :] = _adam_math(
                    w_ref[0, lo:hi, :], gg, m_ref[0, lo:hi, :], v_ref[0, lo:hi, :])
        for t_ref, to_ref in zip(refs[n_in:n_in + n_t], outs):
            to_ref[...] = t_ref[...]

    t_specs = [pl.BlockSpec((t.shape[0] // n_steps, t.shape[1]), lambda i: (i, 0)) for t in thru]
    in_specs, out_specs, out_shape, args = [], [], [], []
    for (w, gs, m, v), (_, _, spec3, g_specs) in zip(params, plan):
        in_specs += [spec3] + g_specs + [spec3, spec3]
        out_specs += [spec3] * 4
        out_shape += [jax.ShapeDtypeStruct(w.shape, F32)] * 4
        args += [w, *gs, m, v]
    res = pl.pallas_call(
        body, name="adamw", grid=(n_steps,), out_shape=tuple(out_shape) + tuple(jax.ShapeDtypeStruct(t.shape, t.dtype) for t in thru),
        in_specs=in_specs + t_specs, out_specs=tuple(out_specs) + tuple(t_specs),
        compiler_params=_cparams(("parallel",)),
    )(*args, *thru)
    return [res[4 * k:4 * k + 4] for k in range(len(params))], res[4 * len(params):]


def _adamw_small(rp, rws, rcv, gcw, params):
    n_p = len(params)

    def body(*refs):
        rp_ref, rws_ref, rcv_ref, gcw_ref = refs[:4]
        ins = refs[4:4 + 3 * n_p]
        outs = refs[4 + 3 * n_p:]
        outs[4 * n_p][...] = rp_ref[0, 3:4, RET_W:RET_W + 1]
        grads = [rp_ref[0, 0:1, :], rp_ref[0, 1:2, :], rp_ref[0, 2:3, :], rp_ref[0, 3:4, 0:RET_W],
                 rp_ref[1, 0:HEADS, 0:128], rp_ref[1, 0:HEADS, 128:256], rp_ref[1, 0:HEADS, 256:384],
                 rws_ref[...], gcw_ref[...], None]
        for p in range(n_p):
            w_ref, m_ref, v_ref = ins[3 * p:3 * p + 3]
            o = outs[4 * p:4 * p + 4]
            if p == n_p - 1:
                for hf in range(2):
                    cs = slice(hf * D_FF, (hf + 1) * D_FF)
                    g = rcv_ref[hf, 3:4, :]
                    res = (g,) + _adam_math(w_ref[:, cs], g, m_ref[:, cs], v_ref[:, cs])
                    for t in range(4):
                        o[t][:, cs] = res[t]
                continue
            lead = w_ref.ndim > grads[p].ndim
            rd = (lambda r: r[0]) if lead else (lambda r: r[...])
            res = (grads[p],) + _adam_math(rd(w_ref), grads[p], rd(m_ref), rd(v_ref))
            for t in range(4):
                if lead:
                    o[t][0] = res[t]
                else:
                    o[t][...] = res[t]

    vm = pl.BlockSpec(memory_space=pltpu.VMEM)
    flat = [a for tr in params for a in tr]
    out_shape = tuple(jax.ShapeDtypeStruct(tr[0].shape, F32) for tr in params for _ in range(4)) + (jax.ShapeDtypeStruct((1, 1), F32),)
    res = pl.pallas_call(
        body, name="adamw_small", out_shape=out_shape, in_specs=[vm] * (4 + len(flat)), out_specs=(vm,) * len(out_shape),
        compiler_params=_cparams(),
    )(rp, rws, rcv, gcw, *flat)
    return [res[4 * p:4 * p + 4] for p in range(n_p)], res[4 * n_p]


def kernel(x, mix_norm_g, w_in, ret_norm_g, sgu_ln_g, sgu_ln_b, sgu_w_s, sgu_b_s, w_out, ffn_norm_g, w_up, conv_w, conv_b, w_down, final_norm_g, loss_target, m_mix_norm_g, m_w_in, m_ret_norm_g, m_sgu_ln_g, m_sgu_ln_b, m_sgu_w_s, m_sgu_b_s, m_w_out, m_ffn_norm_g, m_w_up, m_conv_w, m_conv_b, m_w_down, m_final_norm_g, v_mix_norm_g, v_w_in, v_ret_norm_g, v_sgu_ln_g, v_sgu_ln_b, v_sgu_w_s, v_sgu_b_s, v_w_out, v_ffn_norm_g, v_w_up, v_conv_w, v_conv_b, v_w_down, v_final_norm_g):
    xs = x[0]
    tgt = loss_target[0]
    grn = ret_norm_g.reshape(1, RET_W)
    lng = sgu_ln_g.reshape(1, SGU_W)
    lnb = sgu_ln_b.reshape(1, SGU_W)
    ws = sgu_w_s[0]
    bsb = jnp.broadcast_to(sgu_b_s[0][:, :, None], (HEADS, CHUNK, HEAD_DIM))
    gf = final_norm_g.reshape(1, D_MODEL)
    me = 4 * lax.axis_index("x") + 2 * lax.axis_index("y") + lax.axis_index("c")
    tr = lambda a: jnp.transpose(a[0])[None]
    tr_cw = lambda a: jnp.transpose(a, (1, 0, 2))

    proj, h1, cos2, sin2, win_g, cw_sh, wout_g, wdn_g, su = _fwd_proj(
        xs, mix_norm_g, _rope_freq(), w_in[0], w_out[0], tr(w_up)[0], w_down[0], tr_cw(conv_w))
    cw_g = jnp.transpose(cw_sh, (1, 0, 2)).reshape(8, 2 * D_FF)
    x2, mixcat, o, sprev, wup_g = _fwd_mix(xs, proj, wout_g, grn, lng, lnb, ws, bsb, su)
    h2, up_pre, u_conv, act, x3, loss_parts = _fwd_ffn(x2, ffn_norm_g, wup_g, cw_g, conv_b, wdn_g, gf, tgt)

    dx3, dpre, dx2, dgf, dg2, dcv = _bwd_ffn(x3, tgt, gf, x2, ffn_norm_g, up_pre, u_conv, wup_g, cw_g, wdn_g)
    band = 576
    (gdn_p,) = _wgrad("wgrad_down", act, dx3, tm=FF_TILE)
    (gout_p,) = _wgrad("wgrad_out", mixcat, dx2, tn=512)
    gup_p, g_dn = _wgrad("wgrad_up", dpre, h2, tm=FF_TILE, tn=512, hosted=[(W_DOWN, gdn_p)])
    dproj, dgrn, dlng, dlnb, dws, dbs, g_up_a, g_out = _bwd_mix(
        dx2, proj, o, sprev, wout_g, grn, lng, lnb, ws, bsb, cos2, sin2,
        [((W_UP, 0, band), gup_p), (W_OUT, gout_p)])
    gin_p, g_up_b = _wgrad("wgrad_in", h1, dproj, tm=512, tn=768, hosted=[((W_UP, band, FF_SHARD - band), gup_p)])
    grad_x, g_in, rp, rws, rcv = _bwd_proj(dproj, win_g, xs, mix_norm_g, dx2, gin_p,
                                           (dg2, dgf, dgrn, dlng, dlnb, dbs, loss_parts, dws, dcv))
    gcw = tr_cw(lax.dynamic_slice(rcv, (me // (N_DEV // 2), 0, (me % (N_DEV // 2)) * FF_SHARD), (1, 3, FF_SHARD)))

    table = {}
    big, (grad_x,) = _adamw([(w_in, [g_in], m_w_in, v_w_in), (w_out, [g_out], m_w_out, v_w_out),
                             (tr(w_up), [g_up_a, g_up_b], tr(m_w_up), tr(v_w_up)), (w_down, [g_dn], m_w_down, v_w_down)],
                            [grad_x], n_steps=4)
    table.update(zip(("w_in", "w_out", "w_up", "w_down"), big))
    table["w_up"] = tuple(tr(a) for a in table["w_up"])
    row = lambda a: a.reshape(1, D_MODEL)
    names_small = ["mix_norm_g", "ffn_norm_g", "final_norm_g", "ret_norm_g", "sgu_ln_g", "sgu_ln_b", "sgu_b_s", "sgu_w_s",
                   "conv_w", "conv_b"]
    params = [(mix_norm_g, m_mix_norm_g, v_mix_norm_g), (ffn_norm_g, m_ffn_norm_g, v_ffn_norm_g),
              (row(final_norm_g), row(m_final_norm_g), row(v_final_norm_g)), (ret_norm_g, m_ret_norm_g, v_ret_norm_g),
              (sgu_ln_g, m_sgu_ln_g, v_sgu_ln_g), (sgu_ln_b, m_sgu_ln_b, v_sgu_ln_b), (sgu_b_s, m_sgu_b_s, v_sgu_b_s),
              (sgu_w_s, m_sgu_w_s, v_sgu_w_s), (tr_cw(conv_w), tr_cw(m_conv_w), tr_cw(v_conv_w)), (conv_b, m_conv_b, v_conv_b)]
    small, loss = _adamw_small(rp, rws, rcv, gcw, params)
    for n, res in zip(names_small, small):
        table[n] = res
    table["final_norm_g"] = tuple(a.reshape(D_MODEL) for a in table["final_norm_g"])
    table["conv_w"] = tuple(tr_cw(a) for a in table["conv_w"])

    order = ["mix_norm_g", "w_in", "ret_norm_g", "sgu_ln_g", "sgu_ln_b", "sgu_w_s", "sgu_b_s", "w_out", "ffn_norm_g", "w_up",
             "conv_w", "conv_b", "w_down", "final_norm_g"]
    outs = [loss.reshape(()), grad_x[None]]
    for col in range(4):
        outs += [table[n][col] for n in order]
    return tuple(outs)
```

```python
import functools
import math

import jax
import jax.numpy as jnp
import numpy as np
from jax import lax
from jax.experimental import pallas as pl
from jax.experimental.pallas import tpu as pltpu

F32 = jnp.float32
BF16 = jnp.bfloat16
MESH = pl.DeviceIdType.MESH

N_DEV = 8
SEQ = 2048
D_MODEL = 1024
CHUNK = 128
N_CHUNK = SEQ // CHUNK
HEADS = 4
HEAD_DIM = 128
RET_W = 512
SGU_W = 512
PROJ_W = 3072
D_FF = 2816
FF_SHARD = 704
FF_TILE = 1408
FF_TILES = ((0, D_FF),)
IN_SHARD = PROJ_W // N_DEV
OUT_SHARD = D_MODEL // N_DEV
DOWN_SHARD = D_FF // N_DEV
TM = 256
N_TB = SEQ // TM
FWD_PROJ_PASS_AT = 5
FWD_MIX_PASS_AT = 10
EPS = 1e-6
ROPE_BASE = 10000.0
K_SCALE = HEAD_DIM ** -0.5
INV_SQRT2 = 0.7071067811865476
INV_SQRT_2PI = 0.3989422804014327

ADAM_LR = 0.001
ADAM_B1 = 0.9
ADAM_B2 = 0.999
ADAM_EPS = 1e-08
ADAM_WD = 0.01
ADAM_STEP = 10

VMEM_LIMIT = 56 * 1024 * 1024


def _cparams(sem=None, vmem=VMEM_LIMIT, collective=None):
    return pltpu.CompilerParams(dimension_semantics=sem, vmem_limit_bytes=vmem, collective_id=collective)


COLLECTIVE = {name: k for k, name in enumerate(("fwd_proj", "fwd_mix", "wgrad_up", "bwd_mix", "wgrad_in", "bwd_proj"))}


class _Meet:
    def __init__(self, diagonal):
        x, y, c = lax.axis_index("x"), lax.axis_index("y"), lax.axis_index("c")
        self.peers = [(x, y, 1 - c), (1 - x, y, c), (x, 1 - y, c)] + ([(1 - x, 1 - y, c)] if diagonal else [])

    def signal(self):
        for peer in self.peers:
            pl.semaphore_signal(pltpu.get_barrier_semaphore(), inc=1, device_id=peer, device_id_type=MESH)

    def wait(self):
        pl.semaphore_wait(pltpu.get_barrier_semaphore(), len(self.peers))


def _resident(shape):
    nd = len(shape)
    return pl.BlockSpec(shape, lambda *_: (0,) * nd, pipeline_mode=pl.Buffered(1))


def _dot(a, b):
    return jnp.dot(a, b, preferred_element_type=F32)


def _dot_nt(a, b):
    return lax.dot_general(a, b, (((1,), (1,)), ((), ())), preferred_element_type=F32)


def _dot_tn(a, b):
    return lax.dot_general(a, b, (((0,), (0,)), ((), ())), preferred_element_type=F32)


def _sigmoid(x):
    return 1.0 / (1.0 + jnp.exp(-x))


def _gelu(x):
    return 0.5 * x * (1.0 + lax.erf(x * INV_SQRT2))


def _gelu_grad(x):
    return 0.5 * (1.0 + lax.erf(x * INV_SQRT2)) + x * (jnp.exp(-0.5 * x * x) * INV_SQRT_2PI)


def _rot(xh, cos2, sin2):
    return xh * cos2 + pltpu.roll(xh, HEAD_DIM // 2, 1) * sin2


def _rot_t(dh, cos2, sin2):
    return dh * cos2 + pltpu.roll(dh * sin2, HEAD_DIM // 2, 1)


def _rope_freq():
    half = HEAD_DIM // 2
    inv_freq = jnp.power(ROPE_BASE, -jnp.arange(half, dtype=F32) / half)
    return jnp.concatenate([inv_freq, inv_freq])[None, :]


def _rope_block(inv2, first_row):
    pos = (lax.broadcasted_iota(jnp.int32, (TM, HEAD_DIM), 0) + first_row).astype(F32)
    ang = pos * inv2
    sin = jnp.sin(ang)
    lane = lax.broadcasted_iota(jnp.int32, (TM, HEAD_DIM), 1)
    return jnp.cos(ang), jnp.where(lane < HEAD_DIM // 2, -sin, sin)


def _log_gamma():
    return np.log(np.float32(1.0) - np.power(np.float32(2.0), -5.0 - np.arange(HEADS, dtype=np.float32))).astype(np.float32)


def _fill_decay(mask_ref, qd_ref, kd_ref):
    assert HEAD_DIM == CHUNK
    lg = _log_gamma()
    t = lax.broadcasted_iota(jnp.int32, (CHUNK, CHUNK), 0).astype(F32)
    diff = t - lax.broadcasted_iota(jnp.int32, (CHUNK, CHUNK), 1).astype(F32)
    for h in range(HEADS):
        mask_ref[h] = jnp.where(diff >= 0.0, jnp.exp(float(lg[h]) * jnp.maximum(diff, 0.0)), 0.0)
        qd_ref[h] = jnp.exp(float(lg[h]) * (t + 1.0))
        kd_ref[h] = jnp.exp(float(lg[h]) * (CHUNK - 1.0 - t))


def _chunk_decay():
    lg = _log_gamma()
    return [float(np.exp(lg[h] * np.float32(CHUNK))) for h in range(HEADS)]


W_IN, W_OUT, W_UP, W_DOWN, W_CONV = range(5)
GATHERED = {W_IN: ((D_MODEL, PROJ_W), BF16), W_OUT: ((D_MODEL, D_MODEL), BF16), W_UP: ((2 * D_FF, D_MODEL), BF16),
            W_DOWN: ((D_FF, D_MODEL), BF16), W_CONV: ((N_DEV, 8, FF_SHARD), F32)}
SHARD = {W_IN: (D_MODEL, IN_SHARD), W_OUT: (OUT_SHARD, D_MODEL), W_UP: (FF_SHARD, D_MODEL), W_DOWN: (DOWN_SHARD, D_MODEL),
         W_CONV: (8, FF_SHARD)}


class _Gather:
    N_SEMS = 9

    def __init__(self, ids, stages, gathered, send_sems, recv_sems, local_sems):
        self.ids, self.stages, self.gathered = ids, stages, gathered
        self.send_sems, self.recv_sems, self.local_sems = send_sems, recv_sems, local_sems
        self.x, self.y, self.c = lax.axis_index("x"), lax.axis_index("y"), lax.axis_index("c")
        self.me = (self.x, self.y, self.c)
        self.sibling = (self.x, self.y, 1 - self.c)
        self.chips = [(1 - self.x, self.y), (self.x, 1 - self.y), (1 - self.x, 1 - self.y)]

    def slot(self, n, px, py, pc):
        dev = 4 * px + 2 * py + pc
        w, g = self.ids[n], self.gathered[n]
        if w == W_IN:
            return g.at[:, pl.ds(pl.multiple_of(dev * IN_SHARD, 128), IN_SHARD)]
        if w == W_OUT:
            return g.at[pl.ds(pl.multiple_of(dev * OUT_SHARD, 128), OUT_SHARD), :]
        if w == W_DOWN:
            return g.at[pl.ds(pl.multiple_of(dev * DOWN_SHARD, 32), DOWN_SHARD), :]
        if w == W_UP:
            return g.at[pl.ds(pl.multiple_of(dev * FF_SHARD, 32), FF_SHARD), :]
        return g.at[dev]

    def half(self, n, px, py, pc, h):
        dev = 4 * px + 2 * py + pc
        w, g = self.ids[n], self.gathered[n]
        if w == W_IN:
            return g.at[pl.ds(h * (D_MODEL // 2), D_MODEL // 2), pl.ds(pl.multiple_of(dev * IN_SHARD, 128), IN_SHARD)]
        rows = SHARD[w][0] // 2
        return g.at[pl.ds(pl.multiple_of(dev * SHARD[w][0] + h * rows, 16), rows), :]

    def tree(self, n):
        return self.ids[n] != W_CONV

    def copy(self, n, k, block, to, src=None, h=None):
        ref = self.slot(n, *block) if h is None else self.half(n, *block, h)
        return pltpu.make_async_remote_copy(
            src_ref=ref if src is None else src, dst_ref=ref,
            send_sem=self.send_sems.at[n, k], recv_sem=self.recv_sems.at[n, k], device_id=to, device_id_type=MESH)

    def _mine(self):
        return [pltpu.make_async_copy(self.stages[n], self.slot(n, *self.me), self.local_sems.at[n]) for n in range(len(self.ids))]

    def _first(self):
        out = []
        for n in range(len(self.ids)):
            out.append(self.copy(n, 0, self.me, self.sibling, src=self.stages[n]))
            out += [self.copy(n, 1 + j, self.me, (*chip, self.c), src=self.stages[n])
                    for j, chip in enumerate(self.chips[:2] if self.tree(n) else self.chips)]
        return out

    def start(self):
        for cp in self._mine() + self._first():
            cp.start()

    def _passed(self, j):
        dev = (*self.chips[j], self.c)
        out = []
        for n in range(len(self.ids)):
            if not self.tree(n):
                out.append(self.copy(n, 4 + j, dev, self.sibling))
            elif j < 2:
                out += [self.copy(n, 3 + j, dev, (*self.chips[1 - j], self.c), h=j), self.copy(n, 5 + j, dev, self.sibling)]
            else:
                out += [self.copy(n, 7, dev, self.sibling, h=0), self.copy(n, 8, dev, self.sibling, h=1)]
        return out

    def near(self):
        for j in range(2):
            dev = (*self.chips[j], self.c)
            for n in range(len(self.ids)):
                self.copy(n, 1 + j, dev, self.me).wait_recv()
            for cp in self._passed(j):
                cp.start()

    def finish(self):
        dev = (*self.chips[2], self.c)
        for n in range(len(self.ids)):
            if self.tree(n):
                self.copy(n, 3, dev, self.me, h=0).wait_recv()
                self.copy(n, 4, dev, self.me, h=1).wait_recv()
            else:
                self.copy(n, 3, dev, self.me).wait_recv()
        for cp in self._passed(2):
            cp.start()
        for n in range(len(self.ids)):
            self.copy(n, 0, self.sibling, self.me).wait_recv()
            for j, chip in enumerate(self.chips):
                dev = (*chip, 1 - self.c)
                if not self.tree(n):
                    self.copy(n, 4 + j, dev, self.me).wait_recv()
                elif j < 2:
                    self.copy(n, 5 + j, dev, self.me).wait_recv()
                else:
                    self.copy(n, 7, dev, self.me, h=0).wait_recv()
                    self.copy(n, 8, dev, self.me, h=1).wait_recv()
        for cp in self._mine():
            cp.wait()
        for cp in self._first() + self._passed(0) + self._passed(1) + self._passed(2):
            cp.wait_send()


def _gather_scratch(n):
    return [pltpu.SemaphoreType.DMA((n, _Gather.N_SEMS)), pltpu.SemaphoreType.DMA((n, _Gather.N_SEMS)), pltpu.SemaphoreType.DMA((n,))]


def _gathered_shapes(ids):
    return tuple(jax.ShapeDtypeStruct(*GATHERED[w]) for w in ids)


def _fwd_proj(x, g1, inv2, w_in, w_out, w_up, w_down, conv_w):
    ids_a, ids_b = [W_IN, W_CONV], [W_OUT, W_DOWN]

    def body(x_ref, g_ref, inv_ref, in_hbm, out_hbm, up_hbm, dn_hbm, cw_ref,
             proj_ref, h1_ref, cos_ref, sin_ref, gin, gcw, gout, gdn, su_ref,
             w_vm, s_in, s_cw, s_out, s_dn, f_in, f_out, f_up, f_dn, ld_sems,
             a_send, a_recv, a_local, b_send, b_recv, b_local):
        ag_a = _Gather(ids_a, [s_in, s_cw], [gin, gcw], a_send, a_recv, a_local)
        ag_b = _Gather(ids_b, [s_out, s_dn], [gout, gdn], b_send, b_recv, b_local)

        @pl.when(pl.program_id(0) == 0)
        def _():
            meet = _Meet(diagonal=True)
            meet.signal()
            loads = [pltpu.make_async_copy(src, dst, ld_sems.at[i])
                     for i, (src, dst) in enumerate(((in_hbm, f_in), (out_hbm, f_out), (dn_hbm, f_dn), (up_hbm, f_up)))]
            for cp in loads:
                cp.start()
            s_cw[...] = jnp.zeros_like(s_cw)
            for k in range(3):
                s_cw[k:k + 1, :] = cw_ref[k]
            loads[0].wait()
            s_in[...] = f_in[...].astype(BF16)
            meet.wait()
            ag_a.start()
            loads[1].wait()
            s_out[...] = f_out[...].astype(BF16)
            loads[2].wait()
            s_dn[...] = f_dn[...].astype(BF16)
            ag_a.near()
            ag_b.start()
            loads[3].wait()
            su_ref[...] = f_up[...].astype(BF16)
            ag_a.finish()
            fill = pltpu.make_async_copy(gin, w_vm, ld_sems.at[4])
            fill.start()
            fill.wait()

        pl.when(pl.program_id(0) == FWD_PROJ_PASS_AT)(ag_b.near)

        xb = x_ref[...]
        r = lax.rsqrt(jnp.mean(xb * xb, axis=-1, keepdims=True) + EPS)
        h = ((xb * r) * g_ref[...]).astype(BF16)
        h1_ref[...] = h
        p = _dot(h, w_vm[...])
        c2, s2 = _rope_block(inv_ref[...], pl.program_id(0) * TM)
        cos_ref[...], sin_ref[...] = c2, s2
        for hd in range(HEADS):
            sl = slice(hd * HEAD_DIM, (hd + 1) * HEAD_DIM)
            proj_ref[:, sl] = _rot(p[:, sl], c2, s2)
            ks = slice(RET_W + hd * HEAD_DIM, RET_W + (hd + 1) * HEAD_DIM)
            proj_ref[:, ks] = _rot(p[:, ks], c2, s2) * K_SCALE
        proj_ref[:, 2 * RET_W:] = p[:, 2 * RET_W:]

        pl.when(pl.program_id(0) == N_TB - 1)(ag_b.finish)

    tok = lambda w: pl.BlockSpec((TM, w), lambda i: (i, 0))
    hbm = pl.BlockSpec(memory_space=pl.ANY)
    vm = pl.BlockSpec(memory_space=pltpu.VMEM)
    return pl.pallas_call(
        body, name="fwd_proj", grid=(N_TB,),
        out_shape=(jax.ShapeDtypeStruct((SEQ, PROJ_W), F32), jax.ShapeDtypeStruct((SEQ, D_MODEL), BF16),
                   jax.ShapeDtypeStruct((SEQ, HEAD_DIM), F32), jax.ShapeDtypeStruct((SEQ, HEAD_DIM), F32))
        + _gathered_shapes(ids_a + ids_b) + (jax.ShapeDtypeStruct(SHARD[W_UP], BF16),),
        in_specs=[tok(D_MODEL), _resident((1, D_MODEL)), _resident((1, HEAD_DIM)), hbm, hbm, hbm, hbm, vm],
        out_specs=(tok(PROJ_W), tok(D_MODEL), tok(HEAD_DIM), tok(HEAD_DIM), hbm, hbm, hbm, hbm, vm),
        scratch_shapes=[pltpu.VMEM((D_MODEL, PROJ_W), BF16), pltpu.VMEM(SHARD[W_IN], BF16), pltpu.VMEM(SHARD[W_CONV], F32),
                        pltpu.VMEM(SHARD[W_OUT], BF16), pltpu.VMEM(SHARD[W_DOWN], BF16),
                        pltpu.VMEM(SHARD[W_IN], F32), pltpu.VMEM(SHARD[W_OUT], F32), pltpu.VMEM(SHARD[W_UP], F32),
                        pltpu.VMEM(SHARD[W_DOWN], F32), pltpu.SemaphoreType.DMA((5,))]
        + _gather_scratch(len(ids_a)) + _gather_scratch(len(ids_b)),
        compiler_params=_cparams(("arbitrary",), collective=COLLECTIVE["fwd_proj"]),
    )(x, g1, inv2, w_in, w_out, w_up, w_down, conv_w)


def _causal(w):
    r = lax.broadcasted_iota(jnp.int32, (CHUNK, CHUNK), 0)
    c = lax.broadcasted_iota(jnp.int32, (CHUNK, CHUNK), 1)
    return jnp.where(r >= c, w, 0.0)


def _fwd_mix(x, proj, wout_g, grn, lng, lnb, ws, bsb, su):
    cdec = _chunk_decay()
    ids = [W_UP]

    def body(x_ref, p_ref, w_ref, grn_ref, lng_ref, lnb_ref, ws_ref, bsb_ref, su_ref,
             x2_ref, cat_ref, o_ref, sp_ref, gup, state, m_ref, qd_ref, kd_ref, send_sems, recv_sems, local_sems):
        ag = _Gather(ids, [su_ref], [gup], send_sems, recv_sems, local_sems)

        @pl.when(pl.program_id(0) == 0)
        def _():
            meet = _Meet(diagonal=False)
            meet.signal()
            state[...] = jnp.zeros_like(state)
            _fill_decay(m_ref, qd_ref, kd_ref)
            meet.wait()
            ag.start()

        for h in range(HEADS):
            sl = slice(h * HEAD_DIM, (h + 1) * HEAD_DIM)
            q = p_ref[:, sl]
            k = p_ref[:, RET_W + h * HEAD_DIM:RET_W + (h + 1) * HEAD_DIM]
            v = p_ref[:, 2 * RET_W + h * HEAD_DIM:2 * RET_W + (h + 1) * HEAD_DIM]
            g = p_ref[:, 3 * RET_W + h * HEAD_DIM:3 * RET_W + (h + 1) * HEAD_DIM]
            qb, kb, vb = q.astype(BF16), k.astype(BF16), v.astype(BF16)
            a = _dot_nt(qb, kb) * m_ref[h]
            spb = state[h].astype(BF16)
            sp_ref[0, h] = spb
            o = _dot(a.astype(BF16), vb) + _dot((q * qd_ref[h]).astype(BF16), spb)
            state[h] = state[h] * cdec[h] + _dot_tn((k * kd_ref[h]).astype(BF16), vb)
            o_ref[:, sl] = o
            rinv = lax.rsqrt(jnp.mean(o * o, axis=-1, keepdims=True) + EPS)
            rn = (o * rinv) * grn_ref[:, sl]
            cat_ref[:, sl] = ((g * _sigmoid(g)) * rn).astype(BF16)
        for gi in range(HEADS):
            sl = slice(gi * HEAD_DIM, (gi + 1) * HEAD_DIM)
            u = p_ref[:, 4 * RET_W + gi * HEAD_DIM:4 * RET_W + (gi + 1) * HEAD_DIM]
            sv = p_ref[:, 4 * RET_W + SGU_W + gi * HEAD_DIM:4 * RET_W + SGU_W + (gi + 1) * HEAD_DIM]
            gv = _gelu(sv)
            xc = gv - jnp.mean(gv, axis=-1, keepdims=True)
            vn = (xc * lax.rsqrt(jnp.mean(xc * xc, axis=-1, keepdims=True) + EPS)) * lng_ref[:, sl] + lnb_ref[:, sl]
            mixed = _dot(_causal(ws_ref[gi]).astype(BF16), vn.astype(BF16)) + bsb_ref[gi]
            cat_ref[:, RET_W + gi * HEAD_DIM:RET_W + (gi + 1) * HEAD_DIM] = (_gelu(u) * mixed).astype(BF16)
        x2_ref[...] = x_ref[...] + _dot(cat_ref[...], w_ref[...])

        pl.when(pl.program_id(0) == FWD_MIX_PASS_AT)(ag.near)
        pl.when(pl.program_id(0) == N_CHUNK - 1)(ag.finish)

    ch = lambda w: pl.BlockSpec((CHUNK, w), lambda i: (i, 0))
    hcc = (HEADS, CHUNK, CHUNK)
    hbm = pl.BlockSpec(memory_space=pl.ANY)
    return pl.pallas_call(
        body, name="fwd_mix", grid=(N_CHUNK,),
        out_shape=(jax.ShapeDtypeStruct((SEQ, D_MODEL), F32), jax.ShapeDtypeStruct((SEQ, D_MODEL), BF16),
                   jax.ShapeDtypeStruct((SEQ, RET_W), F32), jax.ShapeDtypeStruct((N_CHUNK, HEADS, HEAD_DIM, HEAD_DIM), BF16))
        + _gathered_shapes(ids),
        in_specs=[ch(D_MODEL), ch(PROJ_W), _resident((D_MODEL, D_MODEL)), _resident((1, RET_W)), _resident((1, SGU_W)),
                  _resident((1, SGU_W)), _resident(hcc), _resident(hcc), hbm],
        out_specs=(ch(D_MODEL), ch(D_MODEL), ch(RET_W), pl.BlockSpec((1, HEADS, HEAD_DIM, HEAD_DIM), lambda i: (i, 0, 0, 0)), hbm),
        scratch_shapes=[pltpu.VMEM((HEADS, HEAD_DIM, HEAD_DIM), F32)] + [pltpu.VMEM(hcc, F32)] * 3 + _gather_scratch(len(ids)),
        compiler_params=_cparams(("arbitrary",), collective=COLLECTIVE["fwd_mix"]),
    )(x, proj, wout_g, grn, lng, lnb, ws, bsb, su)


def _conv_taps(p, prev8):
    row = lax.broadcasted_iota(jnp.int32, p.shape, 0)
    p1 = jnp.where(row == 0, prev8[7:8, :], pltpu.roll(p, 1, 0))
    p2 = jnp.where(row == 0, prev8[6:7, :], jnp.where(row == 1, prev8[7:8, :], pltpu.roll(p, 2, 0)))
    return p1, p2


def _fwd_ffn(x2, g2, wup_g, cw_g, cb_g, wdn_g, gf, tgt):
    def body(x_ref, g_ref, wu_ref, cw_ref, cb_ref, wd_ref, gf_ref, t_ref, h2_ref, up_ref, u_ref, act_ref, x3_ref, loss_ref, carry):
        @pl.when(pl.program_id(0) == 0)
        def _():
            carry[...] = jnp.zeros_like(carry)

        xb = x_ref[...]
        r = lax.rsqrt(jnp.mean(xb * xb, axis=-1, keepdims=True) + EPS)
        h = ((xb * r) * g_ref[...]).astype(BF16)
        h2_ref[...] = h
        acc = xb
        for t0, tw in FF_TILES:
            u = []
            for c0 in (t0, D_FF + t0):
                cs = slice(c0, c0 + tw)
                p = _dot_nt(h, wu_ref[pl.ds(c0, tw), :])
                up_ref[:, cs] = p.astype(BF16)
                p1, p2 = _conv_taps(p, carry[:, cs])
                carry[:, cs] = p[TM - 8:, :]
                us = p2 * cw_ref[0:1, cs] + p1 * cw_ref[1:2, cs] + p * cw_ref[2:3, cs] + cb_ref[:, cs]
                u_ref[:, cs] = us.astype(BF16)
                u.append(us)
            a = ((u[0] * _sigmoid(u[0])) * u[1]).astype(BF16)
            act_ref[:, t0:t0 + tw] = a
            acc = acc + _dot(a, wd_ref[pl.ds(t0, tw), :])
        x3_ref[...] = acc
        r3 = lax.rsqrt(jnp.mean(acc * acc, axis=-1, keepdims=True) + EPS)
        diff = (acc * r3) * gf_ref[...] - t_ref[...]
        loss_ref[...] = jnp.full(loss_ref.shape, 0.5 * jnp.sum(jnp.mean(diff * diff, axis=-1)), F32)

    tok = lambda w: pl.BlockSpec((TM, w), lambda i: (i, 0))
    return pl.pallas_call(
        body, name="fwd_ffn", grid=(N_TB,),
        out_shape=(jax.ShapeDtypeStruct((SEQ, D_MODEL), BF16), jax.ShapeDtypeStruct((SEQ, 2 * D_FF), BF16),
                   jax.ShapeDtypeStruct((SEQ, 2 * D_FF), BF16),
                   jax.ShapeDtypeStruct((SEQ, D_FF), BF16), jax.ShapeDtypeStruct((SEQ, D_MODEL), F32),
                   jax.ShapeDtypeStruct((N_TB, 8, 128), F32)),
        in_specs=[tok(D_MODEL), _resident((1, D_MODEL)), _resident((2 * D_FF, D_MODEL)), _resident((8, 2 * D_FF)),
                  _resident((1, 2 * D_FF)), _resident((D_FF, D_MODEL)), _resident((1, D_MODEL)), tok(D_MODEL)],
        out_specs=(tok(D_MODEL), tok(2 * D_FF), tok(2 * D_FF), tok(D_FF), tok(D_MODEL),
                   pl.BlockSpec((1, 8, 128), lambda i: (i, 0, 0))),
        scratch_shapes=[pltpu.VMEM((8, 2 * D_FF), F32)],
        compiler_params=_cparams(("arbitrary",)),
    )(x2, g2, wup_g, cw_g, cb_g, wdn_g, gf, tgt)


def _bwd_ffn(x3, tgt, gf, x2, g2, up_pre, u_conv, wup_g, cw_g, wdn_g):
    def body(x3_ref, t_ref, gf_ref, x2_ref, g2_ref, up_ref, u_ref, wu_ref, cw_ref, wd_ref,
             dx3_ref, dpre_ref, dx2_ref, dgf_ref, dg2_ref, dcv_ref, nxt):
        i = pl.program_id(0)

        @pl.when(i == 0)
        def _():
            nxt[...] = jnp.zeros_like(nxt)
            dgf_ref[...] = jnp.zeros_like(dgf_ref)
            dg2_ref[...] = jnp.zeros_like(dg2_ref)
            dcv_ref[...] = jnp.zeros_like(dcv_ref)

        x3 = x3_ref[...]
        r3 = lax.rsqrt(jnp.mean(x3 * x3, axis=-1, keepdims=True) + EPS)
        xh3 = x3 * r3
        dy = (xh3 * gf_ref[...] - t_ref[...]) * (1.0 / D_MODEL)
        dgf_ref[0:1, :] += jnp.sum(dy * xh3, axis=0, keepdims=True)
        t3 = dy * gf_ref[...]
        dx3 = r3 * (t3 - xh3 * jnp.mean(t3 * xh3, axis=-1, keepdims=True))
        dx3b = dx3.astype(BF16)
        dx3_ref[...] = dx3b
        dh2 = jnp.zeros((TM, D_MODEL), F32)
        for t0, tw in FF_TILES:
            row = lax.broadcasted_iota(jnp.int32, (TM, tw), 0)
            ts = slice(t0, t0 + tw)
            dact = _dot_nt(dx3b, wd_ref[pl.ds(t0, tw), :])
            ua = u_ref[:, ts].astype(F32)
            ub = u_ref[:, D_FF + t0:D_FF + t0 + tw].astype(F32)
            sg = _sigmoid(ua)
            du = [dact * ub * (sg * (1.0 + ua * (1.0 - sg))), dact * (ua * sg)]
            for n in range(2):
                d = du[n]
                c0 = n * D_FF + t0
                cs = slice(c0, c0 + tw)
                nx = nxt[:, cs]
                n1 = jnp.where(row == TM - 1, nx[0:1, :], pltpu.roll(d, TM - 1, 0))
                n2 = jnp.where(row == TM - 2, nx[0:1, :], jnp.where(row == TM - 1, nx[1:2, :], pltpu.roll(d, TM - 2, 0)))
                nxt[:, cs] = d[0:8, :]
                dp = (d * cw_ref[2:3, cs] + n1 * cw_ref[1:2, cs] + n2 * cw_ref[0:1, cs]).astype(BF16)
                dpre_ref[:, cs] = dp
                p = up_ref[:, cs].astype(F32)
                dcv_ref[n, 0:1, ts] += jnp.sum(n2 * p, axis=0, keepdims=True)
                dcv_ref[n, 1:2, ts] += jnp.sum(n1 * p, axis=0, keepdims=True)
                dcv_ref[n, 2:3, ts] += jnp.sum(d * p, axis=0, keepdims=True)
                dcv_ref[n, 3:4, ts] += jnp.sum(d, axis=0, keepdims=True)
                dh2 = dh2 + _dot(dp, wu_ref[pl.ds(c0, tw), :])
        x2 = x2_ref[...]
        r2 = lax.rsqrt(jnp.mean(x2 * x2, axis=-1, keepdims=True) + EPS)
        xh2 = x2 * r2
        dg2_ref[0:1, :] += jnp.sum(dh2 * xh2, axis=0, keepdims=True)
        t2 = dh2 * g2_ref[...]
        dx2_ref[...] = dx3 + r2 * (t2 - xh2 * jnp.mean(t2 * xh2, axis=-1, keepdims=True))

    rev = lambda w: pl.BlockSpec((TM, w), lambda i: (N_TB - 1 - i, 0))
    acc = lambda s: pl.BlockSpec(s, lambda i: (0,) * len(s))
    return pl.pallas_call(
        body, name="bwd_ffn", grid=(N_TB,),
        out_shape=(jax.ShapeDtypeStruct((SEQ, D_MODEL), BF16), jax.ShapeDtypeStruct((SEQ, 2 * D_FF), BF16),
                   jax.ShapeDtypeStruct((SEQ, D_MODEL), F32), jax.ShapeDtypeStruct((8, D_MODEL), F32),
                   jax.ShapeDtypeStruct((8, D_MODEL), F32), jax.ShapeDtypeStruct((2, 8, D_FF), F32)),
        in_specs=[rev(D_MODEL), rev(D_MODEL), _resident((1, D_MODEL)), rev(D_MODEL), _resident((1, D_MODEL)), rev(2 * D_FF),
                  rev(2 * D_FF), _resident((2 * D_FF, D_MODEL)), _resident((8, 2 * D_FF)), _resident((D_FF, D_MODEL))],
        out_specs=(rev(D_MODEL), rev(2 * D_FF), rev(D_MODEL), acc((8, D_MODEL)), acc((8, D_MODEL)), acc((2, 8, D_FF))),
        scratch_shapes=[pltpu.VMEM((8, 2 * D_FF), F32)],
        compiler_params=_cparams(("arbitrary",)),
    )(x3, tgt, gf, x2, g2, up_pre, u_conv, wup_g, cw_g, wdn_g)


def _bwd_mix(dx2, proj, o, sprev, wout_g, grn, lng, lnb, ws, bsb, cos2, sin2, hosted):
    cdec = _chunk_decay()
    geoms = [g for g, _ in hosted]
    n_h = len(hosted)

    def body(dx2_ref, p_ref, o_ref, sp_ref, w_ref, grn_ref, lng_ref, lnb_ref, ws_ref, bsb_ref, cos_ref, sin_ref, *rest):
        dp_ref, dgrn_ref, dlng_ref, dlnb_ref, dws_ref, dbs_ref = rest[n_h:n_h + 6]
        dstate, dbs_acc, m_ref, qd_ref, kd_ref = rest[2 * n_h + 6:2 * n_h + 11]
        i = pl.program_id(0)
        rs = _Scatters(geoms, rest[:n_h], rest[n_h + 6:2 * n_h + 6], rest[2 * n_h + 11:])
        pl.when(i == 0)(rs.phase1)
        pl.when(i == 3)(rs.phase2)
        pl.when(i == 8)(rs.phase2b)

        @pl.when(i == 0)
        def _():
            _fill_decay(m_ref, qd_ref, kd_ref)
            dstate[...] = jnp.zeros_like(dstate)
            dgrn_ref[...] = jnp.zeros_like(dgrn_ref)
            dlng_ref[...] = jnp.zeros_like(dlng_ref)
            dlnb_ref[...] = jnp.zeros_like(dlnb_ref)
            dws_ref[...] = jnp.zeros_like(dws_ref)
            dbs_ref[...] = jnp.zeros_like(dbs_ref)
            dbs_acc[...] = jnp.zeros_like(dbs_acc)

        dmix = _dot_nt(dx2_ref[...].astype(BF16), w_ref[...])
        for h in range(HEADS):
            sl = slice(h * HEAD_DIM, (h + 1) * HEAD_DIM)
            q = p_ref[:, sl]
            k = p_ref[:, RET_W + h * HEAD_DIM:RET_W + (h + 1) * HEAD_DIM]
            v = p_ref[:, 2 * RET_W + h * HEAD_DIM:2 * RET_W + (h + 1) * HEAD_DIM]
            g = p_ref[:, 3 * RET_W + h * HEAD_DIM:3 * RET_W + (h + 1) * HEAD_DIM]
            o = o_ref[:, sl]
            rinv = lax.rsqrt(jnp.mean(o * o, axis=-1, keepdims=True) + EPS)
            oh = o * rinv
            gr = grn_ref[:, sl]
            sg = _sigmoid(g)
            dret = dmix[:, sl]
            dp_ref[:, 3 * RET_W + h * HEAD_DIM:3 * RET_W + (h + 1) * HEAD_DIM] = (
                dret * (oh * gr) * (sg * (1.0 + g * (1.0 - sg)))).astype(BF16)
            drn = dret * (g * sg)
            dgrn_ref[0:1, sl] += jnp.sum(drn * oh, axis=0, keepdims=True)
            t = drn * gr
            do = rinv * (t - oh * jnp.mean(t * oh, axis=-1, keepdims=True))
            qb, kb, vb, dob = q.astype(BF16), k.astype(BF16), v.astype(BF16), do.astype(BF16)
            m = m_ref[h]
            ab = (_dot_nt(qb, kb) * m).astype(BF16)
            dab = (_dot_nt(dob, vb) * m).astype(BF16)
            spb = sp_ref[0, h]
            dsn = dstate[h]
            dsnb = dsn.astype(BF16)
            qdb = (q * qd_ref[h]).astype(BF16)
            kdb = (k * kd_ref[h]).astype(BF16)
            dq = _dot(dab, kb) + _dot_nt(dob, spb) * qd_ref[h]
            dk = _dot_tn(dab, qb) + _dot_nt(vb, dsnb) * kd_ref[h]
            dv = _dot_tn(ab, dob) + _dot(kdb, dsnb)
            dstate[h] = dsn * cdec[h] + _dot_tn(qdb, dob)
            c2, s2 = cos_ref[...], sin_ref[...]
            dp_ref[:, sl] = _rot_t(dq, c2, s2).astype(BF16)
            dp_ref[:, RET_W + h * HEAD_DIM:RET_W + (h + 1) * HEAD_DIM] = _rot_t(dk * K_SCALE, c2, s2).astype(BF16)
            dp_ref[:, 2 * RET_W + h * HEAD_DIM:2 * RET_W + (h + 1) * HEAD_DIM] = dv.astype(BF16)
        for gi in range(HEADS):
            sl = slice(gi * HEAD_DIM, (gi + 1) * HEAD_DIM)
            u = p_ref[:, 4 * RET_W + gi * HEAD_DIM:4 * RET_W + (gi + 1) * HEAD_DIM]
            sv = p_ref[:, 4 * RET_W + SGU_W + gi * HEAD_DIM:4 * RET_W + SGU_W + (gi + 1) * HEAD_DIM]
            gv = _gelu(sv)
            xc = gv - jnp.mean(gv, axis=-1, keepdims=True)
            rstd = lax.rsqrt(jnp.mean(xc * xc, axis=-1, keepdims=True) + EPS)
            xh = xc * rstd
            lg = lng_ref[:, sl]
            vnb = (xh * lg + lnb_ref[:, sl]).astype(BF16)
            wcb = _causal(ws_ref[gi]).astype(BF16)
            mixed = _dot(wcb, vnb) + bsb_ref[gi]
            dsgu = dmix[:, RET_W + gi * HEAD_DIM:RET_W + (gi + 1) * HEAD_DIM]
            dmixed = dsgu * _gelu(u)
            dmb = dmixed.astype(BF16)
            dws_ref[gi] += _causal(_dot_nt(dmb, vnb))
            dbs_acc[gi] += dmixed
            dvn = _dot_tn(wcb, dmb)
            dlng_ref[gi:gi + 1, :] += jnp.sum(dvn * xh, axis=0, keepdims=True)
            dlnb_ref[gi:gi + 1, :] += jnp.sum(dvn, axis=0, keepdims=True)
            dxh = dvn * lg
            dgv = rstd * (dxh - jnp.mean(dxh, axis=-1, keepdims=True) - xh * jnp.mean(dxh * xh, axis=-1, keepdims=True))
            dp_ref[:, 4 * RET_W + gi * HEAD_DIM:4 * RET_W + (gi + 1) * HEAD_DIM] = (dsgu * mixed * _gelu_grad(u)).astype(BF16)
            dp_ref[:, 4 * RET_W + SGU_W + gi * HEAD_DIM:4 * RET_W + SGU_W + (gi + 1) * HEAD_DIM] = (
                dgv * _gelu_grad(sv)).astype(BF16)

        @pl.when(i == N_CHUNK - 1)
        def _():
            for gi in range(HEADS):
                col = jnp.broadcast_to(jnp.sum(dbs_acc[gi], axis=-1, keepdims=True), (CHUNK, CHUNK))
                dbs_ref[gi:gi + 1, :] = jnp.transpose(col)[0:1, :]
            rs.phase3()

    rev = lambda w: pl.BlockSpec((CHUNK, w), lambda i: (N_CHUNK - 1 - i, 0))
    hcc = (HEADS, CHUNK, CHUNK)
    acc = lambda s: pl.BlockSpec(s, lambda i: (0,) * len(s))
    res = pl.pallas_call(
        body, name="bwd_mix", grid=(N_CHUNK,),
        out_shape=(jax.ShapeDtypeStruct((SEQ, PROJ_W), BF16), jax.ShapeDtypeStruct((8, RET_W), F32),
                   jax.ShapeDtypeStruct((8, HEAD_DIM), F32), jax.ShapeDtypeStruct((8, HEAD_DIM), F32),
                   jax.ShapeDtypeStruct(hcc, F32), jax.ShapeDtypeStruct((8, CHUNK), F32)) + _scatter_out_shapes(geoms),
        in_specs=[rev(D_MODEL), rev(PROJ_W), rev(RET_W),
                  pl.BlockSpec((1, HEADS, HEAD_DIM, HEAD_DIM), lambda i: (N_CHUNK - 1 - i, 0, 0, 0)),
                  _resident((D_MODEL, D_MODEL)), _resident((1, RET_W)), _resident((1, SGU_W)), _resident((1, SGU_W)),
                  _resident(hcc), _resident(hcc), rev(HEAD_DIM), rev(HEAD_DIM)]
        + [pl.BlockSpec(memory_space=pl.ANY)] * n_h,
        out_specs=(rev(PROJ_W), acc((8, RET_W)), acc((8, HEAD_DIM)), acc((8, HEAD_DIM)), acc(hcc), acc((8, CHUNK)))
        + _scatter_out_specs(geoms),
        scratch_shapes=[pltpu.VMEM((HEADS, HEAD_DIM, HEAD_DIM), F32), pltpu.VMEM((HEADS, CHUNK, CHUNK), F32)]
        + [pltpu.VMEM(hcc, F32)] * 3 + _scatter_scratch(geoms),
        compiler_params=_cparams(("arbitrary",), collective=COLLECTIVE["bwd_mix"]),
    )(dx2, proj, o, sprev, wout_g, grn, lng, lnb, ws, bsb, cos2, sin2, *[p for _, p in hosted])
    return tuple(res[:6 + n_h])


def _bwd_proj(dproj, win_g, x, g1, dx2, gin_p, small):
    geoms = [W_IN]
    n_s = len(small)

    def body(dp_ref, w_ref, x_ref, g_ref, dx2_ref, gin_ref, *rest):
        small_refs = rest[:n_s]
        dx_ref, rs_out, rp_ref, rws_ref, rcv_ref, dg_ref = rest[n_s:n_s + 6]
        rs_scratch = rest[n_s + 6:n_s + 6 + N_SCATTER_SCRATCH]
        ar_scratch = rest[n_s + 6 + N_SCATTER_SCRATCH:]
        ar_res = ar_scratch[N_SMALL_SCRATCH:]
        ar = _SmallReduce((dg_ref,) + tuple(small_refs), ar_res, ar_scratch[:N_SMALL_SCRATCH])
        rs = _Scatters(geoms, [gin_ref], [rs_out], rs_scratch)
        pl.when(pl.program_id(0) == 0)(lambda: rs.phase1(diagonal=True))
        pl.when(pl.program_id(0) == 1)(rs.phase2)
        pl.when(pl.program_id(0) == 5)(rs.phase2b)

        @pl.when(pl.program_id(0) == 0)
        def _():
            dg_ref[...] = jnp.zeros_like(dg_ref)

        dh = _dot_nt(dp_ref[...], w_ref[...])
        xb = x_ref[...]
        r = lax.rsqrt(jnp.mean(xb * xb, axis=-1, keepdims=True) + EPS)
        xh = xb * r
        dg_ref[0:1, :] += jnp.sum(dh * xh, axis=0, keepdims=True)
        t = dh * g_ref[...]
        dx_ref[...] = dx2_ref[...] + r * (t - xh * jnp.mean(t * xh, axis=-1, keepdims=True))

        @pl.when(pl.program_id(0) == N_TB - 1)
        def _():
            ar.begin()
            rs.phase3()
            ar.end()
            for o_ref, r_ref in zip((rp_ref, rws_ref, rcv_ref), ar_res):
                o_ref[...] = r_ref[...]

    tok = lambda w: pl.BlockSpec((TM, w), lambda i: (i, 0))
    vm = pl.BlockSpec(memory_space=pltpu.VMEM)
    res = pl.pallas_call(
        body, name="bwd_proj", grid=(N_TB,),
        out_shape=(jax.ShapeDtypeStruct((SEQ, D_MODEL), F32),) + _scatter_out_shapes(geoms)
        + tuple(jax.ShapeDtypeStruct(s, F32) for s in SMALL_FULL),
        in_specs=[tok(PROJ_W), _resident((D_MODEL, PROJ_W)), tok(D_MODEL), _resident((1, D_MODEL)), tok(D_MODEL),
                  pl.BlockSpec(memory_space=pl.ANY)] + [vm] * n_s,
        out_specs=(tok(D_MODEL),) + _scatter_out_specs(geoms) + (vm,) * len(SMALL_FULL),
        scratch_shapes=[pltpu.VMEM((8, D_MODEL), F32)] + _scatter_scratch(geoms) + _small_scratch()
        + [pltpu.VMEM(s, F32) for s in SMALL_FULL],
        compiler_params=_cparams(("arbitrary",), collective=COLLECTIVE["bwd_proj"]),
    )(dproj, win_g, x, g1, dx2, gin_p, *small)
    return res


def _wgrad(name, a, b, tm=None, tn=None, hosted=()):
    m_w, n_w = a.shape[-1], b.shape[-1]
    tm = m_w if tm is None else tm
    tn = n_w if tn is None else tn
    n_steps = (m_w // tm) * (n_w // tn)
    geoms = [g for g, _ in hosted]
    n_h = len(hosted)

    def body(a_ref, b_ref, *rest):
        o_ref = rest[n_h]
        if n_h:
            rs = _Scatters(geoms, rest[:n_h], rest[n_h + 1:2 * n_h + 1], rest[2 * n_h + 1:])
            step = pl.program_id(0) * (n_w // tn) + pl.program_id(1)
            pl.when(step == 0)(rs.phase1)
            pl.when(step == 1)(rs.phase2)
            pl.when(step == n_steps // 2)(rs.phase2b)
        o_ref[...] = _dot_tn(a_ref[...].astype(BF16), b_ref[...].astype(BF16)).astype(BF16)
        if n_h:
            pl.when(step == n_steps - 1)(rs.phase3)

    assert not n_h or n_steps >= 4
    res = pl.pallas_call(
        body, name=name, grid=(m_w // tm, n_w // tn),
        out_shape=(jax.ShapeDtypeStruct((m_w, n_w), BF16),) + _scatter_out_shapes(geoms),
        in_specs=[pl.BlockSpec((SEQ, tm), lambda i, j: (0, i)), pl.BlockSpec((SEQ, tn), lambda i, j: (0, j))]
        + [pl.BlockSpec(memory_space=pl.ANY)] * n_h,
        out_specs=(pl.BlockSpec((tm, tn), lambda i, j: (i, j)),) + _scatter_out_specs(geoms),
        scratch_shapes=_scatter_scratch(geoms),
        compiler_params=_cparams(("arbitrary", "arbitrary"), collective=COLLECTIVE[name]) if n_h else _cparams(("parallel", "parallel")),
    )(a, b, *[p for _, p in hosted])
    return tuple(res[:1 + n_h])


def _row_step(half_rows):
    return max(s for s in range(16, 177, 16) if half_rows % s == 0)


class _Scatter:
    def __init__(self, geom, partial, out, land1, mine, stage2, land2, comb, s1_send, s1_recv, s2_send, s2_recv, ld_sems):
        self.w, self.row0, self.shape = _geom(geom)
        self.partial, self.out, self.land1 = partial, out, land1
        self.mine, self.stage2, self.land2, self.comb = mine, stage2, land2, comb
        self.hr = self.shape[0] // 2
        self.step = _row_step(self.hr)
        self.s1_send, self.s1_recv, self.s2_send, self.s2_recv, self.ld_sems = s1_send, s1_recv, s2_send, s2_recv, ld_sems
        self.x, self.y, self.c = lax.axis_index("x"), lax.axis_index("y"), lax.axis_index("c")
        self.sibling = (self.x, self.y, 1 - self.c)
        self.chips = [(self.x, self.y), (1 - self.x, self.y), (self.x, 1 - self.y), (1 - self.x, 1 - self.y)]

    def block(self, px, py, pc):
        dev = 4 * px + 2 * py + pc
        if self.w == W_IN:
            return self.partial.at[:, pl.ds(pl.multiple_of(dev * IN_SHARD, 128), IN_SHARD)]
        if self.w == W_OUT:
            return self.partial.at[pl.ds(pl.multiple_of(dev * OUT_SHARD, 128), OUT_SHARD), :]
        if self.w == W_DOWN:
            return self.partial.at[pl.ds(pl.multiple_of(dev * DOWN_SHARD, 32), DOWN_SHARD), :]
        return self.partial.at[pl.ds(pl.multiple_of(dev * FF_SHARD + self.row0, 32), self.shape[0]), :]

    def copy1(self, k):
        return pltpu.make_async_remote_copy(
            src_ref=self.block(*self.chips[k], 1 - self.c), dst_ref=self.land1.at[k],
            send_sem=self.s1_send.at[k], recv_sem=self.s1_recv.at[k], device_id=self.sibling, device_id_type=MESH)

    STAGE2 = [(1, 0, 1), (3, 0, 1), (2, 1, 2), (3, 1, 2), (1, 1, 1), (2, 0, 2)]

    def copy2(self, j):
        blk, h, to = self.STAGE2[j]
        src = self.comb.at[j - 4] if j >= 4 else self.stage2.at[blk - 1, pl.ds(h * self.hr, self.hr), :]
        return pltpu.make_async_remote_copy(
            src_ref=src, dst_ref=self.land2.at[j], send_sem=self.s2_send.at[j], recv_sem=self.s2_recv.at[j],
            device_id=(*self.chips[to], self.c), device_id_type=MESH)

    def _rows(self, h=None):
        step = self.step
        lo, n = (0, self.shape[0]) if h is None else (h * self.hr, self.hr)
        return [pl.ds(r0, step) for r0 in range(lo, lo + n, step)]

    def load(self, k):
        return pltpu.make_async_copy(self.block(*self.chips[k], self.c), self.mine.at[k], self.ld_sems.at[k])

    def load_mine(self):
        for k in range(4):
            self.load(k).start()

    def phase1(self):
        for k in range(4):
            self.copy1(k).start()

    def phase2(self, k):
        self.copy1(k).wait_recv()
        self.load(k).wait()
        for rs in self._rows():
            s = self.mine[k, rs, :].astype(F32) + self.land1[k, rs, :].astype(F32)
            if k == 0:
                self.out[rs, :] = s
            else:
                self.stage2[k - 1, rs, :] = s.astype(BF16)
        for j in {3: (1, 3), 1: (0,), 2: (2,), 0: ()}[k]:
            self.copy2(j).start()

    def phase2b(self):
        for j, got in ((4, 3), (5, 1)):
            blk, h, _ = self.STAGE2[j]
            self.copy2(got).wait_recv()
            for i, rs in enumerate(self._rows(h)):
                lr = pl.ds(i * self.step, self.step)
                self.comb[j - 4, lr, :] = (self.stage2[blk - 1, rs, :].astype(F32) + self.land2[got, lr, :].astype(F32)).astype(BF16)
            self.copy2(j).start()

    def phase3(self):
        for j in (0, 5, 4, 2):
            self.copy2(j).wait_recv()
        for h, (first, second) in enumerate(((0, 5), (4, 2))):
            for i, rs in enumerate(self._rows(h)):
                lr = pl.ds(i * self.step, self.step)
                self.out[rs, :] = (self.out[rs, :] + self.land2[first, lr, :].astype(F32)) + self.land2[second, lr, :].astype(F32)
        for k in range(4):
            self.copy1(k).wait_send()
        for j in range(6):
            self.copy2(j).wait_send()


def _geom(geom):
    if isinstance(geom, tuple):
        w, row0, rows = geom
        assert w == W_UP
        return w, row0, (rows, SHARD[w][1])
    return geom, 0, SHARD[geom]


N_SCATTER_SCRATCH = 10


def _scatter_out_shapes(geoms):
    return tuple(jax.ShapeDtypeStruct(_geom(g)[2], F32) for g in geoms)


def _scatter_out_specs(geoms):
    return (pl.BlockSpec(memory_space=pltpu.VMEM),) * len(geoms)


def _scatter_scratch(geoms):
    out = []
    for g in geoms:
        s = _geom(g)[2]
        hs = (s[0] // 2, s[1])
        out += [pltpu.VMEM((4,) + s, BF16), pltpu.VMEM((4,) + s, BF16), pltpu.VMEM((3,) + s, BF16), pltpu.VMEM((6,) + hs, BF16),
                pltpu.VMEM((2,) + hs, BF16),
                pltpu.SemaphoreType.DMA((4,)), pltpu.SemaphoreType.DMA((4,)), pltpu.SemaphoreType.DMA((6,)),
                pltpu.SemaphoreType.DMA((6,)), pltpu.SemaphoreType.DMA((4,))]
    return out


class _Scatters:
    def __init__(self, geoms, p_refs, out_refs, scratch):
        k = N_SCATTER_SCRATCH
        self.items = [_Scatter(g, p_refs[i], out_refs[i], *scratch[k * i:k * i + k]) for i, g in enumerate(geoms)]

    def phase1(self, diagonal=False):
        meet = _Meet(diagonal)
        meet.signal()
        for s in self.items:
            s.load_mine()
        meet.wait()
        for s in self.items:
            s.phase1()

    def phase2(self):
        for k in (3, 1, 2, 0):
            for s in self.items:
                s.phase2(k)

    def phase2b(self):
        for s in self.items:
            s.phase2b()

    def phase3(self):
        for s in self.items:
            s.phase3()


PACK_W = 1024


SMALL_FULL = [(2, 8, PACK_W), (HEADS, CHUNK, CHUNK), (2, 8, D_FF)]
SMALL_HALF = [(s[0] // 2,) + s[1:] for s in SMALL_FULL]
N_SMALL_SCRATCH = 16


def _small_scratch():
    n_a = len(SMALL_FULL)
    return ([pltpu.VMEM(SMALL_FULL[0], F32)] + [pltpu.VMEM(s, F32) for s in SMALL_HALF] + [pltpu.VMEM(s, F32) for s in SMALL_HALF]
            + [pltpu.VMEM((3,) + s, F32) for s in SMALL_HALF]
            + [pltpu.SemaphoreType.DMA((n_a,)), pltpu.SemaphoreType.DMA((n_a,)), pltpu.SemaphoreType.DMA((n_a, 3)),
               pltpu.SemaphoreType.DMA((n_a, 3)), pltpu.SemaphoreType.DMA((n_a,)), pltpu.SemaphoreType.DMA((n_a,))])


class _SmallReduce:
    def __init__(self, ins, outs, scratch):
        self.ins, self.outs = ins, outs
        (self.pack, *rest) = scratch
        self.rxs, self.css, self.gs = rest[0:3], rest[3:6], rest[6:9]
        self.s1_send, self.s1_recv, self.s2_send, self.s2_recv, self.s3_send, self.s3_recv = rest[9:]
        self.x, self.y, self.c = lax.axis_index("x"), lax.axis_index("y"), lax.axis_index("c")
        self.sibling = (self.x, self.y, 1 - self.c)
        self.chips = [(1 - self.x, self.y), (self.x, 1 - self.y), (1 - self.x, 1 - self.y)]
        self.hl = [s[0] for s in SMALL_HALF]

    def half(self, ref, a, h):
        return ref.at[pl.ds(h * self.hl[a], self.hl[a])]

    def begin(self):
        dg1_ref, dg2_ref, dgf_ref, dgrn_ref, dlng_ref, dlnb_ref, dbs_ref, loss_ref, dws_ref, dcv_ref = self.ins
        pack, c = self.pack, self.c
        pack[...] = jnp.zeros_like(pack)
        pack[0, 0:1, :] = dg1_ref[0:1, :]
        pack[0, 1:2, :] = dg2_ref[0:1, :]
        pack[0, 2:3, :] = dgf_ref[0:1, :]
        pack[0, 3:4, 0:RET_W] = dgrn_ref[0:1, :]
        lsum = loss_ref[0, 0:1, :]
        for i in range(1, N_TB):
            lsum = lsum + loss_ref[i, 0:1, :]
        pack[0, 3:4, RET_W:RET_W + 128] = lsum
        pack[1, 0:HEADS, 0:128] = dlng_ref[0:HEADS, :]
        pack[1, 0:HEADS, 128:256] = dlnb_ref[0:HEADS, :]
        pack[1, 0:HEADS, 256:384] = dbs_ref[0:HEADS, :]
        self.srcs = [pack, dws_ref, dcv_ref]
        n_a = len(self.srcs)
        self.ex1 = [pltpu.make_async_remote_copy(src_ref=self.half(self.srcs[a], a, 1 - c), dst_ref=self.rxs[a],
                                                 send_sem=self.s1_send.at[a], recv_sem=self.s1_recv.at[a],
                                                 device_id=self.sibling, device_id_type=MESH) for a in range(n_a)]
        for cp in self.ex1:
            cp.start()
        self.ex2 = []
        for a in range(n_a):
            self.ex1[a].wait_recv()
            self.css[a][...] = self.half(self.srcs[a], a, c)[...] + self.rxs[a][...]
            for j, chip in enumerate(self.chips):
                cp = pltpu.make_async_remote_copy(src_ref=self.css[a], dst_ref=self.gs[a].at[j], send_sem=self.s2_send.at[a, j],
                                                  recv_sem=self.s2_recv.at[a, j], device_id=(*chip, c), device_id_type=MESH)
                cp.start()
                self.ex2.append(cp)

    def end(self):
        c, x, y = self.c, self.x, self.y
        ex3 = []
        for a in range(len(self.srcs)):
            css, gs, out = self.css[a], self.gs[a], self.outs[a]
            for j in range(3):
                self.ex2[3 * a + j].wait_recv()
            tot = None
            for q in range(4):
                k = jnp.where(x != (q >> 1), 1, 0) + jnp.where(y != (q & 1), 2, 0)
                term = jnp.where(k == 0, css[...], jnp.where(k == 1, gs[0], jnp.where(k == 2, gs[1], gs[2])))
                tot = term if tot is None else tot + term
            self.half(out, a, c)[...] = tot
            cp = pltpu.make_async_remote_copy(src_ref=self.half(out, a, c), dst_ref=self.half(out, a, c), send_sem=self.s3_send.at[a],
                                              recv_sem=self.s3_recv.at[a], device_id=self.sibling, device_id_type=MESH)
            cp.start()
            ex3.append(cp)
        for a in range(len(self.srcs)):
            out = self.outs[a]
            pltpu.make_async_remote_copy(src_ref=self.half(out, a, 1 - c), dst_ref=self.half(out, a, 1 - c), send_sem=self.s3_send.at[a],
                                         recv_sem=self.s3_recv.at[a], device_id=self.sibling, device_id_type=MESH).wait_recv()
        for cp in self.ex1 + self.ex2 + ex3:
            cp.wait_send()


def _adam_math(w, g, m, v):
    nm = ADAM_B1 * m + (1.0 - ADAM_B1) * g
    nv = ADAM_B2 * v + (1.0 - ADAM_B2) * (g * g)
    d = -ADAM_LR * ((nm / (1.0 - ADAM_B1 ** ADAM_STEP)) / (jnp.sqrt(nv / (1.0 - ADAM_B2 ** ADAM_STEP)) + ADAM_EPS) + ADAM_WD * w)
    return d, nm, nv


def _adamw(params, thru, n_steps):
    plan = []
    for w, gs, _, _ in params:
        _, r, cdim = w.shape
        if len(gs) == 1:
            edges = [0, r // n_steps]
            spec3 = pl.BlockSpec((1, r // n_steps, cdim), lambda i: (0, i, 0))
            g_specs = [pl.BlockSpec((r // n_steps, cdim), lambda i: (i, 0))]
        else:
            edges = [sum(g.shape[0] for g in gs[:k]) for k in range(len(gs) + 1)]
            spec3 = pl.BlockSpec((1, r, cdim // n_steps), lambda i: (0, 0, i))
            g_specs = [pl.BlockSpec((g.shape[0], cdim // n_steps), lambda i: (0, i)) for g in gs]
        plan.append((len(gs), edges, spec3, g_specs))
    n_in = sum(n_g + 3 for n_g, _, _, _ in plan)
    n_t = len(thru)

    def body(*refs):
        ins, outs = refs[:n_in], refs[n_in + n_t:]
        for n_g, edges, _, _ in plan:
            (w_ref, *g_refs, m_ref, v_ref), ins = ins[:n_g + 3], ins[n_g + 3:]
            (go_ref, d_ref, nm_ref, nv_ref), outs = outs[:4], outs[4:]
            for g_ref, lo, hi in zip(g_refs, edges[:-1], edges[1:]):
                gg = g_ref[...]
                go_ref[0, lo:hi, :] = gg
                d_ref[0, lo:hi, :], nm_ref[0, lo:hi, :], nv_ref[0, lo:hi, :] = _adam_math(
                    w_ref[0, lo:hi, :], gg, m_ref[0, lo:hi, :], v_ref[0, lo:hi, :])
        for t_ref, to_ref in zip(refs[n_in:n_in + n_t], outs):
            to_ref[...] = t_ref[...]

    t_specs = [pl.BlockSpec((t.shape[0] // n_steps, t.shape[1]), lambda i: (i, 0)) for t in thru]
    in_specs, out_specs, out_shape, args = [], [], [], []
    for (w, gs, m, v), (_, _, spec3, g_specs) in zip(params, plan):
        in_specs += [spec3] + g_specs + [spec3, spec3]
        out_specs += [spec3] * 4
        out_shape += [jax.ShapeDtypeStruct(w.shape, F32)] * 4
        args += [w, *gs, m, v]
    res = pl.pallas_call(
        body, name="adamw", grid=(n_steps,), out_shape=tuple(out_shape) + tuple(jax.ShapeDtypeStruct(t.shape, t.dtype) for t in thru),
        in_specs=in_specs + t_specs, out_specs=tuple(out_specs) + tuple(t_specs),
        compiler_params=_cparams(("parallel",)),
    )(*args, *thru)
    return [res[4 * k:4 * k + 4] for k in range(len(params))], res[4 * len(params):]


def _adamw_small(rp, rws, rcv, gcw, params):
    n_p = len(params)

    def body(*refs):
        rp_ref, rws_ref, rcv_ref, gcw_ref = refs[:4]
        ins = refs[4:4 + 3 * n_p]
        outs = refs[4 + 3 * n_p:]
        outs[4 * n_p][...] = rp_ref[0, 3:4, RET_W:RET_W + 1]
        grads = [rp_ref[0, 0:1, :], rp_ref[0, 1:2, :], rp_ref[0, 2:3, :], rp_ref[0, 3:4, 0:RET_W],
                 rp_ref[1, 0:HEADS, 0:128], rp_ref[1, 0:HEADS, 128:256], rp_ref[1, 0:HEADS, 256:384],
                 rws_ref[...], gcw_ref[...], None]
        for p in range(n_p):
            w_ref, m_ref, v_ref = ins[3 * p:3 * p + 3]
            o = outs[4 * p:4 * p + 4]
            if p == n_p - 1:
                for hf in range(2):
                    cs = slice(hf * D_FF, (hf + 1) * D_FF)
                    g = rcv_ref[hf, 3:4, :]
                    res = (g,) + _adam_math(w_ref[:, cs], g, m_ref[:, cs], v_ref[:, cs])
                    for t in range(4):
                        o[t][:, cs] = res[t]
                continue
            lead = w_ref.ndim > grads[p].ndim
            rd = (lambda r: r[0]) if lead else (lambda r: r[...])
            res = (grads[p],) + _adam_math(rd(w_ref), grads[p], rd(m_ref), rd(v_ref))
            for t in range(4):
                if lead:
                    o[t][0] = res[t]
                else:
                    o[t][...] = res[t]

    vm = pl.BlockSpec(memory_space=pltpu.VMEM)
    flat = [a for tr in params for a in tr]
    out_shape = tuple(jax.ShapeDtypeStruct(tr[0].shape, F32) for tr in params for _ in range(4)) + (jax.ShapeDtypeStruct((1, 1), F32),)
    res = pl.pallas_call(
        body, name="adamw_small", out_shape=out_shape, in_specs=[vm] * (4 + len(flat)), out_specs=(vm,) * len(out_shape),
        compiler_params=_cparams(),
    )(rp, rws, rcv, gcw, *flat)
    return [res[4 * p:4 * p + 4] for p in range(n_p)], res[4 * n_p]


def kernel(x, mix_norm_g, w_in, ret_norm_g, sgu_ln_g, sgu_ln_b, sgu_w_s, sgu_b_s, w_out, ffn_norm_g, w_up, conv_w, conv_b, w_down, final_norm_g, loss_target, m_mix_norm_g, m_w_in, m_ret_norm_g, m_sgu_ln_g, m_sgu_ln_b, m_sgu_w_s, m_sgu_b_s, m_w_out, m_ffn_norm_g, m_w_up, m_conv_w, m_conv_b, m_w_down, m_final_norm_g, v_mix_norm_g, v_w_in, v_ret_norm_g, v_sgu_ln_g, v_sgu_ln_b, v_sgu_w_s, v_sgu_b_s, v_w_out, v_ffn_norm_g, v_w_up, v_conv_w, v_conv_b, v_w_down, v_final_norm_g):
    xs = x[0]
    tgt = loss_target[0]
    grn = ret_norm_g.reshape(1, RET_W)
    lng = sgu_ln_g.reshape(1, SGU_W)
    lnb = sgu_ln_b.reshape(1, SGU_W)
    ws = sgu_w_s[0]
    bsb = jnp.broadcast_to(sgu_b_s[0][:, :, None], (HEADS, CHUNK, HEAD_DIM))
    gf = final_norm_g.reshape(1, D_MODEL)
    me = 4 * lax.axis_index("x") + 2 * lax.axis_index("y") + lax.axis_index("c")
    tr = lambda a: jnp.transpose(a[0])[None]
    tr_cw = lambda a: jnp.transpose(a, (1, 0, 2))

    proj, h1, cos2, sin2, win_g, cw_sh, wout_g, wdn_g, su = _fwd_proj(
        xs, mix_norm_g, _rope_freq(), w_in[0], w_out[0], tr(w_up)[0], w_down[0], tr_cw(conv_w))
    cw_g = jnp.transpose(cw_sh, (1, 0, 2)).reshape(8, 2 * D_FF)
    x2, mixcat, o, sprev, wup_g = _fwd_mix(xs, proj, wout_g, grn, lng, lnb, ws, bsb, su)
    h2, up_pre, u_conv, act, x3, loss_parts = _fwd_ffn(x2, ffn_norm_g, wup_g, cw_g, conv_b, wdn_g, gf, tgt)

    dx3, dpre, dx2, dgf, dg2, dcv = _bwd_ffn(x3, tgt, gf, x2, ffn_norm_g, up_pre, u_conv, wup_g, cw_g, wdn_g)
    band = 448
    (gdn_p,) = _wgrad("wgrad_down", act, dx3, tm=FF_TILE)
    (gout_p,) = _wgrad("wgrad_out", mixcat, dx2, tn=512)
    gup_p, g_dn = _wgrad("wgrad_up", dpre, h2, tm=FF_TILE, tn=512, hosted=[(W_DOWN, gdn_p)])
    dproj, dgrn, dlng, dlnb, dws, dbs, g_up_a, g_out = _bwd_mix(
        dx2, proj, o, sprev, wout_g, grn, lng, lnb, ws, bsb, cos2, sin2,
        [((W_UP, 0, band), gup_p), (W_OUT, gout_p)])
    gin_p, g_up_b = _wgrad("wgrad_in", h1, dproj, tm=512, tn=768, hosted=[((W_UP, band, FF_SHARD - band), gup_p)])
    grad_x, g_in, rp, rws, rcv = _bwd_proj(dproj, win_g, xs, mix_norm_g, dx2, gin_p,
                                           (dg2, dgf, dgrn, dlng, dlnb, dbs, loss_parts, dws, dcv))
    gcw = tr_cw(lax.dynamic_slice(rcv, (me // (N_DEV // 2), 0, (me % (N_DEV // 2)) * FF_SHARD), (1, 3, FF_SHARD)))

    table = {}
    big, (grad_x,) = _adamw([(w_in, [g_in], m_w_in, v_w_in), (w_out, [g_out], m_w_out, v_w_out),
                             (tr(w_up), [g_up_a, g_up_b], tr(m_w_up), tr(v_w_up)), (w_down, [g_dn], m_w_down, v_w_down)],
                            [grad_x], n_steps=4)
    table.update(zip(("w_in", "w_out", "w_up", "w_down"), big))
    table["w_up"] = tuple(tr(a) for a in table["w_up"])
    row = lambda a: a.reshape(1, D_MODEL)
    names_small = ["mix_norm_g", "ffn_norm_g", "final_norm_g", "ret_norm_g", "sgu_ln_g", "sgu_ln_b", "sgu_b_s", "sgu_w_s",
                   "conv_w", "conv_b"]
    params = [(mix_norm_g, m_mix_norm_g, v_mix_norm_g), (ffn_norm_g, m_ffn_norm_g, v_ffn_norm_g),
              (row(final_norm_g), row(m_final_norm_g), row(v_final_norm_g)), (ret_norm_g, m_ret_norm_g, v_ret_norm_g),
              (sgu_ln_g, m_sgu_ln_g, v_sgu_ln_g), (sgu_ln_b, m_sgu_ln_b, v_sgu_ln_b), (sgu_b_s, m_sgu_b_s, v_sgu_b_s),
              (sgu_w_s, m_sgu_w_s, v_sgu_w_s), (tr_cw(conv_w), tr_cw(m_conv_w), tr_cw(v_conv_w)), (conv_b, m_conv_b, v_conv_b)]
    small, loss = _adamw_small(rp, rws, rcv, gcw, params)
    for n, res in zip(names_small, small):
        table[n] = res
    table["final_norm_g"] = tuple(a.reshape(D_MODEL) for a in table["final_norm_g"])
    table["conv_w"] = tuple(tr_cw(a) for a in table["conv_w"])

    order = ["mix_norm_g", "w_in", "ret_norm_g", "sgu_ln_g", "sgu_ln_b", "sgu_w_s", "sgu_b_s", "w_out", "ffn_norm_g", "w_up",
             "conv_w", "conv_b", "w_down", "final_norm_g"]
    outs = [loss.reshape(()), grad_x[None]]
    for col in range(4):
        outs += [table[n][col] for n in order]
    return tuple(outs)
```

```python
import functools
import math

import jax
import jax.numpy as jnp
import numpy as np
from jax import lax
from jax.experimental import pallas as pl
from jax.experimental.pallas import tpu as pltpu

F32 = jnp.float32
BF16 = jnp.bfloat16
MESH = pl.DeviceIdType.MESH

N_DEV = 8
SEQ = 2048
D_MODEL = 1024
CHUNK = 128
N_CHUNK = SEQ // CHUNK
HEADS = 4
HEAD_DIM = 128
RET_W = 512
SGU_W = 512
PROJ_W = 3072
D_FF = 2816
FF_SHARD = 704
FF_TILE = 1408
FF_TILES = ((0, D_FF),)
IN_SHARD = PROJ_W // N_DEV
OUT_SHARD = D_MODEL // N_DEV
DOWN_SHARD = D_FF // N_DEV
TM = 256
N_TB = SEQ // TM
FWD_PROJ_PASS_AT = 5
FWD_MIX_PASS_AT = 10
EPS = 1e-6
ROPE_BASE = 10000.0
K_SCALE = HEAD_DIM ** -0.5
INV_SQRT2 = 0.7071067811865476
INV_SQRT_2PI = 0.3989422804014327

ADAM_LR = 0.001
ADAM_B1 = 0.9
ADAM_B2 = 0.999
ADAM_EPS = 1e-08
ADAM_WD = 0.01
ADAM_STEP = 10

VMEM_LIMIT = 56 * 1024 * 1024


def _cparams(sem=None, vmem=VMEM_LIMIT, collective=None):
    return pltpu.CompilerParams(dimension_semantics=sem, vmem_limit_bytes=vmem, collective_id=collective)


COLLECTIVE = {name: k for k, name in enumerate(("fwd_proj", "fwd_mix", "wgrad_down", "wgrad_up", "bwd_mix", "wgrad_in", "bwd_proj"))}


class _Meet:
    def __init__(self, diagonal):
        x, y, c = lax.axis_index("x"), lax.axis_index("y"), lax.axis_index("c")
        self.peers = [(x, y, 1 - c), (1 - x, y, c), (x, 1 - y, c)] + ([(1 - x, 1 - y, c)] if diagonal else [])

    def signal(self):
        for peer in self.peers:
            pl.semaphore_signal(pltpu.get_barrier_semaphore(), inc=1, device_id=peer, device_id_type=MESH)

    def wait(self):
        pl.semaphore_wait(pltpu.get_barrier_semaphore(), len(self.peers))


def _resident(shape):
    nd = len(shape)
    return pl.BlockSpec(shape, lambda *_: (0,) * nd, pipeline_mode=pl.Buffered(1))


def _dot(a, b):
    return jnp.dot(a, b, preferred_element_type=F32)


def _dot_nt(a, b):
    return lax.dot_general(a, b, (((1,), (1,)), ((), ())), preferred_element_type=F32)


def _dot_tn(a, b):
    return lax.dot_general(a, b, (((0,), (0,)), ((), ())), preferred_element_type=F32)


def _sigmoid(x):
    return 1.0 / (1.0 + jnp.exp(-x))


def _gelu(x):
    return 0.5 * x * (1.0 + lax.erf(x * INV_SQRT2))


def _gelu_grad(x):
    return 0.5 * (1.0 + lax.erf(x * INV_SQRT2)) + x * (jnp.exp(-0.5 * x * x) * INV_SQRT_2PI)


def _rot(xh, cos2, sin2):
    return xh * cos2 + pltpu.roll(xh, HEAD_DIM // 2, 1) * sin2


def _rot_t(dh, cos2, sin2):
    return dh * cos2 + pltpu.roll(dh * sin2, HEAD_DIM // 2, 1)


def _rope_freq():
    half = HEAD_DIM // 2
    inv_freq = jnp.power(ROPE_BASE, -jnp.arange(half, dtype=F32) / half)
    return jnp.concatenate([inv_freq, inv_freq])[None, :]


def _rope_block(inv2, first_row):
    pos = (lax.broadcasted_iota(jnp.int32, (TM, HEAD_DIM), 0) + first_row).astype(F32)
    ang = pos * inv2
    sin = jnp.sin(ang)
    lane = lax.broadcasted_iota(jnp.int32, (TM, HEAD_DIM), 1)
    return jnp.cos(ang), jnp.where(lane < HEAD_DIM // 2, -sin, sin)


def _log_gamma():
    return np.log(np.float32(1.0) - np.power(np.float32(2.0), -5.0 - np.arange(HEADS, dtype=np.float32))).astype(np.float32)


def _fill_decay(mask_ref, qd_ref, kd_ref):
    assert HEAD_DIM == CHUNK
    lg = _log_gamma()
    t = lax.broadcasted_iota(jnp.int32, (CHUNK, CHUNK), 0).astype(F32)
    diff = t - lax.broadcasted_iota(jnp.int32, (CHUNK, CHUNK), 1).astype(F32)
    for h in range(HEADS):
        mask_ref[h] = jnp.where(diff >= 0.0, jnp.exp(float(lg[h]) * jnp.maximum(diff, 0.0)), 0.0)
        qd_ref[h] = jnp.exp(float(lg[h]) * (t + 1.0))
        kd_ref[h] = jnp.exp(float(lg[h]) * (CHUNK - 1.0 - t))


def _chunk_decay():
    lg = _log_gamma()
    return [float(np.exp(lg[h] * np.float32(CHUNK))) for h in range(HEADS)]


W_IN, W_OUT, W_UP, W_DOWN, W_CONV = range(5)
GATHERED = {W_IN: ((D_MODEL, PROJ_W), BF16), W_OUT: ((D_MODEL, D_MODEL), BF16), W_UP: ((2 * D_FF, D_MODEL), BF16),
            W_DOWN: ((D_FF, D_MODEL), BF16), W_CONV: ((N_DEV, 8, FF_SHARD), F32)}
SHARD = {W_IN: (D_MODEL, IN_SHARD), W_OUT: (OUT_SHARD, D_MODEL), W_UP: (FF_SHARD, D_MODEL), W_DOWN: (DOWN_SHARD, D_MODEL),
         W_CONV: (8, FF_SHARD)}


class _Gather:
    N_SEMS = 9

    def __init__(self, ids, stages, gathered, send_sems, recv_sems, local_sems):
        self.ids, self.stages, self.gathered = ids, stages, gathered
        self.send_sems, self.recv_sems, self.local_sems = send_sems, recv_sems, local_sems
        self.x, self.y, self.c = lax.axis_index("x"), lax.axis_index("y"), lax.axis_index("c")
        self.me = (self.x, self.y, self.c)
        self.sibling = (self.x, self.y, 1 - self.c)
        self.chips = [(1 - self.x, self.y), (self.x, 1 - self.y), (1 - self.x, 1 - self.y)]

    def slot(self, n, px, py, pc):
        dev = 4 * px + 2 * py + pc
        w, g = self.ids[n], self.gathered[n]
        if w == W_IN:
            return g.at[:, pl.ds(pl.multiple_of(dev * IN_SHARD, 128), IN_SHARD)]
        if w == W_OUT:
            return g.at[pl.ds(pl.multiple_of(dev * OUT_SHARD, 128), OUT_SHARD), :]
        if w == W_DOWN:
            return g.at[pl.ds(pl.multiple_of(dev * DOWN_SHARD, 32), DOWN_SHARD), :]
        if w == W_UP:
            return g.at[pl.ds(pl.multiple_of(dev * FF_SHARD, 32), FF_SHARD), :]
        return g.at[dev]

    def half(self, n, px, py, pc, h):
        dev = 4 * px + 2 * py + pc
        w, g = self.ids[n], self.gathered[n]
        if w == W_IN:
            return g.at[pl.ds(h * (D_MODEL // 2), D_MODEL // 2), pl.ds(pl.multiple_of(dev * IN_SHARD, 128), IN_SHARD)]
        rows = SHARD[w][0] // 2
        return g.at[pl.ds(pl.multiple_of(dev * SHARD[w][0] + h * rows, 16), rows), :]

    def tree(self, n):
        return self.ids[n] != W_CONV

    def copy(self, n, k, block, to, src=None, h=None):
        ref = self.slot(n, *block) if h is None else self.half(n, *block, h)
        return pltpu.make_async_remote_copy(
            src_ref=ref if src is None else src, dst_ref=ref,
            send_sem=self.send_sems.at[n, k], recv_sem=self.recv_sems.at[n, k], device_id=to, device_id_type=MESH)

    def _mine(self):
        return [pltpu.make_async_copy(self.stages[n], self.slot(n, *self.me), self.local_sems.at[n]) for n in range(len(self.ids))]

    def _first(self):
        out = []
        for n in range(len(self.ids)):
            out.append(self.copy(n, 0, self.me, self.sibling, src=self.stages[n]))
            out += [self.copy(n, 1 + j, self.me, (*chip, self.c), src=self.stages[n])
                    for j, chip in enumerate(self.chips[:2] if self.tree(n) else self.chips)]
        return out

    def start(self):
        for cp in self._mine() + self._first():
            cp.start()

    def _passed(self, j):
        dev = (*self.chips[j], self.c)
        out = []
        for n in range(len(self.ids)):
            if not self.tree(n):
                out.append(self.copy(n, 4 + j, dev, self.sibling))
            elif j < 2:
                out += [self.copy(n, 3 + j, dev, (*self.chips[1 - j], self.c), h=j), self.copy(n, 5 + j, dev, self.sibling)]
            else:
                out += [self.copy(n, 7, dev, self.sibling, h=0), self.copy(n, 8, dev, self.sibling, h=1)]
        return out

    def near(self):
        for j in range(2):
            dev = (*self.chips[j], self.c)
            for n in range(len(self.ids)):
                self.copy(n, 1 + j, dev, self.me).wait_recv()
            for cp in self._passed(j):
                cp.start()

    def finish(self):
        dev = (*self.chips[2], self.c)
        for n in range(len(self.ids)):
            if self.tree(n):
                self.copy(n, 3, dev, self.me, h=0).wait_recv()
                self.copy(n, 4, dev, self.me, h=1).wait_recv()
            else:
                self.copy(n, 3, dev, self.me).wait_recv()
        for cp in self._passed(2):
            cp.start()
        for n in range(len(self.ids)):
            self.copy(n, 0, self.sibling, self.me).wait_recv()
            for j, chip in enumerate(self.chips):
                dev = (*chip, 1 - self.c)
                if not self.tree(n):
                    self.copy(n, 4 + j, dev, self.me).wait_recv()
                elif j < 2:
                    self.copy(n, 5 + j, dev, self.me).wait_recv()
                else:
                    self.copy(n, 7, dev, self.me, h=0).wait_recv()
                    self.copy(n, 8, dev, self.me, h=1).wait_recv()
        for cp in self._mine():
            cp.wait()
        for cp in self._first() + self._passed(0) + self._passed(1) + self._passed(2):
            cp.wait_send()


def _gather_scratch(n):
    return [pltpu.SemaphoreType.DMA((n, _Gather.N_SEMS)), pltpu.SemaphoreType.DMA((n, _Gather.N_SEMS)), pltpu.SemaphoreType.DMA((n,))]


def _gathered_shapes(ids):
    return tuple(jax.ShapeDtypeStruct(*GATHERED[w]) for w in ids)


def _fwd_proj(x, g1, inv2, w_in, w_out, w_up, w_down, conv_w):
    ids_a, ids_b = [W_IN, W_CONV], [W_OUT, W_DOWN]

    def body(x_ref, g_ref, inv_ref, in_hbm, out_hbm, up_hbm, dn_hbm, cw_ref,
             proj_ref, h1_ref, cos_ref, sin_ref, gin, gcw, gout, gdn, su_ref,
             w_vm, s_in, s_cw, s_out, s_dn, f_in, f_out, f_up, f_dn, ld_sems,
             a_send, a_recv, a_local, b_send, b_recv, b_local):
        ag_a = _Gather(ids_a, [s_in, s_cw], [gin, gcw], a_send, a_recv, a_local)
        ag_b = _Gather(ids_b, [s_out, s_dn], [gout, gdn], b_send, b_recv, b_local)

        @pl.when(pl.program_id(0) == 0)
        def _():
            meet = _Meet(diagonal=True)
            meet.signal()
            loads = [pltpu.make_async_copy(src, dst, ld_sems.at[i])
                     for i, (src, dst) in enumerate(((in_hbm, f_in), (out_hbm, f_out), (dn_hbm, f_dn), (up_hbm, f_up)))]
            for cp in loads:
                cp.start()
            s_cw[...] = jnp.zeros_like(s_cw)
            for k in range(3):
                s_cw[k:k + 1, :] = cw_ref[k]
            loads[0].wait()
            s_in[...] = f_in[...].astype(BF16)
            meet.wait()
            ag_a.start()
            loads[1].wait()
            s_out[...] = f_out[...].astype(BF16)
            loads[2].wait()
            s_dn[...] = f_dn[...].astype(BF16)
            ag_a.near()
            ag_b.start()
            loads[3].wait()
            su_ref[...] = f_up[...].astype(BF16)
            ag_a.finish()
            fill = pltpu.make_async_copy(gin, w_vm, ld_sems.at[4])
            fill.start()
            fill.wait()

        pl.when(pl.program_id(0) == FWD_PROJ_PASS_AT)(ag_b.near)

        xb = x_ref[...]
        r = lax.rsqrt(jnp.mean(xb * xb, axis=-1, keepdims=True) + EPS)
        h = ((xb * r) * g_ref[...]).astype(BF16)
        h1_ref[...] = h
        p = _dot(h, w_vm[...])
        c2, s2 = _rope_block(inv_ref[...], pl.program_id(0) * TM)
        cos_ref[...], sin_ref[...] = c2, s2
        for hd in range(HEADS):
            sl = slice(hd * HEAD_DIM, (hd + 1) * HEAD_DIM)
            proj_ref[:, sl] = _rot(p[:, sl], c2, s2)
            ks = slice(RET_W + hd * HEAD_DIM, RET_W + (hd + 1) * HEAD_DIM)
            proj_ref[:, ks] = _rot(p[:, ks], c2, s2) * K_SCALE
        proj_ref[:, 2 * RET_W:] = p[:, 2 * RET_W:]

        pl.when(pl.program_id(0) == N_TB - 1)(ag_b.finish)

    tok = lambda w: pl.BlockSpec((TM, w), lambda i: (i, 0))
    hbm = pl.BlockSpec(memory_space=pl.ANY)
    vm = pl.BlockSpec(memory_space=pltpu.VMEM)
    return pl.pallas_call(
        body, name="fwd_proj", grid=(N_TB,),
        out_shape=(jax.ShapeDtypeStruct((SEQ, PROJ_W), F32), jax.ShapeDtypeStruct((SEQ, D_MODEL), BF16),
                   jax.ShapeDtypeStruct((SEQ, HEAD_DIM), F32), jax.ShapeDtypeStruct((SEQ, HEAD_DIM), F32))
        + _gathered_shapes(ids_a + ids_b) + (jax.ShapeDtypeStruct(SHARD[W_UP], BF16),),
        in_specs=[tok(D_MODEL), _resident((1, D_MODEL)), _resident((1, HEAD_DIM)), hbm, hbm, hbm, hbm, vm],
        out_specs=(tok(PROJ_W), tok(D_MODEL), tok(HEAD_DIM), tok(HEAD_DIM), hbm, hbm, hbm, hbm, vm),
        scratch_shapes=[pltpu.VMEM((D_MODEL, PROJ_W), BF16), pltpu.VMEM(SHARD[W_IN], BF16), pltpu.VMEM(SHARD[W_CONV], F32),
                        pltpu.VMEM(SHARD[W_OUT], BF16), pltpu.VMEM(SHARD[W_DOWN], BF16),
                        pltpu.VMEM(SHARD[W_IN], F32), pltpu.VMEM(SHARD[W_OUT], F32), pltpu.VMEM(SHARD[W_UP], F32),
                        pltpu.VMEM(SHARD[W_DOWN], F32), pltpu.SemaphoreType.DMA((5,))]
        + _gather_scratch(len(ids_a)) + _gather_scratch(len(ids_b)),
        compiler_params=_cparams(("arbitrary",), collective=COLLECTIVE["fwd_proj"]),
    )(x, g1, inv2, w_in, w_out, w_up, w_down, conv_w)


def _causal(w):
    r = lax.broadcasted_iota(jnp.int32, (CHUNK, CHUNK), 0)
    c = lax.broadcasted_iota(jnp.int32, (CHUNK, CHUNK), 1)
    return jnp.where(r >= c, w, 0.0)


def _fwd_mix(x, proj, wout_g, grn, lng, lnb, ws, bsb, su):
    cdec = _chunk_decay()
    ids = [W_UP]

    def body(x_ref, p_ref, w_ref, grn_ref, lng_ref, lnb_ref, ws_ref, bsb_ref, su_ref,
             x2_ref, cat_ref, o_ref, sp_ref, gup, state, m_ref, qd_ref, kd_ref, send_sems, recv_sems, local_sems):
        ag = _Gather(ids, [su_ref], [gup], send_sems, recv_sems, local_sems)

        @pl.when(pl.program_id(0) == 0)
        def _():
            meet = _Meet(diagonal=False)
            meet.signal()
            state[...] = jnp.zeros_like(state)
            _fill_decay(m_ref, qd_ref, kd_ref)
            meet.wait()
            ag.start()

        for h in range(HEADS):
            sl = slice(h * HEAD_DIM, (h + 1) * HEAD_DIM)
            q = p_ref[:, sl]
            k = p_ref[:, RET_W + h * HEAD_DIM:RET_W + (h + 1) * HEAD_DIM]
            v = p_ref[:, 2 * RET_W + h * HEAD_DIM:2 * RET_W + (h + 1) * HEAD_DIM]
            g = p_ref[:, 3 * RET_W + h * HEAD_DIM:3 * RET_W + (h + 1) * HEAD_DIM]
            qb, kb, vb = q.astype(BF16), k.astype(BF16), v.astype(BF16)
            a = _dot_nt(qb, kb) * m_ref[h]
            spb = state[h].astype(BF16)
            sp_ref[0, h] = spb
            o = _dot(a.astype(BF16), vb) + _dot((q * qd_ref[h]).astype(BF16), spb)
            state[h] = state[h] * cdec[h] + _dot_tn((k * kd_ref[h]).astype(BF16), vb)
            o_ref[:, sl] = o
            rinv = lax.rsqrt(jnp.mean(o * o, axis=-1, keepdims=True) + EPS)
            rn = (o * rinv) * grn_ref[:, sl]
            cat_ref[:, sl] = ((g * _sigmoid(g)) * rn).astype(BF16)
        for gi in range(HEADS):
            sl = slice(gi * HEAD_DIM, (gi + 1) * HEAD_DIM)
            u = p_ref[:, 4 * RET_W + gi * HEAD_DIM:4 * RET_W + (gi + 1) * HEAD_DIM]
            sv = p_ref[:, 4 * RET_W + SGU_W + gi * HEAD_DIM:4 * RET_W + SGU_W + (gi + 1) * HEAD_DIM]
            gv = _gelu(sv)
            xc = gv - jnp.mean(gv, axis=-1, keepdims=True)
            vn = (xc * lax.rsqrt(jnp.mean(xc * xc, axis=-1, keepdims=True) + EPS)) * lng_ref[:, sl] + lnb_ref[:, sl]
            mixed = _dot(_causal(ws_ref[gi]).astype(BF16), vn.astype(BF16)) + bsb_ref[gi]
            cat_ref[:, RET_W + gi * HEAD_DIM:RET_W + (gi + 1) * HEAD_DIM] = (_gelu(u) * mixed).astype(BF16)
        x2_ref[...] = x_ref[...] + _dot(cat_ref[...], w_ref[...])

        pl.when(pl.program_id(0) == FWD_MIX_PASS_AT)(ag.near)
        pl.when(pl.program_id(0) == N_CHUNK - 1)(ag.finish)

    ch = lambda w: pl.BlockSpec((CHUNK, w), lambda i: (i, 0))
    hcc = (HEADS, CHUNK, CHUNK)
    hbm = pl.BlockSpec(memory_space=pl.ANY)
    return pl.pallas_call(
        body, name="fwd_mix", grid=(N_CHUNK,),
        out_shape=(jax.ShapeDtypeStruct((SEQ, D_MODEL), F32), jax.ShapeDtypeStruct((SEQ, D_MODEL), BF16),
                   jax.ShapeDtypeStruct((SEQ, RET_W), F32), jax.ShapeDtypeStruct((N_CHUNK, HEADS, HEAD_DIM, HEAD_DIM), BF16))
        + _gathered_shapes(ids),
        in_specs=[ch(D_MODEL), ch(PROJ_W), _resident((D_MODEL, D_MODEL)), _resident((1, RET_W)), _resident((1, SGU_W)),
                  _resident((1, SGU_W)), _resident(hcc), _resident(hcc), hbm],
        out_specs=(ch(D_MODEL), ch(D_MODEL), ch(RET_W), pl.BlockSpec((1, HEADS, HEAD_DIM, HEAD_DIM), lambda i: (i, 0, 0, 0)), hbm),
        scratch_shapes=[pltpu.VMEM((HEADS, HEAD_DIM, HEAD_DIM), F32)] + [pltpu.VMEM(hcc, F32)] * 3 + _gather_scratch(len(ids)),
        compiler_params=_cparams(("arbitrary",), collective=COLLECTIVE["fwd_mix"]),
    )(x, proj, wout_g, grn, lng, lnb, ws, bsb, su)


def _conv_taps(p, prev8):
    row = lax.broadcasted_iota(jnp.int32, p.shape, 0)
    p1 = jnp.where(row == 0, prev8[7:8, :], pltpu.roll(p, 1, 0))
    p2 = jnp.where(row == 0, prev8[6:7, :], jnp.where(row == 1, prev8[7:8, :], pltpu.roll(p, 2, 0)))
    return p1, p2


def _fwd_ffn(x2, g2, wup_g, cw_g, cb_g, wdn_g, gf, tgt):
    def body(x_ref, g_ref, wu_ref, cw_ref, cb_ref, wd_ref, gf_ref, t_ref, h2_ref, up_ref, u_ref, act_ref, x3_ref, loss_ref, carry):
        @pl.when(pl.program_id(0) == 0)
        def _():
            carry[...] = jnp.zeros_like(carry)

        xb = x_ref[...]
        r = lax.rsqrt(jnp.mean(xb * xb, axis=-1, keepdims=True) + EPS)
        h = ((xb * r) * g_ref[...]).astype(BF16)
        h2_ref[...] = h
        acc = xb
        for t0, tw in FF_TILES:
            u = []
            for c0 in (t0, D_FF + t0):
                cs = slice(c0, c0 + tw)
                p = _dot_nt(h, wu_ref[pl.ds(c0, tw), :])
                up_ref[:, cs] = p.astype(BF16)
                p1, p2 = _conv_taps(p, carry[:, cs])
                carry[:, cs] = p[TM - 8:, :]
                us = p2 * cw_ref[0:1, cs] + p1 * cw_ref[1:2, cs] + p * cw_ref[2:3, cs] + cb_ref[:, cs]
                u_ref[:, cs] = us.astype(BF16)
                u.append(us)
            a = ((u[0] * _sigmoid(u[0])) * u[1]).astype(BF16)
            act_ref[:, t0:t0 + tw] = a
            acc = acc + _dot(a, wd_ref[pl.ds(t0, tw), :])
        x3_ref[...] = acc
        r3 = lax.rsqrt(jnp.mean(acc * acc, axis=-1, keepdims=True) + EPS)
        diff = (acc * r3) * gf_ref[...] - t_ref[...]
        loss_ref[...] = jnp.full(loss_ref.shape, 0.5 * jnp.sum(jnp.mean(diff * diff, axis=-1)), F32)

    tok = lambda w: pl.BlockSpec((TM, w), lambda i: (i, 0))
    return pl.pallas_call(
        body, name="fwd_ffn", grid=(N_TB,),
        out_shape=(jax.ShapeDtypeStruct((SEQ, D_MODEL), BF16), jax.ShapeDtypeStruct((SEQ, 2 * D_FF), BF16),
                   jax.ShapeDtypeStruct((SEQ, 2 * D_FF), BF16),
                   jax.ShapeDtypeStruct((SEQ, D_FF), BF16), jax.ShapeDtypeStruct((SEQ, D_MODEL), F32),
                   jax.ShapeDtypeStruct((N_TB, 8, 128), F32)),
        in_specs=[tok(D_MODEL), _resident((1, D_MODEL)), _resident((2 * D_FF, D_MODEL)), _resident((8, 2 * D_FF)),
                  _resident((1, 2 * D_FF)), _resident((D_FF, D_MODEL)), _resident((1, D_MODEL)), tok(D_MODEL)],
        out_specs=(tok(D_MODEL), tok(2 * D_FF), tok(2 * D_FF), tok(D_FF), tok(D_MODEL),
                   pl.BlockSpec((1, 8, 128), lambda i: (i, 0, 0))),
        scratch_shapes=[pltpu.VMEM((8, 2 * D_FF), F32)],
        compiler_params=_cparams(("arbitrary",)),
    )(x2, g2, wup_g, cw_g, cb_g, wdn_g, gf, tgt)


def _bwd_ffn(x3, tgt, gf, x2, g2, up_pre, u_conv, wup_g, cw_g, wdn_g):
    def body(x3_ref, t_ref, gf_ref, x2_ref, g2_ref, up_ref, u_ref, wu_ref, cw_ref, wd_ref,
             dx3_ref, dpre_ref, dx2_ref, dgf_ref, dg2_ref, dcv_ref, nxt):
        i = pl.program_id(0)

        @pl.when(i == 0)
        def _():
            nxt[...] = jnp.zeros_like(nxt)
            dgf_ref[...] = jnp.zeros_like(dgf_ref)
            dg2_ref[...] = jnp.zeros_like(dg2_ref)
            dcv_ref[...] = jnp.zeros_like(dcv_ref)

        x3 = x3_ref[...]
        r3 = lax.rsqrt(jnp.mean(x3 * x3, axis=-1, keepdims=True) + EPS)
        xh3 = x3 * r3
        dy = (xh3 * gf_ref[...] - t_ref[...]) * (1.0 / D_MODEL)
        dgf_ref[0:1, :] += jnp.sum(dy * xh3, axis=0, keepdims=True)
        t3 = dy * gf_ref[...]
        dx3 = r3 * (t3 - xh3 * jnp.mean(t3 * xh3, axis=-1, keepdims=True))
        dx3b = dx3.astype(BF16)
        dx3_ref[...] = dx3b
        dh2 = jnp.zeros((TM, D_MODEL), F32)
        for t0, tw in FF_TILES:
            row = lax.broadcasted_iota(jnp.int32, (TM, tw), 0)
            ts = slice(t0, t0 + tw)
            dact = _dot_nt(dx3b, wd_ref[pl.ds(t0, tw), :])
            ua = u_ref[:, ts].astype(F32)
            ub = u_ref[:, D_FF + t0:D_FF + t0 + tw].astype(F32)
            sg = _sigmoid(ua)
            du = [dact * ub * (sg * (1.0 + ua * (1.0 - sg))), dact * (ua * sg)]
            for n in range(2):
                d = du[n]
                c0 = n * D_FF + t0
                cs = slice(c0, c0 + tw)
                nx = nxt[:, cs]
                n1 = jnp.where(row == TM - 1, nx[0:1, :], pltpu.roll(d, TM - 1, 0))
                n2 = jnp.where(row == TM - 2, nx[0:1, :], jnp.where(row == TM - 1, nx[1:2, :], pltpu.roll(d, TM - 2, 0)))
                nxt[:, cs] = d[0:8, :]
                dp = (d * cw_ref[2:3, cs] + n1 * cw_ref[1:2, cs] + n2 * cw_ref[0:1, cs]).astype(BF16)
                dpre_ref[:, cs] = dp
                p = up_ref[:, cs].astype(F32)
                dcv_ref[n, 0:1, ts] += jnp.sum(n2 * p, axis=0, keepdims=True)
                dcv_ref[n, 1:2, ts] += jnp.sum(n1 * p, axis=0, keepdims=True)
                dcv_ref[n, 2:3, ts] += jnp.sum(d * p, axis=0, keepdims=True)
                dcv_ref[n, 3:4, ts] += jnp.sum(d, axis=0, keepdims=True)
                dh2 = dh2 + _dot(dp, wu_ref[pl.ds(c0, tw), :])
        x2 = x2_ref[...]
        r2 = lax.rsqrt(jnp.mean(x2 * x2, axis=-1, keepdims=True) + EPS)
        xh2 = x2 * r2
        dg2_ref[0:1, :] += jnp.sum(dh2 * xh2, axis=0, keepdims=True)
        t2 = dh2 * g2_ref[...]
        dx2_ref[...] = dx3 + r2 * (t2 - xh2 * jnp.mean(t2 * xh2, axis=-1, keepdims=True))

    rev = lambda w: pl.BlockSpec((TM, w), lambda i: (N_TB - 1 - i, 0))
    acc = lambda s: pl.BlockSpec(s, lambda i: (0,) * len(s))
    return pl.pallas_call(
        body, name="bwd_ffn", grid=(N_TB,),
        out_shape=(jax.ShapeDtypeStruct((SEQ, D_MODEL), BF16), jax.ShapeDtypeStruct((SEQ, 2 * D_FF), BF16),
                   jax.ShapeDtypeStruct((SEQ, D_MODEL), F32), jax.ShapeDtypeStruct((8, D_MODEL), F32),
                   jax.ShapeDtypeStruct((8, D_MODEL), F32), jax.ShapeDtypeStruct((2, 8, D_FF), F32)),
        in_specs=[rev(D_MODEL), rev(D_MODEL), _resident((1, D_MODEL)), rev(D_MODEL), _resident((1, D_MODEL)), rev(2 * D_FF),
                  rev(2 * D_FF), _resident((2 * D_FF, D_MODEL)), _resident((8, 2 * D_FF)), _resident((D_FF, D_MODEL))],
        out_specs=(rev(D_MODEL), rev(2 * D_FF), rev(D_MODEL), acc((8, D_MODEL)), acc((8, D_MODEL)), acc((2, 8, D_FF))),
        scratch_shapes=[pltpu.VMEM((8, 2 * D_FF), F32)],
        compiler_params=_cparams(("arbitrary",)),
    )(x3, tgt, gf, x2, g2, up_pre, u_conv, wup_g, cw_g, wdn_g)


def _bwd_mix(dx2, proj, o, sprev, wout_g, grn, lng, lnb, ws, bsb, cos2, sin2, hosted):
    cdec = _chunk_decay()
    geoms = [g for g, _ in hosted]
    n_h = len(hosted)

    def body(dx2_ref, p_ref, o_ref, sp_ref, w_ref, grn_ref, lng_ref, lnb_ref, ws_ref, bsb_ref, cos_ref, sin_ref, *rest):
        dp_ref, dgrn_ref, dlng_ref, dlnb_ref, dws_ref, dbs_ref = rest[n_h:n_h + 6]
        dstate, dbs_acc, m_ref, qd_ref, kd_ref = rest[2 * n_h + 6:2 * n_h + 11]
        i = pl.program_id(0)
        rs = _Scatters(geoms, rest[:n_h], rest[n_h + 6:2 * n_h + 6], rest[2 * n_h + 11:])
        pl.when(i == 0)(rs.phase1)
        pl.when(i == 3)(rs.phase2)
        pl.when(i == 8)(rs.phase2b)

        @pl.when(i == 0)
        def _():
            _fill_decay(m_ref, qd_ref, kd_ref)
            dstate[...] = jnp.zeros_like(dstate)
            dgrn_ref[...] = jnp.zeros_like(dgrn_ref)
            dlng_ref[...] = jnp.zeros_like(dlng_ref)
            dlnb_ref[...] = jnp.zeros_like(dlnb_ref)
            dws_ref[...] = jnp.zeros_like(dws_ref)
            dbs_ref[...] = jnp.zeros_like(dbs_ref)
            dbs_acc[...] = jnp.zeros_like(dbs_acc)

        dmix = _dot_nt(dx2_ref[...].astype(BF16), w_ref[...])
        for h in range(HEADS):
            sl = slice(h * HEAD_DIM, (h + 1) * HEAD_DIM)
            q = p_ref[:, sl]
            k = p_ref[:, RET_W + h * HEAD_DIM:RET_W + (h + 1) * HEAD_DIM]
            v = p_ref[:, 2 * RET_W + h * HEAD_DIM:2 * RET_W + (h + 1) * HEAD_DIM]
            g = p_ref[:, 3 * RET_W + h * HEAD_DIM:3 * RET_W + (h + 1) * HEAD_DIM]
            o = o_ref[:, sl]
            rinv = lax.rsqrt(jnp.mean(o * o, axis=-1, keepdims=True) + EPS)
            oh = o * rinv
            gr = grn_ref[:, sl]
            sg = _sigmoid(g)
            dret = dmix[:, sl]
            dp_ref[:, 3 * RET_W + h * HEAD_DIM:3 * RET_W + (h + 1) * HEAD_DIM] = (
                dret * (oh * gr) * (sg * (1.0 + g * (1.0 - sg)))).astype(BF16)
            drn = dret * (g * sg)
            dgrn_ref[0:1, sl] += jnp.sum(drn * oh, axis=0, keepdims=True)
            t = drn * gr
            do = rinv * (t - oh * jnp.mean(t * oh, axis=-1, keepdims=True))
            qb, kb, vb, dob = q.astype(BF16), k.astype(BF16), v.astype(BF16), do.astype(BF16)
            m = m_ref[h]
            ab = (_dot_nt(qb, kb) * m).astype(BF16)
            dab = (_dot_nt(dob, vb) * m).astype(BF16)
            spb = sp_ref[0, h]
            dsn = dstate[h]
            dsnb = dsn.astype(BF16)
            qdb = (q * qd_ref[h]).astype(BF16)
            kdb = (k * kd_ref[h]).astype(BF16)
            dq = _dot(dab, kb) + _dot_nt(dob, spb) * qd_ref[h]
            dk = _dot_tn(dab, qb) + _dot_nt(vb, dsnb) * kd_ref[h]
            dv = _dot_tn(ab, dob) + _dot(kdb, dsnb)
            dstate[h] = dsn * cdec[h] + _dot_tn(qdb, dob)
            c2, s2 = cos_ref[...], sin_ref[...]
            dp_ref[:, sl] = _rot_t(dq, c2, s2).astype(BF16)
            dp_ref[:, RET_W + h * HEAD_DIM:RET_W + (h + 1) * HEAD_DIM] = _rot_t(dk * K_SCALE, c2, s2).astype(BF16)
            dp_ref[:, 2 * RET_W + h * HEAD_DIM:2 * RET_W + (h + 1) * HEAD_DIM] = dv.astype(BF16)
        for gi in range(HEADS):
            sl = slice(gi * HEAD_DIM, (gi + 1) * HEAD_DIM)
            u = p_ref[:, 4 * RET_W + gi * HEAD_DIM:4 * RET_W + (gi + 1) * HEAD_DIM]
            sv = p_ref[:, 4 * RET_W + SGU_W + gi * HEAD_DIM:4 * RET_W + SGU_W + (gi + 1) * HEAD_DIM]
            gv = _gelu(sv)
            xc = gv - jnp.mean(gv, axis=-1, keepdims=True)
            rstd = lax.rsqrt(jnp.mean(xc * xc, axis=-1, keepdims=True) + EPS)
            xh = xc * rstd
            lg = lng_ref[:, sl]
            vnb = (xh * lg + lnb_ref[:, sl]).astype(BF16)
            wcb = _causal(ws_ref[gi]).astype(BF16)
            mixed = _dot(wcb, vnb) + bsb_ref[gi]
            dsgu = dmix[:, RET_W + gi * HEAD_DIM:RET_W + (gi + 1) * HEAD_DIM]
            dmixed = dsgu * _gelu(u)
            dmb = dmixed.astype(BF16)
            dws_ref[gi] += _causal(_dot_nt(dmb, vnb))
            dbs_acc[gi] += dmixed
            dvn = _dot_tn(wcb, dmb)
            dlng_ref[gi:gi + 1, :] += jnp.sum(dvn * xh, axis=0, keepdims=True)
            dlnb_ref[gi:gi + 1, :] += jnp.sum(dvn, axis=0, keepdims=True)
            dxh = dvn * lg
            dgv = rstd * (dxh - jnp.mean(dxh, axis=-1, keepdims=True) - xh * jnp.mean(dxh * xh, axis=-1, keepdims=True))
            dp_ref[:, 4 * RET_W + gi * HEAD_DIM:4 * RET_W + (gi + 1) * HEAD_DIM] = (dsgu * mixed * _gelu_grad(u)).astype(BF16)
            dp_ref[:, 4 * RET_W + SGU_W + gi * HEAD_DIM:4 * RET_W + SGU_W + (gi + 1) * HEAD_DIM] = (
                dgv * _gelu_grad(sv)).astype(BF16)

        @pl.when(i == N_CHUNK - 1)
        def _():
            for gi in range(HEADS):
                col = jnp.broadcast_to(jnp.sum(dbs_acc[gi], axis=-1, keepdims=True), (CHUNK, CHUNK))
                dbs_ref[gi:gi + 1, :] = jnp.transpose(col)[0:1, :]
            rs.phase3()

    rev = lambda w: pl.BlockSpec((CHUNK, w), lambda i: (N_CHUNK - 1 - i, 0))
    hcc = (HEADS, CHUNK, CHUNK)
    acc = lambda s: pl.BlockSpec(s, lambda i: (0,) * len(s))
    res = pl.pallas_call(
        body, name="bwd_mix", grid=(N_CHUNK,),
        out_shape=(jax.ShapeDtypeStruct((SEQ, PROJ_W), BF16), jax.ShapeDtypeStruct((8, RET_W), F32),
                   jax.ShapeDtypeStruct((8, HEAD_DIM), F32), jax.ShapeDtypeStruct((8, HEAD_DIM), F32),
                   jax.ShapeDtypeStruct(hcc, F32), jax.ShapeDtypeStruct((8, CHUNK), F32)) + _scatter_out_shapes(geoms),
        in_specs=[rev(D_MODEL), rev(PROJ_W), rev(RET_W),
                  pl.BlockSpec((1, HEADS, HEAD_DIM, HEAD_DIM), lambda i: (N_CHUNK - 1 - i, 0, 0, 0)),
                  _resident((D_MODEL, D_MODEL)), _resident((1, RET_W)), _resident((1, SGU_W)), _resident((1, SGU_W)),
                  _resident(hcc), _resident(hcc), rev(HEAD_DIM), rev(HEAD_DIM)]
        + [pl.BlockSpec(memory_space=pl.ANY)] * n_h,
        out_specs=(rev(PROJ_W), acc((8, RET_W)), acc((8, HEAD_DIM)), acc((8, HEAD_DIM)), acc(hcc), acc((8, CHUNK)))
        + _scatter_out_specs(geoms),
        scratch_shapes=[pltpu.VMEM((HEADS, HEAD_DIM, HEAD_DIM), F32), pltpu.VMEM((HEADS, CHUNK, CHUNK), F32)]
        + [pltpu.VMEM(hcc, F32)] * 3 + _scatter_scratch(geoms),
        compiler_params=_cparams(("arbitrary",), collective=COLLECTIVE["bwd_mix"]),
    )(dx2, proj, o, sprev, wout_g, grn, lng, lnb, ws, bsb, cos2, sin2, *[p for _, p in hosted])
    return tuple(res[:6 + n_h])


def _bwd_proj(dproj, win_g, x, g1, dx2, gin_p, small):
    geoms = [W_IN]
    n_s = len(small)

    def body(dp_ref, w_ref, x_ref, g_ref, dx2_ref, gin_ref, *rest):
        small_refs = rest[:n_s]
        dx_ref, rs_out, rp_ref, rws_ref, rcv_ref, dg_ref = rest[n_s:n_s + 6]
        rs_scratch = rest[n_s + 6:n_s + 6 + N_SCATTER_SCRATCH]
        ar_scratch = rest[n_s + 6 + N_SCATTER_SCRATCH:]
        ar_res = ar_scratch[N_SMALL_SCRATCH:]
        ar = _SmallReduce((dg_ref,) + tuple(small_refs), ar_res, ar_scratch[:N_SMALL_SCRATCH])
        rs = _Scatters(geoms, [gin_ref], [rs_out], rs_scratch)
        pl.when(pl.program_id(0) == 0)(lambda: rs.phase1(diagonal=True))
        pl.when(pl.program_id(0) == 1)(rs.phase2)
        pl.when(pl.program_id(0) == 5)(rs.phase2b)

        @pl.when(pl.program_id(0) == 0)
        def _():
            dg_ref[...] = jnp.zeros_like(dg_ref)

        dh = _dot_nt(dp_ref[...], w_ref[...])
        xb = x_ref[...]
        r = lax.rsqrt(jnp.mean(xb * xb, axis=-1, keepdims=True) + EPS)
        xh = xb * r
        dg_ref[0:1, :] += jnp.sum(dh * xh, axis=0, keepdims=True)
        t = dh * g_ref[...]
        dx_ref[...] = dx2_ref[...] + r * (t - xh * jnp.mean(t * xh, axis=-1, keepdims=True))

        @pl.when(pl.program_id(0) == N_TB - 1)
        def _():
            ar.begin()
            rs.phase3()
            ar.end()
            for o_ref, r_ref in zip((rp_ref, rws_ref, rcv_ref), ar_res):
                o_ref[...] = r_ref[...]

    tok = lambda w: pl.BlockSpec((TM, w), lambda i: (i, 0))
    vm = pl.BlockSpec(memory_space=pltpu.VMEM)
    res = pl.pallas_call(
        body, name="bwd_proj", grid=(N_TB,),
        out_shape=(jax.ShapeDtypeStruct((SEQ, D_MODEL), F32),) + _scatter_out_shapes(geoms)
        + tuple(jax.ShapeDtypeStruct(s, F32) for s in SMALL_FULL),
        in_specs=[tok(PROJ_W), _resident((D_MODEL, PROJ_W)), tok(D_MODEL), _resident((1, D_MODEL)), tok(D_MODEL),
                  pl.BlockSpec(memory_space=pl.ANY)] + [vm] * n_s,
        out_specs=(tok(D_MODEL),) + _scatter_out_specs(geoms) + (vm,) * len(SMALL_FULL),
        scratch_shapes=[pltpu.VMEM((8, D_MODEL), F32)] + _scatter_scratch(geoms) + _small_scratch()
        + [pltpu.VMEM(s, F32) for s in SMALL_FULL],
        compiler_params=_cparams(("arbitrary",), collective=COLLECTIVE["bwd_proj"]),
    )(dproj, win_g, x, g1, dx2, gin_p, *small)
    return res


def _wgrad(name, a, b, tm=None, tn=None, hosted=()):
    m_w, n_w = a.shape[-1], b.shape[-1]
    tm = m_w if tm is None else tm
    tn = n_w if tn is None else tn
    n_steps = (m_w // tm) * (n_w // tn)
    geoms = [g for g, _ in hosted]
    n_h = len(hosted)

    def body(a_ref, b_ref, *rest):
        o_ref = rest[n_h]
        if n_h:
            rs = _Scatters(geoms, rest[:n_h], rest[n_h + 1:2 * n_h + 1], rest[2 * n_h + 1:])
            step = pl.program_id(0) * (n_w // tn) + pl.program_id(1)
            pl.when(step == 0)(rs.phase1)
            pl.when(step == 1)(rs.phase2)
            pl.when(step == n_steps // 2)(rs.phase2b)
        o_ref[...] = _dot_tn(a_ref[...].astype(BF16), b_ref[...].astype(BF16)).astype(BF16)
        if n_h:
            pl.when(step == n_steps - 1)(rs.phase3)

    assert not n_h or n_steps >= 4
    res = pl.pallas_call(
        body, name=name, grid=(m_w // tm, n_w // tn),
        out_shape=(jax.ShapeDtypeStruct((m_w, n_w), BF16),) + _scatter_out_shapes(geoms),
        in_specs=[pl.BlockSpec((SEQ, tm), lambda i, j: (0, i)), pl.BlockSpec((SEQ, tn), lambda i, j: (0, j))]
        + [pl.BlockSpec(memory_space=pl.ANY)] * n_h,
        out_specs=(pl.BlockSpec((tm, tn), lambda i, j: (i, j)),) + _scatter_out_specs(geoms),
        scratch_shapes=_scatter_scratch(geoms),
        compiler_params=_cparams(("arbitrary", "arbitrary"), collective=COLLECTIVE[name]) if n_h else _cparams(("parallel", "parallel")),
    )(a, b, *[p for _, p in hosted])
    return tuple(res[:1 + n_h])


def _row_step(half_rows):
    return max(s for s in range(16, 177, 16) if half_rows % s == 0)


class _Scatter:
    def __init__(self, geom, partial, out, land1, mine, stage2, land2, comb, s1_send, s1_recv, s2_send, s2_recv, ld_sems):
        self.w, self.row0, self.shape = _geom(geom)
        self.partial, self.out, self.land1 = partial, out, land1
        self.mine, self.stage2, self.land2, self.comb = mine, stage2, land2, comb
        self.hr = self.shape[0] // 2
        self.step = _row_step(self.hr)
        self.s1_send, self.s1_recv, self.s2_send, self.s2_recv, self.ld_sems = s1_send, s1_recv, s2_send, s2_recv, ld_sems
        self.x, self.y, self.c = lax.axis_index("x"), lax.axis_index("y"), lax.axis_index("c")
        self.sibling = (self.x, self.y, 1 - self.c)
        self.chips = [(self.x, self.y), (1 - self.x, self.y), (self.x, 1 - self.y), (1 - self.x, 1 - self.y)]

    def block(self, px, py, pc):
        dev = 4 * px + 2 * py + pc
        if self.w == W_IN:
            return self.partial.at[:, pl.ds(pl.multiple_of(dev * IN_SHARD, 128), IN_SHARD)]
        if self.w == W_OUT:
            return self.partial.at[pl.ds(pl.multiple_of(dev * OUT_SHARD, 128), OUT_SHARD), :]
        if self.w == W_DOWN:
            return self.partial.at[pl.ds(pl.multiple_of(dev * DOWN_SHARD, 32), DOWN_SHARD), :]
        return self.partial.at[pl.ds(pl.multiple_of(dev * FF_SHARD + self.row0, 32), self.shape[0]), :]

    def copy1(self, k):
        return pltpu.make_async_remote_copy(
            src_ref=self.block(*self.chips[k], 1 - self.c), dst_ref=self.land1.at[k],
            send_sem=self.s1_send.at[k], recv_sem=self.s1_recv.at[k], device_id=self.sibling, device_id_type=MESH)

    STAGE2 = [(1, 0, 1), (3, 0, 1), (2, 1, 2), (3, 1, 2), (1, 1, 1), (2, 0, 2)]

    def copy2(self, j):
        blk, h, to = self.STAGE2[j]
        src = self.comb.at[j - 4] if j >= 4 else self.stage2.at[blk - 1, pl.ds(h * self.hr, self.hr), :]
        return pltpu.make_async_remote_copy(
            src_ref=src, dst_ref=self.land2.at[j], send_sem=self.s2_send.at[j], recv_sem=self.s2_recv.at[j],
            device_id=(*self.chips[to], self.c), device_id_type=MESH)

    def _rows(self, h=None):
        step = self.step
        lo, n = (0, self.shape[0]) if h is None else (h * self.hr, self.hr)
        return [pl.ds(r0, step) for r0 in range(lo, lo + n, step)]

    def load(self, k):
        return pltpu.make_async_copy(self.block(*self.chips[k], self.c), self.mine.at[k], self.ld_sems.at[k])

    def load_mine(self):
        for k in range(4):
            self.load(k).start()

    def phase1(self):
        for k in range(4):
            self.copy1(k).start()

    def phase2(self, k):
        self.copy1(k).wait_recv()
        self.load(k).wait()
        for rs in self._rows():
            s = self.mine[k, rs, :].astype(F32) + self.land1[k, rs, :].astype(F32)
            if k == 0:
                self.out[rs, :] = s
            else:
                self.stage2[k - 1, rs, :] = s.astype(BF16)
        for j in {3: (1, 3), 1: (0,), 2: (2,), 0: ()}[k]:
            self.copy2(j).start()

    def phase2b(self):
        for j, got in ((4, 3), (5, 1)):
            blk, h, _ = self.STAGE2[j]
            self.copy2(got).wait_recv()
            for i, rs in enumerate(self._rows(h)):
                lr = pl.ds(i * self.step, self.step)
                self.comb[j - 4, lr, :] = (self.stage2[blk - 1, rs, :].astype(F32) + self.land2[got, lr, :].astype(F32)).astype(BF16)
            self.copy2(j).start()

    def phase3(self):
        for j in (0, 5, 4, 2):
            self.copy2(j).wait_recv()
        for h, (first, second) in enumerate(((0, 5), (4, 2))):
            for i, rs in enumerate(self._rows(h)):
                lr = pl.ds(i * self.step, self.step)
                self.out[rs, :] = (self.out[rs, :] + self.land2[first, lr, :].astype(F32)) + self.land2[second, lr, :].astype(F32)
        for k in range(4):
            self.copy1(k).wait_send()
        for j in range(6):
            self.copy2(j).wait_send()


def _geom(geom):
    if isinstance(geom, tuple):
        w, row0, rows = geom
        assert w == W_UP
        return w, row0, (rows, SHARD[w][1])
    return geom, 0, SHARD[geom]


N_SCATTER_SCRATCH = 10


def _scatter_out_shapes(geoms):
    return tuple(jax.ShapeDtypeStruct(_geom(g)[2], F32) for g in geoms)


def _scatter_out_specs(geoms):
    return (pl.BlockSpec(memory_space=pltpu.VMEM),) * len(geoms)


def _scatter_scratch(geoms):
    out = []
    for g in geoms:
        s = _geom(g)[2]
        hs = (s[0] // 2, s[1])
        out += [pltpu.VMEM((4,) + s, BF16), pltpu.VMEM((4,) + s, BF16), pltpu.VMEM((3,) + s, BF16), pltpu.VMEM((6,) + hs, BF16),
                pltpu.VMEM((2,) + hs, BF16),
                pltpu.SemaphoreType.DMA((4,)), pltpu.SemaphoreType.DMA((4,)), pltpu.SemaphoreType.DMA((6,)),
                pltpu.SemaphoreType.DMA((6,)), pltpu.SemaphoreType.DMA((4,))]
    return out


class _Scatters:
    def __init__(self, geoms, p_refs, out_refs, scratch):
        k = N_SCATTER_SCRATCH
        self.items = [_Scatter(g, p_refs[i], out_refs[i], *scratch[k * i:k * i + k]) for i, g in enumerate(geoms)]

    def phase1(self, diagonal=False):
        meet = _Meet(diagonal)
        meet.signal()
        for s in self.items:
            s.load_mine()
        meet.wait()
        for s in self.items:
            s.phase1()

    def phase2(self):
        for k in (3, 1, 2, 0):
            for s in self.items:
                s.phase2(k)

    def phase2b(self):
        for s in self.items:
            s.phase2b()

    def phase3(self):
        for s in self.items:
            s.phase3()


PACK_W = 1024


SMALL_FULL = [(2, 8, PACK_W), (HEADS, CHUNK, CHUNK), (2, 8, D_FF)]
SMALL_HALF = [(s[0] // 2,) + s[1:] for s in SMALL_FULL]
N_SMALL_SCRATCH = 16


def _small_scratch():
    n_a = len(SMALL_FULL)
    return ([pltpu.VMEM(SMALL_FULL[0], F32)] + [pltpu.VMEM(s, F32) for s in SMALL_HALF] + [pltpu.VMEM(s, F32) for s in SMALL_HALF]
            + [pltpu.VMEM((3,) + s, F32) for s in SMALL_HALF]
            + [pltpu.SemaphoreType.DMA((n_a,)), pltpu.SemaphoreType.DMA((n_a,)), pltpu.SemaphoreType.DMA((n_a, 3)),
               pltpu.SemaphoreType.DMA((n_a, 3)), pltpu.SemaphoreType.DMA((n_a,)), pltpu.SemaphoreType.DMA((n_a,))])


class _SmallReduce:
    def __init__(self, ins, outs, scratch):
        self.ins, self.outs = ins, outs
        (self.pack, *rest) = scratch
        self.rxs, self.css, self.gs = rest[0:3], rest[3:6], rest[6:9]
        self.s1_send, self.s1_recv, self.s2_send, self.s2_recv, self.s3_send, self.s3_recv = rest[9:]
        self.x, self.y, self.c = lax.axis_index("x"), lax.axis_index("y"), lax.axis_index("c")
        self.sibling = (self.x, self.y, 1 - self.c)
        self.chips = [(1 - self.x, self.y), (self.x, 1 - self.y), (1 - self.x, 1 - self.y)]
        self.hl = [s[0] for s in SMALL_HALF]

    def half(self, ref, a, h):
        return ref.at[pl.ds(h * self.hl[a], self.hl[a])]

    def begin(self):
        dg1_ref, dg2_ref, dgf_ref, dgrn_ref, dlng_ref, dlnb_ref, dbs_ref, loss_ref, dws_ref, dcv_ref = self.ins
        pack, c = self.pack, self.c
        pack[...] = jnp.zeros_like(pack)
        pack[0, 0:1, :] = dg1_ref[0:1, :]
        pack[0, 1:2, :] = dg2_ref[0:1, :]
        pack[0, 2:3, :] = dgf_ref[0:1, :]
        pack[0, 3:4, 0:RET_W] = dgrn_ref[0:1, :]
        lsum = loss_ref[0, 0:1, :]
        for i in range(1, N_TB):
            lsum = lsum + loss_ref[i, 0:1, :]
        pack[0, 3:4, RET_W:RET_W + 128] = lsum
        pack[1, 0:HEADS, 0:128] = dlng_ref[0:HEADS, :]
        pack[1, 0:HEADS, 128:256] = dlnb_ref[0:HEADS, :]
        pack[1, 0:HEADS, 256:384] = dbs_ref[0:HEADS, :]
        self.srcs = [pack, dws_ref, dcv_ref]
        n_a = len(self.srcs)
        self.ex1 = [pltpu.make_async_remote_copy(src_ref=self.half(self.srcs[a], a, 1 - c), dst_ref=self.rxs[a],
                                                 send_sem=self.s1_send.at[a], recv_sem=self.s1_recv.at[a],
                                                 device_id=self.sibling, device_id_type=MESH) for a in range(n_a)]
        for cp in self.ex1:
            cp.start()
        self.ex2 = []
        for a in range(n_a):
            self.ex1[a].wait_recv()
            self.css[a][...] = self.half(self.srcs[a], a, c)[...] + self.rxs[a][...]
            for j, chip in enumerate(self.chips):
                cp = pltpu.make_async_remote_copy(src_ref=self.css[a], dst_ref=self.gs[a].at[j], send_sem=self.s2_send.at[a, j],
                                                  recv_sem=self.s2_recv.at[a, j], device_id=(*chip, c), device_id_type=MESH)
                cp.start()
                self.ex2.append(cp)

    def end(self):
        c, x, y = self.c, self.x, self.y
        ex3 = []
        for a in range(len(self.srcs)):
            css, gs, out = self.css[a], self.gs[a], self.outs[a]
            for j in range(3):
                self.ex2[3 * a + j].wait_recv()
            tot = None
            for q in range(4):
                k = jnp.where(x != (q >> 1), 1, 0) + jnp.where(y != (q & 1), 2, 0)
                term = jnp.where(k == 0, css[...], jnp.where(k == 1, gs[0], jnp.where(k == 2, gs[1], gs[2])))
                tot = term if tot is None else tot + term
            self.half(out, a, c)[...] = tot
            cp = pltpu.make_async_remote_copy(src_ref=self.half(out, a, c), dst_ref=self.half(out, a, c), send_sem=self.s3_send.at[a],
                                              recv_sem=self.s3_recv.at[a], device_id=self.sibling, device_id_type=MESH)
            cp.start()
            ex3.append(cp)
        for a in range(len(self.srcs)):
            out = self.outs[a]
            pltpu.make_async_remote_copy(src_ref=self.half(out, a, 1 - c), dst_ref=self.half(out, a, 1 - c), send_sem=self.s3_send.at[a],
                                         recv_sem=self.s3_recv.at[a], device_id=self.sibling, device_id_type=MESH).wait_recv()
        for cp in self.ex1 + self.ex2 + ex3:
            cp.wait_send()


def _adam_math(w, g, m, v):
    nm = ADAM_B1 * m + (1.0 - ADAM_B1) * g
    nv = ADAM_B2 * v + (1.0 - ADAM_B2) * (g * g)
    d = -ADAM_LR * ((nm / (1.0 - ADAM_B1 ** ADAM_STEP)) / (jnp.sqrt(nv / (1.0 - ADAM_B2 ** ADAM_STEP)) + ADAM_EPS) + ADAM_WD * w)
    return d, nm, nv


def _adamw(params, thru, n_steps):
    plan = []
    for w, gs, _, _ in params:
        _, r, cdim = w.shape
        if len(gs) == 1:
            edges = [0, r // n_steps]
            spec3 = pl.BlockSpec((1, r // n_steps, cdim), lambda i: (0, i, 0))
            g_specs = [pl.BlockSpec((r // n_steps, cdim), lambda i: (i, 0))]
        else:
            edges = [sum(g.shape[0] for g in gs[:k]) for k in range(len(gs) + 1)]
            spec3 = pl.BlockSpec((1, r, cdim // n_steps), lambda i: (0, 0, i))
            g_specs = [pl.BlockSpec((g.shape[0], cdim // n_steps), lambda i: (0, i)) for g in gs]
        plan.append((len(gs), edges, spec3, g_specs))
    n_in = sum(n_g + 3 for n_g, _, _, _ in plan)
    n_t = len(thru)

    def body(*refs):
        ins, outs = refs[:n_in], refs[n_in + n_t:]
        for n_g, edges, _, _ in plan:
            (w_ref, *g_refs, m_ref, v_ref), ins = ins[:n_g + 3], ins[n_g + 3:]
            (go_ref, d_ref, nm_ref, nv_ref), outs = outs[:4], outs[4:]
            for g_ref, lo, hi in zip(g_refs, edges[:-1], edges[1:]):
                gg = g_ref[...]
                go_ref[0, lo:hi, :] = gg
                d_ref[0, lo:hi, :], nm_ref[0, lo:hi, :], nv_ref[0, lo:hi, :] = _adam_math(
                    w_ref[0, lo:hi, :], gg, m_ref[0, lo:hi, :], v_ref[0, lo:hi, :])
        for t_ref, to_ref in zip(refs[n_in:n_in + n_t], outs):
            to_ref[...] = t_ref[...]

    t_specs = [pl.BlockSpec((t.shape[0] // n_steps, t.shape[1]), lambda i: (i, 0)) for t in thru]
    in_specs, out_specs, out_shape, args = [], [], [], []
    for (w, gs, m, v), (_, _, spec3, g_specs) in zip(params, plan):
        in_specs += [spec3] + g_specs + [spec3, spec3]
        out_specs += [spec3] * 4
        out_shape += [jax.ShapeDtypeStruct(w.shape, F32)] * 4
        args += [w, *gs, m, v]
    res = pl.pallas_call(
        body, name="adamw", grid=(n_steps,), out_shape=tuple(out_shape) + tuple(jax.ShapeDtypeStruct(t.shape, t.dtype) for t in thru),
        in_specs=in_specs + t_specs, out_specs=tuple(out_specs) + tuple(t_specs),
        compiler_params=_cparams(("parallel",)),
    )(*args, *thru)
    return [res[4 * k:4 * k + 4] for k in range(len(params))], res[4 * len(params):]


def _adamw_small(rp, rws, rcv, gcw, params):
    n_p = len(params)

    def body(*refs):
        rp_ref, rws_ref, rcv_ref, gcw_ref = refs[:4]
        ins = refs[4:4 + 3 * n_p]
        outs = refs[4 + 3 * n_p:]
        outs[4 * n_p][...] = rp_ref[0, 3:4, RET_W:RET_W + 1]
        grads = [rp_ref[0, 0:1, :], rp_ref[0, 1:2, :], rp_ref[0, 2:3, :], rp_ref[0, 3:4, 0:RET_W],
                 rp_ref[1, 0:HEADS, 0:128], rp_ref[1, 0:HEADS, 128:256], rp_ref[1, 0:HEADS, 256:384],
                 rws_ref[...], gcw_ref[...], None]
        for p in range(n_p):
            w_ref, m_ref, v_ref = ins[3 * p:3 * p + 3]
            o = outs[4 * p:4 * p + 4]
            if p == n_p - 1:
                for hf in range(2):
                    cs = slice(hf * D_FF, (hf + 1) * D_FF)
                    g = rcv_ref[hf, 3:4, :]
                    res = (g,) + _adam_math(w_ref[:, cs], g, m_ref[:, cs], v_ref[:, cs])
                    for t in range(4):
                        o[t][:, cs] = res[t]
                continue
            lead = w_ref.ndim > grads[p].ndim
            rd = (lambda r: r[0]) if lead else (lambda r: r[...])
            res = (grads[p],) + _adam_math(rd(w_ref), grads[p], rd(m_ref), rd(v_ref))
            for t in range(4):
                if lead:
                    o[t][0] = res[t]
                else:
                    o[t][...] = res[t]

    vm = pl.BlockSpec(memory_space=pltpu.VMEM)
    flat = [a for tr in params for a in tr]
    out_shape = tuple(jax.ShapeDtypeStruct(tr[0].shape, F32) for tr in params for _ in range(4)) + (jax.ShapeDtypeStruct((1, 1), F32),)
    res = pl.pallas_call(
        body, name="adamw_small", out_shape=out_shape, in_specs=[vm] * (4 + len(flat)), out_specs=(vm,) * len(out_shape),
        compiler_params=_cparams(),
    )(rp, rws, rcv, gcw, *flat)
    return [res[4 * p:4 * p + 4] for p in range(n_p)], res[4 * n_p]


def kernel(x, mix_norm_g, w_in, ret_norm_g, sgu_ln_g, sgu_ln_b, sgu_w_s, sgu_b_s, w_out, ffn_norm_g, w_up, conv_w, conv_b, w_down, final_norm_g, loss_target, m_mix_norm_g, m_w_in, m_ret_norm_g, m_sgu_ln_g, m_sgu_ln_b, m_sgu_w_s, m_sgu_b_s, m_w_out, m_ffn_norm_g, m_w_up, m_conv_w, m_conv_b, m_w_down, m_final_norm_g, v_mix_norm_g, v_w_in, v_ret_norm_g, v_sgu_ln_g, v_sgu_ln_b, v_sgu_w_s, v_sgu_b_s, v_w_out, v_ffn_norm_g, v_w_up, v_conv_w, v_conv_b, v_w_down, v_final_norm_g):
    xs = x[0]
    tgt = loss_target[0]
    grn = ret_norm_g.reshape(1, RET_W)
    lng = sgu_ln_g.reshape(1, SGU_W)
    lnb = sgu_ln_b.reshape(1, SGU_W)
    ws = sgu_w_s[0]
    bsb = jnp.broadcast_to(sgu_b_s[0][:, :, None], (HEADS, CHUNK, HEAD_DIM))
    gf = final_norm_g.reshape(1, D_MODEL)
    me = 4 * lax.axis_index("x") + 2 * lax.axis_index("y") + lax.axis_index("c")
    tr = lambda a: jnp.transpose(a[0])[None]
    tr_cw = lambda a: jnp.transpose(a, (1, 0, 2))

    proj, h1, cos2, sin2, win_g, cw_sh, wout_g, wdn_g, su = _fwd_proj(
        xs, mix_norm_g, _rope_freq(), w_in[0], w_out[0], tr(w_up)[0], w_down[0], tr_cw(conv_w))
    cw_g = jnp.transpose(cw_sh, (1, 0, 2)).reshape(8, 2 * D_FF)
    x2, mixcat, o, sprev, wup_g = _fwd_mix(xs, proj, wout_g, grn, lng, lnb, ws, bsb, su)
    h2, up_pre, u_conv, act, x3, loss_parts = _fwd_ffn(x2, ffn_norm_g, wup_g, cw_g, conv_b, wdn_g, gf, tgt)

    dx3, dpre, dx2, dgf, dg2, dcv = _bwd_ffn(x3, tgt, gf, x2, ffn_norm_g, up_pre, u_conv, wup_g, cw_g, wdn_g)
    band = 448
    (gout_p,) = _wgrad("wgrad_out", mixcat, dx2, tn=512)
    gdn_p, g_out = _wgrad("wgrad_down", act, dx3, tm=FF_TILE, tn=512, hosted=[(W_OUT, gout_p)])
    gup_p, g_dn = _wgrad("wgrad_up", dpre, h2, tm=FF_TILE, tn=512, hosted=[(W_DOWN, gdn_p)])
    dproj, dgrn, dlng, dlnb, dws, dbs, g_up_a = _bwd_mix(
        dx2, proj, o, sprev, wout_g, grn, lng, lnb, ws, bsb, cos2, sin2, [((W_UP, 0, band), gup_p)])
    gin_p, g_up_b = _wgrad("wgrad_in", h1, dproj, tm=512, tn=768, hosted=[((W_UP, band, FF_SHARD - band), gup_p)])
    grad_x, g_in, rp, rws, rcv = _bwd_proj(dproj, win_g, xs, mix_norm_g, dx2, gin_p,
                                           (dg2, dgf, dgrn, dlng, dlnb, dbs, loss_parts, dws, dcv))
    gcw = tr_cw(lax.dynamic_slice(rcv, (me // (N_DEV // 2), 0, (me % (N_DEV // 2)) * FF_SHARD), (1, 3, FF_SHARD)))

    table = {}
    big, (grad_x,) = _adamw([(w_in, [g_in], m_w_in, v_w_in), (w_out, [g_out], m_w_out, v_w_out),
                             (tr(w_up), [g_up_a, g_up_b], tr(m_w_up), tr(v_w_up)), (w_down, [g_dn], m_w_down, v_w_down)],
                            [grad_x], n_steps=4)
    table.update(zip(("w_in", "w_out", "w_up", "w_down"), big))
    table["w_up"] = tuple(tr(a) for a in table["w_up"])
    row = lambda a: a.reshape(1, D_MODEL)
    names_small = ["mix_norm_g", "ffn_norm_g", "final_norm_g", "ret_norm_g", "sgu_ln_g", "sgu_ln_b", "sgu_b_s", "sgu_w_s",
                   "conv_w", "conv_b"]
    params = [(mix_norm_g, m_mix_norm_g, v_mix_norm_g), (ffn_norm_g, m_ffn_norm_g, v_ffn_norm_g),
              (row(final_norm_g), row(m_final_norm_g), row(v_final_norm_g)), (ret_norm_g, m_ret_norm_g, v_ret_norm_g),
              (sgu_ln_g, m_sgu_ln_g, v_sgu_ln_g), (sgu_ln_b, m_sgu_ln_b, v_sgu_ln_b), (sgu_b_s, m_sgu_b_s, v_sgu_b_s),
              (sgu_w_s, m_sgu_w_s, v_sgu_w_s), (tr_cw(conv_w), tr_cw(m_conv_w), tr_cw(v_conv_w)), (conv_b, m_conv_b, v_conv_b)]
    small, loss = _adamw_small(rp, rws, rcv, gcw, params)
    for n, res in zip(names_small, small):
        table[n] = res
    table["final_norm_g"] = tuple(a.reshape(D_MODEL) for a in table["final_norm_g"])
    table["conv_w"] = tuple(tr_cw(a) for a in table["conv_w"])

    order = ["mix_norm_g", "w_in", "ret_norm_g", "sgu_ln_g", "sgu_ln_b", "sgu_w_s", "sgu_b_s", "w_out", "ffn_norm_g", "w_up",
             "conv_w", "conv_b", "w_down", "final_norm_g"]
    outs = [loss.reshape(()), grad_x[None]]
    for col in range(4):
        outs += [table[n][col] for n in order]
    return tuple(outs)
```

```python
import functools
import math

import jax
import jax.numpy as jnp
import numpy as np
from jax import lax
from jax.experimental import pallas as pl
from jax.experimental.pallas import tpu as pltpu

F32 = jnp.float32
BF16 = jnp.bfloat16
MESH = pl.DeviceIdType.MESH

N_DEV = 8
SEQ = 2048
D_MODEL = 1024
CHUNK = 128
N_CHUNK = SEQ // CHUNK
HEADS = 4
HEAD_DIM = 128
RET_W = 512
SGU_W = 512
PROJ_W = 3072
D_FF = 2816
FF_SHARD = 704
FF_TILE = 1408
FF_TILES = ((0, D_FF),)
IN_SHARD = PROJ_W // N_DEV
OUT_SHARD = D_MODEL // N_DEV
DOWN_SHARD = D_FF // N_DEV
TM = 256
N_TB = SEQ // TM
FWD_PROJ_PASS_AT = 5
FWD_MIX_PASS_AT = 10
EPS = 1e-6
ROPE_BASE = 10000.0
K_SCALE = HEAD_DIM ** -0.5
INV_SQRT2 = 0.7071067811865476
INV_SQRT_2PI = 0.3989422804014327

ADAM_LR = 0.001
ADAM_B1 = 0.9
ADAM_B2 = 0.999
ADAM_EPS = 1e-08
ADAM_WD = 0.01
ADAM_STEP = 10

VMEM_LIMIT = 56 * 1024 * 1024


def _cparams(sem=None, vmem=VMEM_LIMIT, collective=None):
    return pltpu.CompilerParams(dimension_semantics=sem, vmem_limit_bytes=vmem, collective_id=collective)


COLLECTIVE = {name: k for k, name in enumerate(("fwd_proj", "fwd_mix", "wgrad_down", "wgrad_up", "bwd_mix", "wgrad_in", "bwd_proj"))}


class _Meet:
    def __init__(self, diagonal):
        x, y, c = lax.axis_index("x"), lax.axis_index("y"), lax.axis_index("c")
        self.peers = [(x, y, 1 - c), (1 - x, y, c), (x, 1 - y, c)] + ([(1 - x, 1 - y, c)] if diagonal else [])

    def signal(self):
        for peer in self.peers:
            pl.semaphore_signal(pltpu.get_barrier_semaphore(), inc=1, device_id=peer, device_id_type=MESH)

    def wait(self):
        pl.semaphore_wait(pltpu.get_barrier_semaphore(), len(self.peers))


def _resident(shape):
    nd = len(shape)
    return pl.BlockSpec(shape, lambda *_: (0,) * nd, pipeline_mode=pl.Buffered(1))


def _dot(a, b):
    return jnp.dot(a, b, preferred_element_type=F32)


def _dot_nt(a, b):
    return lax.dot_general(a, b, (((1,), (1,)), ((), ())), preferred_element_type=F32)


def _dot_tn(a, b):
    return lax.dot_general(a, b, (((0,), (0,)), ((), ())), preferred_element_type=F32)


def _sigmoid(x):
    return 1.0 / (1.0 + jnp.exp(-x))


def _gelu(x):
    return 0.5 * x * (1.0 + lax.erf(x * INV_SQRT2))


def _gelu_grad(x):
    return 0.5 * (1.0 + lax.erf(x * INV_SQRT2)) + x * (jnp.exp(-0.5 * x * x) * INV_SQRT_2PI)


def _rot(xh, cos2, sin2):
    return xh * cos2 + pltpu.roll(xh, HEAD_DIM // 2, 1) * sin2


def _rot_t(dh, cos2, sin2):
    return dh * cos2 + pltpu.roll(dh * sin2, HEAD_DIM // 2, 1)


def _rope_freq():
    half = HEAD_DIM // 2
    inv_freq = jnp.power(ROPE_BASE, -jnp.arange(half, dtype=F32) / half)
    return jnp.concatenate([inv_freq, inv_freq])[None, :]


def _rope_block(inv2, first_row):
    pos = (lax.broadcasted_iota(jnp.int32, (TM, HEAD_DIM), 0) + first_row).astype(F32)
    ang = pos * inv2
    sin = jnp.sin(ang)
    lane = lax.broadcasted_iota(jnp.int32, (TM, HEAD_DIM), 1)
    return jnp.cos(ang), jnp.where(lane < HEAD_DIM // 2, -sin, sin)


def _log_gamma():
    return np.log(np.float32(1.0) - np.power(np.float32(2.0), -5.0 - np.arange(HEADS, dtype=np.float32))).astype(np.float32)


def _fill_decay(mask_ref, qd_ref, kd_ref):
    assert HEAD_DIM == CHUNK
    lg = _log_gamma()
    t = lax.broadcasted_iota(jnp.int32, (CHUNK, CHUNK), 0).astype(F32)
    diff = t - lax.broadcasted_iota(jnp.int32, (CHUNK, CHUNK), 1).astype(F32)
    for h in range(HEADS):
        mask_ref[h] = jnp.where(diff >= 0.0, jnp.exp(float(lg[h]) * jnp.maximum(diff, 0.0)), 0.0)
        qd_ref[h] = jnp.exp(float(lg[h]) * (t + 1.0))
        kd_ref[h] = jnp.exp(float(lg[h]) * (CHUNK - 1.0 - t))


def _chunk_decay():
    lg = _log_gamma()
    return [float(np.exp(lg[h] * np.float32(CHUNK))) for h in range(HEADS)]


W_IN, W_OUT, W_UP, W_DOWN, W_CONV = range(5)
GATHERED = {W_IN: ((D_MODEL, PROJ_W), BF16), W_OUT: ((D_MODEL, D_MODEL), BF16), W_UP: ((2 * D_FF, D_MODEL), BF16),
            W_DOWN: ((D_FF, D_MODEL), BF16), W_CONV: ((N_DEV, 8, FF_SHARD), F32)}
SHARD = {W_IN: (D_MODEL, IN_SHARD), W_OUT: (OUT_SHARD, D_MODEL), W_UP: (FF_SHARD, D_MODEL), W_DOWN: (DOWN_SHARD, D_MODEL),
         W_CONV: (8, FF_SHARD)}


class _Gather:
    N_SEMS = 9

    def __init__(self, ids, stages, gathered, send_sems, recv_sems, local_sems):
        self.ids, self.stages, self.gathered = ids, stages, gathered
        self.send_sems, self.recv_sems, self.local_sems = send_sems, recv_sems, local_sems
        self.x, self.y, self.c = lax.axis_index("x"), lax.axis_index("y"), lax.axis_index("c")
        self.me = (self.x, self.y, self.c)
        self.sibling = (self.x, self.y, 1 - self.c)
        self.chips = [(1 - self.x, self.y), (self.x, 1 - self.y), (1 - self.x, 1 - self.y)]

    def slot(self, n, px, py, pc):
        dev = 4 * px + 2 * py + pc
        w, g = self.ids[n], self.gathered[n]
        if w == W_IN:
            return g.at[:, pl.ds(pl.multiple_of(dev * IN_SHARD, 128), IN_SHARD)]
        if w == W_OUT:
            return g.at[pl.ds(pl.multiple_of(dev * OUT_SHARD, 128), OUT_SHARD), :]
        if w == W_DOWN:
            return g.at[pl.ds(pl.multiple_of(dev * DOWN_SHARD, 32), DOWN_SHARD), :]
        if w == W_UP:
            return g.at[pl.ds(pl.multiple_of(dev * FF_SHARD, 32), FF_SHARD), :]
        return g.at[dev]

    def half(self, n, px, py, pc, h):
        dev = 4 * px + 2 * py + pc
        w, g = self.ids[n], self.gathered[n]
        if w == W_IN:
            return g.at[pl.ds(h * (D_MODEL // 2), D_MODEL // 2), pl.ds(pl.multiple_of(dev * IN_SHARD, 128), IN_SHARD)]
        rows = SHARD[w][0] // 2
        return g.at[pl.ds(pl.multiple_of(dev * SHARD[w][0] + h * rows, 16), rows), :]

    def tree(self, n):
        return self.ids[n] != W_CONV

    def copy(self, n, k, block, to, src=None, h=None):
        ref = self.slot(n, *block) if h is None else self.half(n, *block, h)
        return pltpu.make_async_remote_copy(
            src_ref=ref if src is None else src, dst_ref=ref,
            send_sem=self.send_sems.at[n, k], recv_sem=self.recv_sems.at[n, k], device_id=to, device_id_type=MESH)

    def _mine(self):
        return [pltpu.make_async_copy(self.stages[n], self.slot(n, *self.me), self.local_sems.at[n]) for n in range(len(self.ids))]

    def _first(self):
        out = []
        for n in range(len(self.ids)):
            out.append(self.copy(n, 0, self.me, self.sibling, src=self.stages[n]))
            out += [self.copy(n, 1 + j, self.me, (*chip, self.c), src=self.stages[n])
                    for j, chip in enumerate(self.chips[:2] if self.tree(n) else self.chips)]
        return out

    def start(self):
        for cp in self._mine() + self._first():
            cp.start()

    def _passed(self, j):
        dev = (*self.chips[j], self.c)
        out = []
        for n in range(len(self.ids)):
            if not self.tree(n):
                out.append(self.copy(n, 4 + j, dev, self.sibling))
            elif j < 2:
                out += [self.copy(n, 3 + j, dev, (*self.chips[1 - j], self.c), h=j), self.copy(n, 5 + j, dev, self.sibling)]
            else:
                out += [self.copy(n, 7, dev, self.sibling, h=0), self.copy(n, 8, dev, self.sibling, h=1)]
        return out

    def near(self):
        for j in range(2):
            dev = (*self.chips[j], self.c)
            for n in range(len(self.ids)):
                self.copy(n, 1 + j, dev, self.me).wait_recv()
            for cp in self._passed(j):
                cp.start()

    def finish(self):
        dev = (*self.chips[2], self.c)
        for n in range(len(self.ids)):
            if self.tree(n):
                self.copy(n, 3, dev, self.me, h=0).wait_recv()
                self.copy(n, 4, dev, self.me, h=1).wait_recv()
            else:
                self.copy(n, 3, dev, self.me).wait_recv()
        for cp in self._passed(2):
            cp.start()
        for n in range(len(self.ids)):
            self.copy(n, 0, self.sibling, self.me).wait_recv()
            for j, chip in enumerate(self.chips):
                dev = (*chip, 1 - self.c)
                if not self.tree(n):
                    self.copy(n, 4 + j, dev, self.me).wait_recv()
                elif j < 2:
                    self.copy(n, 5 + j, dev, self.me).wait_recv()
                else:
                    self.copy(n, 7, dev, self.me, h=0).wait_recv()
                    self.copy(n, 8, dev, self.me, h=1).wait_recv()
        for cp in self._mine():
            cp.wait()
        for cp in self._first() + self._passed(0) + self._passed(1) + self._passed(2):
            cp.wait_send()


def _gather_scratch(n):
    return [pltpu.SemaphoreType.DMA((n, _Gather.N_SEMS)), pltpu.SemaphoreType.DMA((n, _Gather.N_SEMS)), pltpu.SemaphoreType.DMA((n,))]


def _gathered_shapes(ids):
    return tuple(jax.ShapeDtypeStruct(*GATHERED[w]) for w in ids)


def _fwd_proj(x, g1, inv2, w_in, w_out, w_up, w_down, conv_w):
    ids_a, ids_b = [W_IN, W_CONV], [W_OUT, W_DOWN]

    def body(x_ref, g_ref, inv_ref, in_hbm, out_hbm, up_hbm, dn_hbm, cw_ref,
             proj_ref, h1_ref, cos_ref, sin_ref, gin, gcw, gout, gdn, su_ref,
             w_vm, s_in, s_cw, s_out, s_dn, f_in, f_out, f_up, f_dn, ld_sems,
             a_send, a_recv, a_local, b_send, b_recv, b_local):
        ag_a = _Gather(ids_a, [s_in, s_cw], [gin, gcw], a_send, a_recv, a_local)
        ag_b = _Gather(ids_b, [s_out, s_dn], [gout, gdn], b_send, b_recv, b_local)

        @pl.when(pl.program_id(0) == 0)
        def _():
            meet = _Meet(diagonal=True)
            meet.signal()
            loads = [pltpu.make_async_copy(src, dst, ld_sems.at[i])
                     for i, (src, dst) in enumerate(((in_hbm, f_in), (out_hbm, f_out), (dn_hbm, f_dn), (up_hbm, f_up)))]
            for cp in loads:
                cp.start()
            s_cw[...] = jnp.zeros_like(s_cw)
            for k in range(3):
                s_cw[k:k + 1, :] = cw_ref[k]
            loads[0].wait()
            s_in[...] = f_in[...].astype(BF16)
            meet.wait()
            ag_a.start()
            loads[1].wait()
            s_out[...] = f_out[...].astype(BF16)
            loads[2].wait()
            s_dn[...] = f_dn[...].astype(BF16)
            ag_a.near()
            ag_b.start()
            loads[3].wait()
            su_ref[...] = f_up[...].astype(BF16)
            ag_a.finish()
            fill = pltpu.make_async_copy(gin, w_vm, ld_sems.at[4])
            fill.start()
            fill.wait()

        pl.when(pl.program_id(0) == FWD_PROJ_PASS_AT)(ag_b.near)

        xb = x_ref[...]
        r = lax.rsqrt(jnp.mean(xb * xb, axis=-1, keepdims=True) + EPS)
        h = ((xb * r) * g_ref[...]).astype(BF16)
        h1_ref[...] = h
        p = _dot(h, w_vm[...])
        c2, s2 = _rope_block(inv_ref[...], pl.program_id(0) * TM)
        cos_ref[...], sin_ref[...] = c2, s2
        for hd in range(HEADS):
            sl = slice(hd * HEAD_DIM, (hd + 1) * HEAD_DIM)
            proj_ref[:, sl] = _rot(p[:, sl], c2, s2)
            ks = slice(RET_W + hd * HEAD_DIM, RET_W + (hd + 1) * HEAD_DIM)
            proj_ref[:, ks] = _rot(p[:, ks], c2, s2) * K_SCALE
        proj_ref[:, 2 * RET_W:] = p[:, 2 * RET_W:]

        pl.when(pl.program_id(0) == N_TB - 1)(ag_b.finish)

    tok = lambda w: pl.BlockSpec((TM, w), lambda i: (i, 0))
    hbm = pl.BlockSpec(memory_space=pl.ANY)
    vm = pl.BlockSpec(memory_space=pltpu.VMEM)
    return pl.pallas_call(
        body, name="fwd_proj", grid=(N_TB,),
        out_shape=(jax.ShapeDtypeStruct((SEQ, PROJ_W), F32), jax.ShapeDtypeStruct((SEQ, D_MODEL), BF16),
                   jax.ShapeDtypeStruct((SEQ, HEAD_DIM), F32), jax.ShapeDtypeStruct((SEQ, HEAD_DIM), F32))
        + _gathered_shapes(ids_a + ids_b) + (jax.ShapeDtypeStruct(SHARD[W_UP], BF16),),
        in_specs=[tok(D_MODEL), _resident((1, D_MODEL)), _resident((1, HEAD_DIM)), hbm, hbm, hbm, hbm, vm],
        out_specs=(tok(PROJ_W), tok(D_MODEL), tok(HEAD_DIM), tok(HEAD_DIM), hbm, hbm, hbm, hbm, vm),
        scratch_shapes=[pltpu.VMEM((D_MODEL, PROJ_W), BF16), pltpu.VMEM(SHARD[W_IN], BF16), pltpu.VMEM(SHARD[W_CONV], F32),
                        pltpu.VMEM(SHARD[W_OUT], BF16), pltpu.VMEM(SHARD[W_DOWN], BF16),
                        pltpu.VMEM(SHARD[W_IN], F32), pltpu.VMEM(SHARD[W_OUT], F32), pltpu.VMEM(SHARD[W_UP], F32),
                        pltpu.VMEM(SHARD[W_DOWN], F32), pltpu.SemaphoreType.DMA((5,))]
        + _gather_scratch(len(ids_a)) + _gather_scratch(len(ids_b)),
        compiler_params=_cparams(("arbitrary",), collective=COLLECTIVE["fwd_proj"]),
    )(x, g1, inv2, w_in, w_out, w_up, w_down, conv_w)


def _causal(w):
    r = lax.broadcasted_iota(jnp.int32, (CHUNK, CHUNK), 0)
    c = lax.broadcasted_iota(jnp.int32, (CHUNK, CHUNK), 1)
    return jnp.where(r >= c, w, 0.0)


def _fwd_mix(x, proj, wout_g, grn, lng, lnb, ws, bsb, su):
    cdec = _chunk_decay()
    ids = [W_UP]

    def body(x_ref, p_ref, w_ref, grn_ref, lng_ref, lnb_ref, ws_ref, bsb_ref, su_ref,
             x2_ref, cat_ref, o_ref, sp_ref, gup, state, m_ref, qd_ref, kd_ref, send_sems, recv_sems, local_sems):
        ag = _Gather(ids, [su_ref], [gup], send_sems, recv_sems, local_sems)

        @pl.when(pl.program_id(0) == 0)
        def _():
            meet = _Meet(diagonal=False)
            meet.signal()
            state[...] = jnp.zeros_like(state)
            _fill_decay(m_ref, qd_ref, kd_ref)
            meet.wait()
            ag.start()

        for h in range(HEADS):
            sl = slice(h * HEAD_DIM, (h + 1) * HEAD_DIM)
            q = p_ref[:, sl]
            k = p_ref[:, RET_W + h * HEAD_DIM:RET_W + (h + 1) * HEAD_DIM]
            v = p_ref[:, 2 * RET_W + h * HEAD_DIM:2 * RET_W + (h + 1) * HEAD_DIM]
            g = p_ref[:, 3 * RET_W + h * HEAD_DIM:3 * RET_W + (h + 1) * HEAD_DIM]
            qb, kb, vb = q.astype(BF16), k.astype(BF16), v.astype(BF16)
            a = _dot_nt(qb, kb) * m_ref[h]
            spb = state[h].astype(BF16)
            sp_ref[0, h] = spb
            o = _dot(a.astype(BF16), vb) + _dot((q * qd_ref[h]).astype(BF16), spb)
            state[h] = state[h] * cdec[h] + _dot_tn((k * kd_ref[h]).astype(BF16), vb)
            o_ref[:, sl] = o
            rinv = lax.rsqrt(jnp.mean(o * o, axis=-1, keepdims=True) + EPS)
            rn = (o * rinv) * grn_ref[:, sl]
            cat_ref[:, sl] = ((g * _sigmoid(g)) * rn).astype(BF16)
        for gi in range(HEADS):
            sl = slice(gi * HEAD_DIM, (gi + 1) * HEAD_DIM)
            u = p_ref[:, 4 * RET_W + gi * HEAD_DIM:4 * RET_W + (gi + 1) * HEAD_DIM]
            sv = p_ref[:, 4 * RET_W + SGU_W + gi * HEAD_DIM:4 * RET_W + SGU_W + (gi + 1) * HEAD_DIM]
            gv = _gelu(sv)
            xc = gv - jnp.mean(gv, axis=-1, keepdims=True)
            vn = (xc * lax.rsqrt(jnp.mean(xc * xc, axis=-1, keepdims=True) + EPS)) * lng_ref[:, sl] + lnb_ref[:, sl]
            mixed = _dot(_causal(ws_ref[gi]).astype(BF16), vn.astype(BF16)) + bsb_ref[gi]
            cat_ref[:, RET_W + gi * HEAD_DIM:RET_W + (gi + 1) * HEAD_DIM] = (_gelu(u) * mixed).astype(BF16)
        x2_ref[...] = x_ref[...] + _dot(cat_ref[...], w_ref[...])

        pl.when(pl.program_id(0) == FWD_MIX_PASS_AT)(ag.near)
        pl.when(pl.program_id(0) == N_CHUNK - 1)(ag.finish)

    ch = lambda w: pl.BlockSpec((CHUNK, w), lambda i: (i, 0))
    hcc = (HEADS, CHUNK, CHUNK)
    hbm = pl.BlockSpec(memory_space=pl.ANY)
    return pl.pallas_call(
        body, name="fwd_mix", grid=(N_CHUNK,),
        out_shape=(jax.ShapeDtypeStruct((SEQ, D_MODEL), F32), jax.ShapeDtypeStruct((SEQ, D_MODEL), BF16),
                   jax.ShapeDtypeStruct((SEQ, RET_W), F32), jax.ShapeDtypeStruct((N_CHUNK, HEADS, HEAD_DIM, HEAD_DIM), BF16))
        + _gathered_shapes(ids),
        in_specs=[ch(D_MODEL), ch(PROJ_W), _resident((D_MODEL, D_MODEL)), _resident((1, RET_W)), _resident((1, SGU_W)),
                  _resident((1, SGU_W)), _resident(hcc), _resident(hcc), hbm],
        out_specs=(ch(D_MODEL), ch(D_MODEL), ch(RET_W), pl.BlockSpec((1, HEADS, HEAD_DIM, HEAD_DIM), lambda i: (i, 0, 0, 0)), hbm),
        scratch_shapes=[pltpu.VMEM((HEADS, HEAD_DIM, HEAD_DIM), F32)] + [pltpu.VMEM(hcc, F32)] * 3 + _gather_scratch(len(ids)),
        compiler_params=_cparams(("arbitrary",), collective=COLLECTIVE["fwd_mix"]),
    )(x, proj, wout_g, grn, lng, lnb, ws, bsb, su)


def _conv_taps(p, prev8):
    row = lax.broadcasted_iota(jnp.int32, p.shape, 0)
    p1 = jnp.where(row == 0, prev8[7:8, :], pltpu.roll(p, 1, 0))
    p2 = jnp.where(row == 0, prev8[6:7, :], jnp.where(row == 1, prev8[7:8, :], pltpu.roll(p, 2, 0)))
    return p1, p2


def _fwd_ffn(x2, g2, wup_g, cw_g, cb_g, wdn_g, gf, tgt):
    def body(x_ref, g_ref, wu_ref, cw_ref, cb_ref, wd_ref, gf_ref, t_ref, h2_ref, up_ref, u_ref, act_ref, x3_ref, loss_ref, carry):
        @pl.when(pl.program_id(0) == 0)
        def _():
            carry[...] = jnp.zeros_like(carry)

        xb = x_ref[...]
        r = lax.rsqrt(jnp.mean(xb * xb, axis=-1, keepdims=True) + EPS)
        h = ((xb * r) * g_ref[...]).astype(BF16)
        h2_ref[...] = h
        acc = xb
        for t0, tw in FF_TILES:
            u = []
            for c0 in (t0, D_FF + t0):
                cs = slice(c0, c0 + tw)
                p = _dot_nt(h, wu_ref[pl.ds(c0, tw), :])
                up_ref[:, cs] = p.astype(BF16)
                p1, p2 = _conv_taps(p, carry[:, cs])
                carry[:, cs] = p[TM - 8:, :]
                us = p2 * cw_ref[0:1, cs] + p1 * cw_ref[1:2, cs] + p * cw_ref[2:3, cs] + cb_ref[:, cs]
                u_ref[:, cs] = us.astype(BF16)
                u.append(us)
            a = ((u[0] * _sigmoid(u[0])) * u[1]).astype(BF16)
            act_ref[:, t0:t0 + tw] = a
            acc = acc + _dot(a, wd_ref[pl.ds(t0, tw), :])
        x3_ref[...] = acc
        r3 = lax.rsqrt(jnp.mean(acc * acc, axis=-1, keepdims=True) + EPS)
        diff = (acc * r3) * gf_ref[...] - t_ref[...]
        loss_ref[...] = jnp.full(loss_ref.shape, 0.5 * jnp.sum(jnp.mean(diff * diff, axis=-1)), F32)

    tok = lambda w: pl.BlockSpec((TM, w), lambda i: (i, 0))
    return pl.pallas_call(
        body, name="fwd_ffn", grid=(N_TB,),
        out_shape=(jax.ShapeDtypeStruct((SEQ, D_MODEL), BF16), jax.ShapeDtypeStruct((SEQ, 2 * D_FF), BF16),
                   jax.ShapeDtypeStruct((SEQ, 2 * D_FF), BF16),
                   jax.ShapeDtypeStruct((SEQ, D_FF), BF16), jax.ShapeDtypeStruct((SEQ, D_MODEL), F32),
                   jax.ShapeDtypeStruct((N_TB, 8, 128), F32)),
        in_specs=[tok(D_MODEL), _resident((1, D_MODEL)), _resident((2 * D_FF, D_MODEL)), _resident((8, 2 * D_FF)),
                  _resident((1, 2 * D_FF)), _resident((D_FF, D_MODEL)), _resident((1, D_MODEL)), tok(D_MODEL)],
        out_specs=(tok(D_MODEL), tok(2 * D_FF), tok(2 * D_FF), tok(D_FF), tok(D_MODEL),
                   pl.BlockSpec((1, 8, 128), lambda i: (i, 0, 0))),
        scratch_shapes=[pltpu.VMEM((8, 2 * D_FF), F32)],
        compiler_params=_cparams(("arbitrary",)),
    )(x2, g2, wup_g, cw_g, cb_g, wdn_g, gf, tgt)


def _bwd_ffn(x3, tgt, gf, x2, g2, up_pre, u_conv, wup_g, cw_g, wdn_g):
    def body(x3_ref, t_ref, gf_ref, x2_ref, g2_ref, up_ref, u_ref, wu_ref, cw_ref, wd_ref,
             dx3_ref, dpre_ref, dx2_ref, dgf_ref, dg2_ref, dcv_ref, nxt):
        i = pl.program_id(0)

        @pl.when(i == 0)
        def _():
            nxt[...] = jnp.zeros_like(nxt)
            dgf_ref[...] = jnp.zeros_like(dgf_ref)
            dg2_ref[...] = jnp.zeros_like(dg2_ref)
            dcv_ref[...] = jnp.zeros_like(dcv_ref)

        x3 = x3_ref[...]
        r3 = lax.rsqrt(jnp.mean(x3 * x3, axis=-1, keepdims=True) + EPS)
        xh3 = x3 * r3
        dy = (xh3 * gf_ref[...] - t_ref[...]) * (1.0 / D_MODEL)
        dgf_ref[0:1, :] += jnp.sum(dy * xh3, axis=0, keepdims=True)
        t3 = dy * gf_ref[...]
        dx3 = r3 * (t3 - xh3 * jnp.mean(t3 * xh3, axis=-1, keepdims=True))
        dx3b = dx3.astype(BF16)
        dx3_ref[...] = dx3b
        dh2 = jnp.zeros((TM, D_MODEL), F32)
        for t0, tw in FF_TILES:
            row = lax.broadcasted_iota(jnp.int32, (TM, tw), 0)
            ts = slice(t0, t0 + tw)
            dact = _dot_nt(dx3b, wd_ref[pl.ds(t0, tw), :])
            ua = u_ref[:, ts].astype(F32)
            ub = u_ref[:, D_FF + t0:D_FF + t0 + tw].astype(F32)
            sg = _sigmoid(ua)
            du = [dact * ub * (sg * (1.0 + ua * (1.0 - sg))), dact * (ua * sg)]
            for n in range(2):
                d = du[n]
                c0 = n * D_FF + t0
                cs = slice(c0, c0 + tw)
                nx = nxt[:, cs]
                n1 = jnp.where(row == TM - 1, nx[0:1, :], pltpu.roll(d, TM - 1, 0))
                n2 = jnp.where(row == TM - 2, nx[0:1, :], jnp.where(row == TM - 1, nx[1:2, :], pltpu.roll(d, TM - 2, 0)))
                nxt[:, cs] = d[0:8, :]
                dp = (d * cw_ref[2:3, cs] + n1 * cw_ref[1:2, cs] + n2 * cw_ref[0:1, cs]).astype(BF16)
                dpre_ref[:, cs] = dp
                p = up_ref[:, cs].astype(F32)
                dcv_ref[n, 0:1, ts] += jnp.sum(n2 * p, axis=0, keepdims=True)
                dcv_ref[n, 1:2, ts] += jnp.sum(n1 * p, axis=0, keepdims=True)
                dcv_ref[n, 2:3, ts] += jnp.sum(d * p, axis=0, keepdims=True)
                dcv_ref[n, 3:4, ts] += jnp.sum(d, axis=0, keepdims=True)
                dh2 = dh2 + _dot(dp, wu_ref[pl.ds(c0, tw), :])
        x2 = x2_ref[...]
        r2 = lax.rsqrt(jnp.mean(x2 * x2, axis=-1, keepdims=True) + EPS)
        xh2 = x2 * r2
        dg2_ref[0:1, :] += jnp.sum(dh2 * xh2, axis=0, keepdims=True)
        t2 = dh2 * g2_ref[...]
        dx2_ref[...] = dx3 + r2 * (t2 - xh2 * jnp.mean(t2 * xh2, axis=-1, keepdims=True))

    rev = lambda w: pl.BlockSpec((TM, w), lambda i: (N_TB - 1 - i, 0))
    acc = lambda s: pl.BlockSpec(s, lambda i: (0,) * len(s))
    return pl.pallas_call(
        body, name="bwd_ffn", grid=(N_TB,),
        out_shape=(jax.ShapeDtypeStruct((SEQ, D_MODEL), BF16), jax.ShapeDtypeStruct((SEQ, 2 * D_FF), BF16),
                   jax.ShapeDtypeStruct((SEQ, D_MODEL), F32), jax.ShapeDtypeStruct((8, D_MODEL), F32),
                   jax.ShapeDtypeStruct((8, D_MODEL), F32), jax.ShapeDtypeStruct((2, 8, D_FF), F32)),
        in_specs=[rev(D_MODEL), rev(D_MODEL), _resident((1, D_MODEL)), rev(D_MODEL), _resident((1, D_MODEL)), rev(2 * D_FF),
                  rev(2 * D_FF), _resident((2 * D_FF, D_MODEL)), _resident((8, 2 * D_FF)), _resident((D_FF, D_MODEL))],
        out_specs=(rev(D_MODEL), rev(2 * D_FF), rev(D_MODEL), acc((8, D_MODEL)), acc((8, D_MODEL)), acc((2, 8, D_FF))),
        scratch_shapes=[pltpu.VMEM((8, 2 * D_FF), F32)],
        compiler_params=_cparams(("arbitrary",)),
    )(x3, tgt, gf, x2, g2, up_pre, u_conv, wup_g, cw_g, wdn_g)


def _bwd_mix(dx2, proj, o, sprev, wout_g, grn, lng, lnb, ws, bsb, cos2, sin2, hosted):
    cdec = _chunk_decay()
    geoms = [g for g, _ in hosted]
    n_h = len(hosted)

    def body(dx2_ref, p_ref, o_ref, sp_ref, w_ref, grn_ref, lng_ref, lnb_ref, ws_ref, bsb_ref, cos_ref, sin_ref, *rest):
        dp_ref, dgrn_ref, dlng_ref, dlnb_ref, dws_ref, dbs_ref = rest[n_h:n_h + 6]
        dstate, dbs_acc, m_ref, qd_ref, kd_ref = rest[2 * n_h + 6:2 * n_h + 11]
        i = pl.program_id(0)
        rs = _Scatters(geoms, rest[:n_h], rest[n_h + 6:2 * n_h + 6], rest[2 * n_h + 11:])
        pl.when(i == 0)(rs.phase1)
        pl.when(i == 3)(rs.phase2)
        pl.when(i == 8)(rs.phase2b)

        @pl.when(i == 0)
        def _():
            _fill_decay(m_ref, qd_ref, kd_ref)
            dstate[...] = jnp.zeros_like(dstate)
            dgrn_ref[...] = jnp.zeros_like(dgrn_ref)
            dlng_ref[...] = jnp.zeros_like(dlng_ref)
            dlnb_ref[...] = jnp.zeros_like(dlnb_ref)
            dws_ref[...] = jnp.zeros_like(dws_ref)
            dbs_ref[...] = jnp.zeros_like(dbs_ref)
            dbs_acc[...] = jnp.zeros_like(dbs_acc)

        dmix = _dot_nt(dx2_ref[...].astype(BF16), w_ref[...])
        for h in range(HEADS):
            sl = slice(h * HEAD_DIM, (h + 1) * HEAD_DIM)
            q = p_ref[:, sl]
            k = p_ref[:, RET_W + h * HEAD_DIM:RET_W + (h + 1) * HEAD_DIM]
            v = p_ref[:, 2 * RET_W + h * HEAD_DIM:2 * RET_W + (h + 1) * HEAD_DIM]
            g = p_ref[:, 3 * RET_W + h * HEAD_DIM:3 * RET_W + (h + 1) * HEAD_DIM]
            o = o_ref[:, sl]
            rinv = lax.rsqrt(jnp.mean(o * o, axis=-1, keepdims=True) + EPS)
            oh = o * rinv
            gr = grn_ref[:, sl]
            sg = _sigmoid(g)
            dret = dmix[:, sl]
            dp_ref[:, 3 * RET_W + h * HEAD_DIM:3 * RET_W + (h + 1) * HEAD_DIM] = (
                dret * (oh * gr) * (sg * (1.0 + g * (1.0 - sg)))).astype(BF16)
            drn = dret * (g * sg)
            dgrn_ref[0:1, sl] += jnp.sum(drn * oh, axis=0, keepdims=True)
            t = drn * gr
            do = rinv * (t - oh * jnp.mean(t * oh, axis=-1, keepdims=True))
            qb, kb, vb, dob = q.astype(BF16), k.astype(BF16), v.astype(BF16), do.astype(BF16)
            m = m_ref[h]
            ab = (_dot_nt(qb, kb) * m).astype(BF16)
            dab = (_dot_nt(dob, vb) * m).astype(BF16)
            spb = sp_ref[0, h]
            dsn = dstate[h]
            dsnb = dsn.astype(BF16)
            qdb = (q * qd_ref[h]).astype(BF16)
            kdb = (k * kd_ref[h]).astype(BF16)
            dq = _dot(dab, kb) + _dot_nt(dob, spb) * qd_ref[h]
            dk = _dot_tn(dab, qb) + _dot_nt(vb, dsnb) * kd_ref[h]
            dv = _dot_tn(ab, dob) + _dot(kdb, dsnb)
            dstate[h] = dsn * cdec[h] + _dot_tn(qdb, dob)
            c2, s2 = cos_ref[...], sin_ref[...]
            dp_ref[:, sl] = _rot_t(dq, c2, s2).astype(BF16)
            dp_ref[:, RET_W + h * HEAD_DIM:RET_W + (h + 1) * HEAD_DIM] = _rot_t(dk * K_SCALE, c2, s2).astype(BF16)
            dp_ref[:, 2 * RET_W + h * HEAD_DIM:2 * RET_W + (h + 1) * HEAD_DIM] = dv.astype(BF16)
        for gi in range(HEADS):
            sl = slice(gi * HEAD_DIM, (gi + 1) * HEAD_DIM)
            u = p_ref[:, 4 * RET_W + gi * HEAD_DIM:4 * RET_W + (gi + 1) * HEAD_DIM]
            sv = p_ref[:, 4 * RET_W + SGU_W + gi * HEAD_DIM:4 * RET_W + SGU_W + (gi + 1) * HEAD_DIM]
            gv = _gelu(sv)
            xc = gv - jnp.mean(gv, axis=-1, keepdims=True)
            rstd = lax.rsqrt(jnp.mean(xc * xc, axis=-1, keepdims=True) + EPS)
            xh = xc * rstd
            lg = lng_ref[:, sl]
            vnb = (xh * lg + lnb_ref[:, sl]).astype(BF16)
            wcb = _causal(ws_ref[gi]).astype(BF16)
            mixed = _dot(wcb, vnb) + bsb_ref[gi]
            dsgu = dmix[:, RET_W + gi * HEAD_DIM:RET_W + (gi + 1) * HEAD_DIM]
            dmixed = dsgu * _gelu(u)
            dmb = dmixed.astype(BF16)
            dws_ref[gi] += _causal(_dot_nt(dmb, vnb))
            dbs_acc[gi] += dmixed
            dvn = _dot_tn(wcb, dmb)
            dlng_ref[gi:gi + 1, :] += jnp.sum(dvn * xh, axis=0, keepdims=True)
            dlnb_ref[gi:gi + 1, :] += jnp.sum(dvn, axis=0, keepdims=True)
            dxh = dvn * lg
            dgv = rstd * (dxh - jnp.mean(dxh, axis=-1, keepdims=True) - xh * jnp.mean(dxh * xh, axis=-1, keepdims=True))
            dp_ref[:, 4 * RET_W + gi * HEAD_DIM:4 * RET_W + (gi + 1) * HEAD_DIM] = (dsgu * mixed * _gelu_grad(u)).astype(BF16)
            dp_ref[:, 4 * RET_W + SGU_W + gi * HEAD_DIM:4 * RET_W + SGU_W + (gi + 1) * HEAD_DIM] = (
                dgv * _gelu_grad(sv)).astype(BF16)

        @pl.when(i == N_CHUNK - 1)
        def _():
            for gi in range(HEADS):
                col = jnp.broadcast_to(jnp.sum(dbs_acc[gi], axis=-1, keepdims=True), (CHUNK, CHUNK))
                dbs_ref[gi:gi + 1, :] = jnp.transpose(col)[0:1, :]
            rs.phase3()

    rev = lambda w: pl.BlockSpec((CHUNK, w), lambda i: (N_CHUNK - 1 - i, 0))
    hcc = (HEADS, CHUNK, CHUNK)
    acc = lambda s: pl.BlockSpec(s, lambda i: (0,) * len(s))
    res = pl.pallas_call(
        body, name="bwd_mix", grid=(N_CHUNK,),
        out_shape=(jax.ShapeDtypeStruct((SEQ, PROJ_W), BF16), jax.ShapeDtypeStruct((8, RET_W), F32),
                   jax.ShapeDtypeStruct((8, HEAD_DIM), F32), jax.ShapeDtypeStruct((8, HEAD_DIM), F32),
                   jax.ShapeDtypeStruct(hcc, F32), jax.ShapeDtypeStruct((8, CHUNK), F32)) + _scatter_out_shapes(geoms),
        in_specs=[rev(D_MODEL), rev(PROJ_W), rev(RET_W),
                  pl.BlockSpec((1, HEADS, HEAD_DIM, HEAD_DIM), lambda i: (N_CHUNK - 1 - i, 0, 0, 0)),
                  _resident((D_MODEL, D_MODEL)), _resident((1, RET_W)), _resident((1, SGU_W)), _resident((1, SGU_W)),
                  _resident(hcc), _resident(hcc), rev(HEAD_DIM), rev(HEAD_DIM)]
        + [pl.BlockSpec(memory_space=pl.ANY)] * n_h,
        out_specs=(rev(PROJ_W), acc((8, RET_W)), acc((8, HEAD_DIM)), acc((8, HEAD_DIM)), acc(hcc), acc((8, CHUNK)))
        + _scatter_out_specs(geoms),
        scratch_shapes=[pltpu.VMEM((HEADS, HEAD_DIM, HEAD_DIM), F32), pltpu.VMEM((HEADS, CHUNK, CHUNK), F32)]
        + [pltpu.VMEM(hcc, F32)] * 3 + _scatter_scratch(geoms),
        compiler_params=_cparams(("arbitrary",), collective=COLLECTIVE["bwd_mix"]),
    )(dx2, proj, o, sprev, wout_g, grn, lng, lnb, ws, bsb, cos2, sin2, *[p for _, p in hosted])
    return tuple(res[:6 + n_h])


def _bwd_proj(dproj, win_g, x, g1, dx2, gin_p, small):
    geoms = [W_IN]
    n_s = len(small)

    def body(dp_ref, w_ref, x_ref, g_ref, dx2_ref, gin_ref, *rest):
        small_refs = rest[:n_s]
        dx_ref, rs_out, rp_ref, rws_ref, rcv_ref, dg_ref = rest[n_s:n_s + 6]
        rs_scratch = rest[n_s + 6:n_s + 6 + N_SCATTER_SCRATCH]
        ar_scratch = rest[n_s + 6 + N_SCATTER_SCRATCH:]
        ar_res = ar_scratch[N_SMALL_SCRATCH:]
        ar = _SmallReduce((dg_ref,) + tuple(small_refs), ar_res, ar_scratch[:N_SMALL_SCRATCH])
        rs = _Scatters(geoms, [gin_ref], [rs_out], rs_scratch)
        pl.when(pl.program_id(0) == 0)(lambda: rs.phase1(diagonal=True))
        pl.when(pl.program_id(0) == 1)(rs.phase2)
        pl.when(pl.program_id(0) == 5)(rs.phase2b)

        @pl.when(pl.program_id(0) == 0)
        def _():
            dg_ref[...] = jnp.zeros_like(dg_ref)

        dh = _dot_nt(dp_ref[...], w_ref[...])
        xb = x_ref[...]
        r = lax.rsqrt(jnp.mean(xb * xb, axis=-1, keepdims=True) + EPS)
        xh = xb * r
        dg_ref[0:1, :] += jnp.sum(dh * xh, axis=0, keepdims=True)
        t = dh * g_ref[...]
        dx_ref[...] = dx2_ref[...] + r * (t - xh * jnp.mean(t * xh, axis=-1, keepdims=True))

        @pl.when(pl.program_id(0) == N_TB - 1)
        def _():
            ar.begin()
            rs.phase3()
            ar.end()
            for o_ref, r_ref in zip((rp_ref, rws_ref, rcv_ref), ar_res):
                o_ref[...] = r_ref[...]

    tok = lambda w: pl.BlockSpec((TM, w), lambda i: (i, 0))
    vm = pl.BlockSpec(memory_space=pltpu.VMEM)
    res = pl.pallas_call(
        body, name="bwd_proj", grid=(N_TB,),
        out_shape=(jax.ShapeDtypeStruct((SEQ, D_MODEL), F32),) + _scatter_out_shapes(geoms)
        + tuple(jax.ShapeDtypeStruct(s, F32) for s in SMALL_FULL),
        in_specs=[tok(PROJ_W), _resident((D_MODEL, PROJ_W)), tok(D_MODEL), _resident((1, D_MODEL)), tok(D_MODEL),
                  pl.BlockSpec(memory_space=pl.ANY)] + [vm] * n_s,
        out_specs=(tok(D_MODEL),) + _scatter_out_specs(geoms) + (vm,) * len(SMALL_FULL),
        scratch_shapes=[pltpu.VMEM((8, D_MODEL), F32)] + _scatter_scratch(geoms) + _small_scratch()
        + [pltpu.VMEM(s, F32) for s in SMALL_FULL],
        compiler_params=_cparams(("arbitrary",), collective=COLLECTIVE["bwd_proj"]),
    )(dproj, win_g, x, g1, dx2, gin_p, *small)
    return res


def _wgrad(name, a, b, tm=None, tn=None, hosted=()):
    m_w, n_w = a.shape[-1], b.shape[-1]
    tm = m_w if tm is None else tm
    tn = n_w if tn is None else tn
    n_steps = (m_w // tm) * (n_w // tn)
    geoms = [g for g, _ in hosted]
    n_h = len(hosted)

    def body(a_ref, b_ref, *rest):
        o_ref = rest[n_h]
        if n_h:
            rs = _Scatters(geoms, rest[:n_h], rest[n_h + 1:2 * n_h + 1], rest[2 * n_h + 1:])
            step = pl.program_id(0) * (n_w // tn) + pl.program_id(1)
            pl.when(step == 0)(rs.phase1)
            pl.when(step == 1)(rs.phase2)
            pl.when(step == n_steps // 2)(rs.phase2b)
        o_ref[...] = _dot_tn(a_ref[...].astype(BF16), b_ref[...].astype(BF16)).astype(BF16)
        if n_h:
            pl.when(step == n_steps - 1)(rs.phase3)

    assert not n_h or n_steps >= 4
    res = pl.pallas_call(
        body, name=name, grid=(m_w // tm, n_w // tn),
        out_shape=(jax.ShapeDtypeStruct((m_w, n_w), BF16),) + _scatter_out_shapes(geoms),
        in_specs=[pl.BlockSpec((SEQ, tm), lambda i, j: (0, i)), pl.BlockSpec((SEQ, tn), lambda i, j: (0, j))]
        + [pl.BlockSpec(memory_space=pl.ANY)] * n_h,
        out_specs=(pl.BlockSpec((tm, tn), lambda i, j: (i, j)),) + _scatter_out_specs(geoms),
        scratch_shapes=_scatter_scratch(geoms),
        compiler_params=_cparams(("arbitrary", "arbitrary"), collective=COLLECTIVE[name]) if n_h else _cparams(("parallel", "parallel")),
    )(a, b, *[p for _, p in hosted])
    return tuple(res[:1 + n_h])


def _row_step(half_rows):
    return max(s for s in range(16, 177, 16) if half_rows % s == 0)


class _Scatter:
    def __init__(self, geom, partial, out, land1, mine, stage2, land2, comb, s1_send, s1_recv, s2_send, s2_recv, ld_sems):
        self.w, self.row0, self.shape = _geom(geom)
        self.partial, self.out, self.land1 = partial, out, land1
        self.mine, self.stage2, self.land2, self.comb = mine, stage2, land2, comb
        self.hr = self.shape[0] // 2
        self.step = _row_step(self.hr)
        self.s1_send, self.s1_recv, self.s2_send, self.s2_recv, self.ld_sems = s1_send, s1_recv, s2_send, s2_recv, ld_sems
        self.x, self.y, self.c = lax.axis_index("x"), lax.axis_index("y"), lax.axis_index("c")
        self.sibling = (self.x, self.y, 1 - self.c)
        self.chips = [(self.x, self.y), (1 - self.x, self.y), (self.x, 1 - self.y), (1 - self.x, 1 - self.y)]

    def block(self, px, py, pc):
        dev = 4 * px + 2 * py + pc
        if self.w == W_IN:
            return self.partial.at[:, pl.ds(pl.multiple_of(dev * IN_SHARD, 128), IN_SHARD)]
        if self.w == W_OUT:
            return self.partial.at[pl.ds(pl.multiple_of(dev * OUT_SHARD, 128), OUT_SHARD), :]
        if self.w == W_DOWN:
            return self.partial.at[pl.ds(pl.multiple_of(dev * DOWN_SHARD, 32), DOWN_SHARD), :]
        return self.partial.at[pl.ds(pl.multiple_of(dev * FF_SHARD + self.row0, 32), self.shape[0]), :]

    def copy1(self, k):
        return pltpu.make_async_remote_copy(
            src_ref=self.block(*self.chips[k], 1 - self.c), dst_ref=self.land1.at[k],
            send_sem=self.s1_send.at[k], recv_sem=self.s1_recv.at[k], device_id=self.sibling, device_id_type=MESH)

    STAGE2 = [(1, 0, 1), (3, 0, 1), (2, 1, 2), (3, 1, 2), (1, 1, 1), (2, 0, 2)]

    def copy2(self, j):
        blk, h, to = self.STAGE2[j]
        src = self.comb.at[j - 4] if j >= 4 else self.stage2.at[blk - 1, pl.ds(h * self.hr, self.hr), :]
        return pltpu.make_async_remote_copy(
            src_ref=src, dst_ref=self.land2.at[j], send_sem=self.s2_send.at[j], recv_sem=self.s2_recv.at[j],
            device_id=(*self.chips[to], self.c), device_id_type=MESH)

    def _rows(self, h=None):
        step = self.step
        lo, n = (0, self.shape[0]) if h is None else (h * self.hr, self.hr)
        return [pl.ds(r0, step) for r0 in range(lo, lo + n, step)]

    def load(self, k):
        return pltpu.make_async_copy(self.block(*self.chips[k], self.c), self.mine.at[k], self.ld_sems.at[k])

    def load_mine(self):
        for k in range(4):
            self.load(k).start()

    def phase1(self):
        for k in range(4):
            self.copy1(k).start()

    def phase2(self, k):
        self.copy1(k).wait_recv()
        self.load(k).wait()
        for rs in self._rows():
            s = self.mine[k, rs, :].astype(F32) + self.land1[k, rs, :].astype(F32)
            if k == 0:
                self.out[rs, :] = s
            else:
                self.stage2[k - 1, rs, :] = s.astype(BF16)
        for j in {3: (1, 3), 1: (0,), 2: (2,), 0: ()}[k]:
            self.copy2(j).start()

    def phase2b(self):
        for j, got in ((4, 3), (5, 1)):
            blk, h, _ = self.STAGE2[j]
            self.copy2(got).wait_recv()
            for i, rs in enumerate(self._rows(h)):
                lr = pl.ds(i * self.step, self.step)
                self.comb[j - 4, lr, :] = (self.stage2[blk - 1, rs, :].astype(F32) + self.land2[got, lr, :].astype(F32)).astype(BF16)
            self.copy2(j).start()

    def phase3(self):
        for j in (0, 5, 4, 2):
            self.copy2(j).wait_recv()
        for h, (first, second) in enumerate(((0, 5), (4, 2))):
            for i, rs in enumerate(self._rows(h)):
                lr = pl.ds(i * self.step, self.step)
                self.out[rs, :] = (self.out[rs, :] + self.land2[first, lr, :].astype(F32)) + self.land2[second, lr, :].astype(F32)
        for k in range(4):
            self.copy1(k).wait_send()
        for j in range(6):
            self.copy2(j).wait_send()


def _geom(geom):
    if isinstance(geom, tuple):
        w, row0, rows = geom
        assert w == W_UP
        return w, row0, (rows, SHARD[w][1])
    return geom, 0, SHARD[geom]


N_SCATTER_SCRATCH = 10


def _scatter_out_shapes(geoms):
    return tuple(jax.ShapeDtypeStruct(_geom(g)[2], F32) for g in geoms)


def _scatter_out_specs(geoms):
    return (pl.BlockSpec(memory_space=pltpu.VMEM),) * len(geoms)


def _scatter_scratch(geoms):
    out = []
    for g in geoms:
        s = _geom(g)[2]
        hs = (s[0] // 2, s[1])
        out += [pltpu.VMEM((4,) + s, BF16), pltpu.VMEM((4,) + s, BF16), pltpu.VMEM((3,) + s, BF16), pltpu.VMEM((6,) + hs, BF16),
                pltpu.VMEM((2,) + hs, BF16),
                pltpu.SemaphoreType.DMA((4,)), pltpu.SemaphoreType.DMA((4,)), pltpu.SemaphoreType.DMA((6,)),
                pltpu.SemaphoreType.DMA((6,)), pltpu.SemaphoreType.DMA((4,))]
    return out


class _Scatters:
    def __init__(self, geoms, p_refs, out_refs, scratch):
        k = N_SCATTER_SCRATCH
        self.items = [_Scatter(g, p_refs[i], out_refs[i], *scratch[k * i:k * i + k]) for i, g in enumerate(geoms)]

    def phase1(self, diagonal=False):
        meet = _Meet(diagonal)
        meet.signal()
        for s in self.items:
            s.load_mine()
        meet.wait()
        for s in self.items:
            s.phase1()

    def phase2(self):
        for k in (3, 1, 2, 0):
            for s in self.items:
                s.phase2(k)

    def phase2b(self):
        for s in self.items:
            s.phase2b()

    def phase3(self):
        for s in self.items:
            s.phase3()


PACK_W = 1024


SMALL_FULL = [(2, 8, PACK_W), (HEADS, CHUNK, CHUNK), (2, 8, D_FF)]
SMALL_HALF = [(s[0] // 2,) + s[1:] for s in SMALL_FULL]
N_SMALL_SCRATCH = 16


def _small_scratch():
    n_a = len(SMALL_FULL)
    return ([pltpu.VMEM(SMALL_FULL[0], F32)] + [pltpu.VMEM(s, F32) for s in SMALL_HALF] + [pltpu.VMEM(s, F32) for s in SMALL_HALF]
            + [pltpu.VMEM((3,) + s, F32) for s in SMALL_HALF]
            + [pltpu.SemaphoreType.DMA((n_a,)), pltpu.SemaphoreType.DMA((n_a,)), pltpu.SemaphoreType.DMA((n_a, 3)),
               pltpu.SemaphoreType.DMA((n_a, 3)), pltpu.SemaphoreType.DMA((n_a,)), pltpu.SemaphoreType.DMA((n_a,))])


class _SmallReduce:
    def __init__(self, ins, outs, scratch):
        self.ins, self.outs = ins, outs
        (self.pack, *rest) = scratch
        self.rxs, self.css, self.gs = rest[0:3], rest[3:6], rest[6:9]
        self.s1_send, self.s1_recv, self.s2_send, self.s2_recv, self.s3_send, self.s3_recv = rest[9:]
        self.x, self.y, self.c = lax.axis_index("x"), lax.axis_index("y"), lax.axis_index("c")
        self.sibling = (self.x, self.y, 1 - self.c)
        self.chips = [(1 - self.x, self.y), (self.x, 1 - self.y), (1 - self.x, 1 - self.y)]
        self.hl = [s[0] for s in SMALL_HALF]

    def half(self, ref, a, h):
        return ref.at[pl.ds(h * self.hl[a], self.hl[a])]

    def begin(self):
        dg1_ref, dg2_ref, dgf_ref, dgrn_ref, dlng_ref, dlnb_ref, dbs_ref, loss_ref, dws_ref, dcv_ref = self.ins
        pack, c = self.pack, self.c
        pack[...] = jnp.zeros_like(pack)
        pack[0, 0:1, :] = dg1_ref[0:1, :]
        pack[0, 1:2, :] = dg2_ref[0:1, :]
        pack[0, 2:3, :] = dgf_ref[0:1, :]
        pack[0, 3:4, 0:RET_W] = dgrn_ref[0:1, :]
        lsum = loss_ref[0, 0:1, :]
        for i in range(1, N_TB):
            lsum = lsum + loss_ref[i, 0:1, :]
        pack[0, 3:4, RET_W:RET_W + 128] = lsum
        pack[1, 0:HEADS, 0:128] = dlng_ref[0:HEADS, :]
        pack[1, 0:HEADS, 128:256] = dlnb_ref[0:HEADS, :]
        pack[1, 0:HEADS, 256:384] = dbs_ref[0:HEADS, :]
        self.srcs = [pack, dws_ref, dcv_ref]
        n_a = len(self.srcs)
        self.ex1 = [pltpu.make_async_remote_copy(src_ref=self.half(self.srcs[a], a, 1 - c), dst_ref=self.rxs[a],
                                                 send_sem=self.s1_send.at[a], recv_sem=self.s1_recv.at[a],
                                                 device_id=self.sibling, device_id_type=MESH) for a in range(n_a)]
        for cp in self.ex1:
            cp.start()
        self.ex2 = []
        for a in range(n_a):
            self.ex1[a].wait_recv()
            self.css[a][...] = self.half(self.srcs[a], a, c)[...] + self.rxs[a][...]
            for j, chip in enumerate(self.chips):
                cp = pltpu.make_async_remote_copy(src_ref=self.css[a], dst_ref=self.gs[a].at[j], send_sem=self.s2_send.at[a, j],
                                                  recv_sem=self.s2_recv.at[a, j], device_id=(*chip, c), device_id_type=MESH)
                cp.start()
                self.ex2.append(cp)

    def end(self):
        c, x, y = self.c, self.x, self.y
        ex3 = []
        for a in range(len(self.srcs)):
            css, gs, out = self.css[a], self.gs[a], self.outs[a]
            for j in range(3):
                self.ex2[3 * a + j].wait_recv()
            tot = None
            for q in range(4):
                k = jnp.where(x != (q >> 1), 1, 0) + jnp.where(y != (q & 1), 2, 0)
                term = jnp.where(k == 0, css[...], jnp.where(k == 1, gs[0], jnp.where(k == 2, gs[1], gs[2])))
                tot = term if tot is None else tot + term
            self.half(out, a, c)[...] = tot
            cp = pltpu.make_async_remote_copy(src_ref=self.half(out, a, c), dst_ref=self.half(out, a, c), send_sem=self.s3_send.at[a],
                                              recv_sem=self.s3_recv.at[a], device_id=self.sibling, device_id_type=MESH)
            cp.start()
            ex3.append(cp)
        for a in range(len(self.srcs)):
            out = self.outs[a]
            pltpu.make_async_remote_copy(src_ref=self.half(out, a, 1 - c), dst_ref=self.half(out, a, 1 - c), send_sem=self.s3_send.at[a],
                                         recv_sem=self.s3_recv.at[a], device_id=self.sibling, device_id_type=MESH).wait_recv()
        for cp in self.ex1 + self.ex2 + ex3:
            cp.wait_send()


def _adam_math(w, g, m, v):
    nm = ADAM_B1 * m + (1.0 - ADAM_B1) * g
    nv = ADAM_B2 * v + (1.0 - ADAM_B2) * (g * g)
    d = -ADAM_LR * ((nm / (1.0 - ADAM_B1 ** ADAM_STEP)) / (jnp.sqrt(nv / (1.0 - ADAM_B2 ** ADAM_STEP)) + ADAM_EPS) + ADAM_WD * w)
    return d, nm, nv


def _adamw(params, thru, n_steps):
    plan = []
    for w, gs, _, _ in params:
        _, r, cdim = w.shape
        if len(gs) == 1:
            edges = [0, r // n_steps]
            spec3 = pl.BlockSpec((1, r // n_steps, cdim), lambda i: (0, i, 0))
            g_specs = [pl.BlockSpec((r // n_steps, cdim), lambda i: (i, 0))]
        else:
            edges = [sum(g.shape[0] for g in gs[:k]) for k in range(len(gs) + 1)]
            spec3 = pl.BlockSpec((1, r, cdim // n_steps), lambda i: (0, 0, i))
            g_specs = [pl.BlockSpec((g.shape[0], cdim // n_steps), lambda i: (0, i)) for g in gs]
        plan.append((len(gs), edges, spec3, g_specs))
    n_in = sum(n_g + 3 for n_g, _, _, _ in plan)
    n_t = len(thru)

    def body(*refs):
        ins, outs = refs[:n_in], refs[n_in + n_t:]
        for n_g, edges, _, _ in plan:
            (w_ref, *g_refs, m_ref, v_ref), ins = ins[:n_g + 3], ins[n_g + 3:]
            (go_ref, d_ref, nm_ref, nv_ref), outs = outs[:4], outs[4:]
            for g_ref, lo, hi in zip(g_refs, edges[:-1], edges[1:]):
                gg = g_ref[...]
                go_ref[0, lo:hi, :] = gg
                d_ref[0, lo:hi, :], nm_ref[0, lo:hi, :], nv_ref[0, lo:hi, :] = _adam_math(
                    w_ref[0, lo:hi, :], gg, m_ref[0, lo:hi, :], v_ref[0, lo:hi, :])
        for t_ref, to_ref in zip(refs[n_in:n_in + n_t], outs):
            to_ref[...] = t_ref[...]

    t_specs = [pl.BlockSpec((t.shape[0] // n_steps, t.shape[1]), lambda i: (i, 0)) for t in thru]
    in_specs, out_specs, out_shape, args = [], [], [], []
    for (w, gs, m, v), (_, _, spec3, g_specs) in zip(params, plan):
        in_specs += [spec3] + g_specs + [spec3, spec3]
        out_specs += [spec3] * 4
        out_shape += [jax.ShapeDtypeStruct(w.shape, F32)] * 4
        args += [w, *gs, m, v]
    res = pl.pallas_call(
        body, name="adamw", grid=(n_steps,), out_shape=tuple(out_shape) + tuple(jax.ShapeDtypeStruct(t.shape, t.dtype) for t in thru),
        in_specs=in_specs + t_specs, out_specs=tuple(out_specs) + tuple(t_specs),
        compiler_params=_cparams(("parallel",)),
    )(*args, *thru)
    return [res[4 * k:4 * k + 4] for k in range(len(params))], res[4 * len(params):]


def _adamw_small(rp, rws, rcv, gcw, params):
    n_p = len(params)

    def body(*refs):
        rp_ref, rws_ref, rcv_ref, gcw_ref = refs[:4]
        ins = refs[4:4 + 3 * n_p]
        outs = refs[4 + 3 * n_p:]
        outs[4 * n_p][...] = rp_ref[0, 3:4, RET_W:RET_W + 1]
        grads = [rp_ref[0, 0:1, :], rp_ref[0, 1:2, :], rp_ref[0, 2:3, :], rp_ref[0, 3:4, 0:RET_W],
                 rp_ref[1, 0:HEADS, 0:128], rp_ref[1, 0:HEADS, 128:256], rp_ref[1, 0:HEADS, 256:384],
                 rws_ref[...], gcw_ref[...], None]
        for p in range(n_p):
            w_ref, m_ref, v_ref = ins[3 * p:3 * p + 3]
            o = outs[4 * p:4 * p + 4]
            if p == n_p - 1:
                for hf in range(2):
                    cs = slice(hf * D_FF, (hf + 1) * D_FF)
                    g = rcv_ref[hf, 3:4, :]
                    res = (g,) + _adam_math(w_ref[:, cs], g, m_ref[:, cs], v_ref[:, cs])
                    for t in range(4):
                        o[t][:, cs] = res[t]
                continue
            lead = w_ref.ndim > grads[p].ndim
            rd = (lambda r: r[0]) if lead else (lambda r: r[...])
            res = (grads[p],) + _adam_math(rd(w_ref), grads[p], rd(m_ref), rd(v_ref))
            for t in range(4):
                if lead:
                    o[t][0] = res[t]
                else:
                    o[t][...] = res[t]

    vm = pl.BlockSpec(memory_space=pltpu.VMEM)
    flat = [a for tr in params for a in tr]
    out_shape = tuple(jax.ShapeDtypeStruct(tr[0].shape, F32) for tr in params for _ in range(4)) + (jax.ShapeDtypeStruct((1, 1), F32),)
    res = pl.pallas_call(
        body, name="adamw_small", out_shape=out_shape, in_specs=[vm] * (4 + len(flat)), out_specs=(vm,) * len(out_shape),
        compiler_params=_cparams(),
    )(rp, rws, rcv, gcw, *flat)
    return [res[4 * p:4 * p + 4] for p in range(n_p)], res[4 * n_p]


def kernel(x, mix_norm_g, w_in, ret_norm_g, sgu_ln_g, sgu_ln_b, sgu_w_s, sgu_b_s, w_out, ffn_norm_g, w_up, conv_w, conv_b, w_down, final_norm_g, loss_target, m_mix_norm_g, m_w_in, m_ret_norm_g, m_sgu_ln_g, m_sgu_ln_b, m_sgu_w_s, m_sgu_b_s, m_w_out, m_ffn_norm_g, m_w_up, m_conv_w, m_conv_b, m_w_down, m_final_norm_g, v_mix_norm_g, v_w_in, v_ret_norm_g, v_sgu_ln_g, v_sgu_ln_b, v_sgu_w_s, v_sgu_b_s, v_w_out, v_ffn_norm_g, v_w_up, v_conv_w, v_conv_b, v_w_down, v_final_norm_g):
    xs = x[0]
    tgt = loss_target[0]
    grn = ret_norm_g.reshape(1, RET_W)
    lng = sgu_ln_g.reshape(1, SGU_W)
    lnb = sgu_ln_b.reshape(1, SGU_W)
    ws = sgu_w_s[0]
    bsb = jnp.broadcast_to(sgu_b_s[0][:, :, None], (HEADS, CHUNK, HEAD_DIM))
    gf = final_norm_g.reshape(1, D_MODEL)
    me = 4 * lax.axis_index("x") + 2 * lax.axis_index("y") + lax.axis_index("c")
    tr = lambda a: jnp.transpose(a[0])[None]
    tr_cw = lambda a: jnp.transpose(a, (1, 0, 2))

    proj, h1, cos2, sin2, win_g, cw_sh, wout_g, wdn_g, su = _fwd_proj(
        xs, mix_norm_g, _rope_freq(), w_in[0], w_out[0], tr(w_up)[0], w_down[0], tr_cw(conv_w))
    cw_g = jnp.transpose(cw_sh, (1, 0, 2)).reshape(8, 2 * D_FF)
    x2, mixcat, o, sprev, wup_g = _fwd_mix(xs, proj, wout_g, grn, lng, lnb, ws, bsb, su)
    h2, up_pre, u_conv, act, x3, loss_parts = _fwd_ffn(x2, ffn_norm_g, wup_g, cw_g, conv_b, wdn_g, gf, tgt)

    dx3, dpre, dx2, dgf, dg2, dcv = _bwd_ffn(x3, tgt, gf, x2, ffn_norm_g, up_pre, u_conv, wup_g, cw_g, wdn_g)
    band = 512
    (gout_p,) = _wgrad("wgrad_out", mixcat, dx2, tn=512)
    gdn_p, g_out = _wgrad("wgrad_down", act, dx3, tm=FF_TILE, tn=512, hosted=[(W_OUT, gout_p)])
    gup_p, g_dn = _wgrad("wgrad_up", dpre, h2, tm=FF_TILE, tn=512, hosted=[(W_DOWN, gdn_p)])
    dproj, dgrn, dlng, dlnb, dws, dbs, g_up_a = _bwd_mix(
        dx2, proj, o, sprev, wout_g, grn, lng, lnb, ws, bsb, cos2, sin2, [((W_UP, 0, band), gup_p)])
    gin_p, g_up_b = _wgrad("wgrad_in", h1, dproj, tm=512, tn=768, hosted=[((W_UP, band, FF_SHARD - band), gup_p)])
    grad_x, g_in, rp, rws, rcv = _bwd_proj(dproj, win_g, xs, mix_norm_g, dx2, gin_p,
                                           (dg2, dgf, dgrn, dlng, dlnb, dbs, loss_parts, dws, dcv))
    gcw = tr_cw(lax.dynamic_slice(rcv, (me // (N_DEV // 2), 0, (me % (N_DEV // 2)) * FF_SHARD), (1, 3, FF_SHARD)))

    table = {}
    big, (grad_x,) = _adamw([(w_in, [g_in], m_w_in, v_w_in), (w_out, [g_out], m_w_out, v_w_out),
                             (tr(w_up), [g_up_a, g_up_b], tr(m_w_up), tr(v_w_up)), (w_down, [g_dn], m_w_down, v_w_down)],
                            [grad_x], n_steps=4)
    table.update(zip(("w_in", "w_out", "w_up", "w_down"), big))
    table["w_up"] = tuple(tr(a) for a in table["w_up"])
    row = lambda a: a.reshape(1, D_MODEL)
    names_small = ["mix_norm_g", "ffn_norm_g", "final_norm_g", "ret_norm_g", "sgu_ln_g", "sgu_ln_b", "sgu_b_s", "sgu_w_s",
                   "conv_w", "conv_b"]
    params = [(mix_norm_g, m_mix_norm_g, v_mix_norm_g), (ffn_norm_g, m_ffn_norm_g, v_ffn_norm_g),
              (row(final_norm_g), row(m_final_norm_g), row(v_final_norm_g)), (ret_norm_g, m_ret_norm_g, v_ret_norm_g),
              (sgu_ln_g, m_sgu_ln_g, v_sgu_ln_g), (sgu_ln_b, m_sgu_ln_b, v_sgu_ln_b), (sgu_b_s, m_sgu_b_s, v_sgu_b_s),
              (sgu_w_s, m_sgu_w_s, v_sgu_w_s), (tr_cw(conv_w), tr_cw(m_conv_w), tr_cw(v_conv_w)), (conv_b, m_conv_b, v_conv_b)]
    small, loss = _adamw_small(rp, rws, rcv, gcw, params)
    for n, res in zip(names_small, small):
        table[n] = res
    table["final_norm_g"] = tuple(a.reshape(D_MODEL) for a in table["final_norm_g"])
    table["conv_w"] = tuple(tr_cw(a) for a in table["conv_w"])

    order = ["mix_norm_g", "w_in", "ret_norm_g", "sgu_ln_g", "sgu_ln_b", "sgu_w_s", "sgu_b_s", "w_out", "ffn_norm_g", "w_up",
             "conv_w", "conv_b", "w_down", "final_norm_g"]
    outs = [loss.reshape(()), grad_x[None]]
    for col in range(4):
        outs += [table[n][col] for n in order]
    return tuple(outs)
```

```python
import functools
import math

import jax
import jax.numpy as jnp
import numpy as np
from jax import lax
from jax.experimental import pallas as pl
from jax.experimental.pallas import tpu as pltpu

F32 = jnp.float32
BF16 = jnp.bfloat16
MESH = pl.DeviceIdType.MESH

N_DEV = 8
SEQ = 2048
D_MODEL = 1024
CHUNK = 128
N_CHUNK = SEQ // CHUNK
HEADS = 4
HEAD_DIM = 128
RET_W = 512
SGU_W = 512
PROJ_W = 3072
D_FF = 2816
FF_SHARD = 704
FF_TILE = 1408
FF_TILES = ((0, D_FF),)
IN_SHARD = PROJ_W // N_DEV
OUT_SHARD = D_MODEL // N_DEV
DOWN_SHARD = D_FF // N_DEV
TM = 256
N_TB = SEQ // TM
FWD_PROJ_PASS_AT = 5
FWD_MIX_PASS_AT = 10
EPS = 1e-6
ROPE_BASE = 10000.0
K_SCALE = HEAD_DIM ** -0.5
INV_SQRT2 = 0.7071067811865476
INV_SQRT_2PI = 0.3989422804014327

ADAM_LR = 0.001
ADAM_B1 = 0.9
ADAM_B2 = 0.999
ADAM_EPS = 1e-08
ADAM_WD = 0.01
ADAM_STEP = 10

VMEM_LIMIT = 56 * 1024 * 1024


def _cparams(sem=None, vmem=VMEM_LIMIT, collective=None):
    return pltpu.CompilerParams(dimension_semantics=sem, vmem_limit_bytes=vmem, collective_id=collective)


COLLECTIVE = {name: k for k, name in enumerate(("fwd_proj", "fwd_mix", "wgrad_down", "wgrad_up", "bwd_mix", "wgrad_in", "bwd_proj"))}


class _Meet:
    def __init__(self, diagonal):
        x, y, c = lax.axis_index("x"), lax.axis_index("y"), lax.axis_index("c")
        self.peers = [(x, y, 1 - c), (1 - x, y, c), (x, 1 - y, c)] + ([(1 - x, 1 - y, c)] if diagonal else [])

    def signal(self):
        for peer in self.peers:
            pl.semaphore_signal(pltpu.get_barrier_semaphore(), inc=1, device_id=peer, device_id_type=MESH)

    def wait(self):
        pl.semaphore_wait(pltpu.get_barrier_semaphore(), len(self.peers))


def _resident(shape):
    nd = len(shape)
    return pl.BlockSpec(shape, lambda *_: (0,) * nd, pipeline_mode=pl.Buffered(1))


def _dot(a, b):
    return jnp.dot(a, b, preferred_element_type=F32)


def _dot_nt(a, b):
    return lax.dot_general(a, b, (((1,), (1,)), ((), ())), preferred_element_type=F32)


def _dot_tn(a, b):
    return lax.dot_general(a, b, (((0,), (0,)), ((), ())), preferred_element_type=F32)


def _sigmoid(x):
    return 1.0 / (1.0 + jnp.exp(-x))


def _gelu(x):
    return 0.5 * x * (1.0 + lax.erf(x * INV_SQRT2))


def _gelu_grad(x):
    return 0.5 * (1.0 + lax.erf(x * INV_SQRT2)) + x * (jnp.exp(-0.5 * x * x) * INV_SQRT_2PI)


def _rot(xh, cos2, sin2):
    return xh * cos2 + pltpu.roll(xh, HEAD_DIM // 2, 1) * sin2


def _rot_t(dh, cos2, sin2):
    return dh * cos2 + pltpu.roll(dh * sin2, HEAD_DIM // 2, 1)


def _rope_freq():
    half = HEAD_DIM // 2
    inv_freq = jnp.power(ROPE_BASE, -jnp.arange(half, dtype=F32) / half)
    return jnp.concatenate([inv_freq, inv_freq])[None, :]


def _rope_block(inv2, first_row):
    pos = (lax.broadcasted_iota(jnp.int32, (TM, HEAD_DIM), 0) + first_row).astype(F32)
    ang = pos * inv2
    sin = jnp.sin(ang)
    lane = lax.broadcasted_iota(jnp.int32, (TM, HEAD_DIM), 1)
    return jnp.cos(ang), jnp.where(lane < HEAD_DIM // 2, -sin, sin)


def _log_gamma():
    return np.log(np.float32(1.0) - np.power(np.float32(2.0), -5.0 - np.arange(HEADS, dtype=np.float32))).astype(np.float32)


def _fill_decay(mask_ref, qd_ref, kd_ref):
    assert HEAD_DIM == CHUNK
    lg = _log_gamma()
    t = lax.broadcasted_iota(jnp.int32, (CHUNK, CHUNK), 0).astype(F32)
    diff = t - lax.broadcasted_iota(jnp.int32, (CHUNK, CHUNK), 1).astype(F32)
    for h in range(HEADS):
        mask_ref[h] = jnp.where(diff >= 0.0, jnp.exp(float(lg[h]) * jnp.maximum(diff, 0.0)), 0.0)
        qd_ref[h] = jnp.exp(float(lg[h]) * (t + 1.0))
        kd_ref[h] = jnp.exp(float(lg[h]) * (CHUNK - 1.0 - t))


def _chunk_decay():
    lg = _log_gamma()
    return [float(np.exp(lg[h] * np.float32(CHUNK))) for h in range(HEADS)]


W_IN, W_OUT, W_UP, W_DOWN, W_CONV = range(5)
GATHERED = {W_IN: ((D_MODEL, PROJ_W), BF16), W_OUT: ((D_MODEL, D_MODEL), BF16), W_UP: ((2 * D_FF, D_MODEL), BF16),
            W_DOWN: ((D_FF, D_MODEL), BF16), W_CONV: ((N_DEV, 8, FF_SHARD), F32)}
SHARD = {W_IN: (D_MODEL, IN_SHARD), W_OUT: (OUT_SHARD, D_MODEL), W_UP: (FF_SHARD, D_MODEL), W_DOWN: (DOWN_SHARD, D_MODEL),
         W_CONV: (8, FF_SHARD)}


class _Gather:
    N_SEMS = 9

    def __init__(self, ids, stages, gathered, send_sems, recv_sems, local_sems):
        self.ids, self.stages, self.gathered = ids, stages, gathered
        self.send_sems, self.recv_sems, self.local_sems = send_sems, recv_sems, local_sems
        self.x, self.y, self.c = lax.axis_index("x"), lax.axis_index("y"), lax.axis_index("c")
        self.me = (self.x, self.y, self.c)
        self.sibling = (self.x, self.y, 1 - self.c)
        self.chips = [(1 - self.x, self.y), (self.x, 1 - self.y), (1 - self.x, 1 - self.y)]

    def slot(self, n, px, py, pc):
        dev = 4 * px + 2 * py + pc
        w, g = self.ids[n], self.gathered[n]
        if w == W_IN:
            return g.at[:, pl.ds(pl.multiple_of(dev * IN_SHARD, 128), IN_SHARD)]
        if w == W_OUT:
            return g.at[pl.ds(pl.multiple_of(dev * OUT_SHARD, 128), OUT_SHARD), :]
        if w == W_DOWN:
            return g.at[pl.ds(pl.multiple_of(dev * DOWN_SHARD, 32), DOWN_SHARD), :]
        if w == W_UP:
            return g.at[pl.ds(pl.multiple_of(dev * FF_SHARD, 32), FF_SHARD), :]
        return g.at[dev]

    def half(self, n, px, py, pc, h):
        dev = 4 * px + 2 * py + pc
        w, g = self.ids[n], self.gathered[n]
        if w == W_IN:
            return g.at[pl.ds(h * (D_MODEL // 2), D_MODEL // 2), pl.ds(pl.multiple_of(dev * IN_SHARD, 128), IN_SHARD)]
        rows = SHARD[w][0] // 2
        return g.at[pl.ds(pl.multiple_of(dev * SHARD[w][0] + h * rows, 16), rows), :]

    def tree(self, n):
        return self.ids[n] != W_CONV

    def copy(self, n, k, block, to, src=None, h=None):
        ref = self.slot(n, *block) if h is None else self.half(n, *block, h)
        return pltpu.make_async_remote_copy(
            src_ref=ref if src is None else src, dst_ref=ref,
            send_sem=self.send_sems.at[n, k], recv_sem=self.recv_sems.at[n, k], device_id=to, device_id_type=MESH)

    def _mine(self):
        return [pltpu.make_async_copy(self.stages[n], self.slot(n, *self.me), self.local_sems.at[n]) for n in range(len(self.ids))]

    def _first(self):
        out = []
        for n in range(len(self.ids)):
            out.append(self.copy(n, 0, self.me, self.sibling, src=self.stages[n]))
            out += [self.copy(n, 1 + j, self.me, (*chip, self.c), src=self.stages[n])
                    for j, chip in enumerate(self.chips[:2] if self.tree(n) else self.chips)]
        return out

    def start(self):
        for cp in self._mine() + self._first():
            cp.start()

    def _passed(self, j):
        dev = (*self.chips[j], self.c)
        out = []
        for n in range(len(self.ids)):
            if not self.tree(n):
                out.append(self.copy(n, 4 + j, dev, self.sibling))
            elif j < 2:
                out += [self.copy(n, 3 + j, dev, (*self.chips[1 - j], self.c), h=j), self.copy(n, 5 + j, dev, self.sibling)]
            else:
                out += [self.copy(n, 7, dev, self.sibling, h=0), self.copy(n, 8, dev, self.sibling, h=1)]
        return out

    def near(self):
        for j in range(2):
            dev = (*self.chips[j], self.c)
            for n in range(len(self.ids)):
                self.copy(n, 1 + j, dev, self.me).wait_recv()
            for cp in self._passed(j):
                cp.start()

    def finish(self):
        dev = (*self.chips[2], self.c)
        for n in range(len(self.ids)):
            if self.tree(n):
                self.copy(n, 3, dev, self.me, h=0).wait_recv()
                self.copy(n, 4, dev, self.me, h=1).wait_recv()
            else:
                self.copy(n, 3, dev, self.me).wait_recv()
        for cp in self._passed(2):
            cp.start()
        for n in range(len(self.ids)):
            self.copy(n, 0, self.sibling, self.me).wait_recv()
            for j, chip in enumerate(self.chips):
                dev = (*chip, 1 - self.c)
                if not self.tree(n):
                    self.copy(n, 4 + j, dev, self.me).wait_recv()
                elif j < 2:
                    self.copy(n, 5 + j, dev, self.me).wait_recv()
                else:
                    self.copy(n, 7, dev, self.me, h=0).wait_recv()
                    self.copy(n, 8, dev, self.me, h=1).wait_recv()
        for cp in self._mine():
            cp.wait()
        for cp in self._first() + self._passed(0) + self._passed(1) + self._passed(2):
            cp.wait_send()


def _gather_scratch(n):
    return [pltpu.SemaphoreType.DMA((n, _Gather.N_SEMS)), pltpu.SemaphoreType.DMA((n, _Gather.N_SEMS)), pltpu.SemaphoreType.DMA((n,))]


def _gathered_shapes(ids):
    return tuple(jax.ShapeDtypeStruct(*GATHERED[w]) for w in ids)


def _fwd_proj(x, g1, inv2, w_in, w_out, w_up, w_down, conv_w):
    ids_a, ids_b = [W_IN, W_CONV], [W_OUT, W_DOWN]

    def body(x_ref, g_ref, inv_ref, in_hbm, out_hbm, up_hbm, dn_hbm, cw_ref,
             proj_ref, h1_ref, cos_ref, sin_ref, gin, gcw, gout, gdn, su_ref,
             w_vm, s_in, s_cw, s_out, s_dn, f_in, f_out, f_up, f_dn, ld_sems,
             a_send, a_recv, a_local, b_send, b_recv, b_local):
        ag_a = _Gather(ids_a, [s_in, s_cw], [gin, gcw], a_send, a_recv, a_local)
        ag_b = _Gather(ids_b, [s_out, s_dn], [gout, gdn], b_send, b_recv, b_local)

        @pl.when(pl.program_id(0) == 0)
        def _():
            meet = _Meet(diagonal=True)
            meet.signal()
            loads = [pltpu.make_async_copy(src, dst, ld_sems.at[i])
                     for i, (src, dst) in enumerate(((in_hbm, f_in), (out_hbm, f_out), (dn_hbm, f_dn), (up_hbm, f_up)))]
            for cp in loads:
                cp.start()
            s_cw[...] = jnp.zeros_like(s_cw)
            for k in range(3):
                s_cw[k:k + 1, :] = cw_ref[k]
            loads[0].wait()
            s_in[...] = f_in[...].astype(BF16)
            meet.wait()
            ag_a.start()
            loads[1].wait()
            s_out[...] = f_out[...].astype(BF16)
            loads[2].wait()
            s_dn[...] = f_dn[...].astype(BF16)
            ag_a.near()
            ag_b.start()
            loads[3].wait()
            su_ref[...] = f_up[...].astype(BF16)
            ag_a.finish()
            fill = pltpu.make_async_copy(gin, w_vm, ld_sems.at[4])
            fill.start()
            fill.wait()

        pl.when(pl.program_id(0) == FWD_PROJ_PASS_AT)(ag_b.near)

        xb = x_ref[...]
        r = lax.rsqrt(jnp.mean(xb * xb, axis=-1, keepdims=True) + EPS)
        h = ((xb * r) * g_ref[...]).astype(BF16)
        h1_ref[...] = h
        p = _dot(h, w_vm[...])
        c2, s2 = _rope_block(inv_ref[...], pl.program_id(0) * TM)
        cos_ref[...], sin_ref[...] = c2, s2
        for hd in range(HEADS):
            sl = slice(hd * HEAD_DIM, (hd + 1) * HEAD_DIM)
            proj_ref[:, sl] = _rot(p[:, sl], c2, s2)
            ks = slice(RET_W + hd * HEAD_DIM, RET_W + (hd + 1) * HEAD_DIM)
            proj_ref[:, ks] = _rot(p[:, ks], c2, s2) * K_SCALE
        proj_ref[:, 2 * RET_W:] = p[:, 2 * RET_W:]

        pl.when(pl.program_id(0) == N_TB - 1)(ag_b.finish)

    tok = lambda w: pl.BlockSpec((TM, w), lambda i: (i, 0))
    hbm = pl.BlockSpec(memory_space=pl.ANY)
    vm = pl.BlockSpec(memory_space=pltpu.VMEM)
    return pl.pallas_call(
        body, name="fwd_proj", grid=(N_TB,),
        out_shape=(jax.ShapeDtypeStruct((SEQ, PROJ_W), F32), jax.ShapeDtypeStruct((SEQ, D_MODEL), BF16),
                   jax.ShapeDtypeStruct((SEQ, HEAD_DIM), F32), jax.ShapeDtypeStruct((SEQ, HEAD_DIM), F32))
        + _gathered_shapes(ids_a + ids_b) + (jax.ShapeDtypeStruct(SHARD[W_UP], BF16),),
        in_specs=[tok(D_MODEL), _resident((1, D_MODEL)), _resident((1, HEAD_DIM)), hbm, hbm, hbm, hbm, vm],
        out_specs=(tok(PROJ_W), tok(D_MODEL), tok(HEAD_DIM), tok(HEAD_DIM), hbm, hbm, hbm, hbm, vm),
        scratch_shapes=[pltpu.VMEM((D_MODEL, PROJ_W), BF16), pltpu.VMEM(SHARD[W_IN], BF16), pltpu.VMEM(SHARD[W_CONV], F32),
                        pltpu.VMEM(SHARD[W_OUT], BF16), pltpu.VMEM(SHARD[W_DOWN], BF16),
                        pltpu.VMEM(SHARD[W_IN], F32), pltpu.VMEM(SHARD[W_OUT], F32), pltpu.VMEM(SHARD[W_UP], F32),
                        pltpu.VMEM(SHARD[W_DOWN], F32), pltpu.SemaphoreType.DMA((5,))]
        + _gather_scratch(len(ids_a)) + _gather_scratch(len(ids_b)),
        compiler_params=_cparams(("arbitrary",), collective=COLLECTIVE["fwd_proj"]),
    )(x, g1, inv2, w_in, w_out, w_up, w_down, conv_w)


def _causal(w):
    r = lax.broadcasted_iota(jnp.int32, (CHUNK, CHUNK), 0)
    c = lax.broadcasted_iota(jnp.int32, (CHUNK, CHUNK), 1)
    return jnp.where(r >= c, w, 0.0)


def _fwd_mix(x, proj, wout_g, grn, lng, lnb, ws, bsb, su):
    cdec = _chunk_decay()
    ids = [W_UP]

    def body(x_ref, p_ref, w_ref, grn_ref, lng_ref, lnb_ref, ws_ref, bsb_ref, su_ref,
             x2_ref, cat_ref, o_ref, sp_ref, gup, state, m_ref, qd_ref, kd_ref, send_sems, recv_sems, local_sems):
        ag = _Gather(ids, [su_ref], [gup], send_sems, recv_sems, local_sems)

        @pl.when(pl.program_id(0) == 0)
        def _():
            meet = _Meet(diagonal=False)
            meet.signal()
            state[...] = jnp.zeros_like(state)
            _fill_decay(m_ref, qd_ref, kd_ref)
            meet.wait()
            ag.start()

        for h in range(HEADS):
            sl = slice(h * HEAD_DIM, (h + 1) * HEAD_DIM)
            q = p_ref[:, sl]
            k = p_ref[:, RET_W + h * HEAD_DIM:RET_W + (h + 1) * HEAD_DIM]
            v = p_ref[:, 2 * RET_W + h * HEAD_DIM:2 * RET_W + (h + 1) * HEAD_DIM]
            g = p_ref[:, 3 * RET_W + h * HEAD_DIM:3 * RET_W + (h + 1) * HEAD_DIM]
            qb, kb, vb = q.astype(BF16), k.astype(BF16), v.astype(BF16)
            a = _dot_nt(qb, kb) * m_ref[h]
            spb = state[h].astype(BF16)
            sp_ref[0, h] = spb
            o = _dot(a.astype(BF16), vb) + _dot((q * qd_ref[h]).astype(BF16), spb)
            state[h] = state[h] * cdec[h] + _dot_tn((k * kd_ref[h]).astype(BF16), vb)
            o_ref[:, sl] = o
            rinv = lax.rsqrt(jnp.mean(o * o, axis=-1, keepdims=True) + EPS)
            rn = (o * rinv) * grn_ref[:, sl]
            cat_ref[:, sl] = ((g * _sigmoid(g)) * rn).astype(BF16)
        for gi in range(HEADS):
            sl = slice(gi * HEAD_DIM, (gi + 1) * HEAD_DIM)
            u = p_ref[:, 4 * RET_W + gi * HEAD_DIM:4 * RET_W + (gi + 1) * HEAD_DIM]
            sv = p_ref[:, 4 * RET_W + SGU_W + gi * HEAD_DIM:4 * RET_W + SGU_W + (gi + 1) * HEAD_DIM]
            gv = _gelu(sv)
            xc = gv - jnp.mean(gv, axis=-1, keepdims=True)
            vn = (xc * lax.rsqrt(jnp.mean(xc * xc, axis=-1, keepdims=True) + EPS)) * lng_ref[:, sl] + lnb_ref[:, sl]
            mixed = _dot(_causal(ws_ref[gi]).astype(BF16), vn.astype(BF16)) + bsb_ref[gi]
            cat_ref[:, RET_W + gi * HEAD_DIM:RET_W + (gi + 1) * HEAD_DIM] = (_gelu(u) * mixed).astype(BF16)
        x2_ref[...] = x_ref[...] + _dot(cat_ref[...], w_ref[...])

        pl.when(pl.program_id(0) == FWD_MIX_PASS_AT)(ag.near)
        pl.when(pl.program_id(0) == N_CHUNK - 1)(ag.finish)

    ch = lambda w: pl.BlockSpec((CHUNK, w), lambda i: (i, 0))
    hcc = (HEADS, CHUNK, CHUNK)
    hbm = pl.BlockSpec(memory_space=pl.ANY)
    return pl.pallas_call(
        body, name="fwd_mix", grid=(N_CHUNK,),
        out_shape=(jax.ShapeDtypeStruct((SEQ, D_MODEL), F32), jax.ShapeDtypeStruct((SEQ, D_MODEL), BF16),
                   jax.ShapeDtypeStruct((SEQ, RET_W), F32), jax.ShapeDtypeStruct((N_CHUNK, HEADS, HEAD_DIM, HEAD_DIM), BF16))
        + _gathered_shapes(ids),
        in_specs=[ch(D_MODEL), ch(PROJ_W), _resident((D_MODEL, D_MODEL)), _resident((1, RET_W)), _resident((1, SGU_W)),
                  _resident((1, SGU_W)), _resident(hcc), _resident(hcc), hbm],
        out_specs=(ch(D_MODEL), ch(D_MODEL), ch(RET_W), pl.BlockSpec((1, HEADS, HEAD_DIM, HEAD_DIM), lambda i: (i, 0, 0, 0)), hbm),
        scratch_shapes=[pltpu.VMEM((HEADS, HEAD_DIM, HEAD_DIM), F32)] + [pltpu.VMEM(hcc, F32)] * 3 + _gather_scratch(len(ids)),
        compiler_params=_cparams(("arbitrary",), collective=COLLECTIVE["fwd_mix"]),
    )(x, proj, wout_g, grn, lng, lnb, ws, bsb, su)


def _conv_taps(p, prev8):
    row = lax.broadcasted_iota(jnp.int32, p.shape, 0)
    p1 = jnp.where(row == 0, prev8[7:8, :], pltpu.roll(p, 1, 0))
    p2 = jnp.where(row == 0, prev8[6:7, :], jnp.where(row == 1, prev8[7:8, :], pltpu.roll(p, 2, 0)))
    return p1, p2


def _fwd_ffn(x2, g2, wup_g, cw_g, cb_g, wdn_g, gf, tgt):
    def block(carry, x_ref, g_ref, wu_ref, cw_ref, cb_ref, wd_ref, gf_ref, t_ref, h2_ref, up_ref, u_ref, act_ref, x3_ref, loss_ref):
        xb = x_ref[...]
        r = lax.rsqrt(jnp.mean(xb * xb, axis=-1, keepdims=True) + EPS)
        h = ((xb * r) * g_ref[...]).astype(BF16)
        h2_ref[...] = h
        acc = xb
        for t0, tw in FF_TILES:
            u = []
            for c0 in (t0, D_FF + t0):
                cs = slice(c0, c0 + tw)
                p = _dot_nt(h, wu_ref[pl.ds(c0, tw), :])
                up_ref[:, cs] = p.astype(BF16)
                p1, p2 = _conv_taps(p, carry[:, cs])
                carry[:, cs] = p[TM - 8:, :]
                us = p2 * cw_ref[0:1, cs] + p1 * cw_ref[1:2, cs] + p * cw_ref[2:3, cs] + cb_ref[:, cs]
                u_ref[:, cs] = us.astype(BF16)
                u.append(us)
            a = ((u[0] * _sigmoid(u[0])) * u[1]).astype(BF16)
            act_ref[:, t0:t0 + tw] = a
            acc = acc + _dot(a, wd_ref[pl.ds(t0, tw), :])
        x3_ref[...] = acc
        r3 = lax.rsqrt(jnp.mean(acc * acc, axis=-1, keepdims=True) + EPS)
        diff = (acc * r3) * gf_ref[...] - t_ref[...]
        loss_ref[...] = jnp.full(loss_ref.shape, 0.5 * jnp.sum(jnp.mean(diff * diff, axis=-1)), F32)

    tok = lambda w: pl.BlockSpec((TM, w), lambda i: (i, 0))
    in_specs = [tok(D_MODEL), _resident((1, D_MODEL)), _resident((2 * D_FF, D_MODEL)), _resident((8, 2 * D_FF)),
                _resident((1, 2 * D_FF)), _resident((D_FF, D_MODEL)), _resident((1, D_MODEL)), tok(D_MODEL)]
    out_specs = (tok(D_MODEL), tok(2 * D_FF), tok(2 * D_FF), tok(D_FF), tok(D_MODEL), pl.BlockSpec((1, 8, 128), lambda i: (i, 0, 0)))

    def body(*refs):
        *arrays, carry = refs
        carry[...] = jnp.zeros_like(carry)
        pltpu.emit_pipeline(functools.partial(block, carry), grid=(N_TB,), in_specs=in_specs, out_specs=out_specs)(*arrays)

    hbm = pl.BlockSpec(memory_space=pl.ANY)
    return pl.pallas_call(
        body, name="fwd_ffn",
        out_shape=(jax.ShapeDtypeStruct((SEQ, D_MODEL), BF16), jax.ShapeDtypeStruct((SEQ, 2 * D_FF), BF16),
                   jax.ShapeDtypeStruct((SEQ, 2 * D_FF), BF16),
                   jax.ShapeDtypeStruct((SEQ, D_FF), BF16), jax.ShapeDtypeStruct((SEQ, D_MODEL), F32),
                   jax.ShapeDtypeStruct((N_TB, 8, 128), F32)),
        in_specs=[hbm] * len(in_specs), out_specs=(hbm,) * len(out_specs),
        scratch_shapes=[pltpu.VMEM((8, 2 * D_FF), F32)],
        compiler_params=_cparams(),
    )(x2, g2, wup_g, cw_g, cb_g, wdn_g, gf, tgt)


def _bwd_ffn(x3, tgt, gf, x2, g2, up_pre, u_conv, wup_g, cw_g, wdn_g):
    def body(x3_ref, t_ref, gf_ref, x2_ref, g2_ref, up_ref, u_ref, wu_ref, cw_ref, wd_ref,
             dx3_ref, dpre_ref, dx2_ref, dgf_ref, dg2_ref, dcv_ref, nxt):
        i = pl.program_id(0)

        @pl.when(i == 0)
        def _():
            nxt[...] = jnp.zeros_like(nxt)
            dgf_ref[...] = jnp.zeros_like(dgf_ref)
            dg2_ref[...] = jnp.zeros_like(dg2_ref)
            dcv_ref[...] = jnp.zeros_like(dcv_ref)

        x3 = x3_ref[...]
        r3 = lax.rsqrt(jnp.mean(x3 * x3, axis=-1, keepdims=True) + EPS)
        xh3 = x3 * r3
        dy = (xh3 * gf_ref[...] - t_ref[...]) * (1.0 / D_MODEL)
        dgf_ref[0:1, :] += jnp.sum(dy * xh3, axis=0, keepdims=True)
        t3 = dy * gf_ref[...]
        dx3 = r3 * (t3 - xh3 * jnp.mean(t3 * xh3, axis=-1, keepdims=True))
        dx3b = dx3.astype(BF16)
        dx3_ref[...] = dx3b
        dh2 = jnp.zeros((TM, D_MODEL), F32)
        for t0, tw in FF_TILES:
            row = lax.broadcasted_iota(jnp.int32, (TM, tw), 0)
            ts = slice(t0, t0 + tw)
            dact = _dot_nt(dx3b, wd_ref[pl.ds(t0, tw), :])
            ua = u_ref[:, ts].astype(F32)
            ub = u_ref[:, D_FF + t0:D_FF + t0 + tw].astype(F32)
            sg = _sigmoid(ua)
            du = [dact * ub * (sg * (1.0 + ua * (1.0 - sg))), dact * (ua * sg)]
            for n in range(2):
                d = du[n]
                c0 = n * D_FF + t0
                cs = slice(c0, c0 + tw)
                nx = nxt[:, cs]
                n1 = jnp.where(row == TM - 1, nx[0:1, :], pltpu.roll(d, TM - 1, 0))
                n2 = jnp.where(row == TM - 2, nx[0:1, :], jnp.where(row == TM - 1, nx[1:2, :], pltpu.roll(d, TM - 2, 0)))
                nxt[:, cs] = d[0:8, :]
                dp = (d * cw_ref[2:3, cs] + n1 * cw_ref[1:2, cs] + n2 * cw_ref[0:1, cs]).astype(BF16)
                dpre_ref[:, cs] = dp
                p = up_ref[:, cs].astype(F32)
                dcv_ref[n, 0:1, ts] += jnp.sum(n2 * p, axis=0, keepdims=True)
                dcv_ref[n, 1:2, ts] += jnp.sum(n1 * p, axis=0, keepdims=True)
                dcv_ref[n, 2:3, ts] += jnp.sum(d * p, axis=0, keepdims=True)
                dcv_ref[n, 3:4, ts] += jnp.sum(d, axis=0, keepdims=True)
                dh2 = dh2 + _dot(dp, wu_ref[pl.ds(c0, tw), :])
        x2 = x2_ref[...]
        r2 = lax.rsqrt(jnp.mean(x2 * x2, axis=-1, keepdims=True) + EPS)
        xh2 = x2 * r2
        dg2_ref[0:1, :] += jnp.sum(dh2 * xh2, axis=0, keepdims=True)
        t2 = dh2 * g2_ref[...]
        dx2_ref[...] = dx3 + r2 * (t2 - xh2 * jnp.mean(t2 * xh2, axis=-1, keepdims=True))

    rev = lambda w: pl.BlockSpec((TM, w), lambda i: (N_TB - 1 - i, 0))
    acc = lambda s: pl.BlockSpec(s, lambda i: (0,) * len(s))
    return pl.pallas_call(
        body, name="bwd_ffn", grid=(N_TB,),
        out_shape=(jax.ShapeDtypeStruct((SEQ, D_MODEL), BF16), jax.ShapeDtypeStruct((SEQ, 2 * D_FF), BF16),
                   jax.ShapeDtypeStruct((SEQ, D_MODEL), F32), jax.ShapeDtypeStruct((8, D_MODEL), F32),
                   jax.ShapeDtypeStruct((8, D_MODEL), F32), jax.ShapeDtypeStruct((2, 8, D_FF), F32)),
        in_specs=[rev(D_MODEL), rev(D_MODEL), _resident((1, D_MODEL)), rev(D_MODEL), _resident((1, D_MODEL)), rev(2 * D_FF),
                  rev(2 * D_FF), _resident((2 * D_FF, D_MODEL)), _resident((8, 2 * D_FF)), _resident((D_FF, D_MODEL))],
        out_specs=(rev(D_MODEL), rev(2 * D_FF), rev(D_MODEL), acc((8, D_MODEL)), acc((8, D_MODEL)), acc((2, 8, D_FF))),
        scratch_shapes=[pltpu.VMEM((8, 2 * D_FF), F32)],
        compiler_params=_cparams(("arbitrary",)),
    )(x3, tgt, gf, x2, g2, up_pre, u_conv, wup_g, cw_g, wdn_g)


def _bwd_mix(dx2, proj, o, sprev, wout_g, grn, lng, lnb, ws, bsb, cos2, sin2, hosted):
    cdec = _chunk_decay()
    geoms = [g for g, _ in hosted]
    n_h = len(hosted)

    def body(dx2_ref, p_ref, o_ref, sp_ref, w_ref, grn_ref, lng_ref, lnb_ref, ws_ref, bsb_ref, cos_ref, sin_ref, *rest):
        dp_ref, dgrn_ref, dlng_ref, dlnb_ref, dws_ref, dbs_ref = rest[n_h:n_h + 6]
        dstate, dbs_acc, m_ref, qd_ref, kd_ref = rest[2 * n_h + 6:2 * n_h + 11]
        i = pl.program_id(0)
        rs = _Scatters(geoms, rest[:n_h], rest[n_h + 6:2 * n_h + 6], rest[2 * n_h + 11:])
        pl.when(i == 0)(rs.phase1)
        pl.when(i == 3)(rs.phase2)
        pl.when(i == 8)(rs.phase2b)

        @pl.when(i == 0)
        def _():
            _fill_decay(m_ref, qd_ref, kd_ref)
            dstate[...] = jnp.zeros_like(dstate)
            dgrn_ref[...] = jnp.zeros_like(dgrn_ref)
            dlng_ref[...] = jnp.zeros_like(dlng_ref)
            dlnb_ref[...] = jnp.zeros_like(dlnb_ref)
            dws_ref[...] = jnp.zeros_like(dws_ref)
            dbs_ref[...] = jnp.zeros_like(dbs_ref)
            dbs_acc[...] = jnp.zeros_like(dbs_acc)

        dmix = _dot_nt(dx2_ref[...].astype(BF16), w_ref[...])
        for h in range(HEADS):
            sl = slice(h * HEAD_DIM, (h + 1) * HEAD_DIM)
            q = p_ref[:, sl]
            k = p_ref[:, RET_W + h * HEAD_DIM:RET_W + (h + 1) * HEAD_DIM]
            v = p_ref[:, 2 * RET_W + h * HEAD_DIM:2 * RET_W + (h + 1) * HEAD_DIM]
            g = p_ref[:, 3 * RET_W + h * HEAD_DIM:3 * RET_W + (h + 1) * HEAD_DIM]
            o = o_ref[:, sl]
            rinv = lax.rsqrt(jnp.mean(o * o, axis=-1, keepdims=True) + EPS)
            oh = o * rinv
            gr = grn_ref[:, sl]
            sg = _sigmoid(g)
            dret = dmix[:, sl]
            dp_ref[:, 3 * RET_W + h * HEAD_DIM:3 * RET_W + (h + 1) * HEAD_DIM] = (
                dret * (oh * gr) * (sg * (1.0 + g * (1.0 - sg)))).astype(BF16)
            drn = dret * (g * sg)
            dgrn_ref[0:1, sl] += jnp.sum(drn * oh, axis=0, keepdims=True)
            t = drn * gr
            do = rinv * (t - oh * jnp.mean(t * oh, axis=-1, keepdims=True))
            qb, kb, vb, dob = q.astype(BF16), k.astype(BF16), v.astype(BF16), do.astype(BF16)
            m = m_ref[h]
            ab = (_dot_nt(qb, kb) * m).astype(BF16)
            dab = (_dot_nt(dob, vb) * m).astype(BF16)
            spb = sp_ref[0, h]
            dsn = dstate[h]
            dsnb = dsn.astype(BF16)
            qdb = (q * qd_ref[h]).astype(BF16)
            kdb = (k * kd_ref[h]).astype(BF16)
            dq = _dot(dab, kb) + _dot_nt(dob, spb) * qd_ref[h]
            dk = _dot_tn(dab, qb) + _dot_nt(vb, dsnb) * kd_ref[h]
            dv = _dot_tn(ab, dob) + _dot(kdb, dsnb)
            dstate[h] = dsn * cdec[h] + _dot_tn(qdb, dob)
            c2, s2 = cos_ref[...], sin_ref[...]
            dp_ref[:, sl] = _rot_t(dq, c2, s2).astype(BF16)
            dp_ref[:, RET_W + h * HEAD_DIM:RET_W + (h + 1) * HEAD_DIM] = _rot_t(dk * K_SCALE, c2, s2).astype(BF16)
            dp_ref[:, 2 * RET_W + h * HEAD_DIM:2 * RET_W + (h + 1) * HEAD_DIM] = dv.astype(BF16)
        for gi in range(HEADS):
            sl = slice(gi * HEAD_DIM, (gi + 1) * HEAD_DIM)
            u = p_ref[:, 4 * RET_W + gi * HEAD_DIM:4 * RET_W + (gi + 1) * HEAD_DIM]
            sv = p_ref[:, 4 * RET_W + SGU_W + gi * HEAD_DIM:4 * RET_W + SGU_W + (gi + 1) * HEAD_DIM]
            gv = _gelu(sv)
            xc = gv - jnp.mean(gv, axis=-1, keepdims=True)
            rstd = lax.rsqrt(jnp.mean(xc * xc, axis=-1, keepdims=True) + EPS)
            xh = xc * rstd
            lg = lng_ref[:, sl]
            vnb = (xh * lg + lnb_ref[:, sl]).astype(BF16)
            wcb = _causal(ws_ref[gi]).astype(BF16)
            mixed = _dot(wcb, vnb) + bsb_ref[gi]
            dsgu = dmix[:, RET_W + gi * HEAD_DIM:RET_W + (gi + 1) * HEAD_DIM]
            dmixed = dsgu * _gelu(u)
            dmb = dmixed.astype(BF16)
            dws_ref[gi] += _causal(_dot_nt(dmb, vnb))
            dbs_acc[gi] += dmixed
            dvn = _dot_tn(wcb, dmb)
            dlng_ref[gi:gi + 1, :] += jnp.sum(dvn * xh, axis=0, keepdims=True)
            dlnb_ref[gi:gi + 1, :] += jnp.sum(dvn, axis=0, keepdims=True)
            dxh = dvn * lg
            dgv = rstd * (dxh - jnp.mean(dxh, axis=-1, keepdims=True) - xh * jnp.mean(dxh * xh, axis=-1, keepdims=True))
            dp_ref[:, 4 * RET_W + gi * HEAD_DIM:4 * RET_W + (gi + 1) * HEAD_DIM] = (dsgu * mixed * _gelu_grad(u)).astype(BF16)
            dp_ref[:, 4 * RET_W + SGU_W + gi * HEAD_DIM:4 * RET_W + SGU_W + (gi + 1) * HEAD_DIM] = (
                dgv * _gelu_grad(sv)).astype(BF16)

        @pl.when(i == N_CHUNK - 1)
        def _():
            for gi in range(HEADS):
                col = jnp.broadcast_to(jnp.sum(dbs_acc[gi], axis=-1, keepdims=True), (CHUNK, CHUNK))
                dbs_ref[gi:gi + 1, :] = jnp.transpose(col)[0:1, :]
            rs.phase3()

    rev = lambda w: pl.BlockSpec((CHUNK, w), lambda i: (N_CHUNK - 1 - i, 0))
    hcc = (HEADS, CHUNK, CHUNK)
    acc = lambda s: pl.BlockSpec(s, lambda i: (0,) * len(s))
    res = pl.pallas_call(
        body, name="bwd_mix", grid=(N_CHUNK,),
        out_shape=(jax.ShapeDtypeStruct((SEQ, PROJ_W), BF16), jax.ShapeDtypeStruct((8, RET_W), F32),
                   jax.ShapeDtypeStruct((8, HEAD_DIM), F32), jax.ShapeDtypeStruct((8, HEAD_DIM), F32),
                   jax.ShapeDtypeStruct(hcc, F32), jax.ShapeDtypeStruct((8, CHUNK), F32)) + _scatter_out_shapes(geoms),
        in_specs=[rev(D_MODEL), rev(PROJ_W), rev(RET_W),
                  pl.BlockSpec((1, HEADS, HEAD_DIM, HEAD_DIM), lambda i: (N_CHUNK - 1 - i, 0, 0, 0)),
                  _resident((D_MODEL, D_MODEL)), _resident((1, RET_W)), _resident((1, SGU_W)), _resident((1, SGU_W)),
                  _resident(hcc), _resident(hcc), rev(HEAD_DIM), rev(HEAD_DIM)]
        + [pl.BlockSpec(memory_space=pl.ANY)] * n_h,
        out_specs=(rev(PROJ_W), acc((8, RET_W)), acc((8, HEAD_DIM)), acc((8, HEAD_DIM)), acc(hcc), acc((8, CHUNK)))
        + _scatter_out_specs(geoms),
        scratch_shapes=[pltpu.VMEM((HEADS, HEAD_DIM, HEAD_DIM), F32), pltpu.VMEM((HEADS, CHUNK, CHUNK), F32)]
        + [pltpu.VMEM(hcc, F32)] * 3 + _scatter_scratch(geoms),
        compiler_params=_cparams(("arbitrary",), collective=COLLECTIVE["bwd_mix"]),
    )(dx2, proj, o, sprev, wout_g, grn, lng, lnb, ws, bsb, cos2, sin2, *[p for _, p in hosted])
    return tuple(res[:6 + n_h])


def _bwd_proj(dproj, win_g, x, g1, dx2, gin_p, small):
    geoms = [W_IN]
    n_s = len(small)

    def body(dp_ref, w_ref, x_ref, g_ref, dx2_ref, gin_ref, *rest):
        small_refs = rest[:n_s]
        dx_ref, rs_out, rp_ref, rws_ref, rcv_ref, dg_ref = rest[n_s:n_s + 6]
        rs_scratch = rest[n_s + 6:n_s + 6 + N_SCATTER_SCRATCH]
        ar_scratch = rest[n_s + 6 + N_SCATTER_SCRATCH:]
        ar_res = ar_scratch[N_SMALL_SCRATCH:]
        ar = _SmallReduce((dg_ref,) + tuple(small_refs), ar_res, ar_scratch[:N_SMALL_SCRATCH])
        rs = _Scatters(geoms, [gin_ref], [rs_out], rs_scratch)
        pl.when(pl.program_id(0) == 0)(lambda: rs.phase1(diagonal=True))
        pl.when(pl.program_id(0) == 1)(rs.phase2)
        pl.when(pl.program_id(0) == 5)(rs.phase2b)

        @pl.when(pl.program_id(0) == 0)
        def _():
            dg_ref[...] = jnp.zeros_like(dg_ref)

        dh = _dot_nt(dp_ref[...], w_ref[...])
        xb = x_ref[...]
        r = lax.rsqrt(jnp.mean(xb * xb, axis=-1, keepdims=True) + EPS)
        xh = xb * r
        dg_ref[0:1, :] += jnp.sum(dh * xh, axis=0, keepdims=True)
        t = dh * g_ref[...]
        dx_ref[...] = dx2_ref[...] + r * (t - xh * jnp.mean(t * xh, axis=-1, keepdims=True))

        @pl.when(pl.program_id(0) == N_TB - 1)
        def _():
            ar.begin()
            rs.phase3()
            ar.end()
            for o_ref, r_ref in zip((rp_ref, rws_ref, rcv_ref), ar_res):
                o_ref[...] = r_ref[...]

    tok = lambda w: pl.BlockSpec((TM, w), lambda i: (i, 0))
    vm = pl.BlockSpec(memory_space=pltpu.VMEM)
    res = pl.pallas_call(
        body, name="bwd_proj", grid=(N_TB,),
        out_shape=(jax.ShapeDtypeStruct((SEQ, D_MODEL), F32),) + _scatter_out_shapes(geoms)
        + tuple(jax.ShapeDtypeStruct(s, F32) for s in SMALL_FULL),
        in_specs=[tok(PROJ_W), _resident((D_MODEL, PROJ_W)), tok(D_MODEL), _resident((1, D_MODEL)), tok(D_MODEL),
                  pl.BlockSpec(memory_space=pl.ANY)] + [vm] * n_s,
        out_specs=(tok(D_MODEL),) + _scatter_out_specs(geoms) + (vm,) * len(SMALL_FULL),
        scratch_shapes=[pltpu.VMEM((8, D_MODEL), F32)] + _scatter_scratch(geoms) + _small_scratch()
        + [pltpu.VMEM(s, F32) for s in SMALL_FULL],
        compiler_params=_cparams(("arbitrary",), collective=COLLECTIVE["bwd_proj"]),
    )(dproj, win_g, x, g1, dx2, gin_p, *small)
    return res


def _wgrad(name, a, b, tm=None, tn=None, hosted=()):
    m_w, n_w = a.shape[-1], b.shape[-1]
    tm = m_w if tm is None else tm
    tn = n_w if tn is None else tn
    n_steps = (m_w // tm) * (n_w // tn)
    geoms = [g for g, _ in hosted]
    n_h = len(hosted)

    def body(a_ref, b_ref, *rest):
        o_ref = rest[n_h]
        if n_h:
            rs = _Scatters(geoms, rest[:n_h], rest[n_h + 1:2 * n_h + 1], rest[2 * n_h + 1:])
            step = pl.program_id(0) * (n_w // tn) + pl.program_id(1)
            pl.when(step == 0)(rs.phase1)
            pl.when(step == 1)(rs.phase2)
            pl.when(step == n_steps // 2)(rs.phase2b)
        o_ref[...] = _dot_tn(a_ref[...].astype(BF16), b_ref[...].astype(BF16)).astype(BF16)
        if n_h:
            pl.when(step == n_steps - 1)(rs.phase3)

    assert not n_h or n_steps >= 4
    res = pl.pallas_call(
        body, name=name, grid=(m_w // tm, n_w // tn),
        out_shape=(jax.ShapeDtypeStruct((m_w, n_w), BF16),) + _scatter_out_shapes(geoms),
        in_specs=[pl.BlockSpec((SEQ, tm), lambda i, j: (0, i)), pl.BlockSpec((SEQ, tn), lambda i, j: (0, j))]
        + [pl.BlockSpec(memory_space=pl.ANY)] * n_h,
        out_specs=(pl.BlockSpec((tm, tn), lambda i, j: (i, j)),) + _scatter_out_specs(geoms),
        scratch_shapes=_scatter_scratch(geoms),
        compiler_params=_cparams(("arbitrary", "arbitrary"), collective=COLLECTIVE[name]) if n_h else _cparams(("parallel", "parallel")),
    )(a, b, *[p for _, p in hosted])
    return tuple(res[:1 + n_h])


def _row_step(half_rows):
    return max(s for s in range(16, 177, 16) if half_rows % s == 0)


class _Scatter:
    def __init__(self, geom, partial, out, land1, mine, stage2, land2, comb, s1_send, s1_recv, s2_send, s2_recv, ld_sems):
        self.w, self.row0, self.shape = _geom(geom)
        self.partial, self.out, self.land1 = partial, out, land1
        self.mine, self.stage2, self.land2, self.comb = mine, stage2, land2, comb
        self.hr = self.shape[0] // 2
        self.step = _row_step(self.hr)
        self.s1_send, self.s1_recv, self.s2_send, self.s2_recv, self.ld_sems = s1_send, s1_recv, s2_send, s2_recv, ld_sems
        self.x, self.y, self.c = lax.axis_index("x"), lax.axis_index("y"), lax.axis_index("c")
        self.sibling = (self.x, self.y, 1 - self.c)
        self.chips = [(self.x, self.y), (1 - self.x, self.y), (self.x, 1 - self.y), (1 - self.x, 1 - self.y)]

    def block(self, px, py, pc):
        dev = 4 * px + 2 * py + pc
        if self.w == W_IN:
            return self.partial.at[:, pl.ds(pl.multiple_of(dev * IN_SHARD, 128), IN_SHARD)]
        if self.w == W_OUT:
            return self.partial.at[pl.ds(pl.multiple_of(dev * OUT_SHARD, 128), OUT_SHARD), :]
        if self.w == W_DOWN:
            return self.partial.at[pl.ds(pl.multiple_of(dev * DOWN_SHARD, 32), DOWN_SHARD), :]
        return self.partial.at[pl.ds(pl.multiple_of(dev * FF_SHARD + self.row0, 32), self.shape[0]), :]

    def copy1(self, k):
        return pltpu.make_async_remote_copy(
            src_ref=self.block(*self.chips[k], 1 - self.c), dst_ref=self.land1.at[k],
            send_sem=self.s1_send.at[k], recv_sem=self.s1_recv.at[k], device_id=self.sibling, device_id_type=MESH)

    STAGE2 = [(1, 0, 1), (3, 0, 1), (2, 1, 2), (3, 1, 2), (1, 1, 1), (2, 0, 2)]

    def copy2(self, j):
        blk, h, to = self.STAGE2[j]
        src = self.comb.at[j - 4] if j >= 4 else self.stage2.at[blk - 1, pl.ds(h * self.hr, self.hr), :]
        return pltpu.make_async_remote_copy(
            src_ref=src, dst_ref=self.land2.at[j], send_sem=self.s2_send.at[j], recv_sem=self.s2_recv.at[j],
            device_id=(*self.chips[to], self.c), device_id_type=MESH)

    def _rows(self, h=None):
        step = self.step
        lo, n = (0, self.shape[0]) if h is None else (h * self.hr, self.hr)
        return [pl.ds(r0, step) for r0 in range(lo, lo + n, step)]

    def load(self, k):
        return pltpu.make_async_copy(self.block(*self.chips[k], self.c), self.mine.at[k], self.ld_sems.at[k])

    def load_mine(self):
        for k in range(4):
            self.load(k).start()

    def phase1(self):
        for k in range(4):
            self.copy1(k).start()

    def phase2(self, k):
        self.copy1(k).wait_recv()
        self.load(k).wait()
        for rs in self._rows():
            s = self.mine[k, rs, :].astype(F32) + self.land1[k, rs, :].astype(F32)
            if k == 0:
                self.out[rs, :] = s
            else:
                self.stage2[k - 1, rs, :] = s.astype(BF16)
        for j in {3: (1, 3), 1: (0,), 2: (2,), 0: ()}[k]:
            self.copy2(j).start()

    def phase2b(self):
        for j, got in ((4, 3), (5, 1)):
            blk, h, _ = self.STAGE2[j]
            self.copy2(got).wait_recv()
            for i, rs in enumerate(self._rows(h)):
                lr = pl.ds(i * self.step, self.step)
                self.comb[j - 4, lr, :] = (self.stage2[blk - 1, rs, :].astype(F32) + self.land2[got, lr, :].astype(F32)).astype(BF16)
            self.copy2(j).start()

    def phase3(self):
        for j in (0, 5, 4, 2):
            self.copy2(j).wait_recv()
        for h, (first, second) in enumerate(((0, 5), (4, 2))):
            for i, rs in enumerate(self._rows(h)):
                lr = pl.ds(i * self.step, self.step)
                self.out[rs, :] = (self.out[rs, :] + self.land2[first, lr, :].astype(F32)) + self.land2[second, lr, :].astype(F32)
        for k in range(4):
            self.copy1(k).wait_send()
        for j in range(6):
            self.copy2(j).wait_send()


def _geom(geom):
    if isinstance(geom, tuple):
        w, row0, rows = geom
        assert w == W_UP
        return w, row0, (rows, SHARD[w][1])
    return geom, 0, SHARD[geom]


N_SCATTER_SCRATCH = 10


def _scatter_out_shapes(geoms):
    return tuple(jax.ShapeDtypeStruct(_geom(g)[2], F32) for g in geoms)


def _scatter_out_specs(geoms):
    return (pl.BlockSpec(memory_space=pltpu.VMEM),) * len(geoms)


def _scatter_scratch(geoms):
    out = []
    for g in geoms:
        s = _geom(g)[2]
        hs = (s[0] // 2, s[1])
        out += [pltpu.VMEM((4,) + s, BF16), pltpu.VMEM((4,) + s, BF16), pltpu.VMEM((3,) + s, BF16), pltpu.VMEM((6,) + hs, BF16),
                pltpu.VMEM((2,) + hs, BF16),
                pltpu.SemaphoreType.DMA((4,)), pltpu.SemaphoreType.DMA((4,)), pltpu.SemaphoreType.DMA((6,)),
                pltpu.SemaphoreType.DMA((6,)), pltpu.SemaphoreType.DMA((4,))]
    return out


class _Scatters:
    def __init__(self, geoms, p_refs, out_refs, scratch):
        k = N_SCATTER_SCRATCH
        self.items = [_Scatter(g, p_refs[i], out_refs[i], *scratch[k * i:k * i + k]) for i, g in enumerate(geoms)]

    def phase1(self, diagonal=False):
        meet = _Meet(diagonal)
        meet.signal()
        for s in self.items:
            s.load_mine()
        meet.wait()
        for s in self.items:
            s.phase1()

    def phase2(self):
        for k in (3, 1, 2, 0):
            for s in self.items:
                s.phase2(k)

    def phase2b(self):
        for s in self.items:
            s.phase2b()

    def phase3(self):
        for s in self.items:
            s.phase3()


PACK_W = 1024


SMALL_FULL = [(2, 8, PACK_W), (HEADS, CHUNK, CHUNK), (2, 8, D_FF)]
SMALL_HALF = [(s[0] // 2,) + s[1:] for s in SMALL_FULL]
N_SMALL_SCRATCH = 16


def _small_scratch():
    n_a = len(SMALL_FULL)
    return ([pltpu.VMEM(SMALL_FULL[0], F32)] + [pltpu.VMEM(s, F32) for s in SMALL_HALF] + [pltpu.VMEM(s, F32) for s in SMALL_HALF]
            + [pltpu.VMEM((3,) + s, F32) for s in SMALL_HALF]
            + [pltpu.SemaphoreType.DMA((n_a,)), pltpu.SemaphoreType.DMA((n_a,)), pltpu.SemaphoreType.DMA((n_a, 3)),
               pltpu.SemaphoreType.DMA((n_a, 3)), pltpu.SemaphoreType.DMA((n_a,)), pltpu.SemaphoreType.DMA((n_a,))])


class _SmallReduce:
    def __init__(self, ins, outs, scratch):
        self.ins, self.outs = ins, outs
        (self.pack, *rest) = scratch
        self.rxs, self.css, self.gs = rest[0:3], rest[3:6], rest[6:9]
        self.s1_send, self.s1_recv, self.s2_send, self.s2_recv, self.s3_send, self.s3_recv = rest[9:]
        self.x, self.y, self.c = lax.axis_index("x"), lax.axis_index("y"), lax.axis_index("c")
        self.sibling = (self.x, self.y, 1 - self.c)
        self.chips = [(1 - self.x, self.y), (self.x, 1 - self.y), (1 - self.x, 1 - self.y)]
        self.hl = [s[0] for s in SMALL_HALF]

    def half(self, ref, a, h):
        return ref.at[pl.ds(h * self.hl[a], self.hl[a])]

    def begin(self):
        dg1_ref, dg2_ref, dgf_ref, dgrn_ref, dlng_ref, dlnb_ref, dbs_ref, loss_ref, dws_ref, dcv_ref = self.ins
        pack, c = self.pack, self.c
        pack[...] = jnp.zeros_like(pack)
        pack[0, 0:1, :] = dg1_ref[0:1, :]
        pack[0, 1:2, :] = dg2_ref[0:1, :]
        pack[0, 2:3, :] = dgf_ref[0:1, :]
        pack[0, 3:4, 0:RET_W] = dgrn_ref[0:1, :]
        lsum = loss_ref[0, 0:1, :]
        for i in range(1, N_TB):
            lsum = lsum + loss_ref[i, 0:1, :]
        pack[0, 3:4, RET_W:RET_W + 128] = lsum
        pack[1, 0:HEADS, 0:128] = dlng_ref[0:HEADS, :]
        pack[1, 0:HEADS, 128:256] = dlnb_ref[0:HEADS, :]
        pack[1, 0:HEADS, 256:384] = dbs_ref[0:HEADS, :]
        self.srcs = [pack, dws_ref, dcv_ref]
        n_a = len(self.srcs)
        self.ex1 = [pltpu.make_async_remote_copy(src_ref=self.half(self.srcs[a], a, 1 - c), dst_ref=self.rxs[a],
                                                 send_sem=self.s1_send.at[a], recv_sem=self.s1_recv.at[a],
                                                 device_id=self.sibling, device_id_type=MESH) for a in range(n_a)]
        for cp in self.ex1:
            cp.start()
        self.ex2 = []
        for a in range(n_a):
            self.ex1[a].wait_recv()
            self.css[a][...] = self.half(self.srcs[a], a, c)[...] + self.rxs[a][...]
            for j, chip in enumerate(self.chips):
                cp = pltpu.make_async_remote_copy(src_ref=self.css[a], dst_ref=self.gs[a].at[j], send_sem=self.s2_send.at[a, j],
                                                  recv_sem=self.s2_recv.at[a, j], device_id=(*chip, c), device_id_type=MESH)
                cp.start()
                self.ex2.append(cp)

    def end(self):
        c, x, y = self.c, self.x, self.y
        ex3 = []
        for a in range(len(self.srcs)):
            css, gs, out = self.css[a], self.gs[a], self.outs[a]
            for j in range(3):
                self.ex2[3 * a + j].wait_recv()
            tot = None
            for q in range(4):
                k = jnp.where(x != (q >> 1), 1, 0) + jnp.where(y != (q & 1), 2, 0)
                term = jnp.where(k == 0, css[...], jnp.where(k == 1, gs[0], jnp.where(k == 2, gs[1], gs[2])))
                tot = term if tot is None else tot + term
            self.half(out, a, c)[...] = tot
            cp = pltpu.make_async_remote_copy(src_ref=self.half(out, a, c), dst_ref=self.half(out, a, c), send_sem=self.s3_send.at[a],
                                              recv_sem=self.s3_recv.at[a], device_id=self.sibling, device_id_type=MESH)
            cp.start()
            ex3.append(cp)
        for a in range(len(self.srcs)):
            out = self.outs[a]
            pltpu.make_async_remote_copy(src_ref=self.half(out, a, 1 - c), dst_ref=self.half(out, a, 1 - c), send_sem=self.s3_send.at[a],
                                         recv_sem=self.s3_recv.at[a], device_id=self.sibling, device_id_type=MESH).wait_recv()
        for cp in self.ex1 + self.ex2 + ex3:
            cp.wait_send()


def _adam_math(w, g, m, v):
    nm = ADAM_B1 * m + (1.0 - ADAM_B1) * g
    nv = ADAM_B2 * v + (1.0 - ADAM_B2) * (g * g)
    d = -ADAM_LR * ((nm / (1.0 - ADAM_B1 ** ADAM_STEP)) / (jnp.sqrt(nv / (1.0 - ADAM_B2 ** ADAM_STEP)) + ADAM_EPS) + ADAM_WD * w)
    return d, nm, nv


def _adamw(params, thru, n_steps):
    plan = []
    for w, gs, _, _ in params:
        _, r, cdim = w.shape
        if len(gs) == 1:
            edges = [0, r // n_steps]
            spec3 = pl.BlockSpec((1, r // n_steps, cdim), lambda i: (0, i, 0))
            g_specs = [pl.BlockSpec((r // n_steps, cdim), lambda i: (i, 0))]
        else:
            edges = [sum(g.shape[0] for g in gs[:k]) for k in range(len(gs) + 1)]
            spec3 = pl.BlockSpec((1, r, cdim // n_steps), lambda i: (0, 0, i))
            g_specs = [pl.BlockSpec((g.shape[0], cdim // n_steps), lambda i: (0, i)) for g in gs]
        plan.append((len(gs), edges, spec3, g_specs))
    n_in = sum(n_g + 3 for n_g, _, _, _ in plan)
    n_t = len(thru)

    def body(*refs):
        ins, outs = refs[:n_in], refs[n_in + n_t:]
        for n_g, edges, _, _ in plan:
            (w_ref, *g_refs, m_ref, v_ref), ins = ins[:n_g + 3], ins[n_g + 3:]
            (go_ref, d_ref, nm_ref, nv_ref), outs = outs[:4], outs[4:]
            for g_ref, lo, hi in zip(g_refs, edges[:-1], edges[1:]):
                gg = g_ref[...]
                go_ref[0, lo:hi, :] = gg
                d_ref[0, lo:hi, :], nm_ref[0, lo:hi, :], nv_ref[0, lo:hi, :] = _adam_math(
                    w_ref[0, lo:hi, :], gg, m_ref[0, lo:hi, :], v_ref[0, lo:hi, :])
        for t_ref, to_ref in zip(refs[n_in:n_in + n_t], outs):
            to_ref[...] = t_ref[...]

    t_specs = [pl.BlockSpec((t.shape[0] // n_steps, t.shape[1]), lambda i: (i, 0)) for t in thru]
    in_specs, out_specs, out_shape, args = [], [], [], []
    for (w, gs, m, v), (_, _, spec3, g_specs) in zip(params, plan):
        in_specs += [spec3] + g_specs + [spec3, spec3]
        out_specs += [spec3] * 4
        out_shape += [jax.ShapeDtypeStruct(w.shape, F32)] * 4
        args += [w, *gs, m, v]
    res = pl.pallas_call(
        body, name="adamw", grid=(n_steps,), out_shape=tuple(out_shape) + tuple(jax.ShapeDtypeStruct(t.shape, t.dtype) for t in thru),
        in_specs=in_specs + t_specs, out_specs=tuple(out_specs) + tuple(t_specs),
        compiler_params=_cparams(("parallel",)),
    )(*args, *thru)
    return [res[4 * k:4 * k + 4] for k in range(len(params))], res[4 * len(params):]


def _adamw_small(rp, rws, rcv, gcw, params):
    n_p = len(params)

    def body(*refs):
        rp_ref, rws_ref, rcv_ref, gcw_ref = refs[:4]
        ins = refs[4:4 + 3 * n_p]
        outs = refs[4 + 3 * n_p:]
        outs[4 * n_p][...] = rp_ref[0, 3:4, RET_W:RET_W + 1]
        grads = [rp_ref[0, 0:1, :], rp_ref[0, 1:2, :], rp_ref[0, 2:3, :], rp_ref[0, 3:4, 0:RET_W],
                 rp_ref[1, 0:HEADS, 0:128], rp_ref[1, 0:HEADS, 128:256], rp_ref[1, 0:HEADS, 256:384],
                 rws_ref[...], gcw_ref[...], None]
        for p in range(n_p):
            w_ref, m_ref, v_ref = ins[3 * p:3 * p + 3]
            o = outs[4 * p:4 * p + 4]
            if p == n_p - 1:
                for hf in range(2):
                    cs = slice(hf * D_FF, (hf + 1) * D_FF)
                    g = rcv_ref[hf, 3:4, :]
                    res = (g,) + _adam_math(w_ref[:, cs], g, m_ref[:, cs], v_ref[:, cs])
                    for t in range(4):
                        o[t][:, cs] = res[t]
                continue
            lead = w_ref.ndim > grads[p].ndim
            rd = (lambda r: r[0]) if lead else (lambda r: r[...])
            res = (grads[p],) + _adam_math(rd(w_ref), grads[p], rd(m_ref), rd(v_ref))
            for t in range(4):
                if lead:
                    o[t][0] = res[t]
                else:
                    o[t][...] = res[t]

    vm = pl.BlockSpec(memory_space=pltpu.VMEM)
    flat = [a for tr in params for a in tr]
    out_shape = tuple(jax.ShapeDtypeStruct(tr[0].shape, F32) for tr in params for _ in range(4)) + (jax.ShapeDtypeStruct((1, 1), F32),)
    res = pl.pallas_call(
        body, name="adamw_small", out_shape=out_shape, in_specs=[vm] * (4 + len(flat)), out_specs=(vm,) * len(out_shape),
        compiler_params=_cparams(),
    )(rp, rws, rcv, gcw, *flat)
    return [res[4 * p:4 * p + 4] for p in range(n_p)], res[4 * n_p]


def kernel(x, mix_norm_g, w_in, ret_norm_g, sgu_ln_g, sgu_ln_b, sgu_w_s, sgu_b_s, w_out, ffn_norm_g, w_up, conv_w, conv_b, w_down, final_norm_g, loss_target, m_mix_norm_g, m_w_in, m_ret_norm_g, m_sgu_ln_g, m_sgu_ln_b, m_sgu_w_s, m_sgu_b_s, m_w_out, m_ffn_norm_g, m_w_up, m_conv_w, m_conv_b, m_w_down, m_final_norm_g, v_mix_norm_g, v_w_in, v_ret_norm_g, v_sgu_ln_g, v_sgu_ln_b, v_sgu_w_s, v_sgu_b_s, v_w_out, v_ffn_norm_g, v_w_up, v_conv_w, v_conv_b, v_w_down, v_final_norm_g):
    xs = x[0]
    tgt = loss_target[0]
    grn = ret_norm_g.reshape(1, RET_W)
    lng = sgu_ln_g.reshape(1, SGU_W)
    lnb = sgu_ln_b.reshape(1, SGU_W)
    ws = sgu_w_s[0]
    bsb = jnp.broadcast_to(sgu_b_s[0][:, :, None], (HEADS, CHUNK, HEAD_DIM))
    gf = final_norm_g.reshape(1, D_MODEL)
    me = 4 * lax.axis_index("x") + 2 * lax.axis_index("y") + lax.axis_index("c")
    tr = lambda a: jnp.transpose(a[0])[None]
    tr_cw = lambda a: jnp.transpose(a, (1, 0, 2))

    proj, h1, cos2, sin2, win_g, cw_sh, wout_g, wdn_g, su = _fwd_proj(
        xs, mix_norm_g, _rope_freq(), w_in[0], w_out[0], tr(w_up)[0], w_down[0], tr_cw(conv_w))
    cw_g = jnp.transpose(cw_sh, (1, 0, 2)).reshape(8, 2 * D_FF)
    x2, mixcat, o, sprev, wup_g = _fwd_mix(xs, proj, wout_g, grn, lng, lnb, ws, bsb, su)
    h2, up_pre, u_conv, act, x3, loss_parts = _fwd_ffn(x2, ffn_norm_g, wup_g, cw_g, conv_b, wdn_g, gf, tgt)

    dx3, dpre, dx2, dgf, dg2, dcv = _bwd_ffn(x3, tgt, gf, x2, ffn_norm_g, up_pre, u_conv, wup_g, cw_g, wdn_g)
    band = 512
    (gout_p,) = _wgrad("wgrad_out", mixcat, dx2, tn=512)
    gdn_p, g_out = _wgrad("wgrad_down", act, dx3, tm=FF_TILE, tn=512, hosted=[(W_OUT, gout_p)])
    gup_p, g_dn = _wgrad("wgrad_up", dpre, h2, tm=FF_TILE, tn=512, hosted=[(W_DOWN, gdn_p)])
    dproj, dgrn, dlng, dlnb, dws, dbs, g_up_a = _bwd_mix(
        dx2, proj, o, sprev, wout_g, grn, lng, lnb, ws, bsb, cos2, sin2, [((W_UP, 0, band), gup_p)])
    gin_p, g_up_b = _wgrad("wgrad_in", h1, dproj, tm=512, tn=768, hosted=[((W_UP, band, FF_SHARD - band), gup_p)])
    grad_x, g_in, rp, rws, rcv = _bwd_proj(dproj, win_g, xs, mix_norm_g, dx2, gin_p,
                                           (dg2, dgf, dgrn, dlng, dlnb, dbs, loss_parts, dws, dcv))
    gcw = tr_cw(lax.dynamic_slice(rcv, (me // (N_DEV // 2), 0, (me % (N_DEV // 2)) * FF_SHARD), (1, 3, FF_SHARD)))

    table = {}
    big, (grad_x,) = _adamw([(w_in, [g_in], m_w_in, v_w_in), (w_out, [g_out], m_w_out, v_w_out),
                             (tr(w_up), [g_up_a, g_up_b], tr(m_w_up), tr(v_w_up)), (w_down, [g_dn], m_w_down, v_w_down)],
                            [grad_x], n_steps=4)
    table.update(zip(("w_in", "w_out", "w_up", "w_down"), big))
    table["w_up"] = tuple(tr(a) for a in table["w_up"])
    row = lambda a: a.reshape(1, D_MODEL)
    names_small = ["mix_norm_g", "ffn_norm_g", "final_norm_g", "ret_norm_g", "sgu_ln_g", "sgu_ln_b", "sgu_b_s", "sgu_w_s",
                   "conv_w", "conv_b"]
    params = [(mix_norm_g, m_mix_norm_g, v_mix_norm_g), (ffn_norm_g, m_ffn_norm_g, v_ffn_norm_g),
              (row(final_norm_g), row(m_final_norm_g), row(v_final_norm_g)), (ret_norm_g, m_ret_norm_g, v_ret_norm_g),
              (sgu_ln_g, m_sgu_ln_g, v_sgu_ln_g), (sgu_ln_b, m_sgu_ln_b, v_sgu_ln_b), (sgu_b_s, m_sgu_b_s, v_sgu_b_s),
              (sgu_w_s, m_sgu_w_s, v_sgu_w_s), (tr_cw(conv_w), tr_cw(m_conv_w), tr_cw(v_conv_w)), (conv_b, m_conv_b, v_conv_b)]
    small, loss = _adamw_small(rp, rws, rcv, gcw, params)
    for n, res in zip(names_small, small):
        table[n] = res
    table["final_norm_g"] = tuple(a.reshape(D_MODEL) for a in table["final_norm_g"])
    table["conv_w"] = tuple(tr_cw(a) for a in table["conv_w"])

    order = ["mix_norm_g", "w_in", "ret_norm_g", "sgu_ln_g", "sgu_ln_b", "sgu_w_s", "sgu_b_s", "w_out", "ffn_norm_g", "w_up",
             "conv_w", "conv_b", "w_down", "final_norm_g"]
    outs = [loss.reshape(()), grad_x[None]]
    for col in range(4):
        outs += [table[n][col] for n in order]
    return tuple(outs)
```

```python
import functools
import math

import jax
import jax.numpy as jnp
import numpy as np
from jax import lax
from jax.experimental import pallas as pl
from jax.experimental.pallas import tpu as pltpu

F32 = jnp.float32
BF16 = jnp.bfloat16
MESH = pl.DeviceIdType.MESH

N_DEV = 8
SEQ = 2048
D_MODEL = 1024
CHUNK = 128
N_CHUNK = SEQ // CHUNK
HEADS = 4
HEAD_DIM = 128
RET_W = 512
SGU_W = 512
PROJ_W = 3072
D_FF = 2816
FF_SHARD = 704
FF_TILE = 1408
FF_TILES = ((0, D_FF),)
IN_SHARD = PROJ_W // N_DEV
OUT_SHARD = D_MODEL // N_DEV
DOWN_SHARD = D_FF // N_DEV
TM = 256
N_TB = SEQ // TM
FWD_PROJ_PASS_AT = 5
FWD_MIX_PASS_AT = 10
EPS = 1e-6
ROPE_BASE = 10000.0
K_SCALE = HEAD_DIM ** -0.5
INV_SQRT2 = 0.7071067811865476
INV_SQRT_2PI = 0.3989422804014327

ADAM_LR = 0.001
ADAM_B1 = 0.9
ADAM_B2 = 0.999
ADAM_EPS = 1e-08
ADAM_WD = 0.01
ADAM_STEP = 10

VMEM_LIMIT = 56 * 1024 * 1024


def _cparams(sem=None, vmem=VMEM_LIMIT, collective=None):
    return pltpu.CompilerParams(dimension_semantics=sem, vmem_limit_bytes=vmem, collective_id=collective)


COLLECTIVE = {name: k for k, name in enumerate(("fwd_proj", "fwd_mix", "wgrad_down", "wgrad_up", "bwd_mix", "wgrad_in", "bwd_proj"))}


class _Meet:
    def __init__(self, diagonal):
        x, y, c = lax.axis_index("x"), lax.axis_index("y"), lax.axis_index("c")
        self.peers = [(x, y, 1 - c), (1 - x, y, c), (x, 1 - y, c)] + ([(1 - x, 1 - y, c)] if diagonal else [])

    def signal(self):
        for peer in self.peers:
            pl.semaphore_signal(pltpu.get_barrier_semaphore(), inc=1, device_id=peer, device_id_type=MESH)

    def wait(self):
        pl.semaphore_wait(pltpu.get_barrier_semaphore(), len(self.peers))


def _resident(shape):
    nd = len(shape)
    return pl.BlockSpec(shape, lambda *_: (0,) * nd, pipeline_mode=pl.Buffered(1))


def _dot(a, b):
    return jnp.dot(a, b, preferred_element_type=F32)


def _dot_nt(a, b):
    return lax.dot_general(a, b, (((1,), (1,)), ((), ())), preferred_element_type=F32)


def _dot_tn(a, b):
    return lax.dot_general(a, b, (((0,), (0,)), ((), ())), preferred_element_type=F32)


def _sigmoid(x):
    return 1.0 / (1.0 + jnp.exp(-x))


def _gelu(x):
    return 0.5 * x * (1.0 + lax.erf(x * INV_SQRT2))


def _gelu_grad(x):
    return 0.5 * (1.0 + lax.erf(x * INV_SQRT2)) + x * (jnp.exp(-0.5 * x * x) * INV_SQRT_2PI)


def _rot(xh, cos2, sin2):
    return xh * cos2 + pltpu.roll(xh, HEAD_DIM // 2, 1) * sin2


def _rot_t(dh, cos2, sin2):
    return dh * cos2 + pltpu.roll(dh * sin2, HEAD_DIM // 2, 1)


def _rope_freq():
    half = HEAD_DIM // 2
    inv_freq = jnp.power(ROPE_BASE, -jnp.arange(half, dtype=F32) / half)
    return jnp.concatenate([inv_freq, inv_freq])[None, :]


def _rope_block(inv2, first_row):
    pos = (lax.broadcasted_iota(jnp.int32, (TM, HEAD_DIM), 0) + first_row).astype(F32)
    ang = pos * inv2
    sin = jnp.sin(ang)
    lane = lax.broadcasted_iota(jnp.int32, (TM, HEAD_DIM), 1)
    return jnp.cos(ang), jnp.where(lane < HEAD_DIM // 2, -sin, sin)


def _log_gamma():
    return np.log(np.float32(1.0) - np.power(np.float32(2.0), -5.0 - np.arange(HEADS, dtype=np.float32))).astype(np.float32)


def _fill_decay(mask_ref, qd_ref, kd_ref):
    assert HEAD_DIM == CHUNK
    lg = _log_gamma()
    t = lax.broadcasted_iota(jnp.int32, (CHUNK, CHUNK), 0).astype(F32)
    diff = t - lax.broadcasted_iota(jnp.int32, (CHUNK, CHUNK), 1).astype(F32)
    for h in range(HEADS):
        mask_ref[h] = jnp.where(diff >= 0.0, jnp.exp(float(lg[h]) * jnp.maximum(diff, 0.0)), 0.0)
        qd_ref[h] = jnp.exp(float(lg[h]) * (t + 1.0))
        kd_ref[h] = jnp.exp(float(lg[h]) * (CHUNK - 1.0 - t))


def _chunk_decay():
    lg = _log_gamma()
    return [float(np.exp(lg[h] * np.float32(CHUNK))) for h in range(HEADS)]


W_IN, W_OUT, W_UP, W_DOWN, W_CONV = range(5)
GATHERED = {W_IN: ((D_MODEL, PROJ_W), BF16), W_OUT: ((D_MODEL, D_MODEL), BF16), W_UP: ((2 * D_FF, D_MODEL), BF16),
            W_DOWN: ((D_FF, D_MODEL), BF16), W_CONV: ((N_DEV, 8, FF_SHARD), F32)}
SHARD = {W_IN: (D_MODEL, IN_SHARD), W_OUT: (OUT_SHARD, D_MODEL), W_UP: (FF_SHARD, D_MODEL), W_DOWN: (DOWN_SHARD, D_MODEL),
         W_CONV: (8, FF_SHARD)}


class _Gather:
    N_SEMS = 9

    def __init__(self, ids, stages, gathered, send_sems, recv_sems, local_sems):
        self.ids, self.stages, self.gathered = ids, stages, gathered
        self.send_sems, self.recv_sems, self.local_sems = send_sems, recv_sems, local_sems
        self.x, self.y, self.c = lax.axis_index("x"), lax.axis_index("y"), lax.axis_index("c")
        self.me = (self.x, self.y, self.c)
        self.sibling = (self.x, self.y, 1 - self.c)
        self.chips = [(1 - self.x, self.y), (self.x, 1 - self.y), (1 - self.x, 1 - self.y)]

    def slot(self, n, px, py, pc):
        dev = 4 * px + 2 * py + pc
        w, g = self.ids[n], self.gathered[n]
        if w == W_IN:
            return g.at[:, pl.ds(pl.multiple_of(dev * IN_SHARD, 128), IN_SHARD)]
        if w == W_OUT:
            return g.at[pl.ds(pl.multiple_of(dev * OUT_SHARD, 128), OUT_SHARD), :]
        if w == W_DOWN:
            return g.at[pl.ds(pl.multiple_of(dev * DOWN_SHARD, 32), DOWN_SHARD), :]
        if w == W_UP:
            return g.at[pl.ds(pl.multiple_of(dev * FF_SHARD, 32), FF_SHARD), :]
        return g.at[dev]

    def half(self, n, px, py, pc, h):
        dev = 4 * px + 2 * py + pc
        w, g = self.ids[n], self.gathered[n]
        if w == W_IN:
            return g.at[pl.ds(h * (D_MODEL // 2), D_MODEL // 2), pl.ds(pl.multiple_of(dev * IN_SHARD, 128), IN_SHARD)]
        rows = SHARD[w][0] // 2
        return g.at[pl.ds(pl.multiple_of(dev * SHARD[w][0] + h * rows, 16), rows), :]

    def tree(self, n):
        return self.ids[n] != W_CONV

    def copy(self, n, k, block, to, src=None, h=None):
        ref = self.slot(n, *block) if h is None else self.half(n, *block, h)
        return pltpu.make_async_remote_copy(
            src_ref=ref if src is None else src, dst_ref=ref,
            send_sem=self.send_sems.at[n, k], recv_sem=self.recv_sems.at[n, k], device_id=to, device_id_type=MESH)

    def _mine(self):
        return [pltpu.make_async_copy(self.stages[n], self.slot(n, *self.me), self.local_sems.at[n]) for n in range(len(self.ids))]

    def _first(self):
        out = []
        for n in range(len(self.ids)):
            out.append(self.copy(n, 0, self.me, self.sibling, src=self.stages[n]))
            out += [self.copy(n, 1 + j, self.me, (*chip, self.c), src=self.stages[n])
                    for j, chip in enumerate(self.chips[:2] if self.tree(n) else self.chips)]
        return out

    def start(self):
        for cp in self._mine() + self._first():
            cp.start()

    def _passed(self, j):
        dev = (*self.chips[j], self.c)
        out = []
        for n in range(len(self.ids)):
            if not self.tree(n):
                out.append(self.copy(n, 4 + j, dev, self.sibling))
            elif j < 2:
                out += [self.copy(n, 3 + j, dev, (*self.chips[1 - j], self.c), h=j), self.copy(n, 5 + j, dev, self.sibling)]
            else:
                out += [self.copy(n, 7, dev, self.sibling, h=0), self.copy(n, 8, dev, self.sibling, h=1)]
        return out

    def near(self):
        for j in range(2):
            dev = (*self.chips[j], self.c)
            for n in range(len(self.ids)):
                self.copy(n, 1 + j, dev, self.me).wait_recv()
            for cp in self._passed(j):
                cp.start()

    def finish(self):
        dev = (*self.chips[2], self.c)
        for n in range(len(self.ids)):
            if self.tree(n):
                self.copy(n, 3, dev, self.me, h=0).wait_recv()
                self.copy(n, 4, dev, self.me, h=1).wait_recv()
            else:
                self.copy(n, 3, dev, self.me).wait_recv()
        for cp in self._passed(2):
            cp.start()
        for n in range(len(self.ids)):
            self.copy(n, 0, self.sibling, self.me).wait_recv()
            for j, chip in enumerate(self.chips):
                dev = (*chip, 1 - self.c)
                if not self.tree(n):
                    self.copy(n, 4 + j, dev, self.me).wait_recv()
                elif j < 2:
                    self.copy(n, 5 + j, dev, self.me).wait_recv()
                else:
                    self.copy(n, 7, dev, self.me, h=0).wait_recv()
                    self.copy(n, 8, dev, self.me, h=1).wait_recv()
        for cp in self._mine():
            cp.wait()
        for cp in self._first() + self._passed(0) + self._passed(1) + self._passed(2):
            cp.wait_send()


def _gather_scratch(n):
    return [pltpu.SemaphoreType.DMA((n, _Gather.N_SEMS)), pltpu.SemaphoreType.DMA((n, _Gather.N_SEMS)), pltpu.SemaphoreType.DMA((n,))]


def _gathered_shapes(ids):
    return tuple(jax.ShapeDtypeStruct(*GATHERED[w]) for w in ids)


def _fwd_proj(x, g1, inv2, w_in, w_out, w_up, w_down, conv_w):
    ids_a, ids_b = [W_IN, W_CONV], [W_OUT, W_DOWN]

    def body(x_ref, g_ref, inv_ref, in_hbm, out_hbm, up_hbm, dn_hbm, cw_ref,
             proj_ref, h1_ref, cos_ref, sin_ref, gin, gcw, gout, gdn, su_ref,
             w_vm, s_in, s_cw, s_out, s_dn, f_in, f_out, f_up, f_dn, ld_sems,
             a_send, a_recv, a_local, b_send, b_recv, b_local):
        ag_a = _Gather(ids_a, [s_in, s_cw], [gin, gcw], a_send, a_recv, a_local)
        ag_b = _Gather(ids_b, [s_out, s_dn], [gout, gdn], b_send, b_recv, b_local)

        @pl.when(pl.program_id(0) == 0)
        def _():
            meet = _Meet(diagonal=True)
            meet.signal()
            loads = [pltpu.make_async_copy(src, dst, ld_sems.at[i])
                     for i, (src, dst) in enumerate(((in_hbm, f_in), (out_hbm, f_out), (dn_hbm, f_dn), (up_hbm, f_up)))]
            for cp in loads:
                cp.start()
            s_cw[...] = jnp.zeros_like(s_cw)
            for k in range(3):
                s_cw[k:k + 1, :] = cw_ref[k]
            loads[0].wait()
            s_in[...] = f_in[...].astype(BF16)
            meet.wait()
            ag_a.start()
            loads[1].wait()
            s_out[...] = f_out[...].astype(BF16)
            loads[2].wait()
            s_dn[...] = f_dn[...].astype(BF16)
            ag_a.near()
            ag_b.start()
            loads[3].wait()
            su_ref[...] = f_up[...].astype(BF16)
            ag_a.finish()
            fill = pltpu.make_async_copy(gin, w_vm, ld_sems.at[4])
            fill.start()
            fill.wait()

        pl.when(pl.program_id(0) == FWD_PROJ_PASS_AT)(ag_b.near)

        xb = x_ref[...]
        r = lax.rsqrt(jnp.mean(xb * xb, axis=-1, keepdims=True) + EPS)
        h = ((xb * r) * g_ref[...]).astype(BF16)
        h1_ref[...] = h
        p = _dot(h, w_vm[...])
        c2, s2 = _rope_block(inv_ref[...], pl.program_id(0) * TM)
        cos_ref[...], sin_ref[...] = c2, s2
        for hd in range(HEADS):
            sl = slice(hd * HEAD_DIM, (hd + 1) * HEAD_DIM)
            proj_ref[:, sl] = _rot(p[:, sl], c2, s2)
            ks = slice(RET_W + hd * HEAD_DIM, RET_W + (hd + 1) * HEAD_DIM)
            proj_ref[:, ks] = _rot(p[:, ks], c2, s2) * K_SCALE
        proj_ref[:, 2 * RET_W:] = p[:, 2 * RET_W:]

        pl.when(pl.program_id(0) == N_TB - 1)(ag_b.finish)

    tok = lambda w: pl.BlockSpec((TM, w), lambda i: (i, 0))
    hbm = pl.BlockSpec(memory_space=pl.ANY)
    vm = pl.BlockSpec(memory_space=pltpu.VMEM)
    return pl.pallas_call(
        body, name="fwd_proj", grid=(N_TB,),
        out_shape=(jax.ShapeDtypeStruct((SEQ, PROJ_W), F32), jax.ShapeDtypeStruct((SEQ, D_MODEL), BF16),
                   jax.ShapeDtypeStruct((SEQ, HEAD_DIM), F32), jax.ShapeDtypeStruct((SEQ, HEAD_DIM), F32))
        + _gathered_shapes(ids_a + ids_b) + (jax.ShapeDtypeStruct(SHARD[W_UP], BF16),),
        in_specs=[tok(D_MODEL), _resident((1, D_MODEL)), _resident((1, HEAD_DIM)), hbm, hbm, hbm, hbm, vm],
        out_specs=(tok(PROJ_W), tok(D_MODEL), tok(HEAD_DIM), tok(HEAD_DIM), hbm, hbm, hbm, hbm, vm),
        scratch_shapes=[pltpu.VMEM((D_MODEL, PROJ_W), BF16), pltpu.VMEM(SHARD[W_IN], BF16), pltpu.VMEM(SHARD[W_CONV], F32),
                        pltpu.VMEM(SHARD[W_OUT], BF16), pltpu.VMEM(SHARD[W_DOWN], BF16),
                        pltpu.VMEM(SHARD[W_IN], F32), pltpu.VMEM(SHARD[W_OUT], F32), pltpu.VMEM(SHARD[W_UP], F32),
                        pltpu.VMEM(SHARD[W_DOWN], F32), pltpu.SemaphoreType.DMA((5,))]
        + _gather_scratch(len(ids_a)) + _gather_scratch(len(ids_b)),
        compiler_params=_cparams(("arbitrary",), collective=COLLECTIVE["fwd_proj"]),
    )(x, g1, inv2, w_in, w_out, w_up, w_down, conv_w)


def _causal(w):
    r = lax.broadcasted_iota(jnp.int32, (CHUNK, CHUNK), 0)
    c = lax.broadcasted_iota(jnp.int32, (CHUNK, CHUNK), 1)
    return jnp.where(r >= c, w, 0.0)


def _fwd_mix(x, proj, wout_g, grn, lng, lnb, ws, bsb, su):
    cdec = _chunk_decay()
    ids = [W_UP]

    def body(x_ref, p_ref, w_ref, grn_ref, lng_ref, lnb_ref, ws_ref, bsb_ref, su_ref,
             x2_ref, cat_ref, o_ref, sp_ref, gup, state, m_ref, qd_ref, kd_ref, send_sems, recv_sems, local_sems):
        ag = _Gather(ids, [su_ref], [gup], send_sems, recv_sems, local_sems)

        @pl.when(pl.program_id(0) == 0)
        def _():
            meet = _Meet(diagonal=False)
            meet.signal()
            state[...] = jnp.zeros_like(state)
            _fill_decay(m_ref, qd_ref, kd_ref)
            meet.wait()
            ag.start()

        for h in range(HEADS):
            sl = slice(h * HEAD_DIM, (h + 1) * HEAD_DIM)
            q = p_ref[:, sl]
            k = p_ref[:, RET_W + h * HEAD_DIM:RET_W + (h + 1) * HEAD_DIM]
            v = p_ref[:, 2 * RET_W + h * HEAD_DIM:2 * RET_W + (h + 1) * HEAD_DIM]
            g = p_ref[:, 3 * RET_W + h * HEAD_DIM:3 * RET_W + (h + 1) * HEAD_DIM]
            qb, kb, vb = q.astype(BF16), k.astype(BF16), v.astype(BF16)
            a = _dot_nt(qb, kb) * m_ref[h]
            spb = state[h].astype(BF16)
            sp_ref[0, h] = spb
            o = _dot(a.astype(BF16), vb) + _dot((q * qd_ref[h]).astype(BF16), spb)
            state[h] = state[h] * cdec[h] + _dot_tn((k * kd_ref[h]).astype(BF16), vb)
            o_ref[:, sl] = o
            rinv = lax.rsqrt(jnp.mean(o * o, axis=-1, keepdims=True) + EPS)
            rn = (o * rinv) * grn_ref[:, sl]
            cat_ref[:, sl] = ((g * _sigmoid(g)) * rn).astype(BF16)
        for gi in range(HEADS):
            sl = slice(gi * HEAD_DIM, (gi + 1) * HEAD_DIM)
            u = p_ref[:, 4 * RET_W + gi * HEAD_DIM:4 * RET_W + (gi + 1) * HEAD_DIM]
            sv = p_ref[:, 4 * RET_W + SGU_W + gi * HEAD_DIM:4 * RET_W + SGU_W + (gi + 1) * HEAD_DIM]
            gv = _gelu(sv)
            xc = gv - jnp.mean(gv, axis=-1, keepdims=True)
            vn = (xc * lax.rsqrt(jnp.mean(xc * xc, axis=-1, keepdims=True) + EPS)) * lng_ref[:, sl] + lnb_ref[:, sl]
            mixed = _dot(_causal(ws_ref[gi]).astype(BF16), vn.astype(BF16)) + bsb_ref[gi]
            cat_ref[:, RET_W + gi * HEAD_DIM:RET_W + (gi + 1) * HEAD_DIM] = (_gelu(u) * mixed).astype(BF16)
        x2_ref[...] = x_ref[...] + _dot(cat_ref[...], w_ref[...])

        pl.when(pl.program_id(0) == FWD_MIX_PASS_AT)(ag.near)
        pl.when(pl.program_id(0) == N_CHUNK - 1)(ag.finish)

    ch = lambda w: pl.BlockSpec((CHUNK, w), lambda i: (i, 0))
    hcc = (HEADS, CHUNK, CHUNK)
    hbm = pl.BlockSpec(memory_space=pl.ANY)
    return pl.pallas_call(
        body, name="fwd_mix", grid=(N_CHUNK,),
        out_shape=(jax.ShapeDtypeStruct((SEQ, D_MODEL), F32), jax.ShapeDtypeStruct((SEQ, D_MODEL), BF16),
                   jax.ShapeDtypeStruct((SEQ, RET_W), F32), jax.ShapeDtypeStruct((N_CHUNK, HEADS, HEAD_DIM, HEAD_DIM), BF16))
        + _gathered_shapes(ids),
        in_specs=[ch(D_MODEL), ch(PROJ_W), _resident((D_MODEL, D_MODEL)), _resident((1, RET_W)), _resident((1, SGU_W)),
                  _resident((1, SGU_W)), _resident(hcc), _resident(hcc), hbm],
        out_specs=(ch(D_MODEL), ch(D_MODEL), ch(RET_W), pl.BlockSpec((1, HEADS, HEAD_DIM, HEAD_DIM), lambda i: (i, 0, 0, 0)), hbm),
        scratch_shapes=[pltpu.VMEM((HEADS, HEAD_DIM, HEAD_DIM), F32)] + [pltpu.VMEM(hcc, F32)] * 3 + _gather_scratch(len(ids)),
        compiler_params=_cparams(("arbitrary",), collective=COLLECTIVE["fwd_mix"]),
    )(x, proj, wout_g, grn, lng, lnb, ws, bsb, su)


def _conv_taps(p, prev8):
    row = lax.broadcasted_iota(jnp.int32, p.shape, 0)
    p1 = jnp.where(row == 0, prev8[7:8, :], pltpu.roll(p, 1, 0))
    p2 = jnp.where(row == 0, prev8[6:7, :], jnp.where(row == 1, prev8[7:8, :], pltpu.roll(p, 2, 0)))
    return p1, p2


def _fwd_ffn(x2, g2, wup_g, cw_g, cb_g, wdn_g, gf, tgt):
    def body(x_ref, g_ref, wu_ref, cw_ref, cb_ref, wd_ref, gf_ref, t_ref, h2_ref, up_ref, u_ref, act_ref, x3_ref, loss_ref, carry):
        @pl.when(pl.program_id(0) == 0)
        def _():
            carry[...] = jnp.zeros_like(carry)

        xb = x_ref[...]
        r = lax.rsqrt(jnp.mean(xb * xb, axis=-1, keepdims=True) + EPS)
        h = ((xb * r) * g_ref[...]).astype(BF16)
        h2_ref[...] = h
        acc = xb
        for t0, tw in FF_TILES:
            u = []
            for c0 in (t0, D_FF + t0):
                cs = slice(c0, c0 + tw)
                p = _dot_nt(h, wu_ref[pl.ds(c0, tw), :])
                up_ref[:, cs] = p.astype(BF16)
                p1, p2 = _conv_taps(p, carry[:, cs])
                carry[:, cs] = p[TM - 8:, :]
                us = p2 * cw_ref[0:1, cs] + p1 * cw_ref[1:2, cs] + p * cw_ref[2:3, cs] + cb_ref[:, cs]
                u_ref[:, cs] = us.astype(BF16)
                u.append(us)
            a = ((u[0] * _sigmoid(u[0])) * u[1]).astype(BF16)
            act_ref[:, t0:t0 + tw] = a
            acc = acc + _dot(a, wd_ref[pl.ds(t0, tw), :])
        x3_ref[...] = acc
        r3 = lax.rsqrt(jnp.mean(acc * acc, axis=-1, keepdims=True) + EPS)
        diff = (acc * r3) * gf_ref[...] - t_ref[...]
        loss_ref[...] = jnp.full(loss_ref.shape, 0.5 * jnp.sum(jnp.mean(diff * diff, axis=-1)), F32)

    tok = lambda w: pl.BlockSpec((TM, w), lambda i: (i, 0))
    return pl.pallas_call(
        body, name="fwd_ffn", grid=(N_TB,),
        out_shape=(jax.ShapeDtypeStruct((SEQ, D_MODEL), BF16), jax.ShapeDtypeStruct((SEQ, 2 * D_FF), BF16),
                   jax.ShapeDtypeStruct((SEQ, 2 * D_FF), BF16),
                   jax.ShapeDtypeStruct((SEQ, D_FF), BF16), jax.ShapeDtypeStruct((SEQ, D_MODEL), F32),
                   jax.ShapeDtypeStruct((N_TB, 8, 128), F32)),
        in_specs=[tok(D_MODEL), _resident((1, D_MODEL)), _resident((2 * D_FF, D_MODEL)), _resident((8, 2 * D_FF)),
                  _resident((1, 2 * D_FF)), _resident((D_FF, D_MODEL)), _resident((1, D_MODEL)), tok(D_MODEL)],
        out_specs=(tok(D_MODEL), tok(2 * D_FF), tok(2 * D_FF), tok(D_FF), tok(D_MODEL),
                   pl.BlockSpec((1, 8, 128), lambda i: (i, 0, 0))),
        scratch_shapes=[pltpu.VMEM((8, 2 * D_FF), F32)],
        compiler_params=_cparams(("arbitrary",)),
    )(x2, g2, wup_g, cw_g, cb_g, wdn_g, gf, tgt)


def _bwd_ffn(x3, tgt, gf, x2, g2, up_pre, u_conv, wup_g, cw_g, wdn_g):
    def body(x3_ref, t_ref, gf_ref, x2_ref, g2_ref, up_ref, u_ref, wu_ref, cw_ref, wd_ref,
             dx3_ref, dpre_ref, dx2_ref, dgf_ref, dg2_ref, dcv_ref, nxt):
        i = pl.program_id(0)

        @pl.when(i == 0)
        def _():
            nxt[...] = jnp.zeros_like(nxt)
            dgf_ref[...] = jnp.zeros_like(dgf_ref)
            dg2_ref[...] = jnp.zeros_like(dg2_ref)
            dcv_ref[...] = jnp.zeros_like(dcv_ref)

        x3 = x3_ref[...]
        r3 = lax.rsqrt(jnp.mean(x3 * x3, axis=-1, keepdims=True) + EPS)
        xh3 = x3 * r3
        dy = (xh3 * gf_ref[...] - t_ref[...]) * (1.0 / D_MODEL)
        dgf_ref[0:1, :] += jnp.sum(dy * xh3, axis=0, keepdims=True)
        t3 = dy * gf_ref[...]
        dx3 = r3 * (t3 - xh3 * jnp.mean(t3 * xh3, axis=-1, keepdims=True))
        dx3b = dx3.astype(BF16)
        dx3_ref[...] = dx3b
        dh2 = jnp.zeros((TM, D_MODEL), F32)
        for t0, tw in FF_TILES:
            row = lax.broadcasted_iota(jnp.int32, (TM, tw), 0)
            ts = slice(t0, t0 + tw)
            dact = _dot_nt(dx3b, wd_ref[pl.ds(t0, tw), :])
            ua = u_ref[:, ts].astype(F32)
            ub = u_ref[:, D_FF + t0:D_FF + t0 + tw].astype(F32)
            sg = _sigmoid(ua)
            du = [dact * ub * (sg * (1.0 + ua * (1.0 - sg))), dact * (ua * sg)]
            for n in range(2):
                d = du[n]
                c0 = n * D_FF + t0
                cs = slice(c0, c0 + tw)
                nx = nxt[:, cs]
                n1 = jnp.where(row == TM - 1, nx[0:1, :], pltpu.roll(d, TM - 1, 0))
                n2 = jnp.where(row == TM - 2, nx[0:1, :], jnp.where(row == TM - 1, nx[1:2, :], pltpu.roll(d, TM - 2, 0)))
                nxt[:, cs] = d[0:8, :]
                dp = (d * cw_ref[2:3, cs] + n1 * cw_ref[1:2, cs] + n2 * cw_ref[0:1, cs]).astype(BF16)
                dpre_ref[:, cs] = dp
                p = up_ref[:, cs].astype(F32)
                dcv_ref[n, 0:1, ts] += jnp.sum(n2 * p, axis=0, keepdims=True)
                dcv_ref[n, 1:2, ts] += jnp.sum(n1 * p, axis=0, keepdims=True)
                dcv_ref[n, 2:3, ts] += jnp.sum(d * p, axis=0, keepdims=True)
                dcv_ref[n, 3:4, ts] += jnp.sum(d, axis=0, keepdims=True)
                dh2 = dh2 + _dot(dp, wu_ref[pl.ds(c0, tw), :])
        x2 = x2_ref[...]
        r2 = lax.rsqrt(jnp.mean(x2 * x2, axis=-1, keepdims=True) + EPS)
        xh2 = x2 * r2
        dg2_ref[0:1, :] += jnp.sum(dh2 * xh2, axis=0, keepdims=True)
        t2 = dh2 * g2_ref[...]
        dx2_ref[...] = dx3 + r2 * (t2 - xh2 * jnp.mean(t2 * xh2, axis=-1, keepdims=True))

    rev = lambda w: pl.BlockSpec((TM, w), lambda i: (N_TB - 1 - i, 0))
    acc = lambda s: pl.BlockSpec(s, lambda i: (0,) * len(s))
    return pl.pallas_call(
        body, name="bwd_ffn", grid=(N_TB,),
        out_shape=(jax.ShapeDtypeStruct((SEQ, D_MODEL), BF16), jax.ShapeDtypeStruct((SEQ, 2 * D_FF), BF16),
                   jax.ShapeDtypeStruct((SEQ, D_MODEL), F32), jax.ShapeDtypeStruct((8, D_MODEL), F32),
                   jax.ShapeDtypeStruct((8, D_MODEL), F32), jax.ShapeDtypeStruct((2, 8, D_FF), F32)),
        in_specs=[rev(D_MODEL), rev(D_MODEL), _resident((1, D_MODEL)), rev(D_MODEL), _resident((1, D_MODEL)), rev(2 * D_FF),
                  rev(2 * D_FF), _resident((2 * D_FF, D_MODEL)), _resident((8, 2 * D_FF)), _resident((D_FF, D_MODEL))],
        out_specs=(rev(D_MODEL), rev(2 * D_FF), rev(D_MODEL), acc((8, D_MODEL)), acc((8, D_MODEL)), acc((2, 8, D_FF))),
        scratch_shapes=[pltpu.VMEM((8, 2 * D_FF), F32)],
        compiler_params=_cparams(("arbitrary",)),
    )(x3, tgt, gf, x2, g2, up_pre, u_conv, wup_g, cw_g, wdn_g)


def _bwd_mix(dx2, proj, o, sprev, wout_g, grn, lng, lnb, ws, bsb, cos2, sin2, hosted):
    cdec = _chunk_decay()
    geoms = [g for g, _ in hosted]
    n_h = len(hosted)

    def body(dx2_ref, p_ref, o_ref, sp_ref, w_ref, grn_ref, lng_ref, lnb_ref, ws_ref, bsb_ref, cos_ref, sin_ref, *rest):
        dp_ref, dgrn_ref, dlng_ref, dlnb_ref, dws_ref, dbs_ref = rest[n_h:n_h + 6]
        dstate, dbs_acc, m_ref, qd_ref, kd_ref = rest[2 * n_h + 6:2 * n_h + 11]
        i = pl.program_id(0)
        rs = _Scatters(geoms, rest[:n_h], rest[n_h + 6:2 * n_h + 6], rest[2 * n_h + 11:])
        pl.when(i == 0)(rs.phase1)
        pl.when(i == 3)(rs.phase2)
        pl.when(i == 8)(rs.phase2b)

        @pl.when(i == 0)
        def _():
            _fill_decay(m_ref, qd_ref, kd_ref)
            dstate[...] = jnp.zeros_like(dstate)
            dgrn_ref[...] = jnp.zeros_like(dgrn_ref)
            dlng_ref[...] = jnp.zeros_like(dlng_ref)
            dlnb_ref[...] = jnp.zeros_like(dlnb_ref)
            dws_ref[...] = jnp.zeros_like(dws_ref)
            dbs_ref[...] = jnp.zeros_like(dbs_ref)
            dbs_acc[...] = jnp.zeros_like(dbs_acc)

        dmix = _dot_nt(dx2_ref[...].astype(BF16), w_ref[...])
        for h in range(HEADS):
            sl = slice(h * HEAD_DIM, (h + 1) * HEAD_DIM)
            q = p_ref[:, sl]
            k = p_ref[:, RET_W + h * HEAD_DIM:RET_W + (h + 1) * HEAD_DIM]
            v = p_ref[:, 2 * RET_W + h * HEAD_DIM:2 * RET_W + (h + 1) * HEAD_DIM]
            g = p_ref[:, 3 * RET_W + h * HEAD_DIM:3 * RET_W + (h + 1) * HEAD_DIM]
            o = o_ref[:, sl]
            rinv = lax.rsqrt(jnp.mean(o * o, axis=-1, keepdims=True) + EPS)
            oh = o * rinv
            gr = grn_ref[:, sl]
            sg = _sigmoid(g)
            dret = dmix[:, sl]
            dp_ref[:, 3 * RET_W + h * HEAD_DIM:3 * RET_W + (h + 1) * HEAD_DIM] = (
                dret * (oh * gr) * (sg * (1.0 + g * (1.0 - sg)))).astype(BF16)
            drn = dret * (g * sg)
            dgrn_ref[0:1, sl] += jnp.sum(drn * oh, axis=0, keepdims=True)
            t = drn * gr
            do = rinv * (t - oh * jnp.mean(t * oh, axis=-1, keepdims=True))
            qb, kb, vb, dob = q.astype(BF16), k.astype(BF16), v.astype(BF16), do.astype(BF16)
            m = m_ref[h]
            ab = (_dot_nt(qb, kb) * m).astype(BF16)
            dab = (_dot_nt(dob, vb) * m).astype(BF16)
            spb = sp_ref[0, h]
            dsn = dstate[h]
            dsnb = dsn.astype(BF16)
            qdb = (q * qd_ref[h]).astype(BF16)
            kdb = (k * kd_ref[h]).astype(BF16)
            dq = _dot(dab, kb) + _dot_nt(dob, spb) * qd_ref[h]
            dk = _dot_tn(dab, qb) + _dot_nt(vb, dsnb) * kd_ref[h]
            dv = _dot_tn(ab, dob) + _dot(kdb, dsnb)
            dstate[h] = dsn * cdec[h] + _dot_tn(qdb, dob)
            c2, s2 = cos_ref[...], sin_ref[...]
            dp_ref[:, sl] = _rot_t(dq, c2, s2).astype(BF16)
            dp_ref[:, RET_W + h * HEAD_DIM:RET_W + (h + 1) * HEAD_DIM] = _rot_t(dk * K_SCALE, c2, s2).astype(BF16)
            dp_ref[:, 2 * RET_W + h * HEAD_DIM:2 * RET_W + (h + 1) * HEAD_DIM] = dv.astype(BF16)
        for gi in range(HEADS):
            sl = slice(gi * HEAD_DIM, (gi + 1) * HEAD_DIM)
            u = p_ref[:, 4 * RET_W + gi * HEAD_DIM:4 * RET_W + (gi + 1) * HEAD_DIM]
            sv = p_ref[:, 4 * RET_W + SGU_W + gi * HEAD_DIM:4 * RET_W + SGU_W + (gi + 1) * HEAD_DIM]
            gv = _gelu(sv)
            xc = gv - jnp.mean(gv, axis=-1, keepdims=True)
            rstd = lax.rsqrt(jnp.mean(xc * xc, axis=-1, keepdims=True) + EPS)
            xh = xc * rstd
            lg = lng_ref[:, sl]
            vnb = (xh * lg + lnb_ref[:, sl]).astype(BF16)
            wcb = _causal(ws_ref[gi]).astype(BF16)
            mixed = _dot(wcb, vnb) + bsb_ref[gi]
            dsgu = dmix[:, RET_W + gi * HEAD_DIM:RET_W + (gi + 1) * HEAD_DIM]
            dmixed = dsgu * _gelu(u)
            dmb = dmixed.astype(BF16)
            dws_ref[gi] += _causal(_dot_nt(dmb, vnb))
            dbs_acc[gi] += dmixed
            dvn = _dot_tn(wcb, dmb)
            dlng_ref[gi:gi + 1, :] += jnp.sum(dvn * xh, axis=0, keepdims=True)
            dlnb_ref[gi:gi + 1, :] += jnp.sum(dvn, axis=0, keepdims=True)
            dxh = dvn * lg
            dgv = rstd * (dxh - jnp.mean(dxh, axis=-1, keepdims=True) - xh * jnp.mean(dxh * xh, axis=-1, keepdims=True))
            dp_ref[:, 4 * RET_W + gi * HEAD_DIM:4 * RET_W + (gi + 1) * HEAD_DIM] = (dsgu * mixed * _gelu_grad(u)).astype(BF16)
            dp_ref[:, 4 * RET_W + SGU_W + gi * HEAD_DIM:4 * RET_W + SGU_W + (gi + 1) * HEAD_DIM] = (
                dgv * _gelu_grad(sv)).astype(BF16)

        @pl.when(i == N_CHUNK - 1)
        def _():
            for gi in range(HEADS):
                col = jnp.broadcast_to(jnp.sum(dbs_acc[gi], axis=-1, keepdims=True), (CHUNK, CHUNK))
                dbs_ref[gi:gi + 1, :] = jnp.transpose(col)[0:1, :]
            rs.phase3()

    rev = lambda w: pl.BlockSpec((CHUNK, w), lambda i: (N_CHUNK - 1 - i, 0))
    hcc = (HEADS, CHUNK, CHUNK)
    acc = lambda s: pl.BlockSpec(s, lambda i: (0,) * len(s))
    res = pl.pallas_call(
        body, name="bwd_mix", grid=(N_CHUNK,),
        out_shape=(jax.ShapeDtypeStruct((SEQ, PROJ_W), BF16), jax.ShapeDtypeStruct((8, RET_W), F32),
                   jax.ShapeDtypeStruct((8, HEAD_DIM), F32), jax.ShapeDtypeStruct((8, HEAD_DIM), F32),
                   jax.ShapeDtypeStruct(hcc, F32), jax.ShapeDtypeStruct((8, CHUNK), F32)) + _scatter_out_shapes(geoms),
        in_specs=[rev(D_MODEL), rev(PROJ_W), rev(RET_W),
                  pl.BlockSpec((1, HEADS, HEAD_DIM, HEAD_DIM), lambda i: (N_CHUNK - 1 - i, 0, 0, 0)),
                  _resident((D_MODEL, D_MODEL)), _resident((1, RET_W)), _resident((1, SGU_W)), _resident((1, SGU_W)),
                  _resident(hcc), _resident(hcc), rev(HEAD_DIM), rev(HEAD_DIM)]
        + [pl.BlockSpec(memory_space=pl.ANY)] * n_h,
        out_specs=(rev(PROJ_W), acc((8, RET_W)), acc((8, HEAD_DIM)), acc((8, HEAD_DIM)), acc(hcc), acc((8, CHUNK)))
        + _scatter_out_specs(geoms),
        scratch_shapes=[pltpu.VMEM((HEADS, HEAD_DIM, HEAD_DIM), F32), pltpu.VMEM((HEADS, CHUNK, CHUNK), F32)]
        + [pltpu.VMEM(hcc, F32)] * 3 + _scatter_scratch(geoms),
        compiler_params=_cparams(("arbitrary",), collective=COLLECTIVE["bwd_mix"]),
    )(dx2, proj, o, sprev, wout_g, grn, lng, lnb, ws, bsb, cos2, sin2, *[p for _, p in hosted])
    return tuple(res[:6 + n_h])


def _bwd_proj(dproj, win_g, x, g1, dx2, gin_p, small):
    geoms = [W_IN]
    n_s = len(small)

    def body(dp_ref, w_ref, x_ref, g_ref, dx2_ref, gin_ref, *rest):
        small_refs = rest[:n_s]
        dx_ref, rs_out, rp_ref, rws_ref, rcv_ref, dg_ref = rest[n_s:n_s + 6]
        rs_scratch = rest[n_s + 6:n_s + 6 + N_SCATTER_SCRATCH]
        ar_scratch = rest[n_s + 6 + N_SCATTER_SCRATCH:]
        ar_res = ar_scratch[N_SMALL_SCRATCH:]
        ar = _SmallReduce((dg_ref,) + tuple(small_refs), ar_res, ar_scratch[:N_SMALL_SCRATCH])
        rs = _Scatters(geoms, [gin_ref], [rs_out], rs_scratch)
        pl.when(pl.program_id(0) == 0)(lambda: rs.phase1(diagonal=True))
        pl.when(pl.program_id(0) == 1)(rs.phase2)
        pl.when(pl.program_id(0) == 5)(rs.phase2b)

        @pl.when(pl.program_id(0) == 0)
        def _():
            dg_ref[...] = jnp.zeros_like(dg_ref)

        dh = _dot_nt(dp_ref[...], w_ref[...])
        xb = x_ref[...]
        r = lax.rsqrt(jnp.mean(xb * xb, axis=-1, keepdims=True) + EPS)
        xh = xb * r
        dg_ref[0:1, :] += jnp.sum(dh * xh, axis=0, keepdims=True)
        t = dh * g_ref[...]
        dx_ref[...] = dx2_ref[...] + r * (t - xh * jnp.mean(t * xh, axis=-1, keepdims=True))

        @pl.when(pl.program_id(0) == N_TB - 1)
        def _():
            ar.begin()
            rs.phase3()
            ar.end()
            for o_ref, r_ref in zip((rp_ref, rws_ref, rcv_ref), ar_res):
                o_ref[...] = r_ref[...]

    tok = lambda w: pl.BlockSpec((TM, w), lambda i: (i, 0))
    vm = pl.BlockSpec(memory_space=pltpu.VMEM)
    res = pl.pallas_call(
        body, name="bwd_proj", grid=(N_TB,),
        out_shape=(jax.ShapeDtypeStruct((SEQ, D_MODEL), F32),) + _scatter_out_shapes(geoms)
        + tuple(jax.ShapeDtypeStruct(s, F32) for s in SMALL_FULL),
        in_specs=[tok(PROJ_W), _resident((D_MODEL, PROJ_W)), tok(D_MODEL), _resident((1, D_MODEL)), tok(D_MODEL),
                  pl.BlockSpec(memory_space=pl.ANY)] + [vm] * n_s,
        out_specs=(tok(D_MODEL),) + _scatter_out_specs(geoms) + (vm,) * len(SMALL_FULL),
        scratch_shapes=[pltpu.VMEM((8, D_MODEL), F32)] + _scatter_scratch(geoms) + _small_scratch()
        + [pltpu.VMEM(s, F32) for s in SMALL_FULL],
        compiler_params=_cparams(("arbitrary",), collective=COLLECTIVE["bwd_proj"]),
    )(dproj, win_g, x, g1, dx2, gin_p, *small)
    return res


def _wgrad(name, a, b, tm=None, tn=None, hosted=()):
    m_w, n_w = a.shape[-1], b.shape[-1]
    tm = m_w if tm is None else tm
    tn = n_w if tn is None else tn
    n_steps = (m_w // tm) * (n_w // tn)
    geoms = [g for g, _ in hosted]
    n_h = len(hosted)

    def body(a_ref, b_ref, *rest):
        o_ref = rest[n_h]
        if n_h:
            rs = _Scatters(geoms, rest[:n_h], rest[n_h + 1:2 * n_h + 1], rest[2 * n_h + 1:])
            step = pl.program_id(0) * (n_w // tn) + pl.program_id(1)
            pl.when(step == 0)(rs.phase1)
            pl.when(step == 1)(rs.phase2)
            pl.when(step == n_steps // 2)(rs.phase2b)
        o_ref[...] = _dot_tn(a_ref[...].astype(BF16), b_ref[...].astype(BF16)).astype(BF16)
        if n_h:
            pl.when(step == n_steps - 1)(rs.phase3)

    assert not n_h or n_steps >= 4
    res = pl.pallas_call(
        body, name=name, grid=(m_w // tm, n_w // tn),
        out_shape=(jax.ShapeDtypeStruct((m_w, n_w), BF16),) + _scatter_out_shapes(geoms),
        in_specs=[pl.BlockSpec((SEQ, tm), lambda i, j: (0, i)), pl.BlockSpec((SEQ, tn), lambda i, j: (0, j))]
        + [pl.BlockSpec(memory_space=pl.ANY)] * n_h,
        out_specs=(pl.BlockSpec((tm, tn), lambda i, j: (i, j)),) + _scatter_out_specs(geoms),
        scratch_shapes=_scatter_scratch(geoms),
        compiler_params=_cparams(("arbitrary", "arbitrary"), collective=COLLECTIVE[name]) if n_h else _cparams(("parallel", "parallel")),
    )(a, b, *[p for _, p in hosted])
    return tuple(res[:1 + n_h])


def _row_step(half_rows):
    return max(s for s in range(16, 177, 16) if half_rows % s == 0)


class _Scatter:
    def __init__(self, geom, partial, out, land1, mine, stage2, land2, comb, s1_send, s1_recv, s2_send, s2_recv, ld_sems):
        self.w, self.row0, self.shape = _geom(geom)
        self.partial, self.out, self.land1 = partial, out, land1
        self.mine, self.stage2, self.land2, self.comb = mine, stage2, land2, comb
        self.hr = self.shape[0] // 2
        self.step = _row_step(self.hr)
        self.s1_send, self.s1_recv, self.s2_send, self.s2_recv, self.ld_sems = s1_send, s1_recv, s2_send, s2_recv, ld_sems
        self.x, self.y, self.c = lax.axis_index("x"), lax.axis_index("y"), lax.axis_index("c")
        self.sibling = (self.x, self.y, 1 - self.c)
        self.chips = [(self.x, self.y), (1 - self.x, self.y), (self.x, 1 - self.y), (1 - self.x, 1 - self.y)]

    def block(self, px, py, pc):
        dev = 4 * px + 2 * py + pc
        if self.w == W_IN:
            return self.partial.at[:, pl.ds(pl.multiple_of(dev * IN_SHARD, 128), IN_SHARD)]
        if self.w == W_OUT:
            return self.partial.at[pl.ds(pl.multiple_of(dev * OUT_SHARD, 128), OUT_SHARD), :]
        if self.w == W_DOWN:
            return self.partial.at[pl.ds(pl.multiple_of(dev * DOWN_SHARD, 32), DOWN_SHARD), :]
        return self.partial.at[pl.ds(pl.multiple_of(dev * FF_SHARD + self.row0, 32), self.shape[0]), :]

    def copy1(self, k):
        return pltpu.make_async_remote_copy(
            src_ref=self.block(*self.chips[k], 1 - self.c), dst_ref=self.land1.at[k],
            send_sem=self.s1_send.at[k], recv_sem=self.s1_recv.at[k], device_id=self.sibling, device_id_type=MESH)

    STAGE2 = [(1, 0, 1), (3, 0, 1), (2, 1, 2), (3, 1, 2), (1, 1, 1), (2, 0, 2)]

    def copy2(self, j):
        blk, h, to = self.STAGE2[j]
        src = self.comb.at[j - 4] if j >= 4 else self.stage2.at[blk - 1, pl.ds(h * self.hr, self.hr), :]
        return pltpu.make_async_remote_copy(
            src_ref=src, dst_ref=self.land2.at[j], send_sem=self.s2_send.at[j], recv_sem=self.s2_recv.at[j],
            device_id=(*self.chips[to], self.c), device_id_type=MESH)

    def _rows(self, h=None):
        step = self.step
        lo, n = (0, self.shape[0]) if h is None else (h * self.hr, self.hr)
        return [pl.ds(r0, step) for r0 in range(lo, lo + n, step)]

    def load(self, k):
        return pltpu.make_async_copy(self.block(*self.chips[k], self.c), self.mine.at[k], self.ld_sems.at[k])

    def load_mine(self):
        for k in range(4):
            self.load(k).start()

    def phase1(self):
        for k in range(4):
            self.copy1(k).start()

    def phase2(self, k):
        self.copy1(k).wait_recv()
        self.load(k).wait()
        for rs in self._rows():
            s = self.mine[k, rs, :].astype(F32) + self.land1[k, rs, :].astype(F32)
            if k == 0:
                self.out[rs, :] = s
            else:
                self.stage2[k - 1, rs, :] = s.astype(BF16)
        for j in {3: (1, 3), 1: (0,), 2: (2,), 0: ()}[k]:
            self.copy2(j).start()

    def phase2b(self):
        for j, got in ((4, 3), (5, 1)):
            blk, h, _ = self.STAGE2[j]
            self.copy2(got).wait_recv()
            for i, rs in enumerate(self._rows(h)):
                lr = pl.ds(i * self.step, self.step)
                self.comb[j - 4, lr, :] = (self.stage2[blk - 1, rs, :].astype(F32) + self.land2[got, lr, :].astype(F32)).astype(BF16)
            self.copy2(j).start()

    def phase3(self):
        for j in (0, 5, 4, 2):
            self.copy2(j).wait_recv()
        for h, (first, second) in enumerate(((0, 5), (4, 2))):
            for i, rs in enumerate(self._rows(h)):
                lr = pl.ds(i * self.step, self.step)
                self.out[rs, :] = (self.out[rs, :] + self.land2[first, lr, :].astype(F32)) + self.land2[second, lr, :].astype(F32)
        for k in range(4):
            self.copy1(k).wait_send()
        for j in range(6):
            self.copy2(j).wait_send()


def _geom(geom):
    if isinstance(geom, tuple):
        w, row0, rows = geom
        assert w == W_UP
        return w, row0, (rows, SHARD[w][1])
    return geom, 0, SHARD[geom]


N_SCATTER_SCRATCH = 10


def _scatter_out_shapes(geoms):
    return tuple(jax.ShapeDtypeStruct(_geom(g)[2], F32) for g in geoms)


def _scatter_out_specs(geoms):
    return (pl.BlockSpec(memory_space=pltpu.VMEM),) * len(geoms)


def _scatter_scratch(geoms):
    out = []
    for g in geoms:
        s = _geom(g)[2]
        hs = (s[0] // 2, s[1])
        out += [pltpu.VMEM((4,) + s, BF16), pltpu.VMEM((4,) + s, BF16), pltpu.VMEM((3,) + s, BF16), pltpu.VMEM((6,) + hs, BF16),
                pltpu.VMEM((2,) + hs, BF16),
                pltpu.SemaphoreType.DMA((4,)), pltpu.SemaphoreType.DMA((4,)), pltpu.SemaphoreType.DMA((6,)),
                pltpu.SemaphoreType.DMA((6,)), pltpu.SemaphoreType.DMA((4,))]
    return out


class _Scatters:
    def __init__(self, geoms, p_refs, out_refs, scratch):
        k = N_SCATTER_SCRATCH
        self.items = [_Scatter(g, p_refs[i], out_refs[i], *scratch[k * i:k * i + k]) for i, g in enumerate(geoms)]

    def phase1(self, diagonal=False):
        meet = _Meet(diagonal)
        meet.signal()
        for s in self.items:
            s.load_mine()
        meet.wait()
        for s in self.items:
            s.phase1()

    def phase2(self):
        for k in (3, 1, 2, 0):
            for s in self.items:
                s.phase2(k)

    def phase2b(self):
        for s in self.items:
            s.phase2b()

    def phase3(self):
        for s in self.items:
            s.phase3()


PACK_W = 1024


SMALL_FULL = [(2, 8, PACK_W), (HEADS, CHUNK, CHUNK), (2, 8, D_FF)]
SMALL_HALF = [(s[0] // 2,) + s[1:] for s in SMALL_FULL]
N_SMALL_SCRATCH = 16


def _small_scratch():
    n_a = len(SMALL_FULL)
    return ([pltpu.VMEM(SMALL_FULL[0], F32)] + [pltpu.VMEM(s, F32) for s in SMALL_HALF] + [pltpu.VMEM(s, F32) for s in SMALL_HALF]
            + [pltpu.VMEM((3,) + s, F32) for s in SMALL_HALF]
            + [pltpu.SemaphoreType.DMA((n_a,)), pltpu.SemaphoreType.DMA((n_a,)), pltpu.SemaphoreType.DMA((n_a, 3)),
               pltpu.SemaphoreType.DMA((n_a, 3)), pltpu.SemaphoreType.DMA((n_a,)), pltpu.SemaphoreType.DMA((n_a,))])


class _SmallReduce:
    def __init__(self, ins, outs, scratch):
        self.ins, self.outs = ins, outs
        (self.pack, *rest) = scratch
        self.rxs, self.css, self.gs = rest[0:3], rest[3:6], rest[6:9]
        self.s1_send, self.s1_recv, self.s2_send, self.s2_recv, self.s3_send, self.s3_recv = rest[9:]
        self.x, self.y, self.c = lax.axis_index("x"), lax.axis_index("y"), lax.axis_index("c")
        self.sibling = (self.x, self.y, 1 - self.c)
        self.chips = [(1 - self.x, self.y), (self.x, 1 - self.y), (1 - self.x, 1 - self.y)]
        self.hl = [s[0] for s in SMALL_HALF]

    def half(self, ref, a, h):
        return ref.at[pl.ds(h * self.hl[a], self.hl[a])]

    def begin(self):
        dg1_ref, dg2_ref, dgf_ref, dgrn_ref, dlng_ref, dlnb_ref, dbs_ref, loss_ref, dws_ref, dcv_ref = self.ins
        pack, c = self.pack, self.c
        pack[...] = jnp.zeros_like(pack)
        pack[0, 0:1, :] = dg1_ref[0:1, :]
        pack[0, 1:2, :] = dg2_ref[0:1, :]
        pack[0, 2:3, :] = dgf_ref[0:1, :]
        pack[0, 3:4, 0:RET_W] = dgrn_ref[0:1, :]
        lsum = loss_ref[0, 0:1, :]
        for i in range(1, N_TB):
            lsum = lsum + loss_ref[i, 0:1, :]
        pack[0, 3:4, RET_W:RET_W + 128] = lsum
        pack[1, 0:HEADS, 0:128] = dlng_ref[0:HEADS, :]
        pack[1, 0:HEADS, 128:256] = dlnb_ref[0:HEADS, :]
        pack[1, 0:HEADS, 256:384] = dbs_ref[0:HEADS, :]
        self.srcs = [pack, dws_ref, dcv_ref]
        n_a = len(self.srcs)
        self.ex1 = [pltpu.make_async_remote_copy(src_ref=self.half(self.srcs[a], a, 1 - c), dst_ref=self.rxs[a],
                                                 send_sem=self.s1_send.at[a], recv_sem=self.s1_recv.at[a],
                                                 device_id=self.sibling, device_id_type=MESH) for a in range(n_a)]
        for cp in self.ex1:
            cp.start()
        self.ex2 = []
        for a in range(n_a):
            self.ex1[a].wait_recv()
            self.css[a][...] = self.half(self.srcs[a], a, c)[...] + self.rxs[a][...]
            for j, chip in enumerate(self.chips):
                cp = pltpu.make_async_remote_copy(src_ref=self.css[a], dst_ref=self.gs[a].at[j], send_sem=self.s2_send.at[a, j],
                                                  recv_sem=self.s2_recv.at[a, j], device_id=(*chip, c), device_id_type=MESH)
                cp.start()
                self.ex2.append(cp)

    def end(self):
        c, x, y = self.c, self.x, self.y
        ex3 = []
        for a in range(len(self.srcs)):
            css, gs, out = self.css[a], self.gs[a], self.outs[a]
            for j in range(3):
                self.ex2[3 * a + j].wait_recv()
            tot = None
            for q in range(4):
                k = jnp.where(x != (q >> 1), 1, 0) + jnp.where(y != (q & 1), 2, 0)
                term = jnp.where(k == 0, css[...], jnp.where(k == 1, gs[0], jnp.where(k == 2, gs[1], gs[2])))
                tot = term if tot is None else tot + term
            self.half(out, a, c)[...] = tot
            cp = pltpu.make_async_remote_copy(src_ref=self.half(out, a, c), dst_ref=self.half(out, a, c), send_sem=self.s3_send.at[a],
                                              recv_sem=self.s3_recv.at[a], device_id=self.sibling, device_id_type=MESH)
            cp.start()
            ex3.append(cp)
        for a in range(len(self.srcs)):
            out = self.outs[a]
            pltpu.make_async_remote_copy(src_ref=self.half(out, a, 1 - c), dst_ref=self.half(out, a, 1 - c), send_sem=self.s3_send.at[a],
                                         recv_sem=self.s3_recv.at[a], device_id=self.sibling, device_id_type=MESH).wait_recv()
        for cp in self.ex1 + self.ex2 + ex3:
            cp.wait_send()


def _adam_math(w, g, m, v):
    nm = ADAM_B1 * m + (1.0 - ADAM_B1) * g
    nv = ADAM_B2 * v + (1.0 - ADAM_B2) * (g * g)
    d = -ADAM_LR * ((nm / (1.0 - ADAM_B1 ** ADAM_STEP)) / (jnp.sqrt(nv / (1.0 - ADAM_B2 ** ADAM_STEP)) + ADAM_EPS) + ADAM_WD * w)
    return d, nm, nv


def _adamw(params, thru, n_steps, small):
    plan = []
    for w, gs, _, _ in params:
        _, r, cdim = w.shape
        if len(gs) == 1:
            edges = [0, r // n_steps]
            spec3 = pl.BlockSpec((1, r // n_steps, cdim), lambda i: (0, i, 0))
            g_specs = [pl.BlockSpec((r // n_steps, cdim), lambda i: (i, 0))]
        else:
            edges = [sum(g.shape[0] for g in gs[:k]) for k in range(len(gs) + 1)]
            spec3 = pl.BlockSpec((1, r, cdim // n_steps), lambda i: (0, 0, i))
            g_specs = [pl.BlockSpec((g.shape[0], cdim // n_steps), lambda i: (0, i)) for g in gs]
        plan.append((len(gs), edges, spec3, g_specs))
    n_in = sum(n_g + 3 for n_g, _, _, _ in plan)
    n_t = len(thru)
    s_body, s_args, s_shapes = small
    n_s = len(s_args)

    def body(*refs):
        ins, outs = refs[:n_in], refs[n_in + n_t + n_s:]
        pl.when(pl.program_id(0) == 0)(functools.partial(s_body, *refs[n_in + n_t:n_in + n_t + n_s], *outs[4 * len(plan) + n_t:]))
        for n_g, edges, _, _ in plan:
            (w_ref, *g_refs, m_ref, v_ref), ins = ins[:n_g + 3], ins[n_g + 3:]
            (go_ref, d_ref, nm_ref, nv_ref), outs = outs[:4], outs[4:]
            for g_ref, lo, hi in zip(g_refs, edges[:-1], edges[1:]):
                gg = g_ref[...]
                go_ref[0, lo:hi, :] = gg
                d_ref[0, lo:hi, :], nm_ref[0, lo:hi, :], nv_ref[0, lo:hi, :] = _adam_math(
                    w_ref[0, lo:hi, :], gg, m_ref[0, lo:hi, :], v_ref[0, lo:hi, :])
        for t_ref, to_ref in zip(refs[n_in:n_in + n_t], outs):
            to_ref[...] = t_ref[...]

    t_specs = [pl.BlockSpec((t.shape[0] // n_steps, t.shape[1]), lambda i: (i, 0)) for t in thru]
    in_specs, out_specs, out_shape, args = [], [], [], []
    for (w, gs, m, v), (_, _, spec3, g_specs) in zip(params, plan):
        in_specs += [spec3] + g_specs + [spec3, spec3]
        out_specs += [spec3] * 4
        out_shape += [jax.ShapeDtypeStruct(w.shape, F32)] * 4
        args += [w, *gs, m, v]
    vm = pl.BlockSpec(memory_space=pltpu.VMEM)
    res = pl.pallas_call(
        body, name="adamw", grid=(n_steps,),
        out_shape=tuple(out_shape) + tuple(jax.ShapeDtypeStruct(t.shape, t.dtype) for t in thru) + tuple(s_shapes),
        in_specs=in_specs + t_specs + [vm] * n_s, out_specs=tuple(out_specs) + tuple(t_specs) + (vm,) * len(s_shapes),
        compiler_params=_cparams(("arbitrary",)),
    )(*args, *thru, *s_args)
    n_b = 4 * len(params)
    return [res[4 * k:4 * k + 4] for k in range(len(params))], res[n_b:n_b + n_t], res[n_b + n_t:]


def _adamw_small(rp, rws, rcv, gcw, params):
    n_p = len(params)

    def body(*refs):
        rp_ref, rws_ref, rcv_ref, gcw_ref = refs[:4]
        ins = refs[4:4 + 3 * n_p]
        outs = refs[4 + 3 * n_p:]
        outs[4 * n_p][...] = rp_ref[0, 3:4, RET_W:RET_W + 1]
        grads = [rp_ref[0, 0:1, :], rp_ref[0, 1:2, :], rp_ref[0, 2:3, :], rp_ref[0, 3:4, 0:RET_W],
                 rp_ref[1, 0:HEADS, 0:128], rp_ref[1, 0:HEADS, 128:256], rp_ref[1, 0:HEADS, 256:384],
                 rws_ref[...], gcw_ref[...], None]
        for p in range(n_p):
            w_ref, m_ref, v_ref = ins[3 * p:3 * p + 3]
            o = outs[4 * p:4 * p + 4]
            if p == n_p - 1:
                for hf in range(2):
                    cs = slice(hf * D_FF, (hf + 1) * D_FF)
                    g = rcv_ref[hf, 3:4, :]
                    res = (g,) + _adam_math(w_ref[:, cs], g, m_ref[:, cs], v_ref[:, cs])
                    for t in range(4):
                        o[t][:, cs] = res[t]
                continue
            lead = w_ref.ndim > grads[p].ndim
            rd = (lambda r: r[0]) if lead else (lambda r: r[...])
            res = (grads[p],) + _adam_math(rd(w_ref), grads[p], rd(m_ref), rd(v_ref))
            for t in range(4):
                if lead:
                    o[t][0] = res[t]
                else:
                    o[t][...] = res[t]

    flat = [a for tr in params for a in tr]
    out_shape = tuple(jax.ShapeDtypeStruct(tr[0].shape, F32) for tr in params for _ in range(4)) + (jax.ShapeDtypeStruct((1, 1), F32),)
    return body, [rp, rws, rcv, gcw, *flat], out_shape


def kernel(x, mix_norm_g, w_in, ret_norm_g, sgu_ln_g, sgu_ln_b, sgu_w_s, sgu_b_s, w_out, ffn_norm_g, w_up, conv_w, conv_b, w_down, final_norm_g, loss_target, m_mix_norm_g, m_w_in, m_ret_norm_g, m_sgu_ln_g, m_sgu_ln_b, m_sgu_w_s, m_sgu_b_s, m_w_out, m_ffn_norm_g, m_w_up, m_conv_w, m_conv_b, m_w_down, m_final_norm_g, v_mix_norm_g, v_w_in, v_ret_norm_g, v_sgu_ln_g, v_sgu_ln_b, v_sgu_w_s, v_sgu_b_s, v_w_out, v_ffn_norm_g, v_w_up, v_conv_w, v_conv_b, v_w_down, v_final_norm_g):
    xs = x[0]
    tgt = loss_target[0]
    grn = ret_norm_g.reshape(1, RET_W)
    lng = sgu_ln_g.reshape(1, SGU_W)
    lnb = sgu_ln_b.reshape(1, SGU_W)
    ws = sgu_w_s[0]
    bsb = jnp.broadcast_to(sgu_b_s[0][:, :, None], (HEADS, CHUNK, HEAD_DIM))
    gf = final_norm_g.reshape(1, D_MODEL)
    me = 4 * lax.axis_index("x") + 2 * lax.axis_index("y") + lax.axis_index("c")
    tr = lambda a: jnp.transpose(a[0])[None]
    tr_cw = lambda a: jnp.transpose(a, (1, 0, 2))

    proj, h1, cos2, sin2, win_g, cw_sh, wout_g, wdn_g, su = _fwd_proj(
        xs, mix_norm_g, _rope_freq(), w_in[0], w_out[0], tr(w_up)[0], w_down[0], tr_cw(conv_w))
    cw_g = jnp.transpose(cw_sh, (1, 0, 2)).reshape(8, 2 * D_FF)
    x2, mixcat, o, sprev, wup_g = _fwd_mix(xs, proj, wout_g, grn, lng, lnb, ws, bsb, su)
    h2, up_pre, u_conv, act, x3, loss_parts = _fwd_ffn(x2, ffn_norm_g, wup_g, cw_g, conv_b, wdn_g, gf, tgt)

    dx3, dpre, dx2, dgf, dg2, dcv = _bwd_ffn(x3, tgt, gf, x2, ffn_norm_g, up_pre, u_conv, wup_g, cw_g, wdn_g)
    band = 512
    (gout_p,) = _wgrad("wgrad_out", mixcat, dx2, tn=512)
    gdn_p, g_out = _wgrad("wgrad_down", act, dx3, tm=FF_TILE, tn=512, hosted=[(W_OUT, gout_p)])
    gup_p, g_dn = _wgrad("wgrad_up", dpre, h2, tm=FF_TILE, tn=512, hosted=[(W_DOWN, gdn_p)])
    dproj, dgrn, dlng, dlnb, dws, dbs, g_up_a = _bwd_mix(
        dx2, proj, o, sprev, wout_g, grn, lng, lnb, ws, bsb, cos2, sin2, [((W_UP, 0, band), gup_p)])
    gin_p, g_up_b = _wgrad("wgrad_in", h1, dproj, tm=512, tn=768, hosted=[((W_UP, band, FF_SHARD - band), gup_p)])
    grad_x, g_in, rp, rws, rcv = _bwd_proj(dproj, win_g, xs, mix_norm_g, dx2, gin_p,
                                           (dg2, dgf, dgrn, dlng, dlnb, dbs, loss_parts, dws, dcv))
    gcw = tr_cw(lax.dynamic_slice(rcv, (me // (N_DEV // 2), 0, (me % (N_DEV // 2)) * FF_SHARD), (1, 3, FF_SHARD)))

    table = {}
    row = lambda a: a.reshape(1, D_MODEL)
    names_small = ["mix_norm_g", "ffn_norm_g", "final_norm_g", "ret_norm_g", "sgu_ln_g", "sgu_ln_b", "sgu_b_s", "sgu_w_s",
                   "conv_w", "conv_b"]
    params = [(mix_norm_g, m_mix_norm_g, v_mix_norm_g), (ffn_norm_g, m_ffn_norm_g, v_ffn_norm_g),
              (row(final_norm_g), row(m_final_norm_g), row(v_final_norm_g)), (ret_norm_g, m_ret_norm_g, v_ret_norm_g),
              (sgu_ln_g, m_sgu_ln_g, v_sgu_ln_g), (sgu_ln_b, m_sgu_ln_b, v_sgu_ln_b), (sgu_b_s, m_sgu_b_s, v_sgu_b_s),
              (sgu_w_s, m_sgu_w_s, v_sgu_w_s), (tr_cw(conv_w), tr_cw(m_conv_w), tr_cw(v_conv_w)), (conv_b, m_conv_b, v_conv_b)]
    big, (grad_x,), small = _adamw([(w_in, [g_in], m_w_in, v_w_in), (w_out, [g_out], m_w_out, v_w_out),
                                    (tr(w_up), [g_up_a, g_up_b], tr(m_w_up), tr(v_w_up)), (w_down, [g_dn], m_w_down, v_w_down)],
                                   [grad_x], 4, _adamw_small(rp, rws, rcv, gcw, params))
    table.update(zip(("w_in", "w_out", "w_up", "w_down"), big))
    table["w_up"] = tuple(tr(a) for a in table["w_up"])
    loss = small[4 * len(params)]
    for k, n in enumerate(names_small):
        table[n] = small[4 * k:4 * k + 4]
    table["final_norm_g"] = tuple(a.reshape(D_MODEL) for a in table["final_norm_g"])
    table["conv_w"] = tuple(tr_cw(a) for a in table["conv_w"])

    order = ["mix_norm_g", "w_in", "ret_norm_g", "sgu_ln_g", "sgu_ln_b", "sgu_w_s", "sgu_b_s", "w_out", "ffn_norm_g", "w_up",
             "conv_w", "conv_b", "w_down", "final_norm_g"]
    outs = [loss.reshape(()), grad_x[None]]
    for col in range(4):
        outs += [table[n][col] for n in order]
    return tuple(outs)
```

```python
import functools
import math

import jax
import jax.numpy as jnp
import numpy as np
from jax import lax
from jax.experimental import pallas as pl
from jax.experimental.pallas import tpu as pltpu

F32 = jnp.float32
BF16 = jnp.bfloat16
MESH = pl.DeviceIdType.MESH

N_DEV = 8
SEQ = 2048
D_MODEL = 1024
CHUNK = 128
N_CHUNK = SEQ // CHUNK
HEADS = 4
HEAD_DIM = 128
RET_W = 512
SGU_W = 512
PROJ_W = 3072
D_FF = 2816
FF_SHARD = 704
FF_TILE = 1408
FF_TILES = ((0, D_FF),)
IN_SHARD = PROJ_W // N_DEV
OUT_SHARD = D_MODEL // N_DEV
DOWN_SHARD = D_FF // N_DEV
TM = 256
N_TB = SEQ // TM
FWD_PROJ_PASS_AT = 5
FWD_MIX_PASS_AT = 10
EPS = 1e-6
ROPE_BASE = 10000.0
K_SCALE = HEAD_DIM ** -0.5
INV_SQRT2 = 0.7071067811865476
INV_SQRT_2PI = 0.3989422804014327

ADAM_LR = 0.001
ADAM_B1 = 0.9
ADAM_B2 = 0.999
ADAM_EPS = 1e-08
ADAM_WD = 0.01
ADAM_STEP = 10

VMEM_LIMIT = 56 * 1024 * 1024


def _cparams(sem=None, vmem=VMEM_LIMIT, collective=None):
    return pltpu.CompilerParams(dimension_semantics=sem, vmem_limit_bytes=vmem, collective_id=collective)


COLLECTIVE = {name: k for k, name in enumerate(("fwd_proj", "fwd_mix", "wgrad_down", "wgrad_up", "bwd_mix", "wgrad_in", "bwd_proj"))}


class _Meet:
    def __init__(self, diagonal):
        x, y, c = lax.axis_index("x"), lax.axis_index("y"), lax.axis_index("c")
        self.peers = [(x, y, 1 - c), (1 - x, y, c), (x, 1 - y, c)] + ([(1 - x, 1 - y, c)] if diagonal else [])

    def signal(self):
        for peer in self.peers:
            pl.semaphore_signal(pltpu.get_barrier_semaphore(), inc=1, device_id=peer, device_id_type=MESH)

    def wait(self):
        pl.semaphore_wait(pltpu.get_barrier_semaphore(), len(self.peers))


def _resident(shape):
    nd = len(shape)
    return pl.BlockSpec(shape, lambda *_: (0,) * nd, pipeline_mode=pl.Buffered(1))


def _dot(a, b):
    return jnp.dot(a, b, preferred_element_type=F32)


def _dot_nt(a, b):
    return lax.dot_general(a, b, (((1,), (1,)), ((), ())), preferred_element_type=F32)


def _dot_tn(a, b):
    return lax.dot_general(a, b, (((0,), (0,)), ((), ())), preferred_element_type=F32)


def _sigmoid(x):
    return 1.0 / (1.0 + jnp.exp(-x))


def _gelu(x):
    return 0.5 * x * (1.0 + lax.erf(x * INV_SQRT2))


def _gelu_grad(x):
    return 0.5 * (1.0 + lax.erf(x * INV_SQRT2)) + x * (jnp.exp(-0.5 * x * x) * INV_SQRT_2PI)


def _rot(xh, cos2, sin2):
    return xh * cos2 + pltpu.roll(xh, HEAD_DIM // 2, 1) * sin2


def _rot_t(dh, cos2, sin2):
    return dh * cos2 + pltpu.roll(dh * sin2, HEAD_DIM // 2, 1)


def _rope_freq():
    half = HEAD_DIM // 2
    inv_freq = jnp.power(ROPE_BASE, -jnp.arange(half, dtype=F32) / half)
    return jnp.concatenate([inv_freq, inv_freq])[None, :]


def _rope_block(inv2, first_row):
    pos = (lax.broadcasted_iota(jnp.int32, (TM, HEAD_DIM), 0) + first_row).astype(F32)
    ang = pos * inv2
    sin = jnp.sin(ang)
    lane = lax.broadcasted_iota(jnp.int32, (TM, HEAD_DIM), 1)
    return jnp.cos(ang), jnp.where(lane < HEAD_DIM // 2, -sin, sin)


def _log_gamma():
    return np.log(np.float32(1.0) - np.power(np.float32(2.0), -5.0 - np.arange(HEADS, dtype=np.float32))).astype(np.float32)


def _fill_decay(mask_ref, qd_ref, kd_ref):
    assert HEAD_DIM == CHUNK
    lg = _log_gamma()
    t = lax.broadcasted_iota(jnp.int32, (CHUNK, CHUNK), 0).astype(F32)
    diff = t - lax.broadcasted_iota(jnp.int32, (CHUNK, CHUNK), 1).astype(F32)
    for h in range(HEADS):
        mask_ref[h] = jnp.where(diff >= 0.0, jnp.exp(float(lg[h]) * jnp.maximum(diff, 0.0)), 0.0)
        qd_ref[h] = jnp.exp(float(lg[h]) * (t + 1.0))
        kd_ref[h] = jnp.exp(float(lg[h]) * (CHUNK - 1.0 - t))


def _chunk_decay():
    lg = _log_gamma()
    return [float(np.exp(lg[h] * np.float32(CHUNK))) for h in range(HEADS)]


W_IN, W_OUT, W_UP, W_DOWN, W_CONV = range(5)
GATHERED = {W_IN: ((D_MODEL, PROJ_W), BF16), W_OUT: ((D_MODEL, D_MODEL), BF16), W_UP: ((2 * D_FF, D_MODEL), BF16),
            W_DOWN: ((D_FF, D_MODEL), BF16), W_CONV: ((N_DEV, 8, FF_SHARD), F32)}
SHARD = {W_IN: (D_MODEL, IN_SHARD), W_OUT: (OUT_SHARD, D_MODEL), W_UP: (FF_SHARD, D_MODEL), W_DOWN: (DOWN_SHARD, D_MODEL),
         W_CONV: (8, FF_SHARD)}


class _Gather:
    N_SEMS = 9

    def __init__(self, ids, stages, gathered, send_sems, recv_sems, local_sems):
        self.ids, self.stages, self.gathered = ids, stages, gathered
        self.send_sems, self.recv_sems, self.local_sems = send_sems, recv_sems, local_sems
        self.x, self.y, self.c = lax.axis_index("x"), lax.axis_index("y"), lax.axis_index("c")
        self.me = (self.x, self.y, self.c)
        self.sibling = (self.x, self.y, 1 - self.c)
        self.chips = [(1 - self.x, self.y), (self.x, 1 - self.y), (1 - self.x, 1 - self.y)]

    def slot(self, n, px, py, pc):
        dev = 4 * px + 2 * py + pc
        w, g = self.ids[n], self.gathered[n]
        if w == W_IN:
            return g.at[:, pl.ds(pl.multiple_of(dev * IN_SHARD, 128), IN_SHARD)]
        if w == W_OUT:
            return g.at[pl.ds(pl.multiple_of(dev * OUT_SHARD, 128), OUT_SHARD), :]
        if w == W_DOWN:
            return g.at[pl.ds(pl.multiple_of(dev * DOWN_SHARD, 32), DOWN_SHARD), :]
        if w == W_UP:
            return g.at[pl.ds(pl.multiple_of(dev * FF_SHARD, 32), FF_SHARD), :]
        return g.at[dev]

    def half(self, n, px, py, pc, h):
        dev = 4 * px + 2 * py + pc
        w, g = self.ids[n], self.gathered[n]
        if w == W_IN:
            return g.at[pl.ds(h * (D_MODEL // 2), D_MODEL // 2), pl.ds(pl.multiple_of(dev * IN_SHARD, 128), IN_SHARD)]
        rows = SHARD[w][0] // 2
        return g.at[pl.ds(pl.multiple_of(dev * SHARD[w][0] + h * rows, 16), rows), :]

    def tree(self, n):
        return self.ids[n] != W_CONV

    def copy(self, n, k, block, to, src=None, h=None):
        ref = self.slot(n, *block) if h is None else self.half(n, *block, h)
        return pltpu.make_async_remote_copy(
            src_ref=ref if src is None else src, dst_ref=ref,
            send_sem=self.send_sems.at[n, k], recv_sem=self.recv_sems.at[n, k], device_id=to, device_id_type=MESH)

    def _mine(self):
        return [pltpu.make_async_copy(self.stages[n], self.slot(n, *self.me), self.local_sems.at[n]) for n in range(len(self.ids))]

    def _first(self):
        out = []
        for n in range(len(self.ids)):
            out.append(self.copy(n, 0, self.me, self.sibling, src=self.stages[n]))
            out += [self.copy(n, 1 + j, self.me, (*chip, self.c), src=self.stages[n])
                    for j, chip in enumerate(self.chips[:2] if self.tree(n) else self.chips)]
        return out

    def start(self):
        for cp in self._mine() + self._first():
            cp.start()

    def _passed(self, j):
        dev = (*self.chips[j], self.c)
        out = []
        for n in range(len(self.ids)):
            if not self.tree(n):
                out.append(self.copy(n, 4 + j, dev, self.sibling))
            elif j < 2:
                out += [self.copy(n, 3 + j, dev, (*self.chips[1 - j], self.c), h=j), self.copy(n, 5 + j, dev, self.sibling)]
            else:
                out += [self.copy(n, 7, dev, self.sibling, h=0), self.copy(n, 8, dev, self.sibling, h=1)]
        return out

    def near(self):
        for j in range(2):
            dev = (*self.chips[j], self.c)
            for n in range(len(self.ids)):
                self.copy(n, 1 + j, dev, self.me).wait_recv()
            for cp in self._passed(j):
                cp.start()

    def finish(self):
        dev = (*self.chips[2], self.c)
        for n in range(len(self.ids)):
            if self.tree(n):
                self.copy(n, 3, dev, self.me, h=0).wait_recv()
                self.copy(n, 4, dev, self.me, h=1).wait_recv()
            else:
                self.copy(n, 3, dev, self.me).wait_recv()
        for cp in self._passed(2):
            cp.start()
        for n in range(len(self.ids)):
            self.copy(n, 0, self.sibling, self.me).wait_recv()
            for j, chip in enumerate(self.chips):
                dev = (*chip, 1 - self.c)
                if not self.tree(n):
                    self.copy(n, 4 + j, dev, self.me).wait_recv()
                elif j < 2:
                    self.copy(n, 5 + j, dev, self.me).wait_recv()
                else:
                    self.copy(n, 7, dev, self.me, h=0).wait_recv()
                    self.copy(n, 8, dev, self.me, h=1).wait_recv()
        for cp in self._mine():
            cp.wait()
        for cp in self._first() + self._passed(0) + self._passed(1) + self._passed(2):
            cp.wait_send()


def _gather_scratch(n):
    return [pltpu.SemaphoreType.DMA((n, _Gather.N_SEMS)), pltpu.SemaphoreType.DMA((n, _Gather.N_SEMS)), pltpu.SemaphoreType.DMA((n,))]


def _gathered_shapes(ids):
    return tuple(jax.ShapeDtypeStruct(*GATHERED[w]) for w in ids)


def _fwd_proj(x, g1, inv2, w_in, w_out, w_up, w_down, conv_w):
    ids_a, ids_b = [W_IN, W_CONV], [W_OUT, W_DOWN]

    def body(x_ref, g_ref, inv_ref, in_hbm, out_hbm, up_hbm, dn_hbm, cw_ref,
             proj_ref, h1_ref, cos_ref, sin_ref, gin, gcw, gout, gdn, su_ref,
             w_vm, s_in, s_cw, s_out, s_dn, f_in, f_out, f_up, f_dn, ld_sems,
             a_send, a_recv, a_local, b_send, b_recv, b_local):
        ag_a = _Gather(ids_a, [s_in, s_cw], [gin, gcw], a_send, a_recv, a_local)
        ag_b = _Gather(ids_b, [s_out, s_dn], [gout, gdn], b_send, b_recv, b_local)

        @pl.when(pl.program_id(0) == 0)
        def _():
            meet = _Meet(diagonal=True)
            meet.signal()
            loads = [pltpu.make_async_copy(src, dst, ld_sems.at[i])
                     for i, (src, dst) in enumerate(((in_hbm, f_in), (out_hbm, f_out), (dn_hbm, f_dn), (up_hbm, f_up)))]
            for cp in loads:
                cp.start()
            s_cw[...] = jnp.zeros_like(s_cw)
            for k in range(3):
                s_cw[k:k + 1, :] = cw_ref[k]
            loads[0].wait()
            s_in[...] = f_in[...].astype(BF16)
            meet.wait()
            ag_a.start()
            loads[1].wait()
            s_out[...] = f_out[...].astype(BF16)
            loads[2].wait()
            s_dn[...] = f_dn[...].astype(BF16)
            ag_a.near()
            ag_b.start()
            loads[3].wait()
            su_ref[...] = f_up[...].astype(BF16)
            ag_a.finish()
            fill = pltpu.make_async_copy(gin, w_vm, ld_sems.at[4])
            fill.start()
            fill.wait()

        pl.when(pl.program_id(0) == FWD_PROJ_PASS_AT)(ag_b.near)

        xb = x_ref[...]
        r = lax.rsqrt(jnp.mean(xb * xb, axis=-1, keepdims=True) + EPS)
        h = ((xb * r) * g_ref[...]).astype(BF16)
        h1_ref[...] = h
        p = _dot(h, w_vm[...])
        c2, s2 = _rope_block(inv_ref[...], pl.program_id(0) * TM)
        cos_ref[...], sin_ref[...] = c2, s2
        for hd in range(HEADS):
            sl = slice(hd * HEAD_DIM, (hd + 1) * HEAD_DIM)
            proj_ref[:, sl] = _rot(p[:, sl], c2, s2)
            ks = slice(RET_W + hd * HEAD_DIM, RET_W + (hd + 1) * HEAD_DIM)
            proj_ref[:, ks] = _rot(p[:, ks], c2, s2) * K_SCALE
        proj_ref[:, 2 * RET_W:] = p[:, 2 * RET_W:]

        pl.when(pl.program_id(0) == N_TB - 1)(ag_b.finish)

    tok = lambda w: pl.BlockSpec((TM, w), lambda i: (i, 0))
    hbm = pl.BlockSpec(memory_space=pl.ANY)
    vm = pl.BlockSpec(memory_space=pltpu.VMEM)
    return pl.pallas_call(
        body, name="fwd_proj", grid=(N_TB,),
        out_shape=(jax.ShapeDtypeStruct((SEQ, PROJ_W), F32), jax.ShapeDtypeStruct((SEQ, D_MODEL), BF16),
                   jax.ShapeDtypeStruct((SEQ, HEAD_DIM), F32), jax.ShapeDtypeStruct((SEQ, HEAD_DIM), F32))
        + _gathered_shapes(ids_a + ids_b) + (jax.ShapeDtypeStruct(SHARD[W_UP], BF16),),
        in_specs=[tok(D_MODEL), _resident((1, D_MODEL)), _resident((1, HEAD_DIM)), hbm, hbm, hbm, hbm, vm],
        out_specs=(tok(PROJ_W), tok(D_MODEL), tok(HEAD_DIM), tok(HEAD_DIM), hbm, hbm, hbm, hbm, vm),
        scratch_shapes=[pltpu.VMEM((D_MODEL, PROJ_W), BF16), pltpu.VMEM(SHARD[W_IN], BF16), pltpu.VMEM(SHARD[W_CONV], F32),
                        pltpu.VMEM(SHARD[W_OUT], BF16), pltpu.VMEM(SHARD[W_DOWN], BF16),
                        pltpu.VMEM(SHARD[W_IN], F32), pltpu.VMEM(SHARD[W_OUT], F32), pltpu.VMEM(SHARD[W_UP], F32),
                        pltpu.VMEM(SHARD[W_DOWN], F32), pltpu.SemaphoreType.DMA((5,))]
        + _gather_scratch(len(ids_a)) + _gather_scratch(len(ids_b)),
        compiler_params=_cparams(("arbitrary",), collective=COLLECTIVE["fwd_proj"]),
    )(x, g1, inv2, w_in, w_out, w_up, w_down, conv_w)


def _causal(w):
    r = lax.broadcasted_iota(jnp.int32, (CHUNK, CHUNK), 0)
    c = lax.broadcasted_iota(jnp.int32, (CHUNK, CHUNK), 1)
    return jnp.where(r >= c, w, 0.0)


def _fwd_mix(x, proj, wout_g, grn, lng, lnb, ws, bsb, su):
    cdec = _chunk_decay()
    ids = [W_UP]

    def body(x_ref, p_ref, w_ref, grn_ref, lng_ref, lnb_ref, ws_ref, bsb_ref, su_ref,
             x2_ref, cat_ref, o_ref, sp_ref, gup, state, m_ref, qd_ref, kd_ref, send_sems, recv_sems, local_sems):
        ag = _Gather(ids, [su_ref], [gup], send_sems, recv_sems, local_sems)

        @pl.when(pl.program_id(0) == 0)
        def _():
            meet = _Meet(diagonal=False)
            meet.signal()
            state[...] = jnp.zeros_like(state)
            _fill_decay(m_ref, qd_ref, kd_ref)
            meet.wait()
            ag.start()

        for h in range(HEADS):
            sl = slice(h * HEAD_DIM, (h + 1) * HEAD_DIM)
            q = p_ref[:, sl]
            k = p_ref[:, RET_W + h * HEAD_DIM:RET_W + (h + 1) * HEAD_DIM]
            v = p_ref[:, 2 * RET_W + h * HEAD_DIM:2 * RET_W + (h + 1) * HEAD_DIM]
            g = p_ref[:, 3 * RET_W + h * HEAD_DIM:3 * RET_W + (h + 1) * HEAD_DIM]
            qb, kb, vb = q.astype(BF16), k.astype(BF16), v.astype(BF16)
            a = _dot_nt(qb, kb) * m_ref[h]
            spb = state[h].astype(BF16)
            sp_ref[0, h] = spb
            o = _dot(a.astype(BF16), vb) + _dot((q * qd_ref[h]).astype(BF16), spb)
            state[h] = state[h] * cdec[h] + _dot_tn((k * kd_ref[h]).astype(BF16), vb)
            o_ref[:, sl] = o
            rinv = lax.rsqrt(jnp.mean(o * o, axis=-1, keepdims=True) + EPS)
            rn = (o * rinv) * grn_ref[:, sl]
            cat_ref[:, sl] = ((g * _sigmoid(g)) * rn).astype(BF16)
        for gi in range(HEADS):
            sl = slice(gi * HEAD_DIM, (gi + 1) * HEAD_DIM)
            u = p_ref[:, 4 * RET_W + gi * HEAD_DIM:4 * RET_W + (gi + 1) * HEAD_DIM]
            sv = p_ref[:, 4 * RET_W + SGU_W + gi * HEAD_DIM:4 * RET_W + SGU_W + (gi + 1) * HEAD_DIM]
            gv = _gelu(sv)
            xc = gv - jnp.mean(gv, axis=-1, keepdims=True)
            vn = (xc * lax.rsqrt(jnp.mean(xc * xc, axis=-1, keepdims=True) + EPS)) * lng_ref[:, sl] + lnb_ref[:, sl]
            mixed = _dot(_causal(ws_ref[gi]).astype(BF16), vn.astype(BF16)) + bsb_ref[gi]
            cat_ref[:, RET_W + gi * HEAD_DIM:RET_W + (gi + 1) * HEAD_DIM] = (_gelu(u) * mixed).astype(BF16)
        x2_ref[...] = x_ref[...] + _dot(cat_ref[...], w_ref[...])

        pl.when(pl.program_id(0) == FWD_MIX_PASS_AT)(ag.near)
        pl.when(pl.program_id(0) == N_CHUNK - 1)(ag.finish)

    ch = lambda w: pl.BlockSpec((CHUNK, w), lambda i: (i, 0))
    hcc = (HEADS, CHUNK, CHUNK)
    hbm = pl.BlockSpec(memory_space=pl.ANY)
    return pl.pallas_call(
        body, name="fwd_mix", grid=(N_CHUNK,),
        out_shape=(jax.ShapeDtypeStruct((SEQ, D_MODEL), F32), jax.ShapeDtypeStruct((SEQ, D_MODEL), BF16),
                   jax.ShapeDtypeStruct((SEQ, RET_W), F32), jax.ShapeDtypeStruct((N_CHUNK, HEADS, HEAD_DIM, HEAD_DIM), BF16))
        + _gathered_shapes(ids),
        in_specs=[ch(D_MODEL), ch(PROJ_W), _resident((D_MODEL, D_MODEL)), _resident((1, RET_W)), _resident((1, SGU_W)),
                  _resident((1, SGU_W)), _resident(hcc), _resident(hcc), hbm],
        out_specs=(ch(D_MODEL), ch(D_MODEL), ch(RET_W), pl.BlockSpec((1, HEADS, HEAD_DIM, HEAD_DIM), lambda i: (i, 0, 0, 0)), hbm),
        scratch_shapes=[pltpu.VMEM((HEADS, HEAD_DIM, HEAD_DIM), F32)] + [pltpu.VMEM(hcc, F32)] * 3 + _gather_scratch(len(ids)),
        compiler_params=_cparams(("arbitrary",), collective=COLLECTIVE["fwd_mix"]),
    )(x, proj, wout_g, grn, lng, lnb, ws, bsb, su)


def _conv_taps(p, prev8):
    row = lax.broadcasted_iota(jnp.int32, p.shape, 0)
    p1 = jnp.where(row == 0, prev8[7:8, :], pltpu.roll(p, 1, 0))
    p2 = jnp.where(row == 0, prev8[6:7, :], jnp.where(row == 1, prev8[7:8, :], pltpu.roll(p, 2, 0)))
    return p1, p2


def _fwd_ffn(x2, g2, wup_g, cw_g, cb_g, wdn_g, gf, tgt):
    def body(x_ref, g_ref, wu_ref, cw_ref, cb_ref, wd_ref, gf_ref, t_ref, h2_ref, up_ref, u_ref, act_ref, x3_ref, loss_ref, carry):
        @pl.when(pl.program_id(0) == 0)
        def _():
            carry[...] = jnp.zeros_like(carry)

        xb = x_ref[...]
        r = lax.rsqrt(jnp.mean(xb * xb, axis=-1, keepdims=True) + EPS)
        h = ((xb * r) * g_ref[...]).astype(BF16)
        h2_ref[...] = h
        acc = xb
        for t0, tw in FF_TILES:
            u = []
            for c0 in (t0, D_FF + t0):
                cs = slice(c0, c0 + tw)
                p = _dot_nt(h, wu_ref[pl.ds(c0, tw), :])
                up_ref[:, cs] = p.astype(BF16)
                p1, p2 = _conv_taps(p, carry[:, cs])
                carry[:, cs] = p[TM - 8:, :]
                us = p2 * cw_ref[0:1, cs] + p1 * cw_ref[1:2, cs] + p * cw_ref[2:3, cs] + cb_ref[:, cs]
                u_ref[:, cs] = us.astype(BF16)
                u.append(us)
            a = ((u[0] * _sigmoid(u[0])) * u[1]).astype(BF16)
            act_ref[:, t0:t0 + tw] = a
            acc = acc + _dot(a, wd_ref[pl.ds(t0, tw), :])
        x3_ref[...] = acc
        r3 = lax.rsqrt(jnp.mean(acc * acc, axis=-1, keepdims=True) + EPS)
        diff = (acc * r3) * gf_ref[...] - t_ref[...]
        loss_ref[...] = jnp.full(loss_ref.shape, 0.5 * jnp.sum(jnp.mean(diff * diff, axis=-1)), F32)

    tok = lambda w: pl.BlockSpec((TM, w), lambda i: (i, 0))
    return pl.pallas_call(
        body, name="fwd_ffn", grid=(N_TB,),
        out_shape=(jax.ShapeDtypeStruct((SEQ, D_MODEL), BF16), jax.ShapeDtypeStruct((SEQ, 2 * D_FF), BF16),
                   jax.ShapeDtypeStruct((SEQ, 2 * D_FF), BF16),
                   jax.ShapeDtypeStruct((SEQ, D_FF), BF16), jax.ShapeDtypeStruct((SEQ, D_MODEL), F32),
                   jax.ShapeDtypeStruct((N_TB, 8, 128), F32)),
        in_specs=[tok(D_MODEL), _resident((1, D_MODEL)), _resident((2 * D_FF, D_MODEL)), _resident((8, 2 * D_FF)),
                  _resident((1, 2 * D_FF)), _resident((D_FF, D_MODEL)), _resident((1, D_MODEL)), tok(D_MODEL)],
        out_specs=(tok(D_MODEL), tok(2 * D_FF), tok(2 * D_FF), tok(D_FF), tok(D_MODEL),
                   pl.BlockSpec((1, 8, 128), lambda i: (i, 0, 0))),
        scratch_shapes=[pltpu.VMEM((8, 2 * D_FF), F32)],
        compiler_params=_cparams(("arbitrary",)),
    )(x2, g2, wup_g, cw_g, cb_g, wdn_g, gf, tgt)


def _bwd_ffn(x3, tgt, gf, x2, g2, up_pre, u_conv, wup_g, cw_g, wdn_g):
    def body(x3_ref, t_ref, gf_ref, x2_ref, g2_ref, up_ref, u_ref, wu_ref, cw_ref, wd_ref,
             dx3_ref, dpre_ref, dx2_ref, dx2b_ref, dgf_ref, dg2_ref, dcv_ref, nxt):
        i = pl.program_id(0)

        @pl.when(i == 0)
        def _():
            nxt[...] = jnp.zeros_like(nxt)
            dgf_ref[...] = jnp.zeros_like(dgf_ref)
            dg2_ref[...] = jnp.zeros_like(dg2_ref)
            dcv_ref[...] = jnp.zeros_like(dcv_ref)

        x3 = x3_ref[...]
        r3 = lax.rsqrt(jnp.mean(x3 * x3, axis=-1, keepdims=True) + EPS)
        xh3 = x3 * r3
        dy = (xh3 * gf_ref[...] - t_ref[...]) * (1.0 / D_MODEL)
        dgf_ref[0:1, :] += jnp.sum(dy * xh3, axis=0, keepdims=True)
        t3 = dy * gf_ref[...]
        dx3 = r3 * (t3 - xh3 * jnp.mean(t3 * xh3, axis=-1, keepdims=True))
        dx3b = dx3.astype(BF16)
        dx3_ref[...] = dx3b
        dh2 = jnp.zeros((TM, D_MODEL), F32)
        for t0, tw in FF_TILES:
            row = lax.broadcasted_iota(jnp.int32, (TM, tw), 0)
            ts = slice(t0, t0 + tw)
            dact = _dot_nt(dx3b, wd_ref[pl.ds(t0, tw), :])
            ua = u_ref[:, ts].astype(F32)
            ub = u_ref[:, D_FF + t0:D_FF + t0 + tw].astype(F32)
            sg = _sigmoid(ua)
            du = [dact * ub * (sg * (1.0 + ua * (1.0 - sg))), dact * (ua * sg)]
            for n in range(2):
                d = du[n]
                c0 = n * D_FF + t0
                cs = slice(c0, c0 + tw)
                nx = nxt[:, cs]
                n1 = jnp.where(row == TM - 1, nx[0:1, :], pltpu.roll(d, TM - 1, 0))
                n2 = jnp.where(row == TM - 2, nx[0:1, :], jnp.where(row == TM - 1, nx[1:2, :], pltpu.roll(d, TM - 2, 0)))
                nxt[:, cs] = d[0:8, :]
                dp = (d * cw_ref[2:3, cs] + n1 * cw_ref[1:2, cs] + n2 * cw_ref[0:1, cs]).astype(BF16)
                dpre_ref[:, cs] = dp
                p = up_ref[:, cs].astype(F32)
                dcv_ref[n, 0:1, ts] += jnp.sum(n2 * p, axis=0, keepdims=True)
                dcv_ref[n, 1:2, ts] += jnp.sum(n1 * p, axis=0, keepdims=True)
                dcv_ref[n, 2:3, ts] += jnp.sum(d * p, axis=0, keepdims=True)
                dcv_ref[n, 3:4, ts] += jnp.sum(d, axis=0, keepdims=True)
                dh2 = dh2 + _dot(dp, wu_ref[pl.ds(c0, tw), :])
        x2 = x2_ref[...]
        r2 = lax.rsqrt(jnp.mean(x2 * x2, axis=-1, keepdims=True) + EPS)
        xh2 = x2 * r2
        dg2_ref[0:1, :] += jnp.sum(dh2 * xh2, axis=0, keepdims=True)
        t2 = dh2 * g2_ref[...]
        dx2 = dx3 + r2 * (t2 - xh2 * jnp.mean(t2 * xh2, axis=-1, keepdims=True))
        dx2_ref[...] = dx2
        dx2b_ref[...] = dx2.astype(BF16)

    rev = lambda w: pl.BlockSpec((TM, w), lambda i: (N_TB - 1 - i, 0))
    acc = lambda s: pl.BlockSpec(s, lambda i: (0,) * len(s))
    return pl.pallas_call(
        body, name="bwd_ffn", grid=(N_TB,),
        out_shape=(jax.ShapeDtypeStruct((SEQ, D_MODEL), BF16), jax.ShapeDtypeStruct((SEQ, 2 * D_FF), BF16),
                   jax.ShapeDtypeStruct((SEQ, D_MODEL), F32), jax.ShapeDtypeStruct((SEQ, D_MODEL), BF16),
                   jax.ShapeDtypeStruct((8, D_MODEL), F32),
                   jax.ShapeDtypeStruct((8, D_MODEL), F32), jax.ShapeDtypeStruct((2, 8, D_FF), F32)),
        in_specs=[rev(D_MODEL), rev(D_MODEL), _resident((1, D_MODEL)), rev(D_MODEL), _resident((1, D_MODEL)), rev(2 * D_FF),
                  rev(2 * D_FF), _resident((2 * D_FF, D_MODEL)), _resident((8, 2 * D_FF)), _resident((D_FF, D_MODEL))],
        out_specs=(rev(D_MODEL), rev(2 * D_FF), rev(D_MODEL), rev(D_MODEL), acc((8, D_MODEL)), acc((8, D_MODEL)),
                   acc((2, 8, D_FF))),
        scratch_shapes=[pltpu.VMEM((8, 2 * D_FF), F32)],
        compiler_params=_cparams(("arbitrary",)),
    )(x3, tgt, gf, x2, g2, up_pre, u_conv, wup_g, cw_g, wdn_g)


def _bwd_mix(dx2, proj, o, sprev, wout_g, grn, lng, lnb, ws, bsb, cos2, sin2, hosted):
    cdec = _chunk_decay()
    geoms = [g for g, _ in hosted]
    n_h = len(hosted)

    def body(dx2_ref, p_ref, o_ref, sp_ref, w_ref, grn_ref, lng_ref, lnb_ref, ws_ref, bsb_ref, cos_ref, sin_ref, *rest):
        dp_ref, dgrn_ref, dlng_ref, dlnb_ref, dws_ref, dbs_ref = rest[n_h:n_h + 6]
        dstate, dbs_acc, m_ref, qd_ref, kd_ref = rest[2 * n_h + 6:2 * n_h + 11]
        i = pl.program_id(0)
        rs = _Scatters(geoms, rest[:n_h], rest[n_h + 6:2 * n_h + 6], rest[2 * n_h + 11:])
        pl.when(i == 0)(rs.phase1)
        pl.when(i == 3)(rs.phase2)
        pl.when(i == 8)(rs.phase2b)

        @pl.when(i == 0)
        def _():
            _fill_decay(m_ref, qd_ref, kd_ref)
            dstate[...] = jnp.zeros_like(dstate)
            dgrn_ref[...] = jnp.zeros_like(dgrn_ref)
            dlng_ref[...] = jnp.zeros_like(dlng_ref)
            dlnb_ref[...] = jnp.zeros_like(dlnb_ref)
            dws_ref[...] = jnp.zeros_like(dws_ref)
            dbs_ref[...] = jnp.zeros_like(dbs_ref)
            dbs_acc[...] = jnp.zeros_like(dbs_acc)

        dmix = _dot_nt(dx2_ref[...].astype(BF16), w_ref[...])
        for h in range(HEADS):
            sl = slice(h * HEAD_DIM, (h + 1) * HEAD_DIM)
            q = p_ref[:, sl]
            k = p_ref[:, RET_W + h * HEAD_DIM:RET_W + (h + 1) * HEAD_DIM]
            v = p_ref[:, 2 * RET_W + h * HEAD_DIM:2 * RET_W + (h + 1) * HEAD_DIM]
            g = p_ref[:, 3 * RET_W + h * HEAD_DIM:3 * RET_W + (h + 1) * HEAD_DIM]
            o = o_ref[:, sl]
            rinv = lax.rsqrt(jnp.mean(o * o, axis=-1, keepdims=True) + EPS)
            oh = o * rinv
            gr = grn_ref[:, sl]
            sg = _sigmoid(g)
            dret = dmix[:, sl]
            dp_ref[:, 3 * RET_W + h * HEAD_DIM:3 * RET_W + (h + 1) * HEAD_DIM] = (
                dret * (oh * gr) * (sg * (1.0 + g * (1.0 - sg)))).astype(BF16)
            drn = dret * (g * sg)
            dgrn_ref[0:1, sl] += jnp.sum(drn * oh, axis=0, keepdims=True)
            t = drn * gr
            do = rinv * (t - oh * jnp.mean(t * oh, axis=-1, keepdims=True))
            qb, kb, vb, dob = q.astype(BF16), k.astype(BF16), v.astype(BF16), do.astype(BF16)
            m = m_ref[h]
            ab = (_dot_nt(qb, kb) * m).astype(BF16)
            dab = (_dot_nt(dob, vb) * m).astype(BF16)
            spb = sp_ref[0, h]
            dsn = dstate[h]
            dsnb = dsn.astype(BF16)
            qdb = (q * qd_ref[h]).astype(BF16)
            kdb = (k * kd_ref[h]).astype(BF16)
            dq = _dot(dab, kb) + _dot_nt(dob, spb) * qd_ref[h]
            dk = _dot_tn(dab, qb) + _dot_nt(vb, dsnb) * kd_ref[h]
            dv = _dot_tn(ab, dob) + _dot(kdb, dsnb)
            dstate[h] = dsn * cdec[h] + _dot_tn(qdb, dob)
            c2, s2 = cos_ref[...], sin_ref[...]
            dp_ref[:, sl] = _rot_t(dq, c2, s2).astype(BF16)
            dp_ref[:, RET_W + h * HEAD_DIM:RET_W + (h + 1) * HEAD_DIM] = _rot_t(dk * K_SCALE, c2, s2).astype(BF16)
            dp_ref[:, 2 * RET_W + h * HEAD_DIM:2 * RET_W + (h + 1) * HEAD_DIM] = dv.astype(BF16)
        for gi in range(HEADS):
            sl = slice(gi * HEAD_DIM, (gi + 1) * HEAD_DIM)
            u = p_ref[:, 4 * RET_W + gi * HEAD_DIM:4 * RET_W + (gi + 1) * HEAD_DIM]
            sv = p_ref[:, 4 * RET_W + SGU_W + gi * HEAD_DIM:4 * RET_W + SGU_W + (gi + 1) * HEAD_DIM]
            gv = _gelu(sv)
            xc = gv - jnp.mean(gv, axis=-1, keepdims=True)
            rstd = lax.rsqrt(jnp.mean(xc * xc, axis=-1, keepdims=True) + EPS)
            xh = xc * rstd
            lg = lng_ref[:, sl]
            vnb = (xh * lg + lnb_ref[:, sl]).astype(BF16)
            wcb = _causal(ws_ref[gi]).astype(BF16)
            mixed = _dot(wcb, vnb) + bsb_ref[gi]
            dsgu = dmix[:, RET_W + gi * HEAD_DIM:RET_W + (gi + 1) * HEAD_DIM]
            dmixed = dsgu * _gelu(u)
            dmb = dmixed.astype(BF16)
            dws_ref[gi] += _causal(_dot_nt(dmb, vnb))
            dbs_acc[gi] += dmixed
            dvn = _dot_tn(wcb, dmb)
            dlng_ref[gi:gi + 1, :] += jnp.sum(dvn * xh, axis=0, keepdims=True)
            dlnb_ref[gi:gi + 1, :] += jnp.sum(dvn, axis=0, keepdims=True)
            dxh = dvn * lg
            dgv = rstd * (dxh - jnp.mean(dxh, axis=-1, keepdims=True) - xh * jnp.mean(dxh * xh, axis=-1, keepdims=True))
            dp_ref[:, 4 * RET_W + gi * HEAD_DIM:4 * RET_W + (gi + 1) * HEAD_DIM] = (dsgu * mixed * _gelu_grad(u)).astype(BF16)
            dp_ref[:, 4 * RET_W + SGU_W + gi * HEAD_DIM:4 * RET_W + SGU_W + (gi + 1) * HEAD_DIM] = (
                dgv * _gelu_grad(sv)).astype(BF16)

        @pl.when(i == N_CHUNK - 1)
        def _():
            for gi in range(HEADS):
                col = jnp.broadcast_to(jnp.sum(dbs_acc[gi], axis=-1, keepdims=True), (CHUNK, CHUNK))
                dbs_ref[gi:gi + 1, :] = jnp.transpose(col)[0:1, :]
            rs.phase3()

    rev = lambda w: pl.BlockSpec((CHUNK, w), lambda i: (N_CHUNK - 1 - i, 0))
    hcc = (HEADS, CHUNK, CHUNK)
    acc = lambda s: pl.BlockSpec(s, lambda i: (0,) * len(s))
    res = pl.pallas_call(
        body, name="bwd_mix", grid=(N_CHUNK,),
        out_shape=(jax.ShapeDtypeStruct((SEQ, PROJ_W), BF16), jax.ShapeDtypeStruct((8, RET_W), F32),
                   jax.ShapeDtypeStruct((8, HEAD_DIM), F32), jax.ShapeDtypeStruct((8, HEAD_DIM), F32),
                   jax.ShapeDtypeStruct(hcc, F32), jax.ShapeDtypeStruct((8, CHUNK), F32)) + _scatter_out_shapes(geoms),
        in_specs=[rev(D_MODEL), rev(PROJ_W), rev(RET_W),
                  pl.BlockSpec((1, HEADS, HEAD_DIM, HEAD_DIM), lambda i: (N_CHUNK - 1 - i, 0, 0, 0)),
                  _resident((D_MODEL, D_MODEL)), _resident((1, RET_W)), _resident((1, SGU_W)), _resident((1, SGU_W)),
                  _resident(hcc), _resident(hcc), rev(HEAD_DIM), rev(HEAD_DIM)]
        + [pl.BlockSpec(memory_space=pl.ANY)] * n_h,
        out_specs=(rev(PROJ_W), acc((8, RET_W)), acc((8, HEAD_DIM)), acc((8, HEAD_DIM)), acc(hcc), acc((8, CHUNK)))
        + _scatter_out_specs(geoms),
        scratch_shapes=[pltpu.VMEM((HEADS, HEAD_DIM, HEAD_DIM), F32), pltpu.VMEM((HEADS, CHUNK, CHUNK), F32)]
        + [pltpu.VMEM(hcc, F32)] * 3 + _scatter_scratch(geoms),
        compiler_params=_cparams(("arbitrary",), collective=COLLECTIVE["bwd_mix"]),
    )(dx2, proj, o, sprev, wout_g, grn, lng, lnb, ws, bsb, cos2, sin2, *[p for _, p in hosted])
    return tuple(res[:6 + n_h])


def _bwd_proj(dproj, win_g, x, g1, dx2, gin_p, small):
    geoms = [W_IN]
    n_s = len(small)

    def body(dp_ref, w_ref, x_ref, g_ref, dx2_ref, gin_ref, *rest):
        small_refs = rest[:n_s]
        dx_ref, rs_out, rp_ref, rws_ref, rcv_ref, dg_ref = rest[n_s:n_s + 6]
        rs_scratch = rest[n_s + 6:n_s + 6 + N_SCATTER_SCRATCH]
        ar_scratch = rest[n_s + 6 + N_SCATTER_SCRATCH:]
        ar_res = ar_scratch[N_SMALL_SCRATCH:]
        ar = _SmallReduce((dg_ref,) + tuple(small_refs), ar_res, ar_scratch[:N_SMALL_SCRATCH])
        rs = _Scatters(geoms, [gin_ref], [rs_out], rs_scratch)
        pl.when(pl.program_id(0) == 0)(lambda: rs.phase1(diagonal=True))
        pl.when(pl.program_id(0) == 1)(rs.phase2)
        pl.when(pl.program_id(0) == 5)(rs.phase2b)

        @pl.when(pl.program_id(0) == 0)
        def _():
            dg_ref[...] = jnp.zeros_like(dg_ref)

        dh = _dot_nt(dp_ref[...], w_ref[...])
        xb = x_ref[...]
        r = lax.rsqrt(jnp.mean(xb * xb, axis=-1, keepdims=True) + EPS)
        xh = xb * r
        dg_ref[0:1, :] += jnp.sum(dh * xh, axis=0, keepdims=True)
        t = dh * g_ref[...]
        dx_ref[...] = dx2_ref[...] + r * (t - xh * jnp.mean(t * xh, axis=-1, keepdims=True))

        @pl.when(pl.program_id(0) == N_TB - 1)
        def _():
            ar.begin()
            rs.phase3()
            ar.end()
            for o_ref, r_ref in zip((rp_ref, rws_ref, rcv_ref), ar_res):
                o_ref[...] = r_ref[...]

    tok = lambda w: pl.BlockSpec((TM, w), lambda i: (i, 0))
    vm = pl.BlockSpec(memory_space=pltpu.VMEM)
    res = pl.pallas_call(
        body, name="bwd_proj", grid=(N_TB,),
        out_shape=(jax.ShapeDtypeStruct((SEQ, D_MODEL), F32),) + _scatter_out_shapes(geoms)
        + tuple(jax.ShapeDtypeStruct(s, F32) for s in SMALL_FULL),
        in_specs=[tok(PROJ_W), _resident((D_MODEL, PROJ_W)), tok(D_MODEL), _resident((1, D_MODEL)), tok(D_MODEL),
                  pl.BlockSpec(memory_space=pl.ANY)] + [vm] * n_s,
        out_specs=(tok(D_MODEL),) + _scatter_out_specs(geoms) + (vm,) * len(SMALL_FULL),
        scratch_shapes=[pltpu.VMEM((8, D_MODEL), F32)] + _scatter_scratch(geoms) + _small_scratch()
        + [pltpu.VMEM(s, F32) for s in SMALL_FULL],
        compiler_params=_cparams(("arbitrary",), collective=COLLECTIVE["bwd_proj"]),
    )(dproj, win_g, x, g1, dx2, gin_p, *small)
    return res


def _wgrad(name, a, b, tm=None, tn=None, hosted=()):
    m_w, n_w = a.shape[-1], b.shape[-1]
    tm = m_w if tm is None else tm
    tn = n_w if tn is None else tn
    n_steps = (m_w // tm) * (n_w // tn)
    geoms = [g for g, _ in hosted]
    n_h = len(hosted)

    def body(a_ref, b_ref, *rest):
        o_ref = rest[n_h]
        if n_h:
            rs = _Scatters(geoms, rest[:n_h], rest[n_h + 1:2 * n_h + 1], rest[2 * n_h + 1:])
            step = pl.program_id(0) * (n_w // tn) + pl.program_id(1)
            pl.when(step == 0)(rs.phase1)
            pl.when(step == 1)(rs.phase2)
            pl.when(step == n_steps // 2)(rs.phase2b)
        o_ref[...] = _dot_tn(a_ref[...].astype(BF16), b_ref[...].astype(BF16)).astype(BF16)
        if n_h:
            pl.when(step == n_steps - 1)(rs.phase3)

    assert not n_h or n_steps >= 4
    res = pl.pallas_call(
        body, name=name, grid=(m_w // tm, n_w // tn),
        out_shape=(jax.ShapeDtypeStruct((m_w, n_w), BF16),) + _scatter_out_shapes(geoms),
        in_specs=[pl.BlockSpec((SEQ, tm), lambda i, j: (0, i)), pl.BlockSpec((SEQ, tn), lambda i, j: (0, j))]
        + [pl.BlockSpec(memory_space=pl.ANY)] * n_h,
        out_specs=(pl.BlockSpec((tm, tn), lambda i, j: (i, j)),) + _scatter_out_specs(geoms),
        scratch_shapes=_scatter_scratch(geoms),
        compiler_params=_cparams(("arbitrary", "arbitrary"), collective=COLLECTIVE[name]) if n_h else _cparams(("parallel", "parallel")),
    )(a, b, *[p for _, p in hosted])
    return tuple(res[:1 + n_h])


def _row_step(half_rows):
    return max(s for s in range(16, 177, 16) if half_rows % s == 0)


class _Scatter:
    def __init__(self, geom, partial, out, land1, mine, stage2, land2, comb, s1_send, s1_recv, s2_send, s2_recv, ld_sems):
        self.w, self.row0, self.shape = _geom(geom)
        self.partial, self.out, self.land1 = partial, out, land1
        self.mine, self.stage2, self.land2, self.comb = mine, stage2, land2, comb
        self.hr = self.shape[0] // 2
        self.step = _row_step(self.hr)
        self.s1_send, self.s1_recv, self.s2_send, self.s2_recv, self.ld_sems = s1_send, s1_recv, s2_send, s2_recv, ld_sems
        self.x, self.y, self.c = lax.axis_index("x"), lax.axis_index("y"), lax.axis_index("c")
        self.sibling = (self.x, self.y, 1 - self.c)
        self.chips = [(self.x, self.y), (1 - self.x, self.y), (self.x, 1 - self.y), (1 - self.x, 1 - self.y)]

    def block(self, px, py, pc):
        dev = 4 * px + 2 * py + pc
        if self.w == W_IN:
            return self.partial.at[:, pl.ds(pl.multiple_of(dev * IN_SHARD, 128), IN_SHARD)]
        if self.w == W_OUT:
            return self.partial.at[pl.ds(pl.multiple_of(dev * OUT_SHARD, 128), OUT_SHARD), :]
        if self.w == W_DOWN:
            return self.partial.at[pl.ds(pl.multiple_of(dev * DOWN_SHARD, 32), DOWN_SHARD), :]
        return self.partial.at[pl.ds(pl.multiple_of(dev * FF_SHARD + self.row0, 32), self.shape[0]), :]

    def copy1(self, k):
        return pltpu.make_async_remote_copy(
            src_ref=self.block(*self.chips[k], 1 - self.c), dst_ref=self.land1.at[k],
            send_sem=self.s1_send.at[k], recv_sem=self.s1_recv.at[k], device_id=self.sibling, device_id_type=MESH)

    STAGE2 = [(1, 0, 1), (3, 0, 1), (2, 1, 2), (3, 1, 2), (1, 1, 1), (2, 0, 2)]

    def copy2(self, j):
        blk, h, to = self.STAGE2[j]
        src = self.comb.at[j - 4] if j >= 4 else self.stage2.at[blk - 1, pl.ds(h * self.hr, self.hr), :]
        return pltpu.make_async_remote_copy(
            src_ref=src, dst_ref=self.land2.at[j], send_sem=self.s2_send.at[j], recv_sem=self.s2_recv.at[j],
            device_id=(*self.chips[to], self.c), device_id_type=MESH)

    def _rows(self, h=None):
        step = self.step
        lo, n = (0, self.shape[0]) if h is None else (h * self.hr, self.hr)
        return [pl.ds(r0, step) for r0 in range(lo, lo + n, step)]

    def load(self, k):
        return pltpu.make_async_copy(self.block(*self.chips[k], self.c), self.mine.at[k], self.ld_sems.at[k])

    def load_mine(self):
        for k in range(4):
            self.load(k).start()

    def phase1(self):
        for k in range(4):
            self.copy1(k).start()

    def phase2(self, k):
        self.copy1(k).wait_recv()
        self.load(k).wait()
        for rs in self._rows():
            s = self.mine[k, rs, :].astype(F32) + self.land1[k, rs, :].astype(F32)
            if k == 0:
                self.out[rs, :] = s
            else:
                self.stage2[k - 1, rs, :] = s.astype(BF16)
        for j in {3: (1, 3), 1: (0,), 2: (2,), 0: ()}[k]:
            self.copy2(j).start()

    def phase2b(self):
        for j, got in ((4, 3), (5, 1)):
            blk, h, _ = self.STAGE2[j]
            self.copy2(got).wait_recv()
            for i, rs in enumerate(self._rows(h)):
                lr = pl.ds(i * self.step, self.step)
                self.comb[j - 4, lr, :] = (self.stage2[blk - 1, rs, :].astype(F32) + self.land2[got, lr, :].astype(F32)).astype(BF16)
            self.copy2(j).start()

    def phase3(self):
        for j in (0, 5, 4, 2):
            self.copy2(j).wait_recv()
        for h, (first, second) in enumerate(((0, 5), (4, 2))):
            for i, rs in enumerate(self._rows(h)):
                lr = pl.ds(i * self.step, self.step)
                self.out[rs, :] = (self.out[rs, :] + self.land2[first, lr, :].astype(F32)) + self.land2[second, lr, :].astype(F32)
        for k in range(4):
            self.copy1(k).wait_send()
        for j in range(6):
            self.copy2(j).wait_send()


def _geom(geom):
    if isinstance(geom, tuple):
        w, row0, rows = geom
        assert w == W_UP
        return w, row0, (rows, SHARD[w][1])
    return geom, 0, SHARD[geom]


N_SCATTER_SCRATCH = 10


def _scatter_out_shapes(geoms):
    return tuple(jax.ShapeDtypeStruct(_geom(g)[2], F32) for g in geoms)


def _scatter_out_specs(geoms):
    return (pl.BlockSpec(memory_space=pltpu.VMEM),) * len(geoms)


def _scatter_scratch(geoms):
    out = []
    for g in geoms:
        s = _geom(g)[2]
        hs = (s[0] // 2, s[1])
        out += [pltpu.VMEM((4,) + s, BF16), pltpu.VMEM((4,) + s, BF16), pltpu.VMEM((3,) + s, BF16), pltpu.VMEM((6,) + hs, BF16),
                pltpu.VMEM((2,) + hs, BF16),
                pltpu.SemaphoreType.DMA((4,)), pltpu.SemaphoreType.DMA((4,)), pltpu.SemaphoreType.DMA((6,)),
                pltpu.SemaphoreType.DMA((6,)), pltpu.SemaphoreType.DMA((4,))]
    return out


class _Scatters:
    def __init__(self, geoms, p_refs, out_refs, scratch):
        k = N_SCATTER_SCRATCH
        self.items = [_Scatter(g, p_refs[i], out_refs[i], *scratch[k * i:k * i + k]) for i, g in enumerate(geoms)]

    def phase1(self, diagonal=False):
        meet = _Meet(diagonal)
        meet.signal()
        for s in self.items:
            s.load_mine()
        meet.wait()
        for s in self.items:
            s.phase1()

    def phase2(self):
        for k in (3, 1, 2, 0):
            for s in self.items:
                s.phase2(k)

    def phase2b(self):
        for s in self.items:
            s.phase2b()

    def phase3(self):
        for s in self.items:
            s.phase3()


PACK_W = 1024


SMALL_FULL = [(2, 8, PACK_W), (HEADS, CHUNK, CHUNK), (2, 8, D_FF)]
SMALL_HALF = [(s[0] // 2,) + s[1:] for s in SMALL_FULL]
N_SMALL_SCRATCH = 16


def _small_scratch():
    n_a = len(SMALL_FULL)
    return ([pltpu.VMEM(SMALL_FULL[0], F32)] + [pltpu.VMEM(s, F32) for s in SMALL_HALF] + [pltpu.VMEM(s, F32) for s in SMALL_HALF]
            + [pltpu.VMEM((3,) + s, F32) for s in SMALL_HALF]
            + [pltpu.SemaphoreType.DMA((n_a,)), pltpu.SemaphoreType.DMA((n_a,)), pltpu.SemaphoreType.DMA((n_a, 3)),
               pltpu.SemaphoreType.DMA((n_a, 3)), pltpu.SemaphoreType.DMA((n_a,)), pltpu.SemaphoreType.DMA((n_a,))])


class _SmallReduce:
    def __init__(self, ins, outs, scratch):
        self.ins, self.outs = ins, outs
        (self.pack, *rest) = scratch
        self.rxs, self.css, self.gs = rest[0:3], rest[3:6], rest[6:9]
        self.s1_send, self.s1_recv, self.s2_send, self.s2_recv, self.s3_send, self.s3_recv = rest[9:]
        self.x, self.y, self.c = lax.axis_index("x"), lax.axis_index("y"), lax.axis_index("c")
        self.sibling = (self.x, self.y, 1 - self.c)
        self.chips = [(1 - self.x, self.y), (self.x, 1 - self.y), (1 - self.x, 1 - self.y)]
        self.hl = [s[0] for s in SMALL_HALF]

    def half(self, ref, a, h):
        return ref.at[pl.ds(h * self.hl[a], self.hl[a])]

    def begin(self):
        dg1_ref, dg2_ref, dgf_ref, dgrn_ref, dlng_ref, dlnb_ref, dbs_ref, loss_ref, dws_ref, dcv_ref = self.ins
        pack, c = self.pack, self.c
        pack[...] = jnp.zeros_like(pack)
        pack[0, 0:1, :] = dg1_ref[0:1, :]
        pack[0, 1:2, :] = dg2_ref[0:1, :]
        pack[0, 2:3, :] = dgf_ref[0:1, :]
        pack[0, 3:4, 0:RET_W] = dgrn_ref[0:1, :]
        lsum = loss_ref[0, 0:1, :]
        for i in range(1, N_TB):
            lsum = lsum + loss_ref[i, 0:1, :]
        pack[0, 3:4, RET_W:RET_W + 128] = lsum
        pack[1, 0:HEADS, 0:128] = dlng_ref[0:HEADS, :]
        pack[1, 0:HEADS, 128:256] = dlnb_ref[0:HEADS, :]
        pack[1, 0:HEADS, 256:384] = dbs_ref[0:HEADS, :]
        self.srcs = [pack, dws_ref, dcv_ref]
        n_a = len(self.srcs)
        self.ex1 = [pltpu.make_async_remote_copy(src_ref=self.half(self.srcs[a], a, 1 - c), dst_ref=self.rxs[a],
                                                 send_sem=self.s1_send.at[a], recv_sem=self.s1_recv.at[a],
                                                 device_id=self.sibling, device_id_type=MESH) for a in range(n_a)]
        for cp in self.ex1:
            cp.start()
        self.ex2 = []
        for a in range(n_a):
            self.ex1[a].wait_recv()
            self.css[a][...] = self.half(self.srcs[a], a, c)[...] + self.rxs[a][...]
            for j, chip in enumerate(self.chips):
                cp = pltpu.make_async_remote_copy(src_ref=self.css[a], dst_ref=self.gs[a].at[j], send_sem=self.s2_send.at[a, j],
                                                  recv_sem=self.s2_recv.at[a, j], device_id=(*chip, c), device_id_type=MESH)
                cp.start()
                self.ex2.append(cp)

    def end(self):
        c, x, y = self.c, self.x, self.y
        ex3 = []
        for a in range(len(self.srcs)):
            css, gs, out = self.css[a], self.gs[a], self.outs[a]
            for j in range(3):
                self.ex2[3 * a + j].wait_recv()
            tot = None
            for q in range(4):
                k = jnp.where(x != (q >> 1), 1, 0) + jnp.where(y != (q & 1), 2, 0)
                term = jnp.where(k == 0, css[...], jnp.where(k == 1, gs[0], jnp.where(k == 2, gs[1], gs[2])))
                tot = term if tot is None else tot + term
            self.half(out, a, c)[...] = tot
            cp = pltpu.make_async_remote_copy(src_ref=self.half(out, a, c), dst_ref=self.half(out, a, c), send_sem=self.s3_send.at[a],
                                              recv_sem=self.s3_recv.at[a], device_id=self.sibling, device_id_type=MESH)
            cp.start()
            ex3.append(cp)
        for a in range(len(self.srcs)):
            out = self.outs[a]
            pltpu.make_async_remote_copy(src_ref=self.half(out, a, 1 - c), dst_ref=self.half(out, a, 1 - c), send_sem=self.s3_send.at[a],
                                         recv_sem=self.s3_recv.at[a], device_id=self.sibling, device_id_type=MESH).wait_recv()
        for cp in self.ex1 + self.ex2 + ex3:
            cp.wait_send()


def _adam_math(w, g, m, v):
    nm = ADAM_B1 * m + (1.0 - ADAM_B1) * g
    nv = ADAM_B2 * v + (1.0 - ADAM_B2) * (g * g)
    d = -ADAM_LR * ((nm / (1.0 - ADAM_B1 ** ADAM_STEP)) / (jnp.sqrt(nv / (1.0 - ADAM_B2 ** ADAM_STEP)) + ADAM_EPS) + ADAM_WD * w)
    return d, nm, nv


def _adamw(params, thru, n_steps, small):
    plan = []
    for w, gs, _, _ in params:
        _, r, cdim = w.shape
        if len(gs) == 1:
            edges = [0, r // n_steps]
            spec3 = pl.BlockSpec((1, r // n_steps, cdim), lambda i: (0, i, 0))
            g_specs = [pl.BlockSpec((r // n_steps, cdim), lambda i: (i, 0))]
        else:
            edges = [sum(g.shape[0] for g in gs[:k]) for k in range(len(gs) + 1)]
            spec3 = pl.BlockSpec((1, r, cdim // n_steps), lambda i: (0, 0, i))
            g_specs = [pl.BlockSpec((g.shape[0], cdim // n_steps), lambda i: (0, i)) for g in gs]
        plan.append((len(gs), edges, spec3, g_specs))
    n_in = sum(n_g + 3 for n_g, _, _, _ in plan)
    n_t = len(thru)
    s_body, s_args, s_shapes = small
    n_s = len(s_args)

    def body(*refs):
        ins, outs = refs[:n_in], refs[n_in + n_t + n_s:]
        pl.when(pl.program_id(0) == 0)(functools.partial(s_body, *refs[n_in + n_t:n_in + n_t + n_s], *outs[4 * len(plan) + n_t:]))
        for n_g, edges, _, _ in plan:
            (w_ref, *g_refs, m_ref, v_ref), ins = ins[:n_g + 3], ins[n_g + 3:]
            (go_ref, d_ref, nm_ref, nv_ref), outs = outs[:4], outs[4:]
            for g_ref, lo, hi in zip(g_refs, edges[:-1], edges[1:]):
                gg = g_ref[...]
                go_ref[0, lo:hi, :] = gg
                d_ref[0, lo:hi, :], nm_ref[0, lo:hi, :], nv_ref[0, lo:hi, :] = _adam_math(
                    w_ref[0, lo:hi, :], gg, m_ref[0, lo:hi, :], v_ref[0, lo:hi, :])
        for t_ref, to_ref in zip(refs[n_in:n_in + n_t], outs):
            to_ref[...] = t_ref[...]

    t_specs = [pl.BlockSpec((t.shape[0] // n_steps, t.shape[1]), lambda i: (i, 0)) for t in thru]
    in_specs, out_specs, out_shape, args = [], [], [], []
    for (w, gs, m, v), (_, _, spec3, g_specs) in zip(params, plan):
        in_specs += [spec3] + g_specs + [spec3, spec3]
        out_specs += [spec3] * 4
        out_shape += [jax.ShapeDtypeStruct(w.shape, F32)] * 4
        args += [w, *gs, m, v]
    vm = pl.BlockSpec(memory_space=pltpu.VMEM)
    res = pl.pallas_call(
        body, name="adamw", grid=(n_steps,),
        out_shape=tuple(out_shape) + tuple(jax.ShapeDtypeStruct(t.shape, t.dtype) for t in thru) + tuple(s_shapes),
        in_specs=in_specs + t_specs + [vm] * n_s, out_specs=tuple(out_specs) + tuple(t_specs) + (vm,) * len(s_shapes),
        compiler_params=_cparams(("arbitrary",)),
    )(*args, *thru, *s_args)
    n_b = 4 * len(params)
    return [res[4 * k:4 * k + 4] for k in range(len(params))], res[n_b:n_b + n_t], res[n_b + n_t:]


def _adamw_small(rp, rws, rcv, gcw, params):
    n_p = len(params)

    def body(*refs):
        rp_ref, rws_ref, rcv_ref, gcw_ref = refs[:4]
        ins = refs[4:4 + 3 * n_p]
        outs = refs[4 + 3 * n_p:]
        outs[4 * n_p][...] = rp_ref[0, 3:4, RET_W:RET_W + 1]
        grads = [rp_ref[0, 0:1, :], rp_ref[0, 1:2, :], rp_ref[0, 2:3, :], rp_ref[0, 3:4, 0:RET_W],
                 rp_ref[1, 0:HEADS, 0:128], rp_ref[1, 0:HEADS, 128:256], rp_ref[1, 0:HEADS, 256:384],
                 rws_ref[...], gcw_ref[...], None]
        for p in range(n_p):
            w_ref, m_ref, v_ref = ins[3 * p:3 * p + 3]
            o = outs[4 * p:4 * p + 4]
            if p == n_p - 1:
                for hf in range(2):
                    cs = slice(hf * D_FF, (hf + 1) * D_FF)
                    g = rcv_ref[hf, 3:4, :]
                    res = (g,) + _adam_math(w_ref[:, cs], g, m_ref[:, cs], v_ref[:, cs])
                    for t in range(4):
                        o[t][:, cs] = res[t]
                continue
            lead = w_ref.ndim > grads[p].ndim
            rd = (lambda r: r[0]) if lead else (lambda r: r[...])
            res = (grads[p],) + _adam_math(rd(w_ref), grads[p], rd(m_ref), rd(v_ref))
            for t in range(4):
                if lead:
                    o[t][0] = res[t]
                else:
                    o[t][...] = res[t]

    flat = [a for tr in params for a in tr]
    out_shape = tuple(jax.ShapeDtypeStruct(tr[0].shape, F32) for tr in params for _ in range(4)) + (jax.ShapeDtypeStruct((1, 1), F32),)
    return body, [rp, rws, rcv, gcw, *flat], out_shape


def kernel(x, mix_norm_g, w_in, ret_norm_g, sgu_ln_g, sgu_ln_b, sgu_w_s, sgu_b_s, w_out, ffn_norm_g, w_up, conv_w, conv_b, w_down, final_norm_g, loss_target, m_mix_norm_g, m_w_in, m_ret_norm_g, m_sgu_ln_g, m_sgu_ln_b, m_sgu_w_s, m_sgu_b_s, m_w_out, m_ffn_norm_g, m_w_up, m_conv_w, m_conv_b, m_w_down, m_final_norm_g, v_mix_norm_g, v_w_in, v_ret_norm_g, v_sgu_ln_g, v_sgu_ln_b, v_sgu_w_s, v_sgu_b_s, v_w_out, v_ffn_norm_g, v_w_up, v_conv_w, v_conv_b, v_w_down, v_final_norm_g):
    xs = x[0]
    tgt = loss_target[0]
    grn = ret_norm_g.reshape(1, RET_W)
    lng = sgu_ln_g.reshape(1, SGU_W)
    lnb = sgu_ln_b.reshape(1, SGU_W)
    ws = sgu_w_s[0]
    bsb = jnp.broadcast_to(sgu_b_s[0][:, :, None], (HEADS, CHUNK, HEAD_DIM))
    gf = final_norm_g.reshape(1, D_MODEL)
    me = 4 * lax.axis_index("x") + 2 * lax.axis_index("y") + lax.axis_index("c")
    tr = lambda a: jnp.transpose(a[0])[None]
    tr_cw = lambda a: jnp.transpose(a, (1, 0, 2))

    proj, h1, cos2, sin2, win_g, cw_sh, wout_g, wdn_g, su = _fwd_proj(
        xs, mix_norm_g, _rope_freq(), w_in[0], w_out[0], tr(w_up)[0], w_down[0], tr_cw(conv_w))
    cw_g = jnp.transpose(cw_sh, (1, 0, 2)).reshape(8, 2 * D_FF)
    x2, mixcat, o, sprev, wup_g = _fwd_mix(xs, proj, wout_g, grn, lng, lnb, ws, bsb, su)
    h2, up_pre, u_conv, act, x3, loss_parts = _fwd_ffn(x2, ffn_norm_g, wup_g, cw_g, conv_b, wdn_g, gf, tgt)

    dx3, dpre, dx2, dx2b, dgf, dg2, dcv = _bwd_ffn(x3, tgt, gf, x2, ffn_norm_g, up_pre, u_conv, wup_g, cw_g, wdn_g)
    band = 512
    (gout_p,) = _wgrad("wgrad_out", mixcat, dx2b, tn=512)
    gdn_p, g_out = _wgrad("wgrad_down", act, dx3, tm=FF_TILE, tn=512, hosted=[(W_OUT, gout_p)])
    gup_p, g_dn = _wgrad("wgrad_up", dpre, h2, tm=FF_TILE, tn=512, hosted=[(W_DOWN, gdn_p)])
    dproj, dgrn, dlng, dlnb, dws, dbs, g_up_a = _bwd_mix(
        dx2b, proj, o, sprev, wout_g, grn, lng, lnb, ws, bsb, cos2, sin2, [((W_UP, 0, band), gup_p)])
    gin_p, g_up_b = _wgrad("wgrad_in", h1, dproj, tm=512, tn=768, hosted=[((W_UP, band, FF_SHARD - band), gup_p)])
    grad_x, g_in, rp, rws, rcv = _bwd_proj(dproj, win_g, xs, mix_norm_g, dx2, gin_p,
                                           (dg2, dgf, dgrn, dlng, dlnb, dbs, loss_parts, dws, dcv))
    gcw = tr_cw(lax.dynamic_slice(rcv, (me // (N_DEV // 2), 0, (me % (N_DEV // 2)) * FF_SHARD), (1, 3, FF_SHARD)))

    table = {}
    row = lambda a: a.reshape(1, D_MODEL)
    names_small = ["mix_norm_g", "ffn_norm_g", "final_norm_g", "ret_norm_g", "sgu_ln_g", "sgu_ln_b", "sgu_b_s", "sgu_w_s",
                   "conv_w", "conv_b"]
    params = [(mix_norm_g, m_mix_norm_g, v_mix_norm_g), (ffn_norm_g, m_ffn_norm_g, v_ffn_norm_g),
              (row(final_norm_g), row(m_final_norm_g), row(v_final_norm_g)), (ret_norm_g, m_ret_norm_g, v_ret_norm_g),
              (sgu_ln_g, m_sgu_ln_g, v_sgu_ln_g), (sgu_ln_b, m_sgu_ln_b, v_sgu_ln_b), (sgu_b_s, m_sgu_b_s, v_sgu_b_s),
              (sgu_w_s, m_sgu_w_s, v_sgu_w_s), (tr_cw(conv_w), tr_cw(m_conv_w), tr_cw(v_conv_w)), (conv_b, m_conv_b, v_conv_b)]
    big, (grad_x,), small = _adamw([(w_in, [g_in], m_w_in, v_w_in), (w_out, [g_out], m_w_out, v_w_out),
                                    (tr(w_up), [g_up_a, g_up_b], tr(m_w_up), tr(v_w_up)), (w_down, [g_dn], m_w_down, v_w_down)],
                                   [grad_x], 4, _adamw_small(rp, rws, rcv, gcw, params))
    table.update(zip(("w_in", "w_out", "w_up", "w_down"), big))
    table["w_up"] = tuple(tr(a) for a in table["w_up"])
    loss = small[4 * len(params)]
    for k, n in enumerate(names_small):
        table[n] = small[4 * k:4 * k + 4]
    table["final_norm_g"] = tuple(a.reshape(D_MODEL) for a in table["final_norm_g"])
    table["conv_w"] = tuple(tr_cw(a) for a in table["conv_w"])

    order = ["mix_norm_g", "w_in", "ret_norm_g", "sgu_ln_g", "sgu_ln_b", "sgu_w_s", "sgu_b_s", "w_out", "ffn_norm_g", "w_up",
             "conv_w", "conv_b", "w_down", "final_norm_g"]
    outs = [loss.reshape(()), grad_x[None]]
    for col in range(4):
        outs += [table[n][col] for n in order]
    return tuple(outs)
```

```python
import functools
import math

import jax
import jax.numpy as jnp
import numpy as np
from jax import lax
from jax.experimental import pallas as pl
from jax.experimental.pallas import tpu as pltpu

F32 = jnp.float32
BF16 = jnp.bfloat16
MESH = pl.DeviceIdType.MESH

N_DEV = 8
SEQ = 2048
D_MODEL = 1024
CHUNK = 128
N_CHUNK = SEQ // CHUNK
HEADS = 4
HEAD_DIM = 128
RET_W = 512
SGU_W = 512
PROJ_W = 3072
D_FF = 2816
FF_SHARD = 704
FF_TILE = 1408
FF_TILES = ((0, D_FF),)
IN_SHARD = PROJ_W // N_DEV
OUT_SHARD = D_MODEL // N_DEV
DOWN_SHARD = D_FF // N_DEV
TM = 256
N_TB = SEQ // TM
FWD_PROJ_PASS_AT = 5
FWD_MIX_PASS_AT = 10
EPS = 1e-6
ROPE_BASE = 10000.0
K_SCALE = HEAD_DIM ** -0.5
INV_SQRT2 = 0.7071067811865476
INV_SQRT_2PI = 0.3989422804014327

ADAM_LR = 0.001
ADAM_B1 = 0.9
ADAM_B2 = 0.999
ADAM_EPS = 1e-08
ADAM_WD = 0.01
ADAM_STEP = 10

VMEM_LIMIT = 56 * 1024 * 1024


def _cparams(sem=None, vmem=VMEM_LIMIT, collective=None):
    return pltpu.CompilerParams(dimension_semantics=sem, vmem_limit_bytes=vmem, collective_id=collective)


COLLECTIVE = {name: k for k, name in enumerate(("fwd_proj", "fwd_mix", "wgrad_down", "wgrad_up", "bwd_mix", "wgrad_in", "bwd_proj"))}


class _Meet:
    def __init__(self, diagonal):
        x, y, c = lax.axis_index("x"), lax.axis_index("y"), lax.axis_index("c")
        self.peers = [(x, y, 1 - c), (1 - x, y, c), (x, 1 - y, c)] + ([(1 - x, 1 - y, c)] if diagonal else [])

    def signal(self):
        for peer in self.peers:
            pl.semaphore_signal(pltpu.get_barrier_semaphore(), inc=1, device_id=peer, device_id_type=MESH)

    def wait(self):
        pl.semaphore_wait(pltpu.get_barrier_semaphore(), len(self.peers))


def _resident(shape):
    nd = len(shape)
    return pl.BlockSpec(shape, lambda *_: (0,) * nd, pipeline_mode=pl.Buffered(1))


def _dot(a, b):
    return jnp.dot(a, b, preferred_element_type=F32)


def _dot_nt(a, b):
    return lax.dot_general(a, b, (((1,), (1,)), ((), ())), preferred_element_type=F32)


def _dot_tn(a, b):
    return lax.dot_general(a, b, (((0,), (0,)), ((), ())), preferred_element_type=F32)


def _sigmoid(x):
    return 1.0 / (1.0 + jnp.exp(-x))


def _gelu(x):
    return 0.5 * x * (1.0 + lax.erf(x * INV_SQRT2))


def _gelu_grad(x):
    return 0.5 * (1.0 + lax.erf(x * INV_SQRT2)) + x * (jnp.exp(-0.5 * x * x) * INV_SQRT_2PI)


def _rot(xh, cos2, sin2):
    return xh * cos2 + pltpu.roll(xh, HEAD_DIM // 2, 1) * sin2


def _rot_t(dh, cos2, sin2):
    return dh * cos2 + pltpu.roll(dh * sin2, HEAD_DIM // 2, 1)


def _rope_freq():
    half = HEAD_DIM // 2
    inv_freq = jnp.power(ROPE_BASE, -jnp.arange(half, dtype=F32) / half)
    return jnp.concatenate([inv_freq, inv_freq])[None, :]


def _rope_block(inv2, first_row):
    pos = (lax.broadcasted_iota(jnp.int32, (TM, HEAD_DIM), 0) + first_row).astype(F32)
    ang = pos * inv2
    sin = jnp.sin(ang)
    lane = lax.broadcasted_iota(jnp.int32, (TM, HEAD_DIM), 1)
    return jnp.cos(ang), jnp.where(lane < HEAD_DIM // 2, -sin, sin)


def _log_gamma():
    return np.log(np.float32(1.0) - np.power(np.float32(2.0), -5.0 - np.arange(HEADS, dtype=np.float32))).astype(np.float32)


def _fill_decay(mask_ref, qd_ref, kd_ref):
    assert HEAD_DIM == CHUNK
    lg = _log_gamma()
    t = lax.broadcasted_iota(jnp.int32, (CHUNK, CHUNK), 0).astype(F32)
    diff = t - lax.broadcasted_iota(jnp.int32, (CHUNK, CHUNK), 1).astype(F32)
    for h in range(HEADS):
        mask_ref[h] = jnp.where(diff >= 0.0, jnp.exp(float(lg[h]) * jnp.maximum(diff, 0.0)), 0.0)
        qd_ref[h] = jnp.exp(float(lg[h]) * (t + 1.0))
        kd_ref[h] = jnp.exp(float(lg[h]) * (CHUNK - 1.0 - t))


def _chunk_decay():
    lg = _log_gamma()
    return [float(np.exp(lg[h] * np.float32(CHUNK))) for h in range(HEADS)]


W_IN, W_OUT, W_UP, W_DOWN, W_CONV = range(5)
GATHERED = {W_IN: ((D_MODEL, PROJ_W), BF16), W_OUT: ((D_MODEL, D_MODEL), BF16), W_UP: ((2 * D_FF, D_MODEL), BF16),
            W_DOWN: ((D_FF, D_MODEL), BF16), W_CONV: ((N_DEV, 8, FF_SHARD), F32)}
SHARD = {W_IN: (D_MODEL, IN_SHARD), W_OUT: (OUT_SHARD, D_MODEL), W_UP: (FF_SHARD, D_MODEL), W_DOWN: (DOWN_SHARD, D_MODEL),
         W_CONV: (8, FF_SHARD)}


class _Gather:
    N_SEMS = 9

    def __init__(self, ids, stages, gathered, send_sems, recv_sems, local_sems):
        self.ids, self.stages, self.gathered = ids, stages, gathered
        self.send_sems, self.recv_sems, self.local_sems = send_sems, recv_sems, local_sems
        self.x, self.y, self.c = lax.axis_index("x"), lax.axis_index("y"), lax.axis_index("c")
        self.me = (self.x, self.y, self.c)
        self.sibling = (self.x, self.y, 1 - self.c)
        self.chips = [(1 - self.x, self.y), (self.x, 1 - self.y), (1 - self.x, 1 - self.y)]

    def slot(self, n, px, py, pc):
        dev = 4 * px + 2 * py + pc
        w, g = self.ids[n], self.gathered[n]
        if w == W_IN:
            return g.at[:, pl.ds(pl.multiple_of(dev * IN_SHARD, 128), IN_SHARD)]
        if w == W_OUT:
            return g.at[pl.ds(pl.multiple_of(dev * OUT_SHARD, 128), OUT_SHARD), :]
        if w == W_DOWN:
            return g.at[pl.ds(pl.multiple_of(dev * DOWN_SHARD, 32), DOWN_SHARD), :]
        if w == W_UP:
            return g.at[pl.ds(pl.multiple_of(dev * FF_SHARD, 32), FF_SHARD), :]
        return g.at[dev]

    def half(self, n, px, py, pc, h):
        dev = 4 * px + 2 * py + pc
        w, g = self.ids[n], self.gathered[n]
        if w == W_IN:
            return g.at[pl.ds(h * (D_MODEL // 2), D_MODEL // 2), pl.ds(pl.multiple_of(dev * IN_SHARD, 128), IN_SHARD)]
        rows = SHARD[w][0] // 2
        return g.at[pl.ds(pl.multiple_of(dev * SHARD[w][0] + h * rows, 16), rows), :]

    def tree(self, n):
        return self.ids[n] != W_CONV

    def copy(self, n, k, block, to, src=None, h=None):
        ref = self.slot(n, *block) if h is None else self.half(n, *block, h)
        return pltpu.make_async_remote_copy(
            src_ref=ref if src is None else src, dst_ref=ref,
            send_sem=self.send_sems.at[n, k], recv_sem=self.recv_sems.at[n, k], device_id=to, device_id_type=MESH)

    def _mine(self):
        return [pltpu.make_async_copy(self.stages[n], self.slot(n, *self.me), self.local_sems.at[n]) for n in range(len(self.ids))]

    def _first(self):
        out = []
        for n in range(len(self.ids)):
            out.append(self.copy(n, 0, self.me, self.sibling, src=self.stages[n]))
            out += [self.copy(n, 1 + j, self.me, (*chip, self.c), src=self.stages[n])
                    for j, chip in enumerate(self.chips[:2] if self.tree(n) else self.chips)]
        return out

    def start(self):
        for cp in self._mine() + self._first():
            cp.start()

    def _passed(self, j):
        dev = (*self.chips[j], self.c)
        out = []
        for n in range(len(self.ids)):
            if not self.tree(n):
                out.append(self.copy(n, 4 + j, dev, self.sibling))
            elif j < 2:
                out += [self.copy(n, 3 + j, dev, (*self.chips[1 - j], self.c), h=j), self.copy(n, 5 + j, dev, self.sibling)]
            else:
                out += [self.copy(n, 7, dev, self.sibling, h=0), self.copy(n, 8, dev, self.sibling, h=1)]
        return out

    def near(self):
        for j in range(2):
            dev = (*self.chips[j], self.c)
            for n in range(len(self.ids)):
                self.copy(n, 1 + j, dev, self.me).wait_recv()
            for cp in self._passed(j):
                cp.start()

    def finish(self):
        dev = (*self.chips[2], self.c)
        for n in range(len(self.ids)):
            if self.tree(n):
                self.copy(n, 3, dev, self.me, h=0).wait_recv()
                self.copy(n, 4, dev, self.me, h=1).wait_recv()
            else:
                self.copy(n, 3, dev, self.me).wait_recv()
        for cp in self._passed(2):
            cp.start()
        for n in range(len(self.ids)):
            self.copy(n, 0, self.sibling, self.me).wait_recv()
            for j, chip in enumerate(self.chips):
                dev = (*chip, 1 - self.c)
                if not self.tree(n):
                    self.copy(n, 4 + j, dev, self.me).wait_recv()
                elif j < 2:
                    self.copy(n, 5 + j, dev, self.me).wait_recv()
                else:
                    self.copy(n, 7, dev, self.me, h=0).wait_recv()
                    self.copy(n, 8, dev, self.me, h=1).wait_recv()
        for cp in self._mine():
            cp.wait()
        for cp in self._first() + self._passed(0) + self._passed(1) + self._passed(2):
            cp.wait_send()


def _gather_scratch(n):
    return [pltpu.SemaphoreType.DMA((n, _Gather.N_SEMS)), pltpu.SemaphoreType.DMA((n, _Gather.N_SEMS)), pltpu.SemaphoreType.DMA((n,))]


def _gathered_shapes(ids):
    return tuple(jax.ShapeDtypeStruct(*GATHERED[w]) for w in ids)


def _fwd_proj(x, g1, inv2, w_in, w_out, w_up, w_down, conv_w):
    ids_a, ids_b = [W_IN, W_CONV], [W_OUT, W_DOWN]

    def body(x_ref, g_ref, inv_ref, in_hbm, out_hbm, up_hbm, dn_hbm, cw_ref,
             proj_ref, h1_ref, cos_ref, sin_ref, gin, gcw, gout, gdn, su_ref,
             w_vm, s_in, s_cw, s_out, s_dn, f_in, f_out, f_up, f_dn, ld_sems,
             a_send, a_recv, a_local, b_send, b_recv, b_local):
        ag_a = _Gather(ids_a, [s_in, s_cw], [gin, gcw], a_send, a_recv, a_local)
        ag_b = _Gather(ids_b, [s_out, s_dn], [gout, gdn], b_send, b_recv, b_local)

        @pl.when(pl.program_id(0) == 0)
        def _():
            meet = _Meet(diagonal=True)
            meet.signal()
            loads = [pltpu.make_async_copy(src, dst, ld_sems.at[i])
                     for i, (src, dst) in enumerate(((in_hbm, f_in), (out_hbm, f_out), (dn_hbm, f_dn), (up_hbm, f_up)))]
            for cp in loads:
                cp.start()
            s_cw[...] = jnp.zeros_like(s_cw)
            for k in range(3):
                s_cw[k:k + 1, :] = cw_ref[k]
            loads[0].wait()
            s_in[...] = f_in[...].astype(BF16)
            meet.wait()
            ag_a.start()
            loads[1].wait()
            s_out[...] = f_out[...].astype(BF16)
            loads[2].wait()
            s_dn[...] = f_dn[...].astype(BF16)
            ag_a.near()
            ag_b.start()
            loads[3].wait()
            su_ref[...] = f_up[...].astype(BF16)
            ag_a.finish()
            fill = pltpu.make_async_copy(gin, w_vm, ld_sems.at[4])
            fill.start()
            fill.wait()

        pl.when(pl.program_id(0) == FWD_PROJ_PASS_AT)(ag_b.near)

        xb = x_ref[...]
        r = lax.rsqrt(jnp.mean(xb * xb, axis=-1, keepdims=True) + EPS)
        h = ((xb * r) * g_ref[...]).astype(BF16)
        h1_ref[...] = h
        p = _dot(h, w_vm[...])
        c2, s2 = _rope_block(inv_ref[...], pl.program_id(0) * TM)
        cos_ref[...], sin_ref[...] = c2, s2
        for hd in range(HEADS):
            sl = slice(hd * HEAD_DIM, (hd + 1) * HEAD_DIM)
            proj_ref[:, sl] = _rot(p[:, sl], c2, s2)
            ks = slice(RET_W + hd * HEAD_DIM, RET_W + (hd + 1) * HEAD_DIM)
            proj_ref[:, ks] = _rot(p[:, ks], c2, s2) * K_SCALE
        proj_ref[:, 2 * RET_W:] = p[:, 2 * RET_W:]

        pl.when(pl.program_id(0) == N_TB - 1)(ag_b.finish)

    tok = lambda w: pl.BlockSpec((TM, w), lambda i: (i, 0))
    hbm = pl.BlockSpec(memory_space=pl.ANY)
    vm = pl.BlockSpec(memory_space=pltpu.VMEM)
    return pl.pallas_call(
        body, name="fwd_proj", grid=(N_TB,),
        out_shape=(jax.ShapeDtypeStruct((SEQ, PROJ_W), F32), jax.ShapeDtypeStruct((SEQ, D_MODEL), BF16),
                   jax.ShapeDtypeStruct((SEQ, HEAD_DIM), F32), jax.ShapeDtypeStruct((SEQ, HEAD_DIM), F32))
        + _gathered_shapes(ids_a + ids_b) + (jax.ShapeDtypeStruct(SHARD[W_UP], BF16),),
        in_specs=[tok(D_MODEL), _resident((1, D_MODEL)), _resident((1, HEAD_DIM)), hbm, hbm, hbm, hbm, vm],
        out_specs=(tok(PROJ_W), tok(D_MODEL), tok(HEAD_DIM), tok(HEAD_DIM), hbm, hbm, hbm, hbm, vm),
        scratch_shapes=[pltpu.VMEM((D_MODEL, PROJ_W), BF16), pltpu.VMEM(SHARD[W_IN], BF16), pltpu.VMEM(SHARD[W_CONV], F32),
                        pltpu.VMEM(SHARD[W_OUT], BF16), pltpu.VMEM(SHARD[W_DOWN], BF16),
                        pltpu.VMEM(SHARD[W_IN], F32), pltpu.VMEM(SHARD[W_OUT], F32), pltpu.VMEM(SHARD[W_UP], F32),
                        pltpu.VMEM(SHARD[W_DOWN], F32), pltpu.SemaphoreType.DMA((5,))]
        + _gather_scratch(len(ids_a)) + _gather_scratch(len(ids_b)),
        compiler_params=_cparams(("arbitrary",), collective=COLLECTIVE["fwd_proj"]),
    )(x, g1, inv2, w_in, w_out, w_up, w_down, conv_w)


def _causal(w):
    r = lax.broadcasted_iota(jnp.int32, (CHUNK, CHUNK), 0)
    c = lax.broadcasted_iota(jnp.int32, (CHUNK, CHUNK), 1)
    return jnp.where(r >= c, w, 0.0)


def _fwd_mix(x, proj, wout_g, grn, lng, lnb, ws, bsb, su):
    cdec = _chunk_decay()
    ids = [W_UP]

    def body(x_ref, p_ref, w_ref, grn_ref, lng_ref, lnb_ref, ws_ref, bsb_ref, su_ref,
             x2_ref, cat_ref, o_ref, sp_ref, gup, state, m_ref, qd_ref, kd_ref, send_sems, recv_sems, local_sems):
        ag = _Gather(ids, [su_ref], [gup], send_sems, recv_sems, local_sems)

        @pl.when(pl.program_id(0) == 0)
        def _():
            meet = _Meet(diagonal=False)
            meet.signal()
            state[...] = jnp.zeros_like(state)
            _fill_decay(m_ref, qd_ref, kd_ref)
            meet.wait()
            ag.start()

        for h in range(HEADS):
            sl = slice(h * HEAD_DIM, (h + 1) * HEAD_DIM)
            q = p_ref[:, sl]
            k = p_ref[:, RET_W + h * HEAD_DIM:RET_W + (h + 1) * HEAD_DIM]
            v = p_ref[:, 2 * RET_W + h * HEAD_DIM:2 * RET_W + (h + 1) * HEAD_DIM]
            g = p_ref[:, 3 * RET_W + h * HEAD_DIM:3 * RET_W + (h + 1) * HEAD_DIM]
            qb, kb, vb = q.astype(BF16), k.astype(BF16), v.astype(BF16)
            a = _dot_nt(qb, kb) * m_ref[h]
            spb = state[h].astype(BF16)
            sp_ref[0, h] = spb
            o = _dot(a.astype(BF16), vb) + _dot((q * qd_ref[h]).astype(BF16), spb)
            state[h] = state[h] * cdec[h] + _dot_tn((k * kd_ref[h]).astype(BF16), vb)
            o_ref[:, sl] = o
            rinv = lax.rsqrt(jnp.mean(o * o, axis=-1, keepdims=True) + EPS)
            rn = (o * rinv) * grn_ref[:, sl]
            cat_ref[:, sl] = ((g * _sigmoid(g)) * rn).astype(BF16)
        for gi in range(HEADS):
            sl = slice(gi * HEAD_DIM, (gi + 1) * HEAD_DIM)
            u = p_ref[:, 4 * RET_W + gi * HEAD_DIM:4 * RET_W + (gi + 1) * HEAD_DIM]
            sv = p_ref[:, 4 * RET_W + SGU_W + gi * HEAD_DIM:4 * RET_W + SGU_W + (gi + 1) * HEAD_DIM]
            gv = _gelu(sv)
            xc = gv - jnp.mean(gv, axis=-1, keepdims=True)
            vn = (xc * lax.rsqrt(jnp.mean(xc * xc, axis=-1, keepdims=True) + EPS)) * lng_ref[:, sl] + lnb_ref[:, sl]
            mixed = _dot(_causal(ws_ref[gi]).astype(BF16), vn.astype(BF16)) + bsb_ref[gi]
            cat_ref[:, RET_W + gi * HEAD_DIM:RET_W + (gi + 1) * HEAD_DIM] = (_gelu(u) * mixed).astype(BF16)
        x2_ref[...] = x_ref[...] + _dot(cat_ref[...], w_ref[...])

        pl.when(pl.program_id(0) == FWD_MIX_PASS_AT)(ag.near)
        pl.when(pl.program_id(0) == N_CHUNK - 1)(ag.finish)

    ch = lambda w: pl.BlockSpec((CHUNK, w), lambda i: (i, 0))
    hcc = (HEADS, CHUNK, CHUNK)
    hbm = pl.BlockSpec(memory_space=pl.ANY)
    return pl.pallas_call(
        body, name="fwd_mix", grid=(N_CHUNK,),
        out_shape=(jax.ShapeDtypeStruct((SEQ, D_MODEL), F32), jax.ShapeDtypeStruct((SEQ, D_MODEL), BF16),
                   jax.ShapeDtypeStruct((SEQ, RET_W), F32), jax.ShapeDtypeStruct((N_CHUNK, HEADS, HEAD_DIM, HEAD_DIM), BF16))
        + _gathered_shapes(ids),
        in_specs=[ch(D_MODEL), ch(PROJ_W), _resident((D_MODEL, D_MODEL)), _resident((1, RET_W)), _resident((1, SGU_W)),
                  _resident((1, SGU_W)), _resident(hcc), _resident(hcc), hbm],
        out_specs=(ch(D_MODEL), ch(D_MODEL), ch(RET_W), pl.BlockSpec((1, HEADS, HEAD_DIM, HEAD_DIM), lambda i: (i, 0, 0, 0)), hbm),
        scratch_shapes=[pltpu.VMEM((HEADS, HEAD_DIM, HEAD_DIM), F32)] + [pltpu.VMEM(hcc, F32)] * 3 + _gather_scratch(len(ids)),
        compiler_params=_cparams(("arbitrary",), collective=COLLECTIVE["fwd_mix"]),
    )(x, proj, wout_g, grn, lng, lnb, ws, bsb, su)


def _conv_taps(p, prev8):
    row = lax.broadcasted_iota(jnp.int32, p.shape, 0)
    p1 = jnp.where(row == 0, prev8[7:8, :], pltpu.roll(p, 1, 0))
    p2 = jnp.where(row == 0, prev8[6:7, :], jnp.where(row == 1, prev8[7:8, :], pltpu.roll(p, 2, 0)))
    return p1, p2


def _fwd_ffn(x2, g2, wup_g, cw_g, cb_g, wdn_g, gf, tgt):
    def body(x_ref, g_ref, wu_ref, cw_ref, cb_ref, wd_ref, gf_ref, t_ref, h2_ref, up_ref, u_ref, act_ref, x3_ref, loss_ref, carry):
        @pl.when(pl.program_id(0) == 0)
        def _():
            carry[...] = jnp.zeros_like(carry)

        xb = x_ref[...]
        r = lax.rsqrt(jnp.mean(xb * xb, axis=-1, keepdims=True) + EPS)
        h = ((xb * r) * g_ref[...]).astype(BF16)
        h2_ref[...] = h
        acc = xb
        for t0, tw in FF_TILES:
            u = []
            for c0 in (t0, D_FF + t0):
                cs = slice(c0, c0 + tw)
                p = _dot_nt(h, wu_ref[pl.ds(c0, tw), :])
                up_ref[:, cs] = p.astype(BF16)
                p1, p2 = _conv_taps(p, carry[:, cs])
                carry[:, cs] = p[TM - 8:, :]
                us = p2 * cw_ref[0:1, cs] + p1 * cw_ref[1:2, cs] + p * cw_ref[2:3, cs] + cb_ref[:, cs]
                u_ref[:, cs] = us.astype(BF16)
                u.append(us)
            a = ((u[0] * _sigmoid(u[0])) * u[1]).astype(BF16)
            act_ref[:, t0:t0 + tw] = a
            acc = acc + _dot(a, wd_ref[pl.ds(t0, tw), :])
        x3_ref[...] = acc
        r3 = lax.rsqrt(jnp.mean(acc * acc, axis=-1, keepdims=True) + EPS)
        diff = (acc * r3) * gf_ref[...] - t_ref[...]
        loss_ref[...] = jnp.full(loss_ref.shape, 0.5 * jnp.sum(jnp.mean(diff * diff, axis=-1)), F32)

    tok = lambda w: pl.BlockSpec((TM, w), lambda i: (i, 0))
    return pl.pallas_call(
        body, name="fwd_ffn", grid=(N_TB,),
        out_shape=(jax.ShapeDtypeStruct((SEQ, D_MODEL), BF16), jax.ShapeDtypeStruct((SEQ, 2 * D_FF), BF16),
                   jax.ShapeDtypeStruct((SEQ, 2 * D_FF), BF16),
                   jax.ShapeDtypeStruct((SEQ, D_FF), BF16), jax.ShapeDtypeStruct((SEQ, D_MODEL), F32),
                   jax.ShapeDtypeStruct((N_TB, 8, 128), F32)),
        in_specs=[tok(D_MODEL), _resident((1, D_MODEL)), _resident((2 * D_FF, D_MODEL)), _resident((8, 2 * D_FF)),
                  _resident((1, 2 * D_FF)), _resident((D_FF, D_MODEL)), _resident((1, D_MODEL)), tok(D_MODEL)],
        out_specs=(tok(D_MODEL), tok(2 * D_FF), tok(2 * D_FF), tok(D_FF), tok(D_MODEL),
                   pl.BlockSpec((1, 8, 128), lambda i: (i, 0, 0))),
        scratch_shapes=[pltpu.VMEM((8, 2 * D_FF), F32)],
        compiler_params=_cparams(("arbitrary",)),
    )(x2, g2, wup_g, cw_g, cb_g, wdn_g, gf, tgt)


def _bwd_ffn(x3, tgt, gf, x2, g2, up_pre, u_conv, wup_g, cw_g, wdn_g):
    def body(x3_ref, t_ref, gf_ref, x2_ref, g2_ref, up_ref, u_ref, wu_ref, cw_ref, wd_ref,
             dx3_ref, dpre_ref, dx2_ref, dgf_ref, dg2_ref, dcv_ref, nxt):
        i = pl.program_id(0)

        @pl.when(i == 0)
        def _():
            nxt[...] = jnp.zeros_like(nxt)
            dgf_ref[...] = jnp.zeros_like(dgf_ref)
            dg2_ref[...] = jnp.zeros_like(dg2_ref)
            dcv_ref[...] = jnp.zeros_like(dcv_ref)

        x3 = x3_ref[...]
        r3 = lax.rsqrt(jnp.mean(x3 * x3, axis=-1, keepdims=True) + EPS)
        xh3 = x3 * r3
        dy = (xh3 * gf_ref[...] - t_ref[...]) * (1.0 / D_MODEL)
        dgf_ref[0:1, :] += jnp.sum(dy * xh3, axis=0, keepdims=True)
        t3 = dy * gf_ref[...]
        dx3 = r3 * (t3 - xh3 * jnp.mean(t3 * xh3, axis=-1, keepdims=True))
        dx3b = dx3.astype(BF16)
        dx3_ref[...] = dx3b
        dh2 = jnp.zeros((TM, D_MODEL), F32)
        for t0, tw in FF_TILES:
            row = lax.broadcasted_iota(jnp.int32, (TM, tw), 0)
            ts = slice(t0, t0 + tw)
            dact = _dot_nt(dx3b, wd_ref[pl.ds(t0, tw), :])
            ua = u_ref[:, ts].astype(F32)
            ub = u_ref[:, D_FF + t0:D_FF + t0 + tw].astype(F32)
            sg = _sigmoid(ua)
            du = [dact * ub * (sg * (1.0 + ua * (1.0 - sg))), dact * (ua * sg)]
            for n in range(2):
                d = du[n]
                c0 = n * D_FF + t0
                cs = slice(c0, c0 + tw)
                nx = nxt[:, cs]
                n1 = jnp.where(row == TM - 1, nx[0:1, :], pltpu.roll(d, TM - 1, 0))
                n2 = jnp.where(row == TM - 2, nx[0:1, :], jnp.where(row == TM - 1, nx[1:2, :], pltpu.roll(d, TM - 2, 0)))
                nxt[:, cs] = d[0:8, :]
                dp = (d * cw_ref[2:3, cs] + n1 * cw_ref[1:2, cs] + n2 * cw_ref[0:1, cs]).astype(BF16)
                dpre_ref[:, cs] = dp
                p = up_ref[:, cs].astype(F32)
                dcv_ref[n, 0:1, ts] += jnp.sum(n2 * p, axis=0, keepdims=True)
                dcv_ref[n, 1:2, ts] += jnp.sum(n1 * p, axis=0, keepdims=True)
                dcv_ref[n, 2:3, ts] += jnp.sum(d * p, axis=0, keepdims=True)
                dcv_ref[n, 3:4, ts] += jnp.sum(d, axis=0, keepdims=True)
                dh2 = dh2 + _dot(dp, wu_ref[pl.ds(c0, tw), :])
        x2 = x2_ref[...]
        r2 = lax.rsqrt(jnp.mean(x2 * x2, axis=-1, keepdims=True) + EPS)
        xh2 = x2 * r2
        dg2_ref[0:1, :] += jnp.sum(dh2 * xh2, axis=0, keepdims=True)
        t2 = dh2 * g2_ref[...]
        dx2_ref[...] = dx3 + r2 * (t2 - xh2 * jnp.mean(t2 * xh2, axis=-1, keepdims=True))

    rev = lambda w: pl.BlockSpec((TM, w), lambda i: (N_TB - 1 - i, 0))
    acc = lambda s: pl.BlockSpec(s, lambda i: (0,) * len(s))
    return pl.pallas_call(
        body, name="bwd_ffn", grid=(N_TB,),
        out_shape=(jax.ShapeDtypeStruct((SEQ, D_MODEL), BF16), jax.ShapeDtypeStruct((SEQ, 2 * D_FF), BF16),
                   jax.ShapeDtypeStruct((SEQ, D_MODEL), F32), jax.ShapeDtypeStruct((8, D_MODEL), F32),
                   jax.ShapeDtypeStruct((8, D_MODEL), F32), jax.ShapeDtypeStruct((2, 8, D_FF), F32)),
        in_specs=[rev(D_MODEL), rev(D_MODEL), _resident((1, D_MODEL)), rev(D_MODEL), _resident((1, D_MODEL)), rev(2 * D_FF),
                  rev(2 * D_FF), _resident((2 * D_FF, D_MODEL)), _resident((8, 2 * D_FF)), _resident((D_FF, D_MODEL))],
        out_specs=(rev(D_MODEL), rev(2 * D_FF), rev(D_MODEL), acc((8, D_MODEL)), acc((8, D_MODEL)), acc((2, 8, D_FF))),
        scratch_shapes=[pltpu.VMEM((8, 2 * D_FF), F32)],
        compiler_params=_cparams(("arbitrary",)),
    )(x3, tgt, gf, x2, g2, up_pre, u_conv, wup_g, cw_g, wdn_g)


def _bwd_mix(dx2, proj, o, sprev, wout_g, grn, lng, lnb, ws, bsb, cos2, sin2, hosted):
    cdec = _chunk_decay()
    geoms = [g for g, _ in hosted]
    n_h = len(hosted)

    def body(dx2_ref, p_ref, o_ref, sp_ref, w_ref, grn_ref, lng_ref, lnb_ref, ws_ref, bsb_ref, cos_ref, sin_ref, *rest):
        dp_ref, dgrn_ref, dlng_ref, dlnb_ref, dws_ref, dbs_ref = rest[n_h:n_h + 6]
        dstate, dbs_acc, m_ref, qd_ref, kd_ref = rest[2 * n_h + 6:2 * n_h + 11]
        i = pl.program_id(0)
        rs = _Scatters(geoms, rest[:n_h], rest[n_h + 6:2 * n_h + 6], rest[2 * n_h + 11:])
        pl.when(i == 0)(rs.phase1)
        pl.when(i == 3)(rs.phase2)
        pl.when(i == 8)(rs.phase2b)

        @pl.when(i == 0)
        def _():
            _fill_decay(m_ref, qd_ref, kd_ref)
            dstate[...] = jnp.zeros_like(dstate)
            dgrn_ref[...] = jnp.zeros_like(dgrn_ref)
            dlng_ref[...] = jnp.zeros_like(dlng_ref)
            dlnb_ref[...] = jnp.zeros_like(dlnb_ref)
            dws_ref[...] = jnp.zeros_like(dws_ref)
            dbs_ref[...] = jnp.zeros_like(dbs_ref)
            dbs_acc[...] = jnp.zeros_like(dbs_acc)

        dmix = _dot_nt(dx2_ref[...].astype(BF16), w_ref[...])
        for h in range(HEADS):
            sl = slice(h * HEAD_DIM, (h + 1) * HEAD_DIM)
            q = p_ref[:, sl]
            k = p_ref[:, RET_W + h * HEAD_DIM:RET_W + (h + 1) * HEAD_DIM]
            v = p_ref[:, 2 * RET_W + h * HEAD_DIM:2 * RET_W + (h + 1) * HEAD_DIM]
            g = p_ref[:, 3 * RET_W + h * HEAD_DIM:3 * RET_W + (h + 1) * HEAD_DIM]
            o = o_ref[:, sl]
            rinv = lax.rsqrt(jnp.mean(o * o, axis=-1, keepdims=True) + EPS)
            oh = o * rinv
            gr = grn_ref[:, sl]
            sg = _sigmoid(g)
            dret = dmix[:, sl]
            dp_ref[:, 3 * RET_W + h * HEAD_DIM:3 * RET_W + (h + 1) * HEAD_DIM] = (
                dret * (oh * gr) * (sg * (1.0 + g * (1.0 - sg)))).astype(BF16)
            drn = dret * (g * sg)
            dgrn_ref[0:1, sl] += jnp.sum(drn * oh, axis=0, keepdims=True)
            t = drn * gr
            do = rinv * (t - oh * jnp.mean(t * oh, axis=-1, keepdims=True))
            qb, kb, vb, dob = q.astype(BF16), k.astype(BF16), v.astype(BF16), do.astype(BF16)
            m = m_ref[h]
            ab = (_dot_nt(qb, kb) * m).astype(BF16)
            dab = (_dot_nt(dob, vb) * m).astype(BF16)
            spb = sp_ref[0, h]
            dsn = dstate[h]
            dsnb = dsn.astype(BF16)
            qdb = (q * qd_ref[h]).astype(BF16)
            kdb = (k * kd_ref[h]).astype(BF16)
            dq = _dot(dab, kb) + _dot_nt(dob, spb) * qd_ref[h]
            dk = _dot_tn(dab, qb) + _dot_nt(vb, dsnb) * kd_ref[h]
            dv = _dot_tn(ab, dob) + _dot(kdb, dsnb)
            dstate[h] = dsn * cdec[h] + _dot_tn(qdb, dob)
            c2, s2 = cos_ref[...], sin_ref[...]
            dp_ref[:, sl] = _rot_t(dq, c2, s2).astype(BF16)
            dp_ref[:, RET_W + h * HEAD_DIM:RET_W + (h + 1) * HEAD_DIM] = _rot_t(dk * K_SCALE, c2, s2).astype(BF16)
            dp_ref[:, 2 * RET_W + h * HEAD_DIM:2 * RET_W + (h + 1) * HEAD_DIM] = dv.astype(BF16)
        for gi in range(HEADS):
            sl = slice(gi * HEAD_DIM, (gi + 1) * HEAD_DIM)
            u = p_ref[:, 4 * RET_W + gi * HEAD_DIM:4 * RET_W + (gi + 1) * HEAD_DIM]
            sv = p_ref[:, 4 * RET_W + SGU_W + gi * HEAD_DIM:4 * RET_W + SGU_W + (gi + 1) * HEAD_DIM]
            gv = _gelu(sv)
            xc = gv - jnp.mean(gv, axis=-1, keepdims=True)
            rstd = lax.rsqrt(jnp.mean(xc * xc, axis=-1, keepdims=True) + EPS)
            xh = xc * rstd
            lg = lng_ref[:, sl]
            vnb = (xh * lg + lnb_ref[:, sl]).astype(BF16)
            wcb = _causal(ws_ref[gi]).astype(BF16)
            mixed = _dot(wcb, vnb) + bsb_ref[gi]
            dsgu = dmix[:, RET_W + gi * HEAD_DIM:RET_W + (gi + 1) * HEAD_DIM]
            dmixed = dsgu * _gelu(u)
            dmb = dmixed.astype(BF16)
            dws_ref[gi] += _causal(_dot_nt(dmb, vnb))
            dbs_acc[gi] += dmixed
            dvn = _dot_tn(wcb, dmb)
            dlng_ref[gi:gi + 1, :] += jnp.sum(dvn * xh, axis=0, keepdims=True)
            dlnb_ref[gi:gi + 1, :] += jnp.sum(dvn, axis=0, keepdims=True)
            dxh = dvn * lg
            dgv = rstd * (dxh - jnp.mean(dxh, axis=-1, keepdims=True) - xh * jnp.mean(dxh * xh, axis=-1, keepdims=True))
            dp_ref[:, 4 * RET_W + gi * HEAD_DIM:4 * RET_W + (gi + 1) * HEAD_DIM] = (dsgu * mixed * _gelu_grad(u)).astype(BF16)
            dp_ref[:, 4 * RET_W + SGU_W + gi * HEAD_DIM:4 * RET_W + SGU_W + (gi + 1) * HEAD_DIM] = (
                dgv * _gelu_grad(sv)).astype(BF16)

        @pl.when(i == N_CHUNK - 1)
        def _():
            for gi in range(HEADS):
                col = jnp.broadcast_to(jnp.sum(dbs_acc[gi], axis=-1, keepdims=True), (CHUNK, CHUNK))
                dbs_ref[gi:gi + 1, :] = jnp.transpose(col)[0:1, :]
            rs.phase3()

    rev = lambda w: pl.BlockSpec((CHUNK, w), lambda i: (N_CHUNK - 1 - i, 0))
    hcc = (HEADS, CHUNK, CHUNK)
    acc = lambda s: pl.BlockSpec(s, lambda i: (0,) * len(s))
    res = pl.pallas_call(
        body, name="bwd_mix", grid=(N_CHUNK,),
        out_shape=(jax.ShapeDtypeStruct((SEQ, PROJ_W), BF16), jax.ShapeDtypeStruct((8, RET_W), F32),
                   jax.ShapeDtypeStruct((8, HEAD_DIM), F32), jax.ShapeDtypeStruct((8, HEAD_DIM), F32),
                   jax.ShapeDtypeStruct(hcc, F32), jax.ShapeDtypeStruct((8, CHUNK), F32)) + _scatter_out_shapes(geoms),
        in_specs=[rev(D_MODEL), rev(PROJ_W), rev(RET_W),
                  pl.BlockSpec((1, HEADS, HEAD_DIM, HEAD_DIM), lambda i: (N_CHUNK - 1 - i, 0, 0, 0)),
                  _resident((D_MODEL, D_MODEL)), _resident((1, RET_W)), _resident((1, SGU_W)), _resident((1, SGU_W)),
                  _resident(hcc), _resident(hcc), rev(HEAD_DIM), rev(HEAD_DIM)]
        + [pl.BlockSpec(memory_space=pl.ANY)] * n_h,
        out_specs=(rev(PROJ_W), acc((8, RET_W)), acc((8, HEAD_DIM)), acc((8, HEAD_DIM)), acc(hcc), acc((8, CHUNK)))
        + _scatter_out_specs(geoms),
        scratch_shapes=[pltpu.VMEM((HEADS, HEAD_DIM, HEAD_DIM), F32), pltpu.VMEM((HEADS, CHUNK, CHUNK), F32)]
        + [pltpu.VMEM(hcc, F32)] * 3 + _scatter_scratch(geoms),
        compiler_params=_cparams(("arbitrary",), collective=COLLECTIVE["bwd_mix"]),
    )(dx2, proj, o, sprev, wout_g, grn, lng, lnb, ws, bsb, cos2, sin2, *[p for _, p in hosted])
    return tuple(res[:6 + n_h])


def _bwd_proj(dproj, win_g, x, g1, dx2, gin_p, small):
    geoms = [W_IN]
    n_s = len(small)

    def body(dp_ref, w_ref, x_ref, g_ref, dx2_ref, gin_ref, *rest):
        small_refs = rest[:n_s]
        dx_ref, rs_out, rp_ref, rws_ref, rcv_ref, dg_ref = rest[n_s:n_s + 6]
        rs_scratch = rest[n_s + 6:n_s + 6 + N_SCATTER_SCRATCH]
        ar_scratch = rest[n_s + 6 + N_SCATTER_SCRATCH:]
        ar_res = ar_scratch[N_SMALL_SCRATCH:]
        ar = _SmallReduce((dg_ref,) + tuple(small_refs), ar_res, ar_scratch[:N_SMALL_SCRATCH])
        rs = _Scatters(geoms, [gin_ref], [rs_out], rs_scratch)
        pl.when(pl.program_id(0) == 0)(lambda: rs.phase1(diagonal=True))
        pl.when(pl.program_id(0) == 1)(rs.phase2)
        pl.when(pl.program_id(0) == 5)(rs.phase2b)

        @pl.when(pl.program_id(0) == 0)
        def _():
            dg_ref[...] = jnp.zeros_like(dg_ref)

        dh = _dot_nt(dp_ref[...], w_ref[...])
        xb = x_ref[...]
        r = lax.rsqrt(jnp.mean(xb * xb, axis=-1, keepdims=True) + EPS)
        xh = xb * r
        dg_ref[0:1, :] += jnp.sum(dh * xh, axis=0, keepdims=True)
        t = dh * g_ref[...]
        dx_ref[...] = dx2_ref[...] + r * (t - xh * jnp.mean(t * xh, axis=-1, keepdims=True))

        @pl.when(pl.program_id(0) == N_TB - 1)
        def _():
            ar.begin()
            rs.phase3()
            ar.end()
            for o_ref, r_ref in zip((rp_ref, rws_ref, rcv_ref), ar_res):
                o_ref[...] = r_ref[...]

    tok = lambda w: pl.BlockSpec((TM, w), lambda i: (i, 0))
    vm = pl.BlockSpec(memory_space=pltpu.VMEM)
    res = pl.pallas_call(
        body, name="bwd_proj", grid=(N_TB,),
        out_shape=(jax.ShapeDtypeStruct((SEQ, D_MODEL), F32),) + _scatter_out_shapes(geoms)
        + tuple(jax.ShapeDtypeStruct(s, F32) for s in SMALL_FULL),
        in_specs=[tok(PROJ_W), _resident((D_MODEL, PROJ_W)), tok(D_MODEL), _resident((1, D_MODEL)), tok(D_MODEL),
                  pl.BlockSpec(memory_space=pl.ANY)] + [vm] * n_s,
        out_specs=(tok(D_MODEL),) + _scatter_out_specs(geoms) + (vm,) * len(SMALL_FULL),
        scratch_shapes=[pltpu.VMEM((8, D_MODEL), F32)] + _scatter_scratch(geoms) + _small_scratch()
        + [pltpu.VMEM(s, F32) for s in SMALL_FULL],
        compiler_params=_cparams(("arbitrary",), collective=COLLECTIVE["bwd_proj"]),
    )(dproj, win_g, x, g1, dx2, gin_p, *small)
    return res


def _wgrad(name, a, b, tm=None, tn=None, hosted=()):
    m_w, n_w = a.shape[-1], b.shape[-1]
    tm = m_w if tm is None else tm
    tn = n_w if tn is None else tn
    n_steps = (m_w // tm) * (n_w // tn)
    geoms = [g for g, _ in hosted]
    n_h = len(hosted)

    def body(a_ref, b_ref, *rest):
        o_ref = rest[n_h]
        if n_h:
            rs = _Scatters(geoms, rest[:n_h], rest[n_h + 1:2 * n_h + 1], rest[2 * n_h + 1:])
            step = pl.program_id(0) * (n_w // tn) + pl.program_id(1)
            pl.when(step == 0)(rs.phase1)
            pl.when(step == 1)(rs.phase2)
            pl.when(step == n_steps // 2)(rs.phase2b)
        o_ref[...] = _dot_tn(a_ref[...].astype(BF16), b_ref[...].astype(BF16)).astype(BF16)
        if n_h:
            pl.when(step == n_steps - 1)(rs.phase3)

    assert not n_h or n_steps >= 4
    res = pl.pallas_call(
        body, name=name, grid=(m_w // tm, n_w // tn),
        out_shape=(jax.ShapeDtypeStruct((m_w, n_w), BF16),) + _scatter_out_shapes(geoms),
        in_specs=[pl.BlockSpec((SEQ, tm), lambda i, j: (0, i)), pl.BlockSpec((SEQ, tn), lambda i, j: (0, j))]
        + [pl.BlockSpec(memory_space=pl.ANY)] * n_h,
        out_specs=(pl.BlockSpec((tm, tn), lambda i, j: (i, j)),) + _scatter_out_specs(geoms),
        scratch_shapes=_scatter_scratch(geoms),
        compiler_params=_cparams(("arbitrary", "arbitrary"), collective=COLLECTIVE[name]) if n_h else _cparams(("parallel", "parallel")),
    )(a, b, *[p for _, p in hosted])
    return tuple(res[:1 + n_h])


def _row_step(half_rows):
    return max(s for s in range(16, 177, 16) if half_rows % s == 0)


class _Scatter:
    def __init__(self, geom, partial, out, land1, mine, stage2, land2, comb, s1_send, s1_recv, s2_send, s2_recv, ld_sems):
        self.w, self.row0, self.shape = _geom(geom)
        self.partial, self.out, self.land1 = partial, out, land1
        self.mine, self.stage2, self.land2, self.comb = mine, stage2, land2, comb
        self.hr = self.shape[0] // 2
        self.step = _row_step(self.hr)
        self.s1_send, self.s1_recv, self.s2_send, self.s2_recv, self.ld_sems = s1_send, s1_recv, s2_send, s2_recv, ld_sems
        self.x, self.y, self.c = lax.axis_index("x"), lax.axis_index("y"), lax.axis_index("c")
        self.sibling = (self.x, self.y, 1 - self.c)
        self.chips = [(self.x, self.y), (1 - self.x, self.y), (self.x, 1 - self.y), (1 - self.x, 1 - self.y)]

    def block(self, px, py, pc):
        dev = 4 * px + 2 * py + pc
        if self.w == W_IN:
            return self.partial.at[:, pl.ds(pl.multiple_of(dev * IN_SHARD, 128), IN_SHARD)]
        if self.w == W_OUT:
            return self.partial.at[pl.ds(pl.multiple_of(dev * OUT_SHARD, 128), OUT_SHARD), :]
        if self.w == W_DOWN:
            return self.partial.at[pl.ds(pl.multiple_of(dev * DOWN_SHARD, 32), DOWN_SHARD), :]
        return self.partial.at[pl.ds(pl.multiple_of(dev * FF_SHARD + self.row0, 32), self.shape[0]), :]

    def copy1(self, k):
        return pltpu.make_async_remote_copy(
            src_ref=self.block(*self.chips[k], 1 - self.c), dst_ref=self.land1.at[k],
            send_sem=self.s1_send.at[k], recv_sem=self.s1_recv.at[k], device_id=self.sibling, device_id_type=MESH)

    STAGE2 = [(1, 0, 1), (3, 0, 1), (2, 1, 2), (3, 1, 2), (1, 1, 1), (2, 0, 2)]

    def copy2(self, j):
        blk, h, to = self.STAGE2[j]
        src = self.comb.at[j - 4] if j >= 4 else self.stage2.at[blk - 1, pl.ds(h * self.hr, self.hr), :]
        return pltpu.make_async_remote_copy(
            src_ref=src, dst_ref=self.land2.at[j], send_sem=self.s2_send.at[j], recv_sem=self.s2_recv.at[j],
            device_id=(*self.chips[to], self.c), device_id_type=MESH)

    def _rows(self, h=None):
        step = self.step
        lo, n = (0, self.shape[0]) if h is None else (h * self.hr, self.hr)
        return [pl.ds(r0, step) for r0 in range(lo, lo + n, step)]

    def load(self, k):
        return pltpu.make_async_copy(self.block(*self.chips[k], self.c), self.mine.at[k], self.ld_sems.at[k])

    def load_mine(self):
        for k in range(4):
            self.load(k).start()

    def phase1(self):
        for k in range(4):
            self.copy1(k).start()

    def phase2(self, k):
        self.copy1(k).wait_recv()
        self.load(k).wait()
        for rs in self._rows():
            s = self.mine[k, rs, :].astype(F32) + self.land1[k, rs, :].astype(F32)
            if k == 0:
                self.out[rs, :] = s
            else:
                self.stage2[k - 1, rs, :] = s.astype(BF16)
        for j in {3: (1, 3), 1: (0,), 2: (2,), 0: ()}[k]:
            self.copy2(j).start()

    def phase2b(self):
        for j, got in ((4, 3), (5, 1)):
            blk, h, _ = self.STAGE2[j]
            self.copy2(got).wait_recv()
            for i, rs in enumerate(self._rows(h)):
                lr = pl.ds(i * self.step, self.step)
                self.comb[j - 4, lr, :] = (self.stage2[blk - 1, rs, :].astype(F32) + self.land2[got, lr, :].astype(F32)).astype(BF16)
            self.copy2(j).start()

    def phase3(self):
        for j in (0, 5, 4, 2):
            self.copy2(j).wait_recv()
        for h, (first, second) in enumerate(((0, 5), (4, 2))):
            for i, rs in enumerate(self._rows(h)):
                lr = pl.ds(i * self.step, self.step)
                self.out[rs, :] = (self.out[rs, :] + self.land2[first, lr, :].astype(F32)) + self.land2[second, lr, :].astype(F32)
        for k in range(4):
            self.copy1(k).wait_send()
        for j in range(6):
            self.copy2(j).wait_send()


def _geom(geom):
    if isinstance(geom, tuple):
        w, row0, rows = geom
        assert w == W_UP
        return w, row0, (rows, SHARD[w][1])
    return geom, 0, SHARD[geom]


N_SCATTER_SCRATCH = 10


def _scatter_out_shapes(geoms):
    return tuple(jax.ShapeDtypeStruct(_geom(g)[2], F32) for g in geoms)


def _scatter_out_specs(geoms):
    return (pl.BlockSpec(memory_space=pltpu.VMEM),) * len(geoms)


def _scatter_scratch(geoms):
    out = []
    for g in geoms:
        s = _geom(g)[2]
        hs = (s[0] // 2, s[1])
        out += [pltpu.VMEM((4,) + s, BF16), pltpu.VMEM((4,) + s, BF16), pltpu.VMEM((3,) + s, BF16), pltpu.VMEM((6,) + hs, BF16),
                pltpu.VMEM((2,) + hs, BF16),
                pltpu.SemaphoreType.DMA((4,)), pltpu.SemaphoreType.DMA((4,)), pltpu.SemaphoreType.DMA((6,)),
                pltpu.SemaphoreType.DMA((6,)), pltpu.SemaphoreType.DMA((4,))]
    return out


class _Scatters:
    def __init__(self, geoms, p_refs, out_refs, scratch):
        k = N_SCATTER_SCRATCH
        self.items = [_Scatter(g, p_refs[i], out_refs[i], *scratch[k * i:k * i + k]) for i, g in enumerate(geoms)]

    def phase1(self, diagonal=False):
        meet = _Meet(diagonal)
        meet.signal()
        for s in self.items:
            s.load_mine()
        meet.wait()
        for s in self.items:
            s.phase1()

    def phase2(self):
        for k in (3, 1, 2, 0):
            for s in self.items:
                s.phase2(k)

    def phase2b(self):
        for s in self.items:
            s.phase2b()

    def phase3(self):
        for s in self.items:
            s.phase3()


PACK_W = 1024


SMALL_FULL = [(2, 8, PACK_W), (HEADS, CHUNK, CHUNK), (2, 8, D_FF)]
SMALL_HALF = [(s[0] // 2,) + s[1:] for s in SMALL_FULL]
N_SMALL_SCRATCH = 16


def _small_scratch():
    n_a = len(SMALL_FULL)
    return ([pltpu.VMEM(SMALL_FULL[0], F32)] + [pltpu.VMEM(s, F32) for s in SMALL_HALF] + [pltpu.VMEM(s, F32) for s in SMALL_HALF]
            + [pltpu.VMEM((3,) + s, F32) for s in SMALL_HALF]
            + [pltpu.SemaphoreType.DMA((n_a,)), pltpu.SemaphoreType.DMA((n_a,)), pltpu.SemaphoreType.DMA((n_a, 3)),
               pltpu.SemaphoreType.DMA((n_a, 3)), pltpu.SemaphoreType.DMA((n_a,)), pltpu.SemaphoreType.DMA((n_a,))])


class _SmallReduce:
    def __init__(self, ins, outs, scratch):
        self.ins, self.outs = ins, outs
        (self.pack, *rest) = scratch
        self.rxs, self.css, self.gs = rest[0:3], rest[3:6], rest[6:9]
        self.s1_send, self.s1_recv, self.s2_send, self.s2_recv, self.s3_send, self.s3_recv = rest[9:]
        self.x, self.y, self.c = lax.axis_index("x"), lax.axis_index("y"), lax.axis_index("c")
        self.sibling = (self.x, self.y, 1 - self.c)
        self.chips = [(1 - self.x, self.y), (self.x, 1 - self.y), (1 - self.x, 1 - self.y)]
        self.hl = [s[0] for s in SMALL_HALF]

    def half(self, ref, a, h):
        return ref.at[pl.ds(h * self.hl[a], self.hl[a])]

    def begin(self):
        dg1_ref, dg2_ref, dgf_ref, dgrn_ref, dlng_ref, dlnb_ref, dbs_ref, loss_ref, dws_ref, dcv_ref = self.ins
        pack, c = self.pack, self.c
        pack[...] = jnp.zeros_like(pack)
        pack[0, 0:1, :] = dg1_ref[0:1, :]
        pack[0, 1:2, :] = dg2_ref[0:1, :]
        pack[0, 2:3, :] = dgf_ref[0:1, :]
        pack[0, 3:4, 0:RET_W] = dgrn_ref[0:1, :]
        lsum = loss_ref[0, 0:1, :]
        for i in range(1, N_TB):
            lsum = lsum + loss_ref[i, 0:1, :]
        pack[0, 3:4, RET_W:RET_W + 128] = lsum
        pack[1, 0:HEADS, 0:128] = dlng_ref[0:HEADS, :]
        pack[1, 0:HEADS, 128:256] = dlnb_ref[0:HEADS, :]
        pack[1, 0:HEADS, 256:384] = dbs_ref[0:HEADS, :]
        self.srcs = [pack, dws_ref, dcv_ref]
        n_a = len(self.srcs)
        self.ex1 = [pltpu.make_async_remote_copy(src_ref=self.half(self.srcs[a], a, 1 - c), dst_ref=self.rxs[a],
                                                 send_sem=self.s1_send.at[a], recv_sem=self.s1_recv.at[a],
                                                 device_id=self.sibling, device_id_type=MESH) for a in range(n_a)]
        for cp in self.ex1:
            cp.start()
        self.ex2 = []
        for a in range(n_a):
            self.ex1[a].wait_recv()
            self.css[a][...] = self.half(self.srcs[a], a, c)[...] + self.rxs[a][...]
            for j, chip in enumerate(self.chips):
                cp = pltpu.make_async_remote_copy(src_ref=self.css[a], dst_ref=self.gs[a].at[j], send_sem=self.s2_send.at[a, j],
                                                  recv_sem=self.s2_recv.at[a, j], device_id=(*chip, c), device_id_type=MESH)
                cp.start()
                self.ex2.append(cp)

    def end(self):
        c, x, y = self.c, self.x, self.y
        ex3 = []
        for a in range(len(self.srcs)):
            css, gs, out = self.css[a], self.gs[a], self.outs[a]
            for j in range(3):
                self.ex2[3 * a + j].wait_recv()
            tot = None
            for q in range(4):
                k = jnp.where(x != (q >> 1), 1, 0) + jnp.where(y != (q & 1), 2, 0)
                term = jnp.where(k == 0, css[...], jnp.where(k == 1, gs[0], jnp.where(k == 2, gs[1], gs[2])))
                tot = term if tot is None else tot + term
            self.half(out, a, c)[...] = tot
            cp = pltpu.make_async_remote_copy(src_ref=self.half(out, a, c), dst_ref=self.half(out, a, c), send_sem=self.s3_send.at[a],
                                              recv_sem=self.s3_recv.at[a], device_id=self.sibling, device_id_type=MESH)
            cp.start()
            ex3.append(cp)
        for a in range(len(self.srcs)):
            out = self.outs[a]
            pltpu.make_async_remote_copy(src_ref=self.half(out, a, 1 - c), dst_ref=self.half(out, a, 1 - c), send_sem=self.s3_send.at[a],
                                         recv_sem=self.s3_recv.at[a], device_id=self.sibling, device_id_type=MESH).wait_recv()
        for cp in self.ex1 + self.ex2 + ex3:
            cp.wait_send()


def _adam_math(w, g, m, v):
    nm = ADAM_B1 * m + (1.0 - ADAM_B1) * g
    nv = ADAM_B2 * v + (1.0 - ADAM_B2) * (g * g)
    d = -ADAM_LR * ((nm / (1.0 - ADAM_B1 ** ADAM_STEP)) / (jnp.sqrt(nv / (1.0 - ADAM_B2 ** ADAM_STEP)) + ADAM_EPS) + ADAM_WD * w)
    return d, nm, nv


def _adamw(params, thru, n_steps, small):
    plan = []
    for w, gs, _, _ in params:
        _, r, cdim = w.shape
        if len(gs) == 1:
            edges = [0, r // n_steps]
            spec3 = pl.BlockSpec((1, r // n_steps, cdim), lambda i: (0, i, 0))
            g_specs = [pl.BlockSpec((r // n_steps, cdim), lambda i: (i, 0))]
        else:
            edges = [sum(g.shape[0] for g in gs[:k]) for k in range(len(gs) + 1)]
            spec3 = pl.BlockSpec((1, r, cdim // n_steps), lambda i: (0, 0, i))
            g_specs = [pl.BlockSpec((g.shape[0], cdim // n_steps), lambda i: (0, i)) for g in gs]
        plan.append((len(gs), edges, spec3, g_specs))
    n_in = sum(n_g + 3 for n_g, _, _, _ in plan)
    n_t = len(thru)
    s_body, s_args, s_shapes = small
    n_s = len(s_args)

    def body(*refs):
        ins, outs = refs[:n_in], refs[n_in + n_t + n_s:]
        pl.when(pl.program_id(0) == 0)(functools.partial(s_body, *refs[n_in + n_t:n_in + n_t + n_s], *outs[4 * len(plan) + n_t:]))
        for n_g, edges, _, _ in plan:
            (w_ref, *g_refs, m_ref, v_ref), ins = ins[:n_g + 3], ins[n_g + 3:]
            (go_ref, d_ref, nm_ref, nv_ref), outs = outs[:4], outs[4:]
            for g_ref, lo, hi in zip(g_refs, edges[:-1], edges[1:]):
                gg = g_ref[...]
                go_ref[0, lo:hi, :] = gg
                d_ref[0, lo:hi, :], nm_ref[0, lo:hi, :], nv_ref[0, lo:hi, :] = _adam_math(
                    w_ref[0, lo:hi, :], gg, m_ref[0, lo:hi, :], v_ref[0, lo:hi, :])
        for t_ref, to_ref in zip(refs[n_in:n_in + n_t], outs):
            to_ref[...] = t_ref[...]

    t_specs = [pl.BlockSpec((t.shape[0] // n_steps, t.shape[1]), lambda i: (i, 0)) for t in thru]
    in_specs, out_specs, out_shape, args = [], [], [], []
    for (w, gs, m, v), (_, _, spec3, g_specs) in zip(params, plan):
        in_specs += [spec3] + g_specs + [spec3, spec3]
        out_specs += [spec3] * 4
        out_shape += [jax.ShapeDtypeStruct(w.shape, F32)] * 4
        args += [w, *gs, m, v]
    vm = pl.BlockSpec(memory_space=pltpu.VMEM)
    res = pl.pallas_call(
        body, name="adamw", grid=(n_steps,),
        out_shape=tuple(out_shape) + tuple(jax.ShapeDtypeStruct(t.shape, t.dtype) for t in thru) + tuple(s_shapes),
        in_specs=in_specs + t_specs + [vm] * n_s, out_specs=tuple(out_specs) + tuple(t_specs) + (vm,) * len(s_shapes),
        compiler_params=_cparams(("arbitrary",)),
    )(*args, *thru, *s_args)
    n_b = 4 * len(params)
    return [res[4 * k:4 * k + 4] for k in range(len(params))], res[n_b:n_b + n_t], res[n_b + n_t:]


def _adamw_small(rp, rws, rcv, params):
    n_p = len(params)

    def body(*refs):
        rp_ref, rws_ref, rcv_ref = refs[:3]
        ins = refs[3:3 + 3 * n_p]
        outs = refs[3 + 3 * n_p:]
        outs[4 * n_p][...] = rp_ref[0, 3:4, RET_W:RET_W + 1]
        grads = [rp_ref[0, 0:1, :], rp_ref[0, 1:2, :], rp_ref[0, 2:3, :], rp_ref[0, 3:4, 0:RET_W],
                 rp_ref[1, 0:HEADS, 0:128], rp_ref[1, 0:HEADS, 128:256], rp_ref[1, 0:HEADS, 256:384],
                 rws_ref[...], None, None]
        me = 4 * lax.axis_index("x") + 2 * lax.axis_index("y") + lax.axis_index("c")
        for p in range(n_p):
            w_ref, m_ref, v_ref = ins[3 * p:3 * p + 3]
            o = outs[4 * p:4 * p + 4]
            if p == n_p - 2:
                taps = jnp.where(me // (N_DEV // 2) == 0, rcv_ref[0, 0:3, :], rcv_ref[1, 0:3, :])
                g = taps[:, 0:FF_SHARD]
                for k in range(1, N_DEV // 2):
                    g = jnp.where(me % (N_DEV // 2) == k, taps[:, k * FF_SHARD:(k + 1) * FF_SHARD], g)
                for t in range(3):
                    res = (g[t:t + 1, :],) + _adam_math(w_ref[t], g[t:t + 1, :], m_ref[t], v_ref[t])
                    for q in range(4):
                        o[q][t] = res[q]
                continue
            if p == n_p - 1:
                for hf in range(2):
                    cs = slice(hf * D_FF, (hf + 1) * D_FF)
                    g = rcv_ref[hf, 3:4, :]
                    res = (g,) + _adam_math(w_ref[:, cs], g, m_ref[:, cs], v_ref[:, cs])
                    for t in range(4):
                        o[t][:, cs] = res[t]
                continue
            lead = w_ref.ndim > grads[p].ndim
            rd = (lambda r: r[0]) if lead else (lambda r: r[...])
            res = (grads[p],) + _adam_math(rd(w_ref), grads[p], rd(m_ref), rd(v_ref))
            for t in range(4):
                if lead:
                    o[t][0] = res[t]
                else:
                    o[t][...] = res[t]

    flat = [a for tr in params for a in tr]
    out_shape = tuple(jax.ShapeDtypeStruct(tr[0].shape, F32) for tr in params for _ in range(4)) + (jax.ShapeDtypeStruct((1, 1), F32),)
    return body, [rp, rws, rcv, *flat], out_shape


def kernel(x, mix_norm_g, w_in, ret_norm_g, sgu_ln_g, sgu_ln_b, sgu_w_s, sgu_b_s, w_out, ffn_norm_g, w_up, conv_w, conv_b, w_down, final_norm_g, loss_target, m_mix_norm_g, m_w_in, m_ret_norm_g, m_sgu_ln_g, m_sgu_ln_b, m_sgu_w_s, m_sgu_b_s, m_w_out, m_ffn_norm_g, m_w_up, m_conv_w, m_conv_b, m_w_down, m_final_norm_g, v_mix_norm_g, v_w_in, v_ret_norm_g, v_sgu_ln_g, v_sgu_ln_b, v_sgu_w_s, v_sgu_b_s, v_w_out, v_ffn_norm_g, v_w_up, v_conv_w, v_conv_b, v_w_down, v_final_norm_g):
    xs = x[0]
    tgt = loss_target[0]
    grn = ret_norm_g.reshape(1, RET_W)
    lng = sgu_ln_g.reshape(1, SGU_W)
    lnb = sgu_ln_b.reshape(1, SGU_W)
    ws = sgu_w_s[0]
    bsb = jnp.broadcast_to(sgu_b_s[0][:, :, None], (HEADS, CHUNK, HEAD_DIM))
    gf = final_norm_g.reshape(1, D_MODEL)
    tr = lambda a: jnp.transpose(a[0])[None]
    tr_cw = lambda a: jnp.transpose(a, (1, 0, 2))

    proj, h1, cos2, sin2, win_g, cw_sh, wout_g, wdn_g, su = _fwd_proj(
        xs, mix_norm_g, _rope_freq(), w_in[0], w_out[0], tr(w_up)[0], w_down[0], tr_cw(conv_w))
    cw_g = jnp.transpose(cw_sh, (1, 0, 2)).reshape(8, 2 * D_FF)
    x2, mixcat, o, sprev, wup_g = _fwd_mix(xs, proj, wout_g, grn, lng, lnb, ws, bsb, su)
    h2, up_pre, u_conv, act, x3, loss_parts = _fwd_ffn(x2, ffn_norm_g, wup_g, cw_g, conv_b, wdn_g, gf, tgt)

    dx3, dpre, dx2, dgf, dg2, dcv = _bwd_ffn(x3, tgt, gf, x2, ffn_norm_g, up_pre, u_conv, wup_g, cw_g, wdn_g)
    band = 512
    (gout_p,) = _wgrad("wgrad_out", mixcat, dx2, tn=512)
    gdn_p, g_out = _wgrad("wgrad_down", act, dx3, tm=FF_TILE, tn=512, hosted=[(W_OUT, gout_p)])
    gup_p, g_dn = _wgrad("wgrad_up", dpre, h2, tm=FF_TILE, tn=512, hosted=[(W_DOWN, gdn_p)])
    dproj, dgrn, dlng, dlnb, dws, dbs, g_up_a = _bwd_mix(
        dx2, proj, o, sprev, wout_g, grn, lng, lnb, ws, bsb, cos2, sin2, [((W_UP, 0, band), gup_p)])
    gin_p, g_up_b = _wgrad("wgrad_in", h1, dproj, tm=512, tn=768, hosted=[((W_UP, band, FF_SHARD - band), gup_p)])
    grad_x, g_in, rp, rws, rcv = _bwd_proj(dproj, win_g, xs, mix_norm_g, dx2, gin_p,
                                           (dg2, dgf, dgrn, dlng, dlnb, dbs, loss_parts, dws, dcv))

    table = {}
    row = lambda a: a.reshape(1, D_MODEL)
    names_small = ["mix_norm_g", "ffn_norm_g", "final_norm_g", "ret_norm_g", "sgu_ln_g", "sgu_ln_b", "sgu_b_s", "sgu_w_s",
                   "conv_w", "conv_b"]
    params = [(mix_norm_g, m_mix_norm_g, v_mix_norm_g), (ffn_norm_g, m_ffn_norm_g, v_ffn_norm_g),
              (row(final_norm_g), row(m_final_norm_g), row(v_final_norm_g)), (ret_norm_g, m_ret_norm_g, v_ret_norm_g),
              (sgu_ln_g, m_sgu_ln_g, v_sgu_ln_g), (sgu_ln_b, m_sgu_ln_b, v_sgu_ln_b), (sgu_b_s, m_sgu_b_s, v_sgu_b_s),
              (sgu_w_s, m_sgu_w_s, v_sgu_w_s), (tr_cw(conv_w), tr_cw(m_conv_w), tr_cw(v_conv_w)), (conv_b, m_conv_b, v_conv_b)]
    big, (grad_x,), small = _adamw([(w_in, [g_in], m_w_in, v_w_in), (w_out, [g_out], m_w_out, v_w_out),
                                    (tr(w_up), [g_up_a, g_up_b], tr(m_w_up), tr(v_w_up)), (w_down, [g_dn], m_w_down, v_w_down)],
                                   [grad_x], 4, _adamw_small(rp, rws, rcv, params))
    table.update(zip(("w_in", "w_out", "w_up", "w_down"), big))
    table["w_up"] = tuple(tr(a) for a in table["w_up"])
    loss = small[4 * len(params)]
    for k, n in enumerate(names_small):
        table[n] = small[4 * k:4 * k + 4]
    table["final_norm_g"] = tuple(a.reshape(D_MODEL) for a in table["final_norm_g"])
    table["conv_w"] = tuple(tr_cw(a) for a in table["conv_w"])

    order = ["mix_norm_g", "w_in", "ret_norm_g", "sgu_ln_g", "sgu_ln_b", "sgu_w_s", "sgu_b_s", "w_out", "ffn_norm_g", "w_up",
             "conv_w", "conv_b", "w_down", "final_norm_g"]
    outs = [loss.reshape(()), grad_x[None]]
    for col in range(4):
        outs += [table[n][col] for n in order]
    return tuple(outs)
```
